```python
import jax, jax.numpy as jnp
from jax import lax
import numpy as np

D_MODEL = 1024
BATCH = 8
SEQ = 8192
DEPTH = 2

HEAD_DIM = 64
ATTN_WIDTH = D_MODEL // 2
N_HEADS_A = ATTN_WIDTH // HEAD_DIM
Q_BLOCK = 128
CONV_CH = D_MODEL // 2
CONV_K = 3
POOL_WINDOWS = (2, 4, 8, 16)
POOL_GROUPS = len(POOL_WINDOWS)
POOL_CG = D_MODEL // POOL_GROUPS
D_FF = 4 * D_MODEL
RMS_EPS = 1e-6
MIX_IN = 3 * ATTN_WIDTH + N_HEADS_A + 3 * CONV_CH
MIX_OUT = ATTN_WIDTH + CONV_CH

kernel_name = "fox_shortconv_pool_hybrid"


def rms_norm(x, g):
    xf = x.astype(jnp.float32)
    y = xf * lax.rsqrt(jnp.mean(xf * xf, axis=-1, keepdims=True) + RMS_EPS)
    return (y * g.astype(jnp.float32)).astype(x.dtype)


def forgetting_attention(q, k, v, f_logit):
    b, s, h, dh = q.shape
    nblk = s // Q_BLOCK
    log_f = jax.nn.log_sigmoid(f_logit.astype(jnp.float32))
    cum = jnp.cumsum(log_f, axis=1)
    cum_k = cum.transpose(0, 2, 1)
    q_blocks = q.reshape(b, nblk, Q_BLOCK, h, dh).transpose(1, 0, 2, 3, 4)
    cum_q_blocks = cum.reshape(b, nblk, Q_BLOCK, h).transpose(1, 0, 3, 2)
    q_pos = jnp.arange(s).reshape(nblk, Q_BLOCK)
    k_pos = jnp.arange(s)
    scale = dh ** -0.5
    neg = jnp.finfo(jnp.float32).min

    def one_block(args):
        qi, cqi, pi = args
        logits = jnp.einsum('bqhd,bkhd->bhqk', qi, k).astype(jnp.float32) * scale
        logits = logits + cqi[..., None] - cum_k[:, :, None, :]
        mask = k_pos[None, :] <= pi[:, None]
        logits = jnp.where(mask, logits, neg)
        p = jax.nn.softmax(logits, axis=-1)
        return jnp.einsum('bhqk,bkhd->bqhd', p.astype(v.dtype), v)

    out = lax.map(one_block, (q_blocks, cum_q_blocks, q_pos))
    return out.transpose(1, 0, 2, 3, 4).reshape(b, s, h * dh)


def causal_dwconv3(u, w):
    s = u.shape[1]
    up = jnp.pad(u, ((0, 0), (CONV_K - 1, 0), (0, 0)))
    return w[0] * up[:, 0:s] + w[1] * up[:, 1:s + 1] + w[2] * up[:, 2:s + 2]


def attn_conv_mixer(h, w_in, b_f, conv_w, w_out):
    b, s, _ = h.shape
    proj = h @ w_in
    a = ATTN_WIDTH
    splits = [a, 2 * a, 3 * a, 3 * a + N_HEADS_A,
              3 * a + N_HEADS_A + CONV_CH, 3 * a + N_HEADS_A + 2 * CONV_CH]
    q, k, v, f_logit, b_gate, c_gate, x_in = jnp.split(proj, splits, axis=-1)
    shp = (b, s, N_HEADS_A, HEAD_DIM)
    att = forgetting_attention(q.reshape(shp), k.reshape(shp), v.reshape(shp), f_logit + b_f)
    conv = b_gate * causal_dwconv3(c_gate * x_in, conv_w)
    return jnp.concatenate([att, conv], axis=-1) @ w_out


def causal_mean_pool_minus_self(u, window):
    s = u.shape[1]
    uf = u.astype(jnp.float32)
    cs = jnp.pad(jnp.cumsum(uf, axis=1), ((0, 0), (1, 0), (0, 0)))
    lagged = jnp.pad(cs, ((0, 0), (window - 1, 0), (0, 0)))[:, :s]
    count = jnp.minimum(jnp.arange(1, s + 1), window).astype(jnp.float32)[None, :, None]
    return ((cs[:, 1:] - lagged) / count - uf).astype(u.dtype)


def pool_mixer(h, pool_w, pool_scale):
    b, s, d = h.shape
    groups = jnp.split(h, POOL_GROUPS, axis=-1)
    pooled = jnp.stack([causal_mean_pool_minus_self(g, w) for g, w in zip(groups, POOL_WINDOWS)],
                       axis=2)
    y = jnp.einsum('bsgc,gcd->bsgd', pooled, pool_w).reshape(b, s, d)
    return y * pool_scale


def sq_relu_mlp(h, w_up, w_down):
    return jnp.square(jax.nn.relu(h @ w_up)) @ w_down


def _fwd_setup_inputs(seed: int = 0) -> dict:
    key = jax.random.key(seed)
    ks = jax.random.split(key, 20)
    f32 = jnp.float32

    def nrm(k, shape, scale):
        return jax.random.normal(k, shape, f32) * scale

    def gain(k):
        return 1.0 + 0.05 * jax.random.normal(k, (D_MODEL,), f32)

    return {
        "x": jax.random.normal(ks[0], (BATCH, SEQ, D_MODEL), f32),
        "norm_mix_0": gain(ks[1]),
        "w_in_0": nrm(ks[2], (D_MODEL, MIX_IN), D_MODEL ** -0.5),
        "b_f_0": 2.0 + 0.5 * jax.random.normal(ks[3], (N_HEADS_A,), f32),
        "conv_w_0": nrm(ks[4], (CONV_K, CONV_CH), CONV_K ** -0.5),
        "w_out_0": nrm(ks[5], (MIX_OUT, D_MODEL), MIX_OUT ** -0.5),
        "norm_ffn_0": gain(ks[6]),
        "w_up_0": nrm(ks[7], (D_MODEL, D_FF), D_MODEL ** -0.5),
        "w_down_0": nrm(ks[8], (D_FF, D_MODEL), D_FF ** -0.5),
        "norm_mix_1": gain(ks[9]),
        "pool_w_1": nrm(ks[10], (POOL_GROUPS, POOL_CG, POOL_CG), POOL_CG ** -0.5),
        "pool_scale_1": 1.0 + 0.05 * jax.random.normal(ks[11], (D_MODEL,), f32),
        "norm_ffn_1": gain(ks[12]),
        "w_up_1": nrm(ks[13], (D_MODEL, D_FF), D_MODEL ** -0.5),
        "w_down_1": nrm(ks[14], (D_FF, D_MODEL), D_FF ** -0.5),
        "final_norm": gain(ks[15]),
    }


def _fwd_reference(x, norm_mix_0, w_in_0, b_f_0, conv_w_0, w_out_0, norm_ffn_0, w_up_0, w_down_0,
              norm_mix_1, pool_w_1, pool_scale_1, norm_ffn_1, w_up_1, w_down_1, final_norm):
    mix_params = [(norm_mix_0, w_in_0, b_f_0, conv_w_0, w_out_0),
                  (norm_mix_1, pool_w_1, pool_scale_1)]
    ffn_params = [(norm_ffn_0, w_up_0, w_down_0), (norm_ffn_1, w_up_1, w_down_1)]
    h = x
    for i in range(DEPTH):
        mp = mix_params[i]
        if i % 2 == 0:
            h = h + attn_conv_mixer(rms_norm(h, mp[0]), *mp[1:])
        else:
            h = h + pool_mixer(rms_norm(h, mp[0]), *mp[1:])
        g, w_up, w_down = ffn_params[i]
        h = h + sq_relu_mlp(rms_norm(h, g), w_up, w_down)
    return rms_norm(h, final_norm)


import jax as _jax
import jax.numpy as _jnp

TWIN_FORMAT = 'train_step'
FWD_PARAMS = ['x', 'norm_mix_0', 'w_in_0', 'b_f_0', 'conv_w_0', 'w_out_0', 'norm_ffn_0', 'w_up_0', 'w_down_0', 'norm_mix_1', 'pool_w_1', 'pool_scale_1', 'norm_ffn_1', 'w_up_1', 'w_down_1', 'final_norm']
TWIN_WEIGHTS = ['norm_mix_0', 'w_in_0', 'b_f_0', 'conv_w_0', 'w_out_0', 'norm_ffn_0', 'w_up_0', 'w_down_0', 'norm_mix_1', 'pool_w_1', 'pool_scale_1', 'norm_ffn_1', 'w_up_1', 'w_down_1', 'final_norm']
TWIN_DIFF_INPUT = 'x'
TWIN_INPUTS = ['x', 'norm_mix_0', 'w_in_0', 'b_f_0', 'conv_w_0', 'w_out_0', 'norm_ffn_0', 'w_up_0', 'w_down_0', 'norm_mix_1', 'pool_w_1', 'pool_scale_1', 'norm_ffn_1', 'w_up_1', 'w_down_1', 'final_norm', 'loss_target', 'm_norm_mix_0', 'm_w_in_0', 'm_b_f_0', 'm_conv_w_0', 'm_w_out_0', 'm_norm_ffn_0', 'm_w_up_0', 'm_w_down_0', 'm_norm_mix_1', 'm_pool_w_1', 'm_pool_scale_1', 'm_norm_ffn_1', 'm_w_up_1', 'm_w_down_1', 'm_final_norm', 'v_norm_mix_0', 'v_w_in_0', 'v_b_f_0', 'v_conv_w_0', 'v_w_out_0', 'v_norm_ffn_0', 'v_w_up_0', 'v_w_down_0', 'v_norm_mix_1', 'v_pool_w_1', 'v_pool_scale_1', 'v_norm_ffn_1', 'v_w_up_1', 'v_w_down_1', 'v_final_norm']
TWIN_OUTPUTS = ['loss', 'grad_x', 'grad_norm_mix_0', 'grad_w_in_0', 'grad_b_f_0', 'grad_conv_w_0', 'grad_w_out_0', 'grad_norm_ffn_0', 'grad_w_up_0', 'grad_w_down_0', 'grad_norm_mix_1', 'grad_pool_w_1', 'grad_pool_scale_1', 'grad_norm_ffn_1', 'grad_w_up_1', 'grad_w_down_1', 'grad_final_norm', 'delta_norm_mix_0', 'delta_w_in_0', 'delta_b_f_0', 'delta_conv_w_0', 'delta_w_out_0', 'delta_norm_ffn_0', 'delta_w_up_0', 'delta_w_down_0', 'delta_norm_mix_1', 'delta_pool_w_1', 'delta_pool_scale_1', 'delta_norm_ffn_1', 'delta_w_up_1', 'delta_w_down_1', 'delta_final_norm', 'new_m_norm_mix_0', 'new_m_w_in_0', 'new_m_b_f_0', 'new_m_conv_w_0', 'new_m_w_out_0', 'new_m_norm_ffn_0', 'new_m_w_up_0', 'new_m_w_down_0', 'new_m_norm_mix_1', 'new_m_pool_w_1', 'new_m_pool_scale_1', 'new_m_norm_ffn_1', 'new_m_w_up_1', 'new_m_w_down_1', 'new_m_final_norm', 'new_v_norm_mix_0', 'new_v_w_in_0', 'new_v_b_f_0', 'new_v_conv_w_0', 'new_v_w_out_0', 'new_v_norm_ffn_0', 'new_v_w_up_0', 'new_v_w_down_0', 'new_v_norm_mix_1', 'new_v_pool_w_1', 'new_v_pool_scale_1', 'new_v_norm_ffn_1', 'new_v_w_up_1', 'new_v_w_down_1', 'new_v_final_norm']
TWIN_LEAF_KINDS = {'loss': 'loss', 'grad_x': 'grad_x', 'grad_norm_mix_0': 'grad_w', 'grad_w_in_0': 'grad_w', 'grad_b_f_0': 'grad_w', 'grad_conv_w_0': 'grad_w', 'grad_w_out_0': 'grad_w', 'grad_norm_ffn_0': 'grad_w', 'grad_w_up_0': 'grad_w', 'grad_w_down_0': 'grad_w', 'grad_norm_mix_1': 'grad_w', 'grad_pool_w_1': 'grad_w', 'grad_pool_scale_1': 'grad_w', 'grad_norm_ffn_1': 'grad_w', 'grad_w_up_1': 'grad_w', 'grad_w_down_1': 'grad_w', 'grad_final_norm': 'grad_w', 'delta_norm_mix_0': 'delta_w', 'delta_w_in_0': 'delta_w', 'delta_b_f_0': 'delta_w', 'delta_conv_w_0': 'delta_w', 'delta_w_out_0': 'delta_w', 'delta_norm_ffn_0': 'delta_w', 'delta_w_up_0': 'delta_w', 'delta_w_down_0': 'delta_w', 'delta_norm_mix_1': 'delta_w', 'delta_pool_w_1': 'delta_w', 'delta_pool_scale_1': 'delta_w', 'delta_norm_ffn_1': 'delta_w', 'delta_w_up_1': 'delta_w', 'delta_w_down_1': 'delta_w', 'delta_final_norm': 'delta_w', 'new_m_norm_mix_0': 'new_m', 'new_m_w_in_0': 'new_m', 'new_m_b_f_0': 'new_m', 'new_m_conv_w_0': 'new_m', 'new_m_w_out_0': 'new_m', 'new_m_norm_ffn_0': 'new_m', 'new_m_w_up_0': 'new_m', 'new_m_w_down_0': 'new_m', 'new_m_norm_mix_1': 'new_m', 'new_m_pool_w_1': 'new_m', 'new_m_pool_scale_1': 'new_m', 'new_m_norm_ffn_1': 'new_m', 'new_m_w_up_1': 'new_m', 'new_m_w_down_1': 'new_m', 'new_m_final_norm': 'new_m', 'new_v_norm_mix_0': 'new_v', 'new_v_w_in_0': 'new_v', 'new_v_b_f_0': 'new_v', 'new_v_conv_w_0': 'new_v', 'new_v_w_out_0': 'new_v', 'new_v_norm_ffn_0': 'new_v', 'new_v_w_up_0': 'new_v', 'new_v_w_down_0': 'new_v', 'new_v_norm_mix_1': 'new_v', 'new_v_pool_w_1': 'new_v', 'new_v_pool_scale_1': 'new_v', 'new_v_norm_ffn_1': 'new_v', 'new_v_w_up_1': 'new_v', 'new_v_w_down_1': 'new_v', 'new_v_final_norm': 'new_v'}


def _forward(args):
    return _fwd_reference(*[args[k] for k in FWD_PARAMS])


def _output_shape():
    def fwd():
        inp = _fwd_setup_inputs(0)
        return _fwd_reference(*[inp[k] for k in FWD_PARAMS])
    out = _jax.eval_shape(fwd)
    return out.shape, out.dtype

N_MICROBATCH = 1
ADAM_LR = 0.001
ADAM_B1 = 0.9
ADAM_B2 = 0.999
ADAM_EPS = 1e-08
ADAM_WD = 0.01
ADAM_STEP = 10
PER_EXAMPLE_BATCH_AXIS = {'x': 0, 'loss_target': 0}
SHARED_INPUTS = []
_WEIGHT_DTYPES = {'norm_mix_0': _jnp.float32, 'w_in_0': _jnp.float32, 'b_f_0': _jnp.float32, 'conv_w_0': _jnp.float32, 'w_out_0': _jnp.float32, 'norm_ffn_0': _jnp.float32, 'w_up_0': _jnp.float32, 'w_down_0': _jnp.float32, 'norm_mix_1': _jnp.float32, 'pool_w_1': _jnp.float32, 'pool_scale_1': _jnp.float32, 'norm_ffn_1': _jnp.float32, 'w_up_1': _jnp.float32, 'w_down_1': _jnp.float32, 'final_norm': _jnp.float32}
MOMENT_SCALE = {'norm_mix_0': 2.969703e-01, 'w_in_0': 1.699166e-01, 'b_f_0': 5.544910e-01, 'conv_w_0': 2.350885e-01, 'w_out_0': 1.751358e-01, 'norm_ffn_0': 2.165828e-01, 'w_up_0': 1.099368e-01, 'w_down_0': 3.609842e-01, 'norm_mix_1': 1.306588e-01, 'pool_w_1': 1.333604e-01, 'pool_scale_1': 7.709518e-01, 'norm_ffn_1': 1.877965e-01, 'w_up_1': 9.773152e-02, 'w_down_1': 3.725452e-01, 'final_norm': 6.480412e+01}


def _to_microbatches(a, axis):
    t = _jnp.moveaxis(a, axis, 0)
    t = t.reshape((N_MICROBATCH, t.shape[0] // N_MICROBATCH) + t.shape[1:])
    return _jnp.moveaxis(t, 1, axis + 1)


def setup_inputs(seed: int = 0) -> dict:
    inp = _fwd_setup_inputs(seed)
    key = _jax.random.fold_in(_jax.random.key(seed), 7919)
    shape, _ = _output_shape()
    out = dict(inp)
    out["loss_target"] = _jax.random.normal(_jax.random.fold_in(key, 0), shape, _jnp.float32)
    for i, name in enumerate(TWIN_WEIGHTS):
        w = inp[name].astype(_jnp.float32)
        if MOMENT_SCALE is None:
            s = _jnp.sqrt(_jnp.mean(_jnp.square(w)) + 1e-30)
        else:
            s = MOMENT_SCALE[name]
        km, kv = _jax.random.split(_jax.random.fold_in(key, i + 1))
        out[name] = w
        out["m_" + name] = s * _jax.random.normal(km, w.shape, _jnp.float32)
        out["v_" + name] = (s * s) * _jax.random.uniform(kv, w.shape, _jnp.float32, 0.5, 1.5)
    if N_MICROBATCH > 1:
        for name, axis in PER_EXAMPLE_BATCH_AXIS.items():
            out[name] = _to_microbatches(out[name], axis)
    return {'x': out['x'], 'norm_mix_0': out['norm_mix_0'], 'w_in_0': out['w_in_0'], 'b_f_0': out['b_f_0'], 'conv_w_0': out['conv_w_0'], 'w_out_0': out['w_out_0'], 'norm_ffn_0': out['norm_ffn_0'], 'w_up_0': out['w_up_0'], 'w_down_0': out['w_down_0'], 'norm_mix_1': out['norm_mix_1'], 'pool_w_1': out['pool_w_1'], 'pool_scale_1': out['pool_scale_1'], 'norm_ffn_1': out['norm_ffn_1'], 'w_up_1': out['w_up_1'], 'w_down_1': out['w_down_1'], 'final_norm': out['final_norm'], 'loss_target': out['loss_target'], 'm_norm_mix_0': out['m_norm_mix_0'], 'm_w_in_0': out['m_w_in_0'], 'm_b_f_0': out['m_b_f_0'], 'm_conv_w_0': out['m_conv_w_0'], 'm_w_out_0': out['m_w_out_0'], 'm_norm_ffn_0': out['m_norm_ffn_0'], 'm_w_up_0': out['m_w_up_0'], 'm_w_down_0': out['m_w_down_0'], 'm_norm_mix_1': out['m_norm_mix_1'], 'm_pool_w_1': out['m_pool_w_1'], 'm_pool_scale_1': out['m_pool_scale_1'], 'm_norm_ffn_1': out['m_norm_ffn_1'], 'm_w_up_1': out['m_w_up_1'], 'm_w_down_1': out['m_w_down_1'], 'm_final_norm': out['m_final_norm'], 'v_norm_mix_0': out['v_norm_mix_0'], 'v_w_in_0': out['v_w_in_0'], 'v_b_f_0': out['v_b_f_0'], 'v_conv_w_0': out['v_conv_w_0'], 'v_w_out_0': out['v_w_out_0'], 'v_norm_ffn_0': out['v_norm_ffn_0'], 'v_w_up_0': out['v_w_up_0'], 'v_w_down_0': out['v_w_down_0'], 'v_norm_mix_1': out['v_norm_mix_1'], 'v_pool_w_1': out['v_pool_w_1'], 'v_pool_scale_1': out['v_pool_scale_1'], 'v_norm_ffn_1': out['v_norm_ffn_1'], 'v_w_up_1': out['v_w_up_1'], 'v_w_down_1': out['v_w_down_1'], 'v_final_norm': out['v_final_norm']}


def _loss(weights, diff, rest, loss_target):
    with _jax.named_scope("forward"):
        args = {**rest, TWIN_DIFF_INPUT: diff, **{k: w.astype(_WEIGHT_DTYPES[k]) for k, w in weights.items()}}
        y = _forward(args)
    with _jax.named_scope("loss_head"):
        err = _jnp.square(y.astype(_jnp.float32) - loss_target)
        return 0.5 * _jnp.sum(_jnp.mean(err, axis=-1)) if err.ndim else 0.5 * err


def _adamw(w, g, m, v):
    m = ADAM_B1 * m + (1.0 - ADAM_B1) * g
    v = ADAM_B2 * v + (1.0 - ADAM_B2) * _jnp.square(g)
    m_hat = m / (1.0 - ADAM_B1 ** ADAM_STEP)
    v_hat = v / (1.0 - ADAM_B2 ** ADAM_STEP)
    delta = -ADAM_LR * (m_hat / (_jnp.sqrt(v_hat) + ADAM_EPS) + ADAM_WD * w)
    return delta, m, v


def reference(x, norm_mix_0, w_in_0, b_f_0, conv_w_0, w_out_0, norm_ffn_0, w_up_0, w_down_0, norm_mix_1, pool_w_1, pool_scale_1, norm_ffn_1, w_up_1, w_down_1, final_norm, loss_target, m_norm_mix_0, m_w_in_0, m_b_f_0, m_conv_w_0, m_w_out_0, m_norm_ffn_0, m_w_up_0, m_w_down_0, m_norm_mix_1, m_pool_w_1, m_pool_scale_1, m_norm_ffn_1, m_w_up_1, m_w_down_1, m_final_norm, v_norm_mix_0, v_w_in_0, v_b_f_0, v_conv_w_0, v_w_out_0, v_norm_ffn_0, v_w_up_0, v_w_down_0, v_norm_mix_1, v_pool_w_1, v_pool_scale_1, v_norm_ffn_1, v_w_up_1, v_w_down_1, v_final_norm):
    given = dict(x=x, norm_mix_0=norm_mix_0, w_in_0=w_in_0, b_f_0=b_f_0, conv_w_0=conv_w_0, w_out_0=w_out_0, norm_ffn_0=norm_ffn_0, w_up_0=w_up_0, w_down_0=w_down_0, norm_mix_1=norm_mix_1, pool_w_1=pool_w_1, pool_scale_1=pool_scale_1, norm_ffn_1=norm_ffn_1, w_up_1=w_up_1, w_down_1=w_down_1, final_norm=final_norm, loss_target=loss_target, m_norm_mix_0=m_norm_mix_0, m_w_in_0=m_w_in_0, m_b_f_0=m_b_f_0, m_conv_w_0=m_conv_w_0, m_w_out_0=m_w_out_0, m_norm_ffn_0=m_norm_ffn_0, m_w_up_0=m_w_up_0, m_w_down_0=m_w_down_0, m_norm_mix_1=m_norm_mix_1, m_pool_w_1=m_pool_w_1, m_pool_scale_1=m_pool_scale_1, m_norm_ffn_1=m_norm_ffn_1, m_w_up_1=m_w_up_1, m_w_down_1=m_w_down_1, m_final_norm=m_final_norm, v_norm_mix_0=v_norm_mix_0, v_w_in_0=v_w_in_0, v_b_f_0=v_b_f_0, v_conv_w_0=v_conv_w_0, v_w_out_0=v_w_out_0, v_norm_ffn_0=v_norm_ffn_0, v_w_up_0=v_w_up_0, v_w_down_0=v_w_down_0, v_norm_mix_1=v_norm_mix_1, v_pool_w_1=v_pool_w_1, v_pool_scale_1=v_pool_scale_1, v_norm_ffn_1=v_norm_ffn_1, v_w_up_1=v_w_up_1, v_w_down_1=v_w_down_1, v_final_norm=v_final_norm)
    weights = {n: given[n] for n in TWIN_WEIGHTS}
    shared = {n: given[n] for n in SHARED_INPUTS}
    per_example = {n: given[n] for n in ['x']}
    grad_fn = _jax.value_and_grad(_loss, argnums=(0, 1))

    def one_microbatch(ex, loss_target):
        ex = dict(ex)
        diff = ex.pop(TWIN_DIFF_INPUT)
        return grad_fn(weights, diff, {**shared, **ex}, loss_target)

    if N_MICROBATCH == 1:
        loss, (grad_w, grad_x) = one_microbatch(per_example, given["loss_target"])
    else:
        def body(carry, xs):
            loss_sum, grad_sum = carry
            l_k, (gw_k, gx_k) = one_microbatch(xs[0], xs[1])
            with _jax.named_scope("update"):
                return (loss_sum + l_k, _jax.tree.map(_jnp.add, grad_sum, gw_k)), gx_k

        init = (_jnp.zeros((), _jnp.float32), _jax.tree.map(_jnp.zeros_like, weights))
        (loss, grad_w), grad_x = _jax.lax.scan(body, init, (per_example, given["loss_target"]))
    with _jax.named_scope("update"):
        delta_w, new_m, new_v = {}, {}, {}
        for n in TWIN_WEIGHTS:
            delta_w[n], new_m[n], new_v[n] = _adamw(weights[n], grad_w[n], given["m_" + n], given["v_" + n])
    return (loss, grad_x, *[grad_w[n] for n in TWIN_WEIGHTS], *[delta_w[n] for n in TWIN_WEIGHTS],
            *[new_m[n] for n in TWIN_WEIGHTS], *[new_v[n] for n in TWIN_WEIGHTS])
```

```python
import functools

import jax
import jax.numpy as jnp
from jax import lax
from jax.experimental import pallas as pl
from jax.experimental.pallas import tpu as pltpu

F32 = jnp.float32
BF16 = jnp.bfloat16

RMS_EPS = 1e-6
HEAD_DIM = 64
N_HEADS = 8
ATTN_SCALE = HEAD_DIM ** -0.5
POOL_WINDOWS = (2, 4, 8, 16)
POOL_HALO = 16
CONV_HALO = 8
NEG_BIG = -1e30

ADAM_LR = 0.001
ADAM_B1 = 0.9
ADAM_B2 = 0.999
ADAM_EPS = 1e-08
ADAM_WD = 0.01
ADAM_STEP = 10

N_CHIPS = 4
N_DEV = 8
MESH = pl.DeviceIdType.MESH

VMEM_LIMIT_BYTES = 56 * 1024 * 1024

TILE_ROWS = 512
TILE_ATTN = 512
TILE_MLP_ROWS = 1024
TILE_MLP_FF = 512
TILE_WGRAD_K = 512
TILE_WGRAD_N = 1024
TILE_ELEM_ROWS = 256

LANE_CQ = 64
LANE_ONE = 67


def _params(semantics):
    return pltpu.CompilerParams(dimension_semantics=semantics,
                                vmem_limit_bytes=VMEM_LIMIT_BYTES)


def _nn(a, b):
    return lax.dot_general(a, b, (((1,), (0,)), ((), ())), preferred_element_type=F32)


def _nt(a, b):
    return lax.dot_general(a, b, (((1,), (1,)), ((), ())), preferred_element_type=F32)


def _tn(a, b):
    return lax.dot_general(a, b, (((0,), (0,)), ((), ())), preferred_element_type=F32)


def _split3(v):
    hi = v.astype(BF16)
    r1 = v - hi.astype(F32)
    mid = r1.astype(BF16)
    lo = (r1 - mid.astype(F32)).astype(BF16)
    return hi, mid, lo


def _exact_nn(sel, v):
    hi, mid, lo = _split3(v)
    return _nn(sel, hi) + _nn(sel, mid) + _nn(sel, lo)


def _exact_nt(sel, v):
    hi, mid, lo = _split3(v)
    return _nt(sel, hi) + _nt(sel, mid) + _nt(sel, lo)


def _rms_fwd(x, g):
    r = lax.rsqrt(jnp.mean(x * x, axis=-1, keepdims=True) + RMS_EPS)
    return x * r * g, r


def _rms_bwd(dn, x, g):
    r = lax.rsqrt(jnp.mean(x * x, axis=-1, keepdims=True) + RMS_EPS)
    xh = x * r
    gy = dn * g
    dx = r * (gy - xh * jnp.mean(gy * xh, axis=-1, keepdims=True))
    return dx, jnp.sum(dn * xh, axis=0, keepdims=True)


def _lane(shape):
    return lax.broadcasted_iota(jnp.int32, shape, len(shape) - 1)


def _row(shape):
    return lax.broadcasted_iota(jnp.int32, shape, len(shape) - 2)


def _full(a):
    nd = a.ndim
    return pl.BlockSpec(a.shape, lambda *_: (0,) * nd)


def _ln_proj(x, g, w_qkv, w_f, w_bcx):
    s, d = x.shape
    tm = min(TILE_ROWS, s)

    def body(x_ref, g_ref, wq_ref, wf_ref, wb_ref, n_ref, qkv_ref, fl_ref, bcx_ref):
        n, _ = _rms_fwd(x_ref[...], g_ref[...])
        nb = n.astype(BF16)
        n_ref[...] = nb
        qkv_ref[...] = _nn(nb, wq_ref[...]).astype(BF16)
        fl_ref[...] = _nn(nb, wf_ref[...])
        bcx_ref[...] = _nn(nb, wb_ref[...])

    rows = lambda c: pl.BlockSpec((tm, c), lambda i: (i, 0))
    return pl.pallas_call(
        body, name="ln_proj", grid=(s // tm,),
        in_specs=[rows(d), _full(g), _full(w_qkv), _full(w_f), _full(w_bcx)],
        out_specs=[rows(d), rows(w_qkv.shape[1]), rows(w_f.shape[1]), rows(w_bcx.shape[1])],
        out_shape=[jax.ShapeDtypeStruct((s, d), BF16),
                   jax.ShapeDtypeStruct((s, w_qkv.shape[1]), BF16),
                   jax.ShapeDtypeStruct((s, w_f.shape[1]), F32),
                   jax.ShapeDtypeStruct((s, w_bcx.shape[1]), F32)],
        compiler_params=_params(("parallel",)),
    )(x, g, w_qkv, w_f, w_bcx)


def _gate_prep(fl, bf, qkv):
    s = fl.shape[0]
    a = N_HEADS * HEAD_DIM
    tm = min(TILE_ROWS, s)

    def body(fl_ref, bf_ref, q_ref, k_ref, qa_ref, ka_ref, carry_ref):
        i = pl.program_id(0)

        @pl.when(i == 0)
        def _():
            carry_ref[...] = jnp.zeros_like(carry_ref)

        z = fl_ref[...] + bf_ref[...]
        logf = jnp.minimum(z, 0.0) - jnp.log(1.0 + jnp.exp(-jnp.abs(z)))
        lower = (_lane((tm, tm)) <= _row((tm, tm))).astype(BF16)
        cum = _exact_nn(lower, logf) + carry_ref[0:1, :]
        carry_ref[0:1, :] = cum[tm - 1:tm, :]

        lane = _lane((tm, 128))
        for h in range(N_HEADS):
            cb = jnp.sum(jnp.where(lane == h, cum, 0.0), axis=1, keepdims=True)
            hi, mid, lo = (p.astype(F32) for p in _split3(cb))
            pair = slice((h // 2) * 128, (h // 2 + 1) * 128)
            qp = q_ref[:, pair].astype(F32)
            kp = k_ref[:, pair].astype(F32)
            if h % 2:
                qp = pltpu.roll(qp, HEAD_DIM, axis=1)
                kp = pltpu.roll(kp, HEAD_DIM, axis=1)
            q_bias = jnp.where(lane == LANE_CQ, hi,
                               jnp.where(lane == LANE_CQ + 1, mid,
                                         jnp.where(lane == LANE_CQ + 2, lo,
                                                   jnp.where(lane < LANE_ONE + 3, 1.0, 0.0))))
            k_bias = jnp.where(lane < LANE_ONE, 1.0,
                               jnp.where(lane == LANE_ONE, -hi,
                                         jnp.where(lane == LANE_ONE + 1, -mid,
                                                   jnp.where(lane == LANE_ONE + 2, -lo, 0.0))))
            qa_ref[h] = jnp.where(lane < HEAD_DIM, qp * ATTN_SCALE, q_bias).astype(BF16)
            ka_ref[h] = jnp.where(lane < HEAD_DIM, kp, k_bias).astype(BF16)

    aug = jax.ShapeDtypeStruct((N_HEADS, s, 128), BF16)
    aug_spec = pl.BlockSpec((N_HEADS, tm, 128), lambda i: (0, i, 0))
    return pl.pallas_call(
        body, name="gate_prep", grid=(s // tm,),
        in_specs=[pl.BlockSpec((tm, 128), lambda i: (i, 0)), _full(bf),
                  pl.BlockSpec((tm, a), lambda i: (i, 0)),
                  pl.BlockSpec((tm, a), lambda i: (i, 1))],
        out_specs=[aug_spec, aug_spec],
        out_shape=[aug, aug],
        scratch_shapes=[pltpu.VMEM((8, 128), F32)],
        compiler_params=_params(("arbitrary",)),
    )(fl, bf, qkv, qkv)


def _attn_fwd(qa, ka, qkv):
    s = qa.shape[1]
    a = N_HEADS * HEAD_DIM
    t = min(TILE_ATTN, s)
    n_pairs = N_HEADS // 2
    v_block0 = 2 * a // 128

    def body(qa_ref, ka_ref, v_ref, o_ref, lse_ref, m_ref, l_ref, acc_ref):
        i = pl.program_id(1)
        m_ref[...] = jnp.full_like(m_ref, NEG_BIG)
        l_ref[...] = jnp.zeros_like(l_ref)
        acc_ref[...] = jnp.zeros_like(acc_ref)

        def kv_step(j, masked):
            ks = pl.ds(pl.multiple_of(j * t, t), t)
            vb = v_ref[ks, :]
            for e in range(2):
                sc = _nt(qa_ref[e], ka_ref[e, ks, :])
                if masked:
                    sc = jnp.where(_lane((t, t)) <= _row((t, t)), sc, NEG_BIG)
                m_prev = m_ref[e]
                m_new = jnp.maximum(m_prev, jnp.max(sc, axis=1, keepdims=True))
                p = jnp.exp(sc - m_new)
                alpha = jnp.exp(m_prev - m_new)
                l_ref[e] = alpha * l_ref[e] + jnp.sum(p, axis=1, keepdims=True)
                acc_ref[e] = alpha * acc_ref[e] + _nn(p.astype(BF16), vb)
                m_ref[e] = m_new

        def full_step(j, carry):
            kv_step(j, False)
            return carry

        lax.fori_loop(0, i, full_step, 0)
        kv_step(i, True)

        lane = _lane((t, 128))
        o_ref[...] = jnp.where(lane < HEAD_DIM, acc_ref[0] / l_ref[0],
                               acc_ref[1] / l_ref[1]).astype(BF16)
        first = (_lane((8, 128)) == 0).astype(BF16)
        rows = [_exact_nt(first, jnp.broadcast_to(m_ref[e] + jnp.log(l_ref[e]), (t, 128)))
                for e in range(2)]
        lse_ref[...] = jnp.where(_row((8, t)) == 0, rows[0], rows[1])

    return pl.pallas_call(
        body, name="attn_fwd", grid=(n_pairs, s // t),
        in_specs=[pl.BlockSpec((2, t, 128), lambda g, i: (g, i, 0)),
                  pl.BlockSpec((2, s, 128), lambda g, i: (g, 0, 0)),
                  pl.BlockSpec((s, 128), lambda g, i: (0, v_block0 + g))],
        out_specs=[pl.BlockSpec((t, 128), lambda g, i: (i, g)),
                   pl.BlockSpec((None, 8, t), lambda g, i: (g, 0, i))],
        out_shape=[jax.ShapeDtypeStruct((s, a), BF16),
                   jax.ShapeDtypeStruct((n_pairs, 8, s), F32)],
        scratch_shapes=[pltpu.VMEM((2, t, 1), F32), pltpu.VMEM((2, t, 1), F32),
                        pltpu.VMEM((2, t, 128), F32)],
        compiler_params=_params(("parallel", "arbitrary")),
    )(qa, ka, qkv)


def _conv_out(o, bcx, cw, w_out, x):
    s, d = x.shape
    c = o.shape[1]
    tm = min(TILE_ROWS, s)

    def body(o_ref, b_ref, c_ref, xin_ref, cw_ref, w_ref, x_ref, h_ref, ubuf):
        i = pl.program_id(0)

        @pl.when(i == 0)
        def _():
            ubuf[0:CONV_HALO, :] = jnp.zeros((CONV_HALO, c), F32)

        u = c_ref[...] * xin_ref[...]
        ubuf[CONV_HALO:CONV_HALO + tm, :] = u
        u1 = ubuf[CONV_HALO - 1:CONV_HALO - 1 + tm, :]
        u2 = ubuf[CONV_HALO - 2:CONV_HALO - 2 + tm, :]
        cv = (cw_ref[0:1, :] * u2 + cw_ref[1:2, :] * u1) + cw_ref[2:3, :] * u
        y = (b_ref[...] * cv).astype(BF16)
        mix = _nn(o_ref[...], w_ref[0:c, :]) + _nn(y, w_ref[c:2 * c, :])
        h_ref[...] = x_ref[...] + mix
        ubuf[0:CONV_HALO, :] = u[tm - CONV_HALO:tm, :]

    col = lambda k: pl.BlockSpec((tm, c), lambda i: (i, k))
    return pl.pallas_call(
        body, name="conv_out", grid=(s // tm,),
        in_specs=[col(0), col(0), col(1), col(2), _full(cw), _full(w_out),
                  pl.BlockSpec((tm, d), lambda i: (i, 0))],
        out_specs=pl.BlockSpec((tm, d), lambda i: (i, 0)),
        out_shape=jax.ShapeDtypeStruct((s, d), F32),
        scratch_shapes=[pltpu.VMEM((tm + CONV_HALO, c), F32)],
        compiler_params=_params(("arbitrary",)),
    )(o, bcx, bcx, bcx, cw, w_out, x)


def _mlp_fwd(h, g, w_up, w_down, name):
    s, d = h.shape
    ff = w_down.shape[0]
    slot_cols = w_up.shape[2]
    tm = min(TILE_MLP_ROWS, s)
    tf = min(TILE_MLP_FF, slot_cols)
    per_slot = slot_cols // tf
    nf = ff // tf

    def body(h_ref, g_ref, wu_ref, wd_ref, out_ref, a_ref, n_ref, nb_ref, acc_ref):
        f = pl.program_id(1)

        @pl.when(f == 0)
        def _():
            n, _ = _rms_fwd(h_ref[...], g_ref[...])
            nb = n.astype(BF16)
            nb_ref[...] = nb
            n_ref[...] = nb
            acc_ref[...] = jnp.zeros_like(acc_ref)

        pre = _nn(nb_ref[...], wu_ref[...])
        a_ref[...] = pre.astype(BF16)
        r = jnp.square(jnp.maximum(pre, 0.0)).astype(BF16)
        acc_ref[...] += _nn(r, wd_ref[...])

        @pl.when(f == nf - 1)
        def _():
            out_ref[...] = h_ref[...] + acc_ref[...]

    return pl.pallas_call(
        body, name=name, grid=(s // tm, nf),
        in_specs=[pl.BlockSpec((tm, d), lambda i, f: (i, 0)), _full(g),
                  pl.BlockSpec((None, d, tf), lambda i, f: (f // per_slot, 0, f % per_slot)),
                  pl.BlockSpec((tf, d), lambda i, f: (f, 0))],
        out_specs=[pl.BlockSpec((tm, d), lambda i, f: (i, 0)),
                   pl.BlockSpec((tm, tf), lambda i, f: (i, f)),
                   pl.BlockSpec((tm, d), lambda i, f: (i, 0))],
        out_shape=[jax.ShapeDtypeStruct((s, d), F32),
                   jax.ShapeDtypeStruct((s, ff), BF16),
                   jax.ShapeDtypeStruct((s, d), BF16)],
        scratch_shapes=[pltpu.VMEM((tm, d), BF16), pltpu.VMEM((tm, d), F32)],
        compiler_params=_params(("parallel", "arbitrary")),
    )(h, g, w_up, w_down)


def _window_sum_down(e, window):
    step = 1
    while step < window:
        e = e + pltpu.roll(e, step, axis=0)
        step *= 2
    return e


def _window_sum_up(e, window):
    n = e.shape[0]
    step = 1
    while step < window:
        e = e + pltpu.roll(e, n - step, axis=0)
        step *= 2
    return e


def _pool_counts(first_row, tm, window):
    t = first_row + _row((tm, 1))
    return jnp.minimum(t + 1, window).astype(F32)


def _pool_fwd(h, g, pw, ps):
    s, d = h.shape
    cg = d // len(POOL_WINDOWS)
    tm = min(TILE_ROWS, s)

    def body(h_ref, g_ref, pw_ref, ps_ref, out_ref, nbuf):
        i = pl.program_id(0)

        @pl.when(i == 0)
        def _():
            nbuf[0:POOL_HALO, :] = jnp.zeros((POOL_HALO, d), F32)

        n, _ = _rms_fwd(h_ref[...], g_ref[...])
        nbuf[POOL_HALO:POOL_HALO + tm, :] = n
        for k, window in enumerate(POOL_WINDOWS):
            cols = slice(k * cg, (k + 1) * cg)
            sums = _window_sum_down(nbuf[:, cols], window)[POOL_HALO:, :]
            pooled = sums / _pool_counts(i * tm, tm, window) - n[:, cols]
            y = _nn(pooled.astype(BF16), pw_ref[k]) * ps_ref[:, cols]
            out_ref[:, cols] = h_ref[:, cols] + y
        nbuf[0:POOL_HALO, :] = n[tm - POOL_HALO:tm, :]

    return pl.pallas_call(
        body, name="pool_fwd", grid=(s // tm,),
        in_specs=[pl.BlockSpec((tm, d), lambda i: (i, 0)), _full(g), _full(pw), _full(ps)],
        out_specs=pl.BlockSpec((tm, d), lambda i: (i, 0)),
        out_shape=jax.ShapeDtypeStruct((s, d), F32),
        scratch_shapes=[pltpu.VMEM((tm + POOL_HALO, d), F32)],
        compiler_params=_params(("arbitrary",)),
    )(h, g, pw, ps)


def _final_loss(h, g, target):
    s, d = h.shape
    tm = min(TILE_ROWS, s)

    def body(h_ref, g_ref, t_ref, dh_ref, loss_ref, dg_ref):
        i = pl.program_id(0)
        hv = h_ref[...]
        y, _ = _rms_fwd(hv, g_ref[...])
        err = y - t_ref[...]
        part = 0.5 * jnp.sum(jnp.mean(err * err, axis=-1, keepdims=True), axis=0, keepdims=True)
        dx, dg = _rms_bwd(err / d, hv, g_ref[...])
        dh_ref[...] = dx
        part = jnp.broadcast_to(part, loss_ref.shape)

        @pl.when(i == 0)
        def _():
            loss_ref[...] = part
            dg_ref[...] = dg

        @pl.when(i > 0)
        def _():
            loss_ref[...] += part
            dg_ref[...] += dg

    return pl.pallas_call(
        body, name="final_loss", grid=(s // tm,),
        in_specs=[pl.BlockSpec((tm, d), lambda i: (i, 0)), _full(g),
                  pl.BlockSpec((tm, d), lambda i: (i, 0))],
        out_specs=[pl.BlockSpec((tm, d), lambda i: (i, 0)),
                   pl.BlockSpec((1, 128), lambda i: (0, 0)),
                   pl.BlockSpec((1, d), lambda i: (0, 0))],
        out_shape=[jax.ShapeDtypeStruct((s, d), F32),
                   jax.ShapeDtypeStruct((1, 128), F32),
                   jax.ShapeDtypeStruct((1, d), F32)],
        compiler_params=_params(("arbitrary",)),
    )(h, g, target)


def _mlp_bwd_x(dz, a, w_up, w_down, h_in, g, name):
    s, d = dz.shape
    ff = w_down.shape[0]
    slot_cols = w_up.shape[2]
    tm = min(TILE_MLP_ROWS, s)
    tf = min(TILE_MLP_FF, slot_cols)
    per_slot = slot_cols // tf
    nf = ff // tf

    def body(dz_ref, a_ref, wu_ref, wd_ref, h_ref, g_ref, da_ref, dzb_ref, dh_ref, dg_ref,
             dzs_ref, acc_ref):
        i = pl.program_id(0)
        f = pl.program_id(1)

        @pl.when(f == 0)
        def _():
            dzb = dz_ref[...].astype(BF16)
            dzs_ref[...] = dzb
            dzb_ref[...] = dzb
            acc_ref[...] = jnp.zeros_like(acc_ref)

        dr = _nt(dzs_ref[...], wd_ref[...])
        da = (dr * (2.0 * jnp.maximum(a_ref[...].astype(F32), 0.0))).astype(BF16)
        da_ref[...] = da
        acc_ref[...] += _nt(da, wu_ref[...])

        @pl.when(f == nf - 1)
        def _():
            dx, dg = _rms_bwd(acc_ref[...], h_ref[...], g_ref[...])
            dh_ref[...] = dz_ref[...] + dx

            @pl.when(i == 0)
            def _():
                dg_ref[...] = dg

            @pl.when(i > 0)
            def _():
                dg_ref[...] += dg

    return pl.pallas_call(
        body, name=name, grid=(s // tm, nf),
        in_specs=[pl.BlockSpec((tm, d), lambda i, f: (i, 0)),
                  pl.BlockSpec((tm, tf), lambda i, f: (i, f)),
                  pl.BlockSpec((None, d, tf), lambda i, f: (f // per_slot, 0, f % per_slot)),
                  pl.BlockSpec((tf, d), lambda i, f: (f, 0)),
                  pl.BlockSpec((tm, d), lambda i, f: (i, 0)), _full(g)],
        out_specs=[pl.BlockSpec((tm, tf), lambda i, f: (i, f)),
                   pl.BlockSpec((tm, d), lambda i, f: (i, 0)),
                   pl.BlockSpec((tm, d), lambda i, f: (i, 0)),
                   pl.BlockSpec((1, d), lambda i, f: (0, 0))],
        out_shape=[jax.ShapeDtypeStruct((s, ff), BF16),
                   jax.ShapeDtypeStruct((s, d), BF16),
                   jax.ShapeDtypeStruct((s, d), F32),
                   jax.ShapeDtypeStruct((1, d), F32)],
        scratch_shapes=[pltpu.VMEM((tm, d), BF16), pltpu.VMEM((tm, d), F32)],
        compiler_params=_params(("arbitrary", "arbitrary")),
    )(dz, a, w_up, w_down, h_in, g)


def _mlp_bwd_w(n, da, a, dzb, slot_cols, name):
    s, d = n.shape
    ff = a.shape[1]
    tn = min(TILE_WGRAD_N, slot_cols)
    tk = min(TILE_WGRAD_K, s)
    per_slot = slot_cols // tn
    nk = s // tk

    def body(n_ref, da_ref, a_ref, dz_ref, du_ref, dd_ref, accu_ref, accd_ref):
        k = pl.program_id(1)

        @pl.when(k == 0)
        def _():
            accu_ref[...] = jnp.zeros_like(accu_ref)
            accd_ref[...] = jnp.zeros_like(accd_ref)

        accu_ref[...] += _tn(n_ref[...], da_ref[...])
        r = jnp.square(jnp.maximum(a_ref[...].astype(F32), 0.0)).astype(BF16)
        accd_ref[...] += _tn(r, dz_ref[...])

        @pl.when(k == nk - 1)
        def _():
            du_ref[...] = accu_ref[...].astype(BF16)
            dd_ref[...] = accd_ref[...].astype(BF16)

    return pl.pallas_call(
        body, name=name, grid=(ff // tn, nk),
        in_specs=[pl.BlockSpec((tk, d), lambda f, k: (k, 0)),
                  pl.BlockSpec((tk, tn), lambda f, k: (k, f)),
                  pl.BlockSpec((tk, tn), lambda f, k: (k, f)),
                  pl.BlockSpec((tk, d), lambda f, k: (k, 0))],
        out_specs=[pl.BlockSpec((None, d, tn), lambda f, k: (f // per_slot, 0, f % per_slot)),
                   pl.BlockSpec((tn, d), lambda f, k: (f, 0))],
        out_shape=[jax.ShapeDtypeStruct((ff // slot_cols, d, slot_cols), BF16),
                   jax.ShapeDtypeStruct((ff, d), BF16)],
        scratch_shapes=[pltpu.VMEM((d, tn), F32), pltpu.VMEM((tn, d), F32)],
        compiler_params=_params(("parallel", "arbitrary")),
    )(n, da, a, dzb)


def _pool_bwd(dh, h, g, pw, ps):
    s, d = h.shape
    cg = d // len(POOL_WINDOWS)
    tm = min(TILE_ROWS, s)
    nb = s // tm
    halo_per_tile = tm // POOL_HALO

    def body(dh_ref, h_ref, halo_ref, g_ref, pw_ref, ps_ref,
             dx_ref, dpw_ref, dps_ref, dg_ref, nbuf, qbuf, dn_ref, carry, dpw_acc):
        i = pl.program_id(0)
        blk = nb - 1 - i

        @pl.when(i == 0)
        def _():
            carry[...] = jnp.zeros_like(carry)
            dpw_acc[...] = jnp.zeros_like(dpw_acc)
            dps_ref[...] = jnp.zeros_like(dps_ref)
            dg_ref[...] = jnp.zeros_like(dg_ref)

        hv = h_ref[...]
        n, _ = _rms_fwd(hv, g_ref[...])
        nh, _ = _rms_fwd(halo_ref[...], g_ref[...])
        nbuf[0:POOL_HALO, :] = jnp.where(blk == 0, 0.0, nh)
        nbuf[POOL_HALO:POOL_HALO + tm, :] = n
        dhv = dh_ref[...]
        for k, window in enumerate(POOL_WINDOWS):
            cols = slice(k * cg, (k + 1) * cg)
            cnt = _pool_counts(blk * tm, tm, window)
            sums = _window_sum_down(nbuf[:, cols], window)[POOL_HALO:, :]
            pb = (sums / cnt - n[:, cols]).astype(BF16)
            dyk = dhv[:, cols]
            dps_ref[:, cols] += jnp.sum(dyk * _nn(pb, pw_ref[k]), axis=0, keepdims=True)
            dyb = (dyk * ps_ref[:, cols]).astype(BF16)
            dpw_acc[k] += _tn(pb, dyb)
            dpool = _nt(dyb, pw_ref[k])
            qv = dpool / cnt
            qbuf[0:tm, cols] = qv
            qbuf[tm:tm + POOL_HALO, cols] = carry[:, cols]
            dn_ref[:, cols] = _window_sum_up(qbuf[:, cols], window)[0:tm, :] - dpool
            carry[:, cols] = qv[0:POOL_HALO, :]
        dx, dg = _rms_bwd(dn_ref[...], hv, g_ref[...])
        dx_ref[...] = dhv + dx
        dg_ref[...] += dg

        @pl.when(i == nb - 1)
        def _():
            dpw_ref[...] = dpw_acc[...].astype(BF16)

    rev = lambda i: (nb - 1 - i, 0)
    return pl.pallas_call(
        body, name="pool_bwd", grid=(nb,),
        in_specs=[pl.BlockSpec((tm, d), rev), pl.BlockSpec((tm, d), rev),
                  pl.BlockSpec((POOL_HALO, d),
                               lambda i: (jnp.maximum((nb - 1 - i) * halo_per_tile - 1, 0), 0)),
                  _full(g), _full(pw), _full(ps)],
        out_specs=[pl.BlockSpec((tm, d), rev), _full(pw),
                   pl.BlockSpec((1, d), lambda i: (0, 0)),
                   pl.BlockSpec((1, d), lambda i: (0, 0))],
        out_shape=[jax.ShapeDtypeStruct((s, d), F32),
                   jax.ShapeDtypeStruct(pw.shape, BF16),
                   jax.ShapeDtypeStruct((1, d), F32),
                   jax.ShapeDtypeStruct((1, d), F32)],
        scratch_shapes=[pltpu.VMEM((tm + POOL_HALO, d), F32), pltpu.VMEM((tm + POOL_HALO, d), F32),
                        pltpu.VMEM((tm, d), F32), pltpu.VMEM((POOL_HALO, d), F32),
                        pltpu.VMEM(pw.shape, F32)],
        compiler_params=_params(("arbitrary",)),
    )(dh, h, h, g, pw, ps)


def _conv_out_bwd(dh, w_out, o, bcx, cw):
    s, d = dh.shape
    c = o.shape[1]
    tm = min(TILE_ROWS, s)
    nb = s // tm
    halo_per_tile = tm // CONV_HALO

    def body(dh_ref, w_ref, o_ref, b_ref, c_ref, xin_ref, ch_ref, xh_ref, cw_ref,
             do_ref, delta_ref, dbcx_ref, dw_ref, dcw_ref, ubuf, dbuf, carry, acc):
        i = pl.program_id(0)
        blk = nb - 1 - i

        @pl.when(i == 0)
        def _():
            carry[...] = jnp.zeros_like(carry)
            acc[...] = jnp.zeros_like(acc)
            dcw_ref[...] = jnp.zeros_like(dcw_ref)

        dm = dh_ref[...].astype(BF16)
        dcat = _nt(dm, w_ref[...])
        do = dcat[:, 0:c]
        dy = dcat[:, c:2 * c]
        do_ref[...] = do.astype(BF16)
        head_of_lane = lax.shift_right_logical(_lane((8, c)), HEAD_DIM.bit_length() - 1)
        heads = (head_of_lane == _row((8, c))).astype(BF16)
        delta_ref[...] = _exact_nt(heads, do * o_ref[...].astype(F32))

        cv_ = c_ref[...]
        xin = xin_ref[...]
        bv = b_ref[...]
        u = cv_ * xin
        ubuf[0:CONV_HALO, :] = jnp.where(blk == 0, 0.0, ch_ref[...] * xh_ref[...])
        ubuf[CONV_HALO:CONV_HALO + tm, :] = u
        u1 = ubuf[CONV_HALO - 1:CONV_HALO - 1 + tm, :]
        u2 = ubuf[CONV_HALO - 2:CONV_HALO - 2 + tm, :]
        w0, w1, w2 = cw_ref[0:1, :], cw_ref[1:2, :], cw_ref[2:3, :]
        cv = (w0 * u2 + w1 * u1) + w2 * u
        acc[0:c, :] += _tn(o_ref[...], dm)
        acc[c:2 * c, :] += _tn((bv * cv).astype(BF16), dm)

        dcv = dy * bv
        dcw_ref[0:1, :] += jnp.sum(dcv * u2, axis=0, keepdims=True)
        dcw_ref[1:2, :] += jnp.sum(dcv * u1, axis=0, keepdims=True)
        dcw_ref[2:3, :] += jnp.sum(dcv * u, axis=0, keepdims=True)
        dbuf[0:tm, :] = dcv
        dbuf[tm:tm + CONV_HALO, :] = carry[...]
        du = w2 * dcv + w1 * dbuf[1:1 + tm, :] + w0 * dbuf[2:2 + tm, :]
        dbcx_ref[:, 0:c] = (dy * cv).astype(BF16)
        dbcx_ref[:, c:2 * c] = (du * xin).astype(BF16)
        dbcx_ref[:, 2 * c:3 * c] = (du * cv_).astype(BF16)
        carry[...] = dcv[0:CONV_HALO, :]

        @pl.when(i == nb - 1)
        def _():
            dw_ref[...] = acc[...].astype(BF16)

    rev = lambda k: (lambda i: (nb - 1 - i, k))
    halo = lambda k: (lambda i: (jnp.maximum((nb - 1 - i) * halo_per_tile - 1, 0), k))
    return pl.pallas_call(
        body, name="conv_out_bwd", grid=(nb,),
        in_specs=[pl.BlockSpec((tm, d), rev(0)), _full(w_out), pl.BlockSpec((tm, c), rev(0)),
                  pl.BlockSpec((tm, c), rev(0)), pl.BlockSpec((tm, c), rev(1)),
                  pl.BlockSpec((tm, c), rev(2)),
                  pl.BlockSpec((CONV_HALO, c), halo(1)), pl.BlockSpec((CONV_HALO, c), halo(2)),
                  _full(cw)],
        out_specs=[pl.BlockSpec((tm, c), rev(0)),
                   pl.BlockSpec((8, tm), lambda i: (0, nb - 1 - i)),
                   pl.BlockSpec((tm, 3 * c), rev(0)),
                   _full(w_out), _full(cw)],
        out_shape=[jax.ShapeDtypeStruct((s, c), BF16),
                   jax.ShapeDtypeStruct((8, s), F32),
                   jax.ShapeDtypeStruct((s, 3 * c), BF16),
                   jax.ShapeDtypeStruct(w_out.shape, BF16),
                   jax.ShapeDtypeStruct(cw.shape, F32)],
        scratch_shapes=[pltpu.VMEM((tm + CONV_HALO, c), F32), pltpu.VMEM((tm + CONV_HALO, c), F32),
                        pltpu.VMEM((CONV_HALO, c), F32), pltpu.VMEM(w_out.shape, F32)],
        compiler_params=_params(("arbitrary",)),
    )(dh, w_out, o, bcx, bcx, bcx, bcx, bcx, cw)


def _attn_bwd(qa, ka, qkv, do, lse, delta):
    s = qa.shape[1]
    a = N_HEADS * HEAD_DIM
    t = min(TILE_ATTN, s)
    nq = s // t
    n_pairs = N_HEADS // 2
    v_block0 = 2 * a // 128

    def body(ka_ref, v_ref, qa_ref, do_ref, lse_ref, delta_ref,
             dqa_ref, dka_ref, dv_ref, dk_acc, dv_acc):
        g = pl.program_id(0)
        j = pl.program_id(1)

        @pl.when(j == 0)
        def _():
            dqa_ref[...] = jnp.zeros_like(dqa_ref)

        dk_acc[...] = jnp.zeros_like(dk_acc)
        dv_acc[...] = jnp.zeros_like(dv_acc)
        lane = _lane((t, 128))
        vf = v_ref[...].astype(F32)
        v_heads = [jnp.where(lane < HEAD_DIM, vf, 0.0).astype(BF16),
                   jnp.where(lane >= HEAD_DIM, vf, 0.0).astype(BF16)]

        def q_step(i, masked):
            qs = pl.ds(pl.multiple_of(i * t, t), t)
            dob = do_ref[qs, :]
            for e in range(2):
                ke = ka_ref[e]
                qe = qa_ref[e, qs, :]
                sc = _nt(ke, qe)
                if masked:
                    sc = jnp.where(_row((t, t)) <= _lane((t, t)), sc, NEG_BIG)
                p = jnp.exp(sc - lse_ref[pl.ds(e, 1), qs])
                dv_acc[e] += _nn(p.astype(BF16), dob)
                dp = _nt(v_heads[e], dob)
                ds = (p * (dp - delta_ref[pl.ds(2 * g + e, 1), qs])).astype(BF16)
                dk_acc[e] += _nn(ds, qe)
                dqa_ref[e, qs, :] += _tn(ds, ke)

        q_step(j, True)

        def full_step(i, carry):
            q_step(i, False)
            return carry

        lax.fori_loop(j + 1, nq, full_step, 0)
        dka_ref[...] = dk_acc[...]
        dv_ref[...] = jnp.where(lane < HEAD_DIM, dv_acc[0], dv_acc[1]).astype(BF16)

    return pl.pallas_call(
        body, name="attn_bwd", grid=(n_pairs, nq),
        in_specs=[pl.BlockSpec((2, t, 128), lambda g, j: (g, j, 0)),
                  pl.BlockSpec((t, 128), lambda g, j: (j, v_block0 + g)),
                  pl.BlockSpec((2, s, 128), lambda g, j: (g, 0, 0)),
                  pl.BlockSpec((s, 128), lambda g, j: (0, g)),
                  pl.BlockSpec((None, 8, s), lambda g, j: (g, 0, 0)),
                  pl.BlockSpec((8, s), lambda g, j: (0, 0))],
        out_specs=[pl.BlockSpec((2, s, 128), lambda g, j: (g, 0, 0)),
                   pl.BlockSpec((2, t, 128), lambda g, j: (g, j, 0)),
                   pl.BlockSpec((t, 128), lambda g, j: (j, g))],
        out_shape=[jax.ShapeDtypeStruct((N_HEADS, s, 128), F32),
                   jax.ShapeDtypeStruct((N_HEADS, s, 128), F32),
                   jax.ShapeDtypeStruct((s, a), BF16)],
        scratch_shapes=[pltpu.VMEM((2, t, 128), F32), pltpu.VMEM((2, t, 128), F32)],
        compiler_params=_params(("parallel", "arbitrary")),
    )(ka, qkv, qa, do, lse, delta)


def _gate_bwd(dqa, dka, dv, fl, bf):
    s = fl.shape[0]
    a = N_HEADS * HEAD_DIM
    tm = min(TILE_ROWS, s)
    nb = s // tm

    def body(dqa_ref, dka_ref, dv_ref, fl_ref, bf_ref, dqkv_ref, dfl_ref, dbf_ref, carry):
        i = pl.program_id(0)

        @pl.when(i == 0)
        def _():
            carry[...] = jnp.zeros_like(carry)
            dbf_ref[...] = jnp.zeros_like(dbf_ref)

        lane = _lane((tm, 128))
        dcum = jnp.zeros((tm, 128), F32)
        for pair in range(N_HEADS // 2):
            qs, ks = [], []
            for e in range(2):
                h = 2 * pair + e
                dq = dqa_ref[h]
                dk = dka_ref[h]
                dc = jnp.sum(jnp.where(lane == LANE_CQ, dq, 0.0)
                             - jnp.where(lane == LANE_ONE, dk, 0.0), axis=1, keepdims=True)
                dcum = jnp.where(lane == h, dc, dcum)
                qs.append(dq * ATTN_SCALE)
                ks.append(dk)
            cols = slice(pair * 128, (pair + 1) * 128)
            dqkv_ref[:, cols] = jnp.where(
                lane < HEAD_DIM, qs[0], pltpu.roll(qs[1], HEAD_DIM, axis=1)).astype(BF16)
            dqkv_ref[:, a + pair * 128:a + (pair + 1) * 128] = jnp.where(
                lane < HEAD_DIM, ks[0], pltpu.roll(ks[1], HEAD_DIM, axis=1)).astype(BF16)
        dqkv_ref[:, 2 * a:3 * a] = dv_ref[...]

        upper = (_lane((tm, tm)) >= _row((tm, tm))).astype(BF16)
        dlogf = _exact_nn(upper, dcum) + carry[0:1, :]
        carry[0:1, :] = dlogf[0:1, :]
        z = fl_ref[...] + bf_ref[...]
        ez = jnp.exp(-jnp.abs(z))
        sig_neg = jnp.where(z >= 0.0, ez, 1.0) / (1.0 + ez)
        dz = jnp.where(lane < N_HEADS, dlogf * sig_neg, 0.0)
        dfl_ref[...] = dz.astype(BF16)
        dbf_ref[...] += jnp.sum(dz, axis=0, keepdims=True)

    rev3 = lambda i: (0, nb - 1 - i, 0)
    rev = lambda i: (nb - 1 - i, 0)
    return pl.pallas_call(
        body, name="gate_bwd", grid=(nb,),
        in_specs=[pl.BlockSpec((N_HEADS, tm, 128), rev3), pl.BlockSpec((N_HEADS, tm, 128), rev3),
                  pl.BlockSpec((tm, a), rev), pl.BlockSpec((tm, 128), rev), _full(bf)],
        out_specs=[pl.BlockSpec((tm, 3 * a), rev), pl.BlockSpec((tm, 128), rev),
                   pl.BlockSpec((1, 128), lambda i: (0, 0))],
        out_shape=[jax.ShapeDtypeStruct((s, 3 * a), BF16),
                   jax.ShapeDtypeStruct((s, 128), BF16),
                   jax.ShapeDtypeStruct((1, 128), F32)],
        scratch_shapes=[pltpu.VMEM((8, 128), F32)],
        compiler_params=_params(("arbitrary",)),
    )(dqa, dka, dv, fl, bf)


def _in_proj_bwd(dqkv, dfl, dbcx, w_qkv, w_f, w_bcx, x, g, dh):
    s, d = x.shape
    tm = min(TILE_ROWS, s)

    def body(dq_ref, df_ref, db_ref, wq_ref, wf_ref, wb_ref, x_ref, g_ref, dh_ref, gx_ref, dg_ref):
        i = pl.program_id(0)
        dn = (_nt(dq_ref[...], wq_ref[...]) + _nt(df_ref[...], wf_ref[...])
              + _nt(db_ref[...], wb_ref[...]))
        dx, dg = _rms_bwd(dn, x_ref[...], g_ref[...])
        gx_ref[...] = dh_ref[...] + dx

        @pl.when(i == 0)
        def _():
            dg_ref[...] = dg

        @pl.when(i > 0)
        def _():
            dg_ref[...] += dg

    rows = lambda c: pl.BlockSpec((tm, c), lambda i: (i, 0))
    return pl.pallas_call(
        body, name="in_proj_bwd", grid=(s // tm,),
        in_specs=[rows(dqkv.shape[1]), rows(dfl.shape[1]), rows(dbcx.shape[1]),
                  _full(w_qkv), _full(w_f), _full(w_bcx), rows(d), _full(g), rows(d)],
        out_specs=[rows(d), pl.BlockSpec((1, d), lambda i: (0, 0))],
        out_shape=[jax.ShapeDtypeStruct((s, d), F32), jax.ShapeDtypeStruct((1, d), F32)],
        compiler_params=_params(("arbitrary",)),
    )(dqkv, dfl, dbcx, w_qkv, w_f, w_bcx, x, g, dh)


def _wgrad(n, dy, name):
    s, d = n.shape
    cols = dy.shape[1]
    tn = min(512, cols)
    tk = min(TILE_WGRAD_K, s)
    nk = s // tk

    def body(n_ref, dy_ref, dw_ref, acc):
        k = pl.program_id(1)

        @pl.when(k == 0)
        def _():
            acc[...] = jnp.zeros_like(acc)

        acc[...] += _tn(n_ref[...], dy_ref[...])

        @pl.when(k == nk - 1)
        def _():
            dw_ref[...] = acc[...].astype(BF16)

    return pl.pallas_call(
        body, name=name, grid=(cols // tn, nk),
        in_specs=[pl.BlockSpec((tk, d), lambda f, k: (k, 0)),
                  pl.BlockSpec((tk, tn), lambda f, k: (k, f))],
        out_specs=pl.BlockSpec((d, tn), lambda f, k: (0, f)),
        out_shape=jax.ShapeDtypeStruct((d, cols), BF16),
        scratch_shapes=[pltpu.VMEM((d, tn), F32)],
        compiler_params=_params(("parallel", "arbitrary")),
    )(n, dy)


def _row_tile(rows):
    t = min(TILE_ELEM_ROWS, rows)
    while rows % t:
        t //= 2
    return t


def _sum_pair(a, b, name):
    rows, cols = a.shape
    tr = _row_tile(rows)

    def body(a_ref, b_ref, o_ref):
        o_ref[...] = (a_ref[...].astype(F32) + b_ref[...].astype(F32)).astype(BF16)

    spec = pl.BlockSpec((tr, cols), lambda i: (i, 0))
    return pl.pallas_call(
        body, name=name, grid=(rows // tr,), in_specs=[spec, spec], out_specs=spec,
        out_shape=jax.ShapeDtypeStruct(a.shape, BF16),
        compiler_params=_params(("parallel",)),
    )(a, b)


def _sum_chips(own, others, name):
    rows, cols = own.shape
    tr = _row_tile(rows)

    def body(a_ref, b_ref, o_ref):
        acc = a_ref[...].astype(F32)
        for k in range(N_CHIPS - 1):
            acc = acc + b_ref[k].astype(F32)
        o_ref[...] = acc

    return pl.pallas_call(
        body, name=name, grid=(rows // tr,),
        in_specs=[pl.BlockSpec((tr, cols), lambda i: (i, 0)),
                  pl.BlockSpec((N_CHIPS - 1, tr, cols), lambda i: (0, i, 0))],
        out_specs=pl.BlockSpec((tr, cols), lambda i: (i, 0)),
        out_shape=jax.ShapeDtypeStruct(own.shape, F32),
        compiler_params=_params(("parallel",)),
    )(own, others)


def _adamw_math(w, g, m, v):
    m = ADAM_B1 * m + (1.0 - ADAM_B1) * g
    v = ADAM_B2 * v + (1.0 - ADAM_B2) * jnp.square(g)
    m_hat = m / (1.0 - ADAM_B1 ** ADAM_STEP)
    v_hat = v / (1.0 - ADAM_B2 ** ADAM_STEP)
    delta = -ADAM_LR * (m_hat / (jnp.sqrt(v_hat) + ADAM_EPS) + ADAM_WD * w)
    return delta, m, v


def _adamw(w, g, m, v, name):
    rows, cols = w.shape
    tr = _row_tile(rows)

    def body(w_ref, g_ref, m_ref, v_ref, d_ref, nm_ref, nv_ref):
        delta, nm, nv = _adamw_math(w_ref[...], g_ref[...], m_ref[...], v_ref[...])
        d_ref[...] = delta
        nm_ref[...] = nm
        nv_ref[...] = nv

    spec = pl.BlockSpec((tr, cols), lambda i: (i, 0))
    out = jax.ShapeDtypeStruct(w.shape, F32)
    return pl.pallas_call(
        body, name=name, grid=(rows // tr,), in_specs=[spec] * 4, out_specs=[spec] * 3,
        out_shape=[out, out, out], compiler_params=_params(("parallel",)),
    )(w, g, m, v)


def _sum_devices(parts):
    def body(p_ref, g_ref):
        g = p_ref[0]
        for k in range(1, N_DEV):
            g = g + p_ref[k]
        g_ref[...] = g

    return pl.pallas_call(
        body, name="sum_devices",
        in_specs=[pl.BlockSpec(memory_space=pltpu.VMEM)],
        out_specs=pl.BlockSpec(memory_space=pltpu.VMEM),
        out_shape=jax.ShapeDtypeStruct(parts.shape[1:], F32),
    )(parts)


def _mesh_position():
    x, y, c = lax.axis_index("x"), lax.axis_index("y"), lax.axis_index("c")
    chips = [(1 - x, y), (x, 1 - y), (1 - x, 1 - y)]
    return x, y, c, chips


ANY = pl.BlockSpec(memory_space=pl.ANY)


def _gather_weights(shards, small):
    nl, ns = len(shards), len(small)
    n = nl + ns

    def body(*refs):
        ins, outs = refs[:n], refs[n:2 * n]
        send, recv, local = refs[2 * n:]
        x, y, c, chips = _mesh_position()
        me = 2 * x + y
        sibling = (x, y, 1 - c)

        def remote(src, dst, a, k, to):
            return pltpu.make_async_remote_copy(src_ref=src, dst_ref=dst, send_sem=send.at[a, k],
                                                recv_sem=recv.at[a, k], device_id=to,
                                                device_id_type=MESH)

        locals_, sends = [], []
        for a in range(n):
            cp = pltpu.make_async_copy(ins[a], outs[a].at[me], local.at[a])
            cp.start()
            locals_.append(cp)
        for a in range(n):
            half = ins[a].shape[0] // 2
            part = pl.ds(c * half, half) if a < nl else pl.ds(0, ins[a].shape[0])
            for k, (px, py) in enumerate(chips):
                cp = remote(ins[a].at[part], outs[a].at[me, part], a, k, (px, py, c))
                cp.start()
                sends.append(cp)
        for a in range(n):
            half = ins[a].shape[0] // 2
            part = pl.ds(c * half, half) if a < nl else pl.ds(0, ins[a].shape[0])
            for k, (px, py) in enumerate(chips):
                slot = outs[a].at[2 * px + py, part]
                remote(slot, slot, a, k, (px, py, c)).wait_recv()
                if a < nl:
                    cp = remote(slot, slot, a, 3 + k, sibling)
                    cp.start()
                    sends.append(cp)
        for a in range(nl):
            half = ins[a].shape[0] // 2
            other = pl.ds((1 - c) * half, half)
            for k, (px, py) in enumerate(chips):
                slot = outs[a].at[2 * px + py, other]
                remote(slot, slot, a, 3 + k, sibling).wait_recv()
        for cp in sends:
            cp.wait_send()
        for cp in locals_:
            cp.wait()

    arrays = list(shards) + list(small)
    return pl.pallas_call(
        body, name="gather_weights",
        in_specs=[ANY] * n, out_specs=[ANY] * n,
        out_shape=[jax.ShapeDtypeStruct((N_CHIPS,) + a.shape, a.dtype) for a in arrays],
        scratch_shapes=[pltpu.SemaphoreType.DMA((n, 6)), pltpu.SemaphoreType.DMA((n, 6)),
                        pltpu.SemaphoreType.DMA((n,))],
    )(*arrays)


def _half_shape(a):
    return (a.shape[0], a.shape[1] // 2) + a.shape[2:]


def _exchange_siblings(grads):
    n = len(grads)

    def body(*refs):
        ins, mine, theirs = refs[:n], refs[n:2 * n], refs[2 * n:3 * n]
        send, recv, local = refs[3 * n:]
        x, y, c, _ = _mesh_position()
        cps = []
        for a in range(n):
            half = ins[a].shape[1] // 2
            cp = pltpu.make_async_copy(ins[a].at[:, pl.ds(c * half, half)], mine[a], local.at[a])
            cp.start()
            cps.append(cp)
            rc = pltpu.make_async_remote_copy(
                src_ref=ins[a].at[:, pl.ds((1 - c) * half, half)], dst_ref=theirs[a],
                send_sem=send.at[a], recv_sem=recv.at[a], device_id=(x, y, 1 - c),
                device_id_type=MESH)
            rc.start()
            cps.append(rc)
        for cp in cps:
            cp.wait()

    halves = [jax.ShapeDtypeStruct(_half_shape(a), a.dtype) for a in grads]
    outs = pl.pallas_call(
        body, name="exchange_siblings",
        in_specs=[ANY] * n, out_specs=[ANY] * (2 * n), out_shape=halves + halves,
        scratch_shapes=[pltpu.SemaphoreType.DMA((n,)), pltpu.SemaphoreType.DMA((n,)),
                        pltpu.SemaphoreType.DMA((n,))],
    )(*grads)
    return outs[:n], outs[n:]


def _exchange_chips(sums):
    n = len(sums)

    def body(*refs):
        ins, own, got = refs[:n], refs[n:2 * n], refs[2 * n:3 * n]
        send, recv, local = refs[3 * n:]
        x, y, c, chips = _mesh_position()
        cps = []
        for a in range(n):
            cp = pltpu.make_async_copy(ins[a].at[2 * x + y], own[a], local.at[a])
            cp.start()
            cps.append(cp)
            for k, (px, py) in enumerate(chips):
                rc = pltpu.make_async_remote_copy(
                    src_ref=ins[a].at[2 * px + py], dst_ref=got[a].at[k],
                    send_sem=send.at[a, k], recv_sem=recv.at[a, k], device_id=(px, py, c),
                    device_id_type=MESH)
                rc.start()
                cps.append(rc)
        for cp in cps:
            cp.wait()

    own_shapes = [jax.ShapeDtypeStruct(a.shape[1:], a.dtype) for a in sums]
    got_shapes = [jax.ShapeDtypeStruct((N_CHIPS - 1,) + a.shape[1:], a.dtype) for a in sums]
    outs = pl.pallas_call(
        body, name="exchange_chips",
        in_specs=[ANY] * n, out_specs=[ANY] * (2 * n), out_shape=own_shapes + got_shapes,
        scratch_shapes=[pltpu.SemaphoreType.DMA((n, 3)), pltpu.SemaphoreType.DMA((n, 3)),
                        pltpu.SemaphoreType.DMA((n,))],
    )(*sums)
    return outs[:n], outs[n:]


def _share_halves(halves):
    n = len(halves)

    def body(*refs):
        ins, outs = refs[:n], refs[n:2 * n]
        send, recv, local = refs[2 * n:]
        x, y, c, _ = _mesh_position()
        cps = []
        for a in range(n):
            half = ins[a].shape[0]
            part = outs[a].at[pl.ds(c * half, half)]
            cp = pltpu.make_async_copy(ins[a], part, local.at[a])
            cp.start()
            cps.append(cp)
            rc = pltpu.make_async_remote_copy(
                src_ref=ins[a], dst_ref=part, send_sem=send.at[a], recv_sem=recv.at[a],
                device_id=(x, y, 1 - c), device_id_type=MESH)
            rc.start()
            cps.append(rc)
        for cp in cps:
            cp.wait()

    return pl.pallas_call(
        body, name="share_halves",
        in_specs=[ANY] * n, out_specs=[ANY] * n,
        out_shape=[jax.ShapeDtypeStruct((2 * a.shape[0],) + a.shape[1:], a.dtype)
                   for a in halves],
        scratch_shapes=[pltpu.SemaphoreType.DMA((n,)), pltpu.SemaphoreType.DMA((n,)),
                        pltpu.SemaphoreType.DMA((n,))],
    )(*halves)


def _gather_small(part):
    def body(in_ref, out_ref, send, recv, local):
        x, y, c, _ = _mesh_position()
        me = 4 * x + 2 * y + c
        cps = [pltpu.make_async_copy(in_ref, out_ref.at[me], local)]
        k = 0
        for fx in range(2):
            for fy in range(2):
                for fc in range(2):
                    if fx or fy or fc:
                        cps.append(pltpu.make_async_remote_copy(
                            src_ref=in_ref, dst_ref=out_ref.at[me], send_sem=send.at[k],
                            recv_sem=recv.at[k], device_id=(x ^ fx, y ^ fy, c ^ fc),
                            device_id_type=MESH))
                        k += 1
        for cp in cps:
            cp.start()
        for cp in cps:
            cp.wait()

    return pl.pallas_call(
        body, name="gather_small",
        in_specs=[pl.BlockSpec(memory_space=pltpu.VMEM)],
        out_specs=pl.BlockSpec(memory_space=pltpu.VMEM),
        out_shape=jax.ShapeDtypeStruct((N_DEV,) + part.shape, part.dtype),
        scratch_shapes=[pltpu.SemaphoreType.DMA((N_DEV - 1,)), pltpu.SemaphoreType.DMA((N_DEV - 1,)),
                        pltpu.SemaphoreType.DMA],
    )(part)


def _reduce_scatter(grads):
    n = len(grads)
    mine, theirs = _exchange_siblings(grads)
    flat = lambda a: a.reshape(-1, a.shape[-1])
    chip_sums = [_sum_pair(flat(mine[a]), flat(theirs[a]), "sum_siblings_%d" % a).reshape(mine[a].shape)
                 for a in range(n)]
    own, got = _exchange_chips(chip_sums)
    halves = []
    for a in range(n):
        others = got[a].reshape(N_CHIPS - 1, -1, got[a].shape[-1])
        halves.append(_sum_chips(flat(own[a]), others, "sum_chips_%d" % a).reshape(own[a].shape))
    return _share_halves(halves)


def _pad_rows(a, rows):
    return jnp.pad(a, ((0, rows - a.shape[0]), (0, 0)))


def kernel(x, norm_mix_0, w_in_0, b_f_0, conv_w_0, w_out_0, norm_ffn_0, w_up_0, w_down_0, norm_mix_1, pool_w_1, pool_scale_1, norm_ffn_1, w_up_1, w_down_1, final_norm, loss_target, m_norm_mix_0, m_w_in_0, m_b_f_0, m_conv_w_0, m_w_out_0, m_norm_ffn_0, m_w_up_0, m_w_down_0, m_norm_mix_1, m_pool_w_1, m_pool_scale_1, m_norm_ffn_1, m_w_up_1, m_w_down_1, m_final_norm, v_norm_mix_0, v_w_in_0, v_b_f_0, v_conv_w_0, v_w_out_0, v_norm_ffn_0, v_w_up_0, v_w_down_0, v_norm_mix_1, v_pool_w_1, v_pool_scale_1, v_norm_ffn_1, v_w_up_1, v_w_down_1, v_final_norm):
    d = x.shape[-1]
    a = N_HEADS * HEAD_DIM
    c_conv = conv_w_0.shape[1] * N_CHIPS
    xs = x[0]
    target = loss_target[0]
    row = lambda vec: vec.reshape(1, -1)

    big = [w_in_0, w_out_0, w_up_0, w_down_0, pool_w_1, w_up_1, w_down_1]
    g_in, g_out, g_up0, g_down0, g_pool, g_up1, g_down1, g_conv = _gather_weights(
        [w.astype(BF16) for w in big], [conv_w_0])
    w_in = g_in.transpose(1, 0, 2).reshape(d, -1)
    w_qkv = w_in[:, :3 * a]
    w_f = jnp.pad(w_in[:, 3 * a:3 * a + N_HEADS], ((0, 0), (0, 128 - N_HEADS)))
    w_bcx = w_in[:, 3 * a + N_HEADS:]
    w_out = g_out.reshape(-1, d)
    w_down0 = g_down0.reshape(-1, d)
    w_down1 = g_down1.reshape(-1, d)
    pool_w = g_pool.transpose(1, 0, 2, 3).reshape(pool_w_1.shape[0], -1, pool_w_1.shape[2])
    conv_w = _pad_rows(g_conv.transpose(1, 0, 2).reshape(conv_w_0.shape[0], c_conv), 8)
    bf = jnp.pad(b_f_0, (0, 128 - N_HEADS)).reshape(1, 128)

    n0, qkv, fl, bcx = _ln_proj(xs, row(norm_mix_0), w_qkv, w_f, w_bcx)
    qa, ka = _gate_prep(fl, bf, qkv)
    o, lse = _attn_fwd(qa, ka, qkv)
    h1 = _conv_out(o, bcx, conv_w, w_out, xs)
    h2, a0, nf0 = _mlp_fwd(h1, row(norm_ffn_0), g_up0, w_down0, "mlp_fwd_0")
    h3 = _pool_fwd(h2, row(norm_mix_1), pool_w, row(pool_scale_1))
    h4, a1, nf1 = _mlp_fwd(h3, row(norm_ffn_1), g_up1, w_down1, "mlp_fwd_1")
    dh4, loss_part, d_final = _final_loss(h4, row(final_norm), target)

    slot_cols = g_up0.shape[2]
    da1, dz1, dh3, d_nffn1 = _mlp_bwd_x(dh4, a1, g_up1, w_down1, h3, row(norm_ffn_1), "mlp_bwd_x_1")
    dw_up1, dw_down1 = _mlp_bwd_w(nf1, da1, a1, dz1, slot_cols, "mlp_bwd_w_1")
    dh2, dw_pool, d_pscale, d_nmix1 = _pool_bwd(dh3, h2, row(norm_mix_1), pool_w, row(pool_scale_1))
    da0, dz0, dh1, d_nffn0 = _mlp_bwd_x(dh2, a0, g_up0, w_down0, h1, row(norm_ffn_0), "mlp_bwd_x_0")
    dw_up0, dw_down0 = _mlp_bwd_w(nf0, da0, a0, dz0, slot_cols, "mlp_bwd_w_0")
    do, delta, dbcx, dw_out, d_conv = _conv_out_bwd(dh1, w_out, o, bcx, conv_w)
    dqa, dka, dv = _attn_bwd(qa, ka, qkv, do, lse, delta)
    dqkv, dfl, d_bf = _gate_bwd(dqa, dka, dv, fl, bf)
    grad_x, d_nmix0 = _in_proj_bwd(dqkv, dfl, dbcx, w_qkv, w_f, w_bcx, xs, row(norm_mix_0), dh1)
    dw_qkv = _wgrad(n0, dqkv, "wgrad_qkv")
    dw_f = _wgrad(n0, dfl, "wgrad_f")
    dw_bcx = _wgrad(n0, dbcx, "wgrad_bcx")

    dw_in = jnp.concatenate([dw_qkv, dw_f[:, :N_HEADS], dw_bcx], axis=1)
    partial = [
        dw_in.reshape(d, N_CHIPS, -1).transpose(1, 0, 2),
        dw_out.reshape(N_CHIPS, -1, d),
        dw_up0,
        dw_down0.reshape(N_CHIPS, -1, d),
        dw_pool.reshape(pool_w.shape[0], N_CHIPS, -1, pool_w.shape[2]).transpose(1, 0, 2, 3),
        dw_up1,
        dw_down1.reshape(N_CHIPS, -1, d),
    ]
    reduced = _reduce_scatter(partial)
    moments = [(m_w_in_0, v_w_in_0), (m_w_out_0, v_w_out_0), (m_w_up_0, v_w_up_0),
               (m_w_down_0, v_w_down_0), (m_pool_w_1, v_pool_w_1), (m_w_up_1, v_w_up_1),
               (m_w_down_1, v_w_down_1)]
    big_out = []
    for k, (w, g, (m, v)) in enumerate(zip(big, reduced, moments)):
        flat = lambda t: t.reshape(-1, t.shape[-1])
        delta_w, new_m, new_v = _adamw(flat(w), flat(g), flat(m), flat(v), "adamw_%d" % k)
        big_out.append((g.reshape(w.shape), delta_w.reshape(w.shape), new_m.reshape(w.shape),
                        new_v.reshape(w.shape)))

    tail = jnp.concatenate([d_conv[0:3].reshape(-1)[d:], d_bf[0, :N_HEADS], loss_part[0, :1]])
    small_part = jnp.concatenate(
        [d_nmix0, d_nffn0, d_nmix1, d_pscale, d_nffn1, d_final,
         d_conv[0:3].reshape(1, -1)[:, :d],
         jnp.pad(tail, (0, d - tail.shape[0])).reshape(1, d)], axis=0)
    parts = _gather_small(small_part)

    chip = 2 * lax.axis_index("x") + lax.axis_index("y")
    cw_cols = conv_w_0.shape[1]

    def conv_block(full):
        mine = lax.dynamic_slice_in_dim(full, chip * cw_cols, cw_cols, axis=1)
        return jnp.pad(mine.reshape(-1), (0, d - mine.size))

    def small_rows(vals, cw, bfv):
        return jnp.stack(list(vals) + [cw, jnp.pad(bfv, (0, d - N_HEADS))])

    smalls_w = [norm_mix_0, norm_ffn_0, norm_mix_1, pool_scale_1, norm_ffn_1, final_norm]
    smalls_m = [m_norm_mix_0, m_norm_ffn_0, m_norm_mix_1, m_pool_scale_1, m_norm_ffn_1, m_final_norm]
    smalls_v = [v_norm_mix_0, v_norm_ffn_0, v_norm_mix_1, v_pool_scale_1, v_norm_ffn_1, v_final_norm]
    pad_cw = lambda t: jnp.pad(t.reshape(-1), (0, d - t.size))
    w_rows = small_rows(smalls_w, pad_cw(conv_w_0), b_f_0)
    m_rows = small_rows(smalls_m, pad_cw(m_conv_w_0), m_b_f_0)
    v_rows = small_rows(smalls_v, pad_cw(v_conv_w_0), v_b_f_0)

    g_sum = _sum_devices(parts)
    conv_full = jnp.concatenate([g_sum[6], g_sum[7, :3 * c_conv - d]]).reshape(3, c_conv)
    bf_grad = g_sum[7, 3 * c_conv - d:3 * c_conv - d + N_HEADS]
    loss = g_sum[7, 3 * c_conv - d + N_HEADS]
    g_rows = jnp.concatenate(
        [g_sum[0:6], conv_block(conv_full).reshape(1, d),
         jnp.pad(bf_grad, (0, d - N_HEADS)).reshape(1, d)], axis=0)
    d_rows, nm_rows, nv_rows = _adamw(w_rows, g_rows, m_rows, v_rows, "adamw_small")

    def unpack(rows):
        cw = rows[6, :conv_w_0.size].reshape(conv_w_0.shape)
        return [rows[0], rows[1], rows[2], rows[3], rows[4], rows[5], cw, rows[7, :N_HEADS]]

    def assemble(kind):
        sm = unpack([g_rows, d_rows, nm_rows, nv_rows][kind])
        lg = [t[kind] for t in big_out]
        return [sm[0], lg[0], sm[7], sm[6], lg[1], sm[1], lg[2], lg[3],
                sm[2], lg[4], sm[3], sm[4], lg[5], lg[6], sm[5]]

    return (loss, grad_x[None], *assemble(0), *assemble(1), *assemble(2), *assemble(3))
```

```python
import functools

import jax
import jax.numpy as jnp
from jax import lax
from jax.experimental import pallas as pl
from jax.experimental.pallas import tpu as pltpu

F32 = jnp.float32
BF16 = jnp.bfloat16

RMS_EPS = 1e-6
HEAD_DIM = 64
N_HEADS = 8
ATTN_SCALE = HEAD_DIM ** -0.5
POOL_WINDOWS = (2, 4, 8, 16)
POOL_HALO = 16
CONV_HALO = 8
NEG_BIG = -1e30

ADAM_LR = 0.001
ADAM_B1 = 0.9
ADAM_B2 = 0.999
ADAM_EPS = 1e-08
ADAM_WD = 0.01
ADAM_STEP = 10

N_CHIPS = 4
N_DEV = 8
MESH = pl.DeviceIdType.MESH

VMEM_LIMIT_BYTES = 56 * 1024 * 1024

TILE_ROWS = 512
TILE_ATTN = 512
TILE_MLP_ROWS = 1024
TILE_MLP_FF = 512
TILE_WGRAD_K = 512
TILE_WGRAD_N = 1024
TILE_ELEM_ROWS = 256

LANE_CQ = 64
LANE_ONE = 67


def _params(semantics):
    return pltpu.CompilerParams(dimension_semantics=semantics,
                                vmem_limit_bytes=VMEM_LIMIT_BYTES)


def _nn(a, b):
    return lax.dot_general(a, b, (((1,), (0,)), ((), ())), preferred_element_type=F32)


def _nt(a, b):
    return lax.dot_general(a, b, (((1,), (1,)), ((), ())), preferred_element_type=F32)


def _tn(a, b):
    return lax.dot_general(a, b, (((0,), (0,)), ((), ())), preferred_element_type=F32)


def _split3(v):
    hi = v.astype(BF16)
    r1 = v - hi.astype(F32)
    mid = r1.astype(BF16)
    lo = (r1 - mid.astype(F32)).astype(BF16)
    return hi, mid, lo


def _exact_nn(sel, v):
    hi, mid, lo = _split3(v)
    return _nn(sel, hi) + _nn(sel, mid) + _nn(sel, lo)


def _exact_nt(sel, v):
    hi, mid, lo = _split3(v)
    return _nt(sel, hi) + _nt(sel, mid) + _nt(sel, lo)


def _rms_fwd(x, g):
    r = lax.rsqrt(jnp.mean(x * x, axis=-1, keepdims=True) + RMS_EPS)
    return x * r * g, r


def _rms_bwd(dn, x, g):
    r = lax.rsqrt(jnp.mean(x * x, axis=-1, keepdims=True) + RMS_EPS)
    xh = x * r
    gy = dn * g
    dx = r * (gy - xh * jnp.mean(gy * xh, axis=-1, keepdims=True))
    return dx, jnp.sum(dn * xh, axis=0, keepdims=True)


def _lane(shape):
    return lax.broadcasted_iota(jnp.int32, shape, len(shape) - 1)


def _row(shape):
    return lax.broadcasted_iota(jnp.int32, shape, len(shape) - 2)


def _full(a):
    nd = a.ndim
    return pl.BlockSpec(a.shape, lambda *_: (0,) * nd)


def _ln_proj(x, g, w_qkv, w_f, w_bcx):
    s, d = x.shape
    tm = min(TILE_ROWS, s)

    def body(x_ref, g_ref, wq_ref, wf_ref, wb_ref, n_ref, qkv_ref, fl_ref, bcx_ref):
        n, _ = _rms_fwd(x_ref[...], g_ref[...])
        nb = n.astype(BF16)
        n_ref[...] = nb
        qkv_ref[...] = _nn(nb, wq_ref[...]).astype(BF16)
        fl_ref[...] = _nn(nb, wf_ref[...])
        bcx_ref[...] = _nn(nb, wb_ref[...])

    rows = lambda c: pl.BlockSpec((tm, c), lambda i: (i, 0))
    return pl.pallas_call(
        body, name="ln_proj", grid=(s // tm,),
        in_specs=[rows(d), _full(g), _full(w_qkv), _full(w_f), _full(w_bcx)],
        out_specs=[rows(d), rows(w_qkv.shape[1]), rows(w_f.shape[1]), rows(w_bcx.shape[1])],
        out_shape=[jax.ShapeDtypeStruct((s, d), BF16),
                   jax.ShapeDtypeStruct((s, w_qkv.shape[1]), BF16),
                   jax.ShapeDtypeStruct((s, w_f.shape[1]), F32),
                   jax.ShapeDtypeStruct((s, w_bcx.shape[1]), F32)],
        compiler_params=_params(("parallel",)),
    )(x, g, w_qkv, w_f, w_bcx)


def _gate_prep(fl, bf, qkv):
    s = fl.shape[0]
    a = N_HEADS * HEAD_DIM
    tm = min(TILE_ROWS, s)

    def body(fl_ref, bf_ref, q_ref, k_ref, qa_ref, ka_ref, carry_ref):
        i = pl.program_id(0)

        @pl.when(i == 0)
        def _():
            carry_ref[...] = jnp.zeros_like(carry_ref)

        z = fl_ref[...] + bf_ref[...]
        logf = jnp.minimum(z, 0.0) - jnp.log(1.0 + jnp.exp(-jnp.abs(z)))
        lower = (_lane((tm, tm)) <= _row((tm, tm))).astype(BF16)
        cum = _exact_nn(lower, logf) + carry_ref[0:1, :]
        carry_ref[0:1, :] = cum[tm - 1:tm, :]

        lane = _lane((tm, 128))
        for h in range(N_HEADS):
            cb = jnp.sum(jnp.where(lane == h, cum, 0.0), axis=1, keepdims=True)
            hi, mid, lo = (p.astype(F32) for p in _split3(cb))
            pair = slice((h // 2) * 128, (h // 2 + 1) * 128)
            qp = q_ref[:, pair].astype(F32)
            kp = k_ref[:, pair].astype(F32)
            if h % 2:
                qp = pltpu.roll(qp, HEAD_DIM, axis=1)
                kp = pltpu.roll(kp, HEAD_DIM, axis=1)
            q_bias = jnp.where(lane == LANE_CQ, hi,
                               jnp.where(lane == LANE_CQ + 1, mid,
                                         jnp.where(lane == LANE_CQ + 2, lo,
                                                   jnp.where(lane < LANE_ONE + 3, 1.0, 0.0))))
            k_bias = jnp.where(lane < LANE_ONE, 1.0,
                               jnp.where(lane == LANE_ONE, -hi,
                                         jnp.where(lane == LANE_ONE + 1, -mid,
                                                   jnp.where(lane == LANE_ONE + 2, -lo, 0.0))))
            qa_ref[h] = jnp.where(lane < HEAD_DIM, qp * ATTN_SCALE, q_bias).astype(BF16)
            ka_ref[h] = jnp.where(lane < HEAD_DIM, kp, k_bias).astype(BF16)

    aug = jax.ShapeDtypeStruct((N_HEADS, s, 128), BF16)
    aug_spec = pl.BlockSpec((N_HEADS, tm, 128), lambda i: (0, i, 0))
    return pl.pallas_call(
        body, name="gate_prep", grid=(s // tm,),
        in_specs=[pl.BlockSpec((tm, 128), lambda i: (i, 0)), _full(bf),
                  pl.BlockSpec((tm, a), lambda i: (i, 0)),
                  pl.BlockSpec((tm, a), lambda i: (i, 1))],
        out_specs=[aug_spec, aug_spec],
        out_shape=[aug, aug],
        scratch_shapes=[pltpu.VMEM((8, 128), F32)],
        compiler_params=_params(("arbitrary",)),
    )(fl, bf, qkv, qkv)


def _attn_fwd(qa, ka, qkv):
    s = qa.shape[1]
    a = N_HEADS * HEAD_DIM
    t = min(TILE_ATTN, s)
    n_pairs = N_HEADS // 2
    v_block0 = 2 * a // 128

    def body(qa_ref, ka_ref, v_ref, o_ref, lse_ref, m_ref, l_ref, acc_ref):
        i = pl.program_id(1)
        m_ref[...] = jnp.full_like(m_ref, NEG_BIG)
        l_ref[...] = jnp.zeros_like(l_ref)
        acc_ref[...] = jnp.zeros_like(acc_ref)
        upper_rows = _row((128, t)) < HEAD_DIM

        def kv_step(j, masked):
            ks = pl.ds(pl.multiple_of(j * t, t), t)
            vf = v_ref[ks, :].astype(F32)
            lane = _lane((t, 128))
            v_heads = [jnp.where(lane < HEAD_DIM, vf, 0.0).astype(BF16),
                       jnp.where(lane >= HEAD_DIM, vf, 0.0).astype(BF16)]
            alphas, update = [], None
            for e in range(2):
                sc = _nt(ka_ref[e, ks, :], qa_ref[e])
                if masked:
                    sc = jnp.where(_row((t, t)) <= _lane((t, t)), sc, NEG_BIG)
                m_prev = m_ref[e]
                m_new = jnp.maximum(m_prev, jnp.max(sc, axis=0, keepdims=True))
                p = jnp.exp(sc - m_new)
                alpha = jnp.exp(m_prev - m_new)
                l_ref[e] = alpha * l_ref[e] + jnp.sum(p, axis=0, keepdims=True)
                m_ref[e] = m_new
                alphas.append(alpha)
                pv = _tn(v_heads[e], p.astype(BF16))
                update = pv if update is None else update + pv
            acc_ref[...] = acc_ref[...] * jnp.where(upper_rows, alphas[0], alphas[1]) + update

        def full_step(j, carry):
            kv_step(j, False)
            return carry

        lax.fori_loop(0, i, full_step, 0)
        kv_step(i, True)

        out_t = acc_ref[...] / jnp.where(upper_rows, l_ref[0], l_ref[1])
        o_ref[...] = out_t.T.astype(BF16)
        lse = [m_ref[e] + jnp.log(l_ref[e]) for e in range(2)]
        lse_ref[...] = jnp.where(_row((8, t)) == 0, lse[0], lse[1])

    return pl.pallas_call(
        body, name="attn_fwd", grid=(n_pairs, s // t),
        in_specs=[pl.BlockSpec((2, t, 128), lambda g, i: (g, i, 0)),
                  pl.BlockSpec((2, s, 128), lambda g, i: (g, 0, 0)),
                  pl.BlockSpec((s, 128), lambda g, i: (0, v_block0 + g))],
        out_specs=[pl.BlockSpec((t, 128), lambda g, i: (i, g)),
                   pl.BlockSpec((None, 8, t), lambda g, i: (g, 0, i))],
        out_shape=[jax.ShapeDtypeStruct((s, a), BF16),
                   jax.ShapeDtypeStruct((n_pairs, 8, s), F32)],
        scratch_shapes=[pltpu.VMEM((2, 1, t), F32), pltpu.VMEM((2, 1, t), F32),
                        pltpu.VMEM((128, t), F32)],
        compiler_params=_params(("parallel", "arbitrary")),
    )(qa, ka, qkv)


def _conv_out(o, bcx, cw, w_out, x):
    s, d = x.shape
    c = o.shape[1]
    tm = min(TILE_ROWS, s)

    def body(o_ref, b_ref, c_ref, xin_ref, cw_ref, w_ref, x_ref, h_ref, ubuf):
        i = pl.program_id(0)

        @pl.when(i == 0)
        def _():
            ubuf[0:CONV_HALO, :] = jnp.zeros((CONV_HALO, c), F32)

        u = c_ref[...] * xin_ref[...]
        ubuf[CONV_HALO:CONV_HALO + tm, :] = u
        u1 = ubuf[CONV_HALO - 1:CONV_HALO - 1 + tm, :]
        u2 = ubuf[CONV_HALO - 2:CONV_HALO - 2 + tm, :]
        cv = (cw_ref[0:1, :] * u2 + cw_ref[1:2, :] * u1) + cw_ref[2:3, :] * u
        y = (b_ref[...] * cv).astype(BF16)
        mix = _nn(o_ref[...], w_ref[0:c, :]) + _nn(y, w_ref[c:2 * c, :])
        h_ref[...] = x_ref[...] + mix
        ubuf[0:CONV_HALO, :] = u[tm - CONV_HALO:tm, :]

    col = lambda k: pl.BlockSpec((tm, c), lambda i: (i, k))
    return pl.pallas_call(
        body, name="conv_out", grid=(s // tm,),
        in_specs=[col(0), col(0), col(1), col(2), _full(cw), _full(w_out),
                  pl.BlockSpec((tm, d), lambda i: (i, 0))],
        out_specs=pl.BlockSpec((tm, d), lambda i: (i, 0)),
        out_shape=jax.ShapeDtypeStruct((s, d), F32),
        scratch_shapes=[pltpu.VMEM((tm + CONV_HALO, c), F32)],
        compiler_params=_params(("arbitrary",)),
    )(o, bcx, bcx, bcx, cw, w_out, x)


def _mlp_fwd(h, g, w_up, w_down, name):
    s, d = h.shape
    ff = w_down.shape[0]
    slot_cols = w_up.shape[2]
    tm = min(TILE_MLP_ROWS, s)
    tf = min(TILE_MLP_FF, slot_cols)
    per_slot = slot_cols // tf
    nf = ff // tf

    def body(h_ref, g_ref, wu_ref, wd_ref, out_ref, a_ref, n_ref, nb_ref, acc_ref):
        f = pl.program_id(1)

        @pl.when(f == 0)
        def _():
            n, _ = _rms_fwd(h_ref[...], g_ref[...])
            nb = n.astype(BF16)
            nb_ref[...] = nb
            n_ref[...] = nb
            acc_ref[...] = jnp.zeros_like(acc_ref)

        pre = _nn(nb_ref[...], wu_ref[...])
        a_ref[...] = pre.astype(BF16)
        r = jnp.square(jnp.maximum(pre, 0.0)).astype(BF16)
        acc_ref[...] += _nn(r, wd_ref[...])

        @pl.when(f == nf - 1)
        def _():
            out_ref[...] = h_ref[...] + acc_ref[...]

    return pl.pallas_call(
        body, name=name, grid=(s // tm, nf),
        in_specs=[pl.BlockSpec((tm, d), lambda i, f: (i, 0)), _full(g),
                  pl.BlockSpec((None, d, tf), lambda i, f: (f // per_slot, 0, f % per_slot)),
                  pl.BlockSpec((tf, d), lambda i, f: (f, 0))],
        out_specs=[pl.BlockSpec((tm, d), lambda i, f: (i, 0)),
                   pl.BlockSpec((tm, tf), lambda i, f: (i, f)),
                   pl.BlockSpec((tm, d), lambda i, f: (i, 0))],
        out_shape=[jax.ShapeDtypeStruct((s, d), F32),
                   jax.ShapeDtypeStruct((s, ff), BF16),
                   jax.ShapeDtypeStruct((s, d), BF16)],
        scratch_shapes=[pltpu.VMEM((tm, d), BF16), pltpu.VMEM((tm, d), F32)],
        compiler_params=_params(("parallel", "arbitrary")),
    )(h, g, w_up, w_down)


def _window_sum_down(e, window):
    step = 1
    while step < window:
        e = e + pltpu.roll(e, step, axis=0)
        step *= 2
    return e


def _window_sum_up(e, window):
    n = e.shape[0]
    step = 1
    while step < window:
        e = e + pltpu.roll(e, n - step, axis=0)
        step *= 2
    return e


def _pool_counts(first_row, tm, window):
    t = first_row + _row((tm, 1))
    return jnp.minimum(t + 1, window).astype(F32)


def _pool_fwd(h, g, pw, ps):
    s, d = h.shape
    cg = d // len(POOL_WINDOWS)
    tm = min(TILE_ROWS, s)

    def body(h_ref, g_ref, pw_ref, ps_ref, out_ref, nbuf):
        i = pl.program_id(0)

        @pl.when(i == 0)
        def _():
            nbuf[0:POOL_HALO, :] = jnp.zeros((POOL_HALO, d), F32)

        n, _ = _rms_fwd(h_ref[...], g_ref[...])
        nbuf[POOL_HALO:POOL_HALO + tm, :] = n
        for k, window in enumerate(POOL_WINDOWS):
            cols = slice(k * cg, (k + 1) * cg)
            sums = _window_sum_down(nbuf[:, cols], window)[POOL_HALO:, :]
            pooled = sums / _pool_counts(i * tm, tm, window) - n[:, cols]
            y = _nn(pooled.astype(BF16), pw_ref[k]) * ps_ref[:, cols]
            out_ref[:, cols] = h_ref[:, cols] + y
        nbuf[0:POOL_HALO, :] = n[tm - POOL_HALO:tm, :]

    return pl.pallas_call(
        body, name="pool_fwd", grid=(s // tm,),
        in_specs=[pl.BlockSpec((tm, d), lambda i: (i, 0)), _full(g), _full(pw), _full(ps)],
        out_specs=pl.BlockSpec((tm, d), lambda i: (i, 0)),
        out_shape=jax.ShapeDtypeStruct((s, d), F32),
        scratch_shapes=[pltpu.VMEM((tm + POOL_HALO, d), F32)],
        compiler_params=_params(("arbitrary",)),
    )(h, g, pw, ps)


def _final_loss(h, g, target):
    s, d = h.shape
    tm = min(TILE_ROWS, s)

    def body(h_ref, g_ref, t_ref, dh_ref, loss_ref, dg_ref):
        i = pl.program_id(0)
        hv = h_ref[...]
        y, _ = _rms_fwd(hv, g_ref[...])
        err = y - t_ref[...]
        part = 0.5 * jnp.sum(jnp.mean(err * err, axis=-1, keepdims=True), axis=0, keepdims=True)
        dx, dg = _rms_bwd(err / d, hv, g_ref[...])
        dh_ref[...] = dx
        part = jnp.broadcast_to(part, loss_ref.shape)

        @pl.when(i == 0)
        def _():
            loss_ref[...] = part
            dg_ref[...] = dg

        @pl.when(i > 0)
        def _():
            loss_ref[...] += part
            dg_ref[...] += dg

    return pl.pallas_call(
        body, name="final_loss", grid=(s // tm,),
        in_specs=[pl.BlockSpec((tm, d), lambda i: (i, 0)), _full(g),
                  pl.BlockSpec((tm, d), lambda i: (i, 0))],
        out_specs=[pl.BlockSpec((tm, d), lambda i: (i, 0)),
                   pl.BlockSpec((1, 128), lambda i: (0, 0)),
                   pl.BlockSpec((1, d), lambda i: (0, 0))],
        out_shape=[jax.ShapeDtypeStruct((s, d), F32),
                   jax.ShapeDtypeStruct((1, 128), F32),
                   jax.ShapeDtypeStruct((1, d), F32)],
        compiler_params=_params(("arbitrary",)),
    )(h, g, target)


def _mlp_bwd_x(dz, a, w_up, w_down, h_in, g, name):
    s, d = dz.shape
    ff = w_down.shape[0]
    slot_cols = w_up.shape[2]
    tm = min(TILE_MLP_ROWS, s)
    tf = min(TILE_MLP_FF, slot_cols)
    per_slot = slot_cols // tf
    nf = ff // tf

    def body(dz_ref, a_ref, wu_ref, wd_ref, h_ref, g_ref, da_ref, dzb_ref, dh_ref, dg_ref,
             dzs_ref, acc_ref):
        i = pl.program_id(0)
        f = pl.program_id(1)

        @pl.when(f == 0)
        def _():
            dzb = dz_ref[...].astype(BF16)
            dzs_ref[...] = dzb
            dzb_ref[...] = dzb
            acc_ref[...] = jnp.zeros_like(acc_ref)

        dr = _nt(dzs_ref[...], wd_ref[...])
        da = (dr * (2.0 * jnp.maximum(a_ref[...].astype(F32), 0.0))).astype(BF16)
        da_ref[...] = da
        acc_ref[...] += _nt(da, wu_ref[...])

        @pl.when(f == nf - 1)
        def _():
            dx, dg = _rms_bwd(acc_ref[...], h_ref[...], g_ref[...])
            dh_ref[...] = dz_ref[...] + dx

            @pl.when(i == 0)
            def _():
                dg_ref[...] = dg

            @pl.when(i > 0)
            def _():
                dg_ref[...] += dg

    return pl.pallas_call(
        body, name=name, grid=(s // tm, nf),
        in_specs=[pl.BlockSpec((tm, d), lambda i, f: (i, 0)),
                  pl.BlockSpec((tm, tf), lambda i, f: (i, f)),
                  pl.BlockSpec((None, d, tf), lambda i, f: (f // per_slot, 0, f % per_slot)),
                  pl.BlockSpec((tf, d), lambda i, f: (f, 0)),
                  pl.BlockSpec((tm, d), lambda i, f: (i, 0)), _full(g)],
        out_specs=[pl.BlockSpec((tm, tf), lambda i, f: (i, f)),
                   pl.BlockSpec((tm, d), lambda i, f: (i, 0)),
                   pl.BlockSpec((tm, d), lambda i, f: (i, 0)),
                   pl.BlockSpec((1, d), lambda i, f: (0, 0))],
        out_shape=[jax.ShapeDtypeStruct((s, ff), BF16),
                   jax.ShapeDtypeStruct((s, d), BF16),
                   jax.ShapeDtypeStruct((s, d), F32),
                   jax.ShapeDtypeStruct((1, d), F32)],
        scratch_shapes=[pltpu.VMEM((tm, d), BF16), pltpu.VMEM((tm, d), F32)],
        compiler_params=_params(("arbitrary", "arbitrary")),
    )(dz, a, w_up, w_down, h_in, g)


def _mlp_bwd_w(n, da, a, dzb, slot_cols, name):
    s, d = n.shape
    ff = a.shape[1]
    tn = min(TILE_WGRAD_N, slot_cols)
    tk = min(TILE_WGRAD_K, s)
    per_slot = slot_cols // tn
    nk = s // tk

    def body(n_ref, da_ref, a_ref, dz_ref, du_ref, dd_ref, accu_ref, accd_ref):
        k = pl.program_id(1)

        @pl.when(k == 0)
        def _():
            accu_ref[...] = jnp.zeros_like(accu_ref)
            accd_ref[...] = jnp.zeros_like(accd_ref)

        accu_ref[...] += _tn(n_ref[...], da_ref[...])
        r = jnp.square(jnp.maximum(a_ref[...].astype(F32), 0.0)).astype(BF16)
        accd_ref[...] += _tn(r, dz_ref[...])

        @pl.when(k == nk - 1)
        def _():
            du_ref[...] = accu_ref[...].astype(BF16)
            dd_ref[...] = accd_ref[...].astype(BF16)

    return pl.pallas_call(
        body, name=name, grid=(ff // tn, nk),
        in_specs=[pl.BlockSpec((tk, d), lambda f, k: (k, 0)),
                  pl.BlockSpec((tk, tn), lambda f, k: (k, f)),
                  pl.BlockSpec((tk, tn), lambda f, k: (k, f)),
                  pl.BlockSpec((tk, d), lambda f, k: (k, 0))],
        out_specs=[pl.BlockSpec((None, d, tn), lambda f, k: (f // per_slot, 0, f % per_slot)),
                   pl.BlockSpec((tn, d), lambda f, k: (f, 0))],
        out_shape=[jax.ShapeDtypeStruct((ff // slot_cols, d, slot_cols), BF16),
                   jax.ShapeDtypeStruct((ff, d), BF16)],
        scratch_shapes=[pltpu.VMEM((d, tn), F32), pltpu.VMEM((tn, d), F32)],
        compiler_params=_params(("parallel", "arbitrary")),
    )(n, da, a, dzb)


def _pool_bwd(dh, h, g, pw, ps):
    s, d = h.shape
    cg = d // len(POOL_WINDOWS)
    tm = min(TILE_ROWS, s)
    nb = s // tm
    halo_per_tile = tm // POOL_HALO

    def body(dh_ref, h_ref, halo_ref, g_ref, pw_ref, ps_ref,
             dx_ref, dpw_ref, dps_ref, dg_ref, nbuf, qbuf, dn_ref, carry, dpw_acc):
        i = pl.program_id(0)
        blk = nb - 1 - i

        @pl.when(i == 0)
        def _():
            carry[...] = jnp.zeros_like(carry)
            dpw_acc[...] = jnp.zeros_like(dpw_acc)
            dps_ref[...] = jnp.zeros_like(dps_ref)
            dg_ref[...] = jnp.zeros_like(dg_ref)

        hv = h_ref[...]
        n, _ = _rms_fwd(hv, g_ref[...])
        nh, _ = _rms_fwd(halo_ref[...], g_ref[...])
        nbuf[0:POOL_HALO, :] = jnp.where(blk == 0, 0.0, nh)
        nbuf[POOL_HALO:POOL_HALO + tm, :] = n
        dhv = dh_ref[...]
        for k, window in enumerate(POOL_WINDOWS):
            cols = slice(k * cg, (k + 1) * cg)
            cnt = _pool_counts(blk * tm, tm, window)
            sums = _window_sum_down(nbuf[:, cols], window)[POOL_HALO:, :]
            pb = (sums / cnt - n[:, cols]).astype(BF16)
            dyk = dhv[:, cols]
            dps_ref[:, cols] += jnp.sum(dyk * _nn(pb, pw_ref[k]), axis=0, keepdims=True)
            dyb = (dyk * ps_ref[:, cols]).astype(BF16)
            dpw_acc[k] += _tn(pb, dyb)
            dpool = _nt(dyb, pw_ref[k])
            qv = dpool / cnt
            qbuf[0:tm, cols] = qv
            qbuf[tm:tm + POOL_HALO, cols] = carry[:, cols]
            dn_ref[:, cols] = _window_sum_up(qbuf[:, cols], window)[0:tm, :] - dpool
            carry[:, cols] = qv[0:POOL_HALO, :]
        dx, dg = _rms_bwd(dn_ref[...], hv, g_ref[...])
        dx_ref[...] = dhv + dx
        dg_ref[...] += dg

        @pl.when(i == nb - 1)
        def _():
            dpw_ref[...] = dpw_acc[...].astype(BF16)

    rev = lambda i: (nb - 1 - i, 0)
    return pl.pallas_call(
        body, name="pool_bwd", grid=(nb,),
        in_specs=[pl.BlockSpec((tm, d), rev), pl.BlockSpec((tm, d), rev),
                  pl.BlockSpec((POOL_HALO, d),
                               lambda i: (jnp.maximum((nb - 1 - i) * halo_per_tile - 1, 0), 0)),
                  _full(g), _full(pw), _full(ps)],
        out_specs=[pl.BlockSpec((tm, d), rev), _full(pw),
                   pl.BlockSpec((1, d), lambda i: (0, 0)),
                   pl.BlockSpec((1, d), lambda i: (0, 0))],
        out_shape=[jax.ShapeDtypeStruct((s, d), F32),
                   jax.ShapeDtypeStruct(pw.shape, BF16),
                   jax.ShapeDtypeStruct((1, d), F32),
                   jax.ShapeDtypeStruct((1, d), F32)],
        scratch_shapes=[pltpu.VMEM((tm + POOL_HALO, d), F32), pltpu.VMEM((tm + POOL_HALO, d), F32),
                        pltpu.VMEM((tm, d), F32), pltpu.VMEM((POOL_HALO, d), F32),
                        pltpu.VMEM(pw.shape, F32)],
        compiler_params=_params(("arbitrary",)),
    )(dh, h, h, g, pw, ps)


def _conv_out_bwd(dh, w_out, o, bcx, cw):
    s, d = dh.shape
    c = o.shape[1]
    tm = min(TILE_ROWS, s)
    nb = s // tm
    halo_per_tile = tm // CONV_HALO

    def body(dh_ref, w_ref, o_ref, b_ref, c_ref, xin_ref, ch_ref, xh_ref, cw_ref,
             do_ref, delta_ref, dbcx_ref, dw_ref, dcw_ref, ubuf, dbuf, carry, acc):
        i = pl.program_id(0)
        blk = nb - 1 - i

        @pl.when(i == 0)
        def _():
            carry[...] = jnp.zeros_like(carry)
            acc[...] = jnp.zeros_like(acc)
            dcw_ref[...] = jnp.zeros_like(dcw_ref)

        dm = dh_ref[...].astype(BF16)
        dcat = _nt(dm, w_ref[...])
        do = dcat[:, 0:c]
        dy = dcat[:, c:2 * c]
        do_ref[...] = do.astype(BF16)
        head_of_lane = lax.shift_right_logical(_lane((8, c)), HEAD_DIM.bit_length() - 1)
        heads = (head_of_lane == _row((8, c))).astype(BF16)
        delta_ref[...] = _exact_nt(heads, do * o_ref[...].astype(F32))

        cv_ = c_ref[...]
        xin = xin_ref[...]
        bv = b_ref[...]
        u = cv_ * xin
        ubuf[0:CONV_HALO, :] = jnp.where(blk == 0, 0.0, ch_ref[...] * xh_ref[...])
        ubuf[CONV_HALO:CONV_HALO + tm, :] = u
        u1 = ubuf[CONV_HALO - 1:CONV_HALO - 1 + tm, :]
        u2 = ubuf[CONV_HALO - 2:CONV_HALO - 2 + tm, :]
        w0, w1, w2 = cw_ref[0:1, :], cw_ref[1:2, :], cw_ref[2:3, :]
        cv = (w0 * u2 + w1 * u1) + w2 * u
        acc[0:c, :] += _tn(o_ref[...], dm)
        acc[c:2 * c, :] += _tn((bv * cv).astype(BF16), dm)

        dcv = dy * bv
        dcw_ref[0:1, :] += jnp.sum(dcv * u2, axis=0, keepdims=True)
        dcw_ref[1:2, :] += jnp.sum(dcv * u1, axis=0, keepdims=True)
        dcw_ref[2:3, :] += jnp.sum(dcv * u, axis=0, keepdims=True)
        dbuf[0:tm, :] = dcv
        dbuf[tm:tm + CONV_HALO, :] = carry[...]
        du = w2 * dcv + w1 * dbuf[1:1 + tm, :] + w0 * dbuf[2:2 + tm, :]
        dbcx_ref[:, 0:c] = (dy * cv).astype(BF16)
        dbcx_ref[:, c:2 * c] = (du * xin).astype(BF16)
        dbcx_ref[:, 2 * c:3 * c] = (du * cv_).astype(BF16)
        carry[...] = dcv[0:CONV_HALO, :]

        @pl.when(i == nb - 1)
        def _():
            dw_ref[...] = acc[...].astype(BF16)

    rev = lambda k: (lambda i: (nb - 1 - i, k))
    halo = lambda k: (lambda i: (jnp.maximum((nb - 1 - i) * halo_per_tile - 1, 0), k))
    return pl.pallas_call(
        body, name="conv_out_bwd", grid=(nb,),
        in_specs=[pl.BlockSpec((tm, d), rev(0)), _full(w_out), pl.BlockSpec((tm, c), rev(0)),
                  pl.BlockSpec((tm, c), rev(0)), pl.BlockSpec((tm, c), rev(1)),
                  pl.BlockSpec((tm, c), rev(2)),
                  pl.BlockSpec((CONV_HALO, c), halo(1)), pl.BlockSpec((CONV_HALO, c), halo(2)),
                  _full(cw)],
        out_specs=[pl.BlockSpec((tm, c), rev(0)),
                   pl.BlockSpec((8, tm), lambda i: (0, nb - 1 - i)),
                   pl.BlockSpec((tm, 3 * c), rev(0)),
                   _full(w_out), _full(cw)],
        out_shape=[jax.ShapeDtypeStruct((s, c), BF16),
                   jax.ShapeDtypeStruct((8, s), F32),
                   jax.ShapeDtypeStruct((s, 3 * c), BF16),
                   jax.ShapeDtypeStruct(w_out.shape, BF16),
                   jax.ShapeDtypeStruct(cw.shape, F32)],
        scratch_shapes=[pltpu.VMEM((tm + CONV_HALO, c), F32), pltpu.VMEM((tm + CONV_HALO, c), F32),
                        pltpu.VMEM((CONV_HALO, c), F32), pltpu.VMEM(w_out.shape, F32)],
        compiler_params=_params(("arbitrary",)),
    )(dh, w_out, o, bcx, bcx, bcx, bcx, bcx, cw)


def _attn_bwd(qa, ka, qkv, do, lse, delta):
    s = qa.shape[1]
    a = N_HEADS * HEAD_DIM
    t = min(TILE_ATTN, s)
    nq = s // t
    n_pairs = N_HEADS // 2
    v_block0 = 2 * a // 128

    def body(ka_ref, v_ref, qa_ref, do_ref, lse_ref, delta_ref,
             dqa_ref, dka_ref, dv_ref, dk_acc, dv_acc):
        g = pl.program_id(0)
        j = pl.program_id(1)

        @pl.when(j == 0)
        def _():
            dqa_ref[...] = jnp.zeros_like(dqa_ref)

        dk_acc[...] = jnp.zeros_like(dk_acc)
        dv_acc[...] = jnp.zeros_like(dv_acc)
        lane = _lane((t, 128))
        vf = v_ref[...].astype(F32)
        v_heads = [jnp.where(lane < HEAD_DIM, vf, 0.0).astype(BF16),
                   jnp.where(lane >= HEAD_DIM, vf, 0.0).astype(BF16)]

        def q_step(i, masked):
            qs = pl.ds(pl.multiple_of(i * t, t), t)
            dob = do_ref[qs, :]
            for e in range(2):
                ke = ka_ref[e]
                qe = qa_ref[e, qs, :]
                sc = _nt(ke, qe)
                if masked:
                    sc = jnp.where(_row((t, t)) <= _lane((t, t)), sc, NEG_BIG)
                p = jnp.exp(sc - lse_ref[pl.ds(e, 1), qs])
                dv_acc[e] += _nn(p.astype(BF16), dob)
                dp = _nt(v_heads[e], dob)
                ds = (p * (dp - delta_ref[pl.ds(2 * g + e, 1), qs])).astype(BF16)
                dk_acc[e] += _nn(ds, qe)
                dqa_ref[e, qs, :] += _tn(ds, ke)

        q_step(j, True)

        def full_step(i, carry):
            q_step(i, False)
            return carry

        lax.fori_loop(j + 1, nq, full_step, 0)
        dka_ref[...] = dk_acc[...]
        dv_ref[...] = jnp.where(lane < HEAD_DIM, dv_acc[0], dv_acc[1]).astype(BF16)

    return pl.pallas_call(
        body, name="attn_bwd", grid=(n_pairs, nq),
        in_specs=[pl.BlockSpec((2, t, 128), lambda g, j: (g, j, 0)),
                  pl.BlockSpec((t, 128), lambda g, j: (j, v_block0 + g)),
                  pl.BlockSpec((2, s, 128), lambda g, j: (g, 0, 0)),
                  pl.BlockSpec((s, 128), lambda g, j: (0, g)),
                  pl.BlockSpec((None, 8, s), lambda g, j: (g, 0, 0)),
                  pl.BlockSpec((8, s), lambda g, j: (0, 0))],
        out_specs=[pl.BlockSpec((2, s, 128), lambda g, j: (g, 0, 0)),
                   pl.BlockSpec((2, t, 128), lambda g, j: (g, j, 0)),
                   pl.BlockSpec((t, 128), lambda g, j: (j, g))],
        out_shape=[jax.ShapeDtypeStruct((N_HEADS, s, 128), F32),
                   jax.ShapeDtypeStruct((N_HEADS, s, 128), F32),
                   jax.ShapeDtypeStruct((s, a), BF16)],
        scratch_shapes=[pltpu.VMEM((2, t, 128), F32), pltpu.VMEM((2, t, 128), F32)],
        compiler_params=_params(("parallel", "arbitrary")),
    )(ka, qkv, qa, do, lse, delta)


def _gate_bwd(dqa, dka, dv, fl, bf):
    s = fl.shape[0]
    a = N_HEADS * HEAD_DIM
    tm = min(TILE_ROWS, s)
    nb = s // tm

    def body(dqa_ref, dka_ref, dv_ref, fl_ref, bf_ref, dqkv_ref, dfl_ref, dbf_ref, carry):
        i = pl.program_id(0)

        @pl.when(i == 0)
        def _():
            carry[...] = jnp.zeros_like(carry)
            dbf_ref[...] = jnp.zeros_like(dbf_ref)

        lane = _lane((tm, 128))
        dcum = jnp.zeros((tm, 128), F32)
        for pair in range(N_HEADS // 2):
            qs, ks = [], []
            for e in range(2):
                h = 2 * pair + e
                dq = dqa_ref[h]
                dk = dka_ref[h]
                dc = jnp.sum(jnp.where(lane == LANE_CQ, dq, 0.0)
                             - jnp.where(lane == LANE_ONE, dk, 0.0), axis=1, keepdims=True)
                dcum = jnp.where(lane == h, dc, dcum)
                qs.append(dq * ATTN_SCALE)
                ks.append(dk)
            cols = slice(pair * 128, (pair + 1) * 128)
            dqkv_ref[:, cols] = jnp.where(
                lane < HEAD_DIM, qs[0], pltpu.roll(qs[1], HEAD_DIM, axis=1)).astype(BF16)
            dqkv_ref[:, a + pair * 128:a + (pair + 1) * 128] = jnp.where(
                lane < HEAD_DIM, ks[0], pltpu.roll(ks[1], HEAD_DIM, axis=1)).astype(BF16)
        dqkv_ref[:, 2 * a:3 * a] = dv_ref[...]

        upper = (_lane((tm, tm)) >= _row((tm, tm))).astype(BF16)
        dlogf = _exact_nn(upper, dcum) + carry[0:1, :]
        carry[0:1, :] = dlogf[0:1, :]
        z = fl_ref[...] + bf_ref[...]
        ez = jnp.exp(-jnp.abs(z))
        sig_neg = jnp.where(z >= 0.0, ez, 1.0) / (1.0 + ez)
        dz = jnp.where(lane < N_HEADS, dlogf * sig_neg, 0.0)
        dfl_ref[...] = dz.astype(BF16)
        dbf_ref[...] += jnp.sum(dz, axis=0, keepdims=True)

    rev3 = lambda i: (0, nb - 1 - i, 0)
    rev = lambda i: (nb - 1 - i, 0)
    return pl.pallas_call(
        body, name="gate_bwd", grid=(nb,),
        in_specs=[pl.BlockSpec((N_HEADS, tm, 128), rev3), pl.BlockSpec((N_HEADS, tm, 128), rev3),
                  pl.BlockSpec((tm, a), rev), pl.BlockSpec((tm, 128), rev), _full(bf)],
        out_specs=[pl.BlockSpec((tm, 3 * a), rev), pl.BlockSpec((tm, 128), rev),
                   pl.BlockSpec((1, 128), lambda i: (0, 0))],
        out_shape=[jax.ShapeDtypeStruct((s, 3 * a), BF16),
                   jax.ShapeDtypeStruct((s, 128), BF16),
                   jax.ShapeDtypeStruct((1, 128), F32)],
        scratch_shapes=[pltpu.VMEM((8, 128), F32)],
        compiler_params=_params(("arbitrary",)),
    )(dqa, dka, dv, fl, bf)


def _in_proj_bwd(dqkv, dfl, dbcx, w_qkv, w_f, w_bcx, x, g, dh):
    s, d = x.shape
    tm = min(TILE_ROWS, s)

    def body(dq_ref, df_ref, db_ref, wq_ref, wf_ref, wb_ref, x_ref, g_ref, dh_ref, gx_ref, dg_ref):
        i = pl.program_id(0)
        dn = (_nt(dq_ref[...], wq_ref[...]) + _nt(df_ref[...], wf_ref[...])
              + _nt(db_ref[...], wb_ref[...]))
        dx, dg = _rms_bwd(dn, x_ref[...], g_ref[...])
        gx_ref[...] = dh_ref[...] + dx

        @pl.when(i == 0)
        def _():
            dg_ref[...] = dg

        @pl.when(i > 0)
        def _():
            dg_ref[...] += dg

    rows = lambda c: pl.BlockSpec((tm, c), lambda i: (i, 0))
    return pl.pallas_call(
        body, name="in_proj_bwd", grid=(s // tm,),
        in_specs=[rows(dqkv.shape[1]), rows(dfl.shape[1]), rows(dbcx.shape[1]),
                  _full(w_qkv), _full(w_f), _full(w_bcx), rows(d), _full(g), rows(d)],
        out_specs=[rows(d), pl.BlockSpec((1, d), lambda i: (0, 0))],
        out_shape=[jax.ShapeDtypeStruct((s, d), F32), jax.ShapeDtypeStruct((1, d), F32)],
        compiler_params=_params(("arbitrary",)),
    )(dqkv, dfl, dbcx, w_qkv, w_f, w_bcx, x, g, dh)


def _wgrad(n, dy, name):
    s, d = n.shape
    cols = dy.shape[1]
    tn = min(512, cols)
    tk = min(TILE_WGRAD_K, s)
    nk = s // tk

    def body(n_ref, dy_ref, dw_ref, acc):
        k = pl.program_id(1)

        @pl.when(k == 0)
        def _():
            acc[...] = jnp.zeros_like(acc)

        acc[...] += _tn(n_ref[...], dy_ref[...])

        @pl.when(k == nk - 1)
        def _():
            dw_ref[...] = acc[...].astype(BF16)

    return pl.pallas_call(
        body, name=name, grid=(cols // tn, nk),
        in_specs=[pl.BlockSpec((tk, d), lambda f, k: (k, 0)),
                  pl.BlockSpec((tk, tn), lambda f, k: (k, f))],
        out_specs=pl.BlockSpec((d, tn), lambda f, k: (0, f)),
        out_shape=jax.ShapeDtypeStruct((d, cols), BF16),
        scratch_shapes=[pltpu.VMEM((d, tn), F32)],
        compiler_params=_params(("parallel", "arbitrary")),
    )(n, dy)


def _row_tile(rows):
    t = min(TILE_ELEM_ROWS, rows)
    while rows % t:
        t //= 2
    return t


def _sum_pair(grad, theirs, core, name):
    slots, rows, cols = theirs.shape
    tr = _row_tile(rows)
    nb = rows // tr

    def body(core_ref, a_ref, b_ref, o_ref):
        o_ref[...] = (a_ref[...].astype(F32) + b_ref[...].astype(F32)).astype(BF16)

    spec = pl.BlockSpec((None, tr, cols), lambda s, i, core_ref: (s, i, 0))
    return pl.pallas_call(
        body, name=name,
        grid_spec=pltpu.PrefetchScalarGridSpec(
            num_scalar_prefetch=1, grid=(slots, nb),
            in_specs=[pl.BlockSpec((None, tr, cols),
                                   lambda s, i, core_ref: (s, core_ref[0] * nb + i, 0)), spec],
            out_specs=spec),
        out_shape=jax.ShapeDtypeStruct(theirs.shape, BF16),
        compiler_params=_params(("parallel", "parallel")),
    )(core, grad, theirs)


def _sum_chips(sums, others, chip, name):
    _, rows, cols = sums.shape
    tr = _row_tile(rows)

    def body(chip_ref, a_ref, b_ref, o_ref):
        acc = a_ref[...].astype(F32)
        for k in range(N_CHIPS - 1):
            acc = acc + b_ref[k].astype(F32)
        o_ref[...] = acc

    return pl.pallas_call(
        body, name=name,
        grid_spec=pltpu.PrefetchScalarGridSpec(
            num_scalar_prefetch=1, grid=(rows // tr,),
            in_specs=[pl.BlockSpec((None, tr, cols), lambda i, chip_ref: (chip_ref[0], i, 0)),
                      pl.BlockSpec((N_CHIPS - 1, tr, cols), lambda i, chip_ref: (0, i, 0))],
            out_specs=pl.BlockSpec((tr, cols), lambda i, chip_ref: (i, 0))),
        out_shape=jax.ShapeDtypeStruct((rows, cols), F32),
        compiler_params=_params(("parallel",)),
    )(chip, sums, others)


def _adamw_math(w, g, m, v):
    m = ADAM_B1 * m + (1.0 - ADAM_B1) * g
    v = ADAM_B2 * v + (1.0 - ADAM_B2) * jnp.square(g)
    m_hat = m / (1.0 - ADAM_B1 ** ADAM_STEP)
    v_hat = v / (1.0 - ADAM_B2 ** ADAM_STEP)
    delta = -ADAM_LR * (m_hat / (jnp.sqrt(v_hat) + ADAM_EPS) + ADAM_WD * w)
    return delta, m, v


def _adamw(w, g, m, v, name):
    rows, cols = w.shape
    tr = _row_tile(rows)

    def body(w_ref, g_ref, m_ref, v_ref, d_ref, nm_ref, nv_ref):
        delta, nm, nv = _adamw_math(w_ref[...], g_ref[...], m_ref[...], v_ref[...])
        d_ref[...] = delta
        nm_ref[...] = nm
        nv_ref[...] = nv

    spec = pl.BlockSpec((tr, cols), lambda i: (i, 0))
    out = jax.ShapeDtypeStruct(w.shape, F32)
    return pl.pallas_call(
        body, name=name, grid=(rows // tr,), in_specs=[spec] * 4, out_specs=[spec] * 3,
        out_shape=[out, out, out], compiler_params=_params(("parallel",)),
    )(w, g, m, v)


def _sum_devices(parts):
    def body(p_ref, g_ref):
        g = p_ref[0]
        for k in range(1, N_DEV):
            g = g + p_ref[k]
        g_ref[...] = g

    return pl.pallas_call(
        body, name="sum_devices",
        in_specs=[pl.BlockSpec(memory_space=pltpu.VMEM)],
        out_specs=pl.BlockSpec(memory_space=pltpu.VMEM),
        out_shape=jax.ShapeDtypeStruct(parts.shape[1:], F32),
    )(parts)


def _mesh_position():
    x, y, c = lax.axis_index("x"), lax.axis_index("y"), lax.axis_index("c")
    chips = [(1 - x, y), (x, 1 - y), (1 - x, 1 - y)]
    return x, y, c, chips


ANY = pl.BlockSpec(memory_space=pl.ANY)


def _gather_weights(shards, small):
    nl, ns = len(shards), len(small)
    n = nl + ns

    def body(*refs):
        ins, outs = refs[:n], refs[n:2 * n]
        ibuf, dbuf = refs[2 * n:2 * n + nl], refs[2 * n + nl:2 * n + 2 * nl]
        send, recv, local, st_own, st_sib = refs[2 * n + 2 * nl:]
        x, y, c, chips = _mesh_position()
        me = 2 * x + y
        sibling = (x, y, 1 - c)

        def remote(src, dst, a, k, to):
            return pltpu.make_async_remote_copy(src_ref=src, dst_ref=dst, send_sem=send.at[a, k],
                                                recv_sem=recv.at[a, k], device_id=to,
                                                device_id_type=MESH)

        pending = []
        for a in range(n):
            cp = pltpu.make_async_copy(ins[a], outs[a].at[me], local.at[a])
            cp.start()
            pending.append(cp)
        sends = []
        for a in range(n):
            half = ins[a].shape[0] // 2
            for k, (px, py) in enumerate(chips):
                if a < nl:
                    cp = remote(ins[a].at[pl.ds(c * half, half)], ibuf[a].at[k], a, k, (px, py, c))
                else:
                    cp = remote(ins[a], outs[a].at[me], a, k, (px, py, c))
                cp.start()
                sends.append(cp)
        for a in range(nl):
            half = ins[a].shape[0] // 2
            for k, (px, py) in enumerate(chips):
                got = ibuf[a].at[k]
                remote(got, got, a, k, (px, py, c)).wait_recv()
                st = pltpu.make_async_copy(got, outs[a].at[2 * px + py, pl.ds(c * half, half)],
                                           st_own.at[a, k])
                st.start()
                pending.append(st)
                fw = remote(got, dbuf[a].at[k], a, 3 + k, sibling)
                fw.start()
                sends.append(fw)
        for a in range(nl, n):
            for k, (px, py) in enumerate(chips):
                slot = outs[a].at[2 * px + py]
                remote(slot, slot, a, k, (px, py, c)).wait_recv()
        for a in range(nl):
            half = ins[a].shape[0] // 2
            for k, (px, py) in enumerate(chips):
                got = dbuf[a].at[k]
                remote(got, got, a, 3 + k, sibling).wait_recv()
                st = pltpu.make_async_copy(got, outs[a].at[2 * px + py, pl.ds((1 - c) * half, half)],
                                           st_sib.at[a, k])
                st.start()
                pending.append(st)
        for cp in sends:
            cp.wait_send()
        for cp in pending:
            cp.wait()

    arrays = list(shards) + list(small)
    stage = [pltpu.VMEM((N_CHIPS - 1, a.shape[0] // 2) + a.shape[1:], a.dtype) for a in shards]
    return pl.pallas_call(
        body, name="gather_weights",
        in_specs=[ANY] * n, out_specs=[ANY] * n,
        out_shape=[jax.ShapeDtypeStruct((N_CHIPS,) + a.shape, a.dtype) for a in arrays],
        scratch_shapes=stage + stage + [
            pltpu.SemaphoreType.DMA((n, 6)), pltpu.SemaphoreType.DMA((n, 6)),
            pltpu.SemaphoreType.DMA((n,)), pltpu.SemaphoreType.DMA((nl, 3)),
            pltpu.SemaphoreType.DMA((nl, 3))],
        compiler_params=pltpu.CompilerParams(vmem_limit_bytes=VMEM_LIMIT_BYTES),
    )(*arrays)


def _exchange_siblings(grads, name):
    n = len(grads)

    def body(*refs):
        ins, theirs = refs[:n], refs[n:2 * n]
        sbuf, rbuf = refs[2 * n:3 * n], refs[3 * n:4 * n]
        ld, st, send, recv = refs[4 * n:]
        x, y, c, _ = _mesh_position()
        loads, sends, stores = [], [], []
        for a in range(n):
            half = ins[a].shape[1] // 2
            cp = pltpu.make_async_copy(ins[a].at[:, pl.ds((1 - c) * half, half)], sbuf[a], ld.at[a])
            cp.start()
            loads.append(cp)
        for a in range(n):
            loads[a].wait()
            rc = pltpu.make_async_remote_copy(
                src_ref=sbuf[a], dst_ref=rbuf[a], send_sem=send.at[a], recv_sem=recv.at[a],
                device_id=(x, y, 1 - c), device_id_type=MESH)
            rc.start()
            sends.append(rc)
        for a in range(n):
            sends[a].wait_recv()
            cp = pltpu.make_async_copy(rbuf[a], theirs[a], st.at[a])
            cp.start()
            stores.append(cp)
        for a in range(n):
            sends[a].wait_send()
            stores[a].wait()

    half_shape = lambda a: (a.shape[0], a.shape[1] // 2, a.shape[2])
    stage = [pltpu.VMEM(half_shape(a), a.dtype) for a in grads]
    return pl.pallas_call(
        body, name=name,
        in_specs=[ANY] * n, out_specs=[ANY] * n,
        out_shape=[jax.ShapeDtypeStruct(half_shape(a), a.dtype) for a in grads],
        scratch_shapes=stage + stage + [pltpu.SemaphoreType.DMA((n,))] * 4,
        compiler_params=pltpu.CompilerParams(vmem_limit_bytes=VMEM_LIMIT_BYTES),
    )(*grads)


def _exchange_chips(sums):
    n = len(sums)

    def body(*refs):
        ins, got = refs[:n], refs[n:2 * n]
        send, recv = refs[2 * n:]
        x, y, c, chips = _mesh_position()
        cps = []
        for a in range(n):
            for k, (px, py) in enumerate(chips):
                rc = pltpu.make_async_remote_copy(
                    src_ref=ins[a].at[2 * px + py], dst_ref=got[a].at[k],
                    send_sem=send.at[a, k], recv_sem=recv.at[a, k], device_id=(px, py, c),
                    device_id_type=MESH)
                rc.start()
                cps.append(rc)
        for cp in cps:
            cp.wait()

    return pl.pallas_call(
        body, name="exchange_chips",
        in_specs=[ANY] * n, out_specs=[ANY] * n,
        out_shape=[jax.ShapeDtypeStruct((N_CHIPS - 1,) + a.shape[1:], a.dtype) for a in sums],
        scratch_shapes=[pltpu.SemaphoreType.DMA((n, 3)), pltpu.SemaphoreType.DMA((n, 3))],
    )(*sums)


def _share_halves(halves):
    n = len(halves)

    def body(*refs):
        ins, outs = refs[:n], refs[n:2 * n]
        sbuf, rbuf = refs[2 * n:3 * n], refs[3 * n:4 * n]
        ld, st_own, st_sib, send, recv = refs[4 * n:]
        x, y, c, _ = _mesh_position()
        loads, sends, stores = [], [], []
        for a in range(n):
            cp = pltpu.make_async_copy(ins[a], sbuf[a], ld.at[a])
            cp.start()
            loads.append(cp)
        for a in range(n):
            half = ins[a].shape[0]
            loads[a].wait()
            rc = pltpu.make_async_remote_copy(
                src_ref=sbuf[a], dst_ref=rbuf[a], send_sem=send.at[a], recv_sem=recv.at[a],
                device_id=(x, y, 1 - c), device_id_type=MESH)
            rc.start()
            sends.append(rc)
            cp = pltpu.make_async_copy(sbuf[a], outs[a].at[pl.ds(c * half, half)], st_own.at[a])
            cp.start()
            stores.append(cp)
        for a in range(n):
            half = ins[a].shape[0]
            sends[a].wait_recv()
            cp = pltpu.make_async_copy(rbuf[a], outs[a].at[pl.ds((1 - c) * half, half)], st_sib.at[a])
            cp.start()
            stores.append(cp)
        for cp in sends:
            cp.wait_send()
        for cp in stores:
            cp.wait()

    stage = [pltpu.VMEM(a.shape, a.dtype) for a in halves]
    return pl.pallas_call(
        body, name="share_halves",
        in_specs=[ANY] * n, out_specs=[ANY] * n,
        out_shape=[jax.ShapeDtypeStruct((2 * a.shape[0],) + a.shape[1:], a.dtype)
                   for a in halves],
        scratch_shapes=stage + stage + [pltpu.SemaphoreType.DMA((n,))] * 5,
        compiler_params=pltpu.CompilerParams(vmem_limit_bytes=VMEM_LIMIT_BYTES),
    )(*halves)


def _gather_small(part):
    def body(in_ref, out_ref, send, recv, local):
        x, y, c, _ = _mesh_position()
        me = 4 * x + 2 * y + c
        cps = [pltpu.make_async_copy(in_ref, out_ref.at[me], local)]
        k = 0
        for fx in range(2):
            for fy in range(2):
                for fc in range(2):
                    if fx or fy or fc:
                        cps.append(pltpu.make_async_remote_copy(
                            src_ref=in_ref, dst_ref=out_ref.at[me], send_sem=send.at[k],
                            recv_sem=recv.at[k], device_id=(x ^ fx, y ^ fy, c ^ fc),
                            device_id_type=MESH))
                        k += 1
        for cp in cps:
            cp.start()
        for cp in cps:
            cp.wait()

    return pl.pallas_call(
        body, name="gather_small",
        in_specs=[pl.BlockSpec(memory_space=pltpu.VMEM)],
        out_specs=pl.BlockSpec(memory_space=pltpu.VMEM),
        out_shape=jax.ShapeDtypeStruct((N_DEV,) + part.shape, part.dtype),
        scratch_shapes=[pltpu.SemaphoreType.DMA((N_DEV - 1,)), pltpu.SemaphoreType.DMA((N_DEV - 1,)),
                        pltpu.SemaphoreType.DMA],
    )(part)


def _reduce_scatter(grads, groups, core, chip):
    n = len(grads)
    theirs = [None] * n
    for k, group in enumerate(groups):
        for a, t in zip(group, _exchange_siblings([grads[a] for a in group],
                                                  "exchange_siblings_%d" % k)):
            theirs[a] = t
    chip_sums = [_sum_pair(grads[a], theirs[a], core, "sum_siblings_%d" % a) for a in range(n)]
    got = _exchange_chips(chip_sums)
    halves = [_sum_chips(chip_sums[a], got[a], chip, "sum_chips_%d" % a) for a in range(n)]
    return _share_halves(halves)


def _pad_rows(a, rows):
    return jnp.pad(a, ((0, rows - a.shape[0]), (0, 0)))


def kernel(x, norm_mix_0, w_in_0, b_f_0, conv_w_0, w_out_0, norm_ffn_0, w_up_0, w_down_0, norm_mix_1, pool_w_1, pool_scale_1, norm_ffn_1, w_up_1, w_down_1, final_norm, loss_target, m_norm_mix_0, m_w_in_0, m_b_f_0, m_conv_w_0, m_w_out_0, m_norm_ffn_0, m_w_up_0, m_w_down_0, m_norm_mix_1, m_pool_w_1, m_pool_scale_1, m_norm_ffn_1, m_w_up_1, m_w_down_1, m_final_norm, v_norm_mix_0, v_w_in_0, v_b_f_0, v_conv_w_0, v_w_out_0, v_norm_ffn_0, v_w_up_0, v_w_down_0, v_norm_mix_1, v_pool_w_1, v_pool_scale_1, v_norm_ffn_1, v_w_up_1, v_w_down_1, v_final_norm):
    d = x.shape[-1]
    a = N_HEADS * HEAD_DIM
    c_conv = conv_w_0.shape[1] * N_CHIPS
    xs = x[0]
    target = loss_target[0]
    row = lambda vec: vec.reshape(1, -1)

    big = [w_in_0, w_out_0, w_up_0, w_down_0, pool_w_1, w_up_1, w_down_1]
    g_in, g_out, g_up0, g_down0, g_pool, g_up1, g_down1, g_conv = _gather_weights(
        [w.astype(BF16) for w in big], [conv_w_0])
    w_in = g_in.transpose(1, 0, 2).reshape(d, -1)
    w_qkv = w_in[:, :3 * a]
    w_f = jnp.pad(w_in[:, 3 * a:3 * a + N_HEADS], ((0, 0), (0, 128 - N_HEADS)))
    w_bcx = w_in[:, 3 * a + N_HEADS:]
    w_out = g_out.reshape(-1, d)
    w_down0 = g_down0.reshape(-1, d)
    w_down1 = g_down1.reshape(-1, d)
    pool_w = g_pool.transpose(1, 0, 2, 3).reshape(pool_w_1.shape[0], -1, pool_w_1.shape[2])
    conv_w = _pad_rows(g_conv.transpose(1, 0, 2).reshape(conv_w_0.shape[0], c_conv), 8)
    bf = jnp.pad(b_f_0, (0, 128 - N_HEADS)).reshape(1, 128)

    n0, qkv, fl, bcx = _ln_proj(xs, row(norm_mix_0), w_qkv, w_f, w_bcx)
    qa, ka = _gate_prep(fl, bf, qkv)
    o, lse = _attn_fwd(qa, ka, qkv)
    h1 = _conv_out(o, bcx, conv_w, w_out, xs)
    h2, a0, nf0 = _mlp_fwd(h1, row(norm_ffn_0), g_up0, w_down0, "mlp_fwd_0")
    h3 = _pool_fwd(h2, row(norm_mix_1), pool_w, row(pool_scale_1))
    h4, a1, nf1 = _mlp_fwd(h3, row(norm_ffn_1), g_up1, w_down1, "mlp_fwd_1")
    dh4, loss_part, d_final = _final_loss(h4, row(final_norm), target)

    slot_cols = g_up0.shape[2]
    da1, dz1, dh3, d_nffn1 = _mlp_bwd_x(dh4, a1, g_up1, w_down1, h3, row(norm_ffn_1), "mlp_bwd_x_1")
    dw_up1, dw_down1 = _mlp_bwd_w(nf1, da1, a1, dz1, slot_cols, "mlp_bwd_w_1")
    dh2, dw_pool, d_pscale, d_nmix1 = _pool_bwd(dh3, h2, row(norm_mix_1), pool_w, row(pool_scale_1))
    da0, dz0, dh1, d_nffn0 = _mlp_bwd_x(dh2, a0, g_up0, w_down0, h1, row(norm_ffn_0), "mlp_bwd_x_0")
    dw_up0, dw_down0 = _mlp_bwd_w(nf0, da0, a0, dz0, slot_cols, "mlp_bwd_w_0")
    do, delta, dbcx, dw_out, d_conv = _conv_out_bwd(dh1, w_out, o, bcx, conv_w)
    dqa, dka, dv = _attn_bwd(qa, ka, qkv, do, lse, delta)
    dqkv, dfl, d_bf = _gate_bwd(dqa, dka, dv, fl, bf)
    grad_x, d_nmix0 = _in_proj_bwd(dqkv, dfl, dbcx, w_qkv, w_f, w_bcx, xs, row(norm_mix_0), dh1)
    dw_qkv = _wgrad(n0, dqkv, "wgrad_qkv")
    dw_f = _wgrad(n0, dfl, "wgrad_f")
    dw_bcx = _wgrad(n0, dbcx, "wgrad_bcx")

    dw_in = jnp.concatenate([dw_qkv, dw_f[:, :N_HEADS], dw_bcx], axis=1)
    pool_cols = pool_w.shape[2]
    partial = [
        dw_in.reshape(d, N_CHIPS, -1).transpose(1, 0, 2),
        dw_out.reshape(N_CHIPS, -1, d),
        dw_up0,
        dw_down0.reshape(N_CHIPS, -1, d),
        dw_pool.reshape(pool_w.shape[0], N_CHIPS, -1, pool_cols).transpose(1, 0, 2, 3)
               .reshape(N_CHIPS, -1, pool_cols),
        dw_up1,
        dw_down1.reshape(N_CHIPS, -1, d),
    ]
    core = lax.axis_index("c").astype(jnp.int32).reshape(1)
    chip_index = (2 * lax.axis_index("x") + lax.axis_index("y")).astype(jnp.int32).reshape(1)
    reduced = _reduce_scatter(partial, [(5, 6, 4), (2, 3, 1, 0)], core, chip_index)
    moments = [(m_w_in_0, v_w_in_0), (m_w_out_0, v_w_out_0), (m_w_up_0, v_w_up_0),
               (m_w_down_0, v_w_down_0), (m_pool_w_1, v_pool_w_1), (m_w_up_1, v_w_up_1),
               (m_w_down_1, v_w_down_1)]
    big_out = []
    for k, (w, g, (m, v)) in enumerate(zip(big, reduced, moments)):
        flat = lambda t: t.reshape(-1, t.shape[-1])
        delta_w, new_m, new_v = _adamw(flat(w), flat(g), flat(m), flat(v), "adamw_%d" % k)
        big_out.append((g.reshape(w.shape), delta_w.reshape(w.shape), new_m.reshape(w.shape),
                        new_v.reshape(w.shape)))

    tail = jnp.concatenate([d_conv[0:3].reshape(-1)[d:], d_bf[0, :N_HEADS], loss_part[0, :1]])
    small_part = jnp.concatenate(
        [d_nmix0, d_nffn0, d_nmix1, d_pscale, d_nffn1, d_final,
         d_conv[0:3].reshape(1, -1)[:, :d],
         jnp.pad(tail, (0, d - tail.shape[0])).reshape(1, d)], axis=0)
    parts = _gather_small(small_part)

    chip = 2 * lax.axis_index("x") + lax.axis_index("y")
    cw_cols = conv_w_0.shape[1]

    def conv_block(full):
        mine = lax.dynamic_slice_in_dim(full, chip * cw_cols, cw_cols, axis=1)
        return jnp.pad(mine.reshape(-1), (0, d - mine.size))

    def small_rows(vals, cw, bfv):
        return jnp.stack(list(vals) + [cw, jnp.pad(bfv, (0, d - N_HEADS))])

    smalls_w = [norm_mix_0, norm_ffn_0, norm_mix_1, pool_scale_1, norm_ffn_1, final_norm]
    smalls_m = [m_norm_mix_0, m_norm_ffn_0, m_norm_mix_1, m_pool_scale_1, m_norm_ffn_1, m_final_norm]
    smalls_v = [v_norm_mix_0, v_norm_ffn_0, v_norm_mix_1, v_pool_scale_1, v_norm_ffn_1, v_final_norm]
    pad_cw = lambda t: jnp.pad(t.reshape(-1), (0, d - t.size))
    w_rows = small_rows(smalls_w, pad_cw(conv_w_0), b_f_0)
    m_rows = small_rows(smalls_m, pad_cw(m_conv_w_0), m_b_f_0)
    v_rows = small_rows(smalls_v, pad_cw(v_conv_w_0), v_b_f_0)

    g_sum = _sum_devices(parts)
    conv_full = jnp.concatenate([g_sum[6], g_sum[7, :3 * c_conv - d]]).reshape(3, c_conv)
    bf_grad = g_sum[7, 3 * c_conv - d:3 * c_conv - d + N_HEADS]
    loss = g_sum[7, 3 * c_conv - d + N_HEADS]
    g_rows = jnp.concatenate(
        [g_sum[0:6], conv_block(conv_full).reshape(1, d),
         jnp.pad(bf_grad, (0, d - N_HEADS)).reshape(1, d)], axis=0)
    d_rows, nm_rows, nv_rows = _adamw(w_rows, g_rows, m_rows, v_rows, "adamw_small")

    def unpack(rows):
        cw = rows[6, :conv_w_0.size].reshape(conv_w_0.shape)
        return [rows[0], rows[1], rows[2], rows[3], rows[4], rows[5], cw, rows[7, :N_HEADS]]

    def assemble(kind):
        sm = unpack([g_rows, d_rows, nm_rows, nv_rows][kind])
        lg = [t[kind] for t in big_out]
        return [sm[0], lg[0], sm[7], sm[6], lg[1], sm[1], lg[2], lg[3],
                sm[2], lg[4], sm[3], sm[4], lg[5], lg[6], sm[5]]

    return (loss, grad_x[None], *assemble(0), *assemble(1), *assemble(2), *assemble(3))
```

```python
import functools

import jax
import jax.numpy as jnp
from jax import lax
from jax.experimental import pallas as pl
from jax.experimental.pallas import tpu as pltpu

F32 = jnp.float32
BF16 = jnp.bfloat16

RMS_EPS = 1e-6
HEAD_DIM = 64
N_HEADS = 8
ATTN_SCALE = HEAD_DIM ** -0.5
POOL_WINDOWS = (2, 4, 8, 16)
POOL_HALO = 16
CONV_HALO = 8
NEG_BIG = -1e30

ADAM_LR = 0.001
ADAM_B1 = 0.9
ADAM_B2 = 0.999
ADAM_EPS = 1e-08
ADAM_WD = 0.01
ADAM_STEP = 10

N_CHIPS = 4
N_DEV = 8
MESH = pl.DeviceIdType.MESH

VMEM_LIMIT_BYTES = 56 * 1024 * 1024

TILE_ROWS = 512
TILE_ATTN = 512
TILE_MLP_ROWS = 1024
TILE_MLP_FF = 512
TILE_WGRAD_K = 512
TILE_WGRAD_N = 1024
TILE_ELEM_ROWS = 256

LANE_CQ = 64
LANE_ONE = 67


def _params(semantics):
    return pltpu.CompilerParams(dimension_semantics=semantics,
                                vmem_limit_bytes=VMEM_LIMIT_BYTES)


def _nn(a, b):
    return lax.dot_general(a, b, (((1,), (0,)), ((), ())), preferred_element_type=F32)


def _nt(a, b):
    return lax.dot_general(a, b, (((1,), (1,)), ((), ())), preferred_element_type=F32)


def _tn(a, b):
    return lax.dot_general(a, b, (((0,), (0,)), ((), ())), preferred_element_type=F32)


def _split3(v):
    hi = v.astype(BF16)
    r1 = v - hi.astype(F32)
    mid = r1.astype(BF16)
    lo = (r1 - mid.astype(F32)).astype(BF16)
    return hi, mid, lo


def _exact_nn(sel, v):
    hi, mid, lo = _split3(v)
    return _nn(sel, hi) + _nn(sel, mid) + _nn(sel, lo)


def _exact_nt(sel, v):
    hi, mid, lo = _split3(v)
    return _nt(sel, hi) + _nt(sel, mid) + _nt(sel, lo)


def _rms_fwd(x, g):
    r = lax.rsqrt(jnp.mean(x * x, axis=-1, keepdims=True) + RMS_EPS)
    return x * r * g, r


def _rms_bwd(dn, x, g):
    r = lax.rsqrt(jnp.mean(x * x, axis=-1, keepdims=True) + RMS_EPS)
    xh = x * r
    gy = dn * g
    dx = r * (gy - xh * jnp.mean(gy * xh, axis=-1, keepdims=True))
    return dx, jnp.sum(dn * xh, axis=0, keepdims=True)


def _lane(shape):
    return lax.broadcasted_iota(jnp.int32, shape, len(shape) - 1)


def _row(shape):
    return lax.broadcasted_iota(jnp.int32, shape, len(shape) - 2)


def _full(a):
    nd = a.ndim
    return pl.BlockSpec(a.shape, lambda *_: (0,) * nd)


def _ln_proj(x, g, w_qkv, w_f, w_bcx):
    s, d = x.shape
    tm = min(TILE_ROWS, s)

    def body(x_ref, g_ref, wq_ref, wf_ref, wb_ref, n_ref, qkv_ref, fl_ref, bcx_ref):
        n, _ = _rms_fwd(x_ref[...], g_ref[...])
        nb = n.astype(BF16)
        n_ref[...] = nb
        qkv_ref[...] = _nn(nb, wq_ref[...]).astype(BF16)
        fl_ref[...] = _nn(nb, wf_ref[...])
        bcx_ref[...] = _nn(nb, wb_ref[...])

    rows = lambda c: pl.BlockSpec((tm, c), lambda i: (i, 0))
    return pl.pallas_call(
        body, name="ln_proj", grid=(s // tm,),
        in_specs=[rows(d), _full(g), _full(w_qkv), _full(w_f), _full(w_bcx)],
        out_specs=[rows(d), rows(w_qkv.shape[1]), rows(w_f.shape[1]), rows(w_bcx.shape[1])],
        out_shape=[jax.ShapeDtypeStruct((s, d), BF16),
                   jax.ShapeDtypeStruct((s, w_qkv.shape[1]), BF16),
                   jax.ShapeDtypeStruct((s, w_f.shape[1]), F32),
                   jax.ShapeDtypeStruct((s, w_bcx.shape[1]), F32)],
        compiler_params=_params(("parallel",)),
    )(x, g, w_qkv, w_f, w_bcx)


def _gate_prep(fl, bf, qkv):
    s = fl.shape[0]
    a = N_HEADS * HEAD_DIM
    tm = min(TILE_ROWS, s)

    def body(fl_ref, bf_ref, q_ref, k_ref, qa_ref, ka_ref, carry_ref):
        i = pl.program_id(0)

        @pl.when(i == 0)
        def _():
            carry_ref[...] = jnp.zeros_like(carry_ref)

        z = fl_ref[...] + bf_ref[...]
        logf = jnp.minimum(z, 0.0) - jnp.log(1.0 + jnp.exp(-jnp.abs(z)))
        lower = (_lane((tm, tm)) <= _row((tm, tm))).astype(BF16)
        cum = _exact_nn(lower, logf) + carry_ref[0:1, :]
        carry_ref[0:1, :] = cum[tm - 1:tm, :]

        lane = _lane((tm, 128))
        for h in range(N_HEADS):
            cb = jnp.sum(jnp.where(lane == h, cum, 0.0), axis=1, keepdims=True)
            hi, mid, lo = (p.astype(F32) for p in _split3(cb))
            pair = slice((h // 2) * 128, (h // 2 + 1) * 128)
            qp = q_ref[:, pair].astype(F32)
            kp = k_ref[:, pair].astype(F32)
            if h % 2:
                qp = pltpu.roll(qp, HEAD_DIM, axis=1)
                kp = pltpu.roll(kp, HEAD_DIM, axis=1)
            q_bias = jnp.where(lane == LANE_CQ, hi,
                               jnp.where(lane == LANE_CQ + 1, mid,
                                         jnp.where(lane == LANE_CQ + 2, lo,
                                                   jnp.where(lane < LANE_ONE + 3, 1.0, 0.0))))
            k_bias = jnp.where(lane < LANE_ONE, 1.0,
                               jnp.where(lane == LANE_ONE, -hi,
                                         jnp.where(lane == LANE_ONE + 1, -mid,
                                                   jnp.where(lane == LANE_ONE + 2, -lo, 0.0))))
            qa_ref[h] = jnp.where(lane < HEAD_DIM, qp * ATTN_SCALE, q_bias).astype(BF16)
            ka_ref[h] = jnp.where(lane < HEAD_DIM, kp, k_bias).astype(BF16)

    aug = jax.ShapeDtypeStruct((N_HEADS, s, 128), BF16)
    aug_spec = pl.BlockSpec((N_HEADS, tm, 128), lambda i: (0, i, 0))
    return pl.pallas_call(
        body, name="gate_prep", grid=(s // tm,),
        in_specs=[pl.BlockSpec((tm, 128), lambda i: (i, 0)), _full(bf),
                  pl.BlockSpec((tm, a), lambda i: (i, 0)),
                  pl.BlockSpec((tm, a), lambda i: (i, 1))],
        out_specs=[aug_spec, aug_spec],
        out_shape=[aug, aug],
        scratch_shapes=[pltpu.VMEM((8, 128), F32)],
        compiler_params=_params(("arbitrary",)),
    )(fl, bf, qkv, qkv)


def _attn_fwd(qa, ka, qkv):
    s = qa.shape[1]
    a = N_HEADS * HEAD_DIM
    t = min(TILE_ATTN, s)
    n_pairs = N_HEADS // 2
    v_block0 = 2 * a // 128

    def body(qa_ref, ka_ref, v_ref, o_ref, lse_ref, m_ref, l_ref, acc_ref):
        i = pl.program_id(1)
        m_ref[...] = jnp.full_like(m_ref, NEG_BIG)
        l_ref[...] = jnp.zeros_like(l_ref)
        acc_ref[...] = jnp.zeros_like(acc_ref)
        upper_rows = _row((128, t)) < HEAD_DIM

        def kv_step(j, masked):
            ks = pl.ds(pl.multiple_of(j * t, t), t)
            vf = v_ref[ks, :].astype(F32)
            lane = _lane((t, 128))
            v_heads = [jnp.where(lane < HEAD_DIM, vf, 0.0).astype(BF16),
                       jnp.where(lane >= HEAD_DIM, vf, 0.0).astype(BF16)]
            alphas, update = [], None
            for e in range(2):
                sc = _nt(ka_ref[e, ks, :], qa_ref[e])
                if masked:
                    sc = jnp.where(_row((t, t)) <= _lane((t, t)), sc, NEG_BIG)
                m_prev = m_ref[e]
                m_new = jnp.maximum(m_prev, jnp.max(sc, axis=0, keepdims=True))
                p = jnp.exp(sc - m_new)
                alpha = jnp.exp(m_prev - m_new)
                l_ref[e] = alpha * l_ref[e] + jnp.sum(p, axis=0, keepdims=True)
                m_ref[e] = m_new
                alphas.append(alpha)
                pv = _tn(v_heads[e], p.astype(BF16))
                update = pv if update is None else update + pv
            acc_ref[...] = acc_ref[...] * jnp.where(upper_rows, alphas[0], alphas[1]) + update

        def full_step(j, carry):
            kv_step(j, False)
            return carry

        lax.fori_loop(0, i, full_step, 0)
        kv_step(i, True)

        out_t = acc_ref[...] / jnp.where(upper_rows, l_ref[0], l_ref[1])
        o_ref[...] = out_t.T.astype(BF16)
        lse = [m_ref[e] + jnp.log(l_ref[e]) for e in range(2)]
        lse_ref[...] = jnp.where(_row((8, t)) == 0, lse[0], lse[1])

    return pl.pallas_call(
        body, name="attn_fwd", grid=(n_pairs, s // t),
        in_specs=[pl.BlockSpec((2, t, 128), lambda g, i: (g, i, 0)),
                  pl.BlockSpec((2, s, 128), lambda g, i: (g, 0, 0)),
                  pl.BlockSpec((s, 128), lambda g, i: (0, v_block0 + g))],
        out_specs=[pl.BlockSpec((t, 128), lambda g, i: (i, g)),
                   pl.BlockSpec((None, 8, t), lambda g, i: (g, 0, i))],
        out_shape=[jax.ShapeDtypeStruct((s, a), BF16),
                   jax.ShapeDtypeStruct((n_pairs, 8, s), F32)],
        scratch_shapes=[pltpu.VMEM((2, 1, t), F32), pltpu.VMEM((2, 1, t), F32),
                        pltpu.VMEM((128, t), F32)],
        compiler_params=_params(("parallel", "arbitrary")),
    )(qa, ka, qkv)


def _conv_out(o, bcx, cw, w_out, x):
    s, d = x.shape
    c = o.shape[1]
    tm = min(TILE_ROWS, s)

    def body(o_ref, b_ref, c_ref, xin_ref, cw_ref, w_ref, x_ref, h_ref, ubuf):
        i = pl.program_id(0)

        @pl.when(i == 0)
        def _():
            ubuf[0:CONV_HALO, :] = jnp.zeros((CONV_HALO, c), F32)

        u = c_ref[...] * xin_ref[...]
        ubuf[CONV_HALO:CONV_HALO + tm, :] = u
        u1 = ubuf[CONV_HALO - 1:CONV_HALO - 1 + tm, :]
        u2 = ubuf[CONV_HALO - 2:CONV_HALO - 2 + tm, :]
        cv = (cw_ref[0:1, :] * u2 + cw_ref[1:2, :] * u1) + cw_ref[2:3, :] * u
        y = (b_ref[...] * cv).astype(BF16)
        mix = _nn(o_ref[...], w_ref[0:c, :]) + _nn(y, w_ref[c:2 * c, :])
        h_ref[...] = x_ref[...] + mix
        ubuf[0:CONV_HALO, :] = u[tm - CONV_HALO:tm, :]

    col = lambda k: pl.BlockSpec((tm, c), lambda i: (i, k))
    return pl.pallas_call(
        body, name="conv_out", grid=(s // tm,),
        in_specs=[col(0), col(0), col(1), col(2), _full(cw), _full(w_out),
                  pl.BlockSpec((tm, d), lambda i: (i, 0))],
        out_specs=pl.BlockSpec((tm, d), lambda i: (i, 0)),
        out_shape=jax.ShapeDtypeStruct((s, d), F32),
        scratch_shapes=[pltpu.VMEM((tm + CONV_HALO, c), F32)],
        compiler_params=_params(("arbitrary",)),
    )(o, bcx, bcx, bcx, cw, w_out, x)


def _mlp_fwd(h, g, w_up, w_down, name):
    s, d = h.shape
    ff = w_down.shape[0]
    slot_cols = w_up.shape[2]
    tm = min(TILE_MLP_ROWS, s)
    tf = min(TILE_MLP_FF, slot_cols)
    per_slot = slot_cols // tf
    nf = ff // tf

    def body(h_ref, g_ref, wu_ref, wd_ref, out_ref, a_ref, n_ref, nb_ref, acc_ref):
        f = pl.program_id(1)

        @pl.when(f == 0)
        def _():
            n, _ = _rms_fwd(h_ref[...], g_ref[...])
            nb = n.astype(BF16)
            nb_ref[...] = nb
            n_ref[...] = nb
            acc_ref[...] = jnp.zeros_like(acc_ref)

        pre = _nn(nb_ref[...], wu_ref[...])
        a_ref[...] = pre.astype(BF16)
        r = jnp.square(jnp.maximum(pre, 0.0)).astype(BF16)
        acc_ref[...] += _nn(r, wd_ref[...])

        @pl.when(f == nf - 1)
        def _():
            out_ref[...] = h_ref[...] + acc_ref[...]

    return pl.pallas_call(
        body, name=name, grid=(s // tm, nf),
        in_specs=[pl.BlockSpec((tm, d), lambda i, f: (i, 0)), _full(g),
                  pl.BlockSpec((None, d, tf), lambda i, f: (f // per_slot, 0, f % per_slot)),
                  pl.BlockSpec((tf, d), lambda i, f: (f, 0))],
        out_specs=[pl.BlockSpec((tm, d), lambda i, f: (i, 0)),
                   pl.BlockSpec((tm, tf), lambda i, f: (i, f)),
                   pl.BlockSpec((tm, d), lambda i, f: (i, 0))],
        out_shape=[jax.ShapeDtypeStruct((s, d), F32),
                   jax.ShapeDtypeStruct((s, ff), BF16),
                   jax.ShapeDtypeStruct((s, d), BF16)],
        scratch_shapes=[pltpu.VMEM((tm, d), BF16), pltpu.VMEM((tm, d), F32)],
        compiler_params=_params(("parallel", "arbitrary")),
    )(h, g, w_up, w_down)


def _window_sum_down(e, window):
    step = 1
    while step < window:
        e = e + pltpu.roll(e, step, axis=0)
        step *= 2
    return e


def _window_sum_up(e, window):
    n = e.shape[0]
    step = 1
    while step < window:
        e = e + pltpu.roll(e, n - step, axis=0)
        step *= 2
    return e


def _pool_counts(first_row, tm, window):
    t = first_row + _row((tm, 1))
    return jnp.minimum(t + 1, window).astype(F32)


def _pool_fwd(h, g, pw, ps):
    s, d = h.shape
    cg = d // len(POOL_WINDOWS)
    tm = min(TILE_ROWS, s)

    def body(h_ref, g_ref, pw_ref, ps_ref, out_ref, nbuf):
        i = pl.program_id(0)

        @pl.when(i == 0)
        def _():
            nbuf[0:POOL_HALO, :] = jnp.zeros((POOL_HALO, d), F32)

        n, _ = _rms_fwd(h_ref[...], g_ref[...])
        nbuf[POOL_HALO:POOL_HALO + tm, :] = n
        for k, window in enumerate(POOL_WINDOWS):
            cols = slice(k * cg, (k + 1) * cg)
            sums = _window_sum_down(nbuf[:, cols], window)[POOL_HALO:, :]
            pooled = sums / _pool_counts(i * tm, tm, window) - n[:, cols]
            y = _nn(pooled.astype(BF16), pw_ref[k]) * ps_ref[:, cols]
            out_ref[:, cols] = h_ref[:, cols] + y
        nbuf[0:POOL_HALO, :] = n[tm - POOL_HALO:tm, :]

    return pl.pallas_call(
        body, name="pool_fwd", grid=(s // tm,),
        in_specs=[pl.BlockSpec((tm, d), lambda i: (i, 0)), _full(g), _full(pw), _full(ps)],
        out_specs=pl.BlockSpec((tm, d), lambda i: (i, 0)),
        out_shape=jax.ShapeDtypeStruct((s, d), F32),
        scratch_shapes=[pltpu.VMEM((tm + POOL_HALO, d), F32)],
        compiler_params=_params(("arbitrary",)),
    )(h, g, pw, ps)


def _final_loss(h, g, target):
    s, d = h.shape
    tm = min(TILE_ROWS, s)

    def body(h_ref, g_ref, t_ref, dh_ref, loss_ref, dg_ref):
        i = pl.program_id(0)
        hv = h_ref[...]
        y, _ = _rms_fwd(hv, g_ref[...])
        err = y - t_ref[...]
        part = 0.5 * jnp.sum(jnp.mean(err * err, axis=-1, keepdims=True), axis=0, keepdims=True)
        dx, dg = _rms_bwd(err / d, hv, g_ref[...])
        dh_ref[...] = dx
        part = jnp.broadcast_to(part, loss_ref.shape)

        @pl.when(i == 0)
        def _():
            loss_ref[...] = part
            dg_ref[...] = dg

        @pl.when(i > 0)
        def _():
            loss_ref[...] += part
            dg_ref[...] += dg

    return pl.pallas_call(
        body, name="final_loss", grid=(s // tm,),
        in_specs=[pl.BlockSpec((tm, d), lambda i: (i, 0)), _full(g),
                  pl.BlockSpec((tm, d), lambda i: (i, 0))],
        out_specs=[pl.BlockSpec((tm, d), lambda i: (i, 0)),
                   pl.BlockSpec((1, 128), lambda i: (0, 0)),
                   pl.BlockSpec((1, d), lambda i: (0, 0))],
        out_shape=[jax.ShapeDtypeStruct((s, d), F32),
                   jax.ShapeDtypeStruct((1, 128), F32),
                   jax.ShapeDtypeStruct((1, d), F32)],
        compiler_params=_params(("arbitrary",)),
    )(h, g, target)


def _mlp_bwd_x(dz, a, w_up, w_down, h_in, g, name):
    s, d = dz.shape
    ff = w_down.shape[0]
    slot_cols = w_up.shape[2]
    tm = min(TILE_MLP_ROWS, s)
    tf = min(TILE_MLP_FF, slot_cols)
    per_slot = slot_cols // tf
    nf = ff // tf

    def body(dz_ref, a_ref, wu_ref, wd_ref, h_ref, g_ref, da_ref, dzb_ref, dh_ref, dg_ref,
             dzs_ref, acc_ref):
        i = pl.program_id(0)
        f = pl.program_id(1)

        @pl.when(f == 0)
        def _():
            dzb = dz_ref[...].astype(BF16)
            dzs_ref[...] = dzb
            dzb_ref[...] = dzb
            acc_ref[...] = jnp.zeros_like(acc_ref)

        dr = _nt(dzs_ref[...], wd_ref[...])
        da = (dr * (2.0 * jnp.maximum(a_ref[...].astype(F32), 0.0))).astype(BF16)
        da_ref[...] = da
        acc_ref[...] += _nt(da, wu_ref[...])

        @pl.when(f == nf - 1)
        def _():
            dx, dg = _rms_bwd(acc_ref[...], h_ref[...], g_ref[...])
            dh_ref[...] = dz_ref[...] + dx

            @pl.when(i == 0)
            def _():
                dg_ref[...] = dg

            @pl.when(i > 0)
            def _():
                dg_ref[...] += dg

    return pl.pallas_call(
        body, name=name, grid=(s // tm, nf),
        in_specs=[pl.BlockSpec((tm, d), lambda i, f: (i, 0)),
                  pl.BlockSpec((tm, tf), lambda i, f: (i, f)),
                  pl.BlockSpec((None, d, tf), lambda i, f: (f // per_slot, 0, f % per_slot)),
                  pl.BlockSpec((tf, d), lambda i, f: (f, 0)),
                  pl.BlockSpec((tm, d), lambda i, f: (i, 0)), _full(g)],
        out_specs=[pl.BlockSpec((tm, tf), lambda i, f: (i, f)),
                   pl.BlockSpec((tm, d), lambda i, f: (i, 0)),
                   pl.BlockSpec((tm, d), lambda i, f: (i, 0)),
                   pl.BlockSpec((1, d), lambda i, f: (0, 0))],
        out_shape=[jax.ShapeDtypeStruct((s, ff), BF16),
                   jax.ShapeDtypeStruct((s, d), BF16),
                   jax.ShapeDtypeStruct((s, d), F32),
                   jax.ShapeDtypeStruct((1, d), F32)],
        scratch_shapes=[pltpu.VMEM((tm, d), BF16), pltpu.VMEM((tm, d), F32)],
        compiler_params=_params(("arbitrary", "arbitrary")),
    )(dz, a, w_up, w_down, h_in, g)


def _mlp_bwd_w(n, da, a, dzb, slot_cols, name):
    s, d = n.shape
    ff = a.shape[1]
    tn = min(TILE_WGRAD_N, slot_cols)
    tk = min(TILE_WGRAD_K, s)
    per_slot = slot_cols // tn
    nk = s // tk

    def body(n_ref, da_ref, a_ref, dz_ref, du_ref, dd_ref, accu_ref, accd_ref):
        k = pl.program_id(1)

        @pl.when(k == 0)
        def _():
            accu_ref[...] = jnp.zeros_like(accu_ref)
            accd_ref[...] = jnp.zeros_like(accd_ref)

        accu_ref[...] += _tn(n_ref[...], da_ref[...])
        r = jnp.square(jnp.maximum(a_ref[...].astype(F32), 0.0)).astype(BF16)
        accd_ref[...] += _tn(r, dz_ref[...])

        @pl.when(k == nk - 1)
        def _():
            du_ref[...] = accu_ref[...].astype(BF16)
            dd_ref[...] = accd_ref[...].astype(BF16)

    return pl.pallas_call(
        body, name=name, grid=(ff // tn, nk),
        in_specs=[pl.BlockSpec((tk, d), lambda f, k: (k, 0)),
                  pl.BlockSpec((tk, tn), lambda f, k: (k, f)),
                  pl.BlockSpec((tk, tn), lambda f, k: (k, f)),
                  pl.BlockSpec((tk, d), lambda f, k: (k, 0))],
        out_specs=[pl.BlockSpec((None, d, tn), lambda f, k: (f // per_slot, 0, f % per_slot)),
                   pl.BlockSpec((tn, d), lambda f, k: (f, 0))],
        out_shape=[jax.ShapeDtypeStruct((ff // slot_cols, d, slot_cols), BF16),
                   jax.ShapeDtypeStruct((ff, d), BF16)],
        scratch_shapes=[pltpu.VMEM((d, tn), F32), pltpu.VMEM((tn, d), F32)],
        compiler_params=_params(("parallel", "arbitrary")),
    )(n, da, a, dzb)


def _pool_bwd(after, dh, h, g, pw, ps):
    s, d = h.shape
    cg = d // len(POOL_WINDOWS)
    tm = min(TILE_ROWS, s)
    nb = s // tm
    halo_per_tile = tm // POOL_HALO

    def body(after_ref, dh_ref, h_ref, halo_ref, g_ref, pw_ref, ps_ref,
             dx_ref, dpw_ref, dps_ref, dg_ref, nbuf, qbuf, dn_ref, carry, dpw_acc):
        i = pl.program_id(0)
        blk = nb - 1 - i

        @pl.when(i == 0)
        def _():
            carry[...] = jnp.zeros_like(carry)
            dpw_acc[...] = jnp.zeros_like(dpw_acc)
            dps_ref[...] = jnp.zeros_like(dps_ref)
            dg_ref[...] = jnp.zeros_like(dg_ref)

        hv = h_ref[...]
        n, _ = _rms_fwd(hv, g_ref[...])
        nh, _ = _rms_fwd(halo_ref[...], g_ref[...])
        nbuf[0:POOL_HALO, :] = jnp.where(blk == 0, 0.0, nh)
        nbuf[POOL_HALO:POOL_HALO + tm, :] = n
        dhv = dh_ref[...]
        for k, window in enumerate(POOL_WINDOWS):
            cols = slice(k * cg, (k + 1) * cg)
            cnt = _pool_counts(blk * tm, tm, window)
            sums = _window_sum_down(nbuf[:, cols], window)[POOL_HALO:, :]
            pb = (sums / cnt - n[:, cols]).astype(BF16)
            dyk = dhv[:, cols]
            dps_ref[:, cols] += jnp.sum(dyk * _nn(pb, pw_ref[k]), axis=0, keepdims=True)
            dyb = (dyk * ps_ref[:, cols]).astype(BF16)
            dpw_acc[k] += _tn(pb, dyb)
            dpool = _nt(dyb, pw_ref[k])
            qv = dpool / cnt
            qbuf[0:tm, cols] = qv
            qbuf[tm:tm + POOL_HALO, cols] = carry[:, cols]
            dn_ref[:, cols] = _window_sum_up(qbuf[:, cols], window)[0:tm, :] - dpool
            carry[:, cols] = qv[0:POOL_HALO, :]
        dx, dg = _rms_bwd(dn_ref[...], hv, g_ref[...])
        dx_ref[...] = dhv + dx
        dg_ref[...] += dg

        @pl.when(i == nb - 1)
        def _():
            dpw_ref[...] = dpw_acc[...].astype(BF16)

    rev = lambda i: (nb - 1 - i, 0)
    return pl.pallas_call(
        body, name="pool_bwd", grid=(nb,),
        in_specs=[ANY, pl.BlockSpec((tm, d), rev), pl.BlockSpec((tm, d), rev),
                  pl.BlockSpec((POOL_HALO, d),
                               lambda i: (jnp.maximum((nb - 1 - i) * halo_per_tile - 1, 0), 0)),
                  _full(g), _full(pw), _full(ps)],
        out_specs=[pl.BlockSpec((tm, d), rev), _full(pw),
                   pl.BlockSpec((1, d), lambda i: (0, 0)),
                   pl.BlockSpec((1, d), lambda i: (0, 0))],
        out_shape=[jax.ShapeDtypeStruct((s, d), F32),
                   jax.ShapeDtypeStruct(pw.shape, BF16),
                   jax.ShapeDtypeStruct((1, d), F32),
                   jax.ShapeDtypeStruct((1, d), F32)],
        scratch_shapes=[pltpu.VMEM((tm + POOL_HALO, d), F32), pltpu.VMEM((tm + POOL_HALO, d), F32),
                        pltpu.VMEM((tm, d), F32), pltpu.VMEM((POOL_HALO, d), F32),
                        pltpu.VMEM(pw.shape, F32)],
        compiler_params=_params(("arbitrary",)),
    )(after, dh, h, h, g, pw, ps)


def _conv_out_bwd(after, dh, w_out, o, bcx, cw):
    s, d = dh.shape
    c = o.shape[1]
    tm = min(TILE_ROWS, s)
    nb = s // tm
    halo_per_tile = tm // CONV_HALO

    def body(after_ref, dh_ref, w_ref, o_ref, b_ref, c_ref, xin_ref, ch_ref, xh_ref, cw_ref,
             do_ref, delta_ref, dbcx_ref, dw_ref, dcw_ref, ubuf, dbuf, carry, acc):
        i = pl.program_id(0)
        blk = nb - 1 - i

        @pl.when(i == 0)
        def _():
            carry[...] = jnp.zeros_like(carry)
            acc[...] = jnp.zeros_like(acc)
            dcw_ref[...] = jnp.zeros_like(dcw_ref)

        dm = dh_ref[...].astype(BF16)
        dcat = _nt(dm, w_ref[...])
        do = dcat[:, 0:c]
        dy = dcat[:, c:2 * c]
        do_ref[...] = do.astype(BF16)
        head_of_lane = lax.shift_right_logical(_lane((8, c)), HEAD_DIM.bit_length() - 1)
        heads = (head_of_lane == _row((8, c))).astype(BF16)
        delta_ref[...] = _exact_nt(heads, do * o_ref[...].astype(F32))

        cv_ = c_ref[...]
        xin = xin_ref[...]
        bv = b_ref[...]
        u = cv_ * xin
        ubuf[0:CONV_HALO, :] = jnp.where(blk == 0, 0.0, ch_ref[...] * xh_ref[...])
        ubuf[CONV_HALO:CONV_HALO + tm, :] = u
        u1 = ubuf[CONV_HALO - 1:CONV_HALO - 1 + tm, :]
        u2 = ubuf[CONV_HALO - 2:CONV_HALO - 2 + tm, :]
        w0, w1, w2 = cw_ref[0:1, :], cw_ref[1:2, :], cw_ref[2:3, :]
        cv = (w0 * u2 + w1 * u1) + w2 * u
        acc[0:c, :] += _tn(o_ref[...], dm)
        acc[c:2 * c, :] += _tn((bv * cv).astype(BF16), dm)

        dcv = dy * bv
        dcw_ref[0:1, :] += jnp.sum(dcv * u2, axis=0, keepdims=True)
        dcw_ref[1:2, :] += jnp.sum(dcv * u1, axis=0, keepdims=True)
        dcw_ref[2:3, :] += jnp.sum(dcv * u, axis=0, keepdims=True)
        dbuf[0:tm, :] = dcv
        dbuf[tm:tm + CONV_HALO, :] = carry[...]
        du = w2 * dcv + w1 * dbuf[1:1 + tm, :] + w0 * dbuf[2:2 + tm, :]
        dbcx_ref[:, 0:c] = (dy * cv).astype(BF16)
        dbcx_ref[:, c:2 * c] = (du * xin).astype(BF16)
        dbcx_ref[:, 2 * c:3 * c] = (du * cv_).astype(BF16)
        carry[...] = dcv[0:CONV_HALO, :]

        @pl.when(i == nb - 1)
        def _():
            dw_ref[...] = acc[...].astype(BF16)

    rev = lambda k: (lambda i: (nb - 1 - i, k))
    halo = lambda k: (lambda i: (jnp.maximum((nb - 1 - i) * halo_per_tile - 1, 0), k))
    return pl.pallas_call(
        body, name="conv_out_bwd", grid=(nb,),
        in_specs=[ANY, pl.BlockSpec((tm, d), rev(0)), _full(w_out), pl.BlockSpec((tm, c), rev(0)),
                  pl.BlockSpec((tm, c), rev(0)), pl.BlockSpec((tm, c), rev(1)),
                  pl.BlockSpec((tm, c), rev(2)),
                  pl.BlockSpec((CONV_HALO, c), halo(1)), pl.BlockSpec((CONV_HALO, c), halo(2)),
                  _full(cw)],
        out_specs=[pl.BlockSpec((tm, c), rev(0)),
                   pl.BlockSpec((8, tm), lambda i: (0, nb - 1 - i)),
                   pl.BlockSpec((tm, 3 * c), rev(0)),
                   _full(w_out), _full(cw)],
        out_shape=[jax.ShapeDtypeStruct((s, c), BF16),
                   jax.ShapeDtypeStruct((8, s), F32),
                   jax.ShapeDtypeStruct((s, 3 * c), BF16),
                   jax.ShapeDtypeStruct(w_out.shape, BF16),
                   jax.ShapeDtypeStruct(cw.shape, F32)],
        scratch_shapes=[pltpu.VMEM((tm + CONV_HALO, c), F32), pltpu.VMEM((tm + CONV_HALO, c), F32),
                        pltpu.VMEM((CONV_HALO, c), F32), pltpu.VMEM(w_out.shape, F32)],
        compiler_params=_params(("arbitrary",)),
    )(after, dh, w_out, o, bcx, bcx, bcx, bcx, bcx, cw)


def _attn_bwd(qa, ka, qkv, do, lse, delta):
    s = qa.shape[1]
    a = N_HEADS * HEAD_DIM
    t = min(TILE_ATTN, s)
    nq = s // t
    n_pairs = N_HEADS // 2
    v_block0 = 2 * a // 128

    def body(ka_ref, v_ref, qa_ref, do_ref, lse_ref, delta_ref,
             dqa_ref, dka_ref, dv_ref, dk_acc, dv_acc):
        g = pl.program_id(0)
        j = pl.program_id(1)

        @pl.when(j == 0)
        def _():
            dqa_ref[...] = jnp.zeros_like(dqa_ref)

        dk_acc[...] = jnp.zeros_like(dk_acc)
        dv_acc[...] = jnp.zeros_like(dv_acc)
        lane = _lane((t, 128))
        vf = v_ref[...].astype(F32)
        v_heads = [jnp.where(lane < HEAD_DIM, vf, 0.0).astype(BF16),
                   jnp.where(lane >= HEAD_DIM, vf, 0.0).astype(BF16)]

        def q_step(i, masked):
            qs = pl.ds(pl.multiple_of(i * t, t), t)
            dob = do_ref[qs, :]
            for e in range(2):
                ke = ka_ref[e]
                qe = qa_ref[e, qs, :]
                sc = _nt(ke, qe)
                if masked:
                    sc = jnp.where(_row((t, t)) <= _lane((t, t)), sc, NEG_BIG)
                p = jnp.exp(sc - lse_ref[pl.ds(e, 1), qs])
                dv_acc[e] += _nn(p.astype(BF16), dob)
                dp = _nt(v_heads[e], dob)
                ds = (p * (dp - delta_ref[pl.ds(2 * g + e, 1), qs])).astype(BF16)
                dk_acc[e] += _nn(ds, qe)
                dqa_ref[e, qs, :] += _tn(ds, ke)

        q_step(j, True)

        def full_step(i, carry):
            q_step(i, False)
            return carry

        lax.fori_loop(j + 1, nq, full_step, 0)
        dka_ref[...] = dk_acc[...]
        dv_ref[...] = jnp.where(lane < HEAD_DIM, dv_acc[0], dv_acc[1]).astype(BF16)

    return pl.pallas_call(
        body, name="attn_bwd", grid=(n_pairs, nq),
        in_specs=[pl.BlockSpec((2, t, 128), lambda g, j: (g, j, 0)),
                  pl.BlockSpec((t, 128), lambda g, j: (j, v_block0 + g)),
                  pl.BlockSpec((2, s, 128), lambda g, j: (g, 0, 0)),
                  pl.BlockSpec((s, 128), lambda g, j: (0, g)),
                  pl.BlockSpec((None, 8, s), lambda g, j: (g, 0, 0)),
                  pl.BlockSpec((8, s), lambda g, j: (0, 0))],
        out_specs=[pl.BlockSpec((2, s, 128), lambda g, j: (g, 0, 0)),
                   pl.BlockSpec((2, t, 128), lambda g, j: (g, j, 0)),
                   pl.BlockSpec((t, 128), lambda g, j: (j, g))],
        out_shape=[jax.ShapeDtypeStruct((N_HEADS, s, 128), F32),
                   jax.ShapeDtypeStruct((N_HEADS, s, 128), F32),
                   jax.ShapeDtypeStruct((s, a), BF16)],
        scratch_shapes=[pltpu.VMEM((2, t, 128), F32), pltpu.VMEM((2, t, 128), F32)],
        compiler_params=_params(("parallel", "arbitrary")),
    )(ka, qkv, qa, do, lse, delta)


def _gate_bwd(dqa, dka, dv, fl, bf):
    s = fl.shape[0]
    a = N_HEADS * HEAD_DIM
    tm = min(TILE_ROWS, s)
    nb = s // tm

    def body(dqa_ref, dka_ref, dv_ref, fl_ref, bf_ref, dqkv_ref, dfl_ref, dbf_ref, carry):
        i = pl.program_id(0)

        @pl.when(i == 0)
        def _():
            carry[...] = jnp.zeros_like(carry)
            dbf_ref[...] = jnp.zeros_like(dbf_ref)

        lane = _lane((tm, 128))
        dcum = jnp.zeros((tm, 128), F32)
        for pair in range(N_HEADS // 2):
            qs, ks = [], []
            for e in range(2):
                h = 2 * pair + e
                dq = dqa_ref[h]
                dk = dka_ref[h]
                dc = jnp.sum(jnp.where(lane == LANE_CQ, dq, 0.0)
                             - jnp.where(lane == LANE_ONE, dk, 0.0), axis=1, keepdims=True)
                dcum = jnp.where(lane == h, dc, dcum)
                qs.append(dq * ATTN_SCALE)
                ks.append(dk)
            cols = slice(pair * 128, (pair + 1) * 128)
            dqkv_ref[:, cols] = jnp.where(
                lane < HEAD_DIM, qs[0], pltpu.roll(qs[1], HEAD_DIM, axis=1)).astype(BF16)
            dqkv_ref[:, a + pair * 128:a + (pair + 1) * 128] = jnp.where(
                lane < HEAD_DIM, ks[0], pltpu.roll(ks[1], HEAD_DIM, axis=1)).astype(BF16)
        dqkv_ref[:, 2 * a:3 * a] = dv_ref[...]

        upper = (_lane((tm, tm)) >= _row((tm, tm))).astype(BF16)
        dlogf = _exact_nn(upper, dcum) + carry[0:1, :]
        carry[0:1, :] = dlogf[0:1, :]
        z = fl_ref[...] + bf_ref[...]
        ez = jnp.exp(-jnp.abs(z))
        sig_neg = jnp.where(z >= 0.0, ez, 1.0) / (1.0 + ez)
        dz = jnp.where(lane < N_HEADS, dlogf * sig_neg, 0.0)
        dfl_ref[...] = dz.astype(BF16)
        dbf_ref[...] += jnp.sum(dz, axis=0, keepdims=True)

    rev3 = lambda i: (0, nb - 1 - i, 0)
    rev = lambda i: (nb - 1 - i, 0)
    return pl.pallas_call(
        body, name="gate_bwd", grid=(nb,),
        in_specs=[pl.BlockSpec((N_HEADS, tm, 128), rev3), pl.BlockSpec((N_HEADS, tm, 128), rev3),
                  pl.BlockSpec((tm, a), rev), pl.BlockSpec((tm, 128), rev), _full(bf)],
        out_specs=[pl.BlockSpec((tm, 3 * a), rev), pl.BlockSpec((tm, 128), rev),
                   pl.BlockSpec((1, 128), lambda i: (0, 0))],
        out_shape=[jax.ShapeDtypeStruct((s, 3 * a), BF16),
                   jax.ShapeDtypeStruct((s, 128), BF16),
                   jax.ShapeDtypeStruct((1, 128), F32)],
        scratch_shapes=[pltpu.VMEM((8, 128), F32)],
        compiler_params=_params(("arbitrary",)),
    )(dqa, dka, dv, fl, bf)


def _in_proj_bwd(after, dqkv, dfl, dbcx, w_qkv, w_f, w_bcx, x, g, dh):
    s, d = x.shape
    tm = min(TILE_ROWS, s)

    def body(after_ref, dq_ref, df_ref, db_ref, wq_ref, wf_ref, wb_ref, x_ref, g_ref, dh_ref,
             gx_ref, dg_ref):
        i = pl.program_id(0)
        dn = (_nt(dq_ref[...], wq_ref[...]) + _nt(df_ref[...], wf_ref[...])
              + _nt(db_ref[...], wb_ref[...]))
        dx, dg = _rms_bwd(dn, x_ref[...], g_ref[...])
        gx_ref[...] = dh_ref[...] + dx

        @pl.when(i == 0)
        def _():
            dg_ref[...] = dg

        @pl.when(i > 0)
        def _():
            dg_ref[...] += dg

    rows = lambda c: pl.BlockSpec((tm, c), lambda i: (i, 0))
    return pl.pallas_call(
        body, name="in_proj_bwd", grid=(s // tm,),
        in_specs=[ANY, rows(dqkv.shape[1]), rows(dfl.shape[1]), rows(dbcx.shape[1]),
                  _full(w_qkv), _full(w_f), _full(w_bcx), rows(d), _full(g), rows(d)],
        out_specs=[rows(d), pl.BlockSpec((1, d), lambda i: (0, 0))],
        out_shape=[jax.ShapeDtypeStruct((s, d), F32), jax.ShapeDtypeStruct((1, d), F32)],
        compiler_params=_params(("arbitrary",)),
    )(after, dqkv, dfl, dbcx, w_qkv, w_f, w_bcx, x, g, dh)


def _wgrad(n, dy, name):
    s, d = n.shape
    cols = dy.shape[1]
    tn = min(512, cols)
    tk = min(TILE_WGRAD_K, s)
    nk = s // tk

    def body(n_ref, dy_ref, dw_ref, acc):
        k = pl.program_id(1)

        @pl.when(k == 0)
        def _():
            acc[...] = jnp.zeros_like(acc)

        acc[...] += _tn(n_ref[...], dy_ref[...])

        @pl.when(k == nk - 1)
        def _():
            dw_ref[...] = acc[...].astype(BF16)

    return pl.pallas_call(
        body, name=name, grid=(cols // tn, nk),
        in_specs=[pl.BlockSpec((tk, d), lambda f, k: (k, 0)),
                  pl.BlockSpec((tk, tn), lambda f, k: (k, f))],
        out_specs=pl.BlockSpec((d, tn), lambda f, k: (0, f)),
        out_shape=jax.ShapeDtypeStruct((d, cols), BF16),
        scratch_shapes=[pltpu.VMEM((d, tn), F32)],
        compiler_params=_params(("parallel", "arbitrary")),
    )(n, dy)


def _row_tile(rows):
    t = min(TILE_ELEM_ROWS, rows)
    while rows % t:
        t //= 2
    return t


def _sum_pair(grad, theirs, core, name):
    slots, rows, cols = theirs.shape
    tr = _row_tile(rows)
    nb = rows // tr

    def body(core_ref, a_ref, b_ref, o_ref):
        o_ref[...] = (a_ref[...].astype(F32) + b_ref[...].astype(F32)).astype(BF16)

    spec = pl.BlockSpec((None, tr, cols), lambda s, i, core_ref: (s, i, 0))
    return pl.pallas_call(
        body, name=name,
        grid_spec=pltpu.PrefetchScalarGridSpec(
            num_scalar_prefetch=1, grid=(slots, nb),
            in_specs=[pl.BlockSpec((None, tr, cols),
                                   lambda s, i, core_ref: (s, core_ref[0] * nb + i, 0)), spec],
            out_specs=spec),
        out_shape=jax.ShapeDtypeStruct(theirs.shape, BF16),
        compiler_params=_params(("parallel", "parallel")),
    )(core, grad, theirs)


def _sum_chips(sums, others, chip, name):
    _, rows, cols = sums.shape
    tr = _row_tile(rows)

    def body(chip_ref, a_ref, b_ref, o_ref):
        acc = a_ref[...].astype(F32)
        for k in range(N_CHIPS - 1):
            acc = acc + b_ref[k].astype(F32)
        o_ref[...] = acc

    return pl.pallas_call(
        body, name=name,
        grid_spec=pltpu.PrefetchScalarGridSpec(
            num_scalar_prefetch=1, grid=(rows // tr,),
            in_specs=[pl.BlockSpec((None, tr, cols), lambda i, chip_ref: (chip_ref[0], i, 0)),
                      pl.BlockSpec((N_CHIPS - 1, tr, cols), lambda i, chip_ref: (0, i, 0))],
            out_specs=pl.BlockSpec((tr, cols), lambda i, chip_ref: (i, 0))),
        out_shape=jax.ShapeDtypeStruct((rows, cols), F32),
        compiler_params=_params(("parallel",)),
    )(chip, sums, others)


def _adamw_math(w, g, m, v):
    m = ADAM_B1 * m + (1.0 - ADAM_B1) * g
    v = ADAM_B2 * v + (1.0 - ADAM_B2) * jnp.square(g)
    m_hat = m / (1.0 - ADAM_B1 ** ADAM_STEP)
    v_hat = v / (1.0 - ADAM_B2 ** ADAM_STEP)
    delta = -ADAM_LR * (m_hat / (jnp.sqrt(v_hat) + ADAM_EPS) + ADAM_WD * w)
    return delta, m, v


def _adamw(w, g, m, v, name):
    rows, cols = w.shape
    tr = _row_tile(rows)

    def body(w_ref, g_ref, m_ref, v_ref, d_ref, nm_ref, nv_ref):
        delta, nm, nv = _adamw_math(w_ref[...], g_ref[...], m_ref[...], v_ref[...])
        d_ref[...] = delta
        nm_ref[...] = nm
        nv_ref[...] = nv

    spec = pl.BlockSpec((tr, cols), lambda i: (i, 0))
    out = jax.ShapeDtypeStruct(w.shape, F32)
    return pl.pallas_call(
        body, name=name, grid=(rows // tr,), in_specs=[spec] * 4, out_specs=[spec] * 3,
        out_shape=[out, out, out], compiler_params=_params(("parallel",)),
    )(w, g, m, v)


def _sum_devices(parts):
    def body(p_ref, g_ref):
        g = p_ref[0]
        for k in range(1, N_DEV):
            g = g + p_ref[k]
        g_ref[...] = g

    return pl.pallas_call(
        body, name="sum_devices",
        in_specs=[pl.BlockSpec(memory_space=pltpu.VMEM)],
        out_specs=pl.BlockSpec(memory_space=pltpu.VMEM),
        out_shape=jax.ShapeDtypeStruct(parts.shape[1:], F32),
    )(parts)


def _mesh_position():
    x, y, c = lax.axis_index("x"), lax.axis_index("y"), lax.axis_index("c")
    chips = [(1 - x, y), (x, 1 - y), (1 - x, 1 - y)]
    return x, y, c, chips


ANY = pl.BlockSpec(memory_space=pl.ANY)
HBM = pl.BlockSpec(memory_space=pltpu.HBM)
SEM = pl.BlockSpec(memory_space=pltpu.SEMAPHORE)
SPLIT_COPY_EFFECT = pltpu.SideEffectType.DATAFLOW_SIDE_EFFECTING


def _in_hbm(a):
    return pltpu.with_memory_space_constraint(a, pltpu.HBM)


def _chip_copies(views, srcs, lands, send, recv):
    _, _, c, chips = _mesh_position()
    cps = []
    for a in range(len(srcs)):
        for k, (px, py) in enumerate(chips):
            src, dst = views(a, k, srcs[a], lands[a], c, 2 * px + py)
            sem = a * (N_CHIPS - 1) + k
            cps.append(pltpu.make_async_remote_copy(
                src_ref=src, dst_ref=dst, send_sem=send.at[sem], recv_sem=recv.at[sem],
                device_id=(px, py, c), device_id_type=MESH))
    return cps


def _ici_start(sources, land_shapes, views, after, name):
    n = len(sources)

    def body(*refs):
        srcs, lands = refs[:n], refs[n:2 * n]
        send, recv = refs[2 * n + 1], refs[2 * n + 2]
        token = refs[-1]
        for cp in _chip_copies(views, srcs, lands, send, recv):
            cp.start()
        token[...] = jnp.zeros_like(token)

    lands = [_in_hbm(lax.empty(s.shape, s.dtype)) for s in land_shapes]
    outs = pl.pallas_call(
        body, name=name,
        in_specs=[HBM] * (2 * n) + [ANY],
        out_specs=[SEM, SEM] + [HBM] * (2 * n) + [pl.BlockSpec(memory_space=pltpu.VMEM)],
        out_shape=[pltpu.SemaphoreType.DMA((n * (N_CHIPS - 1),))] * 2
        + [pltpu.HBM(a.shape, a.dtype) for a in sources]
        + [pltpu.HBM(s.shape, s.dtype) for s in land_shapes]
        + [jax.ShapeDtypeStruct((8, 128), F32)],
        input_output_aliases={i: 2 + i for i in range(2 * n)},
        compiler_params=pltpu.CompilerParams(has_side_effects=SPLIT_COPY_EFFECT),
    )(*[_in_hbm(a) for a in sources], *lands, after)
    return outs[0], outs[1], list(outs[2:2 + n]), list(outs[2 + n:2 + 2 * n]), outs[-1]


def _ici_wait(handle, views, after, name):
    send, recv, srcs, lands, _ = handle
    n = len(srcs)

    def body(*refs):
        src_refs, land_refs = refs[:n], refs[n:2 * n]
        for cp in _chip_copies(views, src_refs, land_refs, refs[2 * n], refs[2 * n + 1]):
            cp.wait_send()
            cp.wait_recv()

    outs = pl.pallas_call(
        body, name=name,
        in_specs=[HBM] * (2 * n) + [SEM, SEM, ANY],
        out_specs=[HBM] * (2 * n),
        out_shape=[pltpu.HBM(a.shape, a.dtype) for a in srcs]
        + [pltpu.HBM(a.shape, a.dtype) for a in lands],
        input_output_aliases={i: i for i in range(2 * n)},
        compiler_params=pltpu.CompilerParams(has_side_effects=SPLIT_COPY_EFFECT),
    )(*srcs, *lands, send, recv, after)
    return list(outs[:n]), list(outs[n:])


def _gather_views(split):
    def views(a, k, src, land, c, slot):
        if split[a]:
            half = src.shape[0] // 2
            src = src.at[pl.ds(c * half, half)]
        return src, land.at[k]
    return views


def _scatter_views(a, k, src, land, c, slot):
    return src.at[slot], land.at[k]


def _gather_land_shapes(shards, split):
    return [jax.ShapeDtypeStruct(
        (N_CHIPS - 1, a.shape[0] // 2 if sp else a.shape[0]) + a.shape[1:], a.dtype)
        for a, sp in zip(shards, split)]


def _gather_finish(shards, lands, split, name):
    n = len(shards)
    ns = sum(split)
    d_index = {a: i for i, a in enumerate(a for a in range(n) if split[a])}

    def body(*refs):
        shard, land, outs = refs[:n], refs[n:2 * n], refs[2 * n:3 * n]
        obuf, fbuf = refs[3 * n:4 * n], refs[4 * n:5 * n]
        dbuf = refs[5 * n:5 * n + ns]
        ld_own, st_own, ld, st_mine, st_sib, send, recv = refs[5 * n + ns:]
        x, y, c, chips = _mesh_position()
        me = 2 * x + y
        own_loads, loads, sends, pending = [], {}, [], []
        for a in range(n):
            cp = pltpu.make_async_copy(shard[a], obuf[a], ld_own.at[a])
            cp.start()
            own_loads.append(cp)
        for a in range(n):
            for k in range(N_CHIPS - 1):
                cp = pltpu.make_async_copy(land[a].at[k], fbuf[a].at[k], ld.at[a, k])
                cp.start()
                loads[a, k] = cp
        for a in range(n):
            own_loads[a].wait()
            cp = pltpu.make_async_copy(obuf[a], outs[a].at[me], st_own.at[a])
            cp.start()
            pending.append(cp)
        for a in range(n):
            rows = shard[a].shape[0]
            for k, (px, py) in enumerate(chips):
                loads[a, k].wait()
                part = pl.ds(c * (rows // 2), rows // 2) if split[a] else pl.ds(0, rows)
                cp = pltpu.make_async_copy(fbuf[a].at[k], outs[a].at[2 * px + py, part],
                                           st_mine.at[a, k])
                cp.start()
                pending.append(cp)
                if split[a]:
                    fw = pltpu.make_async_remote_copy(
                        src_ref=fbuf[a].at[k], dst_ref=dbuf[d_index[a]].at[k],
                        send_sem=send.at[a, k], recv_sem=recv.at[a, k],
                        device_id=(x, y, 1 - c), device_id_type=MESH)
                    fw.start()
                    sends.append((a, k, fw))
        for a, k, fw in sends:
            px, py = chips[k]
            half = shard[a].shape[0] // 2
            fw.wait_recv()
            cp = pltpu.make_async_copy(dbuf[d_index[a]].at[k],
                                       outs[a].at[2 * px + py, pl.ds((1 - c) * half, half)],
                                       st_sib.at[a, k])
            cp.start()
            pending.append(cp)
        for _, _, fw in sends:
            fw.wait_send()
        for cp in pending:
            cp.wait()

    stage = [pltpu.VMEM(a.shape, a.dtype) for a in lands]
    dma = lambda *shape: pltpu.SemaphoreType.DMA(shape)
    return pl.pallas_call(
        body, name=name,
        in_specs=[ANY] * (2 * n), out_specs=[ANY] * n,
        out_shape=[jax.ShapeDtypeStruct((N_CHIPS,) + a.shape, a.dtype) for a in shards],
        scratch_shapes=[pltpu.VMEM(a.shape, a.dtype) for a in shards] + stage
        + [s for s, sp in zip(stage, split) if sp]
        + [dma(n), dma(n), dma(n, 3), dma(n, 3), dma(n, 3), dma(n, 3), dma(n, 3)],
        compiler_params=pltpu.CompilerParams(vmem_limit_bytes=VMEM_LIMIT_BYTES),
    )(*shards, *lands)


def _exchange_siblings(grads, name):
    n = len(grads)

    def body(*refs):
        ins, theirs = refs[:n], refs[n:2 * n]
        sbuf, rbuf = refs[2 * n:3 * n], refs[3 * n:4 * n]
        ld, st, send, recv = refs[4 * n:]
        x, y, c, _ = _mesh_position()
        loads, sends, stores = [], [], []
        for a in range(n):
            half = ins[a].shape[1] // 2
            cp = pltpu.make_async_copy(ins[a].at[:, pl.ds((1 - c) * half, half)], sbuf[a], ld.at[a])
            cp.start()
            loads.append(cp)
        for a in range(n):
            loads[a].wait()
            rc = pltpu.make_async_remote_copy(
                src_ref=sbuf[a], dst_ref=rbuf[a], send_sem=send.at[a], recv_sem=recv.at[a],
                device_id=(x, y, 1 - c), device_id_type=MESH)
            rc.start()
            sends.append(rc)
        for a in range(n):
            sends[a].wait_recv()
            cp = pltpu.make_async_copy(rbuf[a], theirs[a], st.at[a])
            cp.start()
            stores.append(cp)
        for a in range(n):
            sends[a].wait_send()
            stores[a].wait()

    half_shape = lambda a: (a.shape[0], a.shape[1] // 2, a.shape[2])
    stage = [pltpu.VMEM(half_shape(a), a.dtype) for a in grads]
    return pl.pallas_call(
        body, name=name,
        in_specs=[ANY] * n, out_specs=[ANY] * n,
        out_shape=[jax.ShapeDtypeStruct(half_shape(a), a.dtype) for a in grads],
        scratch_shapes=stage + stage + [pltpu.SemaphoreType.DMA((n,))] * 4,
        compiler_params=pltpu.CompilerParams(vmem_limit_bytes=VMEM_LIMIT_BYTES),
    )(*grads)


def _share_halves(halves):
    n = len(halves)

    def body(*refs):
        ins, outs = refs[:n], refs[n:2 * n]
        sbuf, rbuf = refs[2 * n:3 * n], refs[3 * n:4 * n]
        ld, st_own, st_sib, send, recv = refs[4 * n:]
        x, y, c, _ = _mesh_position()
        loads, sends, stores = [], [], []
        for a in range(n):
            cp = pltpu.make_async_copy(ins[a], sbuf[a], ld.at[a])
            cp.start()
            loads.append(cp)
        for a in range(n):
            half = ins[a].shape[0]
            loads[a].wait()
            rc = pltpu.make_async_remote_copy(
                src_ref=sbuf[a], dst_ref=rbuf[a], send_sem=send.at[a], recv_sem=recv.at[a],
                device_id=(x, y, 1 - c), device_id_type=MESH)
            rc.start()
            sends.append(rc)
            cp = pltpu.make_async_copy(sbuf[a], outs[a].at[pl.ds(c * half, half)], st_own.at[a])
            cp.start()
            stores.append(cp)
        for a in range(n):
            half = ins[a].shape[0]
            sends[a].wait_recv()
            cp = pltpu.make_async_copy(rbuf[a], outs[a].at[pl.ds((1 - c) * half, half)], st_sib.at[a])
            cp.start()
            stores.append(cp)
        for cp in sends:
            cp.wait_send()
        for cp in stores:
            cp.wait()

    stage = [pltpu.VMEM(a.shape, a.dtype) for a in halves]
    return pl.pallas_call(
        body, name="share_halves",
        in_specs=[ANY] * n, out_specs=[ANY] * n,
        out_shape=[jax.ShapeDtypeStruct((2 * a.shape[0],) + a.shape[1:], a.dtype)
                   for a in halves],
        scratch_shapes=stage + stage + [pltpu.SemaphoreType.DMA((n,))] * 5,
        compiler_params=pltpu.CompilerParams(vmem_limit_bytes=VMEM_LIMIT_BYTES),
    )(*halves)


def _gather_small(part):
    def body(in_ref, out_ref, send, recv, local):
        x, y, c, _ = _mesh_position()
        me = 4 * x + 2 * y + c
        cps = [pltpu.make_async_copy(in_ref, out_ref.at[me], local)]
        k = 0
        for fx in range(2):
            for fy in range(2):
                for fc in range(2):
                    if fx or fy or fc:
                        cps.append(pltpu.make_async_remote_copy(
                            src_ref=in_ref, dst_ref=out_ref.at[me], send_sem=send.at[k],
                            recv_sem=recv.at[k], device_id=(x ^ fx, y ^ fy, c ^ fc),
                            device_id_type=MESH))
                        k += 1
        for cp in cps:
            cp.start()
        for cp in cps:
            cp.wait()

    return pl.pallas_call(
        body, name="gather_small",
        in_specs=[pl.BlockSpec(memory_space=pltpu.VMEM)],
        out_specs=pl.BlockSpec(memory_space=pltpu.VMEM),
        out_shape=jax.ShapeDtypeStruct((N_DEV,) + part.shape, part.dtype),
        scratch_shapes=[pltpu.SemaphoreType.DMA((N_DEV - 1,)), pltpu.SemaphoreType.DMA((N_DEV - 1,)),
                        pltpu.SemaphoreType.DMA],
    )(part)


def _scatter_start(grads, core, tag):
    theirs = _exchange_siblings(grads, "exchange_siblings_" + tag)
    sums = [_sum_pair(g, t, core, "sum_siblings_%s_%d" % (tag, i))
            for i, (g, t) in enumerate(zip(grads, theirs))]
    lands = [jax.ShapeDtypeStruct((N_CHIPS - 1,) + s.shape[1:], s.dtype) for s in sums]
    return _ici_start(sums, lands, _scatter_views, theirs[0], "scatter_start_" + tag)


def _scatter_finish(handle, chip, after, tag):
    sums, got = _ici_wait(handle, _scatter_views, after, "scatter_wait_" + tag)
    return [_sum_chips(s, g, chip, "sum_chips_%s_%d" % (tag, i))
            for i, (s, g) in enumerate(zip(sums, got))]


def _pad_rows(a, rows):
    return jnp.pad(a, ((0, rows - a.shape[0]), (0, 0)))


def kernel(x, norm_mix_0, w_in_0, b_f_0, conv_w_0, w_out_0, norm_ffn_0, w_up_0, w_down_0, norm_mix_1, pool_w_1, pool_scale_1, norm_ffn_1, w_up_1, w_down_1, final_norm, loss_target, m_norm_mix_0, m_w_in_0, m_b_f_0, m_conv_w_0, m_w_out_0, m_norm_ffn_0, m_w_up_0, m_w_down_0, m_norm_mix_1, m_pool_w_1, m_pool_scale_1, m_norm_ffn_1, m_w_up_1, m_w_down_1, m_final_norm, v_norm_mix_0, v_w_in_0, v_b_f_0, v_conv_w_0, v_w_out_0, v_norm_ffn_0, v_w_up_0, v_w_down_0, v_norm_mix_1, v_pool_w_1, v_pool_scale_1, v_norm_ffn_1, v_w_up_1, v_w_down_1, v_final_norm):
    d = x.shape[-1]
    a = N_HEADS * HEAD_DIM
    c_conv = conv_w_0.shape[1] * N_CHIPS
    xs = x[0]
    target = loss_target[0]
    row = lambda vec: vec.reshape(1, -1)

    big = [w_in_0, w_out_0, w_up_0, w_down_0, pool_w_1, w_up_1, w_down_1]
    first = [w_in_0.astype(BF16), w_out_0.astype(BF16), conv_w_0]
    first_split = [True, True, False]
    rest = [w.astype(BF16) for w in (w_up_0, w_down_0, pool_w_1, w_up_1, w_down_1)]
    rest_split = [True] * len(rest)
    start_a = _ici_start(first, _gather_land_shapes(first, first_split),
                         _gather_views(first_split), b_f_0, "gather_start_a")
    start_b = _ici_start(rest, _gather_land_shapes(rest, rest_split),
                         _gather_views(rest_split), start_a[-1], "gather_start_b")
    first, land_a = _ici_wait(start_a, _gather_views(first_split), start_b[-1], "gather_wait_a")
    g_in, g_out, g_conv = _gather_finish(first, land_a, first_split, "gather_finish_a")
    w_in = g_in.transpose(1, 0, 2).reshape(d, -1)
    w_qkv = w_in[:, :3 * a]
    w_f = jnp.pad(w_in[:, 3 * a:3 * a + N_HEADS], ((0, 0), (0, 128 - N_HEADS)))
    w_bcx = w_in[:, 3 * a + N_HEADS:]
    w_out = g_out.reshape(-1, d)
    conv_w = _pad_rows(g_conv.transpose(1, 0, 2).reshape(conv_w_0.shape[0], c_conv), 8)
    bf = jnp.pad(b_f_0, (0, 128 - N_HEADS)).reshape(1, 128)

    n0, qkv, fl, bcx = _ln_proj(xs, row(norm_mix_0), w_qkv, w_f, w_bcx)
    qa, ka = _gate_prep(fl, bf, qkv)
    o, lse = _attn_fwd(qa, ka, qkv)
    h1 = _conv_out(o, bcx, conv_w, w_out, xs)
    rest, land_b = _ici_wait(start_b, _gather_views(rest_split), h1, "gather_wait_b")
    g_up0, g_down0, g_pool, g_up1, g_down1 = _gather_finish(rest, land_b, rest_split,
                                                            "gather_finish_b")
    w_down0 = g_down0.reshape(-1, d)
    w_down1 = g_down1.reshape(-1, d)
    pool_w = g_pool.transpose(1, 0, 2, 3).reshape(pool_w_1.shape[0], -1, pool_w_1.shape[2])
    h2, a0, nf0 = _mlp_fwd(h1, row(norm_ffn_0), g_up0, w_down0, "mlp_fwd_0")
    h3 = _pool_fwd(h2, row(norm_mix_1), pool_w, row(pool_scale_1))
    h4, a1, nf1 = _mlp_fwd(h3, row(norm_ffn_1), g_up1, w_down1, "mlp_fwd_1")
    dh4, loss_part, d_final = _final_loss(h4, row(final_norm), target)

    slot_cols = g_up0.shape[2]
    pool_cols = pool_w.shape[2]
    core = lax.axis_index("c").astype(jnp.int32).reshape(1)
    chip_index = (2 * lax.axis_index("x") + lax.axis_index("y")).astype(jnp.int32).reshape(1)
    da1, dz1, dh3, d_nffn1 = _mlp_bwd_x(dh4, a1, g_up1, w_down1, h3, row(norm_ffn_1), "mlp_bwd_x_1")
    dw_up1, dw_down1 = _mlp_bwd_w(nf1, da1, a1, dz1, slot_cols, "mlp_bwd_w_1")
    scatter_1 = _scatter_start([dw_up1, dw_down1.reshape(N_CHIPS, -1, d)], core, "mlp1")
    dh2, dw_pool, d_pscale, d_nmix1 = _pool_bwd(scatter_1[-1], dh3, h2, row(norm_mix_1), pool_w,
                                                row(pool_scale_1))
    da0, dz0, dh1, d_nffn0 = _mlp_bwd_x(dh2, a0, g_up0, w_down0, h1, row(norm_ffn_0), "mlp_bwd_x_0")
    dw_up0, dw_down0 = _mlp_bwd_w(nf0, da0, a0, dz0, slot_cols, "mlp_bwd_w_0")
    dw_pool = (dw_pool.reshape(pool_w.shape[0], N_CHIPS, -1, pool_cols).transpose(1, 0, 2, 3)
               .reshape(N_CHIPS, -1, pool_cols))
    scatter_0 = _scatter_start([dw_up0, dw_down0.reshape(N_CHIPS, -1, d), dw_pool], core, "mlp0")
    do, delta, dbcx, dw_out, d_conv = _conv_out_bwd(scatter_0[-1], dh1, w_out, o, bcx, conv_w)
    dqa, dka, dv = _attn_bwd(qa, ka, qkv, do, lse, delta)
    dqkv, dfl, d_bf = _gate_bwd(dqa, dka, dv, fl, bf)
    dw_qkv = _wgrad(n0, dqkv, "wgrad_qkv")
    dw_f = _wgrad(n0, dfl, "wgrad_f")
    dw_bcx = _wgrad(n0, dbcx, "wgrad_bcx")
    dw_in = jnp.concatenate([dw_qkv, dw_f[:, :N_HEADS], dw_bcx], axis=1)
    scatter_m = _scatter_start([dw_in.reshape(d, N_CHIPS, -1).transpose(1, 0, 2),
                                dw_out.reshape(N_CHIPS, -1, d)], core, "mixer")
    grad_x, d_nmix0 = _in_proj_bwd(scatter_m[-1], dqkv, dfl, dbcx, w_qkv, w_f, w_bcx, xs,
                                   row(norm_mix_0), dh1)

    h_up1, h_down1 = _scatter_finish(scatter_1, chip_index, grad_x, "mlp1")
    h_up0, h_down0, h_pool = _scatter_finish(scatter_0, chip_index, grad_x, "mlp0")
    h_in, h_out = _scatter_finish(scatter_m, chip_index, grad_x, "mixer")
    reduced = _share_halves([h_in, h_out, h_up0, h_down0, h_pool, h_up1, h_down1])
    moments = [(m_w_in_0, v_w_in_0), (m_w_out_0, v_w_out_0), (m_w_up_0, v_w_up_0),
               (m_w_down_0, v_w_down_0), (m_pool_w_1, v_pool_w_1), (m_w_up_1, v_w_up_1),
               (m_w_down_1, v_w_down_1)]
    big_out = []
    for k, (w, g, (m, v)) in enumerate(zip(big, reduced, moments)):
        flat = lambda t: t.reshape(-1, t.shape[-1])
        delta_w, new_m, new_v = _adamw(flat(w), flat(g), flat(m), flat(v), "adamw_%d" % k)
        big_out.append((g.reshape(w.shape), delta_w.reshape(w.shape), new_m.reshape(w.shape),
                        new_v.reshape(w.shape)))

    tail = jnp.concatenate([d_conv[0:3].reshape(-1)[d:], d_bf[0, :N_HEADS], loss_part[0, :1]])
    small_part = jnp.concatenate(
        [d_nmix0, d_nffn0, d_nmix1, d_pscale, d_nffn1, d_final,
         d_conv[0:3].reshape(1, -1)[:, :d],
         jnp.pad(tail, (0, d - tail.shape[0])).reshape(1, d)], axis=0)
    parts = _gather_small(small_part)

    chip = 2 * lax.axis_index("x") + lax.axis_index("y")
    cw_cols = conv_w_0.shape[1]

    def conv_block(full):
        mine = lax.dynamic_slice_in_dim(full, chip * cw_cols, cw_cols, axis=1)
        return jnp.pad(mine.reshape(-1), (0, d - mine.size))

    def small_rows(vals, cw, bfv):
        return jnp.stack(list(vals) + [cw, jnp.pad(bfv, (0, d - N_HEADS))])

    smalls_w = [norm_mix_0, norm_ffn_0, norm_mix_1, pool_scale_1, norm_ffn_1, final_norm]
    smalls_m = [m_norm_mix_0, m_norm_ffn_0, m_norm_mix_1, m_pool_scale_1, m_norm_ffn_1, m_final_norm]
    smalls_v = [v_norm_mix_0, v_norm_ffn_0, v_norm_mix_1, v_pool_scale_1, v_norm_ffn_1, v_final_norm]
    pad_cw = lambda t: jnp.pad(t.reshape(-1), (0, d - t.size))
    w_rows = small_rows(smalls_w, pad_cw(conv_w_0), b_f_0)
    m_rows = small_rows(smalls_m, pad_cw(m_conv_w_0), m_b_f_0)
    v_rows = small_rows(smalls_v, pad_cw(v_conv_w_0), v_b_f_0)

    g_sum = _sum_devices(parts)
    conv_full = jnp.concatenate([g_sum[6], g_sum[7, :3 * c_conv - d]]).reshape(3, c_conv)
    bf_grad = g_sum[7, 3 * c_conv - d:3 * c_conv - d + N_HEADS]
    loss = g_sum[7, 3 * c_conv - d + N_HEADS]
    g_rows = jnp.concatenate(
        [g_sum[0:6], conv_block(conv_full).reshape(1, d),
         jnp.pad(bf_grad, (0, d - N_HEADS)).reshape(1, d)], axis=0)
    d_rows, nm_rows, nv_rows = _adamw(w_rows, g_rows, m_rows, v_rows, "adamw_small")

    def unpack(rows):
        cw = rows[6, :conv_w_0.size].reshape(conv_w_0.shape)
        return [rows[0], rows[1], rows[2], rows[3], rows[4], rows[5], cw, rows[7, :N_HEADS]]

    def assemble(kind):
        sm = unpack([g_rows, d_rows, nm_rows, nv_rows][kind])
        lg = [t[kind] for t in big_out]
        return [sm[0], lg[0], sm[7], sm[6], lg[1], sm[1], lg[2], lg[3],
                sm[2], lg[4], sm[3], sm[4], lg[5], lg[6], sm[5]]

    return (loss, grad_x[None], *assemble(0), *assemble(1), *assemble(2), *assemble(3))
```

```python
import functools

import jax
import jax.numpy as jnp
from jax import lax
from jax.experimental import pallas as pl
from jax.experimental.pallas import tpu as pltpu

F32 = jnp.float32
BF16 = jnp.bfloat16

RMS_EPS = 1e-6
HEAD_DIM = 64
N_HEADS = 8
ATTN_SCALE = HEAD_DIM ** -0.5
POOL_WINDOWS = (2, 4, 8, 16)
POOL_HALO = 16
CONV_HALO = 8
NEG_BIG = -1e30

ADAM_LR = 0.001
ADAM_B1 = 0.9
ADAM_B2 = 0.999
ADAM_EPS = 1e-08
ADAM_WD = 0.01
ADAM_STEP = 10

N_CHIPS = 4
N_DEV = 8
MESH = pl.DeviceIdType.MESH

VMEM_LIMIT_BYTES = 56 * 1024 * 1024

TILE_ROWS = 512
TILE_ATTN = 512
TILE_MLP_ROWS = 1024
TILE_MLP_BWD_ROWS = 512
TILE_MLP_FF = 1024
TILE_WGRAD_K = 1024
TILE_WGRAD_N = 1024
TILE_ELEM_ROWS = 256

LANE_CQ = 64
LANE_ONE = 67


def _params(semantics):
    return pltpu.CompilerParams(dimension_semantics=semantics,
                                vmem_limit_bytes=VMEM_LIMIT_BYTES)


def _nn(a, b):
    return lax.dot_general(a, b, (((1,), (0,)), ((), ())), preferred_element_type=F32)


def _nt(a, b):
    return lax.dot_general(a, b, (((1,), (1,)), ((), ())), preferred_element_type=F32)


def _tn(a, b):
    return lax.dot_general(a, b, (((0,), (0,)), ((), ())), preferred_element_type=F32)


def _split3(v):
    hi = v.astype(BF16)
    r1 = v - hi.astype(F32)
    mid = r1.astype(BF16)
    lo = (r1 - mid.astype(F32)).astype(BF16)
    return hi, mid, lo


def _exact_nn(sel, v):
    hi, mid, lo = _split3(v)
    return _nn(sel, hi) + _nn(sel, mid) + _nn(sel, lo)


def _exact_nt(sel, v):
    hi, mid, lo = _split3(v)
    return _nt(sel, hi) + _nt(sel, mid) + _nt(sel, lo)


def _rms_fwd(x, g):
    r = lax.rsqrt(jnp.mean(x * x, axis=-1, keepdims=True) + RMS_EPS)
    return x * r * g, r


def _rms_bwd(dn, x, g):
    r = lax.rsqrt(jnp.mean(x * x, axis=-1, keepdims=True) + RMS_EPS)
    xh = x * r
    gy = dn * g
    dx = r * (gy - xh * jnp.mean(gy * xh, axis=-1, keepdims=True))
    return dx, jnp.sum(dn * xh, axis=0, keepdims=True)


def _lane(shape):
    return lax.broadcasted_iota(jnp.int32, shape, len(shape) - 1)


def _row(shape):
    return lax.broadcasted_iota(jnp.int32, shape, len(shape) - 2)


def _full(a):
    nd = a.ndim
    return pl.BlockSpec(a.shape, lambda *_: (0,) * nd)


def _ln_proj(x, g, w_qkv, w_f, w_bcx):
    s, d = x.shape
    tm = min(TILE_ROWS, s)

    def body(x_ref, g_ref, wq_ref, wf_ref, wb_ref, n_ref, qkv_ref, fl_ref, bcx_ref):
        n, _ = _rms_fwd(x_ref[...], g_ref[...])
        nb = n.astype(BF16)
        n_ref[...] = nb
        qkv_ref[...] = _nn(nb, wq_ref[...]).astype(BF16)
        fl_ref[...] = _nn(nb, wf_ref[...])
        bcx_ref[...] = _nn(nb, wb_ref[...])

    rows = lambda c: pl.BlockSpec((tm, c), lambda i: (i, 0))
    return pl.pallas_call(
        body, name="ln_proj", grid=(s // tm,),
        in_specs=[rows(d), _full(g), _full(w_qkv), _full(w_f), _full(w_bcx)],
        out_specs=[rows(d), rows(w_qkv.shape[1]), rows(w_f.shape[1]), rows(w_bcx.shape[1])],
        out_shape=[jax.ShapeDtypeStruct((s, d), BF16),
                   jax.ShapeDtypeStruct((s, w_qkv.shape[1]), BF16),
                   jax.ShapeDtypeStruct((s, w_f.shape[1]), F32),
                   jax.ShapeDtypeStruct((s, w_bcx.shape[1]), F32)],
        compiler_params=_params(("parallel",)),
    )(x, g, w_qkv, w_f, w_bcx)


def _gate_prep(fl, bf, qkv):
    s = fl.shape[0]
    a = N_HEADS * HEAD_DIM
    tm = min(TILE_ROWS, s)

    def body(fl_ref, bf_ref, q_ref, k_ref, qa_ref, ka_ref, carry_ref):
        i = pl.program_id(0)

        @pl.when(i == 0)
        def _():
            carry_ref[...] = jnp.zeros_like(carry_ref)

        z = fl_ref[...] + bf_ref[...]
        logf = jnp.minimum(z, 0.0) - jnp.log(1.0 + jnp.exp(-jnp.abs(z)))
        lower = (_lane((tm, tm)) <= _row((tm, tm))).astype(BF16)
        cum = _exact_nn(lower, logf) + carry_ref[0:1, :]
        carry_ref[0:1, :] = cum[tm - 1:tm, :]

        lane = _lane((tm, 128))
        for h in range(N_HEADS):
            cb = jnp.sum(jnp.where(lane == h, cum, 0.0), axis=1, keepdims=True)
            hi, mid, lo = (p.astype(F32) for p in _split3(cb))
            pair = slice((h // 2) * 128, (h // 2 + 1) * 128)
            qp = q_ref[:, pair].astype(F32)
            kp = k_ref[:, pair].astype(F32)
            if h % 2:
                qp = pltpu.roll(qp, HEAD_DIM, axis=1)
                kp = pltpu.roll(kp, HEAD_DIM, axis=1)
            q_bias = jnp.where(lane == LANE_CQ, hi,
                               jnp.where(lane == LANE_CQ + 1, mid,
                                         jnp.where(lane == LANE_CQ + 2, lo,
                                                   jnp.where(lane < LANE_ONE + 3, 1.0, 0.0))))
            k_bias = jnp.where(lane < LANE_ONE, 1.0,
                               jnp.where(lane == LANE_ONE, -hi,
                                         jnp.where(lane == LANE_ONE + 1, -mid,
                                                   jnp.where(lane == LANE_ONE + 2, -lo, 0.0))))
            qa_ref[h] = jnp.where(lane < HEAD_DIM, qp * ATTN_SCALE, q_bias).astype(BF16)
            ka_ref[h] = jnp.where(lane < HEAD_DIM, kp, k_bias).astype(BF16)

    aug = jax.ShapeDtypeStruct((N_HEADS, s, 128), BF16)
    aug_spec = pl.BlockSpec((N_HEADS, tm, 128), lambda i: (0, i, 0))
    return pl.pallas_call(
        body, name="gate_prep", grid=(s // tm,),
        in_specs=[pl.BlockSpec((tm, 128), lambda i: (i, 0)), _full(bf),
                  pl.BlockSpec((tm, a), lambda i: (i, 0)),
                  pl.BlockSpec((tm, a), lambda i: (i, 1))],
        out_specs=[aug_spec, aug_spec],
        out_shape=[aug, aug],
        scratch_shapes=[pltpu.VMEM((8, 128), F32)],
        compiler_params=_params(("arbitrary",)),
    )(fl, bf, qkv, qkv)


def _attn_fwd(qa, ka, qkv):
    s = qa.shape[1]
    a = N_HEADS * HEAD_DIM
    t = min(TILE_ATTN, s)
    n_pairs = N_HEADS // 2
    v_block0 = 2 * a // 128

    def body(qa_ref, ka_ref, v_ref, o_ref, lse_ref, m_ref, l_ref, acc_ref):
        i = pl.program_id(1)
        m_ref[...] = jnp.full_like(m_ref, NEG_BIG)
        l_ref[...] = jnp.zeros_like(l_ref)
        acc_ref[...] = jnp.zeros_like(acc_ref)
        upper_rows = _row((128, t)) < HEAD_DIM

        def kv_step(j, masked):
            ks = pl.ds(pl.multiple_of(j * t, t), t)
            vf = v_ref[ks, :].astype(F32)
            lane = _lane((t, 128))
            v_heads = [jnp.where(lane < HEAD_DIM, vf, 0.0).astype(BF16),
                       jnp.where(lane >= HEAD_DIM, vf, 0.0).astype(BF16)]
            alphas, update = [], None
            for e in range(2):
                sc = _nt(ka_ref[e, ks, :], qa_ref[e])
                if masked:
                    sc = jnp.where(_row((t, t)) <= _lane((t, t)), sc, NEG_BIG)
                m_prev = m_ref[e]
                m_new = jnp.maximum(m_prev, jnp.max(sc, axis=0, keepdims=True))
                p = jnp.exp(sc - m_new)
                alpha = jnp.exp(m_prev - m_new)
                l_ref[e] = alpha * l_ref[e] + jnp.sum(p, axis=0, keepdims=True)
                m_ref[e] = m_new
                alphas.append(alpha)
                pv = _tn(v_heads[e], p.astype(BF16))
                update = pv if update is None else update + pv
            acc_ref[...] = acc_ref[...] * jnp.where(upper_rows, alphas[0], alphas[1]) + update

        def full_step(j, carry):
            kv_step(j, False)
            return carry

        lax.fori_loop(0, i, full_step, 0)
        kv_step(i, True)

        out_t = acc_ref[...] / jnp.where(upper_rows, l_ref[0], l_ref[1])
        o_ref[...] = out_t.T.astype(BF16)
        lse = [m_ref[e] + jnp.log(l_ref[e]) for e in range(2)]
        lse_ref[...] = jnp.where(_row((8, t)) == 0, lse[0], lse[1])

    return pl.pallas_call(
        body, name="attn_fwd", grid=(n_pairs, s // t),
        in_specs=[pl.BlockSpec((2, t, 128), lambda g, i: (g, i, 0)),
                  pl.BlockSpec((2, s, 128), lambda g, i: (g, 0, 0)),
                  pl.BlockSpec((s, 128), lambda g, i: (0, v_block0 + g))],
        out_specs=[pl.BlockSpec((t, 128), lambda g, i: (i, g)),
                   pl.BlockSpec((None, 8, t), lambda g, i: (g, 0, i))],
        out_shape=[jax.ShapeDtypeStruct((s, a), BF16),
                   jax.ShapeDtypeStruct((n_pairs, 8, s), F32)],
        scratch_shapes=[pltpu.VMEM((2, 1, t), F32), pltpu.VMEM((2, 1, t), F32),
                        pltpu.VMEM((128, t), F32)],
        compiler_params=_params(("parallel", "arbitrary")),
    )(qa, ka, qkv)


def _conv_out(o, bcx, cw, w_out, x):
    s, d = x.shape
    c = o.shape[1]
    tm = min(TILE_ROWS, s)

    def body(o_ref, b_ref, c_ref, xin_ref, cw_ref, w_ref, x_ref, h_ref, ubuf):
        i = pl.program_id(0)

        @pl.when(i == 0)
        def _():
            ubuf[0:CONV_HALO, :] = jnp.zeros((CONV_HALO, c), F32)

        u = c_ref[...] * xin_ref[...]
        ubuf[CONV_HALO:CONV_HALO + tm, :] = u
        u1 = ubuf[CONV_HALO - 1:CONV_HALO - 1 + tm, :]
        u2 = ubuf[CONV_HALO - 2:CONV_HALO - 2 + tm, :]
        cv = (cw_ref[0:1, :] * u2 + cw_ref[1:2, :] * u1) + cw_ref[2:3, :] * u
        y = (b_ref[...] * cv).astype(BF16)
        mix = _nn(o_ref[...], w_ref[0:c, :]) + _nn(y, w_ref[c:2 * c, :])
        h_ref[...] = x_ref[...] + mix
        ubuf[0:CONV_HALO, :] = u[tm - CONV_HALO:tm, :]

    col = lambda k: pl.BlockSpec((tm, c), lambda i: (i, k))
    return pl.pallas_call(
        body, name="conv_out", grid=(s // tm,),
        in_specs=[col(0), col(0), col(1), col(2), _full(cw), _full(w_out),
                  pl.BlockSpec((tm, d), lambda i: (i, 0))],
        out_specs=pl.BlockSpec((tm, d), lambda i: (i, 0)),
        out_shape=jax.ShapeDtypeStruct((s, d), F32),
        scratch_shapes=[pltpu.VMEM((tm + CONV_HALO, c), F32)],
        compiler_params=_params(("arbitrary",)),
    )(o, bcx, bcx, bcx, cw, w_out, x)


def _mlp_fwd(h, g, w_up, w_down, name):
    s, d = h.shape
    ff = w_down.shape[0]
    slot_cols = w_up.shape[2]
    tm = min(TILE_MLP_ROWS, s)
    tf = min(TILE_MLP_FF, slot_cols)
    per_slot = slot_cols // tf
    nf = ff // tf

    def body(h_ref, g_ref, wu_ref, wd_ref, out_ref, a_ref, n_ref, nb_ref, acc_ref):
        f = pl.program_id(1)

        @pl.when(f == 0)
        def _():
            n, _ = _rms_fwd(h_ref[...], g_ref[...])
            nb = n.astype(BF16)
            nb_ref[...] = nb
            n_ref[...] = nb
            acc_ref[...] = jnp.zeros_like(acc_ref)

        pre = _nn(nb_ref[...], wu_ref[...])
        a_ref[...] = pre.astype(BF16)
        r = jnp.square(jnp.maximum(pre, 0.0)).astype(BF16)
        acc_ref[...] += _nn(r, wd_ref[...])

        @pl.when(f == nf - 1)
        def _():
            out_ref[...] = h_ref[...] + acc_ref[...]

    return pl.pallas_call(
        body, name=name, grid=(s // tm, nf),
        in_specs=[pl.BlockSpec((tm, d), lambda i, f: (i, 0)), _full(g),
                  pl.BlockSpec((None, d, tf), lambda i, f: (f // per_slot, 0, f % per_slot)),
                  pl.BlockSpec((tf, d), lambda i, f: (f, 0))],
        out_specs=[pl.BlockSpec((tm, d), lambda i, f: (i, 0)),
                   pl.BlockSpec((tm, tf), lambda i, f: (i, f)),
                   pl.BlockSpec((tm, d), lambda i, f: (i, 0))],
        out_shape=[jax.ShapeDtypeStruct((s, d), F32),
                   jax.ShapeDtypeStruct((s, ff), BF16),
                   jax.ShapeDtypeStruct((s, d), BF16)],
        scratch_shapes=[pltpu.VMEM((tm, d), BF16), pltpu.VMEM((tm, d), F32)],
        compiler_params=_params(("parallel", "arbitrary")),
    )(h, g, w_up, w_down)


def _window_sum_down(e, window):
    step = 1
    while step < window:
        e = e + pltpu.roll(e, step, axis=0)
        step *= 2
    return e


def _window_sum_up(e, window):
    n = e.shape[0]
    step = 1
    while step < window:
        e = e + pltpu.roll(e, n - step, axis=0)
        step *= 2
    return e


def _pool_counts(first_row, tm, window):
    t = first_row + _row((tm, 1))
    return jnp.minimum(t + 1, window).astype(F32)


def _pool_fwd(h, g, pw, ps):
    s, d = h.shape
    cg = d // len(POOL_WINDOWS)
    tm = min(TILE_ROWS, s)

    def body(h_ref, g_ref, pw_ref, ps_ref, out_ref, nbuf):
        i = pl.program_id(0)

        @pl.when(i == 0)
        def _():
            nbuf[0:POOL_HALO, :] = jnp.zeros((POOL_HALO, d), F32)

        n, _ = _rms_fwd(h_ref[...], g_ref[...])
        nbuf[POOL_HALO:POOL_HALO + tm, :] = n
        for k, window in enumerate(POOL_WINDOWS):
            cols = slice(k * cg, (k + 1) * cg)
            sums = _window_sum_down(nbuf[:, cols], window)[POOL_HALO:, :]
            pooled = sums / _pool_counts(i * tm, tm, window) - n[:, cols]
            y = _nn(pooled.astype(BF16), pw_ref[k]) * ps_ref[:, cols]
            out_ref[:, cols] = h_ref[:, cols] + y
        nbuf[0:POOL_HALO, :] = n[tm - POOL_HALO:tm, :]

    return pl.pallas_call(
        body, name="pool_fwd", grid=(s // tm,),
        in_specs=[pl.BlockSpec((tm, d), lambda i: (i, 0)), _full(g), _full(pw), _full(ps)],
        out_specs=pl.BlockSpec((tm, d), lambda i: (i, 0)),
        out_shape=jax.ShapeDtypeStruct((s, d), F32),
        scratch_shapes=[pltpu.VMEM((tm + POOL_HALO, d), F32)],
        compiler_params=_params(("arbitrary",)),
    )(h, g, pw, ps)


def _final_loss(h, g, target):
    s, d = h.shape
    tm = min(TILE_ROWS, s)

    def body(h_ref, g_ref, t_ref, dh_ref, loss_ref, dg_ref):
        i = pl.program_id(0)
        hv = h_ref[...]
        y, _ = _rms_fwd(hv, g_ref[...])
        err = y - t_ref[...]
        part = 0.5 * jnp.sum(jnp.mean(err * err, axis=-1, keepdims=True), axis=0, keepdims=True)
        dx, dg = _rms_bwd(err / d, hv, g_ref[...])
        dh_ref[...] = dx
        part = jnp.broadcast_to(part, loss_ref.shape)

        @pl.when(i == 0)
        def _():
            loss_ref[...] = part
            dg_ref[...] = dg

        @pl.when(i > 0)
        def _():
            loss_ref[...] += part
            dg_ref[...] += dg

    return pl.pallas_call(
        body, name="final_loss", grid=(s // tm,),
        in_specs=[pl.BlockSpec((tm, d), lambda i: (i, 0)), _full(g),
                  pl.BlockSpec((tm, d), lambda i: (i, 0))],
        out_specs=[pl.BlockSpec((tm, d), lambda i: (i, 0)),
                   pl.BlockSpec((1, 128), lambda i: (0, 0)),
                   pl.BlockSpec((1, d), lambda i: (0, 0))],
        out_shape=[jax.ShapeDtypeStruct((s, d), F32),
                   jax.ShapeDtypeStruct((1, 128), F32),
                   jax.ShapeDtypeStruct((1, d), F32)],
        compiler_params=_params(("arbitrary",)),
    )(h, g, target)


def _mlp_bwd_x(dz, a, w_up, w_down, h_in, g, name):
    s, d = dz.shape
    ff = w_down.shape[0]
    slot_cols = w_up.shape[2]
    tm = min(TILE_MLP_BWD_ROWS, s)
    tf = min(TILE_MLP_FF, slot_cols)
    per_slot = slot_cols // tf
    nf = ff // tf

    def body(dz_ref, a_ref, wu_ref, wd_ref, h_ref, g_ref, da_ref, dzb_ref, dh_ref, dg_ref,
             dzs_ref, acc_ref):
        i = pl.program_id(0)
        f = pl.program_id(1)

        @pl.when(f == 0)
        def _():
            dzb = dz_ref[...].astype(BF16)
            dzs_ref[...] = dzb
            dzb_ref[...] = dzb
            acc_ref[...] = jnp.zeros_like(acc_ref)

        dr = _nt(dzs_ref[...], wd_ref[...])
        da = (dr * (2.0 * jnp.maximum(a_ref[...].astype(F32), 0.0))).astype(BF16)
        da_ref[...] = da
        acc_ref[...] += _nt(da, wu_ref[...])

        @pl.when(f == nf - 1)
        def _():
            dx, dg = _rms_bwd(acc_ref[...], h_ref[...], g_ref[...])
            dh_ref[...] = dz_ref[...] + dx

            @pl.when(i == 0)
            def _():
                dg_ref[...] = dg

            @pl.when(i > 0)
            def _():
                dg_ref[...] += dg

    return pl.pallas_call(
        body, name=name, grid=(s // tm, nf),
        in_specs=[pl.BlockSpec((tm, d), lambda i, f: (i, 0)),
                  pl.BlockSpec((tm, tf), lambda i, f: (i, f)),
                  pl.BlockSpec((None, d, tf), lambda i, f: (f // per_slot, 0, f % per_slot)),
                  pl.BlockSpec((tf, d), lambda i, f: (f, 0)),
                  pl.BlockSpec((tm, d), lambda i, f: (i, 0)), _full(g)],
        out_specs=[pl.BlockSpec((tm, tf), lambda i, f: (i, f)),
                   pl.BlockSpec((tm, d), lambda i, f: (i, 0)),
                   pl.BlockSpec((tm, d), lambda i, f: (i, 0)),
                   pl.BlockSpec((1, d), lambda i, f: (0, 0))],
        out_shape=[jax.ShapeDtypeStruct((s, ff), BF16),
                   jax.ShapeDtypeStruct((s, d), BF16),
                   jax.ShapeDtypeStruct((s, d), F32),
                   jax.ShapeDtypeStruct((1, d), F32)],
        scratch_shapes=[pltpu.VMEM((tm, d), BF16), pltpu.VMEM((tm, d), F32)],
        compiler_params=_params(("arbitrary", "arbitrary")),
    )(dz, a, w_up, w_down, h_in, g)


def _mlp_bwd_w(n, da, a, dzb, slot_cols, name):
    s, d = n.shape
    ff = a.shape[1]
    tn = min(TILE_WGRAD_N, slot_cols)
    tk = min(TILE_WGRAD_K, s)
    per_slot = slot_cols // tn
    nk = s // tk

    def body(n_ref, da_ref, a_ref, dz_ref, du_ref, dd_ref, accu_ref, accd_ref):
        k = pl.program_id(1)

        @pl.when(k == 0)
        def _():
            accu_ref[...] = jnp.zeros_like(accu_ref)
            accd_ref[...] = jnp.zeros_like(accd_ref)

        accu_ref[...] += _tn(n_ref[...], da_ref[...])
        r = jnp.square(jnp.maximum(a_ref[...].astype(F32), 0.0)).astype(BF16)
        accd_ref[...] += _tn(r, dz_ref[...])

        @pl.when(k == nk - 1)
        def _():
            du_ref[...] = accu_ref[...].astype(BF16)
            dd_ref[...] = accd_ref[...].astype(BF16)

    return pl.pallas_call(
        body, name=name, grid=(ff // tn, nk),
        in_specs=[pl.BlockSpec((tk, d), lambda f, k: (k, 0)),
                  pl.BlockSpec((tk, tn), lambda f, k: (k, f)),
                  pl.BlockSpec((tk, tn), lambda f, k: (k, f)),
                  pl.BlockSpec((tk, d), lambda f, k: (k, 0))],
        out_specs=[pl.BlockSpec((None, d, tn), lambda f, k: (f // per_slot, 0, f % per_slot)),
                   pl.BlockSpec((tn, d), lambda f, k: (f, 0))],
        out_shape=[jax.ShapeDtypeStruct((ff // slot_cols, d, slot_cols), BF16),
                   jax.ShapeDtypeStruct((ff, d), BF16)],
        scratch_shapes=[pltpu.VMEM((d, tn), F32), pltpu.VMEM((tn, d), F32)],
        compiler_params=_params(("parallel", "arbitrary")),
    )(n, da, a, dzb)


def _pool_bwd(after, dh, h, g, pw, ps):
    s, d = h.shape
    cg = d // len(POOL_WINDOWS)
    tm = min(TILE_ROWS, s)
    nb = s // tm
    halo_per_tile = tm // POOL_HALO

    def body(after_ref, dh_ref, h_ref, halo_ref, g_ref, pw_ref, ps_ref,
             dx_ref, dpw_ref, dps_ref, dg_ref, nbuf, qbuf, dn_ref, carry, dpw_acc):
        i = pl.program_id(0)
        blk = nb - 1 - i

        @pl.when(i == 0)
        def _():
            carry[...] = jnp.zeros_like(carry)
            dpw_acc[...] = jnp.zeros_like(dpw_acc)
            dps_ref[...] = jnp.zeros_like(dps_ref)
            dg_ref[...] = jnp.zeros_like(dg_ref)

        hv = h_ref[...]
        n, _ = _rms_fwd(hv, g_ref[...])
        nh, _ = _rms_fwd(halo_ref[...], g_ref[...])
        nbuf[0:POOL_HALO, :] = jnp.where(blk == 0, 0.0, nh)
        nbuf[POOL_HALO:POOL_HALO + tm, :] = n
        dhv = dh_ref[...]
        for k, window in enumerate(POOL_WINDOWS):
            cols = slice(k * cg, (k + 1) * cg)
            cnt = _pool_counts(blk * tm, tm, window)
            sums = _window_sum_down(nbuf[:, cols], window)[POOL_HALO:, :]
            pb = (sums / cnt - n[:, cols]).astype(BF16)
            dyk = dhv[:, cols]
            dps_ref[:, cols] += jnp.sum(dyk * _nn(pb, pw_ref[k]), axis=0, keepdims=True)
            dyb = (dyk * ps_ref[:, cols]).astype(BF16)
            dpw_acc[k] += _tn(pb, dyb)
            dpool = _nt(dyb, pw_ref[k])
            qv = dpool / cnt
            qbuf[0:tm, cols] = qv
            qbuf[tm:tm + POOL_HALO, cols] = carry[:, cols]
            dn_ref[:, cols] = _window_sum_up(qbuf[:, cols], window)[0:tm, :] - dpool
            carry[:, cols] = qv[0:POOL_HALO, :]
        dx, dg = _rms_bwd(dn_ref[...], hv, g_ref[...])
        dx_ref[...] = dhv + dx
        dg_ref[...] += dg

        @pl.when(i == nb - 1)
        def _():
            dpw_ref[...] = dpw_acc[...].astype(BF16)

    rev = lambda i: (nb - 1 - i, 0)
    return pl.pallas_call(
        body, name="pool_bwd", grid=(nb,),
        in_specs=[ANY, pl.BlockSpec((tm, d), rev), pl.BlockSpec((tm, d), rev),
                  pl.BlockSpec((POOL_HALO, d),
                               lambda i: (jnp.maximum((nb - 1 - i) * halo_per_tile - 1, 0), 0)),
                  _full(g), _full(pw), _full(ps)],
        out_specs=[pl.BlockSpec((tm, d), rev), _full(pw),
                   pl.BlockSpec((1, d), lambda i: (0, 0)),
                   pl.BlockSpec((1, d), lambda i: (0, 0))],
        out_shape=[jax.ShapeDtypeStruct((s, d), F32),
                   jax.ShapeDtypeStruct(pw.shape, BF16),
                   jax.ShapeDtypeStruct((1, d), F32),
                   jax.ShapeDtypeStruct((1, d), F32)],
        scratch_shapes=[pltpu.VMEM((tm + POOL_HALO, d), F32), pltpu.VMEM((tm + POOL_HALO, d), F32),
                        pltpu.VMEM((tm, d), F32), pltpu.VMEM((POOL_HALO, d), F32),
                        pltpu.VMEM(pw.shape, F32)],
        compiler_params=_params(("arbitrary",)),
    )(after, dh, h, h, g, pw, ps)


def _conv_out_bwd(after, dh, w_out, o, bcx, cw):
    s, d = dh.shape
    c = o.shape[1]
    tm = min(TILE_ROWS, s)
    nb = s // tm
    halo_per_tile = tm // CONV_HALO

    def body(after_ref, dh_ref, w_ref, o_ref, b_ref, c_ref, xin_ref, ch_ref, xh_ref, cw_ref,
             do_ref, delta_ref, dbcx_ref, dw_ref, dcw_ref, ubuf, dbuf, carry, acc):
        i = pl.program_id(0)
        blk = nb - 1 - i

        @pl.when(i == 0)
        def _():
            carry[...] = jnp.zeros_like(carry)
            acc[...] = jnp.zeros_like(acc)
            dcw_ref[...] = jnp.zeros_like(dcw_ref)

        dm = dh_ref[...].astype(BF16)
        dcat = _nt(dm, w_ref[...])
        do = dcat[:, 0:c]
        dy = dcat[:, c:2 * c]
        do_ref[...] = do.astype(BF16)
        head_of_lane = lax.shift_right_logical(_lane((8, c)), HEAD_DIM.bit_length() - 1)
        heads = (head_of_lane == _row((8, c))).astype(BF16)
        delta_ref[...] = _exact_nt(heads, do * o_ref[...].astype(F32))

        cv_ = c_ref[...]
        xin = xin_ref[...]
        bv = b_ref[...]
        u = cv_ * xin
        ubuf[0:CONV_HALO, :] = jnp.where(blk == 0, 0.0, ch_ref[...] * xh_ref[...])
        ubuf[CONV_HALO:CONV_HALO + tm, :] = u
        u1 = ubuf[CONV_HALO - 1:CONV_HALO - 1 + tm, :]
        u2 = ubuf[CONV_HALO - 2:CONV_HALO - 2 + tm, :]
        w0, w1, w2 = cw_ref[0:1, :], cw_ref[1:2, :], cw_ref[2:3, :]
        cv = (w0 * u2 + w1 * u1) + w2 * u
        acc[0:c, :] += _tn(o_ref[...], dm)
        acc[c:2 * c, :] += _tn((bv * cv).astype(BF16), dm)

        dcv = dy * bv
        dcw_ref[0:1, :] += jnp.sum(dcv * u2, axis=0, keepdims=True)
        dcw_ref[1:2, :] += jnp.sum(dcv * u1, axis=0, keepdims=True)
        dcw_ref[2:3, :] += jnp.sum(dcv * u, axis=0, keepdims=True)
        dbuf[0:tm, :] = dcv
        dbuf[tm:tm + CONV_HALO, :] = carry[...]
        du = w2 * dcv + w1 * dbuf[1:1 + tm, :] + w0 * dbuf[2:2 + tm, :]
        dbcx_ref[:, 0:c] = (dy * cv).astype(BF16)
        dbcx_ref[:, c:2 * c] = (du * xin).astype(BF16)
        dbcx_ref[:, 2 * c:3 * c] = (du * cv_).astype(BF16)
        carry[...] = dcv[0:CONV_HALO, :]

        @pl.when(i == nb - 1)
        def _():
            dw_ref[...] = acc[...].astype(BF16)

    rev = lambda k: (lambda i: (nb - 1 - i, k))
    halo = lambda k: (lambda i: (jnp.maximum((nb - 1 - i) * halo_per_tile - 1, 0), k))
    return pl.pallas_call(
        body, name="conv_out_bwd", grid=(nb,),
        in_specs=[ANY, pl.BlockSpec((tm, d), rev(0)), _full(w_out), pl.BlockSpec((tm, c), rev(0)),
                  pl.BlockSpec((tm, c), rev(0)), pl.BlockSpec((tm, c), rev(1)),
                  pl.BlockSpec((tm, c), rev(2)),
                  pl.BlockSpec((CONV_HALO, c), halo(1)), pl.BlockSpec((CONV_HALO, c), halo(2)),
                  _full(cw)],
        out_specs=[pl.BlockSpec((tm, c), rev(0)),
                   pl.BlockSpec((8, tm), lambda i: (0, nb - 1 - i)),
                   pl.BlockSpec((tm, 3 * c), rev(0)),
                   _full(w_out), _full(cw)],
        out_shape=[jax.ShapeDtypeStruct((s, c), BF16),
                   jax.ShapeDtypeStruct((8, s), F32),
                   jax.ShapeDtypeStruct((s, 3 * c), BF16),
                   jax.ShapeDtypeStruct(w_out.shape, BF16),
                   jax.ShapeDtypeStruct(cw.shape, F32)],
        scratch_shapes=[pltpu.VMEM((tm + CONV_HALO, c), F32), pltpu.VMEM((tm + CONV_HALO, c), F32),
                        pltpu.VMEM((CONV_HALO, c), F32), pltpu.VMEM(w_out.shape, F32)],
        compiler_params=_params(("arbitrary",)),
    )(after, dh, w_out, o, bcx, bcx, bcx, bcx, bcx, cw)


def _attn_bwd(qa, ka, qkv, do, lse, delta):
    s = qa.shape[1]
    a = N_HEADS * HEAD_DIM
    t = min(TILE_ATTN, s)
    nq = s // t
    n_pairs = N_HEADS // 2
    v_block0 = 2 * a // 128

    def body(ka_ref, v_ref, qa_ref, do_ref, lse_ref, delta_ref,
             dqa_ref, dka_ref, dv_ref, dk_acc, dv_acc):
        g = pl.program_id(0)
        j = pl.program_id(1)

        @pl.when(j == 0)
        def _():
            dqa_ref[...] = jnp.zeros_like(dqa_ref)

        dk_acc[...] = jnp.zeros_like(dk_acc)
        dv_acc[...] = jnp.zeros_like(dv_acc)
        lane = _lane((t, 128))
        vf = v_ref[...].astype(F32)
        v_heads = [jnp.where(lane < HEAD_DIM, vf, 0.0).astype(BF16),
                   jnp.where(lane >= HEAD_DIM, vf, 0.0).astype(BF16)]

        def q_step(i, masked):
            qs = pl.ds(pl.multiple_of(i * t, t), t)
            dob = do_ref[qs, :]
            for e in range(2):
                ke = ka_ref[e]
                qe = qa_ref[e, qs, :]
                sc = _nt(ke, qe)
                if masked:
                    sc = jnp.where(_row((t, t)) <= _lane((t, t)), sc, NEG_BIG)
                p = jnp.exp(sc - lse_ref[pl.ds(e, 1), qs])
                dv_acc[e] += _nn(p.astype(BF16), dob)
                dp = _nt(v_heads[e], dob)
                ds = (p * (dp - delta_ref[pl.ds(2 * g + e, 1), qs])).astype(BF16)
                dk_acc[e] += _nn(ds, qe)
                dqa_ref[e, qs, :] += _tn(ds, ke)

        q_step(j, True)

        def full_step(i, carry):
            q_step(i, False)
            return carry

        lax.fori_loop(j + 1, nq, full_step, 0)
        dka_ref[...] = dk_acc[...]
        dv_ref[...] = jnp.where(lane < HEAD_DIM, dv_acc[0], dv_acc[1]).astype(BF16)

    return pl.pallas_call(
        body, name="attn_bwd", grid=(n_pairs, nq),
        in_specs=[pl.BlockSpec((2, t, 128), lambda g, j: (g, j, 0)),
                  pl.BlockSpec((t, 128), lambda g, j: (j, v_block0 + g)),
                  pl.BlockSpec((2, s, 128), lambda g, j: (g, 0, 0)),
                  pl.BlockSpec((s, 128), lambda g, j: (0, g)),
                  pl.BlockSpec((None, 8, s), lambda g, j: (g, 0, 0)),
                  pl.BlockSpec((8, s), lambda g, j: (0, 0))],
        out_specs=[pl.BlockSpec((2, s, 128), lambda g, j: (g, 0, 0)),
                   pl.BlockSpec((2, t, 128), lambda g, j: (g, j, 0)),
                   pl.BlockSpec((t, 128), lambda g, j: (j, g))],
        out_shape=[jax.ShapeDtypeStruct((N_HEADS, s, 128), F32),
                   jax.ShapeDtypeStruct((N_HEADS, s, 128), F32),
                   jax.ShapeDtypeStruct((s, a), BF16)],
        scratch_shapes=[pltpu.VMEM((2, t, 128), F32), pltpu.VMEM((2, t, 128), F32)],
        compiler_params=_params(("parallel", "arbitrary")),
    )(ka, qkv, qa, do, lse, delta)


def _gate_bwd(dqa, dka, dv, fl, bf):
    s = fl.shape[0]
    a = N_HEADS * HEAD_DIM
    tm = min(TILE_ROWS, s)
    nb = s // tm

    def body(dqa_ref, dka_ref, dv_ref, fl_ref, bf_ref, dqkv_ref, dfl_ref, dbf_ref, carry):
        i = pl.program_id(0)

        @pl.when(i == 0)
        def _():
            carry[...] = jnp.zeros_like(carry)
            dbf_ref[...] = jnp.zeros_like(dbf_ref)

        lane = _lane((tm, 128))
        dcum = jnp.zeros((tm, 128), F32)
        for pair in range(N_HEADS // 2):
            qs, ks = [], []
            for e in range(2):
                h = 2 * pair + e
                dq = dqa_ref[h]
                dk = dka_ref[h]
                dc = jnp.sum(jnp.where(lane == LANE_CQ, dq, 0.0)
                             - jnp.where(lane == LANE_ONE, dk, 0.0), axis=1, keepdims=True)
                dcum = jnp.where(lane == h, dc, dcum)
                qs.append(dq * ATTN_SCALE)
                ks.append(dk)
            cols = slice(pair * 128, (pair + 1) * 128)
            dqkv_ref[:, cols] = jnp.where(
                lane < HEAD_DIM, qs[0], pltpu.roll(qs[1], HEAD_DIM, axis=1)).astype(BF16)
            dqkv_ref[:, a + pair * 128:a + (pair + 1) * 128] = jnp.where(
                lane < HEAD_DIM, ks[0], pltpu.roll(ks[1], HEAD_DIM, axis=1)).astype(BF16)
        dqkv_ref[:, 2 * a:3 * a] = dv_ref[...]

        upper = (_lane((tm, tm)) >= _row((tm, tm))).astype(BF16)
        dlogf = _exact_nn(upper, dcum) + carry[0:1, :]
        carry[0:1, :] = dlogf[0:1, :]
        z = fl_ref[...] + bf_ref[...]
        ez = jnp.exp(-jnp.abs(z))
        sig_neg = jnp.where(z >= 0.0, ez, 1.0) / (1.0 + ez)
        dz = jnp.where(lane < N_HEADS, dlogf * sig_neg, 0.0)
        dfl_ref[...] = dz.astype(BF16)
        dbf_ref[...] += jnp.sum(dz, axis=0, keepdims=True)

    rev3 = lambda i: (0, nb - 1 - i, 0)
    rev = lambda i: (nb - 1 - i, 0)
    return pl.pallas_call(
        body, name="gate_bwd", grid=(nb,),
        in_specs=[pl.BlockSpec((N_HEADS, tm, 128), rev3), pl.BlockSpec((N_HEADS, tm, 128), rev3),
                  pl.BlockSpec((tm, a), rev), pl.BlockSpec((tm, 128), rev), _full(bf)],
        out_specs=[pl.BlockSpec((tm, 3 * a), rev), pl.BlockSpec((tm, 128), rev),
                   pl.BlockSpec((1, 128), lambda i: (0, 0))],
        out_shape=[jax.ShapeDtypeStruct((s, 3 * a), BF16),
                   jax.ShapeDtypeStruct((s, 128), BF16),
                   jax.ShapeDtypeStruct((1, 128), F32)],
        scratch_shapes=[pltpu.VMEM((8, 128), F32)],
        compiler_params=_params(("arbitrary",)),
    )(dqa, dka, dv, fl, bf)


def _in_proj_bwd(after, dqkv, dfl, dbcx, w_qkv, w_f, w_bcx, x, g, dh):
    s, d = x.shape
    tm = min(TILE_ROWS, s)

    def body(after_ref, dq_ref, df_ref, db_ref, wq_ref, wf_ref, wb_ref, x_ref, g_ref, dh_ref,
             gx_ref, dg_ref):
        i = pl.program_id(0)
        dn = (_nt(dq_ref[...], wq_ref[...]) + _nt(df_ref[...], wf_ref[...])
              + _nt(db_ref[...], wb_ref[...]))
        dx, dg = _rms_bwd(dn, x_ref[...], g_ref[...])
        gx_ref[...] = dh_ref[...] + dx

        @pl.when(i == 0)
        def _():
            dg_ref[...] = dg

        @pl.when(i > 0)
        def _():
            dg_ref[...] += dg

    rows = lambda c: pl.BlockSpec((tm, c), lambda i: (i, 0))
    return pl.pallas_call(
        body, name="in_proj_bwd", grid=(s // tm,),
        in_specs=[ANY, rows(dqkv.shape[1]), rows(dfl.shape[1]), rows(dbcx.shape[1]),
                  _full(w_qkv), _full(w_f), _full(w_bcx), rows(d), _full(g), rows(d)],
        out_specs=[rows(d), pl.BlockSpec((1, d), lambda i: (0, 0))],
        out_shape=[jax.ShapeDtypeStruct((s, d), F32), jax.ShapeDtypeStruct((1, d), F32)],
        compiler_params=_params(("arbitrary",)),
    )(after, dqkv, dfl, dbcx, w_qkv, w_f, w_bcx, x, g, dh)


def _wgrad_in(n, dys):
    s, d = n.shape
    m = len(dys)
    tk = min(TILE_ROWS, s)
    nk = s // tk

    def body(*refs):
        n_ref, dy_refs, dw_refs, accs = refs[0], refs[1:1 + m], refs[1 + m:1 + 2 * m], refs[1 + 2 * m:]
        k = pl.program_id(0)

        @pl.when(k == 0)
        def _():
            for acc in accs:
                acc[...] = jnp.zeros_like(acc)

        nb = n_ref[...]
        for dy_ref, acc in zip(dy_refs, accs):
            acc[...] += _tn(nb, dy_ref[...])

        @pl.when(k == nk - 1)
        def _():
            for dw_ref, acc in zip(dw_refs, accs):
                dw_ref[...] = acc[...].astype(BF16)

    return pl.pallas_call(
        body, name="wgrad_in", grid=(nk,),
        in_specs=[pl.BlockSpec((tk, d), lambda k: (k, 0))]
        + [pl.BlockSpec((tk, dy.shape[1]), lambda k: (k, 0)) for dy in dys],
        out_specs=[pl.BlockSpec((d, dy.shape[1]), lambda k: (0, 0)) for dy in dys],
        out_shape=[jax.ShapeDtypeStruct((d, dy.shape[1]), BF16) for dy in dys],
        scratch_shapes=[pltpu.VMEM((d, dy.shape[1]), F32) for dy in dys],
        compiler_params=_params(("arbitrary",)),
    )(n, *dys)


def _row_tile(rows):
    t = min(TILE_ELEM_ROWS, rows)
    while rows % t:
        t //= 2
    return t


def _sum_pair(grad, theirs, core, name):
    slots, rows, cols = theirs.shape
    tr = _row_tile(rows)
    nb = rows // tr

    def body(core_ref, a_ref, b_ref, o_ref):
        o_ref[...] = (a_ref[...].astype(F32) + b_ref[...].astype(F32)).astype(BF16)

    spec = pl.BlockSpec((None, tr, cols), lambda s, i, core_ref: (s, i, 0))
    return pl.pallas_call(
        body, name=name,
        grid_spec=pltpu.PrefetchScalarGridSpec(
            num_scalar_prefetch=1, grid=(slots, nb),
            in_specs=[pl.BlockSpec((None, tr, cols),
                                   lambda s, i, core_ref: (s, core_ref[0] * nb + i, 0)), spec],
            out_specs=spec),
        out_shape=jax.ShapeDtypeStruct(theirs.shape, BF16),
        compiler_params=_params(("parallel", "parallel")),
    )(core, grad, theirs)


def _sum_chips(sums, others, chip, name):
    _, rows, cols = sums.shape
    tr = _row_tile(rows)

    def body(chip_ref, a_ref, b_ref, o_ref):
        acc = a_ref[...].astype(F32)
        for k in range(N_CHIPS - 1):
            acc = acc + b_ref[k].astype(F32)
        o_ref[...] = acc

    return pl.pallas_call(
        body, name=name,
        grid_spec=pltpu.PrefetchScalarGridSpec(
            num_scalar_prefetch=1, grid=(rows // tr,),
            in_specs=[pl.BlockSpec((None, tr, cols), lambda i, chip_ref: (chip_ref[0], i, 0)),
                      pl.BlockSpec((N_CHIPS - 1, tr, cols), lambda i, chip_ref: (0, i, 0))],
            out_specs=pl.BlockSpec((tr, cols), lambda i, chip_ref: (i, 0))),
        out_shape=jax.ShapeDtypeStruct((rows, cols), F32),
        compiler_params=_params(("parallel",)),
    )(chip, sums, others)


def _adamw_math(w, g, m, v):
    m = ADAM_B1 * m + (1.0 - ADAM_B1) * g
    v = ADAM_B2 * v + (1.0 - ADAM_B2) * jnp.square(g)
    m_hat = m / (1.0 - ADAM_B1 ** ADAM_STEP)
    v_hat = v / (1.0 - ADAM_B2 ** ADAM_STEP)
    delta = -ADAM_LR * (m_hat / (jnp.sqrt(v_hat) + ADAM_EPS) + ADAM_WD * w)
    return delta, m, v


def _adamw(w, g, m, v, name):
    rows, cols = w.shape
    tr = _row_tile(rows)

    def body(w_ref, g_ref, m_ref, v_ref, d_ref, nm_ref, nv_ref):
        delta, nm, nv = _adamw_math(w_ref[...], g_ref[...], m_ref[...], v_ref[...])
        d_ref[...] = delta
        nm_ref[...] = nm
        nv_ref[...] = nv

    spec = pl.BlockSpec((tr, cols), lambda i: (i, 0))
    out = jax.ShapeDtypeStruct(w.shape, F32)
    return pl.pallas_call(
        body, name=name, grid=(rows // tr,), in_specs=[spec] * 4, out_specs=[spec] * 3,
        out_shape=[out, out, out], compiler_params=_params(("parallel",)),
    )(w, g, m, v)


def _sum_devices(parts):
    def body(p_ref, g_ref):
        g = p_ref[0]
        for k in range(1, N_DEV):
            g = g + p_ref[k]
        g_ref[...] = g

    return pl.pallas_call(
        body, name="sum_devices",
        in_specs=[pl.BlockSpec(memory_space=pltpu.VMEM)],
        out_specs=pl.BlockSpec(memory_space=pltpu.VMEM),
        out_shape=jax.ShapeDtypeStruct(parts.shape[1:], F32),
    )(parts)


def _mesh_position():
    x, y, c = lax.axis_index("x"), lax.axis_index("y"), lax.axis_index("c")
    chips = [(1 - x, y), (x, 1 - y), (1 - x, 1 - y)]
    return x, y, c, chips


ANY = pl.BlockSpec(memory_space=pl.ANY)
HBM = pl.BlockSpec(memory_space=pltpu.HBM)
SEM = pl.BlockSpec(memory_space=pltpu.SEMAPHORE)
SPLIT_COPY_EFFECT = pltpu.SideEffectType.DATAFLOW_SIDE_EFFECTING


def _in_hbm(a):
    return pltpu.with_memory_space_constraint(a, pltpu.HBM)


def _chip_copies(views, srcs, lands, send, recv):
    _, _, c, chips = _mesh_position()
    cps = []
    for a in range(len(srcs)):
        for k, (px, py) in enumerate(chips):
            src, dst = views(a, k, srcs[a], lands[a], c, 2 * px + py)
            sem = a * (N_CHIPS - 1) + k
            cps.append(pltpu.make_async_remote_copy(
                src_ref=src, dst_ref=dst, send_sem=send.at[sem], recv_sem=recv.at[sem],
                device_id=(px, py, c), device_id_type=MESH))
    return cps


def _ici_start(sources, land_shapes, views, after, name):
    n = len(sources)

    def body(*refs):
        srcs, lands = refs[:n], refs[n:2 * n]
        send, recv = refs[2 * n + 1], refs[2 * n + 2]
        token = refs[-1]
        for cp in _chip_copies(views, srcs, lands, send, recv):
            cp.start()
        token[...] = jnp.zeros_like(token)

    lands = [_in_hbm(lax.empty(s.shape, s.dtype)) for s in land_shapes]
    outs = pl.pallas_call(
        body, name=name,
        in_specs=[HBM] * (2 * n) + [ANY],
        out_specs=[SEM, SEM] + [HBM] * (2 * n) + [pl.BlockSpec(memory_space=pltpu.VMEM)],
        out_shape=[pltpu.SemaphoreType.DMA((n * (N_CHIPS - 1),))] * 2
        + [pltpu.HBM(a.shape, a.dtype) for a in sources]
        + [pltpu.HBM(s.shape, s.dtype) for s in land_shapes]
        + [jax.ShapeDtypeStruct((8, 128), F32)],
        input_output_aliases={i: 2 + i for i in range(2 * n)},
        compiler_params=pltpu.CompilerParams(has_side_effects=SPLIT_COPY_EFFECT),
    )(*[_in_hbm(a) for a in sources], *lands, after)
    return outs[0], outs[1], list(outs[2:2 + n]), list(outs[2 + n:2 + 2 * n]), outs[-1]


def _ici_wait(handle, views, after, name):
    send, recv, srcs, lands, _ = handle
    n = len(srcs)

    def body(*refs):
        src_refs, land_refs = refs[:n], refs[n:2 * n]
        for cp in _chip_copies(views, src_refs, land_refs, refs[2 * n], refs[2 * n + 1]):
            cp.wait_send()
            cp.wait_recv()

    outs = pl.pallas_call(
        body, name=name,
        in_specs=[HBM] * (2 * n) + [SEM, SEM, ANY],
        out_specs=[HBM] * (2 * n),
        out_shape=[pltpu.HBM(a.shape, a.dtype) for a in srcs]
        + [pltpu.HBM(a.shape, a.dtype) for a in lands],
        input_output_aliases={i: i for i in range(2 * n)},
        compiler_params=pltpu.CompilerParams(has_side_effects=SPLIT_COPY_EFFECT),
    )(*srcs, *lands, send, recv, after)
    return list(outs[:n]), list(outs[n:])


def _gather_views(split):
    def views(a, k, src, land, c, slot):
        if split[a]:
            half = src.shape[0] // 2
            src = src.at[pl.ds(c * half, half)]
        return src, land.at[k]
    return views


def _scatter_views(a, k, src, land, c, slot):
    return src.at[slot], land.at[k]


def _gather_land_shapes(shards, split):
    return [jax.ShapeDtypeStruct(
        (N_CHIPS - 1, a.shape[0] // 2 if sp else a.shape[0]) + a.shape[1:], a.dtype)
        for a, sp in zip(shards, split)]


def _gather_finish(shards, lands, split, name):
    n = len(shards)
    ns = sum(split)
    d_index = {a: i for i, a in enumerate(a for a in range(n) if split[a])}

    def body(*refs):
        shard, land, outs = refs[:n], refs[n:2 * n], refs[2 * n:3 * n]
        obuf, fbuf = refs[3 * n:4 * n], refs[4 * n:5 * n]
        dbuf = refs[5 * n:5 * n + ns]
        ld_own, st_own, ld, st_mine, st_sib, send, recv = refs[5 * n + ns:]
        x, y, c, chips = _mesh_position()
        me = 2 * x + y
        own_loads, loads, sends, pending = [], {}, [], []
        for a in range(n):
            cp = pltpu.make_async_copy(shard[a], obuf[a], ld_own.at[a])
            cp.start()
            own_loads.append(cp)
        for a in range(n):
            for k in range(N_CHIPS - 1):
                cp = pltpu.make_async_copy(land[a].at[k], fbuf[a].at[k], ld.at[a, k])
                cp.start()
                loads[a, k] = cp
        for a in range(n):
            own_loads[a].wait()
            cp = pltpu.make_async_copy(obuf[a], outs[a].at[me], st_own.at[a])
            cp.start()
            pending.append(cp)
        for a in range(n):
            rows = shard[a].shape[0]
            for k, (px, py) in enumerate(chips):
                loads[a, k].wait()
                part = pl.ds(c * (rows // 2), rows // 2) if split[a] else pl.ds(0, rows)
                cp = pltpu.make_async_copy(fbuf[a].at[k], outs[a].at[2 * px + py, part],
                                           st_mine.at[a, k])
                cp.start()
                pending.append(cp)
                if split[a]:
                    fw = pltpu.make_async_remote_copy(
                        src_ref=fbuf[a].at[k], dst_ref=dbuf[d_index[a]].at[k],
                        send_sem=send.at[a, k], recv_sem=recv.at[a, k],
                        device_id=(x, y, 1 - c), device_id_type=MESH)
                    fw.start()
                    sends.append((a, k, fw))
        for a, k, fw in sends:
            px, py = chips[k]
            half = shard[a].shape[0] // 2
            fw.wait_recv()
            cp = pltpu.make_async_copy(dbuf[d_index[a]].at[k],
                                       outs[a].at[2 * px + py, pl.ds((1 - c) * half, half)],
                                       st_sib.at[a, k])
            cp.start()
            pending.append(cp)
        for _, _, fw in sends:
            fw.wait_send()
        for cp in pending:
            cp.wait()

    stage = [pltpu.VMEM(a.shape, a.dtype) for a in lands]
    dma = lambda *shape: pltpu.SemaphoreType.DMA(shape)
    return pl.pallas_call(
        body, name=name,
        in_specs=[ANY] * (2 * n), out_specs=[ANY] * n,
        out_shape=[jax.ShapeDtypeStruct((N_CHIPS,) + a.shape, a.dtype) for a in shards],
        scratch_shapes=[pltpu.VMEM(a.shape, a.dtype) for a in shards] + stage
        + [s for s, sp in zip(stage, split) if sp]
        + [dma(n), dma(n), dma(n, 3), dma(n, 3), dma(n, 3), dma(n, 3), dma(n, 3)],
        compiler_params=pltpu.CompilerParams(vmem_limit_bytes=VMEM_LIMIT_BYTES),
    )(*shards, *lands)


def _exchange_siblings(grads, name):
    n = len(grads)

    def body(*refs):
        ins, theirs = refs[:n], refs[n:2 * n]
        sbuf, rbuf = refs[2 * n:3 * n], refs[3 * n:4 * n]
        ld, st, send, recv = refs[4 * n:]
        x, y, c, _ = _mesh_position()
        loads, sends, stores = [], [], []
        for a in range(n):
            half = ins[a].shape[1] // 2
            cp = pltpu.make_async_copy(ins[a].at[:, pl.ds((1 - c) * half, half)], sbuf[a], ld.at[a])
            cp.start()
            loads.append(cp)
        for a in range(n):
            loads[a].wait()
            rc = pltpu.make_async_remote_copy(
                src_ref=sbuf[a], dst_ref=rbuf[a], send_sem=send.at[a], recv_sem=recv.at[a],
                device_id=(x, y, 1 - c), device_id_type=MESH)
            rc.start()
            sends.append(rc)
        for a in range(n):
            sends[a].wait_recv()
            cp = pltpu.make_async_copy(rbuf[a], theirs[a], st.at[a])
            cp.start()
            stores.append(cp)
        for a in range(n):
            sends[a].wait_send()
            stores[a].wait()

    half_shape = lambda a: (a.shape[0], a.shape[1] // 2, a.shape[2])
    stage = [pltpu.VMEM(half_shape(a), a.dtype) for a in grads]
    return pl.pallas_call(
        body, name=name,
        in_specs=[ANY] * n, out_specs=[ANY] * n,
        out_shape=[jax.ShapeDtypeStruct(half_shape(a), a.dtype) for a in grads],
        scratch_shapes=stage + stage + [pltpu.SemaphoreType.DMA((n,))] * 4,
        compiler_params=pltpu.CompilerParams(vmem_limit_bytes=VMEM_LIMIT_BYTES),
    )(*grads)


def _share_halves(halves):
    n = len(halves)

    def body(*refs):
        ins, outs = refs[:n], refs[n:2 * n]
        sbuf, rbuf = refs[2 * n:3 * n], refs[3 * n:4 * n]
        ld, st_own, st_sib, send, recv = refs[4 * n:]
        x, y, c, _ = _mesh_position()
        loads, sends, stores = [], [], []
        for a in range(n):
            cp = pltpu.make_async_copy(ins[a], sbuf[a], ld.at[a])
            cp.start()
            loads.append(cp)
        for a in range(n):
            half = ins[a].shape[0]
            loads[a].wait()
            rc = pltpu.make_async_remote_copy(
                src_ref=sbuf[a], dst_ref=rbuf[a], send_sem=send.at[a], recv_sem=recv.at[a],
                device_id=(x, y, 1 - c), device_id_type=MESH)
            rc.start()
            sends.append(rc)
            cp = pltpu.make_async_copy(sbuf[a], outs[a].at[pl.ds(c * half, half)], st_own.at[a])
            cp.start()
            stores.append(cp)
        for a in range(n):
            half = ins[a].shape[0]
            sends[a].wait_recv()
            cp = pltpu.make_async_copy(rbuf[a], outs[a].at[pl.ds((1 - c) * half, half)], st_sib.at[a])
            cp.start()
            stores.append(cp)
        for cp in sends:
            cp.wait_send()
        for cp in stores:
            cp.wait()

    stage = [pltpu.VMEM(a.shape, a.dtype) for a in halves]
    return pl.pallas_call(
        body, name="share_halves",
        in_specs=[ANY] * n, out_specs=[ANY] * n,
        out_shape=[jax.ShapeDtypeStruct((2 * a.shape[0],) + a.shape[1:], a.dtype)
                   for a in halves],
        scratch_shapes=stage + stage + [pltpu.SemaphoreType.DMA((n,))] * 5,
        compiler_params=pltpu.CompilerParams(vmem_limit_bytes=VMEM_LIMIT_BYTES),
    )(*halves)


def _gather_small(part):
    def body(in_ref, out_ref, send, recv, local):
        x, y, c, _ = _mesh_position()
        me = 4 * x + 2 * y + c
        cps = [pltpu.make_async_copy(in_ref, out_ref.at[me], local)]
        k = 0
        for fx in range(2):
            for fy in range(2):
                for fc in range(2):
                    if fx or fy or fc:
                        cps.append(pltpu.make_async_remote_copy(
                            src_ref=in_ref, dst_ref=out_ref.at[me], send_sem=send.at[k],
                            recv_sem=recv.at[k], device_id=(x ^ fx, y ^ fy, c ^ fc),
                            device_id_type=MESH))
                        k += 1
        for cp in cps:
            cp.start()
        for cp in cps:
            cp.wait()

    return pl.pallas_call(
        body, name="gather_small",
        in_specs=[pl.BlockSpec(memory_space=pltpu.VMEM)],
        out_specs=pl.BlockSpec(memory_space=pltpu.VMEM),
        out_shape=jax.ShapeDtypeStruct((N_DEV,) + part.shape, part.dtype),
        scratch_shapes=[pltpu.SemaphoreType.DMA((N_DEV - 1,)), pltpu.SemaphoreType.DMA((N_DEV - 1,)),
                        pltpu.SemaphoreType.DMA],
    )(part)


def _scatter_start(grads, core, tag):
    theirs = _exchange_siblings(grads, "exchange_siblings_" + tag)
    sums = [_sum_pair(g, t, core, "sum_siblings_%s_%d" % (tag, i))
            for i, (g, t) in enumerate(zip(grads, theirs))]
    lands = [jax.ShapeDtypeStruct((N_CHIPS - 1,) + s.shape[1:], s.dtype) for s in sums]
    return _ici_start(sums, lands, _scatter_views, theirs[0], "scatter_start_" + tag)


def _scatter_finish(handle, chip, after, tag):
    sums, got = _ici_wait(handle, _scatter_views, after, "scatter_wait_" + tag)
    return [_sum_chips(s, g, chip, "sum_chips_%s_%d" % (tag, i))
            for i, (s, g) in enumerate(zip(sums, got))]


def _pad_rows(a, rows):
    return jnp.pad(a, ((0, rows - a.shape[0]), (0, 0)))


def kernel(x, norm_mix_0, w_in_0, b_f_0, conv_w_0, w_out_0, norm_ffn_0, w_up_0, w_down_0, norm_mix_1, pool_w_1, pool_scale_1, norm_ffn_1, w_up_1, w_down_1, final_norm, loss_target, m_norm_mix_0, m_w_in_0, m_b_f_0, m_conv_w_0, m_w_out_0, m_norm_ffn_0, m_w_up_0, m_w_down_0, m_norm_mix_1, m_pool_w_1, m_pool_scale_1, m_norm_ffn_1, m_w_up_1, m_w_down_1, m_final_norm, v_norm_mix_0, v_w_in_0, v_b_f_0, v_conv_w_0, v_w_out_0, v_norm_ffn_0, v_w_up_0, v_w_down_0, v_norm_mix_1, v_pool_w_1, v_pool_scale_1, v_norm_ffn_1, v_w_up_1, v_w_down_1, v_final_norm):
    d = x.shape[-1]
    a = N_HEADS * HEAD_DIM
    c_conv = conv_w_0.shape[1] * N_CHIPS
    xs = x[0]
    target = loss_target[0]
    row = lambda vec: vec.reshape(1, -1)

    big = [w_in_0, w_out_0, w_up_0, w_down_0, pool_w_1, w_up_1, w_down_1]
    first = [w_in_0.astype(BF16), w_out_0.astype(BF16), conv_w_0]
    first_split = [True, True, False]
    rest = [w.astype(BF16) for w in (w_up_0, w_down_0, pool_w_1, w_up_1, w_down_1)]
    rest_split = [True] * len(rest)
    start_a = _ici_start(first, _gather_land_shapes(first, first_split),
                         _gather_views(first_split), b_f_0, "gather_start_a")
    start_b = _ici_start(rest, _gather_land_shapes(rest, rest_split),
                         _gather_views(rest_split), start_a[-1], "gather_start_b")
    first, land_a = _ici_wait(start_a, _gather_views(first_split), start_b[-1], "gather_wait_a")
    g_in, g_out, g_conv = _gather_finish(first, land_a, first_split, "gather_finish_a")
    w_in = g_in.transpose(1, 0, 2).reshape(d, -1)
    w_qkv = w_in[:, :3 * a]
    w_f = jnp.pad(w_in[:, 3 * a:3 * a + N_HEADS], ((0, 0), (0, 128 - N_HEADS)))
    w_bcx = w_in[:, 3 * a + N_HEADS:]
    w_out = g_out.reshape(-1, d)
    conv_w = _pad_rows(g_conv.transpose(1, 0, 2).reshape(conv_w_0.shape[0], c_conv), 8)
    bf = jnp.pad(b_f_0, (0, 128 - N_HEADS)).reshape(1, 128)

    n0, qkv, fl, bcx = _ln_proj(xs, row(norm_mix_0), w_qkv, w_f, w_bcx)
    qa, ka = _gate_prep(fl, bf, qkv)
    o, lse = _attn_fwd(qa, ka, qkv)
    h1 = _conv_out(o, bcx, conv_w, w_out, xs)
    rest, land_b = _ici_wait(start_b, _gather_views(rest_split), h1, "gather_wait_b")
    g_up0, g_down0, g_pool, g_up1, g_down1 = _gather_finish(rest, land_b, rest_split,
                                                            "gather_finish_b")
    w_down0 = g_down0.reshape(-1, d)
    w_down1 = g_down1.reshape(-1, d)
    pool_w = g_pool.transpose(1, 0, 2, 3).reshape(pool_w_1.shape[0], -1, pool_w_1.shape[2])
    h2, a0, nf0 = _mlp_fwd(h1, row(norm_ffn_0), g_up0, w_down0, "mlp_fwd_0")
    h3 = _pool_fwd(h2, row(norm_mix_1), pool_w, row(pool_scale_1))
    h4, a1, nf1 = _mlp_fwd(h3, row(norm_ffn_1), g_up1, w_down1, "mlp_fwd_1")
    dh4, loss_part, d_final = _final_loss(h4, row(final_norm), target)

    slot_cols = g_up0.shape[2]
    pool_cols = pool_w.shape[2]
    core = lax.axis_index("c").astype(jnp.int32).reshape(1)
    chip_index = (2 * lax.axis_index("x") + lax.axis_index("y")).astype(jnp.int32).reshape(1)
    da1, dz1, dh3, d_nffn1 = _mlp_bwd_x(dh4, a1, g_up1, w_down1, h3, row(norm_ffn_1), "mlp_bwd_x_1")
    dw_up1, dw_down1 = _mlp_bwd_w(nf1, da1, a1, dz1, slot_cols, "mlp_bwd_w_1")
    scatter_1 = _scatter_start([dw_up1, dw_down1.reshape(N_CHIPS, -1, d)], core, "mlp1")
    dh2, dw_pool, d_pscale, d_nmix1 = _pool_bwd(scatter_1[-1], dh3, h2, row(norm_mix_1), pool_w,
                                                row(pool_scale_1))
    da0, dz0, dh1, d_nffn0 = _mlp_bwd_x(dh2, a0, g_up0, w_down0, h1, row(norm_ffn_0), "mlp_bwd_x_0")
    dw_up0, dw_down0 = _mlp_bwd_w(nf0, da0, a0, dz0, slot_cols, "mlp_bwd_w_0")
    dw_pool = (dw_pool.reshape(pool_w.shape[0], N_CHIPS, -1, pool_cols).transpose(1, 0, 2, 3)
               .reshape(N_CHIPS, -1, pool_cols))
    scatter_0 = _scatter_start([dw_up0, dw_down0.reshape(N_CHIPS, -1, d), dw_pool], core, "mlp0")
    do, delta, dbcx, dw_out, d_conv = _conv_out_bwd(scatter_0[-1], dh1, w_out, o, bcx, conv_w)
    dqa, dka, dv = _attn_bwd(qa, ka, qkv, do, lse, delta)
    dqkv, dfl, d_bf = _gate_bwd(dqa, dka, dv, fl, bf)
    dw_qkv, dw_f, dw_bcx = _wgrad_in(n0, [dqkv, dfl, dbcx])
    dw_in = jnp.concatenate([dw_qkv, dw_f[:, :N_HEADS], dw_bcx], axis=1)
    scatter_m = _scatter_start([dw_in.reshape(d, N_CHIPS, -1).transpose(1, 0, 2),
                                dw_out.reshape(N_CHIPS, -1, d)], core, "mixer")
    grad_x, d_nmix0 = _in_proj_bwd(scatter_m[-1], dqkv, dfl, dbcx, w_qkv, w_f, w_bcx, xs,
                                   row(norm_mix_0), dh1)

    h_up1, h_down1 = _scatter_finish(scatter_1, chip_index, grad_x, "mlp1")
    h_up0, h_down0, h_pool = _scatter_finish(scatter_0, chip_index, grad_x, "mlp0")
    h_in, h_out = _scatter_finish(scatter_m, chip_index, grad_x, "mixer")
    reduced = _share_halves([h_in, h_out, h_up0, h_down0, h_pool, h_up1, h_down1])
    moments = [(m_w_in_0, v_w_in_0), (m_w_out_0, v_w_out_0), (m_w_up_0, v_w_up_0),
               (m_w_down_0, v_w_down_0), (m_pool_w_1, v_pool_w_1), (m_w_up_1, v_w_up_1),
               (m_w_down_1, v_w_down_1)]
    big_out = []
    for k, (w, g, (m, v)) in enumerate(zip(big, reduced, moments)):
        flat = lambda t: t.reshape(-1, t.shape[-1])
        delta_w, new_m, new_v = _adamw(flat(w), flat(g), flat(m), flat(v), "adamw_%d" % k)
        big_out.append((g.reshape(w.shape), delta_w.reshape(w.shape), new_m.reshape(w.shape),
                        new_v.reshape(w.shape)))

    tail = jnp.concatenate([d_conv[0:3].reshape(-1)[d:], d_bf[0, :N_HEADS], loss_part[0, :1]])
    small_part = jnp.concatenate(
        [d_nmix0, d_nffn0, d_nmix1, d_pscale, d_nffn1, d_final,
         d_conv[0:3].reshape(1, -1)[:, :d],
         jnp.pad(tail, (0, d - tail.shape[0])).reshape(1, d)], axis=0)
    parts = _gather_small(small_part)

    chip = 2 * lax.axis_index("x") + lax.axis_index("y")
    cw_cols = conv_w_0.shape[1]

    def conv_block(full):
        mine = lax.dynamic_slice_in_dim(full, chip * cw_cols, cw_cols, axis=1)
        return jnp.pad(mine.reshape(-1), (0, d - mine.size))

    def small_rows(vals, cw, bfv):
        return jnp.stack(list(vals) + [cw, jnp.pad(bfv, (0, d - N_HEADS))])

    smalls_w = [norm_mix_0, norm_ffn_0, norm_mix_1, pool_scale_1, norm_ffn_1, final_norm]
    smalls_m = [m_norm_mix_0, m_norm_ffn_0, m_norm_mix_1, m_pool_scale_1, m_norm_ffn_1, m_final_norm]
    smalls_v = [v_norm_mix_0, v_norm_ffn_0, v_norm_mix_1, v_pool_scale_1, v_norm_ffn_1, v_final_norm]
    pad_cw = lambda t: jnp.pad(t.reshape(-1), (0, d - t.size))
    w_rows = small_rows(smalls_w, pad_cw(conv_w_0), b_f_0)
    m_rows = small_rows(smalls_m, pad_cw(m_conv_w_0), m_b_f_0)
    v_rows = small_rows(smalls_v, pad_cw(v_conv_w_0), v_b_f_0)

    g_sum = _sum_devices(parts)
    conv_full = jnp.concatenate([g_sum[6], g_sum[7, :3 * c_conv - d]]).reshape(3, c_conv)
    bf_grad = g_sum[7, 3 * c_conv - d:3 * c_conv - d + N_HEADS]
    loss = g_sum[7, 3 * c_conv - d + N_HEADS]
    g_rows = jnp.concatenate(
        [g_sum[0:6], conv_block(conv_full).reshape(1, d),
         jnp.pad(bf_grad, (0, d - N_HEADS)).reshape(1, d)], axis=0)
    d_rows, nm_rows, nv_rows = _adamw(w_rows, g_rows, m_rows, v_rows, "adamw_small")

    def unpack(rows):
        cw = rows[6, :conv_w_0.size].reshape(conv_w_0.shape)
        return [rows[0], rows[1], rows[2], rows[3], rows[4], rows[5], cw, rows[7, :N_HEADS]]

    def assemble(kind):
        sm = unpack([g_rows, d_rows, nm_rows, nv_rows][kind])
        lg = [t[kind] for t in big_out]
        return [sm[0], lg[0], sm[7], sm[6], lg[1], sm[1], lg[2], lg[3],
                sm[2], lg[4], sm[3], sm[4], lg[5], lg[6], sm[5]]

    return (loss, grad_x[None], *assemble(0), *assemble(1), *assemble(2), *assemble(3))
```

```python
import functools

import jax
import jax.numpy as jnp
from jax import lax
from jax.experimental import pallas as pl
from jax.experimental.pallas import tpu as pltpu

F32 = jnp.float32
BF16 = jnp.bfloat16

RMS_EPS = 1e-6
HEAD_DIM = 64
N_HEADS = 8
ATTN_SCALE = HEAD_DIM ** -0.5
POOL_WINDOWS = (2, 4, 8, 16)
POOL_HALO = 16
CONV_HALO = 8
NEG_BIG = -1e30

ADAM_LR = 0.001
ADAM_B1 = 0.9
ADAM_B2 = 0.999
ADAM_EPS = 1e-08
ADAM_WD = 0.01
ADAM_STEP = 10

N_CHIPS = 4
N_DEV = 8
MESH = pl.DeviceIdType.MESH

VMEM_LIMIT_BYTES = 56 * 1024 * 1024

TILE_ROWS = 512
TILE_ATTN = 512
TILE_MLP_ROWS = 1024
TILE_MLP_FF = 1024
TILE_MLP_BWD_FF = 512
TILE_WGRAD_K = 1024
TILE_WGRAD_N = 1024
TILE_ELEM_ROWS = 256

LANE_CQ = 64
LANE_ONE = 67


def _params(semantics):
    return pltpu.CompilerParams(dimension_semantics=semantics,
                                vmem_limit_bytes=VMEM_LIMIT_BYTES)


def _nn(a, b):
    return lax.dot_general(a, b, (((1,), (0,)), ((), ())), preferred_element_type=F32)


def _nt(a, b):
    return lax.dot_general(a, b, (((1,), (1,)), ((), ())), preferred_element_type=F32)


def _tn(a, b):
    return lax.dot_general(a, b, (((0,), (0,)), ((), ())), preferred_element_type=F32)


def _split3(v):
    hi = v.astype(BF16)
    r1 = v - hi.astype(F32)
    mid = r1.astype(BF16)
    lo = (r1 - mid.astype(F32)).astype(BF16)
    return hi, mid, lo


def _exact_nn(sel, v):
    hi, mid, lo = _split3(v)
    return _nn(sel, hi) + _nn(sel, mid) + _nn(sel, lo)


def _exact_nt(sel, v):
    hi, mid, lo = _split3(v)
    return _nt(sel, hi) + _nt(sel, mid) + _nt(sel, lo)


def _rms_fwd(x, g):
    r = lax.rsqrt(jnp.mean(x * x, axis=-1, keepdims=True) + RMS_EPS)
    return x * r * g, r


def _rms_bwd(dn, x, g):
    r = lax.rsqrt(jnp.mean(x * x, axis=-1, keepdims=True) + RMS_EPS)
    xh = x * r
    gy = dn * g
    dx = r * (gy - xh * jnp.mean(gy * xh, axis=-1, keepdims=True))
    return dx, jnp.sum(dn * xh, axis=0, keepdims=True)


def _lane(shape):
    return lax.broadcasted_iota(jnp.int32, shape, len(shape) - 1)


def _row(shape):
    return lax.broadcasted_iota(jnp.int32, shape, len(shape) - 2)


def _full(a):
    nd = a.ndim
    return pl.BlockSpec(a.shape, lambda *_: (0,) * nd)


def _ln_proj(x, g, w_qkv, w_f, w_bcx):
    s, d = x.shape
    tm = min(TILE_ROWS, s)

    def body(x_ref, g_ref, wq_ref, wf_ref, wb_ref, n_ref, qkv_ref, fl_ref, bcx_ref):
        n, _ = _rms_fwd(x_ref[...], g_ref[...])
        nb = n.astype(BF16)
        n_ref[...] = nb
        qkv_ref[...] = _nn(nb, wq_ref[...]).astype(BF16)
        fl_ref[...] = _nn(nb, wf_ref[...])
        bcx_ref[...] = _nn(nb, wb_ref[...])

    rows = lambda c: pl.BlockSpec((tm, c), lambda i: (i, 0))
    return pl.pallas_call(
        body, name="ln_proj", grid=(s // tm,),
        in_specs=[rows(d), _full(g), _full(w_qkv), _full(w_f), _full(w_bcx)],
        out_specs=[rows(d), rows(w_qkv.shape[1]), rows(w_f.shape[1]), rows(w_bcx.shape[1])],
        out_shape=[jax.ShapeDtypeStruct((s, d), BF16),
                   jax.ShapeDtypeStruct((s, w_qkv.shape[1]), BF16),
                   jax.ShapeDtypeStruct((s, w_f.shape[1]), F32),
                   jax.ShapeDtypeStruct((s, w_bcx.shape[1]), F32)],
        compiler_params=_params(("parallel",)),
    )(x, g, w_qkv, w_f, w_bcx)


def _gate_prep(fl, bf, qkv):
    s = fl.shape[0]
    a = N_HEADS * HEAD_DIM
    tm = min(TILE_ROWS, s)

    def body(fl_ref, bf_ref, q_ref, k_ref, qa_ref, ka_ref, carry_ref):
        i = pl.program_id(0)

        @pl.when(i == 0)
        def _():
            carry_ref[...] = jnp.zeros_like(carry_ref)

        z = fl_ref[...] + bf_ref[...]
        logf = jnp.minimum(z, 0.0) - jnp.log(1.0 + jnp.exp(-jnp.abs(z)))
        lower = (_lane((tm, tm)) <= _row((tm, tm))).astype(BF16)
        cum = _exact_nn(lower, logf) + carry_ref[0:1, :]
        carry_ref[0:1, :] = cum[tm - 1:tm, :]

        lane = _lane((tm, 128))
        for h in range(N_HEADS):
            cb = jnp.sum(jnp.where(lane == h, cum, 0.0), axis=1, keepdims=True)
            hi, mid, lo = (p.astype(F32) for p in _split3(cb))
            pair = slice((h // 2) * 128, (h // 2 + 1) * 128)
            qp = q_ref[:, pair].astype(F32)
            kp = k_ref[:, pair].astype(F32)
            if h % 2:
                qp = pltpu.roll(qp, HEAD_DIM, axis=1)
                kp = pltpu.roll(kp, HEAD_DIM, axis=1)
            q_bias = jnp.where(lane == LANE_CQ, hi,
                               jnp.where(lane == LANE_CQ + 1, mid,
                                         jnp.where(lane == LANE_CQ + 2, lo,
                                                   jnp.where(lane < LANE_ONE + 3, 1.0, 0.0))))
            k_bias = jnp.where(lane < LANE_ONE, 1.0,
                               jnp.where(lane == LANE_ONE, -hi,
                                         jnp.where(lane == LANE_ONE + 1, -mid,
                                                   jnp.where(lane == LANE_ONE + 2, -lo, 0.0))))
            qa_ref[h] = jnp.where(lane < HEAD_DIM, qp * ATTN_SCALE, q_bias).astype(BF16)
            ka_ref[h] = jnp.where(lane < HEAD_DIM, kp, k_bias).astype(BF16)

    aug = jax.ShapeDtypeStruct((N_HEADS, s, 128), BF16)
    aug_spec = pl.BlockSpec((N_HEADS, tm, 128), lambda i: (0, i, 0))
    return pl.pallas_call(
        body, name="gate_prep", grid=(s // tm,),
        in_specs=[pl.BlockSpec((tm, 128), lambda i: (i, 0)), _full(bf),
                  pl.BlockSpec((tm, a), lambda i: (i, 0)),
                  pl.BlockSpec((tm, a), lambda i: (i, 1))],
        out_specs=[aug_spec, aug_spec],
        out_shape=[aug, aug],
        scratch_shapes=[pltpu.VMEM((8, 128), F32)],
        compiler_params=_params(("arbitrary",)),
    )(fl, bf, qkv, qkv)


def _attn_fwd(qa, ka, qkv):
    s = qa.shape[1]
    a = N_HEADS * HEAD_DIM
    t = min(TILE_ATTN, s)
    n_pairs = N_HEADS // 2
    v_block0 = 2 * a // 128

    def body(qa_ref, ka_ref, v_ref, o_ref, lse_ref, m_ref, l_ref, acc_ref):
        i = pl.program_id(1)
        m_ref[...] = jnp.full_like(m_ref, NEG_BIG)
        l_ref[...] = jnp.zeros_like(l_ref)
        acc_ref[...] = jnp.zeros_like(acc_ref)
        upper_rows = _row((128, t)) < HEAD_DIM

        def kv_step(j, masked):
            ks = pl.ds(pl.multiple_of(j * t, t), t)
            vf = v_ref[ks, :].astype(F32)
            lane = _lane((t, 128))
            v_heads = [jnp.where(lane < HEAD_DIM, vf, 0.0).astype(BF16),
                       jnp.where(lane >= HEAD_DIM, vf, 0.0).astype(BF16)]
            alphas, update = [], None
            for e in range(2):
                sc = _nt(ka_ref[e, ks, :], qa_ref[e])
                if masked:
                    sc = jnp.where(_row((t, t)) <= _lane((t, t)), sc, NEG_BIG)
                m_prev = m_ref[e]
                m_new = jnp.maximum(m_prev, jnp.max(sc, axis=0, keepdims=True))
                p = jnp.exp(sc - m_new)
                alpha = jnp.exp(m_prev - m_new)
                l_ref[e] = alpha * l_ref[e] + jnp.sum(p, axis=0, keepdims=True)
                m_ref[e] = m_new
                alphas.append(alpha)
                pv = _tn(v_heads[e], p.astype(BF16))
                update = pv if update is None else update + pv
            acc_ref[...] = acc_ref[...] * jnp.where(upper_rows, alphas[0], alphas[1]) + update

        def full_step(j, carry):
            kv_step(j, False)
            return carry

        lax.fori_loop(0, i, full_step, 0)
        kv_step(i, True)

        out_t = acc_ref[...] / jnp.where(upper_rows, l_ref[0], l_ref[1])
        o_ref[...] = out_t.T.astype(BF16)
        lse = [m_ref[e] + jnp.log(l_ref[e]) for e in range(2)]
        lse_ref[...] = jnp.where(_row((8, t)) == 0, lse[0], lse[1])

    return pl.pallas_call(
        body, name="attn_fwd", grid=(n_pairs, s // t),
        in_specs=[pl.BlockSpec((2, t, 128), lambda g, i: (g, i, 0)),
                  pl.BlockSpec((2, s, 128), lambda g, i: (g, 0, 0)),
                  pl.BlockSpec((s, 128), lambda g, i: (0, v_block0 + g))],
        out_specs=[pl.BlockSpec((t, 128), lambda g, i: (i, g)),
                   pl.BlockSpec((None, 8, t), lambda g, i: (g, 0, i))],
        out_shape=[jax.ShapeDtypeStruct((s, a), BF16),
                   jax.ShapeDtypeStruct((n_pairs, 8, s), F32)],
        scratch_shapes=[pltpu.VMEM((2, 1, t), F32), pltpu.VMEM((2, 1, t), F32),
                        pltpu.VMEM((128, t), F32)],
        compiler_params=_params(("parallel", "arbitrary")),
    )(qa, ka, qkv)


def _conv_out(o, bcx, cw, w_out, x):
    s, d = x.shape
    c = o.shape[1]
    tm = min(TILE_ROWS, s)

    def body(o_ref, b_ref, c_ref, xin_ref, cw_ref, w_ref, x_ref, h_ref, ubuf):
        i = pl.program_id(0)

        @pl.when(i == 0)
        def _():
            ubuf[0:CONV_HALO, :] = jnp.zeros((CONV_HALO, c), F32)

        u = c_ref[...] * xin_ref[...]
        ubuf[CONV_HALO:CONV_HALO + tm, :] = u
        u1 = ubuf[CONV_HALO - 1:CONV_HALO - 1 + tm, :]
        u2 = ubuf[CONV_HALO - 2:CONV_HALO - 2 + tm, :]
        cv = (cw_ref[0:1, :] * u2 + cw_ref[1:2, :] * u1) + cw_ref[2:3, :] * u
        y = (b_ref[...] * cv).astype(BF16)
        mix = _nn(o_ref[...], w_ref[0:c, :]) + _nn(y, w_ref[c:2 * c, :])
        h_ref[...] = x_ref[...] + mix
        ubuf[0:CONV_HALO, :] = u[tm - CONV_HALO:tm, :]

    col = lambda k: pl.BlockSpec((tm, c), lambda i: (i, k))
    return pl.pallas_call(
        body, name="conv_out", grid=(s // tm,),
        in_specs=[col(0), col(0), col(1), col(2), _full(cw), _full(w_out),
                  pl.BlockSpec((tm, d), lambda i: (i, 0))],
        out_specs=pl.BlockSpec((tm, d), lambda i: (i, 0)),
        out_shape=jax.ShapeDtypeStruct((s, d), F32),
        scratch_shapes=[pltpu.VMEM((tm + CONV_HALO, c), F32)],
        compiler_params=_params(("arbitrary",)),
    )(o, bcx, bcx, bcx, cw, w_out, x)


def _mlp_fwd(h, g, w_up, w_down, name):
    s, d = h.shape
    ff = w_down.shape[0]
    slot_cols = w_up.shape[2]
    tm = min(TILE_MLP_ROWS, s)
    tf = min(TILE_MLP_FF, slot_cols)
    per_slot = slot_cols // tf
    nf = ff // tf

    def body(h_ref, g_ref, wu_ref, wd_ref, out_ref, a_ref, n_ref, nb_ref, acc_ref):
        f = pl.program_id(1)

        @pl.when(f == 0)
        def _():
            n, _ = _rms_fwd(h_ref[...], g_ref[...])
            nb = n.astype(BF16)
            nb_ref[...] = nb
            n_ref[...] = nb
            acc_ref[...] = jnp.zeros_like(acc_ref)

        pre = _nn(nb_ref[...], wu_ref[...])
        a_ref[...] = pre.astype(BF16)
        r = jnp.square(jnp.maximum(pre, 0.0)).astype(BF16)
        acc_ref[...] += _nn(r, wd_ref[...])

        @pl.when(f == nf - 1)
        def _():
            out_ref[...] = h_ref[...] + acc_ref[...]

    return pl.pallas_call(
        body, name=name, grid=(s // tm, nf),
        in_specs=[pl.BlockSpec((tm, d), lambda i, f: (i, 0)), _full(g),
                  pl.BlockSpec((None, d, tf), lambda i, f: (f // per_slot, 0, f % per_slot)),
                  pl.BlockSpec((tf, d), lambda i, f: (f, 0))],
        out_specs=[pl.BlockSpec((tm, d), lambda i, f: (i, 0)),
                   pl.BlockSpec((tm, tf), lambda i, f: (i, f)),
                   pl.BlockSpec((tm, d), lambda i, f: (i, 0))],
        out_shape=[jax.ShapeDtypeStruct((s, d), F32),
                   jax.ShapeDtypeStruct((s, ff), BF16),
                   jax.ShapeDtypeStruct((s, d), BF16)],
        scratch_shapes=[pltpu.VMEM((tm, d), BF16), pltpu.VMEM((tm, d), F32)],
        compiler_params=_params(("parallel", "arbitrary")),
    )(h, g, w_up, w_down)


def _window_sum_down(e, window):
    step = 1
    while step < window:
        e = e + pltpu.roll(e, step, axis=0)
        step *= 2
    return e


def _window_sum_up(e, window):
    n = e.shape[0]
    step = 1
    while step < window:
        e = e + pltpu.roll(e, n - step, axis=0)
        step *= 2
    return e


def _pool_counts(first_row, tm, window):
    t = first_row + _row((tm, 1))
    return jnp.minimum(t + 1, window).astype(F32)


def _pool_fwd(h, g, pw, ps):
    s, d = h.shape
    cg = d // len(POOL_WINDOWS)
    tm = min(TILE_ROWS, s)

    def body(h_ref, g_ref, pw_ref, ps_ref, out_ref, nbuf):
        i = pl.program_id(0)

        @pl.when(i == 0)
        def _():
            nbuf[0:POOL_HALO, :] = jnp.zeros((POOL_HALO, d), F32)

        n, _ = _rms_fwd(h_ref[...], g_ref[...])
        nbuf[POOL_HALO:POOL_HALO + tm, :] = n
        for k, window in enumerate(POOL_WINDOWS):
            cols = slice(k * cg, (k + 1) * cg)
            sums = _window_sum_down(nbuf[:, cols], window)[POOL_HALO:, :]
            pooled = sums / _pool_counts(i * tm, tm, window) - n[:, cols]
            y = _nn(pooled.astype(BF16), pw_ref[k]) * ps_ref[:, cols]
            out_ref[:, cols] = h_ref[:, cols] + y
        nbuf[0:POOL_HALO, :] = n[tm - POOL_HALO:tm, :]

    return pl.pallas_call(
        body, name="pool_fwd", grid=(s // tm,),
        in_specs=[pl.BlockSpec((tm, d), lambda i: (i, 0)), _full(g), _full(pw), _full(ps)],
        out_specs=pl.BlockSpec((tm, d), lambda i: (i, 0)),
        out_shape=jax.ShapeDtypeStruct((s, d), F32),
        scratch_shapes=[pltpu.VMEM((tm + POOL_HALO, d), F32)],
        compiler_params=_params(("arbitrary",)),
    )(h, g, pw, ps)


def _final_loss(h, g, target):
    s, d = h.shape
    tm = min(TILE_ROWS, s)

    def body(h_ref, g_ref, t_ref, dh_ref, loss_ref, dg_ref):
        i = pl.program_id(0)
        hv = h_ref[...]
        y, _ = _rms_fwd(hv, g_ref[...])
        err = y - t_ref[...]
        part = 0.5 * jnp.sum(jnp.mean(err * err, axis=-1, keepdims=True), axis=0, keepdims=True)
        dx, dg = _rms_bwd(err / d, hv, g_ref[...])
        dh_ref[...] = dx
        part = jnp.broadcast_to(part, loss_ref.shape)

        @pl.when(i == 0)
        def _():
            loss_ref[...] = part
            dg_ref[...] = dg

        @pl.when(i > 0)
        def _():
            loss_ref[...] += part
            dg_ref[...] += dg

    return pl.pallas_call(
        body, name="final_loss", grid=(s // tm,),
        in_specs=[pl.BlockSpec((tm, d), lambda i: (i, 0)), _full(g),
                  pl.BlockSpec((tm, d), lambda i: (i, 0))],
        out_specs=[pl.BlockSpec((tm, d), lambda i: (i, 0)),
                   pl.BlockSpec((1, 128), lambda i: (0, 0)),
                   pl.BlockSpec((1, d), lambda i: (0, 0))],
        out_shape=[jax.ShapeDtypeStruct((s, d), F32),
                   jax.ShapeDtypeStruct((1, 128), F32),
                   jax.ShapeDtypeStruct((1, d), F32)],
        compiler_params=_params(("arbitrary",)),
    )(h, g, target)


def _mlp_bwd_x(dz, a, w_up, w_down, h_in, g, name):
    s, d = dz.shape
    ff = w_down.shape[0]
    slot_cols = w_up.shape[2]
    tm = min(TILE_MLP_ROWS, s)
    tf = min(TILE_MLP_BWD_FF, slot_cols)
    per_slot = slot_cols // tf
    nf = ff // tf

    def body(dz_ref, a_ref, wu_ref, wd_ref, h_ref, g_ref, da_ref, dzb_ref, dh_ref, dg_ref,
             dzs_ref, acc_ref):
        i = pl.program_id(0)
        f = pl.program_id(1)

        @pl.when(f == 0)
        def _():
            dzb = dz_ref[...].astype(BF16)
            dzs_ref[...] = dzb
            dzb_ref[...] = dzb
            acc_ref[...] = jnp.zeros_like(acc_ref)

        dr = _nt(dzs_ref[...], wd_ref[...])
        da = (dr * (2.0 * jnp.maximum(a_ref[...].astype(F32), 0.0))).astype(BF16)
        da_ref[...] = da
        acc_ref[...] += _nt(da, wu_ref[...])

        @pl.when(f == nf - 1)
        def _():
            dx, dg = _rms_bwd(acc_ref[...], h_ref[...], g_ref[...])
            dh_ref[...] = dz_ref[...] + dx

            @pl.when(i == 0)
            def _():
                dg_ref[...] = dg

            @pl.when(i > 0)
            def _():
                dg_ref[...] += dg

    return pl.pallas_call(
        body, name=name, grid=(s // tm, nf),
        in_specs=[pl.BlockSpec((tm, d), lambda i, f: (i, 0)),
                  pl.BlockSpec((tm, tf), lambda i, f: (i, f)),
                  pl.BlockSpec((None, d, tf), lambda i, f: (f // per_slot, 0, f % per_slot)),
                  pl.BlockSpec((tf, d), lambda i, f: (f, 0)),
                  pl.BlockSpec((tm, d), lambda i, f: (i, 0)), _full(g)],
        out_specs=[pl.BlockSpec((tm, tf), lambda i, f: (i, f)),
                   pl.BlockSpec((tm, d), lambda i, f: (i, 0)),
                   pl.BlockSpec((tm, d), lambda i, f: (i, 0)),
                   pl.BlockSpec((1, d), lambda i, f: (0, 0))],
        out_shape=[jax.ShapeDtypeStruct((s, ff), BF16),
                   jax.ShapeDtypeStruct((s, d), BF16),
                   jax.ShapeDtypeStruct((s, d), F32),
                   jax.ShapeDtypeStruct((1, d), F32)],
        scratch_shapes=[pltpu.VMEM((tm, d), BF16), pltpu.VMEM((tm, d), F32)],
        compiler_params=_params(("arbitrary", "arbitrary")),
    )(dz, a, w_up, w_down, h_in, g)


def _mlp_bwd_w(n, da, a, dzb, slot_cols, name):
    s, d = n.shape
    ff = a.shape[1]
    tn = min(TILE_WGRAD_N, slot_cols)
    tk = min(TILE_WGRAD_K, s)
    per_slot = slot_cols // tn
    nk = s // tk

    def body(n_ref, da_ref, a_ref, dz_ref, du_ref, dd_ref, accu_ref, accd_ref):
        k = pl.program_id(1)

        @pl.when(k == 0)
        def _():
            accu_ref[...] = jnp.zeros_like(accu_ref)
            accd_ref[...] = jnp.zeros_like(accd_ref)

        accu_ref[...] += _tn(n_ref[...], da_ref[...])
        r = jnp.square(jnp.maximum(a_ref[...].astype(F32), 0.0)).astype(BF16)
        accd_ref[...] += _tn(r, dz_ref[...])

        @pl.when(k == nk - 1)
        def _():
            du_ref[...] = accu_ref[...].astype(BF16)
            dd_ref[...] = accd_ref[...].astype(BF16)

    return pl.pallas_call(
        body, name=name, grid=(ff // tn, nk),
        in_specs=[pl.BlockSpec((tk, d), lambda f, k: (k, 0)),
                  pl.BlockSpec((tk, tn), lambda f, k: (k, f)),
                  pl.BlockSpec((tk, tn), lambda f, k: (k, f)),
                  pl.BlockSpec((tk, d), lambda f, k: (k, 0))],
        out_specs=[pl.BlockSpec((None, d, tn), lambda f, k: (f // per_slot, 0, f % per_slot)),
                   pl.BlockSpec((tn, d), lambda f, k: (f, 0))],
        out_shape=[jax.ShapeDtypeStruct((ff // slot_cols, d, slot_cols), BF16),
                   jax.ShapeDtypeStruct((ff, d), BF16)],
        scratch_shapes=[pltpu.VMEM((d, tn), F32), pltpu.VMEM((tn, d), F32)],
        compiler_params=_params(("parallel", "arbitrary")),
    )(n, da, a, dzb)


def _pool_bwd(after, dh, h, g, pw, ps):
    s, d = h.shape
    cg = d // len(POOL_WINDOWS)
    tm = min(TILE_ROWS, s)
    nb = s // tm
    halo_per_tile = tm // POOL_HALO

    def body(after_ref, dh_ref, h_ref, halo_ref, g_ref, pw_ref, ps_ref,
             dx_ref, dpw_ref, dps_ref, dg_ref, nbuf, qbuf, dn_ref, carry, dpw_acc):
        i = pl.program_id(0)
        blk = nb - 1 - i

        @pl.when(i == 0)
        def _():
            carry[...] = jnp.zeros_like(carry)
            dpw_acc[...] = jnp.zeros_like(dpw_acc)
            dps_ref[...] = jnp.zeros_like(dps_ref)
            dg_ref[...] = jnp.zeros_like(dg_ref)

        hv = h_ref[...]
        n, _ = _rms_fwd(hv, g_ref[...])
        nh, _ = _rms_fwd(halo_ref[...], g_ref[...])
        nbuf[0:POOL_HALO, :] = jnp.where(blk == 0, 0.0, nh)
        nbuf[POOL_HALO:POOL_HALO + tm, :] = n
        dhv = dh_ref[...]
        for k, window in enumerate(POOL_WINDOWS):
            cols = slice(k * cg, (k + 1) * cg)
            cnt = _pool_counts(blk * tm, tm, window)
            sums = _window_sum_down(nbuf[:, cols], window)[POOL_HALO:, :]
            pb = (sums / cnt - n[:, cols]).astype(BF16)
            dyk = dhv[:, cols]
            dps_ref[:, cols] += jnp.sum(dyk * _nn(pb, pw_ref[k]), axis=0, keepdims=True)
            dyb = (dyk * ps_ref[:, cols]).astype(BF16)
            dpw_acc[k] += _tn(pb, dyb)
            dpool = _nt(dyb, pw_ref[k])
            qv = dpool / cnt
            qbuf[0:tm, cols] = qv
            qbuf[tm:tm + POOL_HALO, cols] = carry[:, cols]
            dn_ref[:, cols] = _window_sum_up(qbuf[:, cols], window)[0:tm, :] - dpool
            carry[:, cols] = qv[0:POOL_HALO, :]
        dx, dg = _rms_bwd(dn_ref[...], hv, g_ref[...])
        dx_ref[...] = dhv + dx
        dg_ref[...] += dg

        @pl.when(i == nb - 1)
        def _():
            dpw_ref[...] = dpw_acc[...].astype(BF16)

    rev = lambda i: (nb - 1 - i, 0)
    return pl.pallas_call(
        body, name="pool_bwd", grid=(nb,),
        in_specs=[ANY, pl.BlockSpec((tm, d), rev), pl.BlockSpec((tm, d), rev),
                  pl.BlockSpec((POOL_HALO, d),
                               lambda i: (jnp.maximum((nb - 1 - i) * halo_per_tile - 1, 0), 0)),
                  _full(g), _full(pw), _full(ps)],
        out_specs=[pl.BlockSpec((tm, d), rev), _full(pw),
                   pl.BlockSpec((1, d), lambda i: (0, 0)),
                   pl.BlockSpec((1, d), lambda i: (0, 0))],
        out_shape=[jax.ShapeDtypeStruct((s, d), F32),
                   jax.ShapeDtypeStruct(pw.shape, BF16),
                   jax.ShapeDtypeStruct((1, d), F32),
                   jax.ShapeDtypeStruct((1, d), F32)],
        scratch_shapes=[pltpu.VMEM((tm + POOL_HALO, d), F32), pltpu.VMEM((tm + POOL_HALO, d), F32),
                        pltpu.VMEM((tm, d), F32), pltpu.VMEM((POOL_HALO, d), F32),
                        pltpu.VMEM(pw.shape, F32)],
        compiler_params=_params(("arbitrary",)),
    )(after, dh, h, h, g, pw, ps)


def _conv_out_bwd(after, dh, w_out, o, bcx, cw):
    s, d = dh.shape
    c = o.shape[1]
    tm = min(TILE_ROWS, s)
    nb = s // tm
    halo_per_tile = tm // CONV_HALO

    def body(after_ref, dh_ref, w_ref, o_ref, b_ref, c_ref, xin_ref, ch_ref, xh_ref, cw_ref,
             do_ref, delta_ref, dbcx_ref, dw_ref, dcw_ref, ubuf, dbuf, carry, acc):
        i = pl.program_id(0)
        blk = nb - 1 - i

        @pl.when(i == 0)
        def _():
            carry[...] = jnp.zeros_like(carry)
            acc[...] = jnp.zeros_like(acc)
            dcw_ref[...] = jnp.zeros_like(dcw_ref)

        dm = dh_ref[...].astype(BF16)
        dcat = _nt(dm, w_ref[...])
        do = dcat[:, 0:c]
        dy = dcat[:, c:2 * c]
        do_ref[...] = do.astype(BF16)
        head_of_lane = lax.shift_right_logical(_lane((8, c)), HEAD_DIM.bit_length() - 1)
        heads = (head_of_lane == _row((8, c))).astype(BF16)
        delta_ref[...] = _exact_nt(heads, do * o_ref[...].astype(F32))

        cv_ = c_ref[...]
        xin = xin_ref[...]
        bv = b_ref[...]
        u = cv_ * xin
        ubuf[0:CONV_HALO, :] = jnp.where(blk == 0, 0.0, ch_ref[...] * xh_ref[...])
        ubuf[CONV_HALO:CONV_HALO + tm, :] = u
        u1 = ubuf[CONV_HALO - 1:CONV_HALO - 1 + tm, :]
        u2 = ubuf[CONV_HALO - 2:CONV_HALO - 2 + tm, :]
        w0, w1, w2 = cw_ref[0:1, :], cw_ref[1:2, :], cw_ref[2:3, :]
        cv = (w0 * u2 + w1 * u1) + w2 * u
        acc[0:c, :] += _tn(o_ref[...], dm)
        acc[c:2 * c, :] += _tn((bv * cv).astype(BF16), dm)

        dcv = dy * bv
        dcw_ref[0:1, :] += jnp.sum(dcv * u2, axis=0, keepdims=True)
        dcw_ref[1:2, :] += jnp.sum(dcv * u1, axis=0, keepdims=True)
        dcw_ref[2:3, :] += jnp.sum(dcv * u, axis=0, keepdims=True)
        dbuf[0:tm, :] = dcv
        dbuf[tm:tm + CONV_HALO, :] = carry[...]
        du = w2 * dcv + w1 * dbuf[1:1 + tm, :] + w0 * dbuf[2:2 + tm, :]
        dbcx_ref[:, 0:c] = (dy * cv).astype(BF16)
        dbcx_ref[:, c:2 * c] = (du * xin).astype(BF16)
        dbcx_ref[:, 2 * c:3 * c] = (du * cv_).astype(BF16)
        carry[...] = dcv[0:CONV_HALO, :]

        @pl.when(i == nb - 1)
        def _():
            dw_ref[...] = acc[...].astype(BF16)

    rev = lambda k: (lambda i: (nb - 1 - i, k))
    halo = lambda k: (lambda i: (jnp.maximum((nb - 1 - i) * halo_per_tile - 1, 0), k))
    return pl.pallas_call(
        body, name="conv_out_bwd", grid=(nb,),
        in_specs=[ANY, pl.BlockSpec((tm, d), rev(0)), _full(w_out), pl.BlockSpec((tm, c), rev(0)),
                  pl.BlockSpec((tm, c), rev(0)), pl.BlockSpec((tm, c), rev(1)),
                  pl.BlockSpec((tm, c), rev(2)),
                  pl.BlockSpec((CONV_HALO, c), halo(1)), pl.BlockSpec((CONV_HALO, c), halo(2)),
                  _full(cw)],
        out_specs=[pl.BlockSpec((tm, c), rev(0)),
                   pl.BlockSpec((8, tm), lambda i: (0, nb - 1 - i)),
                   pl.BlockSpec((tm, 3 * c), rev(0)),
                   _full(w_out), _full(cw)],
        out_shape=[jax.ShapeDtypeStruct((s, c), BF16),
                   jax.ShapeDtypeStruct((8, s), F32),
                   jax.ShapeDtypeStruct((s, 3 * c), BF16),
                   jax.ShapeDtypeStruct(w_out.shape, BF16),
                   jax.ShapeDtypeStruct(cw.shape, F32)],
        scratch_shapes=[pltpu.VMEM((tm + CONV_HALO, c), F32), pltpu.VMEM((tm + CONV_HALO, c), F32),
                        pltpu.VMEM((CONV_HALO, c), F32), pltpu.VMEM(w_out.shape, F32)],
        compiler_params=_params(("arbitrary",)),
    )(after, dh, w_out, o, bcx, bcx, bcx, bcx, bcx, cw)


def _attn_bwd(qa, ka, qkv, do, lse, delta):
    s = qa.shape[1]
    a = N_HEADS * HEAD_DIM
    t = min(TILE_ATTN, s)
    nq = s // t
    n_pairs = N_HEADS // 2
    v_block0 = 2 * a // 128

    def body(ka_ref, v_ref, qa_ref, do_ref, lse_ref, delta_ref,
             dqt_ref, dka_ref, dv_ref, dk_acc, dv_acc):
        g = pl.program_id(0)
        j = pl.program_id(1)

        @pl.when(j == 0)
        def _():
            dqt_ref[...] = jnp.zeros_like(dqt_ref)

        dk_acc[...] = jnp.zeros_like(dk_acc)
        dv_acc[...] = jnp.zeros_like(dv_acc)
        lane = _lane((t, 128))
        vf = v_ref[...].astype(F32)
        v_heads = [jnp.where(lane < HEAD_DIM, vf, 0.0).astype(BF16),
                   jnp.where(lane >= HEAD_DIM, vf, 0.0).astype(BF16)]
        ke_t = [ka_ref[e].astype(F32).T.astype(BF16) for e in range(2)]

        def q_step(i, masked):
            qs = pl.ds(pl.multiple_of(i * t, t), t)
            dob = do_ref[qs, :]
            for e in range(2):
                qe = qa_ref[e, qs, :]
                sc = _nt(ka_ref[e], qe)
                if masked:
                    sc = jnp.where(_row((t, t)) <= _lane((t, t)), sc, NEG_BIG)
                p = jnp.exp(sc - lse_ref[pl.ds(e, 1), qs])
                dv_acc[e] += _nn(p.astype(BF16), dob)
                dp = _nt(v_heads[e], dob)
                ds = (p * (dp - delta_ref[pl.ds(2 * g + e, 1), qs])).astype(BF16)
                dk_acc[e] += _nn(ds, qe)
                dqt_ref[e, :, qs] += _nn(ke_t[e], ds)

        q_step(j, True)

        def full_step(i, carry):
            q_step(i, False)
            return carry

        lax.fori_loop(j + 1, nq, full_step, 0)
        dka_ref[...] = dk_acc[...]
        dv_ref[...] = jnp.where(lane < HEAD_DIM, dv_acc[0], dv_acc[1]).astype(BF16)

    return pl.pallas_call(
        body, name="attn_bwd", grid=(n_pairs, nq),
        in_specs=[pl.BlockSpec((2, t, 128), lambda g, j: (g, j, 0)),
                  pl.BlockSpec((t, 128), lambda g, j: (j, v_block0 + g)),
                  pl.BlockSpec((2, s, 128), lambda g, j: (g, 0, 0)),
                  pl.BlockSpec((s, 128), lambda g, j: (0, g)),
                  pl.BlockSpec((None, 8, s), lambda g, j: (g, 0, 0)),
                  pl.BlockSpec((8, s), lambda g, j: (0, 0))],
        out_specs=[pl.BlockSpec((2, 128, s), lambda g, j: (g, 0, 0)),
                   pl.BlockSpec((2, t, 128), lambda g, j: (g, j, 0)),
                   pl.BlockSpec((t, 128), lambda g, j: (j, g))],
        out_shape=[jax.ShapeDtypeStruct((N_HEADS, 128, s), F32),
                   jax.ShapeDtypeStruct((N_HEADS, s, 128), F32),
                   jax.ShapeDtypeStruct((s, a), BF16)],
        scratch_shapes=[pltpu.VMEM((2, t, 128), F32), pltpu.VMEM((2, t, 128), F32)],
        compiler_params=_params(("parallel", "arbitrary")),
    )(ka, qkv, qa, do, lse, delta)


def _gate_bwd(dqa, dka, dv, fl, bf):
    s = fl.shape[0]
    a = N_HEADS * HEAD_DIM
    tm = min(TILE_ROWS, s)
    nb = s // tm

    def body(dqa_ref, dka_ref, dv_ref, fl_ref, bf_ref, dqkv_ref, dfl_ref, dbf_ref, carry):
        i = pl.program_id(0)

        @pl.when(i == 0)
        def _():
            carry[...] = jnp.zeros_like(carry)
            dbf_ref[...] = jnp.zeros_like(dbf_ref)

        lane = _lane((tm, 128))
        dcum = jnp.zeros((tm, 128), F32)
        for pair in range(N_HEADS // 2):
            qs, ks = [], []
            for e in range(2):
                h = 2 * pair + e
                dq = dqa_ref[h].T
                dk = dka_ref[h]
                dc = jnp.sum(jnp.where(lane == LANE_CQ, dq, 0.0)
                             - jnp.where(lane == LANE_ONE, dk, 0.0), axis=1, keepdims=True)
                dcum = jnp.where(lane == h, dc, dcum)
                qs.append(dq * ATTN_SCALE)
                ks.append(dk)
            cols = slice(pair * 128, (pair + 1) * 128)
            dqkv_ref[:, cols] = jnp.where(
                lane < HEAD_DIM, qs[0], pltpu.roll(qs[1], HEAD_DIM, axis=1)).astype(BF16)
            dqkv_ref[:, a + pair * 128:a + (pair + 1) * 128] = jnp.where(
                lane < HEAD_DIM, ks[0], pltpu.roll(ks[1], HEAD_DIM, axis=1)).astype(BF16)
        dqkv_ref[:, 2 * a:3 * a] = dv_ref[...]

        upper = (_lane((tm, tm)) >= _row((tm, tm))).astype(BF16)
        dlogf = _exact_nn(upper, dcum) + carry[0:1, :]
        carry[0:1, :] = dlogf[0:1, :]
        z = fl_ref[...] + bf_ref[...]
        ez = jnp.exp(-jnp.abs(z))
        sig_neg = jnp.where(z >= 0.0, ez, 1.0) / (1.0 + ez)
        dz = jnp.where(lane < N_HEADS, dlogf * sig_neg, 0.0)
        dfl_ref[...] = dz.astype(BF16)
        dbf_ref[...] += jnp.sum(dz, axis=0, keepdims=True)

    rev3 = lambda i: (0, nb - 1 - i, 0)
    rev = lambda i: (nb - 1 - i, 0)
    return pl.pallas_call(
        body, name="gate_bwd", grid=(nb,),
        in_specs=[pl.BlockSpec((N_HEADS, 128, tm), lambda i: (0, 0, nb - 1 - i)),
                  pl.BlockSpec((N_HEADS, tm, 128), rev3),
                  pl.BlockSpec((tm, a), rev), pl.BlockSpec((tm, 128), rev), _full(bf)],
        out_specs=[pl.BlockSpec((tm, 3 * a), rev), pl.BlockSpec((tm, 128), rev),
                   pl.BlockSpec((1, 128), lambda i: (0, 0))],
        out_shape=[jax.ShapeDtypeStruct((s, 3 * a), BF16),
                   jax.ShapeDtypeStruct((s, 128), BF16),
                   jax.ShapeDtypeStruct((1, 128), F32)],
        scratch_shapes=[pltpu.VMEM((8, 128), F32)],
        compiler_params=_params(("arbitrary",)),
    )(dqa, dka, dv, fl, bf)


def _in_proj_bwd(after, dqkv, dfl, dbcx, w_qkv, w_f, w_bcx, x, g, dh):
    s, d = x.shape
    tm = min(TILE_ROWS, s)

    def body(after_ref, dq_ref, df_ref, db_ref, wq_ref, wf_ref, wb_ref, x_ref, g_ref, dh_ref,
             gx_ref, dg_ref):
        i = pl.program_id(0)
        dn = (_nt(dq_ref[...], wq_ref[...]) + _nt(df_ref[...], wf_ref[...])
              + _nt(db_ref[...], wb_ref[...]))
        dx, dg = _rms_bwd(dn, x_ref[...], g_ref[...])
        gx_ref[...] = dh_ref[...] + dx

        @pl.when(i == 0)
        def _():
            dg_ref[...] = dg

        @pl.when(i > 0)
        def _():
            dg_ref[...] += dg

    rows = lambda c: pl.BlockSpec((tm, c), lambda i: (i, 0))
    return pl.pallas_call(
        body, name="in_proj_bwd", grid=(s // tm,),
        in_specs=[ANY, rows(dqkv.shape[1]), rows(dfl.shape[1]), rows(dbcx.shape[1]),
                  _full(w_qkv), _full(w_f), _full(w_bcx), rows(d), _full(g), rows(d)],
        out_specs=[rows(d), pl.BlockSpec((1, d), lambda i: (0, 0))],
        out_shape=[jax.ShapeDtypeStruct((s, d), F32), jax.ShapeDtypeStruct((1, d), F32)],
        compiler_params=_params(("arbitrary",)),
    )(after, dqkv, dfl, dbcx, w_qkv, w_f, w_bcx, x, g, dh)


def _wgrad_in(n, dys):
    s, d = n.shape
    m = len(dys)
    tk = min(TILE_ROWS, s)
    nk = s // tk

    def body(*refs):
        n_ref, dy_refs, dw_refs, accs = refs[0], refs[1:1 + m], refs[1 + m:1 + 2 * m], refs[1 + 2 * m:]
        k = pl.program_id(0)

        @pl.when(k == 0)
        def _():
            for acc in accs:
                acc[...] = jnp.zeros_like(acc)

        nb = n_ref[...]
        for dy_ref, acc in zip(dy_refs, accs):
            acc[...] += _tn(nb, dy_ref[...])

        @pl.when(k == nk - 1)
        def _():
            for dw_ref, acc in zip(dw_refs, accs):
                dw_ref[...] = acc[...].astype(BF16)

    return pl.pallas_call(
        body, name="wgrad_in", grid=(nk,),
        in_specs=[pl.BlockSpec((tk, d), lambda k: (k, 0))]
        + [pl.BlockSpec((tk, dy.shape[1]), lambda k: (k, 0)) for dy in dys],
        out_specs=[pl.BlockSpec((d, dy.shape[1]), lambda k: (0, 0)) for dy in dys],
        out_shape=[jax.ShapeDtypeStruct((d, dy.shape[1]), BF16) for dy in dys],
        scratch_shapes=[pltpu.VMEM((d, dy.shape[1]), F32) for dy in dys],
        compiler_params=_params(("arbitrary",)),
    )(n, *dys)


def _row_tile(rows):
    t = min(TILE_ELEM_ROWS, rows)
    while rows % t:
        t //= 2
    return t


def _sum_pair(grad, theirs, core, name):
    slots, rows, cols = theirs.shape
    tr = _row_tile(rows)
    nb = rows // tr

    def body(core_ref, a_ref, b_ref, o_ref):
        o_ref[...] = (a_ref[...].astype(F32) + b_ref[...].astype(F32)).astype(BF16)

    spec = pl.BlockSpec((None, tr, cols), lambda s, i, core_ref: (s, i, 0))
    return pl.pallas_call(
        body, name=name,
        grid_spec=pltpu.PrefetchScalarGridSpec(
            num_scalar_prefetch=1, grid=(slots, nb),
            in_specs=[pl.BlockSpec((None, tr, cols),
                                   lambda s, i, core_ref: (s, core_ref[0] * nb + i, 0)), spec],
            out_specs=spec),
        out_shape=jax.ShapeDtypeStruct(theirs.shape, BF16),
        compiler_params=_params(("parallel", "parallel")),
    )(core, grad, theirs)


def _sum_chips(sums, others, chip, name):
    _, rows, cols = sums.shape
    tr = _row_tile(rows)

    def body(chip_ref, a_ref, b_ref, o_ref):
        acc = a_ref[...].astype(F32)
        for k in range(N_CHIPS - 1):
            acc = acc + b_ref[k].astype(F32)
        o_ref[...] = acc

    return pl.pallas_call(
        body, name=name,
        grid_spec=pltpu.PrefetchScalarGridSpec(
            num_scalar_prefetch=1, grid=(rows // tr,),
            in_specs=[pl.BlockSpec((None, tr, cols), lambda i, chip_ref: (chip_ref[0], i, 0)),
                      pl.BlockSpec((N_CHIPS - 1, tr, cols), lambda i, chip_ref: (0, i, 0))],
            out_specs=pl.BlockSpec((tr, cols), lambda i, chip_ref: (i, 0))),
        out_shape=jax.ShapeDtypeStruct((rows, cols), F32),
        compiler_params=_params(("parallel",)),
    )(chip, sums, others)


def _adamw_math(w, g, m, v):
    m = ADAM_B1 * m + (1.0 - ADAM_B1) * g
    v = ADAM_B2 * v + (1.0 - ADAM_B2) * jnp.square(g)
    m_hat = m / (1.0 - ADAM_B1 ** ADAM_STEP)
    v_hat = v / (1.0 - ADAM_B2 ** ADAM_STEP)
    delta = -ADAM_LR * (m_hat / (jnp.sqrt(v_hat) + ADAM_EPS) + ADAM_WD * w)
    return delta, m, v


def _adamw(w, g, m, v, name):
    rows, cols = w.shape
    tr = _row_tile(rows)

    def body(w_ref, g_ref, m_ref, v_ref, d_ref, nm_ref, nv_ref):
        delta, nm, nv = _adamw_math(w_ref[...], g_ref[...], m_ref[...], v_ref[...])
        d_ref[...] = delta
        nm_ref[...] = nm
        nv_ref[...] = nv

    spec = pl.BlockSpec((tr, cols), lambda i: (i, 0))
    out = jax.ShapeDtypeStruct(w.shape, F32)
    return pl.pallas_call(
        body, name=name, grid=(rows // tr,), in_specs=[spec] * 4, out_specs=[spec] * 3,
        out_shape=[out, out, out], compiler_params=_params(("parallel",)),
    )(w, g, m, v)


def _sum_devices(parts):
    def body(p_ref, g_ref):
        g = p_ref[0]
        for k in range(1, N_DEV):
            g = g + p_ref[k]
        g_ref[...] = g

    return pl.pallas_call(
        body, name="sum_devices",
        in_specs=[pl.BlockSpec(memory_space=pltpu.VMEM)],
        out_specs=pl.BlockSpec(memory_space=pltpu.VMEM),
        out_shape=jax.ShapeDtypeStruct(parts.shape[1:], F32),
    )(parts)


def _mesh_position():
    x, y, c = lax.axis_index("x"), lax.axis_index("y"), lax.axis_index("c")
    chips = [(1 - x, y), (x, 1 - y), (1 - x, 1 - y)]
    return x, y, c, chips


ANY = pl.BlockSpec(memory_space=pl.ANY)
HBM = pl.BlockSpec(memory_space=pltpu.HBM)
SEM = pl.BlockSpec(memory_space=pltpu.SEMAPHORE)
SPLIT_COPY_EFFECT = pltpu.SideEffectType.DATAFLOW_SIDE_EFFECTING


def _in_hbm(a):
    return pltpu.with_memory_space_constraint(a, pltpu.HBM)


def _chip_copies(views, srcs, lands, send, recv):
    _, _, c, chips = _mesh_position()
    cps = []
    for a in range(len(srcs)):
        for k, (px, py) in enumerate(chips):
            src, dst = views(a, k, srcs[a], lands[a], c, 2 * px + py)
            sem = a * (N_CHIPS - 1) + k
            cps.append(pltpu.make_async_remote_copy(
                src_ref=src, dst_ref=dst, send_sem=send.at[sem], recv_sem=recv.at[sem],
                device_id=(px, py, c), device_id_type=MESH))
    return cps


def _ici_start(sources, land_shapes, views, after, name):
    n = len(sources)

    def body(*refs):
        srcs, lands = refs[:n], refs[n:2 * n]
        send, recv = refs[2 * n + 1], refs[2 * n + 2]
        token = refs[-1]
        for cp in _chip_copies(views, srcs, lands, send, recv):
            cp.start()
        token[...] = jnp.zeros_like(token)

    lands = [_in_hbm(lax.empty(s.shape, s.dtype)) for s in land_shapes]
    outs = pl.pallas_call(
        body, name=name,
        in_specs=[HBM] * (2 * n) + [ANY],
        out_specs=[SEM, SEM] + [HBM] * (2 * n) + [pl.BlockSpec(memory_space=pltpu.VMEM)],
        out_shape=[pltpu.SemaphoreType.DMA((n * (N_CHIPS - 1),))] * 2
        + [pltpu.HBM(a.shape, a.dtype) for a in sources]
        + [pltpu.HBM(s.shape, s.dtype) for s in land_shapes]
        + [jax.ShapeDtypeStruct((8, 128), F32)],
        input_output_aliases={i: 2 + i for i in range(2 * n)},
        compiler_params=pltpu.CompilerParams(has_side_effects=SPLIT_COPY_EFFECT),
    )(*[_in_hbm(a) for a in sources], *lands, after)
    return outs[0], outs[1], list(outs[2:2 + n]), list(outs[2 + n:2 + 2 * n]), outs[-1]


def _ici_wait(handle, views, after, name):
    send, recv, srcs, lands, _ = handle
    n = len(srcs)

    def body(*refs):
        src_refs, land_refs = refs[:n], refs[n:2 * n]
        for cp in _chip_copies(views, src_refs, land_refs, refs[2 * n], refs[2 * n + 1]):
            cp.wait_send()
            cp.wait_recv()

    outs = pl.pallas_call(
        body, name=name,
        in_specs=[HBM] * (2 * n) + [SEM, SEM, ANY],
        out_specs=[HBM] * (2 * n),
        out_shape=[pltpu.HBM(a.shape, a.dtype) for a in srcs]
        + [pltpu.HBM(a.shape, a.dtype) for a in lands],
        input_output_aliases={i: i for i in range(2 * n)},
        compiler_params=pltpu.CompilerParams(has_side_effects=SPLIT_COPY_EFFECT),
    )(*srcs, *lands, send, recv, after)
    return list(outs[:n]), list(outs[n:])


def _gather_views(split):
    def views(a, k, src, land, c, slot):
        if split[a]:
            half = src.shape[0] // 2
            src = src.at[pl.ds(c * half, half)]
        return src, land.at[k]
    return views


def _scatter_views(a, k, src, land, c, slot):
    return src.at[slot], land.at[k]


def _gather_land_shapes(shards, split):
    return [jax.ShapeDtypeStruct(
        (N_CHIPS - 1, a.shape[0] // 2 if sp else a.shape[0]) + a.shape[1:], a.dtype)
        for a, sp in zip(shards, split)]


def _gather_finish(shards, lands, split, name):
    n = len(shards)
    ns = sum(split)
    d_index = {a: i for i, a in enumerate(a for a in range(n) if split[a])}

    def body(*refs):
        shard, land, outs = refs[:n], refs[n:2 * n], refs[2 * n:3 * n]
        obuf, fbuf = refs[3 * n:4 * n], refs[4 * n:5 * n]
        dbuf = refs[5 * n:5 * n + ns]
        ld_own, st_own, ld, st_mine, st_sib, send, recv = refs[5 * n + ns:]
        x, y, c, chips = _mesh_position()
        me = 2 * x + y
        own_loads, loads, sends, pending = [], {}, [], []
        for a in range(n):
            cp = pltpu.make_async_copy(shard[a], obuf[a], ld_own.at[a])
            cp.start()
            own_loads.append(cp)
        for a in range(n):
            for k in range(N_CHIPS - 1):
                cp = pltpu.make_async_copy(land[a].at[k], fbuf[a].at[k], ld.at[a, k])
                cp.start()
                loads[a, k] = cp
        for a in range(n):
            own_loads[a].wait()
            cp = pltpu.make_async_copy(obuf[a], outs[a].at[me], st_own.at[a])
            cp.start()
            pending.append(cp)
        for a in range(n):
            rows = shard[a].shape[0]
            for k, (px, py) in enumerate(chips):
                loads[a, k].wait()
                part = pl.ds(c * (rows // 2), rows // 2) if split[a] else pl.ds(0, rows)
                cp = pltpu.make_async_copy(fbuf[a].at[k], outs[a].at[2 * px + py, part],
                                           st_mine.at[a, k])
                cp.start()
                pending.append(cp)
                if split[a]:
                    fw = pltpu.make_async_remote_copy(
                        src_ref=fbuf[a].at[k], dst_ref=dbuf[d_index[a]].at[k],
                        send_sem=send.at[a, k], recv_sem=recv.at[a, k],
                        device_id=(x, y, 1 - c), device_id_type=MESH)
                    fw.start()
                    sends.append((a, k, fw))
        for a, k, fw in sends:
            px, py = chips[k]
            half = shard[a].shape[0] // 2
            fw.wait_recv()
            cp = pltpu.make_async_copy(dbuf[d_index[a]].at[k],
                                       outs[a].at[2 * px + py, pl.ds((1 - c) * half, half)],
                                       st_sib.at[a, k])
            cp.start()
            pending.append(cp)
        for _, _, fw in sends:
            fw.wait_send()
        for cp in pending:
            cp.wait()

    stage = [pltpu.VMEM(a.shape, a.dtype) for a in lands]
    dma = lambda *shape: pltpu.SemaphoreType.DMA(shape)
    return pl.pallas_call(
        body, name=name,
        in_specs=[ANY] * (2 * n), out_specs=[ANY] * n,
        out_shape=[jax.ShapeDtypeStruct((N_CHIPS,) + a.shape, a.dtype) for a in shards],
        scratch_shapes=[pltpu.VMEM(a.shape, a.dtype) for a in shards] + stage
        + [s for s, sp in zip(stage, split) if sp]
        + [dma(n), dma(n), dma(n, 3), dma(n, 3), dma(n, 3), dma(n, 3), dma(n, 3)],
        compiler_params=pltpu.CompilerParams(vmem_limit_bytes=VMEM_LIMIT_BYTES),
    )(*shards, *lands)


def _exchange_siblings(grads, name):
    n = len(grads)

    def body(*refs):
        ins, theirs = refs[:n], refs[n:2 * n]
        sbuf, rbuf = refs[2 * n:3 * n], refs[3 * n:4 * n]
        ld, st, send, recv = refs[4 * n:]
        x, y, c, _ = _mesh_position()
        loads, sends, stores = [], [], []
        for a in range(n):
            half = ins[a].shape[1] // 2
            cp = pltpu.make_async_copy(ins[a].at[:, pl.ds((1 - c) * half, half)], sbuf[a], ld.at[a])
            cp.start()
            loads.append(cp)
        for a in range(n):
            loads[a].wait()
            rc = pltpu.make_async_remote_copy(
                src_ref=sbuf[a], dst_ref=rbuf[a], send_sem=send.at[a], recv_sem=recv.at[a],
                device_id=(x, y, 1 - c), device_id_type=MESH)
            rc.start()
            sends.append(rc)
        for a in range(n):
            sends[a].wait_recv()
            cp = pltpu.make_async_copy(rbuf[a], theirs[a], st.at[a])
            cp.start()
            stores.append(cp)
        for a in range(n):
            sends[a].wait_send()
            stores[a].wait()

    half_shape = lambda a: (a.shape[0], a.shape[1] // 2, a.shape[2])
    stage = [pltpu.VMEM(half_shape(a), a.dtype) for a in grads]
    return pl.pallas_call(
        body, name=name,
        in_specs=[ANY] * n, out_specs=[ANY] * n,
        out_shape=[jax.ShapeDtypeStruct(half_shape(a), a.dtype) for a in grads],
        scratch_shapes=stage + stage + [pltpu.SemaphoreType.DMA((n,))] * 4,
        compiler_params=pltpu.CompilerParams(vmem_limit_bytes=VMEM_LIMIT_BYTES),
    )(*grads)


def _share_halves(halves):
    n = len(halves)

    def body(*refs):
        ins, outs = refs[:n], refs[n:2 * n]
        sbuf, rbuf = refs[2 * n:3 * n], refs[3 * n:4 * n]
        ld, st_own, st_sib, send, recv = refs[4 * n:]
        x, y, c, _ = _mesh_position()
        loads, sends, stores = [], [], []
        for a in range(n):
            cp = pltpu.make_async_copy(ins[a], sbuf[a], ld.at[a])
            cp.start()
            loads.append(cp)
        for a in range(n):
            half = ins[a].shape[0]
            loads[a].wait()
            rc = pltpu.make_async_remote_copy(
                src_ref=sbuf[a], dst_ref=rbuf[a], send_sem=send.at[a], recv_sem=recv.at[a],
                device_id=(x, y, 1 - c), device_id_type=MESH)
            rc.start()
            sends.append(rc)
            cp = pltpu.make_async_copy(sbuf[a], outs[a].at[pl.ds(c * half, half)], st_own.at[a])
            cp.start()
            stores.append(cp)
        for a in range(n):
            half = ins[a].shape[0]
            sends[a].wait_recv()
            cp = pltpu.make_async_copy(rbuf[a], outs[a].at[pl.ds((1 - c) * half, half)], st_sib.at[a])
            cp.start()
            stores.append(cp)
        for cp in sends:
            cp.wait_send()
        for cp in stores:
            cp.wait()

    stage = [pltpu.VMEM(a.shape, a.dtype) for a in halves]
    return pl.pallas_call(
        body, name="share_halves",
        in_specs=[ANY] * n, out_specs=[ANY] * n,
        out_shape=[jax.ShapeDtypeStruct((2 * a.shape[0],) + a.shape[1:], a.dtype)
                   for a in halves],
        scratch_shapes=stage + stage + [pltpu.SemaphoreType.DMA((n,))] * 5,
        compiler_params=pltpu.CompilerParams(vmem_limit_bytes=VMEM_LIMIT_BYTES),
    )(*halves)


def _gather_small(part):
    def body(in_ref, out_ref, send, recv, local):
        x, y, c, _ = _mesh_position()
        me = 4 * x + 2 * y + c
        cps = [pltpu.make_async_copy(in_ref, out_ref.at[me], local)]
        k = 0
        for fx in range(2):
            for fy in range(2):
                for fc in range(2):
                    if fx or fy or fc:
                        cps.append(pltpu.make_async_remote_copy(
                            src_ref=in_ref, dst_ref=out_ref.at[me], send_sem=send.at[k],
                            recv_sem=recv.at[k], device_id=(x ^ fx, y ^ fy, c ^ fc),
                            device_id_type=MESH))
                        k += 1
        for cp in cps:
            cp.start()
        for cp in cps:
            cp.wait()

    return pl.pallas_call(
        body, name="gather_small",
        in_specs=[pl.BlockSpec(memory_space=pltpu.VMEM)],
        out_specs=pl.BlockSpec(memory_space=pltpu.VMEM),
        out_shape=jax.ShapeDtypeStruct((N_DEV,) + part.shape, part.dtype),
        scratch_shapes=[pltpu.SemaphoreType.DMA((N_DEV - 1,)), pltpu.SemaphoreType.DMA((N_DEV - 1,)),
                        pltpu.SemaphoreType.DMA],
    )(part)


def _scatter_start(grads, core, tag):
    theirs = _exchange_siblings(grads, "exchange_siblings_" + tag)
    sums = [_sum_pair(g, t, core, "sum_siblings_%s_%d" % (tag, i))
            for i, (g, t) in enumerate(zip(grads, theirs))]
    lands = [jax.ShapeDtypeStruct((N_CHIPS - 1,) + s.shape[1:], s.dtype) for s in sums]
    return _ici_start(sums, lands, _scatter_views, theirs[0], "scatter_start_" + tag)


def _scatter_finish(handle, chip, after, tag):
    sums, got = _ici_wait(handle, _scatter_views, after, "scatter_wait_" + tag)
    return [_sum_chips(s, g, chip, "sum_chips_%s_%d" % (tag, i))
            for i, (s, g) in enumerate(zip(sums, got))]


def _pad_rows(a, rows):
    return jnp.pad(a, ((0, rows - a.shape[0]), (0, 0)))


def kernel(x, norm_mix_0, w_in_0, b_f_0, conv_w_0, w_out_0, norm_ffn_0, w_up_0, w_down_0, norm_mix_1, pool_w_1, pool_scale_1, norm_ffn_1, w_up_1, w_down_1, final_norm, loss_target, m_norm_mix_0, m_w_in_0, m_b_f_0, m_conv_w_0, m_w_out_0, m_norm_ffn_0, m_w_up_0, m_w_down_0, m_norm_mix_1, m_pool_w_1, m_pool_scale_1, m_norm_ffn_1, m_w_up_1, m_w_down_1, m_final_norm, v_norm_mix_0, v_w_in_0, v_b_f_0, v_conv_w_0, v_w_out_0, v_norm_ffn_0, v_w_up_0, v_w_down_0, v_norm_mix_1, v_pool_w_1, v_pool_scale_1, v_norm_ffn_1, v_w_up_1, v_w_down_1, v_final_norm):
    d = x.shape[-1]
    a = N_HEADS * HEAD_DIM
    c_conv = conv_w_0.shape[1] * N_CHIPS
    xs = x[0]
    target = loss_target[0]
    row = lambda vec: vec.reshape(1, -1)

    big = [w_in_0, w_out_0, w_up_0, w_down_0, pool_w_1, w_up_1, w_down_1]
    first = [w_in_0.astype(BF16), w_out_0.astype(BF16), conv_w_0]
    first_split = [True, True, False]
    rest = [w.astype(BF16) for w in (w_up_0, w_down_0, pool_w_1, w_up_1, w_down_1)]
    rest_split = [True] * len(rest)
    start_a = _ici_start(first, _gather_land_shapes(first, first_split),
                         _gather_views(first_split), b_f_0, "gather_start_a")
    start_b = _ici_start(rest, _gather_land_shapes(rest, rest_split),
                         _gather_views(rest_split), start_a[-1], "gather_start_b")
    first, land_a = _ici_wait(start_a, _gather_views(first_split), start_b[-1], "gather_wait_a")
    g_in, g_out, g_conv = _gather_finish(first, land_a, first_split, "gather_finish_a")
    w_in = g_in.transpose(1, 0, 2).reshape(d, -1)
    w_qkv = w_in[:, :3 * a]
    w_f = jnp.pad(w_in[:, 3 * a:3 * a + N_HEADS], ((0, 0), (0, 128 - N_HEADS)))
    w_bcx = w_in[:, 3 * a + N_HEADS:]
    w_out = g_out.reshape(-1, d)
    conv_w = _pad_rows(g_conv.transpose(1, 0, 2).reshape(conv_w_0.shape[0], c_conv), 8)
    bf = jnp.pad(b_f_0, (0, 128 - N_HEADS)).reshape(1, 128)

    n0, qkv, fl, bcx = _ln_proj(xs, row(norm_mix_0), w_qkv, w_f, w_bcx)
    qa, ka = _gate_prep(fl, bf, qkv)
    o, lse = _attn_fwd(qa, ka, qkv)
    h1 = _conv_out(o, bcx, conv_w, w_out, xs)
    rest, land_b = _ici_wait(start_b, _gather_views(rest_split), h1, "gather_wait_b")
    g_up0, g_down0, g_pool, g_up1, g_down1 = _gather_finish(rest, land_b, rest_split,
                                                            "gather_finish_b")
    w_down0 = g_down0.reshape(-1, d)
    w_down1 = g_down1.reshape(-1, d)
    pool_w = g_pool.transpose(1, 0, 2, 3).reshape(pool_w_1.shape[0], -1, pool_w_1.shape[2])
    h2, a0, nf0 = _mlp_fwd(h1, row(norm_ffn_0), g_up0, w_down0, "mlp_fwd_0")
    h3 = _pool_fwd(h2, row(norm_mix_1), pool_w, row(pool_scale_1))
    h4, a1, nf1 = _mlp_fwd(h3, row(norm_ffn_1), g_up1, w_down1, "mlp_fwd_1")
    dh4, loss_part, d_final = _final_loss(h4, row(final_norm), target)

    slot_cols = g_up0.shape[2]
    pool_cols = pool_w.shape[2]
    core = lax.axis_index("c").astype(jnp.int32).reshape(1)
    chip_index = (2 * lax.axis_index("x") + lax.axis_index("y")).astype(jnp.int32).reshape(1)
    da1, dz1, dh3, d_nffn1 = _mlp_bwd_x(dh4, a1, g_up1, w_down1, h3, row(norm_ffn_1), "mlp_bwd_x_1")
    dw_up1, dw_down1 = _mlp_bwd_w(nf1, da1, a1, dz1, slot_cols, "mlp_bwd_w_1")
    scatter_1 = _scatter_start([dw_up1, dw_down1.reshape(N_CHIPS, -1, d)], core, "mlp1")
    dh2, dw_pool, d_pscale, d_nmix1 = _pool_bwd(scatter_1[-1], dh3, h2, row(norm_mix_1), pool_w,
                                                row(pool_scale_1))
    da0, dz0, dh1, d_nffn0 = _mlp_bwd_x(dh2, a0, g_up0, w_down0, h1, row(norm_ffn_0), "mlp_bwd_x_0")
    dw_up0, dw_down0 = _mlp_bwd_w(nf0, da0, a0, dz0, slot_cols, "mlp_bwd_w_0")
    dw_pool = (dw_pool.reshape(pool_w.shape[0], N_CHIPS, -1, pool_cols).transpose(1, 0, 2, 3)
               .reshape(N_CHIPS, -1, pool_cols))
    scatter_0 = _scatter_start([dw_up0, dw_down0.reshape(N_CHIPS, -1, d), dw_pool], core, "mlp0")
    do, delta, dbcx, dw_out, d_conv = _conv_out_bwd(scatter_0[-1], dh1, w_out, o, bcx, conv_w)
    dqa, dka, dv = _attn_bwd(qa, ka, qkv, do, lse, delta)
    dqkv, dfl, d_bf = _gate_bwd(dqa, dka, dv, fl, bf)
    dw_qkv, dw_f, dw_bcx = _wgrad_in(n0, [dqkv, dfl, dbcx])
    dw_in = jnp.concatenate([dw_qkv, dw_f[:, :N_HEADS], dw_bcx], axis=1)
    scatter_m = _scatter_start([dw_in.reshape(d, N_CHIPS, -1).transpose(1, 0, 2),
                                dw_out.reshape(N_CHIPS, -1, d)], core, "mixer")
    grad_x, d_nmix0 = _in_proj_bwd(scatter_m[-1], dqkv, dfl, dbcx, w_qkv, w_f, w_bcx, xs,
                                   row(norm_mix_0), dh1)

    h_up1, h_down1 = _scatter_finish(scatter_1, chip_index, grad_x, "mlp1")
    h_up0, h_down0, h_pool = _scatter_finish(scatter_0, chip_index, grad_x, "mlp0")
    h_in, h_out = _scatter_finish(scatter_m, chip_index, grad_x, "mixer")
    reduced = _share_halves([h_in, h_out, h_up0, h_down0, h_pool, h_up1, h_down1])
    moments = [(m_w_in_0, v_w_in_0), (m_w_out_0, v_w_out_0), (m_w_up_0, v_w_up_0),
               (m_w_down_0, v_w_down_0), (m_pool_w_1, v_pool_w_1), (m_w_up_1, v_w_up_1),
               (m_w_down_1, v_w_down_1)]
    big_out = []
    for k, (w, g, (m, v)) in enumerate(zip(big, reduced, moments)):
        flat = lambda t: t.reshape(-1, t.shape[-1])
        delta_w, new_m, new_v = _adamw(flat(w), flat(g), flat(m), flat(v), "adamw_%d" % k)
        big_out.append((g.reshape(w.shape), delta_w.reshape(w.shape), new_m.reshape(w.shape),
                        new_v.reshape(w.shape)))

    tail = jnp.concatenate([d_conv[0:3].reshape(-1)[d:], d_bf[0, :N_HEADS], loss_part[0, :1]])
    small_part = jnp.concatenate(
        [d_nmix0, d_nffn0, d_nmix1, d_pscale, d_nffn1, d_final,
         d_conv[0:3].reshape(1, -1)[:, :d],
         jnp.pad(tail, (0, d - tail.shape[0])).reshape(1, d)], axis=0)
    parts = _gather_small(small_part)

    chip = 2 * lax.axis_index("x") + lax.axis_index("y")
    cw_cols = conv_w_0.shape[1]

    def conv_block(full):
        mine = lax.dynamic_slice_in_dim(full, chip * cw_cols, cw_cols, axis=1)
        return jnp.pad(mine.reshape(-1), (0, d - mine.size))

    def small_rows(vals, cw, bfv):
        return jnp.stack(list(vals) + [cw, jnp.pad(bfv, (0, d - N_HEADS))])

    smalls_w = [norm_mix_0, norm_ffn_0, norm_mix_1, pool_scale_1, norm_ffn_1, final_norm]
    smalls_m = [m_norm_mix_0, m_norm_ffn_0, m_norm_mix_1, m_pool_scale_1, m_norm_ffn_1, m_final_norm]
    smalls_v = [v_norm_mix_0, v_norm_ffn_0, v_norm_mix_1, v_pool_scale_1, v_norm_ffn_1, v_final_norm]
    pad_cw = lambda t: jnp.pad(t.reshape(-1), (0, d - t.size))
    w_rows = small_rows(smalls_w, pad_cw(conv_w_0), b_f_0)
    m_rows = small_rows(smalls_m, pad_cw(m_conv_w_0), m_b_f_0)
    v_rows = small_rows(smalls_v, pad_cw(v_conv_w_0), v_b_f_0)

    g_sum = _sum_devices(parts)
    conv_full = jnp.concatenate([g_sum[6], g_sum[7, :3 * c_conv - d]]).reshape(3, c_conv)
    bf_grad = g_sum[7, 3 * c_conv - d:3 * c_conv - d + N_HEADS]
    loss = g_sum[7, 3 * c_conv - d + N_HEADS]
    g_rows = jnp.concatenate(
        [g_sum[0:6], conv_block(conv_full).reshape(1, d),
         jnp.pad(bf_grad, (0, d - N_HEADS)).reshape(1, d)], axis=0)
    d_rows, nm_rows, nv_rows = _adamw(w_rows, g_rows, m_rows, v_rows, "adamw_small")

    def unpack(rows):
        cw = rows[6, :conv_w_0.size].reshape(conv_w_0.shape)
        return [rows[0], rows[1], rows[2], rows[3], rows[4], rows[5], cw, rows[7, :N_HEADS]]

    def assemble(kind):
        sm = unpack([g_rows, d_rows, nm_rows, nv_rows][kind])
        lg = [t[kind] for t in big_out]
        return [sm[0], lg[0], sm[7], sm[6], lg[1], sm[1], lg[2], lg[3],
                sm[2], lg[4], sm[3], sm[4], lg[5], lg[6], sm[5]]

    return (loss, grad_x[None], *assemble(0), *assemble(1), *assemble(2), *assemble(3))
```

```python
import functools

import jax
import jax.numpy as jnp
from jax import lax
from jax.experimental import pallas as pl
from jax.experimental.pallas import tpu as pltpu

F32 = jnp.float32
BF16 = jnp.bfloat16

RMS_EPS = 1e-6
HEAD_DIM = 64
N_HEADS = 8
ATTN_SCALE = HEAD_DIM ** -0.5
POOL_WINDOWS = (2, 4, 8, 16)
POOL_HALO = 16
CONV_HALO = 8
NEG_BIG = -1e30

ADAM_LR = 0.001
ADAM_B1 = 0.9
ADAM_B2 = 0.999
ADAM_EPS = 1e-08
ADAM_WD = 0.01
ADAM_STEP = 10

N_CHIPS = 4
N_DEV = 8
MESH = pl.DeviceIdType.MESH

VMEM_LIMIT_BYTES = 56 * 1024 * 1024

TILE_ROWS = 512
TILE_ATTN = 512
TILE_MLP_ROWS = 1024
TILE_MLP_FF = 1024
TILE_MLP_BWD_FF = 512
TILE_WGRAD_K = 1024
TILE_WGRAD_N = 1024
TILE_ELEM_ROWS = 256

LANE_CQ = 64
LANE_ONE = 67


def _params(semantics):
    return pltpu.CompilerParams(dimension_semantics=semantics,
                                vmem_limit_bytes=VMEM_LIMIT_BYTES)


def _nn(a, b):
    return lax.dot_general(a, b, (((1,), (0,)), ((), ())), preferred_element_type=F32)


def _nt(a, b):
    return lax.dot_general(a, b, (((1,), (1,)), ((), ())), preferred_element_type=F32)


def _tn(a, b):
    return lax.dot_general(a, b, (((0,), (0,)), ((), ())), preferred_element_type=F32)


def _split3(v):
    hi = v.astype(BF16)
    r1 = v - hi.astype(F32)
    mid = r1.astype(BF16)
    lo = (r1 - mid.astype(F32)).astype(BF16)
    return hi, mid, lo


def _exact_nn(sel, v):
    hi, mid, lo = _split3(v)
    return _nn(sel, hi) + _nn(sel, mid) + _nn(sel, lo)


def _exact_nt(sel, v):
    hi, mid, lo = _split3(v)
    return _nt(sel, hi) + _nt(sel, mid) + _nt(sel, lo)


def _rms_fwd(x, g):
    r = lax.rsqrt(jnp.mean(x * x, axis=-1, keepdims=True) + RMS_EPS)
    return x * r * g, r


def _rms_bwd(dn, x, g):
    r = lax.rsqrt(jnp.mean(x * x, axis=-1, keepdims=True) + RMS_EPS)
    xh = x * r
    gy = dn * g
    dx = r * (gy - xh * jnp.mean(gy * xh, axis=-1, keepdims=True))
    return dx, jnp.sum(dn * xh, axis=0, keepdims=True)


def _lane(shape):
    return lax.broadcasted_iota(jnp.int32, shape, len(shape) - 1)


def _row(shape):
    return lax.broadcasted_iota(jnp.int32, shape, len(shape) - 2)


def _full(a):
    nd = a.ndim
    return pl.BlockSpec(a.shape, lambda *_: (0,) * nd)


def _ln_proj(x, g, w_qkv, w_f, w_bcx):
    s, d = x.shape
    tm = min(TILE_ROWS, s)

    def body(x_ref, g_ref, wq_ref, wf_ref, wb_ref, n_ref, qkv_ref, fl_ref, bcx_ref):
        n, _ = _rms_fwd(x_ref[...], g_ref[...])
        nb = n.astype(BF16)
        n_ref[...] = nb
        qkv_ref[...] = _nn(nb, wq_ref[...]).astype(BF16)
        fl_ref[...] = _nn(nb, wf_ref[...])
        bcx_ref[...] = _nn(nb, wb_ref[...])

    rows = lambda c: pl.BlockSpec((tm, c), lambda i: (i, 0))
    return pl.pallas_call(
        body, name="ln_proj", grid=(s // tm,),
        in_specs=[rows(d), _full(g), _full(w_qkv), _full(w_f), _full(w_bcx)],
        out_specs=[rows(d), rows(w_qkv.shape[1]), rows(w_f.shape[1]), rows(w_bcx.shape[1])],
        out_shape=[jax.ShapeDtypeStruct((s, d), BF16),
                   jax.ShapeDtypeStruct((s, w_qkv.shape[1]), BF16),
                   jax.ShapeDtypeStruct((s, w_f.shape[1]), F32),
                   jax.ShapeDtypeStruct((s, w_bcx.shape[1]), F32)],
        compiler_params=_params(("parallel",)),
    )(x, g, w_qkv, w_f, w_bcx)


def _gate_prep(fl, bf, qkv):
    s = fl.shape[0]
    a = N_HEADS * HEAD_DIM
    tm = min(TILE_ROWS, s)

    def body(fl_ref, bf_ref, q_ref, k_ref, qa_ref, ka_ref, carry_ref):
        i = pl.program_id(0)

        @pl.when(i == 0)
        def _():
            carry_ref[...] = jnp.zeros_like(carry_ref)

        z = fl_ref[...] + bf_ref[...]
        logf = jnp.minimum(z, 0.0) - jnp.log(1.0 + jnp.exp(-jnp.abs(z)))
        lower = (_lane((tm, tm)) <= _row((tm, tm))).astype(BF16)
        cum = _exact_nn(lower, logf) + carry_ref[0:1, :]
        carry_ref[0:1, :] = cum[tm - 1:tm, :]

        lane = _lane((tm, 128))
        for h in range(N_HEADS):
            cb = jnp.sum(jnp.where(lane == h, cum, 0.0), axis=1, keepdims=True)
            hi, mid, lo = (p.astype(F32) for p in _split3(cb))
            pair = slice((h // 2) * 128, (h // 2 + 1) * 128)
            qp = q_ref[:, pair].astype(F32)
            kp = k_ref[:, pair].astype(F32)
            if h % 2:
                qp = pltpu.roll(qp, HEAD_DIM, axis=1)
                kp = pltpu.roll(kp, HEAD_DIM, axis=1)
            q_bias = jnp.where(lane == LANE_CQ, hi,
                               jnp.where(lane == LANE_CQ + 1, mid,
                                         jnp.where(lane == LANE_CQ + 2, lo,
                                                   jnp.where(lane < LANE_ONE + 3, 1.0, 0.0))))
            k_bias = jnp.where(lane < LANE_ONE, 1.0,
                               jnp.where(lane == LANE_ONE, -hi,
                                         jnp.where(lane == LANE_ONE + 1, -mid,
                                                   jnp.where(lane == LANE_ONE + 2, -lo, 0.0))))
            qa_ref[h] = jnp.where(lane < HEAD_DIM, qp * ATTN_SCALE, q_bias).astype(BF16)
            ka_ref[h] = jnp.where(lane < HEAD_DIM, kp, k_bias).astype(BF16)

    aug = jax.ShapeDtypeStruct((N_HEADS, s, 128), BF16)
    aug_spec = pl.BlockSpec((N_HEADS, tm, 128), lambda i: (0, i, 0))
    return pl.pallas_call(
        body, name="gate_prep", grid=(s // tm,),
        in_specs=[pl.BlockSpec((tm, 128), lambda i: (i, 0)), _full(bf),
                  pl.BlockSpec((tm, a), lambda i: (i, 0)),
                  pl.BlockSpec((tm, a), lambda i: (i, 1))],
        out_specs=[aug_spec, aug_spec],
        out_shape=[aug, aug],
        scratch_shapes=[pltpu.VMEM((8, 128), F32)],
        compiler_params=_params(("arbitrary",)),
    )(fl, bf, qkv, qkv)


def _attn_fwd(qa, ka, qkv):
    s = qa.shape[1]
    a = N_HEADS * HEAD_DIM
    t = min(TILE_ATTN, s)
    n_pairs = N_HEADS // 2
    v_block0 = 2 * a // 128

    def body(qa_ref, ka_ref, v_ref, o_ref, lse_ref, m_ref, l_ref, acc_ref, s_even, s_odd):
        i = pl.program_id(1)
        m_ref[...] = jnp.full_like(m_ref, NEG_BIG)
        l_ref[...] = jnp.zeros_like(l_ref)
        acc_ref[...] = jnp.zeros_like(acc_ref)
        upper_rows = _row((128, t)) < HEAD_DIM

        def keys(j):
            return pl.ds(pl.multiple_of(j * t, t), t)

        def scores_into(buf, j):
            for e in range(2):
                buf[e] = _nt(ka_ref[e, keys(j), :], qa_ref[e])

        def consume(buf, j, masked):
            vf = v_ref[keys(j), :].astype(F32)
            lane = _lane((t, 128))
            v_heads = [jnp.where(lane < HEAD_DIM, vf, 0.0).astype(BF16),
                       jnp.where(lane >= HEAD_DIM, vf, 0.0).astype(BF16)]
            alphas, update = [], None
            for e in range(2):
                sc = buf[e]
                if masked:
                    sc = jnp.where(_row((t, t)) <= _lane((t, t)), sc, NEG_BIG)
                m_prev = m_ref[e]
                m_new = jnp.maximum(m_prev, jnp.max(sc, axis=0, keepdims=True))
                p = jnp.exp(sc - m_new)
                alpha = jnp.exp(m_prev - m_new)
                l_ref[e] = alpha * l_ref[e] + jnp.sum(p, axis=0, keepdims=True)
                m_ref[e] = m_new
                alphas.append(alpha)
                pv = _tn(v_heads[e], p.astype(BF16))
                update = pv if update is None else update + pv
            acc_ref[...] = acc_ref[...] * jnp.where(upper_rows, alphas[0], alphas[1]) + update

        scores_into(s_even, 0)

        def two_tiles(p, carry):
            j = 2 * p
            scores_into(s_odd, j + 1)
            consume(s_even, j, False)
            scores_into(s_even, j + 2)
            consume(s_odd, j + 1, False)
            return carry

        lax.fori_loop(0, i // 2, two_tiles, 0)

        @pl.when(i % 2 == 0)
        def _():
            consume(s_even, i, True)

        @pl.when(i % 2 == 1)
        def _():
            scores_into(s_odd, i)
            consume(s_even, i - 1, False)
            consume(s_odd, i, True)

        out_t = acc_ref[...] / jnp.where(upper_rows, l_ref[0], l_ref[1])
        o_ref[...] = out_t.T.astype(BF16)
        lse = [m_ref[e] + jnp.log(l_ref[e]) for e in range(2)]
        lse_ref[...] = jnp.where(_row((8, t)) == 0, lse[0], lse[1])

    return pl.pallas_call(
        body, name="attn_fwd", grid=(n_pairs, s // t),
        in_specs=[pl.BlockSpec((2, t, 128), lambda g, i: (g, i, 0)),
                  pl.BlockSpec((2, s, 128), lambda g, i: (g, 0, 0)),
                  pl.BlockSpec((s, 128), lambda g, i: (0, v_block0 + g))],
        out_specs=[pl.BlockSpec((t, 128), lambda g, i: (i, g)),
                   pl.BlockSpec((None, 8, t), lambda g, i: (g, 0, i))],
        out_shape=[jax.ShapeDtypeStruct((s, a), BF16),
                   jax.ShapeDtypeStruct((n_pairs, 8, s), F32)],
        scratch_shapes=[pltpu.VMEM((2, 1, t), F32), pltpu.VMEM((2, 1, t), F32),
                        pltpu.VMEM((128, t), F32), pltpu.VMEM((2, t, t), F32),
                        pltpu.VMEM((2, t, t), F32)],
        compiler_params=_params(("parallel", "arbitrary")),
    )(qa, ka, qkv)


def _conv_out(o, bcx, cw, w_out, x):
    s, d = x.shape
    c = o.shape[1]
    tm = min(TILE_ROWS, s)

    def body(o_ref, b_ref, c_ref, xin_ref, cw_ref, w_ref, x_ref, h_ref, ubuf):
        i = pl.program_id(0)

        @pl.when(i == 0)
        def _():
            ubuf[0:CONV_HALO, :] = jnp.zeros((CONV_HALO, c), F32)

        u = c_ref[...] * xin_ref[...]
        ubuf[CONV_HALO:CONV_HALO + tm, :] = u
        u1 = ubuf[CONV_HALO - 1:CONV_HALO - 1 + tm, :]
        u2 = ubuf[CONV_HALO - 2:CONV_HALO - 2 + tm, :]
        cv = (cw_ref[0:1, :] * u2 + cw_ref[1:2, :] * u1) + cw_ref[2:3, :] * u
        y = (b_ref[...] * cv).astype(BF16)
        mix = _nn(o_ref[...], w_ref[0:c, :]) + _nn(y, w_ref[c:2 * c, :])
        h_ref[...] = x_ref[...] + mix
        ubuf[0:CONV_HALO, :] = u[tm - CONV_HALO:tm, :]

    col = lambda k: pl.BlockSpec((tm, c), lambda i: (i, k))
    return pl.pallas_call(
        body, name="conv_out", grid=(s // tm,),
        in_specs=[col(0), col(0), col(1), col(2), _full(cw), _full(w_out),
                  pl.BlockSpec((tm, d), lambda i: (i, 0))],
        out_specs=pl.BlockSpec((tm, d), lambda i: (i, 0)),
        out_shape=jax.ShapeDtypeStruct((s, d), F32),
        scratch_shapes=[pltpu.VMEM((tm + CONV_HALO, c), F32)],
        compiler_params=_params(("arbitrary",)),
    )(o, bcx, bcx, bcx, cw, w_out, x)


def _mlp_fwd(h, g, w_up, w_down, name):
    s, d = h.shape
    ff = w_down.shape[0]
    slot_cols = w_up.shape[2]
    tm = min(TILE_MLP_ROWS, s)
    tf = min(TILE_MLP_FF, slot_cols)
    per_slot = slot_cols // tf
    nf = ff // tf

    def body(h_ref, g_ref, wu_ref, wd_ref, out_ref, a_ref, n_ref, nb_ref, acc_ref):
        f = pl.program_id(1)

        @pl.when(f == 0)
        def _():
            n, _ = _rms_fwd(h_ref[...], g_ref[...])
            nb = n.astype(BF16)
            nb_ref[...] = nb
            n_ref[...] = nb
            acc_ref[...] = jnp.zeros_like(acc_ref)

        pre = _nn(nb_ref[...], wu_ref[...])
        a_ref[...] = pre.astype(BF16)
        r = jnp.square(jnp.maximum(pre, 0.0)).astype(BF16)
        acc_ref[...] += _nn(r, wd_ref[...])

        @pl.when(f == nf - 1)
        def _():
            out_ref[...] = h_ref[...] + acc_ref[...]

    return pl.pallas_call(
        body, name=name, grid=(s // tm, nf),
        in_specs=[pl.BlockSpec((tm, d), lambda i, f: (i, 0)), _full(g),
                  pl.BlockSpec((None, d, tf), lambda i, f: (f // per_slot, 0, f % per_slot)),
                  pl.BlockSpec((tf, d), lambda i, f: (f, 0))],
        out_specs=[pl.BlockSpec((tm, d), lambda i, f: (i, 0)),
                   pl.BlockSpec((tm, tf), lambda i, f: (i, f)),
                   pl.BlockSpec((tm, d), lambda i, f: (i, 0))],
        out_shape=[jax.ShapeDtypeStruct((s, d), F32),
                   jax.ShapeDtypeStruct((s, ff), BF16),
                   jax.ShapeDtypeStruct((s, d), BF16)],
        scratch_shapes=[pltpu.VMEM((tm, d), BF16), pltpu.VMEM((tm, d), F32)],
        compiler_params=_params(("parallel", "arbitrary")),
    )(h, g, w_up, w_down)


def _window_sum_down(e, window):
    step = 1
    while step < window:
        e = e + pltpu.roll(e, step, axis=0)
        step *= 2
    return e


def _window_sum_up(e, window):
    n = e.shape[0]
    step = 1
    while step < window:
        e = e + pltpu.roll(e, n - step, axis=0)
        step *= 2
    return e


def _pool_counts(first_row, tm, window):
    t = first_row + _row((tm, 1))
    return jnp.minimum(t + 1, window).astype(F32)


def _pool_fwd(h, g, pw, ps):
    s, d = h.shape
    cg = d // len(POOL_WINDOWS)
    tm = min(TILE_ROWS, s)

    def body(h_ref, g_ref, pw_ref, ps_ref, out_ref, nbuf):
        i = pl.program_id(0)

        @pl.when(i == 0)
        def _():
            nbuf[0:POOL_HALO, :] = jnp.zeros((POOL_HALO, d), F32)

        n, _ = _rms_fwd(h_ref[...], g_ref[...])
        nbuf[POOL_HALO:POOL_HALO + tm, :] = n
        for k, window in enumerate(POOL_WINDOWS):
            cols = slice(k * cg, (k + 1) * cg)
            sums = _window_sum_down(nbuf[:, cols], window)[POOL_HALO:, :]
            pooled = sums / _pool_counts(i * tm, tm, window) - n[:, cols]
            y = _nn(pooled.astype(BF16), pw_ref[k]) * ps_ref[:, cols]
            out_ref[:, cols] = h_ref[:, cols] + y
        nbuf[0:POOL_HALO, :] = n[tm - POOL_HALO:tm, :]

    return pl.pallas_call(
        body, name="pool_fwd", grid=(s // tm,),
        in_specs=[pl.BlockSpec((tm, d), lambda i: (i, 0)), _full(g), _full(pw), _full(ps)],
        out_specs=pl.BlockSpec((tm, d), lambda i: (i, 0)),
        out_shape=jax.ShapeDtypeStruct((s, d), F32),
        scratch_shapes=[pltpu.VMEM((tm + POOL_HALO, d), F32)],
        compiler_params=_params(("arbitrary",)),
    )(h, g, pw, ps)


def _final_loss(h, g, target):
    s, d = h.shape
    tm = min(TILE_ROWS, s)

    def body(h_ref, g_ref, t_ref, dh_ref, loss_ref, dg_ref):
        i = pl.program_id(0)
        hv = h_ref[...]
        y, _ = _rms_fwd(hv, g_ref[...])
        err = y - t_ref[...]
        part = 0.5 * jnp.sum(jnp.mean(err * err, axis=-1, keepdims=True), axis=0, keepdims=True)
        dx, dg = _rms_bwd(err / d, hv, g_ref[...])
        dh_ref[...] = dx
        part = jnp.broadcast_to(part, loss_ref.shape)

        @pl.when(i == 0)
        def _():
            loss_ref[...] = part
            dg_ref[...] = dg

        @pl.when(i > 0)
        def _():
            loss_ref[...] += part
            dg_ref[...] += dg

    return pl.pallas_call(
        body, name="final_loss", grid=(s // tm,),
        in_specs=[pl.BlockSpec((tm, d), lambda i: (i, 0)), _full(g),
                  pl.BlockSpec((tm, d), lambda i: (i, 0))],
        out_specs=[pl.BlockSpec((tm, d), lambda i: (i, 0)),
                   pl.BlockSpec((1, 128), lambda i: (0, 0)),
                   pl.BlockSpec((1, d), lambda i: (0, 0))],
        out_shape=[jax.ShapeDtypeStruct((s, d), F32),
                   jax.ShapeDtypeStruct((1, 128), F32),
                   jax.ShapeDtypeStruct((1, d), F32)],
        compiler_params=_params(("arbitrary",)),
    )(h, g, target)


def _mlp_bwd_x(dz, a, w_up, w_down, h_in, g, name):
    s, d = dz.shape
    ff = w_down.shape[0]
    slot_cols = w_up.shape[2]
    tm = min(TILE_MLP_ROWS, s)
    tf = min(TILE_MLP_BWD_FF, slot_cols)
    per_slot = slot_cols // tf
    nf = ff // tf

    def body(dz_ref, a_ref, wu_ref, wd_ref, h_ref, g_ref, da_ref, dzb_ref, dh_ref, dg_ref,
             dzs_ref, acc_ref):
        i = pl.program_id(0)
        f = pl.program_id(1)

        @pl.when(f == 0)
        def _():
            dzb = dz_ref[...].astype(BF16)
            dzs_ref[...] = dzb
            dzb_ref[...] = dzb
            acc_ref[...] = jnp.zeros_like(acc_ref)

        dr = _nt(dzs_ref[...], wd_ref[...])
        da = (dr * (2.0 * jnp.maximum(a_ref[...].astype(F32), 0.0))).astype(BF16)
        da_ref[...] = da
        acc_ref[...] += _nt(da, wu_ref[...])

        @pl.when(f == nf - 1)
        def _():
            dx, dg = _rms_bwd(acc_ref[...], h_ref[...], g_ref[...])
            dh_ref[...] = dz_ref[...] + dx

            @pl.when(i == 0)
            def _():
                dg_ref[...] = dg

            @pl.when(i > 0)
            def _():
                dg_ref[...] += dg

    return pl.pallas_call(
        body, name=name, grid=(s // tm, nf),
        in_specs=[pl.BlockSpec((tm, d), lambda i, f: (i, 0)),
                  pl.BlockSpec((tm, tf), lambda i, f: (i, f)),
                  pl.BlockSpec((None, d, tf), lambda i, f: (f // per_slot, 0, f % per_slot)),
                  pl.BlockSpec((tf, d), lambda i, f: (f, 0)),
                  pl.BlockSpec((tm, d), lambda i, f: (i, 0)), _full(g)],
        out_specs=[pl.BlockSpec((tm, tf), lambda i, f: (i, f)),
                   pl.BlockSpec((tm, d), lambda i, f: (i, 0)),
                   pl.BlockSpec((tm, d), lambda i, f: (i, 0)),
                   pl.BlockSpec((1, d), lambda i, f: (0, 0))],
        out_shape=[jax.ShapeDtypeStruct((s, ff), BF16),
                   jax.ShapeDtypeStruct((s, d), BF16),
                   jax.ShapeDtypeStruct((s, d), F32),
                   jax.ShapeDtypeStruct((1, d), F32)],
        scratch_shapes=[pltpu.VMEM((tm, d), BF16), pltpu.VMEM((tm, d), F32)],
        compiler_params=_params(("arbitrary", "arbitrary")),
    )(dz, a, w_up, w_down, h_in, g)


def _mlp_bwd_w(n, da, a, dzb, slot_cols, name):
    s, d = n.shape
    ff = a.shape[1]
    tn = min(TILE_WGRAD_N, slot_cols)
    tk = min(TILE_WGRAD_K, s)
    per_slot = slot_cols // tn
    nk = s // tk

    def body(n_ref, da_ref, a_ref, dz_ref, du_ref, dd_ref, accu_ref, accd_ref):
        k = pl.program_id(1)

        @pl.when(k == 0)
        def _():
            accu_ref[...] = jnp.zeros_like(accu_ref)
            accd_ref[...] = jnp.zeros_like(accd_ref)

        accu_ref[...] += _tn(n_ref[...], da_ref[...])
        r = jnp.square(jnp.maximum(a_ref[...].astype(F32), 0.0)).astype(BF16)
        accd_ref[...] += _tn(r, dz_ref[...])

        @pl.when(k == nk - 1)
        def _():
            du_ref[...] = accu_ref[...].astype(BF16)
            dd_ref[...] = accd_ref[...].astype(BF16)

    return pl.pallas_call(
        body, name=name, grid=(ff // tn, nk),
        in_specs=[pl.BlockSpec((tk, d), lambda f, k: (k, 0)),
                  pl.BlockSpec((tk, tn), lambda f, k: (k, f)),
                  pl.BlockSpec((tk, tn), lambda f, k: (k, f)),
                  pl.BlockSpec((tk, d), lambda f, k: (k, 0))],
        out_specs=[pl.BlockSpec((None, d, tn), lambda f, k: (f // per_slot, 0, f % per_slot)),
                   pl.BlockSpec((tn, d), lambda f, k: (f, 0))],
        out_shape=[jax.ShapeDtypeStruct((ff // slot_cols, d, slot_cols), BF16),
                   jax.ShapeDtypeStruct((ff, d), BF16)],
        scratch_shapes=[pltpu.VMEM((d, tn), F32), pltpu.VMEM((tn, d), F32)],
        compiler_params=_params(("parallel", "arbitrary")),
    )(n, da, a, dzb)


def _pool_bwd(after, dh, h, g, pw, ps):
    s, d = h.shape
    cg = d // len(POOL_WINDOWS)
    tm = min(TILE_ROWS, s)
    nb = s // tm
    halo_per_tile = tm // POOL_HALO

    def body(after_ref, dh_ref, h_ref, halo_ref, g_ref, pw_ref, ps_ref,
             dx_ref, dpw_ref, dps_ref, dg_ref, nbuf, qbuf, dn_ref, carry, dpw_acc):
        i = pl.program_id(0)
        blk = nb - 1 - i

        @pl.when(i == 0)
        def _():
            carry[...] = jnp.zeros_like(carry)
            dpw_acc[...] = jnp.zeros_like(dpw_acc)
            dps_ref[...] = jnp.zeros_like(dps_ref)
            dg_ref[...] = jnp.zeros_like(dg_ref)

        hv = h_ref[...]
        n, _ = _rms_fwd(hv, g_ref[...])
        nh, _ = _rms_fwd(halo_ref[...], g_ref[...])
        nbuf[0:POOL_HALO, :] = jnp.where(blk == 0, 0.0, nh)
        nbuf[POOL_HALO:POOL_HALO + tm, :] = n
        dhv = dh_ref[...]
        for k, window in enumerate(POOL_WINDOWS):
            cols = slice(k * cg, (k + 1) * cg)
            cnt = _pool_counts(blk * tm, tm, window)
            sums = _window_sum_down(nbuf[:, cols], window)[POOL_HALO:, :]
            pb = (sums / cnt - n[:, cols]).astype(BF16)
            dyk = dhv[:, cols]
            dps_ref[:, cols] += jnp.sum(dyk * _nn(pb, pw_ref[k]), axis=0, keepdims=True)
            dyb = (dyk * ps_ref[:, cols]).astype(BF16)
            dpw_acc[k] += _tn(pb, dyb)
            dpool = _nt(dyb, pw_ref[k])
            qv = dpool / cnt
            qbuf[0:tm, cols] = qv
            qbuf[tm:tm + POOL_HALO, cols] = carry[:, cols]
            dn_ref[:, cols] = _window_sum_up(qbuf[:, cols], window)[0:tm, :] - dpool
            carry[:, cols] = qv[0:POOL_HALO, :]
        dx, dg = _rms_bwd(dn_ref[...], hv, g_ref[...])
        dx_ref[...] = dhv + dx
        dg_ref[...] += dg

        @pl.when(i == nb - 1)
        def _():
            dpw_ref[...] = dpw_acc[...].astype(BF16)

    rev = lambda i: (nb - 1 - i, 0)
    return pl.pallas_call(
        body, name="pool_bwd", grid=(nb,),
        in_specs=[ANY, pl.BlockSpec((tm, d), rev), pl.BlockSpec((tm, d), rev),
                  pl.BlockSpec((POOL_HALO, d),
                               lambda i: (jnp.maximum((nb - 1 - i) * halo_per_tile - 1, 0), 0)),
                  _full(g), _full(pw), _full(ps)],
        out_specs=[pl.BlockSpec((tm, d), rev), _full(pw),
                   pl.BlockSpec((1, d), lambda i: (0, 0)),
                   pl.BlockSpec((1, d), lambda i: (0, 0))],
        out_shape=[jax.ShapeDtypeStruct((s, d), F32),
                   jax.ShapeDtypeStruct(pw.shape, BF16),
                   jax.ShapeDtypeStruct((1, d), F32),
                   jax.ShapeDtypeStruct((1, d), F32)],
        scratch_shapes=[pltpu.VMEM((tm + POOL_HALO, d), F32), pltpu.VMEM((tm + POOL_HALO, d), F32),
                        pltpu.VMEM((tm, d), F32), pltpu.VMEM((POOL_HALO, d), F32),
                        pltpu.VMEM(pw.shape, F32)],
        compiler_params=_params(("arbitrary",)),
    )(after, dh, h, h, g, pw, ps)


def _conv_out_bwd(after, dh, w_out, o, bcx, cw):
    s, d = dh.shape
    c = o.shape[1]
    tm = min(TILE_ROWS, s)
    nb = s // tm
    halo_per_tile = tm // CONV_HALO

    def body(after_ref, dh_ref, w_ref, o_ref, b_ref, c_ref, xin_ref, ch_ref, xh_ref, cw_ref,
             do_ref, delta_ref, dbcx_ref, dw_ref, dcw_ref, ubuf, dbuf, carry, acc):
        i = pl.program_id(0)
        blk = nb - 1 - i

        @pl.when(i == 0)
        def _():
            carry[...] = jnp.zeros_like(carry)
            acc[...] = jnp.zeros_like(acc)
            dcw_ref[...] = jnp.zeros_like(dcw_ref)

        dm = dh_ref[...].astype(BF16)
        dcat = _nt(dm, w_ref[...])
        do = dcat[:, 0:c]
        dy = dcat[:, c:2 * c]
        do_ref[...] = do.astype(BF16)
        head_of_lane = lax.shift_right_logical(_lane((8, c)), HEAD_DIM.bit_length() - 1)
        heads = (head_of_lane == _row((8, c))).astype(BF16)
        delta_ref[...] = _exact_nt(heads, do * o_ref[...].astype(F32))

        cv_ = c_ref[...]
        xin = xin_ref[...]
        bv = b_ref[...]
        u = cv_ * xin
        ubuf[0:CONV_HALO, :] = jnp.where(blk == 0, 0.0, ch_ref[...] * xh_ref[...])
        ubuf[CONV_HALO:CONV_HALO + tm, :] = u
        u1 = ubuf[CONV_HALO - 1:CONV_HALO - 1 + tm, :]
        u2 = ubuf[CONV_HALO - 2:CONV_HALO - 2 + tm, :]
        w0, w1, w2 = cw_ref[0:1, :], cw_ref[1:2, :], cw_ref[2:3, :]
        cv = (w0 * u2 + w1 * u1) + w2 * u
        acc[0:c, :] += _tn(o_ref[...], dm)
        acc[c:2 * c, :] += _tn((bv * cv).astype(BF16), dm)

        dcv = dy * bv
        dcw_ref[0:1, :] += jnp.sum(dcv * u2, axis=0, keepdims=True)
        dcw_ref[1:2, :] += jnp.sum(dcv * u1, axis=0, keepdims=True)
        dcw_ref[2:3, :] += jnp.sum(dcv * u, axis=0, keepdims=True)
        dbuf[0:tm, :] = dcv
        dbuf[tm:tm + CONV_HALO, :] = carry[...]
        du = w2 * dcv + w1 * dbuf[1:1 + tm, :] + w0 * dbuf[2:2 + tm, :]
        dbcx_ref[:, 0:c] = (dy * cv).astype(BF16)
        dbcx_ref[:, c:2 * c] = (du * xin).astype(BF16)
        dbcx_ref[:, 2 * c:3 * c] = (du * cv_).astype(BF16)
        carry[...] = dcv[0:CONV_HALO, :]

        @pl.when(i == nb - 1)
        def _():
            dw_ref[...] = acc[...].astype(BF16)

    rev = lambda k: (lambda i: (nb - 1 - i, k))
    halo = lambda k: (lambda i: (jnp.maximum((nb - 1 - i) * halo_per_tile - 1, 0), k))
    return pl.pallas_call(
        body, name="conv_out_bwd", grid=(nb,),
        in_specs=[ANY, pl.BlockSpec((tm, d), rev(0)), _full(w_out), pl.BlockSpec((tm, c), rev(0)),
                  pl.BlockSpec((tm, c), rev(0)), pl.BlockSpec((tm, c), rev(1)),
                  pl.BlockSpec((tm, c), rev(2)),
                  pl.BlockSpec((CONV_HALO, c), halo(1)), pl.BlockSpec((CONV_HALO, c), halo(2)),
                  _full(cw)],
        out_specs=[pl.BlockSpec((tm, c), rev(0)),
                   pl.BlockSpec((8, tm), lambda i: (0, nb - 1 - i)),
                   pl.BlockSpec((tm, 3 * c), rev(0)),
                   _full(w_out), _full(cw)],
        out_shape=[jax.ShapeDtypeStruct((s, c), BF16),
                   jax.ShapeDtypeStruct((8, s), F32),
                   jax.ShapeDtypeStruct((s, 3 * c), BF16),
                   jax.ShapeDtypeStruct(w_out.shape, BF16),
                   jax.ShapeDtypeStruct(cw.shape, F32)],
        scratch_shapes=[pltpu.VMEM((tm + CONV_HALO, c), F32), pltpu.VMEM((tm + CONV_HALO, c), F32),
                        pltpu.VMEM((CONV_HALO, c), F32), pltpu.VMEM(w_out.shape, F32)],
        compiler_params=_params(("arbitrary",)),
    )(after, dh, w_out, o, bcx, bcx, bcx, bcx, bcx, cw)


def _attn_bwd(qa, ka, qkv, do, lse, delta):
    s = qa.shape[1]
    a = N_HEADS * HEAD_DIM
    t = min(TILE_ATTN, s)
    nq = s // t
    n_pairs = N_HEADS // 2
    v_block0 = 2 * a // 128

    def body(ka_ref, v_ref, qa_ref, do_ref, lse_ref, delta_ref,
             dqt_ref, dka_ref, dv_ref, dk_acc, dv_acc):
        g = pl.program_id(0)
        j = pl.program_id(1)

        @pl.when(j == 0)
        def _():
            dqt_ref[...] = jnp.zeros_like(dqt_ref)

        dk_acc[...] = jnp.zeros_like(dk_acc)
        dv_acc[...] = jnp.zeros_like(dv_acc)
        lane = _lane((t, 128))
        vf = v_ref[...].astype(F32)
        v_heads = [jnp.where(lane < HEAD_DIM, vf, 0.0).astype(BF16),
                   jnp.where(lane >= HEAD_DIM, vf, 0.0).astype(BF16)]
        ke_t = [ka_ref[e].astype(F32).T.astype(BF16) for e in range(2)]

        def q_step(i, masked):
            qs = pl.ds(pl.multiple_of(i * t, t), t)
            dob = do_ref[qs, :]
            for e in range(2):
                qe = qa_ref[e, qs, :]
                sc = _nt(ka_ref[e], qe)
                if masked:
                    sc = jnp.where(_row((t, t)) <= _lane((t, t)), sc, NEG_BIG)
                p = jnp.exp(sc - lse_ref[pl.ds(e, 1), qs])
                dv_acc[e] += _nn(p.astype(BF16), dob)
                dp = _nt(v_heads[e], dob)
                ds = (p * (dp - delta_ref[pl.ds(2 * g + e, 1), qs])).astype(BF16)
                dk_acc[e] += _nn(ds, qe)
                dqt_ref[e, :, qs] += _nn(ke_t[e], ds)

        q_step(j, True)

        def full_step(i, carry):
            q_step(i, False)
            return carry

        lax.fori_loop(j + 1, nq, full_step, 0)
        dka_ref[...] = dk_acc[...]
        dv_ref[...] = jnp.where(lane < HEAD_DIM, dv_acc[0], dv_acc[1]).astype(BF16)

    return pl.pallas_call(
        body, name="attn_bwd", grid=(n_pairs, nq),
        in_specs=[pl.BlockSpec((2, t, 128), lambda g, j: (g, j, 0)),
                  pl.BlockSpec((t, 128), lambda g, j: (j, v_block0 + g)),
                  pl.BlockSpec((2, s, 128), lambda g, j: (g, 0, 0)),
                  pl.BlockSpec((s, 128), lambda g, j: (0, g)),
                  pl.BlockSpec((None, 8, s), lambda g, j: (g, 0, 0)),
                  pl.BlockSpec((8, s), lambda g, j: (0, 0))],
        out_specs=[pl.BlockSpec((2, 128, s), lambda g, j: (g, 0, 0)),
                   pl.BlockSpec((2, t, 128), lambda g, j: (g, j, 0)),
                   pl.BlockSpec((t, 128), lambda g, j: (j, g))],
        out_shape=[jax.ShapeDtypeStruct((N_HEADS, 128, s), F32),
                   jax.ShapeDtypeStruct((N_HEADS, s, 128), F32),
                   jax.ShapeDtypeStruct((s, a), BF16)],
        scratch_shapes=[pltpu.VMEM((2, t, 128), F32), pltpu.VMEM((2, t, 128), F32)],
        compiler_params=_params(("parallel", "arbitrary")),
    )(ka, qkv, qa, do, lse, delta)


def _gate_bwd(dqa, dka, dv, fl, bf):
    s = fl.shape[0]
    a = N_HEADS * HEAD_DIM
    tm = min(TILE_ROWS, s)
    nb = s // tm

    def body(dqa_ref, dka_ref, dv_ref, fl_ref, bf_ref, dqkv_ref, dfl_ref, dbf_ref, carry):
        i = pl.program_id(0)

        @pl.when(i == 0)
        def _():
            carry[...] = jnp.zeros_like(carry)
            dbf_ref[...] = jnp.zeros_like(dbf_ref)

        lane = _lane((tm, 128))
        dcum = jnp.zeros((tm, 128), F32)
        for pair in range(N_HEADS // 2):
            qs, ks = [], []
            for e in range(2):
                h = 2 * pair + e
                dq = dqa_ref[h].T
                dk = dka_ref[h]
                dc = jnp.sum(jnp.where(lane == LANE_CQ, dq, 0.0)
                             - jnp.where(lane == LANE_ONE, dk, 0.0), axis=1, keepdims=True)
                dcum = jnp.where(lane == h, dc, dcum)
                qs.append(dq * ATTN_SCALE)
                ks.append(dk)
            cols = slice(pair * 128, (pair + 1) * 128)
            dqkv_ref[:, cols] = jnp.where(
                lane < HEAD_DIM, qs[0], pltpu.roll(qs[1], HEAD_DIM, axis=1)).astype(BF16)
            dqkv_ref[:, a + pair * 128:a + (pair + 1) * 128] = jnp.where(
                lane < HEAD_DIM, ks[0], pltpu.roll(ks[1], HEAD_DIM, axis=1)).astype(BF16)
        dqkv_ref[:, 2 * a:3 * a] = dv_ref[...]

        upper = (_lane((tm, tm)) >= _row((tm, tm))).astype(BF16)
        dlogf = _exact_nn(upper, dcum) + carry[0:1, :]
        carry[0:1, :] = dlogf[0:1, :]
        z = fl_ref[...] + bf_ref[...]
        ez = jnp.exp(-jnp.abs(z))
        sig_neg = jnp.where(z >= 0.0, ez, 1.0) / (1.0 + ez)
        dz = jnp.where(lane < N_HEADS, dlogf * sig_neg, 0.0)
        dfl_ref[...] = dz.astype(BF16)
        dbf_ref[...] += jnp.sum(dz, axis=0, keepdims=True)

    rev3 = lambda i: (0, nb - 1 - i, 0)
    rev = lambda i: (nb - 1 - i, 0)
    return pl.pallas_call(
        body, name="gate_bwd", grid=(nb,),
        in_specs=[pl.BlockSpec((N_HEADS, 128, tm), lambda i: (0, 0, nb - 1 - i)),
                  pl.BlockSpec((N_HEADS, tm, 128), rev3),
                  pl.BlockSpec((tm, a), rev), pl.BlockSpec((tm, 128), rev), _full(bf)],
        out_specs=[pl.BlockSpec((tm, 3 * a), rev), pl.BlockSpec((tm, 128), rev),
                   pl.BlockSpec((1, 128), lambda i: (0, 0))],
        out_shape=[jax.ShapeDtypeStruct((s, 3 * a), BF16),
                   jax.ShapeDtypeStruct((s, 128), BF16),
                   jax.ShapeDtypeStruct((1, 128), F32)],
        scratch_shapes=[pltpu.VMEM((8, 128), F32)],
        compiler_params=_params(("arbitrary",)),
    )(dqa, dka, dv, fl, bf)


def _in_proj_bwd(after, dqkv, dfl, dbcx, w_qkv, w_f, w_bcx, x, g, dh):
    s, d = x.shape
    tm = min(TILE_ROWS, s)

    def body(after_ref, dq_ref, df_ref, db_ref, wq_ref, wf_ref, wb_ref, x_ref, g_ref, dh_ref,
             gx_ref, dg_ref):
        i = pl.program_id(0)
        dn = (_nt(dq_ref[...], wq_ref[...]) + _nt(df_ref[...], wf_ref[...])
              + _nt(db_ref[...], wb_ref[...]))
        dx, dg = _rms_bwd(dn, x_ref[...], g_ref[...])
        gx_ref[...] = dh_ref[...] + dx

        @pl.when(i == 0)
        def _():
            dg_ref[...] = dg

        @pl.when(i > 0)
        def _():
            dg_ref[...] += dg

    rows = lambda c: pl.BlockSpec((tm, c), lambda i: (i, 0))
    return pl.pallas_call(
        body, name="in_proj_bwd", grid=(s // tm,),
        in_specs=[ANY, rows(dqkv.shape[1]), rows(dfl.shape[1]), rows(dbcx.shape[1]),
                  _full(w_qkv), _full(w_f), _full(w_bcx), rows(d), _full(g), rows(d)],
        out_specs=[rows(d), pl.BlockSpec((1, d), lambda i: (0, 0))],
        out_shape=[jax.ShapeDtypeStruct((s, d), F32), jax.ShapeDtypeStruct((1, d), F32)],
        compiler_params=_params(("arbitrary",)),
    )(after, dqkv, dfl, dbcx, w_qkv, w_f, w_bcx, x, g, dh)


def _wgrad_in(n, dys):
    s, d = n.shape
    m = len(dys)
    tk = min(TILE_ROWS, s)
    nk = s // tk

    def body(*refs):
        n_ref, dy_refs, dw_refs, accs = refs[0], refs[1:1 + m], refs[1 + m:1 + 2 * m], refs[1 + 2 * m:]
        k = pl.program_id(0)

        @pl.when(k == 0)
        def _():
            for acc in accs:
                acc[...] = jnp.zeros_like(acc)

        nb = n_ref[...]
        for dy_ref, acc in zip(dy_refs, accs):
            acc[...] += _tn(nb, dy_ref[...])

        @pl.when(k == nk - 1)
        def _():
            for dw_ref, acc in zip(dw_refs, accs):
                dw_ref[...] = acc[...].astype(BF16)

    return pl.pallas_call(
        body, name="wgrad_in", grid=(nk,),
        in_specs=[pl.BlockSpec((tk, d), lambda k: (k, 0))]
        + [pl.BlockSpec((tk, dy.shape[1]), lambda k: (k, 0)) for dy in dys],
        out_specs=[pl.BlockSpec((d, dy.shape[1]), lambda k: (0, 0)) for dy in dys],
        out_shape=[jax.ShapeDtypeStruct((d, dy.shape[1]), BF16) for dy in dys],
        scratch_shapes=[pltpu.VMEM((d, dy.shape[1]), F32) for dy in dys],
        compiler_params=_params(("arbitrary",)),
    )(n, *dys)


def _row_tile(rows):
    t = min(TILE_ELEM_ROWS, rows)
    while rows % t:
        t //= 2
    return t


def _sum_pair(grad, theirs, core, name):
    slots, rows, cols = theirs.shape
    tr = _row_tile(rows)
    nb = rows // tr

    def body(core_ref, a_ref, b_ref, o_ref):
        o_ref[...] = (a_ref[...].astype(F32) + b_ref[...].astype(F32)).astype(BF16)

    spec = pl.BlockSpec((None, tr, cols), lambda s, i, core_ref: (s, i, 0))
    return pl.pallas_call(
        body, name=name,
        grid_spec=pltpu.PrefetchScalarGridSpec(
            num_scalar_prefetch=1, grid=(slots, nb),
            in_specs=[pl.BlockSpec((None, tr, cols),
                                   lambda s, i, core_ref: (s, core_ref[0] * nb + i, 0)), spec],
            out_specs=spec),
        out_shape=jax.ShapeDtypeStruct(theirs.shape, BF16),
        compiler_params=_params(("parallel", "parallel")),
    )(core, grad, theirs)


def _sum_chips(sums, others, chip, name):
    _, rows, cols = sums.shape
    tr = _row_tile(rows)

    def body(chip_ref, a_ref, b_ref, o_ref):
        acc = a_ref[...].astype(F32)
        for k in range(N_CHIPS - 1):
            acc = acc + b_ref[k].astype(F32)
        o_ref[...] = acc

    return pl.pallas_call(
        body, name=name,
        grid_spec=pltpu.PrefetchScalarGridSpec(
            num_scalar_prefetch=1, grid=(rows // tr,),
            in_specs=[pl.BlockSpec((None, tr, cols), lambda i, chip_ref: (chip_ref[0], i, 0)),
                      pl.BlockSpec((N_CHIPS - 1, tr, cols), lambda i, chip_ref: (0, i, 0))],
            out_specs=pl.BlockSpec((tr, cols), lambda i, chip_ref: (i, 0))),
        out_shape=jax.ShapeDtypeStruct((rows, cols), F32),
        compiler_params=_params(("parallel",)),
    )(chip, sums, others)


def _adamw_math(w, g, m, v):
    m = ADAM_B1 * m + (1.0 - ADAM_B1) * g
    v = ADAM_B2 * v + (1.0 - ADAM_B2) * jnp.square(g)
    m_hat = m / (1.0 - ADAM_B1 ** ADAM_STEP)
    v_hat = v / (1.0 - ADAM_B2 ** ADAM_STEP)
    delta = -ADAM_LR * (m_hat / (jnp.sqrt(v_hat) + ADAM_EPS) + ADAM_WD * w)
    return delta, m, v


def _adamw(w, g, m, v, name):
    rows, cols = w.shape
    tr = _row_tile(rows)

    def body(w_ref, g_ref, m_ref, v_ref, d_ref, nm_ref, nv_ref):
        delta, nm, nv = _adamw_math(w_ref[...], g_ref[...], m_ref[...], v_ref[...])
        d_ref[...] = delta
        nm_ref[...] = nm
        nv_ref[...] = nv

    spec = pl.BlockSpec((tr, cols), lambda i: (i, 0))
    out = jax.ShapeDtypeStruct(w.shape, F32)
    return pl.pallas_call(
        body, name=name, grid=(rows // tr,), in_specs=[spec] * 4, out_specs=[spec] * 3,
        out_shape=[out, out, out], compiler_params=_params(("parallel",)),
    )(w, g, m, v)


def _sum_devices(parts):
    def body(p_ref, g_ref):
        g = p_ref[0]
        for k in range(1, N_DEV):
            g = g + p_ref[k]
        g_ref[...] = g

    return pl.pallas_call(
        body, name="sum_devices",
        in_specs=[pl.BlockSpec(memory_space=pltpu.VMEM)],
        out_specs=pl.BlockSpec(memory_space=pltpu.VMEM),
        out_shape=jax.ShapeDtypeStruct(parts.shape[1:], F32),
    )(parts)


def _mesh_position():
    x, y, c = lax.axis_index("x"), lax.axis_index("y"), lax.axis_index("c")
    chips = [(1 - x, y), (x, 1 - y), (1 - x, 1 - y)]
    return x, y, c, chips


ANY = pl.BlockSpec(memory_space=pl.ANY)
HBM = pl.BlockSpec(memory_space=pltpu.HBM)
SEM = pl.BlockSpec(memory_space=pltpu.SEMAPHORE)
SPLIT_COPY_EFFECT = pltpu.SideEffectType.DATAFLOW_SIDE_EFFECTING


def _in_hbm(a):
    return pltpu.with_memory_space_constraint(a, pltpu.HBM)


def _chip_copies(views, srcs, lands, send, recv):
    _, _, c, chips = _mesh_position()
    cps = []
    for a in range(len(srcs)):
        for k, (px, py) in enumerate(chips):
            src, dst = views(a, k, srcs[a], lands[a], c, 2 * px + py)
            sem = a * (N_CHIPS - 1) + k
            cps.append(pltpu.make_async_remote_copy(
                src_ref=src, dst_ref=dst, send_sem=send.at[sem], recv_sem=recv.at[sem],
                device_id=(px, py, c), device_id_type=MESH))
    return cps


def _ici_start(sources, land_shapes, views, after, name):
    n = len(sources)

    def body(*refs):
        srcs, lands = refs[:n], refs[n:2 * n]
        send, recv = refs[2 * n + 1], refs[2 * n + 2]
        token = refs[-1]
        for cp in _chip_copies(views, srcs, lands, send, recv):
            cp.start()
        token[...] = jnp.zeros_like(token)

    lands = [_in_hbm(lax.empty(s.shape, s.dtype)) for s in land_shapes]
    outs = pl.pallas_call(
        body, name=name,
        in_specs=[HBM] * (2 * n) + [ANY],
        out_specs=[SEM, SEM] + [HBM] * (2 * n) + [pl.BlockSpec(memory_space=pltpu.VMEM)],
        out_shape=[pltpu.SemaphoreType.DMA((n * (N_CHIPS - 1),))] * 2
        + [pltpu.HBM(a.shape, a.dtype) for a in sources]
        + [pltpu.HBM(s.shape, s.dtype) for s in land_shapes]
        + [jax.ShapeDtypeStruct((8, 128), F32)],
        input_output_aliases={i: 2 + i for i in range(2 * n)},
        compiler_params=pltpu.CompilerParams(has_side_effects=SPLIT_COPY_EFFECT),
    )(*[_in_hbm(a) for a in sources], *lands, after)
    return outs[0], outs[1], list(outs[2:2 + n]), list(outs[2 + n:2 + 2 * n]), outs[-1]


def _ici_wait(handle, views, after, name):
    send, recv, srcs, lands, _ = handle
    n = len(srcs)

    def body(*refs):
        src_refs, land_refs = refs[:n], refs[n:2 * n]
        for cp in _chip_copies(views, src_refs, land_refs, refs[2 * n], refs[2 * n + 1]):
            cp.wait_send()
            cp.wait_recv()

    outs = pl.pallas_call(
        body, name=name,
        in_specs=[HBM] * (2 * n) + [SEM, SEM, ANY],
        out_specs=[HBM] * (2 * n),
        out_shape=[pltpu.HBM(a.shape, a.dtype) for a in srcs]
        + [pltpu.HBM(a.shape, a.dtype) for a in lands],
        input_output_aliases={i: i for i in range(2 * n)},
        compiler_params=pltpu.CompilerParams(has_side_effects=SPLIT_COPY_EFFECT),
    )(*srcs, *lands, send, recv, after)
    return list(outs[:n]), list(outs[n:])


def _gather_views(split):
    def views(a, k, src, land, c, slot):
        if split[a]:
            half = src.shape[0] // 2
            src = src.at[pl.ds(c * half, half)]
        return src, land.at[k]
    return views


def _scatter_views(a, k, src, land, c, slot):
    return src.at[slot], land.at[k]


def _gather_land_shapes(shards, split):
    return [jax.ShapeDtypeStruct(
        (N_CHIPS - 1, a.shape[0] // 2 if sp else a.shape[0]) + a.shape[1:], a.dtype)
        for a, sp in zip(shards, split)]


def _gather_finish(shards, lands, split, name):
    n = len(shards)
    ns = sum(split)
    d_index = {a: i for i, a in enumerate(a for a in range(n) if split[a])}

    def body(*refs):
        shard, land, outs = refs[:n], refs[n:2 * n], refs[2 * n:3 * n]
        obuf, fbuf = refs[3 * n:4 * n], refs[4 * n:5 * n]
        dbuf = refs[5 * n:5 * n + ns]
        ld_own, st_own, ld, st_mine, st_sib, send, recv = refs[5 * n + ns:]
        x, y, c, chips = _mesh_position()
        me = 2 * x + y
        own_loads, loads, sends, pending = [], {}, [], []
        for a in range(n):
            cp = pltpu.make_async_copy(shard[a], obuf[a], ld_own.at[a])
            cp.start()
            own_loads.append(cp)
        for a in range(n):
            for k in range(N_CHIPS - 1):
                cp = pltpu.make_async_copy(land[a].at[k], fbuf[a].at[k], ld.at[a, k])
                cp.start()
                loads[a, k] = cp
        for a in range(n):
            own_loads[a].wait()
            cp = pltpu.make_async_copy(obuf[a], outs[a].at[me], st_own.at[a])
            cp.start()
            pending.append(cp)
        for a in range(n):
            rows = shard[a].shape[0]
            for k, (px, py) in enumerate(chips):
                loads[a, k].wait()
                part = pl.ds(c * (rows // 2), rows // 2) if split[a] else pl.ds(0, rows)
                cp = pltpu.make_async_copy(fbuf[a].at[k], outs[a].at[2 * px + py, part],
                                           st_mine.at[a, k])
                cp.start()
                pending.append(cp)
                if split[a]:
                    fw = pltpu.make_async_remote_copy(
                        src_ref=fbuf[a].at[k], dst_ref=dbuf[d_index[a]].at[k],
                        send_sem=send.at[a, k], recv_sem=recv.at[a, k],
                        device_id=(x, y, 1 - c), device_id_type=MESH)
                    fw.start()
                    sends.append((a, k, fw))
        for a, k, fw in sends:
            px, py = chips[k]
            half = shard[a].shape[0] // 2
            fw.wait_recv()
            cp = pltpu.make_async_copy(dbuf[d_index[a]].at[k],
                                       outs[a].at[2 * px + py, pl.ds((1 - c) * half, half)],
                                       st_sib.at[a, k])
            cp.start()
            pending.append(cp)
        for _, _, fw in sends:
            fw.wait_send()
        for cp in pending:
            cp.wait()

    stage = [pltpu.VMEM(a.shape, a.dtype) for a in lands]
    dma = lambda *shape: pltpu.SemaphoreType.DMA(shape)
    return pl.pallas_call(
        body, name=name,
        in_specs=[ANY] * (2 * n), out_specs=[ANY] * n,
        out_shape=[jax.ShapeDtypeStruct((N_CHIPS,) + a.shape, a.dtype) for a in shards],
        scratch_shapes=[pltpu.VMEM(a.shape, a.dtype) for a in shards] + stage
        + [s for s, sp in zip(stage, split) if sp]
        + [dma(n), dma(n), dma(n, 3), dma(n, 3), dma(n, 3), dma(n, 3), dma(n, 3)],
        compiler_params=pltpu.CompilerParams(vmem_limit_bytes=VMEM_LIMIT_BYTES),
    )(*shards, *lands)


def _exchange_siblings(grads, name):
    n = len(grads)

    def body(*refs):
        ins, theirs = refs[:n], refs[n:2 * n]
        sbuf, rbuf = refs[2 * n:3 * n], refs[3 * n:4 * n]
        ld, st, send, recv = refs[4 * n:]
        x, y, c, _ = _mesh_position()
        loads, sends, stores = [], [], []
        for a in range(n):
            half = ins[a].shape[1] // 2
            cp = pltpu.make_async_copy(ins[a].at[:, pl.ds((1 - c) * half, half)], sbuf[a], ld.at[a])
            cp.start()
            loads.append(cp)
        for a in range(n):
            loads[a].wait()
            rc = pltpu.make_async_remote_copy(
                src_ref=sbuf[a], dst_ref=rbuf[a], send_sem=send.at[a], recv_sem=recv.at[a],
                device_id=(x, y, 1 - c), device_id_type=MESH)
            rc.start()
            sends.append(rc)
        for a in range(n):
            sends[a].wait_recv()
            cp = pltpu.make_async_copy(rbuf[a], theirs[a], st.at[a])
            cp.start()
            stores.append(cp)
        for a in range(n):
            sends[a].wait_send()
            stores[a].wait()

    half_shape = lambda a: (a.shape[0], a.shape[1] // 2, a.shape[2])
    stage = [pltpu.VMEM(half_shape(a), a.dtype) for a in grads]
    return pl.pallas_call(
        body, name=name,
        in_specs=[ANY] * n, out_specs=[ANY] * n,
        out_shape=[jax.ShapeDtypeStruct(half_shape(a), a.dtype) for a in grads],
        scratch_shapes=stage + stage + [pltpu.SemaphoreType.DMA((n,))] * 4,
        compiler_params=pltpu.CompilerParams(vmem_limit_bytes=VMEM_LIMIT_BYTES),
    )(*grads)


def _share_halves(halves):
    n = len(halves)

    def body(*refs):
        ins, outs = refs[:n], refs[n:2 * n]
        sbuf, rbuf = refs[2 * n:3 * n], refs[3 * n:4 * n]
        ld, st_own, st_sib, send, recv = refs[4 * n:]
        x, y, c, _ = _mesh_position()
        loads, sends, stores = [], [], []
        for a in range(n):
            cp = pltpu.make_async_copy(ins[a], sbuf[a], ld.at[a])
            cp.start()
            loads.append(cp)
        for a in range(n):
            half = ins[a].shape[0]
            loads[a].wait()
            rc = pltpu.make_async_remote_copy(
                src_ref=sbuf[a], dst_ref=rbuf[a], send_sem=send.at[a], recv_sem=recv.at[a],
                device_id=(x, y, 1 - c), device_id_type=MESH)
            rc.start()
            sends.append(rc)
            cp = pltpu.make_async_copy(sbuf[a], outs[a].at[pl.ds(c * half, half)], st_own.at[a])
            cp.start()
            stores.append(cp)
        for a in range(n):
            half = ins[a].shape[0]
            sends[a].wait_recv()
            cp = pltpu.make_async_copy(rbuf[a], outs[a].at[pl.ds((1 - c) * half, half)], st_sib.at[a])
            cp.start()
            stores.append(cp)
        for cp in sends:
            cp.wait_send()
        for cp in stores:
            cp.wait()

    stage = [pltpu.VMEM(a.shape, a.dtype) for a in halves]
    return pl.pallas_call(
        body, name="share_halves",
        in_specs=[ANY] * n, out_specs=[ANY] * n,
        out_shape=[jax.ShapeDtypeStruct((2 * a.shape[0],) + a.shape[1:], a.dtype)
                   for a in halves],
        scratch_shapes=stage + stage + [pltpu.SemaphoreType.DMA((n,))] * 5,
        compiler_params=pltpu.CompilerParams(vmem_limit_bytes=VMEM_LIMIT_BYTES),
    )(*halves)


def _gather_small(part):
    def body(in_ref, out_ref, send, recv, local):
        x, y, c, _ = _mesh_position()
        me = 4 * x + 2 * y + c
        cps = [pltpu.make_async_copy(in_ref, out_ref.at[me], local)]
        k = 0
        for fx in range(2):
            for fy in range(2):
                for fc in range(2):
                    if fx or fy or fc:
                        cps.append(pltpu.make_async_remote_copy(
                            src_ref=in_ref, dst_ref=out_ref.at[me], send_sem=send.at[k],
                            recv_sem=recv.at[k], device_id=(x ^ fx, y ^ fy, c ^ fc),
                            device_id_type=MESH))
                        k += 1
        for cp in cps:
            cp.start()
        for cp in cps:
            cp.wait()

    return pl.pallas_call(
        body, name="gather_small",
        in_specs=[pl.BlockSpec(memory_space=pltpu.VMEM)],
        out_specs=pl.BlockSpec(memory_space=pltpu.VMEM),
        out_shape=jax.ShapeDtypeStruct((N_DEV,) + part.shape, part.dtype),
        scratch_shapes=[pltpu.SemaphoreType.DMA((N_DEV - 1,)), pltpu.SemaphoreType.DMA((N_DEV - 1,)),
                        pltpu.SemaphoreType.DMA],
    )(part)


def _scatter_start(grads, core, tag):
    theirs = _exchange_siblings(grads, "exchange_siblings_" + tag)
    sums = [_sum_pair(g, t, core, "sum_siblings_%s_%d" % (tag, i))
            for i, (g, t) in enumerate(zip(grads, theirs))]
    lands = [jax.ShapeDtypeStruct((N_CHIPS - 1,) + s.shape[1:], s.dtype) for s in sums]
    return _ici_start(sums, lands, _scatter_views, theirs[0], "scatter_start_" + tag)


def _scatter_finish(handle, chip, after, tag):
    sums, got = _ici_wait(handle, _scatter_views, after, "scatter_wait_" + tag)
    return [_sum_chips(s, g, chip, "sum_chips_%s_%d" % (tag, i))
            for i, (s, g) in enumerate(zip(sums, got))]


def _pad_rows(a, rows):
    return jnp.pad(a, ((0, rows - a.shape[0]), (0, 0)))


def kernel(x, norm_mix_0, w_in_0, b_f_0, conv_w_0, w_out_0, norm_ffn_0, w_up_0, w_down_0, norm_mix_1, pool_w_1, pool_scale_1, norm_ffn_1, w_up_1, w_down_1, final_norm, loss_target, m_norm_mix_0, m_w_in_0, m_b_f_0, m_conv_w_0, m_w_out_0, m_norm_ffn_0, m_w_up_0, m_w_down_0, m_norm_mix_1, m_pool_w_1, m_pool_scale_1, m_norm_ffn_1, m_w_up_1, m_w_down_1, m_final_norm, v_norm_mix_0, v_w_in_0, v_b_f_0, v_conv_w_0, v_w_out_0, v_norm_ffn_0, v_w_up_0, v_w_down_0, v_norm_mix_1, v_pool_w_1, v_pool_scale_1, v_norm_ffn_1, v_w_up_1, v_w_down_1, v_final_norm):
    d = x.shape[-1]
    a = N_HEADS * HEAD_DIM
    c_conv = conv_w_0.shape[1] * N_CHIPS
    xs = x[0]
    target = loss_target[0]
    row = lambda vec: vec.reshape(1, -1)

    big = [w_in_0, w_out_0, w_up_0, w_down_0, pool_w_1, w_up_1, w_down_1]
    first = [w_in_0.astype(BF16), w_out_0.astype(BF16), conv_w_0]
    first_split = [True, True, False]
    rest = [w.astype(BF16) for w in (w_up_0, w_down_0, pool_w_1, w_up_1, w_down_1)]
    rest_split = [True] * len(rest)
    start_a = _ici_start(first, _gather_land_shapes(first, first_split),
                         _gather_views(first_split), b_f_0, "gather_start_a")
    start_b = _ici_start(rest, _gather_land_shapes(rest, rest_split),
                         _gather_views(rest_split), start_a[-1], "gather_start_b")
    first, land_a = _ici_wait(start_a, _gather_views(first_split), start_b[-1], "gather_wait_a")
    g_in, g_out, g_conv = _gather_finish(first, land_a, first_split, "gather_finish_a")
    w_in = g_in.transpose(1, 0, 2).reshape(d, -1)
    w_qkv = w_in[:, :3 * a]
    w_f = jnp.pad(w_in[:, 3 * a:3 * a + N_HEADS], ((0, 0), (0, 128 - N_HEADS)))
    w_bcx = w_in[:, 3 * a + N_HEADS:]
    w_out = g_out.reshape(-1, d)
    conv_w = _pad_rows(g_conv.transpose(1, 0, 2).reshape(conv_w_0.shape[0], c_conv), 8)
    bf = jnp.pad(b_f_0, (0, 128 - N_HEADS)).reshape(1, 128)

    n0, qkv, fl, bcx = _ln_proj(xs, row(norm_mix_0), w_qkv, w_f, w_bcx)
    qa, ka = _gate_prep(fl, bf, qkv)
    o, lse = _attn_fwd(qa, ka, qkv)
    h1 = _conv_out(o, bcx, conv_w, w_out, xs)
    rest, land_b = _ici_wait(start_b, _gather_views(rest_split), h1, "gather_wait_b")
    g_up0, g_down0, g_pool, g_up1, g_down1 = _gather_finish(rest, land_b, rest_split,
                                                            "gather_finish_b")
    w_down0 = g_down0.reshape(-1, d)
    w_down1 = g_down1.reshape(-1, d)
    pool_w = g_pool.transpose(1, 0, 2, 3).reshape(pool_w_1.shape[0], -1, pool_w_1.shape[2])
    h2, a0, nf0 = _mlp_fwd(h1, row(norm_ffn_0), g_up0, w_down0, "mlp_fwd_0")
    h3 = _pool_fwd(h2, row(norm_mix_1), pool_w, row(pool_scale_1))
    h4, a1, nf1 = _mlp_fwd(h3, row(norm_ffn_1), g_up1, w_down1, "mlp_fwd_1")
    dh4, loss_part, d_final = _final_loss(h4, row(final_norm), target)

    slot_cols = g_up0.shape[2]
    pool_cols = pool_w.shape[2]
    core = lax.axis_index("c").astype(jnp.int32).reshape(1)
    chip_index = (2 * lax.axis_index("x") + lax.axis_index("y")).astype(jnp.int32).reshape(1)
    da1, dz1, dh3, d_nffn1 = _mlp_bwd_x(dh4, a1, g_up1, w_down1, h3, row(norm_ffn_1), "mlp_bwd_x_1")
    dw_up1, dw_down1 = _mlp_bwd_w(nf1, da1, a1, dz1, slot_cols, "mlp_bwd_w_1")
    scatter_1 = _scatter_start([dw_up1, dw_down1.reshape(N_CHIPS, -1, d)], core, "mlp1")
    dh2, dw_pool, d_pscale, d_nmix1 = _pool_bwd(scatter_1[-1], dh3, h2, row(norm_mix_1), pool_w,
                                                row(pool_scale_1))
    da0, dz0, dh1, d_nffn0 = _mlp_bwd_x(dh2, a0, g_up0, w_down0, h1, row(norm_ffn_0), "mlp_bwd_x_0")
    dw_up0, dw_down0 = _mlp_bwd_w(nf0, da0, a0, dz0, slot_cols, "mlp_bwd_w_0")
    dw_pool = (dw_pool.reshape(pool_w.shape[0], N_CHIPS, -1, pool_cols).transpose(1, 0, 2, 3)
               .reshape(N_CHIPS, -1, pool_cols))
    scatter_0 = _scatter_start([dw_up0, dw_down0.reshape(N_CHIPS, -1, d), dw_pool], core, "mlp0")
    do, delta, dbcx, dw_out, d_conv = _conv_out_bwd(scatter_0[-1], dh1, w_out, o, bcx, conv_w)
    dqa, dka, dv = _attn_bwd(qa, ka, qkv, do, lse, delta)
    dqkv, dfl, d_bf = _gate_bwd(dqa, dka, dv, fl, bf)
    dw_qkv, dw_f, dw_bcx = _wgrad_in(n0, [dqkv, dfl, dbcx])
    dw_in = jnp.concatenate([dw_qkv, dw_f[:, :N_HEADS], dw_bcx], axis=1)
    scatter_m = _scatter_start([dw_in.reshape(d, N_CHIPS, -1).transpose(1, 0, 2),
                                dw_out.reshape(N_CHIPS, -1, d)], core, "mixer")
    grad_x, d_nmix0 = _in_proj_bwd(scatter_m[-1], dqkv, dfl, dbcx, w_qkv, w_f, w_bcx, xs,
                                   row(norm_mix_0), dh1)

    h_up1, h_down1 = _scatter_finish(scatter_1, chip_index, grad_x, "mlp1")
    h_up0, h_down0, h_pool = _scatter_finish(scatter_0, chip_index, grad_x, "mlp0")
    h_in, h_out = _scatter_finish(scatter_m, chip_index, grad_x, "mixer")
    reduced = _share_halves([h_in, h_out, h_up0, h_down0, h_pool, h_up1, h_down1])
    moments = [(m_w_in_0, v_w_in_0), (m_w_out_0, v_w_out_0), (m_w_up_0, v_w_up_0),
               (m_w_down_0, v_w_down_0), (m_pool_w_1, v_pool_w_1), (m_w_up_1, v_w_up_1),
               (m_w_down_1, v_w_down_1)]
    big_out = []
    for k, (w, g, (m, v)) in enumerate(zip(big, reduced, moments)):
        flat = lambda t: t.reshape(-1, t.shape[-1])
        delta_w, new_m, new_v = _adamw(flat(w), flat(g), flat(m), flat(v), "adamw_%d" % k)
        big_out.append((g.reshape(w.shape), delta_w.reshape(w.shape), new_m.reshape(w.shape),
                        new_v.reshape(w.shape)))

    tail = jnp.concatenate([d_conv[0:3].reshape(-1)[d:], d_bf[0, :N_HEADS], loss_part[0, :1]])
    small_part = jnp.concatenate(
        [d_nmix0, d_nffn0, d_nmix1, d_pscale, d_nffn1, d_final,
         d_conv[0:3].reshape(1, -1)[:, :d],
         jnp.pad(tail, (0, d - tail.shape[0])).reshape(1, d)], axis=0)
    parts = _gather_small(small_part)

    chip = 2 * lax.axis_index("x") + lax.axis_index("y")
    cw_cols = conv_w_0.shape[1]

    def conv_block(full):
        mine = lax.dynamic_slice_in_dim(full, chip * cw_cols, cw_cols, axis=1)
        return jnp.pad(mine.reshape(-1), (0, d - mine.size))

    def small_rows(vals, cw, bfv):
        return jnp.stack(list(vals) + [cw, jnp.pad(bfv, (0, d - N_HEADS))])

    smalls_w = [norm_mix_0, norm_ffn_0, norm_mix_1, pool_scale_1, norm_ffn_1, final_norm]
    smalls_m = [m_norm_mix_0, m_norm_ffn_0, m_norm_mix_1, m_pool_scale_1, m_norm_ffn_1, m_final_norm]
    smalls_v = [v_norm_mix_0, v_norm_ffn_0, v_norm_mix_1, v_pool_scale_1, v_norm_ffn_1, v_final_norm]
    pad_cw = lambda t: jnp.pad(t.reshape(-1), (0, d - t.size))
    w_rows = small_rows(smalls_w, pad_cw(conv_w_0), b_f_0)
    m_rows = small_rows(smalls_m, pad_cw(m_conv_w_0), m_b_f_0)
    v_rows = small_rows(smalls_v, pad_cw(v_conv_w_0), v_b_f_0)

    g_sum = _sum_devices(parts)
    conv_full = jnp.concatenate([g_sum[6], g_sum[7, :3 * c_conv - d]]).reshape(3, c_conv)
    bf_grad = g_sum[7, 3 * c_conv - d:3 * c_conv - d + N_HEADS]
    loss = g_sum[7, 3 * c_conv - d + N_HEADS]
    g_rows = jnp.concatenate(
        [g_sum[0:6], conv_block(conv_full).reshape(1, d),
         jnp.pad(bf_grad, (0, d - N_HEADS)).reshape(1, d)], axis=0)
    d_rows, nm_rows, nv_rows = _adamw(w_rows, g_rows, m_rows, v_rows, "adamw_small")

    def unpack(rows):
        cw = rows[6, :conv_w_0.size].reshape(conv_w_0.shape)
        return [rows[0], rows[1], rows[2], rows[3], rows[4], rows[5], cw, rows[7, :N_HEADS]]

    def assemble(kind):
        sm = unpack([g_rows, d_rows, nm_rows, nv_rows][kind])
        lg = [t[kind] for t in big_out]
        return [sm[0], lg[0], sm[7], sm[6], lg[1], sm[1], lg[2], lg[3],
                sm[2], lg[4], sm[3], sm[4], lg[5], lg[6], sm[5]]

    return (loss, grad_x[None], *assemble(0), *assemble(1), *assemble(2), *assemble(3))
```

```python
import functools

import jax
import jax.numpy as jnp
from jax import lax
from jax.experimental import pallas as pl
from jax.experimental.pallas import tpu as pltpu

F32 = jnp.float32
BF16 = jnp.bfloat16

RMS_EPS = 1e-6
HEAD_DIM = 64
N_HEADS = 8
ATTN_SCALE = HEAD_DIM ** -0.5
LOG2_E = 1.4426950408889634
POOL_WINDOWS = (2, 4, 8, 16)
POOL_HALO = 16
CONV_HALO = 8
NEG_BIG = -1e30

ADAM_LR = 0.001
ADAM_B1 = 0.9
ADAM_B2 = 0.999
ADAM_EPS = 1e-08
ADAM_WD = 0.01
ADAM_STEP = 10

N_CHIPS = 4
N_DEV = 8
MESH = pl.DeviceIdType.MESH

VMEM_LIMIT_BYTES = 56 * 1024 * 1024

TILE_ROWS = 512
TILE_ATTN = 512
TILE_MLP_ROWS = 1024
TILE_MLP_FF = 1024
TILE_MLP_BWD_FF = 512
TILE_WGRAD_K = 1024
TILE_WGRAD_N = 1024
TILE_ELEM_ROWS = 256

LANE_CQ = 64
LANE_ONE = 67


def _params(semantics):
    return pltpu.CompilerParams(dimension_semantics=semantics,
                                vmem_limit_bytes=VMEM_LIMIT_BYTES)


def _nn(a, b):
    return lax.dot_general(a, b, (((1,), (0,)), ((), ())), preferred_element_type=F32)


def _nt(a, b):
    return lax.dot_general(a, b, (((1,), (1,)), ((), ())), preferred_element_type=F32)


def _tn(a, b):
    return lax.dot_general(a, b, (((0,), (0,)), ((), ())), preferred_element_type=F32)


def _split3(v):
    hi = v.astype(BF16)
    r1 = v - hi.astype(F32)
    mid = r1.astype(BF16)
    lo = (r1 - mid.astype(F32)).astype(BF16)
    return hi, mid, lo


def _exact_nn(sel, v):
    hi, mid, lo = _split3(v)
    return _nn(sel, hi) + _nn(sel, mid) + _nn(sel, lo)


def _exact_nt(sel, v):
    hi, mid, lo = _split3(v)
    return _nt(sel, hi) + _nt(sel, mid) + _nt(sel, lo)


def _rms_fwd(x, g):
    r = lax.rsqrt(jnp.mean(x * x, axis=-1, keepdims=True) + RMS_EPS)
    return x * r * g, r


def _rms_bwd(dn, x, g):
    r = lax.rsqrt(jnp.mean(x * x, axis=-1, keepdims=True) + RMS_EPS)
    xh = x * r
    gy = dn * g
    dx = r * (gy - xh * jnp.mean(gy * xh, axis=-1, keepdims=True))
    return dx, jnp.sum(dn * xh, axis=0, keepdims=True)


def _lane(shape):
    return lax.broadcasted_iota(jnp.int32, shape, len(shape) - 1)


def _row(shape):
    return lax.broadcasted_iota(jnp.int32, shape, len(shape) - 2)


def _full(a):
    nd = a.ndim
    return pl.BlockSpec(a.shape, lambda *_: (0,) * nd)


def _ln_proj(x, g, w_qkv, w_f, w_bcx):
    s, d = x.shape
    tm = min(TILE_ROWS, s)

    def body(x_ref, g_ref, wq_ref, wf_ref, wb_ref, n_ref, qkv_ref, fl_ref, bcx_ref):
        n, _ = _rms_fwd(x_ref[...], g_ref[...])
        nb = n.astype(BF16)
        n_ref[...] = nb
        qkv_ref[...] = _nn(nb, wq_ref[...]).astype(BF16)
        fl_ref[...] = _nn(nb, wf_ref[...])
        bcx_ref[...] = _nn(nb, wb_ref[...])

    rows = lambda c: pl.BlockSpec((tm, c), lambda i: (i, 0))
    return pl.pallas_call(
        body, name="ln_proj", grid=(s // tm,),
        in_specs=[rows(d), _full(g), _full(w_qkv), _full(w_f), _full(w_bcx)],
        out_specs=[rows(d), rows(w_qkv.shape[1]), rows(w_f.shape[1]), rows(w_bcx.shape[1])],
        out_shape=[jax.ShapeDtypeStruct((s, d), BF16),
                   jax.ShapeDtypeStruct((s, w_qkv.shape[1]), BF16),
                   jax.ShapeDtypeStruct((s, w_f.shape[1]), F32),
                   jax.ShapeDtypeStruct((s, w_bcx.shape[1]), F32)],
        compiler_params=_params(("parallel",)),
    )(x, g, w_qkv, w_f, w_bcx)


def _gate_prep(fl, bf, qkv):
    s = fl.shape[0]
    a = N_HEADS * HEAD_DIM
    tm = min(TILE_ROWS, s)

    def body(fl_ref, bf_ref, q_ref, k_ref, qa_ref, ka_ref, carry_ref):
        i = pl.program_id(0)

        @pl.when(i == 0)
        def _():
            carry_ref[...] = jnp.zeros_like(carry_ref)

        z = fl_ref[...] + bf_ref[...]
        logf = jnp.minimum(z, 0.0) - jnp.log(1.0 + jnp.exp(-jnp.abs(z)))
        lower = (_lane((tm, tm)) <= _row((tm, tm))).astype(BF16)
        cum = _exact_nn(lower, logf) + carry_ref[0:1, :]
        carry_ref[0:1, :] = cum[tm - 1:tm, :]

        lane = _lane((tm, 128))
        for h in range(N_HEADS):
            cb = LOG2_E * jnp.sum(jnp.where(lane == h, cum, 0.0), axis=1, keepdims=True)
            hi, mid, lo = (p.astype(F32) for p in _split3(cb))
            pair = slice((h // 2) * 128, (h // 2 + 1) * 128)
            qp = q_ref[:, pair].astype(F32)
            kp = k_ref[:, pair].astype(F32)
            if h % 2:
                qp = pltpu.roll(qp, HEAD_DIM, axis=1)
                kp = pltpu.roll(kp, HEAD_DIM, axis=1)
            q_bias = jnp.where(lane == LANE_CQ, hi,
                               jnp.where(lane == LANE_CQ + 1, mid,
                                         jnp.where(lane == LANE_CQ + 2, lo,
                                                   jnp.where(lane < LANE_ONE + 3, 1.0, 0.0))))
            k_bias = jnp.where(lane < LANE_ONE, 1.0,
                               jnp.where(lane == LANE_ONE, -hi,
                                         jnp.where(lane == LANE_ONE + 1, -mid,
                                                   jnp.where(lane == LANE_ONE + 2, -lo, 0.0))))
            qa_ref[h] = jnp.where(lane < HEAD_DIM, qp * (ATTN_SCALE * LOG2_E), q_bias).astype(BF16)
            ka_ref[h] = jnp.where(lane < HEAD_DIM, kp, k_bias).astype(BF16)

    aug = jax.ShapeDtypeStruct((N_HEADS, s, 128), BF16)
    aug_spec = pl.BlockSpec((N_HEADS, tm, 128), lambda i: (0, i, 0))
    return pl.pallas_call(
        body, name="gate_prep", grid=(s // tm,),
        in_specs=[pl.BlockSpec((tm, 128), lambda i: (i, 0)), _full(bf),
                  pl.BlockSpec((tm, a), lambda i: (i, 0)),
                  pl.BlockSpec((tm, a), lambda i: (i, 1))],
        out_specs=[aug_spec, aug_spec],
        out_shape=[aug, aug],
        scratch_shapes=[pltpu.VMEM((8, 128), F32)],
        compiler_params=_params(("arbitrary",)),
    )(fl, bf, qkv, qkv)


def _attn_fwd(qa, ka, qkv):
    s = qa.shape[1]
    a = N_HEADS * HEAD_DIM
    t = min(TILE_ATTN, s)
    n_pairs = N_HEADS // 2
    v_block0 = 2 * a // 128

    ones_lane = (HEAD_DIM, 0)

    def body(qa_ref, ka_ref, v_ref, o_ref, lse_ref, m_ref, acc_ref, s_even, s_odd):
        i = pl.program_id(1)
        m_ref[...] = jnp.full_like(m_ref, NEG_BIG)
        acc_ref[...] = jnp.zeros_like(acc_ref)
        upper_rows = _row((128, t)) < HEAD_DIM

        def keys(j):
            return pl.ds(pl.multiple_of(j * t, t), t)

        def scores_into(buf, j):
            for e in range(2):
                buf[e] = _nt(ka_ref[e, keys(j), :], qa_ref[e])

        def consume(buf, j, masked):
            vf = v_ref[keys(j), :].astype(F32)
            lane = _lane((t, 128))
            own = [lane < HEAD_DIM, lane >= HEAD_DIM]
            for e in range(2):
                v_head = jnp.where(own[e], vf, jnp.where(lane == ones_lane[e], 1.0, 0.0)).astype(BF16)
                sc = buf[e]
                if masked:
                    sc = jnp.where(_row((t, t)) <= _lane((t, t)), sc, NEG_BIG)
                m_prev = m_ref[e]
                m_new = jnp.maximum(m_prev, jnp.max(sc, axis=0, keepdims=True))
                p = jnp.exp2(sc - m_new).astype(BF16)
                acc_ref[e] = acc_ref[e] * jnp.exp2(m_prev - m_new) + _tn(v_head, p)
                m_ref[e] = m_new

        scores_into(s_even, 0)

        def two_tiles(p, carry):
            j = 2 * p
            scores_into(s_odd, j + 1)
            consume(s_even, j, False)
            scores_into(s_even, j + 2)
            consume(s_odd, j + 1, False)
            return carry

        lax.fori_loop(0, i // 2, two_tiles, 0)

        @pl.when(i % 2 == 0)
        def _():
            consume(s_even, i, True)

        @pl.when(i % 2 == 1)
        def _():
            scores_into(s_odd, i)
            consume(s_even, i - 1, False)
            consume(s_odd, i, True)

        denom = [acc_ref[e, ones_lane[e]:ones_lane[e] + 1, :] for e in range(2)]
        out_t = jnp.where(upper_rows, acc_ref[0] / denom[0], acc_ref[1] / denom[1])
        o_ref[...] = out_t.T.astype(BF16)
        lse = [m_ref[e] + LOG2_E * jnp.log(denom[e]) for e in range(2)]
        lse_ref[...] = jnp.where(_row((8, t)) == 0, lse[0], lse[1])

    return pl.pallas_call(
        body, name="attn_fwd", grid=(n_pairs, s // t),
        in_specs=[pl.BlockSpec((2, t, 128), lambda g, i: (g, i, 0)),
                  pl.BlockSpec((2, s, 128), lambda g, i: (g, 0, 0)),
                  pl.BlockSpec((s, 128), lambda g, i: (0, v_block0 + g))],
        out_specs=[pl.BlockSpec((t, 128), lambda g, i: (i, g)),
                   pl.BlockSpec((None, 8, t), lambda g, i: (g, 0, i))],
        out_shape=[jax.ShapeDtypeStruct((s, a), BF16),
                   jax.ShapeDtypeStruct((n_pairs, 8, s), F32)],
        scratch_shapes=[pltpu.VMEM((2, 1, t), F32), pltpu.VMEM((2, 128, t), F32),
                        pltpu.VMEM((2, t, t), F32), pltpu.VMEM((2, t, t), F32)],
        compiler_params=_params(("parallel", "arbitrary")),
    )(qa, ka, qkv)


def _conv_out(o, bcx, cw, w_out, x):
    s, d = x.shape
    c = o.shape[1]
    tm = min(TILE_ROWS, s)

    def body(o_ref, b_ref, c_ref, xin_ref, cw_ref, w_ref, x_ref, h_ref, ubuf):
        i = pl.program_id(0)

        @pl.when(i == 0)
        def _():
            ubuf[0:CONV_HALO, :] = jnp.zeros((CONV_HALO, c), F32)

        u = c_ref[...] * xin_ref[...]
        ubuf[CONV_HALO:CONV_HALO + tm, :] = u
        u1 = ubuf[CONV_HALO - 1:CONV_HALO - 1 + tm, :]
        u2 = ubuf[CONV_HALO - 2:CONV_HALO - 2 + tm, :]
        cv = (cw_ref[0:1, :] * u2 + cw_ref[1:2, :] * u1) + cw_ref[2:3, :] * u
        y = (b_ref[...] * cv).astype(BF16)
        mix = _nn(o_ref[...], w_ref[0:c, :]) + _nn(y, w_ref[c:2 * c, :])
        h_ref[...] = x_ref[...] + mix
        ubuf[0:CONV_HALO, :] = u[tm - CONV_HALO:tm, :]

    col = lambda k: pl.BlockSpec((tm, c), lambda i: (i, k))
    return pl.pallas_call(
        body, name="conv_out", grid=(s // tm,),
        in_specs=[col(0), col(0), col(1), col(2), _full(cw), _full(w_out),
                  pl.BlockSpec((tm, d), lambda i: (i, 0))],
        out_specs=pl.BlockSpec((tm, d), lambda i: (i, 0)),
        out_shape=jax.ShapeDtypeStruct((s, d), F32),
        scratch_shapes=[pltpu.VMEM((tm + CONV_HALO, c), F32)],
        compiler_params=_params(("arbitrary",)),
    )(o, bcx, bcx, bcx, cw, w_out, x)


def _mlp_fwd(h, g, w_up, w_down, name):
    s, d = h.shape
    ff = w_down.shape[0]
    slot_cols = w_up.shape[2]
    tm = min(TILE_MLP_ROWS, s)
    tf = min(TILE_MLP_FF, slot_cols)
    per_slot = slot_cols // tf
    nf = ff // tf

    def body(h_ref, g_ref, wu_ref, wd_ref, out_ref, a_ref, n_ref, nb_ref, acc_ref):
        f = pl.program_id(1)

        @pl.when(f == 0)
        def _():
            n, _ = _rms_fwd(h_ref[...], g_ref[...])
            nb = n.astype(BF16)
            nb_ref[...] = nb
            n_ref[...] = nb
            acc_ref[...] = jnp.zeros_like(acc_ref)

        pre = _nn(nb_ref[...], wu_ref[...])
        a_ref[...] = pre.astype(BF16)
        r = jnp.square(jnp.maximum(pre, 0.0)).astype(BF16)
        acc_ref[...] += _nn(r, wd_ref[...])

        @pl.when(f == nf - 1)
        def _():
            out_ref[...] = h_ref[...] + acc_ref[...]

    return pl.pallas_call(
        body, name=name, grid=(s // tm, nf),
        in_specs=[pl.BlockSpec((tm, d), lambda i, f: (i, 0)), _full(g),
                  pl.BlockSpec((None, d, tf), lambda i, f: (f // per_slot, 0, f % per_slot)),
                  pl.BlockSpec((tf, d), lambda i, f: (f, 0))],
        out_specs=[pl.BlockSpec((tm, d), lambda i, f: (i, 0)),
                   pl.BlockSpec((tm, tf), lambda i, f: (i, f)),
                   pl.BlockSpec((tm, d), lambda i, f: (i, 0))],
        out_shape=[jax.ShapeDtypeStruct((s, d), F32),
                   jax.ShapeDtypeStruct((s, ff), BF16),
                   jax.ShapeDtypeStruct((s, d), BF16)],
        scratch_shapes=[pltpu.VMEM((tm, d), BF16), pltpu.VMEM((tm, d), F32)],
        compiler_params=_params(("parallel", "arbitrary")),
    )(h, g, w_up, w_down)


def _window_sum_down(e, window):
    step = 1
    while step < window:
        e = e + pltpu.roll(e, step, axis=0)
        step *= 2
    return e


def _window_sum_up(e, window):
    n = e.shape[0]
    step = 1
    while step < window:
        e = e + pltpu.roll(e, n - step, axis=0)
        step *= 2
    return e


def _pool_counts(first_row, tm, window):
    t = first_row + _row((tm, 1))
    return jnp.minimum(t + 1, window).astype(F32)


def _pool_fwd(h, g, pw, ps):
    s, d = h.shape
    cg = d // len(POOL_WINDOWS)
    tm = min(TILE_ROWS, s)

    def body(h_ref, g_ref, pw_ref, ps_ref, out_ref, nbuf):
        i = pl.program_id(0)

        @pl.when(i == 0)
        def _():
            nbuf[0:POOL_HALO, :] = jnp.zeros((POOL_HALO, d), F32)

        n, _ = _rms_fwd(h_ref[...], g_ref[...])
        nbuf[POOL_HALO:POOL_HALO + tm, :] = n
        for k, window in enumerate(POOL_WINDOWS):
            cols = slice(k * cg, (k + 1) * cg)
            sums = _window_sum_down(nbuf[:, cols], window)[POOL_HALO:, :]
            pooled = sums / _pool_counts(i * tm, tm, window) - n[:, cols]
            y = _nn(pooled.astype(BF16), pw_ref[k]) * ps_ref[:, cols]
            out_ref[:, cols] = h_ref[:, cols] + y
        nbuf[0:POOL_HALO, :] = n[tm - POOL_HALO:tm, :]

    return pl.pallas_call(
        body, name="pool_fwd", grid=(s // tm,),
        in_specs=[pl.BlockSpec((tm, d), lambda i: (i, 0)), _full(g), _full(pw), _full(ps)],
        out_specs=pl.BlockSpec((tm, d), lambda i: (i, 0)),
        out_shape=jax.ShapeDtypeStruct((s, d), F32),
        scratch_shapes=[pltpu.VMEM((tm + POOL_HALO, d), F32)],
        compiler_params=_params(("arbitrary",)),
    )(h, g, pw, ps)


def _final_loss(h, g, target):
    s, d = h.shape
    tm = min(TILE_ROWS, s)

    def body(h_ref, g_ref, t_ref, dh_ref, loss_ref, dg_ref):
        i = pl.program_id(0)
        hv = h_ref[...]
        y, _ = _rms_fwd(hv, g_ref[...])
        err = y - t_ref[...]
        part = 0.5 * jnp.sum(jnp.mean(err * err, axis=-1, keepdims=True), axis=0, keepdims=True)
        dx, dg = _rms_bwd(err / d, hv, g_ref[...])
        dh_ref[...] = dx
        part = jnp.broadcast_to(part, loss_ref.shape)

        @pl.when(i == 0)
        def _():
            loss_ref[...] = part
            dg_ref[...] = dg

        @pl.when(i > 0)
        def _():
            loss_ref[...] += part
            dg_ref[...] += dg

    return pl.pallas_call(
        body, name="final_loss", grid=(s // tm,),
        in_specs=[pl.BlockSpec((tm, d), lambda i: (i, 0)), _full(g),
                  pl.BlockSpec((tm, d), lambda i: (i, 0))],
        out_specs=[pl.BlockSpec((tm, d), lambda i: (i, 0)),
                   pl.BlockSpec((1, 128), lambda i: (0, 0)),
                   pl.BlockSpec((1, d), lambda i: (0, 0))],
        out_shape=[jax.ShapeDtypeStruct((s, d), F32),
                   jax.ShapeDtypeStruct((1, 128), F32),
                   jax.ShapeDtypeStruct((1, d), F32)],
        compiler_params=_params(("arbitrary",)),
    )(h, g, target)


def _mlp_bwd_x(dz, a, w_up, w_down, h_in, g, name):
    s, d = dz.shape
    ff = w_down.shape[0]
    slot_cols = w_up.shape[2]
    tm = min(TILE_MLP_ROWS, s)
    tf = min(TILE_MLP_BWD_FF, slot_cols)
    per_slot = slot_cols // tf
    nf = ff // tf

    def body(dz_ref, a_ref, wu_ref, wd_ref, h_ref, g_ref, da_ref, dzb_ref, dh_ref, dg_ref,
             dzs_ref, acc_ref):
        i = pl.program_id(0)
        f = pl.program_id(1)

        @pl.when(f == 0)
        def _():
            dzb = dz_ref[...].astype(BF16)
            dzs_ref[...] = dzb
            dzb_ref[...] = dzb
            acc_ref[...] = jnp.zeros_like(acc_ref)

        dr = _nt(dzs_ref[...], wd_ref[...])
        da = (dr * (2.0 * jnp.maximum(a_ref[...].astype(F32), 0.0))).astype(BF16)
        da_ref[...] = da
        acc_ref[...] += _nt(da, wu_ref[...])

        @pl.when(f == nf - 1)
        def _():
            dx, dg = _rms_bwd(acc_ref[...], h_ref[...], g_ref[...])
            dh_ref[...] = dz_ref[...] + dx

            @pl.when(i == 0)
            def _():
                dg_ref[...] = dg

            @pl.when(i > 0)
            def _():
                dg_ref[...] += dg

    return pl.pallas_call(
        body, name=name, grid=(s // tm, nf),
        in_specs=[pl.BlockSpec((tm, d), lambda i, f: (i, 0)),
                  pl.BlockSpec((tm, tf), lambda i, f: (i, f)),
                  pl.BlockSpec((None, d, tf), lambda i, f: (f // per_slot, 0, f % per_slot)),
                  pl.BlockSpec((tf, d), lambda i, f: (f, 0)),
                  pl.BlockSpec((tm, d), lambda i, f: (i, 0)), _full(g)],
        out_specs=[pl.BlockSpec((tm, tf), lambda i, f: (i, f)),
                   pl.BlockSpec((tm, d), lambda i, f: (i, 0)),
                   pl.BlockSpec((tm, d), lambda i, f: (i, 0)),
                   pl.BlockSpec((1, d), lambda i, f: (0, 0))],
        out_shape=[jax.ShapeDtypeStruct((s, ff), BF16),
                   jax.ShapeDtypeStruct((s, d), BF16),
                   jax.ShapeDtypeStruct((s, d), F32),
                   jax.ShapeDtypeStruct((1, d), F32)],
        scratch_shapes=[pltpu.VMEM((tm, d), BF16), pltpu.VMEM((tm, d), F32)],
        compiler_params=_params(("arbitrary", "arbitrary")),
    )(dz, a, w_up, w_down, h_in, g)


def _mlp_bwd_w(n, da, a, dzb, slot_cols, name):
    s, d = n.shape
    ff = a.shape[1]
    tn = min(TILE_WGRAD_N, slot_cols)
    tk = min(TILE_WGRAD_K, s)
    per_slot = slot_cols // tn
    nk = s // tk

    def body(n_ref, da_ref, a_ref, dz_ref, du_ref, dd_ref, accu_ref, accd_ref):
        k = pl.program_id(1)

        @pl.when(k == 0)
        def _():
            accu_ref[...] = jnp.zeros_like(accu_ref)
            accd_ref[...] = jnp.zeros_like(accd_ref)

        accu_ref[...] += _tn(n_ref[...], da_ref[...])
        r = jnp.square(jnp.maximum(a_ref[...].astype(F32), 0.0)).astype(BF16)
        accd_ref[...] += _tn(r, dz_ref[...])

        @pl.when(k == nk - 1)
        def _():
            du_ref[...] = accu_ref[...].astype(BF16)
            dd_ref[...] = accd_ref[...].astype(BF16)

    return pl.pallas_call(
        body, name=name, grid=(ff // tn, nk),
        in_specs=[pl.BlockSpec((tk, d), lambda f, k: (k, 0)),
                  pl.BlockSpec((tk, tn), lambda f, k: (k, f)),
                  pl.BlockSpec((tk, tn), lambda f, k: (k, f)),
                  pl.BlockSpec((tk, d), lambda f, k: (k, 0))],
        out_specs=[pl.BlockSpec((None, d, tn), lambda f, k: (f // per_slot, 0, f % per_slot)),
                   pl.BlockSpec((tn, d), lambda f, k: (f, 0))],
        out_shape=[jax.ShapeDtypeStruct((ff // slot_cols, d, slot_cols), BF16),
                   jax.ShapeDtypeStruct((ff, d), BF16)],
        scratch_shapes=[pltpu.VMEM((d, tn), F32), pltpu.VMEM((tn, d), F32)],
        compiler_params=_params(("parallel", "arbitrary")),
    )(n, da, a, dzb)


def _pool_bwd(after, dh, h, g, pw, ps):
    s, d = h.shape
    cg = d // len(POOL_WINDOWS)
    tm = min(TILE_ROWS, s)
    nb = s // tm
    halo_per_tile = tm // POOL_HALO

    def body(after_ref, dh_ref, h_ref, halo_ref, g_ref, pw_ref, ps_ref,
             dx_ref, dpw_ref, dps_ref, dg_ref, nbuf, qbuf, dn_ref, carry, dpw_acc):
        i = pl.program_id(0)
        blk = nb - 1 - i

        @pl.when(i == 0)
        def _():
            carry[...] = jnp.zeros_like(carry)
            dpw_acc[...] = jnp.zeros_like(dpw_acc)
            dps_ref[...] = jnp.zeros_like(dps_ref)
            dg_ref[...] = jnp.zeros_like(dg_ref)

        hv = h_ref[...]
        n, _ = _rms_fwd(hv, g_ref[...])
        nh, _ = _rms_fwd(halo_ref[...], g_ref[...])
        nbuf[0:POOL_HALO, :] = jnp.where(blk == 0, 0.0, nh)
        nbuf[POOL_HALO:POOL_HALO + tm, :] = n
        dhv = dh_ref[...]
        for k, window in enumerate(POOL_WINDOWS):
            cols = slice(k * cg, (k + 1) * cg)
            cnt = _pool_counts(blk * tm, tm, window)
            sums = _window_sum_down(nbuf[:, cols], window)[POOL_HALO:, :]
            pb = (sums / cnt - n[:, cols]).astype(BF16)
            dyk = dhv[:, cols]
            dps_ref[:, cols] += jnp.sum(dyk * _nn(pb, pw_ref[k]), axis=0, keepdims=True)
            dyb = (dyk * ps_ref[:, cols]).astype(BF16)
            dpw_acc[k] += _tn(pb, dyb)
            dpool = _nt(dyb, pw_ref[k])
            qv = dpool / cnt
            qbuf[0:tm, cols] = qv
            qbuf[tm:tm + POOL_HALO, cols] = carry[:, cols]
            dn_ref[:, cols] = _window_sum_up(qbuf[:, cols], window)[0:tm, :] - dpool
            carry[:, cols] = qv[0:POOL_HALO, :]
        dx, dg = _rms_bwd(dn_ref[...], hv, g_ref[...])
        dx_ref[...] = dhv + dx
        dg_ref[...] += dg

        @pl.when(i == nb - 1)
        def _():
            dpw_ref[...] = dpw_acc[...].astype(BF16)

    rev = lambda i: (nb - 1 - i, 0)
    return pl.pallas_call(
        body, name="pool_bwd", grid=(nb,),
        in_specs=[ANY, pl.BlockSpec((tm, d), rev), pl.BlockSpec((tm, d), rev),
                  pl.BlockSpec((POOL_HALO, d),
                               lambda i: (jnp.maximum((nb - 1 - i) * halo_per_tile - 1, 0), 0)),
                  _full(g), _full(pw), _full(ps)],
        out_specs=[pl.BlockSpec((tm, d), rev), _full(pw),
                   pl.BlockSpec((1, d), lambda i: (0, 0)),
                   pl.BlockSpec((1, d), lambda i: (0, 0))],
        out_shape=[jax.ShapeDtypeStruct((s, d), F32),
                   jax.ShapeDtypeStruct(pw.shape, BF16),
                   jax.ShapeDtypeStruct((1, d), F32),
                   jax.ShapeDtypeStruct((1, d), F32)],
        scratch_shapes=[pltpu.VMEM((tm + POOL_HALO, d), F32), pltpu.VMEM((tm + POOL_HALO, d), F32),
                        pltpu.VMEM((tm, d), F32), pltpu.VMEM((POOL_HALO, d), F32),
                        pltpu.VMEM(pw.shape, F32)],
        compiler_params=_params(("arbitrary",)),
    )(after, dh, h, h, g, pw, ps)


def _conv_out_bwd(after, dh, w_out, o, bcx, cw):
    s, d = dh.shape
    c = o.shape[1]
    tm = min(TILE_ROWS, s)
    nb = s // tm
    halo_per_tile = tm // CONV_HALO

    def body(after_ref, dh_ref, w_ref, o_ref, b_ref, c_ref, xin_ref, ch_ref, xh_ref, cw_ref,
             do_ref, delta_ref, dbcx_ref, dw_ref, dcw_ref, ubuf, dbuf, carry, acc):
        i = pl.program_id(0)
        blk = nb - 1 - i

        @pl.when(i == 0)
        def _():
            carry[...] = jnp.zeros_like(carry)
            acc[...] = jnp.zeros_like(acc)
            dcw_ref[...] = jnp.zeros_like(dcw_ref)

        dm = dh_ref[...].astype(BF16)
        dcat = _nt(dm, w_ref[...])
        do = dcat[:, 0:c]
        dy = dcat[:, c:2 * c]
        do_ref[...] = do.astype(BF16)
        head_of_lane = lax.shift_right_logical(_lane((8, c)), HEAD_DIM.bit_length() - 1)
        heads = (head_of_lane == _row((8, c))).astype(BF16)
        delta_ref[...] = _exact_nt(heads, do * o_ref[...].astype(F32))

        cv_ = c_ref[...]
        xin = xin_ref[...]
        bv = b_ref[...]
        u = cv_ * xin
        ubuf[0:CONV_HALO, :] = jnp.where(blk == 0, 0.0, ch_ref[...] * xh_ref[...])
        ubuf[CONV_HALO:CONV_HALO + tm, :] = u
        u1 = ubuf[CONV_HALO - 1:CONV_HALO - 1 + tm, :]
        u2 = ubuf[CONV_HALO - 2:CONV_HALO - 2 + tm, :]
        w0, w1, w2 = cw_ref[0:1, :], cw_ref[1:2, :], cw_ref[2:3, :]
        cv = (w0 * u2 + w1 * u1) + w2 * u
        acc[0:c, :] += _tn(o_ref[...], dm)
        acc[c:2 * c, :] += _tn((bv * cv).astype(BF16), dm)

        dcv = dy * bv
        dcw_ref[0:1, :] += jnp.sum(dcv * u2, axis=0, keepdims=True)
        dcw_ref[1:2, :] += jnp.sum(dcv * u1, axis=0, keepdims=True)
        dcw_ref[2:3, :] += jnp.sum(dcv * u, axis=0, keepdims=True)
        dbuf[0:tm, :] = dcv
        dbuf[tm:tm + CONV_HALO, :] = carry[...]
        du = w2 * dcv + w1 * dbuf[1:1 + tm, :] + w0 * dbuf[2:2 + tm, :]
        dbcx_ref[:, 0:c] = (dy * cv).astype(BF16)
        dbcx_ref[:, c:2 * c] = (du * xin).astype(BF16)
        dbcx_ref[:, 2 * c:3 * c] = (du * cv_).astype(BF16)
        carry[...] = dcv[0:CONV_HALO, :]

        @pl.when(i == nb - 1)
        def _():
            dw_ref[...] = acc[...].astype(BF16)

    rev = lambda k: (lambda i: (nb - 1 - i, k))
    halo = lambda k: (lambda i: (jnp.maximum((nb - 1 - i) * halo_per_tile - 1, 0), k))
    return pl.pallas_call(
        body, name="conv_out_bwd", grid=(nb,),
        in_specs=[ANY, pl.BlockSpec((tm, d), rev(0)), _full(w_out), pl.BlockSpec((tm, c), rev(0)),
                  pl.BlockSpec((tm, c), rev(0)), pl.BlockSpec((tm, c), rev(1)),
                  pl.BlockSpec((tm, c), rev(2)),
                  pl.BlockSpec((CONV_HALO, c), halo(1)), pl.BlockSpec((CONV_HALO, c), halo(2)),
                  _full(cw)],
        out_specs=[pl.BlockSpec((tm, c), rev(0)),
                   pl.BlockSpec((8, tm), lambda i: (0, nb - 1 - i)),
                   pl.BlockSpec((tm, 3 * c), rev(0)),
                   _full(w_out), _full(cw)],
        out_shape=[jax.ShapeDtypeStruct((s, c), BF16),
                   jax.ShapeDtypeStruct((8, s), F32),
                   jax.ShapeDtypeStruct((s, 3 * c), BF16),
                   jax.ShapeDtypeStruct(w_out.shape, BF16),
                   jax.ShapeDtypeStruct(cw.shape, F32)],
        scratch_shapes=[pltpu.VMEM((tm + CONV_HALO, c), F32), pltpu.VMEM((tm + CONV_HALO, c), F32),
                        pltpu.VMEM((CONV_HALO, c), F32), pltpu.VMEM(w_out.shape, F32)],
        compiler_params=_params(("arbitrary",)),
    )(after, dh, w_out, o, bcx, bcx, bcx, bcx, bcx, cw)


def _attn_bwd(qa, ka, qkv, do, lse, delta):
    s = qa.shape[1]
    a = N_HEADS * HEAD_DIM
    t = min(TILE_ATTN, s)
    nq = s // t
    n_pairs = N_HEADS // 2
    v_block0 = 2 * a // 128

    def body(ka_ref, v_ref, qa_ref, do_ref, lse_ref, delta_ref,
             dqt_ref, dka_ref, dv_ref, dk_acc, dv_acc):
        g = pl.program_id(0)
        j = pl.program_id(1)

        @pl.when(j == 0)
        def _():
            dqt_ref[...] = jnp.zeros_like(dqt_ref)

        dk_acc[...] = jnp.zeros_like(dk_acc)
        dv_acc[...] = jnp.zeros_like(dv_acc)
        lane = _lane((t, 128))
        vf = v_ref[...].astype(F32)
        v_heads = [jnp.where(lane < HEAD_DIM, vf, 0.0).astype(BF16),
                   jnp.where(lane >= HEAD_DIM, vf, 0.0).astype(BF16)]
        ke_t = [ka_ref[e].astype(F32).T.astype(BF16) for e in range(2)]

        def q_step(i, masked):
            qs = pl.ds(pl.multiple_of(i * t, t), t)
            dob = do_ref[qs, :]
            for e in range(2):
                qe = qa_ref[e, qs, :]
                sc = _nt(ka_ref[e], qe)
                if masked:
                    sc = jnp.where(_row((t, t)) <= _lane((t, t)), sc, NEG_BIG)
                p = jnp.exp2(sc - lse_ref[pl.ds(e, 1), qs])
                dv_acc[e] += _nn(p.astype(BF16), dob)
                dp = _nt(v_heads[e], dob)
                ds = (p * (dp - delta_ref[pl.ds(2 * g + e, 1), qs])).astype(BF16)
                dk_acc[e] += _nn(ds, qe)
                dqt_ref[e, :, qs] += _nn(ke_t[e], ds)

        q_step(j, True)

        def full_step(i, carry):
            q_step(i, False)
            return carry

        lax.fori_loop(j + 1, nq, full_step, 0)
        dka_ref[...] = dk_acc[...]
        dv_ref[...] = jnp.where(lane < HEAD_DIM, dv_acc[0], dv_acc[1]).astype(BF16)

    return pl.pallas_call(
        body, name="attn_bwd", grid=(n_pairs, nq),
        in_specs=[pl.BlockSpec((2, t, 128), lambda g, j: (g, j, 0)),
                  pl.BlockSpec((t, 128), lambda g, j: (j, v_block0 + g)),
                  pl.BlockSpec((2, s, 128), lambda g, j: (g, 0, 0)),
                  pl.BlockSpec((s, 128), lambda g, j: (0, g)),
                  pl.BlockSpec((None, 8, s), lambda g, j: (g, 0, 0)),
                  pl.BlockSpec((8, s), lambda g, j: (0, 0))],
        out_specs=[pl.BlockSpec((2, 128, s), lambda g, j: (g, 0, 0)),
                   pl.BlockSpec((2, t, 128), lambda g, j: (g, j, 0)),
                   pl.BlockSpec((t, 128), lambda g, j: (j, g))],
        out_shape=[jax.ShapeDtypeStruct((N_HEADS, 128, s), F32),
                   jax.ShapeDtypeStruct((N_HEADS, s, 128), F32),
                   jax.ShapeDtypeStruct((s, a), BF16)],
        scratch_shapes=[pltpu.VMEM((2, t, 128), F32), pltpu.VMEM((2, t, 128), F32)],
        compiler_params=_params(("parallel", "arbitrary")),
    )(ka, qkv, qa, do, lse, delta)


def _gate_bwd(dqa, dka, dv, fl, bf):
    s = fl.shape[0]
    a = N_HEADS * HEAD_DIM
    tm = min(TILE_ROWS, s)
    nb = s // tm

    def body(dqa_ref, dka_ref, dv_ref, fl_ref, bf_ref, dqkv_ref, dfl_ref, dbf_ref, carry):
        i = pl.program_id(0)

        @pl.when(i == 0)
        def _():
            carry[...] = jnp.zeros_like(carry)
            dbf_ref[...] = jnp.zeros_like(dbf_ref)

        lane = _lane((tm, 128))
        dcum = jnp.zeros((tm, 128), F32)
        for pair in range(N_HEADS // 2):
            qs, ks = [], []
            for e in range(2):
                h = 2 * pair + e
                dq = dqa_ref[h].T
                dk = dka_ref[h]
                dc = jnp.sum(jnp.where(lane == LANE_CQ, dq, 0.0)
                             - jnp.where(lane == LANE_ONE, dk, 0.0), axis=1, keepdims=True)
                dcum = jnp.where(lane == h, dc, dcum)
                qs.append(dq * ATTN_SCALE)
                ks.append(dk * (1.0 / LOG2_E))
            cols = slice(pair * 128, (pair + 1) * 128)
            dqkv_ref[:, cols] = jnp.where(
                lane < HEAD_DIM, qs[0], pltpu.roll(qs[1], HEAD_DIM, axis=1)).astype(BF16)
            dqkv_ref[:, a + pair * 128:a + (pair + 1) * 128] = jnp.where(
                lane < HEAD_DIM, ks[0], pltpu.roll(ks[1], HEAD_DIM, axis=1)).astype(BF16)
        dqkv_ref[:, 2 * a:3 * a] = dv_ref[...]

        upper = (_lane((tm, tm)) >= _row((tm, tm))).astype(BF16)
        dlogf = _exact_nn(upper, dcum) + carry[0:1, :]
        carry[0:1, :] = dlogf[0:1, :]
        z = fl_ref[...] + bf_ref[...]
        ez = jnp.exp(-jnp.abs(z))
        sig_neg = jnp.where(z >= 0.0, ez, 1.0) / (1.0 + ez)
        dz = jnp.where(lane < N_HEADS, dlogf * sig_neg, 0.0)
        dfl_ref[...] = dz.astype(BF16)
        dbf_ref[...] += jnp.sum(dz, axis=0, keepdims=True)

    rev3 = lambda i: (0, nb - 1 - i, 0)
    rev = lambda i: (nb - 1 - i, 0)
    return pl.pallas_call(
        body, name="gate_bwd", grid=(nb,),
        in_specs=[pl.BlockSpec((N_HEADS, 128, tm), lambda i: (0, 0, nb - 1 - i)),
                  pl.BlockSpec((N_HEADS, tm, 128), rev3),
                  pl.BlockSpec((tm, a), rev), pl.BlockSpec((tm, 128), rev), _full(bf)],
        out_specs=[pl.BlockSpec((tm, 3 * a), rev), pl.BlockSpec((tm, 128), rev),
                   pl.BlockSpec((1, 128), lambda i: (0, 0))],
        out_shape=[jax.ShapeDtypeStruct((s, 3 * a), BF16),
                   jax.ShapeDtypeStruct((s, 128), BF16),
                   jax.ShapeDtypeStruct((1, 128), F32)],
        scratch_shapes=[pltpu.VMEM((8, 128), F32)],
        compiler_params=_params(("arbitrary",)),
    )(dqa, dka, dv, fl, bf)


def _in_proj_bwd(after, dqkv, dfl, dbcx, w_qkv, w_f, w_bcx, x, g, dh):
    s, d = x.shape
    tm = min(TILE_ROWS, s)

    def body(after_ref, dq_ref, df_ref, db_ref, wq_ref, wf_ref, wb_ref, x_ref, g_ref, dh_ref,
             gx_ref, dg_ref):
        i = pl.program_id(0)
        dn = (_nt(dq_ref[...], wq_ref[...]) + _nt(df_ref[...], wf_ref[...])
              + _nt(db_ref[...], wb_ref[...]))
        dx, dg = _rms_bwd(dn, x_ref[...], g_ref[...])
        gx_ref[...] = dh_ref[...] + dx

        @pl.when(i == 0)
        def _():
            dg_ref[...] = dg

        @pl.when(i > 0)
        def _():
            dg_ref[...] += dg

    rows = lambda c: pl.BlockSpec((tm, c), lambda i: (i, 0))
    return pl.pallas_call(
        body, name="in_proj_bwd", grid=(s // tm,),
        in_specs=[ANY, rows(dqkv.shape[1]), rows(dfl.shape[1]), rows(dbcx.shape[1]),
                  _full(w_qkv), _full(w_f), _full(w_bcx), rows(d), _full(g), rows(d)],
        out_specs=[rows(d), pl.BlockSpec((1, d), lambda i: (0, 0))],
        out_shape=[jax.ShapeDtypeStruct((s, d), F32), jax.ShapeDtypeStruct((1, d), F32)],
        compiler_params=_params(("arbitrary",)),
    )(after, dqkv, dfl, dbcx, w_qkv, w_f, w_bcx, x, g, dh)


def _wgrad_in(n, dys):
    s, d = n.shape
    m = len(dys)
    tk = min(TILE_ROWS, s)
    nk = s // tk

    def body(*refs):
        n_ref, dy_refs, dw_refs, accs = refs[0], refs[1:1 + m], refs[1 + m:1 + 2 * m], refs[1 + 2 * m:]
        k = pl.program_id(0)

        @pl.when(k == 0)
        def _():
            for acc in accs:
                acc[...] = jnp.zeros_like(acc)

        nb = n_ref[...]
        for dy_ref, acc in zip(dy_refs, accs):
            acc[...] += _tn(nb, dy_ref[...])

        @pl.when(k == nk - 1)
        def _():
            for dw_ref, acc in zip(dw_refs, accs):
                dw_ref[...] = acc[...].astype(BF16)

    return pl.pallas_call(
        body, name="wgrad_in", grid=(nk,),
        in_specs=[pl.BlockSpec((tk, d), lambda k: (k, 0))]
        + [pl.BlockSpec((tk, dy.shape[1]), lambda k: (k, 0)) for dy in dys],
        out_specs=[pl.BlockSpec((d, dy.shape[1]), lambda k: (0, 0)) for dy in dys],
        out_shape=[jax.ShapeDtypeStruct((d, dy.shape[1]), BF16) for dy in dys],
        scratch_shapes=[pltpu.VMEM((d, dy.shape[1]), F32) for dy in dys],
        compiler_params=_params(("arbitrary",)),
    )(n, *dys)


def _row_tile(rows):
    t = min(TILE_ELEM_ROWS, rows)
    while rows % t:
        t //= 2
    return t


def _sum_pair(grad, theirs, core, name):
    slots, rows, cols = theirs.shape
    tr = _row_tile(rows)
    nb = rows // tr

    def body(core_ref, a_ref, b_ref, o_ref):
        o_ref[...] = (a_ref[...].astype(F32) + b_ref[...].astype(F32)).astype(BF16)

    spec = pl.BlockSpec((None, tr, cols), lambda s, i, core_ref: (s, i, 0))
    return pl.pallas_call(
        body, name=name,
        grid_spec=pltpu.PrefetchScalarGridSpec(
            num_scalar_prefetch=1, grid=(slots, nb),
            in_specs=[pl.BlockSpec((None, tr, cols),
                                   lambda s, i, core_ref: (s, core_ref[0] * nb + i, 0)), spec],
            out_specs=spec),
        out_shape=jax.ShapeDtypeStruct(theirs.shape, BF16),
        compiler_params=_params(("parallel", "parallel")),
    )(core, grad, theirs)


def _sum_chips(sums, others, chip, name):
    _, rows, cols = sums.shape
    tr = _row_tile(rows)

    def body(chip_ref, a_ref, b_ref, o_ref):
        acc = a_ref[...].astype(F32)
        for k in range(N_CHIPS - 1):
            acc = acc + b_ref[k].astype(F32)
        o_ref[...] = acc

    return pl.pallas_call(
        body, name=name,
        grid_spec=pltpu.PrefetchScalarGridSpec(
            num_scalar_prefetch=1, grid=(rows // tr,),
            in_specs=[pl.BlockSpec((None, tr, cols), lambda i, chip_ref: (chip_ref[0], i, 0)),
                      pl.BlockSpec((N_CHIPS - 1, tr, cols), lambda i, chip_ref: (0, i, 0))],
            out_specs=pl.BlockSpec((tr, cols), lambda i, chip_ref: (i, 0))),
        out_shape=jax.ShapeDtypeStruct((rows, cols), F32),
        compiler_params=_params(("parallel",)),
    )(chip, sums, others)


def _adamw_math(w, g, m, v):
    m = ADAM_B1 * m + (1.0 - ADAM_B1) * g
    v = ADAM_B2 * v + (1.0 - ADAM_B2) * jnp.square(g)
    m_hat = m / (1.0 - ADAM_B1 ** ADAM_STEP)
    v_hat = v / (1.0 - ADAM_B2 ** ADAM_STEP)
    delta = -ADAM_LR * (m_hat / (jnp.sqrt(v_hat) + ADAM_EPS) + ADAM_WD * w)
    return delta, m, v


def _adamw(w, g, m, v, name):
    rows, cols = w.shape
    tr = _row_tile(rows)

    def body(w_ref, g_ref, m_ref, v_ref, d_ref, nm_ref, nv_ref):
        delta, nm, nv = _adamw_math(w_ref[...], g_ref[...], m_ref[...], v_ref[...])
        d_ref[...] = delta
        nm_ref[...] = nm
        nv_ref[...] = nv

    spec = pl.BlockSpec((tr, cols), lambda i: (i, 0))
    out = jax.ShapeDtypeStruct(w.shape, F32)
    return pl.pallas_call(
        body, name=name, grid=(rows // tr,), in_specs=[spec] * 4, out_specs=[spec] * 3,
        out_shape=[out, out, out], compiler_params=_params(("parallel",)),
    )(w, g, m, v)


def _sum_devices(parts):
    def body(p_ref, g_ref):
        g = p_ref[0]
        for k in range(1, N_DEV):
            g = g + p_ref[k]
        g_ref[...] = g

    return pl.pallas_call(
        body, name="sum_devices",
        in_specs=[pl.BlockSpec(memory_space=pltpu.VMEM)],
        out_specs=pl.BlockSpec(memory_space=pltpu.VMEM),
        out_shape=jax.ShapeDtypeStruct(parts.shape[1:], F32),
    )(parts)


def _mesh_position():
    x, y, c = lax.axis_index("x"), lax.axis_index("y"), lax.axis_index("c")
    chips = [(1 - x, y), (x, 1 - y), (1 - x, 1 - y)]
    return x, y, c, chips


ANY = pl.BlockSpec(memory_space=pl.ANY)
HBM = pl.BlockSpec(memory_space=pltpu.HBM)
SEM = pl.BlockSpec(memory_space=pltpu.SEMAPHORE)
SPLIT_COPY_EFFECT = pltpu.SideEffectType.DATAFLOW_SIDE_EFFECTING


def _in_hbm(a):
    return pltpu.with_memory_space_constraint(a, pltpu.HBM)


def _chip_copies(views, srcs, lands, send, recv):
    _, _, c, chips = _mesh_position()
    cps = []
    for a in range(len(srcs)):
        for k, (px, py) in enumerate(chips):
            src, dst = views(a, k, srcs[a], lands[a], c, 2 * px + py)
            sem = a * (N_CHIPS - 1) + k
            cps.append(pltpu.make_async_remote_copy(
                src_ref=src, dst_ref=dst, send_sem=send.at[sem], recv_sem=recv.at[sem],
                device_id=(px, py, c), device_id_type=MESH))
    return cps


def _ici_start(sources, land_shapes, views, after, name):
    n = len(sources)

    def body(*refs):
        srcs, lands = refs[:n], refs[n:2 * n]
        send, recv = refs[2 * n + 1], refs[2 * n + 2]
        token = refs[-1]
        for cp in _chip_copies(views, srcs, lands, send, recv):
            cp.start()
        token[...] = jnp.zeros_like(token)

    lands = [_in_hbm(lax.empty(s.shape, s.dtype)) for s in land_shapes]
    outs = pl.pallas_call(
        body, name=name,
        in_specs=[HBM] * (2 * n) + [ANY],
        out_specs=[SEM, SEM] + [HBM] * (2 * n) + [pl.BlockSpec(memory_space=pltpu.VMEM)],
        out_shape=[pltpu.SemaphoreType.DMA((n * (N_CHIPS - 1),))] * 2
        + [pltpu.HBM(a.shape, a.dtype) for a in sources]
        + [pltpu.HBM(s.shape, s.dtype) for s in land_shapes]
        + [jax.ShapeDtypeStruct((8, 128), F32)],
        input_output_aliases={i: 2 + i for i in range(2 * n)},
        compiler_params=pltpu.CompilerParams(has_side_effects=SPLIT_COPY_EFFECT),
    )(*[_in_hbm(a) for a in sources], *lands, after)
    return outs[0], outs[1], list(outs[2:2 + n]), list(outs[2 + n:2 + 2 * n]), outs[-1]


def _ici_wait(handle, views, after, name):
    send, recv, srcs, lands, _ = handle
    n = len(srcs)

    def body(*refs):
        src_refs, land_refs = refs[:n], refs[n:2 * n]
        for cp in _chip_copies(views, src_refs, land_refs, refs[2 * n], refs[2 * n + 1]):
            cp.wait_send()
            cp.wait_recv()

    outs = pl.pallas_call(
        body, name=name,
        in_specs=[HBM] * (2 * n) + [SEM, SEM, ANY],
        out_specs=[HBM] * (2 * n),
        out_shape=[pltpu.HBM(a.shape, a.dtype) for a in srcs]
        + [pltpu.HBM(a.shape, a.dtype) for a in lands],
        input_output_aliases={i: i for i in range(2 * n)},
        compiler_params=pltpu.CompilerParams(has_side_effects=SPLIT_COPY_EFFECT),
    )(*srcs, *lands, send, recv, after)
    return list(outs[:n]), list(outs[n:])


def _gather_views(split):
    def views(a, k, src, land, c, slot):
        if split[a]:
            half = src.shape[0] // 2
            src = src.at[pl.ds(c * half, half)]
        return src, land.at[k]
    return views


def _scatter_views(a, k, src, land, c, slot):
    return src.at[slot], land.at[k]


def _gather_land_shapes(shards, split):
    return [jax.ShapeDtypeStruct(
        (N_CHIPS - 1, a.shape[0] // 2 if sp else a.shape[0]) + a.shape[1:], a.dtype)
        for a, sp in zip(shards, split)]


def _gather_finish(shards, lands, split, name):
    n = len(shards)
    ns = sum(split)
    d_index = {a: i for i, a in enumerate(a for a in range(n) if split[a])}

    def body(*refs):
        shard, land, outs = refs[:n], refs[n:2 * n], refs[2 * n:3 * n]
        obuf, fbuf = refs[3 * n:4 * n], refs[4 * n:5 * n]
        dbuf = refs[5 * n:5 * n + ns]
        ld_own, st_own, ld, st_mine, st_sib, send, recv = refs[5 * n + ns:]
        x, y, c, chips = _mesh_position()
        me = 2 * x + y
        own_loads, loads, sends, pending = [], {}, [], []
        for a in range(n):
            cp = pltpu.make_async_copy(shard[a], obuf[a], ld_own.at[a])
            cp.start()
            own_loads.append(cp)
        for a in range(n):
            for k in range(N_CHIPS - 1):
                cp = pltpu.make_async_copy(land[a].at[k], fbuf[a].at[k], ld.at[a, k])
                cp.start()
                loads[a, k] = cp
        for a in range(n):
            own_loads[a].wait()
            cp = pltpu.make_async_copy(obuf[a], outs[a].at[me], st_own.at[a])
            cp.start()
            pending.append(cp)
        for a in range(n):
            rows = shard[a].shape[0]
            for k, (px, py) in enumerate(chips):
                loads[a, k].wait()
                part = pl.ds(c * (rows // 2), rows // 2) if split[a] else pl.ds(0, rows)
                cp = pltpu.make_async_copy(fbuf[a].at[k], outs[a].at[2 * px + py, part],
                                           st_mine.at[a, k])
                cp.start()
                pending.append(cp)
                if split[a]:
                    fw = pltpu.make_async_remote_copy(
                        src_ref=fbuf[a].at[k], dst_ref=dbuf[d_index[a]].at[k],
                        send_sem=send.at[a, k], recv_sem=recv.at[a, k],
                        device_id=(x, y, 1 - c), device_id_type=MESH)
                    fw.start()
                    sends.append((a, k, fw))
        for a, k, fw in sends:
            px, py = chips[k]
            half = shard[a].shape[0] // 2
            fw.wait_recv()
            cp = pltpu.make_async_copy(dbuf[d_index[a]].at[k],
                                       outs[a].at[2 * px + py, pl.ds((1 - c) * half, half)],
                                       st_sib.at[a, k])
            cp.start()
            pending.append(cp)
        for _, _, fw in sends:
            fw.wait_send()
        for cp in pending:
            cp.wait()

    stage = [pltpu.VMEM(a.shape, a.dtype) for a in lands]
    dma = lambda *shape: pltpu.SemaphoreType.DMA(shape)
    return pl.pallas_call(
        body, name=name,
        in_specs=[ANY] * (2 * n), out_specs=[ANY] * n,
        out_shape=[jax.ShapeDtypeStruct((N_CHIPS,) + a.shape, a.dtype) for a in shards],
        scratch_shapes=[pltpu.VMEM(a.shape, a.dtype) for a in shards] + stage
        + [s for s, sp in zip(stage, split) if sp]
        + [dma(n), dma(n), dma(n, 3), dma(n, 3), dma(n, 3), dma(n, 3), dma(n, 3)],
        compiler_params=pltpu.CompilerParams(vmem_limit_bytes=VMEM_LIMIT_BYTES),
    )(*shards, *lands)


def _exchange_siblings(grads, name):
    n = len(grads)

    def body(*refs):
        ins, theirs = refs[:n], refs[n:2 * n]
        sbuf, rbuf = refs[2 * n:3 * n], refs[3 * n:4 * n]
        ld, st, send, recv = refs[4 * n:]
        x, y, c, _ = _mesh_position()
        loads, sends, stores = [], [], []
        for a in range(n):
            half = ins[a].shape[1] // 2
            cp = pltpu.make_async_copy(ins[a].at[:, pl.ds((1 - c) * half, half)], sbuf[a], ld.at[a])
            cp.start()
            loads.append(cp)
        for a in range(n):
            loads[a].wait()
            rc = pltpu.make_async_remote_copy(
                src_ref=sbuf[a], dst_ref=rbuf[a], send_sem=send.at[a], recv_sem=recv.at[a],
                device_id=(x, y, 1 - c), device_id_type=MESH)
            rc.start()
            sends.append(rc)
        for a in range(n):
            sends[a].wait_recv()
            cp = pltpu.make_async_copy(rbuf[a], theirs[a], st.at[a])
            cp.start()
            stores.append(cp)
        for a in range(n):
            sends[a].wait_send()
            stores[a].wait()

    half_shape = lambda a: (a.shape[0], a.shape[1] // 2, a.shape[2])
    stage = [pltpu.VMEM(half_shape(a), a.dtype) for a in grads]
    return pl.pallas_call(
        body, name=name,
        in_specs=[ANY] * n, out_specs=[ANY] * n,
        out_shape=[jax.ShapeDtypeStruct(half_shape(a), a.dtype) for a in grads],
        scratch_shapes=stage + stage + [pltpu.SemaphoreType.DMA((n,))] * 4,
        compiler_params=pltpu.CompilerParams(vmem_limit_bytes=VMEM_LIMIT_BYTES),
    )(*grads)


def _share_halves(halves):
    n = len(halves)

    def body(*refs):
        ins, outs = refs[:n], refs[n:2 * n]
        sbuf, rbuf = refs[2 * n:3 * n], refs[3 * n:4 * n]
        ld, st_own, st_sib, send, recv = refs[4 * n:]
        x, y, c, _ = _mesh_position()
        loads, sends, stores = [], [], []
        for a in range(n):
            cp = pltpu.make_async_copy(ins[a], sbuf[a], ld.at[a])
            cp.start()
            loads.append(cp)
        for a in range(n):
            half = ins[a].shape[0]
            loads[a].wait()
            rc = pltpu.make_async_remote_copy(
                src_ref=sbuf[a], dst_ref=rbuf[a], send_sem=send.at[a], recv_sem=recv.at[a],
                device_id=(x, y, 1 - c), device_id_type=MESH)
            rc.start()
            sends.append(rc)
            cp = pltpu.make_async_copy(sbuf[a], outs[a].at[pl.ds(c * half, half)], st_own.at[a])
            cp.start()
            stores.append(cp)
        for a in range(n):
            half = ins[a].shape[0]
            sends[a].wait_recv()
            cp = pltpu.make_async_copy(rbuf[a], outs[a].at[pl.ds((1 - c) * half, half)], st_sib.at[a])
            cp.start()
            stores.append(cp)
        for cp in sends:
            cp.wait_send()
        for cp in stores:
            cp.wait()

    stage = [pltpu.VMEM(a.shape, a.dtype) for a in halves]
    return pl.pallas_call(
        body, name="share_halves",
        in_specs=[ANY] * n, out_specs=[ANY] * n,
        out_shape=[jax.ShapeDtypeStruct((2 * a.shape[0],) + a.shape[1:], a.dtype)
                   for a in halves],
        scratch_shapes=stage + stage + [pltpu.SemaphoreType.DMA((n,))] * 5,
        compiler_params=pltpu.CompilerParams(vmem_limit_bytes=VMEM_LIMIT_BYTES),
    )(*halves)


def _gather_small(part):
    def body(in_ref, out_ref, send, recv, local):
        x, y, c, _ = _mesh_position()
        me = 4 * x + 2 * y + c
        cps = [pltpu.make_async_copy(in_ref, out_ref.at[me], local)]
        k = 0
        for fx in range(2):
            for fy in range(2):
                for fc in range(2):
                    if fx or fy or fc:
                        cps.append(pltpu.make_async_remote_copy(
                            src_ref=in_ref, dst_ref=out_ref.at[me], send_sem=send.at[k],
                            recv_sem=recv.at[k], device_id=(x ^ fx, y ^ fy, c ^ fc),
                            device_id_type=MESH))
                        k += 1
        for cp in cps:
            cp.start()
        for cp in cps:
            cp.wait()

    return pl.pallas_call(
        body, name="gather_small",
        in_specs=[pl.BlockSpec(memory_space=pltpu.VMEM)],
        out_specs=pl.BlockSpec(memory_space=pltpu.VMEM),
        out_shape=jax.ShapeDtypeStruct((N_DEV,) + part.shape, part.dtype),
        scratch_shapes=[pltpu.SemaphoreType.DMA((N_DEV - 1,)), pltpu.SemaphoreType.DMA((N_DEV - 1,)),
                        pltpu.SemaphoreType.DMA],
    )(part)


def _scatter_start(grads, core, tag):
    theirs = _exchange_siblings(grads, "exchange_siblings_" + tag)
    sums = [_sum_pair(g, t, core, "sum_siblings_%s_%d" % (tag, i))
            for i, (g, t) in enumerate(zip(grads, theirs))]
    lands = [jax.ShapeDtypeStruct((N_CHIPS - 1,) + s.shape[1:], s.dtype) for s in sums]
    return _ici_start(sums, lands, _scatter_views, theirs[0], "scatter_start_" + tag)


def _scatter_finish(handle, chip, after, tag):
    sums, got = _ici_wait(handle, _scatter_views, after, "scatter_wait_" + tag)
    return [_sum_chips(s, g, chip, "sum_chips_%s_%d" % (tag, i))
            for i, (s, g) in enumerate(zip(sums, got))]


def _pad_rows(a, rows):
    return jnp.pad(a, ((0, rows - a.shape[0]), (0, 0)))


def kernel(x, norm_mix_0, w_in_0, b_f_0, conv_w_0, w_out_0, norm_ffn_0, w_up_0, w_down_0, norm_mix_1, pool_w_1, pool_scale_1, norm_ffn_1, w_up_1, w_down_1, final_norm, loss_target, m_norm_mix_0, m_w_in_0, m_b_f_0, m_conv_w_0, m_w_out_0, m_norm_ffn_0, m_w_up_0, m_w_down_0, m_norm_mix_1, m_pool_w_1, m_pool_scale_1, m_norm_ffn_1, m_w_up_1, m_w_down_1, m_final_norm, v_norm_mix_0, v_w_in_0, v_b_f_0, v_conv_w_0, v_w_out_0, v_norm_ffn_0, v_w_up_0, v_w_down_0, v_norm_mix_1, v_pool_w_1, v_pool_scale_1, v_norm_ffn_1, v_w_up_1, v_w_down_1, v_final_norm):
    d = x.shape[-1]
    a = N_HEADS * HEAD_DIM
    c_conv = conv_w_0.shape[1] * N_CHIPS
    xs = x[0]
    target = loss_target[0]
    row = lambda vec: vec.reshape(1, -1)

    big = [w_in_0, w_out_0, w_up_0, w_down_0, pool_w_1, w_up_1, w_down_1]
    first = [w_in_0.astype(BF16), w_out_0.astype(BF16), conv_w_0]
    first_split = [True, True, False]
    rest = [w.astype(BF16) for w in (w_up_0, w_down_0, pool_w_1, w_up_1, w_down_1)]
    rest_split = [True] * len(rest)
    start_a = _ici_start(first, _gather_land_shapes(first, first_split),
                         _gather_views(first_split), b_f_0, "gather_start_a")
    start_b = _ici_start(rest, _gather_land_shapes(rest, rest_split),
                         _gather_views(rest_split), start_a[-1], "gather_start_b")
    first, land_a = _ici_wait(start_a, _gather_views(first_split), start_b[-1], "gather_wait_a")
    g_in, g_out, g_conv = _gather_finish(first, land_a, first_split, "gather_finish_a")
    w_in = g_in.transpose(1, 0, 2).reshape(d, -1)
    w_qkv = w_in[:, :3 * a]
    w_f = jnp.pad(w_in[:, 3 * a:3 * a + N_HEADS], ((0, 0), (0, 128 - N_HEADS)))
    w_bcx = w_in[:, 3 * a + N_HEADS:]
    w_out = g_out.reshape(-1, d)
    conv_w = _pad_rows(g_conv.transpose(1, 0, 2).reshape(conv_w_0.shape[0], c_conv), 8)
    bf = jnp.pad(b_f_0, (0, 128 - N_HEADS)).reshape(1, 128)

    n0, qkv, fl, bcx = _ln_proj(xs, row(norm_mix_0), w_qkv, w_f, w_bcx)
    qa, ka = _gate_prep(fl, bf, qkv)
    o, lse = _attn_fwd(qa, ka, qkv)
    h1 = _conv_out(o, bcx, conv_w, w_out, xs)
    rest, land_b = _ici_wait(start_b, _gather_views(rest_split), h1, "gather_wait_b")
    g_up0, g_down0, g_pool, g_up1, g_down1 = _gather_finish(rest, land_b, rest_split,
                                                            "gather_finish_b")
    w_down0 = g_down0.reshape(-1, d)
    w_down1 = g_down1.reshape(-1, d)
    pool_w = g_pool.transpose(1, 0, 2, 3).reshape(pool_w_1.shape[0], -1, pool_w_1.shape[2])
    h2, a0, nf0 = _mlp_fwd(h1, row(norm_ffn_0), g_up0, w_down0, "mlp_fwd_0")
    h3 = _pool_fwd(h2, row(norm_mix_1), pool_w, row(pool_scale_1))
    h4, a1, nf1 = _mlp_fwd(h3, row(norm_ffn_1), g_up1, w_down1, "mlp_fwd_1")
    dh4, loss_part, d_final = _final_loss(h4, row(final_norm), target)

    slot_cols = g_up0.shape[2]
    pool_cols = pool_w.shape[2]
    core = lax.axis_index("c").astype(jnp.int32).reshape(1)
    chip_index = (2 * lax.axis_index("x") + lax.axis_index("y")).astype(jnp.int32).reshape(1)
    da1, dz1, dh3, d_nffn1 = _mlp_bwd_x(dh4, a1, g_up1, w_down1, h3, row(norm_ffn_1), "mlp_bwd_x_1")
    dw_up1, dw_down1 = _mlp_bwd_w(nf1, da1, a1, dz1, slot_cols, "mlp_bwd_w_1")
    scatter_1 = _scatter_start([dw_up1, dw_down1.reshape(N_CHIPS, -1, d)], core, "mlp1")
    dh2, dw_pool, d_pscale, d_nmix1 = _pool_bwd(scatter_1[-1], dh3, h2, row(norm_mix_1), pool_w,
                                                row(pool_scale_1))
    da0, dz0, dh1, d_nffn0 = _mlp_bwd_x(dh2, a0, g_up0, w_down0, h1, row(norm_ffn_0), "mlp_bwd_x_0")
    dw_up0, dw_down0 = _mlp_bwd_w(nf0, da0, a0, dz0, slot_cols, "mlp_bwd_w_0")
    dw_pool = (dw_pool.reshape(pool_w.shape[0], N_CHIPS, -1, pool_cols).transpose(1, 0, 2, 3)
               .reshape(N_CHIPS, -1, pool_cols))
    scatter_0 = _scatter_start([dw_up0, dw_down0.reshape(N_CHIPS, -1, d), dw_pool], core, "mlp0")
    do, delta, dbcx, dw_out, d_conv = _conv_out_bwd(scatter_0[-1], dh1, w_out, o, bcx, conv_w)
    dqa, dka, dv = _attn_bwd(qa, ka, qkv, do, lse, delta)
    dqkv, dfl, d_bf = _gate_bwd(dqa, dka, dv, fl, bf)
    dw_qkv, dw_f, dw_bcx = _wgrad_in(n0, [dqkv, dfl, dbcx])
    dw_in = jnp.concatenate([dw_qkv, dw_f[:, :N_HEADS], dw_bcx], axis=1)
    scatter_m = _scatter_start([dw_in.reshape(d, N_CHIPS, -1).transpose(1, 0, 2),
                                dw_out.reshape(N_CHIPS, -1, d)], core, "mixer")
    grad_x, d_nmix0 = _in_proj_bwd(scatter_m[-1], dqkv, dfl, dbcx, w_qkv, w_f, w_bcx, xs,
                                   row(norm_mix_0), dh1)

    h_up1, h_down1 = _scatter_finish(scatter_1, chip_index, grad_x, "mlp1")
    h_up0, h_down0, h_pool = _scatter_finish(scatter_0, chip_index, grad_x, "mlp0")
    h_in, h_out = _scatter_finish(scatter_m, chip_index, grad_x, "mixer")
    reduced = _share_halves([h_in, h_out, h_up0, h_down0, h_pool, h_up1, h_down1])
    moments = [(m_w_in_0, v_w_in_0), (m_w_out_0, v_w_out_0), (m_w_up_0, v_w_up_0),
               (m_w_down_0, v_w_down_0), (m_pool_w_1, v_pool_w_1), (m_w_up_1, v_w_up_1),
               (m_w_down_1, v_w_down_1)]
    big_out = []
    for k, (w, g, (m, v)) in enumerate(zip(big, reduced, moments)):
        flat = lambda t: t.reshape(-1, t.shape[-1])
        delta_w, new_m, new_v = _adamw(flat(w), flat(g), flat(m), flat(v), "adamw_%d" % k)
        big_out.append((g.reshape(w.shape), delta_w.reshape(w.shape), new_m.reshape(w.shape),
                        new_v.reshape(w.shape)))

    tail = jnp.concatenate([d_conv[0:3].reshape(-1)[d:], d_bf[0, :N_HEADS], loss_part[0, :1]])
    small_part = jnp.concatenate(
        [d_nmix0, d_nffn0, d_nmix1, d_pscale, d_nffn1, d_final,
         d_conv[0:3].reshape(1, -1)[:, :d],
         jnp.pad(tail, (0, d - tail.shape[0])).reshape(1, d)], axis=0)
    parts = _gather_small(small_part)

    chip = 2 * lax.axis_index("x") + lax.axis_index("y")
    cw_cols = conv_w_0.shape[1]

    def conv_block(full):
        mine = lax.dynamic_slice_in_dim(full, chip * cw_cols, cw_cols, axis=1)
        return jnp.pad(mine.reshape(-1), (0, d - mine.size))

    def small_rows(vals, cw, bfv):
        return jnp.stack(list(vals) + [cw, jnp.pad(bfv, (0, d - N_HEADS))])

    smalls_w = [norm_mix_0, norm_ffn_0, norm_mix_1, pool_scale_1, norm_ffn_1, final_norm]
    smalls_m = [m_norm_mix_0, m_norm_ffn_0, m_norm_mix_1, m_pool_scale_1, m_norm_ffn_1, m_final_norm]
    smalls_v = [v_norm_mix_0, v_norm_ffn_0, v_norm_mix_1, v_pool_scale_1, v_norm_ffn_1, v_final_norm]
    pad_cw = lambda t: jnp.pad(t.reshape(-1), (0, d - t.size))
    w_rows = small_rows(smalls_w, pad_cw(conv_w_0), b_f_0)
    m_rows = small_rows(smalls_m, pad_cw(m_conv_w_0), m_b_f_0)
    v_rows = small_rows(smalls_v, pad_cw(v_conv_w_0), v_b_f_0)

    g_sum = _sum_devices(parts)
    conv_full = jnp.concatenate([g_sum[6], g_sum[7, :3 * c_conv - d]]).reshape(3, c_conv)
    bf_grad = g_sum[7, 3 * c_conv - d:3 * c_conv - d + N_HEADS]
    loss = g_sum[7, 3 * c_conv - d + N_HEADS]
    g_rows = jnp.concatenate(
        [g_sum[0:6], conv_block(conv_full).reshape(1, d),
         jnp.pad(bf_grad, (0, d - N_HEADS)).reshape(1, d)], axis=0)
    d_rows, nm_rows, nv_rows = _adamw(w_rows, g_rows, m_rows, v_rows, "adamw_small")

    def unpack(rows):
        cw = rows[6, :conv_w_0.size].reshape(conv_w_0.shape)
        return [rows[0], rows[1], rows[2], rows[3], rows[4], rows[5], cw, rows[7, :N_HEADS]]

    def assemble(kind):
        sm = unpack([g_rows, d_rows, nm_rows, nv_rows][kind])
        lg = [t[kind] for t in big_out]
        return [sm[0], lg[0], sm[7], sm[6], lg[1], sm[1], lg[2], lg[3],
                sm[2], lg[4], sm[3], sm[4], lg[5], lg[6], sm[5]]

    return (loss, grad_x[None], *assemble(0), *assemble(1), *assemble(2), *assemble(3))
```

```python
import functools

import jax
import jax.numpy as jnp
from jax import lax
from jax.experimental import pallas as pl
from jax.experimental.pallas import tpu as pltpu

F32 = jnp.float32
BF16 = jnp.bfloat16

RMS_EPS = 1e-6
HEAD_DIM = 64
N_HEADS = 8
ATTN_SCALE = HEAD_DIM ** -0.5
LOG2_E = 1.4426950408889634
POOL_WINDOWS = (2, 4, 8, 16)
POOL_HALO = 16
CONV_HALO = 8
NEG_BIG = -1e30

ADAM_LR = 0.001
ADAM_B1 = 0.9
ADAM_B2 = 0.999
ADAM_EPS = 1e-08
ADAM_WD = 0.01
ADAM_STEP = 10

N_CHIPS = 4
N_DEV = 8
MESH = pl.DeviceIdType.MESH

VMEM_LIMIT_BYTES = 56 * 1024 * 1024

TILE_ROWS = 512
TILE_ATTN = 512
TILE_MLP_ROWS = 1024
TILE_MLP_FF = 1024
TILE_MLP_BWD_FF = 512
TILE_WGRAD_K = 1024
TILE_WGRAD_N = 1024
TILE_ELEM_ROWS = 256

LANE_CQ = 64
LANE_ONE = 67


def _params(semantics):
    return pltpu.CompilerParams(dimension_semantics=semantics,
                                vmem_limit_bytes=VMEM_LIMIT_BYTES)


def _nn(a, b):
    return lax.dot_general(a, b, (((1,), (0,)), ((), ())), preferred_element_type=F32)


def _nt(a, b):
    return lax.dot_general(a, b, (((1,), (1,)), ((), ())), preferred_element_type=F32)


def _tn(a, b):
    return lax.dot_general(a, b, (((0,), (0,)), ((), ())), preferred_element_type=F32)


def _split3(v):
    hi = v.astype(BF16)
    r1 = v - hi.astype(F32)
    mid = r1.astype(BF16)
    lo = (r1 - mid.astype(F32)).astype(BF16)
    return hi, mid, lo


def _exact_nn(sel, v):
    hi, mid, lo = _split3(v)
    return _nn(sel, hi) + _nn(sel, mid) + _nn(sel, lo)


def _exact_nt(sel, v):
    hi, mid, lo = _split3(v)
    return _nt(sel, hi) + _nt(sel, mid) + _nt(sel, lo)


def _rms_fwd(x, g):
    r = lax.rsqrt(jnp.mean(x * x, axis=-1, keepdims=True) + RMS_EPS)
    return x * r * g, r


def _rms_bwd(dn, x, g):
    r = lax.rsqrt(jnp.mean(x * x, axis=-1, keepdims=True) + RMS_EPS)
    xh = x * r
    gy = dn * g
    dx = r * (gy - xh * jnp.mean(gy * xh, axis=-1, keepdims=True))
    return dx, jnp.sum(dn * xh, axis=0, keepdims=True)


def _lane(shape):
    return lax.broadcasted_iota(jnp.int32, shape, len(shape) - 1)


def _row(shape):
    return lax.broadcasted_iota(jnp.int32, shape, len(shape) - 2)


def _full(a):
    nd = a.ndim
    return pl.BlockSpec(a.shape, lambda *_: (0,) * nd)


def _ln_proj(x, g, w_qkv, w_f, w_bcx):
    s, d = x.shape
    tm = min(TILE_ROWS, s)

    def body(x_ref, g_ref, wq_ref, wf_ref, wb_ref, n_ref, qkv_ref, fl_ref, bcx_ref):
        n, _ = _rms_fwd(x_ref[...], g_ref[...])
        nb = n.astype(BF16)
        n_ref[...] = nb
        qkv_ref[...] = _nn(nb, wq_ref[...]).astype(BF16)
        fl_ref[...] = _nn(nb, wf_ref[...])
        bcx_ref[...] = _nn(nb, wb_ref[...])

    rows = lambda c: pl.BlockSpec((tm, c), lambda i: (i, 0))
    return pl.pallas_call(
        body, name="ln_proj", grid=(s // tm,),
        in_specs=[rows(d), _full(g), _full(w_qkv), _full(w_f), _full(w_bcx)],
        out_specs=[rows(d), rows(w_qkv.shape[1]), rows(w_f.shape[1]), rows(w_bcx.shape[1])],
        out_shape=[jax.ShapeDtypeStruct((s, d), BF16),
                   jax.ShapeDtypeStruct((s, w_qkv.shape[1]), BF16),
                   jax.ShapeDtypeStruct((s, w_f.shape[1]), F32),
                   jax.ShapeDtypeStruct((s, w_bcx.shape[1]), F32)],
        compiler_params=_params(("parallel",)),
    )(x, g, w_qkv, w_f, w_bcx)


def _gate_prep(fl, bf, qkv):
    s = fl.shape[0]
    a = N_HEADS * HEAD_DIM
    tm = min(TILE_ROWS, s)

    def body(fl_ref, bf_ref, q_ref, k_ref, qa_ref, ka_ref, carry_ref):
        i = pl.program_id(0)

        @pl.when(i == 0)
        def _():
            carry_ref[...] = jnp.zeros_like(carry_ref)

        z = fl_ref[...] + bf_ref[...]
        logf = jnp.minimum(z, 0.0) - jnp.log(1.0 + jnp.exp(-jnp.abs(z)))
        lower = (_lane((tm, tm)) <= _row((tm, tm))).astype(BF16)
        cum = _exact_nn(lower, logf) + carry_ref[0:1, :]
        carry_ref[0:1, :] = cum[tm - 1:tm, :]

        lane = _lane((tm, 128))
        for h in range(N_HEADS):
            cb = LOG2_E * jnp.sum(jnp.where(lane == h, cum, 0.0), axis=1, keepdims=True)
            hi, mid, lo = (p.astype(F32) for p in _split3(cb))
            pair = slice((h // 2) * 128, (h // 2 + 1) * 128)
            qp = q_ref[:, pair].astype(F32)
            kp = k_ref[:, pair].astype(F32)
            if h % 2:
                qp = pltpu.roll(qp, HEAD_DIM, axis=1)
                kp = pltpu.roll(kp, HEAD_DIM, axis=1)
            q_bias = jnp.where(lane == LANE_CQ, hi,
                               jnp.where(lane == LANE_CQ + 1, mid,
                                         jnp.where(lane == LANE_CQ + 2, lo,
                                                   jnp.where(lane < LANE_ONE + 3, 1.0, 0.0))))
            k_bias = jnp.where(lane < LANE_ONE, 1.0,
                               jnp.where(lane == LANE_ONE, -hi,
                                         jnp.where(lane == LANE_ONE + 1, -mid,
                                                   jnp.where(lane == LANE_ONE + 2, -lo, 0.0))))
            qa_ref[h] = jnp.where(lane < HEAD_DIM, qp * (ATTN_SCALE * LOG2_E), q_bias).astype(BF16)
            ka_ref[h] = jnp.where(lane < HEAD_DIM, kp, k_bias).astype(BF16)

    aug = jax.ShapeDtypeStruct((N_HEADS, s, 128), BF16)
    aug_spec = pl.BlockSpec((N_HEADS, tm, 128), lambda i: (0, i, 0))
    return pl.pallas_call(
        body, name="gate_prep", grid=(s // tm,),
        in_specs=[pl.BlockSpec((tm, 128), lambda i: (i, 0)), _full(bf),
                  pl.BlockSpec((tm, a), lambda i: (i, 0)),
                  pl.BlockSpec((tm, a), lambda i: (i, 1))],
        out_specs=[aug_spec, aug_spec],
        out_shape=[aug, aug],
        scratch_shapes=[pltpu.VMEM((8, 128), F32)],
        compiler_params=_params(("arbitrary",)),
    )(fl, bf, qkv, qkv)


def _attn_fwd(qa, ka, qkv):
    s = qa.shape[1]
    a = N_HEADS * HEAD_DIM
    t = min(TILE_ATTN, s)
    n_pairs = N_HEADS // 2
    v_block0 = 2 * a // 128

    ones_lane = (HEAD_DIM, 0)

    def body(qa_ref, ka_ref, v_ref, o_ref, lse_ref, m_ref, acc_ref, s_even, s_odd):
        i = pl.program_id(1)
        m_ref[...] = jnp.full_like(m_ref, NEG_BIG)
        acc_ref[...] = jnp.zeros_like(acc_ref)
        upper_rows = _row((128, t)) < HEAD_DIM

        def keys(j):
            return pl.ds(pl.multiple_of(j * t, t), t)

        def scores_into(buf, j):
            for e in range(2):
                buf[e] = _nt(ka_ref[e, keys(j), :], qa_ref[e])

        def consume(buf, j, masked):
            vf = v_ref[keys(j), :].astype(F32)
            lane = _lane((t, 128))
            own = [lane < HEAD_DIM, lane >= HEAD_DIM]
            for e in range(2):
                v_head = jnp.where(own[e], vf, jnp.where(lane == ones_lane[e], 1.0, 0.0)).astype(BF16)
                sc = buf[e]
                if masked:
                    sc = jnp.where(_row((t, t)) <= _lane((t, t)), sc, NEG_BIG)
                m_prev = m_ref[e]
                m_new = jnp.maximum(m_prev, jnp.max(sc, axis=0, keepdims=True))
                p = jnp.exp2(sc - m_new).astype(BF16)
                acc_ref[e] = acc_ref[e] * jnp.exp2(m_prev - m_new) + _tn(v_head, p)
                m_ref[e] = m_new

        scores_into(s_even, 0)

        def two_tiles(p, carry):
            j = 2 * p
            scores_into(s_odd, j + 1)
            consume(s_even, j, False)
            scores_into(s_even, j + 2)
            consume(s_odd, j + 1, False)
            return carry

        lax.fori_loop(0, i // 2, two_tiles, 0)

        @pl.when(i % 2 == 0)
        def _():
            consume(s_even, i, True)

        @pl.when(i % 2 == 1)
        def _():
            scores_into(s_odd, i)
            consume(s_even, i - 1, False)
            consume(s_odd, i, True)

        denom = [acc_ref[e, ones_lane[e]:ones_lane[e] + 1, :] for e in range(2)]
        out_t = jnp.where(upper_rows, acc_ref[0] / denom[0], acc_ref[1] / denom[1])
        o_ref[...] = out_t.T.astype(BF16)
        lse = [m_ref[e] + LOG2_E * jnp.log(denom[e]) for e in range(2)]
        lse_ref[...] = jnp.where(_row((8, t)) == 0, lse[0], lse[1])

    return pl.pallas_call(
        body, name="attn_fwd", grid=(n_pairs, s // t),
        in_specs=[pl.BlockSpec((2, t, 128), lambda g, i: (g, i, 0)),
                  pl.BlockSpec((2, s, 128), lambda g, i: (g, 0, 0)),
                  pl.BlockSpec((s, 128), lambda g, i: (0, v_block0 + g))],
        out_specs=[pl.BlockSpec((t, 128), lambda g, i: (i, g)),
                   pl.BlockSpec((None, 8, t), lambda g, i: (g, 0, i))],
        out_shape=[jax.ShapeDtypeStruct((s, a), BF16),
                   jax.ShapeDtypeStruct((n_pairs, 8, s), F32)],
        scratch_shapes=[pltpu.VMEM((2, 1, t), F32), pltpu.VMEM((2, 128, t), F32),
                        pltpu.VMEM((2, t, t), F32), pltpu.VMEM((2, t, t), F32)],
        compiler_params=_params(("parallel", "arbitrary")),
    )(qa, ka, qkv)


def _conv_out(o, bcx, cw, w_out, x):
    s, d = x.shape
    c = o.shape[1]
    tm = min(TILE_ROWS, s)

    def body(o_ref, b_ref, c_ref, xin_ref, cw_ref, w_ref, x_ref, h_ref, ubuf):
        i = pl.program_id(0)

        @pl.when(i == 0)
        def _():
            ubuf[0:CONV_HALO, :] = jnp.zeros((CONV_HALO, c), F32)

        u = c_ref[...] * xin_ref[...]
        ubuf[CONV_HALO:CONV_HALO + tm, :] = u
        u1 = ubuf[CONV_HALO - 1:CONV_HALO - 1 + tm, :]
        u2 = ubuf[CONV_HALO - 2:CONV_HALO - 2 + tm, :]
        cv = (cw_ref[0:1, :] * u2 + cw_ref[1:2, :] * u1) + cw_ref[2:3, :] * u
        y = (b_ref[...] * cv).astype(BF16)
        mix = _nn(o_ref[...], w_ref[0:c, :]) + _nn(y, w_ref[c:2 * c, :])
        h_ref[...] = x_ref[...] + mix
        ubuf[0:CONV_HALO, :] = u[tm - CONV_HALO:tm, :]

    col = lambda k: pl.BlockSpec((tm, c), lambda i: (i, k))
    return pl.pallas_call(
        body, name="conv_out", grid=(s // tm,),
        in_specs=[col(0), col(0), col(1), col(2), _full(cw), _full(w_out),
                  pl.BlockSpec((tm, d), lambda i: (i, 0))],
        out_specs=pl.BlockSpec((tm, d), lambda i: (i, 0)),
        out_shape=jax.ShapeDtypeStruct((s, d), F32),
        scratch_shapes=[pltpu.VMEM((tm + CONV_HALO, c), F32)],
        compiler_params=_params(("arbitrary",)),
    )(o, bcx, bcx, bcx, cw, w_out, x)


def _mlp_fwd(h, g, w_up, w_down, name):
    s, d = h.shape
    ff = w_down.shape[0]
    slot_cols = w_up.shape[2]
    tm = min(TILE_MLP_ROWS, s)
    tf = min(TILE_MLP_FF, slot_cols)
    per_slot = slot_cols // tf
    nf = ff // tf

    def body(h_ref, g_ref, wu_ref, wd_ref, out_ref, a_ref, n_ref, nb_ref, acc_ref):
        f = pl.program_id(1)

        @pl.when(f == 0)
        def _():
            n, _ = _rms_fwd(h_ref[...], g_ref[...])
            nb = n.astype(BF16)
            nb_ref[...] = nb
            n_ref[...] = nb
            acc_ref[...] = jnp.zeros_like(acc_ref)

        pre = _nn(nb_ref[...], wu_ref[...])
        a_ref[...] = pre.astype(BF16)
        r = jnp.square(jnp.maximum(pre, 0.0)).astype(BF16)
        acc_ref[...] += _nn(r, wd_ref[...])

        @pl.when(f == nf - 1)
        def _():
            out_ref[...] = h_ref[...] + acc_ref[...]

    return pl.pallas_call(
        body, name=name, grid=(s // tm, nf),
        in_specs=[pl.BlockSpec((tm, d), lambda i, f: (i, 0)), _full(g),
                  pl.BlockSpec((None, d, tf), lambda i, f: (f // per_slot, 0, f % per_slot)),
                  pl.BlockSpec((tf, d), lambda i, f: (f, 0))],
        out_specs=[pl.BlockSpec((tm, d), lambda i, f: (i, 0)),
                   pl.BlockSpec((tm, tf), lambda i, f: (i, f)),
                   pl.BlockSpec((tm, d), lambda i, f: (i, 0))],
        out_shape=[jax.ShapeDtypeStruct((s, d), F32),
                   jax.ShapeDtypeStruct((s, ff), BF16),
                   jax.ShapeDtypeStruct((s, d), BF16)],
        scratch_shapes=[pltpu.VMEM((tm, d), BF16), pltpu.VMEM((tm, d), F32)],
        compiler_params=_params(("parallel", "arbitrary")),
    )(h, g, w_up, w_down)


def _window_sum_down(e, window):
    step = 1
    while step < window:
        e = e + pltpu.roll(e, step, axis=0)
        step *= 2
    return e


def _window_sum_up(e, window):
    n = e.shape[0]
    step = 1
    while step < window:
        e = e + pltpu.roll(e, n - step, axis=0)
        step *= 2
    return e


def _pool_counts(first_row, tm, window):
    t = first_row + _row((tm, 1))
    return jnp.minimum(t + 1, window).astype(F32)


def _pool_fwd(h, g, pw, ps):
    s, d = h.shape
    cg = d // len(POOL_WINDOWS)
    tm = min(TILE_ROWS, s)

    def body(h_ref, g_ref, pw_ref, ps_ref, out_ref, nbuf):
        i = pl.program_id(0)

        @pl.when(i == 0)
        def _():
            nbuf[0:POOL_HALO, :] = jnp.zeros((POOL_HALO, d), F32)

        n, _ = _rms_fwd(h_ref[...], g_ref[...])
        nbuf[POOL_HALO:POOL_HALO + tm, :] = n
        for k, window in enumerate(POOL_WINDOWS):
            cols = slice(k * cg, (k + 1) * cg)
            sums = _window_sum_down(nbuf[:, cols], window)[POOL_HALO:, :]
            pooled = sums / _pool_counts(i * tm, tm, window) - n[:, cols]
            y = _nn(pooled.astype(BF16), pw_ref[k]) * ps_ref[:, cols]
            out_ref[:, cols] = h_ref[:, cols] + y
        nbuf[0:POOL_HALO, :] = n[tm - POOL_HALO:tm, :]

    return pl.pallas_call(
        body, name="pool_fwd", grid=(s // tm,),
        in_specs=[pl.BlockSpec((tm, d), lambda i: (i, 0)), _full(g), _full(pw), _full(ps)],
        out_specs=pl.BlockSpec((tm, d), lambda i: (i, 0)),
        out_shape=jax.ShapeDtypeStruct((s, d), F32),
        scratch_shapes=[pltpu.VMEM((tm + POOL_HALO, d), F32)],
        compiler_params=_params(("arbitrary",)),
    )(h, g, pw, ps)


def _final_loss(h, g, target):
    s, d = h.shape
    tm = min(TILE_ROWS, s)

    def body(h_ref, g_ref, t_ref, dh_ref, loss_ref, dg_ref):
        i = pl.program_id(0)
        hv = h_ref[...]
        y, _ = _rms_fwd(hv, g_ref[...])
        err = y - t_ref[...]
        part = 0.5 * jnp.sum(jnp.mean(err * err, axis=-1, keepdims=True), axis=0, keepdims=True)
        dx, dg = _rms_bwd(err / d, hv, g_ref[...])
        dh_ref[...] = dx
        part = jnp.broadcast_to(part, loss_ref.shape)

        @pl.when(i == 0)
        def _():
            loss_ref[...] = part
            dg_ref[...] = dg

        @pl.when(i > 0)
        def _():
            loss_ref[...] += part
            dg_ref[...] += dg

    return pl.pallas_call(
        body, name="final_loss", grid=(s // tm,),
        in_specs=[pl.BlockSpec((tm, d), lambda i: (i, 0)), _full(g),
                  pl.BlockSpec((tm, d), lambda i: (i, 0))],
        out_specs=[pl.BlockSpec((tm, d), lambda i: (i, 0)),
                   pl.BlockSpec((1, 128), lambda i: (0, 0)),
                   pl.BlockSpec((1, d), lambda i: (0, 0))],
        out_shape=[jax.ShapeDtypeStruct((s, d), F32),
                   jax.ShapeDtypeStruct((1, 128), F32),
                   jax.ShapeDtypeStruct((1, d), F32)],
        compiler_params=_params(("arbitrary",)),
    )(h, g, target)


def _mlp_bwd_x(dz, a, w_up, w_down, h_in, g, name):
    s, d = dz.shape
    ff = w_down.shape[0]
    slot_cols = w_up.shape[2]
    tm = min(TILE_MLP_ROWS, s)
    tf = min(TILE_MLP_BWD_FF, slot_cols)
    per_slot = slot_cols // tf
    nf = ff // tf

    def body(dz_ref, a_ref, wu_ref, wd_ref, h_ref, g_ref, da_ref, dzb_ref, dh_ref, dg_ref,
             dzs_ref, acc_ref):
        i = pl.program_id(0)
        f = pl.program_id(1)

        @pl.when(f == 0)
        def _():
            dzb = dz_ref[...].astype(BF16)
            dzs_ref[...] = dzb
            dzb_ref[...] = dzb
            acc_ref[...] = jnp.zeros_like(acc_ref)

        dr = _nt(dzs_ref[...], wd_ref[...])
        da = (dr * (2.0 * jnp.maximum(a_ref[...].astype(F32), 0.0))).astype(BF16)
        da_ref[...] = da
        acc_ref[...] += _nt(da, wu_ref[...])

        @pl.when(f == nf - 1)
        def _():
            dx, dg = _rms_bwd(acc_ref[...], h_ref[...], g_ref[...])
            dh_ref[...] = dz_ref[...] + dx

            @pl.when(i == 0)
            def _():
                dg_ref[...] = dg

            @pl.when(i > 0)
            def _():
                dg_ref[...] += dg

    return pl.pallas_call(
        body, name=name, grid=(s // tm, nf),
        in_specs=[pl.BlockSpec((tm, d), lambda i, f: (i, 0)),
                  pl.BlockSpec((tm, tf), lambda i, f: (i, f)),
                  pl.BlockSpec((None, d, tf), lambda i, f: (f // per_slot, 0, f % per_slot)),
                  pl.BlockSpec((tf, d), lambda i, f: (f, 0)),
                  pl.BlockSpec((tm, d), lambda i, f: (i, 0)), _full(g)],
        out_specs=[pl.BlockSpec((tm, tf), lambda i, f: (i, f)),
                   pl.BlockSpec((tm, d), lambda i, f: (i, 0)),
                   pl.BlockSpec((tm, d), lambda i, f: (i, 0)),
                   pl.BlockSpec((1, d), lambda i, f: (0, 0))],
        out_shape=[jax.ShapeDtypeStruct((s, ff), BF16),
                   jax.ShapeDtypeStruct((s, d), BF16),
                   jax.ShapeDtypeStruct((s, d), F32),
                   jax.ShapeDtypeStruct((1, d), F32)],
        scratch_shapes=[pltpu.VMEM((tm, d), BF16), pltpu.VMEM((tm, d), F32)],
        compiler_params=_params(("arbitrary", "arbitrary")),
    )(dz, a, w_up, w_down, h_in, g)


def _mlp_bwd_w(n, da, a, dzb, slot_cols, name):
    s, d = n.shape
    ff = a.shape[1]
    tn = min(TILE_WGRAD_N, slot_cols)
    tk = min(TILE_WGRAD_K, s)
    per_slot = slot_cols // tn
    nk = s // tk

    def body(n_ref, da_ref, a_ref, dz_ref, du_ref, dd_ref, accu_ref, accd_ref):
        k = pl.program_id(1)

        @pl.when(k == 0)
        def _():
            accu_ref[...] = jnp.zeros_like(accu_ref)
            accd_ref[...] = jnp.zeros_like(accd_ref)

        accu_ref[...] += _tn(n_ref[...], da_ref[...])
        r = jnp.square(jnp.maximum(a_ref[...].astype(F32), 0.0)).astype(BF16)
        accd_ref[...] += _tn(r, dz_ref[...])

        @pl.when(k == nk - 1)
        def _():
            du_ref[...] = accu_ref[...].astype(BF16)
            dd_ref[...] = accd_ref[...].astype(BF16)

    return pl.pallas_call(
        body, name=name, grid=(ff // tn, nk),
        in_specs=[pl.BlockSpec((tk, d), lambda f, k: (k, 0)),
                  pl.BlockSpec((tk, tn), lambda f, k: (k, f)),
                  pl.BlockSpec((tk, tn), lambda f, k: (k, f)),
                  pl.BlockSpec((tk, d), lambda f, k: (k, 0))],
        out_specs=[pl.BlockSpec((None, d, tn), lambda f, k: (f // per_slot, 0, f % per_slot)),
                   pl.BlockSpec((tn, d), lambda f, k: (f, 0))],
        out_shape=[jax.ShapeDtypeStruct((ff // slot_cols, d, slot_cols), BF16),
                   jax.ShapeDtypeStruct((ff, d), BF16)],
        scratch_shapes=[pltpu.VMEM((d, tn), F32), pltpu.VMEM((tn, d), F32)],
        compiler_params=_params(("parallel", "arbitrary")),
    )(n, da, a, dzb)


def _pool_bwd(after, dh, h, g, pw, ps):
    s, d = h.shape
    cg = d // len(POOL_WINDOWS)
    tm = min(TILE_ROWS, s)
    nb = s // tm
    halo_per_tile = tm // POOL_HALO

    def body(after_ref, dh_ref, h_ref, halo_ref, g_ref, pw_ref, ps_ref,
             dx_ref, dpw_ref, dps_ref, dg_ref, nbuf, qbuf, dn_ref, carry, dpw_acc):
        i = pl.program_id(0)
        blk = nb - 1 - i

        @pl.when(i == 0)
        def _():
            carry[...] = jnp.zeros_like(carry)
            dpw_acc[...] = jnp.zeros_like(dpw_acc)
            dps_ref[...] = jnp.zeros_like(dps_ref)
            dg_ref[...] = jnp.zeros_like(dg_ref)

        hv = h_ref[...]
        n, _ = _rms_fwd(hv, g_ref[...])
        nh, _ = _rms_fwd(halo_ref[...], g_ref[...])
        nbuf[0:POOL_HALO, :] = jnp.where(blk == 0, 0.0, nh)
        nbuf[POOL_HALO:POOL_HALO + tm, :] = n
        dhv = dh_ref[...]
        for k, window in enumerate(POOL_WINDOWS):
            cols = slice(k * cg, (k + 1) * cg)
            cnt = _pool_counts(blk * tm, tm, window)
            sums = _window_sum_down(nbuf[:, cols], window)[POOL_HALO:, :]
            pb = (sums / cnt - n[:, cols]).astype(BF16)
            dyk = dhv[:, cols]
            dps_ref[:, cols] += jnp.sum(dyk * _nn(pb, pw_ref[k]), axis=0, keepdims=True)
            dyb = (dyk * ps_ref[:, cols]).astype(BF16)
            dpw_acc[k] += _tn(pb, dyb)
            dpool = _nt(dyb, pw_ref[k])
            qv = dpool / cnt
            qbuf[0:tm, cols] = qv
            qbuf[tm:tm + POOL_HALO, cols] = carry[:, cols]
            dn_ref[:, cols] = _window_sum_up(qbuf[:, cols], window)[0:tm, :] - dpool
            carry[:, cols] = qv[0:POOL_HALO, :]
        dx, dg = _rms_bwd(dn_ref[...], hv, g_ref[...])
        dx_ref[...] = dhv + dx
        dg_ref[...] += dg

        @pl.when(i == nb - 1)
        def _():
            dpw_ref[...] = dpw_acc[...].astype(BF16)

    rev = lambda i: (nb - 1 - i, 0)
    return pl.pallas_call(
        body, name="pool_bwd", grid=(nb,),
        in_specs=[ANY, pl.BlockSpec((tm, d), rev), pl.BlockSpec((tm, d), rev),
                  pl.BlockSpec((POOL_HALO, d),
                               lambda i: (jnp.maximum((nb - 1 - i) * halo_per_tile - 1, 0), 0)),
                  _full(g), _full(pw), _full(ps)],
        out_specs=[pl.BlockSpec((tm, d), rev), _full(pw),
                   pl.BlockSpec((1, d), lambda i: (0, 0)),
                   pl.BlockSpec((1, d), lambda i: (0, 0))],
        out_shape=[jax.ShapeDtypeStruct((s, d), F32),
                   jax.ShapeDtypeStruct(pw.shape, BF16),
                   jax.ShapeDtypeStruct((1, d), F32),
                   jax.ShapeDtypeStruct((1, d), F32)],
        scratch_shapes=[pltpu.VMEM((tm + POOL_HALO, d), F32), pltpu.VMEM((tm + POOL_HALO, d), F32),
                        pltpu.VMEM((tm, d), F32), pltpu.VMEM((POOL_HALO, d), F32),
                        pltpu.VMEM(pw.shape, F32)],
        compiler_params=_params(("arbitrary",)),
    )(after, dh, h, h, g, pw, ps)


def _conv_out_bwd(after, dh, w_out, o, bcx, cw):
    s, d = dh.shape
    c = o.shape[1]
    tm = min(TILE_ROWS, s)
    nb = s // tm
    halo_per_tile = tm // CONV_HALO

    def body(after_ref, dh_ref, w_ref, o_ref, b_ref, c_ref, xin_ref, ch_ref, xh_ref, cw_ref,
             do_ref, delta_ref, dbcx_ref, dw_ref, dcw_ref, ubuf, dbuf, carry, acc):
        i = pl.program_id(0)
        blk = nb - 1 - i

        @pl.when(i == 0)
        def _():
            carry[...] = jnp.zeros_like(carry)
            acc[...] = jnp.zeros_like(acc)
            dcw_ref[...] = jnp.zeros_like(dcw_ref)

        dm = dh_ref[...].astype(BF16)
        dcat = _nt(dm, w_ref[...])
        do = dcat[:, 0:c]
        dy = dcat[:, c:2 * c]
        do_ref[...] = do.astype(BF16)
        head_of_lane = lax.shift_right_logical(_lane((8, c)), HEAD_DIM.bit_length() - 1)
        heads = (head_of_lane == _row((8, c))).astype(BF16)
        delta_ref[...] = _exact_nt(heads, do * o_ref[...].astype(F32))

        cv_ = c_ref[...]
        xin = xin_ref[...]
        bv = b_ref[...]
        u = cv_ * xin
        ubuf[0:CONV_HALO, :] = jnp.where(blk == 0, 0.0, ch_ref[...] * xh_ref[...])
        ubuf[CONV_HALO:CONV_HALO + tm, :] = u
        u1 = ubuf[CONV_HALO - 1:CONV_HALO - 1 + tm, :]
        u2 = ubuf[CONV_HALO - 2:CONV_HALO - 2 + tm, :]
        w0, w1, w2 = cw_ref[0:1, :], cw_ref[1:2, :], cw_ref[2:3, :]
        cv = (w0 * u2 + w1 * u1) + w2 * u
        acc[0:c, :] += _tn(o_ref[...], dm)
        acc[c:2 * c, :] += _tn((bv * cv).astype(BF16), dm)

        dcv = dy * bv
        dcw_ref[0:1, :] += jnp.sum(dcv * u2, axis=0, keepdims=True)
        dcw_ref[1:2, :] += jnp.sum(dcv * u1, axis=0, keepdims=True)
        dcw_ref[2:3, :] += jnp.sum(dcv * u, axis=0, keepdims=True)
        dbuf[0:tm, :] = dcv
        dbuf[tm:tm + CONV_HALO, :] = carry[...]
        du = w2 * dcv + w1 * dbuf[1:1 + tm, :] + w0 * dbuf[2:2 + tm, :]
        dbcx_ref[:, 0:c] = (dy * cv).astype(BF16)
        dbcx_ref[:, c:2 * c] = (du * xin).astype(BF16)
        dbcx_ref[:, 2 * c:3 * c] = (du * cv_).astype(BF16)
        carry[...] = dcv[0:CONV_HALO, :]

        @pl.when(i == nb - 1)
        def _():
            dw_ref[...] = acc[...].astype(BF16)

    rev = lambda k: (lambda i: (nb - 1 - i, k))
    halo = lambda k: (lambda i: (jnp.maximum((nb - 1 - i) * halo_per_tile - 1, 0), k))
    return pl.pallas_call(
        body, name="conv_out_bwd", grid=(nb,),
        in_specs=[ANY, pl.BlockSpec((tm, d), rev(0)), _full(w_out), pl.BlockSpec((tm, c), rev(0)),
                  pl.BlockSpec((tm, c), rev(0)), pl.BlockSpec((tm, c), rev(1)),
                  pl.BlockSpec((tm, c), rev(2)),
                  pl.BlockSpec((CONV_HALO, c), halo(1)), pl.BlockSpec((CONV_HALO, c), halo(2)),
                  _full(cw)],
        out_specs=[pl.BlockSpec((tm, c), rev(0)),
                   pl.BlockSpec((8, tm), lambda i: (0, nb - 1 - i)),
                   pl.BlockSpec((tm, 3 * c), rev(0)),
                   _full(w_out), _full(cw)],
        out_shape=[jax.ShapeDtypeStruct((s, c), BF16),
                   jax.ShapeDtypeStruct((8, s), F32),
                   jax.ShapeDtypeStruct((s, 3 * c), BF16),
                   jax.ShapeDtypeStruct(w_out.shape, BF16),
                   jax.ShapeDtypeStruct(cw.shape, F32)],
        scratch_shapes=[pltpu.VMEM((tm + CONV_HALO, c), F32), pltpu.VMEM((tm + CONV_HALO, c), F32),
                        pltpu.VMEM((CONV_HALO, c), F32), pltpu.VMEM(w_out.shape, F32)],
        compiler_params=_params(("arbitrary",)),
    )(after, dh, w_out, o, bcx, bcx, bcx, bcx, bcx, cw)


def _attn_bwd(qa, ka, qkv, do, lse, delta):
    s = qa.shape[1]
    a = N_HEADS * HEAD_DIM
    t = min(TILE_ATTN, s)
    nq = s // t
    n_pairs = N_HEADS // 2
    v_block0 = 2 * a // 128

    def body(ka_ref, v_ref, qa_ref, do_ref, lse_ref, delta_ref,
             dqt_ref, dka_ref, dv_ref, dk_acc, dv_acc):
        g = pl.program_id(0)
        j = pl.program_id(1)

        @pl.when(j == 0)
        def _():
            dqt_ref[...] = jnp.zeros_like(dqt_ref)

        dk_acc[...] = jnp.zeros_like(dk_acc)
        dv_acc[...] = jnp.zeros_like(dv_acc)
        lane = _lane((t, 128))
        vf = v_ref[...].astype(F32)
        v_heads = [jnp.where(lane < HEAD_DIM, vf, 0.0).astype(BF16),
                   jnp.where(lane >= HEAD_DIM, vf, 0.0).astype(BF16)]
        ke_t = [ka_ref[e].astype(F32).T.astype(BF16) for e in range(2)]

        def q_step(i, masked):
            qs = pl.ds(pl.multiple_of(i * t, t), t)
            dob = do_ref[qs, :]
            for e in range(2):
                qe = qa_ref[e, qs, :]
                sc = _nt(ka_ref[e], qe)
                if masked:
                    sc = jnp.where(_row((t, t)) <= _lane((t, t)), sc, NEG_BIG)
                p = jnp.exp2(sc - lse_ref[pl.ds(e, 1), qs])
                dv_acc[e] += _nn(p.astype(BF16), dob)
                dp = _nt(v_heads[e], dob)
                ds = (p * (dp - delta_ref[pl.ds(2 * g + e, 1), qs])).astype(BF16)
                dk_acc[e] += _nn(ds, qe)
                dqt_ref[e, :, qs] += _nn(ke_t[e], ds)

        q_step(j, True)

        def full_step(i, carry):
            q_step(i, False)
            return carry

        lax.fori_loop(j + 1, nq, full_step, 0)
        dka_ref[...] = dk_acc[...]
        dv_ref[...] = jnp.where(lane < HEAD_DIM, dv_acc[0], dv_acc[1]).astype(BF16)

    return pl.pallas_call(
        body, name="attn_bwd", grid=(n_pairs, nq),
        in_specs=[pl.BlockSpec((2, t, 128), lambda g, j: (g, j, 0)),
                  pl.BlockSpec((t, 128), lambda g, j: (j, v_block0 + g)),
                  pl.BlockSpec((2, s, 128), lambda g, j: (g, 0, 0)),
                  pl.BlockSpec((s, 128), lambda g, j: (0, g)),
                  pl.BlockSpec((None, 8, s), lambda g, j: (g, 0, 0)),
                  pl.BlockSpec((8, s), lambda g, j: (0, 0))],
        out_specs=[pl.BlockSpec((2, 128, s), lambda g, j: (g, 0, 0)),
                   pl.BlockSpec((2, t, 128), lambda g, j: (g, j, 0)),
                   pl.BlockSpec((t, 128), lambda g, j: (j, g))],
        out_shape=[jax.ShapeDtypeStruct((N_HEADS, 128, s), F32),
                   jax.ShapeDtypeStruct((N_HEADS, s, 128), F32),
                   jax.ShapeDtypeStruct((s, a), BF16)],
        scratch_shapes=[pltpu.VMEM((2, t, 128), F32), pltpu.VMEM((2, t, 128), F32)],
        compiler_params=_params(("parallel", "arbitrary")),
    )(ka, qkv, qa, do, lse, delta)


def _gate_bwd(dqa, dka, dv, fl, bf):
    s = fl.shape[0]
    a = N_HEADS * HEAD_DIM
    tm = min(TILE_ROWS, s)
    nb = s // tm

    def body(dqa_ref, dka_ref, dv_ref, fl_ref, bf_ref, dqkv_ref, dfl_ref, dbf_ref, carry):
        i = pl.program_id(0)

        @pl.when(i == 0)
        def _():
            carry[...] = jnp.zeros_like(carry)
            dbf_ref[...] = jnp.zeros_like(dbf_ref)

        lane = _lane((tm, 128))
        dcum = jnp.zeros((tm, 128), F32)
        for pair in range(N_HEADS // 2):
            qs, ks = [], []
            for e in range(2):
                h = 2 * pair + e
                dq = dqa_ref[h].T
                dk = dka_ref[h]
                dc = jnp.sum(jnp.where(lane == LANE_CQ, dq, 0.0)
                             - jnp.where(lane == LANE_ONE, dk, 0.0), axis=1, keepdims=True)
                dcum = jnp.where(lane == h, dc, dcum)
                qs.append(dq * ATTN_SCALE)
                ks.append(dk * (1.0 / LOG2_E))
            cols = slice(pair * 128, (pair + 1) * 128)
            dqkv_ref[:, cols] = jnp.where(
                lane < HEAD_DIM, qs[0], pltpu.roll(qs[1], HEAD_DIM, axis=1)).astype(BF16)
            dqkv_ref[:, a + pair * 128:a + (pair + 1) * 128] = jnp.where(
                lane < HEAD_DIM, ks[0], pltpu.roll(ks[1], HEAD_DIM, axis=1)).astype(BF16)
        dqkv_ref[:, 2 * a:3 * a] = dv_ref[...]

        upper = (_lane((tm, tm)) >= _row((tm, tm))).astype(BF16)
        dlogf = _exact_nn(upper, dcum) + carry[0:1, :]
        carry[0:1, :] = dlogf[0:1, :]
        z = fl_ref[...] + bf_ref[...]
        ez = jnp.exp(-jnp.abs(z))
        sig_neg = jnp.where(z >= 0.0, ez, 1.0) / (1.0 + ez)
        dz = jnp.where(lane < N_HEADS, dlogf * sig_neg, 0.0)
        dfl_ref[...] = dz.astype(BF16)
        dbf_ref[...] += jnp.sum(dz, axis=0, keepdims=True)

    rev3 = lambda i: (0, nb - 1 - i, 0)
    rev = lambda i: (nb - 1 - i, 0)
    return pl.pallas_call(
        body, name="gate_bwd", grid=(nb,),
        in_specs=[pl.BlockSpec((N_HEADS, 128, tm), lambda i: (0, 0, nb - 1 - i)),
                  pl.BlockSpec((N_HEADS, tm, 128), rev3),
                  pl.BlockSpec((tm, a), rev), pl.BlockSpec((tm, 128), rev), _full(bf)],
        out_specs=[pl.BlockSpec((tm, 3 * a), rev), pl.BlockSpec((tm, 128), rev),
                   pl.BlockSpec((1, 128), lambda i: (0, 0))],
        out_shape=[jax.ShapeDtypeStruct((s, 3 * a), BF16),
                   jax.ShapeDtypeStruct((s, 128), BF16),
                   jax.ShapeDtypeStruct((1, 128), F32)],
        scratch_shapes=[pltpu.VMEM((8, 128), F32)],
        compiler_params=_params(("arbitrary",)),
    )(dqa, dka, dv, fl, bf)


def _in_proj_bwd(after, dqkv, dfl, dbcx, w_qkv, w_f, w_bcx, x, g, dh):
    s, d = x.shape
    tm = min(TILE_ROWS, s)

    def body(after_ref, dq_ref, df_ref, db_ref, wq_ref, wf_ref, wb_ref, x_ref, g_ref, dh_ref,
             gx_ref, dg_ref):
        i = pl.program_id(0)
        dn = (_nt(dq_ref[...], wq_ref[...]) + _nt(df_ref[...], wf_ref[...])
              + _nt(db_ref[...], wb_ref[...]))
        dx, dg = _rms_bwd(dn, x_ref[...], g_ref[...])
        gx_ref[...] = dh_ref[...] + dx

        @pl.when(i == 0)
        def _():
            dg_ref[...] = dg

        @pl.when(i > 0)
        def _():
            dg_ref[...] += dg

    rows = lambda c: pl.BlockSpec((tm, c), lambda i: (i, 0))
    return pl.pallas_call(
        body, name="in_proj_bwd", grid=(s // tm,),
        in_specs=[ANY, rows(dqkv.shape[1]), rows(dfl.shape[1]), rows(dbcx.shape[1]),
                  _full(w_qkv), _full(w_f), _full(w_bcx), rows(d), _full(g), rows(d)],
        out_specs=[rows(d), pl.BlockSpec((1, d), lambda i: (0, 0))],
        out_shape=[jax.ShapeDtypeStruct((s, d), F32), jax.ShapeDtypeStruct((1, d), F32)],
        compiler_params=_params(("arbitrary",)),
    )(after, dqkv, dfl, dbcx, w_qkv, w_f, w_bcx, x, g, dh)


def _wgrad_in(n, dys):
    s, d = n.shape
    m = len(dys)
    tk = min(TILE_ROWS, s)
    nk = s // tk

    def body(*refs):
        n_ref, dy_refs, dw_refs, accs = refs[0], refs[1:1 + m], refs[1 + m:1 + 2 * m], refs[1 + 2 * m:]
        k = pl.program_id(0)

        @pl.when(k == 0)
        def _():
            for acc in accs:
                acc[...] = jnp.zeros_like(acc)

        nb = n_ref[...]
        for dy_ref, acc in zip(dy_refs, accs):
            acc[...] += _tn(nb, dy_ref[...])

        @pl.when(k == nk - 1)
        def _():
            for dw_ref, acc in zip(dw_refs, accs):
                dw_ref[...] = acc[...].astype(BF16)

    return pl.pallas_call(
        body, name="wgrad_in", grid=(nk,),
        in_specs=[pl.BlockSpec((tk, d), lambda k: (k, 0))]
        + [pl.BlockSpec((tk, dy.shape[1]), lambda k: (k, 0)) for dy in dys],
        out_specs=[pl.BlockSpec((d, dy.shape[1]), lambda k: (0, 0)) for dy in dys],
        out_shape=[jax.ShapeDtypeStruct((d, dy.shape[1]), BF16) for dy in dys],
        scratch_shapes=[pltpu.VMEM((d, dy.shape[1]), F32) for dy in dys],
        compiler_params=_params(("arbitrary",)),
    )(n, *dys)


def _row_tile(rows):
    t = min(TILE_ELEM_ROWS, rows)
    while rows % t:
        t //= 2
    return t


def _sum_chips(sums, others, chip, name):
    _, rows, cols = sums.shape
    tr = _row_tile(rows)

    def body(chip_ref, a_ref, b_ref, o_ref):
        acc = a_ref[...].astype(F32)
        for k in range(N_CHIPS - 1):
            acc = acc + b_ref[k].astype(F32)
        o_ref[...] = acc

    return pl.pallas_call(
        body, name=name,
        grid_spec=pltpu.PrefetchScalarGridSpec(
            num_scalar_prefetch=1, grid=(rows // tr,),
            in_specs=[pl.BlockSpec((None, tr, cols), lambda i, chip_ref: (chip_ref[0], i, 0)),
                      pl.BlockSpec((N_CHIPS - 1, tr, cols), lambda i, chip_ref: (0, i, 0))],
            out_specs=pl.BlockSpec((tr, cols), lambda i, chip_ref: (i, 0))),
        out_shape=jax.ShapeDtypeStruct((rows, cols), F32),
        compiler_params=_params(("parallel",)),
    )(chip, sums, others)


def _adamw_math(w, g, m, v):
    m = ADAM_B1 * m + (1.0 - ADAM_B1) * g
    v = ADAM_B2 * v + (1.0 - ADAM_B2) * jnp.square(g)
    m_hat = m / (1.0 - ADAM_B1 ** ADAM_STEP)
    v_hat = v / (1.0 - ADAM_B2 ** ADAM_STEP)
    delta = -ADAM_LR * (m_hat / (jnp.sqrt(v_hat) + ADAM_EPS) + ADAM_WD * w)
    return delta, m, v


def _adamw(w, g, m, v, name):
    rows, cols = w.shape

    def body(w_ref, g_ref, m_ref, v_ref, d_ref, nm_ref, nv_ref):
        delta, nm, nv = _adamw_math(w_ref[...], g_ref[...], m_ref[...], v_ref[...])
        d_ref[...] = delta
        nm_ref[...] = nm
        nv_ref[...] = nv

    if rows % 8 == 0:
        tr = _row_tile(rows)
        grid, spec = (rows // tr,), pl.BlockSpec((tr, cols), lambda i: (i, 0))
    else:
        grid, spec = (cols // 256,), pl.BlockSpec((rows, 256), lambda i: (0, i))
    out = jax.ShapeDtypeStruct(w.shape, F32)
    return pl.pallas_call(
        body, name=name, grid=grid, in_specs=[spec] * 4, out_specs=[spec] * 3,
        out_shape=[out, out, out], compiler_params=_params(("parallel",)),
    )(w, g, m, v)


def _sum_devices(parts):
    def body(p_ref, g_ref):
        g = p_ref[0]
        for k in range(1, N_DEV):
            g = g + p_ref[k]
        g_ref[...] = g

    return pl.pallas_call(
        body, name="sum_devices",
        in_specs=[pl.BlockSpec(memory_space=pltpu.VMEM)],
        out_specs=pl.BlockSpec(memory_space=pltpu.VMEM),
        out_shape=jax.ShapeDtypeStruct(parts.shape[1:], F32),
    )(parts)


def _mesh_position():
    x, y, c = lax.axis_index("x"), lax.axis_index("y"), lax.axis_index("c")
    chips = [(1 - x, y), (x, 1 - y), (1 - x, 1 - y)]
    return x, y, c, chips


ANY = pl.BlockSpec(memory_space=pl.ANY)
HBM = pl.BlockSpec(memory_space=pltpu.HBM)
SEM = pl.BlockSpec(memory_space=pltpu.SEMAPHORE)
SPLIT_COPY_EFFECT = pltpu.SideEffectType.DATAFLOW_SIDE_EFFECTING


def _in_hbm(a):
    return pltpu.with_memory_space_constraint(a, pltpu.HBM)


def _chip_copies(views, srcs, lands, send, recv):
    _, _, c, chips = _mesh_position()
    cps = []
    for a in range(len(srcs)):
        for k, (px, py) in enumerate(chips):
            src, dst = views(a, k, srcs[a], lands[a], c, 2 * px + py)
            sem = a * (N_CHIPS - 1) + k
            cps.append(pltpu.make_async_remote_copy(
                src_ref=src, dst_ref=dst, send_sem=send.at[sem], recv_sem=recv.at[sem],
                device_id=(px, py, c), device_id_type=MESH))
    return cps


def _ici_start(sources, land_shapes, views, after, name):
    n = len(sources)

    def body(*refs):
        srcs, lands = refs[:n], refs[n:2 * n]
        send, recv = refs[2 * n + 1], refs[2 * n + 2]
        token = refs[-1]
        for cp in _chip_copies(views, srcs, lands, send, recv):
            cp.start()
        token[...] = jnp.zeros_like(token)

    lands = [_in_hbm(lax.empty(s.shape, s.dtype)) for s in land_shapes]
    outs = pl.pallas_call(
        body, name=name,
        in_specs=[HBM] * (2 * n) + [ANY],
        out_specs=[SEM, SEM] + [HBM] * (2 * n) + [pl.BlockSpec(memory_space=pltpu.VMEM)],
        out_shape=[pltpu.SemaphoreType.DMA((n * (N_CHIPS - 1),))] * 2
        + [pltpu.HBM(a.shape, a.dtype) for a in sources]
        + [pltpu.HBM(s.shape, s.dtype) for s in land_shapes]
        + [jax.ShapeDtypeStruct((8, 128), F32)],
        input_output_aliases={i: 2 + i for i in range(2 * n)},
        compiler_params=pltpu.CompilerParams(has_side_effects=SPLIT_COPY_EFFECT),
    )(*[_in_hbm(a) for a in sources], *lands, after)
    return outs[0], outs[1], list(outs[2:2 + n]), list(outs[2 + n:2 + 2 * n]), outs[-1]


def _ici_wait(handle, views, after, name):
    send, recv, srcs, lands, _ = handle
    n = len(srcs)

    def body(*refs):
        src_refs, land_refs = refs[:n], refs[n:2 * n]
        for cp in _chip_copies(views, src_refs, land_refs, refs[2 * n], refs[2 * n + 1]):
            cp.wait_send()
            cp.wait_recv()

    outs = pl.pallas_call(
        body, name=name,
        in_specs=[HBM] * (2 * n) + [SEM, SEM, ANY],
        out_specs=[HBM] * (2 * n),
        out_shape=[pltpu.HBM(a.shape, a.dtype) for a in srcs]
        + [pltpu.HBM(a.shape, a.dtype) for a in lands],
        input_output_aliases={i: i for i in range(2 * n)},
        compiler_params=pltpu.CompilerParams(has_side_effects=SPLIT_COPY_EFFECT),
    )(*srcs, *lands, send, recv, after)
    return list(outs[:n]), list(outs[n:])


def _gather_views(split):
    def views(a, k, src, land, c, slot):
        if split[a]:
            half = src.shape[0] // 2
            src = src.at[pl.ds(c * half, half)]
        return src, land.at[k]
    return views


def _scatter_views(a, k, src, land, c, slot):
    return src.at[slot], land.at[k]


def _gather_land_shapes(shards, split):
    return [jax.ShapeDtypeStruct(
        (N_CHIPS - 1, a.shape[0] // 2 if sp else a.shape[0]) + a.shape[1:], a.dtype)
        for a, sp in zip(shards, split)]


def _gather_finish(shards, lands, split, name):
    n = len(shards)
    ns = sum(split)
    d_index = {a: i for i, a in enumerate(a for a in range(n) if split[a])}

    def body(*refs):
        shard, land, outs = refs[:n], refs[n:2 * n], refs[2 * n:3 * n]
        obuf, fbuf = refs[3 * n:4 * n], refs[4 * n:5 * n]
        dbuf = refs[5 * n:5 * n + ns]
        ld_own, st_own, ld, st_mine, st_sib, send, recv = refs[5 * n + ns:]
        x, y, c, chips = _mesh_position()
        me = 2 * x + y
        own_loads, loads, sends, pending = [], {}, [], []
        for a in range(n):
            cp = pltpu.make_async_copy(shard[a], obuf[a], ld_own.at[a])
            cp.start()
            own_loads.append(cp)
        for a in range(n):
            for k in range(N_CHIPS - 1):
                cp = pltpu.make_async_copy(land[a].at[k], fbuf[a].at[k], ld.at[a, k])
                cp.start()
                loads[a, k] = cp
        for a in range(n):
            own_loads[a].wait()
            cp = pltpu.make_async_copy(obuf[a], outs[a].at[me], st_own.at[a])
            cp.start()
            pending.append(cp)
        for a in range(n):
            rows = shard[a].shape[0]
            for k, (px, py) in enumerate(chips):
                loads[a, k].wait()
                part = pl.ds(c * (rows // 2), rows // 2) if split[a] else pl.ds(0, rows)
                cp = pltpu.make_async_copy(fbuf[a].at[k], outs[a].at[2 * px + py, part],
                                           st_mine.at[a, k])
                cp.start()
                pending.append(cp)
                if split[a]:
                    fw = pltpu.make_async_remote_copy(
                        src_ref=fbuf[a].at[k], dst_ref=dbuf[d_index[a]].at[k],
                        send_sem=send.at[a, k], recv_sem=recv.at[a, k],
                        device_id=(x, y, 1 - c), device_id_type=MESH)
                    fw.start()
                    sends.append((a, k, fw))
        for a, k, fw in sends:
            px, py = chips[k]
            half = shard[a].shape[0] // 2
            fw.wait_recv()
            cp = pltpu.make_async_copy(dbuf[d_index[a]].at[k],
                                       outs[a].at[2 * px + py, pl.ds((1 - c) * half, half)],
                                       st_sib.at[a, k])
            cp.start()
            pending.append(cp)
        for _, _, fw in sends:
            fw.wait_send()
        for cp in pending:
            cp.wait()

    stage = [pltpu.VMEM(a.shape, a.dtype) for a in lands]
    dma = lambda *shape: pltpu.SemaphoreType.DMA(shape)
    return pl.pallas_call(
        body, name=name,
        in_specs=[ANY] * (2 * n), out_specs=[ANY] * n,
        out_shape=[jax.ShapeDtypeStruct((N_CHIPS,) + a.shape, a.dtype) for a in shards],
        scratch_shapes=[pltpu.VMEM(a.shape, a.dtype) for a in shards] + stage
        + [s for s, sp in zip(stage, split) if sp]
        + [dma(n), dma(n), dma(n, 3), dma(n, 3), dma(n, 3), dma(n, 3), dma(n, 3)],
        compiler_params=pltpu.CompilerParams(vmem_limit_bytes=VMEM_LIMIT_BYTES),
    )(*shards, *lands)


SUM_CHUNK_ROWS = 128


def _exchange_siblings(grads, name):
    n = len(grads)

    def body(*refs):
        ins, outs = refs[:n], refs[n:2 * n]
        sbuf, rbuf, mbuf = refs[2 * n:3 * n], refs[3 * n:4 * n], refs[4 * n:5 * n]
        ld_send, ld_mine, st, send, recv = refs[5 * n:]
        x, y, c, _ = _mesh_position()
        loads, mine, sends, stores = [], [], [], []
        for a in range(n):
            half = ins[a].shape[1] // 2
            cp = pltpu.make_async_copy(ins[a].at[:, pl.ds((1 - c) * half, half)], sbuf[a],
                                       ld_send.at[a])
            cp.start()
            loads.append(cp)
        for a in range(n):
            half = ins[a].shape[1] // 2
            cp = pltpu.make_async_copy(ins[a].at[:, pl.ds(c * half, half)], mbuf[a], ld_mine.at[a])
            cp.start()
            mine.append(cp)
        for a in range(n):
            loads[a].wait()
            rc = pltpu.make_async_remote_copy(
                src_ref=sbuf[a], dst_ref=rbuf[a], send_sem=send.at[a], recv_sem=recv.at[a],
                device_id=(x, y, 1 - c), device_id_type=MESH)
            rc.start()
            sends.append(rc)
        for a in range(n):
            sends[a].wait_recv()
            mine[a].wait()
            slots, half, _ = rbuf[a].shape
            rows = min(SUM_CHUNK_ROWS, half)
            per_slot = half // rows

            def add(k, carry, a=a, rows=rows, per_slot=per_slot):
                at = (k // per_slot, pl.ds(pl.multiple_of((k % per_slot) * rows, rows), rows))
                rbuf[a][at] = (rbuf[a][at].astype(F32) + mbuf[a][at].astype(F32)).astype(BF16)
                return carry

            lax.fori_loop(0, slots * per_slot, add, 0)
            cp = pltpu.make_async_copy(rbuf[a], outs[a], st.at[a])
            cp.start()
            stores.append(cp)
        for a in range(n):
            sends[a].wait_send()
            stores[a].wait()

    half_shape = lambda a: (a.shape[0], a.shape[1] // 2, a.shape[2])
    stage = [pltpu.VMEM(half_shape(a), a.dtype) for a in grads]
    return pl.pallas_call(
        body, name=name,
        in_specs=[ANY] * n, out_specs=[ANY] * n,
        out_shape=[jax.ShapeDtypeStruct(half_shape(a), a.dtype) for a in grads],
        scratch_shapes=stage * 3 + [pltpu.SemaphoreType.DMA((n,))] * 5,
        compiler_params=pltpu.CompilerParams(vmem_limit_bytes=VMEM_LIMIT_BYTES),
    )(*grads)


def _share_halves(halves):
    n = len(halves)

    def body(*refs):
        ins, outs = refs[:n], refs[n:2 * n]
        sbuf, rbuf = refs[2 * n:3 * n], refs[3 * n:4 * n]
        ld, st_own, st_sib, send, recv = refs[4 * n:]
        x, y, c, _ = _mesh_position()
        loads, sends, stores = [], [], []
        for a in range(n):
            cp = pltpu.make_async_copy(ins[a], sbuf[a], ld.at[a])
            cp.start()
            loads.append(cp)
        for a in range(n):
            half = ins[a].shape[0]
            loads[a].wait()
            rc = pltpu.make_async_remote_copy(
                src_ref=sbuf[a], dst_ref=rbuf[a], send_sem=send.at[a], recv_sem=recv.at[a],
                device_id=(x, y, 1 - c), device_id_type=MESH)
            rc.start()
            sends.append(rc)
            cp = pltpu.make_async_copy(sbuf[a], outs[a].at[pl.ds(c * half, half)], st_own.at[a])
            cp.start()
            stores.append(cp)
        for a in range(n):
            half = ins[a].shape[0]
            sends[a].wait_recv()
            cp = pltpu.make_async_copy(rbuf[a], outs[a].at[pl.ds((1 - c) * half, half)], st_sib.at[a])
            cp.start()
            stores.append(cp)
        for cp in sends:
            cp.wait_send()
        for cp in stores:
            cp.wait()

    stage = [pltpu.VMEM(a.shape, a.dtype) for a in halves]
    return pl.pallas_call(
        body, name="share_halves",
        in_specs=[ANY] * n, out_specs=[ANY] * n,
        out_shape=[jax.ShapeDtypeStruct((2 * a.shape[0],) + a.shape[1:], a.dtype)
                   for a in halves],
        scratch_shapes=stage + stage + [pltpu.SemaphoreType.DMA((n,))] * 5,
        compiler_params=pltpu.CompilerParams(vmem_limit_bytes=VMEM_LIMIT_BYTES),
    )(*halves)


def _gather_small(part):
    def body(in_ref, out_ref, send, recv, local):
        x, y, c, _ = _mesh_position()
        me = 4 * x + 2 * y + c
        cps = [pltpu.make_async_copy(in_ref, out_ref.at[me], local)]
        k = 0
        for fx in range(2):
            for fy in range(2):
                for fc in range(2):
                    if fx or fy or fc:
                        cps.append(pltpu.make_async_remote_copy(
                            src_ref=in_ref, dst_ref=out_ref.at[me], send_sem=send.at[k],
                            recv_sem=recv.at[k], device_id=(x ^ fx, y ^ fy, c ^ fc),
                            device_id_type=MESH))
                        k += 1
        for cp in cps:
            cp.start()
        for cp in cps:
            cp.wait()

    return pl.pallas_call(
        body, name="gather_small",
        in_specs=[pl.BlockSpec(memory_space=pltpu.VMEM)],
        out_specs=pl.BlockSpec(memory_space=pltpu.VMEM),
        out_shape=jax.ShapeDtypeStruct((N_DEV,) + part.shape, part.dtype),
        scratch_shapes=[pltpu.SemaphoreType.DMA((N_DEV - 1,)), pltpu.SemaphoreType.DMA((N_DEV - 1,)),
                        pltpu.SemaphoreType.DMA],
    )(part)


def _scatter_start(grads, tag):
    sums = _exchange_siblings(grads, "exchange_siblings_" + tag)
    lands = [jax.ShapeDtypeStruct((N_CHIPS - 1,) + s.shape[1:], s.dtype) for s in sums]
    return _ici_start(sums, lands, _scatter_views, grads[0], "scatter_start_" + tag)


def _scatter_finish(handle, chip, after, tag):
    sums, got = _ici_wait(handle, _scatter_views, after, "scatter_wait_" + tag)
    return [_sum_chips(s, g, chip, "sum_chips_%s_%d" % (tag, i))
            for i, (s, g) in enumerate(zip(sums, got))]


def _pad_rows(a, rows):
    return jnp.pad(a, ((0, rows - a.shape[0]), (0, 0)))


def kernel(x, norm_mix_0, w_in_0, b_f_0, conv_w_0, w_out_0, norm_ffn_0, w_up_0, w_down_0, norm_mix_1, pool_w_1, pool_scale_1, norm_ffn_1, w_up_1, w_down_1, final_norm, loss_target, m_norm_mix_0, m_w_in_0, m_b_f_0, m_conv_w_0, m_w_out_0, m_norm_ffn_0, m_w_up_0, m_w_down_0, m_norm_mix_1, m_pool_w_1, m_pool_scale_1, m_norm_ffn_1, m_w_up_1, m_w_down_1, m_final_norm, v_norm_mix_0, v_w_in_0, v_b_f_0, v_conv_w_0, v_w_out_0, v_norm_ffn_0, v_w_up_0, v_w_down_0, v_norm_mix_1, v_pool_w_1, v_pool_scale_1, v_norm_ffn_1, v_w_up_1, v_w_down_1, v_final_norm):
    d = x.shape[-1]
    a = N_HEADS * HEAD_DIM
    c_conv = conv_w_0.shape[1] * N_CHIPS
    xs = x[0]
    target = loss_target[0]
    row = lambda vec: vec.reshape(1, -1)

    big = [w_in_0, w_out_0, w_up_0, w_down_0, pool_w_1, w_up_1, w_down_1]
    first = [w_in_0.astype(BF16)]
    first_split = [True]
    rest = [w.astype(BF16) for w in (w_out_0, w_up_0, w_down_0, pool_w_1, w_up_1, w_down_1)]
    rest = rest + [conv_w_0]
    rest_split = [True] * (len(rest) - 1) + [False]
    start_a = _ici_start(first, _gather_land_shapes(first, first_split),
                         _gather_views(first_split), b_f_0, "gather_start_a")
    start_b = _ici_start(rest, _gather_land_shapes(rest, rest_split),
                         _gather_views(rest_split), start_a[-1], "gather_start_b")
    first, land_a = _ici_wait(start_a, _gather_views(first_split), start_b[-1], "gather_wait_a")
    (g_in,) = _gather_finish(first, land_a, first_split, "gather_finish_a")
    w_in = g_in.transpose(1, 0, 2).reshape(d, -1)
    w_qkv = w_in[:, :3 * a]
    w_f = jnp.pad(w_in[:, 3 * a:3 * a + N_HEADS], ((0, 0), (0, 128 - N_HEADS)))
    w_bcx = w_in[:, 3 * a + N_HEADS:]
    bf = jnp.pad(b_f_0, (0, 128 - N_HEADS)).reshape(1, 128)

    n0, qkv, fl, bcx = _ln_proj(xs, row(norm_mix_0), w_qkv, w_f, w_bcx)
    qa, ka = _gate_prep(fl, bf, qkv)
    o, lse = _attn_fwd(qa, ka, qkv)
    rest, land_b = _ici_wait(start_b, _gather_views(rest_split), o, "gather_wait_b")
    g_out, g_up0, g_down0, g_pool, g_up1, g_down1, g_conv = _gather_finish(
        rest, land_b, rest_split, "gather_finish_b")
    w_out = g_out.reshape(-1, d)
    conv_w = _pad_rows(g_conv.transpose(1, 0, 2).reshape(conv_w_0.shape[0], c_conv), 8)
    h1 = _conv_out(o, bcx, conv_w, w_out, xs)
    w_down0 = g_down0.reshape(-1, d)
    w_down1 = g_down1.reshape(-1, d)
    pool_w = g_pool.transpose(1, 0, 2, 3).reshape(pool_w_1.shape[0], -1, pool_w_1.shape[2])
    h2, a0, nf0 = _mlp_fwd(h1, row(norm_ffn_0), g_up0, w_down0, "mlp_fwd_0")
    h3 = _pool_fwd(h2, row(norm_mix_1), pool_w, row(pool_scale_1))
    h4, a1, nf1 = _mlp_fwd(h3, row(norm_ffn_1), g_up1, w_down1, "mlp_fwd_1")
    dh4, loss_part, d_final = _final_loss(h4, row(final_norm), target)

    slot_cols = g_up0.shape[2]
    pool_cols = pool_w.shape[2]
    chip_index = (2 * lax.axis_index("x") + lax.axis_index("y")).astype(jnp.int32).reshape(1)
    da1, dz1, dh3, d_nffn1 = _mlp_bwd_x(dh4, a1, g_up1, w_down1, h3, row(norm_ffn_1), "mlp_bwd_x_1")
    dw_up1, dw_down1 = _mlp_bwd_w(nf1, da1, a1, dz1, slot_cols, "mlp_bwd_w_1")
    scatter_1 = _scatter_start([dw_up1, dw_down1.reshape(N_CHIPS, -1, d)], "mlp1")
    dh2, dw_pool, d_pscale, d_nmix1 = _pool_bwd(scatter_1[-1], dh3, h2, row(norm_mix_1), pool_w,
                                                row(pool_scale_1))
    da0, dz0, dh1, d_nffn0 = _mlp_bwd_x(dh2, a0, g_up0, w_down0, h1, row(norm_ffn_0), "mlp_bwd_x_0")
    dw_up0, dw_down0 = _mlp_bwd_w(nf0, da0, a0, dz0, slot_cols, "mlp_bwd_w_0")
    dw_pool = (dw_pool.reshape(pool_w.shape[0], N_CHIPS, -1, pool_cols).transpose(1, 0, 2, 3)
               .reshape(N_CHIPS, -1, pool_cols))
    scatter_0 = _scatter_start([dw_up0, dw_down0.reshape(N_CHIPS, -1, d), dw_pool], "mlp0")
    do, delta, dbcx, dw_out, d_conv = _conv_out_bwd(scatter_0[-1], dh1, w_out, o, bcx, conv_w)
    dqa, dka, dv = _attn_bwd(qa, ka, qkv, do, lse, delta)
    dqkv, dfl, d_bf = _gate_bwd(dqa, dka, dv, fl, bf)
    dw_qkv, dw_f, dw_bcx = _wgrad_in(n0, [dqkv, dfl, dbcx])
    dw_in = jnp.concatenate([dw_qkv, dw_f[:, :N_HEADS], dw_bcx], axis=1)
    scatter_m = _scatter_start([dw_in.reshape(d, N_CHIPS, -1).transpose(1, 0, 2),
                                dw_out.reshape(N_CHIPS, -1, d)], "mixer")
    grad_x, d_nmix0 = _in_proj_bwd(scatter_m[-1], dqkv, dfl, dbcx, w_qkv, w_f, w_bcx, xs,
                                   row(norm_mix_0), dh1)

    h_up1, h_down1 = _scatter_finish(scatter_1, chip_index, grad_x, "mlp1")
    h_up0, h_down0, h_pool = _scatter_finish(scatter_0, chip_index, grad_x, "mlp0")
    h_in, h_out = _scatter_finish(scatter_m, chip_index, grad_x, "mixer")
    reduced = _share_halves([h_in, h_out, h_up0, h_down0, h_pool, h_up1, h_down1])
    moments = [(m_w_in_0, v_w_in_0), (m_w_out_0, v_w_out_0), (m_w_up_0, v_w_up_0),
               (m_w_down_0, v_w_down_0), (m_pool_w_1, v_pool_w_1), (m_w_up_1, v_w_up_1),
               (m_w_down_1, v_w_down_1)]
    big_out = []
    for k, (w, g, (m, v)) in enumerate(zip(big, reduced, moments)):
        if w.shape[-1] % 128:
            view = lambda t: t.reshape(-1, t.shape[-1]).T
            back = lambda t: t.T.reshape(w.shape)
        else:
            view = lambda t: t.reshape(-1, t.shape[-1])
            back = lambda t: t.reshape(w.shape)
        g_view = view(g)
        delta_w, new_m, new_v = _adamw(view(w), g_view, view(m), view(v), "adamw_%d" % k)
        big_out.append((back(g_view), back(delta_w), back(new_m), back(new_v)))

    tail = jnp.concatenate([d_conv[0:3].reshape(-1)[d:], d_bf[0, :N_HEADS], loss_part[0, :1]])
    small_part = jnp.concatenate(
        [d_nmix0, d_nffn0, d_nmix1, d_pscale, d_nffn1, d_final,
         d_conv[0:3].reshape(1, -1)[:, :d],
         jnp.pad(tail, (0, d - tail.shape[0])).reshape(1, d)], axis=0)
    parts = _gather_small(small_part)

    chip = 2 * lax.axis_index("x") + lax.axis_index("y")
    cw_cols = conv_w_0.shape[1]

    def conv_block(full):
        mine = lax.dynamic_slice_in_dim(full, chip * cw_cols, cw_cols, axis=1)
        return jnp.pad(mine.reshape(-1), (0, d - mine.size))

    def small_rows(vals, cw, bfv):
        return jnp.stack(list(vals) + [cw, jnp.pad(bfv, (0, d - N_HEADS))])

    smalls_w = [norm_mix_0, norm_ffn_0, norm_mix_1, pool_scale_1, norm_ffn_1, final_norm]
    smalls_m = [m_norm_mix_0, m_norm_ffn_0, m_norm_mix_1, m_pool_scale_1, m_norm_ffn_1, m_final_norm]
    smalls_v = [v_norm_mix_0, v_norm_ffn_0, v_norm_mix_1, v_pool_scale_1, v_norm_ffn_1, v_final_norm]
    pad_cw = lambda t: jnp.pad(t.reshape(-1), (0, d - t.size))
    w_rows = small_rows(smalls_w, pad_cw(conv_w_0), b_f_0)
    m_rows = small_rows(smalls_m, pad_cw(m_conv_w_0), m_b_f_0)
    v_rows = small_rows(smalls_v, pad_cw(v_conv_w_0), v_b_f_0)

    g_sum = _sum_devices(parts)
    conv_full = jnp.concatenate([g_sum[6], g_sum[7, :3 * c_conv - d]]).reshape(3, c_conv)
    bf_grad = g_sum[7, 3 * c_conv - d:3 * c_conv - d + N_HEADS]
    loss = g_sum[7, 3 * c_conv - d + N_HEADS]
    g_rows = jnp.concatenate(
        [g_sum[0:6], conv_block(conv_full).reshape(1, d),
         jnp.pad(bf_grad, (0, d - N_HEADS)).reshape(1, d)], axis=0)
    d_rows, nm_rows, nv_rows = _adamw(w_rows, g_rows, m_rows, v_rows, "adamw_small")

    def unpack(rows):
        cw = rows[6, :conv_w_0.size].reshape(conv_w_0.shape)
        return [rows[0], rows[1], rows[2], rows[3], rows[4], rows[5], cw, rows[7, :N_HEADS]]

    def assemble(kind):
        sm = unpack([g_rows, d_rows, nm_rows, nv_rows][kind])
        lg = [t[kind] for t in big_out]
        return [sm[0], lg[0], sm[7], sm[6], lg[1], sm[1], lg[2], lg[3],
                sm[2], lg[4], sm[3], sm[4], lg[5], lg[6], sm[5]]

    return (loss, grad_x[None], *assemble(0), *assemble(1), *assemble(2), *assemble(3))
```

```python
import functools

import jax
import jax.numpy as jnp
from jax import lax
from jax.experimental import pallas as pl
from jax.experimental.pallas import tpu as pltpu

F32 = jnp.float32
BF16 = jnp.bfloat16

RMS_EPS = 1e-6
HEAD_DIM = 64
N_HEADS = 8
ATTN_SCALE = HEAD_DIM ** -0.5
LOG2_E = 1.4426950408889634
POOL_WINDOWS = (2, 4, 8, 16)
POOL_HALO = 16
CONV_HALO = 8
NEG_BIG = -1e30

ADAM_LR = 0.001
ADAM_B1 = 0.9
ADAM_B2 = 0.999
ADAM_EPS = 1e-08
ADAM_WD = 0.01
ADAM_STEP = 10

N_CHIPS = 4
N_DEV = 8
MESH = pl.DeviceIdType.MESH

VMEM_LIMIT_BYTES = 56 * 1024 * 1024

TILE_ROWS = 512
TILE_ATTN = 512
TILE_MLP_ROWS = 1024
TILE_MLP_FF = 1024
TILE_MLP_BWD_FF = 512
TILE_WGRAD_K = 1024
TILE_WGRAD_N = 1024
TILE_ELEM_ROWS = 256

LANE_CQ = 64
LANE_ONE = 67


def _params(semantics):
    return pltpu.CompilerParams(dimension_semantics=semantics,
                                vmem_limit_bytes=VMEM_LIMIT_BYTES)


def _nn(a, b):
    return lax.dot_general(a, b, (((1,), (0,)), ((), ())), preferred_element_type=F32)


def _nt(a, b):
    return lax.dot_general(a, b, (((1,), (1,)), ((), ())), preferred_element_type=F32)


def _tn(a, b):
    return lax.dot_general(a, b, (((0,), (0,)), ((), ())), preferred_element_type=F32)


def _split3(v):
    hi = v.astype(BF16)
    r1 = v - hi.astype(F32)
    mid = r1.astype(BF16)
    lo = (r1 - mid.astype(F32)).astype(BF16)
    return hi, mid, lo


def _exact_nn(sel, v):
    hi, mid, lo = _split3(v)
    return _nn(sel, hi) + _nn(sel, mid) + _nn(sel, lo)


def _exact_nt(sel, v):
    hi, mid, lo = _split3(v)
    return _nt(sel, hi) + _nt(sel, mid) + _nt(sel, lo)


def _rms_fwd(x, g):
    r = lax.rsqrt(jnp.mean(x * x, axis=-1, keepdims=True) + RMS_EPS)
    return x * r * g, r


def _rms_bwd(dn, x, g):
    r = lax.rsqrt(jnp.mean(x * x, axis=-1, keepdims=True) + RMS_EPS)
    xh = x * r
    gy = dn * g
    dx = r * (gy - xh * jnp.mean(gy * xh, axis=-1, keepdims=True))
    return dx, jnp.sum(dn * xh, axis=0, keepdims=True)


def _lane(shape):
    return lax.broadcasted_iota(jnp.int32, shape, len(shape) - 1)


def _row(shape):
    return lax.broadcasted_iota(jnp.int32, shape, len(shape) - 2)


def _full(a):
    nd = a.ndim
    return pl.BlockSpec(a.shape, lambda *_: (0,) * nd)


def _ln_proj(x, g, w_qkv, w_f, w_bcx):
    s, d = x.shape
    tm = min(TILE_ROWS, s)

    def body(x_ref, g_ref, wq_ref, wf_ref, wb_ref, n_ref, qkv_ref, fl_ref, bcx_ref):
        n, _ = _rms_fwd(x_ref[...], g_ref[...])
        nb = n.astype(BF16)
        n_ref[...] = nb
        qkv_ref[...] = _nn(nb, wq_ref[...]).astype(BF16)
        fl_ref[...] = _nn(nb, wf_ref[...])
        bcx_ref[...] = _nn(nb, wb_ref[...])

    rows = lambda c: pl.BlockSpec((tm, c), lambda i: (i, 0))
    return pl.pallas_call(
        body, name="ln_proj", grid=(s // tm,),
        in_specs=[rows(d), _full(g), _full(w_qkv), _full(w_f), _full(w_bcx)],
        out_specs=[rows(d), rows(w_qkv.shape[1]), rows(w_f.shape[1]), rows(w_bcx.shape[1])],
        out_shape=[jax.ShapeDtypeStruct((s, d), BF16),
                   jax.ShapeDtypeStruct((s, w_qkv.shape[1]), BF16),
                   jax.ShapeDtypeStruct((s, w_f.shape[1]), F32),
                   jax.ShapeDtypeStruct((s, w_bcx.shape[1]), F32)],
        compiler_params=_params(("parallel",)),
    )(x, g, w_qkv, w_f, w_bcx)


def _gate_prep(fl, bf, qkv):
    s = fl.shape[0]
    a = N_HEADS * HEAD_DIM
    tm = min(TILE_ROWS, s)

    def body(fl_ref, bf_ref, q_ref, k_ref, qa_ref, ka_ref, carry_ref):
        i = pl.program_id(0)

        @pl.when(i == 0)
        def _():
            carry_ref[...] = jnp.zeros_like(carry_ref)

        z = fl_ref[...] + bf_ref[...]
        logf = jnp.minimum(z, 0.0) - jnp.log(1.0 + jnp.exp(-jnp.abs(z)))
        lower = (_lane((tm, tm)) <= _row((tm, tm))).astype(BF16)
        cum = _exact_nn(lower, logf) + carry_ref[0:1, :]
        carry_ref[0:1, :] = cum[tm - 1:tm, :]

        lane = _lane((tm, 128))
        for h in range(N_HEADS):
            cb = LOG2_E * jnp.sum(jnp.where(lane == h, cum, 0.0), axis=1, keepdims=True)
            hi, mid, lo = (p.astype(F32) for p in _split3(cb))
            pair = slice((h // 2) * 128, (h // 2 + 1) * 128)
            qp = q_ref[:, pair].astype(F32)
            kp = k_ref[:, pair].astype(F32)
            if h % 2:
                qp = pltpu.roll(qp, HEAD_DIM, axis=1)
                kp = pltpu.roll(kp, HEAD_DIM, axis=1)
            q_bias = jnp.where(lane == LANE_CQ, hi,
                               jnp.where(lane == LANE_CQ + 1, mid,
                                         jnp.where(lane == LANE_CQ + 2, lo,
                                                   jnp.where(lane < LANE_ONE + 3, 1.0, 0.0))))
            k_bias = jnp.where(lane < LANE_ONE, 1.0,
                               jnp.where(lane == LANE_ONE, -hi,
                                         jnp.where(lane == LANE_ONE + 1, -mid,
                                                   jnp.where(lane == LANE_ONE + 2, -lo, 0.0))))
            qa_ref[h] = jnp.where(lane < HEAD_DIM, qp * (ATTN_SCALE * LOG2_E), q_bias).astype(BF16)
            ka_ref[h] = jnp.where(lane < HEAD_DIM, kp, k_bias).astype(BF16)

    aug = jax.ShapeDtypeStruct((N_HEADS, s, 128), BF16)
    aug_spec = pl.BlockSpec((N_HEADS, tm, 128), lambda i: (0, i, 0))
    return pl.pallas_call(
        body, name="gate_prep", grid=(s // tm,),
        in_specs=[pl.BlockSpec((tm, 128), lambda i: (i, 0)), _full(bf),
                  pl.BlockSpec((tm, a), lambda i: (i, 0)),
                  pl.BlockSpec((tm, a), lambda i: (i, 1))],
        out_specs=[aug_spec, aug_spec],
        out_shape=[aug, aug],
        scratch_shapes=[pltpu.VMEM((8, 128), F32)],
        compiler_params=_params(("arbitrary",)),
    )(fl, bf, qkv, qkv)


def _attn_fwd(qa, ka, qkv):
    s = qa.shape[1]
    a = N_HEADS * HEAD_DIM
    t = min(TILE_ATTN, s)
    n_pairs = N_HEADS // 2
    v_block0 = 2 * a // 128

    ones_lane = (HEAD_DIM, 0)

    def body(qa_ref, ka_ref, v_ref, o_ref, lse_ref, m_ref, acc_ref, s_even, s_odd):
        i = pl.program_id(1)
        m_ref[...] = jnp.full_like(m_ref, NEG_BIG)
        acc_ref[...] = jnp.zeros_like(acc_ref)
        upper_rows = _row((128, t)) < HEAD_DIM

        def keys(j):
            return pl.ds(pl.multiple_of(j * t, t), t)

        def scores_into(buf, j):
            for e in range(2):
                buf[e] = _nt(ka_ref[e, keys(j), :], qa_ref[e])

        def consume(buf, j, masked):
            vf = v_ref[keys(j), :].astype(F32)
            lane = _lane((t, 128))
            own = [lane < HEAD_DIM, lane >= HEAD_DIM]
            for e in range(2):
                v_head = jnp.where(own[e], vf, jnp.where(lane == ones_lane[e], 1.0, 0.0)).astype(BF16)
                sc = buf[e]
                if masked:
                    sc = jnp.where(_row((t, t)) <= _lane((t, t)), sc, NEG_BIG)
                m_prev = m_ref[e]
                m_new = jnp.maximum(m_prev, jnp.max(sc, axis=0, keepdims=True))
                p = jnp.exp2(sc - m_new).astype(BF16)
                acc_ref[e] = acc_ref[e] * jnp.exp2(m_prev - m_new) + _tn(v_head, p)
                m_ref[e] = m_new

        scores_into(s_even, 0)

        def two_tiles(p, carry):
            j = 2 * p
            scores_into(s_odd, j + 1)
            consume(s_even, j, False)
            scores_into(s_even, j + 2)
            consume(s_odd, j + 1, False)
            return carry

        lax.fori_loop(0, i // 2, two_tiles, 0)

        @pl.when(i % 2 == 0)
        def _():
            consume(s_even, i, True)

        @pl.when(i % 2 == 1)
        def _():
            scores_into(s_odd, i)
            consume(s_even, i - 1, False)
            consume(s_odd, i, True)

        denom = [acc_ref[e, ones_lane[e]:ones_lane[e] + 1, :] for e in range(2)]
        out_t = jnp.where(upper_rows, acc_ref[0] / denom[0], acc_ref[1] / denom[1])
        o_ref[...] = out_t.T.astype(BF16)
        lse = [m_ref[e] + LOG2_E * jnp.log(denom[e]) for e in range(2)]
        lse_ref[...] = jnp.where(_row((8, t)) == 0, lse[0], lse[1])

    return pl.pallas_call(
        body, name="attn_fwd", grid=(n_pairs, s // t),
        in_specs=[pl.BlockSpec((2, t, 128), lambda g, i: (g, i, 0)),
                  pl.BlockSpec((2, s, 128), lambda g, i: (g, 0, 0)),
                  pl.BlockSpec((s, 128), lambda g, i: (0, v_block0 + g))],
        out_specs=[pl.BlockSpec((t, 128), lambda g, i: (i, g)),
                   pl.BlockSpec((None, 8, t), lambda g, i: (g, 0, i))],
        out_shape=[jax.ShapeDtypeStruct((s, a), BF16),
                   jax.ShapeDtypeStruct((n_pairs, 8, s), F32)],
        scratch_shapes=[pltpu.VMEM((2, 1, t), F32), pltpu.VMEM((2, 128, t), F32),
                        pltpu.VMEM((2, t, t), F32), pltpu.VMEM((2, t, t), F32)],
        compiler_params=_params(("parallel", "arbitrary")),
    )(qa, ka, qkv)


def _conv_out(o, bcx, cw, w_out, x):
    s, d = x.shape
    c = o.shape[1]
    tm = min(TILE_ROWS, s)

    def body(o_ref, b_ref, c_ref, xin_ref, cw_ref, w_ref, x_ref, h_ref, ubuf):
        i = pl.program_id(0)

        @pl.when(i == 0)
        def _():
            ubuf[0:CONV_HALO, :] = jnp.zeros((CONV_HALO, c), F32)

        u = c_ref[...] * xin_ref[...]
        ubuf[CONV_HALO:CONV_HALO + tm, :] = u
        u1 = ubuf[CONV_HALO - 1:CONV_HALO - 1 + tm, :]
        u2 = ubuf[CONV_HALO - 2:CONV_HALO - 2 + tm, :]
        cv = (cw_ref[0:1, :] * u2 + cw_ref[1:2, :] * u1) + cw_ref[2:3, :] * u
        y = (b_ref[...] * cv).astype(BF16)
        mix = _nn(o_ref[...], w_ref[0:c, :]) + _nn(y, w_ref[c:2 * c, :])
        h_ref[...] = x_ref[...] + mix
        ubuf[0:CONV_HALO, :] = u[tm - CONV_HALO:tm, :]

    col = lambda k: pl.BlockSpec((tm, c), lambda i: (i, k))
    return pl.pallas_call(
        body, name="conv_out", grid=(s // tm,),
        in_specs=[col(0), col(0), col(1), col(2), _full(cw), _full(w_out),
                  pl.BlockSpec((tm, d), lambda i: (i, 0))],
        out_specs=pl.BlockSpec((tm, d), lambda i: (i, 0)),
        out_shape=jax.ShapeDtypeStruct((s, d), F32),
        scratch_shapes=[pltpu.VMEM((tm + CONV_HALO, c), F32)],
        compiler_params=_params(("arbitrary",)),
    )(o, bcx, bcx, bcx, cw, w_out, x)


def _mlp_fwd(h, g, w_up, w_down, name, head=None):
    s, d = h.shape
    ff = w_down.shape[0]
    slot_cols = w_up.shape[2]
    tm = min(TILE_MLP_ROWS, s)
    tf = min(TILE_MLP_FF if head is None else TILE_MLP_FF // 2, slot_cols)
    per_slot = slot_cols // tf
    nf = ff // tf
    n_head = 0 if head is None else 2

    def body(*refs):
        h_ref, g_ref, wu_ref, wd_ref = refs[:4]
        out_ref, a_ref, n_ref = refs[4 + n_head:7 + n_head]
        nb_ref, acc_ref = refs[-2:]
        i = pl.program_id(0)
        f = pl.program_id(1)

        @pl.when(f == 0)
        def _():
            n, _ = _rms_fwd(h_ref[...], g_ref[...])
            nb = n.astype(BF16)
            nb_ref[...] = nb
            n_ref[...] = nb
            acc_ref[...] = jnp.zeros_like(acc_ref)

        pre = _nn(nb_ref[...], wu_ref[...])
        a_ref[...] = pre.astype(BF16)
        r = jnp.square(jnp.maximum(pre, 0.0)).astype(BF16)
        acc_ref[...] += _nn(r, wd_ref[...])

        @pl.when(f == nf - 1)
        def _():
            out = h_ref[...] + acc_ref[...]
            if head is None:
                out_ref[...] = out
            else:
                gf_ref, t_ref = refs[4:6]
                loss_ref, dg_ref = refs[7 + n_head:9 + n_head]
                y, _ = _rms_fwd(out, gf_ref[...])
                err = y - t_ref[...]
                part = 0.5 * jnp.sum(jnp.mean(err * err, axis=-1, keepdims=True), axis=0,
                                     keepdims=True)
                dx, dg = _rms_bwd(err / d, out, gf_ref[...])
                out_ref[...] = dx
                part = jnp.broadcast_to(part, loss_ref.shape)

                @pl.when(i == 0)
                def _():
                    loss_ref[...] = part
                    dg_ref[...] = dg

                @pl.when(i > 0)
                def _():
                    loss_ref[...] += part
                    dg_ref[...] += dg

    rows = pl.BlockSpec((tm, d), lambda i, f: (i, 0))
    in_specs = [rows, _full(g),
                pl.BlockSpec((None, d, tf), lambda i, f: (f // per_slot, 0, f % per_slot)),
                pl.BlockSpec((tf, d), lambda i, f: (f, 0))]
    out_specs = [rows, pl.BlockSpec((tm, tf), lambda i, f: (i, f)), rows]
    out_shape = [jax.ShapeDtypeStruct((s, d), F32), jax.ShapeDtypeStruct((s, ff), BF16),
                 jax.ShapeDtypeStruct((s, d), BF16)]
    args = [h, g, w_up, w_down]
    if head is not None:
        in_specs += [_full(head[0]), rows]
        args += list(head)
        out_specs += [pl.BlockSpec((1, 128), lambda i, f: (0, 0)),
                      pl.BlockSpec((1, d), lambda i, f: (0, 0))]
        out_shape += [jax.ShapeDtypeStruct((1, 128), F32), jax.ShapeDtypeStruct((1, d), F32)]
    return pl.pallas_call(
        body, name=name, grid=(s // tm, nf),
        in_specs=in_specs, out_specs=out_specs, out_shape=out_shape,
        scratch_shapes=[pltpu.VMEM((tm, d), BF16), pltpu.VMEM((tm, d), F32)],
        compiler_params=_params(("parallel" if head is None else "arbitrary", "arbitrary")),
    )(*args)


def _window_sum_down(e, window):
    step = 1
    while step < window:
        e = e + pltpu.roll(e, step, axis=0)
        step *= 2
    return e


def _window_sum_up(e, window):
    n = e.shape[0]
    step = 1
    while step < window:
        e = e + pltpu.roll(e, n - step, axis=0)
        step *= 2
    return e


def _pool_counts(first_row, tm, window):
    t = first_row + _row((tm, 1))
    return jnp.minimum(t + 1, window).astype(F32)


def _pool_fwd(h, g, pw, ps):
    s, d = h.shape
    cg = d // len(POOL_WINDOWS)
    tm = min(TILE_ROWS, s)

    def body(h_ref, g_ref, pw_ref, ps_ref, out_ref, nbuf):
        i = pl.program_id(0)

        @pl.when(i == 0)
        def _():
            nbuf[0:POOL_HALO, :] = jnp.zeros((POOL_HALO, d), F32)

        n, _ = _rms_fwd(h_ref[...], g_ref[...])
        nbuf[POOL_HALO:POOL_HALO + tm, :] = n
        for k, window in enumerate(POOL_WINDOWS):
            cols = slice(k * cg, (k + 1) * cg)
            sums = _window_sum_down(nbuf[:, cols], window)[POOL_HALO:, :]
            pooled = sums / _pool_counts(i * tm, tm, window) - n[:, cols]
            y = _nn(pooled.astype(BF16), pw_ref[k]) * ps_ref[:, cols]
            out_ref[:, cols] = h_ref[:, cols] + y
        nbuf[0:POOL_HALO, :] = n[tm - POOL_HALO:tm, :]

    return pl.pallas_call(
        body, name="pool_fwd", grid=(s // tm,),
        in_specs=[pl.BlockSpec((tm, d), lambda i: (i, 0)), _full(g), _full(pw), _full(ps)],
        out_specs=pl.BlockSpec((tm, d), lambda i: (i, 0)),
        out_shape=jax.ShapeDtypeStruct((s, d), F32),
        scratch_shapes=[pltpu.VMEM((tm + POOL_HALO, d), F32)],
        compiler_params=_params(("arbitrary",)),
    )(h, g, pw, ps)


def _mlp_bwd_x(dz, a, w_up, w_down, h_in, g, name):
    s, d = dz.shape
    ff = w_down.shape[0]
    slot_cols = w_up.shape[2]
    tm = min(TILE_MLP_ROWS, s)
    tf = min(TILE_MLP_BWD_FF, slot_cols)
    per_slot = slot_cols // tf
    nf = ff // tf

    def body(dz_ref, a_ref, wu_ref, wd_ref, h_ref, g_ref, da_ref, dzb_ref, dh_ref, dg_ref,
             dzs_ref, acc_ref):
        i = pl.program_id(0)
        f = pl.program_id(1)

        @pl.when(f == 0)
        def _():
            dzb = dz_ref[...].astype(BF16)
            dzs_ref[...] = dzb
            dzb_ref[...] = dzb
            acc_ref[...] = jnp.zeros_like(acc_ref)

        dr = _nt(dzs_ref[...], wd_ref[...])
        da = (dr * (2.0 * jnp.maximum(a_ref[...].astype(F32), 0.0))).astype(BF16)
        da_ref[...] = da
        acc_ref[...] += _nt(da, wu_ref[...])

        @pl.when(f == nf - 1)
        def _():
            dx, dg = _rms_bwd(acc_ref[...], h_ref[...], g_ref[...])
            dh_ref[...] = dz_ref[...] + dx

            @pl.when(i == 0)
            def _():
                dg_ref[...] = dg

            @pl.when(i > 0)
            def _():
                dg_ref[...] += dg

    return pl.pallas_call(
        body, name=name, grid=(s // tm, nf),
        in_specs=[pl.BlockSpec((tm, d), lambda i, f: (i, 0)),
                  pl.BlockSpec((tm, tf), lambda i, f: (i, f)),
                  pl.BlockSpec((None, d, tf), lambda i, f: (f // per_slot, 0, f % per_slot)),
                  pl.BlockSpec((tf, d), lambda i, f: (f, 0)),
                  pl.BlockSpec((tm, d), lambda i, f: (i, 0)), _full(g)],
        out_specs=[pl.BlockSpec((tm, tf), lambda i, f: (i, f)),
                   pl.BlockSpec((tm, d), lambda i, f: (i, 0)),
                   pl.BlockSpec((tm, d), lambda i, f: (i, 0)),
                   pl.BlockSpec((1, d), lambda i, f: (0, 0))],
        out_shape=[jax.ShapeDtypeStruct((s, ff), BF16),
                   jax.ShapeDtypeStruct((s, d), BF16),
                   jax.ShapeDtypeStruct((s, d), F32),
                   jax.ShapeDtypeStruct((1, d), F32)],
        scratch_shapes=[pltpu.VMEM((tm, d), BF16), pltpu.VMEM((tm, d), F32)],
        compiler_params=_params(("arbitrary", "arbitrary")),
    )(dz, a, w_up, w_down, h_in, g)


def _mlp_bwd_w(n, da, a, dzb, slot_cols, name):
    s, d = n.shape
    ff = a.shape[1]
    tn = min(TILE_WGRAD_N, slot_cols)
    tk = min(TILE_WGRAD_K, s)
    per_slot = slot_cols // tn
    nk = s // tk

    def body(n_ref, da_ref, a_ref, dz_ref, du_ref, dd_ref, accu_ref, accd_ref):
        k = pl.program_id(1)

        @pl.when(k == 0)
        def _():
            accu_ref[...] = jnp.zeros_like(accu_ref)
            accd_ref[...] = jnp.zeros_like(accd_ref)

        accu_ref[...] += _tn(n_ref[...], da_ref[...])
        r = jnp.square(jnp.maximum(a_ref[...].astype(F32), 0.0)).astype(BF16)
        accd_ref[...] += _tn(r, dz_ref[...])

        @pl.when(k == nk - 1)
        def _():
            du_ref[...] = accu_ref[...].astype(BF16)
            dd_ref[...] = accd_ref[...].astype(BF16)

    return pl.pallas_call(
        body, name=name, grid=(ff // tn, nk),
        in_specs=[pl.BlockSpec((tk, d), lambda f, k: (k, 0)),
                  pl.BlockSpec((tk, tn), lambda f, k: (k, f)),
                  pl.BlockSpec((tk, tn), lambda f, k: (k, f)),
                  pl.BlockSpec((tk, d), lambda f, k: (k, 0))],
        out_specs=[pl.BlockSpec((None, d, tn), lambda f, k: (f // per_slot, 0, f % per_slot)),
                   pl.BlockSpec((tn, d), lambda f, k: (f, 0))],
        out_shape=[jax.ShapeDtypeStruct((ff // slot_cols, d, slot_cols), BF16),
                   jax.ShapeDtypeStruct((ff, d), BF16)],
        scratch_shapes=[pltpu.VMEM((d, tn), F32), pltpu.VMEM((tn, d), F32)],
        compiler_params=_params(("parallel", "arbitrary")),
    )(n, da, a, dzb)


def _pool_bwd(after, dh, h, g, pw, ps):
    s, d = h.shape
    cg = d // len(POOL_WINDOWS)
    tm = min(TILE_ROWS, s)
    nb = s // tm
    halo_per_tile = tm // POOL_HALO

    def body(after_ref, dh_ref, h_ref, halo_ref, g_ref, pw_ref, ps_ref,
             dx_ref, dpw_ref, dps_ref, dg_ref, nbuf, qbuf, dn_ref, carry, dpw_acc):
        i = pl.program_id(0)
        blk = nb - 1 - i

        @pl.when(i == 0)
        def _():
            carry[...] = jnp.zeros_like(carry)
            dpw_acc[...] = jnp.zeros_like(dpw_acc)
            dps_ref[...] = jnp.zeros_like(dps_ref)
            dg_ref[...] = jnp.zeros_like(dg_ref)

        hv = h_ref[...]
        n, _ = _rms_fwd(hv, g_ref[...])
        nh, _ = _rms_fwd(halo_ref[...], g_ref[...])
        nbuf[0:POOL_HALO, :] = jnp.where(blk == 0, 0.0, nh)
        nbuf[POOL_HALO:POOL_HALO + tm, :] = n
        dhv = dh_ref[...]
        for k, window in enumerate(POOL_WINDOWS):
            cols = slice(k * cg, (k + 1) * cg)
            cnt = _pool_counts(blk * tm, tm, window)
            sums = _window_sum_down(nbuf[:, cols], window)[POOL_HALO:, :]
            pb = (sums / cnt - n[:, cols]).astype(BF16)
            dyk = dhv[:, cols]
            dps_ref[:, cols] += jnp.sum(dyk * _nn(pb, pw_ref[k]), axis=0, keepdims=True)
            dyb = (dyk * ps_ref[:, cols]).astype(BF16)
            dpw_acc[k] += _tn(pb, dyb)
            dpool = _nt(dyb, pw_ref[k])
            qv = dpool / cnt
            qbuf[0:tm, cols] = qv
            qbuf[tm:tm + POOL_HALO, cols] = carry[:, cols]
            dn_ref[:, cols] = _window_sum_up(qbuf[:, cols], window)[0:tm, :] - dpool
            carry[:, cols] = qv[0:POOL_HALO, :]
        dx, dg = _rms_bwd(dn_ref[...], hv, g_ref[...])
        dx_ref[...] = dhv + dx
        dg_ref[...] += dg

        @pl.when(i == nb - 1)
        def _():
            dpw_ref[...] = dpw_acc[...].astype(BF16)

    rev = lambda i: (nb - 1 - i, 0)
    return pl.pallas_call(
        body, name="pool_bwd", grid=(nb,),
        in_specs=[ANY, pl.BlockSpec((tm, d), rev), pl.BlockSpec((tm, d), rev),
                  pl.BlockSpec((POOL_HALO, d),
                               lambda i: (jnp.maximum((nb - 1 - i) * halo_per_tile - 1, 0), 0)),
                  _full(g), _full(pw), _full(ps)],
        out_specs=[pl.BlockSpec((tm, d), rev), _full(pw),
                   pl.BlockSpec((1, d), lambda i: (0, 0)),
                   pl.BlockSpec((1, d), lambda i: (0, 0))],
        out_shape=[jax.ShapeDtypeStruct((s, d), F32),
                   jax.ShapeDtypeStruct(pw.shape, BF16),
                   jax.ShapeDtypeStruct((1, d), F32),
                   jax.ShapeDtypeStruct((1, d), F32)],
        scratch_shapes=[pltpu.VMEM((tm + POOL_HALO, d), F32), pltpu.VMEM((tm + POOL_HALO, d), F32),
                        pltpu.VMEM((tm, d), F32), pltpu.VMEM((POOL_HALO, d), F32),
                        pltpu.VMEM(pw.shape, F32)],
        compiler_params=_params(("arbitrary",)),
    )(after, dh, h, h, g, pw, ps)


def _conv_out_bwd(after, dh, w_out, o, bcx, cw):
    s, d = dh.shape
    c = o.shape[1]
    tm = min(TILE_ROWS, s)
    nb = s // tm
    halo_per_tile = tm // CONV_HALO

    def body(after_ref, dh_ref, w_ref, o_ref, b_ref, c_ref, xin_ref, ch_ref, xh_ref, cw_ref,
             do_ref, delta_ref, dbcx_ref, dw_ref, dcw_ref, ubuf, dbuf, carry, acc):
        i = pl.program_id(0)
        blk = nb - 1 - i

        @pl.when(i == 0)
        def _():
            carry[...] = jnp.zeros_like(carry)
            acc[...] = jnp.zeros_like(acc)
            dcw_ref[...] = jnp.zeros_like(dcw_ref)

        dm = dh_ref[...].astype(BF16)
        dcat = _nt(dm, w_ref[...])
        do = dcat[:, 0:c]
        dy = dcat[:, c:2 * c]
        do_ref[...] = do.astype(BF16)
        head_of_lane = lax.shift_right_logical(_lane((8, c)), HEAD_DIM.bit_length() - 1)
        heads = (head_of_lane == _row((8, c))).astype(BF16)
        delta_ref[...] = _exact_nt(heads, do * o_ref[...].astype(F32))

        cv_ = c_ref[...]
        xin = xin_ref[...]
        bv = b_ref[...]
        u = cv_ * xin
        ubuf[0:CONV_HALO, :] = jnp.where(blk == 0, 0.0, ch_ref[...] * xh_ref[...])
        ubuf[CONV_HALO:CONV_HALO + tm, :] = u
        u1 = ubuf[CONV_HALO - 1:CONV_HALO - 1 + tm, :]
        u2 = ubuf[CONV_HALO - 2:CONV_HALO - 2 + tm, :]
        w0, w1, w2 = cw_ref[0:1, :], cw_ref[1:2, :], cw_ref[2:3, :]
        cv = (w0 * u2 + w1 * u1) + w2 * u
        acc[0:c, :] += _tn(o_ref[...], dm)
        acc[c:2 * c, :] += _tn((bv * cv).astype(BF16), dm)

        dcv = dy * bv
        dcw_ref[0:1, :] += jnp.sum(dcv * u2, axis=0, keepdims=True)
        dcw_ref[1:2, :] += jnp.sum(dcv * u1, axis=0, keepdims=True)
        dcw_ref[2:3, :] += jnp.sum(dcv * u, axis=0, keepdims=True)
        dbuf[0:tm, :] = dcv
        dbuf[tm:tm + CONV_HALO, :] = carry[...]
        du = w2 * dcv + w1 * dbuf[1:1 + tm, :] + w0 * dbuf[2:2 + tm, :]
        dbcx_ref[:, 0:c] = (dy * cv).astype(BF16)
        dbcx_ref[:, c:2 * c] = (du * xin).astype(BF16)
        dbcx_ref[:, 2 * c:3 * c] = (du * cv_).astype(BF16)
        carry[...] = dcv[0:CONV_HALO, :]

        @pl.when(i == nb - 1)
        def _():
            dw_ref[...] = acc[...].astype(BF16)

    rev = lambda k: (lambda i: (nb - 1 - i, k))
    halo = lambda k: (lambda i: (jnp.maximum((nb - 1 - i) * halo_per_tile - 1, 0), k))
    return pl.pallas_call(
        body, name="conv_out_bwd", grid=(nb,),
        in_specs=[ANY, pl.BlockSpec((tm, d), rev(0)), _full(w_out), pl.BlockSpec((tm, c), rev(0)),
                  pl.BlockSpec((tm, c), rev(0)), pl.BlockSpec((tm, c), rev(1)),
                  pl.BlockSpec((tm, c), rev(2)),
                  pl.BlockSpec((CONV_HALO, c), halo(1)), pl.BlockSpec((CONV_HALO, c), halo(2)),
                  _full(cw)],
        out_specs=[pl.BlockSpec((tm, c), rev(0)),
                   pl.BlockSpec((8, tm), lambda i: (0, nb - 1 - i)),
                   pl.BlockSpec((tm, 3 * c), rev(0)),
                   _full(w_out), _full(cw)],
        out_shape=[jax.ShapeDtypeStruct((s, c), BF16),
                   jax.ShapeDtypeStruct((8, s), F32),
                   jax.ShapeDtypeStruct((s, 3 * c), BF16),
                   jax.ShapeDtypeStruct(w_out.shape, BF16),
                   jax.ShapeDtypeStruct(cw.shape, F32)],
        scratch_shapes=[pltpu.VMEM((tm + CONV_HALO, c), F32), pltpu.VMEM((tm + CONV_HALO, c), F32),
                        pltpu.VMEM((CONV_HALO, c), F32), pltpu.VMEM(w_out.shape, F32)],
        compiler_params=_params(("arbitrary",)),
    )(after, dh, w_out, o, bcx, bcx, bcx, bcx, bcx, cw)


def _attn_bwd(qa, ka, qkv, do, lse, delta):
    s = qa.shape[1]
    a = N_HEADS * HEAD_DIM
    t = min(TILE_ATTN, s)
    nq = s // t
    n_pairs = N_HEADS // 2
    v_block0 = 2 * a // 128

    def body(ka_ref, v_ref, qa_ref, do_ref, lse_ref, delta_ref,
             dqt_ref, dka_ref, dv_ref, dk_acc, dv_acc):
        g = pl.program_id(0)
        j = pl.program_id(1)

        @pl.when(j == 0)
        def _():
            dqt_ref[...] = jnp.zeros_like(dqt_ref)

        dk_acc[...] = jnp.zeros_like(dk_acc)
        dv_acc[...] = jnp.zeros_like(dv_acc)
        lane = _lane((t, 128))
        vf = v_ref[...].astype(F32)
        v_heads = [jnp.where(lane < HEAD_DIM, vf, 0.0).astype(BF16),
                   jnp.where(lane >= HEAD_DIM, vf, 0.0).astype(BF16)]
        ke_t = [ka_ref[e].astype(F32).T.astype(BF16) for e in range(2)]

        def q_step(i, masked):
            qs = pl.ds(pl.multiple_of(i * t, t), t)
            dob = do_ref[qs, :]
            for e in range(2):
                qe = qa_ref[e, qs, :]
                sc = _nt(ka_ref[e], qe)
                if masked:
                    sc = jnp.where(_row((t, t)) <= _lane((t, t)), sc, NEG_BIG)
                p = jnp.exp2(sc - lse_ref[pl.ds(e, 1), qs])
                dv_acc[e] += _nn(p.astype(BF16), dob)
                dp = _nt(v_heads[e], dob)
                ds = (p * (dp - delta_ref[pl.ds(2 * g + e, 1), qs])).astype(BF16)
                dk_acc[e] += _nn(ds, qe)
                dqt_ref[e, :, qs] += _nn(ke_t[e], ds)

        q_step(j, True)

        def full_step(i, carry):
            q_step(i, False)
            return carry

        lax.fori_loop(j + 1, nq, full_step, 0)
        dka_ref[...] = dk_acc[...]
        dv_ref[...] = jnp.where(lane < HEAD_DIM, dv_acc[0], dv_acc[1]).astype(BF16)

    return pl.pallas_call(
        body, name="attn_bwd", grid=(n_pairs, nq),
        in_specs=[pl.BlockSpec((2, t, 128), lambda g, j: (g, j, 0)),
                  pl.BlockSpec((t, 128), lambda g, j: (j, v_block0 + g)),
                  pl.BlockSpec((2, s, 128), lambda g, j: (g, 0, 0)),
                  pl.BlockSpec((s, 128), lambda g, j: (0, g)),
                  pl.BlockSpec((None, 8, s), lambda g, j: (g, 0, 0)),
                  pl.BlockSpec((8, s), lambda g, j: (0, 0))],
        out_specs=[pl.BlockSpec((2, 128, s), lambda g, j: (g, 0, 0)),
                   pl.BlockSpec((2, t, 128), lambda g, j: (g, j, 0)),
                   pl.BlockSpec((t, 128), lambda g, j: (j, g))],
        out_shape=[jax.ShapeDtypeStruct((N_HEADS, 128, s), F32),
                   jax.ShapeDtypeStruct((N_HEADS, s, 128), F32),
                   jax.ShapeDtypeStruct((s, a), BF16)],
        scratch_shapes=[pltpu.VMEM((2, t, 128), F32), pltpu.VMEM((2, t, 128), F32)],
        compiler_params=_params(("parallel", "arbitrary")),
    )(ka, qkv, qa, do, lse, delta)


def _gate_bwd(dqa, dka, dv, fl, bf):
    s = fl.shape[0]
    a = N_HEADS * HEAD_DIM
    tm = min(TILE_ROWS, s)
    nb = s // tm

    def body(dqa_ref, dka_ref, dv_ref, fl_ref, bf_ref, dqkv_ref, dfl_ref, dbf_ref, carry):
        i = pl.program_id(0)

        @pl.when(i == 0)
        def _():
            carry[...] = jnp.zeros_like(carry)
            dbf_ref[...] = jnp.zeros_like(dbf_ref)

        lane = _lane((tm, 128))
        dcum = jnp.zeros((tm, 128), F32)
        for pair in range(N_HEADS // 2):
            qs, ks = [], []
            for e in range(2):
                h = 2 * pair + e
                dq = dqa_ref[h].T
                dk = dka_ref[h]
                dc = jnp.sum(jnp.where(lane == LANE_CQ, dq, 0.0)
                             - jnp.where(lane == LANE_ONE, dk, 0.0), axis=1, keepdims=True)
                dcum = jnp.where(lane == h, dc, dcum)
                qs.append(dq * ATTN_SCALE)
                ks.append(dk * (1.0 / LOG2_E))
            cols = slice(pair * 128, (pair + 1) * 128)
            dqkv_ref[:, cols] = jnp.where(
                lane < HEAD_DIM, qs[0], pltpu.roll(qs[1], HEAD_DIM, axis=1)).astype(BF16)
            dqkv_ref[:, a + pair * 128:a + (pair + 1) * 128] = jnp.where(
                lane < HEAD_DIM, ks[0], pltpu.roll(ks[1], HEAD_DIM, axis=1)).astype(BF16)
        dqkv_ref[:, 2 * a:3 * a] = dv_ref[...]

        upper = (_lane((tm, tm)) >= _row((tm, tm))).astype(BF16)
        dlogf = _exact_nn(upper, dcum) + carry[0:1, :]
        carry[0:1, :] = dlogf[0:1, :]
        z = fl_ref[...] + bf_ref[...]
        ez = jnp.exp(-jnp.abs(z))
        sig_neg = jnp.where(z >= 0.0, ez, 1.0) / (1.0 + ez)
        dz = jnp.where(lane < N_HEADS, dlogf * sig_neg, 0.0)
        dfl_ref[...] = dz.astype(BF16)
        dbf_ref[...] += jnp.sum(dz, axis=0, keepdims=True)

    rev3 = lambda i: (0, nb - 1 - i, 0)
    rev = lambda i: (nb - 1 - i, 0)
    return pl.pallas_call(
        body, name="gate_bwd", grid=(nb,),
        in_specs=[pl.BlockSpec((N_HEADS, 128, tm), lambda i: (0, 0, nb - 1 - i)),
                  pl.BlockSpec((N_HEADS, tm, 128), rev3),
                  pl.BlockSpec((tm, a), rev), pl.BlockSpec((tm, 128), rev), _full(bf)],
        out_specs=[pl.BlockSpec((tm, 3 * a), rev), pl.BlockSpec((tm, 128), rev),
                   pl.BlockSpec((1, 128), lambda i: (0, 0))],
        out_shape=[jax.ShapeDtypeStruct((s, 3 * a), BF16),
                   jax.ShapeDtypeStruct((s, 128), BF16),
                   jax.ShapeDtypeStruct((1, 128), F32)],
        scratch_shapes=[pltpu.VMEM((8, 128), F32)],
        compiler_params=_params(("arbitrary",)),
    )(dqa, dka, dv, fl, bf)


def _in_proj_bwd(after, dqkv, dfl, dbcx, w_qkv, w_f, w_bcx, x, g, dh):
    s, d = x.shape
    tm = min(TILE_ROWS, s)

    def body(after_ref, dq_ref, df_ref, db_ref, wq_ref, wf_ref, wb_ref, x_ref, g_ref, dh_ref,
             gx_ref, dg_ref):
        i = pl.program_id(0)
        dn = (_nt(dq_ref[...], wq_ref[...]) + _nt(df_ref[...], wf_ref[...])
              + _nt(db_ref[...], wb_ref[...]))
        dx, dg = _rms_bwd(dn, x_ref[...], g_ref[...])
        gx_ref[...] = dh_ref[...] + dx

        @pl.when(i == 0)
        def _():
            dg_ref[...] = dg

        @pl.when(i > 0)
        def _():
            dg_ref[...] += dg

    rows = lambda c: pl.BlockSpec((tm, c), lambda i: (i, 0))
    return pl.pallas_call(
        body, name="in_proj_bwd", grid=(s // tm,),
        in_specs=[ANY, rows(dqkv.shape[1]), rows(dfl.shape[1]), rows(dbcx.shape[1]),
                  _full(w_qkv), _full(w_f), _full(w_bcx), rows(d), _full(g), rows(d)],
        out_specs=[rows(d), pl.BlockSpec((1, d), lambda i: (0, 0))],
        out_shape=[jax.ShapeDtypeStruct((s, d), F32), jax.ShapeDtypeStruct((1, d), F32)],
        compiler_params=_params(("arbitrary",)),
    )(after, dqkv, dfl, dbcx, w_qkv, w_f, w_bcx, x, g, dh)


def _wgrad_in(n, dys):
    s, d = n.shape
    m = len(dys)
    tk = min(TILE_ROWS, s)
    nk = s // tk

    def body(*refs):
        n_ref, dy_refs, dw_refs, accs = refs[0], refs[1:1 + m], refs[1 + m:1 + 2 * m], refs[1 + 2 * m:]
        k = pl.program_id(0)

        @pl.when(k == 0)
        def _():
            for acc in accs:
                acc[...] = jnp.zeros_like(acc)

        nb = n_ref[...]
        for dy_ref, acc in zip(dy_refs, accs):
            acc[...] += _tn(nb, dy_ref[...])

        @pl.when(k == nk - 1)
        def _():
            for dw_ref, acc in zip(dw_refs, accs):
                dw_ref[...] = acc[...].astype(BF16)

    return pl.pallas_call(
        body, name="wgrad_in", grid=(nk,),
        in_specs=[pl.BlockSpec((tk, d), lambda k: (k, 0))]
        + [pl.BlockSpec((tk, dy.shape[1]), lambda k: (k, 0)) for dy in dys],
        out_specs=[pl.BlockSpec((d, dy.shape[1]), lambda k: (0, 0)) for dy in dys],
        out_shape=[jax.ShapeDtypeStruct((d, dy.shape[1]), BF16) for dy in dys],
        scratch_shapes=[pltpu.VMEM((d, dy.shape[1]), F32) for dy in dys],
        compiler_params=_params(("arbitrary",)),
    )(n, *dys)


def _row_tile(rows):
    t = min(TILE_ELEM_ROWS, rows)
    while rows % t:
        t //= 2
    return t


def _adamw_math(w, g, m, v):
    m = ADAM_B1 * m + (1.0 - ADAM_B1) * g
    v = ADAM_B2 * v + (1.0 - ADAM_B2) * jnp.square(g)
    m_hat = m / (1.0 - ADAM_B1 ** ADAM_STEP)
    v_hat = v / (1.0 - ADAM_B2 ** ADAM_STEP)
    delta = -ADAM_LR * (m_hat / (jnp.sqrt(v_hat) + ADAM_EPS) + ADAM_WD * w)
    return delta, m, v


def _adamw(w, g, m, v, name):
    rows, cols = w.shape

    def body(w_ref, g_ref, m_ref, v_ref, d_ref, nm_ref, nv_ref):
        delta, nm, nv = _adamw_math(w_ref[...], g_ref[...], m_ref[...], v_ref[...])
        d_ref[...] = delta
        nm_ref[...] = nm
        nv_ref[...] = nv

    if rows % 8 == 0:
        tr = _row_tile(rows)
        grid, spec = (rows // tr,), pl.BlockSpec((tr, cols), lambda i: (i, 0))
    else:
        grid, spec = (cols // 256,), pl.BlockSpec((rows, 256), lambda i: (0, i))
    out = jax.ShapeDtypeStruct(w.shape, F32)
    return pl.pallas_call(
        body, name=name, grid=grid, in_specs=[spec] * 4, out_specs=[spec] * 3,
        out_shape=[out, out, out], compiler_params=_params(("parallel",)),
    )(w, g, m, v)


def _sum_devices(parts):
    def body(p_ref, g_ref):
        g = p_ref[0]
        for k in range(1, N_DEV):
            g = g + p_ref[k]
        g_ref[...] = g

    return pl.pallas_call(
        body, name="sum_devices",
        in_specs=[pl.BlockSpec(memory_space=pltpu.VMEM)],
        out_specs=pl.BlockSpec(memory_space=pltpu.VMEM),
        out_shape=jax.ShapeDtypeStruct(parts.shape[1:], F32),
    )(parts)


def _mesh_position():
    x, y, c = lax.axis_index("x"), lax.axis_index("y"), lax.axis_index("c")
    chips = [(1 - x, y), (x, 1 - y), (1 - x, 1 - y)]
    return x, y, c, chips


ANY = pl.BlockSpec(memory_space=pl.ANY)
HBM = pl.BlockSpec(memory_space=pltpu.HBM)
SEM = pl.BlockSpec(memory_space=pltpu.SEMAPHORE)
SPLIT_COPY_EFFECT = pltpu.SideEffectType.DATAFLOW_SIDE_EFFECTING


def _in_hbm(a):
    return pltpu.with_memory_space_constraint(a, pltpu.HBM)


def _chip_copies(views, srcs, lands, send, recv):
    _, _, c, chips = _mesh_position()
    cps = []
    for a in range(len(srcs)):
        for k, (px, py) in enumerate(chips):
            src, dst = views(a, k, srcs[a], lands[a], c, 2 * px + py)
            sem = a * (N_CHIPS - 1) + k
            cps.append(pltpu.make_async_remote_copy(
                src_ref=src, dst_ref=dst, send_sem=send.at[sem], recv_sem=recv.at[sem],
                device_id=(px, py, c), device_id_type=MESH))
    return cps


def _ici_start(sources, land_shapes, views, after, name):
    n = len(sources)

    def body(*refs):
        srcs, lands = refs[:n], refs[n:2 * n]
        send, recv = refs[2 * n + 1], refs[2 * n + 2]
        token = refs[-1]
        for cp in _chip_copies(views, srcs, lands, send, recv):
            cp.start()
        token[...] = jnp.zeros_like(token)

    lands = [_in_hbm(lax.empty(s.shape, s.dtype)) for s in land_shapes]
    outs = pl.pallas_call(
        body, name=name,
        in_specs=[HBM] * (2 * n) + [ANY],
        out_specs=[SEM, SEM] + [HBM] * (2 * n) + [pl.BlockSpec(memory_space=pltpu.VMEM)],
        out_shape=[pltpu.SemaphoreType.DMA((n * (N_CHIPS - 1),))] * 2
        + [pltpu.HBM(a.shape, a.dtype) for a in sources]
        + [pltpu.HBM(s.shape, s.dtype) for s in land_shapes]
        + [jax.ShapeDtypeStruct((8, 128), F32)],
        input_output_aliases={i: 2 + i for i in range(2 * n)},
        compiler_params=pltpu.CompilerParams(has_side_effects=SPLIT_COPY_EFFECT),
    )(*[_in_hbm(a) for a in sources], *lands, after)
    return outs[0], outs[1], list(outs[2:2 + n]), list(outs[2 + n:2 + 2 * n]), outs[-1]


def _ici_wait(handle, views, after, name):
    send, recv, srcs, lands, _ = handle
    n = len(srcs)

    def body(*refs):
        src_refs, land_refs = refs[:n], refs[n:2 * n]
        for cp in _chip_copies(views, src_refs, land_refs, refs[2 * n], refs[2 * n + 1]):
            cp.wait_send()
            cp.wait_recv()

    outs = pl.pallas_call(
        body, name=name,
        in_specs=[HBM] * (2 * n) + [SEM, SEM, ANY],
        out_specs=[HBM] * (2 * n),
        out_shape=[pltpu.HBM(a.shape, a.dtype) for a in srcs]
        + [pltpu.HBM(a.shape, a.dtype) for a in lands],
        input_output_aliases={i: i for i in range(2 * n)},
        compiler_params=pltpu.CompilerParams(has_side_effects=SPLIT_COPY_EFFECT),
    )(*srcs, *lands, send, recv, after)
    return list(outs[:n]), list(outs[n:])


def _gather_views(split):
    def views(a, k, src, land, c, slot):
        if split[a]:
            half = src.shape[0] // 2
            src = src.at[pl.ds(c * half, half)]
        return src, land.at[k]
    return views


def _scatter_views(a, k, src, land, c, slot):
    return src.at[slot], land.at[k]


def _gather_land_shapes(shards, split):
    return [jax.ShapeDtypeStruct(
        (N_CHIPS - 1, a.shape[0] // 2 if sp else a.shape[0]) + a.shape[1:], a.dtype)
        for a, sp in zip(shards, split)]


def _gather_finish(shards, lands, split, name):
    n = len(shards)
    ns = sum(split)
    d_index = {a: i for i, a in enumerate(a for a in range(n) if split[a])}

    def body(*refs):
        shard, land, outs = refs[:n], refs[n:2 * n], refs[2 * n:3 * n]
        obuf, fbuf = refs[3 * n:4 * n], refs[4 * n:5 * n]
        dbuf = refs[5 * n:5 * n + ns]
        ld_own, st_own, ld, st_mine, st_sib, send, recv = refs[5 * n + ns:]
        x, y, c, chips = _mesh_position()
        me = 2 * x + y
        own_loads, loads, sends, pending = [], {}, [], []
        for a in range(n):
            cp = pltpu.make_async_copy(shard[a], obuf[a], ld_own.at[a])
            cp.start()
            own_loads.append(cp)
        for a in range(n):
            for k in range(N_CHIPS - 1):
                cp = pltpu.make_async_copy(land[a].at[k], fbuf[a].at[k], ld.at[a, k])
                cp.start()
                loads[a, k] = cp
        for a in range(n):
            own_loads[a].wait()
            cp = pltpu.make_async_copy(obuf[a], outs[a].at[me], st_own.at[a])
            cp.start()
            pending.append(cp)
        for a in range(n):
            rows = shard[a].shape[0]
            for k, (px, py) in enumerate(chips):
                loads[a, k].wait()
                part = pl.ds(c * (rows // 2), rows // 2) if split[a] else pl.ds(0, rows)
                cp = pltpu.make_async_copy(fbuf[a].at[k], outs[a].at[2 * px + py, part],
                                           st_mine.at[a, k])
                cp.start()
                pending.append(cp)
                if split[a]:
                    fw = pltpu.make_async_remote_copy(
                        src_ref=fbuf[a].at[k], dst_ref=dbuf[d_index[a]].at[k],
                        send_sem=send.at[a, k], recv_sem=recv.at[a, k],
                        device_id=(x, y, 1 - c), device_id_type=MESH)
                    fw.start()
                    sends.append((a, k, fw))
        for a, k, fw in sends:
            px, py = chips[k]
            half = shard[a].shape[0] // 2
            fw.wait_recv()
            cp = pltpu.make_async_copy(dbuf[d_index[a]].at[k],
                                       outs[a].at[2 * px + py, pl.ds((1 - c) * half, half)],
                                       st_sib.at[a, k])
            cp.start()
            pending.append(cp)
        for _, _, fw in sends:
            fw.wait_send()
        for cp in pending:
            cp.wait()

    stage = [pltpu.VMEM(a.shape, a.dtype) for a in lands]
    dma = lambda *shape: pltpu.SemaphoreType.DMA(shape)
    return pl.pallas_call(
        body, name=name,
        in_specs=[ANY] * (2 * n), out_specs=[ANY] * n,
        out_shape=[jax.ShapeDtypeStruct((N_CHIPS,) + a.shape, a.dtype) for a in shards],
        scratch_shapes=[pltpu.VMEM(a.shape, a.dtype) for a in shards] + stage
        + [s for s, sp in zip(stage, split) if sp]
        + [dma(n), dma(n), dma(n, 3), dma(n, 3), dma(n, 3), dma(n, 3), dma(n, 3)],
        compiler_params=pltpu.CompilerParams(vmem_limit_bytes=VMEM_LIMIT_BYTES),
    )(*shards, *lands)


SUM_CHUNK_ROWS = 128


def _exchange_siblings(grads, name):
    n = len(grads)

    def body(*refs):
        ins, outs = refs[:n], refs[n:2 * n]
        sbuf, rbuf, mbuf = refs[2 * n:3 * n], refs[3 * n:4 * n], refs[4 * n:5 * n]
        ld_send, ld_mine, st, send, recv = refs[5 * n:]
        x, y, c, _ = _mesh_position()
        loads, mine, sends, stores = [], [], [], []
        for a in range(n):
            half = ins[a].shape[1] // 2
            cp = pltpu.make_async_copy(ins[a].at[:, pl.ds((1 - c) * half, half)], sbuf[a],
                                       ld_send.at[a])
            cp.start()
            loads.append(cp)
        for a in range(n):
            half = ins[a].shape[1] // 2
            cp = pltpu.make_async_copy(ins[a].at[:, pl.ds(c * half, half)], mbuf[a], ld_mine.at[a])
            cp.start()
            mine.append(cp)
        for a in range(n):
            loads[a].wait()
            rc = pltpu.make_async_remote_copy(
                src_ref=sbuf[a], dst_ref=rbuf[a], send_sem=send.at[a], recv_sem=recv.at[a],
                device_id=(x, y, 1 - c), device_id_type=MESH)
            rc.start()
            sends.append(rc)
        for a in range(n):
            sends[a].wait_recv()
            mine[a].wait()
            slots, half, _ = rbuf[a].shape
            rows = min(SUM_CHUNK_ROWS, half)
            per_slot = half // rows

            def add(k, carry, a=a, rows=rows, per_slot=per_slot):
                at = (k // per_slot, pl.ds(pl.multiple_of((k % per_slot) * rows, rows), rows))
                rbuf[a][at] = (rbuf[a][at].astype(F32) + mbuf[a][at].astype(F32)).astype(BF16)
                return carry

            lax.fori_loop(0, slots * per_slot, add, 0)
            cp = pltpu.make_async_copy(rbuf[a], outs[a], st.at[a])
            cp.start()
            stores.append(cp)
        for a in range(n):
            sends[a].wait_send()
            stores[a].wait()

    half_shape = lambda a: (a.shape[0], a.shape[1] // 2, a.shape[2])
    stage = [pltpu.VMEM(half_shape(a), a.dtype) for a in grads]
    return pl.pallas_call(
        body, name=name,
        in_specs=[ANY] * n, out_specs=[ANY] * n,
        out_shape=[jax.ShapeDtypeStruct(half_shape(a), a.dtype) for a in grads],
        scratch_shapes=stage * 3 + [pltpu.SemaphoreType.DMA((n,))] * 5,
        compiler_params=pltpu.CompilerParams(vmem_limit_bytes=VMEM_LIMIT_BYTES),
    )(*grads)


def _sum_and_share(sums, got, name):
    n = len(sums)

    def body(*refs):
        own, others, outs = refs[:n], refs[n:2 * n], refs[2 * n:3 * n]
        obuf, gbuf, sbuf, rbuf = (refs[(3 + k) * n:(4 + k) * n] for k in range(4))
        ld_own, ld_got, st_own, st_sib, send, recv = refs[7 * n:]
        x, y, c, _ = _mesh_position()
        loads, sends, stores = [], [], []
        for a in range(n):
            cps = [pltpu.make_async_copy(own[a].at[2 * x + y], obuf[a], ld_own.at[a]),
                   pltpu.make_async_copy(others[a], gbuf[a], ld_got.at[a])]
            for cp in cps:
                cp.start()
            loads.append(cps)
        for a in range(n):
            for cp in loads[a]:
                cp.wait()
            half = obuf[a].shape[0]
            rows = min(SUM_CHUNK_ROWS, half)

            def add(k, carry, a=a, rows=rows):
                at = pl.ds(pl.multiple_of(k * rows, rows), rows)
                acc = obuf[a][at].astype(F32)
                for j in range(N_CHIPS - 1):
                    acc = acc + gbuf[a][j, at].astype(F32)
                sbuf[a][at] = acc
                return carry

            lax.fori_loop(0, half // rows, add, 0)
            rc = pltpu.make_async_remote_copy(
                src_ref=sbuf[a], dst_ref=rbuf[a], send_sem=send.at[a], recv_sem=recv.at[a],
                device_id=(x, y, 1 - c), device_id_type=MESH)
            rc.start()
            sends.append(rc)
            cp = pltpu.make_async_copy(sbuf[a], outs[a].at[pl.ds(c * half, half)], st_own.at[a])
            cp.start()
            stores.append(cp)
        for a in range(n):
            half = obuf[a].shape[0]
            sends[a].wait_recv()
            cp = pltpu.make_async_copy(rbuf[a], outs[a].at[pl.ds((1 - c) * half, half)], st_sib.at[a])
            cp.start()
            stores.append(cp)
        for cp in sends:
            cp.wait_send()
        for cp in stores:
            cp.wait()

    halves = [a.shape[1:] for a in sums]
    return pl.pallas_call(
        body, name=name,
        in_specs=[ANY] * (2 * n), out_specs=[ANY] * n,
        out_shape=[jax.ShapeDtypeStruct((2 * h[0],) + h[1:], F32) for h in halves],
        scratch_shapes=[pltpu.VMEM(h, BF16) for h in halves]
        + [pltpu.VMEM(g.shape, BF16) for g in got]
        + [pltpu.VMEM(h, F32) for h in halves] * 2
        + [pltpu.SemaphoreType.DMA((n,))] * 6,
        compiler_params=pltpu.CompilerParams(vmem_limit_bytes=VMEM_LIMIT_BYTES),
    )(*sums, *got)


def _gather_small(part):
    def body(in_ref, out_ref, send, recv, local):
        x, y, c, _ = _mesh_position()
        me = 4 * x + 2 * y + c
        cps = [pltpu.make_async_copy(in_ref, out_ref.at[me], local)]
        k = 0
        for fx in range(2):
            for fy in range(2):
                for fc in range(2):
                    if fx or fy or fc:
                        cps.append(pltpu.make_async_remote_copy(
                            src_ref=in_ref, dst_ref=out_ref.at[me], send_sem=send.at[k],
                            recv_sem=recv.at[k], device_id=(x ^ fx, y ^ fy, c ^ fc),
                            device_id_type=MESH))
                        k += 1
        for cp in cps:
            cp.start()
        for cp in cps:
            cp.wait()

    return pl.pallas_call(
        body, name="gather_small",
        in_specs=[pl.BlockSpec(memory_space=pltpu.VMEM)],
        out_specs=pl.BlockSpec(memory_space=pltpu.VMEM),
        out_shape=jax.ShapeDtypeStruct((N_DEV,) + part.shape, part.dtype),
        scratch_shapes=[pltpu.SemaphoreType.DMA((N_DEV - 1,)), pltpu.SemaphoreType.DMA((N_DEV - 1,)),
                        pltpu.SemaphoreType.DMA],
    )(part)


def _scatter_start(grads, tag):
    sums = _exchange_siblings(grads, "exchange_siblings_" + tag)
    lands = [jax.ShapeDtypeStruct((N_CHIPS - 1,) + s.shape[1:], s.dtype) for s in sums]
    return _ici_start(sums, lands, _scatter_views, grads[0], "scatter_start_" + tag)


def _scatter_finish(handle, after, tag):
    sums, got = _ici_wait(handle, _scatter_views, after, "scatter_wait_" + tag)
    return _sum_and_share(sums, got, "sum_and_share_" + tag)


def _pad_rows(a, rows):
    return jnp.pad(a, ((0, rows - a.shape[0]), (0, 0)))


def kernel(x, norm_mix_0, w_in_0, b_f_0, conv_w_0, w_out_0, norm_ffn_0, w_up_0, w_down_0, norm_mix_1, pool_w_1, pool_scale_1, norm_ffn_1, w_up_1, w_down_1, final_norm, loss_target, m_norm_mix_0, m_w_in_0, m_b_f_0, m_conv_w_0, m_w_out_0, m_norm_ffn_0, m_w_up_0, m_w_down_0, m_norm_mix_1, m_pool_w_1, m_pool_scale_1, m_norm_ffn_1, m_w_up_1, m_w_down_1, m_final_norm, v_norm_mix_0, v_w_in_0, v_b_f_0, v_conv_w_0, v_w_out_0, v_norm_ffn_0, v_w_up_0, v_w_down_0, v_norm_mix_1, v_pool_w_1, v_pool_scale_1, v_norm_ffn_1, v_w_up_1, v_w_down_1, v_final_norm):
    d = x.shape[-1]
    a = N_HEADS * HEAD_DIM
    c_conv = conv_w_0.shape[1] * N_CHIPS
    xs = x[0]
    target = loss_target[0]
    row = lambda vec: vec.reshape(1, -1)

    big = [w_in_0, w_out_0, w_up_0, w_down_0, pool_w_1, w_up_1, w_down_1]
    first = [w_in_0.astype(BF16)]
    first_split = [True]
    rest = [w.astype(BF16) for w in (w_out_0, w_up_0, w_down_0, pool_w_1, w_up_1, w_down_1)]
    rest = rest + [conv_w_0]
    rest_split = [True] * (len(rest) - 1) + [False]
    start_a = _ici_start(first, _gather_land_shapes(first, first_split),
                         _gather_views(first_split), b_f_0, "gather_start_a")
    start_b = _ici_start(rest, _gather_land_shapes(rest, rest_split),
                         _gather_views(rest_split), start_a[-1], "gather_start_b")
    first, land_a = _ici_wait(start_a, _gather_views(first_split), start_b[-1], "gather_wait_a")
    (g_in,) = _gather_finish(first, land_a, first_split, "gather_finish_a")
    w_in = g_in.transpose(1, 0, 2).reshape(d, -1)
    w_qkv = w_in[:, :3 * a]
    w_f = jnp.pad(w_in[:, 3 * a:3 * a + N_HEADS], ((0, 0), (0, 128 - N_HEADS)))
    w_bcx = w_in[:, 3 * a + N_HEADS:]
    bf = jnp.pad(b_f_0, (0, 128 - N_HEADS)).reshape(1, 128)

    n0, qkv, fl, bcx = _ln_proj(xs, row(norm_mix_0), w_qkv, w_f, w_bcx)
    qa, ka = _gate_prep(fl, bf, qkv)
    o, lse = _attn_fwd(qa, ka, qkv)
    rest, land_b = _ici_wait(start_b, _gather_views(rest_split), o, "gather_wait_b")
    g_out, g_up0, g_down0, g_pool, g_up1, g_down1, g_conv = _gather_finish(
        rest, land_b, rest_split, "gather_finish_b")
    w_out = g_out.reshape(-1, d)
    conv_w = _pad_rows(g_conv.transpose(1, 0, 2).reshape(conv_w_0.shape[0], c_conv), 8)
    h1 = _conv_out(o, bcx, conv_w, w_out, xs)
    w_down0 = g_down0.reshape(-1, d)
    w_down1 = g_down1.reshape(-1, d)
    pool_w = g_pool.transpose(1, 0, 2, 3).reshape(pool_w_1.shape[0], -1, pool_w_1.shape[2])
    h2, a0, nf0 = _mlp_fwd(h1, row(norm_ffn_0), g_up0, w_down0, "mlp_fwd_0")
    h3 = _pool_fwd(h2, row(norm_mix_1), pool_w, row(pool_scale_1))
    dh4, a1, nf1, loss_part, d_final = _mlp_fwd(h3, row(norm_ffn_1), g_up1, w_down1, "mlp_fwd_1",
                                                head=(row(final_norm), target))

    slot_cols = g_up0.shape[2]
    pool_cols = pool_w.shape[2]
    da1, dz1, dh3, d_nffn1 = _mlp_bwd_x(dh4, a1, g_up1, w_down1, h3, row(norm_ffn_1), "mlp_bwd_x_1")
    dw_up1, dw_down1 = _mlp_bwd_w(nf1, da1, a1, dz1, slot_cols, "mlp_bwd_w_1")
    scatter_1 = _scatter_start([dw_up1, dw_down1.reshape(N_CHIPS, -1, d)], "mlp1")
    dh2, dw_pool, d_pscale, d_nmix1 = _pool_bwd(scatter_1[-1], dh3, h2, row(norm_mix_1), pool_w,
                                                row(pool_scale_1))
    da0, dz0, dh1, d_nffn0 = _mlp_bwd_x(dh2, a0, g_up0, w_down0, h1, row(norm_ffn_0), "mlp_bwd_x_0")
    dw_up0, dw_down0 = _mlp_bwd_w(nf0, da0, a0, dz0, slot_cols, "mlp_bwd_w_0")
    dw_pool = (dw_pool.reshape(pool_w.shape[0], N_CHIPS, -1, pool_cols).transpose(1, 0, 2, 3)
               .reshape(N_CHIPS, -1, pool_cols))
    scatter_0 = _scatter_start([dw_up0, dw_down0.reshape(N_CHIPS, -1, d), dw_pool], "mlp0")
    do, delta, dbcx, dw_out, d_conv = _conv_out_bwd(scatter_0[-1], dh1, w_out, o, bcx, conv_w)
    dqa, dka, dv = _attn_bwd(qa, ka, qkv, do, lse, delta)
    dqkv, dfl, d_bf = _gate_bwd(dqa, dka, dv, fl, bf)
    dw_qkv, dw_f, dw_bcx = _wgrad_in(n0, [dqkv, dfl, dbcx])
    dw_in = jnp.concatenate([dw_qkv, dw_f[:, :N_HEADS], dw_bcx], axis=1)
    scatter_m = _scatter_start([dw_in.reshape(d, N_CHIPS, -1).transpose(1, 0, 2),
                                dw_out.reshape(N_CHIPS, -1, d)], "mixer")
    grad_x, d_nmix0 = _in_proj_bwd(scatter_m[-1], dqkv, dfl, dbcx, w_qkv, w_f, w_bcx, xs,
                                   row(norm_mix_0), dh1)

    r_up1, r_down1 = _scatter_finish(scatter_1, grad_x, "mlp1")
    r_up0, r_down0, r_pool = _scatter_finish(scatter_0, grad_x, "mlp0")
    r_in, r_out = _scatter_finish(scatter_m, grad_x, "mixer")
    reduced = [r_in, r_out, r_up0, r_down0, r_pool, r_up1, r_down1]
    moments = [(m_w_in_0, v_w_in_0), (m_w_out_0, v_w_out_0), (m_w_up_0, v_w_up_0),
               (m_w_down_0, v_w_down_0), (m_pool_w_1, v_pool_w_1), (m_w_up_1, v_w_up_1),
               (m_w_down_1, v_w_down_1)]
    big_out = []
    for k, (w, g, (m, v)) in enumerate(zip(big, reduced, moments)):
        if w.shape[-1] % 128:
            view = lambda t: t.reshape(-1, t.shape[-1]).T
            back = lambda t: t.T.reshape(w.shape)
        else:
            view = lambda t: t.reshape(-1, t.shape[-1])
            back = lambda t: t.reshape(w.shape)
        g_view = view(g)
        delta_w, new_m, new_v = _adamw(view(w), g_view, view(m), view(v), "adamw_%d" % k)
        big_out.append((back(g_view), back(delta_w), back(new_m), back(new_v)))

    tail = jnp.concatenate([d_conv[0:3].reshape(-1)[d:], d_bf[0, :N_HEADS], loss_part[0, :1]])
    small_part = jnp.concatenate(
        [d_nmix0, d_nffn0, d_nmix1, d_pscale, d_nffn1, d_final,
         d_conv[0:3].reshape(1, -1)[:, :d],
         jnp.pad(tail, (0, d - tail.shape[0])).reshape(1, d)], axis=0)
    parts = _gather_small(small_part)

    chip = 2 * lax.axis_index("x") + lax.axis_index("y")
    cw_cols = conv_w_0.shape[1]

    def conv_block(full):
        mine = lax.dynamic_slice_in_dim(full, chip * cw_cols, cw_cols, axis=1)
        return jnp.pad(mine.reshape(-1), (0, d - mine.size))

    def small_rows(vals, cw, bfv):
        return jnp.stack(list(vals) + [cw, jnp.pad(bfv, (0, d - N_HEADS))])

    smalls_w = [norm_mix_0, norm_ffn_0, norm_mix_1, pool_scale_1, norm_ffn_1, final_norm]
    smalls_m = [m_norm_mix_0, m_norm_ffn_0, m_norm_mix_1, m_pool_scale_1, m_norm_ffn_1, m_final_norm]
    smalls_v = [v_norm_mix_0, v_norm_ffn_0, v_norm_mix_1, v_pool_scale_1, v_norm_ffn_1, v_final_norm]
    pad_cw = lambda t: jnp.pad(t.reshape(-1), (0, d - t.size))
    w_rows = small_rows(smalls_w, pad_cw(conv_w_0), b_f_0)
    m_rows = small_rows(smalls_m, pad_cw(m_conv_w_0), m_b_f_0)
    v_rows = small_rows(smalls_v, pad_cw(v_conv_w_0), v_b_f_0)

    g_sum = _sum_devices(parts)
    conv_full = jnp.concatenate([g_sum[6], g_sum[7, :3 * c_conv - d]]).reshape(3, c_conv)
    bf_grad = g_sum[7, 3 * c_conv - d:3 * c_conv - d + N_HEADS]
    loss = g_sum[7, 3 * c_conv - d + N_HEADS]
    g_rows = jnp.concatenate(
        [g_sum[0:6], conv_block(conv_full).reshape(1, d),
         jnp.pad(bf_grad, (0, d - N_HEADS)).reshape(1, d)], axis=0)
    d_rows, nm_rows, nv_rows = _adamw(w_rows, g_rows, m_rows, v_rows, "adamw_small")

    def unpack(rows):
        cw = rows[6, :conv_w_0.size].reshape(conv_w_0.shape)
        return [rows[0], rows[1], rows[2], rows[3], rows[4], rows[5], cw, rows[7, :N_HEADS]]

    def assemble(kind):
        sm = unpack([g_rows, d_rows, nm_rows, nv_rows][kind])
        lg = [t[kind] for t in big_out]
        return [sm[0], lg[0], sm[7], sm[6], lg[1], sm[1], lg[2], lg[3],
                sm[2], lg[4], sm[3], sm[4], lg[5], lg[6], sm[5]]

    return (loss, grad_x[None], *assemble(0), *assemble(1), *assemble(2), *assemble(3))
```

```python
import functools

import jax
import jax.numpy as jnp
from jax import lax
from jax.experimental import pallas as pl
from jax.experimental.pallas import tpu as pltpu

F32 = jnp.float32
BF16 = jnp.bfloat16

RMS_EPS = 1e-6
HEAD_DIM = 64
N_HEADS = 8
ATTN_SCALE = HEAD_DIM ** -0.5
LOG2_E = 1.4426950408889634
POOL_WINDOWS = (2, 4, 8, 16)
POOL_HALO = 16
CONV_HALO = 8
NEG_BIG = -1e30

ADAM_LR = 0.001
ADAM_B1 = 0.9
ADAM_B2 = 0.999
ADAM_EPS = 1e-08
ADAM_WD = 0.01
ADAM_STEP = 10

N_CHIPS = 4
N_DEV = 8
MESH = pl.DeviceIdType.MESH

VMEM_LIMIT_BYTES = 56 * 1024 * 1024

TILE_ROWS = 512
TILE_ATTN = 512
TILE_MLP_ROWS = 1024
TILE_MLP_FF = 1024
TILE_MLP_BWD_FF = 512
TILE_WGRAD_K = 1024
TILE_WGRAD_N = 1024
TILE_ELEM_ROWS = 256
SUM_CHUNK_ROWS = 128

LANE_CQ = 64
LANE_ONE = 67


def _params(semantics):
    return pltpu.CompilerParams(dimension_semantics=semantics,
                                vmem_limit_bytes=VMEM_LIMIT_BYTES)


def _nn(a, b):
    return lax.dot_general(a, b, (((1,), (0,)), ((), ())), preferred_element_type=F32)


def _nt(a, b):
    return lax.dot_general(a, b, (((1,), (1,)), ((), ())), preferred_element_type=F32)


def _tn(a, b):
    return lax.dot_general(a, b, (((0,), (0,)), ((), ())), preferred_element_type=F32)


def _split3(v):
    hi = v.astype(BF16)
    r1 = v - hi.astype(F32)
    mid = r1.astype(BF16)
    lo = (r1 - mid.astype(F32)).astype(BF16)
    return hi, mid, lo


def _exact_nn(sel, v):
    hi, mid, lo = _split3(v)
    return _nn(sel, hi) + _nn(sel, mid) + _nn(sel, lo)


def _exact_nt(sel, v):
    hi, mid, lo = _split3(v)
    return _nt(sel, hi) + _nt(sel, mid) + _nt(sel, lo)


def _rms_fwd(x, g):
    r = lax.rsqrt(jnp.mean(x * x, axis=-1, keepdims=True) + RMS_EPS)
    return x * r * g, r


def _rms_bwd(dn, x, g):
    r = lax.rsqrt(jnp.mean(x * x, axis=-1, keepdims=True) + RMS_EPS)
    xh = x * r
    gy = dn * g
    dx = r * (gy - xh * jnp.mean(gy * xh, axis=-1, keepdims=True))
    return dx, jnp.sum(dn * xh, axis=0, keepdims=True)


def _lane(shape):
    return lax.broadcasted_iota(jnp.int32, shape, len(shape) - 1)


def _row(shape):
    return lax.broadcasted_iota(jnp.int32, shape, len(shape) - 2)


def _full(a):
    nd = a.ndim
    return pl.BlockSpec(a.shape, lambda *_: (0,) * nd)


def _ln_proj(x, g, w_qkv, w_f, w_bcx):
    s, d = x.shape
    tm = min(TILE_ROWS, s)

    def body(x_ref, g_ref, wq_ref, wf_ref, wb_ref, n_ref, qkv_ref, fl_ref, bcx_ref):
        n, _ = _rms_fwd(x_ref[...], g_ref[...])
        nb = n.astype(BF16)
        n_ref[...] = nb
        qkv_ref[...] = _nn(nb, wq_ref[...]).astype(BF16)
        fl_ref[...] = _nn(nb, wf_ref[...])
        bcx_ref[...] = _nn(nb, wb_ref[...])

    rows = lambda c: pl.BlockSpec((tm, c), lambda i: (i, 0))
    return pl.pallas_call(
        body, name="ln_proj", grid=(s // tm,),
        in_specs=[rows(d), _full(g), _full(w_qkv), _full(w_f), _full(w_bcx)],
        out_specs=[rows(d), rows(w_qkv.shape[1]), rows(w_f.shape[1]), rows(w_bcx.shape[1])],
        out_shape=[jax.ShapeDtypeStruct((s, d), BF16),
                   jax.ShapeDtypeStruct((s, w_qkv.shape[1]), BF16),
                   jax.ShapeDtypeStruct((s, w_f.shape[1]), F32),
                   jax.ShapeDtypeStruct((s, w_bcx.shape[1]), F32)],
        compiler_params=_params(("parallel",)),
    )(x, g, w_qkv, w_f, w_bcx)


def _gate_prep(fl, bf, qkv):
    s = fl.shape[0]
    a = N_HEADS * HEAD_DIM
    tm = min(TILE_ROWS, s)

    def body(fl_ref, bf_ref, q_ref, k_ref, qa_ref, ka_ref, carry_ref):
        i = pl.program_id(0)

        @pl.when(i == 0)
        def _():
            carry_ref[...] = jnp.zeros_like(carry_ref)

        z = fl_ref[...] + bf_ref[...]
        logf = jnp.minimum(z, 0.0) - jnp.log(1.0 + jnp.exp(-jnp.abs(z)))
        lower = (_lane((tm, tm)) <= _row((tm, tm))).astype(BF16)
        cum = _exact_nn(lower, logf) + carry_ref[0:1, :]
        carry_ref[0:1, :] = cum[tm - 1:tm, :]

        lane = _lane((tm, 128))
        for h in range(N_HEADS):
            cb = LOG2_E * jnp.sum(jnp.where(lane == h, cum, 0.0), axis=1, keepdims=True)
            hi, mid, lo = (p.astype(F32) for p in _split3(cb))
            pair = slice((h // 2) * 128, (h // 2 + 1) * 128)
            qp = q_ref[:, pair].astype(F32)
            kp = k_ref[:, pair].astype(F32)
            if h % 2:
                qp = pltpu.roll(qp, HEAD_DIM, axis=1)
                kp = pltpu.roll(kp, HEAD_DIM, axis=1)
            q_bias = jnp.where(lane == LANE_CQ, hi,
                               jnp.where(lane == LANE_CQ + 1, mid,
                                         jnp.where(lane == LANE_CQ + 2, lo,
                                                   jnp.where(lane < LANE_ONE + 3, 1.0, 0.0))))
            k_bias = jnp.where(lane < LANE_ONE, 1.0,
                               jnp.where(lane == LANE_ONE, -hi,
                                         jnp.where(lane == LANE_ONE + 1, -mid,
                                                   jnp.where(lane == LANE_ONE + 2, -lo, 0.0))))
            qa_ref[h] = jnp.where(lane < HEAD_DIM, qp * (ATTN_SCALE * LOG2_E), q_bias).astype(BF16)
            ka_ref[h] = jnp.where(lane < HEAD_DIM, kp, k_bias).astype(BF16)

    aug = jax.ShapeDtypeStruct((N_HEADS, s, 128), BF16)
    aug_spec = pl.BlockSpec((N_HEADS, tm, 128), lambda i: (0, i, 0))
    return pl.pallas_call(
        body, name="gate_prep", grid=(s // tm,),
        in_specs=[pl.BlockSpec((tm, 128), lambda i: (i, 0)), _full(bf),
                  pl.BlockSpec((tm, a), lambda i: (i, 0)),
                  pl.BlockSpec((tm, a), lambda i: (i, 1))],
        out_specs=[aug_spec, aug_spec],
        out_shape=[aug, aug],
        scratch_shapes=[pltpu.VMEM((8, 128), F32)],
        compiler_params=_params(("arbitrary",)),
    )(fl, bf, qkv, qkv)


def _attn_fwd(qa, ka, qkv):
    s = qa.shape[1]
    a = N_HEADS * HEAD_DIM
    t = min(TILE_ATTN, s)
    n_pairs = N_HEADS // 2
    v_block0 = 2 * a // 128

    ones_lane = (HEAD_DIM, 0)

    def body(qa_ref, ka_ref, v_ref, o_ref, lse_ref, m_ref, acc_ref, s_even, s_odd):
        i = pl.program_id(1)
        m_ref[...] = jnp.full_like(m_ref, NEG_BIG)
        acc_ref[...] = jnp.zeros_like(acc_ref)
        upper_rows = _row((128, t)) < HEAD_DIM

        def keys(j):
            return pl.ds(pl.multiple_of(j * t, t), t)

        def scores_into(buf, j):
            for e in range(2):
                buf[e] = _nt(ka_ref[e, keys(j), :], qa_ref[e])

        def consume(buf, j, masked):
            vf = v_ref[keys(j), :].astype(F32)
            lane = _lane((t, 128))
            own = [lane < HEAD_DIM, lane >= HEAD_DIM]
            for e in range(2):
                v_head = jnp.where(own[e], vf, jnp.where(lane == ones_lane[e], 1.0, 0.0)).astype(BF16)
                sc = buf[e]
                if masked:
                    sc = jnp.where(_row((t, t)) <= _lane((t, t)), sc, NEG_BIG)
                m_prev = m_ref[e]
                m_new = jnp.maximum(m_prev, jnp.max(sc, axis=0, keepdims=True))
                p = jnp.exp2(sc - m_new).astype(BF16)
                acc_ref[e] = acc_ref[e] * jnp.exp2(m_prev - m_new) + _tn(v_head, p)
                m_ref[e] = m_new

        scores_into(s_even, 0)

        def two_tiles(p, carry):
            j = 2 * p
            scores_into(s_odd, j + 1)
            consume(s_even, j, False)
            scores_into(s_even, j + 2)
            consume(s_odd, j + 1, False)
            return carry

        lax.fori_loop(0, i // 2, two_tiles, 0)

        @pl.when(i % 2 == 0)
        def _():
            consume(s_even, i, True)

        @pl.when(i % 2 == 1)
        def _():
            scores_into(s_odd, i)
            consume(s_even, i - 1, False)
            consume(s_odd, i, True)

        denom = [acc_ref[e, ones_lane[e]:ones_lane[e] + 1, :] for e in range(2)]
        out_t = jnp.where(upper_rows, acc_ref[0] / denom[0], acc_ref[1] / denom[1])
        o_ref[...] = out_t.T.astype(BF16)
        lse = [m_ref[e] + LOG2_E * jnp.log(denom[e]) for e in range(2)]
        lse_ref[...] = jnp.where(_row((8, t)) == 0, lse[0], lse[1])

    return pl.pallas_call(
        body, name="attn_fwd", grid=(n_pairs, s // t),
        in_specs=[pl.BlockSpec((2, t, 128), lambda g, i: (g, i, 0)),
                  pl.BlockSpec((2, s, 128), lambda g, i: (g, 0, 0)),
                  pl.BlockSpec((s, 128), lambda g, i: (0, v_block0 + g))],
        out_specs=[pl.BlockSpec((t, 128), lambda g, i: (i, g)),
                   pl.BlockSpec((None, 8, t), lambda g, i: (g, 0, i))],
        out_shape=[jax.ShapeDtypeStruct((s, a), BF16),
                   jax.ShapeDtypeStruct((n_pairs, 8, s), F32)],
        scratch_shapes=[pltpu.VMEM((2, 1, t), F32), pltpu.VMEM((2, 128, t), F32),
                        pltpu.VMEM((2, t, t), F32), pltpu.VMEM((2, t, t), F32)],
        compiler_params=_params(("parallel", "arbitrary")),
    )(qa, ka, qkv)


def _conv_out(o, bcx, cw, w_out, x):
    s, d = x.shape
    c = o.shape[1]
    tm = min(TILE_ROWS, s)

    def body(o_ref, b_ref, c_ref, xin_ref, cw_ref, w_ref, x_ref, h_ref, ubuf):
        i = pl.program_id(0)

        @pl.when(i == 0)
        def _():
            ubuf[0:CONV_HALO, :] = jnp.zeros((CONV_HALO, c), F32)

        u = c_ref[...] * xin_ref[...]
        ubuf[CONV_HALO:CONV_HALO + tm, :] = u
        u1 = ubuf[CONV_HALO - 1:CONV_HALO - 1 + tm, :]
        u2 = ubuf[CONV_HALO - 2:CONV_HALO - 2 + tm, :]
        cv = (cw_ref[0:1, :] * u2 + cw_ref[1:2, :] * u1) + cw_ref[2:3, :] * u
        y = (b_ref[...] * cv).astype(BF16)
        mix = _nn(o_ref[...], w_ref[0:c, :]) + _nn(y, w_ref[c:2 * c, :])
        h_ref[...] = x_ref[...] + mix
        ubuf[0:CONV_HALO, :] = u[tm - CONV_HALO:tm, :]

    col = lambda k: pl.BlockSpec((tm, c), lambda i: (i, k))
    return pl.pallas_call(
        body, name="conv_out", grid=(s // tm,),
        in_specs=[col(0), col(0), col(1), col(2), _full(cw), _full(w_out),
                  pl.BlockSpec((tm, d), lambda i: (i, 0))],
        out_specs=pl.BlockSpec((tm, d), lambda i: (i, 0)),
        out_shape=jax.ShapeDtypeStruct((s, d), F32),
        scratch_shapes=[pltpu.VMEM((tm + CONV_HALO, c), F32)],
        compiler_params=_params(("arbitrary",)),
    )(o, bcx, bcx, bcx, cw, w_out, x)


def _mlp_fwd(h, g, w_up, w_down, name, head=None):
    s, d = h.shape
    ff = w_down.shape[0]
    slot_cols = w_up.shape[2]
    tm = min(TILE_MLP_ROWS, s)
    tf = min(TILE_MLP_FF if head is None else TILE_MLP_FF // 2, slot_cols)
    per_slot = slot_cols // tf
    nf = ff // tf
    n_head = 0 if head is None else 2

    def body(*refs):
        h_ref, g_ref, wu_ref, wd_ref = refs[:4]
        out_ref, a_ref, n_ref = refs[4 + n_head:7 + n_head]
        nb_ref, acc_ref = refs[-2:]
        i = pl.program_id(0)
        f = pl.program_id(1)

        @pl.when(f == 0)
        def _():
            n, _ = _rms_fwd(h_ref[...], g_ref[...])
            nb = n.astype(BF16)
            nb_ref[...] = nb
            n_ref[...] = nb
            acc_ref[...] = jnp.zeros_like(acc_ref)

        pre = _nn(nb_ref[...], wu_ref[...])
        a_ref[...] = pre.astype(BF16)
        r = jnp.square(jnp.maximum(pre, 0.0)).astype(BF16)
        acc_ref[...] += _nn(r, wd_ref[...])

        @pl.when(f == nf - 1)
        def _():
            out = h_ref[...] + acc_ref[...]
            if head is None:
                out_ref[...] = out
            else:
                gf_ref, t_ref = refs[4:6]
                loss_ref, dg_ref = refs[7 + n_head:9 + n_head]
                y, _ = _rms_fwd(out, gf_ref[...])
                err = y - t_ref[...]
                part = 0.5 * jnp.sum(jnp.mean(err * err, axis=-1, keepdims=True), axis=0,
                                     keepdims=True)
                dx, dg = _rms_bwd(err / d, out, gf_ref[...])
                out_ref[...] = dx
                part = jnp.broadcast_to(part, loss_ref.shape)

                @pl.when(i == 0)
                def _():
                    loss_ref[...] = part
                    dg_ref[...] = dg

                @pl.when(i > 0)
                def _():
                    loss_ref[...] += part
                    dg_ref[...] += dg

    rows = pl.BlockSpec((tm, d), lambda i, f: (i, 0))
    in_specs = [rows, _full(g),
                pl.BlockSpec((None, d, tf), lambda i, f: (f // per_slot, 0, f % per_slot)),
                pl.BlockSpec((tf, d), lambda i, f: (f, 0))]
    out_specs = [rows, pl.BlockSpec((tm, tf), lambda i, f: (i, f)), rows]
    out_shape = [jax.ShapeDtypeStruct((s, d), F32), jax.ShapeDtypeStruct((s, ff), BF16),
                 jax.ShapeDtypeStruct((s, d), BF16)]
    args = [h, g, w_up, w_down]
    if head is not None:
        in_specs += [_full(head[0]), rows]
        args += list(head)
        out_specs += [pl.BlockSpec((1, 128), lambda i, f: (0, 0)),
                      pl.BlockSpec((1, d), lambda i, f: (0, 0))]
        out_shape += [jax.ShapeDtypeStruct((1, 128), F32), jax.ShapeDtypeStruct((1, d), F32)]
    return pl.pallas_call(
        body, name=name, grid=(s // tm, nf),
        in_specs=in_specs, out_specs=out_specs, out_shape=out_shape,
        scratch_shapes=[pltpu.VMEM((tm, d), BF16), pltpu.VMEM((tm, d), F32)],
        compiler_params=_params(("parallel" if head is None else "arbitrary", "arbitrary")),
    )(*args)


def _window_sum_down(e, window):
    step = 1
    while step < window:
        e = e + pltpu.roll(e, step, axis=0)
        step *= 2
    return e


def _window_sum_up(e, window):
    n = e.shape[0]
    step = 1
    while step < window:
        e = e + pltpu.roll(e, n - step, axis=0)
        step *= 2
    return e


def _pool_counts(first_row, tm, window):
    t = first_row + _row((tm, 1))
    return jnp.minimum(t + 1, window).astype(F32)


def _pool_fwd(h, g, pw, ps):
    s, d = h.shape
    cg = d // len(POOL_WINDOWS)
    tm = min(TILE_ROWS, s)

    def body(h_ref, g_ref, pw_ref, ps_ref, out_ref, nbuf):
        i = pl.program_id(0)

        @pl.when(i == 0)
        def _():
            nbuf[0:POOL_HALO, :] = jnp.zeros((POOL_HALO, d), F32)

        n, _ = _rms_fwd(h_ref[...], g_ref[...])
        nbuf[POOL_HALO:POOL_HALO + tm, :] = n
        for k, window in enumerate(POOL_WINDOWS):
            cols = slice(k * cg, (k + 1) * cg)
            sums = _window_sum_down(nbuf[:, cols], window)[POOL_HALO:, :]
            pooled = sums / _pool_counts(i * tm, tm, window) - n[:, cols]
            y = _nn(pooled.astype(BF16), pw_ref[k]) * ps_ref[:, cols]
            out_ref[:, cols] = h_ref[:, cols] + y
        nbuf[0:POOL_HALO, :] = n[tm - POOL_HALO:tm, :]

    return pl.pallas_call(
        body, name="pool_fwd", grid=(s // tm,),
        in_specs=[pl.BlockSpec((tm, d), lambda i: (i, 0)), _full(g), _full(pw), _full(ps)],
        out_specs=pl.BlockSpec((tm, d), lambda i: (i, 0)),
        out_shape=jax.ShapeDtypeStruct((s, d), F32),
        scratch_shapes=[pltpu.VMEM((tm + POOL_HALO, d), F32)],
        compiler_params=_params(("arbitrary",)),
    )(h, g, pw, ps)


def _mlp_bwd_x(dz, a, w_up, w_down, h_in, g, name):
    s, d = dz.shape
    ff = w_down.shape[0]
    slot_cols = w_up.shape[2]
    tm = min(TILE_MLP_ROWS, s)
    tf = min(TILE_MLP_BWD_FF, slot_cols)
    per_slot = slot_cols // tf
    nf = ff // tf

    def body(dz_ref, a_ref, wu_ref, wd_ref, h_ref, g_ref, da_ref, dzb_ref, dh_ref, dg_ref,
             dzs_ref, acc_ref):
        i = pl.program_id(0)
        f = pl.program_id(1)

        @pl.when(f == 0)
        def _():
            dzb = dz_ref[...].astype(BF16)
            dzs_ref[...] = dzb
            dzb_ref[...] = dzb
            acc_ref[...] = jnp.zeros_like(acc_ref)

        dr = _nt(dzs_ref[...], wd_ref[...])
        da = (dr * (2.0 * jnp.maximum(a_ref[...].astype(F32), 0.0))).astype(BF16)
        da_ref[...] = da
        acc_ref[...] += _nt(da, wu_ref[...])

        @pl.when(f == nf - 1)
        def _():
            dx, dg = _rms_bwd(acc_ref[...], h_ref[...], g_ref[...])
            dh_ref[...] = dz_ref[...] + dx

            @pl.when(i == 0)
            def _():
                dg_ref[...] = dg

            @pl.when(i > 0)
            def _():
                dg_ref[...] += dg

    return pl.pallas_call(
        body, name=name, grid=(s // tm, nf),
        in_specs=[pl.BlockSpec((tm, d), lambda i, f: (i, 0)),
                  pl.BlockSpec((tm, tf), lambda i, f: (i, f)),
                  pl.BlockSpec((None, d, tf), lambda i, f: (f // per_slot, 0, f % per_slot)),
                  pl.BlockSpec((tf, d), lambda i, f: (f, 0)),
                  pl.BlockSpec((tm, d), lambda i, f: (i, 0)), _full(g)],
        out_specs=[pl.BlockSpec((tm, tf), lambda i, f: (i, f)),
                   pl.BlockSpec((tm, d), lambda i, f: (i, 0)),
                   pl.BlockSpec((tm, d), lambda i, f: (i, 0)),
                   pl.BlockSpec((1, d), lambda i, f: (0, 0))],
        out_shape=[jax.ShapeDtypeStruct((s, ff), BF16),
                   jax.ShapeDtypeStruct((s, d), BF16),
                   jax.ShapeDtypeStruct((s, d), F32),
                   jax.ShapeDtypeStruct((1, d), F32)],
        scratch_shapes=[pltpu.VMEM((tm, d), BF16), pltpu.VMEM((tm, d), F32)],
        compiler_params=_params(("arbitrary", "arbitrary")),
    )(dz, a, w_up, w_down, h_in, g)


def _mlp_bwd_w(n, da, a, dzb, slot_cols, name):
    s, d = n.shape
    ff = a.shape[1]
    tn = min(TILE_WGRAD_N, slot_cols)
    tk = min(TILE_WGRAD_K, s)
    per_slot = slot_cols // tn
    nk = s // tk

    def body(n_ref, da_ref, a_ref, dz_ref, du_ref, dd_ref, accu_ref, accd_ref):
        k = pl.program_id(1)

        @pl.when(k == 0)
        def _():
            accu_ref[...] = jnp.zeros_like(accu_ref)
            accd_ref[...] = jnp.zeros_like(accd_ref)

        accu_ref[...] += _tn(n_ref[...], da_ref[...])
        r = jnp.square(jnp.maximum(a_ref[...].astype(F32), 0.0)).astype(BF16)
        accd_ref[...] += _tn(r, dz_ref[...])

        @pl.when(k == nk - 1)
        def _():
            du_ref[...] = accu_ref[...].astype(BF16)
            dd_ref[...] = accd_ref[...].astype(BF16)

    return pl.pallas_call(
        body, name=name, grid=(ff // tn, nk),
        in_specs=[pl.BlockSpec((tk, d), lambda f, k: (k, 0)),
                  pl.BlockSpec((tk, tn), lambda f, k: (k, f)),
                  pl.BlockSpec((tk, tn), lambda f, k: (k, f)),
                  pl.BlockSpec((tk, d), lambda f, k: (k, 0))],
        out_specs=[pl.BlockSpec((None, d, tn), lambda f, k: (f // per_slot, 0, f % per_slot)),
                   pl.BlockSpec((tn, d), lambda f, k: (f, 0))],
        out_shape=[jax.ShapeDtypeStruct((ff // slot_cols, d, slot_cols), BF16),
                   jax.ShapeDtypeStruct((ff, d), BF16)],
        scratch_shapes=[pltpu.VMEM((d, tn), F32), pltpu.VMEM((tn, d), F32)],
        compiler_params=_params(("parallel", "arbitrary")),
    )(n, da, a, dzb)


def _pool_bwd(after, dh, h, g, pw, ps):
    s, d = h.shape
    cg = d // len(POOL_WINDOWS)
    tm = min(TILE_ROWS, s)
    nb = s // tm
    halo_per_tile = tm // POOL_HALO

    def body(after_ref, dh_ref, h_ref, halo_ref, g_ref, pw_ref, ps_ref,
             dx_ref, dpw_ref, dps_ref, dg_ref, nbuf, qbuf, dn_ref, carry, dpw_acc):
        i = pl.program_id(0)
        blk = nb - 1 - i

        @pl.when(i == 0)
        def _():
            carry[...] = jnp.zeros_like(carry)
            dpw_acc[...] = jnp.zeros_like(dpw_acc)
            dps_ref[...] = jnp.zeros_like(dps_ref)
            dg_ref[...] = jnp.zeros_like(dg_ref)

        hv = h_ref[...]
        n, _ = _rms_fwd(hv, g_ref[...])
        nh, _ = _rms_fwd(halo_ref[...], g_ref[...])
        nbuf[0:POOL_HALO, :] = jnp.where(blk == 0, 0.0, nh)
        nbuf[POOL_HALO:POOL_HALO + tm, :] = n
        dhv = dh_ref[...]
        for k, window in enumerate(POOL_WINDOWS):
            cols = slice(k * cg, (k + 1) * cg)
            cnt = _pool_counts(blk * tm, tm, window)
            sums = _window_sum_down(nbuf[:, cols], window)[POOL_HALO:, :]
            pb = (sums / cnt - n[:, cols]).astype(BF16)
            dyk = dhv[:, cols]
            dps_ref[:, cols] += jnp.sum(dyk * _nn(pb, pw_ref[k]), axis=0, keepdims=True)
            dyb = (dyk * ps_ref[:, cols]).astype(BF16)
            dpw_acc[k] += _tn(pb, dyb)
            dpool = _nt(dyb, pw_ref[k])
            qv = dpool / cnt
            qbuf[0:tm, cols] = qv
            qbuf[tm:tm + POOL_HALO, cols] = carry[:, cols]
            dn_ref[:, cols] = _window_sum_up(qbuf[:, cols], window)[0:tm, :] - dpool
            carry[:, cols] = qv[0:POOL_HALO, :]
        dx, dg = _rms_bwd(dn_ref[...], hv, g_ref[...])
        dx_ref[...] = dhv + dx
        dg_ref[...] += dg

        @pl.when(i == nb - 1)
        def _():
            dpw_ref[...] = dpw_acc[...].astype(BF16)

    rev = lambda i: (nb - 1 - i, 0)
    return pl.pallas_call(
        body, name="pool_bwd", grid=(nb,),
        in_specs=[ANY, pl.BlockSpec((tm, d), rev), pl.BlockSpec((tm, d), rev),
                  pl.BlockSpec((POOL_HALO, d),
                               lambda i: (jnp.maximum((nb - 1 - i) * halo_per_tile - 1, 0), 0)),
                  _full(g), _full(pw), _full(ps)],
        out_specs=[pl.BlockSpec((tm, d), rev), _full(pw),
                   pl.BlockSpec((1, d), lambda i: (0, 0)),
                   pl.BlockSpec((1, d), lambda i: (0, 0))],
        out_shape=[jax.ShapeDtypeStruct((s, d), F32),
                   jax.ShapeDtypeStruct(pw.shape, BF16),
                   jax.ShapeDtypeStruct((1, d), F32),
                   jax.ShapeDtypeStruct((1, d), F32)],
        scratch_shapes=[pltpu.VMEM((tm + POOL_HALO, d), F32), pltpu.VMEM((tm + POOL_HALO, d), F32),
                        pltpu.VMEM((tm, d), F32), pltpu.VMEM((POOL_HALO, d), F32),
                        pltpu.VMEM(pw.shape, F32)],
        compiler_params=_params(("arbitrary",)),
    )(after, dh, h, h, g, pw, ps)


def _conv_out_bwd(after, dh, w_out, o, bcx, cw):
    s, d = dh.shape
    c = o.shape[1]
    tm = min(TILE_ROWS, s)
    nb = s // tm
    halo_per_tile = tm // CONV_HALO

    def body(after_ref, dh_ref, w_ref, o_ref, b_ref, c_ref, xin_ref, ch_ref, xh_ref, cw_ref,
             do_ref, delta_ref, dbcx_ref, dw_ref, dcw_ref, ubuf, dbuf, carry, acc):
        i = pl.program_id(0)
        blk = nb - 1 - i

        @pl.when(i == 0)
        def _():
            carry[...] = jnp.zeros_like(carry)
            acc[...] = jnp.zeros_like(acc)
            dcw_ref[...] = jnp.zeros_like(dcw_ref)

        dm = dh_ref[...].astype(BF16)
        dcat = _nt(dm, w_ref[...])
        do = dcat[:, 0:c]
        dy = dcat[:, c:2 * c]
        do_ref[...] = do.astype(BF16)
        head_of_lane = lax.shift_right_logical(_lane((8, c)), HEAD_DIM.bit_length() - 1)
        heads = (head_of_lane == _row((8, c))).astype(BF16)
        delta_ref[...] = _exact_nt(heads, do * o_ref[...].astype(F32))

        cv_ = c_ref[...]
        xin = xin_ref[...]
        bv = b_ref[...]
        u = cv_ * xin
        ubuf[0:CONV_HALO, :] = jnp.where(blk == 0, 0.0, ch_ref[...] * xh_ref[...])
        ubuf[CONV_HALO:CONV_HALO + tm, :] = u
        u1 = ubuf[CONV_HALO - 1:CONV_HALO - 1 + tm, :]
        u2 = ubuf[CONV_HALO - 2:CONV_HALO - 2 + tm, :]
        w0, w1, w2 = cw_ref[0:1, :], cw_ref[1:2, :], cw_ref[2:3, :]
        cv = (w0 * u2 + w1 * u1) + w2 * u
        acc[0:c, :] += _tn(o_ref[...], dm)
        acc[c:2 * c, :] += _tn((bv * cv).astype(BF16), dm)

        dcv = dy * bv
        dcw_ref[0:1, :] += jnp.sum(dcv * u2, axis=0, keepdims=True)
        dcw_ref[1:2, :] += jnp.sum(dcv * u1, axis=0, keepdims=True)
        dcw_ref[2:3, :] += jnp.sum(dcv * u, axis=0, keepdims=True)
        dbuf[0:tm, :] = dcv
        dbuf[tm:tm + CONV_HALO, :] = carry[...]
        du = w2 * dcv + w1 * dbuf[1:1 + tm, :] + w0 * dbuf[2:2 + tm, :]
        dbcx_ref[:, 0:c] = (dy * cv).astype(BF16)
        dbcx_ref[:, c:2 * c] = (du * xin).astype(BF16)
        dbcx_ref[:, 2 * c:3 * c] = (du * cv_).astype(BF16)
        carry[...] = dcv[0:CONV_HALO, :]

        @pl.when(i == nb - 1)
        def _():
            dw_ref[...] = acc[...].astype(BF16)

    rev = lambda k: (lambda i: (nb - 1 - i, k))
    halo = lambda k: (lambda i: (jnp.maximum((nb - 1 - i) * halo_per_tile - 1, 0), k))
    return pl.pallas_call(
        body, name="conv_out_bwd", grid=(nb,),
        in_specs=[ANY, pl.BlockSpec((tm, d), rev(0)), _full(w_out), pl.BlockSpec((tm, c), rev(0)),
                  pl.BlockSpec((tm, c), rev(0)), pl.BlockSpec((tm, c), rev(1)),
                  pl.BlockSpec((tm, c), rev(2)),
                  pl.BlockSpec((CONV_HALO, c), halo(1)), pl.BlockSpec((CONV_HALO, c), halo(2)),
                  _full(cw)],
        out_specs=[pl.BlockSpec((tm, c), rev(0)),
                   pl.BlockSpec((8, tm), lambda i: (0, nb - 1 - i)),
                   pl.BlockSpec((tm, 3 * c), rev(0)),
                   _full(w_out), _full(cw)],
        out_shape=[jax.ShapeDtypeStruct((s, c), BF16),
                   jax.ShapeDtypeStruct((8, s), F32),
                   jax.ShapeDtypeStruct((s, 3 * c), BF16),
                   jax.ShapeDtypeStruct(w_out.shape, BF16),
                   jax.ShapeDtypeStruct(cw.shape, F32)],
        scratch_shapes=[pltpu.VMEM((tm + CONV_HALO, c), F32), pltpu.VMEM((tm + CONV_HALO, c), F32),
                        pltpu.VMEM((CONV_HALO, c), F32), pltpu.VMEM(w_out.shape, F32)],
        compiler_params=_params(("arbitrary",)),
    )(after, dh, w_out, o, bcx, bcx, bcx, bcx, bcx, cw)


def _attn_bwd(qa, ka, qkv, do, lse, delta):
    s = qa.shape[1]
    a = N_HEADS * HEAD_DIM
    t = min(TILE_ATTN, s)
    nq = s // t
    n_pairs = N_HEADS // 2
    v_block0 = 2 * a // 128

    def body(ka_ref, v_ref, qa_ref, do_ref, lse_ref, delta_ref,
             dqt_ref, dka_ref, dv_ref, dk_acc, dv_acc):
        g = pl.program_id(0)
        j = pl.program_id(1)

        @pl.when(j == 0)
        def _():
            dqt_ref[...] = jnp.zeros_like(dqt_ref)

        dk_acc[...] = jnp.zeros_like(dk_acc)
        dv_acc[...] = jnp.zeros_like(dv_acc)
        lane = _lane((t, 128))
        vf = v_ref[...].astype(F32)
        v_heads = [jnp.where(lane < HEAD_DIM, vf, 0.0).astype(BF16),
                   jnp.where(lane >= HEAD_DIM, vf, 0.0).astype(BF16)]
        ke_t = [ka_ref[e].astype(F32).T.astype(BF16) for e in range(2)]

        def q_step(i, masked):
            qs = pl.ds(pl.multiple_of(i * t, t), t)
            dob = do_ref[qs, :]
            for e in range(2):
                qe = qa_ref[e, qs, :]
                sc = _nt(ka_ref[e], qe)
                if masked:
                    sc = jnp.where(_row((t, t)) <= _lane((t, t)), sc, NEG_BIG)
                p = jnp.exp2(sc - lse_ref[pl.ds(e, 1), qs])
                dv_acc[e] += _nn(p.astype(BF16), dob)
                dp = _nt(v_heads[e], dob)
                ds = (p * (dp - delta_ref[pl.ds(2 * g + e, 1), qs])).astype(BF16)
                dk_acc[e] += _nn(ds, qe)
                dqt_ref[e, :, qs] += _nn(ke_t[e], ds)

        q_step(j, True)

        def full_step(i, carry):
            q_step(i, False)
            return carry

        lax.fori_loop(j + 1, nq, full_step, 0)
        dka_ref[...] = dk_acc[...]
        dv_ref[...] = jnp.where(lane < HEAD_DIM, dv_acc[0], dv_acc[1]).astype(BF16)

    return pl.pallas_call(
        body, name="attn_bwd", grid=(n_pairs, nq),
        in_specs=[pl.BlockSpec((2, t, 128), lambda g, j: (g, j, 0)),
                  pl.BlockSpec((t, 128), lambda g, j: (j, v_block0 + g)),
                  pl.BlockSpec((2, s, 128), lambda g, j: (g, 0, 0)),
                  pl.BlockSpec((s, 128), lambda g, j: (0, g)),
                  pl.BlockSpec((None, 8, s), lambda g, j: (g, 0, 0)),
                  pl.BlockSpec((8, s), lambda g, j: (0, 0))],
        out_specs=[pl.BlockSpec((2, 128, s), lambda g, j: (g, 0, 0)),
                   pl.BlockSpec((2, t, 128), lambda g, j: (g, j, 0)),
                   pl.BlockSpec((t, 128), lambda g, j: (j, g))],
        out_shape=[jax.ShapeDtypeStruct((N_HEADS, 128, s), F32),
                   jax.ShapeDtypeStruct((N_HEADS, s, 128), F32),
                   jax.ShapeDtypeStruct((s, a), BF16)],
        scratch_shapes=[pltpu.VMEM((2, t, 128), F32), pltpu.VMEM((2, t, 128), F32)],
        compiler_params=_params(("parallel", "arbitrary")),
    )(ka, qkv, qa, do, lse, delta)


def _gate_bwd(dqa, dka, dv, fl, bf):
    s = fl.shape[0]
    a = N_HEADS * HEAD_DIM
    tm = min(TILE_ROWS, s)
    nb = s // tm

    def body(dqa_ref, dka_ref, dv_ref, fl_ref, bf_ref, dqkv_ref, dfl_ref, dbf_ref, carry):
        i = pl.program_id(0)

        @pl.when(i == 0)
        def _():
            carry[...] = jnp.zeros_like(carry)
            dbf_ref[...] = jnp.zeros_like(dbf_ref)

        lane = _lane((tm, 128))
        dcum = jnp.zeros((tm, 128), F32)
        for pair in range(N_HEADS // 2):
            qs, ks = [], []
            for e in range(2):
                h = 2 * pair + e
                dq = dqa_ref[h].T
                dk = dka_ref[h]
                dc = jnp.sum(jnp.where(lane == LANE_CQ, dq, 0.0)
                             - jnp.where(lane == LANE_ONE, dk, 0.0), axis=1, keepdims=True)
                dcum = jnp.where(lane == h, dc, dcum)
                qs.append(dq * ATTN_SCALE)
                ks.append(dk * (1.0 / LOG2_E))
            cols = slice(pair * 128, (pair + 1) * 128)
            dqkv_ref[:, cols] = jnp.where(
                lane < HEAD_DIM, qs[0], pltpu.roll(qs[1], HEAD_DIM, axis=1)).astype(BF16)
            dqkv_ref[:, a + pair * 128:a + (pair + 1) * 128] = jnp.where(
                lane < HEAD_DIM, ks[0], pltpu.roll(ks[1], HEAD_DIM, axis=1)).astype(BF16)
        dqkv_ref[:, 2 * a:3 * a] = dv_ref[...]

        upper = (_lane((tm, tm)) >= _row((tm, tm))).astype(BF16)
        dlogf = _exact_nn(upper, dcum) + carry[0:1, :]
        carry[0:1, :] = dlogf[0:1, :]
        z = fl_ref[...] + bf_ref[...]
        ez = jnp.exp(-jnp.abs(z))
        sig_neg = jnp.where(z >= 0.0, ez, 1.0) / (1.0 + ez)
        dz = jnp.where(lane < N_HEADS, dlogf * sig_neg, 0.0)
        dfl_ref[...] = dz.astype(BF16)
        dbf_ref[...] += jnp.sum(dz, axis=0, keepdims=True)

    rev3 = lambda i: (0, nb - 1 - i, 0)
    rev = lambda i: (nb - 1 - i, 0)
    return pl.pallas_call(
        body, name="gate_bwd", grid=(nb,),
        in_specs=[pl.BlockSpec((N_HEADS, 128, tm), lambda i: (0, 0, nb - 1 - i)),
                  pl.BlockSpec((N_HEADS, tm, 128), rev3),
                  pl.BlockSpec((tm, a), rev), pl.BlockSpec((tm, 128), rev), _full(bf)],
        out_specs=[pl.BlockSpec((tm, 3 * a), rev), pl.BlockSpec((tm, 128), rev),
                   pl.BlockSpec((1, 128), lambda i: (0, 0))],
        out_shape=[jax.ShapeDtypeStruct((s, 3 * a), BF16),
                   jax.ShapeDtypeStruct((s, 128), BF16),
                   jax.ShapeDtypeStruct((1, 128), F32)],
        scratch_shapes=[pltpu.VMEM((8, 128), F32)],
        compiler_params=_params(("arbitrary",)),
    )(dqa, dka, dv, fl, bf)


def _in_proj_bwd(after, dqkv, dfl, dbcx, w_qkv, w_f, w_bcx, x, g, dh):
    s, d = x.shape
    tm = min(TILE_ROWS, s)

    def body(after_ref, dq_ref, df_ref, db_ref, wq_ref, wf_ref, wb_ref, x_ref, g_ref, dh_ref,
             gx_ref, dg_ref):
        i = pl.program_id(0)
        dn = (_nt(dq_ref[...], wq_ref[...]) + _nt(df_ref[...], wf_ref[...])
              + _nt(db_ref[...], wb_ref[...]))
        dx, dg = _rms_bwd(dn, x_ref[...], g_ref[...])
        gx_ref[...] = dh_ref[...] + dx

        @pl.when(i == 0)
        def _():
            dg_ref[...] = dg

        @pl.when(i > 0)
        def _():
            dg_ref[...] += dg

    rows = lambda c: pl.BlockSpec((tm, c), lambda i: (i, 0))
    return pl.pallas_call(
        body, name="in_proj_bwd", grid=(s // tm,),
        in_specs=[ANY, rows(dqkv.shape[1]), rows(dfl.shape[1]), rows(dbcx.shape[1]),
                  _full(w_qkv), _full(w_f), _full(w_bcx), rows(d), _full(g), rows(d)],
        out_specs=[rows(d), pl.BlockSpec((1, d), lambda i: (0, 0))],
        out_shape=[jax.ShapeDtypeStruct((s, d), F32), jax.ShapeDtypeStruct((1, d), F32)],
        compiler_params=_params(("arbitrary",)),
    )(after, dqkv, dfl, dbcx, w_qkv, w_f, w_bcx, x, g, dh)


def _wgrad_in(n, dys):
    s, d = n.shape
    m = len(dys)
    tk = min(TILE_ROWS, s)
    nk = s // tk

    def body(*refs):
        n_ref, dy_refs, dw_refs, accs = refs[0], refs[1:1 + m], refs[1 + m:1 + 2 * m], refs[1 + 2 * m:]
        k = pl.program_id(0)

        @pl.when(k == 0)
        def _():
            for acc in accs:
                acc[...] = jnp.zeros_like(acc)

        nb = n_ref[...]
        for dy_ref, acc in zip(dy_refs, accs):
            acc[...] += _tn(nb, dy_ref[...])

        @pl.when(k == nk - 1)
        def _():
            for dw_ref, acc in zip(dw_refs, accs):
                dw_ref[...] = acc[...].astype(BF16)

    return pl.pallas_call(
        body, name="wgrad_in", grid=(nk,),
        in_specs=[pl.BlockSpec((tk, d), lambda k: (k, 0))]
        + [pl.BlockSpec((tk, dy.shape[1]), lambda k: (k, 0)) for dy in dys],
        out_specs=[pl.BlockSpec((d, dy.shape[1]), lambda k: (0, 0)) for dy in dys],
        out_shape=[jax.ShapeDtypeStruct((d, dy.shape[1]), BF16) for dy in dys],
        scratch_shapes=[pltpu.VMEM((d, dy.shape[1]), F32) for dy in dys],
        compiler_params=_params(("arbitrary",)),
    )(n, *dys)


def _row_tile(rows):
    t = min(TILE_ELEM_ROWS, rows)
    while rows % t:
        t //= 2
    return t


def _adamw_math(w, g, m, v):
    m = ADAM_B1 * m + (1.0 - ADAM_B1) * g
    v = ADAM_B2 * v + (1.0 - ADAM_B2) * jnp.square(g)
    m_hat = m / (1.0 - ADAM_B1 ** ADAM_STEP)
    v_hat = v / (1.0 - ADAM_B2 ** ADAM_STEP)
    delta = -ADAM_LR * (m_hat / (jnp.sqrt(v_hat) + ADAM_EPS) + ADAM_WD * w)
    return delta, m, v


def _adamw(w, g, m, v, name):
    rows, cols = w.shape

    def body(w_ref, g_ref, m_ref, v_ref, d_ref, nm_ref, nv_ref):
        delta, nm, nv = _adamw_math(w_ref[...], g_ref[...], m_ref[...], v_ref[...])
        d_ref[...] = delta
        nm_ref[...] = nm
        nv_ref[...] = nv

    if rows % 8 == 0:
        tr = _row_tile(rows)
        grid, spec = (rows // tr,), pl.BlockSpec((tr, cols), lambda i: (i, 0))
    else:
        grid, spec = (cols // 256,), pl.BlockSpec((rows, 256), lambda i: (0, i))
    out = jax.ShapeDtypeStruct(w.shape, F32)
    return pl.pallas_call(
        body, name=name, grid=grid, in_specs=[spec] * 4, out_specs=[spec] * 3,
        out_shape=[out, out, out], compiler_params=_params(("parallel",)),
    )(w, g, m, v)


def _sum_devices(parts):
    def body(p_ref, g_ref):
        g = p_ref[0]
        for k in range(1, N_DEV):
            g = g + p_ref[k]
        g_ref[...] = g

    return pl.pallas_call(
        body, name="sum_devices",
        in_specs=[pl.BlockSpec(memory_space=pltpu.VMEM)],
        out_specs=pl.BlockSpec(memory_space=pltpu.VMEM),
        out_shape=jax.ShapeDtypeStruct(parts.shape[1:], F32),
    )(parts)


def _mesh_position():
    x, y, c = lax.axis_index("x"), lax.axis_index("y"), lax.axis_index("c")
    chips = [(1 - x, y), (x, 1 - y), (1 - x, 1 - y)]
    return x, y, c, chips


ANY = pl.BlockSpec(memory_space=pl.ANY)
HBM = pl.BlockSpec(memory_space=pltpu.HBM)
SEM = pl.BlockSpec(memory_space=pltpu.SEMAPHORE)
SPLIT_COPY_EFFECT = pltpu.SideEffectType.DATAFLOW_SIDE_EFFECTING


def _in_hbm(a):
    return pltpu.with_memory_space_constraint(a, pltpu.HBM)


def _chip_copies(views, srcs, lands, send, recv):
    _, _, c, chips = _mesh_position()
    cps = []
    for a in range(len(srcs)):
        for k, (px, py) in enumerate(chips):
            src, dst = views(a, k, srcs[a], lands[a], c, 2 * px + py)
            sem = a * (N_CHIPS - 1) + k
            cps.append(pltpu.make_async_remote_copy(
                src_ref=src, dst_ref=dst, send_sem=send.at[sem], recv_sem=recv.at[sem],
                device_id=(px, py, c), device_id_type=MESH))
    return cps


def _ici_start(sources, land_shapes, views, after, name):
    n = len(sources)

    def body(*refs):
        srcs, lands = refs[:n], refs[n:2 * n]
        send, recv = refs[2 * n + 1], refs[2 * n + 2]
        token = refs[-1]
        for cp in _chip_copies(views, srcs, lands, send, recv):
            cp.start()
        token[...] = jnp.zeros_like(token)

    lands = [_in_hbm(lax.empty(s.shape, s.dtype)) for s in land_shapes]
    outs = pl.pallas_call(
        body, name=name,
        in_specs=[HBM] * (2 * n) + [ANY],
        out_specs=[SEM, SEM] + [HBM] * (2 * n) + [pl.BlockSpec(memory_space=pltpu.VMEM)],
        out_shape=[pltpu.SemaphoreType.DMA((n * (N_CHIPS - 1),))] * 2
        + [pltpu.HBM(a.shape, a.dtype) for a in sources]
        + [pltpu.HBM(s.shape, s.dtype) for s in land_shapes]
        + [jax.ShapeDtypeStruct((8, 128), F32)],
        input_output_aliases={i: 2 + i for i in range(2 * n)},
        compiler_params=pltpu.CompilerParams(has_side_effects=SPLIT_COPY_EFFECT),
    )(*[_in_hbm(a) for a in sources], *lands, after)
    return outs[0], outs[1], list(outs[2:2 + n]), list(outs[2 + n:2 + 2 * n]), outs[-1]


def _ici_wait(handle, views, after, name):
    send, recv, srcs, lands, _ = handle
    n = len(srcs)

    def body(*refs):
        src_refs, land_refs = refs[:n], refs[n:2 * n]
        for cp in _chip_copies(views, src_refs, land_refs, refs[2 * n], refs[2 * n + 1]):
            cp.wait_send()
            cp.wait_recv()

    outs = pl.pallas_call(
        body, name=name,
        in_specs=[HBM] * (2 * n) + [SEM, SEM, ANY],
        out_specs=[HBM] * (2 * n),
        out_shape=[pltpu.HBM(a.shape, a.dtype) for a in srcs]
        + [pltpu.HBM(a.shape, a.dtype) for a in lands],
        input_output_aliases={i: i for i in range(2 * n)},
        compiler_params=pltpu.CompilerParams(has_side_effects=SPLIT_COPY_EFFECT),
    )(*srcs, *lands, send, recv, after)
    return list(outs[:n]), list(outs[n:])


def _gather_views(split):
    def views(a, k, src, land, c, slot):
        if split[a]:
            half = src.shape[0] // 2
            src = src.at[pl.ds(c * half, half)]
        return src, land.at[k]
    return views


def _gather_whole_views(a, k, src, land, c, slot):
    x, y, _, _ = _mesh_position()
    return src, land.at[2 * x + y]


def _scatter_views(a, k, src, land, c, slot):
    return src.at[slot], land.at[k]


def _gather_land_shapes(shards, split):
    return [jax.ShapeDtypeStruct(
        (N_CHIPS - 1, a.shape[0] // 2 if sp else a.shape[0]) + a.shape[1:], a.dtype)
        for a, sp in zip(shards, split)]


def _gather_finish(shards, lands, split, name):
    n = len(shards)
    ns = sum(split)
    d_index = {a: i for i, a in enumerate(a for a in range(n) if split[a])}

    def body(*refs):
        shard, land, outs = refs[:n], refs[n:2 * n], refs[2 * n:3 * n]
        obuf, fbuf = refs[3 * n:4 * n], refs[4 * n:5 * n]
        dbuf = refs[5 * n:5 * n + ns]
        ld_own, st_own, ld, st_mine, st_sib, send, recv = refs[5 * n + ns:]
        x, y, c, chips = _mesh_position()
        me = 2 * x + y
        own_loads, loads, sends, pending = [], {}, [], []
        for a in range(n):
            cp = pltpu.make_async_copy(shard[a], obuf[a], ld_own.at[a])
            cp.start()
            own_loads.append(cp)
        for a in range(n):
            for k in range(N_CHIPS - 1):
                cp = pltpu.make_async_copy(land[a].at[k], fbuf[a].at[k], ld.at[a, k])
                cp.start()
                loads[a, k] = cp
        for a in range(n):
            own_loads[a].wait()
            cp = pltpu.make_async_copy(obuf[a], outs[a].at[me], st_own.at[a])
            cp.start()
            pending.append(cp)
        for a in range(n):
            rows = shard[a].shape[0]
            for k, (px, py) in enumerate(chips):
                loads[a, k].wait()
                part = pl.ds(c * (rows // 2), rows // 2) if split[a] else pl.ds(0, rows)
                cp = pltpu.make_async_copy(fbuf[a].at[k], outs[a].at[2 * px + py, part],
                                           st_mine.at[a, k])
                cp.start()
                pending.append(cp)
                if split[a]:
                    fw = pltpu.make_async_remote_copy(
                        src_ref=fbuf[a].at[k], dst_ref=dbuf[d_index[a]].at[k],
                        send_sem=send.at[a, k], recv_sem=recv.at[a, k],
                        device_id=(x, y, 1 - c), device_id_type=MESH)
                    fw.start()
                    sends.append((a, k, fw))
        for a, k, fw in sends:
            px, py = chips[k]
            half = shard[a].shape[0] // 2
            fw.wait_recv()
            cp = pltpu.make_async_copy(dbuf[d_index[a]].at[k],
                                       outs[a].at[2 * px + py, pl.ds((1 - c) * half, half)],
                                       st_sib.at[a, k])
            cp.start()
            pending.append(cp)
        for _, _, fw in sends:
            fw.wait_send()
        for cp in pending:
            cp.wait()

    stage = [pltpu.VMEM(a.shape, a.dtype) for a in lands]
    dma = lambda *shape: pltpu.SemaphoreType.DMA(shape)
    return pl.pallas_call(
        body, name=name,
        in_specs=[ANY] * (2 * n), out_specs=[ANY] * n,
        out_shape=[jax.ShapeDtypeStruct((N_CHIPS,) + a.shape, a.dtype) for a in shards],
        scratch_shapes=[pltpu.VMEM(a.shape, a.dtype) for a in shards] + stage
        + [s for s, sp in zip(stage, split) if sp]
        + [dma(n), dma(n), dma(n, 3), dma(n, 3), dma(n, 3), dma(n, 3), dma(n, 3)],
        compiler_params=pltpu.CompilerParams(vmem_limit_bytes=VMEM_LIMIT_BYTES),
    )(*shards, *lands)


def _exchange_siblings(grads, name):
    n = len(grads)

    def body(*refs):
        ins, outs = refs[:n], refs[n:2 * n]
        sbuf, rbuf, mbuf = refs[2 * n:3 * n], refs[3 * n:4 * n], refs[4 * n:5 * n]
        ld_send, ld_mine, st, send, recv = refs[5 * n:]
        x, y, c, _ = _mesh_position()
        loads, mine, sends, stores = [], [], [], []
        for a in range(n):
            half = ins[a].shape[1] // 2
            cp = pltpu.make_async_copy(ins[a].at[:, pl.ds((1 - c) * half, half)], sbuf[a],
                                       ld_send.at[a])
            cp.start()
            loads.append(cp)
        for a in range(n):
            half = ins[a].shape[1] // 2
            cp = pltpu.make_async_copy(ins[a].at[:, pl.ds(c * half, half)], mbuf[a], ld_mine.at[a])
            cp.start()
            mine.append(cp)
        for a in range(n):
            loads[a].wait()
            rc = pltpu.make_async_remote_copy(
                src_ref=sbuf[a], dst_ref=rbuf[a], send_sem=send.at[a], recv_sem=recv.at[a],
                device_id=(x, y, 1 - c), device_id_type=MESH)
            rc.start()
            sends.append(rc)
        for a in range(n):
            sends[a].wait_recv()
            mine[a].wait()
            slots, half, _ = rbuf[a].shape
            rows = min(SUM_CHUNK_ROWS, half)
            per_slot = half // rows

            def add(k, carry, a=a, rows=rows, per_slot=per_slot):
                at = (k // per_slot, pl.ds(pl.multiple_of((k % per_slot) * rows, rows), rows))
                rbuf[a][at] = (rbuf[a][at].astype(F32) + mbuf[a][at].astype(F32)).astype(BF16)
                return carry

            lax.fori_loop(0, slots * per_slot, add, 0)
            cp = pltpu.make_async_copy(rbuf[a], outs[a], st.at[a])
            cp.start()
            stores.append(cp)
        for a in range(n):
            sends[a].wait_send()
            stores[a].wait()

    half_shape = lambda a: (a.shape[0], a.shape[1] // 2, a.shape[2])
    stage = [pltpu.VMEM(half_shape(a), a.dtype) for a in grads]
    return pl.pallas_call(
        body, name=name,
        in_specs=[ANY] * n, out_specs=[ANY] * n,
        out_shape=[jax.ShapeDtypeStruct(half_shape(a), a.dtype) for a in grads],
        scratch_shapes=stage * 3 + [pltpu.SemaphoreType.DMA((n,))] * 5,
        compiler_params=pltpu.CompilerParams(vmem_limit_bytes=VMEM_LIMIT_BYTES),
    )(*grads)


def _sum_and_share(sums, got, name):
    n = len(sums)

    def body(*refs):
        own, others, outs = refs[:n], refs[n:2 * n], refs[2 * n:3 * n]
        obuf, gbuf, sbuf, rbuf = (refs[(3 + k) * n:(4 + k) * n] for k in range(4))
        ld_own, ld_got, st_own, st_sib, send, recv = refs[7 * n:]
        x, y, c, _ = _mesh_position()
        loads, sends, stores = [], [], []
        for a in range(n):
            cps = [pltpu.make_async_copy(own[a].at[2 * x + y], obuf[a], ld_own.at[a]),
                   pltpu.make_async_copy(others[a], gbuf[a], ld_got.at[a])]
            for cp in cps:
                cp.start()
            loads.append(cps)
        for a in range(n):
            for cp in loads[a]:
                cp.wait()
            half = obuf[a].shape[0]
            rows = min(SUM_CHUNK_ROWS, half)

            def add(k, carry, a=a, rows=rows):
                at = pl.ds(pl.multiple_of(k * rows, rows), rows)
                acc = obuf[a][at].astype(F32)
                for j in range(N_CHIPS - 1):
                    acc = acc + gbuf[a][j, at].astype(F32)
                sbuf[a][at] = acc
                return carry

            lax.fori_loop(0, half // rows, add, 0)
            rc = pltpu.make_async_remote_copy(
                src_ref=sbuf[a], dst_ref=rbuf[a], send_sem=send.at[a], recv_sem=recv.at[a],
                device_id=(x, y, 1 - c), device_id_type=MESH)
            rc.start()
            sends.append(rc)
            cp = pltpu.make_async_copy(sbuf[a], outs[a].at[pl.ds(c * half, half)], st_own.at[a])
            cp.start()
            stores.append(cp)
        for a in range(n):
            half = obuf[a].shape[0]
            sends[a].wait_recv()
            cp = pltpu.make_async_copy(rbuf[a], outs[a].at[pl.ds((1 - c) * half, half)], st_sib.at[a])
            cp.start()
            stores.append(cp)
        for cp in sends:
            cp.wait_send()
        for cp in stores:
            cp.wait()

    halves = [a.shape[1:] for a in sums]
    return pl.pallas_call(
        body, name=name,
        in_specs=[ANY] * (2 * n), out_specs=[ANY] * n,
        out_shape=[jax.ShapeDtypeStruct((2 * h[0],) + h[1:], F32) for h in halves],
        scratch_shapes=[pltpu.VMEM(h, BF16) for h in halves]
        + [pltpu.VMEM(g.shape, BF16) for g in got]
        + [pltpu.VMEM(h, F32) for h in halves] * 2
        + [pltpu.SemaphoreType.DMA((n,))] * 6,
        compiler_params=pltpu.CompilerParams(vmem_limit_bytes=VMEM_LIMIT_BYTES),
    )(*sums, *got)


def _gather_small(part):
    def body(in_ref, out_ref, send, recv, local):
        x, y, c, _ = _mesh_position()
        me = 4 * x + 2 * y + c
        cps = [pltpu.make_async_copy(in_ref, out_ref.at[me], local)]
        k = 0
        for fx in range(2):
            for fy in range(2):
                for fc in range(2):
                    if fx or fy or fc:
                        cps.append(pltpu.make_async_remote_copy(
                            src_ref=in_ref, dst_ref=out_ref.at[me], send_sem=send.at[k],
                            recv_sem=recv.at[k], device_id=(x ^ fx, y ^ fy, c ^ fc),
                            device_id_type=MESH))
                        k += 1
        for cp in cps:
            cp.start()
        for cp in cps:
            cp.wait()

    return pl.pallas_call(
        body, name="gather_small",
        in_specs=[pl.BlockSpec(memory_space=pltpu.VMEM)],
        out_specs=pl.BlockSpec(memory_space=pltpu.VMEM),
        out_shape=jax.ShapeDtypeStruct((N_DEV,) + part.shape, part.dtype),
        scratch_shapes=[pltpu.SemaphoreType.DMA((N_DEV - 1,)), pltpu.SemaphoreType.DMA((N_DEV - 1,)),
                        pltpu.SemaphoreType.DMA],
    )(part)


def _scatter_start(grads, tag):
    sums = _exchange_siblings(grads, "exchange_siblings_" + tag)
    lands = [jax.ShapeDtypeStruct((N_CHIPS - 1,) + s.shape[1:], s.dtype) for s in sums]
    return _ici_start(sums, lands, _scatter_views, grads[0], "scatter_start_" + tag)


def _scatter_finish(handle, after, tag):
    sums, got = _ici_wait(handle, _scatter_views, after, "scatter_wait_" + tag)
    return _sum_and_share(sums, got, "sum_and_share_" + tag)


def _pad_rows(a, rows):
    return jnp.pad(a, ((0, rows - a.shape[0]), (0, 0)))


def kernel(x, norm_mix_0, w_in_0, b_f_0, conv_w_0, w_out_0, norm_ffn_0, w_up_0, w_down_0, norm_mix_1, pool_w_1, pool_scale_1, norm_ffn_1, w_up_1, w_down_1, final_norm, loss_target, m_norm_mix_0, m_w_in_0, m_b_f_0, m_conv_w_0, m_w_out_0, m_norm_ffn_0, m_w_up_0, m_w_down_0, m_norm_mix_1, m_pool_w_1, m_pool_scale_1, m_norm_ffn_1, m_w_up_1, m_w_down_1, m_final_norm, v_norm_mix_0, v_w_in_0, v_b_f_0, v_conv_w_0, v_w_out_0, v_norm_ffn_0, v_w_up_0, v_w_down_0, v_norm_mix_1, v_pool_w_1, v_pool_scale_1, v_norm_ffn_1, v_w_up_1, v_w_down_1, v_final_norm):
    d = x.shape[-1]
    a = N_HEADS * HEAD_DIM
    c_conv = conv_w_0.shape[1] * N_CHIPS
    xs = x[0]
    target = loss_target[0]
    row = lambda vec: vec.reshape(1, -1)

    big = [w_in_0, w_out_0, w_up_0, w_down_0, pool_w_1, w_up_1, w_down_1]
    first = [w_in_0.astype(BF16)]
    first_split = [True]
    rest = [w.astype(BF16) for w in (w_out_0, w_up_0, w_down_0, pool_w_1, w_up_1, w_down_1)]
    rest = rest + [conv_w_0]
    start_a = _ici_start(first, _gather_land_shapes(first, first_split),
                         _gather_views(first_split), b_f_0, "gather_start_a")
    start_b = _ici_start(rest, [jax.ShapeDtypeStruct((N_CHIPS,) + w.shape, w.dtype) for w in rest],
                         _gather_whole_views, start_a[-1], "gather_start_b")
    first, land_a = _ici_wait(start_a, _gather_views(first_split), start_b[-1], "gather_wait_a")
    (g_in,) = _gather_finish(first, land_a, first_split, "gather_finish_a")
    w_in = g_in.transpose(1, 0, 2).reshape(d, -1)
    w_qkv = w_in[:, :3 * a]
    w_f = jnp.pad(w_in[:, 3 * a:3 * a + N_HEADS], ((0, 0), (0, 128 - N_HEADS)))
    w_bcx = w_in[:, 3 * a + N_HEADS:]
    bf = jnp.pad(b_f_0, (0, 128 - N_HEADS)).reshape(1, 128)

    n0, qkv, fl, bcx = _ln_proj(xs, row(norm_mix_0), w_qkv, w_f, w_bcx)
    qa, ka = _gate_prep(fl, bf, qkv)
    o, lse = _attn_fwd(qa, ka, qkv)
    rest, land_b = _ici_wait(start_b, _gather_whole_views, o, "gather_wait_b")
    own_slot = 2 * lax.axis_index("x") + lax.axis_index("y")
    g_out, g_up0, g_down0, g_pool, g_up1, g_down1, g_conv = [
        lax.dynamic_update_index_in_dim(land, shard, own_slot, 0)
        for land, shard in zip(land_b, rest)]
    w_out = g_out.reshape(-1, d)
    conv_w = _pad_rows(g_conv.transpose(1, 0, 2).reshape(conv_w_0.shape[0], c_conv), 8)
    h1 = _conv_out(o, bcx, conv_w, w_out, xs)
    w_down0 = g_down0.reshape(-1, d)
    w_down1 = g_down1.reshape(-1, d)
    pool_w = g_pool.transpose(1, 0, 2, 3).reshape(pool_w_1.shape[0], -1, pool_w_1.shape[2])
    h2, a0, nf0 = _mlp_fwd(h1, row(norm_ffn_0), g_up0, w_down0, "mlp_fwd_0")
    h3 = _pool_fwd(h2, row(norm_mix_1), pool_w, row(pool_scale_1))
    dh4, a1, nf1, loss_part, d_final = _mlp_fwd(h3, row(norm_ffn_1), g_up1, w_down1, "mlp_fwd_1",
                                                head=(row(final_norm), target))

    slot_cols = g_up0.shape[2]
    pool_cols = pool_w.shape[2]
    da1, dz1, dh3, d_nffn1 = _mlp_bwd_x(dh4, a1, g_up1, w_down1, h3, row(norm_ffn_1), "mlp_bwd_x_1")
    dw_up1, dw_down1 = _mlp_bwd_w(nf1, da1, a1, dz1, slot_cols, "mlp_bwd_w_1")
    scatter_1 = _scatter_start([dw_up1, dw_down1.reshape(N_CHIPS, -1, d)], "mlp1")
    dh2, dw_pool, d_pscale, d_nmix1 = _pool_bwd(scatter_1[-1], dh3, h2, row(norm_mix_1), pool_w,
                                                row(pool_scale_1))
    da0, dz0, dh1, d_nffn0 = _mlp_bwd_x(dh2, a0, g_up0, w_down0, h1, row(norm_ffn_0), "mlp_bwd_x_0")
    dw_up0, dw_down0 = _mlp_bwd_w(nf0, da0, a0, dz0, slot_cols, "mlp_bwd_w_0")
    dw_pool = (dw_pool.reshape(pool_w.shape[0], N_CHIPS, -1, pool_cols).transpose(1, 0, 2, 3)
               .reshape(N_CHIPS, -1, pool_cols))
    scatter_0 = _scatter_start([dw_up0, dw_down0.reshape(N_CHIPS, -1, d), dw_pool], "mlp0")
    do, delta, dbcx, dw_out, d_conv = _conv_out_bwd(scatter_0[-1], dh1, w_out, o, bcx, conv_w)
    dqa, dka, dv = _attn_bwd(qa, ka, qkv, do, lse, delta)
    dqkv, dfl, d_bf = _gate_bwd(dqa, dka, dv, fl, bf)
    dw_qkv, dw_f, dw_bcx = _wgrad_in(n0, [dqkv, dfl, dbcx])
    dw_in = jnp.concatenate([dw_qkv, dw_f[:, :N_HEADS], dw_bcx], axis=1)
    scatter_m = _scatter_start([dw_in.reshape(d, N_CHIPS, -1).transpose(1, 0, 2),
                                dw_out.reshape(N_CHIPS, -1, d)], "mixer")
    grad_x, d_nmix0 = _in_proj_bwd(scatter_m[-1], dqkv, dfl, dbcx, w_qkv, w_f, w_bcx, xs,
                                   row(norm_mix_0), dh1)

    r_up1, r_down1 = _scatter_finish(scatter_1, grad_x, "mlp1")
    r_up0, r_down0, r_pool = _scatter_finish(scatter_0, grad_x, "mlp0")
    r_in, r_out = _scatter_finish(scatter_m, grad_x, "mixer")
    reduced = [r_in, r_out, r_up0, r_down0, r_pool, r_up1, r_down1]
    moments = [(m_w_in_0, v_w_in_0), (m_w_out_0, v_w_out_0), (m_w_up_0, v_w_up_0),
               (m_w_down_0, v_w_down_0), (m_pool_w_1, v_pool_w_1), (m_w_up_1, v_w_up_1),
               (m_w_down_1, v_w_down_1)]
    big_out = []
    for k, (w, g, (m, v)) in enumerate(zip(big, reduced, moments)):
        if w.shape[-1] % 128:
            view = lambda t: t.reshape(-1, t.shape[-1]).T
            back = lambda t: t.T.reshape(w.shape)
        else:
            view = lambda t: t.reshape(-1, t.shape[-1])
            back = lambda t: t.reshape(w.shape)
        g_view = view(g)
        delta_w, new_m, new_v = _adamw(view(w), g_view, view(m), view(v), "adamw_%d" % k)
        big_out.append((back(g_view), back(delta_w), back(new_m), back(new_v)))

    tail = jnp.concatenate([d_conv[0:3].reshape(-1)[d:], d_bf[0, :N_HEADS], loss_part[0, :1]])
    small_part = jnp.concatenate(
        [d_nmix0, d_nffn0, d_nmix1, d_pscale, d_nffn1, d_final,
         d_conv[0:3].reshape(1, -1)[:, :d],
         jnp.pad(tail, (0, d - tail.shape[0])).reshape(1, d)], axis=0)
    parts = _gather_small(small_part)

    chip = 2 * lax.axis_index("x") + lax.axis_index("y")
    cw_cols = conv_w_0.shape[1]

    def conv_block(full):
        mine = lax.dynamic_slice_in_dim(full, chip * cw_cols, cw_cols, axis=1)
        return jnp.pad(mine.reshape(-1), (0, d - mine.size))

    def small_rows(vals, cw, bfv):
        return jnp.stack(list(vals) + [cw, jnp.pad(bfv, (0, d - N_HEADS))])

    smalls_w = [norm_mix_0, norm_ffn_0, norm_mix_1, pool_scale_1, norm_ffn_1, final_norm]
    smalls_m = [m_norm_mix_0, m_norm_ffn_0, m_norm_mix_1, m_pool_scale_1, m_norm_ffn_1, m_final_norm]
    smalls_v = [v_norm_mix_0, v_norm_ffn_0, v_norm_mix_1, v_pool_scale_1, v_norm_ffn_1, v_final_norm]
    pad_cw = lambda t: jnp.pad(t.reshape(-1), (0, d - t.size))
    w_rows = small_rows(smalls_w, pad_cw(conv_w_0), b_f_0)
    m_rows = small_rows(smalls_m, pad_cw(m_conv_w_0), m_b_f_0)
    v_rows = small_rows(smalls_v, pad_cw(v_conv_w_0), v_b_f_0)

    g_sum = _sum_devices(parts)
    conv_full = jnp.concatenate([g_sum[6], g_sum[7, :3 * c_conv - d]]).reshape(3, c_conv)
    bf_grad = g_sum[7, 3 * c_conv - d:3 * c_conv - d + N_HEADS]
    loss = g_sum[7, 3 * c_conv - d + N_HEADS]
    g_rows = jnp.concatenate(
        [g_sum[0:6], conv_block(conv_full).reshape(1, d),
         jnp.pad(bf_grad, (0, d - N_HEADS)).reshape(1, d)], axis=0)
    d_rows, nm_rows, nv_rows = _adamw(w_rows, g_rows, m_rows, v_rows, "adamw_small")

    def unpack(rows):
        cw = rows[6, :conv_w_0.size].reshape(conv_w_0.shape)
        return [rows[0], rows[1], rows[2], rows[3], rows[4], rows[5], cw, rows[7, :N_HEADS]]

    def assemble(kind):
        sm = unpack([g_rows, d_rows, nm_rows, nv_rows][kind])
        lg = [t[kind] for t in big_out]
        return [sm[0], lg[0], sm[7], sm[6], lg[1], sm[1], lg[2], lg[3],
                sm[2], lg[4], sm[3], sm[4], lg[5], lg[6], sm[5]]

    return (loss, grad_x[None], *assemble(0), *assemble(1), *assemble(2), *assemble(3))
```

```python
import functools

import jax
import jax.numpy as jnp
from jax import lax
from jax.experimental import pallas as pl
from jax.experimental.pallas import tpu as pltpu

F32 = jnp.float32
BF16 = jnp.bfloat16

RMS_EPS = 1e-6
HEAD_DIM = 64
N_HEADS = 8
ATTN_SCALE = HEAD_DIM ** -0.5
LOG2_E = 1.4426950408889634
POOL_WINDOWS = (2, 4, 8, 16)
POOL_HALO = 16
CONV_HALO = 8
NEG_BIG = -1e30

ADAM_LR = 0.001
ADAM_B1 = 0.9
ADAM_B2 = 0.999
ADAM_EPS = 1e-08
ADAM_WD = 0.01
ADAM_STEP = 10

N_CHIPS = 4
N_DEV = 8
MESH = pl.DeviceIdType.MESH

VMEM_LIMIT_BYTES = 56 * 1024 * 1024

TILE_ROWS = 512
TILE_ATTN = 512
TILE_MLP_ROWS = 1024
TILE_MLP_FF = 1024
TILE_MLP_BWD_FF = 512
TILE_WGRAD_K = 1024
TILE_WGRAD_N = 1024
TILE_ELEM_ROWS = 256
SUM_CHUNK_ROWS = 128

LANE_CQ = 64
LANE_ONE = 67


def _params(semantics):
    return pltpu.CompilerParams(dimension_semantics=semantics,
                                vmem_limit_bytes=VMEM_LIMIT_BYTES)


def _nn(a, b):
    return lax.dot_general(a, b, (((1,), (0,)), ((), ())), preferred_element_type=F32)


def _nt(a, b):
    return lax.dot_general(a, b, (((1,), (1,)), ((), ())), preferred_element_type=F32)


def _tn(a, b):
    return lax.dot_general(a, b, (((0,), (0,)), ((), ())), preferred_element_type=F32)


def _split3(v):
    hi = v.astype(BF16)
    r1 = v - hi.astype(F32)
    mid = r1.astype(BF16)
    lo = (r1 - mid.astype(F32)).astype(BF16)
    return hi, mid, lo


def _exact_nn(sel, v):
    hi, mid, lo = _split3(v)
    return _nn(sel, hi) + _nn(sel, mid) + _nn(sel, lo)


def _exact_nt(sel, v):
    hi, mid, lo = _split3(v)
    return _nt(sel, hi) + _nt(sel, mid) + _nt(sel, lo)


def _rms_fwd(x, g):
    r = lax.rsqrt(jnp.mean(x * x, axis=-1, keepdims=True) + RMS_EPS)
    return x * r * g, r


def _rms_bwd(dn, x, g):
    r = lax.rsqrt(jnp.mean(x * x, axis=-1, keepdims=True) + RMS_EPS)
    xh = x * r
    gy = dn * g
    dx = r * (gy - xh * jnp.mean(gy * xh, axis=-1, keepdims=True))
    return dx, jnp.sum(dn * xh, axis=0, keepdims=True)


def _lane(shape):
    return lax.broadcasted_iota(jnp.int32, shape, len(shape) - 1)


def _row(shape):
    return lax.broadcasted_iota(jnp.int32, shape, len(shape) - 2)


def _full(a):
    nd = a.ndim
    return pl.BlockSpec(a.shape, lambda *_: (0,) * nd)


def _rms_pre(after, x, g):
    s, d = x.shape
    tm = min(TILE_ROWS, s)

    def body(after_ref, x_ref, g_ref, n_ref):
        n, _ = _rms_fwd(x_ref[...], g_ref[...])
        n_ref[...] = n.astype(BF16)

    rows = pl.BlockSpec((tm, d), lambda i: (i, 0))
    return pl.pallas_call(
        body, name="rms_pre", grid=(s // tm,),
        in_specs=[ANY, rows, _full(g)], out_specs=rows,
        out_shape=jax.ShapeDtypeStruct((s, d), BF16),
        compiler_params=_params(("parallel",)),
    )(after, x, g)


def _in_proj(n, w_qkv, w_f, w_bcx):
    s, d = n.shape
    tm = min(TILE_ROWS, s)

    def body(n_ref, wq_ref, wf_ref, wb_ref, qkv_ref, fl_ref, bcx_ref):
        nb = n_ref[...]
        qkv_ref[...] = _nn(nb, wq_ref[...]).astype(BF16)
        fl_ref[...] = _nn(nb, wf_ref[...])
        bcx_ref[...] = _nn(nb, wb_ref[...])

    rows = lambda c: pl.BlockSpec((tm, c), lambda i: (i, 0))
    return pl.pallas_call(
        body, name="in_proj", grid=(s // tm,),
        in_specs=[rows(d), _full(w_qkv), _full(w_f), _full(w_bcx)],
        out_specs=[rows(w_qkv.shape[1]), rows(w_f.shape[1]), rows(w_bcx.shape[1])],
        out_shape=[jax.ShapeDtypeStruct((s, w_qkv.shape[1]), BF16),
                   jax.ShapeDtypeStruct((s, w_f.shape[1]), F32),
                   jax.ShapeDtypeStruct((s, w_bcx.shape[1]), F32)],
        compiler_params=_params(("parallel",)),
    )(n, w_qkv, w_f, w_bcx)


def _gate_prep(fl, bf, qkv):
    s = fl.shape[0]
    a = N_HEADS * HEAD_DIM
    tm = min(TILE_ROWS, s)

    def body(fl_ref, bf_ref, q_ref, k_ref, qa_ref, ka_ref, carry_ref):
        i = pl.program_id(0)

        @pl.when(i == 0)
        def _():
            carry_ref[...] = jnp.zeros_like(carry_ref)

        z = fl_ref[...] + bf_ref[...]
        logf = jnp.minimum(z, 0.0) - jnp.log(1.0 + jnp.exp(-jnp.abs(z)))
        lower = (_lane((tm, tm)) <= _row((tm, tm))).astype(BF16)
        cum = _exact_nn(lower, logf) + carry_ref[0:1, :]
        carry_ref[0:1, :] = cum[tm - 1:tm, :]

        lane = _lane((tm, 128))
        for h in range(N_HEADS):
            cb = LOG2_E * jnp.sum(jnp.where(lane == h, cum, 0.0), axis=1, keepdims=True)
            hi, mid, lo = (p.astype(F32) for p in _split3(cb))
            pair = slice((h // 2) * 128, (h // 2 + 1) * 128)
            qp = q_ref[:, pair].astype(F32)
            kp = k_ref[:, pair].astype(F32)
            if h % 2:
                qp = pltpu.roll(qp, HEAD_DIM, axis=1)
                kp = pltpu.roll(kp, HEAD_DIM, axis=1)
            q_bias = jnp.where(lane == LANE_CQ, hi,
                               jnp.where(lane == LANE_CQ + 1, mid,
                                         jnp.where(lane == LANE_CQ + 2, lo,
                                                   jnp.where(lane < LANE_ONE + 3, 1.0, 0.0))))
            k_bias = jnp.where(lane < LANE_ONE, 1.0,
                               jnp.where(lane == LANE_ONE, -hi,
                                         jnp.where(lane == LANE_ONE + 1, -mid,
                                                   jnp.where(lane == LANE_ONE + 2, -lo, 0.0))))
            qa_ref[h] = jnp.where(lane < HEAD_DIM, qp * (ATTN_SCALE * LOG2_E), q_bias).astype(BF16)
            ka_ref[h] = jnp.where(lane < HEAD_DIM, kp, k_bias).astype(BF16)

    aug = jax.ShapeDtypeStruct((N_HEADS, s, 128), BF16)
    aug_spec = pl.BlockSpec((N_HEADS, tm, 128), lambda i: (0, i, 0))
    return pl.pallas_call(
        body, name="gate_prep", grid=(s // tm,),
        in_specs=[pl.BlockSpec((tm, 128), lambda i: (i, 0)), _full(bf),
                  pl.BlockSpec((tm, a), lambda i: (i, 0)),
                  pl.BlockSpec((tm, a), lambda i: (i, 1))],
        out_specs=[aug_spec, aug_spec],
        out_shape=[aug, aug],
        scratch_shapes=[pltpu.VMEM((8, 128), F32)],
        compiler_params=_params(("arbitrary",)),
    )(fl, bf, qkv, qkv)


def _attn_fwd(qa, ka, qkv):
    s = qa.shape[1]
    a = N_HEADS * HEAD_DIM
    t = min(TILE_ATTN, s)
    n_pairs = N_HEADS // 2
    v_block0 = 2 * a // 128

    ones_lane = (HEAD_DIM, 0)

    def body(qa_ref, ka_ref, v_ref, o_ref, lse_ref, m_ref, acc_ref, s_even, s_odd):
        i = pl.program_id(1)
        m_ref[...] = jnp.full_like(m_ref, NEG_BIG)
        acc_ref[...] = jnp.zeros_like(acc_ref)
        upper_rows = _row((128, t)) < HEAD_DIM

        def keys(j):
            return pl.ds(pl.multiple_of(j * t, t), t)

        def scores_into(buf, j):
            for e in range(2):
                buf[e] = _nt(ka_ref[e, keys(j), :], qa_ref[e])

        def consume(buf, j, masked):
            vf = v_ref[keys(j), :].astype(F32)
            lane = _lane((t, 128))
            own = [lane < HEAD_DIM, lane >= HEAD_DIM]
            for e in range(2):
                v_head = jnp.where(own[e], vf, jnp.where(lane == ones_lane[e], 1.0, 0.0)).astype(BF16)
                sc = buf[e]
                if masked:
                    sc = jnp.where(_row((t, t)) <= _lane((t, t)), sc, NEG_BIG)
                m_prev = m_ref[e]
                m_new = jnp.maximum(m_prev, jnp.max(sc, axis=0, keepdims=True))
                p = jnp.exp2(sc - m_new).astype(BF16)
                acc_ref[e] = acc_ref[e] * jnp.exp2(m_prev - m_new) + _tn(v_head, p)
                m_ref[e] = m_new

        scores_into(s_even, 0)

        def two_tiles(p, carry):
            j = 2 * p
            scores_into(s_odd, j + 1)
            consume(s_even, j, False)
            scores_into(s_even, j + 2)
            consume(s_odd, j + 1, False)
            return carry

        lax.fori_loop(0, i // 2, two_tiles, 0)

        @pl.when(i % 2 == 0)
        def _():
            consume(s_even, i, True)

        @pl.when(i % 2 == 1)
        def _():
            scores_into(s_odd, i)
            consume(s_even, i - 1, False)
            consume(s_odd, i, True)

        denom = [acc_ref[e, ones_lane[e]:ones_lane[e] + 1, :] for e in range(2)]
        out_t = jnp.where(upper_rows, acc_ref[0] / denom[0], acc_ref[1] / denom[1])
        o_ref[...] = out_t.T.astype(BF16)
        lse = [m_ref[e] + LOG2_E * jnp.log(denom[e]) for e in range(2)]
        lse_ref[...] = jnp.where(_row((8, t)) == 0, lse[0], lse[1])

    return pl.pallas_call(
        body, name="attn_fwd", grid=(n_pairs, s // t),
        in_specs=[pl.BlockSpec((2, t, 128), lambda g, i: (g, i, 0)),
                  pl.BlockSpec((2, s, 128), lambda g, i: (g, 0, 0)),
                  pl.BlockSpec((s, 128), lambda g, i: (0, v_block0 + g))],
        out_specs=[pl.BlockSpec((t, 128), lambda g, i: (i, g)),
                   pl.BlockSpec((None, 8, t), lambda g, i: (g, 0, i))],
        out_shape=[jax.ShapeDtypeStruct((s, a), BF16),
                   jax.ShapeDtypeStruct((n_pairs, 8, s), F32)],
        scratch_shapes=[pltpu.VMEM((2, 1, t), F32), pltpu.VMEM((2, 128, t), F32),
                        pltpu.VMEM((2, t, t), F32), pltpu.VMEM((2, t, t), F32)],
        compiler_params=_params(("parallel", "arbitrary")),
    )(qa, ka, qkv)


def _conv_out(o, bcx, cw, w_out, x):
    s, d = x.shape
    c = o.shape[1]
    tm = min(TILE_ROWS, s)

    def body(o_ref, b_ref, c_ref, xin_ref, cw_ref, w_ref, x_ref, h_ref, ubuf):
        i = pl.program_id(0)

        @pl.when(i == 0)
        def _():
            ubuf[0:CONV_HALO, :] = jnp.zeros((CONV_HALO, c), F32)

        u = c_ref[...] * xin_ref[...]
        ubuf[CONV_HALO:CONV_HALO + tm, :] = u
        u1 = ubuf[CONV_HALO - 1:CONV_HALO - 1 + tm, :]
        u2 = ubuf[CONV_HALO - 2:CONV_HALO - 2 + tm, :]
        cv = (cw_ref[0:1, :] * u2 + cw_ref[1:2, :] * u1) + cw_ref[2:3, :] * u
        y = (b_ref[...] * cv).astype(BF16)
        mix = _nn(o_ref[...], w_ref[0:c, :]) + _nn(y, w_ref[c:2 * c, :])
        h_ref[...] = x_ref[...] + mix
        ubuf[0:CONV_HALO, :] = u[tm - CONV_HALO:tm, :]

    col = lambda k: pl.BlockSpec((tm, c), lambda i: (i, k))
    return pl.pallas_call(
        body, name="conv_out", grid=(s // tm,),
        in_specs=[col(0), col(0), col(1), col(2), _full(cw), _full(w_out),
                  pl.BlockSpec((tm, d), lambda i: (i, 0))],
        out_specs=pl.BlockSpec((tm, d), lambda i: (i, 0)),
        out_shape=jax.ShapeDtypeStruct((s, d), F32),
        scratch_shapes=[pltpu.VMEM((tm + CONV_HALO, c), F32)],
        compiler_params=_params(("arbitrary",)),
    )(o, bcx, bcx, bcx, cw, w_out, x)


def _mlp_fwd(h, g, w_up, w_down, name, head=None):
    s, d = h.shape
    ff = w_down.shape[0]
    slot_cols = w_up.shape[2]
    tm = min(TILE_MLP_ROWS, s)
    tf = min(TILE_MLP_FF if head is None else TILE_MLP_FF // 2, slot_cols)
    per_slot = slot_cols // tf
    nf = ff // tf
    n_head = 0 if head is None else 2

    def body(*refs):
        h_ref, g_ref, wu_ref, wd_ref = refs[:4]
        out_ref, a_ref, n_ref = refs[4 + n_head:7 + n_head]
        nb_ref, acc_ref = refs[-2:]
        i = pl.program_id(0)
        f = pl.program_id(1)

        @pl.when(f == 0)
        def _():
            n, _ = _rms_fwd(h_ref[...], g_ref[...])
            nb = n.astype(BF16)
            nb_ref[...] = nb
            n_ref[...] = nb
            acc_ref[...] = jnp.zeros_like(acc_ref)

        pre = _nn(nb_ref[...], wu_ref[...])
        a_ref[...] = pre.astype(BF16)
        r = jnp.square(jnp.maximum(pre, 0.0)).astype(BF16)
        acc_ref[...] += _nn(r, wd_ref[...])

        @pl.when(f == nf - 1)
        def _():
            out = h_ref[...] + acc_ref[...]
            if head is None:
                out_ref[...] = out
            else:
                gf_ref, t_ref = refs[4:6]
                loss_ref, dg_ref = refs[7 + n_head:9 + n_head]
                y, _ = _rms_fwd(out, gf_ref[...])
                err = y - t_ref[...]
                part = 0.5 * jnp.sum(jnp.mean(err * err, axis=-1, keepdims=True), axis=0,
                                     keepdims=True)
                dx, dg = _rms_bwd(err / d, out, gf_ref[...])
                out_ref[...] = dx
                part = jnp.broadcast_to(part, loss_ref.shape)

                @pl.when(i == 0)
                def _():
                    loss_ref[...] = part
                    dg_ref[...] = dg

                @pl.when(i > 0)
                def _():
                    loss_ref[...] += part
                    dg_ref[...] += dg

    rows = pl.BlockSpec((tm, d), lambda i, f: (i, 0))
    in_specs = [rows, _full(g),
                pl.BlockSpec((None, d, tf), lambda i, f: (f // per_slot, 0, f % per_slot)),
                pl.BlockSpec((tf, d), lambda i, f: (f, 0))]
    out_specs = [rows, pl.BlockSpec((tm, tf), lambda i, f: (i, f)), rows]
    out_shape = [jax.ShapeDtypeStruct((s, d), F32), jax.ShapeDtypeStruct((s, ff), BF16),
                 jax.ShapeDtypeStruct((s, d), BF16)]
    args = [h, g, w_up, w_down]
    if head is not None:
        in_specs += [_full(head[0]), rows]
        args += list(head)
        out_specs += [pl.BlockSpec((1, 128), lambda i, f: (0, 0)),
                      pl.BlockSpec((1, d), lambda i, f: (0, 0))]
        out_shape += [jax.ShapeDtypeStruct((1, 128), F32), jax.ShapeDtypeStruct((1, d), F32)]
    return pl.pallas_call(
        body, name=name, grid=(s // tm, nf),
        in_specs=in_specs, out_specs=out_specs, out_shape=out_shape,
        scratch_shapes=[pltpu.VMEM((tm, d), BF16), pltpu.VMEM((tm, d), F32)],
        compiler_params=_params(("parallel" if head is None else "arbitrary", "arbitrary")),
    )(*args)


def _window_sum_down(e, window):
    step = 1
    while step < window:
        e = e + pltpu.roll(e, step, axis=0)
        step *= 2
    return e


def _window_sum_up(e, window):
    n = e.shape[0]
    step = 1
    while step < window:
        e = e + pltpu.roll(e, n - step, axis=0)
        step *= 2
    return e


def _pool_counts(first_row, tm, window):
    t = first_row + _row((tm, 1))
    return jnp.minimum(t + 1, window).astype(F32)


def _pool_fwd(h, g, pw, ps):
    s, d = h.shape
    cg = d // len(POOL_WINDOWS)
    tm = min(TILE_ROWS, s)

    def body(h_ref, g_ref, pw_ref, ps_ref, out_ref, nbuf):
        i = pl.program_id(0)

        @pl.when(i == 0)
        def _():
            nbuf[0:POOL_HALO, :] = jnp.zeros((POOL_HALO, d), F32)

        n, _ = _rms_fwd(h_ref[...], g_ref[...])
        nbuf[POOL_HALO:POOL_HALO + tm, :] = n
        for k, window in enumerate(POOL_WINDOWS):
            cols = slice(k * cg, (k + 1) * cg)
            sums = _window_sum_down(nbuf[:, cols], window)[POOL_HALO:, :]
            pooled = sums / _pool_counts(i * tm, tm, window) - n[:, cols]
            y = _nn(pooled.astype(BF16), pw_ref[k]) * ps_ref[:, cols]
            out_ref[:, cols] = h_ref[:, cols] + y
        nbuf[0:POOL_HALO, :] = n[tm - POOL_HALO:tm, :]

    return pl.pallas_call(
        body, name="pool_fwd", grid=(s // tm,),
        in_specs=[pl.BlockSpec((tm, d), lambda i: (i, 0)), _full(g), _full(pw), _full(ps)],
        out_specs=pl.BlockSpec((tm, d), lambda i: (i, 0)),
        out_shape=jax.ShapeDtypeStruct((s, d), F32),
        scratch_shapes=[pltpu.VMEM((tm + POOL_HALO, d), F32)],
        compiler_params=_params(("arbitrary",)),
    )(h, g, pw, ps)


def _mlp_bwd_x(dz, a, w_up, w_down, h_in, g, name):
    s, d = dz.shape
    ff = w_down.shape[0]
    slot_cols = w_up.shape[2]
    tm = min(TILE_MLP_ROWS, s)
    tf = min(TILE_MLP_BWD_FF, slot_cols)
    per_slot = slot_cols // tf
    nf = ff // tf

    def body(dz_ref, a_ref, wu_ref, wd_ref, h_ref, g_ref, da_ref, dzb_ref, dh_ref, dg_ref,
             dzs_ref, acc_ref):
        i = pl.program_id(0)
        f = pl.program_id(1)

        @pl.when(f == 0)
        def _():
            dzb = dz_ref[...].astype(BF16)
            dzs_ref[...] = dzb
            dzb_ref[...] = dzb
            acc_ref[...] = jnp.zeros_like(acc_ref)

        dr = _nt(dzs_ref[...], wd_ref[...])
        da = (dr * (2.0 * jnp.maximum(a_ref[...].astype(F32), 0.0))).astype(BF16)
        da_ref[...] = da
        acc_ref[...] += _nt(da, wu_ref[...])

        @pl.when(f == nf - 1)
        def _():
            dx, dg = _rms_bwd(acc_ref[...], h_ref[...], g_ref[...])
            dh_ref[...] = dz_ref[...] + dx

            @pl.when(i == 0)
            def _():
                dg_ref[...] = dg

            @pl.when(i > 0)
            def _():
                dg_ref[...] += dg

    return pl.pallas_call(
        body, name=name, grid=(s // tm, nf),
        in_specs=[pl.BlockSpec((tm, d), lambda i, f: (i, 0)),
                  pl.BlockSpec((tm, tf), lambda i, f: (i, f)),
                  pl.BlockSpec((None, d, tf), lambda i, f: (f // per_slot, 0, f % per_slot)),
                  pl.BlockSpec((tf, d), lambda i, f: (f, 0)),
                  pl.BlockSpec((tm, d), lambda i, f: (i, 0)), _full(g)],
        out_specs=[pl.BlockSpec((tm, tf), lambda i, f: (i, f)),
                   pl.BlockSpec((tm, d), lambda i, f: (i, 0)),
                   pl.BlockSpec((tm, d), lambda i, f: (i, 0)),
                   pl.BlockSpec((1, d), lambda i, f: (0, 0))],
        out_shape=[jax.ShapeDtypeStruct((s, ff), BF16),
                   jax.ShapeDtypeStruct((s, d), BF16),
                   jax.ShapeDtypeStruct((s, d), F32),
                   jax.ShapeDtypeStruct((1, d), F32)],
        scratch_shapes=[pltpu.VMEM((tm, d), BF16), pltpu.VMEM((tm, d), F32)],
        compiler_params=_params(("arbitrary", "arbitrary")),
    )(dz, a, w_up, w_down, h_in, g)


def _mlp_bwd_w(n, da, a, dzb, slot_cols, name):
    s, d = n.shape
    ff = a.shape[1]
    tn = min(TILE_WGRAD_N, slot_cols)
    tk = min(TILE_WGRAD_K, s)
    per_slot = slot_cols // tn
    nk = s // tk

    def body(n_ref, da_ref, a_ref, dz_ref, du_ref, dd_ref, accu_ref, accd_ref):
        k = pl.program_id(1)

        @pl.when(k == 0)
        def _():
            accu_ref[...] = jnp.zeros_like(accu_ref)
            accd_ref[...] = jnp.zeros_like(accd_ref)

        accu_ref[...] += _tn(n_ref[...], da_ref[...])
        r = jnp.square(jnp.maximum(a_ref[...].astype(F32), 0.0)).astype(BF16)
        accd_ref[...] += _tn(r, dz_ref[...])

        @pl.when(k == nk - 1)
        def _():
            du_ref[...] = accu_ref[...].astype(BF16)
            dd_ref[...] = accd_ref[...].astype(BF16)

    return pl.pallas_call(
        body, name=name, grid=(ff // tn, nk),
        in_specs=[pl.BlockSpec((tk, d), lambda f, k: (k, 0)),
                  pl.BlockSpec((tk, tn), lambda f, k: (k, f)),
                  pl.BlockSpec((tk, tn), lambda f, k: (k, f)),
                  pl.BlockSpec((tk, d), lambda f, k: (k, 0))],
        out_specs=[pl.BlockSpec((None, d, tn), lambda f, k: (f // per_slot, 0, f % per_slot)),
                   pl.BlockSpec((tn, d), lambda f, k: (f, 0))],
        out_shape=[jax.ShapeDtypeStruct((ff // slot_cols, d, slot_cols), BF16),
                   jax.ShapeDtypeStruct((ff, d), BF16)],
        scratch_shapes=[pltpu.VMEM((d, tn), F32), pltpu.VMEM((tn, d), F32)],
        compiler_params=_params(("parallel", "arbitrary")),
    )(n, da, a, dzb)


def _pool_bwd(after, dh, h, g, pw, ps):
    s, d = h.shape
    cg = d // len(POOL_WINDOWS)
    tm = min(TILE_ROWS, s)
    nb = s // tm
    halo_per_tile = tm // POOL_HALO

    def body(after_ref, dh_ref, h_ref, halo_ref, g_ref, pw_ref, ps_ref,
             dx_ref, dpw_ref, dps_ref, dg_ref, nbuf, qbuf, dn_ref, carry, dpw_acc):
        i = pl.program_id(0)
        blk = nb - 1 - i

        @pl.when(i == 0)
        def _():
            carry[...] = jnp.zeros_like(carry)
            dpw_acc[...] = jnp.zeros_like(dpw_acc)
            dps_ref[...] = jnp.zeros_like(dps_ref)
            dg_ref[...] = jnp.zeros_like(dg_ref)

        hv = h_ref[...]
        n, _ = _rms_fwd(hv, g_ref[...])
        nh, _ = _rms_fwd(halo_ref[...], g_ref[...])
        nbuf[0:POOL_HALO, :] = jnp.where(blk == 0, 0.0, nh)
        nbuf[POOL_HALO:POOL_HALO + tm, :] = n
        dhv = dh_ref[...]
        for k, window in enumerate(POOL_WINDOWS):
            cols = slice(k * cg, (k + 1) * cg)
            cnt = _pool_counts(blk * tm, tm, window)
            sums = _window_sum_down(nbuf[:, cols], window)[POOL_HALO:, :]
            pb = (sums / cnt - n[:, cols]).astype(BF16)
            dyk = dhv[:, cols]
            dps_ref[:, cols] += jnp.sum(dyk * _nn(pb, pw_ref[k]), axis=0, keepdims=True)
            dyb = (dyk * ps_ref[:, cols]).astype(BF16)
            dpw_acc[k] += _tn(pb, dyb)
            dpool = _nt(dyb, pw_ref[k])
            qv = dpool / cnt
            qbuf[0:tm, cols] = qv
            qbuf[tm:tm + POOL_HALO, cols] = carry[:, cols]
            dn_ref[:, cols] = _window_sum_up(qbuf[:, cols], window)[0:tm, :] - dpool
            carry[:, cols] = qv[0:POOL_HALO, :]
        dx, dg = _rms_bwd(dn_ref[...], hv, g_ref[...])
        dx_ref[...] = dhv + dx
        dg_ref[...] += dg

        @pl.when(i == nb - 1)
        def _():
            dpw_ref[...] = dpw_acc[...].astype(BF16)

    rev = lambda i: (nb - 1 - i, 0)
    return pl.pallas_call(
        body, name="pool_bwd", grid=(nb,),
        in_specs=[ANY, pl.BlockSpec((tm, d), rev), pl.BlockSpec((tm, d), rev),
                  pl.BlockSpec((POOL_HALO, d),
                               lambda i: (jnp.maximum((nb - 1 - i) * halo_per_tile - 1, 0), 0)),
                  _full(g), _full(pw), _full(ps)],
        out_specs=[pl.BlockSpec((tm, d), rev), _full(pw),
                   pl.BlockSpec((1, d), lambda i: (0, 0)),
                   pl.BlockSpec((1, d), lambda i: (0, 0))],
        out_shape=[jax.ShapeDtypeStruct((s, d), F32),
                   jax.ShapeDtypeStruct(pw.shape, BF16),
                   jax.ShapeDtypeStruct((1, d), F32),
                   jax.ShapeDtypeStruct((1, d), F32)],
        scratch_shapes=[pltpu.VMEM((tm + POOL_HALO, d), F32), pltpu.VMEM((tm + POOL_HALO, d), F32),
                        pltpu.VMEM((tm, d), F32), pltpu.VMEM((POOL_HALO, d), F32),
                        pltpu.VMEM(pw.shape, F32)],
        compiler_params=_params(("arbitrary",)),
    )(after, dh, h, h, g, pw, ps)


def _conv_out_bwd(after, dh, w_out, o, bcx, cw):
    s, d = dh.shape
    c = o.shape[1]
    tm = min(TILE_ROWS, s)
    nb = s // tm
    halo_per_tile = tm // CONV_HALO

    def body(after_ref, dh_ref, w_ref, o_ref, b_ref, c_ref, xin_ref, ch_ref, xh_ref, cw_ref,
             do_ref, delta_ref, dbcx_ref, dw_ref, dcw_ref, ubuf, dbuf, carry, acc):
        i = pl.program_id(0)
        blk = nb - 1 - i

        @pl.when(i == 0)
        def _():
            carry[...] = jnp.zeros_like(carry)
            acc[...] = jnp.zeros_like(acc)
            dcw_ref[...] = jnp.zeros_like(dcw_ref)

        dm = dh_ref[...].astype(BF16)
        dcat = _nt(dm, w_ref[...])
        do = dcat[:, 0:c]
        dy = dcat[:, c:2 * c]
        do_ref[...] = do.astype(BF16)
        head_of_lane = lax.shift_right_logical(_lane((8, c)), HEAD_DIM.bit_length() - 1)
        heads = (head_of_lane == _row((8, c))).astype(BF16)
        delta_ref[...] = _exact_nt(heads, do * o_ref[...].astype(F32))

        cv_ = c_ref[...]
        xin = xin_ref[...]
        bv = b_ref[...]
        u = cv_ * xin
        ubuf[0:CONV_HALO, :] = jnp.where(blk == 0, 0.0, ch_ref[...] * xh_ref[...])
        ubuf[CONV_HALO:CONV_HALO + tm, :] = u
        u1 = ubuf[CONV_HALO - 1:CONV_HALO - 1 + tm, :]
        u2 = ubuf[CONV_HALO - 2:CONV_HALO - 2 + tm, :]
        w0, w1, w2 = cw_ref[0:1, :], cw_ref[1:2, :], cw_ref[2:3, :]
        cv = (w0 * u2 + w1 * u1) + w2 * u
        acc[0:c, :] += _tn(o_ref[...], dm)
        acc[c:2 * c, :] += _tn((bv * cv).astype(BF16), dm)

        dcv = dy * bv
        dcw_ref[0:1, :] += jnp.sum(dcv * u2, axis=0, keepdims=True)
        dcw_ref[1:2, :] += jnp.sum(dcv * u1, axis=0, keepdims=True)
        dcw_ref[2:3, :] += jnp.sum(dcv * u, axis=0, keepdims=True)
        dbuf[0:tm, :] = dcv
        dbuf[tm:tm + CONV_HALO, :] = carry[...]
        du = w2 * dcv + w1 * dbuf[1:1 + tm, :] + w0 * dbuf[2:2 + tm, :]
        dbcx_ref[:, 0:c] = (dy * cv).astype(BF16)
        dbcx_ref[:, c:2 * c] = (du * xin).astype(BF16)
        dbcx_ref[:, 2 * c:3 * c] = (du * cv_).astype(BF16)
        carry[...] = dcv[0:CONV_HALO, :]

        @pl.when(i == nb - 1)
        def _():
            dw_ref[...] = acc[...].astype(BF16)

    rev = lambda k: (lambda i: (nb - 1 - i, k))
    halo = lambda k: (lambda i: (jnp.maximum((nb - 1 - i) * halo_per_tile - 1, 0), k))
    return pl.pallas_call(
        body, name="conv_out_bwd", grid=(nb,),
        in_specs=[ANY, pl.BlockSpec((tm, d), rev(0)), _full(w_out), pl.BlockSpec((tm, c), rev(0)),
                  pl.BlockSpec((tm, c), rev(0)), pl.BlockSpec((tm, c), rev(1)),
                  pl.BlockSpec((tm, c), rev(2)),
                  pl.BlockSpec((CONV_HALO, c), halo(1)), pl.BlockSpec((CONV_HALO, c), halo(2)),
                  _full(cw)],
        out_specs=[pl.BlockSpec((tm, c), rev(0)),
                   pl.BlockSpec((8, tm), lambda i: (0, nb - 1 - i)),
                   pl.BlockSpec((tm, 3 * c), rev(0)),
                   _full(w_out), _full(cw)],
        out_shape=[jax.ShapeDtypeStruct((s, c), BF16),
                   jax.ShapeDtypeStruct((8, s), F32),
                   jax.ShapeDtypeStruct((s, 3 * c), BF16),
                   jax.ShapeDtypeStruct(w_out.shape, BF16),
                   jax.ShapeDtypeStruct(cw.shape, F32)],
        scratch_shapes=[pltpu.VMEM((tm + CONV_HALO, c), F32), pltpu.VMEM((tm + CONV_HALO, c), F32),
                        pltpu.VMEM((CONV_HALO, c), F32), pltpu.VMEM(w_out.shape, F32)],
        compiler_params=_params(("arbitrary",)),
    )(after, dh, w_out, o, bcx, bcx, bcx, bcx, bcx, cw)


def _attn_bwd(qa, ka, qkv, do, lse, delta):
    s = qa.shape[1]
    a = N_HEADS * HEAD_DIM
    t = min(TILE_ATTN, s)
    nq = s // t
    n_pairs = N_HEADS // 2
    v_block0 = 2 * a // 128

    def body(ka_ref, v_ref, qa_ref, do_ref, lse_ref, delta_ref,
             dqt_ref, dka_ref, dv_ref, dk_acc, dv_acc):
        g = pl.program_id(0)
        j = pl.program_id(1)

        @pl.when(j == 0)
        def _():
            dqt_ref[...] = jnp.zeros_like(dqt_ref)

        dk_acc[...] = jnp.zeros_like(dk_acc)
        dv_acc[...] = jnp.zeros_like(dv_acc)
        lane = _lane((t, 128))
        vf = v_ref[...].astype(F32)
        v_heads = [jnp.where(lane < HEAD_DIM, vf, 0.0).astype(BF16),
                   jnp.where(lane >= HEAD_DIM, vf, 0.0).astype(BF16)]
        ke_t = [ka_ref[e].astype(F32).T.astype(BF16) for e in range(2)]

        def q_step(i, masked):
            qs = pl.ds(pl.multiple_of(i * t, t), t)
            dob = do_ref[qs, :]
            for e in range(2):
                qe = qa_ref[e, qs, :]
                sc = _nt(ka_ref[e], qe)
                if masked:
                    sc = jnp.where(_row((t, t)) <= _lane((t, t)), sc, NEG_BIG)
                p = jnp.exp2(sc - lse_ref[pl.ds(e, 1), qs])
                dv_acc[e] += _nn(p.astype(BF16), dob)
                dp = _nt(v_heads[e], dob)
                ds = (p * (dp - delta_ref[pl.ds(2 * g + e, 1), qs])).astype(BF16)
                dk_acc[e] += _nn(ds, qe)
                dqt_ref[e, :, qs] += _nn(ke_t[e], ds)

        q_step(j, True)

        def full_step(i, carry):
            q_step(i, False)
            return carry

        lax.fori_loop(j + 1, nq, full_step, 0)
        dka_ref[...] = dk_acc[...]
        dv_ref[...] = jnp.where(lane < HEAD_DIM, dv_acc[0], dv_acc[1]).astype(BF16)

    return pl.pallas_call(
        body, name="attn_bwd", grid=(n_pairs, nq),
        in_specs=[pl.BlockSpec((2, t, 128), lambda g, j: (g, j, 0)),
                  pl.BlockSpec((t, 128), lambda g, j: (j, v_block0 + g)),
                  pl.BlockSpec((2, s, 128), lambda g, j: (g, 0, 0)),
                  pl.BlockSpec((s, 128), lambda g, j: (0, g)),
                  pl.BlockSpec((None, 8, s), lambda g, j: (g, 0, 0)),
                  pl.BlockSpec((8, s), lambda g, j: (0, 0))],
        out_specs=[pl.BlockSpec((2, 128, s), lambda g, j: (g, 0, 0)),
                   pl.BlockSpec((2, t, 128), lambda g, j: (g, j, 0)),
                   pl.BlockSpec((t, 128), lambda g, j: (j, g))],
        out_shape=[jax.ShapeDtypeStruct((N_HEADS, 128, s), F32),
                   jax.ShapeDtypeStruct((N_HEADS, s, 128), F32),
                   jax.ShapeDtypeStruct((s, a), BF16)],
        scratch_shapes=[pltpu.VMEM((2, t, 128), F32), pltpu.VMEM((2, t, 128), F32)],
        compiler_params=_params(("parallel", "arbitrary")),
    )(ka, qkv, qa, do, lse, delta)


def _gate_bwd(dqa, dka, dv, fl, bf):
    s = fl.shape[0]
    a = N_HEADS * HEAD_DIM
    tm = min(TILE_ROWS, s)
    nb = s // tm

    def body(dqa_ref, dka_ref, dv_ref, fl_ref, bf_ref, dqkv_ref, dfl_ref, dbf_ref, carry):
        i = pl.program_id(0)

        @pl.when(i == 0)
        def _():
            carry[...] = jnp.zeros_like(carry)
            dbf_ref[...] = jnp.zeros_like(dbf_ref)

        lane = _lane((tm, 128))
        dcum = jnp.zeros((tm, 128), F32)
        for pair in range(N_HEADS // 2):
            qs, ks = [], []
            for e in range(2):
                h = 2 * pair + e
                dq = dqa_ref[h].T
                dk = dka_ref[h]
                dc = jnp.sum(jnp.where(lane == LANE_CQ, dq, 0.0)
                             - jnp.where(lane == LANE_ONE, dk, 0.0), axis=1, keepdims=True)
                dcum = jnp.where(lane == h, dc, dcum)
                qs.append(dq * ATTN_SCALE)
                ks.append(dk * (1.0 / LOG2_E))
            cols = slice(pair * 128, (pair + 1) * 128)
            dqkv_ref[:, cols] = jnp.where(
                lane < HEAD_DIM, qs[0], pltpu.roll(qs[1], HEAD_DIM, axis=1)).astype(BF16)
            dqkv_ref[:, a + pair * 128:a + (pair + 1) * 128] = jnp.where(
                lane < HEAD_DIM, ks[0], pltpu.roll(ks[1], HEAD_DIM, axis=1)).astype(BF16)
        dqkv_ref[:, 2 * a:3 * a] = dv_ref[...]

        upper = (_lane((tm, tm)) >= _row((tm, tm))).astype(BF16)
        dlogf = _exact_nn(upper, dcum) + carry[0:1, :]
        carry[0:1, :] = dlogf[0:1, :]
        z = fl_ref[...] + bf_ref[...]
        ez = jnp.exp(-jnp.abs(z))
        sig_neg = jnp.where(z >= 0.0, ez, 1.0) / (1.0 + ez)
        dz = jnp.where(lane < N_HEADS, dlogf * sig_neg, 0.0)
        dfl_ref[...] = dz.astype(BF16)
        dbf_ref[...] += jnp.sum(dz, axis=0, keepdims=True)

    rev3 = lambda i: (0, nb - 1 - i, 0)
    rev = lambda i: (nb - 1 - i, 0)
    return pl.pallas_call(
        body, name="gate_bwd", grid=(nb,),
        in_specs=[pl.BlockSpec((N_HEADS, 128, tm), lambda i: (0, 0, nb - 1 - i)),
                  pl.BlockSpec((N_HEADS, tm, 128), rev3),
                  pl.BlockSpec((tm, a), rev), pl.BlockSpec((tm, 128), rev), _full(bf)],
        out_specs=[pl.BlockSpec((tm, 3 * a), rev), pl.BlockSpec((tm, 128), rev),
                   pl.BlockSpec((1, 128), lambda i: (0, 0))],
        out_shape=[jax.ShapeDtypeStruct((s, 3 * a), BF16),
                   jax.ShapeDtypeStruct((s, 128), BF16),
                   jax.ShapeDtypeStruct((1, 128), F32)],
        scratch_shapes=[pltpu.VMEM((8, 128), F32)],
        compiler_params=_params(("arbitrary",)),
    )(dqa, dka, dv, fl, bf)


def _in_proj_bwd(after, dqkv, dfl, dbcx, w_qkv, w_f, w_bcx, x, g, dh):
    s, d = x.shape
    tm = min(TILE_ROWS, s)

    def body(after_ref, dq_ref, df_ref, db_ref, wq_ref, wf_ref, wb_ref, x_ref, g_ref, dh_ref,
             gx_ref, dg_ref):
        i = pl.program_id(0)
        dn = (_nt(dq_ref[...], wq_ref[...]) + _nt(df_ref[...], wf_ref[...])
              + _nt(db_ref[...], wb_ref[...]))
        dx, dg = _rms_bwd(dn, x_ref[...], g_ref[...])
        gx_ref[...] = dh_ref[...] + dx

        @pl.when(i == 0)
        def _():
            dg_ref[...] = dg

        @pl.when(i > 0)
        def _():
            dg_ref[...] += dg

    rows = lambda c: pl.BlockSpec((tm, c), lambda i: (i, 0))
    return pl.pallas_call(
        body, name="in_proj_bwd", grid=(s // tm,),
        in_specs=[ANY, rows(dqkv.shape[1]), rows(dfl.shape[1]), rows(dbcx.shape[1]),
                  _full(w_qkv), _full(w_f), _full(w_bcx), rows(d), _full(g), rows(d)],
        out_specs=[rows(d), pl.BlockSpec((1, d), lambda i: (0, 0))],
        out_shape=[jax.ShapeDtypeStruct((s, d), F32), jax.ShapeDtypeStruct((1, d), F32)],
        compiler_params=_params(("arbitrary",)),
    )(after, dqkv, dfl, dbcx, w_qkv, w_f, w_bcx, x, g, dh)


def _wgrad_in(n, dys):
    s, d = n.shape
    m = len(dys)
    tk = min(TILE_ROWS, s)
    nk = s // tk

    def body(*refs):
        n_ref, dy_refs, dw_refs, accs = refs[0], refs[1:1 + m], refs[1 + m:1 + 2 * m], refs[1 + 2 * m:]
        k = pl.program_id(0)

        @pl.when(k == 0)
        def _():
            for acc in accs:
                acc[...] = jnp.zeros_like(acc)

        nb = n_ref[...]
        for dy_ref, acc in zip(dy_refs, accs):
            acc[...] += _tn(nb, dy_ref[...])

        @pl.when(k == nk - 1)
        def _():
            for dw_ref, acc in zip(dw_refs, accs):
                dw_ref[...] = acc[...].astype(BF16)

    return pl.pallas_call(
        body, name="wgrad_in", grid=(nk,),
        in_specs=[pl.BlockSpec((tk, d), lambda k: (k, 0))]
        + [pl.BlockSpec((tk, dy.shape[1]), lambda k: (k, 0)) for dy in dys],
        out_specs=[pl.BlockSpec((d, dy.shape[1]), lambda k: (0, 0)) for dy in dys],
        out_shape=[jax.ShapeDtypeStruct((d, dy.shape[1]), BF16) for dy in dys],
        scratch_shapes=[pltpu.VMEM((d, dy.shape[1]), F32) for dy in dys],
        compiler_params=_params(("arbitrary",)),
    )(n, *dys)


def _row_tile(rows):
    t = min(TILE_ELEM_ROWS, rows)
    while rows % t:
        t //= 2
    return t


def _adamw_math(w, g, m, v):
    m = ADAM_B1 * m + (1.0 - ADAM_B1) * g
    v = ADAM_B2 * v + (1.0 - ADAM_B2) * jnp.square(g)
    m_hat = m / (1.0 - ADAM_B1 ** ADAM_STEP)
    v_hat = v / (1.0 - ADAM_B2 ** ADAM_STEP)
    delta = -ADAM_LR * (m_hat / (jnp.sqrt(v_hat) + ADAM_EPS) + ADAM_WD * w)
    return delta, m, v


def _adamw(w, g, m, v, name):
    rows, cols = w.shape

    def body(w_ref, g_ref, m_ref, v_ref, d_ref, nm_ref, nv_ref):
        delta, nm, nv = _adamw_math(w_ref[...], g_ref[...], m_ref[...], v_ref[...])
        d_ref[...] = delta
        nm_ref[...] = nm
        nv_ref[...] = nv

    if rows % 8 == 0:
        tr = _row_tile(rows)
        grid, spec = (rows // tr,), pl.BlockSpec((tr, cols), lambda i: (i, 0))
    else:
        grid, spec = (cols // 256,), pl.BlockSpec((rows, 256), lambda i: (0, i))
    out = jax.ShapeDtypeStruct(w.shape, F32)
    return pl.pallas_call(
        body, name=name, grid=grid, in_specs=[spec] * 4, out_specs=[spec] * 3,
        out_shape=[out, out, out], compiler_params=_params(("parallel",)),
    )(w, g, m, v)


def _sum_devices(parts):
    def body(p_ref, g_ref):
        g = p_ref[0]
        for k in range(1, N_DEV):
            g = g + p_ref[k]
        g_ref[...] = g

    return pl.pallas_call(
        body, name="sum_devices",
        in_specs=[pl.BlockSpec(memory_space=pltpu.VMEM)],
        out_specs=pl.BlockSpec(memory_space=pltpu.VMEM),
        out_shape=jax.ShapeDtypeStruct(parts.shape[1:], F32),
    )(parts)


def _mesh_position():
    x, y, c = lax.axis_index("x"), lax.axis_index("y"), lax.axis_index("c")
    chips = [(1 - x, y), (x, 1 - y), (1 - x, 1 - y)]
    return x, y, c, chips


ANY = pl.BlockSpec(memory_space=pl.ANY)
HBM = pl.BlockSpec(memory_space=pltpu.HBM)
SEM = pl.BlockSpec(memory_space=pltpu.SEMAPHORE)
SPLIT_COPY_EFFECT = pltpu.SideEffectType.DATAFLOW_SIDE_EFFECTING


def _in_hbm(a):
    return pltpu.with_memory_space_constraint(a, pltpu.HBM)


def _chip_copies(views, srcs, lands, send, recv):
    _, _, c, chips = _mesh_position()
    cps = []
    for a in range(len(srcs)):
        for k, (px, py) in enumerate(chips):
            src, dst = views(a, k, srcs[a], lands[a], c, 2 * px + py)
            sem = a * (N_CHIPS - 1) + k
            cps.append(pltpu.make_async_remote_copy(
                src_ref=src, dst_ref=dst, send_sem=send.at[sem], recv_sem=recv.at[sem],
                device_id=(px, py, c), device_id_type=MESH))
    return cps


def _ici_start(sources, land_shapes, views, after, name):
    n = len(sources)

    def body(*refs):
        srcs, lands = refs[:n], refs[n:2 * n]
        send, recv = refs[2 * n + 1], refs[2 * n + 2]
        token = refs[-1]
        for cp in _chip_copies(views, srcs, lands, send, recv):
            cp.start()
        token[...] = jnp.zeros_like(token)

    lands = [_in_hbm(lax.empty(s.shape, s.dtype)) for s in land_shapes]
    outs = pl.pallas_call(
        body, name=name,
        in_specs=[HBM] * (2 * n) + [ANY],
        out_specs=[SEM, SEM] + [HBM] * (2 * n) + [pl.BlockSpec(memory_space=pltpu.VMEM)],
        out_shape=[pltpu.SemaphoreType.DMA((n * (N_CHIPS - 1),))] * 2
        + [pltpu.HBM(a.shape, a.dtype) for a in sources]
        + [pltpu.HBM(s.shape, s.dtype) for s in land_shapes]
        + [jax.ShapeDtypeStruct((8, 128), F32)],
        input_output_aliases={i: 2 + i for i in range(2 * n)},
        compiler_params=pltpu.CompilerParams(has_side_effects=SPLIT_COPY_EFFECT),
    )(*[_in_hbm(a) for a in sources], *lands, after)
    return outs[0], outs[1], list(outs[2:2 + n]), list(outs[2 + n:2 + 2 * n]), outs[-1]


def _ici_wait(handle, views, after, name):
    send, recv, srcs, lands, _ = handle
    n = len(srcs)

    def body(*refs):
        src_refs, land_refs = refs[:n], refs[n:2 * n]
        for cp in _chip_copies(views, src_refs, land_refs, refs[2 * n], refs[2 * n + 1]):
            cp.wait_send()
            cp.wait_recv()

    outs = pl.pallas_call(
        body, name=name,
        in_specs=[HBM] * (2 * n) + [SEM, SEM, ANY],
        out_specs=[HBM] * (2 * n),
        out_shape=[pltpu.HBM(a.shape, a.dtype) for a in srcs]
        + [pltpu.HBM(a.shape, a.dtype) for a in lands],
        input_output_aliases={i: i for i in range(2 * n)},
        compiler_params=pltpu.CompilerParams(has_side_effects=SPLIT_COPY_EFFECT),
    )(*srcs, *lands, send, recv, after)
    return list(outs[:n]), list(outs[n:])


def _gather_views(split):
    def views(a, k, src, land, c, slot):
        if split[a]:
            half = src.shape[0] // 2
            src = src.at[pl.ds(c * half, half)]
        return src, land.at[k]
    return views


def _gather_whole_views(a, k, src, land, c, slot):
    x, y, _, _ = _mesh_position()
    return src, land.at[2 * x + y]


def _scatter_views(a, k, src, land, c, slot):
    return src.at[slot], land.at[k]


def _gather_land_shapes(shards, split):
    return [jax.ShapeDtypeStruct(
        (N_CHIPS - 1, a.shape[0] // 2 if sp else a.shape[0]) + a.shape[1:], a.dtype)
        for a, sp in zip(shards, split)]


def _gather_finish(shards, lands, split, name):
    n = len(shards)
    ns = sum(split)
    d_index = {a: i for i, a in enumerate(a for a in range(n) if split[a])}

    def body(*refs):
        shard, land, outs = refs[:n], refs[n:2 * n], refs[2 * n:3 * n]
        obuf, fbuf = refs[3 * n:4 * n], refs[4 * n:5 * n]
        dbuf = refs[5 * n:5 * n + ns]
        ld_own, st_own, ld, st_mine, st_sib, send, recv = refs[5 * n + ns:]
        x, y, c, chips = _mesh_position()
        me = 2 * x + y
        own_loads, loads, sends, pending = [], {}, [], []
        for a in range(n):
            cp = pltpu.make_async_copy(shard[a], obuf[a], ld_own.at[a])
            cp.start()
            own_loads.append(cp)
        for a in range(n):
            for k in range(N_CHIPS - 1):
                cp = pltpu.make_async_copy(land[a].at[k], fbuf[a].at[k], ld.at[a, k])
                cp.start()
                loads[a, k] = cp
        for a in range(n):
            own_loads[a].wait()
            cp = pltpu.make_async_copy(obuf[a], outs[a].at[me], st_own.at[a])
            cp.start()
            pending.append(cp)
        for a in range(n):
            rows = shard[a].shape[0]
            for k, (px, py) in enumerate(chips):
                loads[a, k].wait()
                part = pl.ds(c * (rows // 2), rows // 2) if split[a] else pl.ds(0, rows)
                cp = pltpu.make_async_copy(fbuf[a].at[k], outs[a].at[2 * px + py, part],
                                           st_mine.at[a, k])
                cp.start()
                pending.append(cp)
                if split[a]:
                    fw = pltpu.make_async_remote_copy(
                        src_ref=fbuf[a].at[k], dst_ref=dbuf[d_index[a]].at[k],
                        send_sem=send.at[a, k], recv_sem=recv.at[a, k],
                        device_id=(x, y, 1 - c), device_id_type=MESH)
                    fw.start()
                    sends.append((a, k, fw))
        for a, k, fw in sends:
            px, py = chips[k]
            half = shard[a].shape[0] // 2
            fw.wait_recv()
            cp = pltpu.make_async_copy(dbuf[d_index[a]].at[k],
                                       outs[a].at[2 * px + py, pl.ds((1 - c) * half, half)],
                                       st_sib.at[a, k])
            cp.start()
            pending.append(cp)
        for _, _, fw in sends:
            fw.wait_send()
        for cp in pending:
            cp.wait()

    stage = [pltpu.VMEM(a.shape, a.dtype) for a in lands]
    dma = lambda *shape: pltpu.SemaphoreType.DMA(shape)
    return pl.pallas_call(
        body, name=name,
        in_specs=[ANY] * (2 * n), out_specs=[ANY] * n,
        out_shape=[jax.ShapeDtypeStruct((N_CHIPS,) + a.shape, a.dtype) for a in shards],
        scratch_shapes=[pltpu.VMEM(a.shape, a.dtype) for a in shards] + stage
        + [s for s, sp in zip(stage, split) if sp]
        + [dma(n), dma(n), dma(n, 3), dma(n, 3), dma(n, 3), dma(n, 3), dma(n, 3)],
        compiler_params=pltpu.CompilerParams(vmem_limit_bytes=VMEM_LIMIT_BYTES),
    )(*shards, *lands)


def _exchange_siblings(grads, name):
    n = len(grads)

    def body(*refs):
        ins, outs = refs[:n], refs[n:2 * n]
        sbuf, rbuf, mbuf = refs[2 * n:3 * n], refs[3 * n:4 * n], refs[4 * n:5 * n]
        ld_send, ld_mine, st, send, recv = refs[5 * n:]
        x, y, c, _ = _mesh_position()
        loads, mine, sends, stores = [], [], [], []
        for a in range(n):
            half = ins[a].shape[1] // 2
            cp = pltpu.make_async_copy(ins[a].at[:, pl.ds((1 - c) * half, half)], sbuf[a],
                                       ld_send.at[a])
            cp.start()
            loads.append(cp)
        for a in range(n):
            half = ins[a].shape[1] // 2
            cp = pltpu.make_async_copy(ins[a].at[:, pl.ds(c * half, half)], mbuf[a], ld_mine.at[a])
            cp.start()
            mine.append(cp)
        for a in range(n):
            loads[a].wait()
            rc = pltpu.make_async_remote_copy(
                src_ref=sbuf[a], dst_ref=rbuf[a], send_sem=send.at[a], recv_sem=recv.at[a],
                device_id=(x, y, 1 - c), device_id_type=MESH)
            rc.start()
            sends.append(rc)
        for a in range(n):
            sends[a].wait_recv()
            mine[a].wait()
            slots, half, _ = rbuf[a].shape
            rows = min(SUM_CHUNK_ROWS, half)
            per_slot = half // rows

            def add(k, carry, a=a, rows=rows, per_slot=per_slot):
                at = (k // per_slot, pl.ds(pl.multiple_of((k % per_slot) * rows, rows), rows))
                rbuf[a][at] = (rbuf[a][at].astype(F32) + mbuf[a][at].astype(F32)).astype(BF16)
                return carry

            lax.fori_loop(0, slots * per_slot, add, 0)
            cp = pltpu.make_async_copy(rbuf[a], outs[a], st.at[a])
            cp.start()
            stores.append(cp)
        for a in range(n):
            sends[a].wait_send()
            stores[a].wait()

    half_shape = lambda a: (a.shape[0], a.shape[1] // 2, a.shape[2])
    stage = [pltpu.VMEM(half_shape(a), a.dtype) for a in grads]
    return pl.pallas_call(
        body, name=name,
        in_specs=[ANY] * n, out_specs=[ANY] * n,
        out_shape=[jax.ShapeDtypeStruct(half_shape(a), a.dtype) for a in grads],
        scratch_shapes=stage * 3 + [pltpu.SemaphoreType.DMA((n,))] * 5,
        compiler_params=pltpu.CompilerParams(vmem_limit_bytes=VMEM_LIMIT_BYTES),
    )(*grads)


def _sum_and_share(sums, got, name):
    n = len(sums)

    def body(*refs):
        own, others, outs = refs[:n], refs[n:2 * n], refs[2 * n:3 * n]
        obuf, gbuf, sbuf, rbuf = (refs[(3 + k) * n:(4 + k) * n] for k in range(4))
        ld_own, ld_got, st_own, st_sib, send, recv = refs[7 * n:]
        x, y, c, _ = _mesh_position()
        loads, sends, stores = [], [], []
        for a in range(n):
            cps = [pltpu.make_async_copy(own[a].at[2 * x + y], obuf[a], ld_own.at[a]),
                   pltpu.make_async_copy(others[a], gbuf[a], ld_got.at[a])]
            for cp in cps:
                cp.start()
            loads.append(cps)
        for a in range(n):
            for cp in loads[a]:
                cp.wait()
            half = obuf[a].shape[0]
            rows = min(SUM_CHUNK_ROWS, half)

            def add(k, carry, a=a, rows=rows):
                at = pl.ds(pl.multiple_of(k * rows, rows), rows)
                acc = obuf[a][at].astype(F32)
                for j in range(N_CHIPS - 1):
                    acc = acc + gbuf[a][j, at].astype(F32)
                sbuf[a][at] = acc
                return carry

            lax.fori_loop(0, half // rows, add, 0)
            rc = pltpu.make_async_remote_copy(
                src_ref=sbuf[a], dst_ref=rbuf[a], send_sem=send.at[a], recv_sem=recv.at[a],
                device_id=(x, y, 1 - c), device_id_type=MESH)
            rc.start()
            sends.append(rc)
            cp = pltpu.make_async_copy(sbuf[a], outs[a].at[pl.ds(c * half, half)], st_own.at[a])
            cp.start()
            stores.append(cp)
        for a in range(n):
            half = obuf[a].shape[0]
            sends[a].wait_recv()
            cp = pltpu.make_async_copy(rbuf[a], outs[a].at[pl.ds((1 - c) * half, half)], st_sib.at[a])
            cp.start()
            stores.append(cp)
        for cp in sends:
            cp.wait_send()
        for cp in stores:
            cp.wait()

    halves = [a.shape[1:] for a in sums]
    return pl.pallas_call(
        body, name=name,
        in_specs=[ANY] * (2 * n), out_specs=[ANY] * n,
        out_shape=[jax.ShapeDtypeStruct((2 * h[0],) + h[1:], F32) for h in halves],
        scratch_shapes=[pltpu.VMEM(h, BF16) for h in halves]
        + [pltpu.VMEM(g.shape, BF16) for g in got]
        + [pltpu.VMEM(h, F32) for h in halves] * 2
        + [pltpu.SemaphoreType.DMA((n,))] * 6,
        compiler_params=pltpu.CompilerParams(vmem_limit_bytes=VMEM_LIMIT_BYTES),
    )(*sums, *got)


def _gather_small(part):
    def body(in_ref, out_ref, send, recv, local):
        x, y, c, _ = _mesh_position()
        me = 4 * x + 2 * y + c
        cps = [pltpu.make_async_copy(in_ref, out_ref.at[me], local)]
        k = 0
        for fx in range(2):
            for fy in range(2):
                for fc in range(2):
                    if fx or fy or fc:
                        cps.append(pltpu.make_async_remote_copy(
                            src_ref=in_ref, dst_ref=out_ref.at[me], send_sem=send.at[k],
                            recv_sem=recv.at[k], device_id=(x ^ fx, y ^ fy, c ^ fc),
                            device_id_type=MESH))
                        k += 1
        for cp in cps:
            cp.start()
        for cp in cps:
            cp.wait()

    return pl.pallas_call(
        body, name="gather_small",
        in_specs=[pl.BlockSpec(memory_space=pltpu.VMEM)],
        out_specs=pl.BlockSpec(memory_space=pltpu.VMEM),
        out_shape=jax.ShapeDtypeStruct((N_DEV,) + part.shape, part.dtype),
        scratch_shapes=[pltpu.SemaphoreType.DMA((N_DEV - 1,)), pltpu.SemaphoreType.DMA((N_DEV - 1,)),
                        pltpu.SemaphoreType.DMA],
    )(part)


def _scatter_start(grads, tag):
    sums = _exchange_siblings(grads, "exchange_siblings_" + tag)
    lands = [jax.ShapeDtypeStruct((N_CHIPS - 1,) + s.shape[1:], s.dtype) for s in sums]
    return _ici_start(sums, lands, _scatter_views, grads[0], "scatter_start_" + tag)


def _scatter_finish(handle, after, tag):
    sums, got = _ici_wait(handle, _scatter_views, after, "scatter_wait_" + tag)
    return _sum_and_share(sums, got, "sum_and_share_" + tag)


def _pad_rows(a, rows):
    return jnp.pad(a, ((0, rows - a.shape[0]), (0, 0)))


def kernel(x, norm_mix_0, w_in_0, b_f_0, conv_w_0, w_out_0, norm_ffn_0, w_up_0, w_down_0, norm_mix_1, pool_w_1, pool_scale_1, norm_ffn_1, w_up_1, w_down_1, final_norm, loss_target, m_norm_mix_0, m_w_in_0, m_b_f_0, m_conv_w_0, m_w_out_0, m_norm_ffn_0, m_w_up_0, m_w_down_0, m_norm_mix_1, m_pool_w_1, m_pool_scale_1, m_norm_ffn_1, m_w_up_1, m_w_down_1, m_final_norm, v_norm_mix_0, v_w_in_0, v_b_f_0, v_conv_w_0, v_w_out_0, v_norm_ffn_0, v_w_up_0, v_w_down_0, v_norm_mix_1, v_pool_w_1, v_pool_scale_1, v_norm_ffn_1, v_w_up_1, v_w_down_1, v_final_norm):
    d = x.shape[-1]
    a = N_HEADS * HEAD_DIM
    c_conv = conv_w_0.shape[1] * N_CHIPS
    xs = x[0]
    target = loss_target[0]
    row = lambda vec: vec.reshape(1, -1)

    big = [w_in_0, w_out_0, w_up_0, w_down_0, pool_w_1, w_up_1, w_down_1]
    first = [w_in_0.astype(BF16)]
    first_split = [True]
    start_a = _ici_start(first, _gather_land_shapes(first, first_split),
                         _gather_views(first_split), b_f_0, "gather_start_a")
    zero = start_a[-1][0, 0]
    rest = [(w + zero).astype(BF16)
            for w in (w_out_0, w_up_0, w_down_0, pool_w_1, w_up_1, w_down_1)]
    rest = rest + [conv_w_0]
    start_b = _ici_start(rest, [jax.ShapeDtypeStruct((N_CHIPS,) + w.shape, w.dtype) for w in rest],
                         _gather_whole_views, start_a[-1], "gather_start_b")
    n0 = _rms_pre(start_b[-1], xs, row(norm_mix_0))
    first, land_a = _ici_wait(start_a, _gather_views(first_split), n0, "gather_wait_a")
    (g_in,) = _gather_finish(first, land_a, first_split, "gather_finish_a")
    w_in = g_in.transpose(1, 0, 2).reshape(d, -1)
    w_qkv = w_in[:, :3 * a]
    w_f = jnp.pad(w_in[:, 3 * a:3 * a + N_HEADS], ((0, 0), (0, 128 - N_HEADS)))
    w_bcx = w_in[:, 3 * a + N_HEADS:]
    bf = jnp.pad(b_f_0, (0, 128 - N_HEADS)).reshape(1, 128)

    qkv, fl, bcx = _in_proj(n0, w_qkv, w_f, w_bcx)
    qa, ka = _gate_prep(fl, bf, qkv)
    o, lse = _attn_fwd(qa, ka, qkv)
    rest, land_b = _ici_wait(start_b, _gather_whole_views, o, "gather_wait_b")
    own_slot = 2 * lax.axis_index("x") + lax.axis_index("y")
    g_out, g_up0, g_down0, g_pool, g_up1, g_down1, g_conv = [
        lax.dynamic_update_index_in_dim(land, shard, own_slot, 0)
        for land, shard in zip(land_b, rest)]
    w_out = g_out.reshape(-1, d)
    conv_w = _pad_rows(g_conv.transpose(1, 0, 2).reshape(conv_w_0.shape[0], c_conv), 8)
    h1 = _conv_out(o, bcx, conv_w, w_out, xs)
    w_down0 = g_down0.reshape(-1, d)
    w_down1 = g_down1.reshape(-1, d)
    pool_w = g_pool.transpose(1, 0, 2, 3).reshape(pool_w_1.shape[0], -1, pool_w_1.shape[2])
    h2, a0, nf0 = _mlp_fwd(h1, row(norm_ffn_0), g_up0, w_down0, "mlp_fwd_0")
    h3 = _pool_fwd(h2, row(norm_mix_1), pool_w, row(pool_scale_1))
    dh4, a1, nf1, loss_part, d_final = _mlp_fwd(h3, row(norm_ffn_1), g_up1, w_down1, "mlp_fwd_1",
                                                head=(row(final_norm), target))

    slot_cols = g_up0.shape[2]
    pool_cols = pool_w.shape[2]
    da1, dz1, dh3, d_nffn1 = _mlp_bwd_x(dh4, a1, g_up1, w_down1, h3, row(norm_ffn_1), "mlp_bwd_x_1")
    dw_up1, dw_down1 = _mlp_bwd_w(nf1, da1, a1, dz1, slot_cols, "mlp_bwd_w_1")
    scatter_1 = _scatter_start([dw_up1, dw_down1.reshape(N_CHIPS, -1, d)], "mlp1")
    dh2, dw_pool, d_pscale, d_nmix1 = _pool_bwd(scatter_1[-1], dh3, h2, row(norm_mix_1), pool_w,
                                                row(pool_scale_1))
    da0, dz0, dh1, d_nffn0 = _mlp_bwd_x(dh2, a0, g_up0, w_down0, h1, row(norm_ffn_0), "mlp_bwd_x_0")
    dw_up0, dw_down0 = _mlp_bwd_w(nf0, da0, a0, dz0, slot_cols, "mlp_bwd_w_0")
    dw_pool = (dw_pool.reshape(pool_w.shape[0], N_CHIPS, -1, pool_cols).transpose(1, 0, 2, 3)
               .reshape(N_CHIPS, -1, pool_cols))
    scatter_0 = _scatter_start([dw_up0, dw_down0.reshape(N_CHIPS, -1, d), dw_pool], "mlp0")
    do, delta, dbcx, dw_out, d_conv = _conv_out_bwd(scatter_0[-1], dh1, w_out, o, bcx, conv_w)
    dqa, dka, dv = _attn_bwd(qa, ka, qkv, do, lse, delta)
    dqkv, dfl, d_bf = _gate_bwd(dqa, dka, dv, fl, bf)
    dw_qkv, dw_f, dw_bcx = _wgrad_in(n0, [dqkv, dfl, dbcx])
    dw_in = jnp.concatenate([dw_qkv, dw_f[:, :N_HEADS], dw_bcx], axis=1)
    scatter_m = _scatter_start([dw_in.reshape(d, N_CHIPS, -1).transpose(1, 0, 2),
                                dw_out.reshape(N_CHIPS, -1, d)], "mixer")
    grad_x, d_nmix0 = _in_proj_bwd(scatter_m[-1], dqkv, dfl, dbcx, w_qkv, w_f, w_bcx, xs,
                                   row(norm_mix_0), dh1)

    r_up1, r_down1 = _scatter_finish(scatter_1, grad_x, "mlp1")
    r_up0, r_down0, r_pool = _scatter_finish(scatter_0, grad_x, "mlp0")
    r_in, r_out = _scatter_finish(scatter_m, grad_x, "mixer")
    reduced = [r_in, r_out, r_up0, r_down0, r_pool, r_up1, r_down1]
    moments = [(m_w_in_0, v_w_in_0), (m_w_out_0, v_w_out_0), (m_w_up_0, v_w_up_0),
               (m_w_down_0, v_w_down_0), (m_pool_w_1, v_pool_w_1), (m_w_up_1, v_w_up_1),
               (m_w_down_1, v_w_down_1)]
    big_out = []
    for k, (w, g, (m, v)) in enumerate(zip(big, reduced, moments)):
        if w.shape[-1] % 128:
            view = lambda t: t.reshape(-1, t.shape[-1]).T
            back = lambda t: t.T.reshape(w.shape)
        else:
            view = lambda t: t.reshape(-1, t.shape[-1])
            back = lambda t: t.reshape(w.shape)
        g_view = view(g)
        delta_w, new_m, new_v = _adamw(view(w), g_view, view(m), view(v), "adamw_%d" % k)
        big_out.append((back(g_view), back(delta_w), back(new_m), back(new_v)))

    tail = jnp.concatenate([d_conv[0:3].reshape(-1)[d:], d_bf[0, :N_HEADS], loss_part[0, :1]])
    small_part = jnp.concatenate(
        [d_nmix0, d_nffn0, d_nmix1, d_pscale, d_nffn1, d_final,
         d_conv[0:3].reshape(1, -1)[:, :d],
         jnp.pad(tail, (0, d - tail.shape[0])).reshape(1, d)], axis=0)
    parts = _gather_small(small_part)

    chip = 2 * lax.axis_index("x") + lax.axis_index("y")
    cw_cols = conv_w_0.shape[1]

    def conv_block(full):
        mine = lax.dynamic_slice_in_dim(full, chip * cw_cols, cw_cols, axis=1)
        return jnp.pad(mine.reshape(-1), (0, d - mine.size))

    def small_rows(vals, cw, bfv):
        return jnp.stack(list(vals) + [cw, jnp.pad(bfv, (0, d - N_HEADS))])

    smalls_w = [norm_mix_0, norm_ffn_0, norm_mix_1, pool_scale_1, norm_ffn_1, final_norm]
    smalls_m = [m_norm_mix_0, m_norm_ffn_0, m_norm_mix_1, m_pool_scale_1, m_norm_ffn_1, m_final_norm]
    smalls_v = [v_norm_mix_0, v_norm_ffn_0, v_norm_mix_1, v_pool_scale_1, v_norm_ffn_1, v_final_norm]
    pad_cw = lambda t: jnp.pad(t.reshape(-1), (0, d - t.size))
    w_rows = small_rows(smalls_w, pad_cw(conv_w_0), b_f_0)
    m_rows = small_rows(smalls_m, pad_cw(m_conv_w_0), m_b_f_0)
    v_rows = small_rows(smalls_v, pad_cw(v_conv_w_0), v_b_f_0)

    g_sum = _sum_devices(parts)
    conv_full = jnp.concatenate([g_sum[6], g_sum[7, :3 * c_conv - d]]).reshape(3, c_conv)
    bf_grad = g_sum[7, 3 * c_conv - d:3 * c_conv - d + N_HEADS]
    loss = g_sum[7, 3 * c_conv - d + N_HEADS]
    g_rows = jnp.concatenate(
        [g_sum[0:6], conv_block(conv_full).reshape(1, d),
         jnp.pad(bf_grad, (0, d - N_HEADS)).reshape(1, d)], axis=0)
    d_rows, nm_rows, nv_rows = _adamw(w_rows, g_rows, m_rows, v_rows, "adamw_small")

    def unpack(rows):
        cw = rows[6, :conv_w_0.size].reshape(conv_w_0.shape)
        return [rows[0], rows[1], rows[2], rows[3], rows[4], rows[5], cw, rows[7, :N_HEADS]]

    def assemble(kind):
        sm = unpack([g_rows, d_rows, nm_rows, nv_rows][kind])
        lg = [t[kind] for t in big_out]
        return [sm[0], lg[0], sm[7], sm[6], lg[1], sm[1], lg[2], lg[3],
                sm[2], lg[4], sm[3], sm[4], lg[5], lg[6], sm[5]]

    return (loss, grad_x[None], *assemble(0), *assemble(1), *assemble(2), *assemble(3))
```

```python
import functools

import jax
import jax.numpy as jnp
from jax import lax
from jax.experimental import pallas as pl
from jax.experimental.pallas import tpu as pltpu

F32 = jnp.float32
BF16 = jnp.bfloat16

RMS_EPS = 1e-6
HEAD_DIM = 64
N_HEADS = 8
ATTN_SCALE = HEAD_DIM ** -0.5
LOG2_E = 1.4426950408889634
POOL_WINDOWS = (2, 4, 8, 16)
POOL_HALO = 16
CONV_HALO = 8
NEG_BIG = -1e30

ADAM_LR = 0.001
ADAM_B1 = 0.9
ADAM_B2 = 0.999
ADAM_EPS = 1e-08
ADAM_WD = 0.01
ADAM_STEP = 10

N_CHIPS = 4
N_DEV = 8
MESH = pl.DeviceIdType.MESH

VMEM_LIMIT_BYTES = 56 * 1024 * 1024

TILE_ROWS = 512
TILE_ATTN = 512
TILE_MLP_ROWS = 1024
TILE_MLP_FF = 1024
TILE_MLP_BWD_FF = 512
TILE_WGRAD_K = 1024
TILE_WGRAD_N = 1024
TILE_ELEM_ROWS = 256
SUM_CHUNK_ROWS = 128

LANE_CQ = 64
LANE_ONE = 67


def _params(semantics):
    return pltpu.CompilerParams(dimension_semantics=semantics,
                                vmem_limit_bytes=VMEM_LIMIT_BYTES)


def _nn(a, b):
    return lax.dot_general(a, b, (((1,), (0,)), ((), ())), preferred_element_type=F32)


def _nt(a, b):
    return lax.dot_general(a, b, (((1,), (1,)), ((), ())), preferred_element_type=F32)


def _tn(a, b):
    return lax.dot_general(a, b, (((0,), (0,)), ((), ())), preferred_element_type=F32)


def _split3(v):
    hi = v.astype(BF16)
    r1 = v - hi.astype(F32)
    mid = r1.astype(BF16)
    lo = (r1 - mid.astype(F32)).astype(BF16)
    return hi, mid, lo


def _exact_nn(sel, v):
    hi, mid, lo = _split3(v)
    return _nn(sel, hi) + _nn(sel, mid) + _nn(sel, lo)


def _exact_nt(sel, v):
    hi, mid, lo = _split3(v)
    return _nt(sel, hi) + _nt(sel, mid) + _nt(sel, lo)


def _rms_fwd(x, g):
    r = lax.rsqrt(jnp.mean(x * x, axis=-1, keepdims=True) + RMS_EPS)
    return x * r * g, r


def _rms_bwd(dn, x, g):
    r = lax.rsqrt(jnp.mean(x * x, axis=-1, keepdims=True) + RMS_EPS)
    xh = x * r
    gy = dn * g
    dx = r * (gy - xh * jnp.mean(gy * xh, axis=-1, keepdims=True))
    return dx, jnp.sum(dn * xh, axis=0, keepdims=True)


def _lane(shape):
    return lax.broadcasted_iota(jnp.int32, shape, len(shape) - 1)


def _row(shape):
    return lax.broadcasted_iota(jnp.int32, shape, len(shape) - 2)


def _full(a):
    nd = a.ndim
    return pl.BlockSpec(a.shape, lambda *_: (0,) * nd)


def _rms_pre(after, x, g):
    s, d = x.shape
    tm = min(TILE_ROWS, s)

    def body(after_ref, x_ref, g_ref, n_ref):
        n, _ = _rms_fwd(x_ref[...], g_ref[...])
        n_ref[...] = n.astype(BF16)

    rows = pl.BlockSpec((tm, d), lambda i: (i, 0))
    return pl.pallas_call(
        body, name="rms_pre", grid=(s // tm,),
        in_specs=[ANY, rows, _full(g)], out_specs=rows,
        out_shape=jax.ShapeDtypeStruct((s, d), BF16),
        compiler_params=_params(("parallel",)),
    )(after, x, g)


def _in_proj(n, w_qkv, w_f, w_bcx):
    s, d = n.shape
    tm = min(TILE_ROWS, s)

    def body(n_ref, wq_ref, wf_ref, wb_ref, qkv_ref, fl_ref, bcx_ref):
        nb = n_ref[...]
        qkv_ref[...] = _nn(nb, wq_ref[...]).astype(BF16)
        fl_ref[...] = _nn(nb, wf_ref[...])
        bcx_ref[...] = _nn(nb, wb_ref[...])

    rows = lambda c: pl.BlockSpec((tm, c), lambda i: (i, 0))
    return pl.pallas_call(
        body, name="in_proj", grid=(s // tm,),
        in_specs=[rows(d), _full(w_qkv), _full(w_f), _full(w_bcx)],
        out_specs=[rows(w_qkv.shape[1]), rows(w_f.shape[1]), rows(w_bcx.shape[1])],
        out_shape=[jax.ShapeDtypeStruct((s, w_qkv.shape[1]), BF16),
                   jax.ShapeDtypeStruct((s, w_f.shape[1]), F32),
                   jax.ShapeDtypeStruct((s, w_bcx.shape[1]), F32)],
        compiler_params=_params(("parallel",)),
    )(n, w_qkv, w_f, w_bcx)


def _gate_prep(fl, bf, qkv):
    s = fl.shape[0]
    a = N_HEADS * HEAD_DIM
    tm = min(TILE_ROWS, s)

    def body(fl_ref, bf_ref, q_ref, k_ref, qa_ref, ka_ref, carry_ref):
        i = pl.program_id(0)

        @pl.when(i == 0)
        def _():
            carry_ref[...] = jnp.zeros_like(carry_ref)

        z = fl_ref[...] + bf_ref[...]
        logf = jnp.minimum(z, 0.0) - jnp.log(1.0 + jnp.exp(-jnp.abs(z)))
        lower = (_lane((tm, tm)) <= _row((tm, tm))).astype(BF16)
        cum = _exact_nn(lower, logf) + carry_ref[0:1, :]
        carry_ref[0:1, :] = cum[tm - 1:tm, :]

        lane = _lane((tm, 128))
        for h in range(N_HEADS):
            cb = LOG2_E * jnp.sum(jnp.where(lane == h, cum, 0.0), axis=1, keepdims=True)
            hi, mid, lo = (p.astype(F32) for p in _split3(cb))
            pair = slice((h // 2) * 128, (h // 2 + 1) * 128)
            qp = q_ref[:, pair].astype(F32)
            kp = k_ref[:, pair].astype(F32)
            if h % 2:
                qp = pltpu.roll(qp, HEAD_DIM, axis=1)
                kp = pltpu.roll(kp, HEAD_DIM, axis=1)
            q_bias = jnp.where(lane == LANE_CQ, hi,
                               jnp.where(lane == LANE_CQ + 1, mid,
                                         jnp.where(lane == LANE_CQ + 2, lo,
                                                   jnp.where(lane < LANE_ONE + 3, 1.0, 0.0))))
            k_bias = jnp.where(lane < LANE_ONE, 1.0,
                               jnp.where(lane == LANE_ONE, -hi,
                                         jnp.where(lane == LANE_ONE + 1, -mid,
                                                   jnp.where(lane == LANE_ONE + 2, -lo, 0.0))))
            qa_ref[h] = jnp.where(lane < HEAD_DIM, qp * (ATTN_SCALE * LOG2_E), q_bias).astype(BF16)
            ka_ref[h] = jnp.where(lane < HEAD_DIM, kp, k_bias).astype(BF16)

    aug = jax.ShapeDtypeStruct((N_HEADS, s, 128), BF16)
    aug_spec = pl.BlockSpec((N_HEADS, tm, 128), lambda i: (0, i, 0))
    return pl.pallas_call(
        body, name="gate_prep", grid=(s // tm,),
        in_specs=[pl.BlockSpec((tm, 128), lambda i: (i, 0)), _full(bf),
                  pl.BlockSpec((tm, a), lambda i: (i, 0)),
                  pl.BlockSpec((tm, a), lambda i: (i, 1))],
        out_specs=[aug_spec, aug_spec],
        out_shape=[aug, aug],
        scratch_shapes=[pltpu.VMEM((8, 128), F32)],
        compiler_params=_params(("arbitrary",)),
    )(fl, bf, qkv, qkv)


def _attn_fwd(qa, ka, qkv):
    s = qa.shape[1]
    a = N_HEADS * HEAD_DIM
    t = min(TILE_ATTN, s)
    n_pairs = N_HEADS // 2
    v_block0 = 2 * a // 128

    ones_lane = (HEAD_DIM, 0)

    def body(qa_ref, ka_ref, v_ref, o_ref, lse_ref, m_ref, acc_ref, s_even, s_odd):
        i = pl.program_id(1)
        m_ref[...] = jnp.full_like(m_ref, NEG_BIG)
        acc_ref[...] = jnp.zeros_like(acc_ref)
        upper_rows = _row((128, t)) < HEAD_DIM

        def keys(j):
            return pl.ds(pl.multiple_of(j * t, t), t)

        def scores_into(buf, j):
            for e in range(2):
                buf[e] = _nt(ka_ref[e, keys(j), :], qa_ref[e])

        def consume(buf, j, masked):
            vf = v_ref[keys(j), :].astype(F32)
            lane = _lane((t, 128))
            own = [lane < HEAD_DIM, lane >= HEAD_DIM]
            for e in range(2):
                v_head = jnp.where(own[e], vf, jnp.where(lane == ones_lane[e], 1.0, 0.0)).astype(BF16)
                sc = buf[e]
                if masked:
                    sc = jnp.where(_row((t, t)) <= _lane((t, t)), sc, NEG_BIG)
                m_prev = m_ref[e]
                m_new = jnp.maximum(m_prev, jnp.max(sc, axis=0, keepdims=True))
                p = jnp.exp2(sc - m_new).astype(BF16)
                acc_ref[e] = acc_ref[e] * jnp.exp2(m_prev - m_new) + _tn(v_head, p)
                m_ref[e] = m_new

        scores_into(s_even, 0)

        def two_tiles(p, carry):
            j = 2 * p
            scores_into(s_odd, j + 1)
            consume(s_even, j, False)
            scores_into(s_even, j + 2)
            consume(s_odd, j + 1, False)
            return carry

        lax.fori_loop(0, i // 2, two_tiles, 0)

        @pl.when(i % 2 == 0)
        def _():
            consume(s_even, i, True)

        @pl.when(i % 2 == 1)
        def _():
            scores_into(s_odd, i)
            consume(s_even, i - 1, False)
            consume(s_odd, i, True)

        denom = [acc_ref[e, ones_lane[e]:ones_lane[e] + 1, :] for e in range(2)]
        out_t = jnp.where(upper_rows, acc_ref[0] / denom[0], acc_ref[1] / denom[1])
        o_ref[...] = out_t.T.astype(BF16)
        lse = [m_ref[e] + LOG2_E * jnp.log(denom[e]) for e in range(2)]
        lse_ref[...] = jnp.where(_row((8, t)) == 0, lse[0], lse[1])

    return pl.pallas_call(
        body, name="attn_fwd", grid=(n_pairs, s // t),
        in_specs=[pl.BlockSpec((2, t, 128), lambda g, i: (g, i, 0)),
                  pl.BlockSpec((2, s, 128), lambda g, i: (g, 0, 0)),
                  pl.BlockSpec((s, 128), lambda g, i: (0, v_block0 + g))],
        out_specs=[pl.BlockSpec((t, 128), lambda g, i: (i, g)),
                   pl.BlockSpec((None, 8, t), lambda g, i: (g, 0, i))],
        out_shape=[jax.ShapeDtypeStruct((s, a), BF16),
                   jax.ShapeDtypeStruct((n_pairs, 8, s), F32)],
        scratch_shapes=[pltpu.VMEM((2, 1, t), F32), pltpu.VMEM((2, 128, t), F32),
                        pltpu.VMEM((2, t, t), F32), pltpu.VMEM((2, t, t), F32)],
        compiler_params=_params(("parallel", "arbitrary")),
    )(qa, ka, qkv)


def _conv_out(o, bcx, cw, w_out, x):
    s, d = x.shape
    c = o.shape[1]
    tm = min(TILE_ROWS, s)

    def body(o_ref, b_ref, c_ref, xin_ref, cw_ref, w_ref, x_ref, h_ref, ubuf):
        i = pl.program_id(0)

        @pl.when(i == 0)
        def _():
            ubuf[0:CONV_HALO, :] = jnp.zeros((CONV_HALO, c), F32)

        u = c_ref[...] * xin_ref[...]
        ubuf[CONV_HALO:CONV_HALO + tm, :] = u
        u1 = ubuf[CONV_HALO - 1:CONV_HALO - 1 + tm, :]
        u2 = ubuf[CONV_HALO - 2:CONV_HALO - 2 + tm, :]
        cv = (cw_ref[0:1, :] * u2 + cw_ref[1:2, :] * u1) + cw_ref[2:3, :] * u
        y = (b_ref[...] * cv).astype(BF16)
        mix = _nn(o_ref[...], w_ref[0:c, :]) + _nn(y, w_ref[c:2 * c, :])
        h_ref[...] = x_ref[...] + mix
        ubuf[0:CONV_HALO, :] = u[tm - CONV_HALO:tm, :]

    col = lambda k: pl.BlockSpec((tm, c), lambda i: (i, k))
    return pl.pallas_call(
        body, name="conv_out", grid=(s // tm,),
        in_specs=[col(0), col(0), col(1), col(2), _full(cw), _full(w_out),
                  pl.BlockSpec((tm, d), lambda i: (i, 0))],
        out_specs=pl.BlockSpec((tm, d), lambda i: (i, 0)),
        out_shape=jax.ShapeDtypeStruct((s, d), F32),
        scratch_shapes=[pltpu.VMEM((tm + CONV_HALO, c), F32)],
        compiler_params=_params(("arbitrary",)),
    )(o, bcx, bcx, bcx, cw, w_out, x)


def _mlp_fwd(h, g, w_up, w_down, name, head=None):
    s, d = h.shape
    ff = w_down.shape[0]
    slot_cols = w_up.shape[2]
    tm = min(TILE_MLP_ROWS, s)
    tf = min(TILE_MLP_FF if head is None else TILE_MLP_FF // 2, slot_cols)
    per_slot = slot_cols // tf
    nf = ff // tf
    n_head = 0 if head is None else 2

    def body(*refs):
        h_ref, g_ref, wu_ref, wd_ref = refs[:4]
        out_ref, a_ref, n_ref = refs[4 + n_head:7 + n_head]
        nb_ref, acc_ref = refs[-2:]
        i = pl.program_id(0)
        f = pl.program_id(1)

        @pl.when(f == 0)
        def _():
            n, _ = _rms_fwd(h_ref[...], g_ref[...])
            nb = n.astype(BF16)
            nb_ref[...] = nb
            n_ref[...] = nb
            acc_ref[...] = jnp.zeros_like(acc_ref)

        pre = _nn(nb_ref[...], wu_ref[...])
        a_ref[...] = pre.astype(BF16)
        r = jnp.square(jnp.maximum(pre, 0.0)).astype(BF16)
        acc_ref[...] += _nn(r, wd_ref[...])

        @pl.when(f == nf - 1)
        def _():
            out = h_ref[...] + acc_ref[...]
            if head is None:
                out_ref[...] = out
            else:
                gf_ref, t_ref = refs[4:6]
                loss_ref, dg_ref = refs[7 + n_head:9 + n_head]
                y, _ = _rms_fwd(out, gf_ref[...])
                err = y - t_ref[...]
                part = 0.5 * jnp.sum(jnp.mean(err * err, axis=-1, keepdims=True), axis=0,
                                     keepdims=True)
                dx, dg = _rms_bwd(err / d, out, gf_ref[...])
                out_ref[...] = dx
                part = jnp.broadcast_to(part, loss_ref.shape)

                @pl.when(i == 0)
                def _():
                    loss_ref[...] = part
                    dg_ref[...] = dg

                @pl.when(i > 0)
                def _():
                    loss_ref[...] += part
                    dg_ref[...] += dg

    rows = pl.BlockSpec((tm, d), lambda i, f: (i, 0))
    in_specs = [rows, _full(g),
                pl.BlockSpec((None, d, tf), lambda i, f: (f // per_slot, 0, f % per_slot)),
                pl.BlockSpec((tf, d), lambda i, f: (f, 0))]
    out_specs = [rows, pl.BlockSpec((tm, tf), lambda i, f: (i, f)), rows]
    out_shape = [jax.ShapeDtypeStruct((s, d), F32), jax.ShapeDtypeStruct((s, ff), BF16),
                 jax.ShapeDtypeStruct((s, d), BF16)]
    args = [h, g, w_up, w_down]
    if head is not None:
        in_specs += [_full(head[0]), rows]
        args += list(head)
        out_specs += [pl.BlockSpec((1, 128), lambda i, f: (0, 0)),
                      pl.BlockSpec((1, d), lambda i, f: (0, 0))]
        out_shape += [jax.ShapeDtypeStruct((1, 128), F32), jax.ShapeDtypeStruct((1, d), F32)]
    return pl.pallas_call(
        body, name=name, grid=(s // tm, nf),
        in_specs=in_specs, out_specs=out_specs, out_shape=out_shape,
        scratch_shapes=[pltpu.VMEM((tm, d), BF16), pltpu.VMEM((tm, d), F32)],
        compiler_params=_params(("parallel" if head is None else "arbitrary", "arbitrary")),
    )(*args)


def _window_sum_down(e, window):
    step = 1
    while step < window:
        e = e + pltpu.roll(e, step, axis=0)
        step *= 2
    return e


def _window_sum_up(e, window):
    n = e.shape[0]
    step = 1
    while step < window:
        e = e + pltpu.roll(e, n - step, axis=0)
        step *= 2
    return e


def _pool_counts(first_row, tm, window):
    t = first_row + _row((tm, 1))
    return jnp.minimum(t + 1, window).astype(F32)


def _pool_fwd(h, g, pw, ps):
    s, d = h.shape
    cg = d // len(POOL_WINDOWS)
    tm = min(TILE_ROWS, s)

    def body(h_ref, g_ref, pw_ref, ps_ref, out_ref, nbuf):
        i = pl.program_id(0)

        @pl.when(i == 0)
        def _():
            nbuf[0:POOL_HALO, :] = jnp.zeros((POOL_HALO, d), F32)

        n, _ = _rms_fwd(h_ref[...], g_ref[...])
        nbuf[POOL_HALO:POOL_HALO + tm, :] = n
        for k, window in enumerate(POOL_WINDOWS):
            cols = slice(k * cg, (k + 1) * cg)
            sums = _window_sum_down(nbuf[:, cols], window)[POOL_HALO:, :]
            pooled = sums / _pool_counts(i * tm, tm, window) - n[:, cols]
            y = _nn(pooled.astype(BF16), pw_ref[k]) * ps_ref[:, cols]
            out_ref[:, cols] = h_ref[:, cols] + y
        nbuf[0:POOL_HALO, :] = n[tm - POOL_HALO:tm, :]

    return pl.pallas_call(
        body, name="pool_fwd", grid=(s // tm,),
        in_specs=[pl.BlockSpec((tm, d), lambda i: (i, 0)), _full(g), _full(pw), _full(ps)],
        out_specs=pl.BlockSpec((tm, d), lambda i: (i, 0)),
        out_shape=jax.ShapeDtypeStruct((s, d), F32),
        scratch_shapes=[pltpu.VMEM((tm + POOL_HALO, d), F32)],
        compiler_params=_params(("arbitrary",)),
    )(h, g, pw, ps)


def _mlp_bwd_x(dz, a, w_up, w_down, h_in, g, name):
    s, d = dz.shape
    ff = w_down.shape[0]
    slot_cols = w_up.shape[2]
    tm = min(TILE_MLP_ROWS, s)
    tf = min(TILE_MLP_BWD_FF, slot_cols)
    per_slot = slot_cols // tf
    nf = ff // tf

    def body(dz_ref, a_ref, wu_ref, wd_ref, h_ref, g_ref, da_ref, dzb_ref, dh_ref, dg_ref,
             dzs_ref, acc_ref):
        i = pl.program_id(0)
        f = pl.program_id(1)

        @pl.when(f == 0)
        def _():
            dzb = dz_ref[...].astype(BF16)
            dzs_ref[...] = dzb
            dzb_ref[...] = dzb
            acc_ref[...] = jnp.zeros_like(acc_ref)

        dr = _nt(dzs_ref[...], wd_ref[...])
        da = (dr * (2.0 * jnp.maximum(a_ref[...].astype(F32), 0.0))).astype(BF16)
        da_ref[...] = da
        acc_ref[...] += _nt(da, wu_ref[...])

        @pl.when(f == nf - 1)
        def _():
            dx, dg = _rms_bwd(acc_ref[...], h_ref[...], g_ref[...])
            dh_ref[...] = dz_ref[...] + dx

            @pl.when(i == 0)
            def _():
                dg_ref[...] = dg

            @pl.when(i > 0)
            def _():
                dg_ref[...] += dg

    return pl.pallas_call(
        body, name=name, grid=(s // tm, nf),
        in_specs=[pl.BlockSpec((tm, d), lambda i, f: (i, 0)),
                  pl.BlockSpec((tm, tf), lambda i, f: (i, f)),
                  pl.BlockSpec((None, d, tf), lambda i, f: (f // per_slot, 0, f % per_slot)),
                  pl.BlockSpec((tf, d), lambda i, f: (f, 0)),
                  pl.BlockSpec((tm, d), lambda i, f: (i, 0)), _full(g)],
        out_specs=[pl.BlockSpec((tm, tf), lambda i, f: (i, f)),
                   pl.BlockSpec((tm, d), lambda i, f: (i, 0)),
                   pl.BlockSpec((tm, d), lambda i, f: (i, 0)),
                   pl.BlockSpec((1, d), lambda i, f: (0, 0))],
        out_shape=[jax.ShapeDtypeStruct((s, ff), BF16),
                   jax.ShapeDtypeStruct((s, d), BF16),
                   jax.ShapeDtypeStruct((s, d), F32),
                   jax.ShapeDtypeStruct((1, d), F32)],
        scratch_shapes=[pltpu.VMEM((tm, d), BF16), pltpu.VMEM((tm, d), F32)],
        compiler_params=_params(("arbitrary", "arbitrary")),
    )(dz, a, w_up, w_down, h_in, g)


def _mlp_bwd_w(n, da, a, dzb, slot_cols, name):
    s, d = n.shape
    ff = a.shape[1]
    tn = min(TILE_WGRAD_N, slot_cols)
    tk = min(TILE_WGRAD_K, s)
    per_slot = slot_cols // tn
    nk = s // tk

    def body(n_ref, da_ref, a_ref, dz_ref, du_ref, dd_ref, accu_ref, accd_ref):
        k = pl.program_id(1)

        @pl.when(k == 0)
        def _():
            accu_ref[...] = jnp.zeros_like(accu_ref)
            accd_ref[...] = jnp.zeros_like(accd_ref)

        accu_ref[...] += _tn(n_ref[...], da_ref[...])
        r = jnp.square(jnp.maximum(a_ref[...].astype(F32), 0.0)).astype(BF16)
        accd_ref[...] += _tn(r, dz_ref[...])

        @pl.when(k == nk - 1)
        def _():
            du_ref[...] = accu_ref[...].astype(BF16)
            dd_ref[...] = accd_ref[...].astype(BF16)

    return pl.pallas_call(
        body, name=name, grid=(ff // tn, nk),
        in_specs=[pl.BlockSpec((tk, d), lambda f, k: (k, 0)),
                  pl.BlockSpec((tk, tn), lambda f, k: (k, f)),
                  pl.BlockSpec((tk, tn), lambda f, k: (k, f)),
                  pl.BlockSpec((tk, d), lambda f, k: (k, 0))],
        out_specs=[pl.BlockSpec((None, d, tn), lambda f, k: (f // per_slot, 0, f % per_slot)),
                   pl.BlockSpec((tn, d), lambda f, k: (f, 0))],
        out_shape=[jax.ShapeDtypeStruct((ff // slot_cols, d, slot_cols), BF16),
                   jax.ShapeDtypeStruct((ff, d), BF16)],
        scratch_shapes=[pltpu.VMEM((d, tn), F32), pltpu.VMEM((tn, d), F32)],
        compiler_params=_params(("parallel", "arbitrary")),
    )(n, da, a, dzb)


def _pool_bwd(after, dh, h, g, pw, ps):
    s, d = h.shape
    cg = d // len(POOL_WINDOWS)
    tm = min(TILE_ROWS, s)
    nb = s // tm
    halo_per_tile = tm // POOL_HALO

    def body(after_ref, dh_ref, h_ref, halo_ref, g_ref, pw_ref, ps_ref,
             dx_ref, dpw_ref, dps_ref, dg_ref, nbuf, qbuf, dn_ref, carry, dpw_acc):
        i = pl.program_id(0)
        blk = nb - 1 - i

        @pl.when(i == 0)
        def _():
            carry[...] = jnp.zeros_like(carry)
            dpw_acc[...] = jnp.zeros_like(dpw_acc)
            dps_ref[...] = jnp.zeros_like(dps_ref)
            dg_ref[...] = jnp.zeros_like(dg_ref)

        hv = h_ref[...]
        n, _ = _rms_fwd(hv, g_ref[...])
        nh, _ = _rms_fwd(halo_ref[...], g_ref[...])
        nbuf[0:POOL_HALO, :] = jnp.where(blk == 0, 0.0, nh)
        nbuf[POOL_HALO:POOL_HALO + tm, :] = n
        dhv = dh_ref[...]
        for k, window in enumerate(POOL_WINDOWS):
            cols = slice(k * cg, (k + 1) * cg)
            cnt = _pool_counts(blk * tm, tm, window)
            sums = _window_sum_down(nbuf[:, cols], window)[POOL_HALO:, :]
            pb = (sums / cnt - n[:, cols]).astype(BF16)
            dyk = dhv[:, cols]
            dps_ref[:, cols] += jnp.sum(dyk * _nn(pb, pw_ref[k]), axis=0, keepdims=True)
            dyb = (dyk * ps_ref[:, cols]).astype(BF16)
            dpw_acc[k] += _tn(pb, dyb)
            dpool = _nt(dyb, pw_ref[k])
            qv = dpool / cnt
            qbuf[0:tm, cols] = qv
            qbuf[tm:tm + POOL_HALO, cols] = carry[:, cols]
            dn_ref[:, cols] = _window_sum_up(qbuf[:, cols], window)[0:tm, :] - dpool
            carry[:, cols] = qv[0:POOL_HALO, :]
        dx, dg = _rms_bwd(dn_ref[...], hv, g_ref[...])
        dx_ref[...] = dhv + dx
        dg_ref[...] += dg

        @pl.when(i == nb - 1)
        def _():
            dpw_ref[...] = dpw_acc[...].astype(BF16)

    rev = lambda i: (nb - 1 - i, 0)
    return pl.pallas_call(
        body, name="pool_bwd", grid=(nb,),
        in_specs=[ANY, pl.BlockSpec((tm, d), rev), pl.BlockSpec((tm, d), rev),
                  pl.BlockSpec((POOL_HALO, d),
                               lambda i: (jnp.maximum((nb - 1 - i) * halo_per_tile - 1, 0), 0)),
                  _full(g), _full(pw), _full(ps)],
        out_specs=[pl.BlockSpec((tm, d), rev), _full(pw),
                   pl.BlockSpec((1, d), lambda i: (0, 0)),
                   pl.BlockSpec((1, d), lambda i: (0, 0))],
        out_shape=[jax.ShapeDtypeStruct((s, d), F32),
                   jax.ShapeDtypeStruct(pw.shape, BF16),
                   jax.ShapeDtypeStruct((1, d), F32),
                   jax.ShapeDtypeStruct((1, d), F32)],
        scratch_shapes=[pltpu.VMEM((tm + POOL_HALO, d), F32), pltpu.VMEM((tm + POOL_HALO, d), F32),
                        pltpu.VMEM((tm, d), F32), pltpu.VMEM((POOL_HALO, d), F32),
                        pltpu.VMEM(pw.shape, F32)],
        compiler_params=_params(("arbitrary",)),
    )(after, dh, h, h, g, pw, ps)


def _conv_out_bwd(after, dh, w_out, o, bcx, cw):
    s, d = dh.shape
    c = o.shape[1]
    tm = min(TILE_ROWS, s)
    nb = s // tm
    halo_per_tile = tm // CONV_HALO

    def body(after_ref, dh_ref, w_ref, o_ref, b_ref, c_ref, xin_ref, ch_ref, xh_ref, cw_ref,
             do_ref, delta_ref, dbcx_ref, dw_ref, dcw_ref, ubuf, dbuf, carry, acc):
        i = pl.program_id(0)
        blk = nb - 1 - i

        @pl.when(i == 0)
        def _():
            carry[...] = jnp.zeros_like(carry)
            acc[...] = jnp.zeros_like(acc)
            dcw_ref[...] = jnp.zeros_like(dcw_ref)

        dm = dh_ref[...].astype(BF16)
        dcat = _nt(dm, w_ref[...])
        do = dcat[:, 0:c]
        dy = dcat[:, c:2 * c]
        do_ref[...] = do.astype(BF16)
        head_of_lane = lax.shift_right_logical(_lane((8, c)), HEAD_DIM.bit_length() - 1)
        heads = (head_of_lane == _row((8, c))).astype(BF16)
        delta_ref[...] = _exact_nt(heads, do * o_ref[...].astype(F32))

        cv_ = c_ref[...]
        xin = xin_ref[...]
        bv = b_ref[...]
        u = cv_ * xin
        ubuf[0:CONV_HALO, :] = jnp.where(blk == 0, 0.0, ch_ref[...] * xh_ref[...])
        ubuf[CONV_HALO:CONV_HALO + tm, :] = u
        u1 = ubuf[CONV_HALO - 1:CONV_HALO - 1 + tm, :]
        u2 = ubuf[CONV_HALO - 2:CONV_HALO - 2 + tm, :]
        w0, w1, w2 = cw_ref[0:1, :], cw_ref[1:2, :], cw_ref[2:3, :]
        cv = (w0 * u2 + w1 * u1) + w2 * u
        acc[0:c, :] += _tn(o_ref[...], dm)
        acc[c:2 * c, :] += _tn((bv * cv).astype(BF16), dm)

        dcv = dy * bv
        dcw_ref[0:1, :] += jnp.sum(dcv * u2, axis=0, keepdims=True)
        dcw_ref[1:2, :] += jnp.sum(dcv * u1, axis=0, keepdims=True)
        dcw_ref[2:3, :] += jnp.sum(dcv * u, axis=0, keepdims=True)
        dbuf[0:tm, :] = dcv
        dbuf[tm:tm + CONV_HALO, :] = carry[...]
        du = w2 * dcv + w1 * dbuf[1:1 + tm, :] + w0 * dbuf[2:2 + tm, :]
        dbcx_ref[:, 0:c] = (dy * cv).astype(BF16)
        dbcx_ref[:, c:2 * c] = (du * xin).astype(BF16)
        dbcx_ref[:, 2 * c:3 * c] = (du * cv_).astype(BF16)
        carry[...] = dcv[0:CONV_HALO, :]

        @pl.when(i == nb - 1)
        def _():
            dw_ref[...] = acc[...].astype(BF16)

    rev = lambda k: (lambda i: (nb - 1 - i, k))
    halo = lambda k: (lambda i: (jnp.maximum((nb - 1 - i) * halo_per_tile - 1, 0), k))
    return pl.pallas_call(
        body, name="conv_out_bwd", grid=(nb,),
        in_specs=[ANY, pl.BlockSpec((tm, d), rev(0)), _full(w_out), pl.BlockSpec((tm, c), rev(0)),
                  pl.BlockSpec((tm, c), rev(0)), pl.BlockSpec((tm, c), rev(1)),
                  pl.BlockSpec((tm, c), rev(2)),
                  pl.BlockSpec((CONV_HALO, c), halo(1)), pl.BlockSpec((CONV_HALO, c), halo(2)),
                  _full(cw)],
        out_specs=[pl.BlockSpec((tm, c), rev(0)),
                   pl.BlockSpec((8, tm), lambda i: (0, nb - 1 - i)),
                   pl.BlockSpec((tm, 3 * c), rev(0)),
                   _full(w_out), _full(cw)],
        out_shape=[jax.ShapeDtypeStruct((s, c), BF16),
                   jax.ShapeDtypeStruct((8, s), F32),
                   jax.ShapeDtypeStruct((s, 3 * c), BF16),
                   jax.ShapeDtypeStruct(w_out.shape, BF16),
                   jax.ShapeDtypeStruct(cw.shape, F32)],
        scratch_shapes=[pltpu.VMEM((tm + CONV_HALO, c), F32), pltpu.VMEM((tm + CONV_HALO, c), F32),
                        pltpu.VMEM((CONV_HALO, c), F32), pltpu.VMEM(w_out.shape, F32)],
        compiler_params=_params(("arbitrary",)),
    )(after, dh, w_out, o, bcx, bcx, bcx, bcx, bcx, cw)


def _attn_bwd(qa, ka, qkv, do, lse, delta):
    s = qa.shape[1]
    a = N_HEADS * HEAD_DIM
    t = min(TILE_ATTN, s)
    nq = s // t
    n_pairs = N_HEADS // 2
    v_block0 = 2 * a // 128

    def body(ka_ref, v_ref, qa_ref, do_ref, lse_ref, delta_ref,
             dqt_ref, dka_ref, dv_ref, dk_acc, dv_acc):
        g = pl.program_id(0)
        j = pl.program_id(1)

        @pl.when(j == 0)
        def _():
            dqt_ref[...] = jnp.zeros_like(dqt_ref)

        dk_acc[...] = jnp.zeros_like(dk_acc)
        dv_acc[...] = jnp.zeros_like(dv_acc)
        lane = _lane((t, 128))
        vf = v_ref[...].astype(F32)
        v_heads = [jnp.where(lane < HEAD_DIM, vf, 0.0).astype(BF16),
                   jnp.where(lane >= HEAD_DIM, vf, 0.0).astype(BF16)]
        ke_t = [ka_ref[e].astype(F32).T.astype(BF16) for e in range(2)]

        def q_step(i, masked):
            qs = pl.ds(pl.multiple_of(i * t, t), t)
            dob = do_ref[qs, :]
            for e in range(2):
                qe = qa_ref[e, qs, :]
                sc = _nt(ka_ref[e], qe)
                if masked:
                    sc = jnp.where(_row((t, t)) <= _lane((t, t)), sc, NEG_BIG)
                p = jnp.exp2(sc - lse_ref[pl.ds(e, 1), qs])
                dv_acc[e] += _nn(p.astype(BF16), dob)
                dp = _nt(v_heads[e], dob)
                ds = (p * (dp - delta_ref[pl.ds(2 * g + e, 1), qs])).astype(BF16)
                dk_acc[e] += _nn(ds, qe)
                dqt_ref[e, :, qs] += _nn(ke_t[e], ds)

        q_step(j, True)

        def full_step(i, carry):
            q_step(i, False)
            return carry

        lax.fori_loop(j + 1, nq, full_step, 0)
        dka_ref[...] = dk_acc[...]
        dv_ref[...] = jnp.where(lane < HEAD_DIM, dv_acc[0], dv_acc[1]).astype(BF16)

    return pl.pallas_call(
        body, name="attn_bwd", grid=(n_pairs, nq),
        in_specs=[pl.BlockSpec((2, t, 128), lambda g, j: (g, j, 0)),
                  pl.BlockSpec((t, 128), lambda g, j: (j, v_block0 + g)),
                  pl.BlockSpec((2, s, 128), lambda g, j: (g, 0, 0)),
                  pl.BlockSpec((s, 128), lambda g, j: (0, g)),
                  pl.BlockSpec((None, 8, s), lambda g, j: (g, 0, 0)),
                  pl.BlockSpec((8, s), lambda g, j: (0, 0))],
        out_specs=[pl.BlockSpec((2, 128, s), lambda g, j: (g, 0, 0)),
                   pl.BlockSpec((2, t, 128), lambda g, j: (g, j, 0)),
                   pl.BlockSpec((t, 128), lambda g, j: (j, g))],
        out_shape=[jax.ShapeDtypeStruct((N_HEADS, 128, s), F32),
                   jax.ShapeDtypeStruct((N_HEADS, s, 128), F32),
                   jax.ShapeDtypeStruct((s, a), BF16)],
        scratch_shapes=[pltpu.VMEM((2, t, 128), F32), pltpu.VMEM((2, t, 128), F32)],
        compiler_params=_params(("parallel", "arbitrary")),
    )(ka, qkv, qa, do, lse, delta)


def _gate_bwd(dqa, dka, dv, fl, bf):
    s = fl.shape[0]
    a = N_HEADS * HEAD_DIM
    tm = min(TILE_ROWS, s)
    nb = s // tm

    def body(dqa_ref, dka_ref, dv_ref, fl_ref, bf_ref, dqkv_ref, dfl_ref, dbf_ref, carry):
        i = pl.program_id(0)

        @pl.when(i == 0)
        def _():
            carry[...] = jnp.zeros_like(carry)
            dbf_ref[...] = jnp.zeros_like(dbf_ref)

        lane = _lane((tm, 128))
        dcum = jnp.zeros((tm, 128), F32)
        for pair in range(N_HEADS // 2):
            qs, ks = [], []
            for e in range(2):
                h = 2 * pair + e
                dq = dqa_ref[h].T
                dk = dka_ref[h]
                dc = jnp.sum(jnp.where(lane == LANE_CQ, dq, 0.0)
                             - jnp.where(lane == LANE_ONE, dk, 0.0), axis=1, keepdims=True)
                dcum = jnp.where(lane == h, dc, dcum)
                qs.append(dq * ATTN_SCALE)
                ks.append(dk * (1.0 / LOG2_E))
            cols = slice(pair * 128, (pair + 1) * 128)
            dqkv_ref[:, cols] = jnp.where(
                lane < HEAD_DIM, qs[0], pltpu.roll(qs[1], HEAD_DIM, axis=1)).astype(BF16)
            dqkv_ref[:, a + pair * 128:a + (pair + 1) * 128] = jnp.where(
                lane < HEAD_DIM, ks[0], pltpu.roll(ks[1], HEAD_DIM, axis=1)).astype(BF16)
        dqkv_ref[:, 2 * a:3 * a] = dv_ref[...]

        upper = (_lane((tm, tm)) >= _row((tm, tm))).astype(BF16)
        dlogf = _exact_nn(upper, dcum) + carry[0:1, :]
        carry[0:1, :] = dlogf[0:1, :]
        z = fl_ref[...] + bf_ref[...]
        ez = jnp.exp(-jnp.abs(z))
        sig_neg = jnp.where(z >= 0.0, ez, 1.0) / (1.0 + ez)
        dz = jnp.where(lane < N_HEADS, dlogf * sig_neg, 0.0)
        dfl_ref[...] = dz.astype(BF16)
        dbf_ref[...] += jnp.sum(dz, axis=0, keepdims=True)

    rev3 = lambda i: (0, nb - 1 - i, 0)
    rev = lambda i: (nb - 1 - i, 0)
    return pl.pallas_call(
        body, name="gate_bwd", grid=(nb,),
        in_specs=[pl.BlockSpec((N_HEADS, 128, tm), lambda i: (0, 0, nb - 1 - i)),
                  pl.BlockSpec((N_HEADS, tm, 128), rev3),
                  pl.BlockSpec((tm, a), rev), pl.BlockSpec((tm, 128), rev), _full(bf)],
        out_specs=[pl.BlockSpec((tm, 3 * a), rev), pl.BlockSpec((tm, 128), rev),
                   pl.BlockSpec((1, 128), lambda i: (0, 0))],
        out_shape=[jax.ShapeDtypeStruct((s, 3 * a), BF16),
                   jax.ShapeDtypeStruct((s, 128), BF16),
                   jax.ShapeDtypeStruct((1, 128), F32)],
        scratch_shapes=[pltpu.VMEM((8, 128), F32)],
        compiler_params=_params(("arbitrary",)),
    )(dqa, dka, dv, fl, bf)


def _in_proj_bwd(after, dqkv, dfl, dbcx, w_qkv, w_f, w_bcx, x, g, dh):
    s, d = x.shape
    tm = min(TILE_ROWS, s)

    def body(after_ref, dq_ref, df_ref, db_ref, wq_ref, wf_ref, wb_ref, x_ref, g_ref, dh_ref,
             gx_ref, dg_ref):
        i = pl.program_id(0)
        dn = (_nt(dq_ref[...], wq_ref[...]) + _nt(df_ref[...], wf_ref[...])
              + _nt(db_ref[...], wb_ref[...]))
        dx, dg = _rms_bwd(dn, x_ref[...], g_ref[...])
        gx_ref[...] = dh_ref[...] + dx

        @pl.when(i == 0)
        def _():
            dg_ref[...] = dg

        @pl.when(i > 0)
        def _():
            dg_ref[...] += dg

    rows = lambda c: pl.BlockSpec((tm, c), lambda i: (i, 0))
    return pl.pallas_call(
        body, name="in_proj_bwd", grid=(s // tm,),
        in_specs=[ANY, rows(dqkv.shape[1]), rows(dfl.shape[1]), rows(dbcx.shape[1]),
                  _full(w_qkv), _full(w_f), _full(w_bcx), rows(d), _full(g), rows(d)],
        out_specs=[rows(d), pl.BlockSpec((1, d), lambda i: (0, 0))],
        out_shape=[jax.ShapeDtypeStruct((s, d), F32), jax.ShapeDtypeStruct((1, d), F32)],
        compiler_params=_params(("arbitrary",)),
    )(after, dqkv, dfl, dbcx, w_qkv, w_f, w_bcx, x, g, dh)


def _wgrad_in(n, dys):
    s, d = n.shape
    m = len(dys)
    tk = min(TILE_ROWS, s)
    nk = s // tk

    def body(*refs):
        n_ref, dy_refs, dw_refs, accs = refs[0], refs[1:1 + m], refs[1 + m:1 + 2 * m], refs[1 + 2 * m:]
        k = pl.program_id(0)

        @pl.when(k == 0)
        def _():
            for acc in accs:
                acc[...] = jnp.zeros_like(acc)

        nb = n_ref[...]
        for dy_ref, acc in zip(dy_refs, accs):
            acc[...] += _tn(nb, dy_ref[...])

        @pl.when(k == nk - 1)
        def _():
            for dw_ref, acc in zip(dw_refs, accs):
                dw_ref[...] = acc[...].T.astype(BF16)

    return pl.pallas_call(
        body, name="wgrad_in", grid=(nk,),
        in_specs=[pl.BlockSpec((tk, d), lambda k: (k, 0))]
        + [pl.BlockSpec((tk, dy.shape[1]), lambda k: (k, 0)) for dy in dys],
        out_specs=[pl.BlockSpec((dy.shape[1], d), lambda k: (0, 0)) for dy in dys],
        out_shape=[jax.ShapeDtypeStruct((dy.shape[1], d), BF16) for dy in dys],
        scratch_shapes=[pltpu.VMEM((d, dy.shape[1]), F32) for dy in dys],
        compiler_params=_params(("arbitrary",)),
    )(n, *dys)


def _row_tile(rows):
    t = min(TILE_ELEM_ROWS, rows)
    while rows % t:
        t //= 2
    return t


def _adamw_math(w, g, m, v):
    m = ADAM_B1 * m + (1.0 - ADAM_B1) * g
    v = ADAM_B2 * v + (1.0 - ADAM_B2) * jnp.square(g)
    m_hat = m / (1.0 - ADAM_B1 ** ADAM_STEP)
    v_hat = v / (1.0 - ADAM_B2 ** ADAM_STEP)
    delta = -ADAM_LR * (m_hat / (jnp.sqrt(v_hat) + ADAM_EPS) + ADAM_WD * w)
    return delta, m, v


def _adamw(w, g, m, v, name):
    rows, cols = w.shape

    def body(w_ref, g_ref, m_ref, v_ref, d_ref, nm_ref, nv_ref):
        delta, nm, nv = _adamw_math(w_ref[...], g_ref[...], m_ref[...], v_ref[...])
        d_ref[...] = delta
        nm_ref[...] = nm
        nv_ref[...] = nv

    if rows % 8 == 0:
        tr = _row_tile(rows)
        grid, spec = (rows // tr,), pl.BlockSpec((tr, cols), lambda i: (i, 0))
    else:
        grid, spec = (cols // 256,), pl.BlockSpec((rows, 256), lambda i: (0, i))
    out = jax.ShapeDtypeStruct(w.shape, F32)
    return pl.pallas_call(
        body, name=name, grid=grid, in_specs=[spec] * 4, out_specs=[spec] * 3,
        out_shape=[out, out, out], compiler_params=_params(("parallel",)),
    )(w, g, m, v)


def _sum_devices(parts):
    def body(p_ref, g_ref):
        g = p_ref[0]
        for k in range(1, N_DEV):
            g = g + p_ref[k]
        g_ref[...] = g

    return pl.pallas_call(
        body, name="sum_devices",
        in_specs=[pl.BlockSpec(memory_space=pltpu.VMEM)],
        out_specs=pl.BlockSpec(memory_space=pltpu.VMEM),
        out_shape=jax.ShapeDtypeStruct(parts.shape[1:], F32),
    )(parts)


def _mesh_position():
    x, y, c = lax.axis_index("x"), lax.axis_index("y"), lax.axis_index("c")
    chips = [(1 - x, y), (x, 1 - y), (1 - x, 1 - y)]
    return x, y, c, chips


ANY = pl.BlockSpec(memory_space=pl.ANY)
HBM = pl.BlockSpec(memory_space=pltpu.HBM)
SEM = pl.BlockSpec(memory_space=pltpu.SEMAPHORE)
SPLIT_COPY_EFFECT = pltpu.SideEffectType.DATAFLOW_SIDE_EFFECTING


def _in_hbm(a):
    return pltpu.with_memory_space_constraint(a, pltpu.HBM)


def _chip_copies(views, srcs, lands, send, recv):
    _, _, c, chips = _mesh_position()
    cps = []
    for a in range(len(srcs)):
        for k, (px, py) in enumerate(chips):
            src, dst = views(a, k, srcs[a], lands[a], c, 2 * px + py)
            sem = a * (N_CHIPS - 1) + k
            cps.append(pltpu.make_async_remote_copy(
                src_ref=src, dst_ref=dst, send_sem=send.at[sem], recv_sem=recv.at[sem],
                device_id=(px, py, c), device_id_type=MESH))
    return cps


def _ici_start(sources, land_shapes, views, after, name):
    n = len(sources)

    def body(*refs):
        srcs, lands = refs[:n], refs[n:2 * n]
        send, recv = refs[2 * n + 1], refs[2 * n + 2]
        token = refs[-1]
        for cp in _chip_copies(views, srcs, lands, send, recv):
            cp.start()
        token[...] = jnp.zeros_like(token)

    lands = [_in_hbm(lax.empty(s.shape, s.dtype)) for s in land_shapes]
    outs = pl.pallas_call(
        body, name=name,
        in_specs=[HBM] * (2 * n) + [ANY],
        out_specs=[SEM, SEM] + [HBM] * (2 * n) + [pl.BlockSpec(memory_space=pltpu.VMEM)],
        out_shape=[pltpu.SemaphoreType.DMA((n * (N_CHIPS - 1),))] * 2
        + [pltpu.HBM(a.shape, a.dtype) for a in sources]
        + [pltpu.HBM(s.shape, s.dtype) for s in land_shapes]
        + [jax.ShapeDtypeStruct((8, 128), F32)],
        input_output_aliases={i: 2 + i for i in range(2 * n)},
        compiler_params=pltpu.CompilerParams(has_side_effects=SPLIT_COPY_EFFECT),
    )(*[_in_hbm(a) for a in sources], *lands, after)
    return outs[0], outs[1], list(outs[2:2 + n]), list(outs[2 + n:2 + 2 * n]), outs[-1]


def _ici_wait(handle, views, after, name):
    send, recv, srcs, lands, _ = handle
    n = len(srcs)

    def body(*refs):
        src_refs, land_refs = refs[:n], refs[n:2 * n]
        for cp in _chip_copies(views, src_refs, land_refs, refs[2 * n], refs[2 * n + 1]):
            cp.wait_send()
            cp.wait_recv()

    outs = pl.pallas_call(
        body, name=name,
        in_specs=[HBM] * (2 * n) + [SEM, SEM, ANY],
        out_specs=[HBM] * (2 * n),
        out_shape=[pltpu.HBM(a.shape, a.dtype) for a in srcs]
        + [pltpu.HBM(a.shape, a.dtype) for a in lands],
        input_output_aliases={i: i for i in range(2 * n)},
        compiler_params=pltpu.CompilerParams(has_side_effects=SPLIT_COPY_EFFECT),
    )(*srcs, *lands, send, recv, after)
    return list(outs[:n]), list(outs[n:])


def _gather_views(split):
    def views(a, k, src, land, c, slot):
        if split[a]:
            half = src.shape[0] // 2
            src = src.at[pl.ds(c * half, half)]
        return src, land.at[k]
    return views


def _gather_whole_views(a, k, src, land, c, slot):
    x, y, _, _ = _mesh_position()
    return src, land.at[2 * x + y]


def _scatter_views(a, k, src, land, c, slot):
    return src.at[slot], land.at[k]


def _gather_land_shapes(shards, split):
    return [jax.ShapeDtypeStruct(
        (N_CHIPS - 1, a.shape[0] // 2 if sp else a.shape[0]) + a.shape[1:], a.dtype)
        for a, sp in zip(shards, split)]


def _gather_finish(shards, lands, split, name):
    n = len(shards)
    ns = sum(split)
    d_index = {a: i for i, a in enumerate(a for a in range(n) if split[a])}

    def body(*refs):
        shard, land, outs = refs[:n], refs[n:2 * n], refs[2 * n:3 * n]
        obuf, fbuf = refs[3 * n:4 * n], refs[4 * n:5 * n]
        dbuf = refs[5 * n:5 * n + ns]
        ld_own, st_own, ld, st_mine, st_sib, send, recv = refs[5 * n + ns:]
        x, y, c, chips = _mesh_position()
        me = 2 * x + y
        own_loads, loads, sends, pending = [], {}, [], []
        for a in range(n):
            cp = pltpu.make_async_copy(shard[a], obuf[a], ld_own.at[a])
            cp.start()
            own_loads.append(cp)
        for a in range(n):
            for k in range(N_CHIPS - 1):
                cp = pltpu.make_async_copy(land[a].at[k], fbuf[a].at[k], ld.at[a, k])
                cp.start()
                loads[a, k] = cp
        for a in range(n):
            own_loads[a].wait()
            cp = pltpu.make_async_copy(obuf[a], outs[a].at[me], st_own.at[a])
            cp.start()
            pending.append(cp)
        for a in range(n):
            rows = shard[a].shape[0]
            for k, (px, py) in enumerate(chips):
                loads[a, k].wait()
                part = pl.ds(c * (rows // 2), rows // 2) if split[a] else pl.ds(0, rows)
                cp = pltpu.make_async_copy(fbuf[a].at[k], outs[a].at[2 * px + py, part],
                                           st_mine.at[a, k])
                cp.start()
                pending.append(cp)
                if split[a]:
                    fw = pltpu.make_async_remote_copy(
                        src_ref=fbuf[a].at[k], dst_ref=dbuf[d_index[a]].at[k],
                        send_sem=send.at[a, k], recv_sem=recv.at[a, k],
                        device_id=(x, y, 1 - c), device_id_type=MESH)
                    fw.start()
                    sends.append((a, k, fw))
        for a, k, fw in sends:
            px, py = chips[k]
            half = shard[a].shape[0] // 2
            fw.wait_recv()
            cp = pltpu.make_async_copy(dbuf[d_index[a]].at[k],
                                       outs[a].at[2 * px + py, pl.ds((1 - c) * half, half)],
                                       st_sib.at[a, k])
            cp.start()
            pending.append(cp)
        for _, _, fw in sends:
            fw.wait_send()
        for cp in pending:
            cp.wait()

    stage = [pltpu.VMEM(a.shape, a.dtype) for a in lands]
    dma = lambda *shape: pltpu.SemaphoreType.DMA(shape)
    return pl.pallas_call(
        body, name=name,
        in_specs=[ANY] * (2 * n), out_specs=[ANY] * n,
        out_shape=[jax.ShapeDtypeStruct((N_CHIPS,) + a.shape, a.dtype) for a in shards],
        scratch_shapes=[pltpu.VMEM(a.shape, a.dtype) for a in shards] + stage
        + [s for s, sp in zip(stage, split) if sp]
        + [dma(n), dma(n), dma(n, 3), dma(n, 3), dma(n, 3), dma(n, 3), dma(n, 3)],
        compiler_params=pltpu.CompilerParams(vmem_limit_bytes=VMEM_LIMIT_BYTES),
    )(*shards, *lands)


def _sum_chunk(rows):
    return next(r for r in range(SUM_CHUNK_ROWS, 0, -16) if rows % r == 0)


def _exchange_siblings(grads, name):
    n = len(grads)

    def body(*refs):
        ins, outs = refs[:n], refs[n:2 * n]
        sbuf, rbuf, mbuf = refs[2 * n:3 * n], refs[3 * n:4 * n], refs[4 * n:5 * n]
        ld_send, ld_mine, st, send, recv = refs[5 * n:]
        x, y, c, _ = _mesh_position()
        loads, mine, sends, stores = [], [], [], []
        for a in range(n):
            half = ins[a].shape[1] // 2
            cp = pltpu.make_async_copy(ins[a].at[:, pl.ds((1 - c) * half, half)], sbuf[a],
                                       ld_send.at[a])
            cp.start()
            loads.append(cp)
        for a in range(n):
            half = ins[a].shape[1] // 2
            cp = pltpu.make_async_copy(ins[a].at[:, pl.ds(c * half, half)], mbuf[a], ld_mine.at[a])
            cp.start()
            mine.append(cp)
        for a in range(n):
            loads[a].wait()
            rc = pltpu.make_async_remote_copy(
                src_ref=sbuf[a], dst_ref=rbuf[a], send_sem=send.at[a], recv_sem=recv.at[a],
                device_id=(x, y, 1 - c), device_id_type=MESH)
            rc.start()
            sends.append(rc)
        for a in range(n):
            sends[a].wait_recv()
            mine[a].wait()
            slots, half, _ = rbuf[a].shape
            rows = _sum_chunk(half)
            per_slot = half // rows

            def add(k, carry, a=a, rows=rows, per_slot=per_slot):
                at = (k // per_slot, pl.ds(pl.multiple_of((k % per_slot) * rows, rows), rows))
                rbuf[a][at] = (rbuf[a][at].astype(F32) + mbuf[a][at].astype(F32)).astype(BF16)
                return carry

            lax.fori_loop(0, slots * per_slot, add, 0)
            cp = pltpu.make_async_copy(rbuf[a], outs[a], st.at[a])
            cp.start()
            stores.append(cp)
        for a in range(n):
            sends[a].wait_send()
            stores[a].wait()

    half_shape = lambda a: (a.shape[0], a.shape[1] // 2, a.shape[2])
    stage = [pltpu.VMEM(half_shape(a), a.dtype) for a in grads]
    return pl.pallas_call(
        body, name=name,
        in_specs=[ANY] * n, out_specs=[ANY] * n,
        out_shape=[jax.ShapeDtypeStruct(half_shape(a), a.dtype) for a in grads],
        scratch_shapes=stage * 3 + [pltpu.SemaphoreType.DMA((n,))] * 5,
        compiler_params=pltpu.CompilerParams(vmem_limit_bytes=VMEM_LIMIT_BYTES),
    )(*grads)


def _sum_and_share(sums, got, name):
    n = len(sums)

    def body(*refs):
        own, others, outs = refs[:n], refs[n:2 * n], refs[2 * n:3 * n]
        obuf, gbuf, sbuf, rbuf = (refs[(3 + k) * n:(4 + k) * n] for k in range(4))
        ld_own, ld_got, st_own, st_sib, send, recv = refs[7 * n:]
        x, y, c, _ = _mesh_position()
        loads, sends, stores = [], [], []
        for a in range(n):
            cps = [pltpu.make_async_copy(own[a].at[2 * x + y], obuf[a], ld_own.at[a]),
                   pltpu.make_async_copy(others[a], gbuf[a], ld_got.at[a])]
            for cp in cps:
                cp.start()
            loads.append(cps)
        for a in range(n):
            for cp in loads[a]:
                cp.wait()
            half = obuf[a].shape[0]
            rows = _sum_chunk(half)

            def add(k, carry, a=a, rows=rows):
                at = pl.ds(pl.multiple_of(k * rows, rows), rows)
                acc = obuf[a][at].astype(F32)
                for j in range(N_CHIPS - 1):
                    acc = acc + gbuf[a][j, at].astype(F32)
                sbuf[a][at] = acc
                return carry

            lax.fori_loop(0, half // rows, add, 0)
            rc = pltpu.make_async_remote_copy(
                src_ref=sbuf[a], dst_ref=rbuf[a], send_sem=send.at[a], recv_sem=recv.at[a],
                device_id=(x, y, 1 - c), device_id_type=MESH)
            rc.start()
            sends.append(rc)
            cp = pltpu.make_async_copy(sbuf[a], outs[a].at[pl.ds(c * half, half)], st_own.at[a])
            cp.start()
            stores.append(cp)
        for a in range(n):
            half = obuf[a].shape[0]
            sends[a].wait_recv()
            cp = pltpu.make_async_copy(rbuf[a], outs[a].at[pl.ds((1 - c) * half, half)], st_sib.at[a])
            cp.start()
            stores.append(cp)
        for cp in sends:
            cp.wait_send()
        for cp in stores:
            cp.wait()

    halves = [a.shape[1:] for a in sums]
    return pl.pallas_call(
        body, name=name,
        in_specs=[ANY] * (2 * n), out_specs=[ANY] * n,
        out_shape=[jax.ShapeDtypeStruct((2 * h[0],) + h[1:], F32) for h in halves],
        scratch_shapes=[pltpu.VMEM(h, BF16) for h in halves]
        + [pltpu.VMEM(g.shape, BF16) for g in got]
        + [pltpu.VMEM(h, F32) for h in halves] * 2
        + [pltpu.SemaphoreType.DMA((n,))] * 6,
        compiler_params=pltpu.CompilerParams(vmem_limit_bytes=VMEM_LIMIT_BYTES),
    )(*sums, *got)


def _gather_small(part):
    def body(in_ref, out_ref, send, recv, local):
        x, y, c, _ = _mesh_position()
        me = 4 * x + 2 * y + c
        cps = [pltpu.make_async_copy(in_ref, out_ref.at[me], local)]
        k = 0
        for fx in range(2):
            for fy in range(2):
                for fc in range(2):
                    if fx or fy or fc:
                        cps.append(pltpu.make_async_remote_copy(
                            src_ref=in_ref, dst_ref=out_ref.at[me], send_sem=send.at[k],
                            recv_sem=recv.at[k], device_id=(x ^ fx, y ^ fy, c ^ fc),
                            device_id_type=MESH))
                        k += 1
        for cp in cps:
            cp.start()
        for cp in cps:
            cp.wait()

    return pl.pallas_call(
        body, name="gather_small",
        in_specs=[pl.BlockSpec(memory_space=pltpu.VMEM)],
        out_specs=pl.BlockSpec(memory_space=pltpu.VMEM),
        out_shape=jax.ShapeDtypeStruct((N_DEV,) + part.shape, part.dtype),
        scratch_shapes=[pltpu.SemaphoreType.DMA((N_DEV - 1,)), pltpu.SemaphoreType.DMA((N_DEV - 1,)),
                        pltpu.SemaphoreType.DMA],
    )(part)


def _scatter_start(grads, tag):
    sums = _exchange_siblings(grads, "exchange_siblings_" + tag)
    lands = [jax.ShapeDtypeStruct((N_CHIPS - 1,) + s.shape[1:], s.dtype) for s in sums]
    return _ici_start(sums, lands, _scatter_views, grads[0], "scatter_start_" + tag)


def _scatter_finish(handle, after, tag):
    sums, got = _ici_wait(handle, _scatter_views, after, "scatter_wait_" + tag)
    return _sum_and_share(sums, got, "sum_and_share_" + tag)


def _pad_rows(a, rows):
    return jnp.pad(a, ((0, rows - a.shape[0]), (0, 0)))


def kernel(x, norm_mix_0, w_in_0, b_f_0, conv_w_0, w_out_0, norm_ffn_0, w_up_0, w_down_0, norm_mix_1, pool_w_1, pool_scale_1, norm_ffn_1, w_up_1, w_down_1, final_norm, loss_target, m_norm_mix_0, m_w_in_0, m_b_f_0, m_conv_w_0, m_w_out_0, m_norm_ffn_0, m_w_up_0, m_w_down_0, m_norm_mix_1, m_pool_w_1, m_pool_scale_1, m_norm_ffn_1, m_w_up_1, m_w_down_1, m_final_norm, v_norm_mix_0, v_w_in_0, v_b_f_0, v_conv_w_0, v_w_out_0, v_norm_ffn_0, v_w_up_0, v_w_down_0, v_norm_mix_1, v_pool_w_1, v_pool_scale_1, v_norm_ffn_1, v_w_up_1, v_w_down_1, v_final_norm):
    d = x.shape[-1]
    a = N_HEADS * HEAD_DIM
    c_conv = conv_w_0.shape[1] * N_CHIPS
    xs = x[0]
    target = loss_target[0]
    row = lambda vec: vec.reshape(1, -1)

    big = [w_in_0, w_out_0, w_up_0, w_down_0, pool_w_1, w_up_1, w_down_1]
    first = [w_in_0.astype(BF16)]
    first_split = [True]
    start_a = _ici_start(first, _gather_land_shapes(first, first_split),
                         _gather_views(first_split), b_f_0, "gather_start_a")
    zero = start_a[-1][0, 0]
    rest = [(w + zero).astype(BF16)
            for w in (w_out_0, w_up_0, w_down_0, pool_w_1, w_up_1, w_down_1)]
    rest = rest + [conv_w_0]
    start_b = _ici_start(rest, [jax.ShapeDtypeStruct((N_CHIPS,) + w.shape, w.dtype) for w in rest],
                         _gather_whole_views, start_a[-1], "gather_start_b")
    n0 = _rms_pre(start_b[-1], xs, row(norm_mix_0))
    first, land_a = _ici_wait(start_a, _gather_views(first_split), n0, "gather_wait_a")
    (g_in,) = _gather_finish(first, land_a, first_split, "gather_finish_a")
    w_in = g_in.transpose(1, 0, 2).reshape(d, -1)
    w_qkv = w_in[:, :3 * a]
    w_f = jnp.pad(w_in[:, 3 * a:3 * a + N_HEADS], ((0, 0), (0, 128 - N_HEADS)))
    w_bcx = w_in[:, 3 * a + N_HEADS:]
    bf = jnp.pad(b_f_0, (0, 128 - N_HEADS)).reshape(1, 128)

    qkv, fl, bcx = _in_proj(n0, w_qkv, w_f, w_bcx)
    qa, ka = _gate_prep(fl, bf, qkv)
    o, lse = _attn_fwd(qa, ka, qkv)
    rest, land_b = _ici_wait(start_b, _gather_whole_views, o, "gather_wait_b")
    own_slot = 2 * lax.axis_index("x") + lax.axis_index("y")
    g_out, g_up0, g_down0, g_pool, g_up1, g_down1, g_conv = [
        lax.dynamic_update_index_in_dim(land, shard, own_slot, 0)
        for land, shard in zip(land_b, rest)]
    w_out = g_out.reshape(-1, d)
    conv_w = _pad_rows(g_conv.transpose(1, 0, 2).reshape(conv_w_0.shape[0], c_conv), 8)
    h1 = _conv_out(o, bcx, conv_w, w_out, xs)
    w_down0 = g_down0.reshape(-1, d)
    w_down1 = g_down1.reshape(-1, d)
    pool_w = g_pool.transpose(1, 0, 2, 3).reshape(pool_w_1.shape[0], -1, pool_w_1.shape[2])
    h2, a0, nf0 = _mlp_fwd(h1, row(norm_ffn_0), g_up0, w_down0, "mlp_fwd_0")
    h3 = _pool_fwd(h2, row(norm_mix_1), pool_w, row(pool_scale_1))
    dh4, a1, nf1, loss_part, d_final = _mlp_fwd(h3, row(norm_ffn_1), g_up1, w_down1, "mlp_fwd_1",
                                                head=(row(final_norm), target))

    slot_cols = g_up0.shape[2]
    pool_cols = pool_w.shape[2]
    da1, dz1, dh3, d_nffn1 = _mlp_bwd_x(dh4, a1, g_up1, w_down1, h3, row(norm_ffn_1), "mlp_bwd_x_1")
    dw_up1, dw_down1 = _mlp_bwd_w(nf1, da1, a1, dz1, slot_cols, "mlp_bwd_w_1")
    scatter_1 = _scatter_start([dw_up1, dw_down1.reshape(N_CHIPS, -1, d)], "mlp1")
    dh2, dw_pool, d_pscale, d_nmix1 = _pool_bwd(scatter_1[-1], dh3, h2, row(norm_mix_1), pool_w,
                                                row(pool_scale_1))
    da0, dz0, dh1, d_nffn0 = _mlp_bwd_x(dh2, a0, g_up0, w_down0, h1, row(norm_ffn_0), "mlp_bwd_x_0")
    dw_up0, dw_down0 = _mlp_bwd_w(nf0, da0, a0, dz0, slot_cols, "mlp_bwd_w_0")
    dw_pool = (dw_pool.reshape(pool_w.shape[0], N_CHIPS, -1, pool_cols).transpose(1, 0, 2, 3)
               .reshape(N_CHIPS, -1, pool_cols))
    scatter_0 = _scatter_start([dw_up0, dw_down0.reshape(N_CHIPS, -1, d), dw_pool], "mlp0")
    do, delta, dbcx, dw_out, d_conv = _conv_out_bwd(scatter_0[-1], dh1, w_out, o, bcx, conv_w)
    dqa, dka, dv = _attn_bwd(qa, ka, qkv, do, lse, delta)
    dqkv, dfl, d_bf = _gate_bwd(dqa, dka, dv, fl, bf)
    dw_qkv, dw_f, dw_bcx = _wgrad_in(n0, [dqkv, dfl, dbcx])
    dw_in = jnp.concatenate([dw_qkv, dw_f[:N_HEADS], dw_bcx], axis=0).reshape(N_CHIPS, -1, d)
    slot_rows = -(-dw_in.shape[1] // 32) * 32
    dw_in = jnp.pad(dw_in, ((0, 0), (0, slot_rows - dw_in.shape[1]), (0, 0)))
    scatter_m = _scatter_start([dw_in, dw_out.reshape(N_CHIPS, -1, d)], "mixer")
    grad_x, d_nmix0 = _in_proj_bwd(scatter_m[-1], dqkv, dfl, dbcx, w_qkv, w_f, w_bcx, xs,
                                   row(norm_mix_0), dh1)

    r_up1, r_down1 = _scatter_finish(scatter_1, grad_x, "mlp1")
    r_up0, r_down0, r_pool = _scatter_finish(scatter_0, grad_x, "mlp0")
    r_in, r_out = _scatter_finish(scatter_m, grad_x, "mixer")
    reduced = [r_in, r_out, r_up0, r_down0, r_pool, r_up1, r_down1]
    moments = [(m_w_in_0, v_w_in_0), (m_w_out_0, v_w_out_0), (m_w_up_0, v_w_up_0),
               (m_w_down_0, v_w_down_0), (m_pool_w_1, v_pool_w_1), (m_w_up_1, v_w_up_1),
               (m_w_down_1, v_w_down_1)]
    big_out = []
    for k, (w, g, (m, v)) in enumerate(zip(big, reduced, moments)):
        if w.shape[-1] % 128:
            view = lambda t: t.reshape(-1, t.shape[-1]).T
            back = lambda t: t.T.reshape(w.shape)
            g_view = g[:w.shape[-1]]
        else:
            view = lambda t: t.reshape(-1, t.shape[-1])
            back = lambda t: t.reshape(w.shape)
            g_view = view(g)
        delta_w, new_m, new_v = _adamw(view(w), g_view, view(m), view(v), "adamw_%d" % k)
        big_out.append((back(g_view), back(delta_w), back(new_m), back(new_v)))

    tail = jnp.concatenate([d_conv[0:3].reshape(-1)[d:], d_bf[0, :N_HEADS], loss_part[0, :1]])
    small_part = jnp.concatenate(
        [d_nmix0, d_nffn0, d_nmix1, d_pscale, d_nffn1, d_final,
         d_conv[0:3].reshape(1, -1)[:, :d],
         jnp.pad(tail, (0, d - tail.shape[0])).reshape(1, d)], axis=0)
    parts = _gather_small(small_part)

    chip = 2 * lax.axis_index("x") + lax.axis_index("y")
    cw_cols = conv_w_0.shape[1]

    def conv_block(full):
        mine = lax.dynamic_slice_in_dim(full, chip * cw_cols, cw_cols, axis=1)
        return jnp.pad(mine.reshape(-1), (0, d - mine.size))

    def small_rows(vals, cw, bfv):
        return jnp.stack(list(vals) + [cw, jnp.pad(bfv, (0, d - N_HEADS))])

    smalls_w = [norm_mix_0, norm_ffn_0, norm_mix_1, pool_scale_1, norm_ffn_1, final_norm]
    smalls_m = [m_norm_mix_0, m_norm_ffn_0, m_norm_mix_1, m_pool_scale_1, m_norm_ffn_1, m_final_norm]
    smalls_v = [v_norm_mix_0, v_norm_ffn_0, v_norm_mix_1, v_pool_scale_1, v_norm_ffn_1, v_final_norm]
    pad_cw = lambda t: jnp.pad(t.reshape(-1), (0, d - t.size))
    w_rows = small_rows(smalls_w, pad_cw(conv_w_0), b_f_0)
    m_rows = small_rows(smalls_m, pad_cw(m_conv_w_0), m_b_f_0)
    v_rows = small_rows(smalls_v, pad_cw(v_conv_w_0), v_b_f_0)

    g_sum = _sum_devices(parts)
    conv_full = jnp.concatenate([g_sum[6], g_sum[7, :3 * c_conv - d]]).reshape(3, c_conv)
    bf_grad = g_sum[7, 3 * c_conv - d:3 * c_conv - d + N_HEADS]
    loss = g_sum[7, 3 * c_conv - d + N_HEADS]
    g_rows = jnp.concatenate(
        [g_sum[0:6], conv_block(conv_full).reshape(1, d),
         jnp.pad(bf_grad, (0, d - N_HEADS)).reshape(1, d)], axis=0)
    d_rows, nm_rows, nv_rows = _adamw(w_rows, g_rows, m_rows, v_rows, "adamw_small")

    def unpack(rows):
        cw = rows[6, :conv_w_0.size].reshape(conv_w_0.shape)
        return [rows[0], rows[1], rows[2], rows[3], rows[4], rows[5], cw, rows[7, :N_HEADS]]

    def assemble(kind):
        sm = unpack([g_rows, d_rows, nm_rows, nv_rows][kind])
        lg = [t[kind] for t in big_out]
        return [sm[0], lg[0], sm[7], sm[6], lg[1], sm[1], lg[2], lg[3],
                sm[2], lg[4], sm[3], sm[4], lg[5], lg[6], sm[5]]

    return (loss, grad_x[None], *assemble(0), *assemble(1), *assemble(2), *assemble(3))
```

```python
import functools

import jax
import jax.numpy as jnp
from jax import lax
from jax.experimental import pallas as pl
from jax.experimental.pallas import tpu as pltpu

F32 = jnp.float32
BF16 = jnp.bfloat16

RMS_EPS = 1e-6
HEAD_DIM = 64
N_HEADS = 8
ATTN_SCALE = HEAD_DIM ** -0.5
LOG2_E = 1.4426950408889634
POOL_WINDOWS = (2, 4, 8, 16)
POOL_HALO = 16
CONV_HALO = 8
NEG_BIG = -1e30

ADAM_LR = 0.001
ADAM_B1 = 0.9
ADAM_B2 = 0.999
ADAM_EPS = 1e-08
ADAM_WD = 0.01
ADAM_STEP = 10

N_CHIPS = 4
N_DEV = 8
MESH = pl.DeviceIdType.MESH

VMEM_LIMIT_BYTES = 56 * 1024 * 1024

TILE_ROWS = 512
TILE_ATTN = 512
TILE_MLP_ROWS = 1024
TILE_MLP_FF = 1024
TILE_MLP_BWD_FF = 512
TILE_WGRAD_K = 1024
TILE_WGRAD_N = 1024
TILE_ELEM_ROWS = 256
SUM_CHUNK_ROWS = 128

LANE_CQ = 64
LANE_CK = 88


def _params(semantics):
    return pltpu.CompilerParams(dimension_semantics=semantics,
                                vmem_limit_bytes=VMEM_LIMIT_BYTES)


def _nn(a, b):
    return lax.dot_general(a, b, (((1,), (0,)), ((), ())), preferred_element_type=F32)


def _nt(a, b):
    return lax.dot_general(a, b, (((1,), (1,)), ((), ())), preferred_element_type=F32)


def _tn(a, b):
    return lax.dot_general(a, b, (((0,), (0,)), ((), ())), preferred_element_type=F32)


def _split3(v):
    hi = v.astype(BF16)
    r1 = v - hi.astype(F32)
    mid = r1.astype(BF16)
    lo = (r1 - mid.astype(F32)).astype(BF16)
    return hi, mid, lo


def _exact_nn(sel, v):
    hi, mid, lo = _split3(v)
    return _nn(sel, hi) + _nn(sel, mid) + _nn(sel, lo)


def _exact_nt(sel, v):
    hi, mid, lo = _split3(v)
    return _nt(sel, hi) + _nt(sel, mid) + _nt(sel, lo)


def _rms_fwd(x, g):
    r = lax.rsqrt(jnp.mean(x * x, axis=-1, keepdims=True) + RMS_EPS)
    return x * r * g, r


def _rms_bwd(dn, x, g):
    r = lax.rsqrt(jnp.mean(x * x, axis=-1, keepdims=True) + RMS_EPS)
    xh = x * r
    gy = dn * g
    dx = r * (gy - xh * jnp.mean(gy * xh, axis=-1, keepdims=True))
    return dx, jnp.sum(dn * xh, axis=0, keepdims=True)


def _lane(shape):
    return lax.broadcasted_iota(jnp.int32, shape, len(shape) - 1)


def _row(shape):
    return lax.broadcasted_iota(jnp.int32, shape, len(shape) - 2)


def _full(a):
    nd = a.ndim
    return pl.BlockSpec(a.shape, lambda *_: (0,) * nd)


def _rms_pre(after, x, g):
    s, d = x.shape
    tm = min(TILE_ROWS, s)

    def body(after_ref, x_ref, g_ref, n_ref):
        n, _ = _rms_fwd(x_ref[...], g_ref[...])
        n_ref[...] = n.astype(BF16)

    rows = pl.BlockSpec((tm, d), lambda i: (i, 0))
    return pl.pallas_call(
        body, name="rms_pre", grid=(s // tm,),
        in_specs=[ANY, rows, _full(g)], out_specs=rows,
        out_shape=jax.ShapeDtypeStruct((s, d), BF16),
        compiler_params=_params(("parallel",)),
    )(after, x, g)


def _in_proj(n, w_qkv, w_f, w_bcx):
    s, d = n.shape
    tm = min(TILE_ROWS, s)

    def body(n_ref, wq_ref, wf_ref, wb_ref, qkv_ref, fl_ref, bcx_ref):
        nb = n_ref[...]
        qkv_ref[...] = _nn(nb, wq_ref[...]).astype(BF16)
        fl_ref[...] = _nn(nb, wf_ref[...])
        bcx_ref[...] = _nn(nb, wb_ref[...])

    rows = lambda c: pl.BlockSpec((tm, c), lambda i: (i, 0))
    return pl.pallas_call(
        body, name="in_proj", grid=(s // tm,),
        in_specs=[rows(d), _full(w_qkv), _full(w_f), _full(w_bcx)],
        out_specs=[rows(w_qkv.shape[1]), rows(w_f.shape[1]), rows(w_bcx.shape[1])],
        out_shape=[jax.ShapeDtypeStruct((s, w_qkv.shape[1]), BF16),
                   jax.ShapeDtypeStruct((s, w_f.shape[1]), F32),
                   jax.ShapeDtypeStruct((s, w_bcx.shape[1]), F32)],
        compiler_params=_params(("parallel",)),
    )(n, w_qkv, w_f, w_bcx)


def _gate_prep(fl, bf, qkv):
    s = fl.shape[0]
    a = N_HEADS * HEAD_DIM
    tm = min(TILE_ROWS, s)

    def body(fl_ref, bf_ref, q_ref, k_ref, qa_ref, ka_ref, carry_ref):
        i = pl.program_id(0)

        @pl.when(i == 0)
        def _():
            carry_ref[...] = jnp.zeros_like(carry_ref)

        z = fl_ref[...] + bf_ref[...]
        logf = jnp.minimum(z, 0.0) - jnp.log(1.0 + jnp.exp(-jnp.abs(z)))
        lower = (_lane((tm, tm)) <= _row((tm, tm))).astype(BF16)
        cum = _exact_nn(lower, logf) + carry_ref[0:1, :]
        carry_ref[0:1, :] = cum[tm - 1:tm, :]

        lane = _lane((tm, 128))
        pieces = [p.astype(F32)
                  for p in _split3(jnp.where(lane < N_HEADS, LOG2_E * cum, 0.0))]
        shared_q = sum(pltpu.roll(p, LANE_CQ + N_HEADS * k, axis=1) for k, p in enumerate(pieces))
        shared_k = -sum(pltpu.roll(p, LANE_CK + N_HEADS * k, axis=1) for k, p in enumerate(pieces))
        for h in range(N_HEADS):
            at_q = functools.reduce(jnp.logical_or,
                                    [lane == LANE_CQ + N_HEADS * k + h for k in range(3)])
            at_k = functools.reduce(jnp.logical_or,
                                    [lane == LANE_CK + N_HEADS * k + h for k in range(3)])
            pair = slice((h // 2) * 128, (h // 2 + 1) * 128)
            qp = q_ref[:, pair].astype(F32)
            kp = k_ref[:, pair].astype(F32)
            if h % 2:
                qp = pltpu.roll(qp, HEAD_DIM, axis=1)
                kp = pltpu.roll(kp, HEAD_DIM, axis=1)
            q_bias = jnp.where(at_k, 1.0, shared_q)
            k_bias = jnp.where(at_q, 1.0, shared_k)
            qa_ref[h] = jnp.where(lane < HEAD_DIM, qp * (ATTN_SCALE * LOG2_E), q_bias).astype(BF16)
            ka_ref[h] = jnp.where(lane < HEAD_DIM, kp, k_bias).astype(BF16)

    aug = jax.ShapeDtypeStruct((N_HEADS, s, 128), BF16)
    aug_spec = pl.BlockSpec((N_HEADS, tm, 128), lambda i: (0, i, 0))
    return pl.pallas_call(
        body, name="gate_prep", grid=(s // tm,),
        in_specs=[pl.BlockSpec((tm, 128), lambda i: (i, 0)), _full(bf),
                  pl.BlockSpec((tm, a), lambda i: (i, 0)),
                  pl.BlockSpec((tm, a), lambda i: (i, 1))],
        out_specs=[aug_spec, aug_spec],
        out_shape=[aug, aug],
        scratch_shapes=[pltpu.VMEM((8, 128), F32)],
        compiler_params=_params(("arbitrary",)),
    )(fl, bf, qkv, qkv)


def _attn_fwd(qa, ka, qkv):
    s = qa.shape[1]
    a = N_HEADS * HEAD_DIM
    t = min(TILE_ATTN, s)
    n_pairs = N_HEADS // 2
    v_block0 = 2 * a // 128

    ones_lane = (HEAD_DIM, 0)

    def body(qa_ref, ka_ref, v_ref, o_ref, lse_ref, m_ref, acc_ref, s_even, s_odd):
        i = pl.program_id(1)
        m_ref[...] = jnp.full_like(m_ref, NEG_BIG)
        acc_ref[...] = jnp.zeros_like(acc_ref)
        upper_rows = _row((128, t)) < HEAD_DIM

        def keys(j):
            return pl.ds(pl.multiple_of(j * t, t), t)

        def scores_into(buf, j):
            for e in range(2):
                buf[e] = _nt(ka_ref[e, keys(j), :], qa_ref[e])

        def consume(buf, j, masked):
            vf = v_ref[keys(j), :].astype(F32)
            lane = _lane((t, 128))
            own = [lane < HEAD_DIM, lane >= HEAD_DIM]
            for e in range(2):
                v_head = jnp.where(own[e], vf, jnp.where(lane == ones_lane[e], 1.0, 0.0)).astype(BF16)
                sc = buf[e]
                if masked:
                    sc = jnp.where(_row((t, t)) <= _lane((t, t)), sc, NEG_BIG)
                m_prev = m_ref[e]
                m_new = jnp.maximum(m_prev, jnp.max(sc, axis=0, keepdims=True))
                p = jnp.exp2(sc - m_new).astype(BF16)
                acc_ref[e] = acc_ref[e] * jnp.exp2(m_prev - m_new) + _tn(v_head, p)
                m_ref[e] = m_new

        scores_into(s_even, 0)

        def two_tiles(p, carry):
            j = 2 * p
            scores_into(s_odd, j + 1)
            consume(s_even, j, False)
            scores_into(s_even, j + 2)
            consume(s_odd, j + 1, False)
            return carry

        lax.fori_loop(0, i // 2, two_tiles, 0)

        @pl.when(i % 2 == 0)
        def _():
            consume(s_even, i, True)

        @pl.when(i % 2 == 1)
        def _():
            scores_into(s_odd, i)
            consume(s_even, i - 1, False)
            consume(s_odd, i, True)

        denom = [acc_ref[e, ones_lane[e]:ones_lane[e] + 1, :] for e in range(2)]
        out_t = jnp.where(upper_rows, acc_ref[0] / denom[0], acc_ref[1] / denom[1])
        o_ref[...] = out_t.T.astype(BF16)
        lse = [m_ref[e] + LOG2_E * jnp.log(denom[e]) for e in range(2)]
        lse_ref[...] = jnp.where(_row((8, t)) == 0, lse[0], lse[1])

    return pl.pallas_call(
        body, name="attn_fwd", grid=(n_pairs, s // t),
        in_specs=[pl.BlockSpec((2, t, 128), lambda g, i: (g, i, 0)),
                  pl.BlockSpec((2, s, 128), lambda g, i: (g, 0, 0)),
                  pl.BlockSpec((s, 128), lambda g, i: (0, v_block0 + g))],
        out_specs=[pl.BlockSpec((t, 128), lambda g, i: (i, g)),
                   pl.BlockSpec((None, 8, t), lambda g, i: (g, 0, i))],
        out_shape=[jax.ShapeDtypeStruct((s, a), BF16),
                   jax.ShapeDtypeStruct((n_pairs, 8, s), F32)],
        scratch_shapes=[pltpu.VMEM((2, 1, t), F32), pltpu.VMEM((2, 128, t), F32),
                        pltpu.VMEM((2, t, t), F32), pltpu.VMEM((2, t, t), F32)],
        compiler_params=_params(("parallel", "arbitrary")),
    )(qa, ka, qkv)


def _conv_out(o, bcx, cw, w_out, x):
    s, d = x.shape
    c = o.shape[1]
    tm = min(TILE_ROWS, s)

    def body(o_ref, b_ref, c_ref, xin_ref, cw_ref, w_ref, x_ref, h_ref, ubuf):
        i = pl.program_id(0)

        @pl.when(i == 0)
        def _():
            ubuf[0:CONV_HALO, :] = jnp.zeros((CONV_HALO, c), F32)

        u = c_ref[...] * xin_ref[...]
        ubuf[CONV_HALO:CONV_HALO + tm, :] = u
        u1 = ubuf[CONV_HALO - 1:CONV_HALO - 1 + tm, :]
        u2 = ubuf[CONV_HALO - 2:CONV_HALO - 2 + tm, :]
        cv = (cw_ref[0:1, :] * u2 + cw_ref[1:2, :] * u1) + cw_ref[2:3, :] * u
        y = (b_ref[...] * cv).astype(BF16)
        mix = _nn(o_ref[...], w_ref[0:c, :]) + _nn(y, w_ref[c:2 * c, :])
        h_ref[...] = x_ref[...] + mix
        ubuf[0:CONV_HALO, :] = u[tm - CONV_HALO:tm, :]

    col = lambda k: pl.BlockSpec((tm, c), lambda i: (i, k))
    return pl.pallas_call(
        body, name="conv_out", grid=(s // tm,),
        in_specs=[col(0), col(0), col(1), col(2), _full(cw), _full(w_out),
                  pl.BlockSpec((tm, d), lambda i: (i, 0))],
        out_specs=pl.BlockSpec((tm, d), lambda i: (i, 0)),
        out_shape=jax.ShapeDtypeStruct((s, d), F32),
        scratch_shapes=[pltpu.VMEM((tm + CONV_HALO, c), F32)],
        compiler_params=_params(("arbitrary",)),
    )(o, bcx, bcx, bcx, cw, w_out, x)


def _mlp_fwd(h, g, w_up, w_down, name, head=None):
    s, d = h.shape
    ff = w_down.shape[0]
    slot_cols = w_up.shape[2]
    tm = min(TILE_MLP_ROWS, s)
    tf = min(TILE_MLP_FF if head is None else TILE_MLP_FF // 2, slot_cols)
    per_slot = slot_cols // tf
    nf = ff // tf
    n_head = 0 if head is None else 2

    def body(*refs):
        h_ref, g_ref, wu_ref, wd_ref = refs[:4]
        out_ref, a_ref, n_ref = refs[4 + n_head:7 + n_head]
        nb_ref, acc_ref = refs[-2:]
        i = pl.program_id(0)
        f = pl.program_id(1)

        @pl.when(f == 0)
        def _():
            n, _ = _rms_fwd(h_ref[...], g_ref[...])
            nb = n.astype(BF16)
            nb_ref[...] = nb
            n_ref[...] = nb
            acc_ref[...] = jnp.zeros_like(acc_ref)

        pre = _nn(nb_ref[...], wu_ref[...])
        a_ref[...] = pre.astype(BF16)
        r = jnp.square(jnp.maximum(pre, 0.0)).astype(BF16)
        acc_ref[...] += _nn(r, wd_ref[...])

        @pl.when(f == nf - 1)
        def _():
            out = h_ref[...] + acc_ref[...]
            if head is None:
                out_ref[...] = out
            else:
                gf_ref, t_ref = refs[4:6]
                loss_ref, dg_ref = refs[7 + n_head:9 + n_head]
                y, _ = _rms_fwd(out, gf_ref[...])
                err = y - t_ref[...]
                part = 0.5 * jnp.sum(jnp.mean(err * err, axis=-1, keepdims=True), axis=0,
                                     keepdims=True)
                dx, dg = _rms_bwd(err / d, out, gf_ref[...])
                out_ref[...] = dx
                part = jnp.broadcast_to(part, loss_ref.shape)

                @pl.when(i == 0)
                def _():
                    loss_ref[...] = part
                    dg_ref[...] = dg

                @pl.when(i > 0)
                def _():
                    loss_ref[...] += part
                    dg_ref[...] += dg

    rows = pl.BlockSpec((tm, d), lambda i, f: (i, 0))
    in_specs = [rows, _full(g),
                pl.BlockSpec((None, d, tf), lambda i, f: (f // per_slot, 0, f % per_slot)),
                pl.BlockSpec((tf, d), lambda i, f: (f, 0))]
    out_specs = [rows, pl.BlockSpec((tm, tf), lambda i, f: (i, f)), rows]
    out_shape = [jax.ShapeDtypeStruct((s, d), F32), jax.ShapeDtypeStruct((s, ff), BF16),
                 jax.ShapeDtypeStruct((s, d), BF16)]
    args = [h, g, w_up, w_down]
    if head is not None:
        in_specs += [_full(head[0]), rows]
        args += list(head)
        out_specs += [pl.BlockSpec((1, 128), lambda i, f: (0, 0)),
                      pl.BlockSpec((1, d), lambda i, f: (0, 0))]
        out_shape += [jax.ShapeDtypeStruct((1, 128), F32), jax.ShapeDtypeStruct((1, d), F32)]
    return pl.pallas_call(
        body, name=name, grid=(s // tm, nf),
        in_specs=in_specs, out_specs=out_specs, out_shape=out_shape,
        scratch_shapes=[pltpu.VMEM((tm, d), BF16), pltpu.VMEM((tm, d), F32)],
        compiler_params=_params(("parallel" if head is None else "arbitrary", "arbitrary")),
    )(*args)


def _window_sum_down(e, window):
    step = 1
    while step < window:
        e = e + pltpu.roll(e, step, axis=0)
        step *= 2
    return e


def _window_sum_up(e, window):
    n = e.shape[0]
    step = 1
    while step < window:
        e = e + pltpu.roll(e, n - step, axis=0)
        step *= 2
    return e


def _pool_counts(first_row, tm, window):
    t = first_row + _row((tm, 1))
    return jnp.minimum(t + 1, window).astype(F32)


def _pool_fwd(h, g, pw, ps):
    s, d = h.shape
    cg = d // len(POOL_WINDOWS)
    tm = min(TILE_ROWS, s)

    def body(h_ref, g_ref, pw_ref, ps_ref, out_ref, nbuf):
        i = pl.program_id(0)

        @pl.when(i == 0)
        def _():
            nbuf[0:POOL_HALO, :] = jnp.zeros((POOL_HALO, d), F32)

        n, _ = _rms_fwd(h_ref[...], g_ref[...])
        nbuf[POOL_HALO:POOL_HALO + tm, :] = n
        for k, window in enumerate(POOL_WINDOWS):
            cols = slice(k * cg, (k + 1) * cg)
            sums = _window_sum_down(nbuf[:, cols], window)[POOL_HALO:, :]
            pooled = sums / _pool_counts(i * tm, tm, window) - n[:, cols]
            y = _nn(pooled.astype(BF16), pw_ref[k]) * ps_ref[:, cols]
            out_ref[:, cols] = h_ref[:, cols] + y
        nbuf[0:POOL_HALO, :] = n[tm - POOL_HALO:tm, :]

    return pl.pallas_call(
        body, name="pool_fwd", grid=(s // tm,),
        in_specs=[pl.BlockSpec((tm, d), lambda i: (i, 0)), _full(g), _full(pw), _full(ps)],
        out_specs=pl.BlockSpec((tm, d), lambda i: (i, 0)),
        out_shape=jax.ShapeDtypeStruct((s, d), F32),
        scratch_shapes=[pltpu.VMEM((tm + POOL_HALO, d), F32)],
        compiler_params=_params(("arbitrary",)),
    )(h, g, pw, ps)


def _mlp_bwd_x(dz, a, w_up, w_down, h_in, g, name):
    s, d = dz.shape
    ff = w_down.shape[0]
    slot_cols = w_up.shape[2]
    tm = min(TILE_MLP_ROWS, s)
    tf = min(TILE_MLP_BWD_FF, slot_cols)
    per_slot = slot_cols // tf
    nf = ff // tf

    def body(dz_ref, a_ref, wu_ref, wd_ref, h_ref, g_ref, da_ref, dzb_ref, dh_ref, dg_ref,
             dzs_ref, acc_ref):
        i = pl.program_id(0)
        f = pl.program_id(1)

        @pl.when(f == 0)
        def _():
            dzb = dz_ref[...].astype(BF16)
            dzs_ref[...] = dzb
            dzb_ref[...] = dzb
            acc_ref[...] = jnp.zeros_like(acc_ref)

        dr = _nt(dzs_ref[...], wd_ref[...])
        da = (dr * (2.0 * jnp.maximum(a_ref[...].astype(F32), 0.0))).astype(BF16)
        da_ref[...] = da
        acc_ref[...] += _nt(da, wu_ref[...])

        @pl.when(f == nf - 1)
        def _():
            dx, dg = _rms_bwd(acc_ref[...], h_ref[...], g_ref[...])
            dh_ref[...] = dz_ref[...] + dx

            @pl.when(i == 0)
            def _():
                dg_ref[...] = dg

            @pl.when(i > 0)
            def _():
                dg_ref[...] += dg

    return pl.pallas_call(
        body, name=name, grid=(s // tm, nf),
        in_specs=[pl.BlockSpec((tm, d), lambda i, f: (i, 0)),
                  pl.BlockSpec((tm, tf), lambda i, f: (i, f)),
                  pl.BlockSpec((None, d, tf), lambda i, f: (f // per_slot, 0, f % per_slot)),
                  pl.BlockSpec((tf, d), lambda i, f: (f, 0)),
                  pl.BlockSpec((tm, d), lambda i, f: (i, 0)), _full(g)],
        out_specs=[pl.BlockSpec((tm, tf), lambda i, f: (i, f)),
                   pl.BlockSpec((tm, d), lambda i, f: (i, 0)),
                   pl.BlockSpec((tm, d), lambda i, f: (i, 0)),
                   pl.BlockSpec((1, d), lambda i, f: (0, 0))],
        out_shape=[jax.ShapeDtypeStruct((s, ff), BF16),
                   jax.ShapeDtypeStruct((s, d), BF16),
                   jax.ShapeDtypeStruct((s, d), F32),
                   jax.ShapeDtypeStruct((1, d), F32)],
        scratch_shapes=[pltpu.VMEM((tm, d), BF16), pltpu.VMEM((tm, d), F32)],
        compiler_params=_params(("arbitrary", "arbitrary")),
    )(dz, a, w_up, w_down, h_in, g)


def _mlp_bwd_w(n, da, a, dzb, slot_cols, name):
    s, d = n.shape
    ff = a.shape[1]
    tn = min(TILE_WGRAD_N, slot_cols)
    tk = min(TILE_WGRAD_K, s)
    per_slot = slot_cols // tn
    nk = s // tk

    def body(n_ref, da_ref, a_ref, dz_ref, du_ref, dd_ref, accu_ref, accd_ref):
        k = pl.program_id(1)

        @pl.when(k == 0)
        def _():
            accu_ref[...] = jnp.zeros_like(accu_ref)
            accd_ref[...] = jnp.zeros_like(accd_ref)

        accu_ref[...] += _tn(n_ref[...], da_ref[...])
        r = jnp.square(jnp.maximum(a_ref[...].astype(F32), 0.0)).astype(BF16)
        accd_ref[...] += _tn(r, dz_ref[...])

        @pl.when(k == nk - 1)
        def _():
            du_ref[...] = accu_ref[...].astype(BF16)
            dd_ref[...] = accd_ref[...].astype(BF16)

    return pl.pallas_call(
        body, name=name, grid=(ff // tn, nk),
        in_specs=[pl.BlockSpec((tk, d), lambda f, k: (k, 0)),
                  pl.BlockSpec((tk, tn), lambda f, k: (k, f)),
                  pl.BlockSpec((tk, tn), lambda f, k: (k, f)),
                  pl.BlockSpec((tk, d), lambda f, k: (k, 0))],
        out_specs=[pl.BlockSpec((None, d, tn), lambda f, k: (f // per_slot, 0, f % per_slot)),
                   pl.BlockSpec((tn, d), lambda f, k: (f, 0))],
        out_shape=[jax.ShapeDtypeStruct((ff // slot_cols, d, slot_cols), BF16),
                   jax.ShapeDtypeStruct((ff, d), BF16)],
        scratch_shapes=[pltpu.VMEM((d, tn), F32), pltpu.VMEM((tn, d), F32)],
        compiler_params=_params(("parallel", "arbitrary")),
    )(n, da, a, dzb)


def _pool_bwd(after, dh, h, g, pw, ps):
    s, d = h.shape
    cg = d // len(POOL_WINDOWS)
    tm = min(TILE_ROWS, s)
    nb = s // tm
    halo_per_tile = tm // POOL_HALO

    def body(after_ref, dh_ref, h_ref, halo_ref, g_ref, pw_ref, ps_ref,
             dx_ref, dpw_ref, dps_ref, dg_ref, nbuf, qbuf, dn_ref, carry, dpw_acc):
        i = pl.program_id(0)
        blk = nb - 1 - i

        @pl.when(i == 0)
        def _():
            carry[...] = jnp.zeros_like(carry)
            dpw_acc[...] = jnp.zeros_like(dpw_acc)
            dps_ref[...] = jnp.zeros_like(dps_ref)
            dg_ref[...] = jnp.zeros_like(dg_ref)

        hv = h_ref[...]
        n, _ = _rms_fwd(hv, g_ref[...])
        nh, _ = _rms_fwd(halo_ref[...], g_ref[...])
        nbuf[0:POOL_HALO, :] = jnp.where(blk == 0, 0.0, nh)
        nbuf[POOL_HALO:POOL_HALO + tm, :] = n
        dhv = dh_ref[...]
        for k, window in enumerate(POOL_WINDOWS):
            cols = slice(k * cg, (k + 1) * cg)
            cnt = _pool_counts(blk * tm, tm, window)
            sums = _window_sum_down(nbuf[:, cols], window)[POOL_HALO:, :]
            pb = (sums / cnt - n[:, cols]).astype(BF16)
            dyk = dhv[:, cols]
            dps_ref[:, cols] += jnp.sum(dyk * _nn(pb, pw_ref[k]), axis=0, keepdims=True)
            dyb = (dyk * ps_ref[:, cols]).astype(BF16)
            dpw_acc[k] += _tn(pb, dyb)
            dpool = _nt(dyb, pw_ref[k])
            qv = dpool / cnt
            qbuf[0:tm, cols] = qv
            qbuf[tm:tm + POOL_HALO, cols] = carry[:, cols]
            dn_ref[:, cols] = _window_sum_up(qbuf[:, cols], window)[0:tm, :] - dpool
            carry[:, cols] = qv[0:POOL_HALO, :]
        dx, dg = _rms_bwd(dn_ref[...], hv, g_ref[...])
        dx_ref[...] = dhv + dx
        dg_ref[...] += dg

        @pl.when(i == nb - 1)
        def _():
            dpw_ref[...] = dpw_acc[...].astype(BF16)

    rev = lambda i: (nb - 1 - i, 0)
    return pl.pallas_call(
        body, name="pool_bwd", grid=(nb,),
        in_specs=[ANY, pl.BlockSpec((tm, d), rev), pl.BlockSpec((tm, d), rev),
                  pl.BlockSpec((POOL_HALO, d),
                               lambda i: (jnp.maximum((nb - 1 - i) * halo_per_tile - 1, 0), 0)),
                  _full(g), _full(pw), _full(ps)],
        out_specs=[pl.BlockSpec((tm, d), rev), _full(pw),
                   pl.BlockSpec((1, d), lambda i: (0, 0)),
                   pl.BlockSpec((1, d), lambda i: (0, 0))],
        out_shape=[jax.ShapeDtypeStruct((s, d), F32),
                   jax.ShapeDtypeStruct(pw.shape, BF16),
                   jax.ShapeDtypeStruct((1, d), F32),
                   jax.ShapeDtypeStruct((1, d), F32)],
        scratch_shapes=[pltpu.VMEM((tm + POOL_HALO, d), F32), pltpu.VMEM((tm + POOL_HALO, d), F32),
                        pltpu.VMEM((tm, d), F32), pltpu.VMEM((POOL_HALO, d), F32),
                        pltpu.VMEM(pw.shape, F32)],
        compiler_params=_params(("arbitrary",)),
    )(after, dh, h, h, g, pw, ps)


def _conv_out_bwd(after, dh, w_out, o, bcx, cw):
    s, d = dh.shape
    c = o.shape[1]
    tm = min(TILE_ROWS, s)
    nb = s // tm
    halo_per_tile = tm // CONV_HALO

    def body(after_ref, dh_ref, w_ref, o_ref, b_ref, c_ref, xin_ref, ch_ref, xh_ref, cw_ref,
             do_ref, delta_ref, dbcx_ref, dw_ref, dcw_ref, ubuf, dbuf, carry, acc):
        i = pl.program_id(0)
        blk = nb - 1 - i

        @pl.when(i == 0)
        def _():
            carry[...] = jnp.zeros_like(carry)
            acc[...] = jnp.zeros_like(acc)
            dcw_ref[...] = jnp.zeros_like(dcw_ref)

        dm = dh_ref[...].astype(BF16)
        dcat = _nt(dm, w_ref[...])
        do = dcat[:, 0:c]
        dy = dcat[:, c:2 * c]
        do_ref[...] = do.astype(BF16)
        head_of_lane = lax.shift_right_logical(_lane((8, c)), HEAD_DIM.bit_length() - 1)
        heads = (head_of_lane == _row((8, c))).astype(BF16)
        delta_ref[...] = _exact_nt(heads, do * o_ref[...].astype(F32))

        cv_ = c_ref[...]
        xin = xin_ref[...]
        bv = b_ref[...]
        u = cv_ * xin
        ubuf[0:CONV_HALO, :] = jnp.where(blk == 0, 0.0, ch_ref[...] * xh_ref[...])
        ubuf[CONV_HALO:CONV_HALO + tm, :] = u
        u1 = ubuf[CONV_HALO - 1:CONV_HALO - 1 + tm, :]
        u2 = ubuf[CONV_HALO - 2:CONV_HALO - 2 + tm, :]
        w0, w1, w2 = cw_ref[0:1, :], cw_ref[1:2, :], cw_ref[2:3, :]
        cv = (w0 * u2 + w1 * u1) + w2 * u
        acc[0:c, :] += _tn(o_ref[...], dm)
        acc[c:2 * c, :] += _tn((bv * cv).astype(BF16), dm)

        dcv = dy * bv
        dcw_ref[0:1, :] += jnp.sum(dcv * u2, axis=0, keepdims=True)
        dcw_ref[1:2, :] += jnp.sum(dcv * u1, axis=0, keepdims=True)
        dcw_ref[2:3, :] += jnp.sum(dcv * u, axis=0, keepdims=True)
        dbuf[0:tm, :] = dcv
        dbuf[tm:tm + CONV_HALO, :] = carry[...]
        du = w2 * dcv + w1 * dbuf[1:1 + tm, :] + w0 * dbuf[2:2 + tm, :]
        dbcx_ref[:, 0:c] = (dy * cv).astype(BF16)
        dbcx_ref[:, c:2 * c] = (du * xin).astype(BF16)
        dbcx_ref[:, 2 * c:3 * c] = (du * cv_).astype(BF16)
        carry[...] = dcv[0:CONV_HALO, :]

        @pl.when(i == nb - 1)
        def _():
            dw_ref[...] = acc[...].astype(BF16)

    rev = lambda k: (lambda i: (nb - 1 - i, k))
    halo = lambda k: (lambda i: (jnp.maximum((nb - 1 - i) * halo_per_tile - 1, 0), k))
    return pl.pallas_call(
        body, name="conv_out_bwd", grid=(nb,),
        in_specs=[ANY, pl.BlockSpec((tm, d), rev(0)), _full(w_out), pl.BlockSpec((tm, c), rev(0)),
                  pl.BlockSpec((tm, c), rev(0)), pl.BlockSpec((tm, c), rev(1)),
                  pl.BlockSpec((tm, c), rev(2)),
                  pl.BlockSpec((CONV_HALO, c), halo(1)), pl.BlockSpec((CONV_HALO, c), halo(2)),
                  _full(cw)],
        out_specs=[pl.BlockSpec((tm, c), rev(0)),
                   pl.BlockSpec((8, tm), lambda i: (0, nb - 1 - i)),
                   pl.BlockSpec((tm, 3 * c), rev(0)),
                   _full(w_out), _full(cw)],
        out_shape=[jax.ShapeDtypeStruct((s, c), BF16),
                   jax.ShapeDtypeStruct((8, s), F32),
                   jax.ShapeDtypeStruct((s, 3 * c), BF16),
                   jax.ShapeDtypeStruct(w_out.shape, BF16),
                   jax.ShapeDtypeStruct(cw.shape, F32)],
        scratch_shapes=[pltpu.VMEM((tm + CONV_HALO, c), F32), pltpu.VMEM((tm + CONV_HALO, c), F32),
                        pltpu.VMEM((CONV_HALO, c), F32), pltpu.VMEM(w_out.shape, F32)],
        compiler_params=_params(("arbitrary",)),
    )(after, dh, w_out, o, bcx, bcx, bcx, bcx, bcx, cw)


def _attn_bwd(qa, ka, qkv, do, lse, delta):
    s = qa.shape[1]
    a = N_HEADS * HEAD_DIM
    t = min(TILE_ATTN, s)
    nq = s // t
    n_pairs = N_HEADS // 2
    v_block0 = 2 * a // 128

    def body(ka_ref, v_ref, qa_ref, do_ref, lse_ref, delta_ref,
             dqt_ref, dka_ref, dv_ref, dk_acc, dv_acc):
        g = pl.program_id(0)
        j = pl.program_id(1)

        @pl.when(j == 0)
        def _():
            dqt_ref[...] = jnp.zeros_like(dqt_ref)

        dk_acc[...] = jnp.zeros_like(dk_acc)
        dv_acc[...] = jnp.zeros_like(dv_acc)
        lane = _lane((t, 128))
        vf = v_ref[...].astype(F32)
        v_heads = [jnp.where(lane < HEAD_DIM, vf, 0.0).astype(BF16),
                   jnp.where(lane >= HEAD_DIM, vf, 0.0).astype(BF16)]
        ke_t = [ka_ref[e].astype(F32).T.astype(BF16) for e in range(2)]

        def q_step(i, masked):
            qs = pl.ds(pl.multiple_of(i * t, t), t)
            dob = do_ref[qs, :]
            for e in range(2):
                qe = qa_ref[e, qs, :]
                sc = _nt(ka_ref[e], qe)
                if masked:
                    sc = jnp.where(_row((t, t)) <= _lane((t, t)), sc, NEG_BIG)
                p = jnp.exp2(sc - lse_ref[pl.ds(e, 1), qs])
                dv_acc[e] += _nn(p.astype(BF16), dob)
                dp = _nt(v_heads[e], dob)
                ds = (p * (dp - delta_ref[pl.ds(2 * g + e, 1), qs])).astype(BF16)
                dk_acc[e] += _nn(ds, qe)
                dqt_ref[e, :, qs] += _nn(ke_t[e], ds)

        q_step(j, True)

        def full_step(i, carry):
            q_step(i, False)
            return carry

        lax.fori_loop(j + 1, nq, full_step, 0)
        dka_ref[...] = dk_acc[...]
        dv_ref[...] = jnp.where(lane < HEAD_DIM, dv_acc[0], dv_acc[1]).astype(BF16)

    return pl.pallas_call(
        body, name="attn_bwd", grid=(n_pairs, nq),
        in_specs=[pl.BlockSpec((2, t, 128), lambda g, j: (g, j, 0)),
                  pl.BlockSpec((t, 128), lambda g, j: (j, v_block0 + g)),
                  pl.BlockSpec((2, s, 128), lambda g, j: (g, 0, 0)),
                  pl.BlockSpec((s, 128), lambda g, j: (0, g)),
                  pl.BlockSpec((None, 8, s), lambda g, j: (g, 0, 0)),
                  pl.BlockSpec((8, s), lambda g, j: (0, 0))],
        out_specs=[pl.BlockSpec((2, 128, s), lambda g, j: (g, 0, 0)),
                   pl.BlockSpec((2, t, 128), lambda g, j: (g, j, 0)),
                   pl.BlockSpec((t, 128), lambda g, j: (j, g))],
        out_shape=[jax.ShapeDtypeStruct((N_HEADS, 128, s), F32),
                   jax.ShapeDtypeStruct((N_HEADS, s, 128), F32),
                   jax.ShapeDtypeStruct((s, a), BF16)],
        scratch_shapes=[pltpu.VMEM((2, t, 128), F32), pltpu.VMEM((2, t, 128), F32)],
        compiler_params=_params(("parallel", "arbitrary")),
    )(ka, qkv, qa, do, lse, delta)


def _gate_bwd(dqa, dka, dv, fl, bf):
    s = fl.shape[0]
    a = N_HEADS * HEAD_DIM
    tm = min(TILE_ROWS, s)
    nb = s // tm

    def body(dqa_ref, dka_ref, dv_ref, fl_ref, bf_ref, dqkv_ref, dfl_ref, dbf_ref, carry):
        i = pl.program_id(0)

        @pl.when(i == 0)
        def _():
            carry[...] = jnp.zeros_like(carry)
            dbf_ref[...] = jnp.zeros_like(dbf_ref)

        lane = _lane((tm, 128))
        dq_sum = jnp.zeros((tm, 128), F32)
        dk_sum = jnp.zeros((tm, 128), F32)
        for pair in range(N_HEADS // 2):
            qs, ks = [], []
            for e in range(2):
                h = 2 * pair + e
                dq = dqa_ref[h].T
                dk = dka_ref[h]
                dq_sum = dq_sum + dq
                dk_sum = dk_sum + dk
                qs.append(dq * ATTN_SCALE)
                ks.append(dk * (1.0 / LOG2_E))
            cols = slice(pair * 128, (pair + 1) * 128)
            dqkv_ref[:, cols] = jnp.where(
                lane < HEAD_DIM, qs[0], pltpu.roll(qs[1], HEAD_DIM, axis=1)).astype(BF16)
            dqkv_ref[:, a + pair * 128:a + (pair + 1) * 128] = jnp.where(
                lane < HEAD_DIM, ks[0], pltpu.roll(ks[1], HEAD_DIM, axis=1)).astype(BF16)
        dqkv_ref[:, 2 * a:3 * a] = dv_ref[...]

        in_q = (lane >= LANE_CQ) & (lane < LANE_CQ + N_HEADS)
        in_k = (lane >= LANE_CK) & (lane < LANE_CK + N_HEADS)
        dcum = (pltpu.roll(jnp.where(in_q, dq_sum, 0.0), 128 - LANE_CQ, axis=1)
                - pltpu.roll(jnp.where(in_k, dk_sum, 0.0), 128 - LANE_CK, axis=1))

        upper = (_lane((tm, tm)) >= _row((tm, tm))).astype(BF16)
        dlogf = _exact_nn(upper, dcum) + carry[0:1, :]
        carry[0:1, :] = dlogf[0:1, :]
        z = fl_ref[...] + bf_ref[...]
        ez = jnp.exp(-jnp.abs(z))
        sig_neg = jnp.where(z >= 0.0, ez, 1.0) / (1.0 + ez)
        dz = jnp.where(lane < N_HEADS, dlogf * sig_neg, 0.0)
        dfl_ref[...] = dz.astype(BF16)
        dbf_ref[...] += jnp.sum(dz, axis=0, keepdims=True)

    rev3 = lambda i: (0, nb - 1 - i, 0)
    rev = lambda i: (nb - 1 - i, 0)
    return pl.pallas_call(
        body, name="gate_bwd", grid=(nb,),
        in_specs=[pl.BlockSpec((N_HEADS, 128, tm), lambda i: (0, 0, nb - 1 - i)),
                  pl.BlockSpec((N_HEADS, tm, 128), rev3),
                  pl.BlockSpec((tm, a), rev), pl.BlockSpec((tm, 128), rev), _full(bf)],
        out_specs=[pl.BlockSpec((tm, 3 * a), rev), pl.BlockSpec((tm, 128), rev),
                   pl.BlockSpec((1, 128), lambda i: (0, 0))],
        out_shape=[jax.ShapeDtypeStruct((s, 3 * a), BF16),
                   jax.ShapeDtypeStruct((s, 128), BF16),
                   jax.ShapeDtypeStruct((1, 128), F32)],
        scratch_shapes=[pltpu.VMEM((8, 128), F32)],
        compiler_params=_params(("arbitrary",)),
    )(dqa, dka, dv, fl, bf)


def _in_proj_bwd(after, dqkv, dfl, dbcx, w_qkv, w_f, w_bcx, x, g, dh):
    s, d = x.shape
    tm = min(TILE_ROWS, s)

    def body(after_ref, dq_ref, df_ref, db_ref, wq_ref, wf_ref, wb_ref, x_ref, g_ref, dh_ref,
             gx_ref, dg_ref):
        i = pl.program_id(0)
        dn = (_nt(dq_ref[...], wq_ref[...]) + _nt(df_ref[...], wf_ref[...])
              + _nt(db_ref[...], wb_ref[...]))
        dx, dg = _rms_bwd(dn, x_ref[...], g_ref[...])
        gx_ref[...] = dh_ref[...] + dx

        @pl.when(i == 0)
        def _():
            dg_ref[...] = dg

        @pl.when(i > 0)
        def _():
            dg_ref[...] += dg

    rows = lambda c: pl.BlockSpec((tm, c), lambda i: (i, 0))
    return pl.pallas_call(
        body, name="in_proj_bwd", grid=(s // tm,),
        in_specs=[ANY, rows(dqkv.shape[1]), rows(dfl.shape[1]), rows(dbcx.shape[1]),
                  _full(w_qkv), _full(w_f), _full(w_bcx), rows(d), _full(g), rows(d)],
        out_specs=[rows(d), pl.BlockSpec((1, d), lambda i: (0, 0))],
        out_shape=[jax.ShapeDtypeStruct((s, d), F32), jax.ShapeDtypeStruct((1, d), F32)],
        compiler_params=_params(("arbitrary",)),
    )(after, dqkv, dfl, dbcx, w_qkv, w_f, w_bcx, x, g, dh)


def _wgrad_in(n, dys):
    s, d = n.shape
    m = len(dys)
    tk = min(TILE_ROWS, s)
    nk = s // tk

    def body(*refs):
        n_ref, dy_refs, dw_refs, accs = refs[0], refs[1:1 + m], refs[1 + m:1 + 2 * m], refs[1 + 2 * m:]
        k = pl.program_id(0)

        @pl.when(k == 0)
        def _():
            for acc in accs:
                acc[...] = jnp.zeros_like(acc)

        nb = n_ref[...]
        for dy_ref, acc in zip(dy_refs, accs):
            acc[...] += _tn(nb, dy_ref[...])

        @pl.when(k == nk - 1)
        def _():
            for dw_ref, acc in zip(dw_refs, accs):
                dw_ref[...] = acc[...].T.astype(BF16)

    return pl.pallas_call(
        body, name="wgrad_in", grid=(nk,),
        in_specs=[pl.BlockSpec((tk, d), lambda k: (k, 0))]
        + [pl.BlockSpec((tk, dy.shape[1]), lambda k: (k, 0)) for dy in dys],
        out_specs=[pl.BlockSpec((dy.shape[1], d), lambda k: (0, 0)) for dy in dys],
        out_shape=[jax.ShapeDtypeStruct((dy.shape[1], d), BF16) for dy in dys],
        scratch_shapes=[pltpu.VMEM((d, dy.shape[1]), F32) for dy in dys],
        compiler_params=_params(("arbitrary",)),
    )(n, *dys)


def _row_tile(rows):
    t = min(TILE_ELEM_ROWS, rows)
    while rows % t:
        t //= 2
    return t


def _adamw_math(w, g, m, v):
    m = ADAM_B1 * m + (1.0 - ADAM_B1) * g
    v = ADAM_B2 * v + (1.0 - ADAM_B2) * jnp.square(g)
    m_hat = m / (1.0 - ADAM_B1 ** ADAM_STEP)
    v_hat = v / (1.0 - ADAM_B2 ** ADAM_STEP)
    delta = -ADAM_LR * (m_hat / (jnp.sqrt(v_hat) + ADAM_EPS) + ADAM_WD * w)
    return delta, m, v


def _adamw(w, g, m, v, name):
    rows, cols = w.shape

    def body(w_ref, g_ref, m_ref, v_ref, d_ref, nm_ref, nv_ref):
        delta, nm, nv = _adamw_math(w_ref[...], g_ref[...], m_ref[...], v_ref[...])
        d_ref[...] = delta
        nm_ref[...] = nm
        nv_ref[...] = nv

    if rows % 8 == 0:
        tr = _row_tile(rows)
        grid, spec = (rows // tr,), pl.BlockSpec((tr, cols), lambda i: (i, 0))
    else:
        grid, spec = (cols // 256,), pl.BlockSpec((rows, 256), lambda i: (0, i))
    out = jax.ShapeDtypeStruct(w.shape, F32)
    return pl.pallas_call(
        body, name=name, grid=grid, in_specs=[spec] * 4, out_specs=[spec] * 3,
        out_shape=[out, out, out], compiler_params=_params(("parallel",)),
    )(w, g, m, v)


def _sum_devices(parts):
    def body(p_ref, g_ref):
        g = p_ref[0]
        for k in range(1, N_DEV):
            g = g + p_ref[k]
        g_ref[...] = g

    return pl.pallas_call(
        body, name="sum_devices",
        in_specs=[pl.BlockSpec(memory_space=pltpu.VMEM)],
        out_specs=pl.BlockSpec(memory_space=pltpu.VMEM),
        out_shape=jax.ShapeDtypeStruct(parts.shape[1:], F32),
    )(parts)


def _mesh_position():
    x, y, c = lax.axis_index("x"), lax.axis_index("y"), lax.axis_index("c")
    chips = [(1 - x, y), (x, 1 - y), (1 - x, 1 - y)]
    return x, y, c, chips


ANY = pl.BlockSpec(memory_space=pl.ANY)
HBM = pl.BlockSpec(memory_space=pltpu.HBM)
SEM = pl.BlockSpec(memory_space=pltpu.SEMAPHORE)
SPLIT_COPY_EFFECT = pltpu.SideEffectType.DATAFLOW_SIDE_EFFECTING


def _in_hbm(a):
    return pltpu.with_memory_space_constraint(a, pltpu.HBM)


def _chip_copies(views, srcs, lands, send, recv):
    _, _, c, chips = _mesh_position()
    cps = []
    for a in range(len(srcs)):
        for k, (px, py) in enumerate(chips):
            src, dst = views(a, k, srcs[a], lands[a], c, 2 * px + py)
            sem = a * (N_CHIPS - 1) + k
            cps.append(pltpu.make_async_remote_copy(
                src_ref=src, dst_ref=dst, send_sem=send.at[sem], recv_sem=recv.at[sem],
                device_id=(px, py, c), device_id_type=MESH))
    return cps


def _ici_start(sources, land_shapes, views, after, name):
    n = len(sources)

    def body(*refs):
        srcs, lands = refs[:n], refs[n:2 * n]
        send, recv = refs[2 * n + 1], refs[2 * n + 2]
        token = refs[-1]
        for cp in _chip_copies(views, srcs, lands, send, recv):
            cp.start()
        token[...] = jnp.zeros_like(token)

    lands = [_in_hbm(lax.empty(s.shape, s.dtype)) for s in land_shapes]
    outs = pl.pallas_call(
        body, name=name,
        in_specs=[HBM] * (2 * n) + [ANY],
        out_specs=[SEM, SEM] + [HBM] * (2 * n) + [pl.BlockSpec(memory_space=pltpu.VMEM)],
        out_shape=[pltpu.SemaphoreType.DMA((n * (N_CHIPS - 1),))] * 2
        + [pltpu.HBM(a.shape, a.dtype) for a in sources]
        + [pltpu.HBM(s.shape, s.dtype) for s in land_shapes]
        + [jax.ShapeDtypeStruct((8, 128), F32)],
        input_output_aliases={i: 2 + i for i in range(2 * n)},
        compiler_params=pltpu.CompilerParams(has_side_effects=SPLIT_COPY_EFFECT),
    )(*[_in_hbm(a) for a in sources], *lands, after)
    return outs[0], outs[1], list(outs[2:2 + n]), list(outs[2 + n:2 + 2 * n]), outs[-1]


def _ici_wait(handle, views, after, name):
    send, recv, srcs, lands, _ = handle
    n = len(srcs)

    def body(*refs):
        src_refs, land_refs = refs[:n], refs[n:2 * n]
        for cp in _chip_copies(views, src_refs, land_refs, refs[2 * n], refs[2 * n + 1]):
            cp.wait_send()
            cp.wait_recv()

    outs = pl.pallas_call(
        body, name=name,
        in_specs=[HBM] * (2 * n) + [SEM, SEM, ANY],
        out_specs=[HBM] * (2 * n),
        out_shape=[pltpu.HBM(a.shape, a.dtype) for a in srcs]
        + [pltpu.HBM(a.shape, a.dtype) for a in lands],
        input_output_aliases={i: i for i in range(2 * n)},
        compiler_params=pltpu.CompilerParams(has_side_effects=SPLIT_COPY_EFFECT),
    )(*srcs, *lands, send, recv, after)
    return list(outs[:n]), list(outs[n:])


def _gather_views(split):
    def views(a, k, src, land, c, slot):
        if split[a]:
            half = src.shape[0] // 2
            src = src.at[pl.ds(c * half, half)]
        return src, land.at[k]
    return views


def _gather_whole_views(a, k, src, land, c, slot):
    x, y, _, _ = _mesh_position()
    return src, land.at[2 * x + y]


def _scatter_views(a, k, src, land, c, slot):
    return src.at[slot], land.at[k]


def _gather_land_shapes(shards, split):
    return [jax.ShapeDtypeStruct(
        (N_CHIPS - 1, a.shape[0] // 2 if sp else a.shape[0]) + a.shape[1:], a.dtype)
        for a, sp in zip(shards, split)]


def _gather_finish(shards, lands, split, name):
    n = len(shards)
    ns = sum(split)
    d_index = {a: i for i, a in enumerate(a for a in range(n) if split[a])}

    def body(*refs):
        shard, land, outs = refs[:n], refs[n:2 * n], refs[2 * n:3 * n]
        obuf, fbuf = refs[3 * n:4 * n], refs[4 * n:5 * n]
        dbuf = refs[5 * n:5 * n + ns]
        ld_own, st_own, ld, st_mine, st_sib, send, recv = refs[5 * n + ns:]
        x, y, c, chips = _mesh_position()
        me = 2 * x + y
        own_loads, loads, sends, pending = [], {}, [], []
        for a in range(n):
            cp = pltpu.make_async_copy(shard[a], obuf[a], ld_own.at[a])
            cp.start()
            own_loads.append(cp)
        for a in range(n):
            for k in range(N_CHIPS - 1):
                cp = pltpu.make_async_copy(land[a].at[k], fbuf[a].at[k], ld.at[a, k])
                cp.start()
                loads[a, k] = cp
        for a in range(n):
            own_loads[a].wait()
            cp = pltpu.make_async_copy(obuf[a], outs[a].at[me], st_own.at[a])
            cp.start()
            pending.append(cp)
        for a in range(n):
            rows = shard[a].shape[0]
            for k, (px, py) in enumerate(chips):
                loads[a, k].wait()
                part = pl.ds(c * (rows // 2), rows // 2) if split[a] else pl.ds(0, rows)
                cp = pltpu.make_async_copy(fbuf[a].at[k], outs[a].at[2 * px + py, part],
                                           st_mine.at[a, k])
                cp.start()
                pending.append(cp)
                if split[a]:
                    fw = pltpu.make_async_remote_copy(
                        src_ref=fbuf[a].at[k], dst_ref=dbuf[d_index[a]].at[k],
                        send_sem=send.at[a, k], recv_sem=recv.at[a, k],
                        device_id=(x, y, 1 - c), device_id_type=MESH)
                    fw.start()
                    sends.append((a, k, fw))
        for a, k, fw in sends:
            px, py = chips[k]
            half = shard[a].shape[0] // 2
            fw.wait_recv()
            cp = pltpu.make_async_copy(dbuf[d_index[a]].at[k],
                                       outs[a].at[2 * px + py, pl.ds((1 - c) * half, half)],
                                       st_sib.at[a, k])
            cp.start()
            pending.append(cp)
        for _, _, fw in sends:
            fw.wait_send()
        for cp in pending:
            cp.wait()

    stage = [pltpu.VMEM(a.shape, a.dtype) for a in lands]
    dma = lambda *shape: pltpu.SemaphoreType.DMA(shape)
    return pl.pallas_call(
        body, name=name,
        in_specs=[ANY] * (2 * n), out_specs=[ANY] * n,
        out_shape=[jax.ShapeDtypeStruct((N_CHIPS,) + a.shape, a.dtype) for a in shards],
        scratch_shapes=[pltpu.VMEM(a.shape, a.dtype) for a in shards] + stage
        + [s for s, sp in zip(stage, split) if sp]
        + [dma(n), dma(n), dma(n, 3), dma(n, 3), dma(n, 3), dma(n, 3), dma(n, 3)],
        compiler_params=pltpu.CompilerParams(vmem_limit_bytes=VMEM_LIMIT_BYTES),
    )(*shards, *lands)


def _sum_chunk(rows):
    return next(r for r in range(SUM_CHUNK_ROWS, 0, -16) if rows % r == 0)


def _exchange_siblings(grads, name):
    n = len(grads)

    def body(*refs):
        ins, outs = refs[:n], refs[n:2 * n]
        sbuf, rbuf, mbuf = refs[2 * n:3 * n], refs[3 * n:4 * n], refs[4 * n:5 * n]
        ld_send, ld_mine, st, send, recv = refs[5 * n:]
        x, y, c, _ = _mesh_position()
        loads, mine, sends, stores = [], [], [], []
        for a in range(n):
            half = ins[a].shape[1] // 2
            cp = pltpu.make_async_copy(ins[a].at[:, pl.ds((1 - c) * half, half)], sbuf[a],
                                       ld_send.at[a])
            cp.start()
            loads.append(cp)
        for a in range(n):
            half = ins[a].shape[1] // 2
            cp = pltpu.make_async_copy(ins[a].at[:, pl.ds(c * half, half)], mbuf[a], ld_mine.at[a])
            cp.start()
            mine.append(cp)
        for a in range(n):
            loads[a].wait()
            rc = pltpu.make_async_remote_copy(
                src_ref=sbuf[a], dst_ref=rbuf[a], send_sem=send.at[a], recv_sem=recv.at[a],
                device_id=(x, y, 1 - c), device_id_type=MESH)
            rc.start()
            sends.append(rc)
        for a in range(n):
            sends[a].wait_recv()
            mine[a].wait()
            slots, half, _ = rbuf[a].shape
            rows = _sum_chunk(half)
            per_slot = half // rows

            def add(k, carry, a=a, rows=rows, per_slot=per_slot):
                at = (k // per_slot, pl.ds(pl.multiple_of((k % per_slot) * rows, rows), rows))
                rbuf[a][at] = (rbuf[a][at].astype(F32) + mbuf[a][at].astype(F32)).astype(BF16)
                return carry

            lax.fori_loop(0, slots * per_slot, add, 0)
            cp = pltpu.make_async_copy(rbuf[a], outs[a], st.at[a])
            cp.start()
            stores.append(cp)
        for a in range(n):
            sends[a].wait_send()
            stores[a].wait()

    half_shape = lambda a: (a.shape[0], a.shape[1] // 2, a.shape[2])
    stage = [pltpu.VMEM(half_shape(a), a.dtype) for a in grads]
    return pl.pallas_call(
        body, name=name,
        in_specs=[ANY] * n, out_specs=[ANY] * n,
        out_shape=[jax.ShapeDtypeStruct(half_shape(a), a.dtype) for a in grads],
        scratch_shapes=stage * 3 + [pltpu.SemaphoreType.DMA((n,))] * 5,
        compiler_params=pltpu.CompilerParams(vmem_limit_bytes=VMEM_LIMIT_BYTES),
    )(*grads)


def _sum_and_share(sums, got, name):
    n = len(sums)

    def body(*refs):
        own, others, outs = refs[:n], refs[n:2 * n], refs[2 * n:3 * n]
        obuf, gbuf, sbuf, rbuf = (refs[(3 + k) * n:(4 + k) * n] for k in range(4))
        ld_own, ld_got, st_own, st_sib, send, recv = refs[7 * n:]
        x, y, c, _ = _mesh_position()
        loads, sends, stores = [], [], []
        for a in range(n):
            cps = [pltpu.make_async_copy(own[a].at[2 * x + y], obuf[a], ld_own.at[a]),
                   pltpu.make_async_copy(others[a], gbuf[a], ld_got.at[a])]
            for cp in cps:
                cp.start()
            loads.append(cps)
        for a in range(n):
            for cp in loads[a]:
                cp.wait()
            half = obuf[a].shape[0]
            rows = _sum_chunk(half)

            def add(k, carry, a=a, rows=rows):
                at = pl.ds(pl.multiple_of(k * rows, rows), rows)
                acc = obuf[a][at].astype(F32)
                for j in range(N_CHIPS - 1):
                    acc = acc + gbuf[a][j, at].astype(F32)
                sbuf[a][at] = acc
                return carry

            lax.fori_loop(0, half // rows, add, 0)
            rc = pltpu.make_async_remote_copy(
                src_ref=sbuf[a], dst_ref=rbuf[a], send_sem=send.at[a], recv_sem=recv.at[a],
                device_id=(x, y, 1 - c), device_id_type=MESH)
            rc.start()
            sends.append(rc)
            cp = pltpu.make_async_copy(sbuf[a], outs[a].at[pl.ds(c * half, half)], st_own.at[a])
            cp.start()
            stores.append(cp)
        for a in range(n):
            half = obuf[a].shape[0]
            sends[a].wait_recv()
            cp = pltpu.make_async_copy(rbuf[a], outs[a].at[pl.ds((1 - c) * half, half)], st_sib.at[a])
            cp.start()
            stores.append(cp)
        for cp in sends:
            cp.wait_send()
        for cp in stores:
            cp.wait()

    halves = [a.shape[1:] for a in sums]
    return pl.pallas_call(
        body, name=name,
        in_specs=[ANY] * (2 * n), out_specs=[ANY] * n,
        out_shape=[jax.ShapeDtypeStruct((2 * h[0],) + h[1:], F32) for h in halves],
        scratch_shapes=[pltpu.VMEM(h, BF16) for h in halves]
        + [pltpu.VMEM(g.shape, BF16) for g in got]
        + [pltpu.VMEM(h, F32) for h in halves] * 2
        + [pltpu.SemaphoreType.DMA((n,))] * 6,
        compiler_params=pltpu.CompilerParams(vmem_limit_bytes=VMEM_LIMIT_BYTES),
    )(*sums, *got)


def _gather_small(part):
    def body(in_ref, out_ref, send, recv, local):
        x, y, c, _ = _mesh_position()
        me = 4 * x + 2 * y + c
        cps = [pltpu.make_async_copy(in_ref, out_ref.at[me], local)]
        k = 0
        for fx in range(2):
            for fy in range(2):
                for fc in range(2):
                    if fx or fy or fc:
                        cps.append(pltpu.make_async_remote_copy(
                            src_ref=in_ref, dst_ref=out_ref.at[me], send_sem=send.at[k],
                            recv_sem=recv.at[k], device_id=(x ^ fx, y ^ fy, c ^ fc),
                            device_id_type=MESH))
                        k += 1
        for cp in cps:
            cp.start()
        for cp in cps:
            cp.wait()

    return pl.pallas_call(
        body, name="gather_small",
        in_specs=[pl.BlockSpec(memory_space=pltpu.VMEM)],
        out_specs=pl.BlockSpec(memory_space=pltpu.VMEM),
        out_shape=jax.ShapeDtypeStruct((N_DEV,) + part.shape, part.dtype),
        scratch_shapes=[pltpu.SemaphoreType.DMA((N_DEV - 1,)), pltpu.SemaphoreType.DMA((N_DEV - 1,)),
                        pltpu.SemaphoreType.DMA],
    )(part)


def _scatter_start(grads, tag):
    sums = _exchange_siblings(grads, "exchange_siblings_" + tag)
    lands = [jax.ShapeDtypeStruct((N_CHIPS - 1,) + s.shape[1:], s.dtype) for s in sums]
    return _ici_start(sums, lands, _scatter_views, grads[0], "scatter_start_" + tag)


def _scatter_finish(handle, after, tag):
    sums, got = _ici_wait(handle, _scatter_views, after, "scatter_wait_" + tag)
    return _sum_and_share(sums, got, "sum_and_share_" + tag)


def _pad_rows(a, rows):
    return jnp.pad(a, ((0, rows - a.shape[0]), (0, 0)))


def kernel(x, norm_mix_0, w_in_0, b_f_0, conv_w_0, w_out_0, norm_ffn_0, w_up_0, w_down_0, norm_mix_1, pool_w_1, pool_scale_1, norm_ffn_1, w_up_1, w_down_1, final_norm, loss_target, m_norm_mix_0, m_w_in_0, m_b_f_0, m_conv_w_0, m_w_out_0, m_norm_ffn_0, m_w_up_0, m_w_down_0, m_norm_mix_1, m_pool_w_1, m_pool_scale_1, m_norm_ffn_1, m_w_up_1, m_w_down_1, m_final_norm, v_norm_mix_0, v_w_in_0, v_b_f_0, v_conv_w_0, v_w_out_0, v_norm_ffn_0, v_w_up_0, v_w_down_0, v_norm_mix_1, v_pool_w_1, v_pool_scale_1, v_norm_ffn_1, v_w_up_1, v_w_down_1, v_final_norm):
    d = x.shape[-1]
    a = N_HEADS * HEAD_DIM
    c_conv = conv_w_0.shape[1] * N_CHIPS
    xs = x[0]
    target = loss_target[0]
    row = lambda vec: vec.reshape(1, -1)

    big = [w_in_0, w_out_0, w_up_0, w_down_0, pool_w_1, w_up_1, w_down_1]
    first = [w_in_0.astype(BF16)]
    first_split = [True]
    start_a = _ici_start(first, _gather_land_shapes(first, first_split),
                         _gather_views(first_split), b_f_0, "gather_start_a")
    zero = start_a[-1][0, 0]
    rest = [(w + zero).astype(BF16)
            for w in (w_out_0, w_up_0, w_down_0, pool_w_1, w_up_1, w_down_1)]
    rest = rest + [conv_w_0]
    start_b = _ici_start(rest, [jax.ShapeDtypeStruct((N_CHIPS,) + w.shape, w.dtype) for w in rest],
                         _gather_whole_views, start_a[-1], "gather_start_b")
    n0 = _rms_pre(start_b[-1], xs, row(norm_mix_0))
    first, land_a = _ici_wait(start_a, _gather_views(first_split), n0, "gather_wait_a")
    (g_in,) = _gather_finish(first, land_a, first_split, "gather_finish_a")
    w_in = g_in.transpose(1, 0, 2).reshape(d, -1)
    w_qkv = w_in[:, :3 * a]
    w_f = jnp.pad(w_in[:, 3 * a:3 * a + N_HEADS], ((0, 0), (0, 128 - N_HEADS)))
    w_bcx = w_in[:, 3 * a + N_HEADS:]
    bf = jnp.pad(b_f_0, (0, 128 - N_HEADS)).reshape(1, 128)

    qkv, fl, bcx = _in_proj(n0, w_qkv, w_f, w_bcx)
    qa, ka = _gate_prep(fl, bf, qkv)
    o, lse = _attn_fwd(qa, ka, qkv)
    rest, land_b = _ici_wait(start_b, _gather_whole_views, o, "gather_wait_b")
    own_slot = 2 * lax.axis_index("x") + lax.axis_index("y")
    g_out, g_up0, g_down0, g_pool, g_up1, g_down1, g_conv = [
        lax.dynamic_update_index_in_dim(land, shard, own_slot, 0)
        for land, shard in zip(land_b, rest)]
    w_out = g_out.reshape(-1, d)
    conv_w = _pad_rows(g_conv.transpose(1, 0, 2).reshape(conv_w_0.shape[0], c_conv), 8)
    h1 = _conv_out(o, bcx, conv_w, w_out, xs)
    w_down0 = g_down0.reshape(-1, d)
    w_down1 = g_down1.reshape(-1, d)
    pool_w = g_pool.transpose(1, 0, 2, 3).reshape(pool_w_1.shape[0], -1, pool_w_1.shape[2])
    h2, a0, nf0 = _mlp_fwd(h1, row(norm_ffn_0), g_up0, w_down0, "mlp_fwd_0")
    h3 = _pool_fwd(h2, row(norm_mix_1), pool_w, row(pool_scale_1))
    dh4, a1, nf1, loss_part, d_final = _mlp_fwd(h3, row(norm_ffn_1), g_up1, w_down1, "mlp_fwd_1",
                                                head=(row(final_norm), target))

    slot_cols = g_up0.shape[2]
    pool_cols = pool_w.shape[2]
    da1, dz1, dh3, d_nffn1 = _mlp_bwd_x(dh4, a1, g_up1, w_down1, h3, row(norm_ffn_1), "mlp_bwd_x_1")
    dw_up1, dw_down1 = _mlp_bwd_w(nf1, da1, a1, dz1, slot_cols, "mlp_bwd_w_1")
    scatter_1 = _scatter_start([dw_up1, dw_down1.reshape(N_CHIPS, -1, d)], "mlp1")
    dh2, dw_pool, d_pscale, d_nmix1 = _pool_bwd(scatter_1[-1], dh3, h2, row(norm_mix_1), pool_w,
                                                row(pool_scale_1))
    da0, dz0, dh1, d_nffn0 = _mlp_bwd_x(dh2, a0, g_up0, w_down0, h1, row(norm_ffn_0), "mlp_bwd_x_0")
    dw_up0, dw_down0 = _mlp_bwd_w(nf0, da0, a0, dz0, slot_cols, "mlp_bwd_w_0")
    dw_pool = (dw_pool.reshape(pool_w.shape[0], N_CHIPS, -1, pool_cols).transpose(1, 0, 2, 3)
               .reshape(N_CHIPS, -1, pool_cols))
    scatter_0 = _scatter_start([dw_up0, dw_down0.reshape(N_CHIPS, -1, d), dw_pool], "mlp0")
    do, delta, dbcx, dw_out, d_conv = _conv_out_bwd(scatter_0[-1], dh1, w_out, o, bcx, conv_w)
    dqa, dka, dv = _attn_bwd(qa, ka, qkv, do, lse, delta)
    dqkv, dfl, d_bf = _gate_bwd(dqa, dka, dv, fl, bf)
    dw_qkv, dw_f, dw_bcx = _wgrad_in(n0, [dqkv, dfl, dbcx])
    dw_in = jnp.concatenate([dw_qkv, dw_f[:N_HEADS], dw_bcx], axis=0).reshape(N_CHIPS, -1, d)
    slot_rows = -(-dw_in.shape[1] // 32) * 32
    dw_in = jnp.pad(dw_in, ((0, 0), (0, slot_rows - dw_in.shape[1]), (0, 0)))
    scatter_m = _scatter_start([dw_in, dw_out.reshape(N_CHIPS, -1, d)], "mixer")
    grad_x, d_nmix0 = _in_proj_bwd(scatter_m[-1], dqkv, dfl, dbcx, w_qkv, w_f, w_bcx, xs,
                                   row(norm_mix_0), dh1)

    r_up1, r_down1 = _scatter_finish(scatter_1, grad_x, "mlp1")
    r_up0, r_down0, r_pool = _scatter_finish(scatter_0, grad_x, "mlp0")
    r_in, r_out = _scatter_finish(scatter_m, grad_x, "mixer")
    reduced = [r_in, r_out, r_up0, r_down0, r_pool, r_up1, r_down1]
    moments = [(m_w_in_0, v_w_in_0), (m_w_out_0, v_w_out_0), (m_w_up_0, v_w_up_0),
               (m_w_down_0, v_w_down_0), (m_pool_w_1, v_pool_w_1), (m_w_up_1, v_w_up_1),
               (m_w_down_1, v_w_down_1)]
    big_out = []
    for k, (w, g, (m, v)) in enumerate(zip(big, reduced, moments)):
        if w.shape[-1] % 128:
            view = lambda t: t.reshape(-1, t.shape[-1]).T
            back = lambda t: t.T.reshape(w.shape)
            g_view = g[:w.shape[-1]]
        else:
            view = lambda t: t.reshape(-1, t.shape[-1])
            back = lambda t: t.reshape(w.shape)
            g_view = view(g)
        delta_w, new_m, new_v = _adamw(view(w), g_view, view(m), view(v), "adamw_%d" % k)
        big_out.append((back(g_view), back(delta_w), back(new_m), back(new_v)))

    tail = jnp.concatenate([d_conv[0:3].reshape(-1)[d:], d_bf[0, :N_HEADS], loss_part[0, :1]])
    small_part = jnp.concatenate(
        [d_nmix0, d_nffn0, d_nmix1, d_pscale, d_nffn1, d_final,
         d_conv[0:3].reshape(1, -1)[:, :d],
         jnp.pad(tail, (0, d - tail.shape[0])).reshape(1, d)], axis=0)
    parts = _gather_small(small_part)

    chip = 2 * lax.axis_index("x") + lax.axis_index("y")
    cw_cols = conv_w_0.shape[1]

    def conv_block(full):
        mine = lax.dynamic_slice_in_dim(full, chip * cw_cols, cw_cols, axis=1)
        return jnp.pad(mine.reshape(-1), (0, d - mine.size))

    def small_rows(vals, cw, bfv):
        return jnp.stack(list(vals) + [cw, jnp.pad(bfv, (0, d - N_HEADS))])

    smalls_w = [norm_mix_0, norm_ffn_0, norm_mix_1, pool_scale_1, norm_ffn_1, final_norm]
    smalls_m = [m_norm_mix_0, m_norm_ffn_0, m_norm_mix_1, m_pool_scale_1, m_norm_ffn_1, m_final_norm]
    smalls_v = [v_norm_mix_0, v_norm_ffn_0, v_norm_mix_1, v_pool_scale_1, v_norm_ffn_1, v_final_norm]
    pad_cw = lambda t: jnp.pad(t.reshape(-1), (0, d - t.size))
    w_rows = small_rows(smalls_w, pad_cw(conv_w_0), b_f_0)
    m_rows = small_rows(smalls_m, pad_cw(m_conv_w_0), m_b_f_0)
    v_rows = small_rows(smalls_v, pad_cw(v_conv_w_0), v_b_f_0)

    g_sum = _sum_devices(parts)
    conv_full = jnp.concatenate([g_sum[6], g_sum[7, :3 * c_conv - d]]).reshape(3, c_conv)
    bf_grad = g_sum[7, 3 * c_conv - d:3 * c_conv - d + N_HEADS]
    loss = g_sum[7, 3 * c_conv - d + N_HEADS]
    g_rows = jnp.concatenate(
        [g_sum[0:6], conv_block(conv_full).reshape(1, d),
         jnp.pad(bf_grad, (0, d - N_HEADS)).reshape(1, d)], axis=0)
    d_rows, nm_rows, nv_rows = _adamw(w_rows, g_rows, m_rows, v_rows, "adamw_small")

    def unpack(rows):
        cw = rows[6, :conv_w_0.size].reshape(conv_w_0.shape)
        return [rows[0], rows[1], rows[2], rows[3], rows[4], rows[5], cw, rows[7, :N_HEADS]]

    def assemble(kind):
        sm = unpack([g_rows, d_rows, nm_rows, nv_rows][kind])
        lg = [t[kind] for t in big_out]
        return [sm[0], lg[0], sm[7], sm[6], lg[1], sm[1], lg[2], lg[3],
                sm[2], lg[4], sm[3], sm[4], lg[5], lg[6], sm[5]]

    return (loss, grad_x[None], *assemble(0), *assemble(1), *assemble(2), *assemble(3))
```

```python
import functools

import jax
import jax.numpy as jnp
from jax import lax
from jax.experimental import pallas as pl
from jax.experimental.pallas import tpu as pltpu

F32 = jnp.float32
BF16 = jnp.bfloat16

RMS_EPS = 1e-6
HEAD_DIM = 64
N_HEADS = 8
ATTN_SCALE = HEAD_DIM ** -0.5
LOG2_E = 1.4426950408889634
POOL_WINDOWS = (2, 4, 8, 16)
POOL_HALO = 16
CONV_HALO = 8
NEG_BIG = -1e30

ADAM_LR = 0.001
ADAM_B1 = 0.9
ADAM_B2 = 0.999
ADAM_EPS = 1e-08
ADAM_WD = 0.01
ADAM_STEP = 10

N_CHIPS = 4
N_DEV = 8
MESH = pl.DeviceIdType.MESH

VMEM_LIMIT_BYTES = 56 * 1024 * 1024

TILE_ROWS = 512
TILE_ATTN = 512
TILE_MLP_ROWS = 1024
TILE_MLP_FF = 1024
TILE_MLP_BWD_FF = 512
TILE_WGRAD_K = 1024
TILE_WGRAD_N = 1024
TILE_ELEM_ROWS = 256
SUM_CHUNK_ROWS = 128

LANE_CQ = 64
LANE_CK = 88


def _params(semantics):
    return pltpu.CompilerParams(dimension_semantics=semantics,
                                vmem_limit_bytes=VMEM_LIMIT_BYTES)


def _nn(a, b):
    return lax.dot_general(a, b, (((1,), (0,)), ((), ())), preferred_element_type=F32)


def _nt(a, b):
    return lax.dot_general(a, b, (((1,), (1,)), ((), ())), preferred_element_type=F32)


def _tn(a, b):
    return lax.dot_general(a, b, (((0,), (0,)), ((), ())), preferred_element_type=F32)


def _split3(v):
    hi = v.astype(BF16)
    r1 = v - hi.astype(F32)
    mid = r1.astype(BF16)
    lo = (r1 - mid.astype(F32)).astype(BF16)
    return hi, mid, lo


def _exact_nn(sel, v):
    hi, mid, lo = _split3(v)
    return _nn(sel, hi) + _nn(sel, mid) + _nn(sel, lo)


def _exact_nt(sel, v):
    hi, mid, lo = _split3(v)
    return _nt(sel, hi) + _nt(sel, mid) + _nt(sel, lo)


def _rms_fwd(x, g):
    r = lax.rsqrt(jnp.mean(x * x, axis=-1, keepdims=True) + RMS_EPS)
    return x * r * g, r


def _rms_bwd(dn, x, g):
    r = lax.rsqrt(jnp.mean(x * x, axis=-1, keepdims=True) + RMS_EPS)
    xh = x * r
    gy = dn * g
    dx = r * (gy - xh * jnp.mean(gy * xh, axis=-1, keepdims=True))
    return dx, jnp.sum(dn * xh, axis=0, keepdims=True)


def _lane(shape):
    return lax.broadcasted_iota(jnp.int32, shape, len(shape) - 1)


def _row(shape):
    return lax.broadcasted_iota(jnp.int32, shape, len(shape) - 2)


def _full(a):
    nd = a.ndim
    return pl.BlockSpec(a.shape, lambda *_: (0,) * nd)


def _rms_pre(after, x, g):
    s, d = x.shape
    tm = min(TILE_ROWS, s)

    def body(after_ref, x_ref, g_ref, n_ref):
        n, _ = _rms_fwd(x_ref[...], g_ref[...])
        n_ref[...] = n.astype(BF16)

    rows = pl.BlockSpec((tm, d), lambda i: (i, 0))
    return pl.pallas_call(
        body, name="rms_pre", grid=(s // tm,),
        in_specs=[ANY, rows, _full(g)], out_specs=rows,
        out_shape=jax.ShapeDtypeStruct((s, d), BF16),
        compiler_params=_params(("parallel",)),
    )(after, x, g)


def _in_proj(n, w_qkv, w_f, w_bcx):
    s, d = n.shape
    tm = min(TILE_ROWS, s)

    def body(n_ref, wq_ref, wf_ref, wb_ref, qkv_ref, fl_ref, bcx_ref):
        nb = n_ref[...]
        qkv_ref[...] = _nn(nb, wq_ref[...]).astype(BF16)
        fl_ref[...] = _nn(nb, wf_ref[...])
        bcx_ref[...] = _nn(nb, wb_ref[...])

    rows = lambda c: pl.BlockSpec((tm, c), lambda i: (i, 0))
    return pl.pallas_call(
        body, name="in_proj", grid=(s // tm,),
        in_specs=[rows(d), _full(w_qkv), _full(w_f), _full(w_bcx)],
        out_specs=[rows(w_qkv.shape[1]), rows(w_f.shape[1]), rows(w_bcx.shape[1])],
        out_shape=[jax.ShapeDtypeStruct((s, w_qkv.shape[1]), BF16),
                   jax.ShapeDtypeStruct((s, w_f.shape[1]), F32),
                   jax.ShapeDtypeStruct((s, w_bcx.shape[1]), F32)],
        compiler_params=_params(("parallel",)),
    )(n, w_qkv, w_f, w_bcx)


def _gate_prep(fl, bf, qkv):
    s = fl.shape[0]
    a = N_HEADS * HEAD_DIM
    tm = min(TILE_ROWS, s)

    def body(fl_ref, bf_ref, q_ref, k_ref, qa_ref, ka_ref, carry_ref):
        i = pl.program_id(0)

        @pl.when(i == 0)
        def _():
            carry_ref[...] = jnp.zeros_like(carry_ref)

        z = fl_ref[...] + bf_ref[...]
        logf = jnp.minimum(z, 0.0) - jnp.log(1.0 + jnp.exp(-jnp.abs(z)))
        lower = (_lane((tm, tm)) <= _row((tm, tm))).astype(BF16)
        cum = _exact_nn(lower, logf) + carry_ref[0:1, :]
        carry_ref[0:1, :] = cum[tm - 1:tm, :]

        lane = _lane((tm, 128))
        pieces = [p.astype(F32)
                  for p in _split3(jnp.where(lane < N_HEADS, LOG2_E * cum, 0.0))]
        shared_q = sum(pltpu.roll(p, LANE_CQ + N_HEADS * k, axis=1) for k, p in enumerate(pieces))
        shared_k = -sum(pltpu.roll(p, LANE_CK + N_HEADS * k, axis=1) for k, p in enumerate(pieces))
        for h in range(N_HEADS):
            at_q = functools.reduce(jnp.logical_or,
                                    [lane == LANE_CQ + N_HEADS * k + h for k in range(3)])
            at_k = functools.reduce(jnp.logical_or,
                                    [lane == LANE_CK + N_HEADS * k + h for k in range(3)])
            pair = slice((h // 2) * 128, (h // 2 + 1) * 128)
            qp = q_ref[:, pair].astype(F32)
            kp = k_ref[:, pair].astype(F32)
            if h % 2:
                qp = pltpu.roll(qp, HEAD_DIM, axis=1)
                kp = pltpu.roll(kp, HEAD_DIM, axis=1)
            q_bias = jnp.where(at_k, 1.0, shared_q)
            k_bias = jnp.where(at_q, 1.0, shared_k)
            qa_ref[h] = jnp.where(lane < HEAD_DIM, qp * (ATTN_SCALE * LOG2_E), q_bias).astype(BF16)
            ka_ref[h] = jnp.where(lane < HEAD_DIM, kp, k_bias).astype(BF16)

    aug = jax.ShapeDtypeStruct((N_HEADS, s, 128), BF16)
    aug_spec = pl.BlockSpec((N_HEADS, tm, 128), lambda i: (0, i, 0))
    return pl.pallas_call(
        body, name="gate_prep", grid=(s // tm,),
        in_specs=[pl.BlockSpec((tm, 128), lambda i: (i, 0)), _full(bf),
                  pl.BlockSpec((tm, a), lambda i: (i, 0)),
                  pl.BlockSpec((tm, a), lambda i: (i, 1))],
        out_specs=[aug_spec, aug_spec],
        out_shape=[aug, aug],
        scratch_shapes=[pltpu.VMEM((8, 128), F32)],
        compiler_params=_params(("arbitrary",)),
    )(fl, bf, qkv, qkv)


def _attn_fwd(qa, ka, qkv):
    s = qa.shape[1]
    a = N_HEADS * HEAD_DIM
    t = min(TILE_ATTN, s)
    n_pairs = N_HEADS // 2
    v_block0 = 2 * a // 128

    ones_lane = (HEAD_DIM, 0)

    def body(qa_ref, ka_ref, v_ref, o_ref, lse_ref, m_ref, acc_ref, s_even, s_odd):
        i = pl.program_id(1)
        m_ref[...] = jnp.full_like(m_ref, NEG_BIG)
        acc_ref[...] = jnp.zeros_like(acc_ref)
        upper_rows = _row((128, t)) < HEAD_DIM

        def keys(j):
            return pl.ds(pl.multiple_of(j * t, t), t)

        def scores_into(buf, j):
            for e in range(2):
                buf[e] = _nt(ka_ref[e, keys(j), :], qa_ref[e])

        def consume(buf, j, masked):
            vf = v_ref[keys(j), :].astype(F32)
            lane = _lane((t, 128))
            own = [lane < HEAD_DIM, lane >= HEAD_DIM]
            for e in range(2):
                v_head = jnp.where(own[e], vf, jnp.where(lane == ones_lane[e], 1.0, 0.0)).astype(BF16)
                sc = buf[e]
                if masked:
                    sc = jnp.where(_row((t, t)) <= _lane((t, t)), sc, NEG_BIG)
                m_prev = m_ref[e]
                m_new = jnp.maximum(m_prev, jnp.max(sc, axis=0, keepdims=True))
                p = jnp.exp2(sc - m_new).astype(BF16)
                acc_ref[e] = acc_ref[e] * jnp.exp2(m_prev - m_new) + _tn(v_head, p)
                m_ref[e] = m_new

        scores_into(s_even, 0)

        def two_tiles(p, carry):
            j = 2 * p
            scores_into(s_odd, j + 1)
            consume(s_even, j, False)
            scores_into(s_even, j + 2)
            consume(s_odd, j + 1, False)
            return carry

        lax.fori_loop(0, i // 2, two_tiles, 0)

        @pl.when(i % 2 == 0)
        def _():
            consume(s_even, i, True)

        @pl.when(i % 2 == 1)
        def _():
            scores_into(s_odd, i)
            consume(s_even, i - 1, False)
            consume(s_odd, i, True)

        denom = [acc_ref[e, ones_lane[e]:ones_lane[e] + 1, :] for e in range(2)]
        out_t = jnp.where(upper_rows, acc_ref[0] / denom[0], acc_ref[1] / denom[1])
        o_ref[...] = out_t.T.astype(BF16)
        lse = [m_ref[e] + LOG2_E * jnp.log(denom[e]) for e in range(2)]
        lse_ref[...] = jnp.where(_row((8, t)) == 0, lse[0], lse[1])

    return pl.pallas_call(
        body, name="attn_fwd", grid=(n_pairs, s // t),
        in_specs=[pl.BlockSpec((2, t, 128), lambda g, i: (g, i, 0)),
                  pl.BlockSpec((2, s, 128), lambda g, i: (g, 0, 0)),
                  pl.BlockSpec((s, 128), lambda g, i: (0, v_block0 + g))],
        out_specs=[pl.BlockSpec((t, 128), lambda g, i: (i, g)),
                   pl.BlockSpec((None, 8, t), lambda g, i: (g, 0, i))],
        out_shape=[jax.ShapeDtypeStruct((s, a), BF16),
                   jax.ShapeDtypeStruct((n_pairs, 8, s), F32)],
        scratch_shapes=[pltpu.VMEM((2, 1, t), F32), pltpu.VMEM((2, 128, t), F32),
                        pltpu.VMEM((2, t, t), F32), pltpu.VMEM((2, t, t), F32)],
        compiler_params=_params(("parallel", "arbitrary")),
    )(qa, ka, qkv)


def _conv_out(o, bcx, cw, w_out, x):
    s, d = x.shape
    c = o.shape[1]
    tm = min(TILE_ROWS, s)

    def body(o_ref, b_ref, c_ref, xin_ref, cw_ref, w_ref, x_ref, h_ref, ubuf):
        i = pl.program_id(0)

        @pl.when(i == 0)
        def _():
            ubuf[0:CONV_HALO, :] = jnp.zeros((CONV_HALO, c), F32)

        u = c_ref[...] * xin_ref[...]
        ubuf[CONV_HALO:CONV_HALO + tm, :] = u
        u1 = ubuf[CONV_HALO - 1:CONV_HALO - 1 + tm, :]
        u2 = ubuf[CONV_HALO - 2:CONV_HALO - 2 + tm, :]
        cv = (cw_ref[0:1, :] * u2 + cw_ref[1:2, :] * u1) + cw_ref[2:3, :] * u
        y = (b_ref[...] * cv).astype(BF16)
        mix = _nn(o_ref[...], w_ref[0:c, :]) + _nn(y, w_ref[c:2 * c, :])
        h_ref[...] = x_ref[...] + mix
        ubuf[0:CONV_HALO, :] = u[tm - CONV_HALO:tm, :]

    col = lambda k: pl.BlockSpec((tm, c), lambda i: (i, k))
    return pl.pallas_call(
        body, name="conv_out", grid=(s // tm,),
        in_specs=[col(0), col(0), col(1), col(2), _full(cw), _full(w_out),
                  pl.BlockSpec((tm, d), lambda i: (i, 0))],
        out_specs=pl.BlockSpec((tm, d), lambda i: (i, 0)),
        out_shape=jax.ShapeDtypeStruct((s, d), F32),
        scratch_shapes=[pltpu.VMEM((tm + CONV_HALO, c), F32)],
        compiler_params=_params(("arbitrary",)),
    )(o, bcx, bcx, bcx, cw, w_out, x)


def _mlp_fwd(h, g, w_up, w_down, name, head=None):
    s, d = h.shape
    ff = w_down.shape[0]
    slot_cols = w_up.shape[2]
    tm = min(TILE_MLP_ROWS, s)
    tf = min(TILE_MLP_FF if head is None else TILE_MLP_FF // 2, slot_cols)
    per_slot = slot_cols // tf
    nf = ff // tf
    n_head = 0 if head is None else 2

    def body(*refs):
        h_ref, g_ref, wu_ref, wd_ref = refs[:4]
        out_ref, a_ref, n_ref = refs[4 + n_head:7 + n_head]
        nb_ref, acc_ref = refs[-2:]
        i = pl.program_id(0)
        f = pl.program_id(1)

        @pl.when(f == 0)
        def _():
            n, _ = _rms_fwd(h_ref[...], g_ref[...])
            nb = n.astype(BF16)
            nb_ref[...] = nb
            n_ref[...] = nb
            acc_ref[...] = jnp.zeros_like(acc_ref)

        pre = _nn(nb_ref[...], wu_ref[...])
        a_ref[...] = pre.astype(BF16)
        r = jnp.square(jnp.maximum(pre, 0.0)).astype(BF16)
        acc_ref[...] += _nn(r, wd_ref[...])

        @pl.when(f == nf - 1)
        def _():
            out = h_ref[...] + acc_ref[...]
            if head is None:
                out_ref[...] = out
            else:
                gf_ref, t_ref = refs[4:6]
                loss_ref, dg_ref = refs[7 + n_head:9 + n_head]
                y, _ = _rms_fwd(out, gf_ref[...])
                err = y - t_ref[...]
                part = 0.5 * jnp.sum(jnp.mean(err * err, axis=-1, keepdims=True), axis=0,
                                     keepdims=True)
                dx, dg = _rms_bwd(err / d, out, gf_ref[...])
                out_ref[...] = dx
                part = jnp.broadcast_to(part, loss_ref.shape)

                @pl.when(i == 0)
                def _():
                    loss_ref[...] = part
                    dg_ref[...] = dg

                @pl.when(i > 0)
                def _():
                    loss_ref[...] += part
                    dg_ref[...] += dg

    rows = pl.BlockSpec((tm, d), lambda i, f: (i, 0))
    in_specs = [rows, _full(g),
                pl.BlockSpec((None, d, tf), lambda i, f: (f // per_slot, 0, f % per_slot)),
                pl.BlockSpec((tf, d), lambda i, f: (f, 0))]
    out_specs = [rows, pl.BlockSpec((tm, tf), lambda i, f: (i, f)), rows]
    out_shape = [jax.ShapeDtypeStruct((s, d), F32), jax.ShapeDtypeStruct((s, ff), BF16),
                 jax.ShapeDtypeStruct((s, d), BF16)]
    args = [h, g, w_up, w_down]
    if head is not None:
        in_specs += [_full(head[0]), rows]
        args += list(head)
        out_specs += [pl.BlockSpec((1, 128), lambda i, f: (0, 0)),
                      pl.BlockSpec((1, d), lambda i, f: (0, 0))]
        out_shape += [jax.ShapeDtypeStruct((1, 128), F32), jax.ShapeDtypeStruct((1, d), F32)]
    return pl.pallas_call(
        body, name=name, grid=(s // tm, nf),
        in_specs=in_specs, out_specs=out_specs, out_shape=out_shape,
        scratch_shapes=[pltpu.VMEM((tm, d), BF16), pltpu.VMEM((tm, d), F32)],
        compiler_params=_params(("parallel" if head is None else "arbitrary", "arbitrary")),
    )(*args)


def _window_sum_down(e, window):
    step = 1
    while step < window:
        e = e + pltpu.roll(e, step, axis=0)
        step *= 2
    return e


def _window_sum_up(e, window):
    n = e.shape[0]
    step = 1
    while step < window:
        e = e + pltpu.roll(e, n - step, axis=0)
        step *= 2
    return e


def _pool_counts(first_row, tm, window):
    t = first_row + _row((tm, 1))
    return jnp.minimum(t + 1, window).astype(F32)


def _pool_fwd(h, g, pw, ps):
    s, d = h.shape
    cg = d // len(POOL_WINDOWS)
    tm = min(TILE_ROWS, s)

    def body(h_ref, g_ref, pw_ref, ps_ref, out_ref, nbuf):
        i = pl.program_id(0)

        @pl.when(i == 0)
        def _():
            nbuf[0:POOL_HALO, :] = jnp.zeros((POOL_HALO, d), F32)

        n, _ = _rms_fwd(h_ref[...], g_ref[...])
        nbuf[POOL_HALO:POOL_HALO + tm, :] = n
        for k, window in enumerate(POOL_WINDOWS):
            cols = slice(k * cg, (k + 1) * cg)
            sums = _window_sum_down(nbuf[:, cols], window)[POOL_HALO:, :]
            pooled = sums / _pool_counts(i * tm, tm, window) - n[:, cols]
            y = _nn(pooled.astype(BF16), pw_ref[k]) * ps_ref[:, cols]
            out_ref[:, cols] = h_ref[:, cols] + y
        nbuf[0:POOL_HALO, :] = n[tm - POOL_HALO:tm, :]

    return pl.pallas_call(
        body, name="pool_fwd", grid=(s // tm,),
        in_specs=[pl.BlockSpec((tm, d), lambda i: (i, 0)), _full(g), _full(pw), _full(ps)],
        out_specs=pl.BlockSpec((tm, d), lambda i: (i, 0)),
        out_shape=jax.ShapeDtypeStruct((s, d), F32),
        scratch_shapes=[pltpu.VMEM((tm + POOL_HALO, d), F32)],
        compiler_params=_params(("arbitrary",)),
    )(h, g, pw, ps)


def _mlp_bwd_x(dz, a, w_up, w_down, h_in, g, name):
    s, d = dz.shape
    ff = w_down.shape[0]
    slot_cols = w_up.shape[2]
    tm = min(TILE_MLP_ROWS, s)
    tf = min(TILE_MLP_BWD_FF, slot_cols)
    per_slot = slot_cols // tf
    nf = ff // tf

    def body(dz_ref, a_ref, wu_ref, wd_ref, h_ref, g_ref, da_ref, dzb_ref, dh_ref, dg_ref,
             dzs_ref, acc_ref):
        i = pl.program_id(0)
        f = pl.program_id(1)

        @pl.when(f == 0)
        def _():
            dzb = dz_ref[...].astype(BF16)
            dzs_ref[...] = dzb
            dzb_ref[...] = dzb
            acc_ref[...] = jnp.zeros_like(acc_ref)

        dr = _nt(dzs_ref[...], wd_ref[...])
        da = (dr * (2.0 * jnp.maximum(a_ref[...].astype(F32), 0.0))).astype(BF16)
        da_ref[...] = da
        acc_ref[...] += _nt(da, wu_ref[...])

        @pl.when(f == nf - 1)
        def _():
            dx, dg = _rms_bwd(acc_ref[...], h_ref[...], g_ref[...])
            dh_ref[...] = dz_ref[...] + dx

            @pl.when(i == 0)
            def _():
                dg_ref[...] = dg

            @pl.when(i > 0)
            def _():
                dg_ref[...] += dg

    return pl.pallas_call(
        body, name=name, grid=(s // tm, nf),
        in_specs=[pl.BlockSpec((tm, d), lambda i, f: (i, 0)),
                  pl.BlockSpec((tm, tf), lambda i, f: (i, f)),
                  pl.BlockSpec((None, d, tf), lambda i, f: (f // per_slot, 0, f % per_slot)),
                  pl.BlockSpec((tf, d), lambda i, f: (f, 0)),
                  pl.BlockSpec((tm, d), lambda i, f: (i, 0)), _full(g)],
        out_specs=[pl.BlockSpec((tm, tf), lambda i, f: (i, f)),
                   pl.BlockSpec((tm, d), lambda i, f: (i, 0)),
                   pl.BlockSpec((tm, d), lambda i, f: (i, 0)),
                   pl.BlockSpec((1, d), lambda i, f: (0, 0))],
        out_shape=[jax.ShapeDtypeStruct((s, ff), BF16),
                   jax.ShapeDtypeStruct((s, d), BF16),
                   jax.ShapeDtypeStruct((s, d), F32),
                   jax.ShapeDtypeStruct((1, d), F32)],
        scratch_shapes=[pltpu.VMEM((tm, d), BF16), pltpu.VMEM((tm, d), F32)],
        compiler_params=_params(("arbitrary", "arbitrary")),
    )(dz, a, w_up, w_down, h_in, g)


def _mlp_bwd_w(n, da, a, dzb, slot_cols, name):
    s, d = n.shape
    ff = a.shape[1]
    tn = min(TILE_WGRAD_N, slot_cols)
    tk = min(TILE_WGRAD_K, s)
    per_slot = slot_cols // tn
    nk = s // tk

    def body(n_ref, da_ref, a_ref, dz_ref, du_ref, dd_ref, accu_ref, accd_ref):
        k = pl.program_id(1)

        @pl.when(k == 0)
        def _():
            accu_ref[...] = jnp.zeros_like(accu_ref)
            accd_ref[...] = jnp.zeros_like(accd_ref)

        accu_ref[...] += _tn(n_ref[...], da_ref[...])
        r = jnp.square(jnp.maximum(a_ref[...].astype(F32), 0.0)).astype(BF16)
        accd_ref[...] += _tn(r, dz_ref[...])

        @pl.when(k == nk - 1)
        def _():
            du_ref[...] = accu_ref[...].astype(BF16)
            dd_ref[...] = accd_ref[...].astype(BF16)

    return pl.pallas_call(
        body, name=name, grid=(ff // tn, nk),
        in_specs=[pl.BlockSpec((tk, d), lambda f, k: (k, 0)),
                  pl.BlockSpec((tk, tn), lambda f, k: (k, f)),
                  pl.BlockSpec((tk, tn), lambda f, k: (k, f)),
                  pl.BlockSpec((tk, d), lambda f, k: (k, 0))],
        out_specs=[pl.BlockSpec((None, d, tn), lambda f, k: (f // per_slot, 0, f % per_slot)),
                   pl.BlockSpec((tn, d), lambda f, k: (f, 0))],
        out_shape=[jax.ShapeDtypeStruct((ff // slot_cols, d, slot_cols), BF16),
                   jax.ShapeDtypeStruct((ff, d), BF16)],
        scratch_shapes=[pltpu.VMEM((d, tn), F32), pltpu.VMEM((tn, d), F32)],
        compiler_params=_params(("parallel", "arbitrary")),
    )(n, da, a, dzb)


def _pool_bwd(after, dh, h, g, pw, ps):
    s, d = h.shape
    cg = d // len(POOL_WINDOWS)
    tm = min(TILE_ROWS, s)
    nb = s // tm
    halo_per_tile = tm // POOL_HALO

    def body(after_ref, dh_ref, h_ref, halo_ref, g_ref, pw_ref, ps_ref,
             dx_ref, dpw_ref, dps_ref, dg_ref, nbuf, qbuf, dn_ref, carry, dpw_acc):
        i = pl.program_id(0)
        blk = nb - 1 - i

        @pl.when(i == 0)
        def _():
            carry[...] = jnp.zeros_like(carry)
            dpw_acc[...] = jnp.zeros_like(dpw_acc)
            dps_ref[...] = jnp.zeros_like(dps_ref)
            dg_ref[...] = jnp.zeros_like(dg_ref)

        hv = h_ref[...]
        n, _ = _rms_fwd(hv, g_ref[...])
        nh, _ = _rms_fwd(halo_ref[...], g_ref[...])
        nbuf[0:POOL_HALO, :] = jnp.where(blk == 0, 0.0, nh)
        nbuf[POOL_HALO:POOL_HALO + tm, :] = n
        dhv = dh_ref[...]
        for k, window in enumerate(POOL_WINDOWS):
            cols = slice(k * cg, (k + 1) * cg)
            cnt = _pool_counts(blk * tm, tm, window)
            sums = _window_sum_down(nbuf[:, cols], window)[POOL_HALO:, :]
            pb = (sums / cnt - n[:, cols]).astype(BF16)
            dyk = dhv[:, cols]
            dps_ref[:, cols] += jnp.sum(dyk * _nn(pb, pw_ref[k]), axis=0, keepdims=True)
            dyb = (dyk * ps_ref[:, cols]).astype(BF16)
            dpw_acc[k] += _tn(pb, dyb)
            dpool = _nt(dyb, pw_ref[k])
            qv = dpool / cnt
            qbuf[0:tm, cols] = qv
            qbuf[tm:tm + POOL_HALO, cols] = carry[:, cols]
            dn_ref[:, cols] = _window_sum_up(qbuf[:, cols], window)[0:tm, :] - dpool
            carry[:, cols] = qv[0:POOL_HALO, :]
        dx, dg = _rms_bwd(dn_ref[...], hv, g_ref[...])
        dx_ref[...] = dhv + dx
        dg_ref[...] += dg

        @pl.when(i == nb - 1)
        def _():
            dpw_ref[...] = dpw_acc[...].astype(BF16)

    rev = lambda i: (nb - 1 - i, 0)
    return pl.pallas_call(
        body, name="pool_bwd", grid=(nb,),
        in_specs=[ANY, pl.BlockSpec((tm, d), rev), pl.BlockSpec((tm, d), rev),
                  pl.BlockSpec((POOL_HALO, d),
                               lambda i: (jnp.maximum((nb - 1 - i) * halo_per_tile - 1, 0), 0)),
                  _full(g), _full(pw), _full(ps)],
        out_specs=[pl.BlockSpec((tm, d), rev), _full(pw),
                   pl.BlockSpec((1, d), lambda i: (0, 0)),
                   pl.BlockSpec((1, d), lambda i: (0, 0))],
        out_shape=[jax.ShapeDtypeStruct((s, d), F32),
                   jax.ShapeDtypeStruct(pw.shape, BF16),
                   jax.ShapeDtypeStruct((1, d), F32),
                   jax.ShapeDtypeStruct((1, d), F32)],
        scratch_shapes=[pltpu.VMEM((tm + POOL_HALO, d), F32), pltpu.VMEM((tm + POOL_HALO, d), F32),
                        pltpu.VMEM((tm, d), F32), pltpu.VMEM((POOL_HALO, d), F32),
                        pltpu.VMEM(pw.shape, F32)],
        compiler_params=_params(("arbitrary",)),
    )(after, dh, h, h, g, pw, ps)


def _conv_out_bwd(after, dh, w_out, o, bcx, cw):
    s, d = dh.shape
    c = o.shape[1]
    tm = min(TILE_ROWS, s)
    nb = s // tm
    halo_per_tile = tm // CONV_HALO

    def body(after_ref, dh_ref, w_ref, o_ref, b_ref, c_ref, xin_ref, ch_ref, xh_ref, cw_ref,
             do_ref, delta_ref, dbcx_ref, dw_ref, dcw_ref, ubuf, dbuf, carry, acc):
        i = pl.program_id(0)
        blk = nb - 1 - i

        @pl.when(i == 0)
        def _():
            carry[...] = jnp.zeros_like(carry)
            acc[...] = jnp.zeros_like(acc)
            dcw_ref[...] = jnp.zeros_like(dcw_ref)

        dm = dh_ref[...].astype(BF16)
        dcat = _nt(dm, w_ref[...])
        do = dcat[:, 0:c]
        dy = dcat[:, c:2 * c]
        do_ref[...] = do.astype(BF16)
        head_of_lane = lax.shift_right_logical(_lane((8, c)), HEAD_DIM.bit_length() - 1)
        heads = (head_of_lane == _row((8, c))).astype(BF16)
        delta_ref[...] = _exact_nt(heads, do * o_ref[...].astype(F32))

        cv_ = c_ref[...]
        xin = xin_ref[...]
        bv = b_ref[...]
        u = cv_ * xin
        ubuf[0:CONV_HALO, :] = jnp.where(blk == 0, 0.0, ch_ref[...] * xh_ref[...])
        ubuf[CONV_HALO:CONV_HALO + tm, :] = u
        u1 = ubuf[CONV_HALO - 1:CONV_HALO - 1 + tm, :]
        u2 = ubuf[CONV_HALO - 2:CONV_HALO - 2 + tm, :]
        w0, w1, w2 = cw_ref[0:1, :], cw_ref[1:2, :], cw_ref[2:3, :]
        cv = (w0 * u2 + w1 * u1) + w2 * u
        acc[0:c, :] += _tn(o_ref[...], dm)
        acc[c:2 * c, :] += _tn((bv * cv).astype(BF16), dm)

        dcv = dy * bv
        dcw_ref[0:1, :] += jnp.sum(dcv * u2, axis=0, keepdims=True)
        dcw_ref[1:2, :] += jnp.sum(dcv * u1, axis=0, keepdims=True)
        dcw_ref[2:3, :] += jnp.sum(dcv * u, axis=0, keepdims=True)
        dbuf[0:tm, :] = dcv
        dbuf[tm:tm + CONV_HALO, :] = carry[...]
        du = w2 * dcv + w1 * dbuf[1:1 + tm, :] + w0 * dbuf[2:2 + tm, :]
        dbcx_ref[:, 0:c] = (dy * cv).astype(BF16)
        dbcx_ref[:, c:2 * c] = (du * xin).astype(BF16)
        dbcx_ref[:, 2 * c:3 * c] = (du * cv_).astype(BF16)
        carry[...] = dcv[0:CONV_HALO, :]

        @pl.when(i == nb - 1)
        def _():
            dw_ref[...] = acc[...].astype(BF16)

    rev = lambda k: (lambda i: (nb - 1 - i, k))
    halo = lambda k: (lambda i: (jnp.maximum((nb - 1 - i) * halo_per_tile - 1, 0), k))
    return pl.pallas_call(
        body, name="conv_out_bwd", grid=(nb,),
        in_specs=[ANY, pl.BlockSpec((tm, d), rev(0)), _full(w_out), pl.BlockSpec((tm, c), rev(0)),
                  pl.BlockSpec((tm, c), rev(0)), pl.BlockSpec((tm, c), rev(1)),
                  pl.BlockSpec((tm, c), rev(2)),
                  pl.BlockSpec((CONV_HALO, c), halo(1)), pl.BlockSpec((CONV_HALO, c), halo(2)),
                  _full(cw)],
        out_specs=[pl.BlockSpec((tm, c), rev(0)),
                   pl.BlockSpec((8, tm), lambda i: (0, nb - 1 - i)),
                   pl.BlockSpec((tm, 3 * c), rev(0)),
                   _full(w_out), _full(cw)],
        out_shape=[jax.ShapeDtypeStruct((s, c), BF16),
                   jax.ShapeDtypeStruct((8, s), F32),
                   jax.ShapeDtypeStruct((s, 3 * c), BF16),
                   jax.ShapeDtypeStruct(w_out.shape, BF16),
                   jax.ShapeDtypeStruct(cw.shape, F32)],
        scratch_shapes=[pltpu.VMEM((tm + CONV_HALO, c), F32), pltpu.VMEM((tm + CONV_HALO, c), F32),
                        pltpu.VMEM((CONV_HALO, c), F32), pltpu.VMEM(w_out.shape, F32)],
        compiler_params=_params(("arbitrary",)),
    )(after, dh, w_out, o, bcx, bcx, bcx, bcx, bcx, cw)


def _attn_bwd(qa, ka, qkv, do, lse, delta):
    s = qa.shape[1]
    a = N_HEADS * HEAD_DIM
    t = min(TILE_ATTN, s)
    nq = s // t
    n_pairs = N_HEADS // 2
    v_block0 = 2 * a // 128

    def body(ka_ref, v_ref, qa_ref, do_ref, lse_ref, delta_ref,
             dqt_ref, dka_ref, dv_ref, dk_acc, dv_acc):
        g = pl.program_id(0)
        j = pl.program_id(1)

        @pl.when(j == 0)
        def _():
            dqt_ref[...] = jnp.zeros_like(dqt_ref)

        dk_acc[...] = jnp.zeros_like(dk_acc)
        dv_acc[...] = jnp.zeros_like(dv_acc)
        lane = _lane((t, 128))
        vf = v_ref[...].astype(F32)
        v_heads = [jnp.where(lane < HEAD_DIM, vf, 0.0).astype(BF16),
                   jnp.where(lane >= HEAD_DIM, vf, 0.0).astype(BF16)]
        ke_t = [ka_ref[e].astype(F32).T.astype(BF16) for e in range(2)]

        def q_step(i, masked):
            qs = pl.ds(pl.multiple_of(i * t, t), t)
            dob = do_ref[qs, :]
            for e in range(2):
                qe = qa_ref[e, qs, :]
                sc = _nt(ka_ref[e], qe)
                if masked:
                    sc = jnp.where(_row((t, t)) <= _lane((t, t)), sc, NEG_BIG)
                p = jnp.exp2(sc - lse_ref[pl.ds(e, 1), qs])
                dv_acc[e] += _nn(p.astype(BF16), dob)
                dp = _nt(v_heads[e], dob)
                ds = (p * (dp - delta_ref[pl.ds(2 * g + e, 1), qs])).astype(BF16)
                dk_acc[e] += _nn(ds, qe)
                dqt_ref[e, :, qs] += _nn(ke_t[e], ds)

        q_step(j, True)

        def full_step(i, carry):
            q_step(i, False)
            return carry

        lax.fori_loop(j + 1, nq, full_step, 0)
        dka_ref[...] = dk_acc[...]
        dv_ref[...] = jnp.where(lane < HEAD_DIM, dv_acc[0], dv_acc[1]).astype(BF16)

    return pl.pallas_call(
        body, name="attn_bwd", grid=(n_pairs, nq),
        in_specs=[pl.BlockSpec((2, t, 128), lambda g, j: (g, j, 0)),
                  pl.BlockSpec((t, 128), lambda g, j: (j, v_block0 + g)),
                  pl.BlockSpec((2, s, 128), lambda g, j: (g, 0, 0)),
                  pl.BlockSpec((s, 128), lambda g, j: (0, g)),
                  pl.BlockSpec((None, 8, s), lambda g, j: (g, 0, 0)),
                  pl.BlockSpec((8, s), lambda g, j: (0, 0))],
        out_specs=[pl.BlockSpec((2, 128, s), lambda g, j: (g, 0, 0)),
                   pl.BlockSpec((2, t, 128), lambda g, j: (g, j, 0)),
                   pl.BlockSpec((t, 128), lambda g, j: (j, g))],
        out_shape=[jax.ShapeDtypeStruct((N_HEADS, 128, s), F32),
                   jax.ShapeDtypeStruct((N_HEADS, s, 128), F32),
                   jax.ShapeDtypeStruct((s, a), BF16)],
        scratch_shapes=[pltpu.VMEM((2, t, 128), F32), pltpu.VMEM((2, t, 128), F32)],
        compiler_params=_params(("parallel", "arbitrary")),
    )(ka, qkv, qa, do, lse, delta)


def _gate_bwd(dqa, dka, dv, fl, bf):
    s = fl.shape[0]
    a = N_HEADS * HEAD_DIM
    tm = min(TILE_ROWS, s)
    nb = s // tm

    def body(dqa_ref, dka_ref, dv_ref, fl_ref, bf_ref, dqkv_ref, dfl_ref, dbf_ref, carry):
        i = pl.program_id(0)

        @pl.when(i == 0)
        def _():
            carry[...] = jnp.zeros_like(carry)
            dbf_ref[...] = jnp.zeros_like(dbf_ref)

        lane = _lane((tm, 128))
        dq_sum = jnp.zeros((tm, 128), F32)
        dk_sum = jnp.zeros((tm, 128), F32)
        for pair in range(N_HEADS // 2):
            qs, ks = [], []
            for e in range(2):
                h = 2 * pair + e
                dq = dqa_ref[h].T
                dk = dka_ref[h]
                dq_sum = dq_sum + dq
                dk_sum = dk_sum + dk
                qs.append(dq * ATTN_SCALE)
                ks.append(dk * (1.0 / LOG2_E))
            cols = slice(pair * 128, (pair + 1) * 128)
            dqkv_ref[:, cols] = jnp.where(
                lane < HEAD_DIM, qs[0], pltpu.roll(qs[1], HEAD_DIM, axis=1)).astype(BF16)
            dqkv_ref[:, a + pair * 128:a + (pair + 1) * 128] = jnp.where(
                lane < HEAD_DIM, ks[0], pltpu.roll(ks[1], HEAD_DIM, axis=1)).astype(BF16)
        dqkv_ref[:, 2 * a:3 * a] = dv_ref[...]

        in_q = (lane >= LANE_CQ) & (lane < LANE_CQ + N_HEADS)
        in_k = (lane >= LANE_CK) & (lane < LANE_CK + N_HEADS)
        dcum = (pltpu.roll(jnp.where(in_q, dq_sum, 0.0), 128 - LANE_CQ, axis=1)
                - pltpu.roll(jnp.where(in_k, dk_sum, 0.0), 128 - LANE_CK, axis=1))

        upper = (_lane((tm, tm)) >= _row((tm, tm))).astype(BF16)
        dlogf = _exact_nn(upper, dcum) + carry[0:1, :]
        carry[0:1, :] = dlogf[0:1, :]
        z = fl_ref[...] + bf_ref[...]
        ez = jnp.exp(-jnp.abs(z))
        sig_neg = jnp.where(z >= 0.0, ez, 1.0) / (1.0 + ez)
        dz = jnp.where(lane < N_HEADS, dlogf * sig_neg, 0.0)
        dfl_ref[...] = dz.astype(BF16)
        dbf_ref[...] += jnp.sum(dz, axis=0, keepdims=True)

    rev3 = lambda i: (0, nb - 1 - i, 0)
    rev = lambda i: (nb - 1 - i, 0)
    return pl.pallas_call(
        body, name="gate_bwd", grid=(nb,),
        in_specs=[pl.BlockSpec((N_HEADS, 128, tm), lambda i: (0, 0, nb - 1 - i)),
                  pl.BlockSpec((N_HEADS, tm, 128), rev3),
                  pl.BlockSpec((tm, a), rev), pl.BlockSpec((tm, 128), rev), _full(bf)],
        out_specs=[pl.BlockSpec((tm, 3 * a), rev), pl.BlockSpec((tm, 128), rev),
                   pl.BlockSpec((1, 128), lambda i: (0, 0))],
        out_shape=[jax.ShapeDtypeStruct((s, 3 * a), BF16),
                   jax.ShapeDtypeStruct((s, 128), BF16),
                   jax.ShapeDtypeStruct((1, 128), F32)],
        scratch_shapes=[pltpu.VMEM((8, 128), F32)],
        compiler_params=_params(("arbitrary",)),
    )(dqa, dka, dv, fl, bf)


def _in_proj_bwd(after, dqkv, dfl, dbcx, w_qkv, w_f, w_bcx, x, g, dh):
    s, d = x.shape
    tm = min(TILE_ROWS, s)

    def body(after_ref, dq_ref, df_ref, db_ref, wq_ref, wf_ref, wb_ref, x_ref, g_ref, dh_ref,
             gx_ref, dg_ref):
        i = pl.program_id(0)
        dn = (_nt(dq_ref[...], wq_ref[...]) + _nt(df_ref[...], wf_ref[...])
              + _nt(db_ref[...], wb_ref[...]))
        dx, dg = _rms_bwd(dn, x_ref[...], g_ref[...])
        gx_ref[...] = dh_ref[...] + dx

        @pl.when(i == 0)
        def _():
            dg_ref[...] = dg

        @pl.when(i > 0)
        def _():
            dg_ref[...] += dg

    rows = lambda c: pl.BlockSpec((tm, c), lambda i: (i, 0))
    return pl.pallas_call(
        body, name="in_proj_bwd", grid=(s // tm,),
        in_specs=[ANY, rows(dqkv.shape[1]), rows(dfl.shape[1]), rows(dbcx.shape[1]),
                  _full(w_qkv), _full(w_f), _full(w_bcx), rows(d), _full(g), rows(d)],
        out_specs=[rows(d), pl.BlockSpec((1, d), lambda i: (0, 0))],
        out_shape=[jax.ShapeDtypeStruct((s, d), F32), jax.ShapeDtypeStruct((1, d), F32)],
        compiler_params=_params(("arbitrary",)),
    )(after, dqkv, dfl, dbcx, w_qkv, w_f, w_bcx, x, g, dh)


def _wgrad_in(n, dys):
    s, d = n.shape
    m = len(dys)
    tk = min(TILE_ROWS, s)
    nk = s // tk

    def body(*refs):
        n_ref, dy_refs, dw_refs, accs = refs[0], refs[1:1 + m], refs[1 + m:1 + 2 * m], refs[1 + 2 * m:]
        k = pl.program_id(0)

        @pl.when(k == 0)
        def _():
            for acc in accs:
                acc[...] = jnp.zeros_like(acc)

        nb = n_ref[...]
        for dy_ref, acc in zip(dy_refs, accs):
            acc[...] += _tn(nb, dy_ref[...])

        @pl.when(k == nk - 1)
        def _():
            for dw_ref, acc in zip(dw_refs, accs):
                dw_ref[...] = acc[...].T.astype(BF16)

    return pl.pallas_call(
        body, name="wgrad_in", grid=(nk,),
        in_specs=[pl.BlockSpec((tk, d), lambda k: (k, 0))]
        + [pl.BlockSpec((tk, dy.shape[1]), lambda k: (k, 0)) for dy in dys],
        out_specs=[pl.BlockSpec((dy.shape[1], d), lambda k: (0, 0)) for dy in dys],
        out_shape=[jax.ShapeDtypeStruct((dy.shape[1], d), BF16) for dy in dys],
        scratch_shapes=[pltpu.VMEM((d, dy.shape[1]), F32) for dy in dys],
        compiler_params=_params(("arbitrary",)),
    )(n, *dys)


def _row_tile(rows):
    t = min(TILE_ELEM_ROWS, rows)
    while rows % t:
        t //= 2
    return t


def _adamw_math(w, g, m, v):
    m = ADAM_B1 * m + (1.0 - ADAM_B1) * g
    v = ADAM_B2 * v + (1.0 - ADAM_B2) * jnp.square(g)
    m_hat = m / (1.0 - ADAM_B1 ** ADAM_STEP)
    v_hat = v / (1.0 - ADAM_B2 ** ADAM_STEP)
    delta = -ADAM_LR * (m_hat / (jnp.sqrt(v_hat) + ADAM_EPS) + ADAM_WD * w)
    return delta, m, v


def _adamw(w, g, m, v, name):
    rows, cols = w.shape

    def body(w_ref, g_ref, m_ref, v_ref, d_ref, nm_ref, nv_ref):
        delta, nm, nv = _adamw_math(w_ref[...], g_ref[...], m_ref[...], v_ref[...])
        d_ref[...] = delta
        nm_ref[...] = nm
        nv_ref[...] = nv

    if rows % 8 == 0:
        tr = _row_tile(rows)
        grid, spec = (rows // tr,), pl.BlockSpec((tr, cols), lambda i: (i, 0))
    else:
        grid, spec = (cols // 256,), pl.BlockSpec((rows, 256), lambda i: (0, i))
    out = jax.ShapeDtypeStruct(w.shape, F32)
    return pl.pallas_call(
        body, name=name, grid=grid, in_specs=[spec] * 4, out_specs=[spec] * 3,
        out_shape=[out, out, out], compiler_params=_params(("parallel",)),
    )(w, g, m, v)


def _sum_devices(parts):
    def body(p_ref, g_ref):
        g = p_ref[0]
        for k in range(1, N_DEV):
            g = g + p_ref[k]
        g_ref[...] = g

    return pl.pallas_call(
        body, name="sum_devices",
        in_specs=[pl.BlockSpec(memory_space=pltpu.VMEM)],
        out_specs=pl.BlockSpec(memory_space=pltpu.VMEM),
        out_shape=jax.ShapeDtypeStruct(parts.shape[1:], F32),
    )(parts)


def _mesh_position():
    x, y, c = lax.axis_index("x"), lax.axis_index("y"), lax.axis_index("c")
    chips = [(1 - x, y), (x, 1 - y), (1 - x, 1 - y)]
    return x, y, c, chips


ANY = pl.BlockSpec(memory_space=pl.ANY)
HBM = pl.BlockSpec(memory_space=pltpu.HBM)
SEM = pl.BlockSpec(memory_space=pltpu.SEMAPHORE)
SPLIT_COPY_EFFECT = pltpu.SideEffectType.DATAFLOW_SIDE_EFFECTING


def _in_hbm(a):
    return pltpu.with_memory_space_constraint(a, pltpu.HBM)


def _chip_copies(views, srcs, lands, send, recv, waiting=False):
    _, _, c, chips = _mesh_position()
    cps = []
    for a in range(len(srcs)):
        for k, (px, py) in enumerate(chips):
            src, dst = views(a, k, srcs[a], lands[a], c, 2 * px + py)
            sem = a * (N_CHIPS - 1) + k
            cps.append(pltpu.make_async_remote_copy(
                src_ref=src, dst_ref=dst, send_sem=send.at[sem], recv_sem=recv.at[sem],
                device_id=(px, py, c), device_id_type=MESH))
    return cps


def _ici_start(sources, land_shapes, copies, after, name, per_array=N_CHIPS - 1):
    n = len(sources)

    def body(*refs):
        srcs, lands = refs[:n], refs[n:2 * n]
        send, recv = refs[2 * n + 1], refs[2 * n + 2]
        token = refs[-1]
        for cp in copies(srcs, lands, send, recv, False):
            cp.start()
        token[...] = jnp.zeros_like(token)

    lands = [_in_hbm(lax.empty(s.shape, s.dtype)) for s in land_shapes]
    outs = pl.pallas_call(
        body, name=name,
        in_specs=[HBM] * (2 * n) + [ANY],
        out_specs=[SEM, SEM] + [HBM] * (2 * n) + [pl.BlockSpec(memory_space=pltpu.VMEM)],
        out_shape=[pltpu.SemaphoreType.DMA((n * per_array,))] * 2
        + [pltpu.HBM(a.shape, a.dtype) for a in sources]
        + [pltpu.HBM(s.shape, s.dtype) for s in land_shapes]
        + [jax.ShapeDtypeStruct((8, 128), F32)],
        input_output_aliases={i: 2 + i for i in range(2 * n)},
        compiler_params=pltpu.CompilerParams(has_side_effects=SPLIT_COPY_EFFECT),
    )(*[_in_hbm(a) for a in sources], *lands, after)
    return outs[0], outs[1], list(outs[2:2 + n]), list(outs[2 + n:2 + 2 * n]), outs[-1]


def _ici_wait(handle, copies, after, name):
    send, recv, srcs, lands, _ = handle
    n = len(srcs)

    def body(*refs):
        src_refs, land_refs = refs[:n], refs[n:2 * n]
        for cp in copies(src_refs, land_refs, refs[2 * n], refs[2 * n + 1], True):
            cp.wait_send()
            cp.wait_recv()

    outs = pl.pallas_call(
        body, name=name,
        in_specs=[HBM] * (2 * n) + [SEM, SEM, ANY],
        out_specs=[HBM] * (2 * n),
        out_shape=[pltpu.HBM(a.shape, a.dtype) for a in srcs]
        + [pltpu.HBM(a.shape, a.dtype) for a in lands],
        input_output_aliases={i: i for i in range(2 * n)},
        compiler_params=pltpu.CompilerParams(has_side_effects=SPLIT_COPY_EFFECT),
    )(*srcs, *lands, send, recv, after)
    return list(outs[:n]), list(outs[n:])


def _gather_views(split):
    def views(a, k, src, land, c, slot):
        if split[a]:
            half = src.shape[0] // 2
            src = src.at[pl.ds(c * half, half)]
        return src, land.at[k]
    return views


def _gather_whole_views(a, k, src, land, c, slot):
    x, y, _, _ = _mesh_position()
    return src, land.at[2 * x + y]


SCATTER_COPIES = 2 * (N_CHIPS - 1)


def _scatter_copies(srcs, lands, send, recv, waiting):
    _, _, c, chips = _mesh_position()
    cps = []
    for a in range(len(srcs)):
        half = srcs[a].shape[1] // 2
        for k, (px, py) in enumerate(chips):
            for h in range(2):
                arrival = 2 * k + (h if waiting else c)
                cps.append(pltpu.make_async_remote_copy(
                    src_ref=srcs[a].at[2 * px + py, pl.ds(h * half, half)],
                    dst_ref=lands[a].at[arrival],
                    send_sem=send.at[a * SCATTER_COPIES + 2 * k + h],
                    recv_sem=recv.at[a * SCATTER_COPIES + arrival],
                    device_id=(px, py, h), device_id_type=MESH))
    return cps


def _gather_land_shapes(shards, split):
    return [jax.ShapeDtypeStruct(
        (N_CHIPS - 1, a.shape[0] // 2 if sp else a.shape[0]) + a.shape[1:], a.dtype)
        for a, sp in zip(shards, split)]


def _gather_finish(shards, lands, split, name):
    n = len(shards)
    ns = sum(split)
    d_index = {a: i for i, a in enumerate(a for a in range(n) if split[a])}

    def body(*refs):
        shard, land, outs = refs[:n], refs[n:2 * n], refs[2 * n:3 * n]
        obuf, fbuf = refs[3 * n:4 * n], refs[4 * n:5 * n]
        dbuf = refs[5 * n:5 * n + ns]
        ld_own, st_own, ld, st_mine, st_sib, send, recv = refs[5 * n + ns:]
        x, y, c, chips = _mesh_position()
        me = 2 * x + y
        own_loads, loads, sends, pending = [], {}, [], []
        for a in range(n):
            cp = pltpu.make_async_copy(shard[a], obuf[a], ld_own.at[a])
            cp.start()
            own_loads.append(cp)
        for a in range(n):
            for k in range(N_CHIPS - 1):
                cp = pltpu.make_async_copy(land[a].at[k], fbuf[a].at[k], ld.at[a, k])
                cp.start()
                loads[a, k] = cp
        for a in range(n):
            own_loads[a].wait()
            cp = pltpu.make_async_copy(obuf[a], outs[a].at[me], st_own.at[a])
            cp.start()
            pending.append(cp)
        for a in range(n):
            rows = shard[a].shape[0]
            for k, (px, py) in enumerate(chips):
                loads[a, k].wait()
                part = pl.ds(c * (rows // 2), rows // 2) if split[a] else pl.ds(0, rows)
                cp = pltpu.make_async_copy(fbuf[a].at[k], outs[a].at[2 * px + py, part],
                                           st_mine.at[a, k])
                cp.start()
                pending.append(cp)
                if split[a]:
                    fw = pltpu.make_async_remote_copy(
                        src_ref=fbuf[a].at[k], dst_ref=dbuf[d_index[a]].at[k],
                        send_sem=send.at[a, k], recv_sem=recv.at[a, k],
                        device_id=(x, y, 1 - c), device_id_type=MESH)
                    fw.start()
                    sends.append((a, k, fw))
        for a, k, fw in sends:
            px, py = chips[k]
            half = shard[a].shape[0] // 2
            fw.wait_recv()
            cp = pltpu.make_async_copy(dbuf[d_index[a]].at[k],
                                       outs[a].at[2 * px + py, pl.ds((1 - c) * half, half)],
                                       st_sib.at[a, k])
            cp.start()
            pending.append(cp)
        for _, _, fw in sends:
            fw.wait_send()
        for cp in pending:
            cp.wait()

    stage = [pltpu.VMEM(a.shape, a.dtype) for a in lands]
    dma = lambda *shape: pltpu.SemaphoreType.DMA(shape)
    return pl.pallas_call(
        body, name=name,
        in_specs=[ANY] * (2 * n), out_specs=[ANY] * n,
        out_shape=[jax.ShapeDtypeStruct((N_CHIPS,) + a.shape, a.dtype) for a in shards],
        scratch_shapes=[pltpu.VMEM(a.shape, a.dtype) for a in shards] + stage
        + [s for s, sp in zip(stage, split) if sp]
        + [dma(n), dma(n), dma(n, 3), dma(n, 3), dma(n, 3), dma(n, 3), dma(n, 3)],
        compiler_params=pltpu.CompilerParams(vmem_limit_bytes=VMEM_LIMIT_BYTES),
    )(*shards, *lands)


def _sum_chunk(rows):
    return next(r for r in range(SUM_CHUNK_ROWS, 0, -16) if rows % r == 0)


def _sum_and_share(partials, lands, name):
    n = len(partials)

    def body(*refs):
        own, landed, outs = refs[:n], refs[n:2 * n], refs[2 * n:3 * n]
        obuf, xbuf, ybuf, gbuf, sbuf, rbuf = (refs[(3 + k) * n:(4 + k) * n] for k in range(6))
        ld_own, ld_send, ld_got, st_own, st_sib, send_p, recv_p, send_s, recv_s = refs[9 * n:]
        x, y, c, _ = _mesh_position()
        me = 2 * x + y
        sibling = (x, y, 1 - c)

        def to_sibling(src, dst, send, recv, a):
            return pltpu.make_async_remote_copy(src_ref=src, dst_ref=dst, send_sem=send.at[a],
                                                recv_sem=recv.at[a], device_id=sibling,
                                                device_id_type=MESH)

        loads, firsts, seconds, stores = [], [], [], []
        for a in range(n):
            half = obuf[a].shape[0]
            cps = [pltpu.make_async_copy(own[a].at[me, pl.ds((1 - c) * half, half)], xbuf[a],
                                         ld_send.at[a]),
                   pltpu.make_async_copy(own[a].at[me, pl.ds(c * half, half)], obuf[a], ld_own.at[a]),
                   pltpu.make_async_copy(landed[a], gbuf[a], ld_got.at[a])]
            for cp in cps:
                cp.start()
            loads.append(cps)
        for a in range(n):
            loads[a][0].wait()
            rc = to_sibling(xbuf[a], ybuf[a], send_p, recv_p, a)
            rc.start()
            firsts.append(rc)
        for a in range(n):
            firsts[a].wait_recv()
            loads[a][1].wait()
            loads[a][2].wait()
            half = obuf[a].shape[0]
            rows = _sum_chunk(half)

            def add(k, carry, a=a, rows=rows):
                at = pl.ds(pl.multiple_of(k * rows, rows), rows)
                acc = obuf[a][at].astype(F32) + ybuf[a][at].astype(F32)
                for j in range(SCATTER_COPIES):
                    acc = acc + gbuf[a][j, at].astype(F32)
                sbuf[a][at] = acc
                return carry

            lax.fori_loop(0, half // rows, add, 0)
            rc = to_sibling(sbuf[a], rbuf[a], send_s, recv_s, a)
            rc.start()
            seconds.append(rc)
            cp = pltpu.make_async_copy(sbuf[a], outs[a].at[pl.ds(c * half, half)], st_own.at[a])
            cp.start()
            stores.append(cp)
        for a in range(n):
            half = obuf[a].shape[0]
            seconds[a].wait_recv()
            cp = pltpu.make_async_copy(rbuf[a], outs[a].at[pl.ds((1 - c) * half, half)], st_sib.at[a])
            cp.start()
            stores.append(cp)
        for rc in firsts + seconds:
            rc.wait_send()
        for cp in stores:
            cp.wait()

    halves = [(a.shape[1] // 2, a.shape[2]) for a in partials]
    return pl.pallas_call(
        body, name=name,
        in_specs=[ANY] * (2 * n), out_specs=[ANY] * n,
        out_shape=[jax.ShapeDtypeStruct((2 * h[0], h[1]), F32) for h in halves],
        scratch_shapes=[pltpu.VMEM(h, BF16) for h in halves] * 3
        + [pltpu.VMEM(g.shape, BF16) for g in lands]
        + [pltpu.VMEM(h, F32) for h in halves] * 2
        + [pltpu.SemaphoreType.DMA((n,))] * 9,
        compiler_params=pltpu.CompilerParams(vmem_limit_bytes=VMEM_LIMIT_BYTES),
    )(*partials, *lands)


def _gather_small(part):
    def body(in_ref, out_ref, send, recv, local):
        x, y, c, _ = _mesh_position()
        me = 4 * x + 2 * y + c
        cps = [pltpu.make_async_copy(in_ref, out_ref.at[me], local)]
        k = 0
        for fx in range(2):
            for fy in range(2):
                for fc in range(2):
                    if fx or fy or fc:
                        cps.append(pltpu.make_async_remote_copy(
                            src_ref=in_ref, dst_ref=out_ref.at[me], send_sem=send.at[k],
                            recv_sem=recv.at[k], device_id=(x ^ fx, y ^ fy, c ^ fc),
                            device_id_type=MESH))
                        k += 1
        for cp in cps:
            cp.start()
        for cp in cps:
            cp.wait()

    return pl.pallas_call(
        body, name="gather_small",
        in_specs=[pl.BlockSpec(memory_space=pltpu.VMEM)],
        out_specs=pl.BlockSpec(memory_space=pltpu.VMEM),
        out_shape=jax.ShapeDtypeStruct((N_DEV,) + part.shape, part.dtype),
        scratch_shapes=[pltpu.SemaphoreType.DMA((N_DEV - 1,)), pltpu.SemaphoreType.DMA((N_DEV - 1,)),
                        pltpu.SemaphoreType.DMA],
    )(part)


def _scatter_start(grads, after, tag):
    lands = [jax.ShapeDtypeStruct((SCATTER_COPIES, g.shape[1] // 2, g.shape[2]), g.dtype)
             for g in grads]
    return _ici_start(grads, lands, _scatter_copies, after, "scatter_start_" + tag,
                      per_array=SCATTER_COPIES)


def _scatter_finish(handle, after, tag):
    grads, lands = _ici_wait(handle, _scatter_copies, after, "scatter_wait_" + tag)
    return _sum_and_share(grads, lands, "sum_and_share_" + tag)


def _pad_rows(a, rows):
    return jnp.pad(a, ((0, rows - a.shape[0]), (0, 0)))


def kernel(x, norm_mix_0, w_in_0, b_f_0, conv_w_0, w_out_0, norm_ffn_0, w_up_0, w_down_0, norm_mix_1, pool_w_1, pool_scale_1, norm_ffn_1, w_up_1, w_down_1, final_norm, loss_target, m_norm_mix_0, m_w_in_0, m_b_f_0, m_conv_w_0, m_w_out_0, m_norm_ffn_0, m_w_up_0, m_w_down_0, m_norm_mix_1, m_pool_w_1, m_pool_scale_1, m_norm_ffn_1, m_w_up_1, m_w_down_1, m_final_norm, v_norm_mix_0, v_w_in_0, v_b_f_0, v_conv_w_0, v_w_out_0, v_norm_ffn_0, v_w_up_0, v_w_down_0, v_norm_mix_1, v_pool_w_1, v_pool_scale_1, v_norm_ffn_1, v_w_up_1, v_w_down_1, v_final_norm):
    d = x.shape[-1]
    a = N_HEADS * HEAD_DIM
    c_conv = conv_w_0.shape[1] * N_CHIPS
    xs = x[0]
    target = loss_target[0]
    row = lambda vec: vec.reshape(1, -1)

    big = [w_in_0, w_out_0, w_up_0, w_down_0, pool_w_1, w_up_1, w_down_1]
    first = [w_in_0.astype(BF16)]
    first_split = [True]
    copies_a = functools.partial(_chip_copies, _gather_views(first_split))
    copies_b = functools.partial(_chip_copies, _gather_whole_views)
    start_a = _ici_start(first, _gather_land_shapes(first, first_split), copies_a, b_f_0,
                         "gather_start_a")
    zero = start_a[-1][0, 0]
    rest = [(w + zero).astype(BF16)
            for w in (w_out_0, w_up_0, w_down_0, pool_w_1, w_up_1, w_down_1)]
    rest = rest + [conv_w_0]
    start_b = _ici_start(rest, [jax.ShapeDtypeStruct((N_CHIPS,) + w.shape, w.dtype) for w in rest],
                         copies_b, start_a[-1], "gather_start_b")
    n0 = _rms_pre(start_b[-1], xs, row(norm_mix_0))
    first, land_a = _ici_wait(start_a, copies_a, n0, "gather_wait_a")
    (g_in,) = _gather_finish(first, land_a, first_split, "gather_finish_a")
    w_in = g_in.transpose(1, 0, 2).reshape(d, -1)
    w_qkv = w_in[:, :3 * a]
    w_f = jnp.pad(w_in[:, 3 * a:3 * a + N_HEADS], ((0, 0), (0, 128 - N_HEADS)))
    w_bcx = w_in[:, 3 * a + N_HEADS:]
    bf = jnp.pad(b_f_0, (0, 128 - N_HEADS)).reshape(1, 128)

    qkv, fl, bcx = _in_proj(n0, w_qkv, w_f, w_bcx)
    qa, ka = _gate_prep(fl, bf, qkv)
    o, lse = _attn_fwd(qa, ka, qkv)
    rest, land_b = _ici_wait(start_b, copies_b, o, "gather_wait_b")
    own_slot = 2 * lax.axis_index("x") + lax.axis_index("y")
    g_out, g_up0, g_down0, g_pool, g_up1, g_down1, g_conv = [
        lax.dynamic_update_index_in_dim(land, shard, own_slot, 0)
        for land, shard in zip(land_b, rest)]
    w_out = g_out.reshape(-1, d)
    conv_w = _pad_rows(g_conv.transpose(1, 0, 2).reshape(conv_w_0.shape[0], c_conv), 8)
    h1 = _conv_out(o, bcx, conv_w, w_out, xs)
    w_down0 = g_down0.reshape(-1, d)
    w_down1 = g_down1.reshape(-1, d)
    pool_w = g_pool.transpose(1, 0, 2, 3).reshape(pool_w_1.shape[0], -1, pool_w_1.shape[2])
    h2, a0, nf0 = _mlp_fwd(h1, row(norm_ffn_0), g_up0, w_down0, "mlp_fwd_0")
    h3 = _pool_fwd(h2, row(norm_mix_1), pool_w, row(pool_scale_1))
    dh4, a1, nf1, loss_part, d_final = _mlp_fwd(h3, row(norm_ffn_1), g_up1, w_down1, "mlp_fwd_1",
                                                head=(row(final_norm), target))

    slot_cols = g_up0.shape[2]
    pool_cols = pool_w.shape[2]
    da1, dz1, dh3, d_nffn1 = _mlp_bwd_x(dh4, a1, g_up1, w_down1, h3, row(norm_ffn_1), "mlp_bwd_x_1")
    dw_up1, dw_down1 = _mlp_bwd_w(nf1, da1, a1, dz1, slot_cols, "mlp_bwd_w_1")
    scatter_1 = _scatter_start([dw_up1, dw_down1.reshape(N_CHIPS, -1, d)], bf, "mlp1")
    dh2, dw_pool, d_pscale, d_nmix1 = _pool_bwd(scatter_1[-1], dh3, h2, row(norm_mix_1), pool_w,
                                                row(pool_scale_1))
    da0, dz0, dh1, d_nffn0 = _mlp_bwd_x(dh2, a0, g_up0, w_down0, h1, row(norm_ffn_0), "mlp_bwd_x_0")
    dw_up0, dw_down0 = _mlp_bwd_w(nf0, da0, a0, dz0, slot_cols, "mlp_bwd_w_0")
    dw_pool = (dw_pool.reshape(pool_w.shape[0], N_CHIPS, -1, pool_cols).transpose(1, 0, 2, 3)
               .reshape(N_CHIPS, -1, pool_cols))
    scatter_0 = _scatter_start([dw_up0, dw_down0.reshape(N_CHIPS, -1, d), dw_pool], bf, "mlp0")
    do, delta, dbcx, dw_out, d_conv = _conv_out_bwd(scatter_0[-1], dh1, w_out, o, bcx, conv_w)
    scatter_o = _scatter_start([dw_out.reshape(N_CHIPS, -1, d)], bf, "out")
    dqa, dka, dv = _attn_bwd(qa, ka, qkv, do, lse, delta)
    dqkv, dfl, d_bf = _gate_bwd(dqa, dka, dv, fl, bf)
    dw_qkv, dw_f, dw_bcx = _wgrad_in(n0, [dqkv, dfl, dbcx])
    dw_in = jnp.concatenate([dw_qkv, dw_f[:N_HEADS], dw_bcx], axis=0).reshape(N_CHIPS, -1, d)
    slot_rows = -(-dw_in.shape[1] // 32) * 32
    dw_in = jnp.pad(dw_in, ((0, 0), (0, slot_rows - dw_in.shape[1]), (0, 0)))
    scatter_m = _scatter_start([dw_in], bf, "mixer")
    grad_x, d_nmix0 = _in_proj_bwd(scatter_m[-1], dqkv, dfl, dbcx, w_qkv, w_f, w_bcx, xs,
                                   row(norm_mix_0), dh1)

    r_up1, r_down1 = _scatter_finish(scatter_1, grad_x, "mlp1")
    r_up0, r_down0, r_pool = _scatter_finish(scatter_0, grad_x, "mlp0")
    (r_out,) = _scatter_finish(scatter_o, grad_x, "out")
    (r_in,) = _scatter_finish(scatter_m, grad_x, "mixer")
    reduced = [r_in, r_out, r_up0, r_down0, r_pool, r_up1, r_down1]
    moments = [(m_w_in_0, v_w_in_0), (m_w_out_0, v_w_out_0), (m_w_up_0, v_w_up_0),
               (m_w_down_0, v_w_down_0), (m_pool_w_1, v_pool_w_1), (m_w_up_1, v_w_up_1),
               (m_w_down_1, v_w_down_1)]
    big_out = []
    for k, (w, g, (m, v)) in enumerate(zip(big, reduced, moments)):
        if w.shape[-1] % 128:
            view = lambda t: t.reshape(-1, t.shape[-1]).T
            back = lambda t: t.T.reshape(w.shape)
            g_view = g[:w.shape[-1]]
        else:
            view = lambda t: t.reshape(-1, t.shape[-1])
            back = lambda t: t.reshape(w.shape)
            g_view = view(g)
        delta_w, new_m, new_v = _adamw(view(w), g_view, view(m), view(v), "adamw_%d" % k)
        big_out.append((back(g_view), back(delta_w), back(new_m), back(new_v)))

    tail = jnp.concatenate([d_conv[0:3].reshape(-1)[d:], d_bf[0, :N_HEADS], loss_part[0, :1]])
    small_part = jnp.concatenate(
        [d_nmix0, d_nffn0, d_nmix1, d_pscale, d_nffn1, d_final,
         d_conv[0:3].reshape(1, -1)[:, :d],
         jnp.pad(tail, (0, d - tail.shape[0])).reshape(1, d)], axis=0)
    parts = _gather_small(small_part)

    chip = 2 * lax.axis_index("x") + lax.axis_index("y")
    cw_cols = conv_w_0.shape[1]

    def conv_block(full):
        mine = lax.dynamic_slice_in_dim(full, chip * cw_cols, cw_cols, axis=1)
        return jnp.pad(mine.reshape(-1), (0, d - mine.size))

    def small_rows(vals, cw, bfv):
        return jnp.stack(list(vals) + [cw, jnp.pad(bfv, (0, d - N_HEADS))])

    smalls_w = [norm_mix_0, norm_ffn_0, norm_mix_1, pool_scale_1, norm_ffn_1, final_norm]
    smalls_m = [m_norm_mix_0, m_norm_ffn_0, m_norm_mix_1, m_pool_scale_1, m_norm_ffn_1, m_final_norm]
    smalls_v = [v_norm_mix_0, v_norm_ffn_0, v_norm_mix_1, v_pool_scale_1, v_norm_ffn_1, v_final_norm]
    pad_cw = lambda t: jnp.pad(t.reshape(-1), (0, d - t.size))
    w_rows = small_rows(smalls_w, pad_cw(conv_w_0), b_f_0)
    m_rows = small_rows(smalls_m, pad_cw(m_conv_w_0), m_b_f_0)
    v_rows = small_rows(smalls_v, pad_cw(v_conv_w_0), v_b_f_0)

    g_sum = _sum_devices(parts)
    conv_full = jnp.concatenate([g_sum[6], g_sum[7, :3 * c_conv - d]]).reshape(3, c_conv)
    bf_grad = g_sum[7, 3 * c_conv - d:3 * c_conv - d + N_HEADS]
    loss = g_sum[7, 3 * c_conv - d + N_HEADS]
    g_rows = jnp.concatenate(
        [g_sum[0:6], conv_block(conv_full).reshape(1, d),
         jnp.pad(bf_grad, (0, d - N_HEADS)).reshape(1, d)], axis=0)
    d_rows, nm_rows, nv_rows = _adamw(w_rows, g_rows, m_rows, v_rows, "adamw_small")

    def unpack(rows):
        cw = rows[6, :conv_w_0.size].reshape(conv_w_0.shape)
        return [rows[0], rows[1], rows[2], rows[3], rows[4], rows[5], cw, rows[7, :N_HEADS]]

    def assemble(kind):
        sm = unpack([g_rows, d_rows, nm_rows, nv_rows][kind])
        lg = [t[kind] for t in big_out]
        return [sm[0], lg[0], sm[7], sm[6], lg[1], sm[1], lg[2], lg[3],
                sm[2], lg[4], sm[3], sm[4], lg[5], lg[6], sm[5]]

    return (loss, grad_x[None], *assemble(0), *assemble(1), *assemble(2), *assemble(3))
```

```python
import functools

import jax
import jax.numpy as jnp
from jax import lax
from jax.experimental import pallas as pl
from jax.experimental.pallas import tpu as pltpu

F32 = jnp.float32
BF16 = jnp.bfloat16

RMS_EPS = 1e-6
HEAD_DIM = 64
N_HEADS = 8
ATTN_SCALE = HEAD_DIM ** -0.5
LOG2_E = 1.4426950408889634
POOL_WINDOWS = (2, 4, 8, 16)
POOL_HALO = 16
CONV_HALO = 8
NEG_BIG = -1e30

ADAM_LR = 0.001
ADAM_B1 = 0.9
ADAM_B2 = 0.999
ADAM_EPS = 1e-08
ADAM_WD = 0.01
ADAM_STEP = 10

N_CHIPS = 4
N_DEV = 8
MESH = pl.DeviceIdType.MESH

VMEM_LIMIT_BYTES = 56 * 1024 * 1024

TILE_ROWS = 512
TILE_ATTN = 512
TILE_MLP_ROWS = 1024
TILE_MLP_FF = 1024
TILE_MLP_BWD_FF = 512
TILE_WGRAD_K = 1024
TILE_WGRAD_N = 1024
TILE_ELEM_ROWS = 256
SUM_CHUNK_ROWS = 128

LANE_CQ = 64
LANE_CK = 88


def _params(semantics):
    return pltpu.CompilerParams(dimension_semantics=semantics,
                                vmem_limit_bytes=VMEM_LIMIT_BYTES)


def _nn(a, b):
    return lax.dot_general(a, b, (((1,), (0,)), ((), ())), preferred_element_type=F32)


def _nt(a, b):
    return lax.dot_general(a, b, (((1,), (1,)), ((), ())), preferred_element_type=F32)


def _tn(a, b):
    return lax.dot_general(a, b, (((0,), (0,)), ((), ())), preferred_element_type=F32)


def _split3(v):
    hi = v.astype(BF16)
    r1 = v - hi.astype(F32)
    mid = r1.astype(BF16)
    lo = (r1 - mid.astype(F32)).astype(BF16)
    return hi, mid, lo


def _exact_nn(sel, v):
    hi, mid, lo = _split3(v)
    return _nn(sel, hi) + _nn(sel, mid) + _nn(sel, lo)


def _exact_nt(sel, v):
    hi, mid, lo = _split3(v)
    return _nt(sel, hi) + _nt(sel, mid) + _nt(sel, lo)


def _rms_fwd(x, g):
    r = lax.rsqrt(jnp.mean(x * x, axis=-1, keepdims=True) + RMS_EPS)
    return x * r * g, r


def _rms_bwd(dn, x, g):
    r = lax.rsqrt(jnp.mean(x * x, axis=-1, keepdims=True) + RMS_EPS)
    xh = x * r
    gy = dn * g
    dx = r * (gy - xh * jnp.mean(gy * xh, axis=-1, keepdims=True))
    return dx, jnp.sum(dn * xh, axis=0, keepdims=True)


def _lane(shape):
    return lax.broadcasted_iota(jnp.int32, shape, len(shape) - 1)


def _row(shape):
    return lax.broadcasted_iota(jnp.int32, shape, len(shape) - 2)


def _full(a):
    nd = a.ndim
    return pl.BlockSpec(a.shape, lambda *_: (0,) * nd)


def _rms_pre(after, x, g):
    s, d = x.shape
    tm = min(TILE_ROWS, s)

    def body(after_ref, x_ref, g_ref, n_ref):
        n, _ = _rms_fwd(x_ref[...], g_ref[...])
        n_ref[...] = n.astype(BF16)

    rows = pl.BlockSpec((tm, d), lambda i: (i, 0))
    return pl.pallas_call(
        body, name="rms_pre", grid=(s // tm,),
        in_specs=[ANY, rows, _full(g)], out_specs=rows,
        out_shape=jax.ShapeDtypeStruct((s, d), BF16),
        compiler_params=_params(("parallel",)),
    )(after, x, g)


def _in_proj(n, w_qkv, w_f, w_bcx):
    s, d = n.shape
    tm = min(TILE_ROWS, s)

    def body(n_ref, wq_ref, wf_ref, wb_ref, qkv_ref, fl_ref, bcx_ref):
        nb = n_ref[...]
        qkv_ref[...] = _nn(nb, wq_ref[...]).astype(BF16)
        fl_ref[...] = _nn(nb, wf_ref[...])
        bcx_ref[...] = _nn(nb, wb_ref[...])

    rows = lambda c: pl.BlockSpec((tm, c), lambda i: (i, 0))
    return pl.pallas_call(
        body, name="in_proj", grid=(s // tm,),
        in_specs=[rows(d), _full(w_qkv), _full(w_f), _full(w_bcx)],
        out_specs=[rows(w_qkv.shape[1]), rows(w_f.shape[1]), rows(w_bcx.shape[1])],
        out_shape=[jax.ShapeDtypeStruct((s, w_qkv.shape[1]), BF16),
                   jax.ShapeDtypeStruct((s, w_f.shape[1]), F32),
                   jax.ShapeDtypeStruct((s, w_bcx.shape[1]), F32)],
        compiler_params=_params(("parallel",)),
    )(n, w_qkv, w_f, w_bcx)


def _gate_prep(fl, bf, qkv):
    s = fl.shape[0]
    a = N_HEADS * HEAD_DIM
    tm = min(TILE_ROWS, s)

    def body(fl_ref, bf_ref, q_ref, k_ref, qa_ref, ka_ref, carry_ref):
        i = pl.program_id(0)

        @pl.when(i == 0)
        def _():
            carry_ref[...] = jnp.zeros_like(carry_ref)

        z = fl_ref[...] + bf_ref[...]
        logf = jnp.minimum(z, 0.0) - jnp.log(1.0 + jnp.exp(-jnp.abs(z)))
        lower = (_lane((tm, tm)) <= _row((tm, tm))).astype(BF16)
        cum = _exact_nn(lower, logf) + carry_ref[0:1, :]
        carry_ref[0:1, :] = cum[tm - 1:tm, :]

        lane = _lane((tm, 128))
        pieces = [p.astype(F32)
                  for p in _split3(jnp.where(lane < N_HEADS, LOG2_E * cum, 0.0))]
        shared_q = sum(pltpu.roll(p, LANE_CQ + N_HEADS * k, axis=1) for k, p in enumerate(pieces))
        shared_k = -sum(pltpu.roll(p, LANE_CK + N_HEADS * k, axis=1) for k, p in enumerate(pieces))
        for h in range(N_HEADS):
            at_q = functools.reduce(jnp.logical_or,
                                    [lane == LANE_CQ + N_HEADS * k + h for k in range(3)])
            at_k = functools.reduce(jnp.logical_or,
                                    [lane == LANE_CK + N_HEADS * k + h for k in range(3)])
            pair = slice((h // 2) * 128, (h // 2 + 1) * 128)
            qp = q_ref[:, pair].astype(F32)
            kp = k_ref[:, pair].astype(F32)
            if h % 2:
                qp = pltpu.roll(qp, HEAD_DIM, axis=1)
                kp = pltpu.roll(kp, HEAD_DIM, axis=1)
            q_bias = jnp.where(at_k, 1.0, shared_q)
            k_bias = jnp.where(at_q, 1.0, shared_k)
            qa_ref[h] = jnp.where(lane < HEAD_DIM, qp * (ATTN_SCALE * LOG2_E), q_bias).astype(BF16)
            ka_ref[h] = jnp.where(lane < HEAD_DIM, kp, k_bias).astype(BF16)

    aug = jax.ShapeDtypeStruct((N_HEADS, s, 128), BF16)
    aug_spec = pl.BlockSpec((N_HEADS, tm, 128), lambda i: (0, i, 0))
    return pl.pallas_call(
        body, name="gate_prep", grid=(s // tm,),
        in_specs=[pl.BlockSpec((tm, 128), lambda i: (i, 0)), _full(bf),
                  pl.BlockSpec((tm, a), lambda i: (i, 0)),
                  pl.BlockSpec((tm, a), lambda i: (i, 1))],
        out_specs=[aug_spec, aug_spec],
        out_shape=[aug, aug],
        scratch_shapes=[pltpu.VMEM((8, 128), F32)],
        compiler_params=_params(("arbitrary",)),
    )(fl, bf, qkv, qkv)


def _attn_fwd(qa, ka, qkv):
    s = qa.shape[1]
    a = N_HEADS * HEAD_DIM
    t = min(TILE_ATTN, s)
    n_pairs = N_HEADS // 2
    v_block0 = 2 * a // 128

    ones_lane = (HEAD_DIM, 0)

    def body(qa_ref, ka_ref, v_ref, o_ref, lse_ref, m_ref, acc_ref, s_even, s_odd):
        i = pl.program_id(1)
        m_ref[...] = jnp.full_like(m_ref, NEG_BIG)
        acc_ref[...] = jnp.zeros_like(acc_ref)
        upper_rows = _row((128, t)) < HEAD_DIM

        def keys(j):
            return pl.ds(pl.multiple_of(j * t, t), t)

        def scores_into(buf, j):
            for e in range(2):
                buf[e] = _nt(ka_ref[e, keys(j), :], qa_ref[e])

        def consume(buf, j, masked):
            vf = v_ref[keys(j), :].astype(F32)
            lane = _lane((t, 128))
            own = [lane < HEAD_DIM, lane >= HEAD_DIM]
            for e in range(2):
                v_head = jnp.where(own[e], vf, jnp.where(lane == ones_lane[e], 1.0, 0.0)).astype(BF16)
                sc = buf[e]
                if masked:
                    sc = jnp.where(_row((t, t)) <= _lane((t, t)), sc, NEG_BIG)
                m_prev = m_ref[e]
                m_new = jnp.maximum(m_prev, jnp.max(sc, axis=0, keepdims=True))
                p = jnp.exp2(sc - m_new).astype(BF16)
                acc_ref[e] = acc_ref[e] * jnp.exp2(m_prev - m_new) + _tn(v_head, p)
                m_ref[e] = m_new

        scores_into(s_even, 0)

        def two_tiles(p, carry):
            j = 2 * p
            scores_into(s_odd, j + 1)
            consume(s_even, j, False)
            scores_into(s_even, j + 2)
            consume(s_odd, j + 1, False)
            return carry

        lax.fori_loop(0, i // 2, two_tiles, 0)

        @pl.when(i % 2 == 0)
        def _():
            consume(s_even, i, True)

        @pl.when(i % 2 == 1)
        def _():
            scores_into(s_odd, i)
            consume(s_even, i - 1, False)
            consume(s_odd, i, True)

        denom = [acc_ref[e, ones_lane[e]:ones_lane[e] + 1, :] for e in range(2)]
        out_t = jnp.where(upper_rows, acc_ref[0] / denom[0], acc_ref[1] / denom[1])
        o_ref[...] = out_t.T.astype(BF16)
        lse = [m_ref[e] + LOG2_E * jnp.log(denom[e]) for e in range(2)]
        lse_ref[...] = jnp.where(_row((8, t)) == 0, lse[0], lse[1])

    return pl.pallas_call(
        body, name="attn_fwd", grid=(n_pairs, s // t),
        in_specs=[pl.BlockSpec((2, t, 128), lambda g, i: (g, i, 0)),
                  pl.BlockSpec((2, s, 128), lambda g, i: (g, 0, 0)),
                  pl.BlockSpec((s, 128), lambda g, i: (0, v_block0 + g))],
        out_specs=[pl.BlockSpec((t, 128), lambda g, i: (i, g)),
                   pl.BlockSpec((None, 8, t), lambda g, i: (g, 0, i))],
        out_shape=[jax.ShapeDtypeStruct((s, a), BF16),
                   jax.ShapeDtypeStruct((n_pairs, 8, s), F32)],
        scratch_shapes=[pltpu.VMEM((2, 1, t), F32), pltpu.VMEM((2, 128, t), F32),
                        pltpu.VMEM((2, t, t), F32), pltpu.VMEM((2, t, t), F32)],
        compiler_params=_params(("parallel", "arbitrary")),
    )(qa, ka, qkv)


def _conv_out(o, bcx, cw, w_out, x):
    s, d = x.shape
    c = o.shape[1]
    tm = min(TILE_ROWS, s)

    def body(o_ref, b_ref, c_ref, xin_ref, cw_ref, w_ref, x_ref, h_ref, ubuf):
        i = pl.program_id(0)

        @pl.when(i == 0)
        def _():
            ubuf[0:CONV_HALO, :] = jnp.zeros((CONV_HALO, c), F32)

        u = c_ref[...] * xin_ref[...]
        ubuf[CONV_HALO:CONV_HALO + tm, :] = u
        u1 = ubuf[CONV_HALO - 1:CONV_HALO - 1 + tm, :]
        u2 = ubuf[CONV_HALO - 2:CONV_HALO - 2 + tm, :]
        cv = (cw_ref[0:1, :] * u2 + cw_ref[1:2, :] * u1) + cw_ref[2:3, :] * u
        y = (b_ref[...] * cv).astype(BF16)
        mix = _nn(o_ref[...], w_ref[0:c, :]) + _nn(y, w_ref[c:2 * c, :])
        h_ref[...] = x_ref[...] + mix
        ubuf[0:CONV_HALO, :] = u[tm - CONV_HALO:tm, :]

    col = lambda k: pl.BlockSpec((tm, c), lambda i: (i, k))
    return pl.pallas_call(
        body, name="conv_out", grid=(s // tm,),
        in_specs=[col(0), col(0), col(1), col(2), _full(cw), _full(w_out),
                  pl.BlockSpec((tm, d), lambda i: (i, 0))],
        out_specs=pl.BlockSpec((tm, d), lambda i: (i, 0)),
        out_shape=jax.ShapeDtypeStruct((s, d), F32),
        scratch_shapes=[pltpu.VMEM((tm + CONV_HALO, c), F32)],
        compiler_params=_params(("arbitrary",)),
    )(o, bcx, bcx, bcx, cw, w_out, x)


def _mlp_fwd(h, g, w_up, w_down, name, head=None):
    s, d = h.shape
    ff = w_down.shape[0]
    slot_cols = w_up.shape[2]
    tm = min(TILE_MLP_ROWS, s)
    tf = min(TILE_MLP_FF if head is None else TILE_MLP_FF // 2, slot_cols)
    per_slot = slot_cols // tf
    nf = ff // tf
    n_head = 0 if head is None else 2

    def body(*refs):
        h_ref, g_ref, wu_ref, wd_ref = refs[:4]
        out_ref, a_ref, n_ref = refs[4 + n_head:7 + n_head]
        nb_ref, acc_ref = refs[-2:]
        i = pl.program_id(0)
        f = pl.program_id(1)

        @pl.when(f == 0)
        def _():
            n, _ = _rms_fwd(h_ref[...], g_ref[...])
            nb = n.astype(BF16)
            nb_ref[...] = nb
            n_ref[...] = nb
            acc_ref[...] = jnp.zeros_like(acc_ref)

        pre = _nn(nb_ref[...], wu_ref[...])
        a_ref[...] = pre.astype(BF16)
        r = jnp.square(jnp.maximum(pre, 0.0)).astype(BF16)
        acc_ref[...] += _nn(r, wd_ref[...])

        @pl.when(f == nf - 1)
        def _():
            out = h_ref[...] + acc_ref[...]
            if head is None:
                out_ref[...] = out
            else:
                gf_ref, t_ref = refs[4:6]
                loss_ref, dg_ref = refs[7 + n_head:9 + n_head]
                y, _ = _rms_fwd(out, gf_ref[...])
                err = y - t_ref[...]
                part = 0.5 * jnp.sum(jnp.mean(err * err, axis=-1, keepdims=True), axis=0,
                                     keepdims=True)
                dx, dg = _rms_bwd(err / d, out, gf_ref[...])
                out_ref[...] = dx
                part = jnp.broadcast_to(part, loss_ref.shape)

                @pl.when(i == 0)
                def _():
                    loss_ref[...] = part
                    dg_ref[...] = dg

                @pl.when(i > 0)
                def _():
                    loss_ref[...] += part
                    dg_ref[...] += dg

    rows = pl.BlockSpec((tm, d), lambda i, f: (i, 0))
    in_specs = [rows, _full(g),
                pl.BlockSpec((None, d, tf), lambda i, f: (f // per_slot, 0, f % per_slot)),
                pl.BlockSpec((tf, d), lambda i, f: (f, 0))]
    out_specs = [rows, pl.BlockSpec((tm, tf), lambda i, f: (i, f)), rows]
    out_shape = [jax.ShapeDtypeStruct((s, d), F32), jax.ShapeDtypeStruct((s, ff), BF16),
                 jax.ShapeDtypeStruct((s, d), BF16)]
    args = [h, g, w_up, w_down]
    if head is not None:
        in_specs += [_full(head[0]), rows]
        args += list(head)
        out_specs += [pl.BlockSpec((1, 128), lambda i, f: (0, 0)),
                      pl.BlockSpec((1, d), lambda i, f: (0, 0))]
        out_shape += [jax.ShapeDtypeStruct((1, 128), F32), jax.ShapeDtypeStruct((1, d), F32)]
    return pl.pallas_call(
        body, name=name, grid=(s // tm, nf),
        in_specs=in_specs, out_specs=out_specs, out_shape=out_shape,
        scratch_shapes=[pltpu.VMEM((tm, d), BF16), pltpu.VMEM((tm, d), F32)],
        compiler_params=_params(("parallel" if head is None else "arbitrary", "arbitrary")),
    )(*args)


def _window_sum_down(e, window):
    step = 1
    while step < window:
        e = e + pltpu.roll(e, step, axis=0)
        step *= 2
    return e


def _window_sum_up(e, window):
    n = e.shape[0]
    step = 1
    while step < window:
        e = e + pltpu.roll(e, n - step, axis=0)
        step *= 2
    return e


def _pool_counts(first_row, tm, window):
    t = first_row + _row((tm, 1))
    return jnp.minimum(t + 1, window).astype(F32)


def _pool_fwd(h, g, pw, ps):
    s, d = h.shape
    cg = d // len(POOL_WINDOWS)
    tm = min(TILE_ROWS, s)

    def body(h_ref, g_ref, pw_ref, ps_ref, out_ref, nbuf):
        i = pl.program_id(0)

        @pl.when(i == 0)
        def _():
            nbuf[0:POOL_HALO, :] = jnp.zeros((POOL_HALO, d), F32)

        n, _ = _rms_fwd(h_ref[...], g_ref[...])
        nbuf[POOL_HALO:POOL_HALO + tm, :] = n
        for k, window in enumerate(POOL_WINDOWS):
            cols = slice(k * cg, (k + 1) * cg)
            sums = _window_sum_down(nbuf[:, cols], window)[POOL_HALO:, :]
            pooled = sums / _pool_counts(i * tm, tm, window) - n[:, cols]
            y = _nn(pooled.astype(BF16), pw_ref[k]) * ps_ref[:, cols]
            out_ref[:, cols] = h_ref[:, cols] + y
        nbuf[0:POOL_HALO, :] = n[tm - POOL_HALO:tm, :]

    return pl.pallas_call(
        body, name="pool_fwd", grid=(s // tm,),
        in_specs=[pl.BlockSpec((tm, d), lambda i: (i, 0)), _full(g), _full(pw), _full(ps)],
        out_specs=pl.BlockSpec((tm, d), lambda i: (i, 0)),
        out_shape=jax.ShapeDtypeStruct((s, d), F32),
        scratch_shapes=[pltpu.VMEM((tm + POOL_HALO, d), F32)],
        compiler_params=_params(("arbitrary",)),
    )(h, g, pw, ps)


def _mlp_bwd_x(dz, a, w_up, w_down, h_in, g, name):
    s, d = dz.shape
    ff = w_down.shape[0]
    slot_cols = w_up.shape[2]
    tm = min(TILE_MLP_ROWS, s)
    tf = min(TILE_MLP_BWD_FF, slot_cols)
    per_slot = slot_cols // tf
    nf = ff // tf

    def body(dz_ref, a_ref, wu_ref, wd_ref, h_ref, g_ref, da_ref, dzb_ref, dh_ref, dg_ref,
             dzs_ref, acc_ref):
        i = pl.program_id(0)
        f = pl.program_id(1)

        @pl.when(f == 0)
        def _():
            dzb = dz_ref[...].astype(BF16)
            dzs_ref[...] = dzb
            dzb_ref[...] = dzb
            acc_ref[...] = jnp.zeros_like(acc_ref)

        dr = _nt(dzs_ref[...], wd_ref[...])
        da = (dr * (2.0 * jnp.maximum(a_ref[...].astype(F32), 0.0))).astype(BF16)
        da_ref[...] = da
        acc_ref[...] += _nt(da, wu_ref[...])

        @pl.when(f == nf - 1)
        def _():
            dx, dg = _rms_bwd(acc_ref[...], h_ref[...], g_ref[...])
            dh_ref[...] = dz_ref[...] + dx

            @pl.when(i == 0)
            def _():
                dg_ref[...] = dg

            @pl.when(i > 0)
            def _():
                dg_ref[...] += dg

    return pl.pallas_call(
        body, name=name, grid=(s // tm, nf),
        in_specs=[pl.BlockSpec((tm, d), lambda i, f: (i, 0)),
                  pl.BlockSpec((tm, tf), lambda i, f: (i, f)),
                  pl.BlockSpec((None, d, tf), lambda i, f: (f // per_slot, 0, f % per_slot)),
                  pl.BlockSpec((tf, d), lambda i, f: (f, 0)),
                  pl.BlockSpec((tm, d), lambda i, f: (i, 0)), _full(g)],
        out_specs=[pl.BlockSpec((tm, tf), lambda i, f: (i, f)),
                   pl.BlockSpec((tm, d), lambda i, f: (i, 0)),
                   pl.BlockSpec((tm, d), lambda i, f: (i, 0)),
                   pl.BlockSpec((1, d), lambda i, f: (0, 0))],
        out_shape=[jax.ShapeDtypeStruct((s, ff), BF16),
                   jax.ShapeDtypeStruct((s, d), BF16),
                   jax.ShapeDtypeStruct((s, d), F32),
                   jax.ShapeDtypeStruct((1, d), F32)],
        scratch_shapes=[pltpu.VMEM((tm, d), BF16), pltpu.VMEM((tm, d), F32)],
        compiler_params=_params(("arbitrary", "arbitrary")),
    )(dz, a, w_up, w_down, h_in, g)


def _mlp_bwd_w(n, da, a, dzb, slot_cols, name):
    s, d = n.shape
    ff = a.shape[1]
    tn = min(TILE_WGRAD_N, slot_cols)
    tk = min(TILE_WGRAD_K, s)
    per_slot = slot_cols // tn
    nk = s // tk

    def body(n_ref, da_ref, a_ref, dz_ref, du_ref, dd_ref, accu_ref, accd_ref):
        k = pl.program_id(1)

        @pl.when(k == 0)
        def _():
            accu_ref[...] = jnp.zeros_like(accu_ref)
            accd_ref[...] = jnp.zeros_like(accd_ref)

        accu_ref[...] += _tn(n_ref[...], da_ref[...])
        r = jnp.square(jnp.maximum(a_ref[...].astype(F32), 0.0)).astype(BF16)
        accd_ref[...] += _tn(r, dz_ref[...])

        @pl.when(k == nk - 1)
        def _():
            du_ref[...] = accu_ref[...].astype(BF16)
            dd_ref[...] = accd_ref[...].astype(BF16)

    return pl.pallas_call(
        body, name=name, grid=(ff // tn, nk),
        in_specs=[pl.BlockSpec((tk, d), lambda f, k: (k, 0)),
                  pl.BlockSpec((tk, tn), lambda f, k: (k, f)),
                  pl.BlockSpec((tk, tn), lambda f, k: (k, f)),
                  pl.BlockSpec((tk, d), lambda f, k: (k, 0))],
        out_specs=[pl.BlockSpec((None, d, tn), lambda f, k: (f // per_slot, 0, f % per_slot)),
                   pl.BlockSpec((tn, d), lambda f, k: (f, 0))],
        out_shape=[jax.ShapeDtypeStruct((ff // slot_cols, d, slot_cols), BF16),
                   jax.ShapeDtypeStruct((ff, d), BF16)],
        scratch_shapes=[pltpu.VMEM((d, tn), F32), pltpu.VMEM((tn, d), F32)],
        compiler_params=_params(("parallel", "arbitrary")),
    )(n, da, a, dzb)


def _pool_bwd(after, dh, h, g, pw, ps):
    s, d = h.shape
    cg = d // len(POOL_WINDOWS)
    tm = min(TILE_ROWS, s)
    nb = s // tm
    halo_per_tile = tm // POOL_HALO

    def body(after_ref, dh_ref, h_ref, halo_ref, g_ref, pw_ref, ps_ref,
             dx_ref, dpw_ref, dps_ref, dg_ref, nbuf, qbuf, dn_ref, carry, dpw_acc):
        i = pl.program_id(0)
        blk = nb - 1 - i

        @pl.when(i == 0)
        def _():
            carry[...] = jnp.zeros_like(carry)
            dpw_acc[...] = jnp.zeros_like(dpw_acc)
            dps_ref[...] = jnp.zeros_like(dps_ref)
            dg_ref[...] = jnp.zeros_like(dg_ref)

        hv = h_ref[...]
        n, _ = _rms_fwd(hv, g_ref[...])
        nh, _ = _rms_fwd(halo_ref[...], g_ref[...])
        nbuf[0:POOL_HALO, :] = jnp.where(blk == 0, 0.0, nh)
        nbuf[POOL_HALO:POOL_HALO + tm, :] = n
        dhv = dh_ref[...]
        for k, window in enumerate(POOL_WINDOWS):
            cols = slice(k * cg, (k + 1) * cg)
            cnt = _pool_counts(blk * tm, tm, window)
            sums = _window_sum_down(nbuf[:, cols], window)[POOL_HALO:, :]
            pb = (sums / cnt - n[:, cols]).astype(BF16)
            dyk = dhv[:, cols]
            dps_ref[:, cols] += jnp.sum(dyk * _nn(pb, pw_ref[k]), axis=0, keepdims=True)
            dyb = (dyk * ps_ref[:, cols]).astype(BF16)
            dpw_acc[k] += _tn(pb, dyb)
            dpool = _nt(dyb, pw_ref[k])
            qv = dpool / cnt
            qbuf[0:tm, cols] = qv
            qbuf[tm:tm + POOL_HALO, cols] = carry[:, cols]
            dn_ref[:, cols] = _window_sum_up(qbuf[:, cols], window)[0:tm, :] - dpool
            carry[:, cols] = qv[0:POOL_HALO, :]
        dx, dg = _rms_bwd(dn_ref[...], hv, g_ref[...])
        dx_ref[...] = dhv + dx
        dg_ref[...] += dg

        @pl.when(i == nb - 1)
        def _():
            dpw_ref[...] = dpw_acc[...].astype(BF16)

    rev = lambda i: (nb - 1 - i, 0)
    return pl.pallas_call(
        body, name="pool_bwd", grid=(nb,),
        in_specs=[ANY, pl.BlockSpec((tm, d), rev), pl.BlockSpec((tm, d), rev),
                  pl.BlockSpec((POOL_HALO, d),
                               lambda i: (jnp.maximum((nb - 1 - i) * halo_per_tile - 1, 0), 0)),
                  _full(g), _full(pw), _full(ps)],
        out_specs=[pl.BlockSpec((tm, d), rev), _full(pw),
                   pl.BlockSpec((1, d), lambda i: (0, 0)),
                   pl.BlockSpec((1, d), lambda i: (0, 0))],
        out_shape=[jax.ShapeDtypeStruct((s, d), F32),
                   jax.ShapeDtypeStruct(pw.shape, BF16),
                   jax.ShapeDtypeStruct((1, d), F32),
                   jax.ShapeDtypeStruct((1, d), F32)],
        scratch_shapes=[pltpu.VMEM((tm + POOL_HALO, d), F32), pltpu.VMEM((tm + POOL_HALO, d), F32),
                        pltpu.VMEM((tm, d), F32), pltpu.VMEM((POOL_HALO, d), F32),
                        pltpu.VMEM(pw.shape, F32)],
        compiler_params=_params(("arbitrary",)),
    )(after, dh, h, h, g, pw, ps)


def _conv_out_bwd(after, dh, w_out, o, bcx, cw):
    s, d = dh.shape
    c = o.shape[1]
    tm = min(TILE_ROWS, s)
    nb = s // tm
    halo_per_tile = tm // CONV_HALO

    def body(after_ref, dh_ref, w_ref, o_ref, b_ref, c_ref, xin_ref, ch_ref, xh_ref, cw_ref,
             do_ref, delta_ref, dbcx_ref, dw_ref, dcw_ref, ubuf, dbuf, carry, acc):
        i = pl.program_id(0)
        blk = nb - 1 - i

        @pl.when(i == 0)
        def _():
            carry[...] = jnp.zeros_like(carry)
            acc[...] = jnp.zeros_like(acc)
            dcw_ref[...] = jnp.zeros_like(dcw_ref)

        dm = dh_ref[...].astype(BF16)
        dcat = _nt(dm, w_ref[...])
        do = dcat[:, 0:c]
        dy = dcat[:, c:2 * c]
        do_ref[...] = do.astype(BF16)
        head_of_lane = lax.shift_right_logical(_lane((8, c)), HEAD_DIM.bit_length() - 1)
        heads = (head_of_lane == _row((8, c))).astype(BF16)
        delta_ref[...] = _exact_nt(heads, do * o_ref[...].astype(F32))

        cv_ = c_ref[...]
        xin = xin_ref[...]
        bv = b_ref[...]
        u = cv_ * xin
        ubuf[0:CONV_HALO, :] = jnp.where(blk == 0, 0.0, ch_ref[...] * xh_ref[...])
        ubuf[CONV_HALO:CONV_HALO + tm, :] = u
        u1 = ubuf[CONV_HALO - 1:CONV_HALO - 1 + tm, :]
        u2 = ubuf[CONV_HALO - 2:CONV_HALO - 2 + tm, :]
        w0, w1, w2 = cw_ref[0:1, :], cw_ref[1:2, :], cw_ref[2:3, :]
        cv = (w0 * u2 + w1 * u1) + w2 * u
        acc[0:c, :] += _tn(o_ref[...], dm)
        acc[c:2 * c, :] += _tn((bv * cv).astype(BF16), dm)

        dcv = dy * bv
        dcw_ref[0:1, :] += jnp.sum(dcv * u2, axis=0, keepdims=True)
        dcw_ref[1:2, :] += jnp.sum(dcv * u1, axis=0, keepdims=True)
        dcw_ref[2:3, :] += jnp.sum(dcv * u, axis=0, keepdims=True)
        dbuf[0:tm, :] = dcv
        dbuf[tm:tm + CONV_HALO, :] = carry[...]
        du = w2 * dcv + w1 * dbuf[1:1 + tm, :] + w0 * dbuf[2:2 + tm, :]
        dbcx_ref[:, 0:c] = (dy * cv).astype(BF16)
        dbcx_ref[:, c:2 * c] = (du * xin).astype(BF16)
        dbcx_ref[:, 2 * c:3 * c] = (du * cv_).astype(BF16)
        carry[...] = dcv[0:CONV_HALO, :]

        @pl.when(i == nb - 1)
        def _():
            dw_ref[...] = acc[...].astype(BF16)

    rev = lambda k: (lambda i: (nb - 1 - i, k))
    halo = lambda k: (lambda i: (jnp.maximum((nb - 1 - i) * halo_per_tile - 1, 0), k))
    return pl.pallas_call(
        body, name="conv_out_bwd", grid=(nb,),
        in_specs=[ANY, pl.BlockSpec((tm, d), rev(0)), _full(w_out), pl.BlockSpec((tm, c), rev(0)),
                  pl.BlockSpec((tm, c), rev(0)), pl.BlockSpec((tm, c), rev(1)),
                  pl.BlockSpec((tm, c), rev(2)),
                  pl.BlockSpec((CONV_HALO, c), halo(1)), pl.BlockSpec((CONV_HALO, c), halo(2)),
                  _full(cw)],
        out_specs=[pl.BlockSpec((tm, c), rev(0)),
                   pl.BlockSpec((8, tm), lambda i: (0, nb - 1 - i)),
                   pl.BlockSpec((tm, 3 * c), rev(0)),
                   _full(w_out), _full(cw)],
        out_shape=[jax.ShapeDtypeStruct((s, c), BF16),
                   jax.ShapeDtypeStruct((8, s), F32),
                   jax.ShapeDtypeStruct((s, 3 * c), BF16),
                   jax.ShapeDtypeStruct(w_out.shape, BF16),
                   jax.ShapeDtypeStruct(cw.shape, F32)],
        scratch_shapes=[pltpu.VMEM((tm + CONV_HALO, c), F32), pltpu.VMEM((tm + CONV_HALO, c), F32),
                        pltpu.VMEM((CONV_HALO, c), F32), pltpu.VMEM(w_out.shape, F32)],
        compiler_params=_params(("arbitrary",)),
    )(after, dh, w_out, o, bcx, bcx, bcx, bcx, bcx, cw)


def _attn_bwd(after, qa, ka, qkv, do, lse, delta):
    s = qa.shape[1]
    a = N_HEADS * HEAD_DIM
    t = min(TILE_ATTN, s)
    nq = s // t
    n_pairs = N_HEADS // 2
    v_block0 = 2 * a // 128

    def body(after_ref, ka_ref, v_ref, qa_ref, do_ref, lse_ref, delta_ref,
             dqt_ref, dka_ref, dv_ref, dk_acc, dv_acc):
        g = pl.program_id(0)
        j = pl.program_id(1)

        @pl.when(j == 0)
        def _():
            dqt_ref[...] = jnp.zeros_like(dqt_ref)

        dk_acc[...] = jnp.zeros_like(dk_acc)
        dv_acc[...] = jnp.zeros_like(dv_acc)
        lane = _lane((t, 128))
        vf = v_ref[...].astype(F32)
        v_heads = [jnp.where(lane < HEAD_DIM, vf, 0.0).astype(BF16),
                   jnp.where(lane >= HEAD_DIM, vf, 0.0).astype(BF16)]
        ke_t = [ka_ref[e].astype(F32).T.astype(BF16) for e in range(2)]

        def q_step(i, masked):
            qs = pl.ds(pl.multiple_of(i * t, t), t)
            dob = do_ref[qs, :]
            for e in range(2):
                qe = qa_ref[e, qs, :]
                sc = _nt(ka_ref[e], qe)
                if masked:
                    sc = jnp.where(_row((t, t)) <= _lane((t, t)), sc, NEG_BIG)
                p = jnp.exp2(sc - lse_ref[pl.ds(e, 1), qs])
                dv_acc[e] += _nn(p.astype(BF16), dob)
                dp = _nt(v_heads[e], dob)
                ds = (p * (dp - delta_ref[pl.ds(2 * g + e, 1), qs])).astype(BF16)
                dk_acc[e] += _nn(ds, qe)
                dqt_ref[e, :, qs] += _nn(ke_t[e], ds)

        q_step(j, True)

        def full_step(i, carry):
            q_step(i, False)
            return carry

        lax.fori_loop(j + 1, nq, full_step, 0)
        dka_ref[...] = dk_acc[...]
        dv_ref[...] = jnp.where(lane < HEAD_DIM, dv_acc[0], dv_acc[1]).astype(BF16)

    return pl.pallas_call(
        body, name="attn_bwd", grid=(n_pairs, nq),
        in_specs=[ANY, pl.BlockSpec((2, t, 128), lambda g, j: (g, j, 0)),
                  pl.BlockSpec((t, 128), lambda g, j: (j, v_block0 + g)),
                  pl.BlockSpec((2, s, 128), lambda g, j: (g, 0, 0)),
                  pl.BlockSpec((s, 128), lambda g, j: (0, g)),
                  pl.BlockSpec((None, 8, s), lambda g, j: (g, 0, 0)),
                  pl.BlockSpec((8, s), lambda g, j: (0, 0))],
        out_specs=[pl.BlockSpec((2, 128, s), lambda g, j: (g, 0, 0)),
                   pl.BlockSpec((2, t, 128), lambda g, j: (g, j, 0)),
                   pl.BlockSpec((t, 128), lambda g, j: (j, g))],
        out_shape=[jax.ShapeDtypeStruct((N_HEADS, 128, s), F32),
                   jax.ShapeDtypeStruct((N_HEADS, s, 128), F32),
                   jax.ShapeDtypeStruct((s, a), BF16)],
        scratch_shapes=[pltpu.VMEM((2, t, 128), F32), pltpu.VMEM((2, t, 128), F32)],
        compiler_params=_params(("parallel", "arbitrary")),
    )(after, ka, qkv, qa, do, lse, delta)


def _gate_bwd(dqa, dka, dv, fl, bf):
    s = fl.shape[0]
    a = N_HEADS * HEAD_DIM
    tm = min(TILE_ROWS, s)
    nb = s // tm

    def body(dqa_ref, dka_ref, dv_ref, fl_ref, bf_ref, dqkv_ref, dfl_ref, dbf_ref, carry):
        i = pl.program_id(0)

        @pl.when(i == 0)
        def _():
            carry[...] = jnp.zeros_like(carry)
            dbf_ref[...] = jnp.zeros_like(dbf_ref)

        lane = _lane((tm, 128))
        dq_sum = jnp.zeros((tm, 128), F32)
        dk_sum = jnp.zeros((tm, 128), F32)
        for pair in range(N_HEADS // 2):
            qs, ks = [], []
            for e in range(2):
                h = 2 * pair + e
                dq = dqa_ref[h].T
                dk = dka_ref[h]
                dq_sum = dq_sum + dq
                dk_sum = dk_sum + dk
                qs.append(dq * ATTN_SCALE)
                ks.append(dk * (1.0 / LOG2_E))
            cols = slice(pair * 128, (pair + 1) * 128)
            dqkv_ref[:, cols] = jnp.where(
                lane < HEAD_DIM, qs[0], pltpu.roll(qs[1], HEAD_DIM, axis=1)).astype(BF16)
            dqkv_ref[:, a + pair * 128:a + (pair + 1) * 128] = jnp.where(
                lane < HEAD_DIM, ks[0], pltpu.roll(ks[1], HEAD_DIM, axis=1)).astype(BF16)
        dqkv_ref[:, 2 * a:3 * a] = dv_ref[...]

        in_q = (lane >= LANE_CQ) & (lane < LANE_CQ + N_HEADS)
        in_k = (lane >= LANE_CK) & (lane < LANE_CK + N_HEADS)
        dcum = (pltpu.roll(jnp.where(in_q, dq_sum, 0.0), 128 - LANE_CQ, axis=1)
                - pltpu.roll(jnp.where(in_k, dk_sum, 0.0), 128 - LANE_CK, axis=1))

        upper = (_lane((tm, tm)) >= _row((tm, tm))).astype(BF16)
        dlogf = _exact_nn(upper, dcum) + carry[0:1, :]
        carry[0:1, :] = dlogf[0:1, :]
        z = fl_ref[...] + bf_ref[...]
        ez = jnp.exp(-jnp.abs(z))
        sig_neg = jnp.where(z >= 0.0, ez, 1.0) / (1.0 + ez)
        dz = jnp.where(lane < N_HEADS, dlogf * sig_neg, 0.0)
        dfl_ref[...] = dz.astype(BF16)
        dbf_ref[...] += jnp.sum(dz, axis=0, keepdims=True)

    rev3 = lambda i: (0, nb - 1 - i, 0)
    rev = lambda i: (nb - 1 - i, 0)
    return pl.pallas_call(
        body, name="gate_bwd", grid=(nb,),
        in_specs=[pl.BlockSpec((N_HEADS, 128, tm), lambda i: (0, 0, nb - 1 - i)),
                  pl.BlockSpec((N_HEADS, tm, 128), rev3),
                  pl.BlockSpec((tm, a), rev), pl.BlockSpec((tm, 128), rev), _full(bf)],
        out_specs=[pl.BlockSpec((tm, 3 * a), rev), pl.BlockSpec((tm, 128), rev),
                   pl.BlockSpec((1, 128), lambda i: (0, 0))],
        out_shape=[jax.ShapeDtypeStruct((s, 3 * a), BF16),
                   jax.ShapeDtypeStruct((s, 128), BF16),
                   jax.ShapeDtypeStruct((1, 128), F32)],
        scratch_shapes=[pltpu.VMEM((8, 128), F32)],
        compiler_params=_params(("arbitrary",)),
    )(dqa, dka, dv, fl, bf)


def _in_proj_bwd(after, dqkv, dfl, dbcx, w_qkv, w_f, w_bcx, x, g, dh):
    s, d = x.shape
    tm = min(TILE_ROWS, s)

    def body(after_ref, dq_ref, df_ref, db_ref, wq_ref, wf_ref, wb_ref, x_ref, g_ref, dh_ref,
             gx_ref, dg_ref):
        i = pl.program_id(0)
        dn = (_nt(dq_ref[...], wq_ref[...]) + _nt(df_ref[...], wf_ref[...])
              + _nt(db_ref[...], wb_ref[...]))
        dx, dg = _rms_bwd(dn, x_ref[...], g_ref[...])
        gx_ref[...] = dh_ref[...] + dx

        @pl.when(i == 0)
        def _():
            dg_ref[...] = dg

        @pl.when(i > 0)
        def _():
            dg_ref[...] += dg

    rows = lambda c: pl.BlockSpec((tm, c), lambda i: (i, 0))
    return pl.pallas_call(
        body, name="in_proj_bwd", grid=(s // tm,),
        in_specs=[ANY, rows(dqkv.shape[1]), rows(dfl.shape[1]), rows(dbcx.shape[1]),
                  _full(w_qkv), _full(w_f), _full(w_bcx), rows(d), _full(g), rows(d)],
        out_specs=[rows(d), pl.BlockSpec((1, d), lambda i: (0, 0))],
        out_shape=[jax.ShapeDtypeStruct((s, d), F32), jax.ShapeDtypeStruct((1, d), F32)],
        compiler_params=_params(("arbitrary",)),
    )(after, dqkv, dfl, dbcx, w_qkv, w_f, w_bcx, x, g, dh)


def _wgrad_in(n, dys):
    s, d = n.shape
    m = len(dys)
    tk = min(TILE_ROWS, s)
    nk = s // tk

    def body(*refs):
        n_ref, dy_refs, dw_refs, accs = refs[0], refs[1:1 + m], refs[1 + m:1 + 2 * m], refs[1 + 2 * m:]
        k = pl.program_id(0)

        @pl.when(k == 0)
        def _():
            for acc in accs:
                acc[...] = jnp.zeros_like(acc)

        nb = n_ref[...]
        for dy_ref, acc in zip(dy_refs, accs):
            acc[...] += _tn(nb, dy_ref[...])

        @pl.when(k == nk - 1)
        def _():
            for dw_ref, acc in zip(dw_refs, accs):
                dw_ref[...] = acc[...].T.astype(BF16)

    return pl.pallas_call(
        body, name="wgrad_in", grid=(nk,),
        in_specs=[pl.BlockSpec((tk, d), lambda k: (k, 0))]
        + [pl.BlockSpec((tk, dy.shape[1]), lambda k: (k, 0)) for dy in dys],
        out_specs=[pl.BlockSpec((dy.shape[1], d), lambda k: (0, 0)) for dy in dys],
        out_shape=[jax.ShapeDtypeStruct((dy.shape[1], d), BF16) for dy in dys],
        scratch_shapes=[pltpu.VMEM((d, dy.shape[1]), F32) for dy in dys],
        compiler_params=_params(("arbitrary",)),
    )(n, *dys)


def _row_tile(rows):
    t = min(TILE_ELEM_ROWS, rows)
    while rows % t:
        t //= 2
    return t


def _adamw_math(w, g, m, v):
    m = ADAM_B1 * m + (1.0 - ADAM_B1) * g
    v = ADAM_B2 * v + (1.0 - ADAM_B2) * jnp.square(g)
    m_hat = m / (1.0 - ADAM_B1 ** ADAM_STEP)
    v_hat = v / (1.0 - ADAM_B2 ** ADAM_STEP)
    delta = -ADAM_LR * (m_hat / (jnp.sqrt(v_hat) + ADAM_EPS) + ADAM_WD * w)
    return delta, m, v


def _adamw(w, g, m, v, name):
    rows, cols = w.shape

    def body(w_ref, g_ref, m_ref, v_ref, d_ref, nm_ref, nv_ref):
        delta, nm, nv = _adamw_math(w_ref[...], g_ref[...], m_ref[...], v_ref[...])
        d_ref[...] = delta
        nm_ref[...] = nm
        nv_ref[...] = nv

    if rows % 8 == 0:
        tr = _row_tile(rows)
        grid, spec = (rows // tr,), pl.BlockSpec((tr, cols), lambda i: (i, 0))
    else:
        grid, spec = (cols // 256,), pl.BlockSpec((rows, 256), lambda i: (0, i))
    out = jax.ShapeDtypeStruct(w.shape, F32)
    return pl.pallas_call(
        body, name=name, grid=grid, in_specs=[spec] * 4, out_specs=[spec] * 3,
        out_shape=[out, out, out], compiler_params=_params(("parallel",)),
    )(w, g, m, v)


def _sum_devices(parts):
    def body(p_ref, g_ref):
        g = p_ref[0]
        for k in range(1, N_DEV):
            g = g + p_ref[k]
        g_ref[...] = g

    return pl.pallas_call(
        body, name="sum_devices",
        in_specs=[pl.BlockSpec(memory_space=pltpu.VMEM)],
        out_specs=pl.BlockSpec(memory_space=pltpu.VMEM),
        out_shape=jax.ShapeDtypeStruct(parts.shape[1:], F32),
    )(parts)


def _mesh_position():
    x, y, c = lax.axis_index("x"), lax.axis_index("y"), lax.axis_index("c")
    chips = [(1 - x, y), (x, 1 - y), (1 - x, 1 - y)]
    return x, y, c, chips


ANY = pl.BlockSpec(memory_space=pl.ANY)
HBM = pl.BlockSpec(memory_space=pltpu.HBM)
SEM = pl.BlockSpec(memory_space=pltpu.SEMAPHORE)
SPLIT_COPY_EFFECT = pltpu.SideEffectType.DATAFLOW_SIDE_EFFECTING


def _in_hbm(a):
    return pltpu.with_memory_space_constraint(a, pltpu.HBM)


def _chip_copies(views, srcs, lands, send, recv, waiting=False):
    _, _, c, chips = _mesh_position()
    cps = []
    for a in range(len(srcs)):
        for k, (px, py) in enumerate(chips):
            src, dst = views(a, k, srcs[a], lands[a], c, 2 * px + py)
            sem = a * (N_CHIPS - 1) + k
            cps.append(pltpu.make_async_remote_copy(
                src_ref=src, dst_ref=dst, send_sem=send.at[sem], recv_sem=recv.at[sem],
                device_id=(px, py, c), device_id_type=MESH))
    return cps


def _ici_start(sources, land_shapes, copies, after, name, per_array=N_CHIPS - 1):
    n = len(sources)

    def body(*refs):
        srcs, lands = refs[:n], refs[n:2 * n]
        send, recv = refs[2 * n + 1], refs[2 * n + 2]
        token = refs[-1]
        for cp in copies(srcs, lands, send, recv, False):
            cp.start()
        token[...] = jnp.zeros_like(token)

    lands = [_in_hbm(lax.empty(s.shape, s.dtype)) for s in land_shapes]
    outs = pl.pallas_call(
        body, name=name,
        in_specs=[HBM] * (2 * n) + [ANY],
        out_specs=[SEM, SEM] + [HBM] * (2 * n) + [pl.BlockSpec(memory_space=pltpu.VMEM)],
        out_shape=[pltpu.SemaphoreType.DMA((n * per_array,))] * 2
        + [pltpu.HBM(a.shape, a.dtype) for a in sources]
        + [pltpu.HBM(s.shape, s.dtype) for s in land_shapes]
        + [jax.ShapeDtypeStruct((8, 128), F32)],
        input_output_aliases={i: 2 + i for i in range(2 * n)},
        compiler_params=pltpu.CompilerParams(has_side_effects=SPLIT_COPY_EFFECT),
    )(*[_in_hbm(a) for a in sources], *lands, after)
    return outs[0], outs[1], list(outs[2:2 + n]), list(outs[2 + n:2 + 2 * n]), outs[-1]


def _ici_wait(handle, copies, after, name):
    send, recv, srcs, lands, _ = handle
    n = len(srcs)

    def body(*refs):
        src_refs, land_refs = refs[:n], refs[n:2 * n]
        for cp in copies(src_refs, land_refs, refs[2 * n], refs[2 * n + 1], True):
            cp.wait_send()
            cp.wait_recv()

    outs = pl.pallas_call(
        body, name=name,
        in_specs=[HBM] * (2 * n) + [SEM, SEM, ANY],
        out_specs=[HBM] * (2 * n),
        out_shape=[pltpu.HBM(a.shape, a.dtype) for a in srcs]
        + [pltpu.HBM(a.shape, a.dtype) for a in lands],
        input_output_aliases={i: i for i in range(2 * n)},
        compiler_params=pltpu.CompilerParams(has_side_effects=SPLIT_COPY_EFFECT),
    )(*srcs, *lands, send, recv, after)
    return list(outs[:n]), list(outs[n:])


def _gather_views(split):
    def views(a, k, src, land, c, slot):
        if split[a]:
            half = src.shape[0] // 2
            src = src.at[pl.ds(c * half, half)]
        return src, land.at[k]
    return views


def _gather_whole_views(a, k, src, land, c, slot):
    x, y, _, _ = _mesh_position()
    return src, land.at[2 * x + y]


SCATTER_COPIES = 2 * (N_CHIPS - 1)


def _scatter_copies(srcs, lands, send, recv, waiting):
    _, _, c, chips = _mesh_position()
    cps = []
    for a in range(len(srcs)):
        half = srcs[a].shape[1] // 2
        for k, (px, py) in enumerate(chips):
            for h in range(2):
                arrival = 2 * k + (h if waiting else c)
                cps.append(pltpu.make_async_remote_copy(
                    src_ref=srcs[a].at[2 * px + py, pl.ds(h * half, half)],
                    dst_ref=lands[a].at[arrival],
                    send_sem=send.at[a * SCATTER_COPIES + 2 * k + h],
                    recv_sem=recv.at[a * SCATTER_COPIES + arrival],
                    device_id=(px, py, h), device_id_type=MESH))
    return cps


def _gather_land_shapes(shards, split):
    return [jax.ShapeDtypeStruct(
        (N_CHIPS - 1, a.shape[0] // 2 if sp else a.shape[0]) + a.shape[1:], a.dtype)
        for a, sp in zip(shards, split)]


def _gather_finish(shards, lands, split, name):
    n = len(shards)
    ns = sum(split)
    d_index = {a: i for i, a in enumerate(a for a in range(n) if split[a])}

    def body(*refs):
        shard, land, outs = refs[:n], refs[n:2 * n], refs[2 * n:3 * n]
        obuf, fbuf = refs[3 * n:4 * n], refs[4 * n:5 * n]
        dbuf = refs[5 * n:5 * n + ns]
        ld_own, st_own, ld, st_mine, st_sib, send, recv = refs[5 * n + ns:]
        x, y, c, chips = _mesh_position()
        me = 2 * x + y
        own_loads, loads, sends, pending = [], {}, [], []
        for a in range(n):
            cp = pltpu.make_async_copy(shard[a], obuf[a], ld_own.at[a])
            cp.start()
            own_loads.append(cp)
        for a in range(n):
            for k in range(N_CHIPS - 1):
                cp = pltpu.make_async_copy(land[a].at[k], fbuf[a].at[k], ld.at[a, k])
                cp.start()
                loads[a, k] = cp
        for a in range(n):
            own_loads[a].wait()
            cp = pltpu.make_async_copy(obuf[a], outs[a].at[me], st_own.at[a])
            cp.start()
            pending.append(cp)
        for a in range(n):
            rows = shard[a].shape[0]
            for k, (px, py) in enumerate(chips):
                loads[a, k].wait()
                part = pl.ds(c * (rows // 2), rows // 2) if split[a] else pl.ds(0, rows)
                cp = pltpu.make_async_copy(fbuf[a].at[k], outs[a].at[2 * px + py, part],
                                           st_mine.at[a, k])
                cp.start()
                pending.append(cp)
                if split[a]:
                    fw = pltpu.make_async_remote_copy(
                        src_ref=fbuf[a].at[k], dst_ref=dbuf[d_index[a]].at[k],
                        send_sem=send.at[a, k], recv_sem=recv.at[a, k],
                        device_id=(x, y, 1 - c), device_id_type=MESH)
                    fw.start()
                    sends.append((a, k, fw))
        for a, k, fw in sends:
            px, py = chips[k]
            half = shard[a].shape[0] // 2
            fw.wait_recv()
            cp = pltpu.make_async_copy(dbuf[d_index[a]].at[k],
                                       outs[a].at[2 * px + py, pl.ds((1 - c) * half, half)],
                                       st_sib.at[a, k])
            cp.start()
            pending.append(cp)
        for _, _, fw in sends:
            fw.wait_send()
        for cp in pending:
            cp.wait()

    stage = [pltpu.VMEM(a.shape, a.dtype) for a in lands]
    dma = lambda *shape: pltpu.SemaphoreType.DMA(shape)
    return pl.pallas_call(
        body, name=name,
        in_specs=[ANY] * (2 * n), out_specs=[ANY] * n,
        out_shape=[jax.ShapeDtypeStruct((N_CHIPS,) + a.shape, a.dtype) for a in shards],
        scratch_shapes=[pltpu.VMEM(a.shape, a.dtype) for a in shards] + stage
        + [s for s, sp in zip(stage, split) if sp]
        + [dma(n), dma(n), dma(n, 3), dma(n, 3), dma(n, 3), dma(n, 3), dma(n, 3)],
        compiler_params=pltpu.CompilerParams(vmem_limit_bytes=VMEM_LIMIT_BYTES),
    )(*shards, *lands)


def _sum_chunk(rows):
    return next(r for r in range(SUM_CHUNK_ROWS, 0, -16) if rows % r == 0)


def _sum_and_share(partials, lands, name):
    n = len(partials)

    def body(*refs):
        own, landed, outs = refs[:n], refs[n:2 * n], refs[2 * n:3 * n]
        obuf, xbuf, ybuf, gbuf, sbuf, rbuf = (refs[(3 + k) * n:(4 + k) * n] for k in range(6))
        ld_own, ld_send, ld_got, st_own, st_sib, send_p, recv_p, send_s, recv_s = refs[9 * n:]
        x, y, c, _ = _mesh_position()
        me = 2 * x + y
        sibling = (x, y, 1 - c)

        def to_sibling(src, dst, send, recv, a):
            return pltpu.make_async_remote_copy(src_ref=src, dst_ref=dst, send_sem=send.at[a],
                                                recv_sem=recv.at[a], device_id=sibling,
                                                device_id_type=MESH)

        loads, firsts, seconds, stores = [], [], [], []
        for a in range(n):
            half = obuf[a].shape[0]
            cps = [pltpu.make_async_copy(own[a].at[me, pl.ds((1 - c) * half, half)], xbuf[a],
                                         ld_send.at[a]),
                   pltpu.make_async_copy(own[a].at[me, pl.ds(c * half, half)], obuf[a], ld_own.at[a]),
                   pltpu.make_async_copy(landed[a], gbuf[a], ld_got.at[a])]
            for cp in cps:
                cp.start()
            loads.append(cps)
        for a in range(n):
            loads[a][0].wait()
            rc = to_sibling(xbuf[a], ybuf[a], send_p, recv_p, a)
            rc.start()
            firsts.append(rc)
        for a in range(n):
            firsts[a].wait_recv()
            loads[a][1].wait()
            loads[a][2].wait()
            half = obuf[a].shape[0]
            rows = _sum_chunk(half)

            def add(k, carry, a=a, rows=rows):
                at = pl.ds(pl.multiple_of(k * rows, rows), rows)
                acc = obuf[a][at].astype(F32) + ybuf[a][at].astype(F32)
                for j in range(SCATTER_COPIES):
                    acc = acc + gbuf[a][j, at].astype(F32)
                sbuf[a][at] = acc
                return carry

            lax.fori_loop(0, half // rows, add, 0)
            rc = to_sibling(sbuf[a], rbuf[a], send_s, recv_s, a)
            rc.start()
            seconds.append(rc)
            cp = pltpu.make_async_copy(sbuf[a], outs[a].at[pl.ds(c * half, half)], st_own.at[a])
            cp.start()
            stores.append(cp)
        for a in range(n):
            half = obuf[a].shape[0]
            seconds[a].wait_recv()
            cp = pltpu.make_async_copy(rbuf[a], outs[a].at[pl.ds((1 - c) * half, half)], st_sib.at[a])
            cp.start()
            stores.append(cp)
        for rc in firsts + seconds:
            rc.wait_send()
        for cp in stores:
            cp.wait()

    halves = [(a.shape[1] // 2, a.shape[2]) for a in partials]
    return pl.pallas_call(
        body, name=name,
        in_specs=[ANY] * (2 * n), out_specs=[ANY] * n,
        out_shape=[jax.ShapeDtypeStruct((2 * h[0], h[1]), F32) for h in halves],
        scratch_shapes=[pltpu.VMEM(h, BF16) for h in halves] * 3
        + [pltpu.VMEM(g.shape, BF16) for g in lands]
        + [pltpu.VMEM(h, F32) for h in halves] * 2
        + [pltpu.SemaphoreType.DMA((n,))] * 9,
        compiler_params=pltpu.CompilerParams(vmem_limit_bytes=VMEM_LIMIT_BYTES),
    )(*partials, *lands)


def _gather_small(part):
    def body(in_ref, out_ref, send, recv, local):
        x, y, c, _ = _mesh_position()
        me = 4 * x + 2 * y + c
        cps = [pltpu.make_async_copy(in_ref, out_ref.at[me], local)]
        k = 0
        for fx in range(2):
            for fy in range(2):
                for fc in range(2):
                    if fx or fy or fc:
                        cps.append(pltpu.make_async_remote_copy(
                            src_ref=in_ref, dst_ref=out_ref.at[me], send_sem=send.at[k],
                            recv_sem=recv.at[k], device_id=(x ^ fx, y ^ fy, c ^ fc),
                            device_id_type=MESH))
                        k += 1
        for cp in cps:
            cp.start()
        for cp in cps:
            cp.wait()

    return pl.pallas_call(
        body, name="gather_small",
        in_specs=[pl.BlockSpec(memory_space=pltpu.VMEM)],
        out_specs=pl.BlockSpec(memory_space=pltpu.VMEM),
        out_shape=jax.ShapeDtypeStruct((N_DEV,) + part.shape, part.dtype),
        scratch_shapes=[pltpu.SemaphoreType.DMA((N_DEV - 1,)), pltpu.SemaphoreType.DMA((N_DEV - 1,)),
                        pltpu.SemaphoreType.DMA],
    )(part)


def _scatter_start(grads, after, tag):
    lands = [jax.ShapeDtypeStruct((SCATTER_COPIES, g.shape[1] // 2, g.shape[2]), g.dtype)
             for g in grads]
    return _ici_start(grads, lands, _scatter_copies, after, "scatter_start_" + tag,
                      per_array=SCATTER_COPIES)


def _scatter_finish(handle, after, tag):
    grads, lands = _ici_wait(handle, _scatter_copies, after, "scatter_wait_" + tag)
    return _sum_and_share(grads, lands, "sum_and_share_" + tag)


def _pad_rows(a, rows):
    return jnp.pad(a, ((0, rows - a.shape[0]), (0, 0)))


def kernel(x, norm_mix_0, w_in_0, b_f_0, conv_w_0, w_out_0, norm_ffn_0, w_up_0, w_down_0, norm_mix_1, pool_w_1, pool_scale_1, norm_ffn_1, w_up_1, w_down_1, final_norm, loss_target, m_norm_mix_0, m_w_in_0, m_b_f_0, m_conv_w_0, m_w_out_0, m_norm_ffn_0, m_w_up_0, m_w_down_0, m_norm_mix_1, m_pool_w_1, m_pool_scale_1, m_norm_ffn_1, m_w_up_1, m_w_down_1, m_final_norm, v_norm_mix_0, v_w_in_0, v_b_f_0, v_conv_w_0, v_w_out_0, v_norm_ffn_0, v_w_up_0, v_w_down_0, v_norm_mix_1, v_pool_w_1, v_pool_scale_1, v_norm_ffn_1, v_w_up_1, v_w_down_1, v_final_norm):
    d = x.shape[-1]
    a = N_HEADS * HEAD_DIM
    c_conv = conv_w_0.shape[1] * N_CHIPS
    xs = x[0]
    target = loss_target[0]
    row = lambda vec: vec.reshape(1, -1)

    big = [w_in_0, w_out_0, w_up_0, w_down_0, pool_w_1, w_up_1, w_down_1]
    first = [w_in_0.astype(BF16)]
    first_split = [True]
    copies_a = functools.partial(_chip_copies, _gather_views(first_split))
    copies_b = functools.partial(_chip_copies, _gather_whole_views)
    start_a = _ici_start(first, _gather_land_shapes(first, first_split), copies_a, b_f_0,
                         "gather_start_a")
    zero = start_a[-1][0, 0]
    rest = [(w + zero).astype(BF16)
            for w in (w_out_0, w_up_0, w_down_0, pool_w_1, w_up_1, w_down_1)]
    rest = rest + [conv_w_0]
    start_b = _ici_start(rest, [jax.ShapeDtypeStruct((N_CHIPS,) + w.shape, w.dtype) for w in rest],
                         copies_b, start_a[-1], "gather_start_b")
    n0 = _rms_pre(start_b[-1], xs, row(norm_mix_0))
    first, land_a = _ici_wait(start_a, copies_a, n0, "gather_wait_a")
    (g_in,) = _gather_finish(first, land_a, first_split, "gather_finish_a")
    w_in = g_in.transpose(1, 0, 2).reshape(d, -1)
    w_qkv = w_in[:, :3 * a]
    w_f = jnp.pad(w_in[:, 3 * a:3 * a + N_HEADS], ((0, 0), (0, 128 - N_HEADS)))
    w_bcx = w_in[:, 3 * a + N_HEADS:]
    bf = jnp.pad(b_f_0, (0, 128 - N_HEADS)).reshape(1, 128)

    qkv, fl, bcx = _in_proj(n0, w_qkv, w_f, w_bcx)
    qa, ka = _gate_prep(fl, bf, qkv)
    o, lse = _attn_fwd(qa, ka, qkv)
    rest, land_b = _ici_wait(start_b, copies_b, o, "gather_wait_b")
    own_slot = 2 * lax.axis_index("x") + lax.axis_index("y")
    g_out, g_up0, g_down0, g_pool, g_up1, g_down1, g_conv = [
        lax.dynamic_update_index_in_dim(land, shard, own_slot, 0)
        for land, shard in zip(land_b, rest)]
    w_out = g_out.reshape(-1, d)
    conv_w = _pad_rows(g_conv.transpose(1, 0, 2).reshape(conv_w_0.shape[0], c_conv), 8)
    h1 = _conv_out(o, bcx, conv_w, w_out, xs)
    w_down0 = g_down0.reshape(-1, d)
    w_down1 = g_down1.reshape(-1, d)
    pool_w = g_pool.transpose(1, 0, 2, 3).reshape(pool_w_1.shape[0], -1, pool_w_1.shape[2])
    h2, a0, nf0 = _mlp_fwd(h1, row(norm_ffn_0), g_up0, w_down0, "mlp_fwd_0")
    h3 = _pool_fwd(h2, row(norm_mix_1), pool_w, row(pool_scale_1))
    dh4, a1, nf1, loss_part, d_final = _mlp_fwd(h3, row(norm_ffn_1), g_up1, w_down1, "mlp_fwd_1",
                                                head=(row(final_norm), target))

    slot_cols = g_up0.shape[2]
    pool_cols = pool_w.shape[2]
    da1, dz1, dh3, d_nffn1 = _mlp_bwd_x(dh4, a1, g_up1, w_down1, h3, row(norm_ffn_1), "mlp_bwd_x_1")
    dw_up1, dw_down1 = _mlp_bwd_w(nf1, da1, a1, dz1, slot_cols, "mlp_bwd_w_1")
    scatter_1 = _scatter_start([dw_up1, dw_down1.reshape(N_CHIPS, -1, d)], bf, "mlp1")
    dh2, dw_pool, d_pscale, d_nmix1 = _pool_bwd(scatter_1[-1], dh3, h2, row(norm_mix_1), pool_w,
                                                row(pool_scale_1))
    da0, dz0, dh1, d_nffn0 = _mlp_bwd_x(dh2, a0, g_up0, w_down0, h1, row(norm_ffn_0), "mlp_bwd_x_0")
    dw_up0, dw_down0 = _mlp_bwd_w(nf0, da0, a0, dz0, slot_cols, "mlp_bwd_w_0")
    dw_pool = (dw_pool.reshape(pool_w.shape[0], N_CHIPS, -1, pool_cols).transpose(1, 0, 2, 3)
               .reshape(N_CHIPS, -1, pool_cols))
    scatter_0 = _scatter_start([dw_up0, dw_down0.reshape(N_CHIPS, -1, d), dw_pool], bf, "mlp0")
    do, delta, dbcx, dw_out, d_conv = _conv_out_bwd(scatter_0[-1], dh1, w_out, o, bcx, conv_w)
    scatter_o = _scatter_start([dw_out.reshape(N_CHIPS, -1, d)], bf, "out")
    dqa, dka, dv = _attn_bwd(scatter_o[-1], qa, ka, qkv, do, lse, delta)
    dqkv, dfl, d_bf = _gate_bwd(dqa, dka, dv, fl, bf)
    dw_qkv, dw_f, dw_bcx = _wgrad_in(n0, [dqkv, dfl, dbcx])
    dw_in = jnp.concatenate([dw_qkv, dw_f[:N_HEADS], dw_bcx], axis=0).reshape(N_CHIPS, -1, d)
    slot_rows = -(-dw_in.shape[1] // 32) * 32
    dw_in = jnp.pad(dw_in, ((0, 0), (0, slot_rows - dw_in.shape[1]), (0, 0)))
    scatter_m = _scatter_start([dw_in], bf, "mixer")
    grad_x, d_nmix0 = _in_proj_bwd(scatter_m[-1], dqkv, dfl, dbcx, w_qkv, w_f, w_bcx, xs,
                                   row(norm_mix_0), dh1)

    r_up1, r_down1 = _scatter_finish(scatter_1, grad_x, "mlp1")
    r_up0, r_down0, r_pool = _scatter_finish(scatter_0, grad_x, "mlp0")
    (r_out,) = _scatter_finish(scatter_o, grad_x, "out")
    (r_in,) = _scatter_finish(scatter_m, grad_x, "mixer")
    reduced = [r_in, r_out, r_up0, r_down0, r_pool, r_up1, r_down1]
    moments = [(m_w_in_0, v_w_in_0), (m_w_out_0, v_w_out_0), (m_w_up_0, v_w_up_0),
               (m_w_down_0, v_w_down_0), (m_pool_w_1, v_pool_w_1), (m_w_up_1, v_w_up_1),
               (m_w_down_1, v_w_down_1)]
    big_out = []
    for k, (w, g, (m, v)) in enumerate(zip(big, reduced, moments)):
        if w.shape[-1] % 128:
            view = lambda t: t.reshape(-1, t.shape[-1]).T
            back = lambda t: t.T.reshape(w.shape)
            g_view = g[:w.shape[-1]]
        else:
            view = lambda t: t.reshape(-1, t.shape[-1])
            back = lambda t: t.reshape(w.shape)
            g_view = view(g)
        delta_w, new_m, new_v = _adamw(view(w), g_view, view(m), view(v), "adamw_%d" % k)
        big_out.append((back(g_view), back(delta_w), back(new_m), back(new_v)))

    tail = jnp.concatenate([d_conv[0:3].reshape(-1)[d:], d_bf[0, :N_HEADS], loss_part[0, :1]])
    small_part = jnp.concatenate(
        [d_nmix0, d_nffn0, d_nmix1, d_pscale, d_nffn1, d_final,
         d_conv[0:3].reshape(1, -1)[:, :d],
         jnp.pad(tail, (0, d - tail.shape[0])).reshape(1, d)], axis=0)
    parts = _gather_small(small_part)

    chip = 2 * lax.axis_index("x") + lax.axis_index("y")
    cw_cols = conv_w_0.shape[1]

    def conv_block(full):
        mine = lax.dynamic_slice_in_dim(full, chip * cw_cols, cw_cols, axis=1)
        return jnp.pad(mine.reshape(-1), (0, d - mine.size))

    def small_rows(vals, cw, bfv):
        return jnp.stack(list(vals) + [cw, jnp.pad(bfv, (0, d - N_HEADS))])

    smalls_w = [norm_mix_0, norm_ffn_0, norm_mix_1, pool_scale_1, norm_ffn_1, final_norm]
    smalls_m = [m_norm_mix_0, m_norm_ffn_0, m_norm_mix_1, m_pool_scale_1, m_norm_ffn_1, m_final_norm]
    smalls_v = [v_norm_mix_0, v_norm_ffn_0, v_norm_mix_1, v_pool_scale_1, v_norm_ffn_1, v_final_norm]
    pad_cw = lambda t: jnp.pad(t.reshape(-1), (0, d - t.size))
    w_rows = small_rows(smalls_w, pad_cw(conv_w_0), b_f_0)
    m_rows = small_rows(smalls_m, pad_cw(m_conv_w_0), m_b_f_0)
    v_rows = small_rows(smalls_v, pad_cw(v_conv_w_0), v_b_f_0)

    g_sum = _sum_devices(parts)
    conv_full = jnp.concatenate([g_sum[6], g_sum[7, :3 * c_conv - d]]).reshape(3, c_conv)
    bf_grad = g_sum[7, 3 * c_conv - d:3 * c_conv - d + N_HEADS]
    loss = g_sum[7, 3 * c_conv - d + N_HEADS]
    g_rows = jnp.concatenate(
        [g_sum[0:6], conv_block(conv_full).reshape(1, d),
         jnp.pad(bf_grad, (0, d - N_HEADS)).reshape(1, d)], axis=0)
    d_rows, nm_rows, nv_rows = _adamw(w_rows, g_rows, m_rows, v_rows, "adamw_small")

    def unpack(rows):
        cw = rows[6, :conv_w_0.size].reshape(conv_w_0.shape)
        return [rows[0], rows[1], rows[2], rows[3], rows[4], rows[5], cw, rows[7, :N_HEADS]]

    def assemble(kind):
        sm = unpack([g_rows, d_rows, nm_rows, nv_rows][kind])
        lg = [t[kind] for t in big_out]
        return [sm[0], lg[0], sm[7], sm[6], lg[1], sm[1], lg[2], lg[3],
                sm[2], lg[4], sm[3], sm[4], lg[5], lg[6], sm[5]]

    return (loss, grad_x[None], *assemble(0), *assemble(1), *assemble(2), *assemble(3))
```

```python
import functools

import jax
import jax.numpy as jnp
from jax import lax
from jax.experimental import pallas as pl
from jax.experimental.pallas import tpu as pltpu

F32 = jnp.float32
BF16 = jnp.bfloat16

RMS_EPS = 1e-6
HEAD_DIM = 64
N_HEADS = 8
ATTN_SCALE = HEAD_DIM ** -0.5
LOG2_E = 1.4426950408889634
POOL_WINDOWS = (2, 4, 8, 16)
POOL_HALO = 16
CONV_HALO = 8
NEG_BIG = -1e30

ADAM_LR = 0.001
ADAM_B1 = 0.9
ADAM_B2 = 0.999
ADAM_EPS = 1e-08
ADAM_WD = 0.01
ADAM_STEP = 10

N_CHIPS = 4
N_DEV = 8
MESH = pl.DeviceIdType.MESH

VMEM_LIMIT_BYTES = 56 * 1024 * 1024

TILE_ROWS = 512
TILE_ATTN = 512
TILE_MLP_ROWS = 1024
TILE_MLP_FF = 1024
TILE_MLP_BWD_FF = 512
TILE_HEAD_ROWS = 256
TILE_WGRAD_K = 1024
TILE_WGRAD_N = 1024
TILE_ELEM_ROWS = 256
SUM_CHUNK_ROWS = 128

LANE_CQ = 64
LANE_CK = 88


def _params(semantics):
    return pltpu.CompilerParams(dimension_semantics=semantics,
                                vmem_limit_bytes=VMEM_LIMIT_BYTES)


def _nn(a, b):
    return lax.dot_general(a, b, (((1,), (0,)), ((), ())), preferred_element_type=F32)


def _nt(a, b):
    return lax.dot_general(a, b, (((1,), (1,)), ((), ())), preferred_element_type=F32)


def _tn(a, b):
    return lax.dot_general(a, b, (((0,), (0,)), ((), ())), preferred_element_type=F32)


def _split3(v):
    hi = v.astype(BF16)
    r1 = v - hi.astype(F32)
    mid = r1.astype(BF16)
    lo = (r1 - mid.astype(F32)).astype(BF16)
    return hi, mid, lo


def _exact_nn(sel, v):
    hi, mid, lo = _split3(v)
    return _nn(sel, hi) + _nn(sel, mid) + _nn(sel, lo)


def _exact_nt(sel, v):
    hi, mid, lo = _split3(v)
    return _nt(sel, hi) + _nt(sel, mid) + _nt(sel, lo)


def _rms_fwd(x, g):
    r = lax.rsqrt(jnp.mean(x * x, axis=-1, keepdims=True) + RMS_EPS)
    return x * r * g, r


def _rms_bwd(dn, x, g):
    r = lax.rsqrt(jnp.mean(x * x, axis=-1, keepdims=True) + RMS_EPS)
    xh = x * r
    gy = dn * g
    dx = r * (gy - xh * jnp.mean(gy * xh, axis=-1, keepdims=True))
    return dx, jnp.sum(dn * xh, axis=0, keepdims=True)


def _lane(shape):
    return lax.broadcasted_iota(jnp.int32, shape, len(shape) - 1)


def _row(shape):
    return lax.broadcasted_iota(jnp.int32, shape, len(shape) - 2)


def _full(a):
    nd = a.ndim
    return pl.BlockSpec(a.shape, lambda *_: (0,) * nd)


def _rms_pre(after, x, g):
    s, d = x.shape
    tm = min(TILE_ROWS, s)

    def body(after_ref, x_ref, g_ref, n_ref):
        n, _ = _rms_fwd(x_ref[...], g_ref[...])
        n_ref[...] = n.astype(BF16)

    rows = pl.BlockSpec((tm, d), lambda i: (i, 0))
    return pl.pallas_call(
        body, name="rms_pre", grid=(s // tm,),
        in_specs=[ANY, rows, _full(g)], out_specs=rows,
        out_shape=jax.ShapeDtypeStruct((s, d), BF16),
        compiler_params=_params(("parallel",)),
    )(after, x, g)


def _in_proj(n, w_qkv, w_f, w_bcx):
    s, d = n.shape
    tm = min(TILE_ROWS, s)

    def body(n_ref, wq_ref, wf_ref, wb_ref, qkv_ref, fl_ref, bcx_ref):
        nb = n_ref[...]
        qkv_ref[...] = _nn(nb, wq_ref[...]).astype(BF16)
        fl_ref[...] = _nn(nb, wf_ref[...])
        bcx_ref[...] = _nn(nb, wb_ref[...])

    rows = lambda c: pl.BlockSpec((tm, c), lambda i: (i, 0))
    return pl.pallas_call(
        body, name="in_proj", grid=(s // tm,),
        in_specs=[rows(d), _full(w_qkv), _full(w_f), _full(w_bcx)],
        out_specs=[rows(w_qkv.shape[1]), rows(w_f.shape[1]), rows(w_bcx.shape[1])],
        out_shape=[jax.ShapeDtypeStruct((s, w_qkv.shape[1]), BF16),
                   jax.ShapeDtypeStruct((s, w_f.shape[1]), F32),
                   jax.ShapeDtypeStruct((s, w_bcx.shape[1]), F32)],
        compiler_params=_params(("parallel",)),
    )(n, w_qkv, w_f, w_bcx)


def _gate_prep(fl, bf, qkv):
    s = fl.shape[0]
    a = N_HEADS * HEAD_DIM
    tm = min(TILE_ROWS, s)

    def body(fl_ref, bf_ref, q_ref, k_ref, qa_ref, ka_ref, carry_ref):
        i = pl.program_id(0)

        @pl.when(i == 0)
        def _():
            carry_ref[...] = jnp.zeros_like(carry_ref)

        z = fl_ref[...] + bf_ref[...]
        logf = jnp.minimum(z, 0.0) - jnp.log(1.0 + jnp.exp(-jnp.abs(z)))
        lower = (_lane((tm, tm)) <= _row((tm, tm))).astype(BF16)
        cum = _exact_nn(lower, logf) + carry_ref[0:1, :]
        carry_ref[0:1, :] = cum[tm - 1:tm, :]

        lane = _lane((tm, 128))
        pieces = [p.astype(F32)
                  for p in _split3(jnp.where(lane < N_HEADS, LOG2_E * cum, 0.0))]
        shared_q = sum(pltpu.roll(p, LANE_CQ + N_HEADS * k, axis=1) for k, p in enumerate(pieces))
        shared_k = -sum(pltpu.roll(p, LANE_CK + N_HEADS * k, axis=1) for k, p in enumerate(pieces))
        for h in range(N_HEADS):
            at_q = functools.reduce(jnp.logical_or,
                                    [lane == LANE_CQ + N_HEADS * k + h for k in range(3)])
            at_k = functools.reduce(jnp.logical_or,
                                    [lane == LANE_CK + N_HEADS * k + h for k in range(3)])
            pair = slice((h // 2) * 128, (h // 2 + 1) * 128)
            qp = q_ref[:, pair].astype(F32)
            kp = k_ref[:, pair].astype(F32)
            if h % 2:
                qp = pltpu.roll(qp, HEAD_DIM, axis=1)
                kp = pltpu.roll(kp, HEAD_DIM, axis=1)
            q_bias = jnp.where(at_k, 1.0, shared_q)
            k_bias = jnp.where(at_q, 1.0, shared_k)
            qa_ref[h] = jnp.where(lane < HEAD_DIM, qp * (ATTN_SCALE * LOG2_E), q_bias).astype(BF16)
            ka_ref[h] = jnp.where(lane < HEAD_DIM, kp, k_bias).astype(BF16)

    aug = jax.ShapeDtypeStruct((N_HEADS, s, 128), BF16)
    aug_spec = pl.BlockSpec((N_HEADS, tm, 128), lambda i: (0, i, 0))
    return pl.pallas_call(
        body, name="gate_prep", grid=(s // tm,),
        in_specs=[pl.BlockSpec((tm, 128), lambda i: (i, 0)), _full(bf),
                  pl.BlockSpec((tm, a), lambda i: (i, 0)),
                  pl.BlockSpec((tm, a), lambda i: (i, 1))],
        out_specs=[aug_spec, aug_spec],
        out_shape=[aug, aug],
        scratch_shapes=[pltpu.VMEM((8, 128), F32)],
        compiler_params=_params(("arbitrary",)),
    )(fl, bf, qkv, qkv)


def _attn_fwd(qa, ka, qkv):
    s = qa.shape[1]
    a = N_HEADS * HEAD_DIM
    t = min(TILE_ATTN, s)
    n_pairs = N_HEADS // 2
    v_block0 = 2 * a // 128

    ones_lane = (HEAD_DIM, 0)

    def body(qa_ref, ka_ref, v_ref, o_ref, lse_ref, m_ref, acc_ref, s_even, s_odd):
        i = pl.program_id(1)
        m_ref[...] = jnp.full_like(m_ref, NEG_BIG)
        acc_ref[...] = jnp.zeros_like(acc_ref)
        upper_rows = _row((128, t)) < HEAD_DIM

        def keys(j):
            return pl.ds(pl.multiple_of(j * t, t), t)

        def scores_into(buf, j):
            for e in range(2):
                buf[e] = _nt(ka_ref[e, keys(j), :], qa_ref[e])

        def consume(buf, j, masked):
            vf = v_ref[keys(j), :].astype(F32)
            lane = _lane((t, 128))
            own = [lane < HEAD_DIM, lane >= HEAD_DIM]
            for e in range(2):
                v_head = jnp.where(own[e], vf, jnp.where(lane == ones_lane[e], 1.0, 0.0)).astype(BF16)
                sc = buf[e]
                if masked:
                    sc = jnp.where(_row((t, t)) <= _lane((t, t)), sc, NEG_BIG)
                m_prev = m_ref[e]
                m_new = jnp.maximum(m_prev, jnp.max(sc, axis=0, keepdims=True))
                p = jnp.exp2(sc - m_new).astype(BF16)
                acc_ref[e] = acc_ref[e] * jnp.exp2(m_prev - m_new) + _tn(v_head, p)
                m_ref[e] = m_new

        scores_into(s_even, 0)

        def two_tiles(p, carry):
            j = 2 * p
            scores_into(s_odd, j + 1)
            consume(s_even, j, False)
            scores_into(s_even, j + 2)
            consume(s_odd, j + 1, False)
            return carry

        lax.fori_loop(0, i // 2, two_tiles, 0)

        @pl.when(i % 2 == 0)
        def _():
            consume(s_even, i, True)

        @pl.when(i % 2 == 1)
        def _():
            scores_into(s_odd, i)
            consume(s_even, i - 1, False)
            consume(s_odd, i, True)

        denom = [acc_ref[e, ones_lane[e]:ones_lane[e] + 1, :] for e in range(2)]
        out_t = jnp.where(upper_rows, acc_ref[0] / denom[0], acc_ref[1] / denom[1])
        o_ref[...] = out_t.T.astype(BF16)
        lse = [m_ref[e] + LOG2_E * jnp.log(denom[e]) for e in range(2)]
        lse_ref[...] = jnp.where(_row((8, t)) == 0, lse[0], lse[1])

    return pl.pallas_call(
        body, name="attn_fwd", grid=(n_pairs, s // t),
        in_specs=[pl.BlockSpec((2, t, 128), lambda g, i: (g, i, 0)),
                  pl.BlockSpec((2, s, 128), lambda g, i: (g, 0, 0)),
                  pl.BlockSpec((s, 128), lambda g, i: (0, v_block0 + g))],
        out_specs=[pl.BlockSpec((t, 128), lambda g, i: (i, g)),
                   pl.BlockSpec((None, 8, t), lambda g, i: (g, 0, i))],
        out_shape=[jax.ShapeDtypeStruct((s, a), BF16),
                   jax.ShapeDtypeStruct((n_pairs, 8, s), F32)],
        scratch_shapes=[pltpu.VMEM((2, 1, t), F32), pltpu.VMEM((2, 128, t), F32),
                        pltpu.VMEM((2, t, t), F32), pltpu.VMEM((2, t, t), F32)],
        compiler_params=_params(("parallel", "arbitrary")),
    )(qa, ka, qkv)


def _conv_out(o, bcx, cw, w_out, x):
    s, d = x.shape
    c = o.shape[1]
    tm = min(TILE_ROWS, s)

    def body(o_ref, b_ref, c_ref, xin_ref, cw_ref, w_ref, x_ref, h_ref, ubuf):
        i = pl.program_id(0)

        @pl.when(i == 0)
        def _():
            ubuf[0:CONV_HALO, :] = jnp.zeros((CONV_HALO, c), F32)

        u = c_ref[...] * xin_ref[...]
        ubuf[CONV_HALO:CONV_HALO + tm, :] = u
        u1 = ubuf[CONV_HALO - 1:CONV_HALO - 1 + tm, :]
        u2 = ubuf[CONV_HALO - 2:CONV_HALO - 2 + tm, :]
        cv = (cw_ref[0:1, :] * u2 + cw_ref[1:2, :] * u1) + cw_ref[2:3, :] * u
        y = (b_ref[...] * cv).astype(BF16)
        mix = _nn(o_ref[...], w_ref[0:c, :]) + _nn(y, w_ref[c:2 * c, :])
        h_ref[...] = x_ref[...] + mix
        ubuf[0:CONV_HALO, :] = u[tm - CONV_HALO:tm, :]

    col = lambda k: pl.BlockSpec((tm, c), lambda i: (i, k))
    return pl.pallas_call(
        body, name="conv_out", grid=(s // tm,),
        in_specs=[col(0), col(0), col(1), col(2), _full(cw), _full(w_out),
                  pl.BlockSpec((tm, d), lambda i: (i, 0))],
        out_specs=pl.BlockSpec((tm, d), lambda i: (i, 0)),
        out_shape=jax.ShapeDtypeStruct((s, d), F32),
        scratch_shapes=[pltpu.VMEM((tm + CONV_HALO, c), F32)],
        compiler_params=_params(("arbitrary",)),
    )(o, bcx, bcx, bcx, cw, w_out, x)


def _mlp_fwd(h, g, w_up, w_down, name, head=None):
    s, d = h.shape
    ff = w_down.shape[0]
    slot_cols = w_up.shape[2]
    tm = min(TILE_MLP_ROWS, s)
    tf = min(TILE_MLP_FF, slot_cols)
    per_slot = slot_cols // tf
    nf = ff // tf
    n_head = 0 if head is None else 2
    chunk = min(TILE_HEAD_ROWS, tm)

    def body(*refs):
        h_ref, g_ref, wu_ref, wd_ref = refs[:4]
        out_ref, a_ref, n_ref = refs[4 + n_head:7 + n_head]
        nb_ref, acc_ref = refs[9 + n_head:11 + n_head] if head else refs[-2:]
        i = pl.program_id(0)
        f = pl.program_id(1)

        def target_copy():
            t_hbm, t_buf, t_sem = refs[5], refs[-2], refs[-1]
            return pltpu.make_async_copy(t_hbm.at[pl.ds(pl.multiple_of(i * tm, tm), tm), :],
                                         t_buf, t_sem)

        @pl.when(f == 0)
        def _():
            n, _ = _rms_fwd(h_ref[...], g_ref[...])
            nb = n.astype(BF16)
            nb_ref[...] = nb
            n_ref[...] = nb
            acc_ref[...] = jnp.zeros_like(acc_ref)
            if head is not None:
                target_copy().start()

        pre = _nn(nb_ref[...], wu_ref[...])
        a_ref[...] = pre.astype(BF16)
        r = jnp.square(jnp.maximum(pre, 0.0)).astype(BF16)
        acc_ref[...] += _nn(r, wd_ref[...])

        @pl.when(f == nf - 1)
        def _():
            if head is None:
                out_ref[...] = h_ref[...] + acc_ref[...]
            else:
                gf_ref, t_buf = refs[4], refs[-2]
                loss_ref, dg_ref = refs[7 + n_head:9 + n_head]
                target_copy().wait()
                part, dg = None, None
                for r0 in range(0, tm, chunk):
                    rows_ = slice(r0, r0 + chunk)
                    out = h_ref[rows_, :] + acc_ref[rows_, :]
                    y, _ = _rms_fwd(out, gf_ref[...])
                    err = y - t_buf[rows_, :]
                    p = 0.5 * jnp.sum(jnp.mean(err * err, axis=-1, keepdims=True), axis=0,
                                      keepdims=True)
                    dx, dgp = _rms_bwd(err / d, out, gf_ref[...])
                    out_ref[rows_, :] = dx
                    part = p if part is None else part + p
                    dg = dgp if dg is None else dg + dgp
                part = jnp.broadcast_to(part, loss_ref.shape)

                @pl.when(i == 0)
                def _():
                    loss_ref[...] = part
                    dg_ref[...] = dg

                @pl.when(i > 0)
                def _():
                    loss_ref[...] += part
                    dg_ref[...] += dg

    rows = pl.BlockSpec((tm, d), lambda i, f: (i, 0))
    in_specs = [rows, _full(g),
                pl.BlockSpec((None, d, tf), lambda i, f: (f // per_slot, 0, f % per_slot)),
                pl.BlockSpec((tf, d), lambda i, f: (f, 0))]
    out_specs = [rows, pl.BlockSpec((tm, tf), lambda i, f: (i, f)), rows]
    out_shape = [jax.ShapeDtypeStruct((s, d), F32), jax.ShapeDtypeStruct((s, ff), BF16),
                 jax.ShapeDtypeStruct((s, d), BF16)]
    args = [h, g, w_up, w_down]
    scratch = [pltpu.VMEM((tm, d), BF16), pltpu.VMEM((tm, d), F32)]
    if head is not None:
        in_specs += [_full(head[0]), ANY]
        args += list(head)
        out_specs += [pl.BlockSpec((1, 128), lambda i, f: (0, 0)),
                      pl.BlockSpec((1, d), lambda i, f: (0, 0))]
        out_shape += [jax.ShapeDtypeStruct((1, 128), F32), jax.ShapeDtypeStruct((1, d), F32)]
        scratch += [pltpu.VMEM((tm, d), F32), pltpu.SemaphoreType.DMA]
    return pl.pallas_call(
        body, name=name, grid=(s // tm, nf),
        in_specs=in_specs, out_specs=out_specs, out_shape=out_shape, scratch_shapes=scratch,
        compiler_params=_params(("parallel" if head is None else "arbitrary", "arbitrary")),
    )(*args)


def _window_sum_down(e, window):
    step = 1
    while step < window:
        e = e + pltpu.roll(e, step, axis=0)
        step *= 2
    return e


def _window_sum_up(e, window):
    n = e.shape[0]
    step = 1
    while step < window:
        e = e + pltpu.roll(e, n - step, axis=0)
        step *= 2
    return e


def _pool_counts(first_row, tm, window):
    t = first_row + _row((tm, 1))
    return jnp.minimum(t + 1, window).astype(F32)


def _pool_fwd(h, g, pw, ps):
    s, d = h.shape
    cg = d // len(POOL_WINDOWS)
    tm = min(TILE_ROWS, s)

    def body(h_ref, g_ref, pw_ref, ps_ref, out_ref, nbuf):
        i = pl.program_id(0)

        @pl.when(i == 0)
        def _():
            nbuf[0:POOL_HALO, :] = jnp.zeros((POOL_HALO, d), F32)

        n, _ = _rms_fwd(h_ref[...], g_ref[...])
        nbuf[POOL_HALO:POOL_HALO + tm, :] = n
        for k, window in enumerate(POOL_WINDOWS):
            cols = slice(k * cg, (k + 1) * cg)
            sums = _window_sum_down(nbuf[:, cols], window)[POOL_HALO:, :]
            pooled = sums / _pool_counts(i * tm, tm, window) - n[:, cols]
            y = _nn(pooled.astype(BF16), pw_ref[k]) * ps_ref[:, cols]
            out_ref[:, cols] = h_ref[:, cols] + y
        nbuf[0:POOL_HALO, :] = n[tm - POOL_HALO:tm, :]

    return pl.pallas_call(
        body, name="pool_fwd", grid=(s // tm,),
        in_specs=[pl.BlockSpec((tm, d), lambda i: (i, 0)), _full(g), _full(pw), _full(ps)],
        out_specs=pl.BlockSpec((tm, d), lambda i: (i, 0)),
        out_shape=jax.ShapeDtypeStruct((s, d), F32),
        scratch_shapes=[pltpu.VMEM((tm + POOL_HALO, d), F32)],
        compiler_params=_params(("arbitrary",)),
    )(h, g, pw, ps)


def _mlp_bwd_x(dz, a, w_up, w_down, h_in, g, name):
    s, d = dz.shape
    ff = w_down.shape[0]
    slot_cols = w_up.shape[2]
    tm = min(TILE_MLP_ROWS, s)
    tf = min(TILE_MLP_BWD_FF, slot_cols)
    per_slot = slot_cols // tf
    nf = ff // tf

    def body(dz_ref, a_ref, wu_ref, wd_ref, h_ref, g_ref, da_ref, dzb_ref, dh_ref, dg_ref,
             dzs_ref, acc_ref):
        i = pl.program_id(0)
        f = pl.program_id(1)

        @pl.when(f == 0)
        def _():
            dzb = dz_ref[...].astype(BF16)
            dzs_ref[...] = dzb
            dzb_ref[...] = dzb
            acc_ref[...] = jnp.zeros_like(acc_ref)

        dr = _nt(dzs_ref[...], wd_ref[...])
        da = (dr * (2.0 * jnp.maximum(a_ref[...].astype(F32), 0.0))).astype(BF16)
        da_ref[...] = da
        acc_ref[...] += _nt(da, wu_ref[...])

        @pl.when(f == nf - 1)
        def _():
            dx, dg = _rms_bwd(acc_ref[...], h_ref[...], g_ref[...])
            dh_ref[...] = dz_ref[...] + dx

            @pl.when(i == 0)
            def _():
                dg_ref[...] = dg

            @pl.when(i > 0)
            def _():
                dg_ref[...] += dg

    return pl.pallas_call(
        body, name=name, grid=(s // tm, nf),
        in_specs=[pl.BlockSpec((tm, d), lambda i, f: (i, 0)),
                  pl.BlockSpec((tm, tf), lambda i, f: (i, f)),
                  pl.BlockSpec((None, d, tf), lambda i, f: (f // per_slot, 0, f % per_slot)),
                  pl.BlockSpec((tf, d), lambda i, f: (f, 0)),
                  pl.BlockSpec((tm, d), lambda i, f: (i, 0)), _full(g)],
        out_specs=[pl.BlockSpec((tm, tf), lambda i, f: (i, f)),
                   pl.BlockSpec((tm, d), lambda i, f: (i, 0)),
                   pl.BlockSpec((tm, d), lambda i, f: (i, 0)),
                   pl.BlockSpec((1, d), lambda i, f: (0, 0))],
        out_shape=[jax.ShapeDtypeStruct((s, ff), BF16),
                   jax.ShapeDtypeStruct((s, d), BF16),
                   jax.ShapeDtypeStruct((s, d), F32),
                   jax.ShapeDtypeStruct((1, d), F32)],
        scratch_shapes=[pltpu.VMEM((tm, d), BF16), pltpu.VMEM((tm, d), F32)],
        compiler_params=_params(("arbitrary", "arbitrary")),
    )(dz, a, w_up, w_down, h_in, g)


def _mlp_bwd_w(n, da, a, dzb, slot_cols, name):
    s, d = n.shape
    ff = a.shape[1]
    tn = min(TILE_WGRAD_N, slot_cols)
    tk = min(TILE_WGRAD_K, s)
    per_slot = slot_cols // tn
    nk = s // tk

    def body(n_ref, da_ref, a_ref, dz_ref, du_ref, dd_ref, accu_ref, accd_ref):
        k = pl.program_id(1)

        @pl.when(k == 0)
        def _():
            accu_ref[...] = jnp.zeros_like(accu_ref)
            accd_ref[...] = jnp.zeros_like(accd_ref)

        accu_ref[...] += _tn(n_ref[...], da_ref[...])
        r = jnp.square(jnp.maximum(a_ref[...].astype(F32), 0.0)).astype(BF16)
        accd_ref[...] += _tn(r, dz_ref[...])

        @pl.when(k == nk - 1)
        def _():
            du_ref[...] = accu_ref[...].astype(BF16)
            dd_ref[...] = accd_ref[...].astype(BF16)

    return pl.pallas_call(
        body, name=name, grid=(ff // tn, nk),
        in_specs=[pl.BlockSpec((tk, d), lambda f, k: (k, 0)),
                  pl.BlockSpec((tk, tn), lambda f, k: (k, f)),
                  pl.BlockSpec((tk, tn), lambda f, k: (k, f)),
                  pl.BlockSpec((tk, d), lambda f, k: (k, 0))],
        out_specs=[pl.BlockSpec((None, d, tn), lambda f, k: (f // per_slot, 0, f % per_slot)),
                   pl.BlockSpec((tn, d), lambda f, k: (f, 0))],
        out_shape=[jax.ShapeDtypeStruct((ff // slot_cols, d, slot_cols), BF16),
                   jax.ShapeDtypeStruct((ff, d), BF16)],
        scratch_shapes=[pltpu.VMEM((d, tn), F32), pltpu.VMEM((tn, d), F32)],
        compiler_params=_params(("parallel", "arbitrary")),
    )(n, da, a, dzb)


def _pool_bwd(after, dh, h, g, pw, ps):
    s, d = h.shape
    cg = d // len(POOL_WINDOWS)
    tm = min(TILE_ROWS, s)
    nb = s // tm
    halo_per_tile = tm // POOL_HALO

    def body(after_ref, dh_ref, h_ref, halo_ref, g_ref, pw_ref, ps_ref,
             dx_ref, dpw_ref, dps_ref, dg_ref, nbuf, qbuf, dn_ref, carry, dpw_acc):
        i = pl.program_id(0)
        blk = nb - 1 - i

        @pl.when(i == 0)
        def _():
            carry[...] = jnp.zeros_like(carry)
            dpw_acc[...] = jnp.zeros_like(dpw_acc)
            dps_ref[...] = jnp.zeros_like(dps_ref)
            dg_ref[...] = jnp.zeros_like(dg_ref)

        hv = h_ref[...]
        n, _ = _rms_fwd(hv, g_ref[...])
        nh, _ = _rms_fwd(halo_ref[...], g_ref[...])
        nbuf[0:POOL_HALO, :] = jnp.where(blk == 0, 0.0, nh)
        nbuf[POOL_HALO:POOL_HALO + tm, :] = n
        dhv = dh_ref[...]
        for k, window in enumerate(POOL_WINDOWS):
            cols = slice(k * cg, (k + 1) * cg)
            cnt = _pool_counts(blk * tm, tm, window)
            sums = _window_sum_down(nbuf[:, cols], window)[POOL_HALO:, :]
            pb = (sums / cnt - n[:, cols]).astype(BF16)
            dyk = dhv[:, cols]
            dps_ref[:, cols] += jnp.sum(dyk * _nn(pb, pw_ref[k]), axis=0, keepdims=True)
            dyb = (dyk * ps_ref[:, cols]).astype(BF16)
            dpw_acc[k] += _tn(pb, dyb)
            dpool = _nt(dyb, pw_ref[k])
            qv = dpool / cnt
            qbuf[0:tm, cols] = qv
            qbuf[tm:tm + POOL_HALO, cols] = carry[:, cols]
            dn_ref[:, cols] = _window_sum_up(qbuf[:, cols], window)[0:tm, :] - dpool
            carry[:, cols] = qv[0:POOL_HALO, :]
        dx, dg = _rms_bwd(dn_ref[...], hv, g_ref[...])
        dx_ref[...] = dhv + dx
        dg_ref[...] += dg

        @pl.when(i == nb - 1)
        def _():
            dpw_ref[...] = dpw_acc[...].astype(BF16)

    rev = lambda i: (nb - 1 - i, 0)
    return pl.pallas_call(
        body, name="pool_bwd", grid=(nb,),
        in_specs=[ANY, pl.BlockSpec((tm, d), rev), pl.BlockSpec((tm, d), rev),
                  pl.BlockSpec((POOL_HALO, d),
                               lambda i: (jnp.maximum((nb - 1 - i) * halo_per_tile - 1, 0), 0)),
                  _full(g), _full(pw), _full(ps)],
        out_specs=[pl.BlockSpec((tm, d), rev), _full(pw),
                   pl.BlockSpec((1, d), lambda i: (0, 0)),
                   pl.BlockSpec((1, d), lambda i: (0, 0))],
        out_shape=[jax.ShapeDtypeStruct((s, d), F32),
                   jax.ShapeDtypeStruct(pw.shape, BF16),
                   jax.ShapeDtypeStruct((1, d), F32),
                   jax.ShapeDtypeStruct((1, d), F32)],
        scratch_shapes=[pltpu.VMEM((tm + POOL_HALO, d), F32), pltpu.VMEM((tm + POOL_HALO, d), F32),
                        pltpu.VMEM((tm, d), F32), pltpu.VMEM((POOL_HALO, d), F32),
                        pltpu.VMEM(pw.shape, F32)],
        compiler_params=_params(("arbitrary",)),
    )(after, dh, h, h, g, pw, ps)


def _conv_out_bwd(after, dh, w_out, o, bcx, cw):
    s, d = dh.shape
    c = o.shape[1]
    tm = min(TILE_ROWS, s)
    nb = s // tm
    halo_per_tile = tm // CONV_HALO

    def body(after_ref, dh_ref, w_ref, o_ref, b_ref, c_ref, xin_ref, ch_ref, xh_ref, cw_ref,
             do_ref, delta_ref, dbcx_ref, dw_ref, dcw_ref, ubuf, dbuf, carry, acc):
        i = pl.program_id(0)
        blk = nb - 1 - i

        @pl.when(i == 0)
        def _():
            carry[...] = jnp.zeros_like(carry)
            acc[...] = jnp.zeros_like(acc)
            dcw_ref[...] = jnp.zeros_like(dcw_ref)

        dm = dh_ref[...].astype(BF16)
        dcat = _nt(dm, w_ref[...])
        do = dcat[:, 0:c]
        dy = dcat[:, c:2 * c]
        do_ref[...] = do.astype(BF16)
        head_of_lane = lax.shift_right_logical(_lane((8, c)), HEAD_DIM.bit_length() - 1)
        heads = (head_of_lane == _row((8, c))).astype(BF16)
        delta_ref[...] = _exact_nt(heads, do * o_ref[...].astype(F32))

        cv_ = c_ref[...]
        xin = xin_ref[...]
        bv = b_ref[...]
        u = cv_ * xin
        ubuf[0:CONV_HALO, :] = jnp.where(blk == 0, 0.0, ch_ref[...] * xh_ref[...])
        ubuf[CONV_HALO:CONV_HALO + tm, :] = u
        u1 = ubuf[CONV_HALO - 1:CONV_HALO - 1 + tm, :]
        u2 = ubuf[CONV_HALO - 2:CONV_HALO - 2 + tm, :]
        w0, w1, w2 = cw_ref[0:1, :], cw_ref[1:2, :], cw_ref[2:3, :]
        cv = (w0 * u2 + w1 * u1) + w2 * u
        acc[0:c, :] += _tn(o_ref[...], dm)
        acc[c:2 * c, :] += _tn((bv * cv).astype(BF16), dm)

        dcv = dy * bv
        dcw_ref[0:1, :] += jnp.sum(dcv * u2, axis=0, keepdims=True)
        dcw_ref[1:2, :] += jnp.sum(dcv * u1, axis=0, keepdims=True)
        dcw_ref[2:3, :] += jnp.sum(dcv * u, axis=0, keepdims=True)
        dbuf[0:tm, :] = dcv
        dbuf[tm:tm + CONV_HALO, :] = carry[...]
        du = w2 * dcv + w1 * dbuf[1:1 + tm, :] + w0 * dbuf[2:2 + tm, :]
        dbcx_ref[:, 0:c] = (dy * cv).astype(BF16)
        dbcx_ref[:, c:2 * c] = (du * xin).astype(BF16)
        dbcx_ref[:, 2 * c:3 * c] = (du * cv_).astype(BF16)
        carry[...] = dcv[0:CONV_HALO, :]

        @pl.when(i == nb - 1)
        def _():
            dw_ref[...] = acc[...].astype(BF16)

    rev = lambda k: (lambda i: (nb - 1 - i, k))
    halo = lambda k: (lambda i: (jnp.maximum((nb - 1 - i) * halo_per_tile - 1, 0), k))
    return pl.pallas_call(
        body, name="conv_out_bwd", grid=(nb,),
        in_specs=[ANY, pl.BlockSpec((tm, d), rev(0)), _full(w_out), pl.BlockSpec((tm, c), rev(0)),
                  pl.BlockSpec((tm, c), rev(0)), pl.BlockSpec((tm, c), rev(1)),
                  pl.BlockSpec((tm, c), rev(2)),
                  pl.BlockSpec((CONV_HALO, c), halo(1)), pl.BlockSpec((CONV_HALO, c), halo(2)),
                  _full(cw)],
        out_specs=[pl.BlockSpec((tm, c), rev(0)),
                   pl.BlockSpec((8, tm), lambda i: (0, nb - 1 - i)),
                   pl.BlockSpec((tm, 3 * c), rev(0)),
                   _full(w_out), _full(cw)],
        out_shape=[jax.ShapeDtypeStruct((s, c), BF16),
                   jax.ShapeDtypeStruct((8, s), F32),
                   jax.ShapeDtypeStruct((s, 3 * c), BF16),
                   jax.ShapeDtypeStruct(w_out.shape, BF16),
                   jax.ShapeDtypeStruct(cw.shape, F32)],
        scratch_shapes=[pltpu.VMEM((tm + CONV_HALO, c), F32), pltpu.VMEM((tm + CONV_HALO, c), F32),
                        pltpu.VMEM((CONV_HALO, c), F32), pltpu.VMEM(w_out.shape, F32)],
        compiler_params=_params(("arbitrary",)),
    )(after, dh, w_out, o, bcx, bcx, bcx, bcx, bcx, cw)


def _attn_bwd(after, qa, ka, qkv, do, lse, delta):
    s = qa.shape[1]
    a = N_HEADS * HEAD_DIM
    t = min(TILE_ATTN, s)
    nq = s // t
    n_pairs = N_HEADS // 2
    v_block0 = 2 * a // 128

    def body(after_ref, ka_ref, v_ref, qa_ref, do_ref, lse_ref, delta_ref,
             dqt_ref, dka_ref, dv_ref, dk_acc, dv_acc):
        g = pl.program_id(0)
        j = pl.program_id(1)

        @pl.when(j == 0)
        def _():
            dqt_ref[...] = jnp.zeros_like(dqt_ref)

        dk_acc[...] = jnp.zeros_like(dk_acc)
        dv_acc[...] = jnp.zeros_like(dv_acc)
        lane = _lane((t, 128))
        vf = v_ref[...].astype(F32)
        v_heads = [jnp.where(lane < HEAD_DIM, vf, 0.0).astype(BF16),
                   jnp.where(lane >= HEAD_DIM, vf, 0.0).astype(BF16)]
        ke_t = [ka_ref[e].astype(F32).T.astype(BF16) for e in range(2)]

        def q_step(i, masked):
            qs = pl.ds(pl.multiple_of(i * t, t), t)
            dob = do_ref[qs, :]
            for e in range(2):
                qe = qa_ref[e, qs, :]
                sc = _nt(ka_ref[e], qe)
                if masked:
                    sc = jnp.where(_row((t, t)) <= _lane((t, t)), sc, NEG_BIG)
                p = jnp.exp2(sc - lse_ref[pl.ds(e, 1), qs])
                dv_acc[e] += _nn(p.astype(BF16), dob)
                dp = _nt(v_heads[e], dob)
                ds = (p * (dp - delta_ref[pl.ds(2 * g + e, 1), qs])).astype(BF16)
                dk_acc[e] += _nn(ds, qe)
                dqt_ref[e, :, qs] += _nn(ke_t[e], ds)

        q_step(j, True)

        def full_step(i, carry):
            q_step(i, False)
            return carry

        lax.fori_loop(j + 1, nq, full_step, 0)
        dka_ref[...] = dk_acc[...]
        dv_ref[...] = jnp.where(lane < HEAD_DIM, dv_acc[0], dv_acc[1]).astype(BF16)

    return pl.pallas_call(
        body, name="attn_bwd", grid=(n_pairs, nq),
        in_specs=[ANY, pl.BlockSpec((2, t, 128), lambda g, j: (g, j, 0)),
                  pl.BlockSpec((t, 128), lambda g, j: (j, v_block0 + g)),
                  pl.BlockSpec((2, s, 128), lambda g, j: (g, 0, 0)),
                  pl.BlockSpec((s, 128), lambda g, j: (0, g)),
                  pl.BlockSpec((None, 8, s), lambda g, j: (g, 0, 0)),
                  pl.BlockSpec((8, s), lambda g, j: (0, 0))],
        out_specs=[pl.BlockSpec((2, 128, s), lambda g, j: (g, 0, 0)),
                   pl.BlockSpec((2, t, 128), lambda g, j: (g, j, 0)),
                   pl.BlockSpec((t, 128), lambda g, j: (j, g))],
        out_shape=[jax.ShapeDtypeStruct((N_HEADS, 128, s), F32),
                   jax.ShapeDtypeStruct((N_HEADS, s, 128), F32),
                   jax.ShapeDtypeStruct((s, a), BF16)],
        scratch_shapes=[pltpu.VMEM((2, t, 128), F32), pltpu.VMEM((2, t, 128), F32)],
        compiler_params=_params(("parallel", "arbitrary")),
    )(after, ka, qkv, qa, do, lse, delta)


def _gate_bwd(dqa, dka, dv, fl, bf):
    s = fl.shape[0]
    a = N_HEADS * HEAD_DIM
    tm = min(TILE_ROWS, s)
    nb = s // tm

    def body(dqa_ref, dka_ref, dv_ref, fl_ref, bf_ref, dqkv_ref, dfl_ref, dbf_ref, carry):
        i = pl.program_id(0)

        @pl.when(i == 0)
        def _():
            carry[...] = jnp.zeros_like(carry)
            dbf_ref[...] = jnp.zeros_like(dbf_ref)

        lane = _lane((tm, 128))
        dq_sum = jnp.zeros((tm, 128), F32)
        dk_sum = jnp.zeros((tm, 128), F32)
        for pair in range(N_HEADS // 2):
            qs, ks = [], []
            for e in range(2):
                h = 2 * pair + e
                dq = dqa_ref[h].T
                dk = dka_ref[h]
                dq_sum = dq_sum + dq
                dk_sum = dk_sum + dk
                qs.append(dq * ATTN_SCALE)
                ks.append(dk * (1.0 / LOG2_E))
            cols = slice(pair * 128, (pair + 1) * 128)
            dqkv_ref[:, cols] = jnp.where(
                lane < HEAD_DIM, qs[0], pltpu.roll(qs[1], HEAD_DIM, axis=1)).astype(BF16)
            dqkv_ref[:, a + pair * 128:a + (pair + 1) * 128] = jnp.where(
                lane < HEAD_DIM, ks[0], pltpu.roll(ks[1], HEAD_DIM, axis=1)).astype(BF16)
        dqkv_ref[:, 2 * a:3 * a] = dv_ref[...]

        in_q = (lane >= LANE_CQ) & (lane < LANE_CQ + N_HEADS)
        in_k = (lane >= LANE_CK) & (lane < LANE_CK + N_HEADS)
        dcum = (pltpu.roll(jnp.where(in_q, dq_sum, 0.0), 128 - LANE_CQ, axis=1)
                - pltpu.roll(jnp.where(in_k, dk_sum, 0.0), 128 - LANE_CK, axis=1))

        upper = (_lane((tm, tm)) >= _row((tm, tm))).astype(BF16)
        dlogf = _exact_nn(upper, dcum) + carry[0:1, :]
        carry[0:1, :] = dlogf[0:1, :]
        z = fl_ref[...] + bf_ref[...]
        ez = jnp.exp(-jnp.abs(z))
        sig_neg = jnp.where(z >= 0.0, ez, 1.0) / (1.0 + ez)
        dz = jnp.where(lane < N_HEADS, dlogf * sig_neg, 0.0)
        dfl_ref[...] = dz.astype(BF16)
        dbf_ref[...] += jnp.sum(dz, axis=0, keepdims=True)

    rev3 = lambda i: (0, nb - 1 - i, 0)
    rev = lambda i: (nb - 1 - i, 0)
    return pl.pallas_call(
        body, name="gate_bwd", grid=(nb,),
        in_specs=[pl.BlockSpec((N_HEADS, 128, tm), lambda i: (0, 0, nb - 1 - i)),
                  pl.BlockSpec((N_HEADS, tm, 128), rev3),
                  pl.BlockSpec((tm, a), rev), pl.BlockSpec((tm, 128), rev), _full(bf)],
        out_specs=[pl.BlockSpec((tm, 3 * a), rev), pl.BlockSpec((tm, 128), rev),
                   pl.BlockSpec((1, 128), lambda i: (0, 0))],
        out_shape=[jax.ShapeDtypeStruct((s, 3 * a), BF16),
                   jax.ShapeDtypeStruct((s, 128), BF16),
                   jax.ShapeDtypeStruct((1, 128), F32)],
        scratch_shapes=[pltpu.VMEM((8, 128), F32)],
        compiler_params=_params(("arbitrary",)),
    )(dqa, dka, dv, fl, bf)


def _in_proj_bwd(after, dqkv, dfl, dbcx, w_qkv, w_f, w_bcx, x, g, dh):
    s, d = x.shape
    tm = min(TILE_ROWS, s)

    def body(after_ref, dq_ref, df_ref, db_ref, wq_ref, wf_ref, wb_ref, x_ref, g_ref, dh_ref,
             gx_ref, dg_ref):
        i = pl.program_id(0)
        dn = (_nt(dq_ref[...], wq_ref[...]) + _nt(df_ref[...], wf_ref[...])
              + _nt(db_ref[...], wb_ref[...]))
        dx, dg = _rms_bwd(dn, x_ref[...], g_ref[...])
        gx_ref[...] = dh_ref[...] + dx

        @pl.when(i == 0)
        def _():
            dg_ref[...] = dg

        @pl.when(i > 0)
        def _():
            dg_ref[...] += dg

    rows = lambda c: pl.BlockSpec((tm, c), lambda i: (i, 0))
    return pl.pallas_call(
        body, name="in_proj_bwd", grid=(s // tm,),
        in_specs=[ANY, rows(dqkv.shape[1]), rows(dfl.shape[1]), rows(dbcx.shape[1]),
                  _full(w_qkv), _full(w_f), _full(w_bcx), rows(d), _full(g), rows(d)],
        out_specs=[rows(d), pl.BlockSpec((1, d), lambda i: (0, 0))],
        out_shape=[jax.ShapeDtypeStruct((s, d), F32), jax.ShapeDtypeStruct((1, d), F32)],
        compiler_params=_params(("arbitrary",)),
    )(after, dqkv, dfl, dbcx, w_qkv, w_f, w_bcx, x, g, dh)


def _wgrad_in(n, dys):
    s, d = n.shape
    m = len(dys)
    tk = min(TILE_ROWS, s)
    nk = s // tk

    def body(*refs):
        n_ref, dy_refs, dw_refs, accs = refs[0], refs[1:1 + m], refs[1 + m:1 + 2 * m], refs[1 + 2 * m:]
        k = pl.program_id(0)

        @pl.when(k == 0)
        def _():
            for acc in accs:
                acc[...] = jnp.zeros_like(acc)

        nb = n_ref[...]
        for dy_ref, acc in zip(dy_refs, accs):
            acc[...] += _tn(nb, dy_ref[...])

        @pl.when(k == nk - 1)
        def _():
            for dw_ref, acc in zip(dw_refs, accs):
                dw_ref[...] = acc[...].T.astype(BF16)

    return pl.pallas_call(
        body, name="wgrad_in", grid=(nk,),
        in_specs=[pl.BlockSpec((tk, d), lambda k: (k, 0))]
        + [pl.BlockSpec((tk, dy.shape[1]), lambda k: (k, 0)) for dy in dys],
        out_specs=[pl.BlockSpec((dy.shape[1], d), lambda k: (0, 0)) for dy in dys],
        out_shape=[jax.ShapeDtypeStruct((dy.shape[1], d), BF16) for dy in dys],
        scratch_shapes=[pltpu.VMEM((d, dy.shape[1]), F32) for dy in dys],
        compiler_params=_params(("arbitrary",)),
    )(n, *dys)


def _row_tile(rows):
    t = min(TILE_ELEM_ROWS, rows)
    while rows % t:
        t //= 2
    return t


def _adamw_math(w, g, m, v):
    m = ADAM_B1 * m + (1.0 - ADAM_B1) * g
    v = ADAM_B2 * v + (1.0 - ADAM_B2) * jnp.square(g)
    m_hat = m / (1.0 - ADAM_B1 ** ADAM_STEP)
    v_hat = v / (1.0 - ADAM_B2 ** ADAM_STEP)
    delta = -ADAM_LR * (m_hat / (jnp.sqrt(v_hat) + ADAM_EPS) + ADAM_WD * w)
    return delta, m, v


def _adamw(w, g, m, v, name):
    rows, cols = w.shape

    def body(w_ref, g_ref, m_ref, v_ref, d_ref, nm_ref, nv_ref):
        delta, nm, nv = _adamw_math(w_ref[...], g_ref[...], m_ref[...], v_ref[...])
        d_ref[...] = delta
        nm_ref[...] = nm
        nv_ref[...] = nv

    if rows % 8 == 0:
        tr = _row_tile(rows)
        grid, spec = (rows // tr,), pl.BlockSpec((tr, cols), lambda i: (i, 0))
    else:
        grid, spec = (cols // 256,), pl.BlockSpec((rows, 256), lambda i: (0, i))
    out = jax.ShapeDtypeStruct(w.shape, F32)
    return pl.pallas_call(
        body, name=name, grid=grid, in_specs=[spec] * 4, out_specs=[spec] * 3,
        out_shape=[out, out, out], compiler_params=_params(("parallel",)),
    )(w, g, m, v)


def _sum_devices(parts):
    def body(p_ref, g_ref):
        g = p_ref[0]
        for k in range(1, N_DEV):
            g = g + p_ref[k]
        g_ref[...] = g

    return pl.pallas_call(
        body, name="sum_devices",
        in_specs=[pl.BlockSpec(memory_space=pltpu.VMEM)],
        out_specs=pl.BlockSpec(memory_space=pltpu.VMEM),
        out_shape=jax.ShapeDtypeStruct(parts.shape[1:], F32),
    )(parts)


def _mesh_position():
    x, y, c = lax.axis_index("x"), lax.axis_index("y"), lax.axis_index("c")
    chips = [(1 - x, y), (x, 1 - y), (1 - x, 1 - y)]
    return x, y, c, chips


ANY = pl.BlockSpec(memory_space=pl.ANY)
HBM = pl.BlockSpec(memory_space=pltpu.HBM)
SEM = pl.BlockSpec(memory_space=pltpu.SEMAPHORE)
SPLIT_COPY_EFFECT = pltpu.SideEffectType.DATAFLOW_SIDE_EFFECTING


def _in_hbm(a):
    return pltpu.with_memory_space_constraint(a, pltpu.HBM)


def _chip_copies(views, srcs, lands, send, recv, waiting=False):
    _, _, c, chips = _mesh_position()
    cps = []
    for a in range(len(srcs)):
        for k, (px, py) in enumerate(chips):
            src, dst = views(a, k, srcs[a], lands[a], c, 2 * px + py)
            sem = a * (N_CHIPS - 1) + k
            cps.append(pltpu.make_async_remote_copy(
                src_ref=src, dst_ref=dst, send_sem=send.at[sem], recv_sem=recv.at[sem],
                device_id=(px, py, c), device_id_type=MESH))
    return cps


def _ici_start(sources, land_shapes, copies, after, name, per_array=N_CHIPS - 1):
    n = len(sources)

    def body(*refs):
        srcs, lands = refs[:n], refs[n:2 * n]
        send, recv = refs[2 * n + 1], refs[2 * n + 2]
        token = refs[-1]
        for cp in copies(srcs, lands, send, recv, False):
            cp.start()
        token[...] = jnp.zeros_like(token)

    lands = [_in_hbm(lax.empty(s.shape, s.dtype)) for s in land_shapes]
    outs = pl.pallas_call(
        body, name=name,
        in_specs=[HBM] * (2 * n) + [ANY],
        out_specs=[SEM, SEM] + [HBM] * (2 * n) + [pl.BlockSpec(memory_space=pltpu.VMEM)],
        out_shape=[pltpu.SemaphoreType.DMA((n * per_array,))] * 2
        + [pltpu.HBM(a.shape, a.dtype) for a in sources]
        + [pltpu.HBM(s.shape, s.dtype) for s in land_shapes]
        + [jax.ShapeDtypeStruct((8, 128), F32)],
        input_output_aliases={i: 2 + i for i in range(2 * n)},
        compiler_params=pltpu.CompilerParams(has_side_effects=SPLIT_COPY_EFFECT),
    )(*[_in_hbm(a) for a in sources], *lands, after)
    return outs[0], outs[1], list(outs[2:2 + n]), list(outs[2 + n:2 + 2 * n]), outs[-1]


def _ici_wait(handle, copies, after, name):
    send, recv, srcs, lands, _ = handle
    n = len(srcs)

    def body(*refs):
        src_refs, land_refs = refs[:n], refs[n:2 * n]
        for cp in copies(src_refs, land_refs, refs[2 * n], refs[2 * n + 1], True):
            cp.wait_send()
            cp.wait_recv()

    outs = pl.pallas_call(
        body, name=name,
        in_specs=[HBM] * (2 * n) + [SEM, SEM, ANY],
        out_specs=[HBM] * (2 * n),
        out_shape=[pltpu.HBM(a.shape, a.dtype) for a in srcs]
        + [pltpu.HBM(a.shape, a.dtype) for a in lands],
        input_output_aliases={i: i for i in range(2 * n)},
        compiler_params=pltpu.CompilerParams(has_side_effects=SPLIT_COPY_EFFECT),
    )(*srcs, *lands, send, recv, after)
    return list(outs[:n]), list(outs[n:])


def _gather_views(split):
    def views(a, k, src, land, c, slot):
        if split[a]:
            half = src.shape[0] // 2
            src = src.at[pl.ds(c * half, half)]
        return src, land.at[k]
    return views


def _gather_whole_views(a, k, src, land, c, slot):
    x, y, _, _ = _mesh_position()
    return src, land.at[2 * x + y]


SCATTER_COPIES = 2 * (N_CHIPS - 1)


def _scatter_copies(srcs, lands, send, recv, waiting):
    _, _, c, chips = _mesh_position()
    cps = []
    for a in range(len(srcs)):
        half = srcs[a].shape[1] // 2
        for k, (px, py) in enumerate(chips):
            for h in range(2):
                arrival = 2 * k + (h if waiting else c)
                cps.append(pltpu.make_async_remote_copy(
                    src_ref=srcs[a].at[2 * px + py, pl.ds(h * half, half)],
                    dst_ref=lands[a].at[arrival],
                    send_sem=send.at[a * SCATTER_COPIES + 2 * k + h],
                    recv_sem=recv.at[a * SCATTER_COPIES + arrival],
                    device_id=(px, py, h), device_id_type=MESH))
    return cps


def _gather_land_shapes(shards, split):
    return [jax.ShapeDtypeStruct(
        (N_CHIPS - 1, a.shape[0] // 2 if sp else a.shape[0]) + a.shape[1:], a.dtype)
        for a, sp in zip(shards, split)]


def _gather_finish(shards, lands, split, name):
    n = len(shards)
    ns = sum(split)
    d_index = {a: i for i, a in enumerate(a for a in range(n) if split[a])}

    def body(*refs):
        shard, land, outs = refs[:n], refs[n:2 * n], refs[2 * n:3 * n]
        obuf, fbuf = refs[3 * n:4 * n], refs[4 * n:5 * n]
        dbuf = refs[5 * n:5 * n + ns]
        ld_own, st_own, ld, st_mine, st_sib, send, recv = refs[5 * n + ns:]
        x, y, c, chips = _mesh_position()
        me = 2 * x + y
        own_loads, loads, sends, pending = [], {}, [], []
        for a in range(n):
            cp = pltpu.make_async_copy(shard[a], obuf[a], ld_own.at[a])
            cp.start()
            own_loads.append(cp)
        for a in range(n):
            for k in range(N_CHIPS - 1):
                cp = pltpu.make_async_copy(land[a].at[k], fbuf[a].at[k], ld.at[a, k])
                cp.start()
                loads[a, k] = cp
        for a in range(n):
            own_loads[a].wait()
            cp = pltpu.make_async_copy(obuf[a], outs[a].at[me], st_own.at[a])
            cp.start()
            pending.append(cp)
        for a in range(n):
            rows = shard[a].shape[0]
            for k, (px, py) in enumerate(chips):
                loads[a, k].wait()
                part = pl.ds(c * (rows // 2), rows // 2) if split[a] else pl.ds(0, rows)
                cp = pltpu.make_async_copy(fbuf[a].at[k], outs[a].at[2 * px + py, part],
                                           st_mine.at[a, k])
                cp.start()
                pending.append(cp)
                if split[a]:
                    fw = pltpu.make_async_remote_copy(
                        src_ref=fbuf[a].at[k], dst_ref=dbuf[d_index[a]].at[k],
                        send_sem=send.at[a, k], recv_sem=recv.at[a, k],
                        device_id=(x, y, 1 - c), device_id_type=MESH)
                    fw.start()
                    sends.append((a, k, fw))
        for a, k, fw in sends:
            px, py = chips[k]
            half = shard[a].shape[0] // 2
            fw.wait_recv()
            cp = pltpu.make_async_copy(dbuf[d_index[a]].at[k],
                                       outs[a].at[2 * px + py, pl.ds((1 - c) * half, half)],
                                       st_sib.at[a, k])
            cp.start()
            pending.append(cp)
        for _, _, fw in sends:
            fw.wait_send()
        for cp in pending:
            cp.wait()

    stage = [pltpu.VMEM(a.shape, a.dtype) for a in lands]
    dma = lambda *shape: pltpu.SemaphoreType.DMA(shape)
    return pl.pallas_call(
        body, name=name,
        in_specs=[ANY] * (2 * n), out_specs=[ANY] * n,
        out_shape=[jax.ShapeDtypeStruct((N_CHIPS,) + a.shape, a.dtype) for a in shards],
        scratch_shapes=[pltpu.VMEM(a.shape, a.dtype) for a in shards] + stage
        + [s for s, sp in zip(stage, split) if sp]
        + [dma(n), dma(n), dma(n, 3), dma(n, 3), dma(n, 3), dma(n, 3), dma(n, 3)],
        compiler_params=pltpu.CompilerParams(vmem_limit_bytes=VMEM_LIMIT_BYTES),
    )(*shards, *lands)


def _sum_chunk(rows):
    return next(r for r in range(SUM_CHUNK_ROWS, 0, -16) if rows % r == 0)


def _sum_and_share(partials, lands, name):
    n = len(partials)

    def body(*refs):
        own, landed, outs = refs[:n], refs[n:2 * n], refs[2 * n:3 * n]
        obuf, xbuf, ybuf, gbuf, sbuf, rbuf = (refs[(3 + k) * n:(4 + k) * n] for k in range(6))
        ld_own, ld_send, ld_got, st_own, st_sib, send_p, recv_p, send_s, recv_s = refs[9 * n:]
        x, y, c, _ = _mesh_position()
        me = 2 * x + y
        sibling = (x, y, 1 - c)

        def to_sibling(src, dst, send, recv, a):
            return pltpu.make_async_remote_copy(src_ref=src, dst_ref=dst, send_sem=send.at[a],
                                                recv_sem=recv.at[a], device_id=sibling,
                                                device_id_type=MESH)

        loads, firsts, seconds, stores = [], [], [], []
        for a in range(n):
            half = obuf[a].shape[0]
            cps = [pltpu.make_async_copy(own[a].at[me, pl.ds((1 - c) * half, half)], xbuf[a],
                                         ld_send.at[a]),
                   pltpu.make_async_copy(own[a].at[me, pl.ds(c * half, half)], obuf[a], ld_own.at[a]),
                   pltpu.make_async_copy(landed[a], gbuf[a], ld_got.at[a])]
            for cp in cps:
                cp.start()
            loads.append(cps)
        for a in range(n):
            loads[a][0].wait()
            rc = to_sibling(xbuf[a], ybuf[a], send_p, recv_p, a)
            rc.start()
            firsts.append(rc)
        for a in range(n):
            firsts[a].wait_recv()
            loads[a][1].wait()
            loads[a][2].wait()
            half = obuf[a].shape[0]
            rows = _sum_chunk(half)

            def add(k, carry, a=a, rows=rows):
                at = pl.ds(pl.multiple_of(k * rows, rows), rows)
                acc = obuf[a][at].astype(F32) + ybuf[a][at].astype(F32)
                for j in range(SCATTER_COPIES):
                    acc = acc + gbuf[a][j, at].astype(F32)
                sbuf[a][at] = acc
                return carry

            lax.fori_loop(0, half // rows, add, 0)
            rc = to_sibling(sbuf[a], rbuf[a], send_s, recv_s, a)
            rc.start()
            seconds.append(rc)
            cp = pltpu.make_async_copy(sbuf[a], outs[a].at[pl.ds(c * half, half)], st_own.at[a])
            cp.start()
            stores.append(cp)
        for a in range(n):
            half = obuf[a].shape[0]
            seconds[a].wait_recv()
            cp = pltpu.make_async_copy(rbuf[a], outs[a].at[pl.ds((1 - c) * half, half)], st_sib.at[a])
            cp.start()
            stores.append(cp)
        for rc in firsts + seconds:
            rc.wait_send()
        for cp in stores:
            cp.wait()

    halves = [(a.shape[1] // 2, a.shape[2]) for a in partials]
    return pl.pallas_call(
        body, name=name,
        in_specs=[ANY] * (2 * n), out_specs=[ANY] * n,
        out_shape=[jax.ShapeDtypeStruct((2 * h[0], h[1]), F32) for h in halves],
        scratch_shapes=[pltpu.VMEM(h, BF16) for h in halves] * 3
        + [pltpu.VMEM(g.shape, BF16) for g in lands]
        + [pltpu.VMEM(h, F32) for h in halves] * 2
        + [pltpu.SemaphoreType.DMA((n,))] * 9,
        compiler_params=pltpu.CompilerParams(vmem_limit_bytes=VMEM_LIMIT_BYTES),
    )(*partials, *lands)


def _gather_small(part):
    def body(in_ref, out_ref, send, recv, local):
        x, y, c, _ = _mesh_position()
        me = 4 * x + 2 * y + c
        cps = [pltpu.make_async_copy(in_ref, out_ref.at[me], local)]
        k = 0
        for fx in range(2):
            for fy in range(2):
                for fc in range(2):
                    if fx or fy or fc:
                        cps.append(pltpu.make_async_remote_copy(
                            src_ref=in_ref, dst_ref=out_ref.at[me], send_sem=send.at[k],
                            recv_sem=recv.at[k], device_id=(x ^ fx, y ^ fy, c ^ fc),
                            device_id_type=MESH))
                        k += 1
        for cp in cps:
            cp.start()
        for cp in cps:
            cp.wait()

    return pl.pallas_call(
        body, name="gather_small",
        in_specs=[pl.BlockSpec(memory_space=pltpu.VMEM)],
        out_specs=pl.BlockSpec(memory_space=pltpu.VMEM),
        out_shape=jax.ShapeDtypeStruct((N_DEV,) + part.shape, part.dtype),
        scratch_shapes=[pltpu.SemaphoreType.DMA((N_DEV - 1,)), pltpu.SemaphoreType.DMA((N_DEV - 1,)),
                        pltpu.SemaphoreType.DMA],
    )(part)


def _scatter_start(grads, after, tag):
    lands = [jax.ShapeDtypeStruct((SCATTER_COPIES, g.shape[1] // 2, g.shape[2]), g.dtype)
             for g in grads]
    return _ici_start(grads, lands, _scatter_copies, after, "scatter_start_" + tag,
                      per_array=SCATTER_COPIES)


def _scatter_finish(handle, after, tag):
    grads, lands = _ici_wait(handle, _scatter_copies, after, "scatter_wait_" + tag)
    return _sum_and_share(grads, lands, "sum_and_share_" + tag)


def _pad_rows(a, rows):
    return jnp.pad(a, ((0, rows - a.shape[0]), (0, 0)))


def kernel(x, norm_mix_0, w_in_0, b_f_0, conv_w_0, w_out_0, norm_ffn_0, w_up_0, w_down_0, norm_mix_1, pool_w_1, pool_scale_1, norm_ffn_1, w_up_1, w_down_1, final_norm, loss_target, m_norm_mix_0, m_w_in_0, m_b_f_0, m_conv_w_0, m_w_out_0, m_norm_ffn_0, m_w_up_0, m_w_down_0, m_norm_mix_1, m_pool_w_1, m_pool_scale_1, m_norm_ffn_1, m_w_up_1, m_w_down_1, m_final_norm, v_norm_mix_0, v_w_in_0, v_b_f_0, v_conv_w_0, v_w_out_0, v_norm_ffn_0, v_w_up_0, v_w_down_0, v_norm_mix_1, v_pool_w_1, v_pool_scale_1, v_norm_ffn_1, v_w_up_1, v_w_down_1, v_final_norm):
    d = x.shape[-1]
    a = N_HEADS * HEAD_DIM
    c_conv = conv_w_0.shape[1] * N_CHIPS
    xs = x[0]
    target = loss_target[0]
    row = lambda vec: vec.reshape(1, -1)

    big = [w_in_0, w_out_0, w_up_0, w_down_0, pool_w_1, w_up_1, w_down_1]
    first = [w_in_0.astype(BF16)]
    first_split = [True]
    copies_a = functools.partial(_chip_copies, _gather_views(first_split))
    copies_b = functools.partial(_chip_copies, _gather_whole_views)
    start_a = _ici_start(first, _gather_land_shapes(first, first_split), copies_a, b_f_0,
                         "gather_start_a")
    zero = start_a[-1][0, 0]
    rest = [(w + zero).astype(BF16)
            for w in (w_out_0, w_up_0, w_down_0, pool_w_1, w_up_1, w_down_1)]
    rest = rest + [conv_w_0]
    start_b = _ici_start(rest, [jax.ShapeDtypeStruct((N_CHIPS,) + w.shape, w.dtype) for w in rest],
                         copies_b, start_a[-1], "gather_start_b")
    n0 = _rms_pre(start_b[-1], xs, row(norm_mix_0))
    first, land_a = _ici_wait(start_a, copies_a, n0, "gather_wait_a")
    (g_in,) = _gather_finish(first, land_a, first_split, "gather_finish_a")
    w_in = g_in.transpose(1, 0, 2).reshape(d, -1)
    w_qkv = w_in[:, :3 * a]
    w_f = jnp.pad(w_in[:, 3 * a:3 * a + N_HEADS], ((0, 0), (0, 128 - N_HEADS)))
    w_bcx = w_in[:, 3 * a + N_HEADS:]
    bf = jnp.pad(b_f_0, (0, 128 - N_HEADS)).reshape(1, 128)

    qkv, fl, bcx = _in_proj(n0, w_qkv, w_f, w_bcx)
    qa, ka = _gate_prep(fl, bf, qkv)
    o, lse = _attn_fwd(qa, ka, qkv)
    rest, land_b = _ici_wait(start_b, copies_b, o, "gather_wait_b")
    own_slot = 2 * lax.axis_index("x") + lax.axis_index("y")
    g_out, g_up0, g_down0, g_pool, g_up1, g_down1, g_conv = [
        lax.dynamic_update_index_in_dim(land, shard, own_slot, 0)
        for land, shard in zip(land_b, rest)]
    w_out = g_out.reshape(-1, d)
    conv_w = _pad_rows(g_conv.transpose(1, 0, 2).reshape(conv_w_0.shape[0], c_conv), 8)
    h1 = _conv_out(o, bcx, conv_w, w_out, xs)
    w_down0 = g_down0.reshape(-1, d)
    w_down1 = g_down1.reshape(-1, d)
    pool_w = g_pool.transpose(1, 0, 2, 3).reshape(pool_w_1.shape[0], -1, pool_w_1.shape[2])
    h2, a0, nf0 = _mlp_fwd(h1, row(norm_ffn_0), g_up0, w_down0, "mlp_fwd_0")
    h3 = _pool_fwd(h2, row(norm_mix_1), pool_w, row(pool_scale_1))
    dh4, a1, nf1, loss_part, d_final = _mlp_fwd(h3, row(norm_ffn_1), g_up1, w_down1, "mlp_fwd_1",
                                                head=(row(final_norm), target))

    slot_cols = g_up0.shape[2]
    pool_cols = pool_w.shape[2]
    da1, dz1, dh3, d_nffn1 = _mlp_bwd_x(dh4, a1, g_up1, w_down1, h3, row(norm_ffn_1), "mlp_bwd_x_1")
    dw_up1, dw_down1 = _mlp_bwd_w(nf1, da1, a1, dz1, slot_cols, "mlp_bwd_w_1")
    scatter_1 = _scatter_start([dw_up1, dw_down1.reshape(N_CHIPS, -1, d)], bf, "mlp1")
    dh2, dw_pool, d_pscale, d_nmix1 = _pool_bwd(scatter_1[-1], dh3, h2, row(norm_mix_1), pool_w,
                                                row(pool_scale_1))
    da0, dz0, dh1, d_nffn0 = _mlp_bwd_x(dh2, a0, g_up0, w_down0, h1, row(norm_ffn_0), "mlp_bwd_x_0")
    dw_up0, dw_down0 = _mlp_bwd_w(nf0, da0, a0, dz0, slot_cols, "mlp_bwd_w_0")
    dw_pool = (dw_pool.reshape(pool_w.shape[0], N_CHIPS, -1, pool_cols).transpose(1, 0, 2, 3)
               .reshape(N_CHIPS, -1, pool_cols))
    scatter_0 = _scatter_start([dw_up0, dw_down0.reshape(N_CHIPS, -1, d), dw_pool], bf, "mlp0")
    do, delta, dbcx, dw_out, d_conv = _conv_out_bwd(scatter_0[-1], dh1, w_out, o, bcx, conv_w)
    scatter_o = _scatter_start([dw_out.reshape(N_CHIPS, -1, d)], bf, "out")
    dqa, dka, dv = _attn_bwd(scatter_o[-1], qa, ka, qkv, do, lse, delta)
    dqkv, dfl, d_bf = _gate_bwd(dqa, dka, dv, fl, bf)
    dw_qkv, dw_f, dw_bcx = _wgrad_in(n0, [dqkv, dfl, dbcx])
    dw_in = jnp.concatenate([dw_qkv, dw_f[:N_HEADS], dw_bcx], axis=0).reshape(N_CHIPS, -1, d)
    slot_rows = -(-dw_in.shape[1] // 32) * 32
    dw_in = jnp.pad(dw_in, ((0, 0), (0, slot_rows - dw_in.shape[1]), (0, 0)))
    scatter_m = _scatter_start([dw_in], bf, "mixer")
    grad_x, d_nmix0 = _in_proj_bwd(scatter_m[-1], dqkv, dfl, dbcx, w_qkv, w_f, w_bcx, xs,
                                   row(norm_mix_0), dh1)

    r_up1, r_down1 = _scatter_finish(scatter_1, grad_x, "mlp1")
    r_up0, r_down0, r_pool = _scatter_finish(scatter_0, grad_x, "mlp0")
    (r_out,) = _scatter_finish(scatter_o, grad_x, "out")
    (r_in,) = _scatter_finish(scatter_m, grad_x, "mixer")
    reduced = [r_in, r_out, r_up0, r_down0, r_pool, r_up1, r_down1]
    moments = [(m_w_in_0, v_w_in_0), (m_w_out_0, v_w_out_0), (m_w_up_0, v_w_up_0),
               (m_w_down_0, v_w_down_0), (m_pool_w_1, v_pool_w_1), (m_w_up_1, v_w_up_1),
               (m_w_down_1, v_w_down_1)]
    big_out = []
    for k, (w, g, (m, v)) in enumerate(zip(big, reduced, moments)):
        if w.shape[-1] % 128:
            view = lambda t: t.reshape(-1, t.shape[-1]).T
            back = lambda t: t.T.reshape(w.shape)
            g_view = g[:w.shape[-1]]
        else:
            view = lambda t: t.reshape(-1, t.shape[-1])
            back = lambda t: t.reshape(w.shape)
            g_view = view(g)
        delta_w, new_m, new_v = _adamw(view(w), g_view, view(m), view(v), "adamw_%d" % k)
        big_out.append((back(g_view), back(delta_w), back(new_m), back(new_v)))

    tail = jnp.concatenate([d_conv[0:3].reshape(-1)[d:], d_bf[0, :N_HEADS], loss_part[0, :1]])
    small_part = jnp.concatenate(
        [d_nmix0, d_nffn0, d_nmix1, d_pscale, d_nffn1, d_final,
         d_conv[0:3].reshape(1, -1)[:, :d],
         jnp.pad(tail, (0, d - tail.shape[0])).reshape(1, d)], axis=0)
    parts = _gather_small(small_part)

    chip = 2 * lax.axis_index("x") + lax.axis_index("y")
    cw_cols = conv_w_0.shape[1]

    def conv_block(full):
        mine = lax.dynamic_slice_in_dim(full, chip * cw_cols, cw_cols, axis=1)
        return jnp.pad(mine.reshape(-1), (0, d - mine.size))

    def small_rows(vals, cw, bfv):
        return jnp.stack(list(vals) + [cw, jnp.pad(bfv, (0, d - N_HEADS))])

    smalls_w = [norm_mix_0, norm_ffn_0, norm_mix_1, pool_scale_1, norm_ffn_1, final_norm]
    smalls_m = [m_norm_mix_0, m_norm_ffn_0, m_norm_mix_1, m_pool_scale_1, m_norm_ffn_1, m_final_norm]
    smalls_v = [v_norm_mix_0, v_norm_ffn_0, v_norm_mix_1, v_pool_scale_1, v_norm_ffn_1, v_final_norm]
    pad_cw = lambda t: jnp.pad(t.reshape(-1), (0, d - t.size))
    w_rows = small_rows(smalls_w, pad_cw(conv_w_0), b_f_0)
    m_rows = small_rows(smalls_m, pad_cw(m_conv_w_0), m_b_f_0)
    v_rows = small_rows(smalls_v, pad_cw(v_conv_w_0), v_b_f_0)

    g_sum = _sum_devices(parts)
    conv_full = jnp.concatenate([g_sum[6], g_sum[7, :3 * c_conv - d]]).reshape(3, c_conv)
    bf_grad = g_sum[7, 3 * c_conv - d:3 * c_conv - d + N_HEADS]
    loss = g_sum[7, 3 * c_conv - d + N_HEADS]
    g_rows = jnp.concatenate(
        [g_sum[0:6], conv_block(conv_full).reshape(1, d),
         jnp.pad(bf_grad, (0, d - N_HEADS)).reshape(1, d)], axis=0)
    d_rows, nm_rows, nv_rows = _adamw(w_rows, g_rows, m_rows, v_rows, "adamw_small")

    def unpack(rows):
        cw = rows[6, :conv_w_0.size].reshape(conv_w_0.shape)
        return [rows[0], rows[1], rows[2], rows[3], rows[4], rows[5], cw, rows[7, :N_HEADS]]

    def assemble(kind):
        sm = unpack([g_rows, d_rows, nm_rows, nv_rows][kind])
        lg = [t[kind] for t in big_out]
        return [sm[0], lg[0], sm[7], sm[6], lg[1], sm[1], lg[2], lg[3],
                sm[2], lg[4], sm[3], sm[4], lg[5], lg[6], sm[5]]

    return (loss, grad_x[None], *assemble(0), *assemble(1), *assemble(2), *assemble(3))
```

```python
import functools

import jax
import jax.numpy as jnp
from jax import lax
from jax.experimental import pallas as pl
from jax.experimental.pallas import tpu as pltpu

F32 = jnp.float32
BF16 = jnp.bfloat16

RMS_EPS = 1e-6
HEAD_DIM = 64
N_HEADS = 8
ATTN_SCALE = HEAD_DIM ** -0.5
LOG2_E = 1.4426950408889634
POOL_WINDOWS = (2, 4, 8, 16)
POOL_HALO = 16
CONV_HALO = 8
NEG_BIG = -1e30

ADAM_LR = 0.001
ADAM_B1 = 0.9
ADAM_B2 = 0.999
ADAM_EPS = 1e-08
ADAM_WD = 0.01
ADAM_STEP = 10

N_CHIPS = 4
N_DEV = 8
MESH = pl.DeviceIdType.MESH

VMEM_LIMIT_BYTES = 56 * 1024 * 1024

TILE_ROWS = 512
TILE_ATTN = 512
TILE_MLP_ROWS = 1024
TILE_MLP_FF = 1024
TILE_MLP_BWD_FF = 512
TILE_HEAD_ROWS = 256
TILE_WGRAD_K = 1024
TILE_WGRAD_N = 1024
TILE_ELEM_ROWS = 256
SUM_CHUNK_ROWS = 128

LANE_CQ = 64
LANE_CK = 88


def _params(semantics):
    return pltpu.CompilerParams(dimension_semantics=semantics,
                                vmem_limit_bytes=VMEM_LIMIT_BYTES)


def _nn(a, b):
    return lax.dot_general(a, b, (((1,), (0,)), ((), ())), preferred_element_type=F32)


def _nt(a, b):
    return lax.dot_general(a, b, (((1,), (1,)), ((), ())), preferred_element_type=F32)


def _tn(a, b):
    return lax.dot_general(a, b, (((0,), (0,)), ((), ())), preferred_element_type=F32)


def _split3(v):
    hi = v.astype(BF16)
    r1 = v - hi.astype(F32)
    mid = r1.astype(BF16)
    lo = (r1 - mid.astype(F32)).astype(BF16)
    return hi, mid, lo


def _exact_nn(sel, v):
    hi, mid, lo = _split3(v)
    return _nn(sel, hi) + _nn(sel, mid) + _nn(sel, lo)


def _exact_nt(sel, v):
    hi, mid, lo = _split3(v)
    return _nt(sel, hi) + _nt(sel, mid) + _nt(sel, lo)


def _rms_fwd(x, g):
    r = lax.rsqrt(jnp.mean(x * x, axis=-1, keepdims=True) + RMS_EPS)
    return x * r * g, r


def _rms_bwd(dn, x, g):
    r = lax.rsqrt(jnp.mean(x * x, axis=-1, keepdims=True) + RMS_EPS)
    xh = x * r
    gy = dn * g
    dx = r * (gy - xh * jnp.mean(gy * xh, axis=-1, keepdims=True))
    return dx, jnp.sum(dn * xh, axis=0, keepdims=True)


def _lane(shape):
    return lax.broadcasted_iota(jnp.int32, shape, len(shape) - 1)


def _row(shape):
    return lax.broadcasted_iota(jnp.int32, shape, len(shape) - 2)


def _full(a):
    nd = a.ndim
    return pl.BlockSpec(a.shape, lambda *_: (0,) * nd)


def _rms_pre(after, x, g):
    s, d = x.shape
    tm = min(TILE_ROWS, s)

    def body(after_ref, x_ref, g_ref, n_ref):
        n, _ = _rms_fwd(x_ref[...], g_ref[...])
        n_ref[...] = n.astype(BF16)

    rows = pl.BlockSpec((tm, d), lambda i: (i, 0))
    return pl.pallas_call(
        body, name="rms_pre", grid=(s // tm,),
        in_specs=[ANY, rows, _full(g)], out_specs=rows,
        out_shape=jax.ShapeDtypeStruct((s, d), BF16),
        compiler_params=_params(("parallel",)),
    )(after, x, g)


def _in_proj(n, w_qkv, w_f, w_bcx):
    s, d = n.shape
    tm = min(TILE_ROWS, s)

    def body(n_ref, wq_ref, wf_ref, wb_ref, qkv_ref, fl_ref, bcx_ref):
        nb = n_ref[...]
        qkv_ref[...] = _nn(nb, wq_ref[...]).astype(BF16)
        fl_ref[...] = _nn(nb, wf_ref[...])
        bcx_ref[...] = _nn(nb, wb_ref[...])

    rows = lambda c: pl.BlockSpec((tm, c), lambda i: (i, 0))
    return pl.pallas_call(
        body, name="in_proj", grid=(s // tm,),
        in_specs=[rows(d), _full(w_qkv), _full(w_f), _full(w_bcx)],
        out_specs=[rows(w_qkv.shape[1]), rows(w_f.shape[1]), rows(w_bcx.shape[1])],
        out_shape=[jax.ShapeDtypeStruct((s, w_qkv.shape[1]), BF16),
                   jax.ShapeDtypeStruct((s, w_f.shape[1]), F32),
                   jax.ShapeDtypeStruct((s, w_bcx.shape[1]), F32)],
        compiler_params=_params(("parallel",)),
    )(n, w_qkv, w_f, w_bcx)


def _gate_prep(fl, bf, qkv):
    s = fl.shape[0]
    a = N_HEADS * HEAD_DIM
    tm = min(TILE_ROWS, s)

    def body(fl_ref, bf_ref, q_ref, k_ref, qa_ref, ka_ref, carry_ref):
        i = pl.program_id(0)

        @pl.when(i == 0)
        def _():
            carry_ref[...] = jnp.zeros_like(carry_ref)

        z = fl_ref[...] + bf_ref[...]
        logf = jnp.minimum(z, 0.0) - jnp.log(1.0 + jnp.exp(-jnp.abs(z)))
        lower = (_lane((tm, tm)) <= _row((tm, tm))).astype(BF16)
        cum = _exact_nn(lower, logf) + carry_ref[0:1, :]
        carry_ref[0:1, :] = cum[tm - 1:tm, :]

        lane = _lane((tm, 128))
        pieces = [p.astype(F32)
                  for p in _split3(jnp.where(lane < N_HEADS, LOG2_E * cum, 0.0))]
        shared_q = sum(pltpu.roll(p, LANE_CQ + N_HEADS * k, axis=1) for k, p in enumerate(pieces))
        shared_k = -sum(pltpu.roll(p, LANE_CK + N_HEADS * k, axis=1) for k, p in enumerate(pieces))
        for h in range(N_HEADS):
            at_q = functools.reduce(jnp.logical_or,
                                    [lane == LANE_CQ + N_HEADS * k + h for k in range(3)])
            at_k = functools.reduce(jnp.logical_or,
                                    [lane == LANE_CK + N_HEADS * k + h for k in range(3)])
            pair = slice((h // 2) * 128, (h // 2 + 1) * 128)
            qp = q_ref[:, pair].astype(F32)
            kp = k_ref[:, pair].astype(F32)
            if h % 2:
                qp = pltpu.roll(qp, HEAD_DIM, axis=1)
                kp = pltpu.roll(kp, HEAD_DIM, axis=1)
            q_bias = jnp.where(at_k, 1.0, shared_q)
            k_bias = jnp.where(at_q, 1.0, shared_k)
            qa_ref[h] = jnp.where(lane < HEAD_DIM, qp * (ATTN_SCALE * LOG2_E), q_bias).astype(BF16)
            ka_ref[h] = jnp.where(lane < HEAD_DIM, kp, k_bias).astype(BF16)

    aug = jax.ShapeDtypeStruct((N_HEADS, s, 128), BF16)
    aug_spec = pl.BlockSpec((N_HEADS, tm, 128), lambda i: (0, i, 0))
    return pl.pallas_call(
        body, name="gate_prep", grid=(s // tm,),
        in_specs=[pl.BlockSpec((tm, 128), lambda i: (i, 0)), _full(bf),
                  pl.BlockSpec((tm, a), lambda i: (i, 0)),
                  pl.BlockSpec((tm, a), lambda i: (i, 1))],
        out_specs=[aug_spec, aug_spec],
        out_shape=[aug, aug],
        scratch_shapes=[pltpu.VMEM((8, 128), F32)],
        compiler_params=_params(("arbitrary",)),
    )(fl, bf, qkv, qkv)


def _attn_fwd(qa, ka, qkv):
    s = qa.shape[1]
    a = N_HEADS * HEAD_DIM
    t = min(TILE_ATTN, s)
    n_pairs = N_HEADS // 2
    v_block0 = 2 * a // 128

    ones_lane = (HEAD_DIM, 0)

    def body(qa_ref, ka_ref, v_ref, o_ref, lse_ref, m_ref, acc_ref, s_even, s_odd):
        i = pl.program_id(1)
        m_ref[...] = jnp.full_like(m_ref, NEG_BIG)
        acc_ref[...] = jnp.zeros_like(acc_ref)
        upper_rows = _row((128, t)) < HEAD_DIM

        def keys(j):
            return pl.ds(pl.multiple_of(j * t, t), t)

        def scores_into(buf, j):
            for e in range(2):
                buf[e] = _nt(ka_ref[e, keys(j), :], qa_ref[e])

        def consume(buf, j, masked):
            vf = v_ref[keys(j), :].astype(F32)
            lane = _lane((t, 128))
            own = [lane < HEAD_DIM, lane >= HEAD_DIM]
            for e in range(2):
                v_head = jnp.where(own[e], vf, jnp.where(lane == ones_lane[e], 1.0, 0.0)).astype(BF16)
                sc = buf[e]
                if masked:
                    sc = jnp.where(_row((t, t)) <= _lane((t, t)), sc, NEG_BIG)
                m_prev = m_ref[e]
                m_new = jnp.maximum(m_prev, jnp.max(sc, axis=0, keepdims=True))
                p = jnp.exp2(sc - m_new).astype(BF16)
                acc_ref[e] = acc_ref[e] * jnp.exp2(m_prev - m_new) + _tn(v_head, p)
                m_ref[e] = m_new

        scores_into(s_even, 0)

        def two_tiles(p, carry):
            j = 2 * p
            scores_into(s_odd, j + 1)
            consume(s_even, j, False)
            scores_into(s_even, j + 2)
            consume(s_odd, j + 1, False)
            return carry

        lax.fori_loop(0, i // 2, two_tiles, 0)

        @pl.when(i % 2 == 0)
        def _():
            consume(s_even, i, True)

        @pl.when(i % 2 == 1)
        def _():
            scores_into(s_odd, i)
            consume(s_even, i - 1, False)
            consume(s_odd, i, True)

        denom = [acc_ref[e, ones_lane[e]:ones_lane[e] + 1, :] for e in range(2)]
        out_t = jnp.where(upper_rows, acc_ref[0] / denom[0], acc_ref[1] / denom[1])
        o_ref[...] = out_t.T.astype(BF16)
        lse = [m_ref[e] + LOG2_E * jnp.log(denom[e]) for e in range(2)]
        lse_ref[...] = jnp.where(_row((8, t)) == 0, lse[0], lse[1])

    return pl.pallas_call(
        body, name="attn_fwd", grid=(n_pairs, s // t),
        in_specs=[pl.BlockSpec((2, t, 128), lambda g, i: (g, i, 0)),
                  pl.BlockSpec((2, s, 128), lambda g, i: (g, 0, 0)),
                  pl.BlockSpec((s, 128), lambda g, i: (0, v_block0 + g))],
        out_specs=[pl.BlockSpec((t, 128), lambda g, i: (i, g)),
                   pl.BlockSpec((None, 8, t), lambda g, i: (g, 0, i))],
        out_shape=[jax.ShapeDtypeStruct((s, a), BF16),
                   jax.ShapeDtypeStruct((n_pairs, 8, s), F32)],
        scratch_shapes=[pltpu.VMEM((2, 1, t), F32), pltpu.VMEM((2, 128, t), F32),
                        pltpu.VMEM((2, t, t), F32), pltpu.VMEM((2, t, t), F32)],
        compiler_params=_params(("parallel", "arbitrary")),
    )(qa, ka, qkv)


def _conv_out(o, bcx, cw, w_out, x):
    s, d = x.shape
    c = o.shape[1]
    tm = min(TILE_ROWS, s)

    def body(o_ref, b_ref, c_ref, xin_ref, cw_ref, w_ref, x_ref, h_ref, ubuf):
        i = pl.program_id(0)

        @pl.when(i == 0)
        def _():
            ubuf[0:CONV_HALO, :] = jnp.zeros((CONV_HALO, c), F32)

        u = c_ref[...] * xin_ref[...]
        ubuf[CONV_HALO:CONV_HALO + tm, :] = u
        u1 = ubuf[CONV_HALO - 1:CONV_HALO - 1 + tm, :]
        u2 = ubuf[CONV_HALO - 2:CONV_HALO - 2 + tm, :]
        cv = (cw_ref[0:1, :] * u2 + cw_ref[1:2, :] * u1) + cw_ref[2:3, :] * u
        y = (b_ref[...] * cv).astype(BF16)
        mix = _nn(o_ref[...], w_ref[0:c, :]) + _nn(y, w_ref[c:2 * c, :])
        h_ref[...] = x_ref[...] + mix
        ubuf[0:CONV_HALO, :] = u[tm - CONV_HALO:tm, :]

    col = lambda k: pl.BlockSpec((tm, c), lambda i: (i, k))
    return pl.pallas_call(
        body, name="conv_out", grid=(s // tm,),
        in_specs=[col(0), col(0), col(1), col(2), _full(cw), _full(w_out),
                  pl.BlockSpec((tm, d), lambda i: (i, 0))],
        out_specs=pl.BlockSpec((tm, d), lambda i: (i, 0)),
        out_shape=jax.ShapeDtypeStruct((s, d), F32),
        scratch_shapes=[pltpu.VMEM((tm + CONV_HALO, c), F32)],
        compiler_params=_params(("arbitrary",)),
    )(o, bcx, bcx, bcx, cw, w_out, x)


def _mlp_fwd(h, g, w_up, w_down, name, head=None):
    s, d = h.shape
    ff = w_down.shape[0]
    slot_cols = w_up.shape[2]
    tm = min(TILE_MLP_ROWS, s)
    tf = min(TILE_MLP_FF, slot_cols)
    per_slot = slot_cols // tf
    nf = ff // tf
    n_head = 0 if head is None else 2
    chunk = min(TILE_HEAD_ROWS, tm)

    def body(*refs):
        h_ref, g_ref, wu_ref, wd_ref = refs[:4]
        out_ref, a_ref, n_ref = refs[4 + n_head:7 + n_head]
        nb_ref, acc_ref = refs[9 + n_head:11 + n_head] if head else refs[-2:]
        i = pl.program_id(0)
        f = pl.program_id(1)

        def target_copy():
            t_hbm, t_buf, t_sem = refs[5], refs[-2], refs[-1]
            return pltpu.make_async_copy(t_hbm.at[pl.ds(pl.multiple_of(i * tm, tm), tm), :],
                                         t_buf, t_sem)

        @pl.when(f == 0)
        def _():
            n, _ = _rms_fwd(h_ref[...], g_ref[...])
            nb = n.astype(BF16)
            nb_ref[...] = nb
            n_ref[...] = nb
            acc_ref[...] = jnp.zeros_like(acc_ref)
            if head is not None:
                target_copy().start()

        pre = _nn(nb_ref[...], wu_ref[...])
        a_ref[...] = pre.astype(BF16)
        r = jnp.square(jnp.maximum(pre, 0.0)).astype(BF16)
        acc_ref[...] += _nn(r, wd_ref[...])

        @pl.when(f == nf - 1)
        def _():
            if head is None:
                out_ref[...] = h_ref[...] + acc_ref[...]
            else:
                gf_ref, t_buf = refs[4], refs[-2]
                loss_ref, dg_ref = refs[7 + n_head:9 + n_head]
                target_copy().wait()
                part, dg = None, None
                for r0 in range(0, tm, chunk):
                    rows_ = slice(r0, r0 + chunk)
                    out = h_ref[rows_, :] + acc_ref[rows_, :]
                    y, _ = _rms_fwd(out, gf_ref[...])
                    err = y - t_buf[rows_, :]
                    p = 0.5 * jnp.sum(jnp.mean(err * err, axis=-1, keepdims=True), axis=0,
                                      keepdims=True)
                    dx, dgp = _rms_bwd(err / d, out, gf_ref[...])
                    out_ref[rows_, :] = dx
                    part = p if part is None else part + p
                    dg = dgp if dg is None else dg + dgp
                part = jnp.broadcast_to(part, loss_ref.shape)

                @pl.when(i == 0)
                def _():
                    loss_ref[...] = part
                    dg_ref[...] = dg

                @pl.when(i > 0)
                def _():
                    loss_ref[...] += part
                    dg_ref[...] += dg

    rows = pl.BlockSpec((tm, d), lambda i, f: (i, 0))
    in_specs = [rows, _full(g),
                pl.BlockSpec((None, d, tf), lambda i, f: (f // per_slot, 0, f % per_slot)),
                pl.BlockSpec((tf, d), lambda i, f: (f, 0))]
    out_specs = [rows, pl.BlockSpec((tm, tf), lambda i, f: (i, f)), rows]
    out_shape = [jax.ShapeDtypeStruct((s, d), F32), jax.ShapeDtypeStruct((s, ff), BF16),
                 jax.ShapeDtypeStruct((s, d), BF16)]
    args = [h, g, w_up, w_down]
    scratch = [pltpu.VMEM((tm, d), BF16), pltpu.VMEM((tm, d), F32)]
    if head is not None:
        in_specs += [_full(head[0]), ANY]
        args += list(head)
        out_specs += [pl.BlockSpec((1, 128), lambda i, f: (0, 0)),
                      pl.BlockSpec((1, d), lambda i, f: (0, 0))]
        out_shape += [jax.ShapeDtypeStruct((1, 128), F32), jax.ShapeDtypeStruct((1, d), F32)]
        scratch += [pltpu.VMEM((tm, d), F32), pltpu.SemaphoreType.DMA]
    return pl.pallas_call(
        body, name=name, grid=(s // tm, nf),
        in_specs=in_specs, out_specs=out_specs, out_shape=out_shape, scratch_shapes=scratch,
        compiler_params=_params(("parallel" if head is None else "arbitrary", "arbitrary")),
    )(*args)


def _window_sum_down(e, window):
    step = 1
    while step < window:
        e = e + pltpu.roll(e, step, axis=0)
        step *= 2
    return e


def _window_sum_up(e, window):
    n = e.shape[0]
    step = 1
    while step < window:
        e = e + pltpu.roll(e, n - step, axis=0)
        step *= 2
    return e


def _pool_counts(first_row, tm, window):
    t = first_row + _row((tm, 1))
    return jnp.minimum(t + 1, window).astype(F32)


def _pool_fwd(h, g, pw, ps):
    s, d = h.shape
    cg = d // len(POOL_WINDOWS)
    tm = min(TILE_ROWS, s)

    def body(h_ref, g_ref, pw_ref, ps_ref, out_ref, nbuf):
        i = pl.program_id(0)

        @pl.when(i == 0)
        def _():
            nbuf[0:POOL_HALO, :] = jnp.zeros((POOL_HALO, d), F32)

        n, _ = _rms_fwd(h_ref[...], g_ref[...])
        nbuf[POOL_HALO:POOL_HALO + tm, :] = n
        for k, window in enumerate(POOL_WINDOWS):
            cols = slice(k * cg, (k + 1) * cg)
            sums = _window_sum_down(nbuf[:, cols], window)[POOL_HALO:, :]
            pooled = sums / _pool_counts(i * tm, tm, window) - n[:, cols]
            y = _nn(pooled.astype(BF16), pw_ref[k]) * ps_ref[:, cols]
            out_ref[:, cols] = h_ref[:, cols] + y
        nbuf[0:POOL_HALO, :] = n[tm - POOL_HALO:tm, :]

    return pl.pallas_call(
        body, name="pool_fwd", grid=(s // tm,),
        in_specs=[pl.BlockSpec((tm, d), lambda i: (i, 0)), _full(g), _full(pw), _full(ps)],
        out_specs=pl.BlockSpec((tm, d), lambda i: (i, 0)),
        out_shape=jax.ShapeDtypeStruct((s, d), F32),
        scratch_shapes=[pltpu.VMEM((tm + POOL_HALO, d), F32)],
        compiler_params=_params(("arbitrary",)),
    )(h, g, pw, ps)


def _mlp_bwd_x(dz, a, w_up, w_down, h_in, g, name):
    s, d = dz.shape
    ff = w_down.shape[0]
    slot_cols = w_up.shape[2]
    tm = min(TILE_MLP_ROWS, s)
    tf = min(TILE_MLP_BWD_FF, slot_cols)
    per_slot = slot_cols // tf
    nf = ff // tf

    def body(dz_ref, a_ref, wu_ref, wd_ref, h_ref, g_ref, da_ref, dzb_ref, dh_ref, dg_ref,
             dzs_ref, acc_ref):
        i = pl.program_id(0)
        f = pl.program_id(1)

        @pl.when(f == 0)
        def _():
            dzb = dz_ref[...].astype(BF16)
            dzs_ref[...] = dzb
            dzb_ref[...] = dzb
            acc_ref[...] = jnp.zeros_like(acc_ref)

        dr = _nt(dzs_ref[...], wd_ref[...])
        da = (dr * (2.0 * jnp.maximum(a_ref[...].astype(F32), 0.0))).astype(BF16)
        da_ref[...] = da
        acc_ref[...] += _nt(da, wu_ref[...])

        @pl.when(f == nf - 1)
        def _():
            dx, dg = _rms_bwd(acc_ref[...], h_ref[...], g_ref[...])
            dh_ref[...] = dz_ref[...] + dx

            @pl.when(i == 0)
            def _():
                dg_ref[...] = dg

            @pl.when(i > 0)
            def _():
                dg_ref[...] += dg

    return pl.pallas_call(
        body, name=name, grid=(s // tm, nf),
        in_specs=[pl.BlockSpec((tm, d), lambda i, f: (i, 0)),
                  pl.BlockSpec((tm, tf), lambda i, f: (i, f)),
                  pl.BlockSpec((None, d, tf), lambda i, f: (f // per_slot, 0, f % per_slot)),
                  pl.BlockSpec((tf, d), lambda i, f: (f, 0)),
                  pl.BlockSpec((tm, d), lambda i, f: (i, 0)), _full(g)],
        out_specs=[pl.BlockSpec((tm, tf), lambda i, f: (i, f)),
                   pl.BlockSpec((tm, d), lambda i, f: (i, 0)),
                   pl.BlockSpec((tm, d), lambda i, f: (i, 0)),
                   pl.BlockSpec((1, d), lambda i, f: (0, 0))],
        out_shape=[jax.ShapeDtypeStruct((s, ff), BF16),
                   jax.ShapeDtypeStruct((s, d), BF16),
                   jax.ShapeDtypeStruct((s, d), F32),
                   jax.ShapeDtypeStruct((1, d), F32)],
        scratch_shapes=[pltpu.VMEM((tm, d), BF16), pltpu.VMEM((tm, d), F32)],
        compiler_params=_params(("arbitrary", "arbitrary")),
    )(dz, a, w_up, w_down, h_in, g)


def _mlp_bwd_w(n, da, a, dzb, slot_cols, name):
    s, d = n.shape
    ff = a.shape[1]
    tn = min(TILE_WGRAD_N, slot_cols)
    tk = min(TILE_WGRAD_K, s)
    per_slot = slot_cols // tn
    nk = s // tk

    def body(n_ref, da_ref, a_ref, dz_ref, du_ref, dd_ref, accu_ref, accd_ref):
        k = pl.program_id(1)

        @pl.when(k == 0)
        def _():
            accu_ref[...] = jnp.zeros_like(accu_ref)
            accd_ref[...] = jnp.zeros_like(accd_ref)

        accu_ref[...] += _tn(n_ref[...], da_ref[...])
        r = jnp.square(jnp.maximum(a_ref[...].astype(F32), 0.0)).astype(BF16)
        accd_ref[...] += _tn(r, dz_ref[...])

        @pl.when(k == nk - 1)
        def _():
            du_ref[...] = accu_ref[...].astype(BF16)
            dd_ref[...] = accd_ref[...].astype(BF16)

    return pl.pallas_call(
        body, name=name, grid=(ff // tn, nk),
        in_specs=[pl.BlockSpec((tk, d), lambda f, k: (k, 0)),
                  pl.BlockSpec((tk, tn), lambda f, k: (k, f)),
                  pl.BlockSpec((tk, tn), lambda f, k: (k, f)),
                  pl.BlockSpec((tk, d), lambda f, k: (k, 0))],
        out_specs=[pl.BlockSpec((None, d, tn), lambda f, k: (f // per_slot, 0, f % per_slot)),
                   pl.BlockSpec((tn, d), lambda f, k: (f, 0))],
        out_shape=[jax.ShapeDtypeStruct((ff // slot_cols, d, slot_cols), BF16),
                   jax.ShapeDtypeStruct((ff, d), BF16)],
        scratch_shapes=[pltpu.VMEM((d, tn), F32), pltpu.VMEM((tn, d), F32)],
        compiler_params=_params(("parallel", "arbitrary")),
    )(n, da, a, dzb)


def _pool_bwd(after, dh, h, g, pw, ps):
    s, d = h.shape
    cg = d // len(POOL_WINDOWS)
    tm = min(TILE_ROWS, s)
    nb = s // tm
    halo_per_tile = tm // POOL_HALO

    def body(after_ref, dh_ref, h_ref, halo_ref, g_ref, pw_ref, ps_ref,
             dx_ref, dpw_ref, dps_ref, dg_ref, nbuf, qbuf, dn_ref, carry, dpw_acc):
        i = pl.program_id(0)
        blk = nb - 1 - i

        @pl.when(i == 0)
        def _():
            carry[...] = jnp.zeros_like(carry)
            dpw_acc[...] = jnp.zeros_like(dpw_acc)
            dps_ref[...] = jnp.zeros_like(dps_ref)
            dg_ref[...] = jnp.zeros_like(dg_ref)

        hv = h_ref[...]
        n, _ = _rms_fwd(hv, g_ref[...])
        nh, _ = _rms_fwd(halo_ref[...], g_ref[...])
        nbuf[0:POOL_HALO, :] = jnp.where(blk == 0, 0.0, nh)
        nbuf[POOL_HALO:POOL_HALO + tm, :] = n
        dhv = dh_ref[...]
        for k, window in enumerate(POOL_WINDOWS):
            cols = slice(k * cg, (k + 1) * cg)
            cnt = _pool_counts(blk * tm, tm, window)
            sums = _window_sum_down(nbuf[:, cols], window)[POOL_HALO:, :]
            pb = (sums / cnt - n[:, cols]).astype(BF16)
            dyk = dhv[:, cols]
            dps_ref[:, cols] += jnp.sum(dyk * _nn(pb, pw_ref[k]), axis=0, keepdims=True)
            dyb = (dyk * ps_ref[:, cols]).astype(BF16)
            dpw_acc[k] += _tn(pb, dyb)
            dpool = _nt(dyb, pw_ref[k])
            qv = dpool / cnt
            qbuf[0:tm, cols] = qv
            qbuf[tm:tm + POOL_HALO, cols] = carry[:, cols]
            dn_ref[:, cols] = _window_sum_up(qbuf[:, cols], window)[0:tm, :] - dpool
            carry[:, cols] = qv[0:POOL_HALO, :]
        dx, dg = _rms_bwd(dn_ref[...], hv, g_ref[...])
        dx_ref[...] = dhv + dx
        dg_ref[...] += dg

        @pl.when(i == nb - 1)
        def _():
            dpw_ref[...] = dpw_acc[...].astype(BF16)

    rev = lambda i: (nb - 1 - i, 0)
    return pl.pallas_call(
        body, name="pool_bwd", grid=(nb,),
        in_specs=[ANY, pl.BlockSpec((tm, d), rev), pl.BlockSpec((tm, d), rev),
                  pl.BlockSpec((POOL_HALO, d),
                               lambda i: (jnp.maximum((nb - 1 - i) * halo_per_tile - 1, 0), 0)),
                  _full(g), _full(pw), _full(ps)],
        out_specs=[pl.BlockSpec((tm, d), rev), _full(pw),
                   pl.BlockSpec((1, d), lambda i: (0, 0)),
                   pl.BlockSpec((1, d), lambda i: (0, 0))],
        out_shape=[jax.ShapeDtypeStruct((s, d), F32),
                   jax.ShapeDtypeStruct(pw.shape, BF16),
                   jax.ShapeDtypeStruct((1, d), F32),
                   jax.ShapeDtypeStruct((1, d), F32)],
        scratch_shapes=[pltpu.VMEM((tm + POOL_HALO, d), F32), pltpu.VMEM((tm + POOL_HALO, d), F32),
                        pltpu.VMEM((tm, d), F32), pltpu.VMEM((POOL_HALO, d), F32),
                        pltpu.VMEM(pw.shape, F32)],
        compiler_params=_params(("arbitrary",)),
    )(after, dh, h, h, g, pw, ps)


def _conv_out_bwd(after, dh, w_out, o, bcx, cw):
    s, d = dh.shape
    c = o.shape[1]
    tm = min(TILE_ROWS, s)
    nb = s // tm
    halo_per_tile = tm // CONV_HALO

    def body(after_ref, dh_ref, w_ref, o_ref, b_ref, c_ref, xin_ref, ch_ref, xh_ref, cw_ref,
             do_ref, delta_ref, dbcx_ref, dw_ref, dcw_ref, ubuf, dbuf, carry, acc):
        i = pl.program_id(0)
        blk = nb - 1 - i

        @pl.when(i == 0)
        def _():
            carry[...] = jnp.zeros_like(carry)
            acc[...] = jnp.zeros_like(acc)
            dcw_ref[...] = jnp.zeros_like(dcw_ref)

        dm = dh_ref[...].astype(BF16)
        dcat = _nt(dm, w_ref[...])
        do = dcat[:, 0:c]
        dy = dcat[:, c:2 * c]
        do_ref[...] = do.astype(BF16)
        head_of_lane = lax.shift_right_logical(_lane((8, c)), HEAD_DIM.bit_length() - 1)
        heads = (head_of_lane == _row((8, c))).astype(BF16)
        delta_ref[...] = _exact_nt(heads, do * o_ref[...].astype(F32))

        cv_ = c_ref[...]
        xin = xin_ref[...]
        bv = b_ref[...]
        u = cv_ * xin
        ubuf[0:CONV_HALO, :] = jnp.where(blk == 0, 0.0, ch_ref[...] * xh_ref[...])
        ubuf[CONV_HALO:CONV_HALO + tm, :] = u
        u1 = ubuf[CONV_HALO - 1:CONV_HALO - 1 + tm, :]
        u2 = ubuf[CONV_HALO - 2:CONV_HALO - 2 + tm, :]
        w0, w1, w2 = cw_ref[0:1, :], cw_ref[1:2, :], cw_ref[2:3, :]
        cv = (w0 * u2 + w1 * u1) + w2 * u
        acc[0:c, :] += _tn(o_ref[...], dm)
        acc[c:2 * c, :] += _tn((bv * cv).astype(BF16), dm)

        dcv = dy * bv
        dcw_ref[0:1, :] += jnp.sum(dcv * u2, axis=0, keepdims=True)
        dcw_ref[1:2, :] += jnp.sum(dcv * u1, axis=0, keepdims=True)
        dcw_ref[2:3, :] += jnp.sum(dcv * u, axis=0, keepdims=True)
        dbuf[0:tm, :] = dcv
        dbuf[tm:tm + CONV_HALO, :] = carry[...]
        du = w2 * dcv + w1 * dbuf[1:1 + tm, :] + w0 * dbuf[2:2 + tm, :]
        dbcx_ref[:, 0:c] = (dy * cv).astype(BF16)
        dbcx_ref[:, c:2 * c] = (du * xin).astype(BF16)
        dbcx_ref[:, 2 * c:3 * c] = (du * cv_).astype(BF16)
        carry[...] = dcv[0:CONV_HALO, :]

        @pl.when(i == nb - 1)
        def _():
            dw_ref[...] = acc[...].astype(BF16)

    rev = lambda k: (lambda i: (nb - 1 - i, k))
    halo = lambda k: (lambda i: (jnp.maximum((nb - 1 - i) * halo_per_tile - 1, 0), k))
    return pl.pallas_call(
        body, name="conv_out_bwd", grid=(nb,),
        in_specs=[ANY, pl.BlockSpec((tm, d), rev(0)), _full(w_out), pl.BlockSpec((tm, c), rev(0)),
                  pl.BlockSpec((tm, c), rev(0)), pl.BlockSpec((tm, c), rev(1)),
                  pl.BlockSpec((tm, c), rev(2)),
                  pl.BlockSpec((CONV_HALO, c), halo(1)), pl.BlockSpec((CONV_HALO, c), halo(2)),
                  _full(cw)],
        out_specs=[pl.BlockSpec((tm, c), rev(0)),
                   pl.BlockSpec((8, tm), lambda i: (0, nb - 1 - i)),
                   pl.BlockSpec((tm, 3 * c), rev(0)),
                   _full(w_out), _full(cw)],
        out_shape=[jax.ShapeDtypeStruct((s, c), BF16),
                   jax.ShapeDtypeStruct((8, s), F32),
                   jax.ShapeDtypeStruct((s, 3 * c), BF16),
                   jax.ShapeDtypeStruct(w_out.shape, BF16),
                   jax.ShapeDtypeStruct(cw.shape, F32)],
        scratch_shapes=[pltpu.VMEM((tm + CONV_HALO, c), F32), pltpu.VMEM((tm + CONV_HALO, c), F32),
                        pltpu.VMEM((CONV_HALO, c), F32), pltpu.VMEM(w_out.shape, F32)],
        compiler_params=_params(("arbitrary",)),
    )(after, dh, w_out, o, bcx, bcx, bcx, bcx, bcx, cw)


def _attn_bwd(after, qa, ka, qkv, do, lse, delta):
    s = qa.shape[1]
    a = N_HEADS * HEAD_DIM
    t = min(TILE_ATTN, s)
    nq = s // t
    n_pairs = N_HEADS // 2
    v_block0 = 2 * a // 128

    def body(after_ref, ka_ref, v_ref, qa_ref, do_ref, lse_ref, delta_ref,
             dqt_ref, dka_ref, dv_ref, dv_acc):
        g = pl.program_id(0)
        j = pl.program_id(1)

        @pl.when(j == 0)
        def _():
            dqt_ref[...] = jnp.zeros_like(dqt_ref)

        lane = _lane((t, 128))
        vf = v_ref[...].astype(F32)
        v_heads = [jnp.where(lane < HEAD_DIM, vf, 0.0).astype(BF16),
                   jnp.where(lane >= HEAD_DIM, vf, 0.0).astype(BF16)]
        ke_t = [ka_ref[e].astype(F32).T.astype(BF16) for e in range(2)]

        def q_step(i, first):
            qs = pl.ds(pl.multiple_of(i * t, t), t)
            dob = do_ref[qs, :]
            for e in range(2):
                qe = qa_ref[e, qs, :]
                sc = _nt(ka_ref[e], qe)
                if first:
                    sc = jnp.where(_row((t, t)) <= _lane((t, t)), sc, NEG_BIG)
                p = jnp.exp2(sc - lse_ref[pl.ds(e, 1), qs])
                dv_part = _nn(p.astype(BF16), dob)
                dp = _nt(v_heads[e], dob)
                ds = (p * (dp - delta_ref[pl.ds(2 * g + e, 1), qs])).astype(BF16)
                dk_part = _nn(ds, qe)
                if first:
                    dv_acc[e] = dv_part
                    dka_ref[e] = dk_part
                else:
                    dv_acc[e] += dv_part
                    dka_ref[e] += dk_part
                dqt_ref[e, :, qs] += _nn(ke_t[e], ds)

        q_step(j, True)

        def full_step(i, carry):
            q_step(i, False)
            return carry

        lax.fori_loop(j + 1, nq, full_step, 0)
        dv_ref[...] = jnp.where(lane < HEAD_DIM, dv_acc[0], dv_acc[1]).astype(BF16)

    return pl.pallas_call(
        body, name="attn_bwd", grid=(n_pairs, nq),
        in_specs=[ANY, pl.BlockSpec((2, t, 128), lambda g, j: (g, j, 0)),
                  pl.BlockSpec((t, 128), lambda g, j: (j, v_block0 + g)),
                  pl.BlockSpec((2, s, 128), lambda g, j: (g, 0, 0)),
                  pl.BlockSpec((s, 128), lambda g, j: (0, g)),
                  pl.BlockSpec((None, 8, s), lambda g, j: (g, 0, 0)),
                  pl.BlockSpec((8, s), lambda g, j: (0, 0))],
        out_specs=[pl.BlockSpec((2, 128, s), lambda g, j: (g, 0, 0)),
                   pl.BlockSpec((2, t, 128), lambda g, j: (g, j, 0)),
                   pl.BlockSpec((t, 128), lambda g, j: (j, g))],
        out_shape=[jax.ShapeDtypeStruct((N_HEADS, 128, s), F32),
                   jax.ShapeDtypeStruct((N_HEADS, s, 128), F32),
                   jax.ShapeDtypeStruct((s, a), BF16)],
        scratch_shapes=[pltpu.VMEM((2, t, 128), F32)],
        compiler_params=_params(("parallel", "arbitrary")),
    )(after, ka, qkv, qa, do, lse, delta)


def _gate_bwd(dqa, dka, dv, fl, bf):
    s = fl.shape[0]
    a = N_HEADS * HEAD_DIM
    tm = min(TILE_ROWS, s)
    nb = s // tm

    def body(dqa_ref, dka_ref, dv_ref, fl_ref, bf_ref, dqkv_ref, dfl_ref, dbf_ref, carry):
        i = pl.program_id(0)

        @pl.when(i == 0)
        def _():
            carry[...] = jnp.zeros_like(carry)
            dbf_ref[...] = jnp.zeros_like(dbf_ref)

        lane = _lane((tm, 128))
        dq_sum = jnp.zeros((tm, 128), F32)
        dk_sum = jnp.zeros((tm, 128), F32)
        for pair in range(N_HEADS // 2):
            qs, ks = [], []
            for e in range(2):
                h = 2 * pair + e
                dq = dqa_ref[h].T
                dk = dka_ref[h]
                dq_sum = dq_sum + dq
                dk_sum = dk_sum + dk
                qs.append(dq * ATTN_SCALE)
                ks.append(dk * (1.0 / LOG2_E))
            cols = slice(pair * 128, (pair + 1) * 128)
            dqkv_ref[:, cols] = jnp.where(
                lane < HEAD_DIM, qs[0], pltpu.roll(qs[1], HEAD_DIM, axis=1)).astype(BF16)
            dqkv_ref[:, a + pair * 128:a + (pair + 1) * 128] = jnp.where(
                lane < HEAD_DIM, ks[0], pltpu.roll(ks[1], HEAD_DIM, axis=1)).astype(BF16)
        dqkv_ref[:, 2 * a:3 * a] = dv_ref[...]

        in_q = (lane >= LANE_CQ) & (lane < LANE_CQ + N_HEADS)
        in_k = (lane >= LANE_CK) & (lane < LANE_CK + N_HEADS)
        dcum = (pltpu.roll(jnp.where(in_q, dq_sum, 0.0), 128 - LANE_CQ, axis=1)
                - pltpu.roll(jnp.where(in_k, dk_sum, 0.0), 128 - LANE_CK, axis=1))

        upper = (_lane((tm, tm)) >= _row((tm, tm))).astype(BF16)
        dlogf = _exact_nn(upper, dcum) + carry[0:1, :]
        carry[0:1, :] = dlogf[0:1, :]
        z = fl_ref[...] + bf_ref[...]
        ez = jnp.exp(-jnp.abs(z))
        sig_neg = jnp.where(z >= 0.0, ez, 1.0) / (1.0 + ez)
        dz = jnp.where(lane < N_HEADS, dlogf * sig_neg, 0.0)
        dfl_ref[...] = dz.astype(BF16)
        dbf_ref[...] += jnp.sum(dz, axis=0, keepdims=True)

    rev3 = lambda i: (0, nb - 1 - i, 0)
    rev = lambda i: (nb - 1 - i, 0)
    return pl.pallas_call(
        body, name="gate_bwd", grid=(nb,),
        in_specs=[pl.BlockSpec((N_HEADS, 128, tm), lambda i: (0, 0, nb - 1 - i)),
                  pl.BlockSpec((N_HEADS, tm, 128), rev3),
                  pl.BlockSpec((tm, a), rev), pl.BlockSpec((tm, 128), rev), _full(bf)],
        out_specs=[pl.BlockSpec((tm, 3 * a), rev), pl.BlockSpec((tm, 128), rev),
                   pl.BlockSpec((1, 128), lambda i: (0, 0))],
        out_shape=[jax.ShapeDtypeStruct((s, 3 * a), BF16),
                   jax.ShapeDtypeStruct((s, 128), BF16),
                   jax.ShapeDtypeStruct((1, 128), F32)],
        scratch_shapes=[pltpu.VMEM((8, 128), F32)],
        compiler_params=_params(("arbitrary",)),
    )(dqa, dka, dv, fl, bf)


def _in_proj_bwd(after, dqkv, dfl, dbcx, w_qkv, w_f, w_bcx, x, g, dh):
    s, d = x.shape
    tm = min(TILE_ROWS, s)

    def body(after_ref, dq_ref, df_ref, db_ref, wq_ref, wf_ref, wb_ref, x_ref, g_ref, dh_ref,
             gx_ref, dg_ref):
        i = pl.program_id(0)
        dn = (_nt(dq_ref[...], wq_ref[...]) + _nt(df_ref[...], wf_ref[...])
              + _nt(db_ref[...], wb_ref[...]))
        dx, dg = _rms_bwd(dn, x_ref[...], g_ref[...])
        gx_ref[...] = dh_ref[...] + dx

        @pl.when(i == 0)
        def _():
            dg_ref[...] = dg

        @pl.when(i > 0)
        def _():
            dg_ref[...] += dg

    rows = lambda c: pl.BlockSpec((tm, c), lambda i: (i, 0))
    return pl.pallas_call(
        body, name="in_proj_bwd", grid=(s // tm,),
        in_specs=[ANY, rows(dqkv.shape[1]), rows(dfl.shape[1]), rows(dbcx.shape[1]),
                  _full(w_qkv), _full(w_f), _full(w_bcx), rows(d), _full(g), rows(d)],
        out_specs=[rows(d), pl.BlockSpec((1, d), lambda i: (0, 0))],
        out_shape=[jax.ShapeDtypeStruct((s, d), F32), jax.ShapeDtypeStruct((1, d), F32)],
        compiler_params=_params(("arbitrary",)),
    )(after, dqkv, dfl, dbcx, w_qkv, w_f, w_bcx, x, g, dh)


def _wgrad_in(n, dys):
    s, d = n.shape
    m = len(dys)
    tk = min(TILE_ROWS, s)
    nk = s // tk

    def body(*refs):
        n_ref, dy_refs, dw_refs, accs = refs[0], refs[1:1 + m], refs[1 + m:1 + 2 * m], refs[1 + 2 * m:]
        k = pl.program_id(0)

        @pl.when(k == 0)
        def _():
            for acc in accs:
                acc[...] = jnp.zeros_like(acc)

        nb = n_ref[...]
        for dy_ref, acc in zip(dy_refs, accs):
            acc[...] += _tn(nb, dy_ref[...])

        @pl.when(k == nk - 1)
        def _():
            for dw_ref, acc in zip(dw_refs, accs):
                dw_ref[...] = acc[...].T.astype(BF16)

    return pl.pallas_call(
        body, name="wgrad_in", grid=(nk,),
        in_specs=[pl.BlockSpec((tk, d), lambda k: (k, 0))]
        + [pl.BlockSpec((tk, dy.shape[1]), lambda k: (k, 0)) for dy in dys],
        out_specs=[pl.BlockSpec((dy.shape[1], d), lambda k: (0, 0)) for dy in dys],
        out_shape=[jax.ShapeDtypeStruct((dy.shape[1], d), BF16) for dy in dys],
        scratch_shapes=[pltpu.VMEM((d, dy.shape[1]), F32) for dy in dys],
        compiler_params=_params(("arbitrary",)),
    )(n, *dys)


def _row_tile(rows):
    t = min(TILE_ELEM_ROWS, rows)
    while rows % t:
        t //= 2
    return t


def _adamw_math(w, g, m, v):
    m = ADAM_B1 * m + (1.0 - ADAM_B1) * g
    v = ADAM_B2 * v + (1.0 - ADAM_B2) * jnp.square(g)
    m_hat = m / (1.0 - ADAM_B1 ** ADAM_STEP)
    v_hat = v / (1.0 - ADAM_B2 ** ADAM_STEP)
    delta = -ADAM_LR * (m_hat / (jnp.sqrt(v_hat) + ADAM_EPS) + ADAM_WD * w)
    return delta, m, v


def _adamw(w, g, m, v, name):
    rows, cols = w.shape

    def body(w_ref, g_ref, m_ref, v_ref, d_ref, nm_ref, nv_ref):
        delta, nm, nv = _adamw_math(w_ref[...], g_ref[...], m_ref[...], v_ref[...])
        d_ref[...] = delta
        nm_ref[...] = nm
        nv_ref[...] = nv

    if rows % 8 == 0:
        tr = _row_tile(rows)
        grid, spec = (rows // tr,), pl.BlockSpec((tr, cols), lambda i: (i, 0))
    else:
        grid, spec = (cols // 256,), pl.BlockSpec((rows, 256), lambda i: (0, i))
    out = jax.ShapeDtypeStruct(w.shape, F32)
    return pl.pallas_call(
        body, name=name, grid=grid, in_specs=[spec] * 4, out_specs=[spec] * 3,
        out_shape=[out, out, out], compiler_params=_params(("parallel",)),
    )(w, g, m, v)


def _sum_devices(parts):
    def body(p_ref, g_ref):
        g = p_ref[0]
        for k in range(1, N_DEV):
            g = g + p_ref[k]
        g_ref[...] = g

    return pl.pallas_call(
        body, name="sum_devices",
        in_specs=[pl.BlockSpec(memory_space=pltpu.VMEM)],
        out_specs=pl.BlockSpec(memory_space=pltpu.VMEM),
        out_shape=jax.ShapeDtypeStruct(parts.shape[1:], F32),
    )(parts)


def _mesh_position():
    x, y, c = lax.axis_index("x"), lax.axis_index("y"), lax.axis_index("c")
    chips = [(1 - x, y), (x, 1 - y), (1 - x, 1 - y)]
    return x, y, c, chips


ANY = pl.BlockSpec(memory_space=pl.ANY)
HBM = pl.BlockSpec(memory_space=pltpu.HBM)
SEM = pl.BlockSpec(memory_space=pltpu.SEMAPHORE)
SPLIT_COPY_EFFECT = pltpu.SideEffectType.DATAFLOW_SIDE_EFFECTING


def _in_hbm(a):
    return pltpu.with_memory_space_constraint(a, pltpu.HBM)


def _chip_copies(views, srcs, lands, send, recv, waiting=False):
    _, _, c, chips = _mesh_position()
    cps = []
    for a in range(len(srcs)):
        for k, (px, py) in enumerate(chips):
            src, dst = views(a, k, srcs[a], lands[a], c, 2 * px + py)
            sem = a * (N_CHIPS - 1) + k
            cps.append(pltpu.make_async_remote_copy(
                src_ref=src, dst_ref=dst, send_sem=send.at[sem], recv_sem=recv.at[sem],
                device_id=(px, py, c), device_id_type=MESH))
    return cps


def _ici_start(sources, land_shapes, copies, after, name, per_array=N_CHIPS - 1):
    n = len(sources)

    def body(*refs):
        srcs, lands = refs[:n], refs[n:2 * n]
        send, recv = refs[2 * n + 1], refs[2 * n + 2]
        token = refs[-1]
        for cp in copies(srcs, lands, send, recv, False):
            cp.start()
        token[...] = jnp.zeros_like(token)

    lands = [_in_hbm(lax.empty(s.shape, s.dtype)) for s in land_shapes]
    outs = pl.pallas_call(
        body, name=name,
        in_specs=[HBM] * (2 * n) + [ANY],
        out_specs=[SEM, SEM] + [HBM] * (2 * n) + [pl.BlockSpec(memory_space=pltpu.VMEM)],
        out_shape=[pltpu.SemaphoreType.DMA((n * per_array,))] * 2
        + [pltpu.HBM(a.shape, a.dtype) for a in sources]
        + [pltpu.HBM(s.shape, s.dtype) for s in land_shapes]
        + [jax.ShapeDtypeStruct((8, 128), F32)],
        input_output_aliases={i: 2 + i for i in range(2 * n)},
        compiler_params=pltpu.CompilerParams(has_side_effects=SPLIT_COPY_EFFECT),
    )(*[_in_hbm(a) for a in sources], *lands, after)
    return outs[0], outs[1], list(outs[2:2 + n]), list(outs[2 + n:2 + 2 * n]), outs[-1]


def _ici_wait(handle, copies, after, name):
    send, recv, srcs, lands, _ = handle
    n = len(srcs)

    def body(*refs):
        src_refs, land_refs = refs[:n], refs[n:2 * n]
        for cp in copies(src_refs, land_refs, refs[2 * n], refs[2 * n + 1], True):
            cp.wait_send()
            cp.wait_recv()

    outs = pl.pallas_call(
        body, name=name,
        in_specs=[HBM] * (2 * n) + [SEM, SEM, ANY],
        out_specs=[HBM] * (2 * n),
        out_shape=[pltpu.HBM(a.shape, a.dtype) for a in srcs]
        + [pltpu.HBM(a.shape, a.dtype) for a in lands],
        input_output_aliases={i: i for i in range(2 * n)},
        compiler_params=pltpu.CompilerParams(has_side_effects=SPLIT_COPY_EFFECT),
    )(*srcs, *lands, send, recv, after)
    return list(outs[:n]), list(outs[n:])


def _gather_views(split):
    def views(a, k, src, land, c, slot):
        if split[a]:
            half = src.shape[0] // 2
            src = src.at[pl.ds(c * half, half)]
        return src, land.at[k]
    return views


def _gather_whole_views(a, k, src, land, c, slot):
    x, y, _, _ = _mesh_position()
    return src, land.at[2 * x + y]


SCATTER_COPIES = 2 * (N_CHIPS - 1)


def _scatter_copies(srcs, lands, send, recv, waiting):
    _, _, c, chips = _mesh_position()
    cps = []
    for a in range(len(srcs)):
        half = srcs[a].shape[1] // 2
        for k, (px, py) in enumerate(chips):
            for h in range(2):
                arrival = 2 * k + (h if waiting else c)
                cps.append(pltpu.make_async_remote_copy(
                    src_ref=srcs[a].at[2 * px + py, pl.ds(h * half, half)],
                    dst_ref=lands[a].at[arrival],
                    send_sem=send.at[a * SCATTER_COPIES + 2 * k + h],
                    recv_sem=recv.at[a * SCATTER_COPIES + arrival],
                    device_id=(px, py, h), device_id_type=MESH))
    return cps


def _gather_land_shapes(shards, split):
    return [jax.ShapeDtypeStruct(
        (N_CHIPS - 1, a.shape[0] // 2 if sp else a.shape[0]) + a.shape[1:], a.dtype)
        for a, sp in zip(shards, split)]


def _gather_finish(shards, lands, split, name):
    n = len(shards)
    ns = sum(split)
    d_index = {a: i for i, a in enumerate(a for a in range(n) if split[a])}

    def body(*refs):
        shard, land, outs = refs[:n], refs[n:2 * n], refs[2 * n:3 * n]
        obuf, fbuf = refs[3 * n:4 * n], refs[4 * n:5 * n]
        dbuf = refs[5 * n:5 * n + ns]
        ld_own, st_own, ld, st_mine, st_sib, send, recv = refs[5 * n + ns:]
        x, y, c, chips = _mesh_position()
        me = 2 * x + y
        own_loads, loads, sends, pending = [], {}, [], []
        for a in range(n):
            cp = pltpu.make_async_copy(shard[a], obuf[a], ld_own.at[a])
            cp.start()
            own_loads.append(cp)
        for a in range(n):
            for k in range(N_CHIPS - 1):
                cp = pltpu.make_async_copy(land[a].at[k], fbuf[a].at[k], ld.at[a, k])
                cp.start()
                loads[a, k] = cp
        for a in range(n):
            own_loads[a].wait()
            cp = pltpu.make_async_copy(obuf[a], outs[a].at[me], st_own.at[a])
            cp.start()
            pending.append(cp)
        for a in range(n):
            rows = shard[a].shape[0]
            for k, (px, py) in enumerate(chips):
                loads[a, k].wait()
                part = pl.ds(c * (rows // 2), rows // 2) if split[a] else pl.ds(0, rows)
                cp = pltpu.make_async_copy(fbuf[a].at[k], outs[a].at[2 * px + py, part],
                                           st_mine.at[a, k])
                cp.start()
                pending.append(cp)
                if split[a]:
                    fw = pltpu.make_async_remote_copy(
                        src_ref=fbuf[a].at[k], dst_ref=dbuf[d_index[a]].at[k],
                        send_sem=send.at[a, k], recv_sem=recv.at[a, k],
                        device_id=(x, y, 1 - c), device_id_type=MESH)
                    fw.start()
                    sends.append((a, k, fw))
        for a, k, fw in sends:
            px, py = chips[k]
            half = shard[a].shape[0] // 2
            fw.wait_recv()
            cp = pltpu.make_async_copy(dbuf[d_index[a]].at[k],
                                       outs[a].at[2 * px + py, pl.ds((1 - c) * half, half)],
                                       st_sib.at[a, k])
            cp.start()
            pending.append(cp)
        for _, _, fw in sends:
            fw.wait_send()
        for cp in pending:
            cp.wait()

    stage = [pltpu.VMEM(a.shape, a.dtype) for a in lands]
    dma = lambda *shape: pltpu.SemaphoreType.DMA(shape)
    return pl.pallas_call(
        body, name=name,
        in_specs=[ANY] * (2 * n), out_specs=[ANY] * n,
        out_shape=[jax.ShapeDtypeStruct((N_CHIPS,) + a.shape, a.dtype) for a in shards],
        scratch_shapes=[pltpu.VMEM(a.shape, a.dtype) for a in shards] + stage
        + [s for s, sp in zip(stage, split) if sp]
        + [dma(n), dma(n), dma(n, 3), dma(n, 3), dma(n, 3), dma(n, 3), dma(n, 3)],
        compiler_params=pltpu.CompilerParams(vmem_limit_bytes=VMEM_LIMIT_BYTES),
    )(*shards, *lands)


def _sum_chunk(rows):
    return next(r for r in range(SUM_CHUNK_ROWS, 0, -16) if rows % r == 0)


def _sum_and_share(partials, lands, name):
    n = len(partials)

    def body(*refs):
        own, landed, outs = refs[:n], refs[n:2 * n], refs[2 * n:3 * n]
        obuf, xbuf, ybuf, gbuf, sbuf, rbuf = (refs[(3 + k) * n:(4 + k) * n] for k in range(6))
        ld_own, ld_send, ld_got, st_own, st_sib, send_p, recv_p, send_s, recv_s = refs[9 * n:]
        x, y, c, _ = _mesh_position()
        me = 2 * x + y
        sibling = (x, y, 1 - c)

        def to_sibling(src, dst, send, recv, a):
            return pltpu.make_async_remote_copy(src_ref=src, dst_ref=dst, send_sem=send.at[a],
                                                recv_sem=recv.at[a], device_id=sibling,
                                                device_id_type=MESH)

        loads, firsts, seconds, stores = [], [], [], []
        for a in range(n):
            half = obuf[a].shape[0]
            cps = [pltpu.make_async_copy(own[a].at[me, pl.ds((1 - c) * half, half)], xbuf[a],
                                         ld_send.at[a]),
                   pltpu.make_async_copy(own[a].at[me, pl.ds(c * half, half)], obuf[a], ld_own.at[a]),
                   pltpu.make_async_copy(landed[a], gbuf[a], ld_got.at[a])]
            for cp in cps:
                cp.start()
            loads.append(cps)
        for a in range(n):
            loads[a][0].wait()
            rc = to_sibling(xbuf[a], ybuf[a], send_p, recv_p, a)
            rc.start()
            firsts.append(rc)
        for a in range(n):
            firsts[a].wait_recv()
            loads[a][1].wait()
            loads[a][2].wait()
            half = obuf[a].shape[0]
            rows = _sum_chunk(half)

            def add(k, carry, a=a, rows=rows):
                at = pl.ds(pl.multiple_of(k * rows, rows), rows)
                acc = obuf[a][at].astype(F32) + ybuf[a][at].astype(F32)
                for j in range(SCATTER_COPIES):
                    acc = acc + gbuf[a][j, at].astype(F32)
                sbuf[a][at] = acc
                return carry

            lax.fori_loop(0, half // rows, add, 0)
            rc = to_sibling(sbuf[a], rbuf[a], send_s, recv_s, a)
            rc.start()
            seconds.append(rc)
            cp = pltpu.make_async_copy(sbuf[a], outs[a].at[pl.ds(c * half, half)], st_own.at[a])
            cp.start()
            stores.append(cp)
        for a in range(n):
            half = obuf[a].shape[0]
            seconds[a].wait_recv()
            cp = pltpu.make_async_copy(rbuf[a], outs[a].at[pl.ds((1 - c) * half, half)], st_sib.at[a])
            cp.start()
            stores.append(cp)
        for rc in firsts + seconds:
            rc.wait_send()
        for cp in stores:
            cp.wait()

    halves = [(a.shape[1] // 2, a.shape[2]) for a in partials]
    return pl.pallas_call(
        body, name=name,
        in_specs=[ANY] * (2 * n), out_specs=[ANY] * n,
        out_shape=[jax.ShapeDtypeStruct((2 * h[0], h[1]), F32) for h in halves],
        scratch_shapes=[pltpu.VMEM(h, BF16) for h in halves] * 3
        + [pltpu.VMEM(g.shape, BF16) for g in lands]
        + [pltpu.VMEM(h, F32) for h in halves] * 2
        + [pltpu.SemaphoreType.DMA((n,))] * 9,
        compiler_params=pltpu.CompilerParams(vmem_limit_bytes=VMEM_LIMIT_BYTES),
    )(*partials, *lands)


def _gather_small(part):
    def body(in_ref, out_ref, send, recv, local):
        x, y, c, _ = _mesh_position()
        me = 4 * x + 2 * y + c
        cps = [pltpu.make_async_copy(in_ref, out_ref.at[me], local)]
        k = 0
        for fx in range(2):
            for fy in range(2):
                for fc in range(2):
                    if fx or fy or fc:
                        cps.append(pltpu.make_async_remote_copy(
                            src_ref=in_ref, dst_ref=out_ref.at[me], send_sem=send.at[k],
                            recv_sem=recv.at[k], device_id=(x ^ fx, y ^ fy, c ^ fc),
                            device_id_type=MESH))
                        k += 1
        for cp in cps:
            cp.start()
        for cp in cps:
            cp.wait()

    return pl.pallas_call(
        body, name="gather_small",
        in_specs=[pl.BlockSpec(memory_space=pltpu.VMEM)],
        out_specs=pl.BlockSpec(memory_space=pltpu.VMEM),
        out_shape=jax.ShapeDtypeStruct((N_DEV,) + part.shape, part.dtype),
        scratch_shapes=[pltpu.SemaphoreType.DMA((N_DEV - 1,)), pltpu.SemaphoreType.DMA((N_DEV - 1,)),
                        pltpu.SemaphoreType.DMA],
    )(part)


def _scatter_start(grads, after, tag):
    lands = [jax.ShapeDtypeStruct((SCATTER_COPIES, g.shape[1] // 2, g.shape[2]), g.dtype)
             for g in grads]
    return _ici_start(grads, lands, _scatter_copies, after, "scatter_start_" + tag,
                      per_array=SCATTER_COPIES)


def _scatter_finish(handle, after, tag):
    grads, lands = _ici_wait(handle, _scatter_copies, after, "scatter_wait_" + tag)
    return _sum_and_share(grads, lands, "sum_and_share_" + tag)


def _pad_rows(a, rows):
    return jnp.pad(a, ((0, rows - a.shape[0]), (0, 0)))


def kernel(x, norm_mix_0, w_in_0, b_f_0, conv_w_0, w_out_0, norm_ffn_0, w_up_0, w_down_0, norm_mix_1, pool_w_1, pool_scale_1, norm_ffn_1, w_up_1, w_down_1, final_norm, loss_target, m_norm_mix_0, m_w_in_0, m_b_f_0, m_conv_w_0, m_w_out_0, m_norm_ffn_0, m_w_up_0, m_w_down_0, m_norm_mix_1, m_pool_w_1, m_pool_scale_1, m_norm_ffn_1, m_w_up_1, m_w_down_1, m_final_norm, v_norm_mix_0, v_w_in_0, v_b_f_0, v_conv_w_0, v_w_out_0, v_norm_ffn_0, v_w_up_0, v_w_down_0, v_norm_mix_1, v_pool_w_1, v_pool_scale_1, v_norm_ffn_1, v_w_up_1, v_w_down_1, v_final_norm):
    d = x.shape[-1]
    a = N_HEADS * HEAD_DIM
    c_conv = conv_w_0.shape[1] * N_CHIPS
    xs = x[0]
    target = loss_target[0]
    row = lambda vec: vec.reshape(1, -1)

    big = [w_in_0, w_out_0, w_up_0, w_down_0, pool_w_1, w_up_1, w_down_1]
    first = [w_in_0.astype(BF16)]
    first_split = [True]
    copies_a = functools.partial(_chip_copies, _gather_views(first_split))
    copies_b = functools.partial(_chip_copies, _gather_whole_views)
    start_a = _ici_start(first, _gather_land_shapes(first, first_split), copies_a, b_f_0,
                         "gather_start_a")
    zero = start_a[-1][0, 0]
    rest = [(w + zero).astype(BF16)
            for w in (w_out_0, w_up_0, w_down_0, pool_w_1, w_up_1, w_down_1)]
    rest = rest + [conv_w_0]
    start_b = _ici_start(rest, [jax.ShapeDtypeStruct((N_CHIPS,) + w.shape, w.dtype) for w in rest],
                         copies_b, start_a[-1], "gather_start_b")
    n0 = _rms_pre(start_b[-1], xs, row(norm_mix_0))
    first, land_a = _ici_wait(start_a, copies_a, n0, "gather_wait_a")
    (g_in,) = _gather_finish(first, land_a, first_split, "gather_finish_a")
    w_in = g_in.transpose(1, 0, 2).reshape(d, -1)
    w_qkv = w_in[:, :3 * a]
    w_f = jnp.pad(w_in[:, 3 * a:3 * a + N_HEADS], ((0, 0), (0, 128 - N_HEADS)))
    w_bcx = w_in[:, 3 * a + N_HEADS:]
    bf = jnp.pad(b_f_0, (0, 128 - N_HEADS)).reshape(1, 128)

    qkv, fl, bcx = _in_proj(n0, w_qkv, w_f, w_bcx)
    qa, ka = _gate_prep(fl, bf, qkv)
    o, lse = _attn_fwd(qa, ka, qkv)
    rest, land_b = _ici_wait(start_b, copies_b, o, "gather_wait_b")
    own_slot = 2 * lax.axis_index("x") + lax.axis_index("y")
    g_out, g_up0, g_down0, g_pool, g_up1, g_down1, g_conv = [
        lax.dynamic_update_index_in_dim(land, shard, own_slot, 0)
        for land, shard in zip(land_b, rest)]
    w_out = g_out.reshape(-1, d)
    conv_w = _pad_rows(g_conv.transpose(1, 0, 2).reshape(conv_w_0.shape[0], c_conv), 8)
    h1 = _conv_out(o, bcx, conv_w, w_out, xs)
    w_down0 = g_down0.reshape(-1, d)
    w_down1 = g_down1.reshape(-1, d)
    pool_w = g_pool.transpose(1, 0, 2, 3).reshape(pool_w_1.shape[0], -1, pool_w_1.shape[2])
    h2, a0, nf0 = _mlp_fwd(h1, row(norm_ffn_0), g_up0, w_down0, "mlp_fwd_0")
    h3 = _pool_fwd(h2, row(norm_mix_1), pool_w, row(pool_scale_1))
    dh4, a1, nf1, loss_part, d_final = _mlp_fwd(h3, row(norm_ffn_1), g_up1, w_down1, "mlp_fwd_1",
                                                head=(row(final_norm), target))

    slot_cols = g_up0.shape[2]
    pool_cols = pool_w.shape[2]
    da1, dz1, dh3, d_nffn1 = _mlp_bwd_x(dh4, a1, g_up1, w_down1, h3, row(norm_ffn_1), "mlp_bwd_x_1")
    dw_up1, dw_down1 = _mlp_bwd_w(nf1, da1, a1, dz1, slot_cols, "mlp_bwd_w_1")
    scatter_1 = _scatter_start([dw_up1, dw_down1.reshape(N_CHIPS, -1, d)], bf, "mlp1")
    dh2, dw_pool, d_pscale, d_nmix1 = _pool_bwd(scatter_1[-1], dh3, h2, row(norm_mix_1), pool_w,
                                                row(pool_scale_1))
    da0, dz0, dh1, d_nffn0 = _mlp_bwd_x(dh2, a0, g_up0, w_down0, h1, row(norm_ffn_0), "mlp_bwd_x_0")
    dw_up0, dw_down0 = _mlp_bwd_w(nf0, da0, a0, dz0, slot_cols, "mlp_bwd_w_0")
    dw_pool = (dw_pool.reshape(pool_w.shape[0], N_CHIPS, -1, pool_cols).transpose(1, 0, 2, 3)
               .reshape(N_CHIPS, -1, pool_cols))
    scatter_0 = _scatter_start([dw_up0, dw_down0.reshape(N_CHIPS, -1, d), dw_pool], bf, "mlp0")
    do, delta, dbcx, dw_out, d_conv = _conv_out_bwd(scatter_0[-1], dh1, w_out, o, bcx, conv_w)
    scatter_o = _scatter_start([dw_out.reshape(N_CHIPS, -1, d)], bf, "out")
    dqa, dka, dv = _attn_bwd(scatter_o[-1], qa, ka, qkv, do, lse, delta)
    dqkv, dfl, d_bf = _gate_bwd(dqa, dka, dv, fl, bf)
    dw_qkv, dw_f, dw_bcx = _wgrad_in(n0, [dqkv, dfl, dbcx])
    dw_in = jnp.concatenate([dw_qkv, dw_f[:N_HEADS], dw_bcx], axis=0).reshape(N_CHIPS, -1, d)
    slot_rows = -(-dw_in.shape[1] // 32) * 32
    dw_in = jnp.pad(dw_in, ((0, 0), (0, slot_rows - dw_in.shape[1]), (0, 0)))
    scatter_m = _scatter_start([dw_in], bf, "mixer")
    grad_x, d_nmix0 = _in_proj_bwd(scatter_m[-1], dqkv, dfl, dbcx, w_qkv, w_f, w_bcx, xs,
                                   row(norm_mix_0), dh1)

    r_up1, r_down1 = _scatter_finish(scatter_1, grad_x, "mlp1")
    r_up0, r_down0, r_pool = _scatter_finish(scatter_0, grad_x, "mlp0")
    (r_out,) = _scatter_finish(scatter_o, grad_x, "out")
    (r_in,) = _scatter_finish(scatter_m, grad_x, "mixer")
    reduced = [r_in, r_out, r_up0, r_down0, r_pool, r_up1, r_down1]
    moments = [(m_w_in_0, v_w_in_0), (m_w_out_0, v_w_out_0), (m_w_up_0, v_w_up_0),
               (m_w_down_0, v_w_down_0), (m_pool_w_1, v_pool_w_1), (m_w_up_1, v_w_up_1),
               (m_w_down_1, v_w_down_1)]
    big_out = []
    for k, (w, g, (m, v)) in enumerate(zip(big, reduced, moments)):
        if w.shape[-1] % 128:
            view = lambda t: t.reshape(-1, t.shape[-1]).T
            back = lambda t: t.T.reshape(w.shape)
            g_view = g[:w.shape[-1]]
        else:
            view = lambda t: t.reshape(-1, t.shape[-1])
            back = lambda t: t.reshape(w.shape)
            g_view = view(g)
        delta_w, new_m, new_v = _adamw(view(w), g_view, view(m), view(v), "adamw_%d" % k)
        big_out.append((back(g_view), back(delta_w), back(new_m), back(new_v)))

    tail = jnp.concatenate([d_conv[0:3].reshape(-1)[d:], d_bf[0, :N_HEADS], loss_part[0, :1]])
    small_part = jnp.concatenate(
        [d_nmix0, d_nffn0, d_nmix1, d_pscale, d_nffn1, d_final,
         d_conv[0:3].reshape(1, -1)[:, :d],
         jnp.pad(tail, (0, d - tail.shape[0])).reshape(1, d)], axis=0)
    parts = _gather_small(small_part)

    chip = 2 * lax.axis_index("x") + lax.axis_index("y")
    cw_cols = conv_w_0.shape[1]

    def conv_block(full):
        mine = lax.dynamic_slice_in_dim(full, chip * cw_cols, cw_cols, axis=1)
        return jnp.pad(mine.reshape(-1), (0, d - mine.size))

    def small_rows(vals, cw, bfv):
        return jnp.stack(list(vals) + [cw, jnp.pad(bfv, (0, d - N_HEADS))])

    smalls_w = [norm_mix_0, norm_ffn_0, norm_mix_1, pool_scale_1, norm_ffn_1, final_norm]
    smalls_m = [m_norm_mix_0, m_norm_ffn_0, m_norm_mix_1, m_pool_scale_1, m_norm_ffn_1, m_final_norm]
    smalls_v = [v_norm_mix_0, v_norm_ffn_0, v_norm_mix_1, v_pool_scale_1, v_norm_ffn_1, v_final_norm]
    pad_cw = lambda t: jnp.pad(t.reshape(-1), (0, d - t.size))
    w_rows = small_rows(smalls_w, pad_cw(conv_w_0), b_f_0)
    m_rows = small_rows(smalls_m, pad_cw(m_conv_w_0), m_b_f_0)
    v_rows = small_rows(smalls_v, pad_cw(v_conv_w_0), v_b_f_0)

    g_sum = _sum_devices(parts)
    conv_full = jnp.concatenate([g_sum[6], g_sum[7, :3 * c_conv - d]]).reshape(3, c_conv)
    bf_grad = g_sum[7, 3 * c_conv - d:3 * c_conv - d + N_HEADS]
    loss = g_sum[7, 3 * c_conv - d + N_HEADS]
    g_rows = jnp.concatenate(
        [g_sum[0:6], conv_block(conv_full).reshape(1, d),
         jnp.pad(bf_grad, (0, d - N_HEADS)).reshape(1, d)], axis=0)
    d_rows, nm_rows, nv_rows = _adamw(w_rows, g_rows, m_rows, v_rows, "adamw_small")

    def unpack(rows):
        cw = rows[6, :conv_w_0.size].reshape(conv_w_0.shape)
        return [rows[0], rows[1], rows[2], rows[3], rows[4], rows[5], cw, rows[7, :N_HEADS]]

    def assemble(kind):
        sm = unpack([g_rows, d_rows, nm_rows, nv_rows][kind])
        lg = [t[kind] for t in big_out]
        return [sm[0], lg[0], sm[7], sm[6], lg[1], sm[1], lg[2], lg[3],
                sm[2], lg[4], sm[3], sm[4], lg[5], lg[6], sm[5]]

    return (loss, grad_x[None], *assemble(0), *assemble(1), *assemble(2), *assemble(3))
```

```python
import functools

import jax
import jax.numpy as jnp
from jax import lax
from jax.experimental import pallas as pl
from jax.experimental.pallas import tpu as pltpu

F32 = jnp.float32
BF16 = jnp.bfloat16

RMS_EPS = 1e-6
HEAD_DIM = 64
N_HEADS = 8
ATTN_SCALE = HEAD_DIM ** -0.5
LOG2_E = 1.4426950408889634
POOL_WINDOWS = (2, 4, 8, 16)
POOL_HALO = 16
CONV_HALO = 8
NEG_BIG = -1e30

ADAM_LR = 0.001
ADAM_B1 = 0.9
ADAM_B2 = 0.999
ADAM_EPS = 1e-08
ADAM_WD = 0.01
ADAM_STEP = 10

N_CHIPS = 4
N_DEV = 8
MESH = pl.DeviceIdType.MESH

VMEM_LIMIT_BYTES = 56 * 1024 * 1024

TILE_ROWS = 512
TILE_PROJ_ROWS = 1024
TILE_ATTN = 512
TILE_MLP_ROWS = 1024
TILE_MLP_FF = 1024
TILE_MLP_BWD_FF = 512
TILE_HEAD_ROWS = 256
TILE_WGRAD_K = 1024
TILE_WGRAD_N = 1024
TILE_ELEM_ROWS = 256
SUM_CHUNK_ROWS = 128

LANE_CQ = 64
LANE_CK = 88


def _params(semantics):
    return pltpu.CompilerParams(dimension_semantics=semantics,
                                vmem_limit_bytes=VMEM_LIMIT_BYTES)


def _nn(a, b):
    return lax.dot_general(a, b, (((1,), (0,)), ((), ())), preferred_element_type=F32)


def _nt(a, b):
    return lax.dot_general(a, b, (((1,), (1,)), ((), ())), preferred_element_type=F32)


def _tn(a, b):
    return lax.dot_general(a, b, (((0,), (0,)), ((), ())), preferred_element_type=F32)


def _split3(v):
    hi = v.astype(BF16)
    r1 = v - hi.astype(F32)
    mid = r1.astype(BF16)
    lo = (r1 - mid.astype(F32)).astype(BF16)
    return hi, mid, lo


def _exact_nn(sel, v):
    hi, mid, lo = _split3(v)
    return _nn(sel, hi) + _nn(sel, mid) + _nn(sel, lo)


def _exact_nt(sel, v):
    hi, mid, lo = _split3(v)
    return _nt(sel, hi) + _nt(sel, mid) + _nt(sel, lo)


def _rms_fwd(x, g):
    r = lax.rsqrt(jnp.mean(x * x, axis=-1, keepdims=True) + RMS_EPS)
    return x * r * g, r


def _rms_bwd(dn, x, g):
    r = lax.rsqrt(jnp.mean(x * x, axis=-1, keepdims=True) + RMS_EPS)
    xh = x * r
    gy = dn * g
    dx = r * (gy - xh * jnp.mean(gy * xh, axis=-1, keepdims=True))
    return dx, jnp.sum(dn * xh, axis=0, keepdims=True)


def _lane(shape):
    return lax.broadcasted_iota(jnp.int32, shape, len(shape) - 1)


def _row(shape):
    return lax.broadcasted_iota(jnp.int32, shape, len(shape) - 2)


def _full(a):
    nd = a.ndim
    return pl.BlockSpec(a.shape, lambda *_: (0,) * nd)


def _rms_pre(after, x, g):
    s, d = x.shape
    tm = min(TILE_ROWS, s)

    def body(after_ref, x_ref, g_ref, n_ref):
        n, _ = _rms_fwd(x_ref[...], g_ref[...])
        n_ref[...] = n.astype(BF16)

    rows = pl.BlockSpec((tm, d), lambda i: (i, 0))
    return pl.pallas_call(
        body, name="rms_pre", grid=(s // tm,),
        in_specs=[ANY, rows, _full(g)], out_specs=rows,
        out_shape=jax.ShapeDtypeStruct((s, d), BF16),
        compiler_params=_params(("parallel",)),
    )(after, x, g)


def _in_proj(n, w_qkv, w_f, w_bcx):
    s, d = n.shape
    tm = min(TILE_PROJ_ROWS, s)

    def body(n_ref, wq_ref, wf_ref, wb_ref, qkv_ref, fl_ref, bcx_ref):
        nb = n_ref[...]
        qkv_ref[...] = _nn(nb, wq_ref[...]).astype(BF16)
        fl_ref[...] = _nn(nb, wf_ref[...])
        bcx_ref[...] = _nn(nb, wb_ref[...])

    rows = lambda c: pl.BlockSpec((tm, c), lambda i: (i, 0))
    return pl.pallas_call(
        body, name="in_proj", grid=(s // tm,),
        in_specs=[rows(d), _full(w_qkv), _full(w_f), _full(w_bcx)],
        out_specs=[rows(w_qkv.shape[1]), rows(w_f.shape[1]), rows(w_bcx.shape[1])],
        out_shape=[jax.ShapeDtypeStruct((s, w_qkv.shape[1]), BF16),
                   jax.ShapeDtypeStruct((s, w_f.shape[1]), F32),
                   jax.ShapeDtypeStruct((s, w_bcx.shape[1]), F32)],
        compiler_params=_params(("parallel",)),
    )(n, w_qkv, w_f, w_bcx)


def _gate_prep(fl, bf, qkv):
    s = fl.shape[0]
    a = N_HEADS * HEAD_DIM
    tm = min(TILE_ROWS, s)

    def body(fl_ref, bf_ref, q_ref, k_ref, qa_ref, ka_ref, carry_ref):
        i = pl.program_id(0)

        @pl.when(i == 0)
        def _():
            carry_ref[...] = jnp.zeros_like(carry_ref)

        z = fl_ref[...] + bf_ref[...]
        logf = jnp.minimum(z, 0.0) - jnp.log(1.0 + jnp.exp(-jnp.abs(z)))
        lower = (_lane((tm, tm)) <= _row((tm, tm))).astype(BF16)
        cum = _exact_nn(lower, logf) + carry_ref[0:1, :]
        carry_ref[0:1, :] = cum[tm - 1:tm, :]

        lane = _lane((tm, 128))
        pieces = [p.astype(F32)
                  for p in _split3(jnp.where(lane < N_HEADS, LOG2_E * cum, 0.0))]
        shared_q = sum(pltpu.roll(p, LANE_CQ + N_HEADS * k, axis=1) for k, p in enumerate(pieces))
        shared_k = -sum(pltpu.roll(p, LANE_CK + N_HEADS * k, axis=1) for k, p in enumerate(pieces))
        for h in range(N_HEADS):
            at_q = functools.reduce(jnp.logical_or,
                                    [lane == LANE_CQ + N_HEADS * k + h for k in range(3)])
            at_k = functools.reduce(jnp.logical_or,
                                    [lane == LANE_CK + N_HEADS * k + h for k in range(3)])
            pair = slice((h // 2) * 128, (h // 2 + 1) * 128)
            qp = q_ref[:, pair].astype(F32)
            kp = k_ref[:, pair].astype(F32)
            if h % 2:
                qp = pltpu.roll(qp, HEAD_DIM, axis=1)
                kp = pltpu.roll(kp, HEAD_DIM, axis=1)
            q_bias = jnp.where(at_k, 1.0, shared_q)
            k_bias = jnp.where(at_q, 1.0, shared_k)
            qa_ref[h] = jnp.where(lane < HEAD_DIM, qp * (ATTN_SCALE * LOG2_E), q_bias).astype(BF16)
            ka_ref[h] = jnp.where(lane < HEAD_DIM, kp, k_bias).astype(BF16)

    aug = jax.ShapeDtypeStruct((N_HEADS, s, 128), BF16)
    aug_spec = pl.BlockSpec((N_HEADS, tm, 128), lambda i: (0, i, 0))
    return pl.pallas_call(
        body, name="gate_prep", grid=(s // tm,),
        in_specs=[pl.BlockSpec((tm, 128), lambda i: (i, 0)), _full(bf),
                  pl.BlockSpec((tm, a), lambda i: (i, 0)),
                  pl.BlockSpec((tm, a), lambda i: (i, 1))],
        out_specs=[aug_spec, aug_spec],
        out_shape=[aug, aug],
        scratch_shapes=[pltpu.VMEM((8, 128), F32)],
        compiler_params=_params(("arbitrary",)),
    )(fl, bf, qkv, qkv)


def _attn_fwd(qa, ka, qkv):
    s = qa.shape[1]
    a = N_HEADS * HEAD_DIM
    t = min(TILE_ATTN, s)
    n_pairs = N_HEADS // 2
    v_block0 = 2 * a // 128

    ones_lane = (HEAD_DIM, 0)

    def body(qa_ref, ka_ref, v_ref, o_ref, lse_ref, m_ref, acc_ref, s_even, s_odd):
        i = pl.program_id(1)
        m_ref[...] = jnp.full_like(m_ref, NEG_BIG)
        acc_ref[...] = jnp.zeros_like(acc_ref)
        upper_rows = _row((128, t)) < HEAD_DIM

        def keys(j):
            return pl.ds(pl.multiple_of(j * t, t), t)

        def scores_into(buf, j):
            for e in range(2):
                buf[e] = _nt(ka_ref[e, keys(j), :], qa_ref[e])

        def consume(buf, j, masked):
            vf = v_ref[keys(j), :].astype(F32)
            lane = _lane((t, 128))
            own = [lane < HEAD_DIM, lane >= HEAD_DIM]
            for e in range(2):
                v_head = jnp.where(own[e], vf, jnp.where(lane == ones_lane[e], 1.0, 0.0)).astype(BF16)
                sc = buf[e]
                if masked:
                    sc = jnp.where(_row((t, t)) <= _lane((t, t)), sc, NEG_BIG)
                m_prev = m_ref[e]
                m_new = jnp.maximum(m_prev, jnp.max(sc, axis=0, keepdims=True))
                p = jnp.exp2(sc - m_new).astype(BF16)
                acc_ref[e] = acc_ref[e] * jnp.exp2(m_prev - m_new) + _tn(v_head, p)
                m_ref[e] = m_new

        scores_into(s_even, 0)

        def two_tiles(p, carry):
            j = 2 * p
            scores_into(s_odd, j + 1)
            consume(s_even, j, False)
            scores_into(s_even, j + 2)
            consume(s_odd, j + 1, False)
            return carry

        lax.fori_loop(0, i // 2, two_tiles, 0)

        @pl.when(i % 2 == 0)
        def _():
            consume(s_even, i, True)

        @pl.when(i % 2 == 1)
        def _():
            scores_into(s_odd, i)
            consume(s_even, i - 1, False)
            consume(s_odd, i, True)

        denom = [acc_ref[e, ones_lane[e]:ones_lane[e] + 1, :] for e in range(2)]
        out_t = jnp.where(upper_rows, acc_ref[0] / denom[0], acc_ref[1] / denom[1])
        o_ref[...] = out_t.T.astype(BF16)
        lse = [m_ref[e] + LOG2_E * jnp.log(denom[e]) for e in range(2)]
        lse_ref[...] = jnp.where(_row((8, t)) == 0, lse[0], lse[1])

    return pl.pallas_call(
        body, name="attn_fwd", grid=(n_pairs, s // t),
        in_specs=[pl.BlockSpec((2, t, 128), lambda g, i: (g, i, 0)),
                  pl.BlockSpec((2, s, 128), lambda g, i: (g, 0, 0)),
                  pl.BlockSpec((s, 128), lambda g, i: (0, v_block0 + g))],
        out_specs=[pl.BlockSpec((t, 128), lambda g, i: (i, g)),
                   pl.BlockSpec((None, 8, t), lambda g, i: (g, 0, i))],
        out_shape=[jax.ShapeDtypeStruct((s, a), BF16),
                   jax.ShapeDtypeStruct((n_pairs, 8, s), F32)],
        scratch_shapes=[pltpu.VMEM((2, 1, t), F32), pltpu.VMEM((2, 128, t), F32),
                        pltpu.VMEM((2, t, t), F32), pltpu.VMEM((2, t, t), F32)],
        compiler_params=_params(("parallel", "arbitrary")),
    )(qa, ka, qkv)


def _conv_out(o, bcx, cw, w_out, x):
    s, d = x.shape
    c = o.shape[1]
    tm = min(TILE_ROWS, s)

    def body(o_ref, b_ref, c_ref, xin_ref, cw_ref, w_ref, x_ref, h_ref, ubuf):
        i = pl.program_id(0)

        @pl.when(i == 0)
        def _():
            ubuf[0:CONV_HALO, :] = jnp.zeros((CONV_HALO, c), F32)

        u = c_ref[...] * xin_ref[...]
        ubuf[CONV_HALO:CONV_HALO + tm, :] = u
        u1 = ubuf[CONV_HALO - 1:CONV_HALO - 1 + tm, :]
        u2 = ubuf[CONV_HALO - 2:CONV_HALO - 2 + tm, :]
        cv = (cw_ref[0:1, :] * u2 + cw_ref[1:2, :] * u1) + cw_ref[2:3, :] * u
        y = (b_ref[...] * cv).astype(BF16)
        mix = _nn(o_ref[...], w_ref[0:c, :]) + _nn(y, w_ref[c:2 * c, :])
        h_ref[...] = x_ref[...] + mix
        ubuf[0:CONV_HALO, :] = u[tm - CONV_HALO:tm, :]

    col = lambda k: pl.BlockSpec((tm, c), lambda i: (i, k))
    return pl.pallas_call(
        body, name="conv_out", grid=(s // tm,),
        in_specs=[col(0), col(0), col(1), col(2), _full(cw), _full(w_out),
                  pl.BlockSpec((tm, d), lambda i: (i, 0))],
        out_specs=pl.BlockSpec((tm, d), lambda i: (i, 0)),
        out_shape=jax.ShapeDtypeStruct((s, d), F32),
        scratch_shapes=[pltpu.VMEM((tm + CONV_HALO, c), F32)],
        compiler_params=_params(("arbitrary",)),
    )(o, bcx, bcx, bcx, cw, w_out, x)


def _mlp_fwd(h, g, w_up, w_down, name, head=None):
    s, d = h.shape
    ff = w_down.shape[0]
    slot_cols = w_up.shape[2]
    tm = min(TILE_MLP_ROWS, s)
    tf = min(TILE_MLP_FF, slot_cols)
    per_slot = slot_cols // tf
    nf = ff // tf
    n_head = 0 if head is None else 2
    chunk = min(TILE_HEAD_ROWS, tm)

    def body(*refs):
        h_ref, g_ref, wu_ref, wd_ref = refs[:4]
        out_ref, a_ref, n_ref = refs[4 + n_head:7 + n_head]
        nb_ref, acc_ref = refs[9 + n_head:11 + n_head] if head else refs[-2:]
        i = pl.program_id(0)
        f = pl.program_id(1)

        def target_copy():
            t_hbm, t_buf, t_sem = refs[5], refs[-2], refs[-1]
            return pltpu.make_async_copy(t_hbm.at[pl.ds(pl.multiple_of(i * tm, tm), tm), :],
                                         t_buf, t_sem)

        @pl.when(f == 0)
        def _():
            n, _ = _rms_fwd(h_ref[...], g_ref[...])
            nb = n.astype(BF16)
            nb_ref[...] = nb
            n_ref[...] = nb
            acc_ref[...] = jnp.zeros_like(acc_ref)
            if head is not None:
                target_copy().start()

        pre = _nn(nb_ref[...], wu_ref[...])
        a_ref[...] = pre.astype(BF16)
        r = jnp.square(jnp.maximum(pre, 0.0)).astype(BF16)
        acc_ref[...] += _nn(r, wd_ref[...])

        @pl.when(f == nf - 1)
        def _():
            if head is None:
                out_ref[...] = h_ref[...] + acc_ref[...]
            else:
                gf_ref, t_buf = refs[4], refs[-2]
                loss_ref, dg_ref = refs[7 + n_head:9 + n_head]
                target_copy().wait()
                part, dg = None, None
                for r0 in range(0, tm, chunk):
                    rows_ = slice(r0, r0 + chunk)
                    out = h_ref[rows_, :] + acc_ref[rows_, :]
                    y, _ = _rms_fwd(out, gf_ref[...])
                    err = y - t_buf[rows_, :]
                    p = 0.5 * jnp.sum(jnp.mean(err * err, axis=-1, keepdims=True), axis=0,
                                      keepdims=True)
                    dx, dgp = _rms_bwd(err / d, out, gf_ref[...])
                    out_ref[rows_, :] = dx
                    part = p if part is None else part + p
                    dg = dgp if dg is None else dg + dgp
                part = jnp.broadcast_to(part, loss_ref.shape)

                @pl.when(i == 0)
                def _():
                    loss_ref[...] = part
                    dg_ref[...] = dg

                @pl.when(i > 0)
                def _():
                    loss_ref[...] += part
                    dg_ref[...] += dg

    rows = pl.BlockSpec((tm, d), lambda i, f: (i, 0))
    in_specs = [rows, _full(g),
                pl.BlockSpec((None, d, tf), lambda i, f: (f // per_slot, 0, f % per_slot)),
                pl.BlockSpec((tf, d), lambda i, f: (f, 0))]
    out_specs = [rows, pl.BlockSpec((tm, tf), lambda i, f: (i, f)), rows]
    out_shape = [jax.ShapeDtypeStruct((s, d), F32), jax.ShapeDtypeStruct((s, ff), BF16),
                 jax.ShapeDtypeStruct((s, d), BF16)]
    args = [h, g, w_up, w_down]
    scratch = [pltpu.VMEM((tm, d), BF16), pltpu.VMEM((tm, d), F32)]
    if head is not None:
        in_specs += [_full(head[0]), ANY]
        args += list(head)
        out_specs += [pl.BlockSpec((1, 128), lambda i, f: (0, 0)),
                      pl.BlockSpec((1, d), lambda i, f: (0, 0))]
        out_shape += [jax.ShapeDtypeStruct((1, 128), F32), jax.ShapeDtypeStruct((1, d), F32)]
        scratch += [pltpu.VMEM((tm, d), F32), pltpu.SemaphoreType.DMA]
    return pl.pallas_call(
        body, name=name, grid=(s // tm, nf),
        in_specs=in_specs, out_specs=out_specs, out_shape=out_shape, scratch_shapes=scratch,
        compiler_params=_params(("parallel" if head is None else "arbitrary", "arbitrary")),
    )(*args)


def _window_sum_down(e, window):
    step = 1
    while step < window:
        e = e + pltpu.roll(e, step, axis=0)
        step *= 2
    return e


def _window_sum_up(e, window):
    n = e.shape[0]
    step = 1
    while step < window:
        e = e + pltpu.roll(e, n - step, axis=0)
        step *= 2
    return e


def _pool_counts(first_row, tm, window):
    t = first_row + _row((tm, 1))
    return jnp.minimum(t + 1, window).astype(F32)


def _pool_fwd(h, g, pw, ps):
    s, d = h.shape
    cg = d // len(POOL_WINDOWS)
    tm = min(TILE_ROWS, s)

    def body(h_ref, g_ref, pw_ref, ps_ref, out_ref, nbuf):
        i = pl.program_id(0)

        @pl.when(i == 0)
        def _():
            nbuf[0:POOL_HALO, :] = jnp.zeros((POOL_HALO, d), F32)

        n, _ = _rms_fwd(h_ref[...], g_ref[...])
        nbuf[POOL_HALO:POOL_HALO + tm, :] = n
        for k, window in enumerate(POOL_WINDOWS):
            cols = slice(k * cg, (k + 1) * cg)
            sums = _window_sum_down(nbuf[:, cols], window)[POOL_HALO:, :]
            pooled = sums / _pool_counts(i * tm, tm, window) - n[:, cols]
            y = _nn(pooled.astype(BF16), pw_ref[k]) * ps_ref[:, cols]
            out_ref[:, cols] = h_ref[:, cols] + y
        nbuf[0:POOL_HALO, :] = n[tm - POOL_HALO:tm, :]

    return pl.pallas_call(
        body, name="pool_fwd", grid=(s // tm,),
        in_specs=[pl.BlockSpec((tm, d), lambda i: (i, 0)), _full(g), _full(pw), _full(ps)],
        out_specs=pl.BlockSpec((tm, d), lambda i: (i, 0)),
        out_shape=jax.ShapeDtypeStruct((s, d), F32),
        scratch_shapes=[pltpu.VMEM((tm + POOL_HALO, d), F32)],
        compiler_params=_params(("arbitrary",)),
    )(h, g, pw, ps)


def _mlp_bwd_x(dz, a, w_up, w_down, h_in, g, name):
    s, d = dz.shape
    ff = w_down.shape[0]
    slot_cols = w_up.shape[2]
    tm = min(TILE_MLP_ROWS, s)
    tf = min(TILE_MLP_BWD_FF, slot_cols)
    per_slot = slot_cols // tf
    nf = ff // tf

    def body(dz_ref, a_ref, wu_ref, wd_ref, h_ref, g_ref, da_ref, dzb_ref, dh_ref, dg_ref,
             dzs_ref, acc_ref):
        i = pl.program_id(0)
        f = pl.program_id(1)

        @pl.when(f == 0)
        def _():
            dzb = dz_ref[...].astype(BF16)
            dzs_ref[...] = dzb
            dzb_ref[...] = dzb
            acc_ref[...] = jnp.zeros_like(acc_ref)

        dr = _nt(dzs_ref[...], wd_ref[...])
        da = (dr * (2.0 * jnp.maximum(a_ref[...].astype(F32), 0.0))).astype(BF16)
        da_ref[...] = da
        acc_ref[...] += _nt(da, wu_ref[...])

        @pl.when(f == nf - 1)
        def _():
            dx, dg = _rms_bwd(acc_ref[...], h_ref[...], g_ref[...])
            dh_ref[...] = dz_ref[...] + dx

            @pl.when(i == 0)
            def _():
                dg_ref[...] = dg

            @pl.when(i > 0)
            def _():
                dg_ref[...] += dg

    return pl.pallas_call(
        body, name=name, grid=(s // tm, nf),
        in_specs=[pl.BlockSpec((tm, d), lambda i, f: (i, 0)),
                  pl.BlockSpec((tm, tf), lambda i, f: (i, f)),
                  pl.BlockSpec((None, d, tf), lambda i, f: (f // per_slot, 0, f % per_slot)),
                  pl.BlockSpec((tf, d), lambda i, f: (f, 0)),
                  pl.BlockSpec((tm, d), lambda i, f: (i, 0)), _full(g)],
        out_specs=[pl.BlockSpec((tm, tf), lambda i, f: (i, f)),
                   pl.BlockSpec((tm, d), lambda i, f: (i, 0)),
                   pl.BlockSpec((tm, d), lambda i, f: (i, 0)),
                   pl.BlockSpec((1, d), lambda i, f: (0, 0))],
        out_shape=[jax.ShapeDtypeStruct((s, ff), BF16),
                   jax.ShapeDtypeStruct((s, d), BF16),
                   jax.ShapeDtypeStruct((s, d), F32),
                   jax.ShapeDtypeStruct((1, d), F32)],
        scratch_shapes=[pltpu.VMEM((tm, d), BF16), pltpu.VMEM((tm, d), F32)],
        compiler_params=_params(("arbitrary", "arbitrary")),
    )(dz, a, w_up, w_down, h_in, g)


def _mlp_bwd_w(n, da, a, dzb, slot_cols, name):
    s, d = n.shape
    ff = a.shape[1]
    tn = min(TILE_WGRAD_N, slot_cols)
    tk = min(TILE_WGRAD_K, s)
    per_slot = slot_cols // tn
    nk = s // tk

    def body(n_ref, da_ref, a_ref, dz_ref, du_ref, dd_ref, accu_ref, accd_ref):
        k = pl.program_id(1)

        @pl.when(k == 0)
        def _():
            accu_ref[...] = jnp.zeros_like(accu_ref)
            accd_ref[...] = jnp.zeros_like(accd_ref)

        accu_ref[...] += _tn(n_ref[...], da_ref[...])
        r = jnp.square(jnp.maximum(a_ref[...].astype(F32), 0.0)).astype(BF16)
        accd_ref[...] += _tn(r, dz_ref[...])

        @pl.when(k == nk - 1)
        def _():
            du_ref[...] = accu_ref[...].astype(BF16)
            dd_ref[...] = accd_ref[...].astype(BF16)

    return pl.pallas_call(
        body, name=name, grid=(ff // tn, nk),
        in_specs=[pl.BlockSpec((tk, d), lambda f, k: (k, 0)),
                  pl.BlockSpec((tk, tn), lambda f, k: (k, f)),
                  pl.BlockSpec((tk, tn), lambda f, k: (k, f)),
                  pl.BlockSpec((tk, d), lambda f, k: (k, 0))],
        out_specs=[pl.BlockSpec((None, d, tn), lambda f, k: (f // per_slot, 0, f % per_slot)),
                   pl.BlockSpec((tn, d), lambda f, k: (f, 0))],
        out_shape=[jax.ShapeDtypeStruct((ff // slot_cols, d, slot_cols), BF16),
                   jax.ShapeDtypeStruct((ff, d), BF16)],
        scratch_shapes=[pltpu.VMEM((d, tn), F32), pltpu.VMEM((tn, d), F32)],
        compiler_params=_params(("parallel", "arbitrary")),
    )(n, da, a, dzb)


def _pool_bwd(after, dh, h, g, pw, ps):
    s, d = h.shape
    cg = d // len(POOL_WINDOWS)
    tm = min(TILE_ROWS, s)
    nb = s // tm
    halo_per_tile = tm // POOL_HALO

    def body(after_ref, dh_ref, h_ref, halo_ref, g_ref, pw_ref, ps_ref,
             dx_ref, dpw_ref, dps_ref, dg_ref, nbuf, qbuf, dn_ref, carry, dpw_acc):
        i = pl.program_id(0)
        blk = nb - 1 - i

        @pl.when(i == 0)
        def _():
            carry[...] = jnp.zeros_like(carry)
            dpw_acc[...] = jnp.zeros_like(dpw_acc)
            dps_ref[...] = jnp.zeros_like(dps_ref)
            dg_ref[...] = jnp.zeros_like(dg_ref)

        hv = h_ref[...]
        n, _ = _rms_fwd(hv, g_ref[...])
        nh, _ = _rms_fwd(halo_ref[...], g_ref[...])
        nbuf[0:POOL_HALO, :] = jnp.where(blk == 0, 0.0, nh)
        nbuf[POOL_HALO:POOL_HALO + tm, :] = n
        dhv = dh_ref[...]
        for k, window in enumerate(POOL_WINDOWS):
            cols = slice(k * cg, (k + 1) * cg)
            cnt = _pool_counts(blk * tm, tm, window)
            sums = _window_sum_down(nbuf[:, cols], window)[POOL_HALO:, :]
            pb = (sums / cnt - n[:, cols]).astype(BF16)
            dyk = dhv[:, cols]
            dps_ref[:, cols] += jnp.sum(dyk * _nn(pb, pw_ref[k]), axis=0, keepdims=True)
            dyb = (dyk * ps_ref[:, cols]).astype(BF16)
            dpw_acc[k] += _tn(pb, dyb)
            dpool = _nt(dyb, pw_ref[k])
            qv = dpool / cnt
            qbuf[0:tm, cols] = qv
            qbuf[tm:tm + POOL_HALO, cols] = carry[:, cols]
            dn_ref[:, cols] = _window_sum_up(qbuf[:, cols], window)[0:tm, :] - dpool
            carry[:, cols] = qv[0:POOL_HALO, :]
        dx, dg = _rms_bwd(dn_ref[...], hv, g_ref[...])
        dx_ref[...] = dhv + dx
        dg_ref[...] += dg

        @pl.when(i == nb - 1)
        def _():
            dpw_ref[...] = dpw_acc[...].astype(BF16)

    rev = lambda i: (nb - 1 - i, 0)
    return pl.pallas_call(
        body, name="pool_bwd", grid=(nb,),
        in_specs=[ANY, pl.BlockSpec((tm, d), rev), pl.BlockSpec((tm, d), rev),
                  pl.BlockSpec((POOL_HALO, d),
                               lambda i: (jnp.maximum((nb - 1 - i) * halo_per_tile - 1, 0), 0)),
                  _full(g), _full(pw), _full(ps)],
        out_specs=[pl.BlockSpec((tm, d), rev), _full(pw),
                   pl.BlockSpec((1, d), lambda i: (0, 0)),
                   pl.BlockSpec((1, d), lambda i: (0, 0))],
        out_shape=[jax.ShapeDtypeStruct((s, d), F32),
                   jax.ShapeDtypeStruct(pw.shape, BF16),
                   jax.ShapeDtypeStruct((1, d), F32),
                   jax.ShapeDtypeStruct((1, d), F32)],
        scratch_shapes=[pltpu.VMEM((tm + POOL_HALO, d), F32), pltpu.VMEM((tm + POOL_HALO, d), F32),
                        pltpu.VMEM((tm, d), F32), pltpu.VMEM((POOL_HALO, d), F32),
                        pltpu.VMEM(pw.shape, F32)],
        compiler_params=_params(("arbitrary",)),
    )(after, dh, h, h, g, pw, ps)


def _conv_out_bwd(after, dh, w_out, o, bcx, cw):
    s, d = dh.shape
    c = o.shape[1]
    tm = min(TILE_ROWS, s)
    nb = s // tm
    halo_per_tile = tm // CONV_HALO

    def body(after_ref, dh_ref, w_ref, o_ref, b_ref, c_ref, xin_ref, ch_ref, xh_ref, cw_ref,
             do_ref, delta_ref, dbcx_ref, dw_ref, dcw_ref, ubuf, dbuf, carry, acc):
        i = pl.program_id(0)
        blk = nb - 1 - i

        @pl.when(i == 0)
        def _():
            carry[...] = jnp.zeros_like(carry)
            acc[...] = jnp.zeros_like(acc)
            dcw_ref[...] = jnp.zeros_like(dcw_ref)

        dm = dh_ref[...].astype(BF16)
        dcat = _nt(dm, w_ref[...])
        do = dcat[:, 0:c]
        dy = dcat[:, c:2 * c]
        do_ref[...] = do.astype(BF16)
        head_of_lane = lax.shift_right_logical(_lane((8, c)), HEAD_DIM.bit_length() - 1)
        heads = (head_of_lane == _row((8, c))).astype(BF16)
        delta_ref[...] = _exact_nt(heads, do * o_ref[...].astype(F32))

        cv_ = c_ref[...]
        xin = xin_ref[...]
        bv = b_ref[...]
        u = cv_ * xin
        ubuf[0:CONV_HALO, :] = jnp.where(blk == 0, 0.0, ch_ref[...] * xh_ref[...])
        ubuf[CONV_HALO:CONV_HALO + tm, :] = u
        u1 = ubuf[CONV_HALO - 1:CONV_HALO - 1 + tm, :]
        u2 = ubuf[CONV_HALO - 2:CONV_HALO - 2 + tm, :]
        w0, w1, w2 = cw_ref[0:1, :], cw_ref[1:2, :], cw_ref[2:3, :]
        cv = (w0 * u2 + w1 * u1) + w2 * u
        acc[0:c, :] += _tn(o_ref[...], dm)
        acc[c:2 * c, :] += _tn((bv * cv).astype(BF16), dm)

        dcv = dy * bv
        dcw_ref[0:1, :] += jnp.sum(dcv * u2, axis=0, keepdims=True)
        dcw_ref[1:2, :] += jnp.sum(dcv * u1, axis=0, keepdims=True)
        dcw_ref[2:3, :] += jnp.sum(dcv * u, axis=0, keepdims=True)
        dbuf[0:tm, :] = dcv
        dbuf[tm:tm + CONV_HALO, :] = carry[...]
        du = w2 * dcv + w1 * dbuf[1:1 + tm, :] + w0 * dbuf[2:2 + tm, :]
        dbcx_ref[:, 0:c] = (dy * cv).astype(BF16)
        dbcx_ref[:, c:2 * c] = (du * xin).astype(BF16)
        dbcx_ref[:, 2 * c:3 * c] = (du * cv_).astype(BF16)
        carry[...] = dcv[0:CONV_HALO, :]

        @pl.when(i == nb - 1)
        def _():
            dw_ref[...] = acc[...].astype(BF16)

    rev = lambda k: (lambda i: (nb - 1 - i, k))
    halo = lambda k: (lambda i: (jnp.maximum((nb - 1 - i) * halo_per_tile - 1, 0), k))
    return pl.pallas_call(
        body, name="conv_out_bwd", grid=(nb,),
        in_specs=[ANY, pl.BlockSpec((tm, d), rev(0)), _full(w_out), pl.BlockSpec((tm, c), rev(0)),
                  pl.BlockSpec((tm, c), rev(0)), pl.BlockSpec((tm, c), rev(1)),
                  pl.BlockSpec((tm, c), rev(2)),
                  pl.BlockSpec((CONV_HALO, c), halo(1)), pl.BlockSpec((CONV_HALO, c), halo(2)),
                  _full(cw)],
        out_specs=[pl.BlockSpec((tm, c), rev(0)),
                   pl.BlockSpec((8, tm), lambda i: (0, nb - 1 - i)),
                   pl.BlockSpec((tm, 3 * c), rev(0)),
                   _full(w_out), _full(cw)],
        out_shape=[jax.ShapeDtypeStruct((s, c), BF16),
                   jax.ShapeDtypeStruct((8, s), F32),
                   jax.ShapeDtypeStruct((s, 3 * c), BF16),
                   jax.ShapeDtypeStruct(w_out.shape, BF16),
                   jax.ShapeDtypeStruct(cw.shape, F32)],
        scratch_shapes=[pltpu.VMEM((tm + CONV_HALO, c), F32), pltpu.VMEM((tm + CONV_HALO, c), F32),
                        pltpu.VMEM((CONV_HALO, c), F32), pltpu.VMEM(w_out.shape, F32)],
        compiler_params=_params(("arbitrary",)),
    )(after, dh, w_out, o, bcx, bcx, bcx, bcx, bcx, cw)


def _attn_bwd(after, qa, ka, qkv, do, lse, delta):
    s = qa.shape[1]
    a = N_HEADS * HEAD_DIM
    t = min(TILE_ATTN, s)
    nq = s // t
    n_pairs = N_HEADS // 2
    v_block0 = 2 * a // 128

    def body(after_ref, ka_ref, v_ref, qa_ref, do_ref, lse_ref, delta_ref,
             dqt_ref, dka_ref, dv_ref, dv_acc):
        g = pl.program_id(0)
        j = pl.program_id(1)

        @pl.when(j == 0)
        def _():
            dqt_ref[...] = jnp.zeros_like(dqt_ref)

        lane = _lane((t, 128))
        vf = v_ref[...].astype(F32)
        v_heads = [jnp.where(lane < HEAD_DIM, vf, 0.0).astype(BF16),
                   jnp.where(lane >= HEAD_DIM, vf, 0.0).astype(BF16)]
        ke_t = [ka_ref[e].astype(F32).T.astype(BF16) for e in range(2)]

        def q_step(i, first):
            qs = pl.ds(pl.multiple_of(i * t, t), t)
            dob = do_ref[qs, :]
            for e in range(2):
                qe = qa_ref[e, qs, :]
                sc = _nt(ka_ref[e], qe)
                if first:
                    sc = jnp.where(_row((t, t)) <= _lane((t, t)), sc, NEG_BIG)
                p = jnp.exp2(sc - lse_ref[pl.ds(e, 1), qs])
                dv_part = _nn(p.astype(BF16), dob)
                dp = _nt(v_heads[e], dob)
                ds = (p * (dp - delta_ref[pl.ds(2 * g + e, 1), qs])).astype(BF16)
                dk_part = _nn(ds, qe)
                if first:
                    dv_acc[e] = dv_part
                    dka_ref[e] = dk_part
                else:
                    dv_acc[e] += dv_part
                    dka_ref[e] += dk_part
                dqt_ref[e, :, qs] += _nn(ke_t[e], ds)

        q_step(j, True)

        def full_step(i, carry):
            q_step(i, False)
            return carry

        lax.fori_loop(j + 1, nq, full_step, 0)
        dv_ref[...] = jnp.where(lane < HEAD_DIM, dv_acc[0], dv_acc[1]).astype(BF16)

    return pl.pallas_call(
        body, name="attn_bwd", grid=(n_pairs, nq),
        in_specs=[ANY, pl.BlockSpec((2, t, 128), lambda g, j: (g, j, 0)),
                  pl.BlockSpec((t, 128), lambda g, j: (j, v_block0 + g)),
                  pl.BlockSpec((2, s, 128), lambda g, j: (g, 0, 0)),
                  pl.BlockSpec((s, 128), lambda g, j: (0, g)),
                  pl.BlockSpec((None, 8, s), lambda g, j: (g, 0, 0)),
                  pl.BlockSpec((8, s), lambda g, j: (0, 0))],
        out_specs=[pl.BlockSpec((2, 128, s), lambda g, j: (g, 0, 0)),
                   pl.BlockSpec((2, t, 128), lambda g, j: (g, j, 0)),
                   pl.BlockSpec((t, 128), lambda g, j: (j, g))],
        out_shape=[jax.ShapeDtypeStruct((N_HEADS, 128, s), F32),
                   jax.ShapeDtypeStruct((N_HEADS, s, 128), F32),
                   jax.ShapeDtypeStruct((s, a), BF16)],
        scratch_shapes=[pltpu.VMEM((2, t, 128), F32)],
        compiler_params=_params(("parallel", "arbitrary")),
    )(after, ka, qkv, qa, do, lse, delta)


def _gate_bwd(dqa, dka, dv, fl, bf):
    s = fl.shape[0]
    a = N_HEADS * HEAD_DIM
    tm = min(TILE_ROWS, s)
    nb = s // tm

    def body(dqa_ref, dka_ref, dv_ref, fl_ref, bf_ref, dqkv_ref, dfl_ref, dbf_ref, carry):
        i = pl.program_id(0)

        @pl.when(i == 0)
        def _():
            carry[...] = jnp.zeros_like(carry)
            dbf_ref[...] = jnp.zeros_like(dbf_ref)

        lane = _lane((tm, 128))
        dq_sum = jnp.zeros((tm, 128), F32)
        dk_sum = jnp.zeros((tm, 128), F32)
        for pair in range(N_HEADS // 2):
            qs, ks = [], []
            for e in range(2):
                h = 2 * pair + e
                dq = dqa_ref[h].T
                dk = dka_ref[h]
                dq_sum = dq_sum + dq
                dk_sum = dk_sum + dk
                qs.append(dq * ATTN_SCALE)
                ks.append(dk * (1.0 / LOG2_E))
            cols = slice(pair * 128, (pair + 1) * 128)
            dqkv_ref[:, cols] = jnp.where(
                lane < HEAD_DIM, qs[0], pltpu.roll(qs[1], HEAD_DIM, axis=1)).astype(BF16)
            dqkv_ref[:, a + pair * 128:a + (pair + 1) * 128] = jnp.where(
                lane < HEAD_DIM, ks[0], pltpu.roll(ks[1], HEAD_DIM, axis=1)).astype(BF16)
        dqkv_ref[:, 2 * a:3 * a] = dv_ref[...]

        in_q = (lane >= LANE_CQ) & (lane < LANE_CQ + N_HEADS)
        in_k = (lane >= LANE_CK) & (lane < LANE_CK + N_HEADS)
        dcum = (pltpu.roll(jnp.where(in_q, dq_sum, 0.0), 128 - LANE_CQ, axis=1)
                - pltpu.roll(jnp.where(in_k, dk_sum, 0.0), 128 - LANE_CK, axis=1))

        upper = (_lane((tm, tm)) >= _row((tm, tm))).astype(BF16)
        dlogf = _exact_nn(upper, dcum) + carry[0:1, :]
        carry[0:1, :] = dlogf[0:1, :]
        z = fl_ref[...] + bf_ref[...]
        ez = jnp.exp(-jnp.abs(z))
        sig_neg = jnp.where(z >= 0.0, ez, 1.0) / (1.0 + ez)
        dz = jnp.where(lane < N_HEADS, dlogf * sig_neg, 0.0)
        dfl_ref[...] = dz.astype(BF16)
        dbf_ref[...] += jnp.sum(dz, axis=0, keepdims=True)

    rev3 = lambda i: (0, nb - 1 - i, 0)
    rev = lambda i: (nb - 1 - i, 0)
    return pl.pallas_call(
        body, name="gate_bwd", grid=(nb,),
        in_specs=[pl.BlockSpec((N_HEADS, 128, tm), lambda i: (0, 0, nb - 1 - i)),
                  pl.BlockSpec((N_HEADS, tm, 128), rev3),
                  pl.BlockSpec((tm, a), rev), pl.BlockSpec((tm, 128), rev), _full(bf)],
        out_specs=[pl.BlockSpec((tm, 3 * a), rev), pl.BlockSpec((tm, 128), rev),
                   pl.BlockSpec((1, 128), lambda i: (0, 0))],
        out_shape=[jax.ShapeDtypeStruct((s, 3 * a), BF16),
                   jax.ShapeDtypeStruct((s, 128), BF16),
                   jax.ShapeDtypeStruct((1, 128), F32)],
        scratch_shapes=[pltpu.VMEM((8, 128), F32)],
        compiler_params=_params(("arbitrary",)),
    )(dqa, dka, dv, fl, bf)


def _in_proj_bwd(after, dqkv, dfl, dbcx, w_qkv, w_f, w_bcx, x, g, dh):
    s, d = x.shape
    tm = min(TILE_PROJ_ROWS, s)

    def body(after_ref, dq_ref, df_ref, db_ref, wq_ref, wf_ref, wb_ref, x_ref, g_ref, dh_ref,
             gx_ref, dg_ref):
        i = pl.program_id(0)
        dn = (_nt(dq_ref[...], wq_ref[...]) + _nt(df_ref[...], wf_ref[...])
              + _nt(db_ref[...], wb_ref[...]))
        dx, dg = _rms_bwd(dn, x_ref[...], g_ref[...])
        gx_ref[...] = dh_ref[...] + dx

        @pl.when(i == 0)
        def _():
            dg_ref[...] = dg

        @pl.when(i > 0)
        def _():
            dg_ref[...] += dg

    rows = lambda c: pl.BlockSpec((tm, c), lambda i: (i, 0))
    return pl.pallas_call(
        body, name="in_proj_bwd", grid=(s // tm,),
        in_specs=[ANY, rows(dqkv.shape[1]), rows(dfl.shape[1]), rows(dbcx.shape[1]),
                  _full(w_qkv), _full(w_f), _full(w_bcx), rows(d), _full(g), rows(d)],
        out_specs=[rows(d), pl.BlockSpec((1, d), lambda i: (0, 0))],
        out_shape=[jax.ShapeDtypeStruct((s, d), F32), jax.ShapeDtypeStruct((1, d), F32)],
        compiler_params=_params(("arbitrary",)),
    )(after, dqkv, dfl, dbcx, w_qkv, w_f, w_bcx, x, g, dh)


def _wgrad_in(n, dys):
    s, d = n.shape
    m = len(dys)
    tk = min(TILE_ROWS, s)
    nk = s // tk

    def body(*refs):
        n_ref, dy_refs, dw_refs, accs = refs[0], refs[1:1 + m], refs[1 + m:1 + 2 * m], refs[1 + 2 * m:]
        k = pl.program_id(0)

        @pl.when(k == 0)
        def _():
            for acc in accs:
                acc[...] = jnp.zeros_like(acc)

        nb = n_ref[...]
        for dy_ref, acc in zip(dy_refs, accs):
            acc[...] += _tn(nb, dy_ref[...])

        @pl.when(k == nk - 1)
        def _():
            for dw_ref, acc in zip(dw_refs, accs):
                dw_ref[...] = acc[...].T.astype(BF16)

    return pl.pallas_call(
        body, name="wgrad_in", grid=(nk,),
        in_specs=[pl.BlockSpec((tk, d), lambda k: (k, 0))]
        + [pl.BlockSpec((tk, dy.shape[1]), lambda k: (k, 0)) for dy in dys],
        out_specs=[pl.BlockSpec((dy.shape[1], d), lambda k: (0, 0)) for dy in dys],
        out_shape=[jax.ShapeDtypeStruct((dy.shape[1], d), BF16) for dy in dys],
        scratch_shapes=[pltpu.VMEM((d, dy.shape[1]), F32) for dy in dys],
        compiler_params=_params(("arbitrary",)),
    )(n, *dys)


def _row_tile(rows):
    t = min(TILE_ELEM_ROWS, rows)
    while rows % t:
        t //= 2
    return t


def _adamw_math(w, g, m, v):
    m = ADAM_B1 * m + (1.0 - ADAM_B1) * g
    v = ADAM_B2 * v + (1.0 - ADAM_B2) * jnp.square(g)
    m_hat = m / (1.0 - ADAM_B1 ** ADAM_STEP)
    v_hat = v / (1.0 - ADAM_B2 ** ADAM_STEP)
    delta = -ADAM_LR * (m_hat / (jnp.sqrt(v_hat) + ADAM_EPS) + ADAM_WD * w)
    return delta, m, v


def _adamw(w, g, m, v, name):
    rows, cols = w.shape

    def body(w_ref, g_ref, m_ref, v_ref, d_ref, nm_ref, nv_ref):
        delta, nm, nv = _adamw_math(w_ref[...], g_ref[...], m_ref[...], v_ref[...])
        d_ref[...] = delta
        nm_ref[...] = nm
        nv_ref[...] = nv

    if rows % 8 == 0:
        tr = _row_tile(rows)
        grid, spec = (rows // tr,), pl.BlockSpec((tr, cols), lambda i: (i, 0))
    else:
        grid, spec = (cols // 256,), pl.BlockSpec((rows, 256), lambda i: (0, i))
    out = jax.ShapeDtypeStruct(w.shape, F32)
    return pl.pallas_call(
        body, name=name, grid=grid, in_specs=[spec] * 4, out_specs=[spec] * 3,
        out_shape=[out, out, out], compiler_params=_params(("parallel",)),
    )(w, g, m, v)


def _sum_devices(parts):
    def body(p_ref, g_ref):
        g = p_ref[0]
        for k in range(1, N_DEV):
            g = g + p_ref[k]
        g_ref[...] = g

    return pl.pallas_call(
        body, name="sum_devices",
        in_specs=[pl.BlockSpec(memory_space=pltpu.VMEM)],
        out_specs=pl.BlockSpec(memory_space=pltpu.VMEM),
        out_shape=jax.ShapeDtypeStruct(parts.shape[1:], F32),
    )(parts)


def _mesh_position():
    x, y, c = lax.axis_index("x"), lax.axis_index("y"), lax.axis_index("c")
    chips = [(1 - x, y), (x, 1 - y), (1 - x, 1 - y)]
    return x, y, c, chips


ANY = pl.BlockSpec(memory_space=pl.ANY)
HBM = pl.BlockSpec(memory_space=pltpu.HBM)
SEM = pl.BlockSpec(memory_space=pltpu.SEMAPHORE)
SPLIT_COPY_EFFECT = pltpu.SideEffectType.DATAFLOW_SIDE_EFFECTING


def _in_hbm(a):
    return pltpu.with_memory_space_constraint(a, pltpu.HBM)


def _chip_copies(views, srcs, lands, send, recv, waiting=False):
    _, _, c, chips = _mesh_position()
    cps = []
    for a in range(len(srcs)):
        for k, (px, py) in enumerate(chips):
            src, dst = views(a, k, srcs[a], lands[a], c, 2 * px + py)
            sem = a * (N_CHIPS - 1) + k
            cps.append(pltpu.make_async_remote_copy(
                src_ref=src, dst_ref=dst, send_sem=send.at[sem], recv_sem=recv.at[sem],
                device_id=(px, py, c), device_id_type=MESH))
    return cps


def _ici_start(sources, land_shapes, copies, after, name, per_array=N_CHIPS - 1):
    n = len(sources)

    def body(*refs):
        srcs, lands = refs[:n], refs[n:2 * n]
        send, recv = refs[2 * n + 1], refs[2 * n + 2]
        token = refs[-1]
        for cp in copies(srcs, lands, send, recv, False):
            cp.start()
        token[...] = jnp.zeros_like(token)

    lands = [_in_hbm(lax.empty(s.shape, s.dtype)) for s in land_shapes]
    outs = pl.pallas_call(
        body, name=name,
        in_specs=[HBM] * (2 * n) + [ANY],
        out_specs=[SEM, SEM] + [HBM] * (2 * n) + [pl.BlockSpec(memory_space=pltpu.VMEM)],
        out_shape=[pltpu.SemaphoreType.DMA((n * per_array,))] * 2
        + [pltpu.HBM(a.shape, a.dtype) for a in sources]
        + [pltpu.HBM(s.shape, s.dtype) for s in land_shapes]
        + [jax.ShapeDtypeStruct((8, 128), F32)],
        input_output_aliases={i: 2 + i for i in range(2 * n)},
        compiler_params=pltpu.CompilerParams(has_side_effects=SPLIT_COPY_EFFECT),
    )(*[_in_hbm(a) for a in sources], *lands, after)
    return outs[0], outs[1], list(outs[2:2 + n]), list(outs[2 + n:2 + 2 * n]), outs[-1]


def _ici_wait(handle, copies, after, name):
    send, recv, srcs, lands, _ = handle
    n = len(srcs)

    def body(*refs):
        src_refs, land_refs = refs[:n], refs[n:2 * n]
        for cp in copies(src_refs, land_refs, refs[2 * n], refs[2 * n + 1], True):
            cp.wait_send()
            cp.wait_recv()

    outs = pl.pallas_call(
        body, name=name,
        in_specs=[HBM] * (2 * n) + [SEM, SEM, ANY],
        out_specs=[HBM] * (2 * n),
        out_shape=[pltpu.HBM(a.shape, a.dtype) for a in srcs]
        + [pltpu.HBM(a.shape, a.dtype) for a in lands],
        input_output_aliases={i: i for i in range(2 * n)},
        compiler_params=pltpu.CompilerParams(has_side_effects=SPLIT_COPY_EFFECT),
    )(*srcs, *lands, send, recv, after)
    return list(outs[:n]), list(outs[n:])


def _gather_views(split):
    def views(a, k, src, land, c, slot):
        if split[a]:
            half = src.shape[0] // 2
            src = src.at[pl.ds(c * half, half)]
        return src, land.at[k]
    return views


def _gather_whole_views(a, k, src, land, c, slot):
    x, y, _, _ = _mesh_position()
    return src, land.at[2 * x + y]


SCATTER_COPIES = 2 * (N_CHIPS - 1)


def _scatter_copies(srcs, lands, send, recv, waiting):
    _, _, c, chips = _mesh_position()
    cps = []
    for a in range(len(srcs)):
        half = srcs[a].shape[1] // 2
        for k, (px, py) in enumerate(chips):
            for h in range(2):
                arrival = 2 * k + (h if waiting else c)
                cps.append(pltpu.make_async_remote_copy(
                    src_ref=srcs[a].at[2 * px + py, pl.ds(h * half, half)],
                    dst_ref=lands[a].at[arrival],
                    send_sem=send.at[a * SCATTER_COPIES + 2 * k + h],
                    recv_sem=recv.at[a * SCATTER_COPIES + arrival],
                    device_id=(px, py, h), device_id_type=MESH))
    return cps


def _gather_land_shapes(shards, split):
    return [jax.ShapeDtypeStruct(
        (N_CHIPS - 1, a.shape[0] // 2 if sp else a.shape[0]) + a.shape[1:], a.dtype)
        for a, sp in zip(shards, split)]


def _gather_finish(shards, lands, split, name):
    n = len(shards)
    ns = sum(split)
    d_index = {a: i for i, a in enumerate(a for a in range(n) if split[a])}

    def body(*refs):
        shard, land, outs = refs[:n], refs[n:2 * n], refs[2 * n:3 * n]
        obuf, fbuf = refs[3 * n:4 * n], refs[4 * n:5 * n]
        dbuf = refs[5 * n:5 * n + ns]
        ld_own, st_own, ld, st_mine, st_sib, send, recv = refs[5 * n + ns:]
        x, y, c, chips = _mesh_position()
        me = 2 * x + y
        own_loads, loads, sends, pending = [], {}, [], []
        for a in range(n):
            cp = pltpu.make_async_copy(shard[a], obuf[a], ld_own.at[a])
            cp.start()
            own_loads.append(cp)
        for a in range(n):
            for k in range(N_CHIPS - 1):
                cp = pltpu.make_async_copy(land[a].at[k], fbuf[a].at[k], ld.at[a, k])
                cp.start()
                loads[a, k] = cp
        for a in range(n):
            own_loads[a].wait()
            cp = pltpu.make_async_copy(obuf[a], outs[a].at[me], st_own.at[a])
            cp.start()
            pending.append(cp)
        for a in range(n):
            rows = shard[a].shape[0]
            for k, (px, py) in enumerate(chips):
                loads[a, k].wait()
                part = pl.ds(c * (rows // 2), rows // 2) if split[a] else pl.ds(0, rows)
                cp = pltpu.make_async_copy(fbuf[a].at[k], outs[a].at[2 * px + py, part],
                                           st_mine.at[a, k])
                cp.start()
                pending.append(cp)
                if split[a]:
                    fw = pltpu.make_async_remote_copy(
                        src_ref=fbuf[a].at[k], dst_ref=dbuf[d_index[a]].at[k],
                        send_sem=send.at[a, k], recv_sem=recv.at[a, k],
                        device_id=(x, y, 1 - c), device_id_type=MESH)
                    fw.start()
                    sends.append((a, k, fw))
        for a, k, fw in sends:
            px, py = chips[k]
            half = shard[a].shape[0] // 2
            fw.wait_recv()
            cp = pltpu.make_async_copy(dbuf[d_index[a]].at[k],
                                       outs[a].at[2 * px + py, pl.ds((1 - c) * half, half)],
                                       st_sib.at[a, k])
            cp.start()
            pending.append(cp)
        for _, _, fw in sends:
            fw.wait_send()
        for cp in pending:
            cp.wait()

    stage = [pltpu.VMEM(a.shape, a.dtype) for a in lands]
    dma = lambda *shape: pltpu.SemaphoreType.DMA(shape)
    return pl.pallas_call(
        body, name=name,
        in_specs=[ANY] * (2 * n), out_specs=[ANY] * n,
        out_shape=[jax.ShapeDtypeStruct((N_CHIPS,) + a.shape, a.dtype) for a in shards],
        scratch_shapes=[pltpu.VMEM(a.shape, a.dtype) for a in shards] + stage
        + [s for s, sp in zip(stage, split) if sp]
        + [dma(n), dma(n), dma(n, 3), dma(n, 3), dma(n, 3), dma(n, 3), dma(n, 3)],
        compiler_params=pltpu.CompilerParams(vmem_limit_bytes=VMEM_LIMIT_BYTES),
    )(*shards, *lands)


def _sum_chunk(rows):
    return next(r for r in range(SUM_CHUNK_ROWS, 0, -16) if rows % r == 0)


def _sum_and_share(partials, lands, name):
    n = len(partials)

    def body(*refs):
        own, landed, outs = refs[:n], refs[n:2 * n], refs[2 * n:3 * n]
        obuf, xbuf, ybuf, gbuf, sbuf, rbuf = (refs[(3 + k) * n:(4 + k) * n] for k in range(6))
        ld_own, ld_send, ld_got, st_own, st_sib, send_p, recv_p, send_s, recv_s = refs[9 * n:]
        x, y, c, _ = _mesh_position()
        me = 2 * x + y
        sibling = (x, y, 1 - c)

        def to_sibling(src, dst, send, recv, a):
            return pltpu.make_async_remote_copy(src_ref=src, dst_ref=dst, send_sem=send.at[a],
                                                recv_sem=recv.at[a], device_id=sibling,
                                                device_id_type=MESH)

        loads, firsts, seconds, stores = [], [], [], []
        for a in range(n):
            half = obuf[a].shape[0]
            cps = [pltpu.make_async_copy(own[a].at[me, pl.ds((1 - c) * half, half)], xbuf[a],
                                         ld_send.at[a]),
                   pltpu.make_async_copy(own[a].at[me, pl.ds(c * half, half)], obuf[a], ld_own.at[a]),
                   pltpu.make_async_copy(landed[a], gbuf[a], ld_got.at[a])]
            for cp in cps:
                cp.start()
            loads.append(cps)
        for a in range(n):
            loads[a][0].wait()
            rc = to_sibling(xbuf[a], ybuf[a], send_p, recv_p, a)
            rc.start()
            firsts.append(rc)
        for a in range(n):
            firsts[a].wait_recv()
            loads[a][1].wait()
            loads[a][2].wait()
            half = obuf[a].shape[0]
            rows = _sum_chunk(half)

            def add(k, carry, a=a, rows=rows):
                at = pl.ds(pl.multiple_of(k * rows, rows), rows)
                acc = obuf[a][at].astype(F32) + ybuf[a][at].astype(F32)
                for j in range(SCATTER_COPIES):
                    acc = acc + gbuf[a][j, at].astype(F32)
                sbuf[a][at] = acc
                return carry

            lax.fori_loop(0, half // rows, add, 0)
            rc = to_sibling(sbuf[a], rbuf[a], send_s, recv_s, a)
            rc.start()
            seconds.append(rc)
            cp = pltpu.make_async_copy(sbuf[a], outs[a].at[pl.ds(c * half, half)], st_own.at[a])
            cp.start()
            stores.append(cp)
        for a in range(n):
            half = obuf[a].shape[0]
            seconds[a].wait_recv()
            cp = pltpu.make_async_copy(rbuf[a], outs[a].at[pl.ds((1 - c) * half, half)], st_sib.at[a])
            cp.start()
            stores.append(cp)
        for rc in firsts + seconds:
            rc.wait_send()
        for cp in stores:
            cp.wait()

    halves = [(a.shape[1] // 2, a.shape[2]) for a in partials]
    return pl.pallas_call(
        body, name=name,
        in_specs=[ANY] * (2 * n), out_specs=[ANY] * n,
        out_shape=[jax.ShapeDtypeStruct((2 * h[0], h[1]), F32) for h in halves],
        scratch_shapes=[pltpu.VMEM(h, BF16) for h in halves] * 3
        + [pltpu.VMEM(g.shape, BF16) for g in lands]
        + [pltpu.VMEM(h, F32) for h in halves] * 2
        + [pltpu.SemaphoreType.DMA((n,))] * 9,
        compiler_params=pltpu.CompilerParams(vmem_limit_bytes=VMEM_LIMIT_BYTES),
    )(*partials, *lands)


def _gather_small(part):
    def body(in_ref, out_ref, send, recv, local):
        x, y, c, _ = _mesh_position()
        me = 4 * x + 2 * y + c
        cps = [pltpu.make_async_copy(in_ref, out_ref.at[me], local)]
        k = 0
        for fx in range(2):
            for fy in range(2):
                for fc in range(2):
                    if fx or fy or fc:
                        cps.append(pltpu.make_async_remote_copy(
                            src_ref=in_ref, dst_ref=out_ref.at[me], send_sem=send.at[k],
                            recv_sem=recv.at[k], device_id=(x ^ fx, y ^ fy, c ^ fc),
                            device_id_type=MESH))
                        k += 1
        for cp in cps:
            cp.start()
        for cp in cps:
            cp.wait()

    return pl.pallas_call(
        body, name="gather_small",
        in_specs=[pl.BlockSpec(memory_space=pltpu.VMEM)],
        out_specs=pl.BlockSpec(memory_space=pltpu.VMEM),
        out_shape=jax.ShapeDtypeStruct((N_DEV,) + part.shape, part.dtype),
        scratch_shapes=[pltpu.SemaphoreType.DMA((N_DEV - 1,)), pltpu.SemaphoreType.DMA((N_DEV - 1,)),
                        pltpu.SemaphoreType.DMA],
    )(part)


def _scatter_start(grads, after, tag):
    lands = [jax.ShapeDtypeStruct((SCATTER_COPIES, g.shape[1] // 2, g.shape[2]), g.dtype)
             for g in grads]
    return _ici_start(grads, lands, _scatter_copies, after, "scatter_start_" + tag,
                      per_array=SCATTER_COPIES)


def _scatter_finish(handle, after, tag):
    grads, lands = _ici_wait(handle, _scatter_copies, after, "scatter_wait_" + tag)
    return _sum_and_share(grads, lands, "sum_and_share_" + tag)


def _pad_rows(a, rows):
    return jnp.pad(a, ((0, rows - a.shape[0]), (0, 0)))


def kernel(x, norm_mix_0, w_in_0, b_f_0, conv_w_0, w_out_0, norm_ffn_0, w_up_0, w_down_0, norm_mix_1, pool_w_1, pool_scale_1, norm_ffn_1, w_up_1, w_down_1, final_norm, loss_target, m_norm_mix_0, m_w_in_0, m_b_f_0, m_conv_w_0, m_w_out_0, m_norm_ffn_0, m_w_up_0, m_w_down_0, m_norm_mix_1, m_pool_w_1, m_pool_scale_1, m_norm_ffn_1, m_w_up_1, m_w_down_1, m_final_norm, v_norm_mix_0, v_w_in_0, v_b_f_0, v_conv_w_0, v_w_out_0, v_norm_ffn_0, v_w_up_0, v_w_down_0, v_norm_mix_1, v_pool_w_1, v_pool_scale_1, v_norm_ffn_1, v_w_up_1, v_w_down_1, v_final_norm):
    d = x.shape[-1]
    a = N_HEADS * HEAD_DIM
    c_conv = conv_w_0.shape[1] * N_CHIPS
    xs = x[0]
    target = loss_target[0]
    row = lambda vec: vec.reshape(1, -1)

    big = [w_in_0, w_out_0, w_up_0, w_down_0, pool_w_1, w_up_1, w_down_1]
    first = [w_in_0.astype(BF16)]
    first_split = [True]
    copies_a = functools.partial(_chip_copies, _gather_views(first_split))
    copies_b = functools.partial(_chip_copies, _gather_whole_views)
    start_a = _ici_start(first, _gather_land_shapes(first, first_split), copies_a, b_f_0,
                         "gather_start_a")
    zero = start_a[-1][0, 0]
    rest = [(w + zero).astype(BF16)
            for w in (w_out_0, w_up_0, w_down_0, pool_w_1, w_up_1, w_down_1)]
    rest = rest + [conv_w_0]
    start_b = _ici_start(rest, [jax.ShapeDtypeStruct((N_CHIPS,) + w.shape, w.dtype) for w in rest],
                         copies_b, start_a[-1], "gather_start_b")
    n0 = _rms_pre(start_b[-1], xs, row(norm_mix_0))
    first, land_a = _ici_wait(start_a, copies_a, n0, "gather_wait_a")
    (g_in,) = _gather_finish(first, land_a, first_split, "gather_finish_a")
    w_in = g_in.transpose(1, 0, 2).reshape(d, -1)
    w_qkv = w_in[:, :3 * a]
    w_f = jnp.pad(w_in[:, 3 * a:3 * a + N_HEADS], ((0, 0), (0, 128 - N_HEADS)))
    w_bcx = w_in[:, 3 * a + N_HEADS:]
    bf = jnp.pad(b_f_0, (0, 128 - N_HEADS)).reshape(1, 128)

    qkv, fl, bcx = _in_proj(n0, w_qkv, w_f, w_bcx)
    qa, ka = _gate_prep(fl, bf, qkv)
    o, lse = _attn_fwd(qa, ka, qkv)
    rest, land_b = _ici_wait(start_b, copies_b, o, "gather_wait_b")
    own_slot = 2 * lax.axis_index("x") + lax.axis_index("y")
    g_out, g_up0, g_down0, g_pool, g_up1, g_down1, g_conv = [
        lax.dynamic_update_index_in_dim(land, shard, own_slot, 0)
        for land, shard in zip(land_b, rest)]
    w_out = g_out.reshape(-1, d)
    conv_w = _pad_rows(g_conv.transpose(1, 0, 2).reshape(conv_w_0.shape[0], c_conv), 8)
    h1 = _conv_out(o, bcx, conv_w, w_out, xs)
    w_down0 = g_down0.reshape(-1, d)
    w_down1 = g_down1.reshape(-1, d)
    pool_w = g_pool.transpose(1, 0, 2, 3).reshape(pool_w_1.shape[0], -1, pool_w_1.shape[2])
    h2, a0, nf0 = _mlp_fwd(h1, row(norm_ffn_0), g_up0, w_down0, "mlp_fwd_0")
    h3 = _pool_fwd(h2, row(norm_mix_1), pool_w, row(pool_scale_1))
    dh4, a1, nf1, loss_part, d_final = _mlp_fwd(h3, row(norm_ffn_1), g_up1, w_down1, "mlp_fwd_1",
                                                head=(row(final_norm), target))

    slot_cols = g_up0.shape[2]
    pool_cols = pool_w.shape[2]
    da1, dz1, dh3, d_nffn1 = _mlp_bwd_x(dh4, a1, g_up1, w_down1, h3, row(norm_ffn_1), "mlp_bwd_x_1")
    dw_up1, dw_down1 = _mlp_bwd_w(nf1, da1, a1, dz1, slot_cols, "mlp_bwd_w_1")
    scatter_1 = _scatter_start([dw_up1, dw_down1.reshape(N_CHIPS, -1, d)], bf, "mlp1")
    dh2, dw_pool, d_pscale, d_nmix1 = _pool_bwd(scatter_1[-1], dh3, h2, row(norm_mix_1), pool_w,
                                                row(pool_scale_1))
    da0, dz0, dh1, d_nffn0 = _mlp_bwd_x(dh2, a0, g_up0, w_down0, h1, row(norm_ffn_0), "mlp_bwd_x_0")
    dw_up0, dw_down0 = _mlp_bwd_w(nf0, da0, a0, dz0, slot_cols, "mlp_bwd_w_0")
    dw_pool = (dw_pool.reshape(pool_w.shape[0], N_CHIPS, -1, pool_cols).transpose(1, 0, 2, 3)
               .reshape(N_CHIPS, -1, pool_cols))
    scatter_0 = _scatter_start([dw_up0, dw_down0.reshape(N_CHIPS, -1, d), dw_pool], bf, "mlp0")
    do, delta, dbcx, dw_out, d_conv = _conv_out_bwd(scatter_0[-1], dh1, w_out, o, bcx, conv_w)
    scatter_o = _scatter_start([dw_out.reshape(N_CHIPS, -1, d)], bf, "out")
    dqa, dka, dv = _attn_bwd(scatter_o[-1], qa, ka, qkv, do, lse, delta)
    dqkv, dfl, d_bf = _gate_bwd(dqa, dka, dv, fl, bf)
    dw_qkv, dw_f, dw_bcx = _wgrad_in(n0, [dqkv, dfl, dbcx])
    dw_in = jnp.concatenate([dw_qkv, dw_f[:N_HEADS], dw_bcx], axis=0).reshape(N_CHIPS, -1, d)
    slot_rows = -(-dw_in.shape[1] // 32) * 32
    dw_in = jnp.pad(dw_in, ((0, 0), (0, slot_rows - dw_in.shape[1]), (0, 0)))
    scatter_m = _scatter_start([dw_in], bf, "mixer")
    grad_x, d_nmix0 = _in_proj_bwd(scatter_m[-1], dqkv, dfl, dbcx, w_qkv, w_f, w_bcx, xs,
                                   row(norm_mix_0), dh1)

    r_up1, r_down1 = _scatter_finish(scatter_1, grad_x, "mlp1")
    r_up0, r_down0, r_pool = _scatter_finish(scatter_0, grad_x, "mlp0")
    (r_out,) = _scatter_finish(scatter_o, grad_x, "out")
    (r_in,) = _scatter_finish(scatter_m, grad_x, "mixer")
    reduced = [r_in, r_out, r_up0, r_down0, r_pool, r_up1, r_down1]
    moments = [(m_w_in_0, v_w_in_0), (m_w_out_0, v_w_out_0), (m_w_up_0, v_w_up_0),
               (m_w_down_0, v_w_down_0), (m_pool_w_1, v_pool_w_1), (m_w_up_1, v_w_up_1),
               (m_w_down_1, v_w_down_1)]
    big_out = []
    for k, (w, g, (m, v)) in enumerate(zip(big, reduced, moments)):
        if w.shape[-1] % 128:
            view = lambda t: t.reshape(-1, t.shape[-1]).T
            back = lambda t: t.T.reshape(w.shape)
            g_view = g[:w.shape[-1]]
        else:
            view = lambda t: t.reshape(-1, t.shape[-1])
            back = lambda t: t.reshape(w.shape)
            g_view = view(g)
        delta_w, new_m, new_v = _adamw(view(w), g_view, view(m), view(v), "adamw_%d" % k)
        big_out.append((back(g_view), back(delta_w), back(new_m), back(new_v)))

    tail = jnp.concatenate([d_conv[0:3].reshape(-1)[d:], d_bf[0, :N_HEADS], loss_part[0, :1]])
    small_part = jnp.concatenate(
        [d_nmix0, d_nffn0, d_nmix1, d_pscale, d_nffn1, d_final,
         d_conv[0:3].reshape(1, -1)[:, :d],
         jnp.pad(tail, (0, d - tail.shape[0])).reshape(1, d)], axis=0)
    parts = _gather_small(small_part)

    chip = 2 * lax.axis_index("x") + lax.axis_index("y")
    cw_cols = conv_w_0.shape[1]

    def conv_block(full):
        mine = lax.dynamic_slice_in_dim(full, chip * cw_cols, cw_cols, axis=1)
        return jnp.pad(mine.reshape(-1), (0, d - mine.size))

    def small_rows(vals, cw, bfv):
        return jnp.stack(list(vals) + [cw, jnp.pad(bfv, (0, d - N_HEADS))])

    smalls_w = [norm_mix_0, norm_ffn_0, norm_mix_1, pool_scale_1, norm_ffn_1, final_norm]
    smalls_m = [m_norm_mix_0, m_norm_ffn_0, m_norm_mix_1, m_pool_scale_1, m_norm_ffn_1, m_final_norm]
    smalls_v = [v_norm_mix_0, v_norm_ffn_0, v_norm_mix_1, v_pool_scale_1, v_norm_ffn_1, v_final_norm]
    pad_cw = lambda t: jnp.pad(t.reshape(-1), (0, d - t.size))
    w_rows = small_rows(smalls_w, pad_cw(conv_w_0), b_f_0)
    m_rows = small_rows(smalls_m, pad_cw(m_conv_w_0), m_b_f_0)
    v_rows = small_rows(smalls_v, pad_cw(v_conv_w_0), v_b_f_0)

    g_sum = _sum_devices(parts)
    conv_full = jnp.concatenate([g_sum[6], g_sum[7, :3 * c_conv - d]]).reshape(3, c_conv)
    bf_grad = g_sum[7, 3 * c_conv - d:3 * c_conv - d + N_HEADS]
    loss = g_sum[7, 3 * c_conv - d + N_HEADS]
    g_rows = jnp.concatenate(
        [g_sum[0:6], conv_block(conv_full).reshape(1, d),
         jnp.pad(bf_grad, (0, d - N_HEADS)).reshape(1, d)], axis=0)
    d_rows, nm_rows, nv_rows = _adamw(w_rows, g_rows, m_rows, v_rows, "adamw_small")

    def unpack(rows):
        cw = rows[6, :conv_w_0.size].reshape(conv_w_0.shape)
        return [rows[0], rows[1], rows[2], rows[3], rows[4], rows[5], cw, rows[7, :N_HEADS]]

    def assemble(kind):
        sm = unpack([g_rows, d_rows, nm_rows, nv_rows][kind])
        lg = [t[kind] for t in big_out]
        return [sm[0], lg[0], sm[7], sm[6], lg[1], sm[1], lg[2], lg[3],
                sm[2], lg[4], sm[3], sm[4], lg[5], lg[6], sm[5]]

    return (loss, grad_x[None], *assemble(0), *assemble(1), *assemble(2), *assemble(3))
```

```python
import functools

import jax
import jax.numpy as jnp
from jax import lax
from jax.experimental import pallas as pl
from jax.experimental.pallas import tpu as pltpu

F32 = jnp.float32
BF16 = jnp.bfloat16

RMS_EPS = 1e-6
HEAD_DIM = 64
N_HEADS = 8
ATTN_SCALE = HEAD_DIM ** -0.5
LOG2_E = 1.4426950408889634
POOL_WINDOWS = (2, 4, 8, 16)
POOL_HALO = 16
CONV_HALO = 8
NEG_BIG = -1e30

ADAM_LR = 0.001
ADAM_B1 = 0.9
ADAM_B2 = 0.999
ADAM_EPS = 1e-08
ADAM_WD = 0.01
ADAM_STEP = 10

N_CHIPS = 4
N_DEV = 8
MESH = pl.DeviceIdType.MESH

VMEM_LIMIT_BYTES = 56 * 1024 * 1024

TILE_ROWS = 512
TILE_PROJ_ROWS = 1024
TILE_ATTN = 512
TILE_MLP_ROWS = 1024
TILE_MLP_FF = 1024
TILE_MLP_BWD_FF = 512
FF_SLAB = TILE_MLP_BWD_FF
TILE_HEAD_ROWS = 256
TILE_WGRAD_K = 1024
TILE_WGRAD_N = 1024
TILE_ELEM_ROWS = 256
SUM_CHUNK_ROWS = 128

LANE_CQ = 64
LANE_CK = 88


def _params(semantics):
    return pltpu.CompilerParams(dimension_semantics=semantics,
                                vmem_limit_bytes=VMEM_LIMIT_BYTES)


def _nn(a, b):
    return lax.dot_general(a, b, (((1,), (0,)), ((), ())), preferred_element_type=F32)


def _nt(a, b):
    return lax.dot_general(a, b, (((1,), (1,)), ((), ())), preferred_element_type=F32)


def _tn(a, b):
    return lax.dot_general(a, b, (((0,), (0,)), ((), ())), preferred_element_type=F32)


def _split3(v):
    hi = v.astype(BF16)
    r1 = v - hi.astype(F32)
    mid = r1.astype(BF16)
    lo = (r1 - mid.astype(F32)).astype(BF16)
    return hi, mid, lo


def _exact_nn(sel, v):
    hi, mid, lo = _split3(v)
    return _nn(sel, hi) + _nn(sel, mid) + _nn(sel, lo)


def _exact_nt(sel, v):
    hi, mid, lo = _split3(v)
    return _nt(sel, hi) + _nt(sel, mid) + _nt(sel, lo)


def _rms_fwd(x, g):
    r = lax.rsqrt(jnp.mean(x * x, axis=-1, keepdims=True) + RMS_EPS)
    return x * r * g, r


def _rms_bwd(dn, x, g):
    r = lax.rsqrt(jnp.mean(x * x, axis=-1, keepdims=True) + RMS_EPS)
    xh = x * r
    gy = dn * g
    dx = r * (gy - xh * jnp.mean(gy * xh, axis=-1, keepdims=True))
    return dx, jnp.sum(dn * xh, axis=0, keepdims=True)


def _lane(shape):
    return lax.broadcasted_iota(jnp.int32, shape, len(shape) - 1)


def _row(shape):
    return lax.broadcasted_iota(jnp.int32, shape, len(shape) - 2)


def _full(a):
    nd = a.ndim
    return pl.BlockSpec(a.shape, lambda *_: (0,) * nd)


def _rms_pre(after, x, g):
    s, d = x.shape
    tm = min(TILE_ROWS, s)

    def body(after_ref, x_ref, g_ref, n_ref):
        n, _ = _rms_fwd(x_ref[...], g_ref[...])
        n_ref[...] = n.astype(BF16)

    rows = pl.BlockSpec((tm, d), lambda i: (i, 0))
    return pl.pallas_call(
        body, name="rms_pre", grid=(s // tm,),
        in_specs=[ANY, rows, _full(g)], out_specs=rows,
        out_shape=jax.ShapeDtypeStruct((s, d), BF16),
        compiler_params=_params(("parallel",)),
    )(after, x, g)


def _in_proj(n, w_qkv, w_f, w_bcx):
    s, d = n.shape
    tm = min(TILE_PROJ_ROWS, s)

    def body(n_ref, wq_ref, wf_ref, wb_ref, qkv_ref, fl_ref, bcx_ref):
        nb = n_ref[...]
        qkv_ref[...] = _nn(nb, wq_ref[...]).astype(BF16)
        fl_ref[...] = _nn(nb, wf_ref[...])
        bcx_ref[...] = _nn(nb, wb_ref[...])

    rows = lambda c: pl.BlockSpec((tm, c), lambda i: (i, 0))
    return pl.pallas_call(
        body, name="in_proj", grid=(s // tm,),
        in_specs=[rows(d), _full(w_qkv), _full(w_f), _full(w_bcx)],
        out_specs=[rows(w_qkv.shape[1]), rows(w_f.shape[1]), rows(w_bcx.shape[1])],
        out_shape=[jax.ShapeDtypeStruct((s, w_qkv.shape[1]), BF16),
                   jax.ShapeDtypeStruct((s, w_f.shape[1]), F32),
                   jax.ShapeDtypeStruct((s, w_bcx.shape[1]), F32)],
        compiler_params=_params(("parallel",)),
    )(n, w_qkv, w_f, w_bcx)


def _gate_prep(fl, bf, qkv):
    s = fl.shape[0]
    a = N_HEADS * HEAD_DIM
    tm = min(TILE_ROWS, s)

    def body(fl_ref, bf_ref, q_ref, k_ref, qa_ref, ka_ref, carry_ref):
        i = pl.program_id(0)

        @pl.when(i == 0)
        def _():
            carry_ref[...] = jnp.zeros_like(carry_ref)

        z = fl_ref[...] + bf_ref[...]
        logf = jnp.minimum(z, 0.0) - jnp.log(1.0 + jnp.exp(-jnp.abs(z)))
        lower = (_lane((tm, tm)) <= _row((tm, tm))).astype(BF16)
        cum = _exact_nn(lower, logf) + carry_ref[0:1, :]
        carry_ref[0:1, :] = cum[tm - 1:tm, :]

        lane = _lane((tm, 128))
        pieces = [p.astype(F32)
                  for p in _split3(jnp.where(lane < N_HEADS, LOG2_E * cum, 0.0))]
        shared_q = sum(pltpu.roll(p, LANE_CQ + N_HEADS * k, axis=1) for k, p in enumerate(pieces))
        shared_k = -sum(pltpu.roll(p, LANE_CK + N_HEADS * k, axis=1) for k, p in enumerate(pieces))
        for h in range(N_HEADS):
            at_q = functools.reduce(jnp.logical_or,
                                    [lane == LANE_CQ + N_HEADS * k + h for k in range(3)])
            at_k = functools.reduce(jnp.logical_or,
                                    [lane == LANE_CK + N_HEADS * k + h for k in range(3)])
            pair = slice((h // 2) * 128, (h // 2 + 1) * 128)
            qp = q_ref[:, pair].astype(F32)
            kp = k_ref[:, pair].astype(F32)
            if h % 2:
                qp = pltpu.roll(qp, HEAD_DIM, axis=1)
                kp = pltpu.roll(kp, HEAD_DIM, axis=1)
            q_bias = jnp.where(at_k, 1.0, shared_q)
            k_bias = jnp.where(at_q, 1.0, shared_k)
            qa_ref[h] = jnp.where(lane < HEAD_DIM, qp * (ATTN_SCALE * LOG2_E), q_bias).astype(BF16)
            ka_ref[h] = jnp.where(lane < HEAD_DIM, kp, k_bias).astype(BF16)

    aug = jax.ShapeDtypeStruct((N_HEADS, s, 128), BF16)
    aug_spec = pl.BlockSpec((N_HEADS, tm, 128), lambda i: (0, i, 0))
    return pl.pallas_call(
        body, name="gate_prep", grid=(s // tm,),
        in_specs=[pl.BlockSpec((tm, 128), lambda i: (i, 0)), _full(bf),
                  pl.BlockSpec((tm, a), lambda i: (i, 0)),
                  pl.BlockSpec((tm, a), lambda i: (i, 1))],
        out_specs=[aug_spec, aug_spec],
        out_shape=[aug, aug],
        scratch_shapes=[pltpu.VMEM((8, 128), F32)],
        compiler_params=_params(("arbitrary",)),
    )(fl, bf, qkv, qkv)


def _attn_fwd(qa, ka, qkv):
    s = qa.shape[1]
    a = N_HEADS * HEAD_DIM
    t = min(TILE_ATTN, s)
    n_pairs = N_HEADS // 2
    v_block0 = 2 * a // 128
    ones_lane = (HEAD_DIM, 0)

    def body(qa_ref, ka_ref, v_ref, o_ref, lse_ref, m_ref, acc_ref, s_even, s_odd):
        i = pl.program_id(1)
        m_ref[...] = jnp.full_like(m_ref, NEG_BIG)
        acc_ref[...] = jnp.zeros_like(acc_ref)
        upper_rows = _row((128, t)) < HEAD_DIM

        def keys(j):
            return pl.ds(pl.multiple_of(j * t, t), t)

        def scores_into(buf, j):
            for e in range(2):
                buf[e] = _nt(ka_ref[e, keys(j), :], qa_ref[e])

        def consume(buf, j, masked):
            vf = v_ref[keys(j), :].astype(F32)
            lane = _lane((t, 128))
            own = [lane < HEAD_DIM, lane >= HEAD_DIM]
            for e in range(2):
                v_head = jnp.where(own[e], vf, jnp.where(lane == ones_lane[e], 1.0, 0.0)).astype(BF16)
                sc = buf[e]
                if masked:
                    sc = jnp.where(_row((t, t)) <= _lane((t, t)), sc, NEG_BIG)
                m_prev = m_ref[e]
                m_new = jnp.maximum(m_prev, jnp.max(sc, axis=0, keepdims=True))
                p = jnp.exp2(sc - m_new).astype(BF16)
                acc_ref[e] = acc_ref[e] * jnp.exp2(m_prev - m_new) + _tn(v_head, p)
                m_ref[e] = m_new

        scores_into(s_even, 0)

        def two_tiles(p, carry):
            j = 2 * p
            scores_into(s_odd, j + 1)
            consume(s_even, j, False)
            scores_into(s_even, j + 2)
            consume(s_odd, j + 1, False)
            return carry

        lax.fori_loop(0, i // 2, two_tiles, 0)

        @pl.when(i % 2 == 0)
        def _():
            consume(s_even, i, True)

        @pl.when(i % 2 == 1)
        def _():
            scores_into(s_odd, i)
            consume(s_even, i - 1, False)
            consume(s_odd, i, True)

        denom = [acc_ref[e, ones_lane[e]:ones_lane[e] + 1, :] for e in range(2)]
        out_t = jnp.where(upper_rows, acc_ref[0] / denom[0], acc_ref[1] / denom[1])
        o_ref[...] = out_t.T.astype(BF16)
        lse = [m_ref[e] + LOG2_E * jnp.log(denom[e]) for e in range(2)]
        lse_ref[...] = jnp.where(_row((8, t)) == 0, lse[0], lse[1])

    return pl.pallas_call(
        body, name="attn_fwd", grid=(n_pairs, s // t),
        in_specs=[pl.BlockSpec((2, t, 128), lambda g, i: (g, i, 0)),
                  pl.BlockSpec((2, s, 128), lambda g, i: (g, 0, 0)),
                  pl.BlockSpec((s, 128), lambda g, i: (0, v_block0 + g))],
        out_specs=[pl.BlockSpec((t, 128), lambda g, i: (i, g)),
                   pl.BlockSpec((None, 8, t), lambda g, i: (g, 0, i))],
        out_shape=[jax.ShapeDtypeStruct((s, a), BF16),
                   jax.ShapeDtypeStruct((n_pairs, 8, s), F32)],
        scratch_shapes=[pltpu.VMEM((2, 1, t), F32), pltpu.VMEM((2, 128, t), F32),
                        pltpu.VMEM((2, t, t), F32), pltpu.VMEM((2, t, t), F32)],
        compiler_params=_params(("parallel", "arbitrary")),
    )(qa, ka, qkv)


def _conv_out(o, bcx, cw, w_out, x):
    s, d = x.shape
    c = o.shape[1]
    tm = min(TILE_ROWS, s)

    def body(o_ref, b_ref, c_ref, xin_ref, cw_ref, w_ref, x_ref, h_ref, ubuf):
        i = pl.program_id(0)

        @pl.when(i == 0)
        def _():
            ubuf[0:CONV_HALO, :] = jnp.zeros((CONV_HALO, c), F32)

        u = c_ref[...] * xin_ref[...]
        ubuf[CONV_HALO:CONV_HALO + tm, :] = u
        u1 = ubuf[CONV_HALO - 1:CONV_HALO - 1 + tm, :]
        u2 = ubuf[CONV_HALO - 2:CONV_HALO - 2 + tm, :]
        cv = (cw_ref[0:1, :] * u2 + cw_ref[1:2, :] * u1) + cw_ref[2:3, :] * u
        y = (b_ref[...] * cv).astype(BF16)
        mix = _nn(o_ref[...], w_ref[0:c, :]) + _nn(y, w_ref[c:2 * c, :])
        h_ref[...] = x_ref[...] + mix
        ubuf[0:CONV_HALO, :] = u[tm - CONV_HALO:tm, :]

    col = lambda k: pl.BlockSpec((tm, c), lambda i: (i, k))
    return pl.pallas_call(
        body, name="conv_out", grid=(s // tm,),
        in_specs=[col(0), col(0), col(1), col(2), _full(cw), _full(w_out),
                  pl.BlockSpec((tm, d), lambda i: (i, 0))],
        out_specs=pl.BlockSpec((tm, d), lambda i: (i, 0)),
        out_shape=jax.ShapeDtypeStruct((s, d), F32),
        scratch_shapes=[pltpu.VMEM((tm + CONV_HALO, c), F32)],
        compiler_params=_params(("arbitrary",)),
    )(o, bcx, bcx, bcx, cw, w_out, x)


def _mlp_fwd(h, g, w_up, w_down, name, head=None):
    s, d = h.shape
    ff = w_down.shape[0]
    slot_cols = w_up.shape[2]
    tm = min(TILE_MLP_ROWS, s)
    tf = min(TILE_MLP_FF, slot_cols)
    per_slot = slot_cols // tf
    nf = ff // tf
    n_head = 0 if head is None else 2
    chunk = min(TILE_HEAD_ROWS, tm)

    def body(*refs):
        h_ref, g_ref, wu_ref, wd_ref = refs[:4]
        out_ref, a_ref, n_ref = refs[4 + n_head:7 + n_head]
        nb_ref, acc_ref = refs[9 + n_head:11 + n_head] if head else refs[-2:]
        i = pl.program_id(0)
        f = pl.program_id(1)

        def target_copy():
            t_hbm, t_buf, t_sem = refs[5], refs[-2], refs[-1]
            return pltpu.make_async_copy(t_hbm.at[pl.ds(pl.multiple_of(i * tm, tm), tm), :],
                                         t_buf, t_sem)

        @pl.when(f == 0)
        def _():
            n, _ = _rms_fwd(h_ref[...], g_ref[...])
            nb = n.astype(BF16)
            nb_ref[...] = nb
            n_ref[...] = nb
            acc_ref[...] = jnp.zeros_like(acc_ref)
            if head is not None:
                target_copy().start()

        pre = _nn(nb_ref[...], wu_ref[...])
        for k in range(tf // FF_SLAB):
            a_ref[k] = pre[:, k * FF_SLAB:(k + 1) * FF_SLAB].astype(BF16)
        r = jnp.square(jnp.maximum(pre, 0.0)).astype(BF16)
        acc_ref[...] += _nn(r, wd_ref[...])

        @pl.when(f == nf - 1)
        def _():
            if head is None:
                out_ref[...] = h_ref[...] + acc_ref[...]
            else:
                gf_ref, t_buf = refs[4], refs[-2]
                loss_ref, dg_ref = refs[7 + n_head:9 + n_head]
                target_copy().wait()
                part, dg = None, None
                for r0 in range(0, tm, chunk):
                    rows_ = slice(r0, r0 + chunk)
                    out = h_ref[rows_, :] + acc_ref[rows_, :]
                    y, _ = _rms_fwd(out, gf_ref[...])
                    err = y - t_buf[rows_, :]
                    p = 0.5 * jnp.sum(jnp.mean(err * err, axis=-1, keepdims=True), axis=0,
                                      keepdims=True)
                    dx, dgp = _rms_bwd(err / d, out, gf_ref[...])
                    out_ref[rows_, :] = dx
                    part = p if part is None else part + p
                    dg = dgp if dg is None else dg + dgp
                part = jnp.broadcast_to(part, loss_ref.shape)

                @pl.when(i == 0)
                def _():
                    loss_ref[...] = part
                    dg_ref[...] = dg

                @pl.when(i > 0)
                def _():
                    loss_ref[...] += part
                    dg_ref[...] += dg

    rows = pl.BlockSpec((tm, d), lambda i, f: (i, 0))
    in_specs = [rows, _full(g),
                pl.BlockSpec((None, d, tf), lambda i, f: (f // per_slot, 0, f % per_slot)),
                pl.BlockSpec((tf, d), lambda i, f: (f, 0))]
    out_specs = [rows, pl.BlockSpec((tf // FF_SLAB, tm, FF_SLAB), lambda i, f: (f, i, 0)), rows]
    out_shape = [jax.ShapeDtypeStruct((s, d), F32),
                 jax.ShapeDtypeStruct((ff // FF_SLAB, s, FF_SLAB), BF16),
                 jax.ShapeDtypeStruct((s, d), BF16)]
    args = [h, g, w_up, w_down]
    scratch = [pltpu.VMEM((tm, d), BF16), pltpu.VMEM((tm, d), F32)]
    if head is not None:
        in_specs += [_full(head[0]), ANY]
        args += list(head)
        out_specs += [pl.BlockSpec((1, 128), lambda i, f: (0, 0)),
                      pl.BlockSpec((1, d), lambda i, f: (0, 0))]
        out_shape += [jax.ShapeDtypeStruct((1, 128), F32), jax.ShapeDtypeStruct((1, d), F32)]
        scratch += [pltpu.VMEM((tm, d), F32), pltpu.SemaphoreType.DMA]
    return pl.pallas_call(
        body, name=name, grid=(s // tm, nf),
        in_specs=in_specs, out_specs=out_specs, out_shape=out_shape, scratch_shapes=scratch,
        compiler_params=_params(("parallel" if head is None else "arbitrary", "arbitrary")),
    )(*args)


def _window_sum_down(e, window):
    step = 1
    while step < window:
        e = e + pltpu.roll(e, step, axis=0)
        step *= 2
    return e


def _window_sum_up(e, window):
    n = e.shape[0]
    step = 1
    while step < window:
        e = e + pltpu.roll(e, n - step, axis=0)
        step *= 2
    return e


def _pool_counts(first_row, tm, window):
    t = first_row + _row((tm, 1))
    return jnp.minimum(t + 1, window).astype(F32)


def _pool_fwd(h, g, pw, ps):
    s, d = h.shape
    cg = d // len(POOL_WINDOWS)
    tm = min(TILE_ROWS, s)

    def body(h_ref, g_ref, pw_ref, ps_ref, out_ref, nbuf):
        i = pl.program_id(0)

        @pl.when(i == 0)
        def _():
            nbuf[0:POOL_HALO, :] = jnp.zeros((POOL_HALO, d), F32)

        n, _ = _rms_fwd(h_ref[...], g_ref[...])
        nbuf[POOL_HALO:POOL_HALO + tm, :] = n
        for k, window in enumerate(POOL_WINDOWS):
            cols = slice(k * cg, (k + 1) * cg)
            sums = _window_sum_down(nbuf[:, cols], window)[POOL_HALO:, :]
            pooled = sums / _pool_counts(i * tm, tm, window) - n[:, cols]
            y = _nn(pooled.astype(BF16), pw_ref[k]) * ps_ref[:, cols]
            out_ref[:, cols] = h_ref[:, cols] + y
        nbuf[0:POOL_HALO, :] = n[tm - POOL_HALO:tm, :]

    return pl.pallas_call(
        body, name="pool_fwd", grid=(s // tm,),
        in_specs=[pl.BlockSpec((tm, d), lambda i: (i, 0)), _full(g), _full(pw), _full(ps)],
        out_specs=pl.BlockSpec((tm, d), lambda i: (i, 0)),
        out_shape=jax.ShapeDtypeStruct((s, d), F32),
        scratch_shapes=[pltpu.VMEM((tm + POOL_HALO, d), F32)],
        compiler_params=_params(("arbitrary",)),
    )(h, g, pw, ps)


def _mlp_bwd_x(dz, a, w_up, w_down, h_in, g, name):
    s, d = dz.shape
    ff = w_down.shape[0]
    tm = min(TILE_MLP_ROWS, s)
    tf = FF_SLAB
    nf = ff // tf

    def body(dz_ref, a_ref, wu_ref, wd_ref, h_ref, g_ref, da_ref, dzb_ref, dh_ref, dg_ref,
             dzs_ref, acc_ref):
        i = pl.program_id(0)
        f = pl.program_id(1)

        @pl.when(f == 0)
        def _():
            dzb = dz_ref[...].astype(BF16)
            dzs_ref[...] = dzb
            dzb_ref[...] = dzb
            acc_ref[...] = jnp.zeros_like(acc_ref)

        dr = _nt(dzs_ref[...], wd_ref[...])
        da = (dr * (2.0 * jnp.maximum(a_ref[...].astype(F32), 0.0))).astype(BF16)
        da_ref[...] = da
        acc_ref[...] += _nt(da, wu_ref[...])

        @pl.when(f == nf - 1)
        def _():
            dx, dg = _rms_bwd(acc_ref[...], h_ref[...], g_ref[...])
            dh_ref[...] = dz_ref[...] + dx

            @pl.when(i == 0)
            def _():
                dg_ref[...] = dg

            @pl.when(i > 0)
            def _():
                dg_ref[...] += dg

    return pl.pallas_call(
        body, name=name, grid=(s // tm, nf),
        in_specs=[pl.BlockSpec((tm, d), lambda i, f: (i, 0)),
                  pl.BlockSpec((None, tm, tf), lambda i, f: (f, i, 0)),
                  pl.BlockSpec((None, d, tf), lambda i, f: (f, 0, 0)),
                  pl.BlockSpec((tf, d), lambda i, f: (f, 0)),
                  pl.BlockSpec((tm, d), lambda i, f: (i, 0)), _full(g)],
        out_specs=[pl.BlockSpec((None, tm, tf), lambda i, f: (f, i, 0)),
                   pl.BlockSpec((tm, d), lambda i, f: (i, 0)),
                   pl.BlockSpec((tm, d), lambda i, f: (i, 0)),
                   pl.BlockSpec((1, d), lambda i, f: (0, 0))],
        out_shape=[jax.ShapeDtypeStruct((nf, s, tf), BF16),
                   jax.ShapeDtypeStruct((s, d), BF16),
                   jax.ShapeDtypeStruct((s, d), F32),
                   jax.ShapeDtypeStruct((1, d), F32)],
        scratch_shapes=[pltpu.VMEM((tm, d), BF16), pltpu.VMEM((tm, d), F32)],
        compiler_params=_params(("arbitrary", "arbitrary")),
    )(dz, a, w_up, w_down, h_in, g)


def _mlp_bwd_w(n, da, a, dzb, slot_cols, name):
    s, d = n.shape
    ff = a.shape[0] * a.shape[2]
    tn = min(TILE_WGRAD_N, slot_cols)
    tk = min(TILE_WGRAD_K, s)
    per_slot = slot_cols // tn
    nk = s // tk

    def body(n_ref, da_ref, a_ref, dz_ref, du_ref, dd_ref, accu_ref, accd_ref):
        k = pl.program_id(1)

        @pl.when(k == 0)
        def _():
            accu_ref[...] = jnp.zeros_like(accu_ref)
            accd_ref[...] = jnp.zeros_like(accd_ref)

        for j in range(tn // FF_SLAB):
            cols = slice(j * FF_SLAB, (j + 1) * FF_SLAB)
            accu_ref[:, cols] += _tn(n_ref[...], da_ref[j])
            r = jnp.square(jnp.maximum(a_ref[j].astype(F32), 0.0)).astype(BF16)
            accd_ref[cols, :] += _tn(r, dz_ref[...])

        @pl.when(k == nk - 1)
        def _():
            du_ref[...] = accu_ref[...].astype(BF16)
            dd_ref[...] = accd_ref[...].astype(BF16)

    return pl.pallas_call(
        body, name=name, grid=(ff // tn, nk),
        in_specs=[pl.BlockSpec((tk, d), lambda f, k: (k, 0)),
                  pl.BlockSpec((tn // FF_SLAB, tk, FF_SLAB), lambda f, k: (f, k, 0)),
                  pl.BlockSpec((tn // FF_SLAB, tk, FF_SLAB), lambda f, k: (f, k, 0)),
                  pl.BlockSpec((tk, d), lambda f, k: (k, 0))],
        out_specs=[pl.BlockSpec((None, d, tn), lambda f, k: (f // per_slot, 0, f % per_slot)),
                   pl.BlockSpec((tn, d), lambda f, k: (f, 0))],
        out_shape=[jax.ShapeDtypeStruct((ff // slot_cols, d, slot_cols), BF16),
                   jax.ShapeDtypeStruct((ff, d), BF16)],
        scratch_shapes=[pltpu.VMEM((d, tn), F32), pltpu.VMEM((tn, d), F32)],
        compiler_params=_params(("parallel", "arbitrary")),
    )(n, da, a, dzb)


def _pool_bwd(after, dh, h, g, pw, ps):
    s, d = h.shape
    cg = d // len(POOL_WINDOWS)
    tm = min(TILE_ROWS, s)
    nb = s // tm
    halo_per_tile = tm // POOL_HALO

    def body(after_ref, dh_ref, h_ref, halo_ref, g_ref, pw_ref, ps_ref,
             dx_ref, dpw_ref, dps_ref, dg_ref, nbuf, qbuf, dn_ref, carry, dpw_acc):
        i = pl.program_id(0)
        blk = nb - 1 - i

        @pl.when(i == 0)
        def _():
            carry[...] = jnp.zeros_like(carry)
            dpw_acc[...] = jnp.zeros_like(dpw_acc)
            dps_ref[...] = jnp.zeros_like(dps_ref)
            dg_ref[...] = jnp.zeros_like(dg_ref)

        hv = h_ref[...]
        n, _ = _rms_fwd(hv, g_ref[...])
        nh, _ = _rms_fwd(halo_ref[...], g_ref[...])
        nbuf[0:POOL_HALO, :] = jnp.where(blk == 0, 0.0, nh)
        nbuf[POOL_HALO:POOL_HALO + tm, :] = n
        dhv = dh_ref[...]
        for k, window in enumerate(POOL_WINDOWS):
            cols = slice(k * cg, (k + 1) * cg)
            cnt = _pool_counts(blk * tm, tm, window)
            sums = _window_sum_down(nbuf[:, cols], window)[POOL_HALO:, :]
            pb = (sums / cnt - n[:, cols]).astype(BF16)
            dyk = dhv[:, cols]
            dps_ref[:, cols] += jnp.sum(dyk * _nn(pb, pw_ref[k]), axis=0, keepdims=True)
            dyb = (dyk * ps_ref[:, cols]).astype(BF16)
            dpw_acc[k] += _tn(pb, dyb)
            dpool = _nt(dyb, pw_ref[k])
            qv = dpool / cnt
            qbuf[0:tm, cols] = qv
            qbuf[tm:tm + POOL_HALO, cols] = carry[:, cols]
            dn_ref[:, cols] = _window_sum_up(qbuf[:, cols], window)[0:tm, :] - dpool
            carry[:, cols] = qv[0:POOL_HALO, :]
        dx, dg = _rms_bwd(dn_ref[...], hv, g_ref[...])
        dx_ref[...] = dhv + dx
        dg_ref[...] += dg

        @pl.when(i == nb - 1)
        def _():
            dpw_ref[...] = dpw_acc[...].astype(BF16)

    rev = lambda i: (nb - 1 - i, 0)
    return pl.pallas_call(
        body, name="pool_bwd", grid=(nb,),
        in_specs=[ANY, pl.BlockSpec((tm, d), rev), pl.BlockSpec((tm, d), rev),
                  pl.BlockSpec((POOL_HALO, d),
                               lambda i: (jnp.maximum((nb - 1 - i) * halo_per_tile - 1, 0), 0)),
                  _full(g), _full(pw), _full(ps)],
        out_specs=[pl.BlockSpec((tm, d), rev), _full(pw),
                   pl.BlockSpec((1, d), lambda i: (0, 0)),
                   pl.BlockSpec((1, d), lambda i: (0, 0))],
        out_shape=[jax.ShapeDtypeStruct((s, d), F32),
                   jax.ShapeDtypeStruct(pw.shape, BF16),
                   jax.ShapeDtypeStruct((1, d), F32),
                   jax.ShapeDtypeStruct((1, d), F32)],
        scratch_shapes=[pltpu.VMEM((tm + POOL_HALO, d), F32), pltpu.VMEM((tm + POOL_HALO, d), F32),
                        pltpu.VMEM((tm, d), F32), pltpu.VMEM((POOL_HALO, d), F32),
                        pltpu.VMEM(pw.shape, F32)],
        compiler_params=_params(("arbitrary",)),
    )(after, dh, h, h, g, pw, ps)


def _conv_out_bwd(after, dh, w_out, o, bcx, cw):
    s, d = dh.shape
    c = o.shape[1]
    tm = min(TILE_ROWS, s)
    nb = s // tm
    halo_per_tile = tm // CONV_HALO

    def body(after_ref, dh_ref, w_ref, o_ref, b_ref, c_ref, xin_ref, ch_ref, xh_ref, cw_ref,
             do_ref, delta_ref, dbcx_ref, dw_ref, dcw_ref, ubuf, dbuf, carry, acc):
        i = pl.program_id(0)
        blk = nb - 1 - i

        @pl.when(i == 0)
        def _():
            carry[...] = jnp.zeros_like(carry)
            acc[...] = jnp.zeros_like(acc)
            dcw_ref[...] = jnp.zeros_like(dcw_ref)

        dm = dh_ref[...].astype(BF16)
        dcat = _nt(dm, w_ref[...])
        do = dcat[:, 0:c]
        dy = dcat[:, c:2 * c]
        do_ref[...] = do.astype(BF16)
        head_of_lane = lax.shift_right_logical(_lane((8, c)), HEAD_DIM.bit_length() - 1)
        heads = (head_of_lane == _row((8, c))).astype(BF16)
        delta_ref[...] = _exact_nt(heads, do * o_ref[...].astype(F32))

        cv_ = c_ref[...]
        xin = xin_ref[...]
        bv = b_ref[...]
        u = cv_ * xin
        ubuf[0:CONV_HALO, :] = jnp.where(blk == 0, 0.0, ch_ref[...] * xh_ref[...])
        ubuf[CONV_HALO:CONV_HALO + tm, :] = u
        u1 = ubuf[CONV_HALO - 1:CONV_HALO - 1 + tm, :]
        u2 = ubuf[CONV_HALO - 2:CONV_HALO - 2 + tm, :]
        w0, w1, w2 = cw_ref[0:1, :], cw_ref[1:2, :], cw_ref[2:3, :]
        cv = (w0 * u2 + w1 * u1) + w2 * u
        acc[0:c, :] += _tn(o_ref[...], dm)
        acc[c:2 * c, :] += _tn((bv * cv).astype(BF16), dm)

        dcv = dy * bv
        dcw_ref[0:1, :] += jnp.sum(dcv * u2, axis=0, keepdims=True)
        dcw_ref[1:2, :] += jnp.sum(dcv * u1, axis=0, keepdims=True)
        dcw_ref[2:3, :] += jnp.sum(dcv * u, axis=0, keepdims=True)
        dbuf[0:tm, :] = dcv
        dbuf[tm:tm + CONV_HALO, :] = carry[...]
        du = w2 * dcv + w1 * dbuf[1:1 + tm, :] + w0 * dbuf[2:2 + tm, :]
        dbcx_ref[:, 0:c] = (dy * cv).astype(BF16)
        dbcx_ref[:, c:2 * c] = (du * xin).astype(BF16)
        dbcx_ref[:, 2 * c:3 * c] = (du * cv_).astype(BF16)
        carry[...] = dcv[0:CONV_HALO, :]

        @pl.when(i == nb - 1)
        def _():
            dw_ref[...] = acc[...].astype(BF16)

    rev = lambda k: (lambda i: (nb - 1 - i, k))
    halo = lambda k: (lambda i: (jnp.maximum((nb - 1 - i) * halo_per_tile - 1, 0), k))
    return pl.pallas_call(
        body, name="conv_out_bwd", grid=(nb,),
        in_specs=[ANY, pl.BlockSpec((tm, d), rev(0)), _full(w_out), pl.BlockSpec((tm, c), rev(0)),
                  pl.BlockSpec((tm, c), rev(0)), pl.BlockSpec((tm, c), rev(1)),
                  pl.BlockSpec((tm, c), rev(2)),
                  pl.BlockSpec((CONV_HALO, c), halo(1)), pl.BlockSpec((CONV_HALO, c), halo(2)),
                  _full(cw)],
        out_specs=[pl.BlockSpec((tm, c), rev(0)),
                   pl.BlockSpec((8, tm), lambda i: (0, nb - 1 - i)),
                   pl.BlockSpec((tm, 3 * c), rev(0)),
                   _full(w_out), _full(cw)],
        out_shape=[jax.ShapeDtypeStruct((s, c), BF16),
                   jax.ShapeDtypeStruct((8, s), F32),
                   jax.ShapeDtypeStruct((s, 3 * c), BF16),
                   jax.ShapeDtypeStruct(w_out.shape, BF16),
                   jax.ShapeDtypeStruct(cw.shape, F32)],
        scratch_shapes=[pltpu.VMEM((tm + CONV_HALO, c), F32), pltpu.VMEM((tm + CONV_HALO, c), F32),
                        pltpu.VMEM((CONV_HALO, c), F32), pltpu.VMEM(w_out.shape, F32)],
        compiler_params=_params(("arbitrary",)),
    )(after, dh, w_out, o, bcx, bcx, bcx, bcx, bcx, cw)


def _attn_bwd(after, qa, ka, qkv, do, lse, delta):
    s = qa.shape[1]
    a = N_HEADS * HEAD_DIM
    t = min(TILE_ATTN, s)
    nq = s // t
    n_pairs = N_HEADS // 2
    v_block0 = 2 * a // 128

    def body(after_ref, ka_ref, v_ref, qa_ref, do_ref, lse_ref, delta_ref,
             dqt_ref, dka_ref, dv_ref, dv_acc):
        g = pl.program_id(0)
        j = pl.program_id(1)

        @pl.when(j == 0)
        def _():
            dqt_ref[...] = jnp.zeros_like(dqt_ref)

        lane = _lane((t, 128))
        vf = v_ref[...].astype(F32)
        v_heads = [jnp.where(lane < HEAD_DIM, vf, 0.0).astype(BF16),
                   jnp.where(lane >= HEAD_DIM, vf, 0.0).astype(BF16)]
        ke_t = [ka_ref[e].astype(F32).T.astype(BF16) for e in range(2)]

        def q_step(i, first):
            qs = pl.ds(pl.multiple_of(i * t, t), t)
            dob = do_ref[qs, :]
            for e in range(2):
                qe = qa_ref[e, qs, :]
                sc = _nt(ka_ref[e], qe)
                if first:
                    sc = jnp.where(_row((t, t)) <= _lane((t, t)), sc, NEG_BIG)
                p = jnp.exp2(sc - lse_ref[pl.ds(e, 1), qs])
                dv_part = _nn(p.astype(BF16), dob)
                dp = _nt(v_heads[e], dob)
                ds = (p * (dp - delta_ref[pl.ds(2 * g + e, 1), qs])).astype(BF16)
                dk_part = _nn(ds, qe)
                if first:
                    dv_acc[e] = dv_part
                    dka_ref[e] = dk_part
                else:
                    dv_acc[e] += dv_part
                    dka_ref[e] += dk_part
                dqt_ref[e, :, qs] += _nn(ke_t[e], ds)

        q_step(j, True)

        def full_step(i, carry):
            q_step(i, False)
            return carry

        lax.fori_loop(j + 1, nq, full_step, 0)
        dv_ref[...] = jnp.where(lane < HEAD_DIM, dv_acc[0], dv_acc[1]).astype(BF16)

    return pl.pallas_call(
        body, name="attn_bwd", grid=(n_pairs, nq),
        in_specs=[ANY, pl.BlockSpec((2, t, 128), lambda g, j: (g, j, 0)),
                  pl.BlockSpec((t, 128), lambda g, j: (j, v_block0 + g)),
                  pl.BlockSpec((2, s, 128), lambda g, j: (g, 0, 0)),
                  pl.BlockSpec((s, 128), lambda g, j: (0, g)),
                  pl.BlockSpec((None, 8, s), lambda g, j: (g, 0, 0)),
                  pl.BlockSpec((8, s), lambda g, j: (0, 0))],
        out_specs=[pl.BlockSpec((2, 128, s), lambda g, j: (g, 0, 0)),
                   pl.BlockSpec((2, t, 128), lambda g, j: (g, j, 0)),
                   pl.BlockSpec((t, 128), lambda g, j: (j, g))],
        out_shape=[jax.ShapeDtypeStruct((N_HEADS, 128, s), F32),
                   jax.ShapeDtypeStruct((N_HEADS, s, 128), F32),
                   jax.ShapeDtypeStruct((s, a), BF16)],
        scratch_shapes=[pltpu.VMEM((2, t, 128), F32)],
        compiler_params=_params(("parallel", "arbitrary")),
    )(after, ka, qkv, qa, do, lse, delta)


def _gate_bwd(dqa, dka, dv, fl, bf):
    s = fl.shape[0]
    a = N_HEADS * HEAD_DIM
    tm = min(TILE_ROWS, s)
    nb = s // tm

    def body(dqa_ref, dka_ref, dv_ref, fl_ref, bf_ref, dqkv_ref, dfl_ref, dbf_ref, carry):
        i = pl.program_id(0)

        @pl.when(i == 0)
        def _():
            carry[...] = jnp.zeros_like(carry)
            dbf_ref[...] = jnp.zeros_like(dbf_ref)

        lane = _lane((tm, 128))
        dq_sum = jnp.zeros((tm, 128), F32)
        dk_sum = jnp.zeros((tm, 128), F32)
        for pair in range(N_HEADS // 2):
            qs, ks = [], []
            for e in range(2):
                h = 2 * pair + e
                dq = dqa_ref[h].T
                dk = dka_ref[h]
                dq_sum = dq_sum + dq
                dk_sum = dk_sum + dk
                qs.append(dq * ATTN_SCALE)
                ks.append(dk * (1.0 / LOG2_E))
            cols = slice(pair * 128, (pair + 1) * 128)
            dqkv_ref[:, cols] = jnp.where(
                lane < HEAD_DIM, qs[0], pltpu.roll(qs[1], HEAD_DIM, axis=1)).astype(BF16)
            dqkv_ref[:, a + pair * 128:a + (pair + 1) * 128] = jnp.where(
                lane < HEAD_DIM, ks[0], pltpu.roll(ks[1], HEAD_DIM, axis=1)).astype(BF16)
        dqkv_ref[:, 2 * a:3 * a] = dv_ref[...]

        in_q = (lane >= LANE_CQ) & (lane < LANE_CQ + N_HEADS)
        in_k = (lane >= LANE_CK) & (lane < LANE_CK + N_HEADS)
        dcum = (pltpu.roll(jnp.where(in_q, dq_sum, 0.0), 128 - LANE_CQ, axis=1)
                - pltpu.roll(jnp.where(in_k, dk_sum, 0.0), 128 - LANE_CK, axis=1))

        upper = (_lane((tm, tm)) >= _row((tm, tm))).astype(BF16)
        dlogf = _exact_nn(upper, dcum) + carry[0:1, :]
        carry[0:1, :] = dlogf[0:1, :]
        z = fl_ref[...] + bf_ref[...]
        ez = jnp.exp(-jnp.abs(z))
        sig_neg = jnp.where(z >= 0.0, ez, 1.0) / (1.0 + ez)
        dz = jnp.where(lane < N_HEADS, dlogf * sig_neg, 0.0)
        dfl_ref[...] = dz.astype(BF16)
        dbf_ref[...] += jnp.sum(dz, axis=0, keepdims=True)

    rev3 = lambda i: (0, nb - 1 - i, 0)
    rev = lambda i: (nb - 1 - i, 0)
    return pl.pallas_call(
        body, name="gate_bwd", grid=(nb,),
        in_specs=[pl.BlockSpec((N_HEADS, 128, tm), lambda i: (0, 0, nb - 1 - i)),
                  pl.BlockSpec((N_HEADS, tm, 128), rev3),
                  pl.BlockSpec((tm, a), rev), pl.BlockSpec((tm, 128), rev), _full(bf)],
        out_specs=[pl.BlockSpec((tm, 3 * a), rev), pl.BlockSpec((tm, 128), rev),
                   pl.BlockSpec((1, 128), lambda i: (0, 0))],
        out_shape=[jax.ShapeDtypeStruct((s, 3 * a), BF16),
                   jax.ShapeDtypeStruct((s, 128), BF16),
                   jax.ShapeDtypeStruct((1, 128), F32)],
        scratch_shapes=[pltpu.VMEM((8, 128), F32)],
        compiler_params=_params(("arbitrary",)),
    )(dqa, dka, dv, fl, bf)


def _in_proj_bwd(after, dqkv, dfl, dbcx, w_qkv, w_f, w_bcx, x, g, dh):
    s, d = x.shape
    tm = min(TILE_PROJ_ROWS, s)

    def body(after_ref, dq_ref, df_ref, db_ref, wq_ref, wf_ref, wb_ref, x_ref, g_ref, dh_ref,
             gx_ref, dg_ref):
        i = pl.program_id(0)
        dn = (_nt(dq_ref[...], wq_ref[...]) + _nt(df_ref[...], wf_ref[...])
              + _nt(db_ref[...], wb_ref[...]))
        dx, dg = _rms_bwd(dn, x_ref[...], g_ref[...])
        gx_ref[...] = dh_ref[...] + dx

        @pl.when(i == 0)
        def _():
            dg_ref[...] = dg

        @pl.when(i > 0)
        def _():
            dg_ref[...] += dg

    rows = lambda c: pl.BlockSpec((tm, c), lambda i: (i, 0))
    return pl.pallas_call(
        body, name="in_proj_bwd", grid=(s // tm,),
        in_specs=[ANY, rows(dqkv.shape[1]), rows(dfl.shape[1]), rows(dbcx.shape[1]),
                  _full(w_qkv), _full(w_f), _full(w_bcx), rows(d), _full(g), rows(d)],
        out_specs=[rows(d), pl.BlockSpec((1, d), lambda i: (0, 0))],
        out_shape=[jax.ShapeDtypeStruct((s, d), F32), jax.ShapeDtypeStruct((1, d), F32)],
        compiler_params=_params(("arbitrary",)),
    )(after, dqkv, dfl, dbcx, w_qkv, w_f, w_bcx, x, g, dh)


def _wgrad_in(n, dys):
    s, d = n.shape
    m = len(dys)
    tk = min(TILE_ROWS, s)
    nk = s // tk

    def body(*refs):
        n_ref, dy_refs, dw_refs, accs = refs[0], refs[1:1 + m], refs[1 + m:1 + 2 * m], refs[1 + 2 * m:]
        k = pl.program_id(0)

        @pl.when(k == 0)
        def _():
            for acc in accs:
                acc[...] = jnp.zeros_like(acc)

        nb = n_ref[...]
        for dy_ref, acc in zip(dy_refs, accs):
            acc[...] += _tn(nb, dy_ref[...])

        @pl.when(k == nk - 1)
        def _():
            for dw_ref, acc in zip(dw_refs, accs):
                dw_ref[...] = acc[...].T.astype(BF16)

    return pl.pallas_call(
        body, name="wgrad_in", grid=(nk,),
        in_specs=[pl.BlockSpec((tk, d), lambda k: (k, 0))]
        + [pl.BlockSpec((tk, dy.shape[1]), lambda k: (k, 0)) for dy in dys],
        out_specs=[pl.BlockSpec((dy.shape[1], d), lambda k: (0, 0)) for dy in dys],
        out_shape=[jax.ShapeDtypeStruct((dy.shape[1], d), BF16) for dy in dys],
        scratch_shapes=[pltpu.VMEM((d, dy.shape[1]), F32) for dy in dys],
        compiler_params=_params(("arbitrary",)),
    )(n, *dys)


def _row_tile(rows):
    t = min(TILE_ELEM_ROWS, rows)
    while rows % t:
        t //= 2
    return t


def _adamw_math(w, g, m, v):
    m = ADAM_B1 * m + (1.0 - ADAM_B1) * g
    v = ADAM_B2 * v + (1.0 - ADAM_B2) * jnp.square(g)
    m_hat = m / (1.0 - ADAM_B1 ** ADAM_STEP)
    v_hat = v / (1.0 - ADAM_B2 ** ADAM_STEP)
    delta = -ADAM_LR * (m_hat / (jnp.sqrt(v_hat) + ADAM_EPS) + ADAM_WD * w)
    return delta, m, v


def _adamw(w, g, m, v, name):
    rows, cols = w.shape

    def body(w_ref, g_ref, m_ref, v_ref, d_ref, nm_ref, nv_ref):
        delta, nm, nv = _adamw_math(w_ref[...], g_ref[...], m_ref[...], v_ref[...])
        d_ref[...] = delta
        nm_ref[...] = nm
        nv_ref[...] = nv

    if rows % 8 == 0:
        tr = _row_tile(rows)
        grid, spec = (rows // tr,), pl.BlockSpec((tr, cols), lambda i: (i, 0))
    else:
        grid, spec = (cols // 256,), pl.BlockSpec((rows, 256), lambda i: (0, i))
    out = jax.ShapeDtypeStruct(w.shape, F32)
    return pl.pallas_call(
        body, name=name, grid=grid, in_specs=[spec] * 4, out_specs=[spec] * 3,
        out_shape=[out, out, out], compiler_params=_params(("parallel",)),
    )(w, g, m, v)


def _sum_devices(parts):
    def body(p_ref, g_ref):
        g = p_ref[0]
        for k in range(1, N_DEV):
            g = g + p_ref[k]
        g_ref[...] = g

    return pl.pallas_call(
        body, name="sum_devices",
        in_specs=[pl.BlockSpec(memory_space=pltpu.VMEM)],
        out_specs=pl.BlockSpec(memory_space=pltpu.VMEM),
        out_shape=jax.ShapeDtypeStruct(parts.shape[1:], F32),
    )(parts)


def _mesh_position():
    x, y, c = lax.axis_index("x"), lax.axis_index("y"), lax.axis_index("c")
    chips = [(1 - x, y), (x, 1 - y), (1 - x, 1 - y)]
    return x, y, c, chips


ANY = pl.BlockSpec(memory_space=pl.ANY)
HBM = pl.BlockSpec(memory_space=pltpu.HBM)
SEM = pl.BlockSpec(memory_space=pltpu.SEMAPHORE)
SPLIT_COPY_EFFECT = pltpu.SideEffectType.DATAFLOW_SIDE_EFFECTING


def _in_hbm(a):
    return pltpu.with_memory_space_constraint(a, pltpu.HBM)


def _chip_copies(views, srcs, lands, send, recv, waiting=False):
    _, _, c, chips = _mesh_position()
    cps = []
    for a in range(len(srcs)):
        for k, (px, py) in enumerate(chips):
            src, dst = views(a, k, srcs[a], lands[a], c, 2 * px + py)
            sem = a * (N_CHIPS - 1) + k
            cps.append(pltpu.make_async_remote_copy(
                src_ref=src, dst_ref=dst, send_sem=send.at[sem], recv_sem=recv.at[sem],
                device_id=(px, py, c), device_id_type=MESH))
    return cps


def _ici_start(sources, land_shapes, copies, after, name, per_array=N_CHIPS - 1):
    n = len(sources)

    def body(*refs):
        srcs, lands = refs[:n], refs[n:2 * n]
        send, recv = refs[2 * n + 1], refs[2 * n + 2]
        token = refs[-1]
        for cp in copies(srcs, lands, send, recv, False):
            cp.start()
        token[...] = jnp.zeros_like(token)

    lands = [_in_hbm(lax.empty(s.shape, s.dtype)) for s in land_shapes]
    outs = pl.pallas_call(
        body, name=name,
        in_specs=[HBM] * (2 * n) + [ANY],
        out_specs=[SEM, SEM] + [HBM] * (2 * n) + [pl.BlockSpec(memory_space=pltpu.VMEM)],
        out_shape=[pltpu.SemaphoreType.DMA((n * per_array,))] * 2
        + [pltpu.HBM(a.shape, a.dtype) for a in sources]
        + [pltpu.HBM(s.shape, s.dtype) for s in land_shapes]
        + [jax.ShapeDtypeStruct((8, 128), F32)],
        input_output_aliases={i: 2 + i for i in range(2 * n)},
        compiler_params=pltpu.CompilerParams(has_side_effects=SPLIT_COPY_EFFECT),
    )(*[_in_hbm(a) for a in sources], *lands, after)
    return outs[0], outs[1], list(outs[2:2 + n]), list(outs[2 + n:2 + 2 * n]), outs[-1]


def _ici_wait(handle, copies, after, name):
    send, recv, srcs, lands, _ = handle
    n = len(srcs)

    def body(*refs):
        src_refs, land_refs = refs[:n], refs[n:2 * n]
        for cp in copies(src_refs, land_refs, refs[2 * n], refs[2 * n + 1], True):
            cp.wait_send()
            cp.wait_recv()

    outs = pl.pallas_call(
        body, name=name,
        in_specs=[HBM] * (2 * n) + [SEM, SEM, ANY],
        out_specs=[HBM] * (2 * n),
        out_shape=[pltpu.HBM(a.shape, a.dtype) for a in srcs]
        + [pltpu.HBM(a.shape, a.dtype) for a in lands],
        input_output_aliases={i: i for i in range(2 * n)},
        compiler_params=pltpu.CompilerParams(has_side_effects=SPLIT_COPY_EFFECT),
    )(*srcs, *lands, send, recv, after)
    return list(outs[:n]), list(outs[n:])


def _gather_views(split):
    def views(a, k, src, land, c, slot):
        if split[a]:
            half = src.shape[0] // 2
            src = src.at[pl.ds(c * half, half)]
        return src, land.at[k]
    return views


def _gather_whole_views(a, k, src, land, c, slot):
    x, y, _, _ = _mesh_position()
    return src, land.at[2 * x + y]


SCATTER_COPIES = 2 * (N_CHIPS - 1)


def _scatter_copies(srcs, lands, send, recv, waiting):
    _, _, c, chips = _mesh_position()
    cps = []
    for a in range(len(srcs)):
        half = srcs[a].shape[1] // 2
        for k, (px, py) in enumerate(chips):
            for h in range(2):
                arrival = 2 * k + (h if waiting else c)
                cps.append(pltpu.make_async_remote_copy(
                    src_ref=srcs[a].at[2 * px + py, pl.ds(h * half, half)],
                    dst_ref=lands[a].at[arrival],
                    send_sem=send.at[a * SCATTER_COPIES + 2 * k + h],
                    recv_sem=recv.at[a * SCATTER_COPIES + arrival],
                    device_id=(px, py, h), device_id_type=MESH))
    return cps


def _gather_land_shapes(shards, split):
    return [jax.ShapeDtypeStruct(
        (N_CHIPS - 1, a.shape[0] // 2 if sp else a.shape[0]) + a.shape[1:], a.dtype)
        for a, sp in zip(shards, split)]


def _gather_finish(shards, lands, split, name):
    n = len(shards)
    ns = sum(split)
    d_index = {a: i for i, a in enumerate(a for a in range(n) if split[a])}

    def body(*refs):
        shard, land, outs = refs[:n], refs[n:2 * n], refs[2 * n:3 * n]
        obuf, fbuf = refs[3 * n:4 * n], refs[4 * n:5 * n]
        dbuf = refs[5 * n:5 * n + ns]
        ld_own, st_own, ld, st_mine, st_sib, send, recv = refs[5 * n + ns:]
        x, y, c, chips = _mesh_position()
        me = 2 * x + y
        own_loads, loads, sends, pending = [], {}, [], []
        for a in range(n):
            cp = pltpu.make_async_copy(shard[a], obuf[a], ld_own.at[a])
            cp.start()
            own_loads.append(cp)
        for a in range(n):
            for k in range(N_CHIPS - 1):
                cp = pltpu.make_async_copy(land[a].at[k], fbuf[a].at[k], ld.at[a, k])
                cp.start()
                loads[a, k] = cp
        for a in range(n):
            own_loads[a].wait()
            cp = pltpu.make_async_copy(obuf[a], outs[a].at[me], st_own.at[a])
            cp.start()
            pending.append(cp)
        for a in range(n):
            rows = shard[a].shape[0]
            for k, (px, py) in enumerate(chips):
                loads[a, k].wait()
                part = pl.ds(c * (rows // 2), rows // 2) if split[a] else pl.ds(0, rows)
                cp = pltpu.make_async_copy(fbuf[a].at[k], outs[a].at[2 * px + py, part],
                                           st_mine.at[a, k])
                cp.start()
                pending.append(cp)
                if split[a]:
                    fw = pltpu.make_async_remote_copy(
                        src_ref=fbuf[a].at[k], dst_ref=dbuf[d_index[a]].at[k],
                        send_sem=send.at[a, k], recv_sem=recv.at[a, k],
                        device_id=(x, y, 1 - c), device_id_type=MESH)
                    fw.start()
                    sends.append((a, k, fw))
        for a, k, fw in sends:
            px, py = chips[k]
            half = shard[a].shape[0] // 2
            fw.wait_recv()
            cp = pltpu.make_async_copy(dbuf[d_index[a]].at[k],
                                       outs[a].at[2 * px + py, pl.ds((1 - c) * half, half)],
                                       st_sib.at[a, k])
            cp.start()
            pending.append(cp)
        for _, _, fw in sends:
            fw.wait_send()
        for cp in pending:
            cp.wait()

    stage = [pltpu.VMEM(a.shape, a.dtype) for a in lands]
    dma = lambda *shape: pltpu.SemaphoreType.DMA(shape)
    return pl.pallas_call(
        body, name=name,
        in_specs=[ANY] * (2 * n), out_specs=[ANY] * n,
        out_shape=[jax.ShapeDtypeStruct((N_CHIPS,) + a.shape, a.dtype) for a in shards],
        scratch_shapes=[pltpu.VMEM(a.shape, a.dtype) for a in shards] + stage
        + [s for s, sp in zip(stage, split) if sp]
        + [dma(n), dma(n), dma(n, 3), dma(n, 3), dma(n, 3), dma(n, 3), dma(n, 3)],
        compiler_params=pltpu.CompilerParams(vmem_limit_bytes=VMEM_LIMIT_BYTES),
    )(*shards, *lands)


def _sum_chunk(rows):
    return next(r for r in range(SUM_CHUNK_ROWS, 0, -16) if rows % r == 0)


def _sum_and_share(partials, lands, name):
    n = len(partials)

    def body(*refs):
        own, landed, outs = refs[:n], refs[n:2 * n], refs[2 * n:3 * n]
        obuf, xbuf, ybuf, gbuf, sbuf, rbuf = (refs[(3 + k) * n:(4 + k) * n] for k in range(6))
        ld_own, ld_send, ld_got, st_own, st_sib, send_p, recv_p, send_s, recv_s = refs[9 * n:]
        x, y, c, _ = _mesh_position()
        me = 2 * x + y
        sibling = (x, y, 1 - c)

        def to_sibling(src, dst, send, recv, a):
            return pltpu.make_async_remote_copy(src_ref=src, dst_ref=dst, send_sem=send.at[a],
                                                recv_sem=recv.at[a], device_id=sibling,
                                                device_id_type=MESH)

        loads, firsts, seconds, stores = [], [], [], []
        for a in range(n):
            half = obuf[a].shape[0]
            cps = [pltpu.make_async_copy(own[a].at[me, pl.ds((1 - c) * half, half)], xbuf[a],
                                         ld_send.at[a]),
                   pltpu.make_async_copy(own[a].at[me, pl.ds(c * half, half)], obuf[a], ld_own.at[a]),
                   pltpu.make_async_copy(landed[a], gbuf[a], ld_got.at[a])]
            for cp in cps:
                cp.start()
            loads.append(cps)
        for a in range(n):
            loads[a][0].wait()
            rc = to_sibling(xbuf[a], ybuf[a], send_p, recv_p, a)
            rc.start()
            firsts.append(rc)
        for a in range(n):
            firsts[a].wait_recv()
            loads[a][1].wait()
            loads[a][2].wait()
            half = obuf[a].shape[0]
            rows = _sum_chunk(half)

            def add(k, carry, a=a, rows=rows):
                at = pl.ds(pl.multiple_of(k * rows, rows), rows)
                acc = obuf[a][at].astype(F32) + ybuf[a][at].astype(F32)
                for j in range(SCATTER_COPIES):
                    acc = acc + gbuf[a][j, at].astype(F32)
                sbuf[a][at] = acc
                return carry

            lax.fori_loop(0, half // rows, add, 0)
            rc = to_sibling(sbuf[a], rbuf[a], send_s, recv_s, a)
            rc.start()
            seconds.append(rc)
            cp = pltpu.make_async_copy(sbuf[a], outs[a].at[pl.ds(c * half, half)], st_own.at[a])
            cp.start()
            stores.append(cp)
        for a in range(n):
            half = obuf[a].shape[0]
            seconds[a].wait_recv()
            cp = pltpu.make_async_copy(rbuf[a], outs[a].at[pl.ds((1 - c) * half, half)], st_sib.at[a])
            cp.start()
            stores.append(cp)
        for rc in firsts + seconds:
            rc.wait_send()
        for cp in stores:
            cp.wait()

    halves = [(a.shape[1] // 2, a.shape[2]) for a in partials]
    return pl.pallas_call(
        body, name=name,
        in_specs=[ANY] * (2 * n), out_specs=[ANY] * n,
        out_shape=[jax.ShapeDtypeStruct((2 * h[0], h[1]), F32) for h in halves],
        scratch_shapes=[pltpu.VMEM(h, BF16) for h in halves] * 3
        + [pltpu.VMEM(g.shape, BF16) for g in lands]
        + [pltpu.VMEM(h, F32) for h in halves] * 2
        + [pltpu.SemaphoreType.DMA((n,))] * 9,
        compiler_params=pltpu.CompilerParams(vmem_limit_bytes=VMEM_LIMIT_BYTES),
    )(*partials, *lands)


def _gather_small(part):
    def body(in_ref, out_ref, send, recv, local):
        x, y, c, _ = _mesh_position()
        me = 4 * x + 2 * y + c
        cps = [pltpu.make_async_copy(in_ref, out_ref.at[me], local)]
        k = 0
        for fx in range(2):
            for fy in range(2):
                for fc in range(2):
                    if fx or fy or fc:
                        cps.append(pltpu.make_async_remote_copy(
                            src_ref=in_ref, dst_ref=out_ref.at[me], send_sem=send.at[k],
                            recv_sem=recv.at[k], device_id=(x ^ fx, y ^ fy, c ^ fc),
                            device_id_type=MESH))
                        k += 1
        for cp in cps:
            cp.start()
        for cp in cps:
            cp.wait()

    return pl.pallas_call(
        body, name="gather_small",
        in_specs=[pl.BlockSpec(memory_space=pltpu.VMEM)],
        out_specs=pl.BlockSpec(memory_space=pltpu.VMEM),
        out_shape=jax.ShapeDtypeStruct((N_DEV,) + part.shape, part.dtype),
        scratch_shapes=[pltpu.SemaphoreType.DMA((N_DEV - 1,)), pltpu.SemaphoreType.DMA((N_DEV - 1,)),
                        pltpu.SemaphoreType.DMA],
    )(part)


def _scatter_start(grads, after, tag):
    lands = [jax.ShapeDtypeStruct((SCATTER_COPIES, g.shape[1] // 2, g.shape[2]), g.dtype)
             for g in grads]
    return _ici_start(grads, lands, _scatter_copies, after, "scatter_start_" + tag,
                      per_array=SCATTER_COPIES)


def _scatter_finish(handle, after, tag):
    grads, lands = _ici_wait(handle, _scatter_copies, after, "scatter_wait_" + tag)
    return _sum_and_share(grads, lands, "sum_and_share_" + tag)


def _pad_rows(a, rows):
    return jnp.pad(a, ((0, rows - a.shape[0]), (0, 0)))


def kernel(x, norm_mix_0, w_in_0, b_f_0, conv_w_0, w_out_0, norm_ffn_0, w_up_0, w_down_0, norm_mix_1, pool_w_1, pool_scale_1, norm_ffn_1, w_up_1, w_down_1, final_norm, loss_target, m_norm_mix_0, m_w_in_0, m_b_f_0, m_conv_w_0, m_w_out_0, m_norm_ffn_0, m_w_up_0, m_w_down_0, m_norm_mix_1, m_pool_w_1, m_pool_scale_1, m_norm_ffn_1, m_w_up_1, m_w_down_1, m_final_norm, v_norm_mix_0, v_w_in_0, v_b_f_0, v_conv_w_0, v_w_out_0, v_norm_ffn_0, v_w_up_0, v_w_down_0, v_norm_mix_1, v_pool_w_1, v_pool_scale_1, v_norm_ffn_1, v_w_up_1, v_w_down_1, v_final_norm):
    d = x.shape[-1]
    a = N_HEADS * HEAD_DIM
    c_conv = conv_w_0.shape[1] * N_CHIPS
    xs = x[0]
    target = loss_target[0]
    row = lambda vec: vec.reshape(1, -1)

    big = [w_in_0, w_out_0, w_up_0, w_down_0, pool_w_1, w_up_1, w_down_1]
    first = [w_in_0.astype(BF16)]
    first_split = [True]
    copies_a = functools.partial(_chip_copies, _gather_views(first_split))
    copies_b = functools.partial(_chip_copies, _gather_whole_views)
    start_a = _ici_start(first, _gather_land_shapes(first, first_split), copies_a, b_f_0,
                         "gather_start_a")
    zero = start_a[-1][0, 0]
    rest = [(w + zero).astype(BF16)
            for w in (w_out_0, w_up_0, w_down_0, pool_w_1, w_up_1, w_down_1)]
    rest = rest + [conv_w_0]
    start_b = _ici_start(rest, [jax.ShapeDtypeStruct((N_CHIPS,) + w.shape, w.dtype) for w in rest],
                         copies_b, start_a[-1], "gather_start_b")
    n0 = _rms_pre(start_b[-1], xs, row(norm_mix_0))
    first, land_a = _ici_wait(start_a, copies_a, n0, "gather_wait_a")
    (g_in,) = _gather_finish(first, land_a, first_split, "gather_finish_a")
    w_in = g_in.transpose(1, 0, 2).reshape(d, -1)
    w_qkv = w_in[:, :3 * a]
    w_f = jnp.pad(w_in[:, 3 * a:3 * a + N_HEADS], ((0, 0), (0, 128 - N_HEADS)))
    w_bcx = w_in[:, 3 * a + N_HEADS:]
    bf = jnp.pad(b_f_0, (0, 128 - N_HEADS)).reshape(1, 128)

    qkv, fl, bcx = _in_proj(n0, w_qkv, w_f, w_bcx)
    qa, ka = _gate_prep(fl, bf, qkv)
    o, lse = _attn_fwd(qa, ka, qkv)
    rest, land_b = _ici_wait(start_b, copies_b, o, "gather_wait_b")
    own_slot = 2 * lax.axis_index("x") + lax.axis_index("y")
    g_out, g_up0, g_down0, g_pool, g_up1, g_down1, g_conv = [
        lax.dynamic_update_index_in_dim(land, shard, own_slot, 0)
        for land, shard in zip(land_b, rest)]
    w_out = g_out.reshape(-1, d)
    conv_w = _pad_rows(g_conv.transpose(1, 0, 2).reshape(conv_w_0.shape[0], c_conv), 8)
    h1 = _conv_out(o, bcx, conv_w, w_out, xs)
    w_down0 = g_down0.reshape(-1, d)
    w_down1 = g_down1.reshape(-1, d)
    pool_w = g_pool.transpose(1, 0, 2, 3).reshape(pool_w_1.shape[0], -1, pool_w_1.shape[2])
    h2, a0, nf0 = _mlp_fwd(h1, row(norm_ffn_0), g_up0, w_down0, "mlp_fwd_0")
    h3 = _pool_fwd(h2, row(norm_mix_1), pool_w, row(pool_scale_1))
    dh4, a1, nf1, loss_part, d_final = _mlp_fwd(h3, row(norm_ffn_1), g_up1, w_down1, "mlp_fwd_1",
                                                head=(row(final_norm), target))

    slot_cols = g_up0.shape[2]
    pool_cols = pool_w.shape[2]
    up_slabs = lambda w: (w.reshape(N_CHIPS, d, -1, FF_SLAB).transpose(0, 2, 1, 3)
                          .reshape(-1, d, FF_SLAB))
    da1, dz1, dh3, d_nffn1 = _mlp_bwd_x(dh4, a1, up_slabs(g_up1), w_down1, h3, row(norm_ffn_1),
                                        "mlp_bwd_x_1")
    dw_up1, dw_down1 = _mlp_bwd_w(nf1, da1, a1, dz1, slot_cols, "mlp_bwd_w_1")
    scatter_1 = _scatter_start([dw_up1, dw_down1.reshape(N_CHIPS, -1, d)], bf, "mlp1")
    dh2, dw_pool, d_pscale, d_nmix1 = _pool_bwd(scatter_1[-1], dh3, h2, row(norm_mix_1), pool_w,
                                                row(pool_scale_1))
    da0, dz0, dh1, d_nffn0 = _mlp_bwd_x(dh2, a0, up_slabs(g_up0), w_down0, h1, row(norm_ffn_0),
                                        "mlp_bwd_x_0")
    dw_up0, dw_down0 = _mlp_bwd_w(nf0, da0, a0, dz0, slot_cols, "mlp_bwd_w_0")
    dw_pool = (dw_pool.reshape(pool_w.shape[0], N_CHIPS, -1, pool_cols).transpose(1, 0, 2, 3)
               .reshape(N_CHIPS, -1, pool_cols))
    scatter_0 = _scatter_start([dw_up0, dw_down0.reshape(N_CHIPS, -1, d), dw_pool], bf, "mlp0")
    do, delta, dbcx, dw_out, d_conv = _conv_out_bwd(scatter_0[-1], dh1, w_out, o, bcx, conv_w)
    scatter_o = _scatter_start([dw_out.reshape(N_CHIPS, -1, d)], bf, "out")
    dqa, dka, dv = _attn_bwd(scatter_o[-1], qa, ka, qkv, do, lse, delta)
    dqkv, dfl, d_bf = _gate_bwd(dqa, dka, dv, fl, bf)
    dw_qkv, dw_f, dw_bcx = _wgrad_in(n0, [dqkv, dfl, dbcx])
    dw_in = jnp.concatenate([dw_qkv, dw_f[:N_HEADS], dw_bcx], axis=0).reshape(N_CHIPS, -1, d)
    slot_rows = -(-dw_in.shape[1] // 32) * 32
    dw_in = jnp.pad(dw_in, ((0, 0), (0, slot_rows - dw_in.shape[1]), (0, 0)))
    scatter_m = _scatter_start([dw_in], bf, "mixer")
    grad_x, d_nmix0 = _in_proj_bwd(scatter_m[-1], dqkv, dfl, dbcx, w_qkv, w_f, w_bcx, xs,
                                   row(norm_mix_0), dh1)

    r_up1, r_down1 = _scatter_finish(scatter_1, grad_x, "mlp1")
    r_up0, r_down0, r_pool = _scatter_finish(scatter_0, grad_x, "mlp0")
    (r_out,) = _scatter_finish(scatter_o, grad_x, "out")
    (r_in,) = _scatter_finish(scatter_m, grad_x, "mixer")
    reduced = [r_in, r_out, r_up0, r_down0, r_pool, r_up1, r_down1]
    moments = [(m_w_in_0, v_w_in_0), (m_w_out_0, v_w_out_0), (m_w_up_0, v_w_up_0),
               (m_w_down_0, v_w_down_0), (m_pool_w_1, v_pool_w_1), (m_w_up_1, v_w_up_1),
               (m_w_down_1, v_w_down_1)]
    big_out = []
    for k, (w, g, (m, v)) in enumerate(zip(big, reduced, moments)):
        if w.shape[-1] % 128:
            view = lambda t: t.reshape(-1, t.shape[-1]).T
            back = lambda t: t.T.reshape(w.shape)
            g_view = g[:w.shape[-1]]
        else:
            view = lambda t: t.reshape(-1, t.shape[-1])
            back = lambda t: t.reshape(w.shape)
            g_view = view(g)
        delta_w, new_m, new_v = _adamw(view(w), g_view, view(m), view(v), "adamw_%d" % k)
        big_out.append((back(g_view), back(delta_w), back(new_m), back(new_v)))

    tail = jnp.concatenate([d_conv[0:3].reshape(-1)[d:], d_bf[0, :N_HEADS], loss_part[0, :1]])
    small_part = jnp.concatenate(
        [d_nmix0, d_nffn0, d_nmix1, d_pscale, d_nffn1, d_final,
         d_conv[0:3].reshape(1, -1)[:, :d],
         jnp.pad(tail, (0, d - tail.shape[0])).reshape(1, d)], axis=0)
    parts = _gather_small(small_part)

    chip = 2 * lax.axis_index("x") + lax.axis_index("y")
    cw_cols = conv_w_0.shape[1]

    def conv_block(full):
        mine = lax.dynamic_slice_in_dim(full, chip * cw_cols, cw_cols, axis=1)
        return jnp.pad(mine.reshape(-1), (0, d - mine.size))

    def small_rows(vals, cw, bfv):
        return jnp.stack(list(vals) + [cw, jnp.pad(bfv, (0, d - N_HEADS))])

    smalls_w = [norm_mix_0, norm_ffn_0, norm_mix_1, pool_scale_1, norm_ffn_1, final_norm]
    smalls_m = [m_norm_mix_0, m_norm_ffn_0, m_norm_mix_1, m_pool_scale_1, m_norm_ffn_1, m_final_norm]
    smalls_v = [v_norm_mix_0, v_norm_ffn_0, v_norm_mix_1, v_pool_scale_1, v_norm_ffn_1, v_final_norm]
    pad_cw = lambda t: jnp.pad(t.reshape(-1), (0, d - t.size))
    w_rows = small_rows(smalls_w, pad_cw(conv_w_0), b_f_0)
    m_rows = small_rows(smalls_m, pad_cw(m_conv_w_0), m_b_f_0)
    v_rows = small_rows(smalls_v, pad_cw(v_conv_w_0), v_b_f_0)

    g_sum = _sum_devices(parts)
    conv_full = jnp.concatenate([g_sum[6], g_sum[7, :3 * c_conv - d]]).reshape(3, c_conv)
    bf_grad = g_sum[7, 3 * c_conv - d:3 * c_conv - d + N_HEADS]
    loss = g_sum[7, 3 * c_conv - d + N_HEADS]
    g_rows = jnp.concatenate(
        [g_sum[0:6], conv_block(conv_full).reshape(1, d),
         jnp.pad(bf_grad, (0, d - N_HEADS)).reshape(1, d)], axis=0)
    d_rows, nm_rows, nv_rows = _adamw(w_rows, g_rows, m_rows, v_rows, "adamw_small")

    def unpack(rows):
        cw = rows[6, :conv_w_0.size].reshape(conv_w_0.shape)
        return [rows[0], rows[1], rows[2], rows[3], rows[4], rows[5], cw, rows[7, :N_HEADS]]

    def assemble(kind):
        sm = unpack([g_rows, d_rows, nm_rows, nv_rows][kind])
        lg = [t[kind] for t in big_out]
        return [sm[0], lg[0], sm[7], sm[6], lg[1], sm[1], lg[2], lg[3],
                sm[2], lg[4], sm[3], sm[4], lg[5], lg[6], sm[5]]

    return (loss, grad_x[None], *assemble(0), *assemble(1), *assemble(2), *assemble(3))
```

```python
import functools

import jax
import jax.numpy as jnp
from jax import lax
from jax.experimental import pallas as pl
from jax.experimental.pallas import tpu as pltpu

F32 = jnp.float32
BF16 = jnp.bfloat16

RMS_EPS = 1e-6
HEAD_DIM = 64
N_HEADS = 8
ATTN_SCALE = HEAD_DIM ** -0.5
LOG2_E = 1.4426950408889634
POOL_WINDOWS = (2, 4, 8, 16)
POOL_HALO = 16
CONV_HALO = 8
NEG_BIG = -1e30

ADAM_LR = 0.001
ADAM_B1 = 0.9
ADAM_B2 = 0.999
ADAM_EPS = 1e-08
ADAM_WD = 0.01
ADAM_STEP = 10

N_CHIPS = 4
N_DEV = 8
MESH = pl.DeviceIdType.MESH

VMEM_LIMIT_BYTES = 56 * 1024 * 1024

TILE_ROWS = 512
TILE_PROJ_ROWS = 1024
TILE_ATTN = 512
TILE_MLP_ROWS = 1024
TILE_MLP_FF = 1024
TILE_MLP_BWD_FF = 512
TILE_HEAD_ROWS = 256
TILE_WGRAD_K = 1024
TILE_WGRAD_N = 1024
TILE_ELEM_ROWS = 256
SUM_CHUNK_ROWS = 128

LANE_CQ = 64
LANE_CK = 88


def _params(semantics):
    return pltpu.CompilerParams(dimension_semantics=semantics,
                                vmem_limit_bytes=VMEM_LIMIT_BYTES)


def _nn(a, b):
    return lax.dot_general(a, b, (((1,), (0,)), ((), ())), preferred_element_type=F32)


def _nt(a, b):
    return lax.dot_general(a, b, (((1,), (1,)), ((), ())), preferred_element_type=F32)


def _tn(a, b):
    return lax.dot_general(a, b, (((0,), (0,)), ((), ())), preferred_element_type=F32)


def _split3(v):
    hi = v.astype(BF16)
    r1 = v - hi.astype(F32)
    mid = r1.astype(BF16)
    lo = (r1 - mid.astype(F32)).astype(BF16)
    return hi, mid, lo


def _exact_nn(sel, v):
    hi, mid, lo = _split3(v)
    return _nn(sel, hi) + _nn(sel, mid) + _nn(sel, lo)


def _exact_nt(sel, v):
    hi, mid, lo = _split3(v)
    return _nt(sel, hi) + _nt(sel, mid) + _nt(sel, lo)


def _rms_fwd(x, g):
    r = lax.rsqrt(jnp.mean(x * x, axis=-1, keepdims=True) + RMS_EPS)
    return x * r * g, r


def _rms_bwd(dn, x, g):
    r = lax.rsqrt(jnp.mean(x * x, axis=-1, keepdims=True) + RMS_EPS)
    xh = x * r
    gy = dn * g
    dx = r * (gy - xh * jnp.mean(gy * xh, axis=-1, keepdims=True))
    return dx, jnp.sum(dn * xh, axis=0, keepdims=True)


def _lane(shape):
    return lax.broadcasted_iota(jnp.int32, shape, len(shape) - 1)


def _row(shape):
    return lax.broadcasted_iota(jnp.int32, shape, len(shape) - 2)


def _full(a):
    nd = a.ndim
    return pl.BlockSpec(a.shape, lambda *_: (0,) * nd)


def _rms_pre(after, x, g):
    s, d = x.shape
    tm = min(TILE_ROWS, s)

    def body(after_ref, x_ref, g_ref, n_ref):
        n, _ = _rms_fwd(x_ref[...], g_ref[...])
        n_ref[...] = n.astype(BF16)

    rows = pl.BlockSpec((tm, d), lambda i: (i, 0))
    return pl.pallas_call(
        body, name="rms_pre", grid=(s // tm,),
        in_specs=[ANY, rows, _full(g)], out_specs=rows,
        out_shape=jax.ShapeDtypeStruct((s, d), BF16),
        compiler_params=_params(("parallel",)),
    )(after, x, g)


def _in_proj(n, w_qkv, w_f, w_bcx):
    s, d = n.shape
    tm = min(TILE_PROJ_ROWS, s)

    def body(n_ref, wq_ref, wf_ref, wb_ref, qkv_ref, fl_ref, bcx_ref):
        nb = n_ref[...]
        qkv_ref[...] = _nn(nb, wq_ref[...]).astype(BF16)
        fl_ref[...] = _nn(nb, wf_ref[...])
        bcx_ref[...] = _nn(nb, wb_ref[...])

    rows = lambda c: pl.BlockSpec((tm, c), lambda i: (i, 0))
    return pl.pallas_call(
        body, name="in_proj", grid=(s // tm,),
        in_specs=[rows(d), _full(w_qkv), _full(w_f), _full(w_bcx)],
        out_specs=[rows(w_qkv.shape[1]), rows(w_f.shape[1]), rows(w_bcx.shape[1])],
        out_shape=[jax.ShapeDtypeStruct((s, w_qkv.shape[1]), BF16),
                   jax.ShapeDtypeStruct((s, w_f.shape[1]), F32),
                   jax.ShapeDtypeStruct((s, w_bcx.shape[1]), F32)],
        compiler_params=_params(("parallel",)),
    )(n, w_qkv, w_f, w_bcx)


def _gate_prep(fl, bf, qkv):
    s = fl.shape[0]
    a = N_HEADS * HEAD_DIM
    tm = min(TILE_ROWS, s)

    def body(fl_ref, bf_ref, q_ref, k_ref, qa_ref, ka_ref, carry_ref):
        i = pl.program_id(0)

        @pl.when(i == 0)
        def _():
            carry_ref[...] = jnp.zeros_like(carry_ref)

        z = fl_ref[...] + bf_ref[...]
        logf = jnp.minimum(z, 0.0) - jnp.log(1.0 + jnp.exp(-jnp.abs(z)))
        lower = (_lane((tm, tm)) <= _row((tm, tm))).astype(BF16)
        cum = _exact_nn(lower, logf) + carry_ref[0:1, :]
        carry_ref[0:1, :] = cum[tm - 1:tm, :]

        lane = _lane((tm, 128))
        pieces = [p.astype(F32)
                  for p in _split3(jnp.where(lane < N_HEADS, LOG2_E * cum, 0.0))]
        shared_q = sum(pltpu.roll(p, LANE_CQ + N_HEADS * k, axis=1) for k, p in enumerate(pieces))
        shared_k = -sum(pltpu.roll(p, LANE_CK + N_HEADS * k, axis=1) for k, p in enumerate(pieces))
        for h in range(N_HEADS):
            at_q = functools.reduce(jnp.logical_or,
                                    [lane == LANE_CQ + N_HEADS * k + h for k in range(3)])
            at_k = functools.reduce(jnp.logical_or,
                                    [lane == LANE_CK + N_HEADS * k + h for k in range(3)])
            pair = slice((h // 2) * 128, (h // 2 + 1) * 128)
            qp = q_ref[:, pair].astype(F32)
            kp = k_ref[:, pair].astype(F32)
            if h % 2:
                qp = pltpu.roll(qp, HEAD_DIM, axis=1)
                kp = pltpu.roll(kp, HEAD_DIM, axis=1)
            q_bias = jnp.where(at_k, 1.0, shared_q)
            k_bias = jnp.where(at_q, 1.0, shared_k)
            qa_ref[h] = jnp.where(lane < HEAD_DIM, qp * (ATTN_SCALE * LOG2_E), q_bias).astype(BF16)
            ka_ref[h] = jnp.where(lane < HEAD_DIM, kp, k_bias).astype(BF16)

    aug = jax.ShapeDtypeStruct((N_HEADS, s, 128), BF16)
    aug_spec = pl.BlockSpec((N_HEADS, tm, 128), lambda i: (0, i, 0))
    return pl.pallas_call(
        body, name="gate_prep", grid=(s // tm,),
        in_specs=[pl.BlockSpec((tm, 128), lambda i: (i, 0)), _full(bf),
                  pl.BlockSpec((tm, a), lambda i: (i, 0)),
                  pl.BlockSpec((tm, a), lambda i: (i, 1))],
        out_specs=[aug_spec, aug_spec],
        out_shape=[aug, aug],
        scratch_shapes=[pltpu.VMEM((8, 128), F32)],
        compiler_params=_params(("arbitrary",)),
    )(fl, bf, qkv, qkv)


def _attn_fwd(qa, ka, qkv):
    s = qa.shape[1]
    a = N_HEADS * HEAD_DIM
    t = min(TILE_ATTN, s)
    n_pairs = N_HEADS // 2
    v_block0 = 2 * a // 128
    ones_lane = (HEAD_DIM, 0)

    def body(qa_ref, ka_ref, v_ref, o_ref, lse_ref, m_ref, acc_ref, s_even, s_odd):
        i = pl.program_id(1)
        m_ref[...] = jnp.full_like(m_ref, NEG_BIG)
        acc_ref[...] = jnp.zeros_like(acc_ref)
        upper_rows = _row((128, t)) < HEAD_DIM

        def keys(j):
            return pl.ds(pl.multiple_of(j * t, t), t)

        def scores_into(buf, j):
            for e in range(2):
                buf[e] = _nt(ka_ref[e, keys(j), :], qa_ref[e])

        def consume(buf, j, masked):
            vf = v_ref[keys(j), :].astype(F32)
            lane = _lane((t, 128))
            own = [lane < HEAD_DIM, lane >= HEAD_DIM]
            for e in range(2):
                v_head = jnp.where(own[e], vf, jnp.where(lane == ones_lane[e], 1.0, 0.0)).astype(BF16)
                sc = buf[e]
                if masked:
                    sc = jnp.where(_row((t, t)) <= _lane((t, t)), sc, NEG_BIG)
                m_prev = m_ref[e]
                m_new = jnp.maximum(m_prev, jnp.max(sc, axis=0, keepdims=True))
                p = jnp.exp2(sc - m_new).astype(BF16)
                acc_ref[e] = acc_ref[e] * jnp.exp2(m_prev - m_new) + _tn(v_head, p)
                m_ref[e] = m_new

        scores_into(s_even, 0)

        def two_tiles(p, carry):
            j = 2 * p
            scores_into(s_odd, j + 1)
            consume(s_even, j, False)
            scores_into(s_even, j + 2)
            consume(s_odd, j + 1, False)
            return carry

        lax.fori_loop(0, i // 2, two_tiles, 0)

        @pl.when(i % 2 == 0)
        def _():
            consume(s_even, i, True)

        @pl.when(i % 2 == 1)
        def _():
            scores_into(s_odd, i)
            consume(s_even, i - 1, False)
            consume(s_odd, i, True)

        denom = [acc_ref[e, ones_lane[e]:ones_lane[e] + 1, :] for e in range(2)]
        out_t = jnp.where(upper_rows, acc_ref[0] / denom[0], acc_ref[1] / denom[1])
        o_ref[...] = out_t.T.astype(BF16)
        lse = [m_ref[e] + LOG2_E * jnp.log(denom[e]) for e in range(2)]
        lse_ref[...] = jnp.where(_row((8, t)) == 0, lse[0], lse[1])

    return pl.pallas_call(
        body, name="attn_fwd", grid=(n_pairs, s // t),
        in_specs=[pl.BlockSpec((2, t, 128), lambda g, i: (g, i, 0)),
                  pl.BlockSpec((2, s, 128), lambda g, i: (g, 0, 0)),
                  pl.BlockSpec((s, 128), lambda g, i: (0, v_block0 + g))],
        out_specs=[pl.BlockSpec((t, 128), lambda g, i: (i, g)),
                   pl.BlockSpec((None, 8, t), lambda g, i: (g, 0, i))],
        out_shape=[jax.ShapeDtypeStruct((s, a), BF16),
                   jax.ShapeDtypeStruct((n_pairs, 8, s), F32)],
        scratch_shapes=[pltpu.VMEM((2, 1, t), F32), pltpu.VMEM((2, 128, t), F32),
                        pltpu.VMEM((2, t, t), F32), pltpu.VMEM((2, t, t), F32)],
        compiler_params=_params(("parallel", "arbitrary")),
    )(qa, ka, qkv)


def _conv_out(o, bcx, cw, w_out, x):
    s, d = x.shape
    c = o.shape[1]
    tm = min(TILE_ROWS, s)

    def body(o_ref, b_ref, c_ref, xin_ref, cw_ref, w_ref, x_ref, h_ref, ubuf):
        i = pl.program_id(0)

        @pl.when(i == 0)
        def _():
            ubuf[0:CONV_HALO, :] = jnp.zeros((CONV_HALO, c), F32)

        u = c_ref[...] * xin_ref[...]
        ubuf[CONV_HALO:CONV_HALO + tm, :] = u
        u1 = ubuf[CONV_HALO - 1:CONV_HALO - 1 + tm, :]
        u2 = ubuf[CONV_HALO - 2:CONV_HALO - 2 + tm, :]
        cv = (cw_ref[0:1, :] * u2 + cw_ref[1:2, :] * u1) + cw_ref[2:3, :] * u
        y = (b_ref[...] * cv).astype(BF16)
        mix = _nn(o_ref[...], w_ref[0:c, :]) + _nn(y, w_ref[c:2 * c, :])
        h_ref[...] = x_ref[...] + mix
        ubuf[0:CONV_HALO, :] = u[tm - CONV_HALO:tm, :]

    col = lambda k: pl.BlockSpec((tm, c), lambda i: (i, k))
    return pl.pallas_call(
        body, name="conv_out", grid=(s // tm,),
        in_specs=[col(0), col(0), col(1), col(2), _full(cw), _full(w_out),
                  pl.BlockSpec((tm, d), lambda i: (i, 0))],
        out_specs=pl.BlockSpec((tm, d), lambda i: (i, 0)),
        out_shape=jax.ShapeDtypeStruct((s, d), F32),
        scratch_shapes=[pltpu.VMEM((tm + CONV_HALO, c), F32)],
        compiler_params=_params(("arbitrary",)),
    )(o, bcx, bcx, bcx, cw, w_out, x)


def _mlp_fwd(h, g, w_up, w_down, name, head=None):
    s, d = h.shape
    ff = w_down.shape[0]
    slot_cols = w_up.shape[2]
    tm = min(TILE_MLP_ROWS, s)
    tf = min(TILE_MLP_FF, slot_cols)
    per_slot = slot_cols // tf
    nf = ff // tf
    n_head = 0 if head is None else 2
    chunk = min(TILE_HEAD_ROWS, tm)

    def body(*refs):
        h_ref, g_ref, wu_ref, wd_ref = refs[:4]
        out_ref, a_ref, n_ref = refs[4 + n_head:7 + n_head]
        nb_ref, acc_ref = refs[9 + n_head:11 + n_head] if head else refs[-2:]
        i = pl.program_id(0)
        f = pl.program_id(1)

        def target_copy():
            t_hbm, t_buf, t_sem = refs[5], refs[-2], refs[-1]
            return pltpu.make_async_copy(t_hbm.at[pl.ds(pl.multiple_of(i * tm, tm), tm), :],
                                         t_buf, t_sem)

        @pl.when(f == 0)
        def _():
            n, _ = _rms_fwd(h_ref[...], g_ref[...])
            nb = n.astype(BF16)
            nb_ref[...] = nb
            n_ref[...] = nb
            acc_ref[...] = jnp.zeros_like(acc_ref)
            if head is not None:
                target_copy().start()

        pre = _nn(nb_ref[...], wu_ref[...])
        a_ref[...] = pre.astype(BF16)
        r = jnp.square(jnp.maximum(pre, 0.0)).astype(BF16)
        acc_ref[...] += _nn(r, wd_ref[...])

        @pl.when(f == nf - 1)
        def _():
            if head is None:
                out_ref[...] = h_ref[...] + acc_ref[...]
            else:
                gf_ref, t_buf = refs[4], refs[-2]
                loss_ref, dg_ref = refs[7 + n_head:9 + n_head]
                target_copy().wait()
                part, dg = None, None
                for r0 in range(0, tm, chunk):
                    rows_ = slice(r0, r0 + chunk)
                    out = h_ref[rows_, :] + acc_ref[rows_, :]
                    y, _ = _rms_fwd(out, gf_ref[...])
                    err = y - t_buf[rows_, :]
                    p = 0.5 * jnp.sum(jnp.mean(err * err, axis=-1, keepdims=True), axis=0,
                                      keepdims=True)
                    dx, dgp = _rms_bwd(err / d, out, gf_ref[...])
                    out_ref[rows_, :] = dx
                    part = p if part is None else part + p
                    dg = dgp if dg is None else dg + dgp
                part = jnp.broadcast_to(part, loss_ref.shape)

                @pl.when(i == 0)
                def _():
                    loss_ref[...] = part
                    dg_ref[...] = dg

                @pl.when(i > 0)
                def _():
                    loss_ref[...] += part
                    dg_ref[...] += dg

    rows = pl.BlockSpec((tm, d), lambda i, f: (i, 0))
    in_specs = [rows, _full(g),
                pl.BlockSpec((None, d, tf), lambda i, f: (f // per_slot, 0, f % per_slot)),
                pl.BlockSpec((tf, d), lambda i, f: (f, 0))]
    out_specs = [rows, pl.BlockSpec((tm, tf), lambda i, f: (i, f)), rows]
    out_shape = [jax.ShapeDtypeStruct((s, d), F32), jax.ShapeDtypeStruct((s, ff), BF16),
                 jax.ShapeDtypeStruct((s, d), BF16)]
    args = [h, g, w_up, w_down]
    scratch = [pltpu.VMEM((tm, d), BF16), pltpu.VMEM((tm, d), F32)]
    if head is not None:
        in_specs += [_full(head[0]), ANY]
        args += list(head)
        out_specs += [pl.BlockSpec((1, 128), lambda i, f: (0, 0)),
                      pl.BlockSpec((1, d), lambda i, f: (0, 0))]
        out_shape += [jax.ShapeDtypeStruct((1, 128), F32), jax.ShapeDtypeStruct((1, d), F32)]
        scratch += [pltpu.VMEM((tm, d), F32), pltpu.SemaphoreType.DMA]
    return pl.pallas_call(
        body, name=name, grid=(s // tm, nf),
        in_specs=in_specs, out_specs=out_specs, out_shape=out_shape, scratch_shapes=scratch,
        compiler_params=_params(("parallel" if head is None else "arbitrary", "arbitrary")),
    )(*args)


def _window_sum_down(e, window):
    step = 1
    while step < window:
        e = e + pltpu.roll(e, step, axis=0)
        step *= 2
    return e


def _window_sum_up(e, window):
    n = e.shape[0]
    step = 1
    while step < window:
        e = e + pltpu.roll(e, n - step, axis=0)
        step *= 2
    return e


def _pool_counts(first_row, tm, window):
    t = first_row + _row((tm, 1))
    return jnp.minimum(t + 1, window).astype(F32)


def _pool_fwd(h, g, pw, ps):
    s, d = h.shape
    cg = d // len(POOL_WINDOWS)
    tm = min(TILE_ROWS, s)

    def body(h_ref, g_ref, pw_ref, ps_ref, out_ref, nbuf):
        i = pl.program_id(0)

        @pl.when(i == 0)
        def _():
            nbuf[0:POOL_HALO, :] = jnp.zeros((POOL_HALO, d), F32)

        n, _ = _rms_fwd(h_ref[...], g_ref[...])
        nbuf[POOL_HALO:POOL_HALO + tm, :] = n
        for k, window in enumerate(POOL_WINDOWS):
            cols = slice(k * cg, (k + 1) * cg)
            sums = _window_sum_down(nbuf[:, cols], window)[POOL_HALO:, :]
            pooled = sums / _pool_counts(i * tm, tm, window) - n[:, cols]
            y = _nn(pooled.astype(BF16), pw_ref[k]) * ps_ref[:, cols]
            out_ref[:, cols] = h_ref[:, cols] + y
        nbuf[0:POOL_HALO, :] = n[tm - POOL_HALO:tm, :]

    return pl.pallas_call(
        body, name="pool_fwd", grid=(s // tm,),
        in_specs=[pl.BlockSpec((tm, d), lambda i: (i, 0)), _full(g), _full(pw), _full(ps)],
        out_specs=pl.BlockSpec((tm, d), lambda i: (i, 0)),
        out_shape=jax.ShapeDtypeStruct((s, d), F32),
        scratch_shapes=[pltpu.VMEM((tm + POOL_HALO, d), F32)],
        compiler_params=_params(("arbitrary",)),
    )(h, g, pw, ps)


def _mlp_bwd_x(dz, a, w_up, w_down, h_in, g, name):
    s, d = dz.shape
    ff = w_down.shape[0]
    slot_cols = w_up.shape[2]
    tm = min(TILE_MLP_ROWS, s)
    tf = min(TILE_MLP_BWD_FF, slot_cols)
    per_slot = slot_cols // tf
    nf = ff // tf

    def body(dz_ref, a_ref, wu_ref, wd_ref, h_ref, g_ref, da_ref, dzb_ref, dh_ref, dg_ref,
             dzs_ref, acc_ref):
        i = pl.program_id(0)
        f = pl.program_id(1)

        @pl.when(f == 0)
        def _():
            dzb = dz_ref[...].astype(BF16)
            dzs_ref[...] = dzb
            dzb_ref[...] = dzb
            acc_ref[...] = jnp.zeros_like(acc_ref)

        dr = _nt(dzs_ref[...], wd_ref[...])
        da = (dr * (2.0 * jnp.maximum(a_ref[...].astype(F32), 0.0))).astype(BF16)
        da_ref[...] = da
        acc_ref[...] += _nt(da, wu_ref[...])

        @pl.when(f == nf - 1)
        def _():
            dx, dg = _rms_bwd(acc_ref[...], h_ref[...], g_ref[...])
            dh_ref[...] = dz_ref[...] + dx

            @pl.when(i == 0)
            def _():
                dg_ref[...] = dg

            @pl.when(i > 0)
            def _():
                dg_ref[...] += dg

    return pl.pallas_call(
        body, name=name, grid=(s // tm, nf),
        in_specs=[pl.BlockSpec((tm, d), lambda i, f: (i, 0)),
                  pl.BlockSpec((tm, tf), lambda i, f: (i, f)),
                  pl.BlockSpec((None, d, tf), lambda i, f: (f // per_slot, 0, f % per_slot)),
                  pl.BlockSpec((tf, d), lambda i, f: (f, 0)),
                  pl.BlockSpec((tm, d), lambda i, f: (i, 0)), _full(g)],
        out_specs=[pl.BlockSpec((tm, tf), lambda i, f: (i, f)),
                   pl.BlockSpec((tm, d), lambda i, f: (i, 0)),
                   pl.BlockSpec((tm, d), lambda i, f: (i, 0)),
                   pl.BlockSpec((1, d), lambda i, f: (0, 0))],
        out_shape=[jax.ShapeDtypeStruct((s, ff), BF16),
                   jax.ShapeDtypeStruct((s, d), BF16),
                   jax.ShapeDtypeStruct((s, d), F32),
                   jax.ShapeDtypeStruct((1, d), F32)],
        scratch_shapes=[pltpu.VMEM((tm, d), BF16), pltpu.VMEM((tm, d), F32)],
        compiler_params=_params(("arbitrary", "arbitrary")),
    )(dz, a, w_up, w_down, h_in, g)


def _mlp_bwd_w(n, da, a, dzb, slot_cols, name):
    s, d = n.shape
    ff = a.shape[1]
    tn = min(TILE_WGRAD_N, slot_cols)
    tk = min(TILE_WGRAD_K, s)
    per_slot = slot_cols // tn
    nk = s // tk

    def body(n_ref, da_ref, a_ref, dz_ref, du_ref, dd_ref, accu_ref, accd_ref):
        k = pl.program_id(1)

        @pl.when(k == 0)
        def _():
            accu_ref[...] = jnp.zeros_like(accu_ref)
            accd_ref[...] = jnp.zeros_like(accd_ref)

        accu_ref[...] += _tn(n_ref[...], da_ref[...])
        r = jnp.square(jnp.maximum(a_ref[...].astype(F32), 0.0)).astype(BF16)
        accd_ref[...] += _tn(r, dz_ref[...])

        @pl.when(k == nk - 1)
        def _():
            du_ref[...] = accu_ref[...].astype(BF16)
            dd_ref[...] = accd_ref[...].astype(BF16)

    return pl.pallas_call(
        body, name=name, grid=(ff // tn, nk),
        in_specs=[pl.BlockSpec((tk, d), lambda f, k: (k, 0)),
                  pl.BlockSpec((tk, tn), lambda f, k: (k, f)),
                  pl.BlockSpec((tk, tn), lambda f, k: (k, f)),
                  pl.BlockSpec((tk, d), lambda f, k: (k, 0))],
        out_specs=[pl.BlockSpec((None, d, tn), lambda f, k: (f // per_slot, 0, f % per_slot)),
                   pl.BlockSpec((tn, d), lambda f, k: (f, 0))],
        out_shape=[jax.ShapeDtypeStruct((ff // slot_cols, d, slot_cols), BF16),
                   jax.ShapeDtypeStruct((ff, d), BF16)],
        scratch_shapes=[pltpu.VMEM((d, tn), F32), pltpu.VMEM((tn, d), F32)],
        compiler_params=_params(("parallel", "arbitrary")),
    )(n, da, a, dzb)


def _pool_bwd(after, dh, h, g, pw, ps):
    s, d = h.shape
    cg = d // len(POOL_WINDOWS)
    tm = min(TILE_ROWS, s)
    nb = s // tm
    halo_per_tile = tm // POOL_HALO

    def body(after_ref, dh_ref, h_ref, halo_ref, g_ref, pw_ref, ps_ref,
             dx_ref, dpw_ref, dps_ref, dg_ref, nbuf, qbuf, dn_ref, carry, dpw_acc):
        i = pl.program_id(0)
        blk = nb - 1 - i

        @pl.when(i == 0)
        def _():
            carry[...] = jnp.zeros_like(carry)
            dpw_acc[...] = jnp.zeros_like(dpw_acc)
            dps_ref[...] = jnp.zeros_like(dps_ref)
            dg_ref[...] = jnp.zeros_like(dg_ref)

        hv = h_ref[...]
        n, _ = _rms_fwd(hv, g_ref[...])
        nh, _ = _rms_fwd(halo_ref[...], g_ref[...])
        nbuf[0:POOL_HALO, :] = jnp.where(blk == 0, 0.0, nh)
        nbuf[POOL_HALO:POOL_HALO + tm, :] = n
        dhv = dh_ref[...]
        for k, window in enumerate(POOL_WINDOWS):
            cols = slice(k * cg, (k + 1) * cg)
            cnt = _pool_counts(blk * tm, tm, window)
            sums = _window_sum_down(nbuf[:, cols], window)[POOL_HALO:, :]
            pb = (sums / cnt - n[:, cols]).astype(BF16)
            dyk = dhv[:, cols]
            dps_ref[:, cols] += jnp.sum(dyk * _nn(pb, pw_ref[k]), axis=0, keepdims=True)
            dyb = (dyk * ps_ref[:, cols]).astype(BF16)
            dpw_acc[k] += _tn(pb, dyb)
            dpool = _nt(dyb, pw_ref[k])
            qv = dpool / cnt
            qbuf[0:tm, cols] = qv
            qbuf[tm:tm + POOL_HALO, cols] = carry[:, cols]
            dn_ref[:, cols] = _window_sum_up(qbuf[:, cols], window)[0:tm, :] - dpool
            carry[:, cols] = qv[0:POOL_HALO, :]
        dx, dg = _rms_bwd(dn_ref[...], hv, g_ref[...])
        dx_ref[...] = dhv + dx
        dg_ref[...] += dg

        @pl.when(i == nb - 1)
        def _():
            dpw_ref[...] = dpw_acc[...].astype(BF16)

    rev = lambda i: (nb - 1 - i, 0)
    return pl.pallas_call(
        body, name="pool_bwd", grid=(nb,),
        in_specs=[ANY, pl.BlockSpec((tm, d), rev), pl.BlockSpec((tm, d), rev),
                  pl.BlockSpec((POOL_HALO, d),
                               lambda i: (jnp.maximum((nb - 1 - i) * halo_per_tile - 1, 0), 0)),
                  _full(g), _full(pw), _full(ps)],
        out_specs=[pl.BlockSpec((tm, d), rev), _full(pw),
                   pl.BlockSpec((1, d), lambda i: (0, 0)),
                   pl.BlockSpec((1, d), lambda i: (0, 0))],
        out_shape=[jax.ShapeDtypeStruct((s, d), F32),
                   jax.ShapeDtypeStruct(pw.shape, BF16),
                   jax.ShapeDtypeStruct((1, d), F32),
                   jax.ShapeDtypeStruct((1, d), F32)],
        scratch_shapes=[pltpu.VMEM((tm + POOL_HALO, d), F32), pltpu.VMEM((tm + POOL_HALO, d), F32),
                        pltpu.VMEM((tm, d), F32), pltpu.VMEM((POOL_HALO, d), F32),
                        pltpu.VMEM(pw.shape, F32)],
        compiler_params=_params(("arbitrary",)),
    )(after, dh, h, h, g, pw, ps)


def _conv_out_bwd(after, dh, w_out, o, bcx, cw):
    s, d = dh.shape
    c = o.shape[1]
    tm = min(TILE_ROWS, s)
    nb = s // tm
    halo_per_tile = tm // CONV_HALO

    def body(after_ref, dh_ref, w_ref, o_ref, b_ref, c_ref, xin_ref, ch_ref, xh_ref, cw_ref,
             do_ref, delta_ref, dbcx_ref, dw_ref, dcw_ref, ubuf, dbuf, carry, acc):
        i = pl.program_id(0)
        blk = nb - 1 - i

        @pl.when(i == 0)
        def _():
            carry[...] = jnp.zeros_like(carry)
            acc[...] = jnp.zeros_like(acc)
            dcw_ref[...] = jnp.zeros_like(dcw_ref)

        dm = dh_ref[...].astype(BF16)
        dcat = _nt(dm, w_ref[...])
        do = dcat[:, 0:c]
        dy = dcat[:, c:2 * c]
        do_ref[...] = do.astype(BF16)
        head_of_lane = lax.shift_right_logical(_lane((8, c)), HEAD_DIM.bit_length() - 1)
        heads = (head_of_lane == _row((8, c))).astype(BF16)
        delta_ref[...] = _exact_nt(heads, do * o_ref[...].astype(F32))

        cv_ = c_ref[...]
        xin = xin_ref[...]
        bv = b_ref[...]
        u = cv_ * xin
        ubuf[0:CONV_HALO, :] = jnp.where(blk == 0, 0.0, ch_ref[...] * xh_ref[...])
        ubuf[CONV_HALO:CONV_HALO + tm, :] = u
        u1 = ubuf[CONV_HALO - 1:CONV_HALO - 1 + tm, :]
        u2 = ubuf[CONV_HALO - 2:CONV_HALO - 2 + tm, :]
        w0, w1, w2 = cw_ref[0:1, :], cw_ref[1:2, :], cw_ref[2:3, :]
        cv = (w0 * u2 + w1 * u1) + w2 * u
        acc[0:c, :] += _tn(o_ref[...], dm)
        acc[c:2 * c, :] += _tn((bv * cv).astype(BF16), dm)

        dcv = dy * bv
        dcw_ref[0:1, :] += jnp.sum(dcv * u2, axis=0, keepdims=True)
        dcw_ref[1:2, :] += jnp.sum(dcv * u1, axis=0, keepdims=True)
        dcw_ref[2:3, :] += jnp.sum(dcv * u, axis=0, keepdims=True)
        dbuf[0:tm, :] = dcv
        dbuf[tm:tm + CONV_HALO, :] = carry[...]
        du = w2 * dcv + w1 * dbuf[1:1 + tm, :] + w0 * dbuf[2:2 + tm, :]
        dbcx_ref[:, 0:c] = (dy * cv).astype(BF16)
        dbcx_ref[:, c:2 * c] = (du * xin).astype(BF16)
        dbcx_ref[:, 2 * c:3 * c] = (du * cv_).astype(BF16)
        carry[...] = dcv[0:CONV_HALO, :]

        @pl.when(i == nb - 1)
        def _():
            dw_ref[...] = acc[...].astype(BF16)

    rev = lambda k: (lambda i: (nb - 1 - i, k))
    halo = lambda k: (lambda i: (jnp.maximum((nb - 1 - i) * halo_per_tile - 1, 0), k))
    return pl.pallas_call(
        body, name="conv_out_bwd", grid=(nb,),
        in_specs=[ANY, pl.BlockSpec((tm, d), rev(0)), _full(w_out), pl.BlockSpec((tm, c), rev(0)),
                  pl.BlockSpec((tm, c), rev(0)), pl.BlockSpec((tm, c), rev(1)),
                  pl.BlockSpec((tm, c), rev(2)),
                  pl.BlockSpec((CONV_HALO, c), halo(1)), pl.BlockSpec((CONV_HALO, c), halo(2)),
                  _full(cw)],
        out_specs=[pl.BlockSpec((tm, c), rev(0)),
                   pl.BlockSpec((8, tm), lambda i: (0, nb - 1 - i)),
                   pl.BlockSpec((tm, 3 * c), rev(0)),
                   _full(w_out), _full(cw)],
        out_shape=[jax.ShapeDtypeStruct((s, c), BF16),
                   jax.ShapeDtypeStruct((8, s), F32),
                   jax.ShapeDtypeStruct((s, 3 * c), BF16),
                   jax.ShapeDtypeStruct(w_out.shape, BF16),
                   jax.ShapeDtypeStruct(cw.shape, F32)],
        scratch_shapes=[pltpu.VMEM((tm + CONV_HALO, c), F32), pltpu.VMEM((tm + CONV_HALO, c), F32),
                        pltpu.VMEM((CONV_HALO, c), F32), pltpu.VMEM(w_out.shape, F32)],
        compiler_params=_params(("arbitrary",)),
    )(after, dh, w_out, o, bcx, bcx, bcx, bcx, bcx, cw)


def _attn_bwd(after, qa, ka, qkv, do, lse, delta):
    s = qa.shape[1]
    a = N_HEADS * HEAD_DIM
    t = min(TILE_ATTN, s)
    nq = s // t
    n_pairs = N_HEADS // 2
    v_block0 = 2 * a // 128

    def body(after_ref, ka_ref, v_ref, qa_ref, do_ref, lse_ref, delta_ref,
             dqt_ref, dka_ref, dv_ref, dv_acc):
        g = pl.program_id(0)
        j = pl.program_id(1)

        @pl.when(j == 0)
        def _():
            dqt_ref[...] = jnp.zeros_like(dqt_ref)

        lane = _lane((t, 128))
        vf = v_ref[...].astype(F32)
        v_heads = [jnp.where(lane < HEAD_DIM, vf, 0.0).astype(BF16),
                   jnp.where(lane >= HEAD_DIM, vf, 0.0).astype(BF16)]
        ke_t = [ka_ref[e].astype(F32).T.astype(BF16) for e in range(2)]

        def q_step(i, first):
            qs = pl.ds(pl.multiple_of(i * t, t), t)
            dob = do_ref[qs, :]
            for e in range(2):
                qe = qa_ref[e, qs, :]
                sc = _nt(ka_ref[e], qe)
                if first:
                    sc = jnp.where(_row((t, t)) <= _lane((t, t)), sc, NEG_BIG)
                p = jnp.exp2(sc - lse_ref[pl.ds(e, 1), qs])
                dv_part = _nn(p.astype(BF16), dob)
                dp = _nt(v_heads[e], dob)
                ds = (p * (dp - delta_ref[pl.ds(2 * g + e, 1), qs])).astype(BF16)
                dk_part = _nn(ds, qe)
                if first:
                    dv_acc[e] = dv_part
                    dka_ref[e] = dk_part
                else:
                    dv_acc[e] += dv_part
                    dka_ref[e] += dk_part
                dqt_ref[e, :, qs] += _nn(ke_t[e], ds)

        q_step(j, True)

        def full_step(i, carry):
            q_step(i, False)
            return carry

        lax.fori_loop(j + 1, nq, full_step, 0)
        dv_ref[...] = jnp.where(lane < HEAD_DIM, dv_acc[0], dv_acc[1]).astype(BF16)

    return pl.pallas_call(
        body, name="attn_bwd", grid=(n_pairs, nq),
        in_specs=[ANY, pl.BlockSpec((2, t, 128), lambda g, j: (g, j, 0)),
                  pl.BlockSpec((t, 128), lambda g, j: (j, v_block0 + g)),
                  pl.BlockSpec((2, s, 128), lambda g, j: (g, 0, 0)),
                  pl.BlockSpec((s, 128), lambda g, j: (0, g)),
                  pl.BlockSpec((None, 8, s), lambda g, j: (g, 0, 0)),
                  pl.BlockSpec((8, s), lambda g, j: (0, 0))],
        out_specs=[pl.BlockSpec((2, 128, s), lambda g, j: (g, 0, 0)),
                   pl.BlockSpec((2, t, 128), lambda g, j: (g, j, 0)),
                   pl.BlockSpec((t, 128), lambda g, j: (j, g))],
        out_shape=[jax.ShapeDtypeStruct((N_HEADS, 128, s), F32),
                   jax.ShapeDtypeStruct((N_HEADS, s, 128), F32),
                   jax.ShapeDtypeStruct((s, a), BF16)],
        scratch_shapes=[pltpu.VMEM((2, t, 128), F32)],
        compiler_params=_params(("parallel", "arbitrary")),
    )(after, ka, qkv, qa, do, lse, delta)


def _gate_bwd(dqa, dka, dv, fl, bf):
    s = fl.shape[0]
    a = N_HEADS * HEAD_DIM
    tm = min(TILE_ROWS, s)
    nb = s // tm

    def body(dqa_ref, dka_ref, dv_ref, fl_ref, bf_ref, dqkv_ref, dfl_ref, dbf_ref, carry):
        i = pl.program_id(0)

        @pl.when(i == 0)
        def _():
            carry[...] = jnp.zeros_like(carry)
            dbf_ref[...] = jnp.zeros_like(dbf_ref)

        lane = _lane((tm, 128))
        dq_sum = jnp.zeros((tm, 128), F32)
        dk_sum = jnp.zeros((tm, 128), F32)
        for pair in range(N_HEADS // 2):
            qs, ks = [], []
            for e in range(2):
                h = 2 * pair + e
                dq = dqa_ref[h].T
                dk = dka_ref[h]
                dq_sum = dq_sum + dq
                dk_sum = dk_sum + dk
                qs.append(dq * ATTN_SCALE)
                ks.append(dk * (1.0 / LOG2_E))
            cols = slice(pair * 128, (pair + 1) * 128)
            dqkv_ref[:, cols] = jnp.where(
                lane < HEAD_DIM, qs[0], pltpu.roll(qs[1], HEAD_DIM, axis=1)).astype(BF16)
            dqkv_ref[:, a + pair * 128:a + (pair + 1) * 128] = jnp.where(
                lane < HEAD_DIM, ks[0], pltpu.roll(ks[1], HEAD_DIM, axis=1)).astype(BF16)
        dqkv_ref[:, 2 * a:3 * a] = dv_ref[...]

        in_q = (lane >= LANE_CQ) & (lane < LANE_CQ + N_HEADS)
        in_k = (lane >= LANE_CK) & (lane < LANE_CK + N_HEADS)
        dcum = (pltpu.roll(jnp.where(in_q, dq_sum, 0.0), 128 - LANE_CQ, axis=1)
                - pltpu.roll(jnp.where(in_k, dk_sum, 0.0), 128 - LANE_CK, axis=1))

        upper = (_lane((tm, tm)) >= _row((tm, tm))).astype(BF16)
        dlogf = _exact_nn(upper, dcum) + carry[0:1, :]
        carry[0:1, :] = dlogf[0:1, :]
        z = fl_ref[...] + bf_ref[...]
        ez = jnp.exp(-jnp.abs(z))
        sig_neg = jnp.where(z >= 0.0, ez, 1.0) / (1.0 + ez)
        dz = jnp.where(lane < N_HEADS, dlogf * sig_neg, 0.0)
        dfl_ref[...] = dz.astype(BF16)
        dbf_ref[...] += jnp.sum(dz, axis=0, keepdims=True)

    rev3 = lambda i: (0, nb - 1 - i, 0)
    rev = lambda i: (nb - 1 - i, 0)
    return pl.pallas_call(
        body, name="gate_bwd", grid=(nb,),
        in_specs=[pl.BlockSpec((N_HEADS, 128, tm), lambda i: (0, 0, nb - 1 - i)),
                  pl.BlockSpec((N_HEADS, tm, 128), rev3),
                  pl.BlockSpec((tm, a), rev), pl.BlockSpec((tm, 128), rev), _full(bf)],
        out_specs=[pl.BlockSpec((tm, 3 * a), rev), pl.BlockSpec((tm, 128), rev),
                   pl.BlockSpec((1, 128), lambda i: (0, 0))],
        out_shape=[jax.ShapeDtypeStruct((s, 3 * a), BF16),
                   jax.ShapeDtypeStruct((s, 128), BF16),
                   jax.ShapeDtypeStruct((1, 128), F32)],
        scratch_shapes=[pltpu.VMEM((8, 128), F32)],
        compiler_params=_params(("arbitrary",)),
    )(dqa, dka, dv, fl, bf)


def _in_proj_bwd(after, dqkv, dfl, dbcx, w_qkv, w_f, w_bcx, x, g, dh):
    s, d = x.shape
    tm = min(TILE_PROJ_ROWS, s)

    def body(after_ref, dq_ref, df_ref, db_ref, wq_ref, wf_ref, wb_ref, x_ref, g_ref, dh_ref,
             gx_ref, dg_ref):
        i = pl.program_id(0)
        dn = (_nt(dq_ref[...], wq_ref[...]) + _nt(df_ref[...], wf_ref[...])
              + _nt(db_ref[...], wb_ref[...]))
        dx, dg = _rms_bwd(dn, x_ref[...], g_ref[...])
        gx_ref[...] = dh_ref[...] + dx

        @pl.when(i == 0)
        def _():
            dg_ref[...] = dg

        @pl.when(i > 0)
        def _():
            dg_ref[...] += dg

    rows = lambda c: pl.BlockSpec((tm, c), lambda i: (i, 0))
    return pl.pallas_call(
        body, name="in_proj_bwd", grid=(s // tm,),
        in_specs=[ANY, rows(dqkv.shape[1]), rows(dfl.shape[1]), rows(dbcx.shape[1]),
                  _full(w_qkv), _full(w_f), _full(w_bcx), rows(d), _full(g), rows(d)],
        out_specs=[rows(d), pl.BlockSpec((1, d), lambda i: (0, 0))],
        out_shape=[jax.ShapeDtypeStruct((s, d), F32), jax.ShapeDtypeStruct((1, d), F32)],
        compiler_params=_params(("arbitrary",)),
    )(after, dqkv, dfl, dbcx, w_qkv, w_f, w_bcx, x, g, dh)


def _wgrad_in(n, dys):
    s, d = n.shape
    m = len(dys)
    tk = min(TILE_ROWS, s)
    nk = s // tk

    def body(*refs):
        n_ref, dy_refs, dw_refs, accs = refs[0], refs[1:1 + m], refs[1 + m:1 + 2 * m], refs[1 + 2 * m:]
        k = pl.program_id(0)

        @pl.when(k == 0)
        def _():
            for acc in accs:
                acc[...] = jnp.zeros_like(acc)

        nb = n_ref[...]
        for dy_ref, acc in zip(dy_refs, accs):
            acc[...] += _tn(nb, dy_ref[...])

        @pl.when(k == nk - 1)
        def _():
            for dw_ref, acc in zip(dw_refs, accs):
                dw_ref[...] = acc[...].T.astype(BF16)

    return pl.pallas_call(
        body, name="wgrad_in", grid=(nk,),
        in_specs=[pl.BlockSpec((tk, d), lambda k: (k, 0))]
        + [pl.BlockSpec((tk, dy.shape[1]), lambda k: (k, 0)) for dy in dys],
        out_specs=[pl.BlockSpec((dy.shape[1], d), lambda k: (0, 0)) for dy in dys],
        out_shape=[jax.ShapeDtypeStruct((dy.shape[1], d), BF16) for dy in dys],
        scratch_shapes=[pltpu.VMEM((d, dy.shape[1]), F32) for dy in dys],
        compiler_params=_params(("arbitrary",)),
    )(n, *dys)


def _row_tile(rows):
    t = min(TILE_ELEM_ROWS, rows)
    while rows % t:
        t //= 2
    return t


def _adamw_math(w, g, m, v):
    m = ADAM_B1 * m + (1.0 - ADAM_B1) * g
    v = ADAM_B2 * v + (1.0 - ADAM_B2) * jnp.square(g)
    m_hat = m / (1.0 - ADAM_B1 ** ADAM_STEP)
    v_hat = v / (1.0 - ADAM_B2 ** ADAM_STEP)
    delta = -ADAM_LR * (m_hat / (jnp.sqrt(v_hat) + ADAM_EPS) + ADAM_WD * w)
    return delta, m, v


def _adamw(w, g, m, v, name):
    rows, cols = w.shape

    def body(w_ref, g_ref, m_ref, v_ref, d_ref, nm_ref, nv_ref):
        delta, nm, nv = _adamw_math(w_ref[...], g_ref[...], m_ref[...], v_ref[...])
        d_ref[...] = delta
        nm_ref[...] = nm
        nv_ref[...] = nv

    if rows % 8 == 0:
        tr = _row_tile(rows)
        grid, spec = (rows // tr,), pl.BlockSpec((tr, cols), lambda i: (i, 0))
    else:
        grid, spec = (cols // 256,), pl.BlockSpec((rows, 256), lambda i: (0, i))
    out = jax.ShapeDtypeStruct(w.shape, F32)
    return pl.pallas_call(
        body, name=name, grid=grid, in_specs=[spec] * 4, out_specs=[spec] * 3,
        out_shape=[out, out, out], compiler_params=_params(("parallel",)),
    )(w, g, m, v)


def _sum_devices(parts):
    def body(p_ref, g_ref):
        g = p_ref[0]
        for k in range(1, N_DEV):
            g = g + p_ref[k]
        g_ref[...] = g

    return pl.pallas_call(
        body, name="sum_devices",
        in_specs=[pl.BlockSpec(memory_space=pltpu.VMEM)],
        out_specs=pl.BlockSpec(memory_space=pltpu.VMEM),
        out_shape=jax.ShapeDtypeStruct(parts.shape[1:], F32),
    )(parts)


def _mesh_position():
    x, y, c = lax.axis_index("x"), lax.axis_index("y"), lax.axis_index("c")
    chips = [(1 - x, y), (x, 1 - y), (1 - x, 1 - y)]
    return x, y, c, chips


ANY = pl.BlockSpec(memory_space=pl.ANY)
HBM = pl.BlockSpec(memory_space=pltpu.HBM)
SEM = pl.BlockSpec(memory_space=pltpu.SEMAPHORE)
SPLIT_COPY_EFFECT = pltpu.SideEffectType.DATAFLOW_SIDE_EFFECTING


def _in_hbm(a):
    return pltpu.with_memory_space_constraint(a, pltpu.HBM)


def _chip_copies(views, srcs, lands, send, recv, waiting=False):
    _, _, c, chips = _mesh_position()
    cps = []
    for a in range(len(srcs)):
        for k, (px, py) in enumerate(chips):
            src, dst = views(a, k, srcs[a], lands[a], c, 2 * px + py)
            sem = a * (N_CHIPS - 1) + k
            cps.append(pltpu.make_async_remote_copy(
                src_ref=src, dst_ref=dst, send_sem=send.at[sem], recv_sem=recv.at[sem],
                device_id=(px, py, c), device_id_type=MESH))
    return cps


def _ici_start(sources, land_shapes, copies, after, name, per_array=N_CHIPS - 1):
    n = len(sources)

    def body(*refs):
        srcs, lands = refs[:n], refs[n:2 * n]
        send, recv = refs[2 * n + 1], refs[2 * n + 2]
        token = refs[-1]
        for cp in copies(srcs, lands, send, recv, False):
            cp.start()
        token[...] = jnp.zeros_like(token)

    lands = [_in_hbm(lax.empty(s.shape, s.dtype)) for s in land_shapes]
    outs = pl.pallas_call(
        body, name=name,
        in_specs=[HBM] * (2 * n) + [ANY],
        out_specs=[SEM, SEM] + [HBM] * (2 * n) + [pl.BlockSpec(memory_space=pltpu.VMEM)],
        out_shape=[pltpu.SemaphoreType.DMA((n * per_array,))] * 2
        + [pltpu.HBM(a.shape, a.dtype) for a in sources]
        + [pltpu.HBM(s.shape, s.dtype) for s in land_shapes]
        + [jax.ShapeDtypeStruct((8, 128), F32)],
        input_output_aliases={i: 2 + i for i in range(2 * n)},
        compiler_params=pltpu.CompilerParams(has_side_effects=SPLIT_COPY_EFFECT),
    )(*[_in_hbm(a) for a in sources], *lands, after)
    return outs[0], outs[1], list(outs[2:2 + n]), list(outs[2 + n:2 + 2 * n]), outs[-1]


def _ici_wait(handle, copies, after, name):
    send, recv, srcs, lands, _ = handle
    n = len(srcs)

    def body(*refs):
        src_refs, land_refs = refs[:n], refs[n:2 * n]
        for cp in copies(src_refs, land_refs, refs[2 * n], refs[2 * n + 1], True):
            cp.wait_send()
            cp.wait_recv()

    outs = pl.pallas_call(
        body, name=name,
        in_specs=[HBM] * (2 * n) + [SEM, SEM, ANY],
        out_specs=[HBM] * (2 * n),
        out_shape=[pltpu.HBM(a.shape, a.dtype) for a in srcs]
        + [pltpu.HBM(a.shape, a.dtype) for a in lands],
        input_output_aliases={i: i for i in range(2 * n)},
        compiler_params=pltpu.CompilerParams(has_side_effects=SPLIT_COPY_EFFECT),
    )(*srcs, *lands, send, recv, after)
    return list(outs[:n]), list(outs[n:])


def _gather_views(split):
    def views(a, k, src, land, c, slot):
        if split[a]:
            half = src.shape[0] // 2
            src = src.at[pl.ds(c * half, half)]
        return src, land.at[k]
    return views


def _gather_whole_views(a, k, src, land, c, slot):
    x, y, _, _ = _mesh_position()
    return src, land.at[2 * x + y]


SCATTER_COPIES = 2 * (N_CHIPS - 1)


def _scatter_copies(srcs, lands, send, recv, waiting):
    _, _, c, chips = _mesh_position()
    cps = []
    for a in range(len(srcs)):
        half = srcs[a].shape[1] // 2
        for k, (px, py) in enumerate(chips):
            for h in range(2):
                arrival = 2 * k + (h if waiting else c)
                cps.append(pltpu.make_async_remote_copy(
                    src_ref=srcs[a].at[2 * px + py, pl.ds(h * half, half)],
                    dst_ref=lands[a].at[arrival],
                    send_sem=send.at[a * SCATTER_COPIES + 2 * k + h],
                    recv_sem=recv.at[a * SCATTER_COPIES + arrival],
                    device_id=(px, py, h), device_id_type=MESH))
    return cps


def _gather_land_shapes(shards, split):
    return [jax.ShapeDtypeStruct(
        (N_CHIPS - 1, a.shape[0] // 2 if sp else a.shape[0]) + a.shape[1:], a.dtype)
        for a, sp in zip(shards, split)]


def _gather_finish(shards, lands, split, name):
    n = len(shards)
    ns = sum(split)
    d_index = {a: i for i, a in enumerate(a for a in range(n) if split[a])}

    def body(*refs):
        shard, land, outs = refs[:n], refs[n:2 * n], refs[2 * n:3 * n]
        obuf, fbuf = refs[3 * n:4 * n], refs[4 * n:5 * n]
        dbuf = refs[5 * n:5 * n + ns]
        ld_own, st_own, ld, st_mine, st_sib, send, recv = refs[5 * n + ns:]
        x, y, c, chips = _mesh_position()
        me = 2 * x + y
        own_loads, loads, sends, pending = [], {}, [], []
        for a in range(n):
            cp = pltpu.make_async_copy(shard[a], obuf[a], ld_own.at[a])
            cp.start()
            own_loads.append(cp)
        for a in range(n):
            for k in range(N_CHIPS - 1):
                cp = pltpu.make_async_copy(land[a].at[k], fbuf[a].at[k], ld.at[a, k])
                cp.start()
                loads[a, k] = cp
        for a in range(n):
            own_loads[a].wait()
            cp = pltpu.make_async_copy(obuf[a], outs[a].at[me], st_own.at[a])
            cp.start()
            pending.append(cp)
        for a in range(n):
            rows = shard[a].shape[0]
            for k, (px, py) in enumerate(chips):
                loads[a, k].wait()
                part = pl.ds(c * (rows // 2), rows // 2) if split[a] else pl.ds(0, rows)
                cp = pltpu.make_async_copy(fbuf[a].at[k], outs[a].at[2 * px + py, part],
                                           st_mine.at[a, k])
                cp.start()
                pending.append(cp)
                if split[a]:
                    fw = pltpu.make_async_remote_copy(
                        src_ref=fbuf[a].at[k], dst_ref=dbuf[d_index[a]].at[k],
                        send_sem=send.at[a, k], recv_sem=recv.at[a, k],
                        device_id=(x, y, 1 - c), device_id_type=MESH)
                    fw.start()
                    sends.append((a, k, fw))
        for a, k, fw in sends:
            px, py = chips[k]
            half = shard[a].shape[0] // 2
            fw.wait_recv()
            cp = pltpu.make_async_copy(dbuf[d_index[a]].at[k],
                                       outs[a].at[2 * px + py, pl.ds((1 - c) * half, half)],
                                       st_sib.at[a, k])
            cp.start()
            pending.append(cp)
        for _, _, fw in sends:
            fw.wait_send()
        for cp in pending:
            cp.wait()

    stage = [pltpu.VMEM(a.shape, a.dtype) for a in lands]
    dma = lambda *shape: pltpu.SemaphoreType.DMA(shape)
    return pl.pallas_call(
        body, name=name,
        in_specs=[ANY] * (2 * n), out_specs=[ANY] * n,
        out_shape=[jax.ShapeDtypeStruct((N_CHIPS,) + a.shape, a.dtype) for a in shards],
        scratch_shapes=[pltpu.VMEM(a.shape, a.dtype) for a in shards] + stage
        + [s for s, sp in zip(stage, split) if sp]
        + [dma(n), dma(n), dma(n, 3), dma(n, 3), dma(n, 3), dma(n, 3), dma(n, 3)],
        compiler_params=pltpu.CompilerParams(vmem_limit_bytes=VMEM_LIMIT_BYTES),
    )(*shards, *lands)


def _sum_chunk(rows):
    return next(r for r in range(SUM_CHUNK_ROWS, 0, -16) if rows % r == 0)


def _sum_and_share(partials, lands, name):
    n = len(partials)

    def body(*refs):
        own, landed, outs = refs[:n], refs[n:2 * n], refs[2 * n:3 * n]
        obuf, xbuf, ybuf, gbuf, sbuf, rbuf = (refs[(3 + k) * n:(4 + k) * n] for k in range(6))
        ld_own, ld_send, ld_got, st_own, st_sib, send_p, recv_p, send_s, recv_s = refs[9 * n:]
        x, y, c, _ = _mesh_position()
        me = 2 * x + y
        sibling = (x, y, 1 - c)

        def to_sibling(src, dst, send, recv, a):
            return pltpu.make_async_remote_copy(src_ref=src, dst_ref=dst, send_sem=send.at[a],
                                                recv_sem=recv.at[a], device_id=sibling,
                                                device_id_type=MESH)

        loads, firsts, seconds, stores = [], [], [], []
        for a in range(n):
            half = obuf[a].shape[0]
            cps = [pltpu.make_async_copy(own[a].at[me, pl.ds((1 - c) * half, half)], xbuf[a],
                                         ld_send.at[a]),
                   pltpu.make_async_copy(own[a].at[me, pl.ds(c * half, half)], obuf[a], ld_own.at[a]),
                   pltpu.make_async_copy(landed[a], gbuf[a], ld_got.at[a])]
            for cp in cps:
                cp.start()
            loads.append(cps)
        for a in range(n):
            loads[a][0].wait()
            rc = to_sibling(xbuf[a], ybuf[a], send_p, recv_p, a)
            rc.start()
            firsts.append(rc)
        for a in range(n):
            firsts[a].wait_recv()
            loads[a][1].wait()
            loads[a][2].wait()
            half = obuf[a].shape[0]
            rows = _sum_chunk(half)

            def add(k, carry, a=a, rows=rows):
                at = pl.ds(pl.multiple_of(k * rows, rows), rows)
                acc = obuf[a][at].astype(F32) + ybuf[a][at].astype(F32)
                for j in range(SCATTER_COPIES):
                    acc = acc + gbuf[a][j, at].astype(F32)
                sbuf[a][at] = acc
                return carry

            lax.fori_loop(0, half // rows, add, 0)
            rc = to_sibling(sbuf[a], rbuf[a], send_s, recv_s, a)
            rc.start()
            seconds.append(rc)
            cp = pltpu.make_async_copy(sbuf[a], outs[a].at[pl.ds(c * half, half)], st_own.at[a])
            cp.start()
            stores.append(cp)
        for a in range(n):
            half = obuf[a].shape[0]
            seconds[a].wait_recv()
            cp = pltpu.make_async_copy(rbuf[a], outs[a].at[pl.ds((1 - c) * half, half)], st_sib.at[a])
            cp.start()
            stores.append(cp)
        for rc in firsts + seconds:
            rc.wait_send()
        for cp in stores:
            cp.wait()

    halves = [(a.shape[1] // 2, a.shape[2]) for a in partials]
    return pl.pallas_call(
        body, name=name,
        in_specs=[ANY] * (2 * n), out_specs=[ANY] * n,
        out_shape=[jax.ShapeDtypeStruct((2 * h[0], h[1]), F32) for h in halves],
        scratch_shapes=[pltpu.VMEM(h, BF16) for h in halves] * 3
        + [pltpu.VMEM(g.shape, BF16) for g in lands]
        + [pltpu.VMEM(h, F32) for h in halves] * 2
        + [pltpu.SemaphoreType.DMA((n,))] * 9,
        compiler_params=pltpu.CompilerParams(vmem_limit_bytes=VMEM_LIMIT_BYTES),
    )(*partials, *lands)


def _gather_small(part):
    def body(in_ref, out_ref, send, recv, local):
        x, y, c, _ = _mesh_position()
        me = 4 * x + 2 * y + c
        cps = [pltpu.make_async_copy(in_ref, out_ref.at[me], local)]
        k = 0
        for fx in range(2):
            for fy in range(2):
                for fc in range(2):
                    if fx or fy or fc:
                        cps.append(pltpu.make_async_remote_copy(
                            src_ref=in_ref, dst_ref=out_ref.at[me], send_sem=send.at[k],
                            recv_sem=recv.at[k], device_id=(x ^ fx, y ^ fy, c ^ fc),
                            device_id_type=MESH))
                        k += 1
        for cp in cps:
            cp.start()
        for cp in cps:
            cp.wait()

    return pl.pallas_call(
        body, name="gather_small",
        in_specs=[pl.BlockSpec(memory_space=pltpu.VMEM)],
        out_specs=pl.BlockSpec(memory_space=pltpu.VMEM),
        out_shape=jax.ShapeDtypeStruct((N_DEV,) + part.shape, part.dtype),
        scratch_shapes=[pltpu.SemaphoreType.DMA((N_DEV - 1,)), pltpu.SemaphoreType.DMA((N_DEV - 1,)),
                        pltpu.SemaphoreType.DMA],
    )(part)


def _scatter_start(grads, after, tag):
    lands = [jax.ShapeDtypeStruct((SCATTER_COPIES, g.shape[1] // 2, g.shape[2]), g.dtype)
             for g in grads]
    return _ici_start(grads, lands, _scatter_copies, after, "scatter_start_" + tag,
                      per_array=SCATTER_COPIES)


def _scatter_finish(handles, after, tag):
    grads, lands = [], []
    for k, handle in enumerate(handles):
        g, l = _ici_wait(handle, _scatter_copies, after, "scatter_wait_%s_%d" % (tag, k))
        grads += g
        lands += l
    return _sum_and_share(grads, lands, "sum_and_share_" + tag)


def _pad_rows(a, rows):
    return jnp.pad(a, ((0, rows - a.shape[0]), (0, 0)))


def kernel(x, norm_mix_0, w_in_0, b_f_0, conv_w_0, w_out_0, norm_ffn_0, w_up_0, w_down_0, norm_mix_1, pool_w_1, pool_scale_1, norm_ffn_1, w_up_1, w_down_1, final_norm, loss_target, m_norm_mix_0, m_w_in_0, m_b_f_0, m_conv_w_0, m_w_out_0, m_norm_ffn_0, m_w_up_0, m_w_down_0, m_norm_mix_1, m_pool_w_1, m_pool_scale_1, m_norm_ffn_1, m_w_up_1, m_w_down_1, m_final_norm, v_norm_mix_0, v_w_in_0, v_b_f_0, v_conv_w_0, v_w_out_0, v_norm_ffn_0, v_w_up_0, v_w_down_0, v_norm_mix_1, v_pool_w_1, v_pool_scale_1, v_norm_ffn_1, v_w_up_1, v_w_down_1, v_final_norm):
    d = x.shape[-1]
    a = N_HEADS * HEAD_DIM
    c_conv = conv_w_0.shape[1] * N_CHIPS
    xs = x[0]
    target = loss_target[0]
    row = lambda vec: vec.reshape(1, -1)

    big = [w_in_0, w_out_0, w_up_0, w_down_0, pool_w_1, w_up_1, w_down_1]
    first = [w_in_0.astype(BF16)]
    first_split = [True]
    copies_a = functools.partial(_chip_copies, _gather_views(first_split))
    copies_b = functools.partial(_chip_copies, _gather_whole_views)
    start_a = _ici_start(first, _gather_land_shapes(first, first_split), copies_a, b_f_0,
                         "gather_start_a")
    zero = start_a[-1][0, 0]
    rest = [(w + zero).astype(BF16)
            for w in (w_out_0, w_up_0, w_down_0, pool_w_1, w_up_1, w_down_1)]
    rest = rest + [conv_w_0]
    start_b = _ici_start(rest, [jax.ShapeDtypeStruct((N_CHIPS,) + w.shape, w.dtype) for w in rest],
                         copies_b, start_a[-1], "gather_start_b")
    n0 = _rms_pre(start_b[-1], xs, row(norm_mix_0))
    first, land_a = _ici_wait(start_a, copies_a, n0, "gather_wait_a")
    (g_in,) = _gather_finish(first, land_a, first_split, "gather_finish_a")
    w_in = g_in.transpose(1, 0, 2).reshape(d, -1)
    w_qkv = w_in[:, :3 * a]
    w_f = jnp.pad(w_in[:, 3 * a:3 * a + N_HEADS], ((0, 0), (0, 128 - N_HEADS)))
    w_bcx = w_in[:, 3 * a + N_HEADS:]
    bf = jnp.pad(b_f_0, (0, 128 - N_HEADS)).reshape(1, 128)

    qkv, fl, bcx = _in_proj(n0, w_qkv, w_f, w_bcx)
    qa, ka = _gate_prep(fl, bf, qkv)
    o, lse = _attn_fwd(qa, ka, qkv)
    rest, land_b = _ici_wait(start_b, copies_b, o, "gather_wait_b")
    own_slot = 2 * lax.axis_index("x") + lax.axis_index("y")
    g_out, g_up0, g_down0, g_pool, g_up1, g_down1, g_conv = [
        lax.dynamic_update_index_in_dim(land, shard, own_slot, 0)
        for land, shard in zip(land_b, rest)]
    w_out = g_out.reshape(-1, d)
    conv_w = _pad_rows(g_conv.transpose(1, 0, 2).reshape(conv_w_0.shape[0], c_conv), 8)
    h1 = _conv_out(o, bcx, conv_w, w_out, xs)
    w_down0 = g_down0.reshape(-1, d)
    w_down1 = g_down1.reshape(-1, d)
    pool_w = g_pool.transpose(1, 0, 2, 3).reshape(pool_w_1.shape[0], -1, pool_w_1.shape[2])
    h2, a0, nf0 = _mlp_fwd(h1, row(norm_ffn_0), g_up0, w_down0, "mlp_fwd_0")
    h3 = _pool_fwd(h2, row(norm_mix_1), pool_w, row(pool_scale_1))
    dh4, a1, nf1, loss_part, d_final = _mlp_fwd(h3, row(norm_ffn_1), g_up1, w_down1, "mlp_fwd_1",
                                                head=(row(final_norm), target))

    slot_cols = g_up0.shape[2]
    pool_cols = pool_w.shape[2]
    da1, dz1, dh3, d_nffn1 = _mlp_bwd_x(dh4, a1, g_up1, w_down1, h3, row(norm_ffn_1), "mlp_bwd_x_1")
    dw_up1, dw_down1 = _mlp_bwd_w(nf1, da1, a1, dz1, slot_cols, "mlp_bwd_w_1")
    scatter_1 = _scatter_start([dw_up1, dw_down1.reshape(N_CHIPS, -1, d)], bf, "mlp1")
    dh2, dw_pool, d_pscale, d_nmix1 = _pool_bwd(scatter_1[-1], dh3, h2, row(norm_mix_1), pool_w,
                                                row(pool_scale_1))
    da0, dz0, dh1, d_nffn0 = _mlp_bwd_x(dh2, a0, g_up0, w_down0, h1, row(norm_ffn_0), "mlp_bwd_x_0")
    dw_up0, dw_down0 = _mlp_bwd_w(nf0, da0, a0, dz0, slot_cols, "mlp_bwd_w_0")
    dw_pool = (dw_pool.reshape(pool_w.shape[0], N_CHIPS, -1, pool_cols).transpose(1, 0, 2, 3)
               .reshape(N_CHIPS, -1, pool_cols))
    scatter_0 = _scatter_start([dw_up0, dw_down0.reshape(N_CHIPS, -1, d), dw_pool], bf, "mlp0")
    do, delta, dbcx, dw_out, d_conv = _conv_out_bwd(scatter_0[-1], dh1, w_out, o, bcx, conv_w)
    scatter_o = _scatter_start([dw_out.reshape(N_CHIPS, -1, d)], bf, "out")
    dqa, dka, dv = _attn_bwd(scatter_o[-1], qa, ka, qkv, do, lse, delta)
    dqkv, dfl, d_bf = _gate_bwd(dqa, dka, dv, fl, bf)
    dw_qkv, dw_f, dw_bcx = _wgrad_in(n0, [dqkv, dfl, dbcx])
    dw_in = jnp.concatenate([dw_qkv, dw_f[:N_HEADS], dw_bcx], axis=0).reshape(N_CHIPS, -1, d)
    slot_rows = -(-dw_in.shape[1] // 32) * 32
    dw_in = jnp.pad(dw_in, ((0, 0), (0, slot_rows - dw_in.shape[1]), (0, 0)))
    scatter_m = _scatter_start([dw_in], bf, "mixer")
    grad_x, d_nmix0 = _in_proj_bwd(scatter_m[-1], dqkv, dfl, dbcx, w_qkv, w_f, w_bcx, xs,
                                   row(norm_mix_0), dh1)

    r_up1, r_down1, r_out = _scatter_finish([scatter_1, scatter_o], grad_x, "early")
    r_up0, r_down0, r_pool, r_in = _scatter_finish([scatter_0, scatter_m], grad_x, "late")
    reduced = [r_in, r_out, r_up0, r_down0, r_pool, r_up1, r_down1]
    moments = [(m_w_in_0, v_w_in_0), (m_w_out_0, v_w_out_0), (m_w_up_0, v_w_up_0),
               (m_w_down_0, v_w_down_0), (m_pool_w_1, v_pool_w_1), (m_w_up_1, v_w_up_1),
               (m_w_down_1, v_w_down_1)]
    big_out = []
    for k, (w, g, (m, v)) in enumerate(zip(big, reduced, moments)):
        if w.shape[-1] % 128:
            view = lambda t: t.reshape(-1, t.shape[-1]).T
            back = lambda t: t.T.reshape(w.shape)
            g_view = g[:w.shape[-1]]
        else:
            view = lambda t: t.reshape(-1, t.shape[-1])
            back = lambda t: t.reshape(w.shape)
            g_view = view(g)
        delta_w, new_m, new_v = _adamw(view(w), g_view, view(m), view(v), "adamw_%d" % k)
        big_out.append((back(g_view), back(delta_w), back(new_m), back(new_v)))

    tail = jnp.concatenate([d_conv[0:3].reshape(-1)[d:], d_bf[0, :N_HEADS], loss_part[0, :1]])
    small_part = jnp.concatenate(
        [d_nmix0, d_nffn0, d_nmix1, d_pscale, d_nffn1, d_final,
         d_conv[0:3].reshape(1, -1)[:, :d],
         jnp.pad(tail, (0, d - tail.shape[0])).reshape(1, d)], axis=0)
    parts = _gather_small(small_part)

    chip = 2 * lax.axis_index("x") + lax.axis_index("y")
    cw_cols = conv_w_0.shape[1]

    def conv_block(full):
        mine = lax.dynamic_slice_in_dim(full, chip * cw_cols, cw_cols, axis=1)
        return jnp.pad(mine.reshape(-1), (0, d - mine.size))

    def small_rows(vals, cw, bfv):
        return jnp.stack(list(vals) + [cw, jnp.pad(bfv, (0, d - N_HEADS))])

    smalls_w = [norm_mix_0, norm_ffn_0, norm_mix_1, pool_scale_1, norm_ffn_1, final_norm]
    smalls_m = [m_norm_mix_0, m_norm_ffn_0, m_norm_mix_1, m_pool_scale_1, m_norm_ffn_1, m_final_norm]
    smalls_v = [v_norm_mix_0, v_norm_ffn_0, v_norm_mix_1, v_pool_scale_1, v_norm_ffn_1, v_final_norm]
    pad_cw = lambda t: jnp.pad(t.reshape(-1), (0, d - t.size))
    w_rows = small_rows(smalls_w, pad_cw(conv_w_0), b_f_0)
    m_rows = small_rows(smalls_m, pad_cw(m_conv_w_0), m_b_f_0)
    v_rows = small_rows(smalls_v, pad_cw(v_conv_w_0), v_b_f_0)

    g_sum = _sum_devices(parts)
    conv_full = jnp.concatenate([g_sum[6], g_sum[7, :3 * c_conv - d]]).reshape(3, c_conv)
    bf_grad = g_sum[7, 3 * c_conv - d:3 * c_conv - d + N_HEADS]
    loss = g_sum[7, 3 * c_conv - d + N_HEADS]
    g_rows = jnp.concatenate(
        [g_sum[0:6], conv_block(conv_full).reshape(1, d),
         jnp.pad(bf_grad, (0, d - N_HEADS)).reshape(1, d)], axis=0)
    d_rows, nm_rows, nv_rows = _adamw(w_rows, g_rows, m_rows, v_rows, "adamw_small")

    def unpack(rows):
        cw = rows[6, :conv_w_0.size].reshape(conv_w_0.shape)
        return [rows[0], rows[1], rows[2], rows[3], rows[4], rows[5], cw, rows[7, :N_HEADS]]

    def assemble(kind):
        sm = unpack([g_rows, d_rows, nm_rows, nv_rows][kind])
        lg = [t[kind] for t in big_out]
        return [sm[0], lg[0], sm[7], sm[6], lg[1], sm[1], lg[2], lg[3],
                sm[2], lg[4], sm[3], sm[4], lg[5], lg[6], sm[5]]

    return (loss, grad_x[None], *assemble(0), *assemble(1), *assemble(2), *assemble(3))
```

```python
import functools

import jax
import jax.numpy as jnp
from jax import lax
from jax.experimental import pallas as pl
from jax.experimental.pallas import tpu as pltpu

F32 = jnp.float32
BF16 = jnp.bfloat16

RMS_EPS = 1e-6
HEAD_DIM = 64
N_HEADS = 8
ATTN_SCALE = HEAD_DIM ** -0.5
LOG2_E = 1.4426950408889634
POOL_WINDOWS = (2, 4, 8, 16)
POOL_HALO = 16
CONV_HALO = 8
NEG_BIG = -1e30

ADAM_LR = 0.001
ADAM_B1 = 0.9
ADAM_B2 = 0.999
ADAM_EPS = 1e-08
ADAM_WD = 0.01
ADAM_STEP = 10

N_CHIPS = 4
N_DEV = 8
MESH = pl.DeviceIdType.MESH

VMEM_LIMIT_BYTES = 56 * 1024 * 1024

TILE_ROWS = 512
TILE_PROJ_ROWS = 1024
TILE_ATTN = 512
TILE_MLP_ROWS = 1024
TILE_MLP_FF = 1024
TILE_MLP_BWD_FF = 512
TILE_HEAD_ROWS = 256
TILE_WGRAD_K = 1024
TILE_WGRAD_N = 1024
TILE_ELEM_ROWS = 256
SUM_CHUNK_ROWS = 128

LANE_CQ = 64
LANE_CK = 88


def _params(semantics):
    return pltpu.CompilerParams(dimension_semantics=semantics,
                                vmem_limit_bytes=VMEM_LIMIT_BYTES)


def _nn(a, b):
    return lax.dot_general(a, b, (((1,), (0,)), ((), ())), preferred_element_type=F32)


def _nt(a, b):
    return lax.dot_general(a, b, (((1,), (1,)), ((), ())), preferred_element_type=F32)


def _tn(a, b):
    return lax.dot_general(a, b, (((0,), (0,)), ((), ())), preferred_element_type=F32)


def _split3(v):
    hi = v.astype(BF16)
    r1 = v - hi.astype(F32)
    mid = r1.astype(BF16)
    lo = (r1 - mid.astype(F32)).astype(BF16)
    return hi, mid, lo


def _exact_nn(sel, v):
    hi, mid, lo = _split3(v)
    return _nn(sel, hi) + _nn(sel, mid) + _nn(sel, lo)


def _exact_nt(sel, v):
    hi, mid, lo = _split3(v)
    return _nt(sel, hi) + _nt(sel, mid) + _nt(sel, lo)


def _rms_fwd(x, g):
    r = lax.rsqrt(jnp.mean(x * x, axis=-1, keepdims=True) + RMS_EPS)
    return x * r * g, r


def _rms_bwd(dn, x, g):
    r = lax.rsqrt(jnp.mean(x * x, axis=-1, keepdims=True) + RMS_EPS)
    xh = x * r
    gy = dn * g
    dx = r * (gy - xh * jnp.mean(gy * xh, axis=-1, keepdims=True))
    return dx, jnp.sum(dn * xh, axis=0, keepdims=True)


def _lane(shape):
    return lax.broadcasted_iota(jnp.int32, shape, len(shape) - 1)


def _row(shape):
    return lax.broadcasted_iota(jnp.int32, shape, len(shape) - 2)


def _full(a):
    nd = a.ndim
    return pl.BlockSpec(a.shape, lambda *_: (0,) * nd)


def _rms_pre(after, x, g):
    s, d = x.shape
    tm = min(TILE_ROWS, s)

    def body(after_ref, x_ref, g_ref, n_ref):
        n, _ = _rms_fwd(x_ref[...], g_ref[...])
        n_ref[...] = n.astype(BF16)

    rows = pl.BlockSpec((tm, d), lambda i: (i, 0))
    return pl.pallas_call(
        body, name="rms_pre", grid=(s // tm,),
        in_specs=[ANY, rows, _full(g)], out_specs=rows,
        out_shape=jax.ShapeDtypeStruct((s, d), BF16),
        compiler_params=_params(("parallel",)),
    )(after, x, g)


def _in_proj(n, w_qkv, w_f, w_bcx):
    s, d = n.shape
    tm = min(TILE_PROJ_ROWS, s)

    def body(n_ref, wq_ref, wf_ref, wb_ref, qkv_ref, fl_ref, bcx_ref):
        nb = n_ref[...]
        qkv_ref[...] = _nn(nb, wq_ref[...]).astype(BF16)
        fl_ref[...] = _nn(nb, wf_ref[...])
        bcx_ref[...] = _nn(nb, wb_ref[...])

    rows = lambda c: pl.BlockSpec((tm, c), lambda i: (i, 0))
    return pl.pallas_call(
        body, name="in_proj", grid=(s // tm,),
        in_specs=[rows(d), _full(w_qkv), _full(w_f), _full(w_bcx)],
        out_specs=[rows(w_qkv.shape[1]), rows(w_f.shape[1]), rows(w_bcx.shape[1])],
        out_shape=[jax.ShapeDtypeStruct((s, w_qkv.shape[1]), BF16),
                   jax.ShapeDtypeStruct((s, w_f.shape[1]), F32),
                   jax.ShapeDtypeStruct((s, w_bcx.shape[1]), F32)],
        compiler_params=_params(("parallel",)),
    )(n, w_qkv, w_f, w_bcx)


def _gate_prep(fl, bf, qkv):
    s = fl.shape[0]
    a = N_HEADS * HEAD_DIM
    tm = min(TILE_ROWS, s)

    def body(fl_ref, bf_ref, q_ref, k_ref, qa_ref, ka_ref, carry_ref):
        i = pl.program_id(0)

        @pl.when(i == 0)
        def _():
            carry_ref[...] = jnp.zeros_like(carry_ref)

        z = fl_ref[...] + bf_ref[...]
        logf = jnp.minimum(z, 0.0) - jnp.log(1.0 + jnp.exp(-jnp.abs(z)))
        lower = (_lane((tm, tm)) <= _row((tm, tm))).astype(BF16)
        cum = _exact_nn(lower, logf) + carry_ref[0:1, :]
        carry_ref[0:1, :] = cum[tm - 1:tm, :]

        lane = _lane((tm, 128))
        pieces = [p.astype(F32)
                  for p in _split3(jnp.where(lane < N_HEADS, LOG2_E * cum, 0.0))]
        shared_q = sum(pltpu.roll(p, LANE_CQ + N_HEADS * k, axis=1) for k, p in enumerate(pieces))
        shared_k = -sum(pltpu.roll(p, LANE_CK + N_HEADS * k, axis=1) for k, p in enumerate(pieces))
        for h in range(N_HEADS):
            at_q = functools.reduce(jnp.logical_or,
                                    [lane == LANE_CQ + N_HEADS * k + h for k in range(3)])
            at_k = functools.reduce(jnp.logical_or,
                                    [lane == LANE_CK + N_HEADS * k + h for k in range(3)])
            pair = slice((h // 2) * 128, (h // 2 + 1) * 128)
            qp = q_ref[:, pair].astype(F32)
            kp = k_ref[:, pair].astype(F32)
            if h % 2:
                qp = pltpu.roll(qp, HEAD_DIM, axis=1)
                kp = pltpu.roll(kp, HEAD_DIM, axis=1)
            q_bias = jnp.where(at_k, 1.0, shared_q)
            k_bias = jnp.where(at_q, 1.0, shared_k)
            qa_ref[h] = jnp.where(lane < HEAD_DIM, qp * (ATTN_SCALE * LOG2_E), q_bias).astype(BF16)
            ka_ref[h] = jnp.where(lane < HEAD_DIM, kp, k_bias).astype(BF16)

    aug = jax.ShapeDtypeStruct((N_HEADS, s, 128), BF16)
    aug_spec = pl.BlockSpec((N_HEADS, tm, 128), lambda i: (0, i, 0))
    return pl.pallas_call(
        body, name="gate_prep", grid=(s // tm,),
        in_specs=[pl.BlockSpec((tm, 128), lambda i: (i, 0)), _full(bf),
                  pl.BlockSpec((tm, a), lambda i: (i, 0)),
                  pl.BlockSpec((tm, a), lambda i: (i, 1))],
        out_specs=[aug_spec, aug_spec],
        out_shape=[aug, aug],
        scratch_shapes=[pltpu.VMEM((8, 128), F32)],
        compiler_params=_params(("arbitrary",)),
    )(fl, bf, qkv, qkv)


def _attn_fwd(qa, ka, qkv):
    s = qa.shape[1]
    a = N_HEADS * HEAD_DIM
    t = min(TILE_ATTN, s)
    n_pairs = N_HEADS // 2
    v_block0 = 2 * a // 128
    ones_lane = (HEAD_DIM, 0)

    def body(qa_ref, ka_ref, v_ref, o_ref, lse_ref, m_ref, acc_ref, s_even, s_odd):
        i = pl.program_id(1)
        m_ref[...] = jnp.full_like(m_ref, NEG_BIG)
        acc_ref[...] = jnp.zeros_like(acc_ref)
        upper_rows = _row((128, t)) < HEAD_DIM

        def keys(j):
            return pl.ds(pl.multiple_of(j * t, t), t)

        def scores_into(buf, j):
            for e in range(2):
                buf[e] = _nt(ka_ref[e, keys(j), :], qa_ref[e])

        def consume(buf, j, masked):
            vf = v_ref[keys(j), :].astype(F32)
            lane = _lane((t, 128))
            own = [lane < HEAD_DIM, lane >= HEAD_DIM]
            for e in range(2):
                v_head = jnp.where(own[e], vf, jnp.where(lane == ones_lane[e], 1.0, 0.0)).astype(BF16)
                sc = buf[e]
                if masked:
                    sc = jnp.where(_row((t, t)) <= _lane((t, t)), sc, NEG_BIG)
                m_prev = m_ref[e]
                m_new = jnp.maximum(m_prev, jnp.max(sc, axis=0, keepdims=True))
                p = jnp.exp2(sc - m_new).astype(BF16)
                acc_ref[e] = acc_ref[e] * jnp.exp2(m_prev - m_new) + _tn(v_head, p)
                m_ref[e] = m_new

        scores_into(s_even, 0)

        def two_tiles(p, carry):
            j = 2 * p
            scores_into(s_odd, j + 1)
            consume(s_even, j, False)
            scores_into(s_even, j + 2)
            consume(s_odd, j + 1, False)
            return carry

        lax.fori_loop(0, i // 2, two_tiles, 0)

        @pl.when(i % 2 == 0)
        def _():
            consume(s_even, i, True)

        @pl.when(i % 2 == 1)
        def _():
            scores_into(s_odd, i)
            consume(s_even, i - 1, False)
            consume(s_odd, i, True)

        denom = [acc_ref[e, ones_lane[e]:ones_lane[e] + 1, :] for e in range(2)]
        out_t = jnp.where(upper_rows, acc_ref[0] / denom[0], acc_ref[1] / denom[1])
        o_ref[...] = out_t.T.astype(BF16)
        lse = [m_ref[e] + LOG2_E * jnp.log(denom[e]) for e in range(2)]
        lse_ref[...] = jnp.where(_row((8, t)) == 0, lse[0], lse[1])

    return pl.pallas_call(
        body, name="attn_fwd", grid=(n_pairs, s // t),
        in_specs=[pl.BlockSpec((2, t, 128), lambda g, i: (g, i, 0)),
                  pl.BlockSpec((2, s, 128), lambda g, i: (g, 0, 0)),
                  pl.BlockSpec((s, 128), lambda g, i: (0, v_block0 + g))],
        out_specs=[pl.BlockSpec((t, 128), lambda g, i: (i, g)),
                   pl.BlockSpec((None, 8, t), lambda g, i: (g, 0, i))],
        out_shape=[jax.ShapeDtypeStruct((s, a), BF16),
                   jax.ShapeDtypeStruct((n_pairs, 8, s), F32)],
        scratch_shapes=[pltpu.VMEM((2, 1, t), F32), pltpu.VMEM((2, 128, t), F32),
                        pltpu.VMEM((2, t, t), F32), pltpu.VMEM((2, t, t), F32)],
        compiler_params=_params(("parallel", "arbitrary")),
    )(qa, ka, qkv)


def _conv_out(o, bcx, cw, w_out, x):
    s, d = x.shape
    c = o.shape[1]
    tm = min(TILE_PROJ_ROWS, s)

    def body(o_ref, b_ref, c_ref, xin_ref, cw_ref, w_ref, x_ref, h_ref, ubuf):
        i = pl.program_id(0)

        @pl.when(i == 0)
        def _():
            ubuf[0:CONV_HALO, :] = jnp.zeros((CONV_HALO, c), F32)

        u = c_ref[...] * xin_ref[...]
        ubuf[CONV_HALO:CONV_HALO + tm, :] = u
        u1 = ubuf[CONV_HALO - 1:CONV_HALO - 1 + tm, :]
        u2 = ubuf[CONV_HALO - 2:CONV_HALO - 2 + tm, :]
        cv = (cw_ref[0:1, :] * u2 + cw_ref[1:2, :] * u1) + cw_ref[2:3, :] * u
        y = (b_ref[...] * cv).astype(BF16)
        mix = _nn(o_ref[...], w_ref[0:c, :]) + _nn(y, w_ref[c:2 * c, :])
        h_ref[...] = x_ref[...] + mix
        ubuf[0:CONV_HALO, :] = u[tm - CONV_HALO:tm, :]

    col = lambda k: pl.BlockSpec((tm, c), lambda i: (i, k))
    return pl.pallas_call(
        body, name="conv_out", grid=(s // tm,),
        in_specs=[col(0), col(0), col(1), col(2), _full(cw), _full(w_out),
                  pl.BlockSpec((tm, d), lambda i: (i, 0))],
        out_specs=pl.BlockSpec((tm, d), lambda i: (i, 0)),
        out_shape=jax.ShapeDtypeStruct((s, d), F32),
        scratch_shapes=[pltpu.VMEM((tm + CONV_HALO, c), F32)],
        compiler_params=_params(("arbitrary",)),
    )(o, bcx, bcx, bcx, cw, w_out, x)


def _mlp_fwd(h, g, w_up, w_down, name, head=None):
    s, d = h.shape
    ff = w_down.shape[0]
    slot_cols = w_up.shape[2]
    tm = min(TILE_MLP_ROWS, s)
    tf = min(TILE_MLP_FF, slot_cols)
    per_slot = slot_cols // tf
    nf = ff // tf
    n_head = 0 if head is None else 2
    chunk = min(TILE_HEAD_ROWS, tm)

    def body(*refs):
        h_ref, g_ref, wu_ref, wd_ref = refs[:4]
        out_ref, a_ref, n_ref = refs[4 + n_head:7 + n_head]
        nb_ref, acc_ref = refs[9 + n_head:11 + n_head] if head else refs[-2:]
        i = pl.program_id(0)
        f = pl.program_id(1)

        def target_copy():
            t_hbm, t_buf, t_sem = refs[5], refs[-2], refs[-1]
            return pltpu.make_async_copy(t_hbm.at[pl.ds(pl.multiple_of(i * tm, tm), tm), :],
                                         t_buf, t_sem)

        @pl.when(f == 0)
        def _():
            n, _ = _rms_fwd(h_ref[...], g_ref[...])
            nb = n.astype(BF16)
            nb_ref[...] = nb
            n_ref[...] = nb
            acc_ref[...] = jnp.zeros_like(acc_ref)
            if head is not None:
                target_copy().start()

        pre = _nn(nb_ref[...], wu_ref[...])
        a_ref[...] = pre.astype(BF16)
        r = jnp.square(jnp.maximum(pre, 0.0)).astype(BF16)
        acc_ref[...] += _nn(r, wd_ref[...])

        @pl.when(f == nf - 1)
        def _():
            if head is None:
                out_ref[...] = h_ref[...] + acc_ref[...]
            else:
                gf_ref, t_buf = refs[4], refs[-2]
                loss_ref, dg_ref = refs[7 + n_head:9 + n_head]
                target_copy().wait()
                part, dg = None, None
                for r0 in range(0, tm, chunk):
                    rows_ = slice(r0, r0 + chunk)
                    out = h_ref[rows_, :] + acc_ref[rows_, :]
                    y, _ = _rms_fwd(out, gf_ref[...])
                    err = y - t_buf[rows_, :]
                    p = 0.5 * jnp.sum(jnp.mean(err * err, axis=-1, keepdims=True), axis=0,
                                      keepdims=True)
                    dx, dgp = _rms_bwd(err / d, out, gf_ref[...])
                    out_ref[rows_, :] = dx
                    part = p if part is None else part + p
                    dg = dgp if dg is None else dg + dgp
                part = jnp.broadcast_to(part, loss_ref.shape)

                @pl.when(i == 0)
                def _():
                    loss_ref[...] = part
                    dg_ref[...] = dg

                @pl.when(i > 0)
                def _():
                    loss_ref[...] += part
                    dg_ref[...] += dg

    rows = pl.BlockSpec((tm, d), lambda i, f: (i, 0))
    in_specs = [rows, _full(g),
                pl.BlockSpec((None, d, tf), lambda i, f: (f // per_slot, 0, f % per_slot)),
                pl.BlockSpec((tf, d), lambda i, f: (f, 0))]
    out_specs = [rows, pl.BlockSpec((tm, tf), lambda i, f: (i, f)), rows]
    out_shape = [jax.ShapeDtypeStruct((s, d), F32), jax.ShapeDtypeStruct((s, ff), BF16),
                 jax.ShapeDtypeStruct((s, d), BF16)]
    args = [h, g, w_up, w_down]
    scratch = [pltpu.VMEM((tm, d), BF16), pltpu.VMEM((tm, d), F32)]
    if head is not None:
        in_specs += [_full(head[0]), ANY]
        args += list(head)
        out_specs += [pl.BlockSpec((1, 128), lambda i, f: (0, 0)),
                      pl.BlockSpec((1, d), lambda i, f: (0, 0))]
        out_shape += [jax.ShapeDtypeStruct((1, 128), F32), jax.ShapeDtypeStruct((1, d), F32)]
        scratch += [pltpu.VMEM((tm, d), F32), pltpu.SemaphoreType.DMA]
    return pl.pallas_call(
        body, name=name, grid=(s // tm, nf),
        in_specs=in_specs, out_specs=out_specs, out_shape=out_shape, scratch_shapes=scratch,
        compiler_params=_params(("parallel" if head is None else "arbitrary", "arbitrary")),
    )(*args)


def _window_sum_down(e, window):
    step = 1
    while step < window:
        e = e + pltpu.roll(e, step, axis=0)
        step *= 2
    return e


def _window_sum_up(e, window):
    n = e.shape[0]
    step = 1
    while step < window:
        e = e + pltpu.roll(e, n - step, axis=0)
        step *= 2
    return e


def _pool_counts(first_row, tm, window):
    t = first_row + _row((tm, 1))
    return jnp.minimum(t + 1, window).astype(F32)


def _pool_fwd(h, g, pw, ps):
    s, d = h.shape
    cg = d // len(POOL_WINDOWS)
    tm = min(TILE_ROWS, s)

    def body(h_ref, g_ref, pw_ref, ps_ref, out_ref, nbuf):
        i = pl.program_id(0)

        @pl.when(i == 0)
        def _():
            nbuf[0:POOL_HALO, :] = jnp.zeros((POOL_HALO, d), F32)

        n, _ = _rms_fwd(h_ref[...], g_ref[...])
        nbuf[POOL_HALO:POOL_HALO + tm, :] = n
        for k, window in enumerate(POOL_WINDOWS):
            cols = slice(k * cg, (k + 1) * cg)
            sums = _window_sum_down(nbuf[:, cols], window)[POOL_HALO:, :]
            pooled = sums / _pool_counts(i * tm, tm, window) - n[:, cols]
            y = _nn(pooled.astype(BF16), pw_ref[k]) * ps_ref[:, cols]
            out_ref[:, cols] = h_ref[:, cols] + y
        nbuf[0:POOL_HALO, :] = n[tm - POOL_HALO:tm, :]

    return pl.pallas_call(
        body, name="pool_fwd", grid=(s // tm,),
        in_specs=[pl.BlockSpec((tm, d), lambda i: (i, 0)), _full(g), _full(pw), _full(ps)],
        out_specs=pl.BlockSpec((tm, d), lambda i: (i, 0)),
        out_shape=jax.ShapeDtypeStruct((s, d), F32),
        scratch_shapes=[pltpu.VMEM((tm + POOL_HALO, d), F32)],
        compiler_params=_params(("arbitrary",)),
    )(h, g, pw, ps)


def _mlp_bwd_x(dz, a, w_up, w_down, h_in, g, name):
    s, d = dz.shape
    ff = w_down.shape[0]
    slot_cols = w_up.shape[2]
    tm = min(TILE_MLP_ROWS, s)
    tf = min(TILE_MLP_BWD_FF, slot_cols)
    per_slot = slot_cols // tf
    nf = ff // tf

    def body(dz_ref, a_ref, wu_ref, wd_ref, h_ref, g_ref, da_ref, dzb_ref, dh_ref, dg_ref,
             dzs_ref, acc_ref):
        i = pl.program_id(0)
        f = pl.program_id(1)

        @pl.when(f == 0)
        def _():
            dzb = dz_ref[...].astype(BF16)
            dzs_ref[...] = dzb
            dzb_ref[...] = dzb
            acc_ref[...] = jnp.zeros_like(acc_ref)

        dr = _nt(dzs_ref[...], wd_ref[...])
        da = (dr * (2.0 * jnp.maximum(a_ref[...].astype(F32), 0.0))).astype(BF16)
        da_ref[...] = da
        acc_ref[...] += _nt(da, wu_ref[...])

        @pl.when(f == nf - 1)
        def _():
            dx, dg = _rms_bwd(acc_ref[...], h_ref[...], g_ref[...])
            dh_ref[...] = dz_ref[...] + dx

            @pl.when(i == 0)
            def _():
                dg_ref[...] = dg

            @pl.when(i > 0)
            def _():
                dg_ref[...] += dg

    return pl.pallas_call(
        body, name=name, grid=(s // tm, nf),
        in_specs=[pl.BlockSpec((tm, d), lambda i, f: (i, 0)),
                  pl.BlockSpec((tm, tf), lambda i, f: (i, f)),
                  pl.BlockSpec((None, d, tf), lambda i, f: (f // per_slot, 0, f % per_slot)),
                  pl.BlockSpec((tf, d), lambda i, f: (f, 0)),
                  pl.BlockSpec((tm, d), lambda i, f: (i, 0)), _full(g)],
        out_specs=[pl.BlockSpec((tm, tf), lambda i, f: (i, f)),
                   pl.BlockSpec((tm, d), lambda i, f: (i, 0)),
                   pl.BlockSpec((tm, d), lambda i, f: (i, 0)),
                   pl.BlockSpec((1, d), lambda i, f: (0, 0))],
        out_shape=[jax.ShapeDtypeStruct((s, ff), BF16),
                   jax.ShapeDtypeStruct((s, d), BF16),
                   jax.ShapeDtypeStruct((s, d), F32),
                   jax.ShapeDtypeStruct((1, d), F32)],
        scratch_shapes=[pltpu.VMEM((tm, d), BF16), pltpu.VMEM((tm, d), F32)],
        compiler_params=_params(("arbitrary", "arbitrary")),
    )(dz, a, w_up, w_down, h_in, g)


def _mlp_bwd_w(n, da, a, dzb, slot_cols, name):
    s, d = n.shape
    ff = a.shape[1]
    tn = min(TILE_WGRAD_N, slot_cols)
    tk = min(TILE_WGRAD_K, s)
    per_slot = slot_cols // tn
    nk = s // tk

    def body(n_ref, da_ref, a_ref, dz_ref, du_ref, dd_ref, accu_ref, accd_ref):
        k = pl.program_id(1)

        @pl.when(k == 0)
        def _():
            accu_ref[...] = jnp.zeros_like(accu_ref)
            accd_ref[...] = jnp.zeros_like(accd_ref)

        accu_ref[...] += _tn(n_ref[...], da_ref[...])
        r = jnp.square(jnp.maximum(a_ref[...].astype(F32), 0.0)).astype(BF16)
        accd_ref[...] += _tn(r, dz_ref[...])

        @pl.when(k == nk - 1)
        def _():
            du_ref[...] = accu_ref[...].astype(BF16)
            dd_ref[...] = accd_ref[...].astype(BF16)

    return pl.pallas_call(
        body, name=name, grid=(ff // tn, nk),
        in_specs=[pl.BlockSpec((tk, d), lambda f, k: (k, 0)),
                  pl.BlockSpec((tk, tn), lambda f, k: (k, f)),
                  pl.BlockSpec((tk, tn), lambda f, k: (k, f)),
                  pl.BlockSpec((tk, d), lambda f, k: (k, 0))],
        out_specs=[pl.BlockSpec((None, d, tn), lambda f, k: (f // per_slot, 0, f % per_slot)),
                   pl.BlockSpec((tn, d), lambda f, k: (f, 0))],
        out_shape=[jax.ShapeDtypeStruct((ff // slot_cols, d, slot_cols), BF16),
                   jax.ShapeDtypeStruct((ff, d), BF16)],
        scratch_shapes=[pltpu.VMEM((d, tn), F32), pltpu.VMEM((tn, d), F32)],
        compiler_params=_params(("parallel", "arbitrary")),
    )(n, da, a, dzb)


def _pool_bwd(after, dh, h, g, pw, ps):
    s, d = h.shape
    cg = d // len(POOL_WINDOWS)
    tm = min(TILE_ROWS, s)
    nb = s // tm
    halo_per_tile = tm // POOL_HALO

    def body(after_ref, dh_ref, h_ref, halo_ref, g_ref, pw_ref, ps_ref,
             dx_ref, dpw_ref, dps_ref, dg_ref, nbuf, qbuf, dn_ref, carry, dpw_acc):
        i = pl.program_id(0)
        blk = nb - 1 - i

        @pl.when(i == 0)
        def _():
            carry[...] = jnp.zeros_like(carry)
            dpw_acc[...] = jnp.zeros_like(dpw_acc)
            dps_ref[...] = jnp.zeros_like(dps_ref)
            dg_ref[...] = jnp.zeros_like(dg_ref)

        hv = h_ref[...]
        n, _ = _rms_fwd(hv, g_ref[...])
        nh, _ = _rms_fwd(halo_ref[...], g_ref[...])
        nbuf[0:POOL_HALO, :] = jnp.where(blk == 0, 0.0, nh)
        nbuf[POOL_HALO:POOL_HALO + tm, :] = n
        dhv = dh_ref[...]
        for k, window in enumerate(POOL_WINDOWS):
            cols = slice(k * cg, (k + 1) * cg)
            cnt = _pool_counts(blk * tm, tm, window)
            sums = _window_sum_down(nbuf[:, cols], window)[POOL_HALO:, :]
            pb = (sums / cnt - n[:, cols]).astype(BF16)
            dyk = dhv[:, cols]
            dps_ref[:, cols] += jnp.sum(dyk * _nn(pb, pw_ref[k]), axis=0, keepdims=True)
            dyb = (dyk * ps_ref[:, cols]).astype(BF16)
            dpw_acc[k] += _tn(pb, dyb)
            dpool = _nt(dyb, pw_ref[k])
            qv = dpool / cnt
            qbuf[0:tm, cols] = qv
            qbuf[tm:tm + POOL_HALO, cols] = carry[:, cols]
            dn_ref[:, cols] = _window_sum_up(qbuf[:, cols], window)[0:tm, :] - dpool
            carry[:, cols] = qv[0:POOL_HALO, :]
        dx, dg = _rms_bwd(dn_ref[...], hv, g_ref[...])
        dx_ref[...] = dhv + dx
        dg_ref[...] += dg

        @pl.when(i == nb - 1)
        def _():
            dpw_ref[...] = dpw_acc[...].astype(BF16)

    rev = lambda i: (nb - 1 - i, 0)
    return pl.pallas_call(
        body, name="pool_bwd", grid=(nb,),
        in_specs=[ANY, pl.BlockSpec((tm, d), rev), pl.BlockSpec((tm, d), rev),
                  pl.BlockSpec((POOL_HALO, d),
                               lambda i: (jnp.maximum((nb - 1 - i) * halo_per_tile - 1, 0), 0)),
                  _full(g), _full(pw), _full(ps)],
        out_specs=[pl.BlockSpec((tm, d), rev), _full(pw),
                   pl.BlockSpec((1, d), lambda i: (0, 0)),
                   pl.BlockSpec((1, d), lambda i: (0, 0))],
        out_shape=[jax.ShapeDtypeStruct((s, d), F32),
                   jax.ShapeDtypeStruct(pw.shape, BF16),
                   jax.ShapeDtypeStruct((1, d), F32),
                   jax.ShapeDtypeStruct((1, d), F32)],
        scratch_shapes=[pltpu.VMEM((tm + POOL_HALO, d), F32), pltpu.VMEM((tm + POOL_HALO, d), F32),
                        pltpu.VMEM((tm, d), F32), pltpu.VMEM((POOL_HALO, d), F32),
                        pltpu.VMEM(pw.shape, F32)],
        compiler_params=_params(("arbitrary",)),
    )(after, dh, h, h, g, pw, ps)


def _conv_out_bwd(after, dh, w_out, o, bcx, cw):
    s, d = dh.shape
    c = o.shape[1]
    tm = min(TILE_ROWS, s)
    nb = s // tm
    halo_per_tile = tm // CONV_HALO

    def body(after_ref, dh_ref, w_ref, o_ref, b_ref, c_ref, xin_ref, ch_ref, xh_ref, cw_ref,
             do_ref, delta_ref, dbcx_ref, dw_ref, dcw_ref, ubuf, dbuf, carry, acc):
        i = pl.program_id(0)
        blk = nb - 1 - i

        @pl.when(i == 0)
        def _():
            carry[...] = jnp.zeros_like(carry)
            acc[...] = jnp.zeros_like(acc)
            dcw_ref[...] = jnp.zeros_like(dcw_ref)

        dm = dh_ref[...].astype(BF16)
        dcat = _nt(dm, w_ref[...])
        do = dcat[:, 0:c]
        dy = dcat[:, c:2 * c]
        do_ref[...] = do.astype(BF16)
        head_of_lane = lax.shift_right_logical(_lane((8, c)), HEAD_DIM.bit_length() - 1)
        heads = (head_of_lane == _row((8, c))).astype(BF16)
        delta_ref[...] = _exact_nt(heads, do * o_ref[...].astype(F32))

        cv_ = c_ref[...]
        xin = xin_ref[...]
        bv = b_ref[...]
        u = cv_ * xin
        ubuf[0:CONV_HALO, :] = jnp.where(blk == 0, 0.0, ch_ref[...] * xh_ref[...])
        ubuf[CONV_HALO:CONV_HALO + tm, :] = u
        u1 = ubuf[CONV_HALO - 1:CONV_HALO - 1 + tm, :]
        u2 = ubuf[CONV_HALO - 2:CONV_HALO - 2 + tm, :]
        w0, w1, w2 = cw_ref[0:1, :], cw_ref[1:2, :], cw_ref[2:3, :]
        cv = (w0 * u2 + w1 * u1) + w2 * u
        acc[0:c, :] += _tn(o_ref[...], dm)
        acc[c:2 * c, :] += _tn((bv * cv).astype(BF16), dm)

        dcv = dy * bv
        dcw_ref[0:1, :] += jnp.sum(dcv * u2, axis=0, keepdims=True)
        dcw_ref[1:2, :] += jnp.sum(dcv * u1, axis=0, keepdims=True)
        dcw_ref[2:3, :] += jnp.sum(dcv * u, axis=0, keepdims=True)
        dbuf[0:tm, :] = dcv
        dbuf[tm:tm + CONV_HALO, :] = carry[...]
        du = w2 * dcv + w1 * dbuf[1:1 + tm, :] + w0 * dbuf[2:2 + tm, :]
        dbcx_ref[:, 0:c] = (dy * cv).astype(BF16)
        dbcx_ref[:, c:2 * c] = (du * xin).astype(BF16)
        dbcx_ref[:, 2 * c:3 * c] = (du * cv_).astype(BF16)
        carry[...] = dcv[0:CONV_HALO, :]

        @pl.when(i == nb - 1)
        def _():
            dw_ref[...] = acc[...].astype(BF16)

    rev = lambda k: (lambda i: (nb - 1 - i, k))
    halo = lambda k: (lambda i: (jnp.maximum((nb - 1 - i) * halo_per_tile - 1, 0), k))
    return pl.pallas_call(
        body, name="conv_out_bwd", grid=(nb,),
        in_specs=[ANY, pl.BlockSpec((tm, d), rev(0)), _full(w_out), pl.BlockSpec((tm, c), rev(0)),
                  pl.BlockSpec((tm, c), rev(0)), pl.BlockSpec((tm, c), rev(1)),
                  pl.BlockSpec((tm, c), rev(2)),
                  pl.BlockSpec((CONV_HALO, c), halo(1)), pl.BlockSpec((CONV_HALO, c), halo(2)),
                  _full(cw)],
        out_specs=[pl.BlockSpec((tm, c), rev(0)),
                   pl.BlockSpec((8, tm), lambda i: (0, nb - 1 - i)),
                   pl.BlockSpec((tm, 3 * c), rev(0)),
                   _full(w_out), _full(cw)],
        out_shape=[jax.ShapeDtypeStruct((s, c), BF16),
                   jax.ShapeDtypeStruct((8, s), F32),
                   jax.ShapeDtypeStruct((s, 3 * c), BF16),
                   jax.ShapeDtypeStruct(w_out.shape, BF16),
                   jax.ShapeDtypeStruct(cw.shape, F32)],
        scratch_shapes=[pltpu.VMEM((tm + CONV_HALO, c), F32), pltpu.VMEM((tm + CONV_HALO, c), F32),
                        pltpu.VMEM((CONV_HALO, c), F32), pltpu.VMEM(w_out.shape, F32)],
        compiler_params=_params(("arbitrary",)),
    )(after, dh, w_out, o, bcx, bcx, bcx, bcx, bcx, cw)


def _attn_bwd(after, qa, ka, qkv, do, lse, delta):
    s = qa.shape[1]
    a = N_HEADS * HEAD_DIM
    t = min(TILE_ATTN, s)
    nq = s // t
    n_pairs = N_HEADS // 2
    v_block0 = 2 * a // 128

    def body(after_ref, ka_ref, v_ref, qa_ref, do_ref, lse_ref, delta_ref,
             dqt_ref, dka_ref, dv_ref, dv_acc):
        g = pl.program_id(0)
        j = pl.program_id(1)

        @pl.when(j == 0)
        def _():
            dqt_ref[...] = jnp.zeros_like(dqt_ref)

        lane = _lane((t, 128))
        vf = v_ref[...].astype(F32)
        v_heads = [jnp.where(lane < HEAD_DIM, vf, 0.0).astype(BF16),
                   jnp.where(lane >= HEAD_DIM, vf, 0.0).astype(BF16)]
        ke_t = [ka_ref[e].astype(F32).T.astype(BF16) for e in range(2)]

        def q_step(i, first):
            qs = pl.ds(pl.multiple_of(i * t, t), t)
            dob = do_ref[qs, :]
            for e in range(2):
                qe = qa_ref[e, qs, :]
                sc = _nt(ka_ref[e], qe)
                if first:
                    sc = jnp.where(_row((t, t)) <= _lane((t, t)), sc, NEG_BIG)
                p = jnp.exp2(sc - lse_ref[pl.ds(e, 1), qs])
                dv_part = _nn(p.astype(BF16), dob)
                dp = _nt(v_heads[e], dob)
                ds = (p * (dp - delta_ref[pl.ds(2 * g + e, 1), qs])).astype(BF16)
                dk_part = _nn(ds, qe)
                if first:
                    dv_acc[e] = dv_part
                    dka_ref[e] = dk_part
                else:
                    dv_acc[e] += dv_part
                    dka_ref[e] += dk_part
                dqt_ref[e, :, qs] += _nn(ke_t[e], ds)

        q_step(j, True)

        def full_step(i, carry):
            q_step(i, False)
            return carry

        lax.fori_loop(j + 1, nq, full_step, 0)
        dv_ref[...] = jnp.where(lane < HEAD_DIM, dv_acc[0], dv_acc[1]).astype(BF16)

    return pl.pallas_call(
        body, name="attn_bwd", grid=(n_pairs, nq),
        in_specs=[ANY, pl.BlockSpec((2, t, 128), lambda g, j: (g, j, 0)),
                  pl.BlockSpec((t, 128), lambda g, j: (j, v_block0 + g)),
                  pl.BlockSpec((2, s, 128), lambda g, j: (g, 0, 0)),
                  pl.BlockSpec((s, 128), lambda g, j: (0, g)),
                  pl.BlockSpec((None, 8, s), lambda g, j: (g, 0, 0)),
                  pl.BlockSpec((8, s), lambda g, j: (0, 0))],
        out_specs=[pl.BlockSpec((2, 128, s), lambda g, j: (g, 0, 0)),
                   pl.BlockSpec((2, t, 128), lambda g, j: (g, j, 0)),
                   pl.BlockSpec((t, 128), lambda g, j: (j, g))],
        out_shape=[jax.ShapeDtypeStruct((N_HEADS, 128, s), F32),
                   jax.ShapeDtypeStruct((N_HEADS, s, 128), F32),
                   jax.ShapeDtypeStruct((s, a), BF16)],
        scratch_shapes=[pltpu.VMEM((2, t, 128), F32)],
        compiler_params=_params(("parallel", "arbitrary")),
    )(after, ka, qkv, qa, do, lse, delta)


def _gate_bwd(dqa, dka, dv, fl, bf):
    s = fl.shape[0]
    a = N_HEADS * HEAD_DIM
    tm = min(TILE_ROWS, s)
    nb = s // tm

    def body(dqa_ref, dka_ref, dv_ref, fl_ref, bf_ref, dqkv_ref, dfl_ref, dbf_ref, carry):
        i = pl.program_id(0)

        @pl.when(i == 0)
        def _():
            carry[...] = jnp.zeros_like(carry)
            dbf_ref[...] = jnp.zeros_like(dbf_ref)

        lane = _lane((tm, 128))
        dq_sum = jnp.zeros((tm, 128), F32)
        dk_sum = jnp.zeros((tm, 128), F32)
        for pair in range(N_HEADS // 2):
            qs, ks = [], []
            for e in range(2):
                h = 2 * pair + e
                dq = dqa_ref[h].T
                dk = dka_ref[h]
                dq_sum = dq_sum + dq
                dk_sum = dk_sum + dk
                qs.append(dq * ATTN_SCALE)
                ks.append(dk * (1.0 / LOG2_E))
            cols = slice(pair * 128, (pair + 1) * 128)
            dqkv_ref[:, cols] = jnp.where(
                lane < HEAD_DIM, qs[0], pltpu.roll(qs[1], HEAD_DIM, axis=1)).astype(BF16)
            dqkv_ref[:, a + pair * 128:a + (pair + 1) * 128] = jnp.where(
                lane < HEAD_DIM, ks[0], pltpu.roll(ks[1], HEAD_DIM, axis=1)).astype(BF16)
        dqkv_ref[:, 2 * a:3 * a] = dv_ref[...]

        in_q = (lane >= LANE_CQ) & (lane < LANE_CQ + N_HEADS)
        in_k = (lane >= LANE_CK) & (lane < LANE_CK + N_HEADS)
        dcum = (pltpu.roll(jnp.where(in_q, dq_sum, 0.0), 128 - LANE_CQ, axis=1)
                - pltpu.roll(jnp.where(in_k, dk_sum, 0.0), 128 - LANE_CK, axis=1))

        upper = (_lane((tm, tm)) >= _row((tm, tm))).astype(BF16)
        dlogf = _exact_nn(upper, dcum) + carry[0:1, :]
        carry[0:1, :] = dlogf[0:1, :]
        z = fl_ref[...] + bf_ref[...]
        ez = jnp.exp(-jnp.abs(z))
        sig_neg = jnp.where(z >= 0.0, ez, 1.0) / (1.0 + ez)
        dz = jnp.where(lane < N_HEADS, dlogf * sig_neg, 0.0)
        dfl_ref[...] = dz.astype(BF16)
        dbf_ref[...] += jnp.sum(dz, axis=0, keepdims=True)

    rev3 = lambda i: (0, nb - 1 - i, 0)
    rev = lambda i: (nb - 1 - i, 0)
    return pl.pallas_call(
        body, name="gate_bwd", grid=(nb,),
        in_specs=[pl.BlockSpec((N_HEADS, 128, tm), lambda i: (0, 0, nb - 1 - i)),
                  pl.BlockSpec((N_HEADS, tm, 128), rev3),
                  pl.BlockSpec((tm, a), rev), pl.BlockSpec((tm, 128), rev), _full(bf)],
        out_specs=[pl.BlockSpec((tm, 3 * a), rev), pl.BlockSpec((tm, 128), rev),
                   pl.BlockSpec((1, 128), lambda i: (0, 0))],
        out_shape=[jax.ShapeDtypeStruct((s, 3 * a), BF16),
                   jax.ShapeDtypeStruct((s, 128), BF16),
                   jax.ShapeDtypeStruct((1, 128), F32)],
        scratch_shapes=[pltpu.VMEM((8, 128), F32)],
        compiler_params=_params(("arbitrary",)),
    )(dqa, dka, dv, fl, bf)


def _in_proj_bwd(after, dqkv, dfl, dbcx, w_qkv, w_f, w_bcx, x, g, dh):
    s, d = x.shape
    tm = min(TILE_ROWS, s)

    def body(after_ref, dq_ref, df_ref, db_ref, wq_ref, wf_ref, wb_ref, x_ref, g_ref, dh_ref,
             gx_ref, dg_ref):
        i = pl.program_id(0)
        dn = (_nt(dq_ref[...], wq_ref[...]) + _nt(df_ref[...], wf_ref[...])
              + _nt(db_ref[...], wb_ref[...]))
        dx, dg = _rms_bwd(dn, x_ref[...], g_ref[...])
        gx_ref[...] = dh_ref[...] + dx

        @pl.when(i == 0)
        def _():
            dg_ref[...] = dg

        @pl.when(i > 0)
        def _():
            dg_ref[...] += dg

    rows = lambda c: pl.BlockSpec((tm, c), lambda i: (i, 0))
    return pl.pallas_call(
        body, name="in_proj_bwd", grid=(s // tm,),
        in_specs=[ANY, rows(dqkv.shape[1]), rows(dfl.shape[1]), rows(dbcx.shape[1]),
                  _full(w_qkv), _full(w_f), _full(w_bcx), rows(d), _full(g), rows(d)],
        out_specs=[rows(d), pl.BlockSpec((1, d), lambda i: (0, 0))],
        out_shape=[jax.ShapeDtypeStruct((s, d), F32), jax.ShapeDtypeStruct((1, d), F32)],
        compiler_params=_params(("arbitrary",)),
    )(after, dqkv, dfl, dbcx, w_qkv, w_f, w_bcx, x, g, dh)


def _wgrad_in(n, dys):
    s, d = n.shape
    m = len(dys)
    tk = min(TILE_ROWS, s)
    nk = s // tk

    def body(*refs):
        n_ref, dy_refs, dw_refs, accs = refs[0], refs[1:1 + m], refs[1 + m:1 + 2 * m], refs[1 + 2 * m:]
        k = pl.program_id(0)

        @pl.when(k == 0)
        def _():
            for acc in accs:
                acc[...] = jnp.zeros_like(acc)

        nb = n_ref[...]
        for dy_ref, acc in zip(dy_refs, accs):
            acc[...] += _tn(nb, dy_ref[...])

        @pl.when(k == nk - 1)
        def _():
            for dw_ref, acc in zip(dw_refs, accs):
                dw_ref[...] = acc[...].T.astype(BF16)

    return pl.pallas_call(
        body, name="wgrad_in", grid=(nk,),
        in_specs=[pl.BlockSpec((tk, d), lambda k: (k, 0))]
        + [pl.BlockSpec((tk, dy.shape[1]), lambda k: (k, 0)) for dy in dys],
        out_specs=[pl.BlockSpec((dy.shape[1], d), lambda k: (0, 0)) for dy in dys],
        out_shape=[jax.ShapeDtypeStruct((dy.shape[1], d), BF16) for dy in dys],
        scratch_shapes=[pltpu.VMEM((d, dy.shape[1]), F32) for dy in dys],
        compiler_params=_params(("arbitrary",)),
    )(n, *dys)


def _row_tile(rows):
    t = min(TILE_ELEM_ROWS, rows)
    while rows % t:
        t //= 2
    return t


def _adamw_math(w, g, m, v):
    m = ADAM_B1 * m + (1.0 - ADAM_B1) * g
    v = ADAM_B2 * v + (1.0 - ADAM_B2) * jnp.square(g)
    m_hat = m / (1.0 - ADAM_B1 ** ADAM_STEP)
    v_hat = v / (1.0 - ADAM_B2 ** ADAM_STEP)
    delta = -ADAM_LR * (m_hat / (jnp.sqrt(v_hat) + ADAM_EPS) + ADAM_WD * w)
    return delta, m, v


def _adamw(w, g, m, v, name):
    rows, cols = w.shape

    def body(w_ref, g_ref, m_ref, v_ref, d_ref, nm_ref, nv_ref):
        delta, nm, nv = _adamw_math(w_ref[...], g_ref[...], m_ref[...], v_ref[...])
        d_ref[...] = delta
        nm_ref[...] = nm
        nv_ref[...] = nv

    if rows % 8 == 0:
        tr = _row_tile(rows)
        grid, spec = (rows // tr,), pl.BlockSpec((tr, cols), lambda i: (i, 0))
    else:
        grid, spec = (cols // 256,), pl.BlockSpec((rows, 256), lambda i: (0, i))
    out = jax.ShapeDtypeStruct(w.shape, F32)
    return pl.pallas_call(
        body, name=name, grid=grid, in_specs=[spec] * 4, out_specs=[spec] * 3,
        out_shape=[out, out, out], compiler_params=_params(("parallel",)),
    )(w, g, m, v)


def _sum_devices(parts):
    def body(p_ref, g_ref):
        g = p_ref[0]
        for k in range(1, N_DEV):
            g = g + p_ref[k]
        g_ref[...] = g

    return pl.pallas_call(
        body, name="sum_devices",
        in_specs=[pl.BlockSpec(memory_space=pltpu.VMEM)],
        out_specs=pl.BlockSpec(memory_space=pltpu.VMEM),
        out_shape=jax.ShapeDtypeStruct(parts.shape[1:], F32),
    )(parts)


def _mesh_position():
    x, y, c = lax.axis_index("x"), lax.axis_index("y"), lax.axis_index("c")
    chips = [(1 - x, y), (x, 1 - y), (1 - x, 1 - y)]
    return x, y, c, chips


ANY = pl.BlockSpec(memory_space=pl.ANY)
HBM = pl.BlockSpec(memory_space=pltpu.HBM)
SEM = pl.BlockSpec(memory_space=pltpu.SEMAPHORE)
SPLIT_COPY_EFFECT = pltpu.SideEffectType.DATAFLOW_SIDE_EFFECTING


def _in_hbm(a):
    return pltpu.with_memory_space_constraint(a, pltpu.HBM)


def _chip_copies(views, srcs, lands, send, recv, waiting=False):
    _, _, c, chips = _mesh_position()
    cps = []
    for a in range(len(srcs)):
        for k, (px, py) in enumerate(chips):
            src, dst = views(a, k, srcs[a], lands[a], c, 2 * px + py)
            sem = a * (N_CHIPS - 1) + k
            cps.append(pltpu.make_async_remote_copy(
                src_ref=src, dst_ref=dst, send_sem=send.at[sem], recv_sem=recv.at[sem],
                device_id=(px, py, c), device_id_type=MESH))
    return cps


def _ici_start(sources, land_shapes, copies, after, name, per_array=N_CHIPS - 1):
    n = len(sources)

    def body(*refs):
        srcs, lands = refs[:n], refs[n:2 * n]
        send, recv = refs[2 * n + 1], refs[2 * n + 2]
        token = refs[-1]
        for cp in copies(srcs, lands, send, recv, False):
            cp.start()
        token[...] = jnp.zeros_like(token)

    lands = [_in_hbm(lax.empty(s.shape, s.dtype)) for s in land_shapes]
    outs = pl.pallas_call(
        body, name=name,
        in_specs=[HBM] * (2 * n) + [ANY],
        out_specs=[SEM, SEM] + [HBM] * (2 * n) + [pl.BlockSpec(memory_space=pltpu.VMEM)],
        out_shape=[pltpu.SemaphoreType.DMA((n * per_array,))] * 2
        + [pltpu.HBM(a.shape, a.dtype) for a in sources]
        + [pltpu.HBM(s.shape, s.dtype) for s in land_shapes]
        + [jax.ShapeDtypeStruct((8, 128), F32)],
        input_output_aliases={i: 2 + i for i in range(2 * n)},
        compiler_params=pltpu.CompilerParams(has_side_effects=SPLIT_COPY_EFFECT),
    )(*[_in_hbm(a) for a in sources], *lands, after)
    return outs[0], outs[1], list(outs[2:2 + n]), list(outs[2 + n:2 + 2 * n]), outs[-1]


def _ici_wait(handle, copies, after, name):
    send, recv, srcs, lands, _ = handle
    n = len(srcs)

    def body(*refs):
        src_refs, land_refs = refs[:n], refs[n:2 * n]
        for cp in copies(src_refs, land_refs, refs[2 * n], refs[2 * n + 1], True):
            cp.wait_send()
            cp.wait_recv()

    outs = pl.pallas_call(
        body, name=name,
        in_specs=[HBM] * (2 * n) + [SEM, SEM, ANY],
        out_specs=[HBM] * (2 * n),
        out_shape=[pltpu.HBM(a.shape, a.dtype) for a in srcs]
        + [pltpu.HBM(a.shape, a.dtype) for a in lands],
        input_output_aliases={i: i for i in range(2 * n)},
        compiler_params=pltpu.CompilerParams(has_side_effects=SPLIT_COPY_EFFECT),
    )(*srcs, *lands, send, recv, after)
    return list(outs[:n]), list(outs[n:])


def _gather_views(split):
    def views(a, k, src, land, c, slot):
        if split[a]:
            half = src.shape[0] // 2
            src = src.at[pl.ds(c * half, half)]
        return src, land.at[k]
    return views


def _gather_whole_views(a, k, src, land, c, slot):
    x, y, _, _ = _mesh_position()
    return src, land.at[2 * x + y]


SCATTER_COPIES = 2 * (N_CHIPS - 1)


def _scatter_copies(srcs, lands, send, recv, waiting):
    _, _, c, chips = _mesh_position()
    cps = []
    for a in range(len(srcs)):
        half = srcs[a].shape[1] // 2
        for k, (px, py) in enumerate(chips):
            for h in range(2):
                arrival = 2 * k + (h if waiting else c)
                cps.append(pltpu.make_async_remote_copy(
                    src_ref=srcs[a].at[2 * px + py, pl.ds(h * half, half)],
                    dst_ref=lands[a].at[arrival],
                    send_sem=send.at[a * SCATTER_COPIES + 2 * k + h],
                    recv_sem=recv.at[a * SCATTER_COPIES + arrival],
                    device_id=(px, py, h), device_id_type=MESH))
    return cps


def _gather_land_shapes(shards, split):
    return [jax.ShapeDtypeStruct(
        (N_CHIPS - 1, a.shape[0] // 2 if sp else a.shape[0]) + a.shape[1:], a.dtype)
        for a, sp in zip(shards, split)]


def _gather_finish(shards, lands, split, name):
    n = len(shards)
    ns = sum(split)
    d_index = {a: i for i, a in enumerate(a for a in range(n) if split[a])}

    def body(*refs):
        shard, land, outs = refs[:n], refs[n:2 * n], refs[2 * n:3 * n]
        obuf, fbuf = refs[3 * n:4 * n], refs[4 * n:5 * n]
        dbuf = refs[5 * n:5 * n + ns]
        ld_own, st_own, ld, st_mine, st_sib, send, recv = refs[5 * n + ns:]
        x, y, c, chips = _mesh_position()
        me = 2 * x + y
        own_loads, loads, sends, pending = [], {}, [], []
        for a in range(n):
            cp = pltpu.make_async_copy(shard[a], obuf[a], ld_own.at[a])
            cp.start()
            own_loads.append(cp)
        for a in range(n):
            for k in range(N_CHIPS - 1):
                cp = pltpu.make_async_copy(land[a].at[k], fbuf[a].at[k], ld.at[a, k])
                cp.start()
                loads[a, k] = cp
        for a in range(n):
            own_loads[a].wait()
            cp = pltpu.make_async_copy(obuf[a], outs[a].at[me], st_own.at[a])
            cp.start()
            pending.append(cp)
        for a in range(n):
            rows = shard[a].shape[0]
            for k, (px, py) in enumerate(chips):
                loads[a, k].wait()
                part = pl.ds(c * (rows // 2), rows // 2) if split[a] else pl.ds(0, rows)
                cp = pltpu.make_async_copy(fbuf[a].at[k], outs[a].at[2 * px + py, part],
                                           st_mine.at[a, k])
                cp.start()
                pending.append(cp)
                if split[a]:
                    fw = pltpu.make_async_remote_copy(
                        src_ref=fbuf[a].at[k], dst_ref=dbuf[d_index[a]].at[k],
                        send_sem=send.at[a, k], recv_sem=recv.at[a, k],
                        device_id=(x, y, 1 - c), device_id_type=MESH)
                    fw.start()
                    sends.append((a, k, fw))
        for a, k, fw in sends:
            px, py = chips[k]
            half = shard[a].shape[0] // 2
            fw.wait_recv()
            cp = pltpu.make_async_copy(dbuf[d_index[a]].at[k],
                                       outs[a].at[2 * px + py, pl.ds((1 - c) * half, half)],
                                       st_sib.at[a, k])
            cp.start()
            pending.append(cp)
        for _, _, fw in sends:
            fw.wait_send()
        for cp in pending:
            cp.wait()

    stage = [pltpu.VMEM(a.shape, a.dtype) for a in lands]
    dma = lambda *shape: pltpu.SemaphoreType.DMA(shape)
    return pl.pallas_call(
        body, name=name,
        in_specs=[ANY] * (2 * n), out_specs=[ANY] * n,
        out_shape=[jax.ShapeDtypeStruct((N_CHIPS,) + a.shape, a.dtype) for a in shards],
        scratch_shapes=[pltpu.VMEM(a.shape, a.dtype) for a in shards] + stage
        + [s for s, sp in zip(stage, split) if sp]
        + [dma(n), dma(n), dma(n, 3), dma(n, 3), dma(n, 3), dma(n, 3), dma(n, 3)],
        compiler_params=pltpu.CompilerParams(vmem_limit_bytes=VMEM_LIMIT_BYTES),
    )(*shards, *lands)


def _sum_chunk(rows):
    return next(r for r in range(SUM_CHUNK_ROWS, 0, -16) if rows % r == 0)


def _sum_and_share(partials, lands, name):
    n = len(partials)

    def body(*refs):
        own, landed, outs = refs[:n], refs[n:2 * n], refs[2 * n:3 * n]
        obuf, xbuf, ybuf, gbuf, sbuf, rbuf = (refs[(3 + k) * n:(4 + k) * n] for k in range(6))
        ld_own, ld_send, ld_got, st_own, st_sib, send_p, recv_p, send_s, recv_s = refs[9 * n:]
        x, y, c, _ = _mesh_position()
        me = 2 * x + y
        sibling = (x, y, 1 - c)

        def to_sibling(src, dst, send, recv, a):
            return pltpu.make_async_remote_copy(src_ref=src, dst_ref=dst, send_sem=send.at[a],
                                                recv_sem=recv.at[a], device_id=sibling,
                                                device_id_type=MESH)

        loads, firsts, seconds, stores = [], [], [], []
        for a in range(n):
            half = obuf[a].shape[0]
            cps = [pltpu.make_async_copy(own[a].at[me, pl.ds((1 - c) * half, half)], xbuf[a],
                                         ld_send.at[a]),
                   pltpu.make_async_copy(own[a].at[me, pl.ds(c * half, half)], obuf[a], ld_own.at[a]),
                   pltpu.make_async_copy(landed[a], gbuf[a], ld_got.at[a])]
            for cp in cps:
                cp.start()
            loads.append(cps)
        for a in range(n):
            loads[a][0].wait()
            rc = to_sibling(xbuf[a], ybuf[a], send_p, recv_p, a)
            rc.start()
            firsts.append(rc)
        for a in range(n):
            firsts[a].wait_recv()
            loads[a][1].wait()
            loads[a][2].wait()
            half = obuf[a].shape[0]
            rows = _sum_chunk(half)

            def add(k, carry, a=a, rows=rows):
                at = pl.ds(pl.multiple_of(k * rows, rows), rows)
                acc = obuf[a][at].astype(F32) + ybuf[a][at].astype(F32)
                for j in range(SCATTER_COPIES):
                    acc = acc + gbuf[a][j, at].astype(F32)
                sbuf[a][at] = acc
                return carry

            lax.fori_loop(0, half // rows, add, 0)
            rc = to_sibling(sbuf[a], rbuf[a], send_s, recv_s, a)
            rc.start()
            seconds.append(rc)
            cp = pltpu.make_async_copy(sbuf[a], outs[a].at[pl.ds(c * half, half)], st_own.at[a])
            cp.start()
            stores.append(cp)
        for a in range(n):
            half = obuf[a].shape[0]
            seconds[a].wait_recv()
            cp = pltpu.make_async_copy(rbuf[a], outs[a].at[pl.ds((1 - c) * half, half)], st_sib.at[a])
            cp.start()
            stores.append(cp)
        for rc in firsts + seconds:
            rc.wait_send()
        for cp in stores:
            cp.wait()

    halves = [(a.shape[1] // 2, a.shape[2]) for a in partials]
    return pl.pallas_call(
        body, name=name,
        in_specs=[ANY] * (2 * n), out_specs=[ANY] * n,
        out_shape=[jax.ShapeDtypeStruct((2 * h[0], h[1]), F32) for h in halves],
        scratch_shapes=[pltpu.VMEM(h, BF16) for h in halves] * 3
        + [pltpu.VMEM(g.shape, BF16) for g in lands]
        + [pltpu.VMEM(h, F32) for h in halves] * 2
        + [pltpu.SemaphoreType.DMA((n,))] * 9,
        compiler_params=pltpu.CompilerParams(vmem_limit_bytes=VMEM_LIMIT_BYTES),
    )(*partials, *lands)


def _gather_small(part):
    def body(in_ref, out_ref, send, recv, local):
        x, y, c, _ = _mesh_position()
        me = 4 * x + 2 * y + c
        cps = [pltpu.make_async_copy(in_ref, out_ref.at[me], local)]
        k = 0
        for fx in range(2):
            for fy in range(2):
                for fc in range(2):
                    if fx or fy or fc:
                        cps.append(pltpu.make_async_remote_copy(
                            src_ref=in_ref, dst_ref=out_ref.at[me], send_sem=send.at[k],
                            recv_sem=recv.at[k], device_id=(x ^ fx, y ^ fy, c ^ fc),
                            device_id_type=MESH))
                        k += 1
        for cp in cps:
            cp.start()
        for cp in cps:
            cp.wait()

    return pl.pallas_call(
        body, name="gather_small",
        in_specs=[pl.BlockSpec(memory_space=pltpu.VMEM)],
        out_specs=pl.BlockSpec(memory_space=pltpu.VMEM),
        out_shape=jax.ShapeDtypeStruct((N_DEV,) + part.shape, part.dtype),
        scratch_shapes=[pltpu.SemaphoreType.DMA((N_DEV - 1,)), pltpu.SemaphoreType.DMA((N_DEV - 1,)),
                        pltpu.SemaphoreType.DMA],
    )(part)


def _scatter_start(grads, after, tag):
    lands = [jax.ShapeDtypeStruct((SCATTER_COPIES, g.shape[1] // 2, g.shape[2]), g.dtype)
             for g in grads]
    return _ici_start(grads, lands, _scatter_copies, after, "scatter_start_" + tag,
                      per_array=SCATTER_COPIES)


def _scatter_finish(handles, after, tag):
    grads, lands = [], []
    for k, handle in enumerate(handles):
        g, l = _ici_wait(handle, _scatter_copies, after, "scatter_wait_%s_%d" % (tag, k))
        grads += g
        lands += l
    return _sum_and_share(grads, lands, "sum_and_share_" + tag)


def _pad_rows(a, rows):
    return jnp.pad(a, ((0, rows - a.shape[0]), (0, 0)))


def kernel(x, norm_mix_0, w_in_0, b_f_0, conv_w_0, w_out_0, norm_ffn_0, w_up_0, w_down_0, norm_mix_1, pool_w_1, pool_scale_1, norm_ffn_1, w_up_1, w_down_1, final_norm, loss_target, m_norm_mix_0, m_w_in_0, m_b_f_0, m_conv_w_0, m_w_out_0, m_norm_ffn_0, m_w_up_0, m_w_down_0, m_norm_mix_1, m_pool_w_1, m_pool_scale_1, m_norm_ffn_1, m_w_up_1, m_w_down_1, m_final_norm, v_norm_mix_0, v_w_in_0, v_b_f_0, v_conv_w_0, v_w_out_0, v_norm_ffn_0, v_w_up_0, v_w_down_0, v_norm_mix_1, v_pool_w_1, v_pool_scale_1, v_norm_ffn_1, v_w_up_1, v_w_down_1, v_final_norm):
    d = x.shape[-1]
    a = N_HEADS * HEAD_DIM
    c_conv = conv_w_0.shape[1] * N_CHIPS
    xs = x[0]
    target = loss_target[0]
    row = lambda vec: vec.reshape(1, -1)

    big = [w_in_0, w_out_0, w_up_0, w_down_0, pool_w_1, w_up_1, w_down_1]
    first = [w_in_0.astype(BF16)]
    first_split = [True]
    copies_a = functools.partial(_chip_copies, _gather_views(first_split))
    copies_b = functools.partial(_chip_copies, _gather_whole_views)
    start_a = _ici_start(first, _gather_land_shapes(first, first_split), copies_a, b_f_0,
                         "gather_start_a")
    zero = start_a[-1][0, 0]
    rest = [(w + zero).astype(BF16)
            for w in (w_out_0, w_up_0, w_down_0, pool_w_1, w_up_1, w_down_1)]
    rest = rest + [conv_w_0]
    start_b = _ici_start(rest, [jax.ShapeDtypeStruct((N_CHIPS,) + w.shape, w.dtype) for w in rest],
                         copies_b, start_a[-1], "gather_start_b")
    n0 = _rms_pre(start_b[-1], xs, row(norm_mix_0))
    first, land_a = _ici_wait(start_a, copies_a, n0, "gather_wait_a")
    (g_in,) = _gather_finish(first, land_a, first_split, "gather_finish_a")
    w_in = g_in.transpose(1, 0, 2).reshape(d, -1)
    w_qkv = w_in[:, :3 * a]
    w_f = jnp.pad(w_in[:, 3 * a:3 * a + N_HEADS], ((0, 0), (0, 128 - N_HEADS)))
    w_bcx = w_in[:, 3 * a + N_HEADS:]
    bf = jnp.pad(b_f_0, (0, 128 - N_HEADS)).reshape(1, 128)

    qkv, fl, bcx = _in_proj(n0, w_qkv, w_f, w_bcx)
    qa, ka = _gate_prep(fl, bf, qkv)
    o, lse = _attn_fwd(qa, ka, qkv)
    rest, land_b = _ici_wait(start_b, copies_b, o, "gather_wait_b")
    own_slot = 2 * lax.axis_index("x") + lax.axis_index("y")
    g_out, g_up0, g_down0, g_pool, g_up1, g_down1, g_conv = [
        lax.dynamic_update_index_in_dim(land, shard, own_slot, 0)
        for land, shard in zip(land_b, rest)]
    w_out = g_out.reshape(-1, d)
    conv_w = _pad_rows(g_conv.transpose(1, 0, 2).reshape(conv_w_0.shape[0], c_conv), 8)
    h1 = _conv_out(o, bcx, conv_w, w_out, xs)
    w_down0 = g_down0.reshape(-1, d)
    w_down1 = g_down1.reshape(-1, d)
    pool_w = g_pool.transpose(1, 0, 2, 3).reshape(pool_w_1.shape[0], -1, pool_w_1.shape[2])
    h2, a0, nf0 = _mlp_fwd(h1, row(norm_ffn_0), g_up0, w_down0, "mlp_fwd_0")
    h3 = _pool_fwd(h2, row(norm_mix_1), pool_w, row(pool_scale_1))
    dh4, a1, nf1, loss_part, d_final = _mlp_fwd(h3, row(norm_ffn_1), g_up1, w_down1, "mlp_fwd_1",
                                                head=(row(final_norm), target))

    slot_cols = g_up0.shape[2]
    pool_cols = pool_w.shape[2]
    da1, dz1, dh3, d_nffn1 = _mlp_bwd_x(dh4, a1, g_up1, w_down1, h3, row(norm_ffn_1), "mlp_bwd_x_1")
    dw_up1, dw_down1 = _mlp_bwd_w(nf1, da1, a1, dz1, slot_cols, "mlp_bwd_w_1")
    scatter_1 = _scatter_start([dw_up1, dw_down1.reshape(N_CHIPS, -1, d)], bf, "mlp1")
    dh2, dw_pool, d_pscale, d_nmix1 = _pool_bwd(scatter_1[-1], dh3, h2, row(norm_mix_1), pool_w,
                                                row(pool_scale_1))
    da0, dz0, dh1, d_nffn0 = _mlp_bwd_x(dh2, a0, g_up0, w_down0, h1, row(norm_ffn_0), "mlp_bwd_x_0")
    dw_up0, dw_down0 = _mlp_bwd_w(nf0, da0, a0, dz0, slot_cols, "mlp_bwd_w_0")
    dw_pool = (dw_pool.reshape(pool_w.shape[0], N_CHIPS, -1, pool_cols).transpose(1, 0, 2, 3)
               .reshape(N_CHIPS, -1, pool_cols))
    scatter_0 = _scatter_start([dw_up0, dw_down0.reshape(N_CHIPS, -1, d), dw_pool], bf, "mlp0")
    do, delta, dbcx, dw_out, d_conv = _conv_out_bwd(scatter_0[-1], dh1, w_out, o, bcx, conv_w)
    scatter_o = _scatter_start([dw_out.reshape(N_CHIPS, -1, d)], bf, "out")
    dqa, dka, dv = _attn_bwd(scatter_o[-1], qa, ka, qkv, do, lse, delta)
    dqkv, dfl, d_bf = _gate_bwd(dqa, dka, dv, fl, bf)
    dw_qkv, dw_f, dw_bcx = _wgrad_in(n0, [dqkv, dfl, dbcx])
    dw_in = jnp.concatenate([dw_qkv, dw_f[:N_HEADS], dw_bcx], axis=0).reshape(N_CHIPS, -1, d)
    slot_rows = -(-dw_in.shape[1] // 32) * 32
    dw_in = jnp.pad(dw_in, ((0, 0), (0, slot_rows - dw_in.shape[1]), (0, 0)))
    scatter_m = _scatter_start([dw_in], bf, "mixer")
    grad_x, d_nmix0 = _in_proj_bwd(scatter_m[-1], dqkv, dfl, dbcx, w_qkv, w_f, w_bcx, xs,
                                   row(norm_mix_0), dh1)

    r_up1, r_down1, r_out = _scatter_finish([scatter_1, scatter_o], grad_x, "early")
    r_up0, r_down0, r_pool, r_in = _scatter_finish([scatter_0, scatter_m], grad_x, "late")
    reduced = [r_in, r_out, r_up0, r_down0, r_pool, r_up1, r_down1]
    moments = [(m_w_in_0, v_w_in_0), (m_w_out_0, v_w_out_0), (m_w_up_0, v_w_up_0),
               (m_w_down_0, v_w_down_0), (m_pool_w_1, v_pool_w_1), (m_w_up_1, v_w_up_1),
               (m_w_down_1, v_w_down_1)]
    big_out = []
    for k, (w, g, (m, v)) in enumerate(zip(big, reduced, moments)):
        if w.shape[-1] % 128:
            view = lambda t: t.reshape(-1, t.shape[-1]).T
            back = lambda t: t.T.reshape(w.shape)
            g_view = g[:w.shape[-1]]
        else:
            view = lambda t: t.reshape(-1, t.shape[-1])
            back = lambda t: t.reshape(w.shape)
            g_view = view(g)
        delta_w, new_m, new_v = _adamw(view(w), g_view, view(m), view(v), "adamw_%d" % k)
        big_out.append((back(g_view), back(delta_w), back(new_m), back(new_v)))

    tail = jnp.concatenate([d_conv[0:3].reshape(-1)[d:], d_bf[0, :N_HEADS], loss_part[0, :1]])
    small_part = jnp.concatenate(
        [d_nmix0, d_nffn0, d_nmix1, d_pscale, d_nffn1, d_final,
         d_conv[0:3].reshape(1, -1)[:, :d],
         jnp.pad(tail, (0, d - tail.shape[0])).reshape(1, d)], axis=0)
    parts = _gather_small(small_part)

    chip = 2 * lax.axis_index("x") + lax.axis_index("y")
    cw_cols = conv_w_0.shape[1]

    def conv_block(full):
        mine = lax.dynamic_slice_in_dim(full, chip * cw_cols, cw_cols, axis=1)
        return jnp.pad(mine.reshape(-1), (0, d - mine.size))

    def small_rows(vals, cw, bfv):
        return jnp.stack(list(vals) + [cw, jnp.pad(bfv, (0, d - N_HEADS))])

    smalls_w = [norm_mix_0, norm_ffn_0, norm_mix_1, pool_scale_1, norm_ffn_1, final_norm]
    smalls_m = [m_norm_mix_0, m_norm_ffn_0, m_norm_mix_1, m_pool_scale_1, m_norm_ffn_1, m_final_norm]
    smalls_v = [v_norm_mix_0, v_norm_ffn_0, v_norm_mix_1, v_pool_scale_1, v_norm_ffn_1, v_final_norm]
    pad_cw = lambda t: jnp.pad(t.reshape(-1), (0, d - t.size))
    w_rows = small_rows(smalls_w, pad_cw(conv_w_0), b_f_0)
    m_rows = small_rows(smalls_m, pad_cw(m_conv_w_0), m_b_f_0)
    v_rows = small_rows(smalls_v, pad_cw(v_conv_w_0), v_b_f_0)

    g_sum = _sum_devices(parts)
    conv_full = jnp.concatenate([g_sum[6], g_sum[7, :3 * c_conv - d]]).reshape(3, c_conv)
    bf_grad = g_sum[7, 3 * c_conv - d:3 * c_conv - d + N_HEADS]
    loss = g_sum[7, 3 * c_conv - d + N_HEADS]
    g_rows = jnp.concatenate(
        [g_sum[0:6], conv_block(conv_full).reshape(1, d),
         jnp.pad(bf_grad, (0, d - N_HEADS)).reshape(1, d)], axis=0)
    d_rows, nm_rows, nv_rows = _adamw(w_rows, g_rows, m_rows, v_rows, "adamw_small")

    def unpack(rows):
        cw = rows[6, :conv_w_0.size].reshape(conv_w_0.shape)
        return [rows[0], rows[1], rows[2], rows[3], rows[4], rows[5], cw, rows[7, :N_HEADS]]

    def assemble(kind):
        sm = unpack([g_rows, d_rows, nm_rows, nv_rows][kind])
        lg = [t[kind] for t in big_out]
        return [sm[0], lg[0], sm[7], sm[6], lg[1], sm[1], lg[2], lg[3],
                sm[2], lg[4], sm[3], sm[4], lg[5], lg[6], sm[5]]

    return (loss, grad_x[None], *assemble(0), *assemble(1), *assemble(2), *assemble(3))
```

```python
import functools

import jax
import jax.numpy as jnp
from jax import lax
from jax.experimental import pallas as pl
from jax.experimental.pallas import tpu as pltpu

F32 = jnp.float32
BF16 = jnp.bfloat16

RMS_EPS = 1e-6
HEAD_DIM = 64
N_HEADS = 8
ATTN_SCALE = HEAD_DIM ** -0.5
LOG2_E = 1.4426950408889634
POOL_WINDOWS = (2, 4, 8, 16)
POOL_HALO = 16
CONV_HALO = 8
NEG_BIG = -1e30

ADAM_LR = 0.001
ADAM_B1 = 0.9
ADAM_B2 = 0.999
ADAM_EPS = 1e-08
ADAM_WD = 0.01
ADAM_STEP = 10

N_CHIPS = 4
N_DEV = 8
MESH = pl.DeviceIdType.MESH

VMEM_LIMIT_BYTES = 56 * 1024 * 1024

TILE_ROWS = 512
TILE_PROJ_ROWS = 1024
TILE_ATTN = 512
TILE_MLP_ROWS = 1024
TILE_MLP_FF = 1024
TILE_MLP_BWD_FF = 512
TILE_HEAD_ROWS = 256
TILE_WGRAD_K = 1024
TILE_WGRAD_N = 1024
TILE_ELEM_ROWS = 256
SUM_CHUNK_ROWS = 128

LANE_CQ = 64
LANE_CK = 88


def _params(semantics):
    return pltpu.CompilerParams(dimension_semantics=semantics,
                                vmem_limit_bytes=VMEM_LIMIT_BYTES)


def _nn(a, b):
    return lax.dot_general(a, b, (((1,), (0,)), ((), ())), preferred_element_type=F32)


def _nt(a, b):
    return lax.dot_general(a, b, (((1,), (1,)), ((), ())), preferred_element_type=F32)


def _tn(a, b):
    return lax.dot_general(a, b, (((0,), (0,)), ((), ())), preferred_element_type=F32)


def _split3(v):
    hi = v.astype(BF16)
    r1 = v - hi.astype(F32)
    mid = r1.astype(BF16)
    lo = (r1 - mid.astype(F32)).astype(BF16)
    return hi, mid, lo


def _exact_nn(sel, v):
    hi, mid, lo = _split3(v)
    return _nn(sel, hi) + _nn(sel, mid) + _nn(sel, lo)


def _exact_nt(sel, v):
    hi, mid, lo = _split3(v)
    return _nt(sel, hi) + _nt(sel, mid) + _nt(sel, lo)


def _rms_fwd(x, g):
    r = lax.rsqrt(jnp.mean(x * x, axis=-1, keepdims=True) + RMS_EPS)
    return x * r * g, r


def _rms_bwd(dn, x, g):
    r = lax.rsqrt(jnp.mean(x * x, axis=-1, keepdims=True) + RMS_EPS)
    xh = x * r
    gy = dn * g
    dx = r * (gy - xh * jnp.mean(gy * xh, axis=-1, keepdims=True))
    return dx, jnp.sum(dn * xh, axis=0, keepdims=True)


def _lane(shape):
    return lax.broadcasted_iota(jnp.int32, shape, len(shape) - 1)


def _row(shape):
    return lax.broadcasted_iota(jnp.int32, shape, len(shape) - 2)


def _full(a):
    nd = a.ndim
    return pl.BlockSpec(a.shape, lambda *_: (0,) * nd)


def _rms_pre(after, x, g):
    s, d = x.shape
    tm = min(TILE_ROWS, s)

    def body(after_ref, x_ref, g_ref, n_ref):
        n, _ = _rms_fwd(x_ref[...], g_ref[...])
        n_ref[...] = n.astype(BF16)

    rows = pl.BlockSpec((tm, d), lambda i: (i, 0))
    return pl.pallas_call(
        body, name="rms_pre", grid=(s // tm,),
        in_specs=[ANY, rows, _full(g)], out_specs=rows,
        out_shape=jax.ShapeDtypeStruct((s, d), BF16),
        compiler_params=_params(("parallel",)),
    )(after, x, g)


def _in_proj(n, w_qkv, w_f, w_bcx):
    s, d = n.shape
    tm = min(TILE_PROJ_ROWS, s)

    def body(n_ref, wq_ref, wf_ref, wb_ref, qkv_ref, fl_ref, bcx_ref):
        nb = n_ref[...]
        qkv_ref[...] = _nn(nb, wq_ref[...]).astype(BF16)
        fl_ref[...] = _nn(nb, wf_ref[...])
        bcx_ref[...] = _nn(nb, wb_ref[...])

    rows = lambda c: pl.BlockSpec((tm, c), lambda i: (i, 0))
    return pl.pallas_call(
        body, name="in_proj", grid=(s // tm,),
        in_specs=[rows(d), _full(w_qkv), _full(w_f), _full(w_bcx)],
        out_specs=[rows(w_qkv.shape[1]), rows(w_f.shape[1]), rows(w_bcx.shape[1])],
        out_shape=[jax.ShapeDtypeStruct((s, w_qkv.shape[1]), BF16),
                   jax.ShapeDtypeStruct((s, w_f.shape[1]), F32),
                   jax.ShapeDtypeStruct((s, w_bcx.shape[1]), F32)],
        compiler_params=_params(("parallel",)),
    )(n, w_qkv, w_f, w_bcx)


def _gate_prep(fl, bf, qkv):
    s = fl.shape[0]
    a = N_HEADS * HEAD_DIM
    tm = min(TILE_ROWS, s)

    def body(fl_ref, bf_ref, q_ref, k_ref, qa_ref, ka_ref, carry_ref):
        i = pl.program_id(0)

        @pl.when(i == 0)
        def _():
            carry_ref[...] = jnp.zeros_like(carry_ref)

        z = fl_ref[...] + bf_ref[...]
        logf = jnp.minimum(z, 0.0) - jnp.log(1.0 + jnp.exp(-jnp.abs(z)))
        lower = (_lane((tm, tm)) <= _row((tm, tm))).astype(BF16)
        cum = _exact_nn(lower, logf) + carry_ref[0:1, :]
        carry_ref[0:1, :] = cum[tm - 1:tm, :]

        lane = _lane((tm, 128))
        pieces = [p.astype(F32)
                  for p in _split3(jnp.where(lane < N_HEADS, LOG2_E * cum, 0.0))]
        shared_q = sum(pltpu.roll(p, LANE_CQ + N_HEADS * k, axis=1) for k, p in enumerate(pieces))
        shared_k = -sum(pltpu.roll(p, LANE_CK + N_HEADS * k, axis=1) for k, p in enumerate(pieces))
        for h in range(N_HEADS):
            at_q = functools.reduce(jnp.logical_or,
                                    [lane == LANE_CQ + N_HEADS * k + h for k in range(3)])
            at_k = functools.reduce(jnp.logical_or,
                                    [lane == LANE_CK + N_HEADS * k + h for k in range(3)])
            pair = slice((h // 2) * 128, (h // 2 + 1) * 128)
            qp = q_ref[:, pair].astype(F32)
            kp = k_ref[:, pair].astype(F32)
            if h % 2:
                qp = pltpu.roll(qp, HEAD_DIM, axis=1)
                kp = pltpu.roll(kp, HEAD_DIM, axis=1)
            q_bias = jnp.where(at_k, 1.0, shared_q)
            k_bias = jnp.where(at_q, 1.0, shared_k)
            qa_ref[h] = jnp.where(lane < HEAD_DIM, qp * (ATTN_SCALE * LOG2_E), q_bias).astype(BF16)
            ka_ref[h] = jnp.where(lane < HEAD_DIM, kp, k_bias).astype(BF16)

    aug = jax.ShapeDtypeStruct((N_HEADS, s, 128), BF16)
    aug_spec = pl.BlockSpec((N_HEADS, tm, 128), lambda i: (0, i, 0))
    return pl.pallas_call(
        body, name="gate_prep", grid=(s // tm,),
        in_specs=[pl.BlockSpec((tm, 128), lambda i: (i, 0)), _full(bf),
                  pl.BlockSpec((tm, a), lambda i: (i, 0)),
                  pl.BlockSpec((tm, a), lambda i: (i, 1))],
        out_specs=[aug_spec, aug_spec],
        out_shape=[aug, aug],
        scratch_shapes=[pltpu.VMEM((8, 128), F32)],
        compiler_params=_params(("arbitrary",)),
    )(fl, bf, qkv, qkv)


def _attn_fwd(qa, ka, qkv):
    s = qa.shape[1]
    a = N_HEADS * HEAD_DIM
    t = min(TILE_ATTN, s)
    n_pairs = N_HEADS // 2
    v_block0 = 2 * a // 128
    ones_lane = (HEAD_DIM, 0)

    def body(qa_ref, ka_ref, v_ref, o_ref, lse_ref, m_ref, acc_ref, s_even, s_odd):
        i = pl.program_id(1)
        m_ref[...] = jnp.full_like(m_ref, NEG_BIG)
        acc_ref[...] = jnp.zeros_like(acc_ref)
        upper_rows = _row((128, t)) < HEAD_DIM

        def keys(j):
            return pl.ds(pl.multiple_of(j * t, t), t)

        def scores_into(buf, j):
            for e in range(2):
                buf[e] = _nt(ka_ref[e, keys(j), :], qa_ref[e])

        def consume(buf, j, masked):
            vf = v_ref[keys(j), :].astype(F32)
            lane = _lane((t, 128))
            own = [lane < HEAD_DIM, lane >= HEAD_DIM]
            for e in range(2):
                v_head = jnp.where(own[e], vf, jnp.where(lane == ones_lane[e], 1.0, 0.0)).astype(BF16)
                sc = buf[e]
                if masked:
                    sc = jnp.where(_row((t, t)) <= _lane((t, t)), sc, NEG_BIG)
                m_prev = m_ref[e]
                m_new = jnp.maximum(m_prev, jnp.max(sc, axis=0, keepdims=True))
                p = jnp.exp2(sc - m_new).astype(BF16)
                acc_ref[e] = acc_ref[e] * jnp.exp2(m_prev - m_new) + _tn(v_head, p)
                m_ref[e] = m_new

        scores_into(s_even, 0)

        def two_tiles(p, carry):
            j = 2 * p
            scores_into(s_odd, j + 1)
            consume(s_even, j, False)
            scores_into(s_even, j + 2)
            consume(s_odd, j + 1, False)
            return carry

        lax.fori_loop(0, i // 2, two_tiles, 0)

        @pl.when(i % 2 == 0)
        def _():
            consume(s_even, i, True)

        @pl.when(i % 2 == 1)
        def _():
            scores_into(s_odd, i)
            consume(s_even, i - 1, False)
            consume(s_odd, i, True)

        denom = [acc_ref[e, ones_lane[e]:ones_lane[e] + 1, :] for e in range(2)]
        out_t = jnp.where(upper_rows, acc_ref[0] / denom[0], acc_ref[1] / denom[1])
        o_ref[...] = out_t.T.astype(BF16)
        lse = [m_ref[e] + LOG2_E * jnp.log(denom[e]) for e in range(2)]
        lse_ref[...] = jnp.where(_row((8, t)) == 0, lse[0], lse[1])

    return pl.pallas_call(
        body, name="attn_fwd", grid=(n_pairs, s // t),
        in_specs=[pl.BlockSpec((2, t, 128), lambda g, i: (g, i, 0)),
                  pl.BlockSpec((2, s, 128), lambda g, i: (g, 0, 0)),
                  pl.BlockSpec((s, 128), lambda g, i: (0, v_block0 + g))],
        out_specs=[pl.BlockSpec((t, 128), lambda g, i: (i, g)),
                   pl.BlockSpec((None, 8, t), lambda g, i: (g, 0, i))],
        out_shape=[jax.ShapeDtypeStruct((s, a), BF16),
                   jax.ShapeDtypeStruct((n_pairs, 8, s), F32)],
        scratch_shapes=[pltpu.VMEM((2, 1, t), F32), pltpu.VMEM((2, 128, t), F32),
                        pltpu.VMEM((2, t, t), F32), pltpu.VMEM((2, t, t), F32)],
        compiler_params=_params(("parallel", "arbitrary")),
    )(qa, ka, qkv)


def _conv_out(o, bcx, cw, w_out, x):
    s, d = x.shape
    c = o.shape[1]
    tm = min(TILE_PROJ_ROWS, s)

    def body(o_ref, b_ref, c_ref, xin_ref, cw_ref, w_ref, x_ref, h_ref, ubuf):
        i = pl.program_id(0)

        @pl.when(i == 0)
        def _():
            ubuf[0:CONV_HALO, :] = jnp.zeros((CONV_HALO, c), F32)

        u = c_ref[...] * xin_ref[...]
        ubuf[CONV_HALO:CONV_HALO + tm, :] = u
        u1 = ubuf[CONV_HALO - 1:CONV_HALO - 1 + tm, :]
        u2 = ubuf[CONV_HALO - 2:CONV_HALO - 2 + tm, :]
        cv = (cw_ref[0:1, :] * u2 + cw_ref[1:2, :] * u1) + cw_ref[2:3, :] * u
        y = (b_ref[...] * cv).astype(BF16)
        mix = _nn(o_ref[...], w_ref[0:c, :]) + _nn(y, w_ref[c:2 * c, :])
        h_ref[...] = x_ref[...] + mix
        ubuf[0:CONV_HALO, :] = u[tm - CONV_HALO:tm, :]

    col = lambda k: pl.BlockSpec((tm, c), lambda i: (i, k))
    return pl.pallas_call(
        body, name="conv_out", grid=(s // tm,),
        in_specs=[col(0), col(0), col(1), col(2), _full(cw), _full(w_out),
                  pl.BlockSpec((tm, d), lambda i: (i, 0))],
        out_specs=pl.BlockSpec((tm, d), lambda i: (i, 0)),
        out_shape=jax.ShapeDtypeStruct((s, d), F32),
        scratch_shapes=[pltpu.VMEM((tm + CONV_HALO, c), F32)],
        compiler_params=_params(("arbitrary",)),
    )(o, bcx, bcx, bcx, cw, w_out, x)


def _mlp_fwd(h, g, w_up, w_down, name, head=None):
    s, d = h.shape
    ff = w_down.shape[0]
    slot_cols = w_up.shape[2]
    tm = min(TILE_MLP_ROWS, s)
    tf = min(TILE_MLP_FF, slot_cols)
    per_slot = slot_cols // tf
    nf = ff // tf
    n_head = 0 if head is None else 2
    chunk = min(TILE_HEAD_ROWS, tm)

    def body(*refs):
        h_ref, g_ref, wu_ref, wd_ref = refs[:4]
        out_ref, a_ref, n_ref = refs[4 + n_head:7 + n_head]
        nb_ref, acc_ref = refs[9 + n_head:11 + n_head] if head else refs[-2:]
        i = pl.program_id(0)
        f = pl.program_id(1)

        def target_copy():
            t_hbm, t_buf, t_sem = refs[5], refs[-2], refs[-1]
            return pltpu.make_async_copy(t_hbm.at[pl.ds(pl.multiple_of(i * tm, tm), tm), :],
                                         t_buf, t_sem)

        @pl.when(f == 0)
        def _():
            n, _ = _rms_fwd(h_ref[...], g_ref[...])
            nb = n.astype(BF16)
            nb_ref[...] = nb
            n_ref[...] = nb
            acc_ref[...] = jnp.zeros_like(acc_ref)
            if head is not None:
                target_copy().start()

        pre = _nn(nb_ref[...], wu_ref[...])
        a_ref[...] = pre.astype(BF16)
        r = jnp.square(jnp.maximum(pre, 0.0)).astype(BF16)
        acc_ref[...] += _nn(r, wd_ref[...])

        @pl.when(f == nf - 1)
        def _():
            if head is None:
                out_ref[...] = h_ref[...] + acc_ref[...]
            else:
                gf_ref, t_buf = refs[4], refs[-2]
                loss_ref, dg_ref = refs[7 + n_head:9 + n_head]
                target_copy().wait()
                part, dg = None, None
                for r0 in range(0, tm, chunk):
                    rows_ = slice(r0, r0 + chunk)
                    out = h_ref[rows_, :] + acc_ref[rows_, :]
                    y, _ = _rms_fwd(out, gf_ref[...])
                    err = y - t_buf[rows_, :]
                    p = 0.5 * jnp.sum(jnp.mean(err * err, axis=-1, keepdims=True), axis=0,
                                      keepdims=True)
                    dx, dgp = _rms_bwd(err / d, out, gf_ref[...])
                    out_ref[rows_, :] = dx
                    part = p if part is None else part + p
                    dg = dgp if dg is None else dg + dgp
                part = jnp.broadcast_to(part, loss_ref.shape)

                @pl.when(i == 0)
                def _():
                    loss_ref[...] = part
                    dg_ref[...] = dg

                @pl.when(i > 0)
                def _():
                    loss_ref[...] += part
                    dg_ref[...] += dg

    rows = pl.BlockSpec((tm, d), lambda i, f: (i, 0))
    in_specs = [rows, _full(g),
                pl.BlockSpec((None, d, tf), lambda i, f: (f // per_slot, 0, f % per_slot)),
                pl.BlockSpec((tf, d), lambda i, f: (f, 0))]
    out_specs = [rows, pl.BlockSpec((tm, tf), lambda i, f: (i, f)), rows]
    out_shape = [jax.ShapeDtypeStruct((s, d), F32), jax.ShapeDtypeStruct((s, ff), BF16),
                 jax.ShapeDtypeStruct((s, d), BF16)]
    args = [h, g, w_up, w_down]
    scratch = [pltpu.VMEM((tm, d), BF16), pltpu.VMEM((tm, d), F32)]
    if head is not None:
        in_specs += [_full(head[0]), ANY]
        args += list(head)
        out_specs += [pl.BlockSpec((1, 128), lambda i, f: (0, 0)),
                      pl.BlockSpec((1, d), lambda i, f: (0, 0))]
        out_shape += [jax.ShapeDtypeStruct((1, 128), F32), jax.ShapeDtypeStruct((1, d), F32)]
        scratch += [pltpu.VMEM((tm, d), F32), pltpu.SemaphoreType.DMA]
    return pl.pallas_call(
        body, name=name, grid=(s // tm, nf),
        in_specs=in_specs, out_specs=out_specs, out_shape=out_shape, scratch_shapes=scratch,
        compiler_params=_params(("parallel" if head is None else "arbitrary", "arbitrary")),
    )(*args)


def _window_sum_down(e, window):
    step = 1
    while step < window:
        e = e + pltpu.roll(e, step, axis=0)
        step *= 2
    return e


def _window_sum_up(e, window):
    n = e.shape[0]
    step = 1
    while step < window:
        e = e + pltpu.roll(e, n - step, axis=0)
        step *= 2
    return e


def _pool_counts(first_row, tm, window):
    t = first_row + _row((tm, 1))
    return jnp.minimum(t + 1, window).astype(F32)


def _pool_fwd(h, g, pw, ps):
    s, d = h.shape
    cg = d // len(POOL_WINDOWS)
    tm = min(TILE_PROJ_ROWS, s)

    def body(h_ref, g_ref, pw_ref, ps_ref, out_ref, nbuf):
        i = pl.program_id(0)

        @pl.when(i == 0)
        def _():
            nbuf[0:POOL_HALO, :] = jnp.zeros((POOL_HALO, d), F32)

        n, _ = _rms_fwd(h_ref[...], g_ref[...])
        nbuf[POOL_HALO:POOL_HALO + tm, :] = n
        for k, window in enumerate(POOL_WINDOWS):
            cols = slice(k * cg, (k + 1) * cg)
            sums = _window_sum_down(nbuf[:, cols], window)[POOL_HALO:, :]
            pooled = sums / _pool_counts(i * tm, tm, window) - n[:, cols]
            y = _nn(pooled.astype(BF16), pw_ref[k]) * ps_ref[:, cols]
            out_ref[:, cols] = h_ref[:, cols] + y
        nbuf[0:POOL_HALO, :] = n[tm - POOL_HALO:tm, :]

    return pl.pallas_call(
        body, name="pool_fwd", grid=(s // tm,),
        in_specs=[pl.BlockSpec((tm, d), lambda i: (i, 0)), _full(g), _full(pw), _full(ps)],
        out_specs=pl.BlockSpec((tm, d), lambda i: (i, 0)),
        out_shape=jax.ShapeDtypeStruct((s, d), F32),
        scratch_shapes=[pltpu.VMEM((tm + POOL_HALO, d), F32)],
        compiler_params=_params(("arbitrary",)),
    )(h, g, pw, ps)


def _mlp_bwd_x(dz, a, w_up, w_down, h_in, g, name):
    s, d = dz.shape
    ff = w_down.shape[0]
    slot_cols = w_up.shape[2]
    tm = min(TILE_MLP_ROWS, s)
    tf = min(TILE_MLP_BWD_FF, slot_cols)
    per_slot = slot_cols // tf
    nf = ff // tf

    def body(dz_ref, a_ref, wu_ref, wd_ref, h_ref, g_ref, da_ref, dzb_ref, dh_ref, dg_ref,
             dzs_ref, acc_ref):
        i = pl.program_id(0)
        f = pl.program_id(1)

        @pl.when(f == 0)
        def _():
            dzb = dz_ref[...].astype(BF16)
            dzs_ref[...] = dzb
            dzb_ref[...] = dzb
            acc_ref[...] = jnp.zeros_like(acc_ref)

        dr = _nt(dzs_ref[...], wd_ref[...])
        da = (dr * (2.0 * jnp.maximum(a_ref[...].astype(F32), 0.0))).astype(BF16)
        da_ref[...] = da
        acc_ref[...] += _nt(da, wu_ref[...])

        @pl.when(f == nf - 1)
        def _():
            dx, dg = _rms_bwd(acc_ref[...], h_ref[...], g_ref[...])
            dh_ref[...] = dz_ref[...] + dx

            @pl.when(i == 0)
            def _():
                dg_ref[...] = dg

            @pl.when(i > 0)
            def _():
                dg_ref[...] += dg

    return pl.pallas_call(
        body, name=name, grid=(s // tm, nf),
        in_specs=[pl.BlockSpec((tm, d), lambda i, f: (i, 0)),
                  pl.BlockSpec((tm, tf), lambda i, f: (i, f)),
                  pl.BlockSpec((None, d, tf), lambda i, f: (f // per_slot, 0, f % per_slot)),
                  pl.BlockSpec((tf, d), lambda i, f: (f, 0)),
                  pl.BlockSpec((tm, d), lambda i, f: (i, 0)), _full(g)],
        out_specs=[pl.BlockSpec((tm, tf), lambda i, f: (i, f)),
                   pl.BlockSpec((tm, d), lambda i, f: (i, 0)),
                   pl.BlockSpec((tm, d), lambda i, f: (i, 0)),
                   pl.BlockSpec((1, d), lambda i, f: (0, 0))],
        out_shape=[jax.ShapeDtypeStruct((s, ff), BF16),
                   jax.ShapeDtypeStruct((s, d), BF16),
                   jax.ShapeDtypeStruct((s, d), F32),
                   jax.ShapeDtypeStruct((1, d), F32)],
        scratch_shapes=[pltpu.VMEM((tm, d), BF16), pltpu.VMEM((tm, d), F32)],
        compiler_params=_params(("arbitrary", "arbitrary")),
    )(dz, a, w_up, w_down, h_in, g)


def _mlp_bwd_w(n, da, a, dzb, slot_cols, name):
    s, d = n.shape
    ff = a.shape[1]
    tn = min(TILE_WGRAD_N, slot_cols)
    tk = min(TILE_WGRAD_K, s)
    per_slot = slot_cols // tn
    nk = s // tk

    def body(n_ref, da_ref, a_ref, dz_ref, du_ref, dd_ref, accu_ref, accd_ref):
        k = pl.program_id(1)

        @pl.when(k == 0)
        def _():
            accu_ref[...] = jnp.zeros_like(accu_ref)
            accd_ref[...] = jnp.zeros_like(accd_ref)

        accu_ref[...] += _tn(n_ref[...], da_ref[...])
        r = jnp.square(jnp.maximum(a_ref[...].astype(F32), 0.0)).astype(BF16)
        accd_ref[...] += _tn(r, dz_ref[...])

        @pl.when(k == nk - 1)
        def _():
            du_ref[...] = accu_ref[...].astype(BF16)
            dd_ref[...] = accd_ref[...].astype(BF16)

    return pl.pallas_call(
        body, name=name, grid=(ff // tn, nk),
        in_specs=[pl.BlockSpec((tk, d), lambda f, k: (k, 0)),
                  pl.BlockSpec((tk, tn), lambda f, k: (k, f)),
                  pl.BlockSpec((tk, tn), lambda f, k: (k, f)),
                  pl.BlockSpec((tk, d), lambda f, k: (k, 0))],
        out_specs=[pl.BlockSpec((None, d, tn), lambda f, k: (f // per_slot, 0, f % per_slot)),
                   pl.BlockSpec((tn, d), lambda f, k: (f, 0))],
        out_shape=[jax.ShapeDtypeStruct((ff // slot_cols, d, slot_cols), BF16),
                   jax.ShapeDtypeStruct((ff, d), BF16)],
        scratch_shapes=[pltpu.VMEM((d, tn), F32), pltpu.VMEM((tn, d), F32)],
        compiler_params=_params(("parallel", "arbitrary")),
    )(n, da, a, dzb)


def _pool_bwd(after, dh, h, g, pw, ps):
    s, d = h.shape
    cg = d // len(POOL_WINDOWS)
    tm = min(TILE_ROWS, s)
    nb = s // tm
    halo_per_tile = tm // POOL_HALO

    def body(after_ref, dh_ref, h_ref, halo_ref, g_ref, pw_ref, ps_ref,
             dx_ref, dpw_ref, dps_ref, dg_ref, nbuf, qbuf, dn_ref, carry, dpw_acc):
        i = pl.program_id(0)
        blk = nb - 1 - i

        @pl.when(i == 0)
        def _():
            carry[...] = jnp.zeros_like(carry)
            dpw_acc[...] = jnp.zeros_like(dpw_acc)
            dps_ref[...] = jnp.zeros_like(dps_ref)
            dg_ref[...] = jnp.zeros_like(dg_ref)

        hv = h_ref[...]
        n, _ = _rms_fwd(hv, g_ref[...])
        nh, _ = _rms_fwd(halo_ref[...], g_ref[...])
        nbuf[0:POOL_HALO, :] = jnp.where(blk == 0, 0.0, nh)
        nbuf[POOL_HALO:POOL_HALO + tm, :] = n
        dhv = dh_ref[...]
        for k, window in enumerate(POOL_WINDOWS):
            cols = slice(k * cg, (k + 1) * cg)
            cnt = _pool_counts(blk * tm, tm, window)
            sums = _window_sum_down(nbuf[:, cols], window)[POOL_HALO:, :]
            pb = (sums / cnt - n[:, cols]).astype(BF16)
            dyk = dhv[:, cols]
            dps_ref[:, cols] += jnp.sum(dyk * _nn(pb, pw_ref[k]), axis=0, keepdims=True)
            dyb = (dyk * ps_ref[:, cols]).astype(BF16)
            dpw_acc[k] += _tn(pb, dyb)
            dpool = _nt(dyb, pw_ref[k])
            qv = dpool / cnt
            qbuf[0:tm, cols] = qv
            qbuf[tm:tm + POOL_HALO, cols] = carry[:, cols]
            dn_ref[:, cols] = _window_sum_up(qbuf[:, cols], window)[0:tm, :] - dpool
            carry[:, cols] = qv[0:POOL_HALO, :]
        dx, dg = _rms_bwd(dn_ref[...], hv, g_ref[...])
        dx_ref[...] = dhv + dx
        dg_ref[...] += dg

        @pl.when(i == nb - 1)
        def _():
            dpw_ref[...] = dpw_acc[...].astype(BF16)

    rev = lambda i: (nb - 1 - i, 0)
    return pl.pallas_call(
        body, name="pool_bwd", grid=(nb,),
        in_specs=[ANY, pl.BlockSpec((tm, d), rev), pl.BlockSpec((tm, d), rev),
                  pl.BlockSpec((POOL_HALO, d),
                               lambda i: (jnp.maximum((nb - 1 - i) * halo_per_tile - 1, 0), 0)),
                  _full(g), _full(pw), _full(ps)],
        out_specs=[pl.BlockSpec((tm, d), rev), _full(pw),
                   pl.BlockSpec((1, d), lambda i: (0, 0)),
                   pl.BlockSpec((1, d), lambda i: (0, 0))],
        out_shape=[jax.ShapeDtypeStruct((s, d), F32),
                   jax.ShapeDtypeStruct(pw.shape, BF16),
                   jax.ShapeDtypeStruct((1, d), F32),
                   jax.ShapeDtypeStruct((1, d), F32)],
        scratch_shapes=[pltpu.VMEM((tm + POOL_HALO, d), F32), pltpu.VMEM((tm + POOL_HALO, d), F32),
                        pltpu.VMEM((tm, d), F32), pltpu.VMEM((POOL_HALO, d), F32),
                        pltpu.VMEM(pw.shape, F32)],
        compiler_params=_params(("arbitrary",)),
    )(after, dh, h, h, g, pw, ps)


def _conv_out_bwd(after, dh, w_out, o, bcx, cw):
    s, d = dh.shape
    c = o.shape[1]
    tm = min(TILE_ROWS, s)
    nb = s // tm
    halo_per_tile = tm // CONV_HALO

    def body(after_ref, dh_ref, w_ref, o_ref, b_ref, c_ref, xin_ref, ch_ref, xh_ref, cw_ref,
             do_ref, delta_ref, dbcx_ref, dw_ref, dcw_ref, ubuf, dbuf, carry, acc):
        i = pl.program_id(0)
        blk = nb - 1 - i

        @pl.when(i == 0)
        def _():
            carry[...] = jnp.zeros_like(carry)
            acc[...] = jnp.zeros_like(acc)
            dcw_ref[...] = jnp.zeros_like(dcw_ref)

        dm = dh_ref[...].astype(BF16)
        dcat = _nt(dm, w_ref[...])
        do = dcat[:, 0:c]
        dy = dcat[:, c:2 * c]
        do_ref[...] = do.astype(BF16)
        head_of_lane = lax.shift_right_logical(_lane((8, c)), HEAD_DIM.bit_length() - 1)
        heads = (head_of_lane == _row((8, c))).astype(BF16)
        delta_ref[...] = _exact_nt(heads, do * o_ref[...].astype(F32))

        cv_ = c_ref[...]
        xin = xin_ref[...]
        bv = b_ref[...]
        u = cv_ * xin
        ubuf[0:CONV_HALO, :] = jnp.where(blk == 0, 0.0, ch_ref[...] * xh_ref[...])
        ubuf[CONV_HALO:CONV_HALO + tm, :] = u
        u1 = ubuf[CONV_HALO - 1:CONV_HALO - 1 + tm, :]
        u2 = ubuf[CONV_HALO - 2:CONV_HALO - 2 + tm, :]
        w0, w1, w2 = cw_ref[0:1, :], cw_ref[1:2, :], cw_ref[2:3, :]
        cv = (w0 * u2 + w1 * u1) + w2 * u
        acc[0:c, :] += _tn(o_ref[...], dm)
        acc[c:2 * c, :] += _tn((bv * cv).astype(BF16), dm)

        dcv = dy * bv
        dcw_ref[0:1, :] += jnp.sum(dcv * u2, axis=0, keepdims=True)
        dcw_ref[1:2, :] += jnp.sum(dcv * u1, axis=0, keepdims=True)
        dcw_ref[2:3, :] += jnp.sum(dcv * u, axis=0, keepdims=True)
        dbuf[0:tm, :] = dcv
        dbuf[tm:tm + CONV_HALO, :] = carry[...]
        du = w2 * dcv + w1 * dbuf[1:1 + tm, :] + w0 * dbuf[2:2 + tm, :]
        dbcx_ref[:, 0:c] = (dy * cv).astype(BF16)
        dbcx_ref[:, c:2 * c] = (du * xin).astype(BF16)
        dbcx_ref[:, 2 * c:3 * c] = (du * cv_).astype(BF16)
        carry[...] = dcv[0:CONV_HALO, :]

        @pl.when(i == nb - 1)
        def _():
            dw_ref[...] = acc[...].astype(BF16)

    rev = lambda k: (lambda i: (nb - 1 - i, k))
    halo = lambda k: (lambda i: (jnp.maximum((nb - 1 - i) * halo_per_tile - 1, 0), k))
    return pl.pallas_call(
        body, name="conv_out_bwd", grid=(nb,),
        in_specs=[ANY, pl.BlockSpec((tm, d), rev(0)), _full(w_out), pl.BlockSpec((tm, c), rev(0)),
                  pl.BlockSpec((tm, c), rev(0)), pl.BlockSpec((tm, c), rev(1)),
                  pl.BlockSpec((tm, c), rev(2)),
                  pl.BlockSpec((CONV_HALO, c), halo(1)), pl.BlockSpec((CONV_HALO, c), halo(2)),
                  _full(cw)],
        out_specs=[pl.BlockSpec((tm, c), rev(0)),
                   pl.BlockSpec((8, tm), lambda i: (0, nb - 1 - i)),
                   pl.BlockSpec((tm, 3 * c), rev(0)),
                   _full(w_out), _full(cw)],
        out_shape=[jax.ShapeDtypeStruct((s, c), BF16),
                   jax.ShapeDtypeStruct((8, s), F32),
                   jax.ShapeDtypeStruct((s, 3 * c), BF16),
                   jax.ShapeDtypeStruct(w_out.shape, BF16),
                   jax.ShapeDtypeStruct(cw.shape, F32)],
        scratch_shapes=[pltpu.VMEM((tm + CONV_HALO, c), F32), pltpu.VMEM((tm + CONV_HALO, c), F32),
                        pltpu.VMEM((CONV_HALO, c), F32), pltpu.VMEM(w_out.shape, F32)],
        compiler_params=_params(("arbitrary",)),
    )(after, dh, w_out, o, bcx, bcx, bcx, bcx, bcx, cw)


def _attn_bwd(after, qa, ka, qkv, do, lse, delta):
    s = qa.shape[1]
    a = N_HEADS * HEAD_DIM
    t = min(TILE_ATTN, s)
    nq = s // t
    n_pairs = N_HEADS // 2
    v_block0 = 2 * a // 128

    def body(after_ref, ka_ref, v_ref, qa_ref, do_ref, lse_ref, delta_ref,
             dqt_ref, dka_ref, dv_ref, dv_acc):
        g = pl.program_id(0)
        j = pl.program_id(1)

        @pl.when(j == 0)
        def _():
            dqt_ref[...] = jnp.zeros_like(dqt_ref)

        lane = _lane((t, 128))
        vf = v_ref[...].astype(F32)
        v_heads = [jnp.where(lane < HEAD_DIM, vf, 0.0).astype(BF16),
                   jnp.where(lane >= HEAD_DIM, vf, 0.0).astype(BF16)]
        ke_t = [ka_ref[e].astype(F32).T.astype(BF16) for e in range(2)]

        def q_step(i, first):
            qs = pl.ds(pl.multiple_of(i * t, t), t)
            dob = do_ref[qs, :]
            for e in range(2):
                qe = qa_ref[e, qs, :]
                sc = _nt(ka_ref[e], qe)
                if first:
                    sc = jnp.where(_row((t, t)) <= _lane((t, t)), sc, NEG_BIG)
                p = jnp.exp2(sc - lse_ref[pl.ds(e, 1), qs])
                dv_part = _nn(p.astype(BF16), dob)
                dp = _nt(v_heads[e], dob)
                ds = (p * (dp - delta_ref[pl.ds(2 * g + e, 1), qs])).astype(BF16)
                dk_part = _nn(ds, qe)
                if first:
                    dv_acc[e] = dv_part
                    dka_ref[e] = dk_part
                else:
                    dv_acc[e] += dv_part
                    dka_ref[e] += dk_part
                dqt_ref[e, :, qs] += _nn(ke_t[e], ds)

        q_step(j, True)

        def full_step(i, carry):
            q_step(i, False)
            return carry

        lax.fori_loop(j + 1, nq, full_step, 0)
        dv_ref[...] = jnp.where(lane < HEAD_DIM, dv_acc[0], dv_acc[1]).astype(BF16)

    return pl.pallas_call(
        body, name="attn_bwd", grid=(n_pairs, nq),
        in_specs=[ANY, pl.BlockSpec((2, t, 128), lambda g, j: (g, j, 0)),
                  pl.BlockSpec((t, 128), lambda g, j: (j, v_block0 + g)),
                  pl.BlockSpec((2, s, 128), lambda g, j: (g, 0, 0)),
                  pl.BlockSpec((s, 128), lambda g, j: (0, g)),
                  pl.BlockSpec((None, 8, s), lambda g, j: (g, 0, 0)),
                  pl.BlockSpec((8, s), lambda g, j: (0, 0))],
        out_specs=[pl.BlockSpec((2, 128, s), lambda g, j: (g, 0, 0)),
                   pl.BlockSpec((2, t, 128), lambda g, j: (g, j, 0)),
                   pl.BlockSpec((t, 128), lambda g, j: (j, g))],
        out_shape=[jax.ShapeDtypeStruct((N_HEADS, 128, s), F32),
                   jax.ShapeDtypeStruct((N_HEADS, s, 128), F32),
                   jax.ShapeDtypeStruct((s, a), BF16)],
        scratch_shapes=[pltpu.VMEM((2, t, 128), F32)],
        compiler_params=_params(("parallel", "arbitrary")),
    )(after, ka, qkv, qa, do, lse, delta)


def _gate_bwd(dqa, dka, dv, fl, bf):
    s = fl.shape[0]
    a = N_HEADS * HEAD_DIM
    tm = min(TILE_ROWS, s)
    nb = s // tm

    def body(dqa_ref, dka_ref, dv_ref, fl_ref, bf_ref, dqkv_ref, dfl_ref, dbf_ref, carry):
        i = pl.program_id(0)

        @pl.when(i == 0)
        def _():
            carry[...] = jnp.zeros_like(carry)
            dbf_ref[...] = jnp.zeros_like(dbf_ref)

        lane = _lane((tm, 128))
        dq_sum = jnp.zeros((tm, 128), F32)
        dk_sum = jnp.zeros((tm, 128), F32)
        for pair in range(N_HEADS // 2):
            qs, ks = [], []
            for e in range(2):
                h = 2 * pair + e
                dq = dqa_ref[h].T
                dk = dka_ref[h]
                dq_sum = dq_sum + dq
                dk_sum = dk_sum + dk
                qs.append(dq * ATTN_SCALE)
                ks.append(dk * (1.0 / LOG2_E))
            cols = slice(pair * 128, (pair + 1) * 128)
            dqkv_ref[:, cols] = jnp.where(
                lane < HEAD_DIM, qs[0], pltpu.roll(qs[1], HEAD_DIM, axis=1)).astype(BF16)
            dqkv_ref[:, a + pair * 128:a + (pair + 1) * 128] = jnp.where(
                lane < HEAD_DIM, ks[0], pltpu.roll(ks[1], HEAD_DIM, axis=1)).astype(BF16)
        dqkv_ref[:, 2 * a:3 * a] = dv_ref[...]

        in_q = (lane >= LANE_CQ) & (lane < LANE_CQ + N_HEADS)
        in_k = (lane >= LANE_CK) & (lane < LANE_CK + N_HEADS)
        dcum = (pltpu.roll(jnp.where(in_q, dq_sum, 0.0), 128 - LANE_CQ, axis=1)
                - pltpu.roll(jnp.where(in_k, dk_sum, 0.0), 128 - LANE_CK, axis=1))

        upper = (_lane((tm, tm)) >= _row((tm, tm))).astype(BF16)
        dlogf = _exact_nn(upper, dcum) + carry[0:1, :]
        carry[0:1, :] = dlogf[0:1, :]
        z = fl_ref[...] + bf_ref[...]
        ez = jnp.exp(-jnp.abs(z))
        sig_neg = jnp.where(z >= 0.0, ez, 1.0) / (1.0 + ez)
        dz = jnp.where(lane < N_HEADS, dlogf * sig_neg, 0.0)
        dfl_ref[...] = dz.astype(BF16)
        dbf_ref[...] += jnp.sum(dz, axis=0, keepdims=True)

    rev3 = lambda i: (0, nb - 1 - i, 0)
    rev = lambda i: (nb - 1 - i, 0)
    return pl.pallas_call(
        body, name="gate_bwd", grid=(nb,),
        in_specs=[pl.BlockSpec((N_HEADS, 128, tm), lambda i: (0, 0, nb - 1 - i)),
                  pl.BlockSpec((N_HEADS, tm, 128), rev3),
                  pl.BlockSpec((tm, a), rev), pl.BlockSpec((tm, 128), rev), _full(bf)],
        out_specs=[pl.BlockSpec((tm, 3 * a), rev), pl.BlockSpec((tm, 128), rev),
                   pl.BlockSpec((1, 128), lambda i: (0, 0))],
        out_shape=[jax.ShapeDtypeStruct((s, 3 * a), BF16),
                   jax.ShapeDtypeStruct((s, 128), BF16),
                   jax.ShapeDtypeStruct((1, 128), F32)],
        scratch_shapes=[pltpu.VMEM((8, 128), F32)],
        compiler_params=_params(("arbitrary",)),
    )(dqa, dka, dv, fl, bf)


def _in_proj_bwd(after, dqkv, dfl, dbcx, w_qkv, w_f, w_bcx, x, g, dh):
    s, d = x.shape
    tm = min(TILE_ROWS, s)

    def body(after_ref, dq_ref, df_ref, db_ref, wq_ref, wf_ref, wb_ref, x_ref, g_ref, dh_ref,
             gx_ref, dg_ref):
        i = pl.program_id(0)
        dn = (_nt(dq_ref[...], wq_ref[...]) + _nt(df_ref[...], wf_ref[...])
              + _nt(db_ref[...], wb_ref[...]))
        dx, dg = _rms_bwd(dn, x_ref[...], g_ref[...])
        gx_ref[...] = dh_ref[...] + dx

        @pl.when(i == 0)
        def _():
            dg_ref[...] = dg

        @pl.when(i > 0)
        def _():
            dg_ref[...] += dg

    rows = lambda c: pl.BlockSpec((tm, c), lambda i: (i, 0))
    return pl.pallas_call(
        body, name="in_proj_bwd", grid=(s // tm,),
        in_specs=[ANY, rows(dqkv.shape[1]), rows(dfl.shape[1]), rows(dbcx.shape[1]),
                  _full(w_qkv), _full(w_f), _full(w_bcx), rows(d), _full(g), rows(d)],
        out_specs=[rows(d), pl.BlockSpec((1, d), lambda i: (0, 0))],
        out_shape=[jax.ShapeDtypeStruct((s, d), F32), jax.ShapeDtypeStruct((1, d), F32)],
        compiler_params=_params(("arbitrary",)),
    )(after, dqkv, dfl, dbcx, w_qkv, w_f, w_bcx, x, g, dh)


def _wgrad_in(n, dys):
    s, d = n.shape
    m = len(dys)
    tk = min(TILE_ROWS, s)
    nk = s // tk

    def body(*refs):
        n_ref, dy_refs, dw_refs, accs = refs[0], refs[1:1 + m], refs[1 + m:1 + 2 * m], refs[1 + 2 * m:]
        k = pl.program_id(0)

        @pl.when(k == 0)
        def _():
            for acc in accs:
                acc[...] = jnp.zeros_like(acc)

        nb = n_ref[...]
        for dy_ref, acc in zip(dy_refs, accs):
            acc[...] += _tn(nb, dy_ref[...])

        @pl.when(k == nk - 1)
        def _():
            for dw_ref, acc in zip(dw_refs, accs):
                dw_ref[...] = acc[...].T.astype(BF16)

    return pl.pallas_call(
        body, name="wgrad_in", grid=(nk,),
        in_specs=[pl.BlockSpec((tk, d), lambda k: (k, 0))]
        + [pl.BlockSpec((tk, dy.shape[1]), lambda k: (k, 0)) for dy in dys],
        out_specs=[pl.BlockSpec((dy.shape[1], d), lambda k: (0, 0)) for dy in dys],
        out_shape=[jax.ShapeDtypeStruct((dy.shape[1], d), BF16) for dy in dys],
        scratch_shapes=[pltpu.VMEM((d, dy.shape[1]), F32) for dy in dys],
        compiler_params=_params(("arbitrary",)),
    )(n, *dys)


def _row_tile(rows):
    t = min(TILE_ELEM_ROWS, rows)
    while rows % t:
        t //= 2
    return t


def _adamw_math(w, g, m, v):
    m = ADAM_B1 * m + (1.0 - ADAM_B1) * g
    v = ADAM_B2 * v + (1.0 - ADAM_B2) * jnp.square(g)
    m_hat = m / (1.0 - ADAM_B1 ** ADAM_STEP)
    v_hat = v / (1.0 - ADAM_B2 ** ADAM_STEP)
    delta = -ADAM_LR * (m_hat / (jnp.sqrt(v_hat) + ADAM_EPS) + ADAM_WD * w)
    return delta, m, v


def _adamw(w, g, m, v, name):
    rows, cols = w.shape

    def body(w_ref, g_ref, m_ref, v_ref, d_ref, nm_ref, nv_ref):
        delta, nm, nv = _adamw_math(w_ref[...], g_ref[...], m_ref[...], v_ref[...])
        d_ref[...] = delta
        nm_ref[...] = nm
        nv_ref[...] = nv

    if rows % 8 == 0:
        tr = _row_tile(rows)
        grid, spec = (rows // tr,), pl.BlockSpec((tr, cols), lambda i: (i, 0))
    else:
        grid, spec = (cols // 256,), pl.BlockSpec((rows, 256), lambda i: (0, i))
    out = jax.ShapeDtypeStruct(w.shape, F32)
    return pl.pallas_call(
        body, name=name, grid=grid, in_specs=[spec] * 4, out_specs=[spec] * 3,
        out_shape=[out, out, out], compiler_params=_params(("parallel",)),
    )(w, g, m, v)


def _sum_devices(parts):
    def body(p_ref, g_ref):
        g = p_ref[0]
        for k in range(1, N_DEV):
            g = g + p_ref[k]
        g_ref[...] = g

    return pl.pallas_call(
        body, name="sum_devices",
        in_specs=[pl.BlockSpec(memory_space=pltpu.VMEM)],
        out_specs=pl.BlockSpec(memory_space=pltpu.VMEM),
        out_shape=jax.ShapeDtypeStruct(parts.shape[1:], F32),
    )(parts)


def _mesh_position():
    x, y, c = lax.axis_index("x"), lax.axis_index("y"), lax.axis_index("c")
    chips = [(1 - x, y), (x, 1 - y), (1 - x, 1 - y)]
    return x, y, c, chips


ANY = pl.BlockSpec(memory_space=pl.ANY)
HBM = pl.BlockSpec(memory_space=pltpu.HBM)
SEM = pl.BlockSpec(memory_space=pltpu.SEMAPHORE)
SPLIT_COPY_EFFECT = pltpu.SideEffectType.DATAFLOW_SIDE_EFFECTING


def _in_hbm(a):
    return pltpu.with_memory_space_constraint(a, pltpu.HBM)


def _chip_copies(views, srcs, lands, send, recv, waiting=False):
    _, _, c, chips = _mesh_position()
    cps = []
    for a in range(len(srcs)):
        for k, (px, py) in enumerate(chips):
            src, dst = views(a, k, srcs[a], lands[a], c, 2 * px + py)
            sem = a * (N_CHIPS - 1) + k
            cps.append(pltpu.make_async_remote_copy(
                src_ref=src, dst_ref=dst, send_sem=send.at[sem], recv_sem=recv.at[sem],
                device_id=(px, py, c), device_id_type=MESH))
    return cps


def _ici_start(sources, land_shapes, copies, after, name, per_array=N_CHIPS - 1):
    n = len(sources)

    def body(*refs):
        srcs, lands = refs[:n], refs[n:2 * n]
        send, recv = refs[2 * n + 1], refs[2 * n + 2]
        token = refs[-1]
        for cp in copies(srcs, lands, send, recv, False):
            cp.start()
        token[...] = jnp.zeros_like(token)

    lands = [_in_hbm(lax.empty(s.shape, s.dtype)) for s in land_shapes]
    outs = pl.pallas_call(
        body, name=name,
        in_specs=[HBM] * (2 * n) + [ANY],
        out_specs=[SEM, SEM] + [HBM] * (2 * n) + [pl.BlockSpec(memory_space=pltpu.VMEM)],
        out_shape=[pltpu.SemaphoreType.DMA((n * per_array,))] * 2
        + [pltpu.HBM(a.shape, a.dtype) for a in sources]
        + [pltpu.HBM(s.shape, s.dtype) for s in land_shapes]
        + [jax.ShapeDtypeStruct((8, 128), F32)],
        input_output_aliases={i: 2 + i for i in range(2 * n)},
        compiler_params=pltpu.CompilerParams(has_side_effects=SPLIT_COPY_EFFECT),
    )(*[_in_hbm(a) for a in sources], *lands, after)
    return outs[0], outs[1], list(outs[2:2 + n]), list(outs[2 + n:2 + 2 * n]), outs[-1]


def _ici_wait(handle, copies, after, name):
    send, recv, srcs, lands, _ = handle
    n = len(srcs)

    def body(*refs):
        src_refs, land_refs = refs[:n], refs[n:2 * n]
        for cp in copies(src_refs, land_refs, refs[2 * n], refs[2 * n + 1], True):
            cp.wait_send()
            cp.wait_recv()

    outs = pl.pallas_call(
        body, name=name,
        in_specs=[HBM] * (2 * n) + [SEM, SEM, ANY],
        out_specs=[HBM] * (2 * n),
        out_shape=[pltpu.HBM(a.shape, a.dtype) for a in srcs]
        + [pltpu.HBM(a.shape, a.dtype) for a in lands],
        input_output_aliases={i: i for i in range(2 * n)},
        compiler_params=pltpu.CompilerParams(has_side_effects=SPLIT_COPY_EFFECT),
    )(*srcs, *lands, send, recv, after)
    return list(outs[:n]), list(outs[n:])


def _gather_views(split):
    def views(a, k, src, land, c, slot):
        if split[a]:
            half = src.shape[0] // 2
            src = src.at[pl.ds(c * half, half)]
        return src, land.at[k]
    return views


def _gather_whole_views(a, k, src, land, c, slot):
    x, y, _, _ = _mesh_position()
    return src, land.at[2 * x + y]


SCATTER_COPIES = 2 * (N_CHIPS - 1)


def _scatter_copies(srcs, lands, send, recv, waiting):
    _, _, c, chips = _mesh_position()
    cps = []
    for a in range(len(srcs)):
        half = srcs[a].shape[1] // 2
        for k, (px, py) in enumerate(chips):
            for h in range(2):
                arrival = 2 * k + (h if waiting else c)
                cps.append(pltpu.make_async_remote_copy(
                    src_ref=srcs[a].at[2 * px + py, pl.ds(h * half, half)],
                    dst_ref=lands[a].at[arrival],
                    send_sem=send.at[a * SCATTER_COPIES + 2 * k + h],
                    recv_sem=recv.at[a * SCATTER_COPIES + arrival],
                    device_id=(px, py, h), device_id_type=MESH))
    return cps


def _gather_land_shapes(shards, split):
    return [jax.ShapeDtypeStruct(
        (N_CHIPS - 1, a.shape[0] // 2 if sp else a.shape[0]) + a.shape[1:], a.dtype)
        for a, sp in zip(shards, split)]


def _gather_finish(shards, lands, split, name):
    n = len(shards)
    ns = sum(split)
    d_index = {a: i for i, a in enumerate(a for a in range(n) if split[a])}

    def body(*refs):
        shard, land, outs = refs[:n], refs[n:2 * n], refs[2 * n:3 * n]
        obuf, fbuf = refs[3 * n:4 * n], refs[4 * n:5 * n]
        dbuf = refs[5 * n:5 * n + ns]
        ld_own, st_own, ld, st_mine, st_sib, send, recv = refs[5 * n + ns:]
        x, y, c, chips = _mesh_position()
        me = 2 * x + y
        own_loads, loads, sends, pending = [], {}, [], []
        for a in range(n):
            cp = pltpu.make_async_copy(shard[a], obuf[a], ld_own.at[a])
            cp.start()
            own_loads.append(cp)
        for a in range(n):
            for k in range(N_CHIPS - 1):
                cp = pltpu.make_async_copy(land[a].at[k], fbuf[a].at[k], ld.at[a, k])
                cp.start()
                loads[a, k] = cp
        for a in range(n):
            own_loads[a].wait()
            cp = pltpu.make_async_copy(obuf[a], outs[a].at[me], st_own.at[a])
            cp.start()
            pending.append(cp)
        for a in range(n):
            rows = shard[a].shape[0]
            for k, (px, py) in enumerate(chips):
                loads[a, k].wait()
                part = pl.ds(c * (rows // 2), rows // 2) if split[a] else pl.ds(0, rows)
                cp = pltpu.make_async_copy(fbuf[a].at[k], outs[a].at[2 * px + py, part],
                                           st_mine.at[a, k])
                cp.start()
                pending.append(cp)
                if split[a]:
                    fw = pltpu.make_async_remote_copy(
                        src_ref=fbuf[a].at[k], dst_ref=dbuf[d_index[a]].at[k],
                        send_sem=send.at[a, k], recv_sem=recv.at[a, k],
                        device_id=(x, y, 1 - c), device_id_type=MESH)
                    fw.start()
                    sends.append((a, k, fw))
        for a, k, fw in sends:
            px, py = chips[k]
            half = shard[a].shape[0] // 2
            fw.wait_recv()
            cp = pltpu.make_async_copy(dbuf[d_index[a]].at[k],
                                       outs[a].at[2 * px + py, pl.ds((1 - c) * half, half)],
                                       st_sib.at[a, k])
            cp.start()
            pending.append(cp)
        for _, _, fw in sends:
            fw.wait_send()
        for cp in pending:
            cp.wait()

    stage = [pltpu.VMEM(a.shape, a.dtype) for a in lands]
    dma = lambda *shape: pltpu.SemaphoreType.DMA(shape)
    return pl.pallas_call(
        body, name=name,
        in_specs=[ANY] * (2 * n), out_specs=[ANY] * n,
        out_shape=[jax.ShapeDtypeStruct((N_CHIPS,) + a.shape, a.dtype) for a in shards],
        scratch_shapes=[pltpu.VMEM(a.shape, a.dtype) for a in shards] + stage
        + [s for s, sp in zip(stage, split) if sp]
        + [dma(n), dma(n), dma(n, 3), dma(n, 3), dma(n, 3), dma(n, 3), dma(n, 3)],
        compiler_params=pltpu.CompilerParams(vmem_limit_bytes=VMEM_LIMIT_BYTES),
    )(*shards, *lands)


def _sum_chunk(rows):
    return next(r for r in range(SUM_CHUNK_ROWS, 0, -16) if rows % r == 0)


def _sum_and_share(partials, lands, name):
    n = len(partials)

    def body(*refs):
        own, landed, outs = refs[:n], refs[n:2 * n], refs[2 * n:3 * n]
        obuf, xbuf, ybuf, gbuf, sbuf, rbuf = (refs[(3 + k) * n:(4 + k) * n] for k in range(6))
        ld_own, ld_send, ld_got, st_own, st_sib, send_p, recv_p, send_s, recv_s = refs[9 * n:]
        x, y, c, _ = _mesh_position()
        me = 2 * x + y
        sibling = (x, y, 1 - c)

        def to_sibling(src, dst, send, recv, a):
            return pltpu.make_async_remote_copy(src_ref=src, dst_ref=dst, send_sem=send.at[a],
                                                recv_sem=recv.at[a], device_id=sibling,
                                                device_id_type=MESH)

        loads, firsts, seconds, stores = [], [], [], []
        for a in range(n):
            half = obuf[a].shape[0]
            cps = [pltpu.make_async_copy(own[a].at[me, pl.ds((1 - c) * half, half)], xbuf[a],
                                         ld_send.at[a]),
                   pltpu.make_async_copy(own[a].at[me, pl.ds(c * half, half)], obuf[a], ld_own.at[a]),
                   pltpu.make_async_copy(landed[a], gbuf[a], ld_got.at[a])]
            for cp in cps:
                cp.start()
            loads.append(cps)
        for a in range(n):
            loads[a][0].wait()
            rc = to_sibling(xbuf[a], ybuf[a], send_p, recv_p, a)
            rc.start()
            firsts.append(rc)
        for a in range(n):
            firsts[a].wait_recv()
            loads[a][1].wait()
            loads[a][2].wait()
            half = obuf[a].shape[0]
            rows = _sum_chunk(half)

            def add(k, carry, a=a, rows=rows):
                at = pl.ds(pl.multiple_of(k * rows, rows), rows)
                acc = obuf[a][at].astype(F32) + ybuf[a][at].astype(F32)
                for j in range(SCATTER_COPIES):
                    acc = acc + gbuf[a][j, at].astype(F32)
                sbuf[a][at] = acc
                return carry

            lax.fori_loop(0, half // rows, add, 0)
            rc = to_sibling(sbuf[a], rbuf[a], send_s, recv_s, a)
            rc.start()
            seconds.append(rc)
            cp = pltpu.make_async_copy(sbuf[a], outs[a].at[pl.ds(c * half, half)], st_own.at[a])
            cp.start()
            stores.append(cp)
        for a in range(n):
            half = obuf[a].shape[0]
            seconds[a].wait_recv()
            cp = pltpu.make_async_copy(rbuf[a], outs[a].at[pl.ds((1 - c) * half, half)], st_sib.at[a])
            cp.start()
            stores.append(cp)
        for rc in firsts + seconds:
            rc.wait_send()
        for cp in stores:
            cp.wait()

    halves = [(a.shape[1] // 2, a.shape[2]) for a in partials]
    return pl.pallas_call(
        body, name=name,
        in_specs=[ANY] * (2 * n), out_specs=[ANY] * n,
        out_shape=[jax.ShapeDtypeStruct((2 * h[0], h[1]), F32) for h in halves],
        scratch_shapes=[pltpu.VMEM(h, BF16) for h in halves] * 3
        + [pltpu.VMEM(g.shape, BF16) for g in lands]
        + [pltpu.VMEM(h, F32) for h in halves] * 2
        + [pltpu.SemaphoreType.DMA((n,))] * 9,
        compiler_params=pltpu.CompilerParams(vmem_limit_bytes=VMEM_LIMIT_BYTES),
    )(*partials, *lands)


def _gather_small(part):
    def body(in_ref, out_ref, send, recv, local):
        x, y, c, _ = _mesh_position()
        me = 4 * x + 2 * y + c
        cps = [pltpu.make_async_copy(in_ref, out_ref.at[me], local)]
        k = 0
        for fx in range(2):
            for fy in range(2):
                for fc in range(2):
                    if fx or fy or fc:
                        cps.append(pltpu.make_async_remote_copy(
                            src_ref=in_ref, dst_ref=out_ref.at[me], send_sem=send.at[k],
                            recv_sem=recv.at[k], device_id=(x ^ fx, y ^ fy, c ^ fc),
                            device_id_type=MESH))
                        k += 1
        for cp in cps:
            cp.start()
        for cp in cps:
            cp.wait()

    return pl.pallas_call(
        body, name="gather_small",
        in_specs=[pl.BlockSpec(memory_space=pltpu.VMEM)],
        out_specs=pl.BlockSpec(memory_space=pltpu.VMEM),
        out_shape=jax.ShapeDtypeStruct((N_DEV,) + part.shape, part.dtype),
        scratch_shapes=[pltpu.SemaphoreType.DMA((N_DEV - 1,)), pltpu.SemaphoreType.DMA((N_DEV - 1,)),
                        pltpu.SemaphoreType.DMA],
    )(part)


def _scatter_start(grads, after, tag):
    lands = [jax.ShapeDtypeStruct((SCATTER_COPIES, g.shape[1] // 2, g.shape[2]), g.dtype)
             for g in grads]
    return _ici_start(grads, lands, _scatter_copies, after, "scatter_start_" + tag,
                      per_array=SCATTER_COPIES)


def _scatter_finish(handles, after, tag):
    grads, lands = [], []
    for k, handle in enumerate(handles):
        g, l = _ici_wait(handle, _scatter_copies, after, "scatter_wait_%s_%d" % (tag, k))
        grads += g
        lands += l
    return _sum_and_share(grads, lands, "sum_and_share_" + tag)


def _pad_rows(a, rows):
    return jnp.pad(a, ((0, rows - a.shape[0]), (0, 0)))


def kernel(x, norm_mix_0, w_in_0, b_f_0, conv_w_0, w_out_0, norm_ffn_0, w_up_0, w_down_0, norm_mix_1, pool_w_1, pool_scale_1, norm_ffn_1, w_up_1, w_down_1, final_norm, loss_target, m_norm_mix_0, m_w_in_0, m_b_f_0, m_conv_w_0, m_w_out_0, m_norm_ffn_0, m_w_up_0, m_w_down_0, m_norm_mix_1, m_pool_w_1, m_pool_scale_1, m_norm_ffn_1, m_w_up_1, m_w_down_1, m_final_norm, v_norm_mix_0, v_w_in_0, v_b_f_0, v_conv_w_0, v_w_out_0, v_norm_ffn_0, v_w_up_0, v_w_down_0, v_norm_mix_1, v_pool_w_1, v_pool_scale_1, v_norm_ffn_1, v_w_up_1, v_w_down_1, v_final_norm):
    d = x.shape[-1]
    a = N_HEADS * HEAD_DIM
    c_conv = conv_w_0.shape[1] * N_CHIPS
    xs = x[0]
    target = loss_target[0]
    row = lambda vec: vec.reshape(1, -1)

    big = [w_in_0, w_out_0, w_up_0, w_down_0, pool_w_1, w_up_1, w_down_1]
    first = [w_in_0.astype(BF16)]
    first_split = [True]
    copies_a = functools.partial(_chip_copies, _gather_views(first_split))
    copies_b = functools.partial(_chip_copies, _gather_whole_views)
    start_a = _ici_start(first, _gather_land_shapes(first, first_split), copies_a, b_f_0,
                         "gather_start_a")
    zero = start_a[-1][0, 0]
    rest = [(w + zero).astype(BF16)
            for w in (w_out_0, w_up_0, w_down_0, pool_w_1, w_up_1, w_down_1)]
    rest = rest + [conv_w_0]
    start_b = _ici_start(rest, [jax.ShapeDtypeStruct((N_CHIPS,) + w.shape, w.dtype) for w in rest],
                         copies_b, start_a[-1], "gather_start_b")
    n0 = _rms_pre(start_b[-1], xs, row(norm_mix_0))
    first, land_a = _ici_wait(start_a, copies_a, n0, "gather_wait_a")
    (g_in,) = _gather_finish(first, land_a, first_split, "gather_finish_a")
    w_in = g_in.transpose(1, 0, 2).reshape(d, -1)
    w_qkv = w_in[:, :3 * a]
    w_f = jnp.pad(w_in[:, 3 * a:3 * a + N_HEADS], ((0, 0), (0, 128 - N_HEADS)))
    w_bcx = w_in[:, 3 * a + N_HEADS:]
    bf = jnp.pad(b_f_0, (0, 128 - N_HEADS)).reshape(1, 128)

    qkv, fl, bcx = _in_proj(n0, w_qkv, w_f, w_bcx)
    qa, ka = _gate_prep(fl, bf, qkv)
    o, lse = _attn_fwd(qa, ka, qkv)
    rest, land_b = _ici_wait(start_b, copies_b, o, "gather_wait_b")
    own_slot = 2 * lax.axis_index("x") + lax.axis_index("y")
    g_out, g_up0, g_down0, g_pool, g_up1, g_down1, g_conv = [
        lax.dynamic_update_index_in_dim(land, shard, own_slot, 0)
        for land, shard in zip(land_b, rest)]
    w_out = g_out.reshape(-1, d)
    conv_w = _pad_rows(g_conv.transpose(1, 0, 2).reshape(conv_w_0.shape[0], c_conv), 8)
    h1 = _conv_out(o, bcx, conv_w, w_out, xs)
    w_down0 = g_down0.reshape(-1, d)
    w_down1 = g_down1.reshape(-1, d)
    pool_w = g_pool.transpose(1, 0, 2, 3).reshape(pool_w_1.shape[0], -1, pool_w_1.shape[2])
    h2, a0, nf0 = _mlp_fwd(h1, row(norm_ffn_0), g_up0, w_down0, "mlp_fwd_0")
    h3 = _pool_fwd(h2, row(norm_mix_1), pool_w, row(pool_scale_1))
    dh4, a1, nf1, loss_part, d_final = _mlp_fwd(h3, row(norm_ffn_1), g_up1, w_down1, "mlp_fwd_1",
                                                head=(row(final_norm), target))

    slot_cols = g_up0.shape[2]
    pool_cols = pool_w.shape[2]
    da1, dz1, dh3, d_nffn1 = _mlp_bwd_x(dh4, a1, g_up1, w_down1, h3, row(norm_ffn_1), "mlp_bwd_x_1")
    dw_up1, dw_down1 = _mlp_bwd_w(nf1, da1, a1, dz1, slot_cols, "mlp_bwd_w_1")
    scatter_1 = _scatter_start([dw_up1, dw_down1.reshape(N_CHIPS, -1, d)], bf, "mlp1")
    dh2, dw_pool, d_pscale, d_nmix1 = _pool_bwd(scatter_1[-1], dh3, h2, row(norm_mix_1), pool_w,
                                                row(pool_scale_1))
    da0, dz0, dh1, d_nffn0 = _mlp_bwd_x(dh2, a0, g_up0, w_down0, h1, row(norm_ffn_0), "mlp_bwd_x_0")
    dw_up0, dw_down0 = _mlp_bwd_w(nf0, da0, a0, dz0, slot_cols, "mlp_bwd_w_0")
    dw_pool = (dw_pool.reshape(pool_w.shape[0], N_CHIPS, -1, pool_cols).transpose(1, 0, 2, 3)
               .reshape(N_CHIPS, -1, pool_cols))
    scatter_0 = _scatter_start([dw_up0, dw_down0.reshape(N_CHIPS, -1, d), dw_pool], bf, "mlp0")
    do, delta, dbcx, dw_out, d_conv = _conv_out_bwd(scatter_0[-1], dh1, w_out, o, bcx, conv_w)
    scatter_o = _scatter_start([dw_out.reshape(N_CHIPS, -1, d)], bf, "out")
    dqa, dka, dv = _attn_bwd(scatter_o[-1], qa, ka, qkv, do, lse, delta)
    dqkv, dfl, d_bf = _gate_bwd(dqa, dka, dv, fl, bf)
    dw_qkv, dw_f, dw_bcx = _wgrad_in(n0, [dqkv, dfl, dbcx])
    dw_in = jnp.concatenate([dw_qkv, dw_f[:N_HEADS], dw_bcx], axis=0).reshape(N_CHIPS, -1, d)
    slot_rows = -(-dw_in.shape[1] // 32) * 32
    dw_in = jnp.pad(dw_in, ((0, 0), (0, slot_rows - dw_in.shape[1]), (0, 0)))
    scatter_m = _scatter_start([dw_in], bf, "mixer")
    grad_x, d_nmix0 = _in_proj_bwd(scatter_m[-1], dqkv, dfl, dbcx, w_qkv, w_f, w_bcx, xs,
                                   row(norm_mix_0), dh1)

    r_up1, r_down1, r_out = _scatter_finish([scatter_1, scatter_o], grad_x, "early")
    r_up0, r_down0, r_pool, r_in = _scatter_finish([scatter_0, scatter_m], grad_x, "late")
    reduced = [r_in, r_out, r_up0, r_down0, r_pool, r_up1, r_down1]
    moments = [(m_w_in_0, v_w_in_0), (m_w_out_0, v_w_out_0), (m_w_up_0, v_w_up_0),
               (m_w_down_0, v_w_down_0), (m_pool_w_1, v_pool_w_1), (m_w_up_1, v_w_up_1),
               (m_w_down_1, v_w_down_1)]
    big_out = []
    for k, (w, g, (m, v)) in enumerate(zip(big, reduced, moments)):
        if w.shape[-1] % 128:
            view = lambda t: t.reshape(-1, t.shape[-1]).T
            back = lambda t: t.T.reshape(w.shape)
            g_view = g[:w.shape[-1]]
        else:
            view = lambda t: t.reshape(-1, t.shape[-1])
            back = lambda t: t.reshape(w.shape)
            g_view = view(g)
        delta_w, new_m, new_v = _adamw(view(w), g_view, view(m), view(v), "adamw_%d" % k)
        big_out.append((back(g_view), back(delta_w), back(new_m), back(new_v)))

    tail = jnp.concatenate([d_conv[0:3].reshape(-1)[d:], d_bf[0, :N_HEADS], loss_part[0, :1]])
    small_part = jnp.concatenate(
        [d_nmix0, d_nffn0, d_nmix1, d_pscale, d_nffn1, d_final,
         d_conv[0:3].reshape(1, -1)[:, :d],
         jnp.pad(tail, (0, d - tail.shape[0])).reshape(1, d)], axis=0)
    parts = _gather_small(small_part)

    chip = 2 * lax.axis_index("x") + lax.axis_index("y")
    cw_cols = conv_w_0.shape[1]

    def conv_block(full):
        mine = lax.dynamic_slice_in_dim(full, chip * cw_cols, cw_cols, axis=1)
        return jnp.pad(mine.reshape(-1), (0, d - mine.size))

    def small_rows(vals, cw, bfv):
        return jnp.stack(list(vals) + [cw, jnp.pad(bfv, (0, d - N_HEADS))])

    smalls_w = [norm_mix_0, norm_ffn_0, norm_mix_1, pool_scale_1, norm_ffn_1, final_norm]
    smalls_m = [m_norm_mix_0, m_norm_ffn_0, m_norm_mix_1, m_pool_scale_1, m_norm_ffn_1, m_final_norm]
    smalls_v = [v_norm_mix_0, v_norm_ffn_0, v_norm_mix_1, v_pool_scale_1, v_norm_ffn_1, v_final_norm]
    pad_cw = lambda t: jnp.pad(t.reshape(-1), (0, d - t.size))
    w_rows = small_rows(smalls_w, pad_cw(conv_w_0), b_f_0)
    m_rows = small_rows(smalls_m, pad_cw(m_conv_w_0), m_b_f_0)
    v_rows = small_rows(smalls_v, pad_cw(v_conv_w_0), v_b_f_0)

    g_sum = _sum_devices(parts)
    conv_full = jnp.concatenate([g_sum[6], g_sum[7, :3 * c_conv - d]]).reshape(3, c_conv)
    bf_grad = g_sum[7, 3 * c_conv - d:3 * c_conv - d + N_HEADS]
    loss = g_sum[7, 3 * c_conv - d + N_HEADS]
    g_rows = jnp.concatenate(
        [g_sum[0:6], conv_block(conv_full).reshape(1, d),
         jnp.pad(bf_grad, (0, d - N_HEADS)).reshape(1, d)], axis=0)
    d_rows, nm_rows, nv_rows = _adamw(w_rows, g_rows, m_rows, v_rows, "adamw_small")

    def unpack(rows):
        cw = rows[6, :conv_w_0.size].reshape(conv_w_0.shape)
        return [rows[0], rows[1], rows[2], rows[3], rows[4], rows[5], cw, rows[7, :N_HEADS]]

    def assemble(kind):
        sm = unpack([g_rows, d_rows, nm_rows, nv_rows][kind])
        lg = [t[kind] for t in big_out]
        return [sm[0], lg[0], sm[7], sm[6], lg[1], sm[1], lg[2], lg[3],
                sm[2], lg[4], sm[3], sm[4], lg[5], lg[6], sm[5]]

    return (loss, grad_x[None], *assemble(0), *assemble(1), *assemble(2), *assemble(3))
```

```python
import functools

import jax
import jax.numpy as jnp
from jax import lax
from jax.experimental import pallas as pl
from jax.experimental.pallas import tpu as pltpu

F32 = jnp.float32
BF16 = jnp.bfloat16

RMS_EPS = 1e-6
HEAD_DIM = 64
N_HEADS = 8
ATTN_SCALE = HEAD_DIM ** -0.5
LOG2_E = 1.4426950408889634
POOL_WINDOWS = (2, 4, 8, 16)
POOL_HALO = 16
CONV_HALO = 8
NEG_BIG = -1e30

ADAM_LR = 0.001
ADAM_B1 = 0.9
ADAM_B2 = 0.999
ADAM_EPS = 1e-08
ADAM_WD = 0.01
ADAM_STEP = 10

N_CHIPS = 4
N_DEV = 8
MESH = pl.DeviceIdType.MESH

VMEM_LIMIT_BYTES = 56 * 1024 * 1024

TILE_ROWS = 512
TILE_PROJ_ROWS = 1024
TILE_ATTN = 512
TILE_MLP_ROWS = 1024
TILE_MLP_FF = 1024
TILE_MLP_BWD_FF = 512
TILE_HEAD_ROWS = 256
TILE_WGRAD_K = 1024
TILE_WGRAD_N = 1024
TILE_ELEM_ROWS = 256
SUM_CHUNK_ROWS = 128

LANE_CQ = 64
LANE_CK = 88


def _params(semantics):
    return pltpu.CompilerParams(dimension_semantics=semantics,
                                vmem_limit_bytes=VMEM_LIMIT_BYTES)


def _nn(a, b):
    return lax.dot_general(a, b, (((1,), (0,)), ((), ())), preferred_element_type=F32)


def _nt(a, b):
    return lax.dot_general(a, b, (((1,), (1,)), ((), ())), preferred_element_type=F32)


def _tn(a, b):
    return lax.dot_general(a, b, (((0,), (0,)), ((), ())), preferred_element_type=F32)


def _split3(v):
    hi = v.astype(BF16)
    r1 = v - hi.astype(F32)
    mid = r1.astype(BF16)
    lo = (r1 - mid.astype(F32)).astype(BF16)
    return hi, mid, lo


def _exact_nn(sel, v):
    hi, mid, lo = _split3(v)
    return _nn(sel, hi) + _nn(sel, mid) + _nn(sel, lo)


def _exact_nt(sel, v):
    hi, mid, lo = _split3(v)
    return _nt(sel, hi) + _nt(sel, mid) + _nt(sel, lo)


def _rms_fwd(x, g):
    r = lax.rsqrt(jnp.mean(x * x, axis=-1, keepdims=True) + RMS_EPS)
    return x * r * g, r


def _rms_bwd(dn, x, g):
    r = lax.rsqrt(jnp.mean(x * x, axis=-1, keepdims=True) + RMS_EPS)
    xh = x * r
    gy = dn * g
    dx = r * (gy - xh * jnp.mean(gy * xh, axis=-1, keepdims=True))
    return dx, jnp.sum(dn * xh, axis=0, keepdims=True)


def _lane(shape):
    return lax.broadcasted_iota(jnp.int32, shape, len(shape) - 1)


def _row(shape):
    return lax.broadcasted_iota(jnp.int32, shape, len(shape) - 2)


def _full(a):
    nd = a.ndim
    return pl.BlockSpec(a.shape, lambda *_: (0,) * nd)


def _rms_pre(after, x, g):
    s, d = x.shape
    tm = min(TILE_ROWS, s)

    def body(after_ref, x_ref, g_ref, n_ref):
        n, _ = _rms_fwd(x_ref[...], g_ref[...])
        n_ref[...] = n.astype(BF16)

    rows = pl.BlockSpec((tm, d), lambda i: (i, 0))
    return pl.pallas_call(
        body, name="rms_pre", grid=(s // tm,),
        in_specs=[ANY, rows, _full(g)], out_specs=rows,
        out_shape=jax.ShapeDtypeStruct((s, d), BF16),
        compiler_params=_params(("parallel",)),
    )(after, x, g)


def _in_proj(n, w_qkv, w_f, w_bcx):
    s, d = n.shape
    tm = min(TILE_PROJ_ROWS, s)

    def body(n_ref, wq_ref, wf_ref, wb_ref, qkv_ref, fl_ref, bcx_ref):
        nb = n_ref[...]
        qkv_ref[...] = _nn(nb, wq_ref[...]).astype(BF16)
        fl_ref[...] = _nn(nb, wf_ref[...])
        bcx_ref[...] = _nn(nb, wb_ref[...])

    rows = lambda c: pl.BlockSpec((tm, c), lambda i: (i, 0))
    return pl.pallas_call(
        body, name="in_proj", grid=(s // tm,),
        in_specs=[rows(d), _full(w_qkv), _full(w_f), _full(w_bcx)],
        out_specs=[rows(w_qkv.shape[1]), rows(w_f.shape[1]), rows(w_bcx.shape[1])],
        out_shape=[jax.ShapeDtypeStruct((s, w_qkv.shape[1]), BF16),
                   jax.ShapeDtypeStruct((s, w_f.shape[1]), F32),
                   jax.ShapeDtypeStruct((s, w_bcx.shape[1]), F32)],
        compiler_params=_params(("parallel",)),
    )(n, w_qkv, w_f, w_bcx)


def _gate_prep(fl, bf, qkv):
    s = fl.shape[0]
    a = N_HEADS * HEAD_DIM
    tm = min(TILE_ROWS, s)

    def body(fl_ref, bf_ref, q_ref, k_ref, qa_ref, ka_ref, carry_ref):
        i = pl.program_id(0)

        @pl.when(i == 0)
        def _():
            carry_ref[...] = jnp.zeros_like(carry_ref)

        z = fl_ref[...] + bf_ref[...]
        logf = jnp.minimum(z, 0.0) - jnp.log(1.0 + jnp.exp(-jnp.abs(z)))
        lower = (_lane((tm, tm)) <= _row((tm, tm))).astype(BF16)
        cum = _exact_nn(lower, logf) + carry_ref[0:1, :]
        carry_ref[0:1, :] = cum[tm - 1:tm, :]

        lane = _lane((tm, 128))
        pieces = [p.astype(F32)
                  for p in _split3(jnp.where(lane < N_HEADS, LOG2_E * cum, 0.0))]
        shared_q = sum(pltpu.roll(p, LANE_CQ + N_HEADS * k, axis=1) for k, p in enumerate(pieces))
        shared_k = -sum(pltpu.roll(p, LANE_CK + N_HEADS * k, axis=1) for k, p in enumerate(pieces))
        for h in range(N_HEADS):
            at_q = functools.reduce(jnp.logical_or,
                                    [lane == LANE_CQ + N_HEADS * k + h for k in range(3)])
            at_k = functools.reduce(jnp.logical_or,
                                    [lane == LANE_CK + N_HEADS * k + h for k in range(3)])
            pair = slice((h // 2) * 128, (h // 2 + 1) * 128)
            qp = q_ref[:, pair].astype(F32)
            kp = k_ref[:, pair].astype(F32)
            if h % 2:
                qp = pltpu.roll(qp, HEAD_DIM, axis=1)
                kp = pltpu.roll(kp, HEAD_DIM, axis=1)
            q_bias = jnp.where(at_k, 1.0, shared_q)
            k_bias = jnp.where(at_q, 1.0, shared_k)
            qa_ref[h] = jnp.where(lane < HEAD_DIM, qp * (ATTN_SCALE * LOG2_E), q_bias).astype(BF16)
            ka_ref[h] = jnp.where(lane < HEAD_DIM, kp, k_bias).astype(BF16)

    aug = jax.ShapeDtypeStruct((N_HEADS, s, 128), BF16)
    aug_spec = pl.BlockSpec((N_HEADS, tm, 128), lambda i: (0, i, 0))
    return pl.pallas_call(
        body, name="gate_prep", grid=(s // tm,),
        in_specs=[pl.BlockSpec((tm, 128), lambda i: (i, 0)), _full(bf),
                  pl.BlockSpec((tm, a), lambda i: (i, 0)),
                  pl.BlockSpec((tm, a), lambda i: (i, 1))],
        out_specs=[aug_spec, aug_spec],
        out_shape=[aug, aug],
        scratch_shapes=[pltpu.VMEM((8, 128), F32)],
        compiler_params=_params(("arbitrary",)),
    )(fl, bf, qkv, qkv)


def _attn_fwd(qa, ka, qkv):
    s = qa.shape[1]
    a = N_HEADS * HEAD_DIM
    t = min(TILE_ATTN, s)
    n_pairs = N_HEADS // 2
    v_block0 = 2 * a // 128
    ones_lane = (HEAD_DIM, 0)

    def body(qa_ref, ka_ref, v_ref, o_ref, lse_ref, m_ref, acc_ref, s_even, s_odd):
        i = pl.program_id(1)
        m_ref[...] = jnp.full_like(m_ref, NEG_BIG)
        acc_ref[...] = jnp.zeros_like(acc_ref)
        upper_rows = _row((128, t)) < HEAD_DIM

        def keys(j):
            return pl.ds(pl.multiple_of(j * t, t), t)

        def scores_into(buf, j):
            for e in range(2):
                buf[e] = _nt(ka_ref[e, keys(j), :], qa_ref[e])

        def consume(buf, j, masked):
            vf = v_ref[keys(j), :].astype(F32)
            lane = _lane((t, 128))
            own = [lane < HEAD_DIM, lane >= HEAD_DIM]
            for e in range(2):
                v_head = jnp.where(own[e], vf, jnp.where(lane == ones_lane[e], 1.0, 0.0)).astype(BF16)
                sc = buf[e]
                if masked:
                    sc = jnp.where(_row((t, t)) <= _lane((t, t)), sc, NEG_BIG)
                m_prev = m_ref[e]
                m_new = jnp.maximum(m_prev, jnp.max(sc, axis=0, keepdims=True))
                p = jnp.exp2(sc - m_new).astype(BF16)
                acc_ref[e] = acc_ref[e] * jnp.exp2(m_prev - m_new) + _tn(v_head, p)
                m_ref[e] = m_new

        scores_into(s_even, 0)

        def two_tiles(p, carry):
            j = 2 * p
            scores_into(s_odd, j + 1)
            consume(s_even, j, False)
            scores_into(s_even, j + 2)
            consume(s_odd, j + 1, False)
            return carry

        lax.fori_loop(0, i // 2, two_tiles, 0)

        @pl.when(i % 2 == 0)
        def _():
            consume(s_even, i, True)

        @pl.when(i % 2 == 1)
        def _():
            scores_into(s_odd, i)
            consume(s_even, i - 1, False)
            consume(s_odd, i, True)

        denom = [acc_ref[e, ones_lane[e]:ones_lane[e] + 1, :] for e in range(2)]
        out_t = jnp.where(upper_rows, acc_ref[0] / denom[0], acc_ref[1] / denom[1])
        o_ref[...] = out_t.T.astype(BF16)
        lse = [m_ref[e] + LOG2_E * jnp.log(denom[e]) for e in range(2)]
        lse_ref[...] = jnp.where(_row((8, t)) == 0, lse[0], lse[1])

    return pl.pallas_call(
        body, name="attn_fwd", grid=(n_pairs, s // t),
        in_specs=[pl.BlockSpec((2, t, 128), lambda g, i: (g, i, 0)),
                  pl.BlockSpec((2, s, 128), lambda g, i: (g, 0, 0)),
                  pl.BlockSpec((s, 128), lambda g, i: (0, v_block0 + g))],
        out_specs=[pl.BlockSpec((t, 128), lambda g, i: (i, g)),
                   pl.BlockSpec((None, 8, t), lambda g, i: (g, 0, i))],
        out_shape=[jax.ShapeDtypeStruct((s, a), BF16),
                   jax.ShapeDtypeStruct((n_pairs, 8, s), F32)],
        scratch_shapes=[pltpu.VMEM((2, 1, t), F32), pltpu.VMEM((2, 128, t), F32),
                        pltpu.VMEM((2, t, t), F32), pltpu.VMEM((2, t, t), F32)],
        compiler_params=_params(("parallel", "arbitrary")),
    )(qa, ka, qkv)


def _conv_out(o, bcx, cw, w_out, x):
    s, d = x.shape
    c = o.shape[1]
    tm = min(TILE_PROJ_ROWS, s)

    def body(o_ref, b_ref, c_ref, xin_ref, cw_ref, w_ref, x_ref, h_ref, ubuf):
        i = pl.program_id(0)

        @pl.when(i == 0)
        def _():
            ubuf[0:CONV_HALO, :] = jnp.zeros((CONV_HALO, c), F32)

        u = c_ref[...] * xin_ref[...]
        ubuf[CONV_HALO:CONV_HALO + tm, :] = u
        u1 = ubuf[CONV_HALO - 1:CONV_HALO - 1 + tm, :]
        u2 = ubuf[CONV_HALO - 2:CONV_HALO - 2 + tm, :]
        cv = (cw_ref[0:1, :] * u2 + cw_ref[1:2, :] * u1) + cw_ref[2:3, :] * u
        y = (b_ref[...] * cv).astype(BF16)
        mix = _nn(o_ref[...], w_ref[0:c, :]) + _nn(y, w_ref[c:2 * c, :])
        h_ref[...] = x_ref[...] + mix
        ubuf[0:CONV_HALO, :] = u[tm - CONV_HALO:tm, :]

    col = lambda k: pl.BlockSpec((tm, c), lambda i: (i, k))
    return pl.pallas_call(
        body, name="conv_out", grid=(s // tm,),
        in_specs=[col(0), col(0), col(1), col(2), _full(cw), _full(w_out),
                  pl.BlockSpec((tm, d), lambda i: (i, 0))],
        out_specs=pl.BlockSpec((tm, d), lambda i: (i, 0)),
        out_shape=jax.ShapeDtypeStruct((s, d), F32),
        scratch_shapes=[pltpu.VMEM((tm + CONV_HALO, c), F32)],
        compiler_params=_params(("arbitrary",)),
    )(o, bcx, bcx, bcx, cw, w_out, x)


def _mlp_fwd(h, g, w_up, w_down, name, head=None):
    s, d = h.shape
    ff = w_down.shape[0]
    slot_cols = w_up.shape[2]
    tm = min(TILE_MLP_ROWS, s)
    tf = min(TILE_MLP_FF, slot_cols)
    per_slot = slot_cols // tf
    nf = ff // tf
    n_head = 0 if head is None else 2
    chunk = min(TILE_HEAD_ROWS, tm)

    def body(*refs):
        h_ref, g_ref, wu_ref, wd_ref = refs[:4]
        out_ref, a_ref, n_ref = refs[4 + n_head:7 + n_head]
        nb_ref, acc_ref = refs[9 + n_head:11 + n_head] if head else refs[-2:]
        i = pl.program_id(0)
        f = pl.program_id(1)

        def target_copy():
            t_hbm, t_buf, t_sem = refs[5], refs[-2], refs[-1]
            return pltpu.make_async_copy(t_hbm.at[pl.ds(pl.multiple_of(i * tm, tm), tm), :],
                                         t_buf, t_sem)

        @pl.when(f == 0)
        def _():
            n, _ = _rms_fwd(h_ref[...], g_ref[...])
            nb = n.astype(BF16)
            nb_ref[...] = nb
            n_ref[...] = nb
            acc_ref[...] = jnp.zeros_like(acc_ref)
            if head is not None:
                target_copy().start()

        pre = _nn(nb_ref[...], wu_ref[...])
        a_ref[...] = pre.astype(BF16)
        r = jnp.square(jnp.maximum(pre, 0.0)).astype(BF16)
        acc_ref[...] += _nn(r, wd_ref[...])

        @pl.when(f == nf - 1)
        def _():
            if head is None:
                out_ref[...] = h_ref[...] + acc_ref[...]
            else:
                gf_ref, t_buf = refs[4], refs[-2]
                loss_ref, dg_ref = refs[7 + n_head:9 + n_head]
                target_copy().wait()
                part, dg = None, None
                for r0 in range(0, tm, chunk):
                    rows_ = slice(r0, r0 + chunk)
                    out = h_ref[rows_, :] + acc_ref[rows_, :]
                    y, _ = _rms_fwd(out, gf_ref[...])
                    err = y - t_buf[rows_, :]
                    p = 0.5 * jnp.sum(jnp.mean(err * err, axis=-1, keepdims=True), axis=0,
                                      keepdims=True)
                    dx, dgp = _rms_bwd(err / d, out, gf_ref[...])
                    out_ref[rows_, :] = dx
                    part = p if part is None else part + p
                    dg = dgp if dg is None else dg + dgp
                part = jnp.broadcast_to(part, loss_ref.shape)

                @pl.when(i == 0)
                def _():
                    loss_ref[...] = part
                    dg_ref[...] = dg

                @pl.when(i > 0)
                def _():
                    loss_ref[...] += part
                    dg_ref[...] += dg

    rows = pl.BlockSpec((tm, d), lambda i, f: (i, 0))
    in_specs = [rows, _full(g),
                pl.BlockSpec((None, d, tf), lambda i, f: (f // per_slot, 0, f % per_slot)),
                pl.BlockSpec((tf, d), lambda i, f: (f, 0))]
    out_specs = [rows, pl.BlockSpec((tm, tf), lambda i, f: (i, f)), rows]
    out_shape = [jax.ShapeDtypeStruct((s, d), F32), jax.ShapeDtypeStruct((s, ff), BF16),
                 jax.ShapeDtypeStruct((s, d), BF16)]
    args = [h, g, w_up, w_down]
    scratch = [pltpu.VMEM((tm, d), BF16), pltpu.VMEM((tm, d), F32)]
    if head is not None:
        in_specs += [_full(head[0]), ANY]
        args += list(head)
        out_specs += [pl.BlockSpec((1, 128), lambda i, f: (0, 0)),
                      pl.BlockSpec((1, d), lambda i, f: (0, 0))]
        out_shape += [jax.ShapeDtypeStruct((1, 128), F32), jax.ShapeDtypeStruct((1, d), F32)]
        scratch += [pltpu.VMEM((tm, d), F32), pltpu.SemaphoreType.DMA]
    return pl.pallas_call(
        body, name=name, grid=(s // tm, nf),
        in_specs=in_specs, out_specs=out_specs, out_shape=out_shape, scratch_shapes=scratch,
        compiler_params=_params(("parallel" if head is None else "arbitrary", "arbitrary")),
    )(*args)


def _window_sum_down(e, window):
    step = 1
    while step < window:
        e = e + pltpu.roll(e, step, axis=0)
        step *= 2
    return e


def _window_sum_up(e, window):
    n = e.shape[0]
    step = 1
    while step < window:
        e = e + pltpu.roll(e, n - step, axis=0)
        step *= 2
    return e


def _pool_counts(first_row, tm, window):
    t = first_row + _row((tm, 1))
    return jnp.minimum(t + 1, window).astype(F32)


def _pool_fwd(h, g, pw, ps):
    s, d = h.shape
    cg = d // len(POOL_WINDOWS)
    tm = min(TILE_PROJ_ROWS, s)

    def body(h_ref, g_ref, pw_ref, ps_ref, out_ref, nbuf):
        i = pl.program_id(0)

        @pl.when(i == 0)
        def _():
            nbuf[0:POOL_HALO, :] = jnp.zeros((POOL_HALO, d), F32)

        n, _ = _rms_fwd(h_ref[...], g_ref[...])
        nbuf[POOL_HALO:POOL_HALO + tm, :] = n
        for k, window in enumerate(POOL_WINDOWS):
            cols = slice(k * cg, (k + 1) * cg)
            sums = _window_sum_down(nbuf[:, cols], window)[POOL_HALO:, :]
            pooled = sums / _pool_counts(i * tm, tm, window) - n[:, cols]
            y = _nn(pooled.astype(BF16), pw_ref[k]) * ps_ref[:, cols]
            out_ref[:, cols] = h_ref[:, cols] + y
        nbuf[0:POOL_HALO, :] = n[tm - POOL_HALO:tm, :]

    return pl.pallas_call(
        body, name="pool_fwd", grid=(s // tm,),
        in_specs=[pl.BlockSpec((tm, d), lambda i: (i, 0)), _full(g), _full(pw), _full(ps)],
        out_specs=pl.BlockSpec((tm, d), lambda i: (i, 0)),
        out_shape=jax.ShapeDtypeStruct((s, d), F32),
        scratch_shapes=[pltpu.VMEM((tm + POOL_HALO, d), F32)],
        compiler_params=_params(("arbitrary",)),
    )(h, g, pw, ps)


def _mlp_bwd_x(dz, a, w_up, w_down, h_in, g, name):
    s, d = dz.shape
    ff = w_down.shape[0]
    slot_cols = w_up.shape[2]
    tm = min(TILE_MLP_ROWS, s)
    tf = min(TILE_MLP_BWD_FF, slot_cols)
    per_slot = slot_cols // tf
    nf = ff // tf

    def body(dz_ref, a_ref, wu_ref, wd_ref, h_ref, g_ref, da_ref, dzb_ref, dh_ref, dg_ref,
             dzs_ref, acc_ref):
        i = pl.program_id(0)
        f = pl.program_id(1)

        @pl.when(f == 0)
        def _():
            dzb = dz_ref[...].astype(BF16)
            dzs_ref[...] = dzb
            dzb_ref[...] = dzb
            acc_ref[...] = jnp.zeros_like(acc_ref)

        dr = _nt(dzs_ref[...], wd_ref[...])
        da = (dr * (2.0 * jnp.maximum(a_ref[...].astype(F32), 0.0))).astype(BF16)
        da_ref[...] = da
        acc_ref[...] += _nt(da, wu_ref[...])

        @pl.when(f == nf - 1)
        def _():
            dx, dg = _rms_bwd(acc_ref[...], h_ref[...], g_ref[...])
            dh_ref[...] = dz_ref[...] + dx

            @pl.when(i == 0)
            def _():
                dg_ref[...] = dg

            @pl.when(i > 0)
            def _():
                dg_ref[...] += dg

    return pl.pallas_call(
        body, name=name, grid=(s // tm, nf),
        in_specs=[pl.BlockSpec((tm, d), lambda i, f: (i, 0)),
                  pl.BlockSpec((tm, tf), lambda i, f: (i, f)),
                  pl.BlockSpec((None, d, tf), lambda i, f: (f // per_slot, 0, f % per_slot)),
                  pl.BlockSpec((tf, d), lambda i, f: (f, 0)),
                  pl.BlockSpec((tm, d), lambda i, f: (i, 0)), _full(g)],
        out_specs=[pl.BlockSpec((tm, tf), lambda i, f: (i, f)),
                   pl.BlockSpec((tm, d), lambda i, f: (i, 0)),
                   pl.BlockSpec((tm, d), lambda i, f: (i, 0)),
                   pl.BlockSpec((1, d), lambda i, f: (0, 0))],
        out_shape=[jax.ShapeDtypeStruct((s, ff), BF16),
                   jax.ShapeDtypeStruct((s, d), BF16),
                   jax.ShapeDtypeStruct((s, d), F32),
                   jax.ShapeDtypeStruct((1, d), F32)],
        scratch_shapes=[pltpu.VMEM((tm, d), BF16), pltpu.VMEM((tm, d), F32)],
        compiler_params=_params(("arbitrary", "arbitrary")),
    )(dz, a, w_up, w_down, h_in, g)


def _mlp_bwd_w(n, da, a, dzb, slot_cols, name):
    s, d = n.shape
    ff = a.shape[1]
    tn = min(TILE_WGRAD_N, slot_cols)
    tk = min(TILE_WGRAD_K, s)
    per_slot = slot_cols // tn
    nk = s // tk

    def body(n_ref, da_ref, a_ref, dz_ref, du_ref, dd_ref, accu_ref, accd_ref):
        k = pl.program_id(1)

        @pl.when(k == 0)
        def _():
            accu_ref[...] = jnp.zeros_like(accu_ref)
            accd_ref[...] = jnp.zeros_like(accd_ref)

        accu_ref[...] += _tn(n_ref[...], da_ref[...])
        r = jnp.square(jnp.maximum(a_ref[...].astype(F32), 0.0)).astype(BF16)
        accd_ref[...] += _tn(r, dz_ref[...])

        @pl.when(k == nk - 1)
        def _():
            du_ref[...] = accu_ref[...].astype(BF16)
            dd_ref[...] = accd_ref[...].astype(BF16)

    return pl.pallas_call(
        body, name=name, grid=(ff // tn, nk),
        in_specs=[pl.BlockSpec((tk, d), lambda f, k: (k, 0)),
                  pl.BlockSpec((tk, tn), lambda f, k: (k, f)),
                  pl.BlockSpec((tk, tn), lambda f, k: (k, f)),
                  pl.BlockSpec((tk, d), lambda f, k: (k, 0))],
        out_specs=[pl.BlockSpec((None, d, tn), lambda f, k: (f // per_slot, 0, f % per_slot)),
                   pl.BlockSpec((tn, d), lambda f, k: (f, 0))],
        out_shape=[jax.ShapeDtypeStruct((ff // slot_cols, d, slot_cols), BF16),
                   jax.ShapeDtypeStruct((ff, d), BF16)],
        scratch_shapes=[pltpu.VMEM((d, tn), F32), pltpu.VMEM((tn, d), F32)],
        compiler_params=_params(("parallel", "arbitrary")),
    )(n, da, a, dzb)


def _pool_bwd(after, dh, h, g, pw, ps):
    s, d = h.shape
    cg = d // len(POOL_WINDOWS)
    tm = min(TILE_ROWS, s)
    nb = s // tm
    halo_per_tile = tm // POOL_HALO

    def body(after_ref, dh_ref, h_ref, halo_ref, g_ref, pw_ref, ps_ref,
             dx_ref, dpw_ref, dps_ref, dg_ref, nbuf, qbuf, dn_ref, carry, dpw_acc):
        i = pl.program_id(0)
        blk = nb - 1 - i

        @pl.when(i == 0)
        def _():
            carry[...] = jnp.zeros_like(carry)
            dpw_acc[...] = jnp.zeros_like(dpw_acc)
            dps_ref[...] = jnp.zeros_like(dps_ref)
            dg_ref[...] = jnp.zeros_like(dg_ref)

        hv = h_ref[...]
        n, _ = _rms_fwd(hv, g_ref[...])
        nh, _ = _rms_fwd(halo_ref[...], g_ref[...])
        nbuf[0:POOL_HALO, :] = jnp.where(blk == 0, 0.0, nh)
        nbuf[POOL_HALO:POOL_HALO + tm, :] = n
        dhv = dh_ref[...]
        for k, window in enumerate(POOL_WINDOWS):
            cols = slice(k * cg, (k + 1) * cg)
            cnt = _pool_counts(blk * tm, tm, window)
            sums = _window_sum_down(nbuf[:, cols], window)[POOL_HALO:, :]
            pb = (sums / cnt - n[:, cols]).astype(BF16)
            dyk = dhv[:, cols]
            dps_ref[:, cols] += jnp.sum(dyk * _nn(pb, pw_ref[k]), axis=0, keepdims=True)
            dyb = (dyk * ps_ref[:, cols]).astype(BF16)
            dpw_acc[k] += _tn(pb, dyb)
            dpool = _nt(dyb, pw_ref[k])
            qv = dpool / cnt
            qbuf[0:tm, cols] = qv
            qbuf[tm:tm + POOL_HALO, cols] = carry[:, cols]
            dn_ref[:, cols] = _window_sum_up(qbuf[:, cols], window)[0:tm, :] - dpool
            carry[:, cols] = qv[0:POOL_HALO, :]
        dx, dg = _rms_bwd(dn_ref[...], hv, g_ref[...])
        dx_ref[...] = dhv + dx
        dg_ref[...] += dg

        @pl.when(i == nb - 1)
        def _():
            dpw_ref[...] = dpw_acc[...].astype(BF16)

    rev = lambda i: (nb - 1 - i, 0)
    return pl.pallas_call(
        body, name="pool_bwd", grid=(nb,),
        in_specs=[ANY, pl.BlockSpec((tm, d), rev), pl.BlockSpec((tm, d), rev),
                  pl.BlockSpec((POOL_HALO, d),
                               lambda i: (jnp.maximum((nb - 1 - i) * halo_per_tile - 1, 0), 0)),
                  _full(g), _full(pw), _full(ps)],
        out_specs=[pl.BlockSpec((tm, d), rev), _full(pw),
                   pl.BlockSpec((1, d), lambda i: (0, 0)),
                   pl.BlockSpec((1, d), lambda i: (0, 0))],
        out_shape=[jax.ShapeDtypeStruct((s, d), F32),
                   jax.ShapeDtypeStruct(pw.shape, BF16),
                   jax.ShapeDtypeStruct((1, d), F32),
                   jax.ShapeDtypeStruct((1, d), F32)],
        scratch_shapes=[pltpu.VMEM((tm + POOL_HALO, d), F32), pltpu.VMEM((tm + POOL_HALO, d), F32),
                        pltpu.VMEM((tm, d), F32), pltpu.VMEM((POOL_HALO, d), F32),
                        pltpu.VMEM(pw.shape, F32)],
        compiler_params=_params(("arbitrary",)),
    )(after, dh, h, h, g, pw, ps)


def _conv_out_bwd(after, dh, w_out, o, bcx, cw):
    s, d = dh.shape
    c = o.shape[1]
    tm = min(TILE_PROJ_ROWS, s)
    nb = s // tm
    halo_per_tile = tm // CONV_HALO

    def body(after_ref, dh_ref, w_ref, o_ref, b_ref, c_ref, xin_ref, ch_ref, xh_ref, cw_ref,
             do_ref, delta_ref, dbcx_ref, dw_ref, dcw_ref, ubuf, dbuf, carry, acc):
        i = pl.program_id(0)
        blk = nb - 1 - i

        @pl.when(i == 0)
        def _():
            carry[...] = jnp.zeros_like(carry)
            acc[...] = jnp.zeros_like(acc)
            dcw_ref[...] = jnp.zeros_like(dcw_ref)

        dm = dh_ref[...].astype(BF16)
        dcat = _nt(dm, w_ref[...])
        do = dcat[:, 0:c]
        dy = dcat[:, c:2 * c]
        do_ref[...] = do.astype(BF16)
        head_of_lane = lax.shift_right_logical(_lane((8, c)), HEAD_DIM.bit_length() - 1)
        heads = (head_of_lane == _row((8, c))).astype(BF16)
        delta_ref[...] = _exact_nt(heads, do * o_ref[...].astype(F32))

        cv_ = c_ref[...]
        xin = xin_ref[...]
        bv = b_ref[...]
        u = cv_ * xin
        ubuf[0:CONV_HALO, :] = jnp.where(blk == 0, 0.0, ch_ref[...] * xh_ref[...])
        ubuf[CONV_HALO:CONV_HALO + tm, :] = u
        u1 = ubuf[CONV_HALO - 1:CONV_HALO - 1 + tm, :]
        u2 = ubuf[CONV_HALO - 2:CONV_HALO - 2 + tm, :]
        w0, w1, w2 = cw_ref[0:1, :], cw_ref[1:2, :], cw_ref[2:3, :]
        cv = (w0 * u2 + w1 * u1) + w2 * u
        acc[0:c, :] += _tn(o_ref[...], dm)
        acc[c:2 * c, :] += _tn((bv * cv).astype(BF16), dm)

        dcv = dy * bv
        dcw_ref[0:1, :] += jnp.sum(dcv * u2, axis=0, keepdims=True)
        dcw_ref[1:2, :] += jnp.sum(dcv * u1, axis=0, keepdims=True)
        dcw_ref[2:3, :] += jnp.sum(dcv * u, axis=0, keepdims=True)
        dbuf[0:tm, :] = dcv
        dbuf[tm:tm + CONV_HALO, :] = carry[...]
        du = w2 * dcv + w1 * dbuf[1:1 + tm, :] + w0 * dbuf[2:2 + tm, :]
        dbcx_ref[:, 0:c] = (dy * cv).astype(BF16)
        dbcx_ref[:, c:2 * c] = (du * xin).astype(BF16)
        dbcx_ref[:, 2 * c:3 * c] = (du * cv_).astype(BF16)
        carry[...] = dcv[0:CONV_HALO, :]

        @pl.when(i == nb - 1)
        def _():
            dw_ref[...] = acc[...].astype(BF16)

    rev = lambda k: (lambda i: (nb - 1 - i, k))
    halo = lambda k: (lambda i: (jnp.maximum((nb - 1 - i) * halo_per_tile - 1, 0), k))
    return pl.pallas_call(
        body, name="conv_out_bwd", grid=(nb,),
        in_specs=[ANY, pl.BlockSpec((tm, d), rev(0)), _full(w_out), pl.BlockSpec((tm, c), rev(0)),
                  pl.BlockSpec((tm, c), rev(0)), pl.BlockSpec((tm, c), rev(1)),
                  pl.BlockSpec((tm, c), rev(2)),
                  pl.BlockSpec((CONV_HALO, c), halo(1)), pl.BlockSpec((CONV_HALO, c), halo(2)),
                  _full(cw)],
        out_specs=[pl.BlockSpec((tm, c), rev(0)),
                   pl.BlockSpec((8, tm), lambda i: (0, nb - 1 - i)),
                   pl.BlockSpec((tm, 3 * c), rev(0)),
                   _full(w_out), _full(cw)],
        out_shape=[jax.ShapeDtypeStruct((s, c), BF16),
                   jax.ShapeDtypeStruct((8, s), F32),
                   jax.ShapeDtypeStruct((s, 3 * c), BF16),
                   jax.ShapeDtypeStruct(w_out.shape, BF16),
                   jax.ShapeDtypeStruct(cw.shape, F32)],
        scratch_shapes=[pltpu.VMEM((tm + CONV_HALO, c), F32), pltpu.VMEM((tm + CONV_HALO, c), F32),
                        pltpu.VMEM((CONV_HALO, c), F32), pltpu.VMEM(w_out.shape, F32)],
        compiler_params=_params(("arbitrary",)),
    )(after, dh, w_out, o, bcx, bcx, bcx, bcx, bcx, cw)


def _attn_bwd(after, qa, ka, qkv, do, lse, delta):
    s = qa.shape[1]
    a = N_HEADS * HEAD_DIM
    t = min(TILE_ATTN, s)
    nq = s // t
    n_pairs = N_HEADS // 2
    v_block0 = 2 * a // 128

    def body(after_ref, ka_ref, v_ref, qa_ref, do_ref, lse_ref, delta_ref,
             dqt_ref, dka_ref, dv_ref, dv_acc):
        g = pl.program_id(0)
        j = pl.program_id(1)

        @pl.when(j == 0)
        def _():
            dqt_ref[...] = jnp.zeros_like(dqt_ref)

        lane = _lane((t, 128))
        vf = v_ref[...].astype(F32)
        v_heads = [jnp.where(lane < HEAD_DIM, vf, 0.0).astype(BF16),
                   jnp.where(lane >= HEAD_DIM, vf, 0.0).astype(BF16)]
        ke_t = [ka_ref[e].astype(F32).T.astype(BF16) for e in range(2)]

        def q_step(i, first):
            qs = pl.ds(pl.multiple_of(i * t, t), t)
            dob = do_ref[qs, :]
            for e in range(2):
                qe = qa_ref[e, qs, :]
                sc = _nt(ka_ref[e], qe)
                if first:
                    sc = jnp.where(_row((t, t)) <= _lane((t, t)), sc, NEG_BIG)
                p = jnp.exp2(sc - lse_ref[pl.ds(e, 1), qs])
                dv_part = _nn(p.astype(BF16), dob)
                dp = _nt(v_heads[e], dob)
                ds = (p * (dp - delta_ref[pl.ds(2 * g + e, 1), qs])).astype(BF16)
                dk_part = _nn(ds, qe)
                if first:
                    dv_acc[e] = dv_part
                    dka_ref[e] = dk_part
                else:
                    dv_acc[e] += dv_part
                    dka_ref[e] += dk_part
                dqt_ref[e, :, qs] += _nn(ke_t[e], ds)

        q_step(j, True)

        def full_step(i, carry):
            q_step(i, False)
            return carry

        lax.fori_loop(j + 1, nq, full_step, 0)
        dv_ref[...] = jnp.where(lane < HEAD_DIM, dv_acc[0], dv_acc[1]).astype(BF16)

    return pl.pallas_call(
        body, name="attn_bwd", grid=(n_pairs, nq),
        in_specs=[ANY, pl.BlockSpec((2, t, 128), lambda g, j: (g, j, 0)),
                  pl.BlockSpec((t, 128), lambda g, j: (j, v_block0 + g)),
                  pl.BlockSpec((2, s, 128), lambda g, j: (g, 0, 0)),
                  pl.BlockSpec((s, 128), lambda g, j: (0, g)),
                  pl.BlockSpec((None, 8, s), lambda g, j: (g, 0, 0)),
                  pl.BlockSpec((8, s), lambda g, j: (0, 0))],
        out_specs=[pl.BlockSpec((2, 128, s), lambda g, j: (g, 0, 0)),
                   pl.BlockSpec((2, t, 128), lambda g, j: (g, j, 0)),
                   pl.BlockSpec((t, 128), lambda g, j: (j, g))],
        out_shape=[jax.ShapeDtypeStruct((N_HEADS, 128, s), F32),
                   jax.ShapeDtypeStruct((N_HEADS, s, 128), F32),
                   jax.ShapeDtypeStruct((s, a), BF16)],
        scratch_shapes=[pltpu.VMEM((2, t, 128), F32)],
        compiler_params=_params(("parallel", "arbitrary")),
    )(after, ka, qkv, qa, do, lse, delta)


def _gate_bwd(dqa, dka, dv, fl, bf):
    s = fl.shape[0]
    a = N_HEADS * HEAD_DIM
    tm = min(TILE_ROWS, s)
    nb = s // tm

    def body(dqa_ref, dka_ref, dv_ref, fl_ref, bf_ref, dqkv_ref, dfl_ref, dbf_ref, carry):
        i = pl.program_id(0)

        @pl.when(i == 0)
        def _():
            carry[...] = jnp.zeros_like(carry)
            dbf_ref[...] = jnp.zeros_like(dbf_ref)

        lane = _lane((tm, 128))
        dq_sum = jnp.zeros((tm, 128), F32)
        dk_sum = jnp.zeros((tm, 128), F32)
        for pair in range(N_HEADS // 2):
            qs, ks = [], []
            for e in range(2):
                h = 2 * pair + e
                dq = dqa_ref[h].T
                dk = dka_ref[h]
                dq_sum = dq_sum + dq
                dk_sum = dk_sum + dk
                qs.append(dq * ATTN_SCALE)
                ks.append(dk * (1.0 / LOG2_E))
            cols = slice(pair * 128, (pair + 1) * 128)
            dqkv_ref[:, cols] = jnp.where(
                lane < HEAD_DIM, qs[0], pltpu.roll(qs[1], HEAD_DIM, axis=1)).astype(BF16)
            dqkv_ref[:, a + pair * 128:a + (pair + 1) * 128] = jnp.where(
                lane < HEAD_DIM, ks[0], pltpu.roll(ks[1], HEAD_DIM, axis=1)).astype(BF16)
        dqkv_ref[:, 2 * a:3 * a] = dv_ref[...]

        in_q = (lane >= LANE_CQ) & (lane < LANE_CQ + N_HEADS)
        in_k = (lane >= LANE_CK) & (lane < LANE_CK + N_HEADS)
        dcum = (pltpu.roll(jnp.where(in_q, dq_sum, 0.0), 128 - LANE_CQ, axis=1)
                - pltpu.roll(jnp.where(in_k, dk_sum, 0.0), 128 - LANE_CK, axis=1))

        upper = (_lane((tm, tm)) >= _row((tm, tm))).astype(BF16)
        dlogf = _exact_nn(upper, dcum) + carry[0:1, :]
        carry[0:1, :] = dlogf[0:1, :]
        z = fl_ref[...] + bf_ref[...]
        ez = jnp.exp(-jnp.abs(z))
        sig_neg = jnp.where(z >= 0.0, ez, 1.0) / (1.0 + ez)
        dz = jnp.where(lane < N_HEADS, dlogf * sig_neg, 0.0)
        dfl_ref[...] = dz.astype(BF16)
        dbf_ref[...] += jnp.sum(dz, axis=0, keepdims=True)

    rev3 = lambda i: (0, nb - 1 - i, 0)
    rev = lambda i: (nb - 1 - i, 0)
    return pl.pallas_call(
        body, name="gate_bwd", grid=(nb,),
        in_specs=[pl.BlockSpec((N_HEADS, 128, tm), lambda i: (0, 0, nb - 1 - i)),
                  pl.BlockSpec((N_HEADS, tm, 128), rev3),
                  pl.BlockSpec((tm, a), rev), pl.BlockSpec((tm, 128), rev), _full(bf)],
        out_specs=[pl.BlockSpec((tm, 3 * a), rev), pl.BlockSpec((tm, 128), rev),
                   pl.BlockSpec((1, 128), lambda i: (0, 0))],
        out_shape=[jax.ShapeDtypeStruct((s, 3 * a), BF16),
                   jax.ShapeDtypeStruct((s, 128), BF16),
                   jax.ShapeDtypeStruct((1, 128), F32)],
        scratch_shapes=[pltpu.VMEM((8, 128), F32)],
        compiler_params=_params(("arbitrary",)),
    )(dqa, dka, dv, fl, bf)


def _in_proj_bwd(after, dqkv, dfl, dbcx, w_qkv, w_f, w_bcx, x, g, dh):
    s, d = x.shape
    tm = min(TILE_ROWS, s)

    def body(after_ref, dq_ref, df_ref, db_ref, wq_ref, wf_ref, wb_ref, x_ref, g_ref, dh_ref,
             gx_ref, dg_ref):
        i = pl.program_id(0)
        dn = (_nt(dq_ref[...], wq_ref[...]) + _nt(df_ref[...], wf_ref[...])
              + _nt(db_ref[...], wb_ref[...]))
        dx, dg = _rms_bwd(dn, x_ref[...], g_ref[...])
        gx_ref[...] = dh_ref[...] + dx

        @pl.when(i == 0)
        def _():
            dg_ref[...] = dg

        @pl.when(i > 0)
        def _():
            dg_ref[...] += dg

    rows = lambda c: pl.BlockSpec((tm, c), lambda i: (i, 0))
    return pl.pallas_call(
        body, name="in_proj_bwd", grid=(s // tm,),
        in_specs=[ANY, rows(dqkv.shape[1]), rows(dfl.shape[1]), rows(dbcx.shape[1]),
                  _full(w_qkv), _full(w_f), _full(w_bcx), rows(d), _full(g), rows(d)],
        out_specs=[rows(d), pl.BlockSpec((1, d), lambda i: (0, 0))],
        out_shape=[jax.ShapeDtypeStruct((s, d), F32), jax.ShapeDtypeStruct((1, d), F32)],
        compiler_params=_params(("arbitrary",)),
    )(after, dqkv, dfl, dbcx, w_qkv, w_f, w_bcx, x, g, dh)


def _wgrad_in(n, dys):
    s, d = n.shape
    m = len(dys)
    tk = min(TILE_ROWS, s)
    nk = s // tk

    def body(*refs):
        n_ref, dy_refs, dw_refs, accs = refs[0], refs[1:1 + m], refs[1 + m:1 + 2 * m], refs[1 + 2 * m:]
        k = pl.program_id(0)

        @pl.when(k == 0)
        def _():
            for acc in accs:
                acc[...] = jnp.zeros_like(acc)

        nb = n_ref[...]
        for dy_ref, acc in zip(dy_refs, accs):
            acc[...] += _tn(nb, dy_ref[...])

        @pl.when(k == nk - 1)
        def _():
            for dw_ref, acc in zip(dw_refs, accs):
                dw_ref[...] = acc[...].T.astype(BF16)

    return pl.pallas_call(
        body, name="wgrad_in", grid=(nk,),
        in_specs=[pl.BlockSpec((tk, d), lambda k: (k, 0))]
        + [pl.BlockSpec((tk, dy.shape[1]), lambda k: (k, 0)) for dy in dys],
        out_specs=[pl.BlockSpec((dy.shape[1], d), lambda k: (0, 0)) for dy in dys],
        out_shape=[jax.ShapeDtypeStruct((dy.shape[1], d), BF16) for dy in dys],
        scratch_shapes=[pltpu.VMEM((d, dy.shape[1]), F32) for dy in dys],
        compiler_params=_params(("arbitrary",)),
    )(n, *dys)


def _row_tile(rows):
    t = min(TILE_ELEM_ROWS, rows)
    while rows % t:
        t //= 2
    return t


def _adamw_math(w, g, m, v):
    m = ADAM_B1 * m + (1.0 - ADAM_B1) * g
    v = ADAM_B2 * v + (1.0 - ADAM_B2) * jnp.square(g)
    m_hat = m / (1.0 - ADAM_B1 ** ADAM_STEP)
    v_hat = v / (1.0 - ADAM_B2 ** ADAM_STEP)
    delta = -ADAM_LR * (m_hat / (jnp.sqrt(v_hat) + ADAM_EPS) + ADAM_WD * w)
    return delta, m, v


def _adamw(w, g, m, v, name):
    rows, cols = w.shape

    def body(w_ref, g_ref, m_ref, v_ref, d_ref, nm_ref, nv_ref):
        delta, nm, nv = _adamw_math(w_ref[...], g_ref[...], m_ref[...], v_ref[...])
        d_ref[...] = delta
        nm_ref[...] = nm
        nv_ref[...] = nv

    if rows % 8 == 0:
        tr = _row_tile(rows)
        grid, spec = (rows // tr,), pl.BlockSpec((tr, cols), lambda i: (i, 0))
    else:
        grid, spec = (cols // 256,), pl.BlockSpec((rows, 256), lambda i: (0, i))
    out = jax.ShapeDtypeStruct(w.shape, F32)
    return pl.pallas_call(
        body, name=name, grid=grid, in_specs=[spec] * 4, out_specs=[spec] * 3,
        out_shape=[out, out, out], compiler_params=_params(("parallel",)),
    )(w, g, m, v)


def _sum_devices(parts):
    def body(p_ref, g_ref):
        g = p_ref[0]
        for k in range(1, N_DEV):
            g = g + p_ref[k]
        g_ref[...] = g

    return pl.pallas_call(
        body, name="sum_devices",
        in_specs=[pl.BlockSpec(memory_space=pltpu.VMEM)],
        out_specs=pl.BlockSpec(memory_space=pltpu.VMEM),
        out_shape=jax.ShapeDtypeStruct(parts.shape[1:], F32),
    )(parts)


def _mesh_position():
    x, y, c = lax.axis_index("x"), lax.axis_index("y"), lax.axis_index("c")
    chips = [(1 - x, y), (x, 1 - y), (1 - x, 1 - y)]
    return x, y, c, chips


ANY = pl.BlockSpec(memory_space=pl.ANY)
HBM = pl.BlockSpec(memory_space=pltpu.HBM)
SEM = pl.BlockSpec(memory_space=pltpu.SEMAPHORE)
SPLIT_COPY_EFFECT = pltpu.SideEffectType.DATAFLOW_SIDE_EFFECTING


def _in_hbm(a):
    return pltpu.with_memory_space_constraint(a, pltpu.HBM)


def _chip_copies(views, srcs, lands, send, recv, waiting=False):
    _, _, c, chips = _mesh_position()
    cps = []
    for a in range(len(srcs)):
        for k, (px, py) in enumerate(chips):
            src, dst = views(a, k, srcs[a], lands[a], c, 2 * px + py)
            sem = a * (N_CHIPS - 1) + k
            cps.append(pltpu.make_async_remote_copy(
                src_ref=src, dst_ref=dst, send_sem=send.at[sem], recv_sem=recv.at[sem],
                device_id=(px, py, c), device_id_type=MESH))
    return cps


def _ici_start(sources, land_shapes, copies, after, name, per_array=N_CHIPS - 1):
    n = len(sources)

    def body(*refs):
        srcs, lands = refs[:n], refs[n:2 * n]
        send, recv = refs[2 * n + 1], refs[2 * n + 2]
        token = refs[-1]
        for cp in copies(srcs, lands, send, recv, False):
            cp.start()
        token[...] = jnp.zeros_like(token)

    lands = [_in_hbm(lax.empty(s.shape, s.dtype)) for s in land_shapes]
    outs = pl.pallas_call(
        body, name=name,
        in_specs=[HBM] * (2 * n) + [ANY],
        out_specs=[SEM, SEM] + [HBM] * (2 * n) + [pl.BlockSpec(memory_space=pltpu.VMEM)],
        out_shape=[pltpu.SemaphoreType.DMA((n * per_array,))] * 2
        + [pltpu.HBM(a.shape, a.dtype) for a in sources]
        + [pltpu.HBM(s.shape, s.dtype) for s in land_shapes]
        + [jax.ShapeDtypeStruct((8, 128), F32)],
        input_output_aliases={i: 2 + i for i in range(2 * n)},
        compiler_params=pltpu.CompilerParams(has_side_effects=SPLIT_COPY_EFFECT),
    )(*[_in_hbm(a) for a in sources], *lands, after)
    return outs[0], outs[1], list(outs[2:2 + n]), list(outs[2 + n:2 + 2 * n]), outs[-1]


def _ici_wait(handle, copies, after, name):
    send, recv, srcs, lands, _ = handle
    n = len(srcs)

    def body(*refs):
        src_refs, land_refs = refs[:n], refs[n:2 * n]
        for cp in copies(src_refs, land_refs, refs[2 * n], refs[2 * n + 1], True):
            cp.wait_send()
            cp.wait_recv()

    outs = pl.pallas_call(
        body, name=name,
        in_specs=[HBM] * (2 * n) + [SEM, SEM, ANY],
        out_specs=[HBM] * (2 * n),
        out_shape=[pltpu.HBM(a.shape, a.dtype) for a in srcs]
        + [pltpu.HBM(a.shape, a.dtype) for a in lands],
        input_output_aliases={i: i for i in range(2 * n)},
        compiler_params=pltpu.CompilerParams(has_side_effects=SPLIT_COPY_EFFECT),
    )(*srcs, *lands, send, recv, after)
    return list(outs[:n]), list(outs[n:])


def _gather_views(split):
    def views(a, k, src, land, c, slot):
        if split[a]:
            half = src.shape[0] // 2
            src = src.at[pl.ds(c * half, half)]
        return src, land.at[k]
    return views


def _gather_whole_views(a, k, src, land, c, slot):
    x, y, _, _ = _mesh_position()
    return src, land.at[2 * x + y]


SCATTER_COPIES = 2 * (N_CHIPS - 1)


def _scatter_copies(srcs, lands, send, recv, waiting):
    _, _, c, chips = _mesh_position()
    cps = []
    for a in range(len(srcs)):
        half = srcs[a].shape[1] // 2
        for k, (px, py) in enumerate(chips):
            for h in range(2):
                arrival = 2 * k + (h if waiting else c)
                cps.append(pltpu.make_async_remote_copy(
                    src_ref=srcs[a].at[2 * px + py, pl.ds(h * half, half)],
                    dst_ref=lands[a].at[arrival],
                    send_sem=send.at[a * SCATTER_COPIES + 2 * k + h],
                    recv_sem=recv.at[a * SCATTER_COPIES + arrival],
                    device_id=(px, py, h), device_id_type=MESH))
    return cps


def _gather_land_shapes(shards, split):
    return [jax.ShapeDtypeStruct(
        (N_CHIPS - 1, a.shape[0] // 2 if sp else a.shape[0]) + a.shape[1:], a.dtype)
        for a, sp in zip(shards, split)]


def _gather_finish(shards, lands, split, name):
    n = len(shards)
    ns = sum(split)
    d_index = {a: i for i, a in enumerate(a for a in range(n) if split[a])}

    def body(*refs):
        shard, land, outs = refs[:n], refs[n:2 * n], refs[2 * n:3 * n]
        obuf, fbuf = refs[3 * n:4 * n], refs[4 * n:5 * n]
        dbuf = refs[5 * n:5 * n + ns]
        ld_own, st_own, ld, st_mine, st_sib, send, recv = refs[5 * n + ns:]
        x, y, c, chips = _mesh_position()
        me = 2 * x + y
        own_loads, loads, sends, pending = [], {}, [], []
        for a in range(n):
            cp = pltpu.make_async_copy(shard[a], obuf[a], ld_own.at[a])
            cp.start()
            own_loads.append(cp)
        for a in range(n):
            for k in range(N_CHIPS - 1):
                cp = pltpu.make_async_copy(land[a].at[k], fbuf[a].at[k], ld.at[a, k])
                cp.start()
                loads[a, k] = cp
        for a in range(n):
            own_loads[a].wait()
            cp = pltpu.make_async_copy(obuf[a], outs[a].at[me], st_own.at[a])
            cp.start()
            pending.append(cp)
        for a in range(n):
            rows = shard[a].shape[0]
            for k, (px, py) in enumerate(chips):
                loads[a, k].wait()
                part = pl.ds(c * (rows // 2), rows // 2) if split[a] else pl.ds(0, rows)
                cp = pltpu.make_async_copy(fbuf[a].at[k], outs[a].at[2 * px + py, part],
                                           st_mine.at[a, k])
                cp.start()
                pending.append(cp)
                if split[a]:
                    fw = pltpu.make_async_remote_copy(
                        src_ref=fbuf[a].at[k], dst_ref=dbuf[d_index[a]].at[k],
                        send_sem=send.at[a, k], recv_sem=recv.at[a, k],
                        device_id=(x, y, 1 - c), device_id_type=MESH)
                    fw.start()
                    sends.append((a, k, fw))
        for a, k, fw in sends:
            px, py = chips[k]
            half = shard[a].shape[0] // 2
            fw.wait_recv()
            cp = pltpu.make_async_copy(dbuf[d_index[a]].at[k],
                                       outs[a].at[2 * px + py, pl.ds((1 - c) * half, half)],
                                       st_sib.at[a, k])
            cp.start()
            pending.append(cp)
        for _, _, fw in sends:
            fw.wait_send()
        for cp in pending:
            cp.wait()

    stage = [pltpu.VMEM(a.shape, a.dtype) for a in lands]
    dma = lambda *shape: pltpu.SemaphoreType.DMA(shape)
    return pl.pallas_call(
        body, name=name,
        in_specs=[ANY] * (2 * n), out_specs=[ANY] * n,
        out_shape=[jax.ShapeDtypeStruct((N_CHIPS,) + a.shape, a.dtype) for a in shards],
        scratch_shapes=[pltpu.VMEM(a.shape, a.dtype) for a in shards] + stage
        + [s for s, sp in zip(stage, split) if sp]
        + [dma(n), dma(n), dma(n, 3), dma(n, 3), dma(n, 3), dma(n, 3), dma(n, 3)],
        compiler_params=pltpu.CompilerParams(vmem_limit_bytes=VMEM_LIMIT_BYTES),
    )(*shards, *lands)


def _sum_chunk(rows):
    return next(r for r in range(SUM_CHUNK_ROWS, 0, -16) if rows % r == 0)


def _sum_and_share(partials, lands, name):
    n = len(partials)

    def body(*refs):
        own, landed, outs = refs[:n], refs[n:2 * n], refs[2 * n:3 * n]
        obuf, xbuf, ybuf, gbuf, sbuf, rbuf = (refs[(3 + k) * n:(4 + k) * n] for k in range(6))
        ld_own, ld_send, ld_got, st_own, st_sib, send_p, recv_p, send_s, recv_s = refs[9 * n:]
        x, y, c, _ = _mesh_position()
        me = 2 * x + y
        sibling = (x, y, 1 - c)

        def to_sibling(src, dst, send, recv, a):
            return pltpu.make_async_remote_copy(src_ref=src, dst_ref=dst, send_sem=send.at[a],
                                                recv_sem=recv.at[a], device_id=sibling,
                                                device_id_type=MESH)

        loads, firsts, seconds, stores = [], [], [], []
        for a in range(n):
            half = obuf[a].shape[0]
            cps = [pltpu.make_async_copy(own[a].at[me, pl.ds((1 - c) * half, half)], xbuf[a],
                                         ld_send.at[a]),
                   pltpu.make_async_copy(own[a].at[me, pl.ds(c * half, half)], obuf[a], ld_own.at[a]),
                   pltpu.make_async_copy(landed[a], gbuf[a], ld_got.at[a])]
            for cp in cps:
                cp.start()
            loads.append(cps)
        for a in range(n):
            loads[a][0].wait()
            rc = to_sibling(xbuf[a], ybuf[a], send_p, recv_p, a)
            rc.start()
            firsts.append(rc)
        for a in range(n):
            firsts[a].wait_recv()
            loads[a][1].wait()
            loads[a][2].wait()
            half = obuf[a].shape[0]
            rows = _sum_chunk(half)

            def add(k, carry, a=a, rows=rows):
                at = pl.ds(pl.multiple_of(k * rows, rows), rows)
                acc = obuf[a][at].astype(F32) + ybuf[a][at].astype(F32)
                for j in range(SCATTER_COPIES):
                    acc = acc + gbuf[a][j, at].astype(F32)
                sbuf[a][at] = acc
                return carry

            lax.fori_loop(0, half // rows, add, 0)
            rc = to_sibling(sbuf[a], rbuf[a], send_s, recv_s, a)
            rc.start()
            seconds.append(rc)
            cp = pltpu.make_async_copy(sbuf[a], outs[a].at[pl.ds(c * half, half)], st_own.at[a])
            cp.start()
            stores.append(cp)
        for a in range(n):
            half = obuf[a].shape[0]
            seconds[a].wait_recv()
            cp = pltpu.make_async_copy(rbuf[a], outs[a].at[pl.ds((1 - c) * half, half)], st_sib.at[a])
            cp.start()
            stores.append(cp)
        for rc in firsts + seconds:
            rc.wait_send()
        for cp in stores:
            cp.wait()

    halves = [(a.shape[1] // 2, a.shape[2]) for a in partials]
    return pl.pallas_call(
        body, name=name,
        in_specs=[ANY] * (2 * n), out_specs=[ANY] * n,
        out_shape=[jax.ShapeDtypeStruct((2 * h[0], h[1]), F32) for h in halves],
        scratch_shapes=[pltpu.VMEM(h, BF16) for h in halves] * 3
        + [pltpu.VMEM(g.shape, BF16) for g in lands]
        + [pltpu.VMEM(h, F32) for h in halves] * 2
        + [pltpu.SemaphoreType.DMA((n,))] * 9,
        compiler_params=pltpu.CompilerParams(vmem_limit_bytes=VMEM_LIMIT_BYTES),
    )(*partials, *lands)


def _gather_small(part):
    def body(in_ref, out_ref, send, recv, local):
        x, y, c, _ = _mesh_position()
        me = 4 * x + 2 * y + c
        cps = [pltpu.make_async_copy(in_ref, out_ref.at[me], local)]
        k = 0
        for fx in range(2):
            for fy in range(2):
                for fc in range(2):
                    if fx or fy or fc:
                        cps.append(pltpu.make_async_remote_copy(
                            src_ref=in_ref, dst_ref=out_ref.at[me], send_sem=send.at[k],
                            recv_sem=recv.at[k], device_id=(x ^ fx, y ^ fy, c ^ fc),
                            device_id_type=MESH))
                        k += 1
        for cp in cps:
            cp.start()
        for cp in cps:
            cp.wait()

    return pl.pallas_call(
        body, name="gather_small",
        in_specs=[pl.BlockSpec(memory_space=pltpu.VMEM)],
        out_specs=pl.BlockSpec(memory_space=pltpu.VMEM),
        out_shape=jax.ShapeDtypeStruct((N_DEV,) + part.shape, part.dtype),
        scratch_shapes=[pltpu.SemaphoreType.DMA((N_DEV - 1,)), pltpu.SemaphoreType.DMA((N_DEV - 1,)),
                        pltpu.SemaphoreType.DMA],
    )(part)


def _scatter_start(grads, after, tag):
    lands = [jax.ShapeDtypeStruct((SCATTER_COPIES, g.shape[1] // 2, g.shape[2]), g.dtype)
             for g in grads]
    return _ici_start(grads, lands, _scatter_copies, after, "scatter_start_" + tag,
                      per_array=SCATTER_COPIES)


def _scatter_finish(handles, after, tag):
    grads, lands = [], []
    for k, handle in enumerate(handles):
        g, l = _ici_wait(handle, _scatter_copies, after, "scatter_wait_%s_%d" % (tag, k))
        grads += g
        lands += l
    return _sum_and_share(grads, lands, "sum_and_share_" + tag)


def _pad_rows(a, rows):
    return jnp.pad(a, ((0, rows - a.shape[0]), (0, 0)))


def kernel(x, norm_mix_0, w_in_0, b_f_0, conv_w_0, w_out_0, norm_ffn_0, w_up_0, w_down_0, norm_mix_1, pool_w_1, pool_scale_1, norm_ffn_1, w_up_1, w_down_1, final_norm, loss_target, m_norm_mix_0, m_w_in_0, m_b_f_0, m_conv_w_0, m_w_out_0, m_norm_ffn_0, m_w_up_0, m_w_down_0, m_norm_mix_1, m_pool_w_1, m_pool_scale_1, m_norm_ffn_1, m_w_up_1, m_w_down_1, m_final_norm, v_norm_mix_0, v_w_in_0, v_b_f_0, v_conv_w_0, v_w_out_0, v_norm_ffn_0, v_w_up_0, v_w_down_0, v_norm_mix_1, v_pool_w_1, v_pool_scale_1, v_norm_ffn_1, v_w_up_1, v_w_down_1, v_final_norm):
    d = x.shape[-1]
    a = N_HEADS * HEAD_DIM
    c_conv = conv_w_0.shape[1] * N_CHIPS
    xs = x[0]
    target = loss_target[0]
    row = lambda vec: vec.reshape(1, -1)

    big = [w_in_0, w_out_0, w_up_0, w_down_0, pool_w_1, w_up_1, w_down_1]
    first = [w_in_0.astype(BF16)]
    first_split = [True]
    copies_a = functools.partial(_chip_copies, _gather_views(first_split))
    copies_b = functools.partial(_chip_copies, _gather_whole_views)
    start_a = _ici_start(first, _gather_land_shapes(first, first_split), copies_a, b_f_0,
                         "gather_start_a")
    zero = start_a[-1][0, 0]
    rest = [(w + zero).astype(BF16)
            for w in (w_out_0, w_up_0, w_down_0, pool_w_1, w_up_1, w_down_1)]
    rest = rest + [conv_w_0]
    start_b = _ici_start(rest, [jax.ShapeDtypeStruct((N_CHIPS,) + w.shape, w.dtype) for w in rest],
                         copies_b, start_a[-1], "gather_start_b")
    n0 = _rms_pre(start_b[-1], xs, row(norm_mix_0))
    first, land_a = _ici_wait(start_a, copies_a, n0, "gather_wait_a")
    (g_in,) = _gather_finish(first, land_a, first_split, "gather_finish_a")
    w_in = g_in.transpose(1, 0, 2).reshape(d, -1)
    w_qkv = w_in[:, :3 * a]
    w_f = jnp.pad(w_in[:, 3 * a:3 * a + N_HEADS], ((0, 0), (0, 128 - N_HEADS)))
    w_bcx = w_in[:, 3 * a + N_HEADS:]
    bf = jnp.pad(b_f_0, (0, 128 - N_HEADS)).reshape(1, 128)

    qkv, fl, bcx = _in_proj(n0, w_qkv, w_f, w_bcx)
    qa, ka = _gate_prep(fl, bf, qkv)
    o, lse = _attn_fwd(qa, ka, qkv)
    rest, land_b = _ici_wait(start_b, copies_b, o, "gather_wait_b")
    own_slot = 2 * lax.axis_index("x") + lax.axis_index("y")
    g_out, g_up0, g_down0, g_pool, g_up1, g_down1, g_conv = [
        lax.dynamic_update_index_in_dim(land, shard, own_slot, 0)
        for land, shard in zip(land_b, rest)]
    w_out = g_out.reshape(-1, d)
    conv_w = _pad_rows(g_conv.transpose(1, 0, 2).reshape(conv_w_0.shape[0], c_conv), 8)
    h1 = _conv_out(o, bcx, conv_w, w_out, xs)
    w_down0 = g_down0.reshape(-1, d)
    w_down1 = g_down1.reshape(-1, d)
    pool_w = g_pool.transpose(1, 0, 2, 3).reshape(pool_w_1.shape[0], -1, pool_w_1.shape[2])
    h2, a0, nf0 = _mlp_fwd(h1, row(norm_ffn_0), g_up0, w_down0, "mlp_fwd_0")
    h3 = _pool_fwd(h2, row(norm_mix_1), pool_w, row(pool_scale_1))
    dh4, a1, nf1, loss_part, d_final = _mlp_fwd(h3, row(norm_ffn_1), g_up1, w_down1, "mlp_fwd_1",
                                                head=(row(final_norm), target))

    slot_cols = g_up0.shape[2]
    pool_cols = pool_w.shape[2]
    da1, dz1, dh3, d_nffn1 = _mlp_bwd_x(dh4, a1, g_up1, w_down1, h3, row(norm_ffn_1), "mlp_bwd_x_1")
    dw_up1, dw_down1 = _mlp_bwd_w(nf1, da1, a1, dz1, slot_cols, "mlp_bwd_w_1")
    scatter_1 = _scatter_start([dw_up1, dw_down1.reshape(N_CHIPS, -1, d)], bf, "mlp1")
    dh2, dw_pool, d_pscale, d_nmix1 = _pool_bwd(scatter_1[-1], dh3, h2, row(norm_mix_1), pool_w,
                                                row(pool_scale_1))
    da0, dz0, dh1, d_nffn0 = _mlp_bwd_x(dh2, a0, g_up0, w_down0, h1, row(norm_ffn_0), "mlp_bwd_x_0")
    dw_up0, dw_down0 = _mlp_bwd_w(nf0, da0, a0, dz0, slot_cols, "mlp_bwd_w_0")
    dw_pool = (dw_pool.reshape(pool_w.shape[0], N_CHIPS, -1, pool_cols).transpose(1, 0, 2, 3)
               .reshape(N_CHIPS, -1, pool_cols))
    scatter_0 = _scatter_start([dw_up0, dw_down0.reshape(N_CHIPS, -1, d), dw_pool], bf, "mlp0")
    do, delta, dbcx, dw_out, d_conv = _conv_out_bwd(scatter_0[-1], dh1, w_out, o, bcx, conv_w)
    scatter_o = _scatter_start([dw_out.reshape(N_CHIPS, -1, d)], bf, "out")
    dqa, dka, dv = _attn_bwd(scatter_o[-1], qa, ka, qkv, do, lse, delta)
    dqkv, dfl, d_bf = _gate_bwd(dqa, dka, dv, fl, bf)
    dw_qkv, dw_f, dw_bcx = _wgrad_in(n0, [dqkv, dfl, dbcx])
    dw_in = jnp.concatenate([dw_qkv, dw_f[:N_HEADS], dw_bcx], axis=0).reshape(N_CHIPS, -1, d)
    slot_rows = -(-dw_in.shape[1] // 32) * 32
    dw_in = jnp.pad(dw_in, ((0, 0), (0, slot_rows - dw_in.shape[1]), (0, 0)))
    scatter_m = _scatter_start([dw_in], bf, "mixer")
    grad_x, d_nmix0 = _in_proj_bwd(scatter_m[-1], dqkv, dfl, dbcx, w_qkv, w_f, w_bcx, xs,
                                   row(norm_mix_0), dh1)

    r_up1, r_down1, r_out = _scatter_finish([scatter_1, scatter_o], grad_x, "early")
    r_up0, r_down0, r_pool, r_in = _scatter_finish([scatter_0, scatter_m], grad_x, "late")
    reduced = [r_in, r_out, r_up0, r_down0, r_pool, r_up1, r_down1]
    moments = [(m_w_in_0, v_w_in_0), (m_w_out_0, v_w_out_0), (m_w_up_0, v_w_up_0),
               (m_w_down_0, v_w_down_0), (m_pool_w_1, v_pool_w_1), (m_w_up_1, v_w_up_1),
               (m_w_down_1, v_w_down_1)]
    big_out = []
    for k, (w, g, (m, v)) in enumerate(zip(big, reduced, moments)):
        if w.shape[-1] % 128:
            view = lambda t: t.reshape(-1, t.shape[-1]).T
            back = lambda t: t.T.reshape(w.shape)
            g_view = g[:w.shape[-1]]
        else:
            view = lambda t: t.reshape(-1, t.shape[-1])
            back = lambda t: t.reshape(w.shape)
            g_view = view(g)
        delta_w, new_m, new_v = _adamw(view(w), g_view, view(m), view(v), "adamw_%d" % k)
        big_out.append((back(g_view), back(delta_w), back(new_m), back(new_v)))

    tail = jnp.concatenate([d_conv[0:3].reshape(-1)[d:], d_bf[0, :N_HEADS], loss_part[0, :1]])
    small_part = jnp.concatenate(
        [d_nmix0, d_nffn0, d_nmix1, d_pscale, d_nffn1, d_final,
         d_conv[0:3].reshape(1, -1)[:, :d],
         jnp.pad(tail, (0, d - tail.shape[0])).reshape(1, d)], axis=0)
    parts = _gather_small(small_part)

    chip = 2 * lax.axis_index("x") + lax.axis_index("y")
    cw_cols = conv_w_0.shape[1]

    def conv_block(full):
        mine = lax.dynamic_slice_in_dim(full, chip * cw_cols, cw_cols, axis=1)
        return jnp.pad(mine.reshape(-1), (0, d - mine.size))

    def small_rows(vals, cw, bfv):
        return jnp.stack(list(vals) + [cw, jnp.pad(bfv, (0, d - N_HEADS))])

    smalls_w = [norm_mix_0, norm_ffn_0, norm_mix_1, pool_scale_1, norm_ffn_1, final_norm]
    smalls_m = [m_norm_mix_0, m_norm_ffn_0, m_norm_mix_1, m_pool_scale_1, m_norm_ffn_1, m_final_norm]
    smalls_v = [v_norm_mix_0, v_norm_ffn_0, v_norm_mix_1, v_pool_scale_1, v_norm_ffn_1, v_final_norm]
    pad_cw = lambda t: jnp.pad(t.reshape(-1), (0, d - t.size))
    w_rows = small_rows(smalls_w, pad_cw(conv_w_0), b_f_0)
    m_rows = small_rows(smalls_m, pad_cw(m_conv_w_0), m_b_f_0)
    v_rows = small_rows(smalls_v, pad_cw(v_conv_w_0), v_b_f_0)

    g_sum = _sum_devices(parts)
    conv_full = jnp.concatenate([g_sum[6], g_sum[7, :3 * c_conv - d]]).reshape(3, c_conv)
    bf_grad = g_sum[7, 3 * c_conv - d:3 * c_conv - d + N_HEADS]
    loss = g_sum[7, 3 * c_conv - d + N_HEADS]
    g_rows = jnp.concatenate(
        [g_sum[0:6], conv_block(conv_full).reshape(1, d),
         jnp.pad(bf_grad, (0, d - N_HEADS)).reshape(1, d)], axis=0)
    d_rows, nm_rows, nv_rows = _adamw(w_rows, g_rows, m_rows, v_rows, "adamw_small")

    def unpack(rows):
        cw = rows[6, :conv_w_0.size].reshape(conv_w_0.shape)
        return [rows[0], rows[1], rows[2], rows[3], rows[4], rows[5], cw, rows[7, :N_HEADS]]

    def assemble(kind):
        sm = unpack([g_rows, d_rows, nm_rows, nv_rows][kind])
        lg = [t[kind] for t in big_out]
        return [sm[0], lg[0], sm[7], sm[6], lg[1], sm[1], lg[2], lg[3],
                sm[2], lg[4], sm[3], sm[4], lg[5], lg[6], sm[5]]

    return (loss, grad_x[None], *assemble(0), *assemble(1), *assemble(2), *assemble(3))
```
